```python
import jax, jax.numpy as jnp
from jax import lax
import numpy as np

D_MODEL = 1024
BATCH = 8
SEQ = 8192
DEPTH = 1

N_ATTN_HEADS = 8
HEAD_DIM = 64
ATTN_WIDTH = N_ATTN_HEADS * HEAD_DIM
CONV_GROUPS = 8
CONV_WIDTH = 512
CONV_K = 3
D_FF = 4 * D_MODEL
PLE_DIM = 256
Q_BLOCK = 128
EPS = 1e-6
SPLIT_SIZES = (ATTN_WIDTH, ATTN_WIDTH, ATTN_WIDTH, CONV_WIDTH, CONV_WIDTH, CONV_WIDTH, D_MODEL, D_MODEL)
SPLIT_POINTS = tuple(int(v) for v in np.cumsum(SPLIT_SIZES)[:-1])
D_IN = sum(SPLIT_SIZES)

kernel_name = 'hybrid_stickbreak_shortconv_block'


def rms_norm(x, g):
    xf = x.astype(jnp.float32)
    var = jnp.mean(xf * xf, axis=-1, keepdims=True)
    return (xf * lax.rsqrt(var + EPS) * g.astype(jnp.float32)).astype(x.dtype)


def stick_breaking_attention(q, k, v):
    b, h, s, dh = q.shape
    nblk = s // Q_BLOCK
    scale = dh ** -0.5
    kf = k.astype(jnp.float32)
    vf = v.astype(jnp.float32)
    q_blocks = q.reshape(b, h, nblk, Q_BLOCK, dh).transpose(2, 0, 1, 3, 4)
    key_pos = jnp.arange(s, dtype=jnp.int32)
    starts = jnp.arange(nblk, dtype=jnp.int32) * Q_BLOCK

    def block(args):
        qb, start = args
        z = jnp.einsum('bhqd,bhkd->bhqk', qb.astype(jnp.float32), kf) * scale
        q_pos = start + jnp.arange(Q_BLOCK, dtype=jnp.int32)
        causal = key_pos[None, :] < q_pos[:, None]
        log_beta = jax.nn.log_sigmoid(z)
        log_keep = jnp.where(causal, log_beta - z, 0.0)
        between = lax.cumsum(log_keep, axis=3, reverse=True) - log_keep
        w = jnp.where(causal, jnp.exp(log_beta + between), 0.0)
        return jnp.einsum('bhqk,bhkd->bhqd', w, vf)

    out = lax.map(block, (q_blocks, starts))
    return out.transpose(1, 2, 0, 3, 4).reshape(b, h, s, dh).astype(q.dtype)


def causal_depthwise_conv(u, w):
    c = u.shape[-1]
    return lax.conv_general_dilated(
        u, w[:, None, :].astype(u.dtype), window_strides=(1,),
        padding=((CONV_K - 1, 0),), dimension_numbers=('NWC', 'WIO', 'NWC'),
        feature_group_count=c)


def _fwd_setup_inputs(seed: int = 0) -> dict:
    key = jax.random.key(seed)
    ks = jax.random.split(key, 20)
    f32 = jnp.float32

    def nrm(k, shape, fan_in):
        return jax.random.normal(k, shape, f32) * (fan_in ** -0.5)

    def gain(k, shape):
        return jnp.ones(shape, f32) + 0.05 * jax.random.normal(k, shape, f32)

    return {
        'x': jax.random.normal(ks[0], (BATCH, SEQ, D_MODEL), f32),
        'p': jax.random.normal(ks[1], (DEPTH, BATCH, SEQ, PLE_DIM), f32),
        'g_pre_mix': gain(ks[2], (DEPTH, D_MODEL)),
        'w_in': nrm(ks[3], (DEPTH, D_MODEL, D_IN), D_MODEL),
        'b_gate': 0.1 * jax.random.normal(ks[4], (DEPTH, 2 * D_MODEL), f32),
        'w_conv': nrm(ks[5], (DEPTH, CONV_K, CONV_WIDTH), CONV_K),
        'w_attn_out': nrm(ks[6], (DEPTH, ATTN_WIDTH, D_MODEL), ATTN_WIDTH),
        'w_conv_out': nrm(ks[7], (DEPTH, CONV_WIDTH, D_MODEL), CONV_WIDTH),
        'w_o': nrm(ks[8], (DEPTH, D_MODEL, D_MODEL), D_MODEL),
        'g_post_mix': gain(ks[9], (DEPTH, D_MODEL)),
        'g_pre_mlp': gain(ks[10], (DEPTH, D_MODEL)),
        'w_up': nrm(ks[11], (DEPTH, D_MODEL, D_FF), D_MODEL),
        'w_down': nrm(ks[12], (DEPTH, D_FF, D_MODEL), D_FF),
        'g_post_mlp': gain(ks[13], (DEPTH, D_MODEL)),
        'g_ple': gain(ks[14], (DEPTH, D_MODEL)),
        'w_ple_gate': nrm(ks[15], (DEPTH, D_MODEL, D_MODEL), D_MODEL),
        'w_ple_proj': nrm(ks[16], (DEPTH, PLE_DIM, D_MODEL), PLE_DIM),
    }


def _fwd_reference(x, p, g_pre_mix, w_in, b_gate, w_conv, w_attn_out, w_conv_out, w_o,
              g_post_mix, g_pre_mlp, w_up, w_down, g_post_mlp, g_ple, w_ple_gate, w_ple_proj):
    bsz, seq, _ = x.shape
    for i in range(DEPTH):
        h = rms_norm(x, g_pre_mix[i])
        proj = h @ w_in[i]
        q, k, v, cb, cc, cu, ga, gc = jnp.split(proj, SPLIT_POINTS, axis=-1)

        def heads(t):
            return t.reshape(bsz, seq, N_ATTN_HEADS, HEAD_DIM).transpose(0, 2, 1, 3)

        o = stick_breaking_attention(heads(q), heads(k), heads(v))
        o = o.transpose(0, 2, 1, 3).reshape(bsz, seq, ATTN_WIDTH)
        y_attn = o @ w_attn_out[i]

        y_conv = (cb * causal_depthwise_conv(cc * cu, w_conv[i])) @ w_conv_out[i]

        gates = jax.nn.sigmoid(jnp.concatenate([ga, gc], axis=-1) + b_gate[i])
        gate_attn, gate_conv = jnp.split(gates, 2, axis=-1)
        mixed = (gate_attn * y_attn + gate_conv * y_conv) @ w_o[i]
        x = x + rms_norm(mixed, g_post_mix[i])

        h = rms_norm(x, g_pre_mlp[i])
        f = jnp.square(jax.nn.relu(h @ w_up[i])) @ w_down[i]
        x = x + rms_norm(f, g_post_mlp[i])

        ple_gate = jax.nn.sigmoid(rms_norm(x, g_ple[i]) @ w_ple_gate[i])
        x = x + ple_gate * (p[i] @ w_ple_proj[i])
    return x


import jax as _jax
import jax.numpy as _jnp

TWIN_FORMAT = 'train_step'
FWD_PARAMS = ['x', 'p', 'g_pre_mix', 'w_in', 'b_gate', 'w_conv', 'w_attn_out', 'w_conv_out', 'w_o', 'g_post_mix', 'g_pre_mlp', 'w_up', 'w_down', 'g_post_mlp', 'g_ple', 'w_ple_gate', 'w_ple_proj']
TWIN_WEIGHTS = ['g_pre_mix', 'w_in', 'b_gate', 'w_conv', 'w_attn_out', 'w_conv_out', 'w_o', 'g_post_mix', 'g_pre_mlp', 'w_up', 'w_down', 'g_post_mlp', 'g_ple', 'w_ple_gate', 'w_ple_proj']
TWIN_DIFF_INPUT = 'x'
TWIN_INPUTS = ['x', 'p', 'g_pre_mix', 'w_in', 'b_gate', 'w_conv', 'w_attn_out', 'w_conv_out', 'w_o', 'g_post_mix', 'g_pre_mlp', 'w_up', 'w_down', 'g_post_mlp', 'g_ple', 'w_ple_gate', 'w_ple_proj', 'loss_target', 'm_g_pre_mix', 'm_w_in', 'm_b_gate', 'm_w_conv', 'm_w_attn_out', 'm_w_conv_out', 'm_w_o', 'm_g_post_mix', 'm_g_pre_mlp', 'm_w_up', 'm_w_down', 'm_g_post_mlp', 'm_g_ple', 'm_w_ple_gate', 'm_w_ple_proj', 'v_g_pre_mix', 'v_w_in', 'v_b_gate', 'v_w_conv', 'v_w_attn_out', 'v_w_conv_out', 'v_w_o', 'v_g_post_mix', 'v_g_pre_mlp', 'v_w_up', 'v_w_down', 'v_g_post_mlp', 'v_g_ple', 'v_w_ple_gate', 'v_w_ple_proj']
TWIN_OUTPUTS = ['loss', 'grad_x', 'grad_g_pre_mix', 'grad_w_in', 'grad_b_gate', 'grad_w_conv', 'grad_w_attn_out', 'grad_w_conv_out', 'grad_w_o', 'grad_g_post_mix', 'grad_g_pre_mlp', 'grad_w_up', 'grad_w_down', 'grad_g_post_mlp', 'grad_g_ple', 'grad_w_ple_gate', 'grad_w_ple_proj', 'delta_g_pre_mix', 'delta_w_in', 'delta_b_gate', 'delta_w_conv', 'delta_w_attn_out', 'delta_w_conv_out', 'delta_w_o', 'delta_g_post_mix', 'delta_g_pre_mlp', 'delta_w_up', 'delta_w_down', 'delta_g_post_mlp', 'delta_g_ple', 'delta_w_ple_gate', 'delta_w_ple_proj', 'new_m_g_pre_mix', 'new_m_w_in', 'new_m_b_gate', 'new_m_w_conv', 'new_m_w_attn_out', 'new_m_w_conv_out', 'new_m_w_o', 'new_m_g_post_mix', 'new_m_g_pre_mlp', 'new_m_w_up', 'new_m_w_down', 'new_m_g_post_mlp', 'new_m_g_ple', 'new_m_w_ple_gate', 'new_m_w_ple_proj', 'new_v_g_pre_mix', 'new_v_w_in', 'new_v_b_gate', 'new_v_w_conv', 'new_v_w_attn_out', 'new_v_w_conv_out', 'new_v_w_o', 'new_v_g_post_mix', 'new_v_g_pre_mlp', 'new_v_w_up', 'new_v_w_down', 'new_v_g_post_mlp', 'new_v_g_ple', 'new_v_w_ple_gate', 'new_v_w_ple_proj']
TWIN_LEAF_KINDS = {'loss': 'loss', 'grad_x': 'grad_x', 'grad_g_pre_mix': 'grad_w', 'grad_w_in': 'grad_w', 'grad_b_gate': 'grad_w', 'grad_w_conv': 'grad_w', 'grad_w_attn_out': 'grad_w', 'grad_w_conv_out': 'grad_w', 'grad_w_o': 'grad_w', 'grad_g_post_mix': 'grad_w', 'grad_g_pre_mlp': 'grad_w', 'grad_w_up': 'grad_w', 'grad_w_down': 'grad_w', 'grad_g_post_mlp': 'grad_w', 'grad_g_ple': 'grad_w', 'grad_w_ple_gate': 'grad_w', 'grad_w_ple_proj': 'grad_w', 'delta_g_pre_mix': 'delta_w', 'delta_w_in': 'delta_w', 'delta_b_gate': 'delta_w', 'delta_w_conv': 'delta_w', 'delta_w_attn_out': 'delta_w', 'delta_w_conv_out': 'delta_w', 'delta_w_o': 'delta_w', 'delta_g_post_mix': 'delta_w', 'delta_g_pre_mlp': 'delta_w', 'delta_w_up': 'delta_w', 'delta_w_down': 'delta_w', 'delta_g_post_mlp': 'delta_w', 'delta_g_ple': 'delta_w', 'delta_w_ple_gate': 'delta_w', 'delta_w_ple_proj': 'delta_w', 'new_m_g_pre_mix': 'new_m', 'new_m_w_in': 'new_m', 'new_m_b_gate': 'new_m', 'new_m_w_conv': 'new_m', 'new_m_w_attn_out': 'new_m', 'new_m_w_conv_out': 'new_m', 'new_m_w_o': 'new_m', 'new_m_g_post_mix': 'new_m', 'new_m_g_pre_mlp': 'new_m', 'new_m_w_up': 'new_m', 'new_m_w_down': 'new_m', 'new_m_g_post_mlp': 'new_m', 'new_m_g_ple': 'new_m', 'new_m_w_ple_gate': 'new_m', 'new_m_w_ple_proj': 'new_m', 'new_v_g_pre_mix': 'new_v', 'new_v_w_in': 'new_v', 'new_v_b_gate': 'new_v', 'new_v_w_conv': 'new_v', 'new_v_w_attn_out': 'new_v', 'new_v_w_conv_out': 'new_v', 'new_v_w_o': 'new_v', 'new_v_g_post_mix': 'new_v', 'new_v_g_pre_mlp': 'new_v', 'new_v_w_up': 'new_v', 'new_v_w_down': 'new_v', 'new_v_g_post_mlp': 'new_v', 'new_v_g_ple': 'new_v', 'new_v_w_ple_gate': 'new_v', 'new_v_w_ple_proj': 'new_v'}


def _forward(args):
    return _fwd_reference(*[args[k] for k in FWD_PARAMS])


def _output_shape():
    def fwd():
        inp = _fwd_setup_inputs(0)
        return _fwd_reference(*[inp[k] for k in FWD_PARAMS])
    out = _jax.eval_shape(fwd)
    return out.shape, out.dtype

N_MICROBATCH = 1
ADAM_LR = 0.001
ADAM_B1 = 0.9
ADAM_B2 = 0.999
ADAM_EPS = 1e-08
ADAM_WD = 0.01
ADAM_STEP = 10
PER_EXAMPLE_BATCH_AXIS = {'x': 0, 'p': 1, 'loss_target': 0}
SHARED_INPUTS = []
_WEIGHT_DTYPES = {'g_pre_mix': _jnp.float32, 'w_in': _jnp.float32, 'b_gate': _jnp.float32, 'w_conv': _jnp.float32, 'w_attn_out': _jnp.float32, 'w_conv_out': _jnp.float32, 'w_o': _jnp.float32, 'g_post_mix': _jnp.float32, 'g_pre_mlp': _jnp.float32, 'w_up': _jnp.float32, 'w_down': _jnp.float32, 'g_post_mlp': _jnp.float32, 'g_ple': _jnp.float32, 'w_ple_gate': _jnp.float32, 'w_ple_proj': _jnp.float32}
MOMENT_SCALE = {'g_pre_mix': 1.017373e+00, 'w_in': 4.442877e-01, 'b_gate': 3.943781e-01, 'w_conv': 9.873979e-01, 'w_attn_out': 4.435868e-01, 'w_conv_out': 1.022158e+00, 'w_o': 1.443169e+00, 'g_post_mix': 6.402833e+01, 'g_pre_mlp': 1.711975e+00, 'w_up': 8.649609e-01, 'w_down': 2.487352e+00, 'g_post_mlp': 6.556566e+01, 'g_ple': 2.487856e+00, 'w_ple_gate': 1.498129e+00, 'w_ple_proj': 8.718560e-01}


def _to_microbatches(a, axis):
    t = _jnp.moveaxis(a, axis, 0)
    t = t.reshape((N_MICROBATCH, t.shape[0] // N_MICROBATCH) + t.shape[1:])
    return _jnp.moveaxis(t, 1, axis + 1)


def setup_inputs(seed: int = 0) -> dict:
    inp = _fwd_setup_inputs(seed)
    key = _jax.random.fold_in(_jax.random.key(seed), 7919)
    shape, _ = _output_shape()
    out = dict(inp)
    out["loss_target"] = _jax.random.normal(_jax.random.fold_in(key, 0), shape, _jnp.float32)
    for i, name in enumerate(TWIN_WEIGHTS):
        w = inp[name].astype(_jnp.float32)
        if MOMENT_SCALE is None:
            s = _jnp.sqrt(_jnp.mean(_jnp.square(w)) + 1e-30)
        else:
            s = MOMENT_SCALE[name]
        km, kv = _jax.random.split(_jax.random.fold_in(key, i + 1))
        out[name] = w
        out["m_" + name] = s * _jax.random.normal(km, w.shape, _jnp.float32)
        out["v_" + name] = (s * s) * _jax.random.uniform(kv, w.shape, _jnp.float32, 0.5, 1.5)
    if N_MICROBATCH > 1:
        for name, axis in PER_EXAMPLE_BATCH_AXIS.items():
            out[name] = _to_microbatches(out[name], axis)
    return {'x': out['x'], 'p': out['p'], 'g_pre_mix': out['g_pre_mix'], 'w_in': out['w_in'], 'b_gate': out['b_gate'], 'w_conv': out['w_conv'], 'w_attn_out': out['w_attn_out'], 'w_conv_out': out['w_conv_out'], 'w_o': out['w_o'], 'g_post_mix': out['g_post_mix'], 'g_pre_mlp': out['g_pre_mlp'], 'w_up': out['w_up'], 'w_down': out['w_down'], 'g_post_mlp': out['g_post_mlp'], 'g_ple': out['g_ple'], 'w_ple_gate': out['w_ple_gate'], 'w_ple_proj': out['w_ple_proj'], 'loss_target': out['loss_target'], 'm_g_pre_mix': out['m_g_pre_mix'], 'm_w_in': out['m_w_in'], 'm_b_gate': out['m_b_gate'], 'm_w_conv': out['m_w_conv'], 'm_w_attn_out': out['m_w_attn_out'], 'm_w_conv_out': out['m_w_conv_out'], 'm_w_o': out['m_w_o'], 'm_g_post_mix': out['m_g_post_mix'], 'm_g_pre_mlp': out['m_g_pre_mlp'], 'm_w_up': out['m_w_up'], 'm_w_down': out['m_w_down'], 'm_g_post_mlp': out['m_g_post_mlp'], 'm_g_ple': out['m_g_ple'], 'm_w_ple_gate': out['m_w_ple_gate'], 'm_w_ple_proj': out['m_w_ple_proj'], 'v_g_pre_mix': out['v_g_pre_mix'], 'v_w_in': out['v_w_in'], 'v_b_gate': out['v_b_gate'], 'v_w_conv': out['v_w_conv'], 'v_w_attn_out': out['v_w_attn_out'], 'v_w_conv_out': out['v_w_conv_out'], 'v_w_o': out['v_w_o'], 'v_g_post_mix': out['v_g_post_mix'], 'v_g_pre_mlp': out['v_g_pre_mlp'], 'v_w_up': out['v_w_up'], 'v_w_down': out['v_w_down'], 'v_g_post_mlp': out['v_g_post_mlp'], 'v_g_ple': out['v_g_ple'], 'v_w_ple_gate': out['v_w_ple_gate'], 'v_w_ple_proj': out['v_w_ple_proj']}


def _loss(weights, diff, rest, loss_target):
    with _jax.named_scope("forward"):
        args = {**rest, TWIN_DIFF_INPUT: diff, **{k: w.astype(_WEIGHT_DTYPES[k]) for k, w in weights.items()}}
        y = _forward(args)
    with _jax.named_scope("loss_head"):
        err = _jnp.square(y.astype(_jnp.float32) - loss_target)
        return 0.5 * _jnp.sum(_jnp.mean(err, axis=-1)) if err.ndim else 0.5 * err


def _adamw(w, g, m, v):
    m = ADAM_B1 * m + (1.0 - ADAM_B1) * g
    v = ADAM_B2 * v + (1.0 - ADAM_B2) * _jnp.square(g)
    m_hat = m / (1.0 - ADAM_B1 ** ADAM_STEP)
    v_hat = v / (1.0 - ADAM_B2 ** ADAM_STEP)
    delta = -ADAM_LR * (m_hat / (_jnp.sqrt(v_hat) + ADAM_EPS) + ADAM_WD * w)
    return delta, m, v


def reference(x, p, g_pre_mix, w_in, b_gate, w_conv, w_attn_out, w_conv_out, w_o, g_post_mix, g_pre_mlp, w_up, w_down, g_post_mlp, g_ple, w_ple_gate, w_ple_proj, loss_target, m_g_pre_mix, m_w_in, m_b_gate, m_w_conv, m_w_attn_out, m_w_conv_out, m_w_o, m_g_post_mix, m_g_pre_mlp, m_w_up, m_w_down, m_g_post_mlp, m_g_ple, m_w_ple_gate, m_w_ple_proj, v_g_pre_mix, v_w_in, v_b_gate, v_w_conv, v_w_attn_out, v_w_conv_out, v_w_o, v_g_post_mix, v_g_pre_mlp, v_w_up, v_w_down, v_g_post_mlp, v_g_ple, v_w_ple_gate, v_w_ple_proj):
    given = dict(x=x, p=p, g_pre_mix=g_pre_mix, w_in=w_in, b_gate=b_gate, w_conv=w_conv, w_attn_out=w_attn_out, w_conv_out=w_conv_out, w_o=w_o, g_post_mix=g_post_mix, g_pre_mlp=g_pre_mlp, w_up=w_up, w_down=w_down, g_post_mlp=g_post_mlp, g_ple=g_ple, w_ple_gate=w_ple_gate, w_ple_proj=w_ple_proj, loss_target=loss_target, m_g_pre_mix=m_g_pre_mix, m_w_in=m_w_in, m_b_gate=m_b_gate, m_w_conv=m_w_conv, m_w_attn_out=m_w_attn_out, m_w_conv_out=m_w_conv_out, m_w_o=m_w_o, m_g_post_mix=m_g_post_mix, m_g_pre_mlp=m_g_pre_mlp, m_w_up=m_w_up, m_w_down=m_w_down, m_g_post_mlp=m_g_post_mlp, m_g_ple=m_g_ple, m_w_ple_gate=m_w_ple_gate, m_w_ple_proj=m_w_ple_proj, v_g_pre_mix=v_g_pre_mix, v_w_in=v_w_in, v_b_gate=v_b_gate, v_w_conv=v_w_conv, v_w_attn_out=v_w_attn_out, v_w_conv_out=v_w_conv_out, v_w_o=v_w_o, v_g_post_mix=v_g_post_mix, v_g_pre_mlp=v_g_pre_mlp, v_w_up=v_w_up, v_w_down=v_w_down, v_g_post_mlp=v_g_post_mlp, v_g_ple=v_g_ple, v_w_ple_gate=v_w_ple_gate, v_w_ple_proj=v_w_ple_proj)
    weights = {n: given[n] for n in TWIN_WEIGHTS}
    shared = {n: given[n] for n in SHARED_INPUTS}
    per_example = {n: given[n] for n in ['x', 'p']}
    grad_fn = _jax.value_and_grad(_loss, argnums=(0, 1))

    def one_microbatch(ex, loss_target):
        ex = dict(ex)
        diff = ex.pop(TWIN_DIFF_INPUT)
        return grad_fn(weights, diff, {**shared, **ex}, loss_target)

    if N_MICROBATCH == 1:
        loss, (grad_w, grad_x) = one_microbatch(per_example, given["loss_target"])
    else:
        def body(carry, xs):
            loss_sum, grad_sum = carry
            l_k, (gw_k, gx_k) = one_microbatch(xs[0], xs[1])
            with _jax.named_scope("update"):
                return (loss_sum + l_k, _jax.tree.map(_jnp.add, grad_sum, gw_k)), gx_k

        init = (_jnp.zeros((), _jnp.float32), _jax.tree.map(_jnp.zeros_like, weights))
        (loss, grad_w), grad_x = _jax.lax.scan(body, init, (per_example, given["loss_target"]))
    with _jax.named_scope("update"):
        delta_w, new_m, new_v = {}, {}, {}
        for n in TWIN_WEIGHTS:
            delta_w[n], new_m[n], new_v[n] = _adamw(weights[n], grad_w[n], given["m_" + n], given["v_" + n])
    return (loss, grad_x, *[grad_w[n] for n in TWIN_WEIGHTS], *[delta_w[n] for n in TWIN_WEIGHTS],
            *[new_m[n] for n in TWIN_WEIGHTS], *[new_v[n] for n in TWIN_WEIGHTS])
```

```python
import functools

import jax
import jax.numpy as jnp
from jax import lax
from jax.experimental import pallas as pl
from jax.experimental.pallas import tpu as pltpu

F32 = jnp.float32
BF16 = jnp.bfloat16
MESH = pl.DeviceIdType.MESH

D_MODEL = 1024
N_HEADS = 8
HEAD_DIM = 64
ATTN_W = N_HEADS * HEAD_DIM
CONV_W = 512
D_FF = 4096
PLE_DIM = 256
D_IN = 5120
N_CHIPS = 4
EPS = 1e-6
Q_SCALE = HEAD_DIM ** -0.5

ADAM_LR = 0.001
ADAM_B1 = 0.9
ADAM_B2 = 0.999
ADAM_EPS = 1e-08
ADAM_WD = 0.01
ADAM_STEP = 10

V7X_VMEM_BYTES = 64 * 1024 * 1024
VMEM_LIMIT = V7X_VMEM_BYTES - 8 * 1024 * 1024
LANES = 128
ATT_BLK = 256
SMALL_ROWS = 16
CONV_PAD_ROWS = 16


def _cparams(n_grid):
    return pltpu.CompilerParams(dimension_semantics=("arbitrary",) * n_grid, vmem_limit_bytes=VMEM_LIMIT)


def _bs(shape, fn):
    return pl.BlockSpec(shape, fn)


def _rms_stats(xf):
    return lax.rsqrt(jnp.mean(xf * xf, axis=-1, keepdims=True) + EPS)


def _rms(xf, g):
    return xf * _rms_stats(xf) * g


def _rms_bwd(xf, g, dy):
    r = _rms_stats(xf)
    xh = xf * r
    dyg = dy * g
    dx = r * (dyg - xh * jnp.mean(dyg * xh, axis=-1, keepdims=True))
    return dx, jnp.sum(dy * xh, axis=0, keepdims=True)


def _sig(z):
    return 1.0 / (1.0 + jnp.exp(-z))


def _ident(a):
    return a


def _to_bf16(a):
    return a.astype(BF16)


_DIMS = {"nn": (((1,), (0,)), ((), ())), "nt": (((1,), (1,)), ((), ())), "tn": (((0,), (0,)), ((), ()))}


def _mm(name, mode, grid, a_ins, a_fn, b_ins, b_fn, outs, acc_shape, epi_ins=(), epi_fn=None,
        a_cache=None, a_outs=()):
    nk = grid[2]
    na, nb, ne, no, nao = len(a_ins), len(b_ins), len(epi_ins), len(outs), len(a_outs)
    assert a_cache is None or nk == 1
    assert not a_outs or a_cache is not None
    dims = _DIMS[mode]
    if epi_fn is None:
        epi_fn = lambda acc: (acc,)

    def body(*refs):
        a_refs = refs[:na]
        b_refs = refs[na:na + nb]
        e_refs = refs[na + nb:na + nb + ne]
        o_refs = refs[na + nb + ne:na + nb + ne + no]
        ao_refs = refs[na + nb + ne + no:na + nb + ne + no + nao]
        scratch = list(refs[na + nb + ne + no + nao:])
        acc_ref = scratch.pop(0) if nk > 1 else None
        a_sc = scratch.pop(0) if a_cache is not None else None
        j = pl.program_id(1)
        k = pl.program_id(2)

        def finish(acc):
            res = epi_fn(acc, *[r[...] for r in e_refs])
            for r, val in zip(o_refs, res):
                r[...] = val.astype(r.dtype)

        if a_sc is not None:
            @pl.when(j == 0)
            def _():
                res = a_fn(*[r[...] for r in a_refs])
                if nao:
                    for r, val in zip(ao_refs, res[1:]):
                        r[...] = val.astype(r.dtype)
                    res = res[0]
                a_sc[...] = res
            a = a_sc[...]
        else:
            a = a_fn(*[r[...] for r in a_refs])
        b = b_fn(*[r[...] for r in b_refs])
        prod = lax.dot_general(a, b, dims, preferred_element_type=F32)
        if nk == 1:
            finish(prod)
        else:
            @pl.when(k == 0)
            def _():
                acc_ref[...] = prod

            @pl.when(k > 0)
            def _():
                acc_ref[...] += prod

            @pl.when(k == nk - 1)
            def _():
                finish(acc_ref[...])

    scratch_shapes = []
    if nk > 1:
        scratch_shapes.append(pltpu.VMEM(acc_shape, F32))
    if a_cache is not None:
        scratch_shapes.append(pltpu.VMEM(*a_cache))
    all_outs = list(outs) + list(a_outs)
    res = pl.pallas_call(
        body, name=name, grid=grid,
        in_specs=[s for _, s in a_ins] + [s for _, s in b_ins] + [s for _, s in epi_ins],
        out_specs=[s for _, s in all_outs],
        out_shape=[o for o, _ in all_outs],
        scratch_shapes=scratch_shapes,
        compiler_params=_cparams(3),
    )(*[a for a, _ in a_ins], *[a for a, _ in b_ins], *[a for a, _ in epi_ins])
    return res


def _sds(shape, dtype):
    return jax.ShapeDtypeStruct(shape, dtype)


def _qkv_cast(proj, seq, tr):
    def body(p_ref, o_ref):
        scale = jnp.where(pl.program_id(1) == 0, Q_SCALE, 1.0).astype(F32)
        o_ref[...] = (p_ref[...] * scale).astype(BF16)

    return pl.pallas_call(
        body, name="qkv_cast", grid=(seq // tr, 3),
        in_specs=[_bs((tr, ATTN_W), lambda i, c: (i, c))],
        out_specs=_bs((tr, ATTN_W), lambda i, c: (i, c)),
        out_shape=_sds((seq, 3 * ATTN_W), BF16),
        compiler_params=_cparams(2),
    )(proj)


def _shift_rows_down(u, prev, n):
    rows = u.shape[0]
    ridx = lax.broadcasted_iota(jnp.int32, u.shape, 0)
    out = pltpu.roll(u, n, 0)
    for r in range(n):
        out = jnp.where(ridx == r, prev[8 - n + r:8 - n + r + 1, :], out)
    del rows
    return out


def _shift_rows_up(u, nxt, n):
    rows = u.shape[0]
    ridx = lax.broadcasted_iota(jnp.int32, u.shape, 0)
    out = pltpu.roll(u, rows - n, 0)
    for r in range(n):
        out = jnp.where(ridx == rows - n + r, nxt[r:r + 1, :], out)
    return out


CONV_COL0 = 3


def _conv_fwd(proj, w_conv, seq, tr):
    hb = tr // 8

    def body(cb_ref, cc_ref, cu_ref, ccp_ref, cup_ref, w_ref, e_ref, d_ref):
        i = pl.program_id(0)
        u = cc_ref[...] * cu_ref[...]
        up = jnp.where(i > 0, ccp_ref[...] * cup_ref[...], 0.0)
        w = w_ref[...]
        d = w[0:1, :] * _shift_rows_down(u, up, 2) + w[1:2, :] * _shift_rows_down(u, up, 1) + w[2:3, :] * u
        d_ref[...] = d
        e_ref[...] = (cb_ref[...] * d).astype(BF16)

    prev = lambda c: (lambda i: (jnp.maximum(i * hb - 1, 0), c))
    return pl.pallas_call(
        body, name="conv_fwd", grid=(seq // tr,),
        in_specs=[_bs((tr, CONV_W), lambda i: (i, CONV_COL0)),
                  _bs((tr, CONV_W), lambda i: (i, CONV_COL0 + 1)),
                  _bs((tr, CONV_W), lambda i: (i, CONV_COL0 + 2)),
                  _bs((8, CONV_W), prev(CONV_COL0 + 1)),
                  _bs((8, CONV_W), prev(CONV_COL0 + 2)),
                  _bs((3, CONV_W), lambda i: (0, 0))],
        out_specs=[_bs((tr, CONV_W), lambda i: (i, 0)), _bs((tr, CONV_W), lambda i: (i, 0))],
        out_shape=[_sds((seq, CONV_W), BF16), _sds((seq, CONV_W), F32)],
        compiler_params=_cparams(1),
    )(proj, proj, proj, proj, proj, w_conv)


def _conv_bwd(proj, de, d, w_conv, seq, tr):
    hb = tr // 8
    nblk = seq // tr

    def body(cb_ref, cc_ref, cu_ref, ccp_ref, cup_ref, cbn_ref, de_ref, den_ref, d_ref, w_ref, o_ref, dw_ref):
        i = pl.program_id(0)
        cc, cu, cb = cc_ref[...], cu_ref[...], cb_ref[...]
        u = cc * cu
        up = jnp.where(i > 0, ccp_ref[...] * cup_ref[...], 0.0)
        u1 = _shift_rows_down(u, up, 1)
        u2 = _shift_rows_down(u, up, 2)
        de_ = de_ref[...]
        dd = de_ * cb
        ddn = jnp.where(i < nblk - 1, den_ref[...] * cbn_ref[...], 0.0)
        w = w_ref[...]
        du = w[2:3, :] * dd + w[1:2, :] * _shift_rows_up(dd, ddn, 1) + w[0:1, :] * _shift_rows_up(dd, ddn, 2)
        o_ref[:, 0:CONV_W] = (de_ * d_ref[...]).astype(BF16)
        o_ref[:, CONV_W:2 * CONV_W] = (du * cu).astype(BF16)
        o_ref[:, 2 * CONV_W:3 * CONV_W] = (du * cc).astype(BF16)
        ridx = lax.broadcasted_iota(jnp.int32, (8, CONV_W), 0)
        dw0 = jnp.sum(dd * u2, axis=0, keepdims=True)
        dw1 = jnp.sum(dd * u1, axis=0, keepdims=True)
        dw2 = jnp.sum(dd * u, axis=0, keepdims=True)
        dw_ref[...] = jnp.where(ridx == 0, dw0, jnp.where(ridx == 1, dw1, jnp.where(ridx == 2, dw2, 0.0)))

    prev = lambda c: (lambda i: (jnp.maximum(i * hb - 1, 0), c))
    nxt = lambda c: (lambda i: (jnp.minimum((i + 1) * hb, seq // 8 - 1), c))
    return pl.pallas_call(
        body, name="conv_bwd", grid=(nblk,),
        in_specs=[_bs((tr, CONV_W), lambda i: (i, CONV_COL0)),
                  _bs((tr, CONV_W), lambda i: (i, CONV_COL0 + 1)),
                  _bs((tr, CONV_W), lambda i: (i, CONV_COL0 + 2)),
                  _bs((8, CONV_W), prev(CONV_COL0 + 1)),
                  _bs((8, CONV_W), prev(CONV_COL0 + 2)),
                  _bs((8, CONV_W), nxt(CONV_COL0)),
                  _bs((tr, CONV_W), lambda i: (i, 0)),
                  _bs((8, CONV_W), nxt(0)),
                  _bs((tr, CONV_W), lambda i: (i, 0)),
                  _bs((3, CONV_W), lambda i: (0, 0))],
        out_specs=[_bs((tr, 3 * CONV_W), lambda i: (i, 0)), _bs((None, 8, CONV_W), lambda i: (i, 0, 0))],
        out_shape=[_sds((seq, 3 * CONV_W), BF16), _sds((nblk, 8, CONV_W), F32)],
        compiler_params=_cparams(1),
    )(proj, proj, proj, proj, proj, proj, de, de, d, w_conv)


def _nt(a, b):
    return lax.dot_general(a, b, _DIMS["nt"], preferred_element_type=F32)


def _tn(a, b):
    return lax.dot_general(a, b, _DIMS["tn"], preferred_element_type=F32)


def _nn(a, b):
    return lax.dot_general(a, b, _DIMS["nn"], preferred_element_type=F32)


def _cumsum_mm(vals, tri):
    hi = vals.astype(BF16)
    lo = (vals - hi.astype(F32)).astype(BF16)
    return _nn(hi, tri) + _nn(lo, tri)


def _log_gates(z):
    e = jnp.exp(-jnp.abs(z))
    lse = jnp.log(1.0 + e)
    log_beta = jnp.minimum(z, 0.0) - lse
    return log_beta, log_beta - z, e


def _attn_fwd(qkv, seq):
    blk = ATT_BLK
    nq = seq // blk
    npair = N_HEADS // 2

    def body(q_ref, k_ref, v_ref, o_ref, lta_ref, ltb_ref):
        i = pl.program_id(1)
        is_a = lax.broadcasted_iota(jnp.int32, (1, LANES), 1) < HEAD_DIM
        q2 = q_ref[...]
        zero = jnp.zeros_like(q2)
        qs = (jnp.where(is_a, q2, zero), jnp.where(is_a, zero, q2))
        row = lax.broadcasted_iota(jnp.int32, (blk, blk), 0)
        col = lax.broadcasted_iota(jnp.int32, (blk, blk), 1)
        tri = (row > col).astype(BF16)
        causal = col < row

        def tile(j, carry, diag):
            tot = [carry[0], carry[1]]
            acc = carry[2]
            off = pl.multiple_of(j * blk, blk)
            k2 = k_ref[pl.ds(off, blk), :]
            v2 = v_ref[pl.ds(off, blk), :]
            pv = []
            for h in range(2):
                z = _nt(qs[h], k2)
                log_beta, log_keep, _ = _log_gates(z)
                if diag:
                    log_keep = jnp.where(causal, log_keep, 0.0)
                between = tot[h] + _cumsum_mm(log_keep, tri)
                w = jnp.exp(log_beta + between)
                if diag:
                    w = jnp.where(causal, w, 0.0)
                pv.append(_nn(w.astype(BF16), v2))
                tot[h] = tot[h] + jnp.sum(log_keep, axis=-1, keepdims=True)
            return tot[0], tot[1], acc + jnp.where(is_a, pv[0], pv[1])

        init = (jnp.zeros((blk, 1), F32), jnp.zeros((blk, 1), F32), jnp.zeros((blk, LANES), F32))
        carry = tile(i, init, True)
        carry = lax.fori_loop(0, i, lambda jj, c: tile(i - 1 - jj, c, False), carry)
        o_ref[...] = carry[2].astype(BF16)
        lta_ref[...] = jnp.broadcast_to(carry[0], (blk, LANES))
        ltb_ref[...] = jnp.broadcast_to(carry[1], (blk, LANES))

    return pl.pallas_call(
        body, name="attn_fwd", grid=(npair, nq),
        in_specs=[_bs((blk, LANES), lambda p, i: (i, p)),
                  _bs((seq, LANES), lambda p, i: (0, npair + p)),
                  _bs((seq, LANES), lambda p, i: (0, 2 * npair + p))],
        out_specs=[_bs((blk, LANES), lambda p, i: (i, p))] * 3,
        out_shape=[_sds((seq, ATTN_W), BF16), _sds((seq, ATTN_W), F32), _sds((seq, ATTN_W), F32)],
        compiler_params=_cparams(2),
    )(qkv, qkv, qkv)


def _attn_bwd(qkv, do, lta, ltb, seq):
    blk = ATT_BLK
    nq = seq // blk
    npair = N_HEADS // 2

    def body(q_ref, k_ref, v_ref, do_ref, lta_ref, ltb_ref, dq_ref, dk_ref, dv_ref):
        i = pl.program_id(1)

        @pl.when(i == 0)
        def _():
            dk_ref[...] = jnp.zeros_like(dk_ref)
            dv_ref[...] = jnp.zeros_like(dv_ref)

        is_a = lax.broadcasted_iota(jnp.int32, (1, LANES), 1) < HEAD_DIM
        q2 = q_ref[...]
        do2 = do_ref[...]
        zero = jnp.zeros_like(q2)
        qs = (jnp.where(is_a, q2, zero), jnp.where(is_a, zero, q2))
        dos = (jnp.where(is_a, do2, zero), jnp.where(is_a, zero, do2))
        ltot = (jnp.max(lta_ref[...], axis=-1, keepdims=True), jnp.max(ltb_ref[...], axis=-1, keepdims=True))
        row = lax.broadcasted_iota(jnp.int32, (blk, blk), 0)
        col = lax.broadcasted_iota(jnp.int32, (blk, blk), 1)
        tri_incl = (row <= col).astype(BF16)
        tri_excl = (row < col).astype(BF16)
        causal = col < row

        def tile(j, carry, diag):
            cum = [carry[0], carry[1]]
            pre = [carry[2], carry[3]]
            dq = carry[4]
            off = pl.multiple_of(j * blk, blk)
            k2 = k_ref[pl.ds(off, blk), :]
            v2 = v_ref[pl.ds(off, blk), :]
            dqs, dks, dvs = [], [], []
            for h in range(2):
                z = _nt(qs[h], k2)
                log_beta, log_keep, e = _log_gates(z)
                rcp = 1.0 / (1.0 + e)
                pos = z >= 0.0
                beta = jnp.where(pos, rcp, e * rcp)
                keep = jnp.where(pos, e * rcp, rcp)
                if diag:
                    log_keep = jnp.where(causal, log_keep, 0.0)
                between = ltot[h] - cum[h] - _cumsum_mm(log_keep, tri_incl)
                w = jnp.exp(log_beta + between)
                if diag:
                    w = jnp.where(causal, w, 0.0)
                g = _nt(dos[h], v2) * w
                before = pre[h] + _cumsum_mm(g, tri_excl)
                dz = g * keep - before * beta
                if diag:
                    dz = jnp.where(causal, dz, 0.0)
                dzb = dz.astype(BF16)
                dqs.append(_nn(dzb, k2))
                dks.append(_tn(dzb, q2))
                dvs.append(_tn(w.astype(BF16), do2))
                cum[h] = cum[h] + jnp.sum(log_keep, axis=-1, keepdims=True)
                pre[h] = pre[h] + jnp.sum(g, axis=-1, keepdims=True)
            dk_ref[pl.ds(off, blk), :] += jnp.where(is_a, dks[0], dks[1])
            dv_ref[pl.ds(off, blk), :] += jnp.where(is_a, dvs[0], dvs[1])
            return cum[0], cum[1], pre[0], pre[1], dq + jnp.where(is_a, dqs[0], dqs[1])

        zc = jnp.zeros((blk, 1), F32)
        init = (zc, zc, zc, zc, jnp.zeros((blk, LANES), F32))
        carry = lax.fori_loop(0, i, lambda j, c: tile(j, c, False), init)
        carry = tile(i, carry, True)
        dq_ref[...] = carry[4] * Q_SCALE

    qmap = lambda p, i: (i, p)
    return pl.pallas_call(
        body, name="attn_bwd", grid=(npair, nq),
        in_specs=[_bs((blk, LANES), qmap),
                  _bs((seq, LANES), lambda p, i: (0, npair + p)),
                  _bs((seq, LANES), lambda p, i: (0, 2 * npair + p)),
                  _bs((blk, LANES), qmap), _bs((blk, LANES), qmap), _bs((blk, LANES), qmap)],
        out_specs=[_bs((blk, LANES), qmap),
                   _bs((seq, LANES), lambda p, i: (0, p)),
                   _bs((seq, LANES), lambda p, i: (0, p))],
        out_shape=[_sds((seq, ATTN_W), F32)] * 3,
        compiler_params=_cparams(2),
    )(qkv, qkv, qkv, do, lta, ltb)


def _elementwise(name, fn, ins, out_dtypes):
    rows, cols = ins[0].shape
    tr = rows
    for cand in (512, 256, 128, 64, 32, 16, 8):
        if rows % cand == 0 and cand * cols * 4 <= 2 * 1024 * 1024:
            tr = cand
            break
    n_in = len(ins)

    def body(*refs):
        res = fn(*[r[...] for r in refs[:n_in]])
        for r, val in zip(refs[n_in:], res):
            r[...] = val.astype(r.dtype)

    spec = _bs((tr, cols), lambda i: (i, 0))
    return pl.pallas_call(
        body, name=name, grid=(rows // tr,),
        in_specs=[spec] * n_in, out_specs=[spec] * len(out_dtypes),
        out_shape=[_sds((rows, cols), dt) for dt in out_dtypes],
        compiler_params=_cparams(1),
    )(*ins)


def _adamw_fn(w, g, m, v):
    m = ADAM_B1 * m + (1.0 - ADAM_B1) * g
    v = ADAM_B2 * v + (1.0 - ADAM_B2) * (g * g)
    m_hat = m / (1.0 - ADAM_B1 ** ADAM_STEP)
    v_hat = v / (1.0 - ADAM_B2 ** ADAM_STEP)
    delta = -ADAM_LR * (m_hat / (jnp.sqrt(v_hat) + ADAM_EPS) + ADAM_WD * w)
    return delta, m, v


def _adamw(name, w, g, m, v):
    shape = w.shape
    as2d = lambda a: a.reshape(-1, shape[-1])
    delta, nm, nv = _elementwise(name, _adamw_fn, [as2d(w), as2d(g), as2d(m), as2d(v)], [F32, F32, F32])
    return delta.reshape(shape), nm.reshape(shape), nv.reshape(shape)


def _place():
    x, y, c = lax.axis_index("x"), lax.axis_index("y"), lax.axis_index("c")
    chips = [(1 - x, y), (x, 1 - y), (1 - x, 1 - y)]
    return x, y, c, chips


ANY = pl.BlockSpec(memory_space=pl.ANY)


def _allgather_weights(shards):
    n = len(shards)

    def body(*refs):
        src, dst = refs[:n], refs[n:2 * n]
        send_sems, recv_sems, local_sems = refs[2 * n:]
        x, y, c, chips = _place()
        me, sibling, mychip = (x, y, c), (x, y, 1 - c), 2 * x + y

        def piece(w, chip, half):
            hr = src[w].shape[0] // 2
            return dst[w].at[chip, pl.ds(half * hr, hr)]

        def copy(w, k, src_ref, dst_ref, to):
            return pltpu.make_async_remote_copy(src_ref=src_ref, dst_ref=dst_ref, send_sem=send_sems.at[w, k],
                                                recv_sem=recv_sems.at[w, k], device_id=to, device_id_type=MESH)

        started, local = [], []
        for w in range(n):
            hr = src[w].shape[0] // 2
            own = pltpu.make_async_copy(src[w], dst[w].at[mychip], local_sems.at[w])
            own.start()
            local.append(own)
            for r, (cx, cy) in enumerate(chips):
                cp = copy(w, r, src[w].at[pl.ds(c * hr, hr)], piece(w, mychip, c), (cx, cy, c))
                cp.start()
                started.append(cp)
        for w in range(n):
            for r, (cx, cy) in enumerate(chips):
                landed = piece(w, 2 * cx + cy, c)
                copy(w, r, landed, landed, me).wait_recv()
                fwd = copy(w, 3 + r, landed, landed, sibling)
                fwd.start()
                started.append(fwd)
        for w in range(n):
            for r, (cx, cy) in enumerate(chips):
                from_sib = piece(w, 2 * cx + cy, 1 - c)
                copy(w, 3 + r, from_sib, from_sib, me).wait_recv()
        for cp in local:
            cp.wait()
        for cp in started:
            cp.wait_send()

    return pl.pallas_call(
        body, name="allgather_weights",
        in_specs=[ANY] * n, out_specs=[ANY] * n,
        out_shape=[_sds((N_CHIPS,) + s.shape, s.dtype) for s in shards],
        scratch_shapes=[pltpu.SemaphoreType.DMA((n, 6)), pltpu.SemaphoreType.DMA((n, 6)),
                        pltpu.SemaphoreType.DMA((n,))],
    )(*shards)


def _rs_pair_swap(grads):
    n = len(grads)

    def body(*refs):
        g, mine, theirs = refs[:n], refs[n:2 * n], refs[2 * n:3 * n]
        send_sems, recv_sems, local_sems = refs[3 * n:]
        x, y, c, _ = _place()
        sibling = (x, y, 1 - c)
        copies = []
        for w in range(n):
            hr = g[w].shape[1] // 2
            keep = pltpu.make_async_copy(g[w].at[:, pl.ds(c * hr, hr)], mine[w], local_sems.at[w])
            give = pltpu.make_async_remote_copy(src_ref=g[w].at[:, pl.ds((1 - c) * hr, hr)], dst_ref=theirs[w],
                                                send_sem=send_sems.at[w], recv_sem=recv_sems.at[w],
                                                device_id=sibling, device_id_type=MESH)
            keep.start()
            give.start()
            copies.append((keep, give))
        for keep, give in copies:
            give.wait_recv()
        for keep, give in copies:
            give.wait_send()
            keep.wait()

    half = [_sds((N_CHIPS, a.shape[1] // 2, a.shape[2]), a.dtype) for a in grads]
    res = pl.pallas_call(
        body, name="rs_pair_swap",
        in_specs=[ANY] * n, out_specs=[ANY] * (2 * n), out_shape=half + half,
        scratch_shapes=[pltpu.SemaphoreType.DMA((n,)), pltpu.SemaphoreType.DMA((n,)), pltpu.SemaphoreType.DMA((n,))],
    )(*grads)
    return res[:n], res[n:]


def _rs_chip_exchange(parts):
    n = len(parts)

    def body(*refs):
        t, got = refs[:n], refs[n:2 * n]
        send_sems, recv_sems, local_sems = refs[2 * n:]
        x, y, c, chips = _place()
        mychip = 2 * x + y
        copies = []
        for w in range(n):
            own = pltpu.make_async_copy(t[w].at[mychip], got[w].at[3], local_sems.at[w])
            own.start()
            copies.append(own)
            for r, (cx, cy) in enumerate(chips):
                cp = pltpu.make_async_remote_copy(src_ref=t[w].at[2 * cx + cy], dst_ref=got[w].at[r],
                                                  send_sem=send_sems.at[w, r], recv_sem=recv_sems.at[w, r],
                                                  device_id=(cx, cy, c), device_id_type=MESH)
                cp.start()
                copies.append(cp)
        for cp in copies:
            cp.wait()

    return pl.pallas_call(
        body, name="rs_chip_exchange",
        in_specs=[ANY] * n, out_specs=[ANY] * n, out_shape=[_sds(a.shape, a.dtype) for a in parts],
        scratch_shapes=[pltpu.SemaphoreType.DMA((n, 3)), pltpu.SemaphoreType.DMA((n, 3)),
                        pltpu.SemaphoreType.DMA((n,))],
    )(*parts)


def _rs_join_halves(halves):
    n = len(halves)

    def body(*refs):
        f, full = refs[:n], refs[n:2 * n]
        send_sems, recv_sems, local_sems = refs[2 * n:]
        x, y, c, _ = _place()
        sibling = (x, y, 1 - c)
        copies = []
        for w in range(n):
            hr = f[w].shape[0]
            own = pltpu.make_async_copy(f[w], full[w].at[pl.ds(c * hr, hr)], local_sems.at[w])
            give = pltpu.make_async_remote_copy(src_ref=f[w], dst_ref=full[w].at[pl.ds(c * hr, hr)],
                                                send_sem=send_sems.at[w], recv_sem=recv_sems.at[w],
                                                device_id=sibling, device_id_type=MESH)
            own.start()
            give.start()
            copies.append((own, give))
        for w, (own, give) in enumerate(copies):
            hr = f[w].shape[0]
            theirs = full[w].at[pl.ds((1 - c) * hr, hr)]
            pltpu.make_async_remote_copy(src_ref=theirs, dst_ref=theirs, send_sem=send_sems.at[w],
                                         recv_sem=recv_sems.at[w], device_id=sibling, device_id_type=MESH).wait_recv()
        for own, give in copies:
            give.wait_send()
            own.wait()

    return pl.pallas_call(
        body, name="rs_join_halves",
        in_specs=[ANY] * n, out_specs=[ANY] * n,
        out_shape=[_sds((2 * a.shape[0], a.shape[1]), a.dtype) for a in halves],
        scratch_shapes=[pltpu.SemaphoreType.DMA((n,)), pltpu.SemaphoreType.DMA((n,)), pltpu.SemaphoreType.DMA((n,))],
    )(*halves)


def _small_allreduce(loss_p, dg_parts, dbg_a, dbg_c, dwc):
    ins = [loss_p] + list(dg_parts) + [dbg_a, dbg_c, dwc]
    n_in = len(ins)
    vmem = pl.BlockSpec(memory_space=pltpu.VMEM)

    def body(*refs):
        in_refs = refs[:n_in]
        out_ref, vec, buf, send_sems, recv_sems = refs[n_in:]
        x, y, c, _ = _place()
        me = 4 * x + 2 * y + c
        vec[...] = jnp.zeros_like(vec)
        vec[0:1, :] = jnp.sum(in_refs[0][...], axis=0)
        for r in range(5):
            vec[1 + r:2 + r, :] = jnp.sum(in_refs[1 + r][...], axis=0)
        vec[6:7, :] = jnp.sum(in_refs[6][...], axis=0)
        vec[7:8, :] = jnp.sum(in_refs[7][...], axis=0)
        vec[8:16, 0:CONV_W] = jnp.sum(in_refs[8][...], axis=0)
        buf[pl.ds(me, 1)] = vec[...][None]
        copies = []
        for r in range(1, 8):
            fx, fy, fc = (r >> 2) & 1, (r >> 1) & 1, r & 1
            to = (1 - x if fx else x, 1 - y if fy else y, 1 - c if fc else c)
            cp = pltpu.make_async_remote_copy(src_ref=vec, dst_ref=buf.at[me], send_sem=send_sems.at[r - 1],
                                              recv_sem=recv_sems.at[r - 1], device_id=to, device_id_type=MESH)
            cp.start()
            copies.append(cp)
        for cp in copies:
            cp.wait()
        total = buf[0]
        for s in range(1, 8):
            total = total + buf[s]
        out_ref[...] = total
        out_ref[0:1, :] = jnp.broadcast_to(jnp.sum(total[0:1, :], axis=-1, keepdims=True), (1, D_MODEL))

    return pl.pallas_call(
        body, name="small_allreduce",
        in_specs=[vmem] * n_in, out_specs=vmem, out_shape=_sds((SMALL_ROWS, D_MODEL), F32),
        scratch_shapes=[pltpu.VMEM((SMALL_ROWS, D_MODEL), F32), pltpu.VMEM((8, SMALL_ROWS, D_MODEL), F32),
                        pltpu.SemaphoreType.DMA((7,)), pltpu.SemaphoreType.DMA((7,))],
    )(*ins)


def _local_step(x, p, tgt, g, b_gate, w_conv, wf):
    seq = x.shape[0]
    tm = min(seq, 1024)
    th = min(seq, 512)
    ni, nh = seq // tm, seq // th
    g_pre_mix, g_post_mix, g_pre_mlp, g_post_mlp, g_ple = g
    w_in, w_ao, w_co, w_o, w_up, w_down, w_pg, w_pp = wf
    D = D_MODEL
    vec = lambda a, blk=0: (a, _bs((1, D), lambda i, j, k: (0, blk)))
    rows_i = lambda a, t, blk=0: (a, _bs((t, D), lambda i, j, k: (i, blk)))
    rows_k = lambda a, t, blk=0: (a, _bs((t, D), lambda i, j, k: (k, blk)))
    part = lambda n: (_sds((n, 1, D), F32), _bs((None, 1, D), lambda i, j, k: (i, 0, 0)))
    full2 = lambda a: (a, _bs(a.shape, lambda i, j, k: (0, 0)))

    (proj,) = _mm("proj_in", "nn", (ni, 4, 1),
                  a_ins=[rows_i(x, tm), vec(g_pre_mix)], a_fn=lambda xb, gb: _rms(xb, gb).astype(BF16),
                  b_ins=[(w_in, _bs((None, D, 1280), lambda i, j, k: (j, 0, 0)))], b_fn=_ident,
                  outs=[(_sds((seq, D_IN), F32), _bs((tm, 1280), lambda i, j, k: (i, j)))],
                  acc_shape=(tm, 1280), a_cache=((tm, D), BF16))
    qkv = _qkv_cast(proj, seq, tm)
    o, lta, ltb = _attn_fwd(qkv, seq)
    (y_attn,) = _mm("attn_out", "nn", (ni, 1, 1),
                    a_ins=[(o, _bs((tm, ATTN_W), lambda i, j, k: (i, 0)))], a_fn=_ident,
                    b_ins=[full2(w_ao)], b_fn=_ident,
                    outs=[(_sds((seq, D), F32), _bs((tm, D), lambda i, j, k: (i, 0)))], acc_shape=(tm, D))
    e, d = _conv_fwd(proj, w_conv, seq, tm)
    (y_conv,) = _mm("conv_out", "nn", (ni, 1, 1),
                    a_ins=[(e, _bs((tm, CONV_W), lambda i, j, k: (i, 0)))], a_fn=_ident,
                    b_ins=[full2(w_co)], b_fn=_ident,
                    outs=[(_sds((seq, D), F32), _bs((tm, D), lambda i, j, k: (i, 0)))], acc_shape=(tm, D))

    def mix_fn(ga, gc, ya, yc, ba, bc):
        return (_sig(ga + ba) * ya + _sig(gc + bc) * yc).astype(BF16)

    def post_mix(acc, xb, gb):
        return acc, xb + _rms(acc, gb)

    mix_ins = lambda rows: [rows(proj, th, 3), rows(proj, th, 4), rows(y_attn, th), rows(y_conv, th),
                            vec(b_gate, 0), vec(b_gate, 1)]
    mixed, x1 = _mm("mix_out", "nn", (nh, 1, 1),
                    a_ins=mix_ins(rows_i), a_fn=mix_fn, b_ins=[full2(w_o)], b_fn=_ident,
                    epi_ins=[rows_i(x, th), vec(g_post_mix)], epi_fn=post_mix,
                    outs=[(_sds((seq, D), F32), _bs((th, D), lambda i, j, k: (i, 0)))] * 2,
                    acc_shape=(th, D), a_cache=((th, D), BF16))
    (up,) = _mm("mlp_up", "nn", (ni, 4, 1),
                a_ins=[rows_i(x1, tm), vec(g_pre_mlp)], a_fn=lambda xb, gb: _rms(xb, gb).astype(BF16),
                b_ins=[(w_up, _bs((None, D, D), lambda i, j, k: (j, 0, 0)))], b_fn=_ident,
                outs=[(_sds((seq, D_FF), F32), _bs((tm, D), lambda i, j, k: (i, j)))],
                acc_shape=(tm, D), a_cache=((tm, D), BF16))

    def relu2(ub):
        r = jnp.maximum(ub, 0.0)
        return (r * r).astype(BF16)

    f, x2 = _mm("mlp_down", "nn", (nh, 1, 4),
                a_ins=[(up, _bs((th, D), lambda i, j, k: (i, k)))], a_fn=relu2,
                b_ins=[(w_down, _bs((D, D), lambda i, j, k: (k, 0)))], b_fn=_ident,
                epi_ins=[rows_i(x1, th), vec(g_post_mlp)], epi_fn=post_mix,
                outs=[(_sds((seq, D), F32), _bs((th, D), lambda i, j, k: (i, 0)))] * 2, acc_shape=(th, D))
    (pp,) = _mm("ple_proj", "nn", (ni, 1, 1),
                a_ins=[(p, _bs((tm, PLE_DIM), lambda i, j, k: (i, 0)))], a_fn=_to_bf16,
                b_ins=[full2(w_pp)], b_fn=_ident,
                outs=[(_sds((seq, D), F32), _bs((tm, D), lambda i, j, k: (i, 0)))], acc_shape=(tm, D))

    def head(acc, x2b, ppb, tb):
        pg = _sig(acc)
        err = x2b + pg * ppb - tb
        return pg, err * (1.0 / D), jnp.sum(err * err, axis=0, keepdims=True) * (0.5 / D)

    pg, dx3, loss_p = _mm("ple_gate_loss", "nn", (nh, 1, 1),
                          a_ins=[rows_i(x2, th), vec(g_ple)], a_fn=lambda xb, gb: _rms(xb, gb).astype(BF16),
                          b_ins=[full2(w_pg)], b_fn=_ident,
                          epi_ins=[rows_i(x2, th), rows_i(pp, th), rows_i(tgt, th)], epi_fn=head,
                          outs=[(_sds((seq, D), F32), _bs((th, D), lambda i, j, k: (i, 0)))] * 2 + [part(nh)],
                          acc_shape=(th, D), a_cache=((th, D), BF16))

    (dw_pp,) = _mm("dw_ple_proj", "tn", (1, 1, nh),
                   a_ins=[(p, _bs((th, PLE_DIM), lambda i, j, k: (k, 0)))], a_fn=_to_bf16,
                   b_ins=[rows_k(dx3, th), rows_k(pg, th)], b_fn=lambda a, b: (a * b).astype(BF16),
                   outs=[(_sds((PLE_DIM, D), F32), _bs((PLE_DIM, D), lambda i, j, k: (0, 0)))],
                   acc_shape=(PLE_DIM, D))

    def dpre_fn(dx3b, ppb, pgb):
        return (dx3b * ppb * pgb * (1.0 - pgb)).astype(BF16)

    def ple_norm_bwd(acc, x2b, dx3b, gb):
        dxn, dg = _rms_bwd(x2b, gb, acc)
        return dx3b + dxn, dg

    dx2, dg_ple_p, dpre = _mm("d_ple_gate", "nt", (nh, 1, 1),
                              a_ins=[rows_i(dx3, th), rows_i(pp, th), rows_i(pg, th)],
                              a_fn=lambda a, b, c: (dpre_fn(a, b, c),) * 2,
                              b_ins=[full2(w_pg)], b_fn=_ident,
                              epi_ins=[rows_i(x2, th), rows_i(dx3, th), vec(g_ple)], epi_fn=ple_norm_bwd,
                              outs=[(_sds((seq, D), F32), _bs((th, D), lambda i, j, k: (i, 0))), part(nh)],
                              acc_shape=(th, D), a_cache=((th, D), BF16),
                              a_outs=[(_sds((seq, D), BF16), _bs((th, D), lambda i, j, k: (i, 0)))])
    (dw_pg,) = _mm("dw_ple_gate", "tn", (1, 1, nh),
                   a_ins=[rows_k(x2, th), vec(g_ple)], a_fn=lambda xb, gb: _rms(xb, gb).astype(BF16),
                   b_ins=[rows_k(dpre, th)], b_fn=_ident,
                   outs=[(_sds((D, D), F32), _bs((D, D), lambda i, j, k: (0, 0)))], acc_shape=(D, D))

    def df_fn(fb, dx2b, gb):
        dfb, dg = _rms_bwd(fb, gb, dx2b)
        dfb = dfb.astype(BF16)
        return dfb, dfb, dg

    def dup_fn(acc, ub):
        return (acc * (2.0 * jnp.maximum(ub, 0.0)),)

    dup, df, dg_post_mlp_p = _mm("d_mlp_down", "nt", (nh, 4, 1),
                                 a_ins=[rows_i(f, th), rows_i(dx2, th), vec(g_post_mlp)], a_fn=df_fn,
                                 b_ins=[(w_down, _bs((D, D), lambda i, j, k: (j, 0)))], b_fn=_ident,
                                 epi_ins=[(up, _bs((th, D), lambda i, j, k: (i, j)))], epi_fn=dup_fn,
                                 outs=[(_sds((seq, D_FF), BF16), _bs((th, D), lambda i, j, k: (i, j)))],
                                 acc_shape=(th, D), a_cache=((th, D), BF16),
                                 a_outs=[(_sds((seq, D), BF16), _bs((th, D), lambda i, j, k: (i, 0))), part(nh)])
    (dw_down,) = _mm("dw_mlp_down", "tn", (4, 1, nh),
                     a_ins=[(up, _bs((th, D), lambda i, j, k: (k, i)))], a_fn=relu2,
                     b_ins=[rows_k(df, th)], b_fn=_ident,
                     outs=[(_sds((D_FF, D), F32), _bs((D, D), lambda i, j, k: (i, 0)))], acc_shape=(D, D))
    (dw_up,) = _mm("dw_mlp_up", "tn", (1, 4, nh),
                   a_ins=[rows_k(x1, th), vec(g_pre_mlp)], a_fn=lambda xb, gb: _rms(xb, gb).astype(BF16),
                   b_ins=[(dup, _bs((th, D), lambda i, j, k: (k, j)))], b_fn=_ident,
                   outs=[(_sds((N_CHIPS, D, D), F32), _bs((None, D, D), lambda i, j, k: (j, 0, 0)))],
                   acc_shape=(D, D))

    def mlp_norm_bwd(acc, x1b, dx2b, mixedb, g_mlp, g_mix):
        dxn, dg_mlp = _rms_bwd(x1b, g_mlp, acc)
        dx1b = dx2b + dxn
        dmixedb, dg_mix = _rms_bwd(mixedb, g_mix, dx1b)
        return dx1b, dmixedb, dg_mlp, dg_mix

    dx1, dmixed, dg_pre_mlp_p, dg_post_mix_p = _mm(
        "d_mlp_up", "nt", (nh, 1, 4),
        a_ins=[(dup, _bs((th, D), lambda i, j, k: (i, k)))], a_fn=_ident,
        b_ins=[(w_up, _bs((None, D, D), lambda i, j, k: (k, 0, 0)))], b_fn=_ident,
        epi_ins=[rows_i(x1, th), rows_i(dx2, th), rows_i(mixed, th), vec(g_pre_mlp), vec(g_post_mix)],
        epi_fn=mlp_norm_bwd,
        outs=[(_sds((seq, D), F32), _bs((th, D), lambda i, j, k: (i, 0))),
              (_sds((seq, D), BF16), _bs((th, D), lambda i, j, k: (i, 0))), part(nh), part(nh)],
        acc_shape=(th, D))
    (dw_o,) = _mm("dw_mix_out", "tn", (1, 1, nh),
                  a_ins=mix_ins(rows_k), a_fn=mix_fn, b_ins=[rows_k(dmixed, th)], b_fn=_ident,
                  outs=[(_sds((D, D), F32), _bs((D, D), lambda i, j, k: (0, 0)))], acc_shape=(D, D))

    def gate_bwd(acc, ga, gc, ya, yc, ba, bc):
        sa, sc = _sig(ga + ba), _sig(gc + bc)
        dga = acc * ya * sa * (1.0 - sa)
        dgc = acc * yc * sc * (1.0 - sc)
        return (acc * sa, acc * sc, jnp.concatenate([dga, dgc], axis=1),
                jnp.sum(dga, axis=0, keepdims=True), jnp.sum(dgc, axis=0, keepdims=True))

    dya, dyc, dgate, dbg_a_p, dbg_c_p = _mm(
        "d_mix_out", "nt", (nh, 1, 1),
        a_ins=[rows_i(dmixed, th)], a_fn=_ident, b_ins=[full2(w_o)], b_fn=_ident,
        epi_ins=mix_ins(rows_i), epi_fn=gate_bwd,
        outs=[(_sds((seq, D), BF16), _bs((th, D), lambda i, j, k: (i, 0)))] * 2
             + [(_sds((seq, 2 * D), BF16), _bs((th, 2 * D), lambda i, j, k: (i, 0))), part(nh), part(nh)],
        acc_shape=(th, D))
    (dw_ao,) = _mm("dw_attn_out", "tn", (1, 1, nh),
                   a_ins=[(o, _bs((th, ATTN_W), lambda i, j, k: (k, 0)))], a_fn=_ident,
                   b_ins=[rows_k(dya, th)], b_fn=_ident,
                   outs=[(_sds((ATTN_W, D), F32), _bs((ATTN_W, D), lambda i, j, k: (0, 0)))], acc_shape=(ATTN_W, D))
    (do,) = _mm("d_attn_out", "nt", (ni, 1, 1),
                a_ins=[rows_i(dya, tm)], a_fn=_ident, b_ins=[full2(w_ao)], b_fn=_ident,
                outs=[(_sds((seq, ATTN_W), BF16), _bs((tm, ATTN_W), lambda i, j, k: (i, 0)))],
                acc_shape=(tm, ATTN_W))
    dq, dk, dv = _attn_bwd(qkv, do, lta, ltb, seq)
    (dw_co,) = _mm("dw_conv_out", "tn", (1, 1, nh),
                   a_ins=[(e, _bs((th, CONV_W), lambda i, j, k: (k, 0)))], a_fn=_ident,
                   b_ins=[rows_k(dyc, th)], b_fn=_ident,
                   outs=[(_sds((CONV_W, D), F32), _bs((CONV_W, D), lambda i, j, k: (0, 0)))], acc_shape=(CONV_W, D))
    (de,) = _mm("d_conv_out", "nt", (ni, 1, 1),
                a_ins=[rows_i(dyc, tm)], a_fn=_ident, b_ins=[full2(w_co)], b_fn=_ident,
                outs=[(_sds((seq, CONV_W), F32), _bs((tm, CONV_W), lambda i, j, k: (i, 0)))],
                acc_shape=(tm, CONV_W))
    dconv, dwc_p = _conv_bwd(proj, de, d, w_conv, seq, tm)
    dproj = jnp.concatenate([dq.astype(BF16), dk.astype(BF16), dv.astype(BF16), dconv, dgate], axis=1)
    (dw_in,) = _mm("dw_proj_in", "tn", (1, 4, nh),
                   a_ins=[rows_k(x, th), vec(g_pre_mix)], a_fn=lambda xb, gb: _rms(xb, gb).astype(BF16),
                   b_ins=[(dproj, _bs((th, 1280), lambda i, j, k: (k, j)))], b_fn=_ident,
                   outs=[(_sds((N_CHIPS, D, 1280), F32), _bs((None, D, 1280), lambda i, j, k: (j, 0, 0)))],
                   acc_shape=(D, 1280))

    def in_norm_bwd(acc, xb, dx1b, gb):
        dxn, dg = _rms_bwd(xb, gb, acc)
        return dx1b + dxn, dg

    grad_x, dg_pre_mix_p = _mm("d_proj_in", "nt", (nh, 1, 4),
                               a_ins=[(dproj, _bs((th, 1280), lambda i, j, k: (i, k)))], a_fn=_ident,
                               b_ins=[(w_in, _bs((None, D, 1280), lambda i, j, k: (k, 0, 0)))], b_fn=_ident,
                               epi_ins=[rows_i(x, th), rows_i(dx1, th), vec(g_pre_mix)], epi_fn=in_norm_bwd,
                               outs=[(_sds((seq, D), F32), _bs((th, D), lambda i, j, k: (i, 0))), part(nh)],
                               acc_shape=(th, D))

    chip_major = lambda a: a.reshape(a.shape[0], N_CHIPS, a.shape[1] // N_CHIPS).transpose(1, 0, 2)
    big = [dw_in, chip_major(dw_ao), chip_major(dw_co), dw_o.reshape(N_CHIPS, D // N_CHIPS, D), dw_up,
           dw_down.reshape(N_CHIPS, D_FF // N_CHIPS, D), dw_pg.reshape(N_CHIPS, D // N_CHIPS, D), chip_major(dw_pp)]
    small = (loss_p, [dg_pre_mix_p, dg_post_mix_p, dg_pre_mlp_p, dg_post_mlp_p, dg_ple_p], dbg_a_p, dbg_c_p, dwc_p)
    return grad_x, big, small


def _add2(a, b):
    return (a + b,)


def _add4(a, b, c, d):
    return (((a + b) + c) + d,)


def _reduce_scatter(big):
    mine, theirs = _rs_pair_swap(big)
    pair = []
    for w, (a, b) in enumerate(zip(mine, theirs)):
        shape = a.shape
        (s,) = _elementwise(f"rs_add_pair_{w}", _add2, [a.reshape(-1, shape[-1]), b.reshape(-1, shape[-1])], [F32])
        pair.append(s.reshape(shape))
    got = _rs_chip_exchange(pair)
    halves = []
    for w, a in enumerate(got):
        (s,) = _elementwise(f"rs_add_chips_{w}", _add4, [a[3], a[0], a[1], a[2]], [F32])
        halves.append(s)
    return _rs_join_halves(halves)


def kernel(x, p, g_pre_mix, w_in, b_gate, w_conv, w_attn_out, w_conv_out, w_o, g_post_mix, g_pre_mlp, w_up, w_down, g_post_mlp, g_ple, w_ple_gate, w_ple_proj, loss_target, m_g_pre_mix, m_w_in, m_b_gate, m_w_conv, m_w_attn_out, m_w_conv_out, m_w_o, m_g_post_mix, m_g_pre_mlp, m_w_up, m_w_down, m_g_post_mlp, m_g_ple, m_w_ple_gate, m_w_ple_proj, v_g_pre_mix, v_w_in, v_b_gate, v_w_conv, v_w_attn_out, v_w_conv_out, v_w_o, v_g_post_mix, v_g_pre_mlp, v_w_up, v_w_down, v_g_post_mlp, v_g_ple, v_w_ple_gate, v_w_ple_proj):
    mats = [w_in, w_attn_out, w_conv_out, w_o, w_up, w_down, w_ple_gate, w_ple_proj]
    mats_m = [m_w_in, m_w_attn_out, m_w_conv_out, m_w_o, m_w_up, m_w_down, m_w_ple_gate, m_w_ple_proj]
    mats_v = [v_w_in, v_w_attn_out, v_w_conv_out, v_w_o, v_w_up, v_w_down, v_w_ple_gate, v_w_ple_proj]
    gains = [g_pre_mix, g_post_mix, g_pre_mlp, g_post_mlp, g_ple]
    gains_m = [m_g_pre_mix, m_g_post_mix, m_g_pre_mlp, m_g_post_mlp, m_g_ple]
    gains_v = [v_g_pre_mix, v_g_post_mix, v_g_pre_mlp, v_g_post_mlp, v_g_ple]

    taps = jnp.concatenate([w_conv[0], jnp.zeros((CONV_PAD_ROWS - 3, LANES), F32)], axis=0)
    gathered = _allgather_weights([w[0].astype(BF16) for w in mats] + [taps])
    cols_joined = lambda a: a.transpose(1, 0, 2).reshape(a.shape[1], N_CHIPS * a.shape[2])
    rows_joined = lambda a: a.reshape(N_CHIPS * a.shape[1], a.shape[2])
    wf = [gathered[0], cols_joined(gathered[1]), cols_joined(gathered[2]), rows_joined(gathered[3]), gathered[4],
          rows_joined(gathered[5]), rows_joined(gathered[6]), cols_joined(gathered[7])]
    w_conv_full = cols_joined(gathered[8])[0:3, :]
    chip = 2 * lax.axis_index("x") + lax.axis_index("y")

    grad_x, big, small = _local_step(x[0], p[0, 0], loss_target[0], gains, b_gate, w_conv_full, wf)

    shard_grads = _reduce_scatter(big)
    red = _small_allreduce(*small)
    loss = red[0, 0]
    grad_gains = [red[1 + r:2 + r, :] for r in range(5)]
    grad_b_gate = jnp.concatenate([red[6:7, :], red[7:8, :]], axis=1)
    grad_w_conv = lax.dynamic_slice(red[8:11, :], (0, chip * LANES), (3, LANES))[None]

    grads_big = [gr.reshape(w.shape) for gr, w in zip(shard_grads, mats)]
    upd_big = [_adamw(f"adamw_{i}", w, gr, m, v) for i, (w, gr, m, v) in enumerate(zip(mats, grads_big, mats_m, mats_v))]
    pack = lambda vs, bg: jnp.concatenate(list(vs) + [bg.reshape(2, D_MODEL), jnp.zeros((1, D_MODEL), F32)], axis=0)
    upd_small = _adamw("adamw_small", pack(gains, b_gate), pack(grad_gains, grad_b_gate),
                       pack(gains_m, m_b_gate), pack(gains_v, v_b_gate))
    upd_conv = _adamw("adamw_conv", w_conv, grad_w_conv, m_w_conv, v_w_conv)

    def small_out(a, which):
        gains_out = [a[r:r + 1, :] for r in range(5)]
        return gains_out, a[5:7, :].reshape(1, 2 * D_MODEL)

    def ordered(g_pre_mix_, big_, b_gate_, conv_, g_rest):
        return [g_pre_mix_, big_[0], b_gate_, conv_, big_[1], big_[2], big_[3], g_rest[0], g_rest[1], big_[4], big_[5],
                g_rest[2], g_rest[3], big_[6], big_[7]]

    outs = [loss, grad_x[None]]
    outs += ordered(grad_gains[0], grads_big, grad_b_gate, grad_w_conv, grad_gains[1:])
    for which in range(3):
        g_out, b_out = small_out(upd_small[which], which)
        outs += ordered(g_out[0], [u[which] for u in upd_big], b_out, upd_conv[which], g_out[1:])
    return tuple(outs)
```

```python
import functools

import jax
import jax.numpy as jnp
from jax import lax
from jax.experimental import pallas as pl
from jax.experimental.pallas import tpu as pltpu

F32 = jnp.float32
BF16 = jnp.bfloat16
MESH = pl.DeviceIdType.MESH

D_MODEL = 1024
N_HEADS = 8
HEAD_DIM = 64
ATTN_W = N_HEADS * HEAD_DIM
CONV_W = 512
D_FF = 4096
PLE_DIM = 256
D_IN = 5120
N_CHIPS = 4
EPS = 1e-6
Q_SCALE = HEAD_DIM ** -0.5

ADAM_LR = 0.001
ADAM_B1 = 0.9
ADAM_B2 = 0.999
ADAM_EPS = 1e-08
ADAM_WD = 0.01
ADAM_STEP = 10

V7X_VMEM_BYTES = 64 * 1024 * 1024
VMEM_LIMIT = V7X_VMEM_BYTES - 8 * 1024 * 1024
LANES = 128
ATT_BLK = 256
SMALL_ROWS = 16
DMA_CHUNK_BYTES = 512 * 1024
CONV_PAD_ROWS = 16


def _cparams(n_grid):
    return pltpu.CompilerParams(dimension_semantics=("arbitrary",) * n_grid, vmem_limit_bytes=VMEM_LIMIT)


def _bs(shape, fn):
    return pl.BlockSpec(shape, fn)


def _rms_stats(xf):
    return lax.rsqrt(jnp.mean(xf * xf, axis=-1, keepdims=True) + EPS)


def _rms(xf, g):
    return xf * _rms_stats(xf) * g


def _rms_bwd(xf, g, dy):
    r = _rms_stats(xf)
    xh = xf * r
    dyg = dy * g
    dx = r * (dyg - xh * jnp.mean(dyg * xh, axis=-1, keepdims=True))
    return dx, jnp.sum(dy * xh, axis=0, keepdims=True)


def _sig(z):
    return 1.0 / (1.0 + jnp.exp(-z))


def _ident(a):
    return a


def _to_bf16(a):
    return a.astype(BF16)


_DIMS = {"nn": (((1,), (0,)), ((), ())), "nt": (((1,), (1,)), ((), ())), "tn": (((0,), (0,)), ((), ()))}


def _mm(name, mode, grid, a_ins, a_fn, b_ins, b_fn, outs, acc_shape, epi_ins=(), epi_fn=None,
        a_cache=None, a_outs=()):
    nk = grid[2]
    na, nb, ne, no, nao = len(a_ins), len(b_ins), len(epi_ins), len(outs), len(a_outs)
    assert a_cache is None or nk == 1
    assert not a_outs or a_cache is not None
    dims = _DIMS[mode]
    if epi_fn is None:
        epi_fn = lambda acc: (acc,)

    def body(*refs):
        a_refs = refs[:na]
        b_refs = refs[na:na + nb]
        e_refs = refs[na + nb:na + nb + ne]
        o_refs = refs[na + nb + ne:na + nb + ne + no]
        ao_refs = refs[na + nb + ne + no:na + nb + ne + no + nao]
        scratch = list(refs[na + nb + ne + no + nao:])
        acc_ref = scratch.pop(0) if nk > 1 else None
        a_sc = scratch.pop(0) if a_cache is not None else None
        j = pl.program_id(1)
        k = pl.program_id(2)

        def finish(acc):
            res = epi_fn(acc, *[r[...] for r in e_refs])
            for r, val in zip(o_refs, res):
                r[...] = val.astype(r.dtype)

        if a_sc is not None:
            @pl.when(j == 0)
            def _():
                res = a_fn(*[r[...] for r in a_refs])
                if nao:
                    for r, val in zip(ao_refs, res[1:]):
                        r[...] = val.astype(r.dtype)
                    res = res[0]
                a_sc[...] = res
            a = a_sc[...]
        else:
            a = a_fn(*[r[...] for r in a_refs])
        b = b_fn(*[r[...] for r in b_refs])
        prod = lax.dot_general(a, b, dims, preferred_element_type=F32)
        if nk == 1:
            finish(prod)
        else:
            @pl.when(k == 0)
            def _():
                acc_ref[...] = prod

            @pl.when(k > 0)
            def _():
                acc_ref[...] += prod

            @pl.when(k == nk - 1)
            def _():
                finish(acc_ref[...])

    scratch_shapes = []
    if nk > 1:
        scratch_shapes.append(pltpu.VMEM(acc_shape, F32))
    if a_cache is not None:
        scratch_shapes.append(pltpu.VMEM(*a_cache))
    all_outs = list(outs) + list(a_outs)
    res = pl.pallas_call(
        body, name=name, grid=grid,
        in_specs=[s for _, s in a_ins] + [s for _, s in b_ins] + [s for _, s in epi_ins],
        out_specs=[s for _, s in all_outs],
        out_shape=[o for o, _ in all_outs],
        scratch_shapes=scratch_shapes,
        compiler_params=_cparams(3),
    )(*[a for a, _ in a_ins], *[a for a, _ in b_ins], *[a for a, _ in epi_ins])
    return res


def _sds(shape, dtype):
    return jax.ShapeDtypeStruct(shape, dtype)


def _qkv_cast(proj, seq, tr):
    def body(p_ref, o_ref):
        scale = jnp.where(pl.program_id(1) == 0, Q_SCALE, 1.0).astype(F32)
        o_ref[...] = (p_ref[...] * scale).astype(BF16)

    return pl.pallas_call(
        body, name="qkv_cast", grid=(seq // tr, 3),
        in_specs=[_bs((tr, ATTN_W), lambda i, c: (i, c))],
        out_specs=_bs((tr, ATTN_W), lambda i, c: (i, c)),
        out_shape=_sds((seq, 3 * ATTN_W), BF16),
        compiler_params=_cparams(2),
    )(proj)


def _shift_rows_down(u, prev, n):
    rows = u.shape[0]
    ridx = lax.broadcasted_iota(jnp.int32, u.shape, 0)
    out = pltpu.roll(u, n, 0)
    for r in range(n):
        out = jnp.where(ridx == r, prev[8 - n + r:8 - n + r + 1, :], out)
    del rows
    return out


def _shift_rows_up(u, nxt, n):
    rows = u.shape[0]
    ridx = lax.broadcasted_iota(jnp.int32, u.shape, 0)
    out = pltpu.roll(u, rows - n, 0)
    for r in range(n):
        out = jnp.where(ridx == rows - n + r, nxt[r:r + 1, :], out)
    return out


CONV_COL0 = 3


def _conv_fwd(proj, w_conv, seq, tr):
    hb = tr // 8

    def body(cb_ref, cc_ref, cu_ref, ccp_ref, cup_ref, w_ref, e_ref, d_ref):
        i = pl.program_id(0)
        u = cc_ref[...] * cu_ref[...]
        up = jnp.where(i > 0, ccp_ref[...] * cup_ref[...], 0.0)
        w = w_ref[...]
        d = w[0:1, :] * _shift_rows_down(u, up, 2) + w[1:2, :] * _shift_rows_down(u, up, 1) + w[2:3, :] * u
        d_ref[...] = d
        e_ref[...] = (cb_ref[...] * d).astype(BF16)

    prev = lambda c: (lambda i: (jnp.maximum(i * hb - 1, 0), c))
    return pl.pallas_call(
        body, name="conv_fwd", grid=(seq // tr,),
        in_specs=[_bs((tr, CONV_W), lambda i: (i, CONV_COL0)),
                  _bs((tr, CONV_W), lambda i: (i, CONV_COL0 + 1)),
                  _bs((tr, CONV_W), lambda i: (i, CONV_COL0 + 2)),
                  _bs((8, CONV_W), prev(CONV_COL0 + 1)),
                  _bs((8, CONV_W), prev(CONV_COL0 + 2)),
                  _bs((3, CONV_W), lambda i: (0, 0))],
        out_specs=[_bs((tr, CONV_W), lambda i: (i, 0)), _bs((tr, CONV_W), lambda i: (i, 0))],
        out_shape=[_sds((seq, CONV_W), BF16), _sds((seq, CONV_W), F32)],
        compiler_params=_cparams(1),
    )(proj, proj, proj, proj, proj, w_conv)


def _conv_bwd(proj, de, d, w_conv, seq, tr):
    hb = tr // 8
    nblk = seq // tr

    def body(cb_ref, cc_ref, cu_ref, ccp_ref, cup_ref, cbn_ref, de_ref, den_ref, d_ref, w_ref, o_ref, dw_ref):
        i = pl.program_id(0)
        cc, cu, cb = cc_ref[...], cu_ref[...], cb_ref[...]
        u = cc * cu
        up = jnp.where(i > 0, ccp_ref[...] * cup_ref[...], 0.0)
        u1 = _shift_rows_down(u, up, 1)
        u2 = _shift_rows_down(u, up, 2)
        de_ = de_ref[...]
        dd = de_ * cb
        ddn = jnp.where(i < nblk - 1, den_ref[...] * cbn_ref[...], 0.0)
        w = w_ref[...]
        du = w[2:3, :] * dd + w[1:2, :] * _shift_rows_up(dd, ddn, 1) + w[0:1, :] * _shift_rows_up(dd, ddn, 2)
        o_ref[:, 0:CONV_W] = (de_ * d_ref[...]).astype(BF16)
        o_ref[:, CONV_W:2 * CONV_W] = (du * cu).astype(BF16)
        o_ref[:, 2 * CONV_W:3 * CONV_W] = (du * cc).astype(BF16)
        ridx = lax.broadcasted_iota(jnp.int32, (8, CONV_W), 0)
        dw0 = jnp.sum(dd * u2, axis=0, keepdims=True)
        dw1 = jnp.sum(dd * u1, axis=0, keepdims=True)
        dw2 = jnp.sum(dd * u, axis=0, keepdims=True)
        dw_ref[...] = jnp.where(ridx == 0, dw0, jnp.where(ridx == 1, dw1, jnp.where(ridx == 2, dw2, 0.0)))

    prev = lambda c: (lambda i: (jnp.maximum(i * hb - 1, 0), c))
    nxt = lambda c: (lambda i: (jnp.minimum((i + 1) * hb, seq // 8 - 1), c))
    return pl.pallas_call(
        body, name="conv_bwd", grid=(nblk,),
        in_specs=[_bs((tr, CONV_W), lambda i: (i, CONV_COL0)),
                  _bs((tr, CONV_W), lambda i: (i, CONV_COL0 + 1)),
                  _bs((tr, CONV_W), lambda i: (i, CONV_COL0 + 2)),
                  _bs((8, CONV_W), prev(CONV_COL0 + 1)),
                  _bs((8, CONV_W), prev(CONV_COL0 + 2)),
                  _bs((8, CONV_W), nxt(CONV_COL0)),
                  _bs((tr, CONV_W), lambda i: (i, 0)),
                  _bs((8, CONV_W), nxt(0)),
                  _bs((tr, CONV_W), lambda i: (i, 0)),
                  _bs((3, CONV_W), lambda i: (0, 0))],
        out_specs=[_bs((tr, 3 * CONV_W), lambda i: (i, 0)), _bs((None, 8, CONV_W), lambda i: (i, 0, 0))],
        out_shape=[_sds((seq, 3 * CONV_W), BF16), _sds((nblk, 8, CONV_W), F32)],
        compiler_params=_cparams(1),
    )(proj, proj, proj, proj, proj, proj, de, de, d, w_conv)


def _nt(a, b):
    return lax.dot_general(a, b, _DIMS["nt"], preferred_element_type=F32)


def _tn(a, b):
    return lax.dot_general(a, b, _DIMS["tn"], preferred_element_type=F32)


def _nn(a, b):
    return lax.dot_general(a, b, _DIMS["nn"], preferred_element_type=F32)


def _log_gates(z):
    lse = jnp.log(1.0 + jnp.exp(-jnp.abs(z)))
    log_beta = jnp.minimum(z, 0.0) - lse
    return log_beta, log_beta - z


def _attn_fwd(qkv, seq):
    blk = ATT_BLK
    nq = seq // blk
    npair = N_HEADS // 2

    def body(q_ref, k_ref, v_ref, o_ref, lta_ref, ltb_ref):
        i = pl.program_id(1)
        is_a = lax.broadcasted_iota(jnp.int32, (1, LANES), 1) < HEAD_DIM
        q2 = q_ref[...]
        zero = jnp.zeros_like(q2)
        qs = (jnp.where(is_a, q2, zero), jnp.where(is_a, zero, q2))
        row = lax.broadcasted_iota(jnp.int32, (blk, blk), 0)
        col = lax.broadcasted_iota(jnp.int32, (blk, blk), 1)
        tri = (row > col).astype(BF16)
        causal = col < row

        def tile(j, carry, diag):
            tot = [carry[0], carry[1]]
            acc = carry[2]
            off = pl.multiple_of(j * blk, blk)
            k2 = k_ref[pl.ds(off, blk), :]
            v2 = v_ref[pl.ds(off, blk), :]
            zs = [_nt(qs[h], k2) for h in range(2)]
            gates = [_log_gates(z) for z in zs]
            keeps = [jnp.where(causal, g[1], 0.0) if diag else g[1] for g in gates]
            sums = [_nn(lk.astype(BF16), tri) for lk in keeps]
            pv = []
            for h in range(2):
                w = jnp.exp(gates[h][0] + (tot[h] + sums[h]))
                if diag:
                    w = jnp.where(causal, w, 0.0)
                pv.append(_nn(w.astype(BF16), v2))
                tot[h] = tot[h] + jnp.sum(keeps[h], axis=-1, keepdims=True)
            return tot[0], tot[1], acc + jnp.where(is_a, pv[0], pv[1])

        init = (jnp.zeros((blk, 1), F32), jnp.zeros((blk, 1), F32), jnp.zeros((blk, LANES), F32))
        carry = tile(i, init, True)
        carry = lax.fori_loop(0, i, lambda jj, c: tile(i - 1 - jj, c, False), carry)
        o_ref[...] = carry[2].astype(BF16)
        lta_ref[...] = jnp.broadcast_to(carry[0], (blk, LANES))
        ltb_ref[...] = jnp.broadcast_to(carry[1], (blk, LANES))

    return pl.pallas_call(
        body, name="attn_fwd", grid=(npair, nq),
        in_specs=[_bs((blk, LANES), lambda p, i: (i, p)),
                  _bs((seq, LANES), lambda p, i: (0, npair + p)),
                  _bs((seq, LANES), lambda p, i: (0, 2 * npair + p))],
        out_specs=[_bs((blk, LANES), lambda p, i: (i, p))] * 3,
        out_shape=[_sds((seq, ATTN_W), BF16), _sds((seq, ATTN_W), F32), _sds((seq, ATTN_W), F32)],
        compiler_params=_cparams(2),
    )(qkv, qkv, qkv)


def _attn_bwd(qkv, do, lta, ltb, seq):
    blk = ATT_BLK
    nq = seq // blk
    npair = N_HEADS // 2

    def body(q_ref, k_ref, v_ref, do_ref, lta_ref, ltb_ref, dq_ref, dk_ref, dv_ref):
        i = pl.program_id(1)

        @pl.when(i == 0)
        def _():
            dk_ref[...] = jnp.zeros_like(dk_ref)
            dv_ref[...] = jnp.zeros_like(dv_ref)

        is_a = lax.broadcasted_iota(jnp.int32, (1, LANES), 1) < HEAD_DIM
        q2 = q_ref[...]
        do2 = do_ref[...]
        zero = jnp.zeros_like(q2)
        qs = (jnp.where(is_a, q2, zero), jnp.where(is_a, zero, q2))
        dos = (jnp.where(is_a, do2, zero), jnp.where(is_a, zero, do2))
        ltot = (jnp.max(lta_ref[...], axis=-1, keepdims=True), jnp.max(ltb_ref[...], axis=-1, keepdims=True))
        row = lax.broadcasted_iota(jnp.int32, (blk, blk), 0)
        col = lax.broadcasted_iota(jnp.int32, (blk, blk), 1)
        tri_after = (row > col).astype(BF16)
        tri_excl = (row < col).astype(BF16)
        causal = col < row

        def tile(j, carry, diag):
            cum = [carry[0], carry[1]]
            pre = [carry[2], carry[3]]
            dq = carry[4]
            off = pl.multiple_of(j * blk, blk)
            k2 = k_ref[pl.ds(off, blk), :]
            v2 = v_ref[pl.ds(off, blk), :]
            zs = [_nt(qs[h], k2) for h in range(2)]
            dws = [_nt(dos[h], v2) for h in range(2)]
            gates = [_log_gates(z) for z in zs]
            keeps = [jnp.where(causal, g[1], 0.0) if diag else g[1] for g in gates]
            sums = [_nn(lk.astype(BF16), tri_after) for lk in keeps]
            ws, gs = [], []
            for h in range(2):
                cum[h] = cum[h] + jnp.sum(keeps[h], axis=-1, keepdims=True)
                w = jnp.exp(gates[h][0] + ((ltot[h] - cum[h]) + sums[h]))
                if diag:
                    w = jnp.where(causal, w, 0.0)
                ws.append(w)
                gs.append(dws[h] * w)
            befores = [_nn(g.astype(BF16), tri_excl) for g in gs]
            dqs, dks, dvs = [], [], []
            for h in range(2):
                beta = jnp.exp(gates[h][0])
                dz = gs[h] * (1.0 - beta) - (pre[h] + befores[h]) * beta
                if diag:
                    dz = jnp.where(causal, dz, 0.0)
                dzb = dz.astype(BF16)
                dqs.append(_nn(dzb, k2))
                dks.append(_tn(dzb, q2))
                dvs.append(_tn(ws[h].astype(BF16), do2))
                pre[h] = pre[h] + jnp.sum(gs[h], axis=-1, keepdims=True)
            dk_ref[pl.ds(off, blk), :] += jnp.where(is_a, dks[0], dks[1])
            dv_ref[pl.ds(off, blk), :] += jnp.where(is_a, dvs[0], dvs[1])
            return cum[0], cum[1], pre[0], pre[1], dq + jnp.where(is_a, dqs[0], dqs[1])

        zc = jnp.zeros((blk, 1), F32)
        init = (zc, zc, zc, zc, jnp.zeros((blk, LANES), F32))
        carry = lax.fori_loop(0, i, lambda j, c: tile(j, c, False), init)
        carry = tile(i, carry, True)
        dq_ref[...] = carry[4] * Q_SCALE

    qmap = lambda p, i: (i, p)
    return pl.pallas_call(
        body, name="attn_bwd", grid=(npair, nq),
        in_specs=[_bs((blk, LANES), qmap),
                  _bs((seq, LANES), lambda p, i: (0, npair + p)),
                  _bs((seq, LANES), lambda p, i: (0, 2 * npair + p)),
                  _bs((blk, LANES), qmap), _bs((blk, LANES), qmap), _bs((blk, LANES), qmap)],
        out_specs=[_bs((blk, LANES), qmap),
                   _bs((seq, LANES), lambda p, i: (0, p)),
                   _bs((seq, LANES), lambda p, i: (0, p))],
        out_shape=[_sds((seq, ATTN_W), F32)] * 3,
        compiler_params=_cparams(2),
    )(qkv, qkv, qkv, do, lta, ltb)


def _elementwise(name, fn, ins, out_dtypes):
    rows, cols = ins[0].shape
    tr = rows
    for cand in (512, 256, 128, 64, 32, 16, 8):
        if rows % cand == 0 and cand * cols * 4 <= 2 * 1024 * 1024:
            tr = cand
            break
    n_in = len(ins)

    def body(*refs):
        res = fn(*[r[...] for r in refs[:n_in]])
        for r, val in zip(refs[n_in:], res):
            r[...] = val.astype(r.dtype)

    spec = _bs((tr, cols), lambda i: (i, 0))
    return pl.pallas_call(
        body, name=name, grid=(rows // tr,),
        in_specs=[spec] * n_in, out_specs=[spec] * len(out_dtypes),
        out_shape=[_sds((rows, cols), dt) for dt in out_dtypes],
        compiler_params=_cparams(1),
    )(*ins)


def _adamw_fn(w, g, m, v):
    m = ADAM_B1 * m + (1.0 - ADAM_B1) * g
    v = ADAM_B2 * v + (1.0 - ADAM_B2) * (g * g)
    m_hat = m / (1.0 - ADAM_B1 ** ADAM_STEP)
    v_hat = v / (1.0 - ADAM_B2 ** ADAM_STEP)
    delta = -ADAM_LR * (m_hat / (jnp.sqrt(v_hat) + ADAM_EPS) + ADAM_WD * w)
    return delta, m, v


def _adamw(name, w, g, m, v):
    shape = w.shape
    as2d = lambda a: a.reshape(-1, shape[-1])
    delta, nm, nv = _elementwise(name, _adamw_fn, [as2d(w), as2d(g), as2d(m), as2d(v)], [F32, F32, F32])
    return delta.reshape(shape), nm.reshape(shape), nv.reshape(shape)


def _place():
    x, y, c = lax.axis_index("x"), lax.axis_index("y"), lax.axis_index("c")
    chips = [(1 - x, y), (x, 1 - y), (1 - x, 1 - y)]
    return x, y, c, chips


ANY = pl.BlockSpec(memory_space=pl.ANY)


def _row_chunks(rows, row_bytes):
    per = max(16, DMA_CHUNK_BYTES // row_bytes // 16 * 16)
    return [(s, min(per, rows - s)) for s in range(0, rows, per)]


def _start_chunked(make, rows, row_bytes):
    for s, m in _row_chunks(rows, row_bytes):
        make(s, m).start()


def _row_bytes(ref):
    return ref.shape[-1] * jnp.dtype(ref.dtype).itemsize


def _allgather_weights(shards):
    n = len(shards)

    def body(*refs):
        src, dst = refs[:n], refs[n:2 * n]
        send_sems, recv_sems, local_sems = refs[2 * n:]
        x, y, c, chips = _place()
        me, sibling, mychip = (x, y, c), (x, y, 1 - c), 2 * x + y

        def remote(w, k, src_rows, dst_rows, to):
            return lambda s, m: pltpu.make_async_remote_copy(
                src_ref=src_rows(s, m), dst_ref=dst_rows(s, m), send_sem=send_sems.at[w, k],
                recv_sem=recv_sems.at[w, k], device_id=to, device_id_type=MESH)

        def piece(w, chip, half):
            hr = src[w].shape[0] // 2
            return lambda s, m: dst[w].at[chip, pl.ds(half * hr + s, m)]

        sends, local = [], []
        for w in range(n):
            rows, hr, rb = src[w].shape[0], src[w].shape[0] // 2, _row_bytes(src[w])
            own = lambda s, m, w=w: pltpu.make_async_copy(src[w].at[pl.ds(s, m)], dst[w].at[mychip, pl.ds(s, m)],
                                                          local_sems.at[w])
            _start_chunked(own, rows, rb)
            local.append((own, rows))
            for r, (cx, cy) in enumerate(chips):
                mk = remote(w, r, lambda s, m, w=w, hr=hr: src[w].at[pl.ds(c * hr + s, m)], piece(w, mychip, c),
                            (cx, cy, c))
                _start_chunked(mk, hr, rb)
                sends.append((mk, hr))
        for w in range(n):
            hr, rb = src[w].shape[0] // 2, _row_bytes(src[w])
            for r, (cx, cy) in enumerate(chips):
                landed = piece(w, 2 * cx + cy, c)
                remote(w, r, landed, landed, me)(0, hr).wait_recv()
                fwd = remote(w, 3 + r, landed, landed, sibling)
                _start_chunked(fwd, hr, rb)
                sends.append((fwd, hr))
        for w in range(n):
            hr = src[w].shape[0] // 2
            for r, (cx, cy) in enumerate(chips):
                from_sib = piece(w, 2 * cx + cy, 1 - c)
                remote(w, 3 + r, from_sib, from_sib, me)(0, hr).wait_recv()
        for own, rows in local:
            own(0, rows).wait()
        for mk, rows in sends:
            mk(0, rows).wait_send()

    return pl.pallas_call(
        body, name="allgather_weights",
        in_specs=[ANY] * n, out_specs=[ANY] * n,
        out_shape=[_sds((N_CHIPS,) + s.shape, s.dtype) for s in shards],
        scratch_shapes=[pltpu.SemaphoreType.DMA((n, 6)), pltpu.SemaphoreType.DMA((n, 6)),
                        pltpu.SemaphoreType.DMA((n,))],
    )(*shards)


def _rs_pair_swap(grads):
    n = len(grads)

    def body(*refs):
        g, mine, theirs = refs[:n], refs[n:2 * n], refs[2 * n:3 * n]
        send_sems, recv_sems, local_sems = refs[3 * n:]
        x, y, c, _ = _place()
        sibling = (x, y, 1 - c)
        copies = []
        for w in range(n):
            hr, rb = g[w].shape[1] // 2, _row_bytes(g[w])
            for k in range(N_CHIPS):
                keep = lambda s, m, w=w, k=k, hr=hr: pltpu.make_async_copy(
                    g[w].at[k, pl.ds(c * hr + s, m)], mine[w].at[k, pl.ds(s, m)], local_sems.at[w, k])
                give = lambda s, m, w=w, k=k, hr=hr: pltpu.make_async_remote_copy(
                    src_ref=g[w].at[k, pl.ds((1 - c) * hr + s, m)], dst_ref=theirs[w].at[k, pl.ds(s, m)],
                    send_sem=send_sems.at[w, k], recv_sem=recv_sems.at[w, k], device_id=sibling, device_id_type=MESH)
                _start_chunked(give, hr, rb)
                _start_chunked(keep, hr, rb)
                copies.append((keep, give, hr))
        for keep, give, hr in copies:
            give(0, hr).wait_recv()
        for keep, give, hr in copies:
            give(0, hr).wait_send()
            keep(0, hr).wait()

    half = [_sds((N_CHIPS, a.shape[1] // 2, a.shape[2]), a.dtype) for a in grads]
    sems = pltpu.SemaphoreType.DMA((n, N_CHIPS))
    res = pl.pallas_call(
        body, name="rs_pair_swap",
        in_specs=[ANY] * n, out_specs=[ANY] * (2 * n), out_shape=half + half,
        scratch_shapes=[sems, sems, sems],
    )(*grads)
    return res[:n], res[n:]


def _rs_chip_exchange(parts):
    n = len(parts)

    def body(*refs):
        t, got = refs[:n], refs[n:2 * n]
        send_sems, recv_sems, local_sems = refs[2 * n:]
        x, y, c, chips = _place()
        mychip = 2 * x + y
        local, remote = [], []
        for w in range(n):
            hr, rb = t[w].shape[1], _row_bytes(t[w])
            own = lambda s, m, w=w: pltpu.make_async_copy(t[w].at[mychip, pl.ds(s, m)], got[w].at[3, pl.ds(s, m)],
                                                          local_sems.at[w])
            _start_chunked(own, hr, rb)
            local.append((own, hr))
            for r, (cx, cy) in enumerate(chips):
                cp = lambda s, m, w=w, r=r, cx=cx, cy=cy: pltpu.make_async_remote_copy(
                    src_ref=t[w].at[2 * cx + cy, pl.ds(s, m)], dst_ref=got[w].at[r, pl.ds(s, m)],
                    send_sem=send_sems.at[w, r], recv_sem=recv_sems.at[w, r], device_id=(cx, cy, c),
                    device_id_type=MESH)
                _start_chunked(cp, hr, rb)
                remote.append((cp, hr))
        for cp, hr in remote:
            cp(0, hr).wait()
        for own, hr in local:
            own(0, hr).wait()

    return pl.pallas_call(
        body, name="rs_chip_exchange",
        in_specs=[ANY] * n, out_specs=[ANY] * n, out_shape=[_sds(a.shape, a.dtype) for a in parts],
        scratch_shapes=[pltpu.SemaphoreType.DMA((n, 3)), pltpu.SemaphoreType.DMA((n, 3)),
                        pltpu.SemaphoreType.DMA((n,))],
    )(*parts)


def _rs_join_halves(halves):
    n = len(halves)

    def body(*refs):
        f, full = refs[:n], refs[n:2 * n]
        send_sems, recv_sems, local_sems = refs[2 * n:]
        x, y, c, _ = _place()
        sibling = (x, y, 1 - c)
        copies = []
        for w in range(n):
            hr, rb = f[w].shape[0], _row_bytes(f[w])
            own = lambda s, m, w=w, hr=hr: pltpu.make_async_copy(
                f[w].at[pl.ds(s, m)], full[w].at[pl.ds(c * hr + s, m)], local_sems.at[w])
            give = lambda s, m, w=w, hr=hr: pltpu.make_async_remote_copy(
                src_ref=f[w].at[pl.ds(s, m)], dst_ref=full[w].at[pl.ds(c * hr + s, m)], send_sem=send_sems.at[w],
                recv_sem=recv_sems.at[w], device_id=sibling, device_id_type=MESH)
            _start_chunked(give, hr, rb)
            _start_chunked(own, hr, rb)
            copies.append((own, give, hr))
        for w, (own, give, hr) in enumerate(copies):
            theirs = full[w].at[pl.ds((1 - c) * hr, hr)]
            pltpu.make_async_remote_copy(src_ref=theirs, dst_ref=theirs, send_sem=send_sems.at[w],
                                         recv_sem=recv_sems.at[w], device_id=sibling, device_id_type=MESH).wait_recv()
        for own, give, hr in copies:
            give(0, hr).wait_send()
            own(0, hr).wait()

    return pl.pallas_call(
        body, name="rs_join_halves",
        in_specs=[ANY] * n, out_specs=[ANY] * n,
        out_shape=[_sds((2 * a.shape[0], a.shape[1]), a.dtype) for a in halves],
        scratch_shapes=[pltpu.SemaphoreType.DMA((n,)), pltpu.SemaphoreType.DMA((n,)), pltpu.SemaphoreType.DMA((n,))],
    )(*halves)


def _small_allreduce(loss_p, dg_parts, dbg_a, dbg_c, dwc):
    ins = [loss_p] + list(dg_parts) + [dbg_a, dbg_c, dwc]
    n_in = len(ins)
    vmem = pl.BlockSpec(memory_space=pltpu.VMEM)

    def body(*refs):
        in_refs = refs[:n_in]
        out_ref, vec, buf, send_sems, recv_sems = refs[n_in:]
        x, y, c, _ = _place()
        me = 4 * x + 2 * y + c
        vec[...] = jnp.zeros_like(vec)
        vec[0:1, :] = jnp.sum(in_refs[0][...], axis=0)
        for r in range(5):
            vec[1 + r:2 + r, :] = jnp.sum(in_refs[1 + r][...], axis=0)
        vec[6:7, :] = jnp.sum(in_refs[6][...], axis=0)
        vec[7:8, :] = jnp.sum(in_refs[7][...], axis=0)
        vec[8:16, 0:CONV_W] = jnp.sum(in_refs[8][...], axis=0)
        buf[pl.ds(me, 1)] = vec[...][None]
        copies = []
        for r in range(1, 8):
            fx, fy, fc = (r >> 2) & 1, (r >> 1) & 1, r & 1
            to = (1 - x if fx else x, 1 - y if fy else y, 1 - c if fc else c)
            cp = pltpu.make_async_remote_copy(src_ref=vec, dst_ref=buf.at[me], send_sem=send_sems.at[r - 1],
                                              recv_sem=recv_sems.at[r - 1], device_id=to, device_id_type=MESH)
            cp.start()
            copies.append(cp)
        for cp in copies:
            cp.wait()
        total = buf[0]
        for s in range(1, 8):
            total = total + buf[s]
        out_ref[...] = total
        out_ref[0:1, :] = jnp.broadcast_to(jnp.sum(total[0:1, :], axis=-1, keepdims=True), (1, D_MODEL))

    return pl.pallas_call(
        body, name="small_allreduce",
        in_specs=[vmem] * n_in, out_specs=vmem, out_shape=_sds((SMALL_ROWS, D_MODEL), F32),
        scratch_shapes=[pltpu.VMEM((SMALL_ROWS, D_MODEL), F32), pltpu.VMEM((8, SMALL_ROWS, D_MODEL), F32),
                        pltpu.SemaphoreType.DMA((7,)), pltpu.SemaphoreType.DMA((7,))],
    )(*ins)


def _local_step(x, p, tgt, g, b_gate, w_conv, wf):
    seq = x.shape[0]
    tm = min(seq, 1024)
    th = min(seq, 512)
    ni, nh = seq // tm, seq // th
    g_pre_mix, g_post_mix, g_pre_mlp, g_post_mlp, g_ple = g
    w_in, w_ao, w_co, w_o, w_up, w_down, w_pg, w_pp = wf
    D = D_MODEL
    vec = lambda a, blk=0: (a, _bs((1, D), lambda i, j, k: (0, blk)))
    rows_i = lambda a, t, blk=0: (a, _bs((t, D), lambda i, j, k: (i, blk)))
    rows_k = lambda a, t, blk=0: (a, _bs((t, D), lambda i, j, k: (k, blk)))
    part = lambda n: (_sds((n, 1, D), F32), _bs((None, 1, D), lambda i, j, k: (i, 0, 0)))
    full2 = lambda a: (a, _bs(a.shape, lambda i, j, k: (0, 0)))

    (proj,) = _mm("proj_in", "nn", (ni, 4, 1),
                  a_ins=[rows_i(x, tm), vec(g_pre_mix)], a_fn=lambda xb, gb: _rms(xb, gb).astype(BF16),
                  b_ins=[(w_in, _bs((None, D, 1280), lambda i, j, k: (j, 0, 0)))], b_fn=_ident,
                  outs=[(_sds((seq, D_IN), F32), _bs((tm, 1280), lambda i, j, k: (i, j)))],
                  acc_shape=(tm, 1280), a_cache=((tm, D), BF16))
    qkv = _qkv_cast(proj, seq, tm)
    o, lta, ltb = _attn_fwd(qkv, seq)
    (y_attn,) = _mm("attn_out", "nn", (ni, 1, 1),
                    a_ins=[(o, _bs((tm, ATTN_W), lambda i, j, k: (i, 0)))], a_fn=_ident,
                    b_ins=[full2(w_ao)], b_fn=_ident,
                    outs=[(_sds((seq, D), F32), _bs((tm, D), lambda i, j, k: (i, 0)))], acc_shape=(tm, D))
    e, d = _conv_fwd(proj, w_conv, seq, tm)
    (y_conv,) = _mm("conv_out", "nn", (ni, 1, 1),
                    a_ins=[(e, _bs((tm, CONV_W), lambda i, j, k: (i, 0)))], a_fn=_ident,
                    b_ins=[full2(w_co)], b_fn=_ident,
                    outs=[(_sds((seq, D), F32), _bs((tm, D), lambda i, j, k: (i, 0)))], acc_shape=(tm, D))

    def mix_fn(ga, gc, ya, yc, ba, bc):
        return (_sig(ga + ba) * ya + _sig(gc + bc) * yc).astype(BF16)

    def post_mix(acc, xb, gb):
        return acc, xb + _rms(acc, gb)

    mix_ins = lambda rows: [rows(proj, th, 3), rows(proj, th, 4), rows(y_attn, th), rows(y_conv, th),
                            vec(b_gate, 0), vec(b_gate, 1)]
    mixed, x1 = _mm("mix_out", "nn", (nh, 1, 1),
                    a_ins=mix_ins(rows_i), a_fn=mix_fn, b_ins=[full2(w_o)], b_fn=_ident,
                    epi_ins=[rows_i(x, th), vec(g_post_mix)], epi_fn=post_mix,
                    outs=[(_sds((seq, D), F32), _bs((th, D), lambda i, j, k: (i, 0)))] * 2,
                    acc_shape=(th, D), a_cache=((th, D), BF16))
    (up,) = _mm("mlp_up", "nn", (ni, 4, 1),
                a_ins=[rows_i(x1, tm), vec(g_pre_mlp)], a_fn=lambda xb, gb: _rms(xb, gb).astype(BF16),
                b_ins=[(w_up, _bs((None, D, D), lambda i, j, k: (j, 0, 0)))], b_fn=_ident,
                outs=[(_sds((seq, D_FF), F32), _bs((tm, D), lambda i, j, k: (i, j)))],
                acc_shape=(tm, D), a_cache=((tm, D), BF16))

    def relu2(ub):
        r = jnp.maximum(ub, 0.0)
        return (r * r).astype(BF16)

    f, x2 = _mm("mlp_down", "nn", (nh, 1, 4),
                a_ins=[(up, _bs((th, D), lambda i, j, k: (i, k)))], a_fn=relu2,
                b_ins=[(w_down, _bs((D, D), lambda i, j, k: (k, 0)))], b_fn=_ident,
                epi_ins=[rows_i(x1, th), vec(g_post_mlp)], epi_fn=post_mix,
                outs=[(_sds((seq, D), F32), _bs((th, D), lambda i, j, k: (i, 0)))] * 2, acc_shape=(th, D))
    (pp,) = _mm("ple_proj", "nn", (ni, 1, 1),
                a_ins=[(p, _bs((tm, PLE_DIM), lambda i, j, k: (i, 0)))], a_fn=_to_bf16,
                b_ins=[full2(w_pp)], b_fn=_ident,
                outs=[(_sds((seq, D), F32), _bs((tm, D), lambda i, j, k: (i, 0)))], acc_shape=(tm, D))

    def head(acc, x2b, ppb, tb):
        pg = _sig(acc)
        err = x2b + pg * ppb - tb
        return pg, err * (1.0 / D), jnp.sum(err * err, axis=0, keepdims=True) * (0.5 / D)

    pg, dx3, loss_p = _mm("ple_gate_loss", "nn", (nh, 1, 1),
                          a_ins=[rows_i(x2, th), vec(g_ple)], a_fn=lambda xb, gb: _rms(xb, gb).astype(BF16),
                          b_ins=[full2(w_pg)], b_fn=_ident,
                          epi_ins=[rows_i(x2, th), rows_i(pp, th), rows_i(tgt, th)], epi_fn=head,
                          outs=[(_sds((seq, D), F32), _bs((th, D), lambda i, j, k: (i, 0)))] * 2 + [part(nh)],
                          acc_shape=(th, D), a_cache=((th, D), BF16))

    (dw_pp,) = _mm("dw_ple_proj", "tn", (1, 1, nh),
                   a_ins=[(p, _bs((th, PLE_DIM), lambda i, j, k: (k, 0)))], a_fn=_to_bf16,
                   b_ins=[rows_k(dx3, th), rows_k(pg, th)], b_fn=lambda a, b: (a * b).astype(BF16),
                   outs=[(_sds((PLE_DIM, D), F32), _bs((PLE_DIM, D), lambda i, j, k: (0, 0)))],
                   acc_shape=(PLE_DIM, D))

    def dpre_fn(dx3b, ppb, pgb):
        return (dx3b * ppb * pgb * (1.0 - pgb)).astype(BF16)

    def ple_norm_bwd(acc, x2b, dx3b, gb):
        dxn, dg = _rms_bwd(x2b, gb, acc)
        return dx3b + dxn, dg

    dx2, dg_ple_p, dpre = _mm("d_ple_gate", "nt", (nh, 1, 1),
                              a_ins=[rows_i(dx3, th), rows_i(pp, th), rows_i(pg, th)],
                              a_fn=lambda a, b, c: (dpre_fn(a, b, c),) * 2,
                              b_ins=[full2(w_pg)], b_fn=_ident,
                              epi_ins=[rows_i(x2, th), rows_i(dx3, th), vec(g_ple)], epi_fn=ple_norm_bwd,
                              outs=[(_sds((seq, D), F32), _bs((th, D), lambda i, j, k: (i, 0))), part(nh)],
                              acc_shape=(th, D), a_cache=((th, D), BF16),
                              a_outs=[(_sds((seq, D), BF16), _bs((th, D), lambda i, j, k: (i, 0)))])
    (dw_pg,) = _mm("dw_ple_gate", "tn", (1, 1, nh),
                   a_ins=[rows_k(x2, th), vec(g_ple)], a_fn=lambda xb, gb: _rms(xb, gb).astype(BF16),
                   b_ins=[rows_k(dpre, th)], b_fn=_ident,
                   outs=[(_sds((D, D), F32), _bs((D, D), lambda i, j, k: (0, 0)))], acc_shape=(D, D))

    def df_fn(fb, dx2b, gb):
        dfb, dg = _rms_bwd(fb, gb, dx2b)
        dfb = dfb.astype(BF16)
        return dfb, dfb, dg

    def dup_fn(acc, ub):
        return (acc * (2.0 * jnp.maximum(ub, 0.0)),)

    dup, df, dg_post_mlp_p = _mm("d_mlp_down", "nt", (nh, 4, 1),
                                 a_ins=[rows_i(f, th), rows_i(dx2, th), vec(g_post_mlp)], a_fn=df_fn,
                                 b_ins=[(w_down, _bs((D, D), lambda i, j, k: (j, 0)))], b_fn=_ident,
                                 epi_ins=[(up, _bs((th, D), lambda i, j, k: (i, j)))], epi_fn=dup_fn,
                                 outs=[(_sds((seq, D_FF), BF16), _bs((th, D), lambda i, j, k: (i, j)))],
                                 acc_shape=(th, D), a_cache=((th, D), BF16),
                                 a_outs=[(_sds((seq, D), BF16), _bs((th, D), lambda i, j, k: (i, 0))), part(nh)])
    (dw_down,) = _mm("dw_mlp_down", "tn", (4, 1, nh),
                     a_ins=[(up, _bs((th, D), lambda i, j, k: (k, i)))], a_fn=relu2,
                     b_ins=[rows_k(df, th)], b_fn=_ident,
                     outs=[(_sds((D_FF, D), F32), _bs((D, D), lambda i, j, k: (i, 0)))], acc_shape=(D, D))
    (dw_up,) = _mm("dw_mlp_up", "tn", (1, 4, nh),
                   a_ins=[rows_k(x1, th), vec(g_pre_mlp)], a_fn=lambda xb, gb: _rms(xb, gb).astype(BF16),
                   b_ins=[(dup, _bs((th, D), lambda i, j, k: (k, j)))], b_fn=_ident,
                   outs=[(_sds((N_CHIPS, D, D), F32), _bs((None, D, D), lambda i, j, k: (j, 0, 0)))],
                   acc_shape=(D, D))

    def mlp_norm_bwd(acc, x1b, dx2b, mixedb, g_mlp, g_mix):
        dxn, dg_mlp = _rms_bwd(x1b, g_mlp, acc)
        dx1b = dx2b + dxn
        dmixedb, dg_mix = _rms_bwd(mixedb, g_mix, dx1b)
        return dx1b, dmixedb, dg_mlp, dg_mix

    dx1, dmixed, dg_pre_mlp_p, dg_post_mix_p = _mm(
        "d_mlp_up", "nt", (nh, 1, 4),
        a_ins=[(dup, _bs((th, D), lambda i, j, k: (i, k)))], a_fn=_ident,
        b_ins=[(w_up, _bs((None, D, D), lambda i, j, k: (k, 0, 0)))], b_fn=_ident,
        epi_ins=[rows_i(x1, th), rows_i(dx2, th), rows_i(mixed, th), vec(g_pre_mlp), vec(g_post_mix)],
        epi_fn=mlp_norm_bwd,
        outs=[(_sds((seq, D), F32), _bs((th, D), lambda i, j, k: (i, 0))),
              (_sds((seq, D), BF16), _bs((th, D), lambda i, j, k: (i, 0))), part(nh), part(nh)],
        acc_shape=(th, D))
    (dw_o,) = _mm("dw_mix_out", "tn", (1, 1, nh),
                  a_ins=mix_ins(rows_k), a_fn=mix_fn, b_ins=[rows_k(dmixed, th)], b_fn=_ident,
                  outs=[(_sds((D, D), F32), _bs((D, D), lambda i, j, k: (0, 0)))], acc_shape=(D, D))

    def gate_bwd(acc, ga, gc, ya, yc, ba, bc):
        sa, sc = _sig(ga + ba), _sig(gc + bc)
        dga = acc * ya * sa * (1.0 - sa)
        dgc = acc * yc * sc * (1.0 - sc)
        return (acc * sa, acc * sc, jnp.concatenate([dga, dgc], axis=1),
                jnp.sum(dga, axis=0, keepdims=True), jnp.sum(dgc, axis=0, keepdims=True))

    dya, dyc, dgate, dbg_a_p, dbg_c_p = _mm(
        "d_mix_out", "nt", (nh, 1, 1),
        a_ins=[rows_i(dmixed, th)], a_fn=_ident, b_ins=[full2(w_o)], b_fn=_ident,
        epi_ins=mix_ins(rows_i), epi_fn=gate_bwd,
        outs=[(_sds((seq, D), BF16), _bs((th, D), lambda i, j, k: (i, 0)))] * 2
             + [(_sds((seq, 2 * D), BF16), _bs((th, 2 * D), lambda i, j, k: (i, 0))), part(nh), part(nh)],
        acc_shape=(th, D))
    (dw_ao,) = _mm("dw_attn_out", "tn", (1, 1, nh),
                   a_ins=[(o, _bs((th, ATTN_W), lambda i, j, k: (k, 0)))], a_fn=_ident,
                   b_ins=[rows_k(dya, th)], b_fn=_ident,
                   outs=[(_sds((ATTN_W, D), F32), _bs((ATTN_W, D), lambda i, j, k: (0, 0)))], acc_shape=(ATTN_W, D))
    (do,) = _mm("d_attn_out", "nt", (ni, 1, 1),
                a_ins=[rows_i(dya, tm)], a_fn=_ident, b_ins=[full2(w_ao)], b_fn=_ident,
                outs=[(_sds((seq, ATTN_W), BF16), _bs((tm, ATTN_W), lambda i, j, k: (i, 0)))],
                acc_shape=(tm, ATTN_W))
    dq, dk, dv = _attn_bwd(qkv, do, lta, ltb, seq)
    (dw_co,) = _mm("dw_conv_out", "tn", (1, 1, nh),
                   a_ins=[(e, _bs((th, CONV_W), lambda i, j, k: (k, 0)))], a_fn=_ident,
                   b_ins=[rows_k(dyc, th)], b_fn=_ident,
                   outs=[(_sds((CONV_W, D), F32), _bs((CONV_W, D), lambda i, j, k: (0, 0)))], acc_shape=(CONV_W, D))
    (de,) = _mm("d_conv_out", "nt", (ni, 1, 1),
                a_ins=[rows_i(dyc, tm)], a_fn=_ident, b_ins=[full2(w_co)], b_fn=_ident,
                outs=[(_sds((seq, CONV_W), F32), _bs((tm, CONV_W), lambda i, j, k: (i, 0)))],
                acc_shape=(tm, CONV_W))
    dconv, dwc_p = _conv_bwd(proj, de, d, w_conv, seq, tm)
    dproj = jnp.concatenate([dq.astype(BF16), dk.astype(BF16), dv.astype(BF16), dconv, dgate], axis=1)
    (dw_in,) = _mm("dw_proj_in", "tn", (1, 4, nh),
                   a_ins=[rows_k(x, th), vec(g_pre_mix)], a_fn=lambda xb, gb: _rms(xb, gb).astype(BF16),
                   b_ins=[(dproj, _bs((th, 1280), lambda i, j, k: (k, j)))], b_fn=_ident,
                   outs=[(_sds((N_CHIPS, D, 1280), F32), _bs((None, D, 1280), lambda i, j, k: (j, 0, 0)))],
                   acc_shape=(D, 1280))

    def in_norm_bwd(acc, xb, dx1b, gb):
        dxn, dg = _rms_bwd(xb, gb, acc)
        return dx1b + dxn, dg

    grad_x, dg_pre_mix_p = _mm("d_proj_in", "nt", (nh, 1, 4),
                               a_ins=[(dproj, _bs((th, 1280), lambda i, j, k: (i, k)))], a_fn=_ident,
                               b_ins=[(w_in, _bs((None, D, 1280), lambda i, j, k: (k, 0, 0)))], b_fn=_ident,
                               epi_ins=[rows_i(x, th), rows_i(dx1, th), vec(g_pre_mix)], epi_fn=in_norm_bwd,
                               outs=[(_sds((seq, D), F32), _bs((th, D), lambda i, j, k: (i, 0))), part(nh)],
                               acc_shape=(th, D))

    chip_major = lambda a: a.reshape(a.shape[0], N_CHIPS, a.shape[1] // N_CHIPS).transpose(1, 0, 2)
    big = [dw_in, chip_major(dw_ao), chip_major(dw_co), dw_o.reshape(N_CHIPS, D // N_CHIPS, D), dw_up,
           dw_down.reshape(N_CHIPS, D_FF // N_CHIPS, D), dw_pg.reshape(N_CHIPS, D // N_CHIPS, D), chip_major(dw_pp)]
    small = (loss_p, [dg_pre_mix_p, dg_post_mix_p, dg_pre_mlp_p, dg_post_mlp_p, dg_ple_p], dbg_a_p, dbg_c_p, dwc_p)
    return grad_x, big, small


def _add2(a, b):
    return (a + b,)


def _add4(a, b, c, d):
    f = lambda v: v.astype(F32)
    return (((f(a) + f(b)) + f(c)) + f(d),)


def _reduce_scatter(big):
    mine, theirs = _rs_pair_swap(big)
    pair = []
    for w, (a, b) in enumerate(zip(mine, theirs)):
        shape = a.shape
        (s,) = _elementwise(f"rs_add_pair_{w}", _add2, [a.reshape(-1, shape[-1]), b.reshape(-1, shape[-1])], [BF16])
        pair.append(s.reshape(shape))
    got = _rs_chip_exchange(pair)
    halves = []
    for w, a in enumerate(got):
        (s,) = _elementwise(f"rs_add_chips_{w}", _add4, [a[3], a[0], a[1], a[2]], [F32])
        halves.append(s)
    return _rs_join_halves(halves)


def kernel(x, p, g_pre_mix, w_in, b_gate, w_conv, w_attn_out, w_conv_out, w_o, g_post_mix, g_pre_mlp, w_up, w_down, g_post_mlp, g_ple, w_ple_gate, w_ple_proj, loss_target, m_g_pre_mix, m_w_in, m_b_gate, m_w_conv, m_w_attn_out, m_w_conv_out, m_w_o, m_g_post_mix, m_g_pre_mlp, m_w_up, m_w_down, m_g_post_mlp, m_g_ple, m_w_ple_gate, m_w_ple_proj, v_g_pre_mix, v_w_in, v_b_gate, v_w_conv, v_w_attn_out, v_w_conv_out, v_w_o, v_g_post_mix, v_g_pre_mlp, v_w_up, v_w_down, v_g_post_mlp, v_g_ple, v_w_ple_gate, v_w_ple_proj):
    mats = [w_in, w_attn_out, w_conv_out, w_o, w_up, w_down, w_ple_gate, w_ple_proj]
    mats_m = [m_w_in, m_w_attn_out, m_w_conv_out, m_w_o, m_w_up, m_w_down, m_w_ple_gate, m_w_ple_proj]
    mats_v = [v_w_in, v_w_attn_out, v_w_conv_out, v_w_o, v_w_up, v_w_down, v_w_ple_gate, v_w_ple_proj]
    gains = [g_pre_mix, g_post_mix, g_pre_mlp, g_post_mlp, g_ple]
    gains_m = [m_g_pre_mix, m_g_post_mix, m_g_pre_mlp, m_g_post_mlp, m_g_ple]
    gains_v = [v_g_pre_mix, v_g_post_mix, v_g_pre_mlp, v_g_post_mlp, v_g_ple]

    taps = jnp.concatenate([w_conv[0], jnp.zeros((CONV_PAD_ROWS - 3, LANES), F32)], axis=0)
    gathered = _allgather_weights([w[0].astype(BF16) for w in mats] + [taps])
    cols_joined = lambda a: a.transpose(1, 0, 2).reshape(a.shape[1], N_CHIPS * a.shape[2])
    rows_joined = lambda a: a.reshape(N_CHIPS * a.shape[1], a.shape[2])
    wf = [gathered[0], cols_joined(gathered[1]), cols_joined(gathered[2]), rows_joined(gathered[3]), gathered[4],
          rows_joined(gathered[5]), rows_joined(gathered[6]), cols_joined(gathered[7])]
    w_conv_full = cols_joined(gathered[8])[0:3, :]
    chip = 2 * lax.axis_index("x") + lax.axis_index("y")

    grad_x, big, small = _local_step(x[0], p[0, 0], loss_target[0], gains, b_gate, w_conv_full, wf)

    shard_grads = _reduce_scatter(big)
    red = _small_allreduce(*small)
    loss = red[0, 0]
    grad_gains = [red[1 + r:2 + r, :] for r in range(5)]
    grad_b_gate = jnp.concatenate([red[6:7, :], red[7:8, :]], axis=1)
    grad_w_conv = lax.dynamic_slice(red[8:11, :], (0, chip * LANES), (3, LANES))[None]

    grads_big = [gr.reshape(w.shape) for gr, w in zip(shard_grads, mats)]
    upd_big = [_adamw(f"adamw_{i}", w, gr, m, v) for i, (w, gr, m, v) in enumerate(zip(mats, grads_big, mats_m, mats_v))]
    pack = lambda vs, bg: jnp.concatenate(list(vs) + [bg.reshape(2, D_MODEL), jnp.zeros((1, D_MODEL), F32)], axis=0)
    upd_small = _adamw("adamw_small", pack(gains, b_gate), pack(grad_gains, grad_b_gate),
                       pack(gains_m, m_b_gate), pack(gains_v, v_b_gate))
    upd_conv = _adamw("adamw_conv", w_conv, grad_w_conv, m_w_conv, v_w_conv)

    def small_out(a, which):
        gains_out = [a[r:r + 1, :] for r in range(5)]
        return gains_out, a[5:7, :].reshape(1, 2 * D_MODEL)

    def ordered(g_pre_mix_, big_, b_gate_, conv_, g_rest):
        return [g_pre_mix_, big_[0], b_gate_, conv_, big_[1], big_[2], big_[3], g_rest[0], g_rest[1], big_[4], big_[5],
                g_rest[2], g_rest[3], big_[6], big_[7]]

    outs = [loss, grad_x[None]]
    outs += ordered(grad_gains[0], grads_big, grad_b_gate, grad_w_conv, grad_gains[1:])
    for which in range(3):
        g_out, b_out = small_out(upd_small[which], which)
        outs += ordered(g_out[0], [u[which] for u in upd_big], b_out, upd_conv[which], g_out[1:])
    return tuple(outs)
```

```python
import functools

import jax
import jax.numpy as jnp
from jax import lax
from jax.experimental import pallas as pl
from jax.experimental.pallas import tpu as pltpu

F32 = jnp.float32
BF16 = jnp.bfloat16
MESH = pl.DeviceIdType.MESH

D_MODEL = 1024
N_HEADS = 8
HEAD_DIM = 64
ATTN_W = N_HEADS * HEAD_DIM
CONV_W = 512
D_FF = 4096
PLE_DIM = 256
D_IN = 5120
N_CHIPS = 4
EPS = 1e-6
Q_SCALE = HEAD_DIM ** -0.5

ADAM_LR = 0.001
ADAM_B1 = 0.9
ADAM_B2 = 0.999
ADAM_EPS = 1e-08
ADAM_WD = 0.01
ADAM_STEP = 10

V7X_VMEM_BYTES = 64 * 1024 * 1024
VMEM_LIMIT = V7X_VMEM_BYTES - 8 * 1024 * 1024
LANES = 128
ATT_BLK = 256
SMALL_ROWS = 16
CONV_PAD_ROWS = 16


def _cparams(n_grid):
    return pltpu.CompilerParams(dimension_semantics=("arbitrary",) * n_grid, vmem_limit_bytes=VMEM_LIMIT)


def _bs(shape, fn):
    return pl.BlockSpec(shape, fn)


def _rms_stats(xf):
    return lax.rsqrt(jnp.mean(xf * xf, axis=-1, keepdims=True) + EPS)


def _rms(xf, g):
    return xf * _rms_stats(xf) * g


def _rms_bwd(xf, g, dy):
    r = _rms_stats(xf)
    xh = xf * r
    dyg = dy * g
    dx = r * (dyg - xh * jnp.mean(dyg * xh, axis=-1, keepdims=True))
    return dx, jnp.sum(dy * xh, axis=0, keepdims=True)


def _sig(z):
    return 1.0 / (1.0 + jnp.exp(-z))


def _ident(a):
    return a


def _to_bf16(a):
    return a.astype(BF16)


_DIMS = {"nn": (((1,), (0,)), ((), ())), "nt": (((1,), (1,)), ((), ())), "tn": (((0,), (0,)), ((), ()))}


def _mm(name, mode, grid, a_ins, a_fn, b_ins, b_fn, outs, acc_shape, epi_ins=(), epi_fn=None,
        a_cache=None, a_outs=()):
    nk = grid[2]
    na, nb, ne, no, nao = len(a_ins), len(b_ins), len(epi_ins), len(outs), len(a_outs)
    assert a_cache is None or nk == 1
    assert not a_outs or a_cache is not None
    dims = _DIMS[mode]
    if epi_fn is None:
        epi_fn = lambda acc: (acc,)

    def body(*refs):
        a_refs = refs[:na]
        b_refs = refs[na:na + nb]
        e_refs = refs[na + nb:na + nb + ne]
        o_refs = refs[na + nb + ne:na + nb + ne + no]
        ao_refs = refs[na + nb + ne + no:na + nb + ne + no + nao]
        scratch = list(refs[na + nb + ne + no + nao:])
        acc_ref = scratch.pop(0) if nk > 1 else None
        a_sc = scratch.pop(0) if a_cache is not None else None
        j = pl.program_id(1)
        k = pl.program_id(2)

        def finish(acc):
            res = epi_fn(acc, *[r[...] for r in e_refs])
            for r, val in zip(o_refs, res):
                r[...] = val.astype(r.dtype)

        if a_sc is not None:
            @pl.when(j == 0)
            def _():
                res = a_fn(*[r[...] for r in a_refs])
                if nao:
                    for r, val in zip(ao_refs, res[1:]):
                        r[...] = val.astype(r.dtype)
                    res = res[0]
                a_sc[...] = res
            a = a_sc[...]
        else:
            a = a_fn(*[r[...] for r in a_refs])
        b = b_fn(*[r[...] for r in b_refs])
        prod = lax.dot_general(a, b, dims, preferred_element_type=F32)
        if nk == 1:
            finish(prod)
        else:
            @pl.when(k == 0)
            def _():
                acc_ref[...] = prod

            @pl.when(k > 0)
            def _():
                acc_ref[...] += prod

            @pl.when(k == nk - 1)
            def _():
                finish(acc_ref[...])

    scratch_shapes = []
    if nk > 1:
        scratch_shapes.append(pltpu.VMEM(acc_shape, F32))
    if a_cache is not None:
        scratch_shapes.append(pltpu.VMEM(*a_cache))
    all_outs = list(outs) + list(a_outs)
    res = pl.pallas_call(
        body, name=name, grid=grid,
        in_specs=[s for _, s in a_ins] + [s for _, s in b_ins] + [s for _, s in epi_ins],
        out_specs=[s for _, s in all_outs],
        out_shape=[o for o, _ in all_outs],
        scratch_shapes=scratch_shapes,
        compiler_params=_cparams(3),
    )(*[a for a, _ in a_ins], *[a for a, _ in b_ins], *[a for a, _ in epi_ins])
    return res


def _sds(shape, dtype):
    return jax.ShapeDtypeStruct(shape, dtype)


def _qkv_cast(proj, seq, tr):
    def body(p_ref, o_ref):
        scale = jnp.where(pl.program_id(1) == 0, Q_SCALE, 1.0).astype(F32)
        o_ref[...] = (p_ref[...] * scale).astype(BF16)

    return pl.pallas_call(
        body, name="qkv_cast", grid=(seq // tr, 3),
        in_specs=[_bs((tr, ATTN_W), lambda i, c: (i, c))],
        out_specs=_bs((tr, ATTN_W), lambda i, c: (i, c)),
        out_shape=_sds((seq, 3 * ATTN_W), BF16),
        compiler_params=_cparams(2),
    )(proj)


def _shift_rows_down(u, prev, n):
    rows = u.shape[0]
    ridx = lax.broadcasted_iota(jnp.int32, u.shape, 0)
    out = pltpu.roll(u, n, 0)
    for r in range(n):
        out = jnp.where(ridx == r, prev[8 - n + r:8 - n + r + 1, :], out)
    del rows
    return out


def _shift_rows_up(u, nxt, n):
    rows = u.shape[0]
    ridx = lax.broadcasted_iota(jnp.int32, u.shape, 0)
    out = pltpu.roll(u, rows - n, 0)
    for r in range(n):
        out = jnp.where(ridx == rows - n + r, nxt[r:r + 1, :], out)
    return out


CONV_COL0 = 3


def _conv_fwd(proj, w_conv, seq, tr):
    hb = tr // 8

    def body(cb_ref, cc_ref, cu_ref, ccp_ref, cup_ref, w_ref, e_ref, d_ref):
        i = pl.program_id(0)
        u = cc_ref[...] * cu_ref[...]
        up = jnp.where(i > 0, ccp_ref[...] * cup_ref[...], 0.0)
        w = w_ref[...]
        d = w[0:1, :] * _shift_rows_down(u, up, 2) + w[1:2, :] * _shift_rows_down(u, up, 1) + w[2:3, :] * u
        d_ref[...] = d
        e_ref[...] = (cb_ref[...] * d).astype(BF16)

    prev = lambda c: (lambda i: (jnp.maximum(i * hb - 1, 0), c))
    return pl.pallas_call(
        body, name="conv_fwd", grid=(seq // tr,),
        in_specs=[_bs((tr, CONV_W), lambda i: (i, CONV_COL0)),
                  _bs((tr, CONV_W), lambda i: (i, CONV_COL0 + 1)),
                  _bs((tr, CONV_W), lambda i: (i, CONV_COL0 + 2)),
                  _bs((8, CONV_W), prev(CONV_COL0 + 1)),
                  _bs((8, CONV_W), prev(CONV_COL0 + 2)),
                  _bs((3, CONV_W), lambda i: (0, 0))],
        out_specs=[_bs((tr, CONV_W), lambda i: (i, 0)), _bs((tr, CONV_W), lambda i: (i, 0))],
        out_shape=[_sds((seq, CONV_W), BF16), _sds((seq, CONV_W), F32)],
        compiler_params=_cparams(1),
    )(proj, proj, proj, proj, proj, w_conv)


def _conv_bwd(proj, de, d, w_conv, seq, tr):
    hb = tr // 8
    nblk = seq // tr

    def body(cb_ref, cc_ref, cu_ref, ccp_ref, cup_ref, cbn_ref, de_ref, den_ref, d_ref, w_ref, o_ref, dw_ref):
        i = pl.program_id(0)
        cc, cu, cb = cc_ref[...], cu_ref[...], cb_ref[...]
        u = cc * cu
        up = jnp.where(i > 0, ccp_ref[...] * cup_ref[...], 0.0)
        u1 = _shift_rows_down(u, up, 1)
        u2 = _shift_rows_down(u, up, 2)
        de_ = de_ref[...]
        dd = de_ * cb
        ddn = jnp.where(i < nblk - 1, den_ref[...] * cbn_ref[...], 0.0)
        w = w_ref[...]
        du = w[2:3, :] * dd + w[1:2, :] * _shift_rows_up(dd, ddn, 1) + w[0:1, :] * _shift_rows_up(dd, ddn, 2)
        o_ref[:, 0:CONV_W] = (de_ * d_ref[...]).astype(BF16)
        o_ref[:, CONV_W:2 * CONV_W] = (du * cu).astype(BF16)
        o_ref[:, 2 * CONV_W:3 * CONV_W] = (du * cc).astype(BF16)
        ridx = lax.broadcasted_iota(jnp.int32, (8, CONV_W), 0)
        dw0 = jnp.sum(dd * u2, axis=0, keepdims=True)
        dw1 = jnp.sum(dd * u1, axis=0, keepdims=True)
        dw2 = jnp.sum(dd * u, axis=0, keepdims=True)
        dw_ref[...] = jnp.where(ridx == 0, dw0, jnp.where(ridx == 1, dw1, jnp.where(ridx == 2, dw2, 0.0)))

    prev = lambda c: (lambda i: (jnp.maximum(i * hb - 1, 0), c))
    nxt = lambda c: (lambda i: (jnp.minimum((i + 1) * hb, seq // 8 - 1), c))
    return pl.pallas_call(
        body, name="conv_bwd", grid=(nblk,),
        in_specs=[_bs((tr, CONV_W), lambda i: (i, CONV_COL0)),
                  _bs((tr, CONV_W), lambda i: (i, CONV_COL0 + 1)),
                  _bs((tr, CONV_W), lambda i: (i, CONV_COL0 + 2)),
                  _bs((8, CONV_W), prev(CONV_COL0 + 1)),
                  _bs((8, CONV_W), prev(CONV_COL0 + 2)),
                  _bs((8, CONV_W), nxt(CONV_COL0)),
                  _bs((tr, CONV_W), lambda i: (i, 0)),
                  _bs((8, CONV_W), nxt(0)),
                  _bs((tr, CONV_W), lambda i: (i, 0)),
                  _bs((3, CONV_W), lambda i: (0, 0))],
        out_specs=[_bs((tr, 3 * CONV_W), lambda i: (i, 0)), _bs((None, 8, CONV_W), lambda i: (i, 0, 0))],
        out_shape=[_sds((seq, 3 * CONV_W), BF16), _sds((nblk, 8, CONV_W), F32)],
        compiler_params=_cparams(1),
    )(proj, proj, proj, proj, proj, proj, de, de, d, w_conv)


def _nt(a, b):
    return lax.dot_general(a, b, _DIMS["nt"], preferred_element_type=F32)


def _tn(a, b):
    return lax.dot_general(a, b, _DIMS["tn"], preferred_element_type=F32)


def _nn(a, b):
    return lax.dot_general(a, b, _DIMS["nn"], preferred_element_type=F32)


def _log_gates(z):
    lse = jnp.log(1.0 + jnp.exp(-jnp.abs(z)))
    log_beta = jnp.minimum(z, 0.0) - lse
    return log_beta, log_beta - z


def _attn_fwd(qkv, seq):
    blk = ATT_BLK
    nq = seq // blk
    npair = N_HEADS // 2

    def body(q_ref, k_ref, v_ref, o_ref, lta_ref, ltb_ref):
        i = pl.program_id(1)
        is_a = lax.broadcasted_iota(jnp.int32, (1, LANES), 1) < HEAD_DIM
        q2 = q_ref[...]
        zero = jnp.zeros_like(q2)
        qs = (jnp.where(is_a, q2, zero), jnp.where(is_a, zero, q2))
        row = lax.broadcasted_iota(jnp.int32, (blk, blk), 0)
        col = lax.broadcasted_iota(jnp.int32, (blk, blk), 1)
        tri = (row > col).astype(BF16)
        causal = col < row

        def tile(j, carry, diag):
            tot = [carry[0], carry[1]]
            acc = carry[2]
            off = pl.multiple_of(j * blk, blk)
            k2 = k_ref[pl.ds(off, blk), :]
            v2 = v_ref[pl.ds(off, blk), :]
            zs = [_nt(qs[h], k2) for h in range(2)]
            gates = [_log_gates(z) for z in zs]
            keeps = [jnp.where(causal, g[1], 0.0) if diag else g[1] for g in gates]
            sums = [_nn(lk.astype(BF16), tri) for lk in keeps]
            pv = []
            for h in range(2):
                w = jnp.exp(gates[h][0] + (tot[h] + sums[h]))
                if diag:
                    w = jnp.where(causal, w, 0.0)
                pv.append(_nn(w.astype(BF16), v2))
                tot[h] = tot[h] + jnp.sum(keeps[h], axis=-1, keepdims=True)
            return tot[0], tot[1], acc + jnp.where(is_a, pv[0], pv[1])

        init = (jnp.zeros((blk, 1), F32), jnp.zeros((blk, 1), F32), jnp.zeros((blk, LANES), F32))
        carry = tile(i, init, True)
        carry = lax.fori_loop(0, i, lambda jj, c: tile(i - 1 - jj, c, False), carry)
        o_ref[...] = carry[2].astype(BF16)
        lta_ref[...] = jnp.broadcast_to(carry[0], (blk, LANES))
        ltb_ref[...] = jnp.broadcast_to(carry[1], (blk, LANES))

    return pl.pallas_call(
        body, name="attn_fwd", grid=(npair, nq),
        in_specs=[_bs((blk, LANES), lambda p, i: (i, p)),
                  _bs((seq, LANES), lambda p, i: (0, npair + p)),
                  _bs((seq, LANES), lambda p, i: (0, 2 * npair + p))],
        out_specs=[_bs((blk, LANES), lambda p, i: (i, p))] * 3,
        out_shape=[_sds((seq, ATTN_W), BF16), _sds((seq, ATTN_W), F32), _sds((seq, ATTN_W), F32)],
        compiler_params=_cparams(2),
    )(qkv, qkv, qkv)


def _attn_bwd(qkv, do, lta, ltb, seq):
    blk = ATT_BLK
    nq = seq // blk
    npair = N_HEADS // 2

    def body(q_ref, k_ref, v_ref, do_ref, lta_ref, ltb_ref, dq_ref, dk_ref, dv_ref):
        i = pl.program_id(1)

        @pl.when(i == 0)
        def _():
            dk_ref[...] = jnp.zeros_like(dk_ref)
            dv_ref[...] = jnp.zeros_like(dv_ref)

        is_a = lax.broadcasted_iota(jnp.int32, (1, LANES), 1) < HEAD_DIM
        q2 = q_ref[...]
        do2 = do_ref[...]
        zero = jnp.zeros_like(q2)
        qs = (jnp.where(is_a, q2, zero), jnp.where(is_a, zero, q2))
        dos = (jnp.where(is_a, do2, zero), jnp.where(is_a, zero, do2))
        ltot = (jnp.max(lta_ref[...], axis=-1, keepdims=True), jnp.max(ltb_ref[...], axis=-1, keepdims=True))
        row = lax.broadcasted_iota(jnp.int32, (blk, blk), 0)
        col = lax.broadcasted_iota(jnp.int32, (blk, blk), 1)
        tri_after = (row > col).astype(BF16)
        tri_excl = (row < col).astype(BF16)
        causal = col < row

        def tile(j, carry, diag):
            cum = [carry[0], carry[1]]
            pre = [carry[2], carry[3]]
            dq = carry[4]
            off = pl.multiple_of(j * blk, blk)
            k2 = k_ref[pl.ds(off, blk), :]
            v2 = v_ref[pl.ds(off, blk), :]
            zs = [_nt(qs[h], k2) for h in range(2)]
            dws = [_nt(dos[h], v2) for h in range(2)]
            gates = [_log_gates(z) for z in zs]
            keeps = [jnp.where(causal, g[1], 0.0) if diag else g[1] for g in gates]
            sums = [_nn(lk.astype(BF16), tri_after) for lk in keeps]
            ws, gs = [], []
            for h in range(2):
                cum[h] = cum[h] + jnp.sum(keeps[h], axis=-1, keepdims=True)
                w = jnp.exp(gates[h][0] + ((ltot[h] - cum[h]) + sums[h]))
                if diag:
                    w = jnp.where(causal, w, 0.0)
                ws.append(w)
                gs.append(dws[h] * w)
            befores = [_nn(g.astype(BF16), tri_excl) for g in gs]
            dqs, dks, dvs = [], [], []
            for h in range(2):
                beta = jnp.exp(gates[h][0])
                dz = gs[h] * (1.0 - beta) - (pre[h] + befores[h]) * beta
                if diag:
                    dz = jnp.where(causal, dz, 0.0)
                dzb = dz.astype(BF16)
                dqs.append(_nn(dzb, k2))
                dks.append(_tn(dzb, q2))
                dvs.append(_tn(ws[h].astype(BF16), do2))
                pre[h] = pre[h] + jnp.sum(gs[h], axis=-1, keepdims=True)
            dk_ref[pl.ds(off, blk), :] += jnp.where(is_a, dks[0], dks[1])
            dv_ref[pl.ds(off, blk), :] += jnp.where(is_a, dvs[0], dvs[1])
            return cum[0], cum[1], pre[0], pre[1], dq + jnp.where(is_a, dqs[0], dqs[1])

        zc = jnp.zeros((blk, 1), F32)
        init = (zc, zc, zc, zc, jnp.zeros((blk, LANES), F32))
        carry = lax.fori_loop(0, i, lambda j, c: tile(j, c, False), init)
        carry = tile(i, carry, True)
        dq_ref[...] = carry[4] * Q_SCALE

    qmap = lambda p, i: (i, p)
    return pl.pallas_call(
        body, name="attn_bwd", grid=(npair, nq),
        in_specs=[_bs((blk, LANES), qmap),
                  _bs((seq, LANES), lambda p, i: (0, npair + p)),
                  _bs((seq, LANES), lambda p, i: (0, 2 * npair + p)),
                  _bs((blk, LANES), qmap), _bs((blk, LANES), qmap), _bs((blk, LANES), qmap)],
        out_specs=[_bs((blk, LANES), qmap),
                   _bs((seq, LANES), lambda p, i: (0, p)),
                   _bs((seq, LANES), lambda p, i: (0, p))],
        out_shape=[_sds((seq, ATTN_W), F32)] * 3,
        compiler_params=_cparams(2),
    )(qkv, qkv, qkv, do, lta, ltb)


def _elementwise(name, fn, ins, out_dtypes):
    rows, cols = ins[0].shape
    tr = rows
    for cand in (512, 256, 128, 64, 32, 16, 8):
        if rows % cand == 0 and cand * cols * 4 <= 2 * 1024 * 1024:
            tr = cand
            break
    n_in = len(ins)

    def body(*refs):
        res = fn(*[r[...] for r in refs[:n_in]])
        for r, val in zip(refs[n_in:], res):
            r[...] = val.astype(r.dtype)

    spec = _bs((tr, cols), lambda i: (i, 0))
    return pl.pallas_call(
        body, name=name, grid=(rows // tr,),
        in_specs=[spec] * n_in, out_specs=[spec] * len(out_dtypes),
        out_shape=[_sds((rows, cols), dt) for dt in out_dtypes],
        compiler_params=_cparams(1),
    )(*ins)


def _adamw_fn(w, g, m, v):
    m = ADAM_B1 * m + (1.0 - ADAM_B1) * g
    v = ADAM_B2 * v + (1.0 - ADAM_B2) * (g * g)
    m_hat = m / (1.0 - ADAM_B1 ** ADAM_STEP)
    v_hat = v / (1.0 - ADAM_B2 ** ADAM_STEP)
    delta = -ADAM_LR * (m_hat / (jnp.sqrt(v_hat) + ADAM_EPS) + ADAM_WD * w)
    return delta, m, v


def _adamw(name, w, g, m, v):
    shape = w.shape
    as2d = lambda a: a.reshape(-1, shape[-1])
    delta, nm, nv = _elementwise(name, _adamw_fn, [as2d(w), as2d(g), as2d(m), as2d(v)], [F32, F32, F32])
    return delta.reshape(shape), nm.reshape(shape), nv.reshape(shape)


def _place():
    x, y, c = lax.axis_index("x"), lax.axis_index("y"), lax.axis_index("c")
    chips = [(1 - x, y), (x, 1 - y), (1 - x, 1 - y)]
    return x, y, c, chips


ANY = pl.BlockSpec(memory_space=pl.ANY)
VMEM_WHOLE = pl.BlockSpec(memory_space=pltpu.VMEM)


def _allgather_weights(shards):
    n = len(shards)

    def body(*refs):
        src, dst = refs[:n], refs[n:2 * n]
        send_sems, recv_sems, local_sems = refs[2 * n:]
        x, y, c, chips = _place()
        me, sibling, mychip = (x, y, c), (x, y, 1 - c), 2 * x + y

        def piece(w, chip, half):
            hr = src[w].shape[0] // 2
            return dst[w].at[chip, pl.ds(half * hr, hr)]

        def copy(w, k, src_ref, dst_ref, to):
            return pltpu.make_async_remote_copy(src_ref=src_ref, dst_ref=dst_ref, send_sem=send_sems.at[w, k],
                                                recv_sem=recv_sems.at[w, k], device_id=to, device_id_type=MESH)

        started, local = [], []
        for w in range(n):
            hr = src[w].shape[0] // 2
            own = pltpu.make_async_copy(src[w], dst[w].at[mychip], local_sems.at[w])
            own.start()
            local.append(own)
            for r, (cx, cy) in enumerate(chips):
                cp = copy(w, r, src[w].at[pl.ds(c * hr, hr)], piece(w, mychip, c), (cx, cy, c))
                cp.start()
                started.append(cp)
        for w in range(n):
            for r, (cx, cy) in enumerate(chips):
                landed = piece(w, 2 * cx + cy, c)
                copy(w, r, landed, landed, me).wait_recv()
                fwd = copy(w, 3 + r, landed, landed, sibling)
                fwd.start()
                started.append(fwd)
        for w in range(n):
            for r, (cx, cy) in enumerate(chips):
                from_sib = piece(w, 2 * cx + cy, 1 - c)
                copy(w, 3 + r, from_sib, from_sib, me).wait_recv()
        for cp in local:
            cp.wait()
        for cp in started:
            cp.wait_send()

    return pl.pallas_call(
        body, name="allgather_weights",
        in_specs=[VMEM_WHOLE] * n, out_specs=[VMEM_WHOLE] * n,
        out_shape=[_sds((N_CHIPS,) + s.shape, s.dtype) for s in shards],
        scratch_shapes=[pltpu.SemaphoreType.DMA((n, 6)), pltpu.SemaphoreType.DMA((n, 6)),
                        pltpu.SemaphoreType.DMA((n,))],
        compiler_params=pltpu.CompilerParams(vmem_limit_bytes=VMEM_LIMIT),
    )(*shards)


SUM_ROWS = 64


def _rs_pair_sum(name, grads):
    n = len(grads)

    def body(*refs):
        g, out = refs[:n], refs[n:2 * n]
        stage, land, keep = refs[2 * n:3 * n], refs[3 * n:4 * n], refs[4 * n:5 * n]
        send_sems, recv_sems, stage_sems, keep_sems = refs[5 * n:]
        x, y, c, _ = _place()
        sibling = (x, y, 1 - c)
        loads = []
        for w in range(n):
            hr = g[w].shape[1] // 2
            st = pltpu.make_async_copy(g[w].at[:, pl.ds((1 - c) * hr, hr)], stage[w], stage_sems.at[w])
            kp = pltpu.make_async_copy(g[w].at[:, pl.ds(c * hr, hr)], keep[w], keep_sems.at[w])
            st.start()
            kp.start()
            loads.append((st, kp))
        gives = []
        for w in range(n):
            loads[w][0].wait()
            give = pltpu.make_async_remote_copy(src_ref=stage[w], dst_ref=land[w], send_sem=send_sems.at[w],
                                                recv_sem=recv_sems.at[w], device_id=sibling, device_id_type=MESH)
            give.start()
            gives.append(give)
        for w in range(n):
            loads[w][1].wait()
            gives[w].wait_recv()
            nb = g[w].shape[1] // 2 // SUM_ROWS

            def add(idx, carry, w=w, nb=nb):
                k, r = idx // nb, pl.multiple_of((idx % nb) * SUM_ROWS, SUM_ROWS)
                rows = pl.ds(r, SUM_ROWS)
                out[w][k, rows, :] = (keep[w][k, rows, :] + land[w][k, rows, :]).astype(BF16)
                return carry

            lax.fori_loop(0, N_CHIPS * nb, add, 0)
        for give in gives:
            give.wait_send()

    half = [(N_CHIPS, a.shape[1] // 2, a.shape[2]) for a in grads]
    bufs = [pltpu.VMEM(s, F32) for s in half]
    sems = pltpu.SemaphoreType.DMA((n,))
    return pl.pallas_call(
        body, name=name,
        in_specs=[ANY] * n, out_specs=[VMEM_WHOLE] * n, out_shape=[_sds(s, BF16) for s in half],
        scratch_shapes=bufs + bufs + bufs + [sems, sems, sems, sems],
        compiler_params=pltpu.CompilerParams(vmem_limit_bytes=VMEM_LIMIT),
    )(*grads)


def _rs_exchange_join(parts):
    n = len(parts)

    def body(*refs):
        t, full, got = refs[:n], refs[n:2 * n], refs[2 * n:3 * n]
        send_sems, recv_sems = refs[3 * n:]
        x, y, c, chips = _place()
        mychip, sibling = 2 * x + y, (x, y, 1 - c)
        sends = []
        for w in range(n):
            for r, (cx, cy) in enumerate(chips):
                cp = pltpu.make_async_remote_copy(src_ref=t[w].at[2 * cx + cy], dst_ref=got[w].at[r],
                                                  send_sem=send_sems.at[w, r], recv_sem=recv_sems.at[w, r],
                                                  device_id=(cx, cy, c), device_id_type=MESH)
                cp.start()
                sends.append(cp)
        for w in range(n):
            hr = t[w].shape[1]
            for r in range(3):
                pltpu.make_async_remote_copy(src_ref=got[w].at[r], dst_ref=got[w].at[r], send_sem=send_sems.at[w, r],
                                             recv_sem=recv_sems.at[w, r], device_id=sibling,
                                             device_id_type=MESH).wait_recv()

            def add(idx, carry, w=w, hr=hr):
                r = pl.multiple_of(idx * SUM_ROWS, SUM_ROWS)
                rows = pl.ds(r, SUM_ROWS)
                f = lambda v: v.astype(F32)
                total = ((f(t[w][mychip, rows, :]) + f(got[w][0, rows, :])) + f(got[w][1, rows, :])) \
                    + f(got[w][2, rows, :])
                full[w][pl.ds(pl.multiple_of(c * hr + r, SUM_ROWS), SUM_ROWS), :] = total
                return carry

            lax.fori_loop(0, hr // SUM_ROWS, add, 0)
            mine = full[w].at[pl.ds(c * hr, hr)]
            give = pltpu.make_async_remote_copy(src_ref=mine, dst_ref=mine, send_sem=send_sems.at[w, 3],
                                                recv_sem=recv_sems.at[w, 3], device_id=sibling, device_id_type=MESH)
            give.start()
            sends.append(give)
        for w in range(n):
            hr = t[w].shape[1]
            theirs = full[w].at[pl.ds((1 - c) * hr, hr)]
            pltpu.make_async_remote_copy(src_ref=theirs, dst_ref=theirs, send_sem=send_sems.at[w, 3],
                                         recv_sem=recv_sems.at[w, 3], device_id=sibling, device_id_type=MESH).wait_recv()
        for cp in sends:
            cp.wait_send()

    return pl.pallas_call(
        body, name="rs_exchange_join",
        in_specs=[VMEM_WHOLE] * n, out_specs=[VMEM_WHOLE] * n,
        out_shape=[_sds((2 * a.shape[1], a.shape[2]), F32) for a in parts],
        scratch_shapes=[pltpu.VMEM((3,) + a.shape[1:], a.dtype) for a in parts]
        + [pltpu.SemaphoreType.DMA((n, 4)), pltpu.SemaphoreType.DMA((n, 4))],
        compiler_params=pltpu.CompilerParams(vmem_limit_bytes=VMEM_LIMIT),
    )(*parts)


def _small_allreduce(loss_p, dg_parts, dbg_a, dbg_c, dwc):
    ins = [loss_p] + list(dg_parts) + [dbg_a, dbg_c, dwc]
    n_in = len(ins)
    vmem = pl.BlockSpec(memory_space=pltpu.VMEM)

    def body(*refs):
        in_refs = refs[:n_in]
        out_ref, vec, buf, send_sems, recv_sems = refs[n_in:]
        x, y, c, _ = _place()
        me = 4 * x + 2 * y + c
        vec[...] = jnp.zeros_like(vec)
        vec[0:1, :] = jnp.sum(in_refs[0][...], axis=0)
        for r in range(5):
            vec[1 + r:2 + r, :] = jnp.sum(in_refs[1 + r][...], axis=0)
        vec[6:7, :] = jnp.sum(in_refs[6][...], axis=0)
        vec[7:8, :] = jnp.sum(in_refs[7][...], axis=0)
        vec[8:16, 0:CONV_W] = jnp.sum(in_refs[8][...], axis=0)
        buf[pl.ds(me, 1)] = vec[...][None]
        copies = []
        for r in range(1, 8):
            fx, fy, fc = (r >> 2) & 1, (r >> 1) & 1, r & 1
            to = (1 - x if fx else x, 1 - y if fy else y, 1 - c if fc else c)
            cp = pltpu.make_async_remote_copy(src_ref=vec, dst_ref=buf.at[me], send_sem=send_sems.at[r - 1],
                                              recv_sem=recv_sems.at[r - 1], device_id=to, device_id_type=MESH)
            cp.start()
            copies.append(cp)
        for cp in copies:
            cp.wait()
        total = buf[0]
        for s in range(1, 8):
            total = total + buf[s]
        out_ref[...] = total
        out_ref[0:1, :] = jnp.broadcast_to(jnp.sum(total[0:1, :], axis=-1, keepdims=True), (1, D_MODEL))

    return pl.pallas_call(
        body, name="small_allreduce",
        in_specs=[vmem] * n_in, out_specs=vmem, out_shape=_sds((SMALL_ROWS, D_MODEL), F32),
        scratch_shapes=[pltpu.VMEM((SMALL_ROWS, D_MODEL), F32), pltpu.VMEM((8, SMALL_ROWS, D_MODEL), F32),
                        pltpu.SemaphoreType.DMA((7,)), pltpu.SemaphoreType.DMA((7,))],
    )(*ins)


def _local_step(x, p, tgt, g, b_gate, w_conv, wf):
    seq = x.shape[0]
    tm = min(seq, 1024)
    th = min(seq, 512)
    ni, nh = seq // tm, seq // th
    g_pre_mix, g_post_mix, g_pre_mlp, g_post_mlp, g_ple = g
    w_in, w_ao, w_co, w_o, w_up, w_down, w_pg, w_pp = wf
    D = D_MODEL
    vec = lambda a, blk=0: (a, _bs((1, D), lambda i, j, k: (0, blk)))
    rows_i = lambda a, t, blk=0: (a, _bs((t, D), lambda i, j, k: (i, blk)))
    rows_k = lambda a, t, blk=0: (a, _bs((t, D), lambda i, j, k: (k, blk)))
    part = lambda n: (_sds((n, 1, D), F32), _bs((None, 1, D), lambda i, j, k: (i, 0, 0)))
    full2 = lambda a: (a, _bs(a.shape, lambda i, j, k: (0, 0)))

    (proj,) = _mm("proj_in", "nn", (ni, 4, 1),
                  a_ins=[rows_i(x, tm), vec(g_pre_mix)], a_fn=lambda xb, gb: _rms(xb, gb).astype(BF16),
                  b_ins=[(w_in, _bs((None, D, 1280), lambda i, j, k: (j, 0, 0)))], b_fn=_ident,
                  outs=[(_sds((seq, D_IN), F32), _bs((tm, 1280), lambda i, j, k: (i, j)))],
                  acc_shape=(tm, 1280), a_cache=((tm, D), BF16))
    qkv = _qkv_cast(proj, seq, tm)
    o, lta, ltb = _attn_fwd(qkv, seq)
    (y_attn,) = _mm("attn_out", "nn", (ni, 1, 1),
                    a_ins=[(o, _bs((tm, ATTN_W), lambda i, j, k: (i, 0)))], a_fn=_ident,
                    b_ins=[full2(w_ao)], b_fn=_ident,
                    outs=[(_sds((seq, D), F32), _bs((tm, D), lambda i, j, k: (i, 0)))], acc_shape=(tm, D))
    e, d = _conv_fwd(proj, w_conv, seq, tm)
    (y_conv,) = _mm("conv_out", "nn", (ni, 1, 1),
                    a_ins=[(e, _bs((tm, CONV_W), lambda i, j, k: (i, 0)))], a_fn=_ident,
                    b_ins=[full2(w_co)], b_fn=_ident,
                    outs=[(_sds((seq, D), F32), _bs((tm, D), lambda i, j, k: (i, 0)))], acc_shape=(tm, D))

    def mix_fn(ga, gc, ya, yc, ba, bc):
        return (_sig(ga + ba) * ya + _sig(gc + bc) * yc).astype(BF16)

    def post_mix(acc, xb, gb):
        return acc, xb + _rms(acc, gb)

    mix_ins = lambda rows: [rows(proj, th, 3), rows(proj, th, 4), rows(y_attn, th), rows(y_conv, th),
                            vec(b_gate, 0), vec(b_gate, 1)]
    mixed, x1 = _mm("mix_out", "nn", (nh, 1, 1),
                    a_ins=mix_ins(rows_i), a_fn=mix_fn, b_ins=[full2(w_o)], b_fn=_ident,
                    epi_ins=[rows_i(x, th), vec(g_post_mix)], epi_fn=post_mix,
                    outs=[(_sds((seq, D), F32), _bs((th, D), lambda i, j, k: (i, 0)))] * 2,
                    acc_shape=(th, D), a_cache=((th, D), BF16))
    (up,) = _mm("mlp_up", "nn", (ni, 4, 1),
                a_ins=[rows_i(x1, tm), vec(g_pre_mlp)], a_fn=lambda xb, gb: _rms(xb, gb).astype(BF16),
                b_ins=[(w_up, _bs((None, D, D), lambda i, j, k: (j, 0, 0)))], b_fn=_ident,
                outs=[(_sds((seq, D_FF), F32), _bs((tm, D), lambda i, j, k: (i, j)))],
                acc_shape=(tm, D), a_cache=((tm, D), BF16))

    def relu2(ub):
        r = jnp.maximum(ub, 0.0)
        return (r * r).astype(BF16)

    f, x2 = _mm("mlp_down", "nn", (nh, 1, 4),
                a_ins=[(up, _bs((th, D), lambda i, j, k: (i, k)))], a_fn=relu2,
                b_ins=[(w_down, _bs((D, D), lambda i, j, k: (k, 0)))], b_fn=_ident,
                epi_ins=[rows_i(x1, th), vec(g_post_mlp)], epi_fn=post_mix,
                outs=[(_sds((seq, D), F32), _bs((th, D), lambda i, j, k: (i, 0)))] * 2, acc_shape=(th, D))
    (pp,) = _mm("ple_proj", "nn", (ni, 1, 1),
                a_ins=[(p, _bs((tm, PLE_DIM), lambda i, j, k: (i, 0)))], a_fn=_to_bf16,
                b_ins=[full2(w_pp)], b_fn=_ident,
                outs=[(_sds((seq, D), F32), _bs((tm, D), lambda i, j, k: (i, 0)))], acc_shape=(tm, D))

    def head(acc, x2b, ppb, tb):
        pg = _sig(acc)
        err = x2b + pg * ppb - tb
        return pg, err * (1.0 / D), jnp.sum(err * err, axis=0, keepdims=True) * (0.5 / D)

    pg, dx3, loss_p = _mm("ple_gate_loss", "nn", (nh, 1, 1),
                          a_ins=[rows_i(x2, th), vec(g_ple)], a_fn=lambda xb, gb: _rms(xb, gb).astype(BF16),
                          b_ins=[full2(w_pg)], b_fn=_ident,
                          epi_ins=[rows_i(x2, th), rows_i(pp, th), rows_i(tgt, th)], epi_fn=head,
                          outs=[(_sds((seq, D), F32), _bs((th, D), lambda i, j, k: (i, 0)))] * 2 + [part(nh)],
                          acc_shape=(th, D), a_cache=((th, D), BF16))

    (dw_pp,) = _mm("dw_ple_proj", "tn", (1, 1, nh),
                   a_ins=[(p, _bs((th, PLE_DIM), lambda i, j, k: (k, 0)))], a_fn=_to_bf16,
                   b_ins=[rows_k(dx3, th), rows_k(pg, th)], b_fn=lambda a, b: (a * b).astype(BF16),
                   outs=[(_sds((PLE_DIM, D), F32), _bs((PLE_DIM, D), lambda i, j, k: (0, 0)))],
                   acc_shape=(PLE_DIM, D))

    def dpre_fn(dx3b, ppb, pgb):
        return (dx3b * ppb * pgb * (1.0 - pgb)).astype(BF16)

    def ple_norm_bwd(acc, x2b, dx3b, gb):
        dxn, dg = _rms_bwd(x2b, gb, acc)
        return dx3b + dxn, dg

    dx2, dg_ple_p, dpre = _mm("d_ple_gate", "nt", (nh, 1, 1),
                              a_ins=[rows_i(dx3, th), rows_i(pp, th), rows_i(pg, th)],
                              a_fn=lambda a, b, c: (dpre_fn(a, b, c),) * 2,
                              b_ins=[full2(w_pg)], b_fn=_ident,
                              epi_ins=[rows_i(x2, th), rows_i(dx3, th), vec(g_ple)], epi_fn=ple_norm_bwd,
                              outs=[(_sds((seq, D), F32), _bs((th, D), lambda i, j, k: (i, 0))), part(nh)],
                              acc_shape=(th, D), a_cache=((th, D), BF16),
                              a_outs=[(_sds((seq, D), BF16), _bs((th, D), lambda i, j, k: (i, 0)))])
    (dw_pg,) = _mm("dw_ple_gate", "tn", (1, 1, nh),
                   a_ins=[rows_k(x2, th), vec(g_ple)], a_fn=lambda xb, gb: _rms(xb, gb).astype(BF16),
                   b_ins=[rows_k(dpre, th)], b_fn=_ident,
                   outs=[(_sds((D, D), F32), _bs((D, D), lambda i, j, k: (0, 0)))], acc_shape=(D, D))

    def df_fn(fb, dx2b, gb):
        dfb, dg = _rms_bwd(fb, gb, dx2b)
        dfb = dfb.astype(BF16)
        return dfb, dfb, dg

    def dup_fn(acc, ub):
        return (acc * (2.0 * jnp.maximum(ub, 0.0)),)

    dup, df, dg_post_mlp_p = _mm("d_mlp_down", "nt", (nh, 4, 1),
                                 a_ins=[rows_i(f, th), rows_i(dx2, th), vec(g_post_mlp)], a_fn=df_fn,
                                 b_ins=[(w_down, _bs((D, D), lambda i, j, k: (j, 0)))], b_fn=_ident,
                                 epi_ins=[(up, _bs((th, D), lambda i, j, k: (i, j)))], epi_fn=dup_fn,
                                 outs=[(_sds((seq, D_FF), BF16), _bs((th, D), lambda i, j, k: (i, j)))],
                                 acc_shape=(th, D), a_cache=((th, D), BF16),
                                 a_outs=[(_sds((seq, D), BF16), _bs((th, D), lambda i, j, k: (i, 0))), part(nh)])
    (dw_down,) = _mm("dw_mlp_down", "tn", (4, 1, nh),
                     a_ins=[(up, _bs((th, D), lambda i, j, k: (k, i)))], a_fn=relu2,
                     b_ins=[rows_k(df, th)], b_fn=_ident,
                     outs=[(_sds((D_FF, D), F32), _bs((D, D), lambda i, j, k: (i, 0)))], acc_shape=(D, D))
    (dw_up,) = _mm("dw_mlp_up", "tn", (1, 4, nh),
                   a_ins=[rows_k(x1, th), vec(g_pre_mlp)], a_fn=lambda xb, gb: _rms(xb, gb).astype(BF16),
                   b_ins=[(dup, _bs((th, D), lambda i, j, k: (k, j)))], b_fn=_ident,
                   outs=[(_sds((N_CHIPS, D, D), F32), _bs((None, D, D), lambda i, j, k: (j, 0, 0)))],
                   acc_shape=(D, D))

    def mlp_norm_bwd(acc, x1b, dx2b, mixedb, g_mlp, g_mix):
        dxn, dg_mlp = _rms_bwd(x1b, g_mlp, acc)
        dx1b = dx2b + dxn
        dmixedb, dg_mix = _rms_bwd(mixedb, g_mix, dx1b)
        return dx1b, dmixedb, dg_mlp, dg_mix

    dx1, dmixed, dg_pre_mlp_p, dg_post_mix_p = _mm(
        "d_mlp_up", "nt", (nh, 1, 4),
        a_ins=[(dup, _bs((th, D), lambda i, j, k: (i, k)))], a_fn=_ident,
        b_ins=[(w_up, _bs((None, D, D), lambda i, j, k: (k, 0, 0)))], b_fn=_ident,
        epi_ins=[rows_i(x1, th), rows_i(dx2, th), rows_i(mixed, th), vec(g_pre_mlp), vec(g_post_mix)],
        epi_fn=mlp_norm_bwd,
        outs=[(_sds((seq, D), F32), _bs((th, D), lambda i, j, k: (i, 0))),
              (_sds((seq, D), BF16), _bs((th, D), lambda i, j, k: (i, 0))), part(nh), part(nh)],
        acc_shape=(th, D))
    (dw_o,) = _mm("dw_mix_out", "tn", (1, 1, nh),
                  a_ins=mix_ins(rows_k), a_fn=mix_fn, b_ins=[rows_k(dmixed, th)], b_fn=_ident,
                  outs=[(_sds((D, D), F32), _bs((D, D), lambda i, j, k: (0, 0)))], acc_shape=(D, D))

    def gate_bwd(acc, ga, gc, ya, yc, ba, bc):
        sa, sc = _sig(ga + ba), _sig(gc + bc)
        dga = acc * ya * sa * (1.0 - sa)
        dgc = acc * yc * sc * (1.0 - sc)
        return (acc * sa, acc * sc, jnp.concatenate([dga, dgc], axis=1),
                jnp.sum(dga, axis=0, keepdims=True), jnp.sum(dgc, axis=0, keepdims=True))

    dya, dyc, dgate, dbg_a_p, dbg_c_p = _mm(
        "d_mix_out", "nt", (nh, 1, 1),
        a_ins=[rows_i(dmixed, th)], a_fn=_ident, b_ins=[full2(w_o)], b_fn=_ident,
        epi_ins=mix_ins(rows_i), epi_fn=gate_bwd,
        outs=[(_sds((seq, D), BF16), _bs((th, D), lambda i, j, k: (i, 0)))] * 2
             + [(_sds((seq, 2 * D), BF16), _bs((th, 2 * D), lambda i, j, k: (i, 0))), part(nh), part(nh)],
        acc_shape=(th, D))
    (dw_ao,) = _mm("dw_attn_out", "tn", (1, 1, nh),
                   a_ins=[(o, _bs((th, ATTN_W), lambda i, j, k: (k, 0)))], a_fn=_ident,
                   b_ins=[rows_k(dya, th)], b_fn=_ident,
                   outs=[(_sds((ATTN_W, D), F32), _bs((ATTN_W, D), lambda i, j, k: (0, 0)))], acc_shape=(ATTN_W, D))
    (do,) = _mm("d_attn_out", "nt", (ni, 1, 1),
                a_ins=[rows_i(dya, tm)], a_fn=_ident, b_ins=[full2(w_ao)], b_fn=_ident,
                outs=[(_sds((seq, ATTN_W), BF16), _bs((tm, ATTN_W), lambda i, j, k: (i, 0)))],
                acc_shape=(tm, ATTN_W))
    dq, dk, dv = _attn_bwd(qkv, do, lta, ltb, seq)
    (dw_co,) = _mm("dw_conv_out", "tn", (1, 1, nh),
                   a_ins=[(e, _bs((th, CONV_W), lambda i, j, k: (k, 0)))], a_fn=_ident,
                   b_ins=[rows_k(dyc, th)], b_fn=_ident,
                   outs=[(_sds((CONV_W, D), F32), _bs((CONV_W, D), lambda i, j, k: (0, 0)))], acc_shape=(CONV_W, D))
    (de,) = _mm("d_conv_out", "nt", (ni, 1, 1),
                a_ins=[rows_i(dyc, tm)], a_fn=_ident, b_ins=[full2(w_co)], b_fn=_ident,
                outs=[(_sds((seq, CONV_W), F32), _bs((tm, CONV_W), lambda i, j, k: (i, 0)))],
                acc_shape=(tm, CONV_W))
    dconv, dwc_p = _conv_bwd(proj, de, d, w_conv, seq, tm)
    dproj = jnp.concatenate([dq.astype(BF16), dk.astype(BF16), dv.astype(BF16), dconv, dgate], axis=1)
    (dw_in,) = _mm("dw_proj_in", "tn", (1, 4, nh),
                   a_ins=[rows_k(x, th), vec(g_pre_mix)], a_fn=lambda xb, gb: _rms(xb, gb).astype(BF16),
                   b_ins=[(dproj, _bs((th, 1280), lambda i, j, k: (k, j)))], b_fn=_ident,
                   outs=[(_sds((N_CHIPS, D, 1280), F32), _bs((None, D, 1280), lambda i, j, k: (j, 0, 0)))],
                   acc_shape=(D, 1280))

    def in_norm_bwd(acc, xb, dx1b, gb):
        dxn, dg = _rms_bwd(xb, gb, acc)
        return dx1b + dxn, dg

    grad_x, dg_pre_mix_p = _mm("d_proj_in", "nt", (nh, 1, 4),
                               a_ins=[(dproj, _bs((th, 1280), lambda i, j, k: (i, k)))], a_fn=_ident,
                               b_ins=[(w_in, _bs((None, D, 1280), lambda i, j, k: (k, 0, 0)))], b_fn=_ident,
                               epi_ins=[rows_i(x, th), rows_i(dx1, th), vec(g_pre_mix)], epi_fn=in_norm_bwd,
                               outs=[(_sds((seq, D), F32), _bs((th, D), lambda i, j, k: (i, 0))), part(nh)],
                               acc_shape=(th, D))

    chip_major = lambda a: a.reshape(a.shape[0], N_CHIPS, a.shape[1] // N_CHIPS).transpose(1, 0, 2)
    big = [dw_in, chip_major(dw_ao), chip_major(dw_co), dw_o.reshape(N_CHIPS, D // N_CHIPS, D), dw_up,
           dw_down.reshape(N_CHIPS, D_FF // N_CHIPS, D), dw_pg.reshape(N_CHIPS, D // N_CHIPS, D), chip_major(dw_pp)]
    small = (loss_p, [dg_pre_mix_p, dg_post_mix_p, dg_pre_mlp_p, dg_post_mlp_p, dg_ple_p], dbg_a_p, dbg_c_p, dwc_p)
    return grad_x, big, small


RS_GROUPS = ((0,), (4,), (5,), (1, 2, 3, 6, 7))


def _reduce_scatter(big):
    pair = [None] * len(big)
    for gi, group in enumerate(RS_GROUPS):
        for w, s in zip(group, _rs_pair_sum(f"rs_pair_sum_{gi}", [big[w] for w in group])):
            pair[w] = s
    return _rs_exchange_join(pair)


def kernel(x, p, g_pre_mix, w_in, b_gate, w_conv, w_attn_out, w_conv_out, w_o, g_post_mix, g_pre_mlp, w_up, w_down, g_post_mlp, g_ple, w_ple_gate, w_ple_proj, loss_target, m_g_pre_mix, m_w_in, m_b_gate, m_w_conv, m_w_attn_out, m_w_conv_out, m_w_o, m_g_post_mix, m_g_pre_mlp, m_w_up, m_w_down, m_g_post_mlp, m_g_ple, m_w_ple_gate, m_w_ple_proj, v_g_pre_mix, v_w_in, v_b_gate, v_w_conv, v_w_attn_out, v_w_conv_out, v_w_o, v_g_post_mix, v_g_pre_mlp, v_w_up, v_w_down, v_g_post_mlp, v_g_ple, v_w_ple_gate, v_w_ple_proj):
    mats = [w_in, w_attn_out, w_conv_out, w_o, w_up, w_down, w_ple_gate, w_ple_proj]
    mats_m = [m_w_in, m_w_attn_out, m_w_conv_out, m_w_o, m_w_up, m_w_down, m_w_ple_gate, m_w_ple_proj]
    mats_v = [v_w_in, v_w_attn_out, v_w_conv_out, v_w_o, v_w_up, v_w_down, v_w_ple_gate, v_w_ple_proj]
    gains = [g_pre_mix, g_post_mix, g_pre_mlp, g_post_mlp, g_ple]
    gains_m = [m_g_pre_mix, m_g_post_mix, m_g_pre_mlp, m_g_post_mlp, m_g_ple]
    gains_v = [v_g_pre_mix, v_g_post_mix, v_g_pre_mlp, v_g_post_mlp, v_g_ple]

    taps = jnp.concatenate([w_conv[0], jnp.zeros((CONV_PAD_ROWS - 3, LANES), F32)], axis=0)
    gathered = _allgather_weights([w[0].astype(BF16) for w in mats] + [taps])
    cols_joined = lambda a: a.transpose(1, 0, 2).reshape(a.shape[1], N_CHIPS * a.shape[2])
    rows_joined = lambda a: a.reshape(N_CHIPS * a.shape[1], a.shape[2])
    wf = [gathered[0], cols_joined(gathered[1]), cols_joined(gathered[2]), rows_joined(gathered[3]), gathered[4],
          rows_joined(gathered[5]), rows_joined(gathered[6]), cols_joined(gathered[7])]
    w_conv_full = cols_joined(gathered[8])[0:3, :]
    chip = 2 * lax.axis_index("x") + lax.axis_index("y")

    grad_x, big, small = _local_step(x[0], p[0, 0], loss_target[0], gains, b_gate, w_conv_full, wf)

    shard_grads = _reduce_scatter(big)
    red = _small_allreduce(*small)
    loss = red[0, 0]
    grad_gains = [red[1 + r:2 + r, :] for r in range(5)]
    grad_b_gate = jnp.concatenate([red[6:7, :], red[7:8, :]], axis=1)
    grad_w_conv = lax.dynamic_slice(red[8:11, :], (0, chip * LANES), (3, LANES))[None]

    grads_big = [gr.reshape(w.shape) for gr, w in zip(shard_grads, mats)]
    upd_big = [_adamw(f"adamw_{i}", w, gr, m, v) for i, (w, gr, m, v) in enumerate(zip(mats, grads_big, mats_m, mats_v))]
    pack = lambda vs, bg: jnp.concatenate(list(vs) + [bg.reshape(2, D_MODEL), jnp.zeros((1, D_MODEL), F32)], axis=0)
    upd_small = _adamw("adamw_small", pack(gains, b_gate), pack(grad_gains, grad_b_gate),
                       pack(gains_m, m_b_gate), pack(gains_v, v_b_gate))
    upd_conv = _adamw("adamw_conv", w_conv, grad_w_conv, m_w_conv, v_w_conv)

    def small_out(a, which):
        gains_out = [a[r:r + 1, :] for r in range(5)]
        return gains_out, a[5:7, :].reshape(1, 2 * D_MODEL)

    def ordered(g_pre_mix_, big_, b_gate_, conv_, g_rest):
        return [g_pre_mix_, big_[0], b_gate_, conv_, big_[1], big_[2], big_[3], g_rest[0], g_rest[1], big_[4], big_[5],
                g_rest[2], g_rest[3], big_[6], big_[7]]

    outs = [loss, grad_x[None]]
    outs += ordered(grad_gains[0], grads_big, grad_b_gate, grad_w_conv, grad_gains[1:])
    for which in range(3):
        g_out, b_out = small_out(upd_small[which], which)
        outs += ordered(g_out[0], [u[which] for u in upd_big], b_out, upd_conv[which], g_out[1:])
    return tuple(outs)
```

```python
import functools

import jax
import jax.numpy as jnp
from jax import lax
from jax.experimental import pallas as pl
from jax.experimental.pallas import tpu as pltpu

F32 = jnp.float32
BF16 = jnp.bfloat16
MESH = pl.DeviceIdType.MESH

D_MODEL = 1024
N_HEADS = 8
HEAD_DIM = 64
ATTN_W = N_HEADS * HEAD_DIM
CONV_W = 512
D_FF = 4096
PLE_DIM = 256
D_IN = 5120
N_CHIPS = 4
EPS = 1e-6
Q_SCALE = HEAD_DIM ** -0.5

ADAM_LR = 0.001
ADAM_B1 = 0.9
ADAM_B2 = 0.999
ADAM_EPS = 1e-08
ADAM_WD = 0.01
ADAM_STEP = 10

V7X_VMEM_BYTES = 64 * 1024 * 1024
VMEM_LIMIT = V7X_VMEM_BYTES - 8 * 1024 * 1024
LANES = 128
ATT_BLK = 256
SMALL_ROWS = 16
CONV_PAD_ROWS = 16


def _cparams(n_grid):
    return pltpu.CompilerParams(dimension_semantics=("arbitrary",) * n_grid, vmem_limit_bytes=VMEM_LIMIT)


def _bs(shape, fn):
    return pl.BlockSpec(shape, fn)


def _rms_stats(xf):
    return lax.rsqrt(jnp.mean(xf * xf, axis=-1, keepdims=True) + EPS)


def _rms(xf, g):
    return xf * _rms_stats(xf) * g


def _rms_bwd(xf, g, dy):
    r = _rms_stats(xf)
    xh = xf * r
    dyg = dy * g
    dx = r * (dyg - xh * jnp.mean(dyg * xh, axis=-1, keepdims=True))
    return dx, jnp.sum(dy * xh, axis=0, keepdims=True)


def _sig(z):
    return 1.0 / (1.0 + jnp.exp(-z))


def _ident(a):
    return a


def _to_bf16(a):
    return a.astype(BF16)


_DIMS = {"nn": (((1,), (0,)), ((), ())), "nt": (((1,), (1,)), ((), ())), "tn": (((0,), (0,)), ((), ()))}


def _mm(name, mode, grid, a_ins, a_fn, b_ins, b_fn, outs, acc_shape, epi_ins=(), epi_fn=None,
        a_cache=None, a_outs=()):
    nk = grid[2]
    na, nb, ne, no, nao = len(a_ins), len(b_ins), len(epi_ins), len(outs), len(a_outs)
    assert a_cache is None or nk == 1
    assert not a_outs or a_cache is not None
    dims = _DIMS[mode]
    if epi_fn is None:
        epi_fn = lambda acc: (acc,)

    def body(*refs):
        a_refs = refs[:na]
        b_refs = refs[na:na + nb]
        e_refs = refs[na + nb:na + nb + ne]
        o_refs = refs[na + nb + ne:na + nb + ne + no]
        ao_refs = refs[na + nb + ne + no:na + nb + ne + no + nao]
        scratch = list(refs[na + nb + ne + no + nao:])
        acc_ref = scratch.pop(0) if nk > 1 else None
        a_sc = scratch.pop(0) if a_cache is not None else None
        j = pl.program_id(1)
        k = pl.program_id(2)

        def finish(acc):
            res = epi_fn(acc, *[r[...] for r in e_refs])
            for r, val in zip(o_refs, res):
                r[...] = val.astype(r.dtype)

        if a_sc is not None:
            @pl.when(j == 0)
            def _():
                res = a_fn(*[r[...] for r in a_refs])
                if nao:
                    for r, val in zip(ao_refs, res[1:]):
                        r[...] = val.astype(r.dtype)
                    res = res[0]
                a_sc[...] = res
            a = a_sc[...]
        else:
            a = a_fn(*[r[...] for r in a_refs])
        b = b_fn(*[r[...] for r in b_refs])
        prod = lax.dot_general(a, b, dims, preferred_element_type=F32)
        if nk == 1:
            finish(prod)
        else:
            @pl.when(k == 0)
            def _():
                acc_ref[...] = prod

            @pl.when(k > 0)
            def _():
                acc_ref[...] += prod

            @pl.when(k == nk - 1)
            def _():
                finish(acc_ref[...])

    scratch_shapes = []
    if nk > 1:
        scratch_shapes.append(pltpu.VMEM(acc_shape, F32))
    if a_cache is not None:
        scratch_shapes.append(pltpu.VMEM(*a_cache))
    all_outs = list(outs) + list(a_outs)
    res = pl.pallas_call(
        body, name=name, grid=grid,
        in_specs=[s for _, s in a_ins] + [s for _, s in b_ins] + [s for _, s in epi_ins],
        out_specs=[s for _, s in all_outs],
        out_shape=[o for o, _ in all_outs],
        scratch_shapes=scratch_shapes,
        compiler_params=_cparams(3),
    )(*[a for a, _ in a_ins], *[a for a, _ in b_ins], *[a for a, _ in epi_ins])
    return res


def _sds(shape, dtype):
    return jax.ShapeDtypeStruct(shape, dtype)


def _qkv_cast(proj, seq, tr):
    def body(p_ref, o_ref):
        scale = jnp.where(pl.program_id(1) == 0, Q_SCALE, 1.0).astype(F32)
        o_ref[...] = (p_ref[...] * scale).astype(BF16)

    return pl.pallas_call(
        body, name="qkv_cast", grid=(seq // tr, 3),
        in_specs=[_bs((tr, ATTN_W), lambda i, c: (i, c))],
        out_specs=_bs((tr, ATTN_W), lambda i, c: (i, c)),
        out_shape=_sds((seq, 3 * ATTN_W), BF16),
        compiler_params=_cparams(2),
    )(proj)


def _shift_rows_down(u, prev, n):
    rows = u.shape[0]
    ridx = lax.broadcasted_iota(jnp.int32, u.shape, 0)
    out = pltpu.roll(u, n, 0)
    for r in range(n):
        out = jnp.where(ridx == r, prev[8 - n + r:8 - n + r + 1, :], out)
    del rows
    return out


def _shift_rows_up(u, nxt, n):
    rows = u.shape[0]
    ridx = lax.broadcasted_iota(jnp.int32, u.shape, 0)
    out = pltpu.roll(u, rows - n, 0)
    for r in range(n):
        out = jnp.where(ridx == rows - n + r, nxt[r:r + 1, :], out)
    return out


CONV_COL0 = 3


def _conv_fwd(proj, w_conv, seq, tr):
    hb = tr // 8

    def body(cb_ref, cc_ref, cu_ref, ccp_ref, cup_ref, w_ref, e_ref, d_ref):
        i = pl.program_id(0)
        u = cc_ref[...] * cu_ref[...]
        up = jnp.where(i > 0, ccp_ref[...] * cup_ref[...], 0.0)
        w = w_ref[...]
        d = w[0:1, :] * _shift_rows_down(u, up, 2) + w[1:2, :] * _shift_rows_down(u, up, 1) + w[2:3, :] * u
        d_ref[...] = d
        e_ref[...] = (cb_ref[...] * d).astype(BF16)

    prev = lambda c: (lambda i: (jnp.maximum(i * hb - 1, 0), c))
    return pl.pallas_call(
        body, name="conv_fwd", grid=(seq // tr,),
        in_specs=[_bs((tr, CONV_W), lambda i: (i, CONV_COL0)),
                  _bs((tr, CONV_W), lambda i: (i, CONV_COL0 + 1)),
                  _bs((tr, CONV_W), lambda i: (i, CONV_COL0 + 2)),
                  _bs((8, CONV_W), prev(CONV_COL0 + 1)),
                  _bs((8, CONV_W), prev(CONV_COL0 + 2)),
                  _bs((3, CONV_W), lambda i: (0, 0))],
        out_specs=[_bs((tr, CONV_W), lambda i: (i, 0)), _bs((tr, CONV_W), lambda i: (i, 0))],
        out_shape=[_sds((seq, CONV_W), BF16), _sds((seq, CONV_W), F32)],
        compiler_params=_cparams(1),
    )(proj, proj, proj, proj, proj, w_conv)


def _conv_bwd(proj, de, d, w_conv, seq, tr):
    hb = tr // 8
    nblk = seq // tr

    def body(cb_ref, cc_ref, cu_ref, ccp_ref, cup_ref, cbn_ref, de_ref, den_ref, d_ref, w_ref, o_ref, dw_ref):
        i = pl.program_id(0)
        cc, cu, cb = cc_ref[...], cu_ref[...], cb_ref[...]
        u = cc * cu
        up = jnp.where(i > 0, ccp_ref[...] * cup_ref[...], 0.0)
        u1 = _shift_rows_down(u, up, 1)
        u2 = _shift_rows_down(u, up, 2)
        de_ = de_ref[...]
        dd = de_ * cb
        ddn = jnp.where(i < nblk - 1, den_ref[...] * cbn_ref[...], 0.0)
        w = w_ref[...]
        du = w[2:3, :] * dd + w[1:2, :] * _shift_rows_up(dd, ddn, 1) + w[0:1, :] * _shift_rows_up(dd, ddn, 2)
        o_ref[:, 0:CONV_W] = (de_ * d_ref[...]).astype(BF16)
        o_ref[:, CONV_W:2 * CONV_W] = (du * cu).astype(BF16)
        o_ref[:, 2 * CONV_W:3 * CONV_W] = (du * cc).astype(BF16)
        ridx = lax.broadcasted_iota(jnp.int32, (8, CONV_W), 0)
        dw0 = jnp.sum(dd * u2, axis=0, keepdims=True)
        dw1 = jnp.sum(dd * u1, axis=0, keepdims=True)
        dw2 = jnp.sum(dd * u, axis=0, keepdims=True)
        dw_ref[...] = jnp.where(ridx == 0, dw0, jnp.where(ridx == 1, dw1, jnp.where(ridx == 2, dw2, 0.0)))

    prev = lambda c: (lambda i: (jnp.maximum(i * hb - 1, 0), c))
    nxt = lambda c: (lambda i: (jnp.minimum((i + 1) * hb, seq // 8 - 1), c))
    return pl.pallas_call(
        body, name="conv_bwd", grid=(nblk,),
        in_specs=[_bs((tr, CONV_W), lambda i: (i, CONV_COL0)),
                  _bs((tr, CONV_W), lambda i: (i, CONV_COL0 + 1)),
                  _bs((tr, CONV_W), lambda i: (i, CONV_COL0 + 2)),
                  _bs((8, CONV_W), prev(CONV_COL0 + 1)),
                  _bs((8, CONV_W), prev(CONV_COL0 + 2)),
                  _bs((8, CONV_W), nxt(CONV_COL0)),
                  _bs((tr, CONV_W), lambda i: (i, 0)),
                  _bs((8, CONV_W), nxt(0)),
                  _bs((tr, CONV_W), lambda i: (i, 0)),
                  _bs((3, CONV_W), lambda i: (0, 0))],
        out_specs=[_bs((tr, 3 * CONV_W), lambda i: (i, 0)), _bs((None, 8, CONV_W), lambda i: (i, 0, 0))],
        out_shape=[_sds((seq, 3 * CONV_W), BF16), _sds((nblk, 8, CONV_W), F32)],
        compiler_params=_cparams(1),
    )(proj, proj, proj, proj, proj, proj, de, de, d, w_conv)


def _nt(a, b):
    return lax.dot_general(a, b, _DIMS["nt"], preferred_element_type=F32)


def _tn(a, b):
    return lax.dot_general(a, b, _DIMS["tn"], preferred_element_type=F32)


def _nn(a, b):
    return lax.dot_general(a, b, _DIMS["nn"], preferred_element_type=F32)


def _log_gates(z):
    lse = jnp.log(1.0 + jnp.exp(-jnp.abs(z)))
    log_beta = jnp.minimum(z, 0.0) - lse
    return log_beta, log_beta - z


def _attn_fwd(qkv, seq):
    blk = ATT_BLK
    nq = seq // blk
    npair = N_HEADS // 2

    def body(q_ref, k_ref, v_ref, o_ref, lta_ref, ltb_ref, z0_sc, z1_sc, w0_sc, w1_sc, tot_sc, acc_sc):
        i = pl.program_id(1)
        is_a = lax.broadcasted_iota(jnp.int32, (1, LANES), 1) < HEAD_DIM
        q2 = q_ref[...]
        zero = jnp.zeros_like(q2)
        qs = (jnp.where(is_a, q2, zero), jnp.where(is_a, zero, q2))
        row = lax.broadcasted_iota(jnp.int32, (blk, blk), 0)
        col = lax.broadcasted_iota(jnp.int32, (blk, blk), 1)
        tri = (row > col).astype(BF16)
        causal = col < row

        def scores(j):
            k2 = k_ref[pl.ds(pl.multiple_of(j * blk, blk), blk), :]
            return [_nt(qs[h], k2) for h in range(2)]

        def weights(zs, tot, diag):
            gates = [_log_gates(z) for z in zs]
            keeps = [jnp.where(causal, g[1], 0.0) if diag else g[1] for g in gates]
            sums = [_nn(lk.astype(BF16), tri) for lk in keeps]
            ws, new_tot = [], []
            for h in range(2):
                w = jnp.exp(gates[h][0] + (tot[h] + sums[h]))
                if diag:
                    w = jnp.where(causal, w, 0.0)
                ws.append(w.astype(BF16))
                new_tot.append(tot[h] + jnp.sum(keeps[h], axis=-1, keepdims=True))
            return ws, new_tot

        def values(ws, j, acc):
            v2 = v_ref[pl.ds(pl.multiple_of(j * blk, blk), blk), :]
            return acc + jnp.where(is_a, _nn(ws[0], v2), _nn(ws[1], v2))

        z_bufs, w_bufs = (z0_sc, z1_sc), (w0_sc, w1_sc)

        def put(ref, vals):
            for h in range(2):
                ref[h] = vals[h]

        def trip(j, s):
            acc_sc[...] = values((w_bufs[s][0], w_bufs[s][1]), j + 1, acc_sc[...])
            put(z_bufs[1 - s], scores(jnp.maximum(j - 1, 0)))
            ws, tot = weights((z_bufs[s][0], z_bufs[s][1]), [tot_sc[0], tot_sc[1]], False)
            put(w_bufs[1 - s], ws)
            put(tot_sc, tot)

        zero_col = jnp.zeros((blk, 1), F32)
        ws, tot = weights(scores(i), [zero_col, zero_col], True)
        put(w0_sc, ws)
        put(tot_sc, tot)
        put(z0_sc, scores(jnp.maximum(i - 1, 0)))
        acc_sc[...] = jnp.zeros_like(acc_sc)

        def two_trips(pp, carry):
            j = i - 1 - 2 * pp
            trip(j, 0)
            trip(j - 1, 1)
            return carry

        lax.fori_loop(0, i // 2, two_trips, 0)
        odd = i % 2 == 1

        @pl.when(odd)
        def _():
            trip(0, 0)

        @pl.when(odd)
        def _():
            o_ref[...] = values((w1_sc[0], w1_sc[1]), 0, acc_sc[...]).astype(BF16)

        @pl.when(jnp.logical_not(odd))
        def _():
            o_ref[...] = values((w0_sc[0], w0_sc[1]), 0, acc_sc[...]).astype(BF16)

        lta_ref[...] = jnp.broadcast_to(tot_sc[0], (blk, LANES))
        ltb_ref[...] = jnp.broadcast_to(tot_sc[1], (blk, LANES))

    return pl.pallas_call(
        body, name="attn_fwd", grid=(npair, nq),
        in_specs=[_bs((blk, LANES), lambda p, i: (i, p)),
                  _bs((seq, LANES), lambda p, i: (0, npair + p)),
                  _bs((seq, LANES), lambda p, i: (0, 2 * npair + p))],
        out_specs=[_bs((blk, LANES), lambda p, i: (i, p))] * 3,
        out_shape=[_sds((seq, ATTN_W), BF16), _sds((seq, ATTN_W), F32), _sds((seq, ATTN_W), F32)],
        scratch_shapes=[pltpu.VMEM((2, blk, blk), F32), pltpu.VMEM((2, blk, blk), F32),
                        pltpu.VMEM((2, blk, blk), BF16), pltpu.VMEM((2, blk, blk), BF16),
                        pltpu.VMEM((2, blk, 1), F32), pltpu.VMEM((blk, LANES), F32)],
        compiler_params=_cparams(2),
    )(qkv, qkv, qkv)


def _attn_bwd(qkv, do, lta, ltb, seq):
    blk = ATT_BLK
    nq = seq // blk
    npair = N_HEADS // 2

    def body(q_ref, k_ref, v_ref, do_ref, lta_ref, ltb_ref, dq_ref, dk_ref, dv_ref,
             prod0_sc, prod1_sc, pend0_sc, pend1_sc, cum_sc, pre_sc, dq_sc):
        i = pl.program_id(1)

        @pl.when(i == 0)
        def _():
            dk_ref[...] = jnp.zeros_like(dk_ref)
            dv_ref[...] = jnp.zeros_like(dv_ref)

        is_a = lax.broadcasted_iota(jnp.int32, (1, LANES), 1) < HEAD_DIM
        q2 = q_ref[...]
        do2 = do_ref[...]
        zero = jnp.zeros_like(q2)
        qs = (jnp.where(is_a, q2, zero), jnp.where(is_a, zero, q2))
        dos = (jnp.where(is_a, do2, zero), jnp.where(is_a, zero, do2))
        ltot = (jnp.max(lta_ref[...], axis=-1, keepdims=True), jnp.max(ltb_ref[...], axis=-1, keepdims=True))
        row = lax.broadcasted_iota(jnp.int32, (blk, blk), 0)
        col = lax.broadcasted_iota(jnp.int32, (blk, blk), 1)
        tri_after = (row > col).astype(BF16)
        tri_excl = (row < col).astype(BF16)
        causal = col < row

        prod_bufs, pend_bufs = (prod0_sc, prod1_sc), (pend0_sc, pend1_sc)

        def products(j):
            off = pl.multiple_of(j * blk, blk)
            k2 = k_ref[pl.ds(off, blk), :]
            v2 = v_ref[pl.ds(off, blk), :]
            return [_nt(qs[h], k2) for h in range(2)] + [_nt(dos[h], v2) for h in range(2)]

        def local_grads(prods, diag):
            zs, dws = prods[:2], prods[2:]
            gates = [_log_gates(z) for z in zs]
            keeps = [jnp.where(causal, g[1], 0.0) if diag else g[1] for g in gates]
            sums = [_nn(lk.astype(BF16), tri_after) for lk in keeps]
            ws, gs = [], []
            for h in range(2):
                cum = cum_sc[h] + jnp.sum(keeps[h], axis=-1, keepdims=True)
                cum_sc[h] = cum
                w = jnp.exp(gates[h][0] + ((ltot[h] - cum) + sums[h]))
                if diag:
                    w = jnp.where(causal, w, 0.0)
                ws.append(w)
                gs.append(dws[h] * w)
            befores = [_nn(g.astype(BF16), tri_excl) for g in gs]
            dzs = []
            for h in range(2):
                beta = jnp.exp(gates[h][0])
                dz = gs[h] * (1.0 - beta) - (pre_sc[h] + befores[h]) * beta
                if diag:
                    dz = jnp.where(causal, dz, 0.0)
                dzs.append(dz.astype(BF16))
                pre_sc[h] = pre_sc[h] + jnp.sum(gs[h], axis=-1, keepdims=True)
            return [w.astype(BF16) for w in ws] + dzs

        def grad_matmuls(pend, j):
            off = pl.multiple_of(j * blk, blk)
            k2 = k_ref[pl.ds(off, blk), :]
            dq_sc[...] += jnp.where(is_a, _nn(pend[2], k2), _nn(pend[3], k2))
            dk_ref[pl.ds(off, blk), :] += jnp.where(is_a, _tn(pend[2], q2), _tn(pend[3], q2))
            dv_ref[pl.ds(off, blk), :] += jnp.where(is_a, _tn(pend[0], do2), _tn(pend[1], do2))

        def put(ref, vals):
            for n, val in enumerate(vals):
                ref[n] = val

        def take(ref):
            return [ref[n] for n in range(4)]

        def trip(j, s):
            grad_matmuls(take(pend_bufs[s]), jnp.maximum(j - 1, 0))
            put(prod_bufs[1 - s], products(j + 1))
            put(pend_bufs[1 - s], local_grads(take(prod_bufs[s]), False))

        cum_sc[...] = jnp.zeros_like(cum_sc)
        pre_sc[...] = jnp.zeros_like(pre_sc)
        dq_sc[...] = jnp.zeros_like(dq_sc)
        pend0_sc[...] = jnp.zeros_like(pend0_sc)
        put(prod0_sc, products(0))

        def two_trips(pp, carry):
            trip(2 * pp, 0)
            trip(2 * pp + 1, 1)
            return carry

        lax.fori_loop(0, i // 2, two_trips, 0)
        odd = i % 2 == 1

        @pl.when(odd)
        def _():
            trip(i - 1, 0)

        def finish(s):
            grad_matmuls(take(pend_bufs[s]), jnp.maximum(i - 1, 0))
            grad_matmuls(local_grads(take(prod_bufs[s]), True), i)
            dq_ref[...] = dq_sc[...] * Q_SCALE

        @pl.when(odd)
        def _():
            finish(1)

        @pl.when(jnp.logical_not(odd))
        def _():
            finish(0)

    qmap = lambda p, i: (i, p)
    return pl.pallas_call(
        body, name="attn_bwd", grid=(npair, nq),
        in_specs=[_bs((blk, LANES), qmap),
                  _bs((seq, LANES), lambda p, i: (0, npair + p)),
                  _bs((seq, LANES), lambda p, i: (0, 2 * npair + p)),
                  _bs((blk, LANES), qmap), _bs((blk, LANES), qmap), _bs((blk, LANES), qmap)],
        out_specs=[_bs((blk, LANES), qmap),
                   _bs((seq, LANES), lambda p, i: (0, p)),
                   _bs((seq, LANES), lambda p, i: (0, p))],
        out_shape=[_sds((seq, ATTN_W), F32)] * 3,
        scratch_shapes=[pltpu.VMEM((4, blk, blk), F32), pltpu.VMEM((4, blk, blk), F32),
                        pltpu.VMEM((4, blk, blk), BF16), pltpu.VMEM((4, blk, blk), BF16),
                        pltpu.VMEM((2, blk, 1), F32), pltpu.VMEM((2, blk, 1), F32), pltpu.VMEM((blk, LANES), F32)],
        compiler_params=_cparams(2),
    )(qkv, qkv, qkv, do, lta, ltb)


def _elementwise(name, fn, ins, out_dtypes):
    rows, cols = ins[0].shape
    tr = rows
    for cand in (512, 256, 128, 64, 32, 16, 8):
        if rows % cand == 0 and cand * cols * 4 <= 2 * 1024 * 1024:
            tr = cand
            break
    n_in = len(ins)

    def body(*refs):
        res = fn(*[r[...] for r in refs[:n_in]])
        for r, val in zip(refs[n_in:], res):
            r[...] = val.astype(r.dtype)

    spec = _bs((tr, cols), lambda i: (i, 0))
    return pl.pallas_call(
        body, name=name, grid=(rows // tr,),
        in_specs=[spec] * n_in, out_specs=[spec] * len(out_dtypes),
        out_shape=[_sds((rows, cols), dt) for dt in out_dtypes],
        compiler_params=_cparams(1),
    )(*ins)


def _adamw_fn(w, g, m, v):
    m = ADAM_B1 * m + (1.0 - ADAM_B1) * g
    v = ADAM_B2 * v + (1.0 - ADAM_B2) * (g * g)
    m_hat = m / (1.0 - ADAM_B1 ** ADAM_STEP)
    v_hat = v / (1.0 - ADAM_B2 ** ADAM_STEP)
    delta = -ADAM_LR * (m_hat / (jnp.sqrt(v_hat) + ADAM_EPS) + ADAM_WD * w)
    return delta, m, v


def _adamw(name, w, g, m, v):
    shape = w.shape
    as2d = lambda a: a.reshape(-1, shape[-1])
    delta, nm, nv = _elementwise(name, _adamw_fn, [as2d(w), as2d(g), as2d(m), as2d(v)], [F32, F32, F32])
    return delta.reshape(shape), nm.reshape(shape), nv.reshape(shape)


def _place():
    x, y, c = lax.axis_index("x"), lax.axis_index("y"), lax.axis_index("c")
    chips = [(1 - x, y), (x, 1 - y), (1 - x, 1 - y)]
    return x, y, c, chips


ANY = pl.BlockSpec(memory_space=pl.ANY)
VMEM_WHOLE = pl.BlockSpec(memory_space=pltpu.VMEM)


def _allgather_weights(shards):
    n = len(shards)

    def body(*refs):
        src, dst = refs[:n], refs[n:2 * n]
        send_sems, recv_sems, local_sems = refs[2 * n:]
        x, y, c, chips = _place()
        me, sibling, mychip = (x, y, c), (x, y, 1 - c), 2 * x + y

        def piece(w, chip, half):
            hr = src[w].shape[0] // 2
            return dst[w].at[chip, pl.ds(half * hr, hr)]

        def copy(w, k, src_ref, dst_ref, to):
            return pltpu.make_async_remote_copy(src_ref=src_ref, dst_ref=dst_ref, send_sem=send_sems.at[w, k],
                                                recv_sem=recv_sems.at[w, k], device_id=to, device_id_type=MESH)

        started, local = [], []
        for w in range(n):
            hr = src[w].shape[0] // 2
            own = pltpu.make_async_copy(src[w], dst[w].at[mychip], local_sems.at[w])
            own.start()
            local.append(own)
            for r, (cx, cy) in enumerate(chips):
                cp = copy(w, r, src[w].at[pl.ds(c * hr, hr)], piece(w, mychip, c), (cx, cy, c))
                cp.start()
                started.append(cp)
        for w in range(n):
            for r, (cx, cy) in enumerate(chips):
                landed = piece(w, 2 * cx + cy, c)
                copy(w, r, landed, landed, me).wait_recv()
                fwd = copy(w, 3 + r, landed, landed, sibling)
                fwd.start()
                started.append(fwd)
        for w in range(n):
            for r, (cx, cy) in enumerate(chips):
                from_sib = piece(w, 2 * cx + cy, 1 - c)
                copy(w, 3 + r, from_sib, from_sib, me).wait_recv()
        for cp in local:
            cp.wait()
        for cp in started:
            cp.wait_send()

    return pl.pallas_call(
        body, name="allgather_weights",
        in_specs=[VMEM_WHOLE] * n, out_specs=[VMEM_WHOLE] * n,
        out_shape=[_sds((N_CHIPS,) + s.shape, s.dtype) for s in shards],
        scratch_shapes=[pltpu.SemaphoreType.DMA((n, 6)), pltpu.SemaphoreType.DMA((n, 6)),
                        pltpu.SemaphoreType.DMA((n,))],
        compiler_params=pltpu.CompilerParams(vmem_limit_bytes=VMEM_LIMIT),
    )(*shards)


SUM_ROWS = 64


def _rs_pair_sum(name, grads):
    n = len(grads)

    def body(*refs):
        g, out = refs[:n], refs[n:2 * n]
        stage, land, keep = refs[2 * n:3 * n], refs[3 * n:4 * n], refs[4 * n:5 * n]
        send_sems, recv_sems, stage_sems, keep_sems = refs[5 * n:]
        x, y, c, _ = _place()
        sibling = (x, y, 1 - c)
        loads = []
        for w in range(n):
            hr = g[w].shape[1] // 2
            st = pltpu.make_async_copy(g[w].at[:, pl.ds((1 - c) * hr, hr)], stage[w], stage_sems.at[w])
            kp = pltpu.make_async_copy(g[w].at[:, pl.ds(c * hr, hr)], keep[w], keep_sems.at[w])
            st.start()
            kp.start()
            loads.append((st, kp))
        gives = []
        for w in range(n):
            loads[w][0].wait()
            give = pltpu.make_async_remote_copy(src_ref=stage[w], dst_ref=land[w], send_sem=send_sems.at[w],
                                                recv_sem=recv_sems.at[w], device_id=sibling, device_id_type=MESH)
            give.start()
            gives.append(give)
        for w in range(n):
            loads[w][1].wait()
            gives[w].wait_recv()
            nb = g[w].shape[1] // 2 // SUM_ROWS

            def add(idx, carry, w=w, nb=nb):
                k, r = idx // nb, pl.multiple_of((idx % nb) * SUM_ROWS, SUM_ROWS)
                rows = pl.ds(r, SUM_ROWS)
                out[w][k, rows, :] = (keep[w][k, rows, :] + land[w][k, rows, :]).astype(BF16)
                return carry

            lax.fori_loop(0, N_CHIPS * nb, add, 0)
        for give in gives:
            give.wait_send()

    half = [(N_CHIPS, a.shape[1] // 2, a.shape[2]) for a in grads]
    bufs = [pltpu.VMEM(s, F32) for s in half]
    sems = pltpu.SemaphoreType.DMA((n,))
    return pl.pallas_call(
        body, name=name,
        in_specs=[ANY] * n, out_specs=[VMEM_WHOLE] * n, out_shape=[_sds(s, BF16) for s in half],
        scratch_shapes=bufs + bufs + bufs + [sems, sems, sems, sems],
        compiler_params=pltpu.CompilerParams(vmem_limit_bytes=VMEM_LIMIT),
    )(*grads)


def _rs_exchange_join(parts):
    n = len(parts)

    def body(*refs):
        t, full, got = refs[:n], refs[n:2 * n], refs[2 * n:3 * n]
        send_sems, recv_sems = refs[3 * n:]
        x, y, c, chips = _place()
        mychip, sibling = 2 * x + y, (x, y, 1 - c)
        sends = []
        for w in range(n):
            for r, (cx, cy) in enumerate(chips):
                cp = pltpu.make_async_remote_copy(src_ref=t[w].at[2 * cx + cy], dst_ref=got[w].at[r],
                                                  send_sem=send_sems.at[w, r], recv_sem=recv_sems.at[w, r],
                                                  device_id=(cx, cy, c), device_id_type=MESH)
                cp.start()
                sends.append(cp)
        for w in range(n):
            hr = t[w].shape[1]
            for r in range(3):
                pltpu.make_async_remote_copy(src_ref=got[w].at[r], dst_ref=got[w].at[r], send_sem=send_sems.at[w, r],
                                             recv_sem=recv_sems.at[w, r], device_id=sibling,
                                             device_id_type=MESH).wait_recv()

            def add(idx, carry, w=w, hr=hr):
                r = pl.multiple_of(idx * SUM_ROWS, SUM_ROWS)
                rows = pl.ds(r, SUM_ROWS)
                f = lambda v: v.astype(F32)
                total = ((f(t[w][mychip, rows, :]) + f(got[w][0, rows, :])) + f(got[w][1, rows, :])) \
                    + f(got[w][2, rows, :])
                full[w][pl.ds(pl.multiple_of(c * hr + r, SUM_ROWS), SUM_ROWS), :] = total
                return carry

            lax.fori_loop(0, hr // SUM_ROWS, add, 0)
            mine = full[w].at[pl.ds(c * hr, hr)]
            give = pltpu.make_async_remote_copy(src_ref=mine, dst_ref=mine, send_sem=send_sems.at[w, 3],
                                                recv_sem=recv_sems.at[w, 3], device_id=sibling, device_id_type=MESH)
            give.start()
            sends.append(give)
        for w in range(n):
            hr = t[w].shape[1]
            theirs = full[w].at[pl.ds((1 - c) * hr, hr)]
            pltpu.make_async_remote_copy(src_ref=theirs, dst_ref=theirs, send_sem=send_sems.at[w, 3],
                                         recv_sem=recv_sems.at[w, 3], device_id=sibling, device_id_type=MESH).wait_recv()
        for cp in sends:
            cp.wait_send()

    return pl.pallas_call(
        body, name="rs_exchange_join",
        in_specs=[VMEM_WHOLE] * n, out_specs=[VMEM_WHOLE] * n,
        out_shape=[_sds((2 * a.shape[1], a.shape[2]), F32) for a in parts],
        scratch_shapes=[pltpu.VMEM((3,) + a.shape[1:], a.dtype) for a in parts]
        + [pltpu.SemaphoreType.DMA((n, 4)), pltpu.SemaphoreType.DMA((n, 4))],
        compiler_params=pltpu.CompilerParams(vmem_limit_bytes=VMEM_LIMIT),
    )(*parts)


def _small_allreduce(loss_p, dg_parts, dbg_a, dbg_c, dwc):
    ins = [loss_p] + list(dg_parts) + [dbg_a, dbg_c, dwc]
    n_in = len(ins)
    vmem = pl.BlockSpec(memory_space=pltpu.VMEM)

    def body(*refs):
        in_refs = refs[:n_in]
        out_ref, vec, buf, send_sems, recv_sems = refs[n_in:]
        x, y, c, _ = _place()
        me = 4 * x + 2 * y + c
        vec[...] = jnp.zeros_like(vec)
        vec[0:1, :] = jnp.sum(in_refs[0][...], axis=0)
        for r in range(5):
            vec[1 + r:2 + r, :] = jnp.sum(in_refs[1 + r][...], axis=0)
        vec[6:7, :] = jnp.sum(in_refs[6][...], axis=0)
        vec[7:8, :] = jnp.sum(in_refs[7][...], axis=0)
        vec[8:16, 0:CONV_W] = jnp.sum(in_refs[8][...], axis=0)
        buf[pl.ds(me, 1)] = vec[...][None]
        copies = []
        for r in range(1, 8):
            fx, fy, fc = (r >> 2) & 1, (r >> 1) & 1, r & 1
            to = (1 - x if fx else x, 1 - y if fy else y, 1 - c if fc else c)
            cp = pltpu.make_async_remote_copy(src_ref=vec, dst_ref=buf.at[me], send_sem=send_sems.at[r - 1],
                                              recv_sem=recv_sems.at[r - 1], device_id=to, device_id_type=MESH)
            cp.start()
            copies.append(cp)
        for cp in copies:
            cp.wait()
        total = buf[0]
        for s in range(1, 8):
            total = total + buf[s]
        out_ref[...] = total
        out_ref[0:1, :] = jnp.broadcast_to(jnp.sum(total[0:1, :], axis=-1, keepdims=True), (1, D_MODEL))

    return pl.pallas_call(
        body, name="small_allreduce",
        in_specs=[vmem] * n_in, out_specs=vmem, out_shape=_sds((SMALL_ROWS, D_MODEL), F32),
        scratch_shapes=[pltpu.VMEM((SMALL_ROWS, D_MODEL), F32), pltpu.VMEM((8, SMALL_ROWS, D_MODEL), F32),
                        pltpu.SemaphoreType.DMA((7,)), pltpu.SemaphoreType.DMA((7,))],
    )(*ins)


def _local_step(x, p, tgt, g, b_gate, w_conv, wf):
    seq = x.shape[0]
    tm = min(seq, 1024)
    th = min(seq, 512)
    ni, nh = seq // tm, seq // th
    g_pre_mix, g_post_mix, g_pre_mlp, g_post_mlp, g_ple = g
    w_in, w_ao, w_co, w_o, w_up, w_down, w_pg, w_pp = wf
    D = D_MODEL
    vec = lambda a, blk=0: (a, _bs((1, D), lambda i, j, k: (0, blk)))
    rows_i = lambda a, t, blk=0: (a, _bs((t, D), lambda i, j, k: (i, blk)))
    rows_k = lambda a, t, blk=0: (a, _bs((t, D), lambda i, j, k: (k, blk)))
    part = lambda n: (_sds((n, 1, D), F32), _bs((None, 1, D), lambda i, j, k: (i, 0, 0)))
    full2 = lambda a: (a, _bs(a.shape, lambda i, j, k: (0, 0)))

    (proj,) = _mm("proj_in", "nn", (ni, 4, 1),
                  a_ins=[rows_i(x, tm), vec(g_pre_mix)], a_fn=lambda xb, gb: _rms(xb, gb).astype(BF16),
                  b_ins=[(w_in, _bs((None, D, 1280), lambda i, j, k: (j, 0, 0)))], b_fn=_ident,
                  outs=[(_sds((seq, D_IN), F32), _bs((tm, 1280), lambda i, j, k: (i, j)))],
                  acc_shape=(tm, 1280), a_cache=((tm, D), BF16))
    qkv = _qkv_cast(proj, seq, tm)
    o, lta, ltb = _attn_fwd(qkv, seq)
    (y_attn,) = _mm("attn_out", "nn", (ni, 1, 1),
                    a_ins=[(o, _bs((tm, ATTN_W), lambda i, j, k: (i, 0)))], a_fn=_ident,
                    b_ins=[full2(w_ao)], b_fn=_ident,
                    outs=[(_sds((seq, D), F32), _bs((tm, D), lambda i, j, k: (i, 0)))], acc_shape=(tm, D))
    e, d = _conv_fwd(proj, w_conv, seq, tm)
    (y_conv,) = _mm("conv_out", "nn", (ni, 1, 1),
                    a_ins=[(e, _bs((tm, CONV_W), lambda i, j, k: (i, 0)))], a_fn=_ident,
                    b_ins=[full2(w_co)], b_fn=_ident,
                    outs=[(_sds((seq, D), F32), _bs((tm, D), lambda i, j, k: (i, 0)))], acc_shape=(tm, D))

    def mix_fn(ga, gc, ya, yc, ba, bc):
        return (_sig(ga + ba) * ya + _sig(gc + bc) * yc).astype(BF16)

    def post_mix(acc, xb, gb):
        return acc, xb + _rms(acc, gb)

    mix_ins = lambda rows: [rows(proj, th, 3), rows(proj, th, 4), rows(y_attn, th), rows(y_conv, th),
                            vec(b_gate, 0), vec(b_gate, 1)]
    mixed, x1 = _mm("mix_out", "nn", (nh, 1, 1),
                    a_ins=mix_ins(rows_i), a_fn=mix_fn, b_ins=[full2(w_o)], b_fn=_ident,
                    epi_ins=[rows_i(x, th), vec(g_post_mix)], epi_fn=post_mix,
                    outs=[(_sds((seq, D), F32), _bs((th, D), lambda i, j, k: (i, 0)))] * 2,
                    acc_shape=(th, D), a_cache=((th, D), BF16))
    (up,) = _mm("mlp_up", "nn", (ni, 4, 1),
                a_ins=[rows_i(x1, tm), vec(g_pre_mlp)], a_fn=lambda xb, gb: _rms(xb, gb).astype(BF16),
                b_ins=[(w_up, _bs((None, D, D), lambda i, j, k: (j, 0, 0)))], b_fn=_ident,
                outs=[(_sds((seq, D_FF), F32), _bs((tm, D), lambda i, j, k: (i, j)))],
                acc_shape=(tm, D), a_cache=((tm, D), BF16))

    def relu2(ub):
        r = jnp.maximum(ub, 0.0)
        return (r * r).astype(BF16)

    f, x2 = _mm("mlp_down", "nn", (nh, 1, 4),
                a_ins=[(up, _bs((th, D), lambda i, j, k: (i, k)))], a_fn=relu2,
                b_ins=[(w_down, _bs((D, D), lambda i, j, k: (k, 0)))], b_fn=_ident,
                epi_ins=[rows_i(x1, th), vec(g_post_mlp)], epi_fn=post_mix,
                outs=[(_sds((seq, D), F32), _bs((th, D), lambda i, j, k: (i, 0)))] * 2, acc_shape=(th, D))
    (pp,) = _mm("ple_proj", "nn", (ni, 1, 1),
                a_ins=[(p, _bs((tm, PLE_DIM), lambda i, j, k: (i, 0)))], a_fn=_to_bf16,
                b_ins=[full2(w_pp)], b_fn=_ident,
                outs=[(_sds((seq, D), F32), _bs((tm, D), lambda i, j, k: (i, 0)))], acc_shape=(tm, D))

    def head(acc, x2b, ppb, tb):
        pg = _sig(acc)
        err = x2b + pg * ppb - tb
        return pg, err * (1.0 / D), jnp.sum(err * err, axis=0, keepdims=True) * (0.5 / D)

    pg, dx3, loss_p = _mm("ple_gate_loss", "nn", (nh, 1, 1),
                          a_ins=[rows_i(x2, th), vec(g_ple)], a_fn=lambda xb, gb: _rms(xb, gb).astype(BF16),
                          b_ins=[full2(w_pg)], b_fn=_ident,
                          epi_ins=[rows_i(x2, th), rows_i(pp, th), rows_i(tgt, th)], epi_fn=head,
                          outs=[(_sds((seq, D), F32), _bs((th, D), lambda i, j, k: (i, 0)))] * 2 + [part(nh)],
                          acc_shape=(th, D), a_cache=((th, D), BF16))

    (dw_pp,) = _mm("dw_ple_proj", "tn", (1, 1, nh),
                   a_ins=[(p, _bs((th, PLE_DIM), lambda i, j, k: (k, 0)))], a_fn=_to_bf16,
                   b_ins=[rows_k(dx3, th), rows_k(pg, th)], b_fn=lambda a, b: (a * b).astype(BF16),
                   outs=[(_sds((PLE_DIM, D), F32), _bs((PLE_DIM, D), lambda i, j, k: (0, 0)))],
                   acc_shape=(PLE_DIM, D))

    def dpre_fn(dx3b, ppb, pgb):
        return (dx3b * ppb * pgb * (1.0 - pgb)).astype(BF16)

    def ple_norm_bwd(acc, x2b, dx3b, gb):
        dxn, dg = _rms_bwd(x2b, gb, acc)
        return dx3b + dxn, dg

    dx2, dg_ple_p, dpre = _mm("d_ple_gate", "nt", (nh, 1, 1),
                              a_ins=[rows_i(dx3, th), rows_i(pp, th), rows_i(pg, th)],
                              a_fn=lambda a, b, c: (dpre_fn(a, b, c),) * 2,
                              b_ins=[full2(w_pg)], b_fn=_ident,
                              epi_ins=[rows_i(x2, th), rows_i(dx3, th), vec(g_ple)], epi_fn=ple_norm_bwd,
                              outs=[(_sds((seq, D), F32), _bs((th, D), lambda i, j, k: (i, 0))), part(nh)],
                              acc_shape=(th, D), a_cache=((th, D), BF16),
                              a_outs=[(_sds((seq, D), BF16), _bs((th, D), lambda i, j, k: (i, 0)))])
    (dw_pg,) = _mm("dw_ple_gate", "tn", (1, 1, nh),
                   a_ins=[rows_k(x2, th), vec(g_ple)], a_fn=lambda xb, gb: _rms(xb, gb).astype(BF16),
                   b_ins=[rows_k(dpre, th)], b_fn=_ident,
                   outs=[(_sds((D, D), F32), _bs((D, D), lambda i, j, k: (0, 0)))], acc_shape=(D, D))

    def df_fn(fb, dx2b, gb):
        dfb, dg = _rms_bwd(fb, gb, dx2b)
        dfb = dfb.astype(BF16)
        return dfb, dfb, dg

    def dup_fn(acc, ub):
        return (acc * (2.0 * jnp.maximum(ub, 0.0)),)

    dup, df, dg_post_mlp_p = _mm("d_mlp_down", "nt", (nh, 4, 1),
                                 a_ins=[rows_i(f, th), rows_i(dx2, th), vec(g_post_mlp)], a_fn=df_fn,
                                 b_ins=[(w_down, _bs((D, D), lambda i, j, k: (j, 0)))], b_fn=_ident,
                                 epi_ins=[(up, _bs((th, D), lambda i, j, k: (i, j)))], epi_fn=dup_fn,
                                 outs=[(_sds((seq, D_FF), BF16), _bs((th, D), lambda i, j, k: (i, j)))],
                                 acc_shape=(th, D), a_cache=((th, D), BF16),
                                 a_outs=[(_sds((seq, D), BF16), _bs((th, D), lambda i, j, k: (i, 0))), part(nh)])
    (dw_down,) = _mm("dw_mlp_down", "tn", (4, 1, nh),
                     a_ins=[(up, _bs((th, D), lambda i, j, k: (k, i)))], a_fn=relu2,
                     b_ins=[rows_k(df, th)], b_fn=_ident,
                     outs=[(_sds((D_FF, D), F32), _bs((D, D), lambda i, j, k: (i, 0)))], acc_shape=(D, D))
    (dw_up,) = _mm("dw_mlp_up", "tn", (1, 4, nh),
                   a_ins=[rows_k(x1, th), vec(g_pre_mlp)], a_fn=lambda xb, gb: _rms(xb, gb).astype(BF16),
                   b_ins=[(dup, _bs((th, D), lambda i, j, k: (k, j)))], b_fn=_ident,
                   outs=[(_sds((N_CHIPS, D, D), F32), _bs((None, D, D), lambda i, j, k: (j, 0, 0)))],
                   acc_shape=(D, D))

    def mlp_norm_bwd(acc, x1b, dx2b, mixedb, g_mlp, g_mix):
        dxn, dg_mlp = _rms_bwd(x1b, g_mlp, acc)
        dx1b = dx2b + dxn
        dmixedb, dg_mix = _rms_bwd(mixedb, g_mix, dx1b)
        return dx1b, dmixedb, dg_mlp, dg_mix

    dx1, dmixed, dg_pre_mlp_p, dg_post_mix_p = _mm(
        "d_mlp_up", "nt", (nh, 1, 4),
        a_ins=[(dup, _bs((th, D), lambda i, j, k: (i, k)))], a_fn=_ident,
        b_ins=[(w_up, _bs((None, D, D), lambda i, j, k: (k, 0, 0)))], b_fn=_ident,
        epi_ins=[rows_i(x1, th), rows_i(dx2, th), rows_i(mixed, th), vec(g_pre_mlp), vec(g_post_mix)],
        epi_fn=mlp_norm_bwd,
        outs=[(_sds((seq, D), F32), _bs((th, D), lambda i, j, k: (i, 0))),
              (_sds((seq, D), BF16), _bs((th, D), lambda i, j, k: (i, 0))), part(nh), part(nh)],
        acc_shape=(th, D))
    (dw_o,) = _mm("dw_mix_out", "tn", (1, 1, nh),
                  a_ins=mix_ins(rows_k), a_fn=mix_fn, b_ins=[rows_k(dmixed, th)], b_fn=_ident,
                  outs=[(_sds((D, D), F32), _bs((D, D), lambda i, j, k: (0, 0)))], acc_shape=(D, D))

    def gate_bwd(acc, ga, gc, ya, yc, ba, bc):
        sa, sc = _sig(ga + ba), _sig(gc + bc)
        dga = acc * ya * sa * (1.0 - sa)
        dgc = acc * yc * sc * (1.0 - sc)
        return (acc * sa, acc * sc, jnp.concatenate([dga, dgc], axis=1),
                jnp.sum(dga, axis=0, keepdims=True), jnp.sum(dgc, axis=0, keepdims=True))

    dya, dyc, dgate, dbg_a_p, dbg_c_p = _mm(
        "d_mix_out", "nt", (nh, 1, 1),
        a_ins=[rows_i(dmixed, th)], a_fn=_ident, b_ins=[full2(w_o)], b_fn=_ident,
        epi_ins=mix_ins(rows_i), epi_fn=gate_bwd,
        outs=[(_sds((seq, D), BF16), _bs((th, D), lambda i, j, k: (i, 0)))] * 2
             + [(_sds((seq, 2 * D), BF16), _bs((th, 2 * D), lambda i, j, k: (i, 0))), part(nh), part(nh)],
        acc_shape=(th, D))
    (dw_ao,) = _mm("dw_attn_out", "tn", (1, 1, nh),
                   a_ins=[(o, _bs((th, ATTN_W), lambda i, j, k: (k, 0)))], a_fn=_ident,
                   b_ins=[rows_k(dya, th)], b_fn=_ident,
                   outs=[(_sds((ATTN_W, D), F32), _bs((ATTN_W, D), lambda i, j, k: (0, 0)))], acc_shape=(ATTN_W, D))
    (do,) = _mm("d_attn_out", "nt", (ni, 1, 1),
                a_ins=[rows_i(dya, tm)], a_fn=_ident, b_ins=[full2(w_ao)], b_fn=_ident,
                outs=[(_sds((seq, ATTN_W), BF16), _bs((tm, ATTN_W), lambda i, j, k: (i, 0)))],
                acc_shape=(tm, ATTN_W))
    dq, dk, dv = _attn_bwd(qkv, do, lta, ltb, seq)
    (dw_co,) = _mm("dw_conv_out", "tn", (1, 1, nh),
                   a_ins=[(e, _bs((th, CONV_W), lambda i, j, k: (k, 0)))], a_fn=_ident,
                   b_ins=[rows_k(dyc, th)], b_fn=_ident,
                   outs=[(_sds((CONV_W, D), F32), _bs((CONV_W, D), lambda i, j, k: (0, 0)))], acc_shape=(CONV_W, D))
    (de,) = _mm("d_conv_out", "nt", (ni, 1, 1),
                a_ins=[rows_i(dyc, tm)], a_fn=_ident, b_ins=[full2(w_co)], b_fn=_ident,
                outs=[(_sds((seq, CONV_W), F32), _bs((tm, CONV_W), lambda i, j, k: (i, 0)))],
                acc_shape=(tm, CONV_W))
    dconv, dwc_p = _conv_bwd(proj, de, d, w_conv, seq, tm)
    dproj = jnp.concatenate([dq.astype(BF16), dk.astype(BF16), dv.astype(BF16), dconv, dgate], axis=1)
    (dw_in,) = _mm("dw_proj_in", "tn", (1, 4, nh),
                   a_ins=[rows_k(x, th), vec(g_pre_mix)], a_fn=lambda xb, gb: _rms(xb, gb).astype(BF16),
                   b_ins=[(dproj, _bs((th, 1280), lambda i, j, k: (k, j)))], b_fn=_ident,
                   outs=[(_sds((N_CHIPS, D, 1280), F32), _bs((None, D, 1280), lambda i, j, k: (j, 0, 0)))],
                   acc_shape=(D, 1280))

    def in_norm_bwd(acc, xb, dx1b, gb):
        dxn, dg = _rms_bwd(xb, gb, acc)
        return dx1b + dxn, dg

    grad_x, dg_pre_mix_p = _mm("d_proj_in", "nt", (nh, 1, 4),
                               a_ins=[(dproj, _bs((th, 1280), lambda i, j, k: (i, k)))], a_fn=_ident,
                               b_ins=[(w_in, _bs((None, D, 1280), lambda i, j, k: (k, 0, 0)))], b_fn=_ident,
                               epi_ins=[rows_i(x, th), rows_i(dx1, th), vec(g_pre_mix)], epi_fn=in_norm_bwd,
                               outs=[(_sds((seq, D), F32), _bs((th, D), lambda i, j, k: (i, 0))), part(nh)],
                               acc_shape=(th, D))

    chip_major = lambda a: a.reshape(a.shape[0], N_CHIPS, a.shape[1] // N_CHIPS).transpose(1, 0, 2)
    big = [dw_in, chip_major(dw_ao), chip_major(dw_co), dw_o.reshape(N_CHIPS, D // N_CHIPS, D), dw_up,
           dw_down.reshape(N_CHIPS, D_FF // N_CHIPS, D), dw_pg.reshape(N_CHIPS, D // N_CHIPS, D), chip_major(dw_pp)]
    small = (loss_p, [dg_pre_mix_p, dg_post_mix_p, dg_pre_mlp_p, dg_post_mlp_p, dg_ple_p], dbg_a_p, dbg_c_p, dwc_p)
    return grad_x, big, small


RS_GROUPS = ((0,), (4,), (5,), (1, 2, 3, 6, 7))


def _reduce_scatter(big):
    pair = [None] * len(big)
    for gi, group in enumerate(RS_GROUPS):
        for w, s in zip(group, _rs_pair_sum(f"rs_pair_sum_{gi}", [big[w] for w in group])):
            pair[w] = s
    return _rs_exchange_join(pair)


def kernel(x, p, g_pre_mix, w_in, b_gate, w_conv, w_attn_out, w_conv_out, w_o, g_post_mix, g_pre_mlp, w_up, w_down, g_post_mlp, g_ple, w_ple_gate, w_ple_proj, loss_target, m_g_pre_mix, m_w_in, m_b_gate, m_w_conv, m_w_attn_out, m_w_conv_out, m_w_o, m_g_post_mix, m_g_pre_mlp, m_w_up, m_w_down, m_g_post_mlp, m_g_ple, m_w_ple_gate, m_w_ple_proj, v_g_pre_mix, v_w_in, v_b_gate, v_w_conv, v_w_attn_out, v_w_conv_out, v_w_o, v_g_post_mix, v_g_pre_mlp, v_w_up, v_w_down, v_g_post_mlp, v_g_ple, v_w_ple_gate, v_w_ple_proj):
    mats = [w_in, w_attn_out, w_conv_out, w_o, w_up, w_down, w_ple_gate, w_ple_proj]
    mats_m = [m_w_in, m_w_attn_out, m_w_conv_out, m_w_o, m_w_up, m_w_down, m_w_ple_gate, m_w_ple_proj]
    mats_v = [v_w_in, v_w_attn_out, v_w_conv_out, v_w_o, v_w_up, v_w_down, v_w_ple_gate, v_w_ple_proj]
    gains = [g_pre_mix, g_post_mix, g_pre_mlp, g_post_mlp, g_ple]
    gains_m = [m_g_pre_mix, m_g_post_mix, m_g_pre_mlp, m_g_post_mlp, m_g_ple]
    gains_v = [v_g_pre_mix, v_g_post_mix, v_g_pre_mlp, v_g_post_mlp, v_g_ple]

    taps = jnp.concatenate([w_conv[0], jnp.zeros((CONV_PAD_ROWS - 3, LANES), F32)], axis=0)
    gathered = _allgather_weights([w[0].astype(BF16) for w in mats] + [taps])
    cols_joined = lambda a: a.transpose(1, 0, 2).reshape(a.shape[1], N_CHIPS * a.shape[2])
    rows_joined = lambda a: a.reshape(N_CHIPS * a.shape[1], a.shape[2])
    wf = [gathered[0], cols_joined(gathered[1]), cols_joined(gathered[2]), rows_joined(gathered[3]), gathered[4],
          rows_joined(gathered[5]), rows_joined(gathered[6]), cols_joined(gathered[7])]
    w_conv_full = cols_joined(gathered[8])[0:3, :]
    chip = 2 * lax.axis_index("x") + lax.axis_index("y")

    grad_x, big, small = _local_step(x[0], p[0, 0], loss_target[0], gains, b_gate, w_conv_full, wf)

    shard_grads = _reduce_scatter(big)
    red = _small_allreduce(*small)
    loss = red[0, 0]
    grad_gains = [red[1 + r:2 + r, :] for r in range(5)]
    grad_b_gate = jnp.concatenate([red[6:7, :], red[7:8, :]], axis=1)
    grad_w_conv = lax.dynamic_slice(red[8:11, :], (0, chip * LANES), (3, LANES))[None]

    grads_big = [gr.reshape(w.shape) for gr, w in zip(shard_grads, mats)]
    upd_big = [_adamw(f"adamw_{i}", w, gr, m, v) for i, (w, gr, m, v) in enumerate(zip(mats, grads_big, mats_m, mats_v))]
    pack = lambda vs, bg: jnp.concatenate(list(vs) + [bg.reshape(2, D_MODEL), jnp.zeros((1, D_MODEL), F32)], axis=0)
    upd_small = _adamw("adamw_small", pack(gains, b_gate), pack(grad_gains, grad_b_gate),
                       pack(gains_m, m_b_gate), pack(gains_v, v_b_gate))
    upd_conv = _adamw("adamw_conv", w_conv, grad_w_conv, m_w_conv, v_w_conv)

    def small_out(a, which):
        gains_out = [a[r:r + 1, :] for r in range(5)]
        return gains_out, a[5:7, :].reshape(1, 2 * D_MODEL)

    def ordered(g_pre_mix_, big_, b_gate_, conv_, g_rest):
        return [g_pre_mix_, big_[0], b_gate_, conv_, big_[1], big_[2], big_[3], g_rest[0], g_rest[1], big_[4], big_[5],
                g_rest[2], g_rest[3], big_[6], big_[7]]

    outs = [loss, grad_x[None]]
    outs += ordered(grad_gains[0], grads_big, grad_b_gate, grad_w_conv, grad_gains[1:])
    for which in range(3):
        g_out, b_out = small_out(upd_small[which], which)
        outs += ordered(g_out[0], [u[which] for u in upd_big], b_out, upd_conv[which], g_out[1:])
    return tuple(outs)
```

```python
import functools

import jax
import jax.numpy as jnp
from jax import lax
from jax.experimental import pallas as pl
from jax.experimental.pallas import tpu as pltpu

F32 = jnp.float32
BF16 = jnp.bfloat16
MESH = pl.DeviceIdType.MESH

D_MODEL = 1024
N_HEADS = 8
HEAD_DIM = 64
ATTN_W = N_HEADS * HEAD_DIM
CONV_W = 512
D_FF = 4096
PLE_DIM = 256
D_IN = 5120
N_CHIPS = 4
EPS = 1e-6
Q_SCALE = HEAD_DIM ** -0.5

ADAM_LR = 0.001
ADAM_B1 = 0.9
ADAM_B2 = 0.999
ADAM_EPS = 1e-08
ADAM_WD = 0.01
ADAM_STEP = 10

V7X_VMEM_BYTES = 64 * 1024 * 1024
VMEM_LIMIT = V7X_VMEM_BYTES - 8 * 1024 * 1024
LANES = 128
ATT_BLK = 256
SMALL_ROWS = 16
CONV_PAD_ROWS = 16


def _cparams(n_grid):
    return pltpu.CompilerParams(dimension_semantics=("arbitrary",) * n_grid, vmem_limit_bytes=VMEM_LIMIT)


def _bs(shape, fn):
    return pl.BlockSpec(shape, fn)


def _rms_stats(xf):
    return lax.rsqrt(jnp.mean(xf * xf, axis=-1, keepdims=True) + EPS)


def _rms(xf, g):
    return xf * _rms_stats(xf) * g


def _rms_bwd(xf, g, dy):
    r = _rms_stats(xf)
    xh = xf * r
    dyg = dy * g
    dx = r * (dyg - xh * jnp.mean(dyg * xh, axis=-1, keepdims=True))
    return dx, jnp.sum(dy * xh, axis=0, keepdims=True)


def _sig(z):
    return 1.0 / (1.0 + jnp.exp(-z))


def _ident(a):
    return a


def _to_bf16(a):
    return a.astype(BF16)


_DIMS = {"nn": (((1,), (0,)), ((), ())), "nt": (((1,), (1,)), ((), ())), "tn": (((0,), (0,)), ((), ()))}


def _mm(name, mode, grid, a_ins, a_fn, b_ins, b_fn, outs, acc_shape, epi_ins=(), epi_fn=None,
        a_cache=None, a_outs=()):
    nk = grid[2]
    na, nb, ne, no, nao = len(a_ins), len(b_ins), len(epi_ins), len(outs), len(a_outs)
    assert a_cache is None or nk == 1
    assert not a_outs or a_cache is not None
    dims = _DIMS[mode]
    if epi_fn is None:
        epi_fn = lambda acc: (acc,)

    def body(*refs):
        a_refs = refs[:na]
        b_refs = refs[na:na + nb]
        e_refs = refs[na + nb:na + nb + ne]
        o_refs = refs[na + nb + ne:na + nb + ne + no]
        ao_refs = refs[na + nb + ne + no:na + nb + ne + no + nao]
        scratch = list(refs[na + nb + ne + no + nao:])
        acc_ref = scratch.pop(0) if nk > 1 else None
        a_sc = scratch.pop(0) if a_cache is not None else None
        j = pl.program_id(1)
        k = pl.program_id(2)

        def finish(acc):
            res = epi_fn(acc, *[r[...] for r in e_refs])
            for r, val in zip(o_refs, res):
                r[...] = val.astype(r.dtype)

        if a_sc is not None:
            @pl.when(j == 0)
            def _():
                res = a_fn(*[r[...] for r in a_refs])
                if nao:
                    for r, val in zip(ao_refs, res[1:]):
                        r[...] = val.astype(r.dtype)
                    res = res[0]
                a_sc[...] = res
            a = a_sc[...]
        else:
            a = a_fn(*[r[...] for r in a_refs])
        b = b_fn(*[r[...] for r in b_refs])
        prod = lax.dot_general(a, b, dims, preferred_element_type=F32)
        if nk == 1:
            finish(prod)
        else:
            @pl.when(k == 0)
            def _():
                acc_ref[...] = prod

            @pl.when(k > 0)
            def _():
                acc_ref[...] += prod

            @pl.when(k == nk - 1)
            def _():
                finish(acc_ref[...])

    scratch_shapes = []
    if nk > 1:
        scratch_shapes.append(pltpu.VMEM(acc_shape, F32))
    if a_cache is not None:
        scratch_shapes.append(pltpu.VMEM(*a_cache))
    all_outs = list(outs) + list(a_outs)
    res = pl.pallas_call(
        body, name=name, grid=grid,
        in_specs=[s for _, s in a_ins] + [s for _, s in b_ins] + [s for _, s in epi_ins],
        out_specs=[s for _, s in all_outs],
        out_shape=[o for o, _ in all_outs],
        scratch_shapes=scratch_shapes,
        compiler_params=_cparams(3),
    )(*[a for a, _ in a_ins], *[a for a, _ in b_ins], *[a for a, _ in epi_ins])
    return res


def _sds(shape, dtype):
    return jax.ShapeDtypeStruct(shape, dtype)


def _qkv_cast(proj, seq, tr):
    def body(p_ref, o_ref):
        scale = jnp.where(pl.program_id(1) == 0, Q_SCALE, 1.0).astype(F32)
        o_ref[...] = (p_ref[...] * scale).astype(BF16)

    return pl.pallas_call(
        body, name="qkv_cast", grid=(seq // tr, 3),
        in_specs=[_bs((tr, ATTN_W), lambda i, c: (i, c))],
        out_specs=_bs((tr, ATTN_W), lambda i, c: (i, c)),
        out_shape=_sds((seq, 3 * ATTN_W), BF16),
        compiler_params=_cparams(2),
    )(proj)


def _shift_rows_down(u, prev, n):
    rows = u.shape[0]
    ridx = lax.broadcasted_iota(jnp.int32, u.shape, 0)
    out = pltpu.roll(u, n, 0)
    for r in range(n):
        out = jnp.where(ridx == r, prev[8 - n + r:8 - n + r + 1, :], out)
    del rows
    return out


def _shift_rows_up(u, nxt, n):
    rows = u.shape[0]
    ridx = lax.broadcasted_iota(jnp.int32, u.shape, 0)
    out = pltpu.roll(u, rows - n, 0)
    for r in range(n):
        out = jnp.where(ridx == rows - n + r, nxt[r:r + 1, :], out)
    return out


CONV_COL0 = 3


def _conv_fwd(proj, w_conv, seq, tr):
    hb = tr // 8

    def body(cb_ref, cc_ref, cu_ref, ccp_ref, cup_ref, w_ref, e_ref, d_ref):
        i = pl.program_id(0)
        u = cc_ref[...] * cu_ref[...]
        up = jnp.where(i > 0, ccp_ref[...] * cup_ref[...], 0.0)
        w = w_ref[...]
        d = w[0:1, :] * _shift_rows_down(u, up, 2) + w[1:2, :] * _shift_rows_down(u, up, 1) + w[2:3, :] * u
        d_ref[...] = d
        e_ref[...] = (cb_ref[...] * d).astype(BF16)

    prev = lambda c: (lambda i: (jnp.maximum(i * hb - 1, 0), c))
    return pl.pallas_call(
        body, name="conv_fwd", grid=(seq // tr,),
        in_specs=[_bs((tr, CONV_W), lambda i: (i, CONV_COL0)),
                  _bs((tr, CONV_W), lambda i: (i, CONV_COL0 + 1)),
                  _bs((tr, CONV_W), lambda i: (i, CONV_COL0 + 2)),
                  _bs((8, CONV_W), prev(CONV_COL0 + 1)),
                  _bs((8, CONV_W), prev(CONV_COL0 + 2)),
                  _bs((3, CONV_W), lambda i: (0, 0))],
        out_specs=[_bs((tr, CONV_W), lambda i: (i, 0)), _bs((tr, CONV_W), lambda i: (i, 0))],
        out_shape=[_sds((seq, CONV_W), BF16), _sds((seq, CONV_W), F32)],
        compiler_params=_cparams(1),
    )(proj, proj, proj, proj, proj, w_conv)


def _conv_bwd(proj, de, d, w_conv, seq, tr):
    hb = tr // 8
    nblk = seq // tr

    def body(cb_ref, cc_ref, cu_ref, ccp_ref, cup_ref, cbn_ref, de_ref, den_ref, d_ref, w_ref, o_ref, dw_ref):
        i = pl.program_id(0)
        cc, cu, cb = cc_ref[...], cu_ref[...], cb_ref[...]
        u = cc * cu
        up = jnp.where(i > 0, ccp_ref[...] * cup_ref[...], 0.0)
        u1 = _shift_rows_down(u, up, 1)
        u2 = _shift_rows_down(u, up, 2)
        de_ = de_ref[...]
        dd = de_ * cb
        ddn = jnp.where(i < nblk - 1, den_ref[...] * cbn_ref[...], 0.0)
        w = w_ref[...]
        du = w[2:3, :] * dd + w[1:2, :] * _shift_rows_up(dd, ddn, 1) + w[0:1, :] * _shift_rows_up(dd, ddn, 2)
        o_ref[:, 0:CONV_W] = (de_ * d_ref[...]).astype(BF16)
        o_ref[:, CONV_W:2 * CONV_W] = (du * cu).astype(BF16)
        o_ref[:, 2 * CONV_W:3 * CONV_W] = (du * cc).astype(BF16)
        ridx = lax.broadcasted_iota(jnp.int32, (8, CONV_W), 0)
        dw0 = jnp.sum(dd * u2, axis=0, keepdims=True)
        dw1 = jnp.sum(dd * u1, axis=0, keepdims=True)
        dw2 = jnp.sum(dd * u, axis=0, keepdims=True)
        dw_ref[...] = jnp.where(ridx == 0, dw0, jnp.where(ridx == 1, dw1, jnp.where(ridx == 2, dw2, 0.0)))

    prev = lambda c: (lambda i: (jnp.maximum(i * hb - 1, 0), c))
    nxt = lambda c: (lambda i: (jnp.minimum((i + 1) * hb, seq // 8 - 1), c))
    return pl.pallas_call(
        body, name="conv_bwd", grid=(nblk,),
        in_specs=[_bs((tr, CONV_W), lambda i: (i, CONV_COL0)),
                  _bs((tr, CONV_W), lambda i: (i, CONV_COL0 + 1)),
                  _bs((tr, CONV_W), lambda i: (i, CONV_COL0 + 2)),
                  _bs((8, CONV_W), prev(CONV_COL0 + 1)),
                  _bs((8, CONV_W), prev(CONV_COL0 + 2)),
                  _bs((8, CONV_W), nxt(CONV_COL0)),
                  _bs((tr, CONV_W), lambda i: (i, 0)),
                  _bs((8, CONV_W), nxt(0)),
                  _bs((tr, CONV_W), lambda i: (i, 0)),
                  _bs((3, CONV_W), lambda i: (0, 0))],
        out_specs=[_bs((tr, 3 * CONV_W), lambda i: (i, 0)), _bs((None, 8, CONV_W), lambda i: (i, 0, 0))],
        out_shape=[_sds((seq, 3 * CONV_W), BF16), _sds((nblk, 8, CONV_W), F32)],
        compiler_params=_cparams(1),
    )(proj, proj, proj, proj, proj, proj, de, de, d, w_conv)


def _nt(a, b):
    return lax.dot_general(a, b, _DIMS["nt"], preferred_element_type=F32)


def _tn(a, b):
    return lax.dot_general(a, b, _DIMS["tn"], preferred_element_type=F32)


def _nn(a, b):
    return lax.dot_general(a, b, _DIMS["nn"], preferred_element_type=F32)


def _log_gates(z):
    lse = jnp.log(1.0 + jnp.exp(-jnp.abs(z)))
    log_beta = jnp.minimum(z, 0.0) - lse
    return log_beta, log_beta - z


def _attn_fwd(qkv, seq):
    blk = ATT_BLK
    nq = seq // blk
    npair = N_HEADS // 2

    def body(q_ref, k_ref, v_ref, o_ref, lta_ref, ltb_ref, z0_sc, z1_sc, w0_sc, w1_sc, tot_sc, acc_sc):
        i = pl.program_id(1)
        is_a = lax.broadcasted_iota(jnp.int32, (1, LANES), 1) < HEAD_DIM
        q2 = q_ref[...]
        zero = jnp.zeros_like(q2)
        qs = (jnp.where(is_a, q2, zero), jnp.where(is_a, zero, q2))
        row = lax.broadcasted_iota(jnp.int32, (blk, blk), 0)
        col = lax.broadcasted_iota(jnp.int32, (blk, blk), 1)
        tri = (row > col).astype(BF16)
        causal = col < row

        def scores(j):
            k2 = k_ref[pl.ds(pl.multiple_of(j * blk, blk), blk), :]
            return [_nt(qs[h], k2) for h in range(2)]

        def weights(zs, tot, diag):
            gates = [_log_gates(z) for z in zs]
            keeps = [jnp.where(causal, g[1], 0.0) if diag else g[1] for g in gates]
            sums = [_nn(lk.astype(BF16), tri) for lk in keeps]
            ws, new_tot = [], []
            for h in range(2):
                w = jnp.exp(gates[h][0] + (tot[h] + sums[h]))
                if diag:
                    w = jnp.where(causal, w, 0.0)
                ws.append(w.astype(BF16))
                new_tot.append(tot[h] + jnp.sum(keeps[h], axis=-1, keepdims=True))
            return ws, new_tot

        def values(ws, j, acc):
            v2 = v_ref[pl.ds(pl.multiple_of(j * blk, blk), blk), :]
            return acc + jnp.where(is_a, _nn(ws[0], v2), _nn(ws[1], v2))

        z_bufs, w_bufs = (z0_sc, z1_sc), (w0_sc, w1_sc)

        def put(ref, vals):
            for h in range(2):
                ref[h] = vals[h]

        def trip(j, s):
            acc_sc[...] = values((w_bufs[s][0], w_bufs[s][1]), j + 1, acc_sc[...])
            put(z_bufs[1 - s], scores(jnp.maximum(j - 1, 0)))
            ws, tot = weights((z_bufs[s][0], z_bufs[s][1]), [tot_sc[0], tot_sc[1]], False)
            put(w_bufs[1 - s], ws)
            put(tot_sc, tot)

        zero_col = jnp.zeros((blk, 1), F32)
        ws, tot = weights(scores(i), [zero_col, zero_col], True)
        put(w0_sc, ws)
        put(tot_sc, tot)
        put(z0_sc, scores(jnp.maximum(i - 1, 0)))
        acc_sc[...] = jnp.zeros_like(acc_sc)

        def two_trips(pp, carry):
            j = i - 1 - 2 * pp
            trip(j, 0)
            trip(j - 1, 1)
            return carry

        lax.fori_loop(0, i // 2, two_trips, 0)
        odd = i % 2 == 1

        @pl.when(odd)
        def _():
            trip(0, 0)

        @pl.when(odd)
        def _():
            o_ref[...] = values((w1_sc[0], w1_sc[1]), 0, acc_sc[...]).astype(BF16)

        @pl.when(jnp.logical_not(odd))
        def _():
            o_ref[...] = values((w0_sc[0], w0_sc[1]), 0, acc_sc[...]).astype(BF16)

        lta_ref[...] = jnp.broadcast_to(tot_sc[0], (blk, LANES))
        ltb_ref[...] = jnp.broadcast_to(tot_sc[1], (blk, LANES))

    return pl.pallas_call(
        body, name="attn_fwd", grid=(npair, nq),
        in_specs=[_bs((blk, LANES), lambda p, i: (i, p)),
                  _bs((seq, LANES), lambda p, i: (0, npair + p)),
                  _bs((seq, LANES), lambda p, i: (0, 2 * npair + p))],
        out_specs=[_bs((blk, LANES), lambda p, i: (i, p))] * 3,
        out_shape=[_sds((seq, ATTN_W), BF16), _sds((seq, ATTN_W), F32), _sds((seq, ATTN_W), F32)],
        scratch_shapes=[pltpu.VMEM((2, blk, blk), F32), pltpu.VMEM((2, blk, blk), F32),
                        pltpu.VMEM((2, blk, blk), BF16), pltpu.VMEM((2, blk, blk), BF16),
                        pltpu.VMEM((2, blk, 1), F32), pltpu.VMEM((blk, LANES), F32)],
        compiler_params=_cparams(2),
    )(qkv, qkv, qkv)


def _attn_bwd(qkv, do, lta, ltb, seq):
    blk = ATT_BLK
    nq = seq // blk
    npair = N_HEADS // 2

    def body(q_ref, k_ref, v_ref, do_ref, lta_ref, ltb_ref, dq_ref, dk_ref, dv_ref,
             prod0_sc, prod1_sc, pend0_sc, pend1_sc, cum_sc, pre_sc, dq_sc):
        i = pl.program_id(1)

        @pl.when(i == 0)
        def _():
            dk_ref[...] = jnp.zeros_like(dk_ref)
            dv_ref[...] = jnp.zeros_like(dv_ref)

        is_a = lax.broadcasted_iota(jnp.int32, (1, LANES), 1) < HEAD_DIM
        q2 = q_ref[...]
        do2 = do_ref[...]
        zero = jnp.zeros_like(q2)
        qs = (jnp.where(is_a, q2, zero), jnp.where(is_a, zero, q2))
        dos = (jnp.where(is_a, do2, zero), jnp.where(is_a, zero, do2))
        ltot = (jnp.max(lta_ref[...], axis=-1, keepdims=True), jnp.max(ltb_ref[...], axis=-1, keepdims=True))
        row = lax.broadcasted_iota(jnp.int32, (blk, blk), 0)
        col = lax.broadcasted_iota(jnp.int32, (blk, blk), 1)
        tri_after = (row > col).astype(BF16)
        tri_excl = (row < col).astype(BF16)
        causal = col < row

        prod_bufs, pend_bufs = (prod0_sc, prod1_sc), (pend0_sc, pend1_sc)

        def products(j):
            off = pl.multiple_of(j * blk, blk)
            k2 = k_ref[pl.ds(off, blk), :]
            v2 = v_ref[pl.ds(off, blk), :]
            return [_nt(qs[h], k2) for h in range(2)] + [_nt(dos[h], v2) for h in range(2)]

        def local_grads(prods, diag):
            zs, dws = prods[:2], prods[2:]
            gates = [_log_gates(z) for z in zs]
            keeps = [jnp.where(causal, g[1], 0.0) if diag else g[1] for g in gates]
            sums = [_nn(lk.astype(BF16), tri_after) for lk in keeps]
            ws, gs = [], []
            for h in range(2):
                cum = cum_sc[h] + jnp.sum(keeps[h], axis=-1, keepdims=True)
                cum_sc[h] = cum
                w = jnp.exp(gates[h][0] + ((ltot[h] - cum) + sums[h]))
                if diag:
                    w = jnp.where(causal, w, 0.0)
                ws.append(w)
                gs.append(dws[h] * w)
            befores = [_nn(g.astype(BF16), tri_excl) for g in gs]
            dzs = []
            for h in range(2):
                beta = jnp.exp(gates[h][0])
                dz = gs[h] * (1.0 - beta) - (pre_sc[h] + befores[h]) * beta
                if diag:
                    dz = jnp.where(causal, dz, 0.0)
                dzs.append(dz.astype(BF16))
                pre_sc[h] = pre_sc[h] + jnp.sum(gs[h], axis=-1, keepdims=True)
            return [w.astype(BF16) for w in ws] + dzs

        def grad_matmuls(pend, j):
            off = pl.multiple_of(j * blk, blk)
            k2 = k_ref[pl.ds(off, blk), :]
            dq_sc[...] += jnp.where(is_a, _nn(pend[2], k2), _nn(pend[3], k2))
            dk_ref[pl.ds(off, blk), :] += jnp.where(is_a, _tn(pend[2], q2), _tn(pend[3], q2))
            dv_ref[pl.ds(off, blk), :] += jnp.where(is_a, _tn(pend[0], do2), _tn(pend[1], do2))

        def put(ref, vals):
            for n, val in enumerate(vals):
                ref[n] = val

        def take(ref):
            return [ref[n] for n in range(4)]

        def trip(j, s):
            grad_matmuls(take(pend_bufs[s]), jnp.maximum(j - 1, 0))
            put(prod_bufs[1 - s], products(j + 1))
            put(pend_bufs[1 - s], local_grads(take(prod_bufs[s]), False))

        cum_sc[...] = jnp.zeros_like(cum_sc)
        pre_sc[...] = jnp.zeros_like(pre_sc)
        dq_sc[...] = jnp.zeros_like(dq_sc)
        pend0_sc[...] = jnp.zeros_like(pend0_sc)
        put(prod0_sc, products(0))

        def two_trips(pp, carry):
            trip(2 * pp, 0)
            trip(2 * pp + 1, 1)
            return carry

        lax.fori_loop(0, i // 2, two_trips, 0)
        odd = i % 2 == 1

        @pl.when(odd)
        def _():
            trip(i - 1, 0)

        def finish(s):
            grad_matmuls(take(pend_bufs[s]), jnp.maximum(i - 1, 0))
            grad_matmuls(local_grads(take(prod_bufs[s]), True), i)
            dq_ref[...] = dq_sc[...] * Q_SCALE

        @pl.when(odd)
        def _():
            finish(1)

        @pl.when(jnp.logical_not(odd))
        def _():
            finish(0)

    qmap = lambda p, i: (i, p)
    return pl.pallas_call(
        body, name="attn_bwd", grid=(npair, nq),
        in_specs=[_bs((blk, LANES), qmap),
                  _bs((seq, LANES), lambda p, i: (0, npair + p)),
                  _bs((seq, LANES), lambda p, i: (0, 2 * npair + p)),
                  _bs((blk, LANES), qmap), _bs((blk, LANES), qmap), _bs((blk, LANES), qmap)],
        out_specs=[_bs((blk, LANES), qmap),
                   _bs((seq, LANES), lambda p, i: (0, p)),
                   _bs((seq, LANES), lambda p, i: (0, p))],
        out_shape=[_sds((seq, ATTN_W), F32)] * 3,
        scratch_shapes=[pltpu.VMEM((4, blk, blk), F32), pltpu.VMEM((4, blk, blk), F32),
                        pltpu.VMEM((4, blk, blk), BF16), pltpu.VMEM((4, blk, blk), BF16),
                        pltpu.VMEM((2, blk, 1), F32), pltpu.VMEM((2, blk, 1), F32), pltpu.VMEM((blk, LANES), F32)],
        compiler_params=_cparams(2),
    )(qkv, qkv, qkv, do, lta, ltb)


def _elementwise(name, fn, ins, out_dtypes):
    rows, cols = ins[0].shape
    tr = rows
    for cand in (512, 256, 128, 64, 32, 16, 8):
        if rows % cand == 0 and cand * cols * 4 <= 2 * 1024 * 1024:
            tr = cand
            break
    n_in = len(ins)

    def body(*refs):
        res = fn(*[r[...] for r in refs[:n_in]])
        for r, val in zip(refs[n_in:], res):
            r[...] = val.astype(r.dtype)

    spec = _bs((tr, cols), lambda i: (i, 0))
    return pl.pallas_call(
        body, name=name, grid=(rows // tr,),
        in_specs=[spec] * n_in, out_specs=[spec] * len(out_dtypes),
        out_shape=[_sds((rows, cols), dt) for dt in out_dtypes],
        compiler_params=_cparams(1),
    )(*ins)


def _adamw_fn(w, g, m, v):
    m = ADAM_B1 * m + (1.0 - ADAM_B1) * g
    v = ADAM_B2 * v + (1.0 - ADAM_B2) * (g * g)
    m_hat = m / (1.0 - ADAM_B1 ** ADAM_STEP)
    v_hat = v / (1.0 - ADAM_B2 ** ADAM_STEP)
    delta = -ADAM_LR * (m_hat / (jnp.sqrt(v_hat) + ADAM_EPS) + ADAM_WD * w)
    return delta, m, v


def _adamw(name, w, g, m, v):
    shape = w.shape
    as2d = lambda a: a.reshape(-1, shape[-1])
    delta, nm, nv = _elementwise(name, _adamw_fn, [as2d(w), as2d(g), as2d(m), as2d(v)], [F32, F32, F32])
    return delta.reshape(shape), nm.reshape(shape), nv.reshape(shape)


def _place():
    x, y, c = lax.axis_index("x"), lax.axis_index("y"), lax.axis_index("c")
    chips = [(1 - x, y), (x, 1 - y), (1 - x, 1 - y)]
    return x, y, c, chips


ANY = pl.BlockSpec(memory_space=pl.ANY)
VMEM_WHOLE = pl.BlockSpec(memory_space=pltpu.VMEM)


def _allgather_weights(shards):
    n = len(shards)

    def body(*refs):
        src, dst = refs[:n], refs[n:2 * n]
        send_sems, recv_sems, local_sems = refs[2 * n:]
        x, y, c, chips = _place()
        me, sibling, mychip = (x, y, c), (x, y, 1 - c), 2 * x + y

        def piece(w, chip, half):
            hr = src[w].shape[0] // 2
            return dst[w].at[chip, pl.ds(half * hr, hr)]

        def copy(w, k, src_ref, dst_ref, to):
            return pltpu.make_async_remote_copy(src_ref=src_ref, dst_ref=dst_ref, send_sem=send_sems.at[w, k],
                                                recv_sem=recv_sems.at[w, k], device_id=to, device_id_type=MESH)

        started, local = [], []
        for w in range(n):
            hr = src[w].shape[0] // 2
            own = pltpu.make_async_copy(src[w], dst[w].at[mychip], local_sems.at[w])
            own.start()
            local.append(own)
            for r, (cx, cy) in enumerate(chips):
                cp = copy(w, r, src[w].at[pl.ds(c * hr, hr)], piece(w, mychip, c), (cx, cy, c))
                cp.start()
                started.append(cp)
        for w in range(n):
            for r, (cx, cy) in enumerate(chips):
                landed = piece(w, 2 * cx + cy, c)
                copy(w, r, landed, landed, me).wait_recv()
                fwd = copy(w, 3 + r, landed, landed, sibling)
                fwd.start()
                started.append(fwd)
        for w in range(n):
            for r, (cx, cy) in enumerate(chips):
                from_sib = piece(w, 2 * cx + cy, 1 - c)
                copy(w, 3 + r, from_sib, from_sib, me).wait_recv()
        for cp in local:
            cp.wait()
        for cp in started:
            cp.wait_send()

    return pl.pallas_call(
        body, name="allgather_weights",
        in_specs=[VMEM_WHOLE] * n, out_specs=[VMEM_WHOLE] * n,
        out_shape=[_sds((N_CHIPS,) + s.shape, s.dtype) for s in shards],
        scratch_shapes=[pltpu.SemaphoreType.DMA((n, 6)), pltpu.SemaphoreType.DMA((n, 6)),
                        pltpu.SemaphoreType.DMA((n,))],
        compiler_params=pltpu.CompilerParams(vmem_limit_bytes=VMEM_LIMIT),
    )(*shards)


SUM_ROWS = 64


def _rs_pair_sum(name, grads):
    n = len(grads)

    def body(*refs):
        g, out = refs[:n], refs[n:2 * n]
        stage, land, keep = refs[2 * n:3 * n], refs[3 * n:4 * n], refs[4 * n:5 * n]
        send_sems, recv_sems, stage_sems, keep_sems = refs[5 * n:]
        x, y, c, _ = _place()
        sibling = (x, y, 1 - c)
        loads = []
        for w in range(n):
            hr = g[w].shape[1] // 2
            st = pltpu.make_async_copy(g[w].at[:, pl.ds((1 - c) * hr, hr)], stage[w], stage_sems.at[w])
            kp = pltpu.make_async_copy(g[w].at[:, pl.ds(c * hr, hr)], keep[w], keep_sems.at[w])
            st.start()
            kp.start()
            loads.append((st, kp))
        gives = []
        for w in range(n):
            loads[w][0].wait()
            give = pltpu.make_async_remote_copy(src_ref=stage[w], dst_ref=land[w], send_sem=send_sems.at[w],
                                                recv_sem=recv_sems.at[w], device_id=sibling, device_id_type=MESH)
            give.start()
            gives.append(give)
        for w in range(n):
            loads[w][1].wait()
            gives[w].wait_recv()
            nb = g[w].shape[1] // 2 // SUM_ROWS

            def add(idx, carry, w=w, nb=nb):
                k, r = idx // nb, pl.multiple_of((idx % nb) * SUM_ROWS, SUM_ROWS)
                rows = pl.ds(r, SUM_ROWS)
                out[w][k, rows, :] = (keep[w][k, rows, :] + land[w][k, rows, :]).astype(BF16)
                return carry

            lax.fori_loop(0, N_CHIPS * nb, add, 0)
        for give in gives:
            give.wait_send()

    half = [(N_CHIPS, a.shape[1] // 2, a.shape[2]) for a in grads]
    bufs = [pltpu.VMEM(s, F32) for s in half]
    sems = pltpu.SemaphoreType.DMA((n,))
    return pl.pallas_call(
        body, name=name,
        in_specs=[ANY] * n, out_specs=[VMEM_WHOLE] * n, out_shape=[_sds(s, BF16) for s in half],
        scratch_shapes=bufs + bufs + bufs + [sems, sems, sems, sems],
        compiler_params=pltpu.CompilerParams(vmem_limit_bytes=VMEM_LIMIT),
    )(*grads)


def _rs_exchange_join(parts):
    n = len(parts)

    def body(*refs):
        t, full, got = refs[:n], refs[n:2 * n], refs[2 * n:3 * n]
        send_sems, recv_sems = refs[3 * n:]
        x, y, c, chips = _place()
        mychip, sibling = 2 * x + y, (x, y, 1 - c)
        sends = []
        for w in range(n):
            for r, (cx, cy) in enumerate(chips):
                cp = pltpu.make_async_remote_copy(src_ref=t[w].at[2 * cx + cy], dst_ref=got[w].at[r],
                                                  send_sem=send_sems.at[w, r], recv_sem=recv_sems.at[w, r],
                                                  device_id=(cx, cy, c), device_id_type=MESH)
                cp.start()
                sends.append(cp)
        for w in range(n):
            hr = t[w].shape[1]
            for r in range(3):
                pltpu.make_async_remote_copy(src_ref=got[w].at[r], dst_ref=got[w].at[r], send_sem=send_sems.at[w, r],
                                             recv_sem=recv_sems.at[w, r], device_id=sibling,
                                             device_id_type=MESH).wait_recv()

            def add(idx, carry, w=w, hr=hr):
                r = pl.multiple_of(idx * SUM_ROWS, SUM_ROWS)
                rows = pl.ds(r, SUM_ROWS)
                f = lambda v: v.astype(F32)
                total = ((f(t[w][mychip, rows, :]) + f(got[w][0, rows, :])) + f(got[w][1, rows, :])) \
                    + f(got[w][2, rows, :])
                full[w][pl.ds(pl.multiple_of(c * hr + r, SUM_ROWS), SUM_ROWS), :] = total
                return carry

            lax.fori_loop(0, hr // SUM_ROWS, add, 0)
            mine = full[w].at[pl.ds(c * hr, hr)]
            give = pltpu.make_async_remote_copy(src_ref=mine, dst_ref=mine, send_sem=send_sems.at[w, 3],
                                                recv_sem=recv_sems.at[w, 3], device_id=sibling, device_id_type=MESH)
            give.start()
            sends.append(give)
        for w in range(n):
            hr = t[w].shape[1]
            theirs = full[w].at[pl.ds((1 - c) * hr, hr)]
            pltpu.make_async_remote_copy(src_ref=theirs, dst_ref=theirs, send_sem=send_sems.at[w, 3],
                                         recv_sem=recv_sems.at[w, 3], device_id=sibling, device_id_type=MESH).wait_recv()
        for cp in sends:
            cp.wait_send()

    return pl.pallas_call(
        body, name="rs_exchange_join",
        in_specs=[VMEM_WHOLE] * n, out_specs=[VMEM_WHOLE] * n,
        out_shape=[_sds((2 * a.shape[1], a.shape[2]), F32) for a in parts],
        scratch_shapes=[pltpu.VMEM((3,) + a.shape[1:], a.dtype) for a in parts]
        + [pltpu.SemaphoreType.DMA((n, 4)), pltpu.SemaphoreType.DMA((n, 4))],
        compiler_params=pltpu.CompilerParams(vmem_limit_bytes=VMEM_LIMIT),
    )(*parts)


def _small_allreduce(loss_p, dg_parts, dbg_a, dbg_c, dwc):
    ins = [loss_p] + list(dg_parts) + [dbg_a, dbg_c, dwc]
    n_in = len(ins)
    vmem = pl.BlockSpec(memory_space=pltpu.VMEM)

    def body(*refs):
        in_refs = refs[:n_in]
        out_ref, vec, buf, send_sems, recv_sems = refs[n_in:]
        x, y, c, _ = _place()
        me = 4 * x + 2 * y + c
        vec[...] = jnp.zeros_like(vec)
        vec[0:1, :] = jnp.sum(in_refs[0][...], axis=0)
        for r in range(5):
            vec[1 + r:2 + r, :] = jnp.sum(in_refs[1 + r][...], axis=0)
        vec[6:7, :] = jnp.sum(in_refs[6][...], axis=0)
        vec[7:8, :] = jnp.sum(in_refs[7][...], axis=0)
        vec[8:16, 0:CONV_W] = jnp.sum(in_refs[8][...], axis=0)
        buf[pl.ds(me, 1)] = vec[...][None]
        copies = []
        for r in range(1, 8):
            fx, fy, fc = (r >> 2) & 1, (r >> 1) & 1, r & 1
            to = (1 - x if fx else x, 1 - y if fy else y, 1 - c if fc else c)
            cp = pltpu.make_async_remote_copy(src_ref=vec, dst_ref=buf.at[me], send_sem=send_sems.at[r - 1],
                                              recv_sem=recv_sems.at[r - 1], device_id=to, device_id_type=MESH)
            cp.start()
            copies.append(cp)
        for cp in copies:
            cp.wait()
        total = buf[0]
        for s in range(1, 8):
            total = total + buf[s]
        out_ref[...] = total
        out_ref[0:1, :] = jnp.broadcast_to(jnp.sum(total[0:1, :], axis=-1, keepdims=True), (1, D_MODEL))

    return pl.pallas_call(
        body, name="small_allreduce",
        in_specs=[vmem] * n_in, out_specs=vmem, out_shape=_sds((SMALL_ROWS, D_MODEL), F32),
        scratch_shapes=[pltpu.VMEM((SMALL_ROWS, D_MODEL), F32), pltpu.VMEM((8, SMALL_ROWS, D_MODEL), F32),
                        pltpu.SemaphoreType.DMA((7,)), pltpu.SemaphoreType.DMA((7,))],
    )(*ins)


def _local_step(x, p, tgt, g, b_gate, w_conv, wf):
    seq = x.shape[0]
    tm = min(seq, 1024)
    th = min(seq, 512)
    ni, nh = seq // tm, seq // th
    g_pre_mix, g_post_mix, g_pre_mlp, g_post_mlp, g_ple = g
    w_in, w_ao, w_co, w_o, w_up, w_down, w_pg, w_pp = wf
    D = D_MODEL
    vec = lambda a, blk=0: (a, _bs((1, D), lambda i, j, k: (0, blk)))
    rows_i = lambda a, t, blk=0: (a, _bs((t, D), lambda i, j, k: (i, blk)))
    rows_k = lambda a, t, blk=0: (a, _bs((t, D), lambda i, j, k: (k, blk)))
    part = lambda n: (_sds((n, 1, D), F32), _bs((None, 1, D), lambda i, j, k: (i, 0, 0)))
    full2 = lambda a: (a, _bs(a.shape, lambda i, j, k: (0, 0)))

    normed = lambda xb, gb: (_rms(xb, gb).astype(BF16),) * 2
    keep_a = lambda t: [(_sds((seq, D), BF16), _bs((t, D), lambda i, j, k: (i, 0)))]
    proj, h1 = _mm("proj_in", "nn", (ni, 4, 1),
                   a_ins=[rows_i(x, tm), vec(g_pre_mix)], a_fn=normed,
                   b_ins=[(w_in, _bs((None, D, 1280), lambda i, j, k: (j, 0, 0)))], b_fn=_ident,
                   outs=[(_sds((seq, D_IN), F32), _bs((tm, 1280), lambda i, j, k: (i, j)))],
                   acc_shape=(tm, 1280), a_cache=((tm, D), BF16), a_outs=keep_a(tm))
    qkv = _qkv_cast(proj, seq, tm)
    o, lta, ltb = _attn_fwd(qkv, seq)
    (y_attn,) = _mm("attn_out", "nn", (ni, 1, 1),
                    a_ins=[(o, _bs((tm, ATTN_W), lambda i, j, k: (i, 0)))], a_fn=_ident,
                    b_ins=[full2(w_ao)], b_fn=_ident,
                    outs=[(_sds((seq, D), F32), _bs((tm, D), lambda i, j, k: (i, 0)))], acc_shape=(tm, D))
    e, d = _conv_fwd(proj, w_conv, seq, tm)
    (y_conv,) = _mm("conv_out", "nn", (ni, 1, 1),
                    a_ins=[(e, _bs((tm, CONV_W), lambda i, j, k: (i, 0)))], a_fn=_ident,
                    b_ins=[full2(w_co)], b_fn=_ident,
                    outs=[(_sds((seq, D), F32), _bs((tm, D), lambda i, j, k: (i, 0)))], acc_shape=(tm, D))

    def mix_fn(ga, gc, ya, yc, ba, bc):
        return ((_sig(ga + ba) * ya + _sig(gc + bc) * yc).astype(BF16),) * 2

    def post_mix(acc, xb, gb):
        return acc, xb + _rms(acc, gb)

    mix_ins = lambda rows: [rows(proj, th, 3), rows(proj, th, 4), rows(y_attn, th), rows(y_conv, th),
                            vec(b_gate, 0), vec(b_gate, 1)]
    mixed, x1, mixin = _mm("mix_out", "nn", (nh, 1, 1),
                           a_ins=mix_ins(rows_i), a_fn=mix_fn, b_ins=[full2(w_o)], b_fn=_ident,
                           epi_ins=[rows_i(x, th), vec(g_post_mix)], epi_fn=post_mix,
                           outs=[(_sds((seq, D), F32), _bs((th, D), lambda i, j, k: (i, 0)))] * 2,
                           acc_shape=(th, D), a_cache=((th, D), BF16), a_outs=keep_a(th))
    up, h2 = _mm("mlp_up", "nn", (ni, 4, 1),
                 a_ins=[rows_i(x1, tm), vec(g_pre_mlp)], a_fn=normed,
                 b_ins=[(w_up, _bs((None, D, D), lambda i, j, k: (j, 0, 0)))], b_fn=_ident,
                 outs=[(_sds((seq, D_FF), BF16), _bs((tm, D), lambda i, j, k: (i, j)))],
                 acc_shape=(tm, D), a_cache=((tm, D), BF16), a_outs=keep_a(tm))

    def relu2(ub):
        r = jnp.maximum(ub.astype(F32), 0.0)
        return (r * r).astype(BF16)

    f, x2 = _mm("mlp_down", "nn", (ni, 1, 4),
                a_ins=[(up, _bs((tm, D), lambda i, j, k: (i, k)))], a_fn=relu2,
                b_ins=[(w_down, _bs((D, D), lambda i, j, k: (k, 0)))], b_fn=_ident,
                epi_ins=[rows_i(x1, tm), vec(g_post_mlp)], epi_fn=post_mix,
                outs=[(_sds((seq, D), F32), _bs((tm, D), lambda i, j, k: (i, 0)))] * 2, acc_shape=(tm, D))
    (pp,) = _mm("ple_proj", "nn", (ni, 1, 1),
                a_ins=[(p, _bs((tm, PLE_DIM), lambda i, j, k: (i, 0)))], a_fn=_to_bf16,
                b_ins=[full2(w_pp)], b_fn=_ident,
                outs=[(_sds((seq, D), F32), _bs((tm, D), lambda i, j, k: (i, 0)))], acc_shape=(tm, D))

    def head(acc, x2b, ppb, tb):
        pg = _sig(acc)
        err = x2b + pg * ppb - tb
        return pg, err * (1.0 / D), jnp.sum(err * err, axis=0, keepdims=True) * (0.5 / D)

    pg, dx3, loss_p, h3 = _mm("ple_gate_loss", "nn", (nh, 1, 1),
                              a_ins=[rows_i(x2, th), vec(g_ple)], a_fn=normed,
                              b_ins=[full2(w_pg)], b_fn=_ident,
                              epi_ins=[rows_i(x2, th), rows_i(pp, th), rows_i(tgt, th)], epi_fn=head,
                              outs=[(_sds((seq, D), F32), _bs((th, D), lambda i, j, k: (i, 0)))] * 2 + [part(nh)],
                              acc_shape=(th, D), a_cache=((th, D), BF16), a_outs=keep_a(th))

    (dw_pp,) = _mm("dw_ple_proj", "tn", (1, 1, nh),
                   a_ins=[(p, _bs((th, PLE_DIM), lambda i, j, k: (k, 0)))], a_fn=_to_bf16,
                   b_ins=[rows_k(dx3, th), rows_k(pg, th)], b_fn=lambda a, b: (a * b).astype(BF16),
                   outs=[(_sds((PLE_DIM, D), F32), _bs((PLE_DIM, D), lambda i, j, k: (0, 0)))],
                   acc_shape=(PLE_DIM, D))

    def dpre_fn(dx3b, ppb, pgb):
        return (dx3b * ppb * pgb * (1.0 - pgb)).astype(BF16)

    def ple_norm_bwd(acc, x2b, dx3b, gb):
        dxn, dg = _rms_bwd(x2b, gb, acc)
        return dx3b + dxn, dg

    dx2, dg_ple_p, dpre = _mm("d_ple_gate", "nt", (nh, 1, 1),
                              a_ins=[rows_i(dx3, th), rows_i(pp, th), rows_i(pg, th)],
                              a_fn=lambda a, b, c: (dpre_fn(a, b, c),) * 2,
                              b_ins=[full2(w_pg)], b_fn=_ident,
                              epi_ins=[rows_i(x2, th), rows_i(dx3, th), vec(g_ple)], epi_fn=ple_norm_bwd,
                              outs=[(_sds((seq, D), F32), _bs((th, D), lambda i, j, k: (i, 0))), part(nh)],
                              acc_shape=(th, D), a_cache=((th, D), BF16),
                              a_outs=[(_sds((seq, D), BF16), _bs((th, D), lambda i, j, k: (i, 0)))])
    (dw_pg,) = _mm("dw_ple_gate", "tn", (1, 1, ni),
                   a_ins=[rows_k(h3, tm)], a_fn=_ident, b_ins=[rows_k(dpre, tm)], b_fn=_ident,
                   outs=[(_sds((D, D), F32), _bs((D, D), lambda i, j, k: (0, 0)))], acc_shape=(D, D))

    def df_fn(fb, dx2b, gb):
        dfb, dg = _rms_bwd(fb, gb, dx2b)
        dfb = dfb.astype(BF16)
        return dfb, dfb, dg

    def dup_fn(acc, ub):
        return (acc * (2.0 * jnp.maximum(ub.astype(F32), 0.0)),)

    dup, df, dg_post_mlp_p = _mm("d_mlp_down", "nt", (ni, 4, 1),
                                 a_ins=[rows_i(f, tm), rows_i(dx2, tm), vec(g_post_mlp)], a_fn=df_fn,
                                 b_ins=[(w_down, _bs((D, D), lambda i, j, k: (j, 0)))], b_fn=_ident,
                                 epi_ins=[(up, _bs((tm, D), lambda i, j, k: (i, j)))], epi_fn=dup_fn,
                                 outs=[(_sds((seq, D_FF), BF16), _bs((tm, D), lambda i, j, k: (i, j)))],
                                 acc_shape=(tm, D), a_cache=((tm, D), BF16),
                                 a_outs=[(_sds((seq, D), BF16), _bs((tm, D), lambda i, j, k: (i, 0))), part(ni)])
    (dw_down,) = _mm("dw_mlp_down", "tn", (4, 1, ni),
                     a_ins=[(up, _bs((tm, D), lambda i, j, k: (k, i)))], a_fn=relu2,
                     b_ins=[rows_k(df, tm)], b_fn=_ident,
                     outs=[(_sds((D_FF, D), F32), _bs((D, D), lambda i, j, k: (i, 0)))], acc_shape=(D, D))
    (dw_up,) = _mm("dw_mlp_up", "tn", (1, 4, ni),
                   a_ins=[rows_k(h2, tm)], a_fn=_ident,
                   b_ins=[(dup, _bs((tm, D), lambda i, j, k: (k, j)))], b_fn=_ident,
                   outs=[(_sds((N_CHIPS, D, D), F32), _bs((None, D, D), lambda i, j, k: (j, 0, 0)))],
                   acc_shape=(D, D))

    def mlp_norm_bwd(acc, x1b, dx2b, mixedb, g_mlp, g_mix):
        dxn, dg_mlp = _rms_bwd(x1b, g_mlp, acc)
        dx1b = dx2b + dxn
        dmixedb, dg_mix = _rms_bwd(mixedb, g_mix, dx1b)
        return dx1b, dmixedb, dg_mlp, dg_mix

    dx1, dmixed, dg_pre_mlp_p, dg_post_mix_p = _mm(
        "d_mlp_up", "nt", (nh, 1, 4),
        a_ins=[(dup, _bs((th, D), lambda i, j, k: (i, k)))], a_fn=_ident,
        b_ins=[(w_up, _bs((None, D, D), lambda i, j, k: (k, 0, 0)))], b_fn=_ident,
        epi_ins=[rows_i(x1, th), rows_i(dx2, th), rows_i(mixed, th), vec(g_pre_mlp), vec(g_post_mix)],
        epi_fn=mlp_norm_bwd,
        outs=[(_sds((seq, D), F32), _bs((th, D), lambda i, j, k: (i, 0))),
              (_sds((seq, D), BF16), _bs((th, D), lambda i, j, k: (i, 0))), part(nh), part(nh)],
        acc_shape=(th, D))
    (dw_o,) = _mm("dw_mix_out", "tn", (1, 1, ni),
                  a_ins=[rows_k(mixin, tm)], a_fn=_ident, b_ins=[rows_k(dmixed, tm)], b_fn=_ident,
                  outs=[(_sds((D, D), F32), _bs((D, D), lambda i, j, k: (0, 0)))], acc_shape=(D, D))

    def gate_bwd(acc, ga, gc, ya, yc, ba, bc):
        sa, sc = _sig(ga + ba), _sig(gc + bc)
        dga = acc * ya * sa * (1.0 - sa)
        dgc = acc * yc * sc * (1.0 - sc)
        return (acc * sa, acc * sc, jnp.concatenate([dga, dgc], axis=1),
                jnp.sum(dga, axis=0, keepdims=True), jnp.sum(dgc, axis=0, keepdims=True))

    dya, dyc, dgate, dbg_a_p, dbg_c_p = _mm(
        "d_mix_out", "nt", (nh, 1, 1),
        a_ins=[rows_i(dmixed, th)], a_fn=_ident, b_ins=[full2(w_o)], b_fn=_ident,
        epi_ins=mix_ins(rows_i), epi_fn=gate_bwd,
        outs=[(_sds((seq, D), BF16), _bs((th, D), lambda i, j, k: (i, 0)))] * 2
             + [(_sds((seq, 2 * D), BF16), _bs((th, 2 * D), lambda i, j, k: (i, 0))), part(nh), part(nh)],
        acc_shape=(th, D))
    (dw_ao,) = _mm("dw_attn_out", "tn", (1, 1, nh),
                   a_ins=[(o, _bs((th, ATTN_W), lambda i, j, k: (k, 0)))], a_fn=_ident,
                   b_ins=[rows_k(dya, th)], b_fn=_ident,
                   outs=[(_sds((ATTN_W, D), F32), _bs((ATTN_W, D), lambda i, j, k: (0, 0)))], acc_shape=(ATTN_W, D))
    (do,) = _mm("d_attn_out", "nt", (ni, 1, 1),
                a_ins=[rows_i(dya, tm)], a_fn=_ident, b_ins=[full2(w_ao)], b_fn=_ident,
                outs=[(_sds((seq, ATTN_W), BF16), _bs((tm, ATTN_W), lambda i, j, k: (i, 0)))],
                acc_shape=(tm, ATTN_W))
    dq, dk, dv = _attn_bwd(qkv, do, lta, ltb, seq)
    (dw_co,) = _mm("dw_conv_out", "tn", (1, 1, nh),
                   a_ins=[(e, _bs((th, CONV_W), lambda i, j, k: (k, 0)))], a_fn=_ident,
                   b_ins=[rows_k(dyc, th)], b_fn=_ident,
                   outs=[(_sds((CONV_W, D), F32), _bs((CONV_W, D), lambda i, j, k: (0, 0)))], acc_shape=(CONV_W, D))
    (de,) = _mm("d_conv_out", "nt", (ni, 1, 1),
                a_ins=[rows_i(dyc, tm)], a_fn=_ident, b_ins=[full2(w_co)], b_fn=_ident,
                outs=[(_sds((seq, CONV_W), F32), _bs((tm, CONV_W), lambda i, j, k: (i, 0)))],
                acc_shape=(tm, CONV_W))
    dconv, dwc_p = _conv_bwd(proj, de, d, w_conv, seq, tm)
    dproj = jnp.concatenate([dq.astype(BF16), dk.astype(BF16), dv.astype(BF16), dconv, dgate], axis=1)
    (dw_in,) = _mm("dw_proj_in", "tn", (1, 4, ni),
                   a_ins=[rows_k(h1, tm)], a_fn=_ident,
                   b_ins=[(dproj, _bs((tm, 1280), lambda i, j, k: (k, j)))], b_fn=_ident,
                   outs=[(_sds((N_CHIPS, D, 1280), F32), _bs((None, D, 1280), lambda i, j, k: (j, 0, 0)))],
                   acc_shape=(D, 1280))

    def in_norm_bwd(acc, xb, dx1b, gb):
        dxn, dg = _rms_bwd(xb, gb, acc)
        return dx1b + dxn, dg

    grad_x, dg_pre_mix_p = _mm("d_proj_in", "nt", (ni, 1, 4),
                               a_ins=[(dproj, _bs((tm, 1280), lambda i, j, k: (i, k)))], a_fn=_ident,
                               b_ins=[(w_in, _bs((None, D, 1280), lambda i, j, k: (k, 0, 0)))], b_fn=_ident,
                               epi_ins=[rows_i(x, tm), rows_i(dx1, tm), vec(g_pre_mix)], epi_fn=in_norm_bwd,
                               outs=[(_sds((seq, D), F32), _bs((tm, D), lambda i, j, k: (i, 0))), part(ni)],
                               acc_shape=(tm, D))

    chip_major = lambda a: a.reshape(a.shape[0], N_CHIPS, a.shape[1] // N_CHIPS).transpose(1, 0, 2)
    big = [dw_in, chip_major(dw_ao), chip_major(dw_co), dw_o.reshape(N_CHIPS, D // N_CHIPS, D), dw_up,
           dw_down.reshape(N_CHIPS, D_FF // N_CHIPS, D), dw_pg.reshape(N_CHIPS, D // N_CHIPS, D), chip_major(dw_pp)]
    small = (loss_p, [dg_pre_mix_p, dg_post_mix_p, dg_pre_mlp_p, dg_post_mlp_p, dg_ple_p], dbg_a_p, dbg_c_p, dwc_p)
    return grad_x, big, small


RS_GROUPS = ((0,), (4,), (5,), (1, 2, 3, 6, 7))


def _reduce_scatter(big):
    pair = [None] * len(big)
    for gi, group in enumerate(RS_GROUPS):
        for w, s in zip(group, _rs_pair_sum(f"rs_pair_sum_{gi}", [big[w] for w in group])):
            pair[w] = s
    return _rs_exchange_join(pair)


def kernel(x, p, g_pre_mix, w_in, b_gate, w_conv, w_attn_out, w_conv_out, w_o, g_post_mix, g_pre_mlp, w_up, w_down, g_post_mlp, g_ple, w_ple_gate, w_ple_proj, loss_target, m_g_pre_mix, m_w_in, m_b_gate, m_w_conv, m_w_attn_out, m_w_conv_out, m_w_o, m_g_post_mix, m_g_pre_mlp, m_w_up, m_w_down, m_g_post_mlp, m_g_ple, m_w_ple_gate, m_w_ple_proj, v_g_pre_mix, v_w_in, v_b_gate, v_w_conv, v_w_attn_out, v_w_conv_out, v_w_o, v_g_post_mix, v_g_pre_mlp, v_w_up, v_w_down, v_g_post_mlp, v_g_ple, v_w_ple_gate, v_w_ple_proj):
    mats = [w_in, w_attn_out, w_conv_out, w_o, w_up, w_down, w_ple_gate, w_ple_proj]
    mats_m = [m_w_in, m_w_attn_out, m_w_conv_out, m_w_o, m_w_up, m_w_down, m_w_ple_gate, m_w_ple_proj]
    mats_v = [v_w_in, v_w_attn_out, v_w_conv_out, v_w_o, v_w_up, v_w_down, v_w_ple_gate, v_w_ple_proj]
    gains = [g_pre_mix, g_post_mix, g_pre_mlp, g_post_mlp, g_ple]
    gains_m = [m_g_pre_mix, m_g_post_mix, m_g_pre_mlp, m_g_post_mlp, m_g_ple]
    gains_v = [v_g_pre_mix, v_g_post_mix, v_g_pre_mlp, v_g_post_mlp, v_g_ple]

    taps = jnp.concatenate([w_conv[0], jnp.zeros((CONV_PAD_ROWS - 3, LANES), F32)], axis=0)
    gathered = _allgather_weights([w[0].astype(BF16) for w in mats] + [taps])
    cols_joined = lambda a: a.transpose(1, 0, 2).reshape(a.shape[1], N_CHIPS * a.shape[2])
    rows_joined = lambda a: a.reshape(N_CHIPS * a.shape[1], a.shape[2])
    wf = [gathered[0], cols_joined(gathered[1]), cols_joined(gathered[2]), rows_joined(gathered[3]), gathered[4],
          rows_joined(gathered[5]), rows_joined(gathered[6]), cols_joined(gathered[7])]
    w_conv_full = cols_joined(gathered[8])[0:3, :]
    chip = 2 * lax.axis_index("x") + lax.axis_index("y")

    grad_x, big, small = _local_step(x[0], p[0, 0], loss_target[0], gains, b_gate, w_conv_full, wf)

    shard_grads = _reduce_scatter(big)
    red = _small_allreduce(*small)
    loss = red[0, 0]
    grad_gains = [red[1 + r:2 + r, :] for r in range(5)]
    grad_b_gate = jnp.concatenate([red[6:7, :], red[7:8, :]], axis=1)
    grad_w_conv = lax.dynamic_slice(red[8:11, :], (0, chip * LANES), (3, LANES))[None]

    grads_big = [gr.reshape(w.shape) for gr, w in zip(shard_grads, mats)]
    upd_big = [_adamw(f"adamw_{i}", w, gr, m, v) for i, (w, gr, m, v) in enumerate(zip(mats, grads_big, mats_m, mats_v))]
    pack = lambda vs, bg: jnp.concatenate(list(vs) + [bg.reshape(2, D_MODEL), jnp.zeros((1, D_MODEL), F32)], axis=0)
    upd_small = _adamw("adamw_small", pack(gains, b_gate), pack(grad_gains, grad_b_gate),
                       pack(gains_m, m_b_gate), pack(gains_v, v_b_gate))
    upd_conv = _adamw("adamw_conv", w_conv, grad_w_conv, m_w_conv, v_w_conv)

    def small_out(a, which):
        gains_out = [a[r:r + 1, :] for r in range(5)]
        return gains_out, a[5:7, :].reshape(1, 2 * D_MODEL)

    def ordered(g_pre_mix_, big_, b_gate_, conv_, g_rest):
        return [g_pre_mix_, big_[0], b_gate_, conv_, big_[1], big_[2], big_[3], g_rest[0], g_rest[1], big_[4], big_[5],
                g_rest[2], g_rest[3], big_[6], big_[7]]

    outs = [loss, grad_x[None]]
    outs += ordered(grad_gains[0], grads_big, grad_b_gate, grad_w_conv, grad_gains[1:])
    for which in range(3):
        g_out, b_out = small_out(upd_small[which], which)
        outs += ordered(g_out[0], [u[which] for u in upd_big], b_out, upd_conv[which], g_out[1:])
    return tuple(outs)
```

```python
import functools

import jax
import jax.numpy as jnp
from jax import lax
from jax.experimental import pallas as pl
from jax.experimental.pallas import tpu as pltpu

F32 = jnp.float32
BF16 = jnp.bfloat16
MESH = pl.DeviceIdType.MESH

D_MODEL = 1024
N_HEADS = 8
HEAD_DIM = 64
ATTN_W = N_HEADS * HEAD_DIM
CONV_W = 512
D_FF = 4096
PLE_DIM = 256
D_IN = 5120
N_CHIPS = 4
EPS = 1e-6
Q_SCALE = HEAD_DIM ** -0.5

ADAM_LR = 0.001
ADAM_B1 = 0.9
ADAM_B2 = 0.999
ADAM_EPS = 1e-08
ADAM_WD = 0.01
ADAM_STEP = 10

V7X_VMEM_BYTES = 64 * 1024 * 1024
VMEM_LIMIT = V7X_VMEM_BYTES - 8 * 1024 * 1024
LANES = 128
ATT_BLK = 256
SMALL_ROWS = 16
CONV_PAD_ROWS = 16


def _cparams(n_grid):
    return pltpu.CompilerParams(dimension_semantics=("arbitrary",) * n_grid, vmem_limit_bytes=VMEM_LIMIT)


def _bs(shape, fn):
    return pl.BlockSpec(shape, fn)


def _rms_stats(xf):
    return lax.rsqrt(jnp.mean(xf * xf, axis=-1, keepdims=True) + EPS)


def _rms(xf, g):
    return xf * _rms_stats(xf) * g


def _rms_bwd(xf, g, dy):
    r = _rms_stats(xf)
    xh = xf * r
    dyg = dy * g
    dx = r * (dyg - xh * jnp.mean(dyg * xh, axis=-1, keepdims=True))
    return dx, jnp.sum(dy * xh, axis=0, keepdims=True)


def _sig(z):
    return 1.0 / (1.0 + jnp.exp(-z))


def _ident(a):
    return a


def _to_bf16(a):
    return a.astype(BF16)


_DIMS = {"nn": (((1,), (0,)), ((), ())), "nt": (((1,), (1,)), ((), ())), "tn": (((0,), (0,)), ((), ()))}


def _mm(name, mode, grid, a_ins, a_fn, b_ins, b_fn, outs, acc_shape, epi_ins=(), epi_fn=None,
        a_cache=None, a_outs=()):
    nk = grid[2]
    na, nb, ne, no, nao = len(a_ins), len(b_ins), len(epi_ins), len(outs), len(a_outs)
    assert a_cache is None or nk == 1
    assert not a_outs or a_cache is not None
    dims = _DIMS[mode]
    if epi_fn is None:
        epi_fn = lambda acc: (acc,)

    def body(*refs):
        a_refs = refs[:na]
        b_refs = refs[na:na + nb]
        e_refs = refs[na + nb:na + nb + ne]
        o_refs = refs[na + nb + ne:na + nb + ne + no]
        ao_refs = refs[na + nb + ne + no:na + nb + ne + no + nao]
        scratch = list(refs[na + nb + ne + no + nao:])
        acc_ref = scratch.pop(0) if nk > 1 else None
        a_sc = scratch.pop(0) if a_cache is not None else None
        j = pl.program_id(1)
        k = pl.program_id(2)

        def finish(acc):
            res = epi_fn(acc, *[r[...] for r in e_refs])
            for r, val in zip(o_refs, res):
                r[...] = val.astype(r.dtype)

        if a_sc is not None:
            @pl.when(j == 0)
            def _():
                res = a_fn(*[r[...] for r in a_refs])
                if nao:
                    for r, val in zip(ao_refs, res[1:]):
                        r[...] = val.astype(r.dtype)
                    res = res[0]
                a_sc[...] = res
            a = a_sc[...]
        else:
            a = a_fn(*[r[...] for r in a_refs])
        b = b_fn(*[r[...] for r in b_refs])
        prod = lax.dot_general(a, b, dims, preferred_element_type=F32)
        if nk == 1:
            finish(prod)
        else:
            @pl.when(k == 0)
            def _():
                acc_ref[...] = prod

            @pl.when(k > 0)
            def _():
                acc_ref[...] += prod

            @pl.when(k == nk - 1)
            def _():
                finish(acc_ref[...])

    scratch_shapes = []
    if nk > 1:
        scratch_shapes.append(pltpu.VMEM(acc_shape, F32))
    if a_cache is not None:
        scratch_shapes.append(pltpu.VMEM(*a_cache))
    all_outs = list(outs) + list(a_outs)
    res = pl.pallas_call(
        body, name=name, grid=grid,
        in_specs=[s for _, s in a_ins] + [s for _, s in b_ins] + [s for _, s in epi_ins],
        out_specs=[s for _, s in all_outs],
        out_shape=[o for o, _ in all_outs],
        scratch_shapes=scratch_shapes,
        compiler_params=_cparams(3),
    )(*[a for a, _ in a_ins], *[a for a, _ in b_ins], *[a for a, _ in epi_ins])
    return res


def _sds(shape, dtype):
    return jax.ShapeDtypeStruct(shape, dtype)


def _qkv_cast(proj, seq, tr):
    def body(p_ref, o_ref):
        scale = jnp.where(pl.program_id(1) == 0, Q_SCALE, 1.0).astype(F32)
        o_ref[...] = (p_ref[...] * scale).astype(BF16)

    return pl.pallas_call(
        body, name="qkv_cast", grid=(seq // tr, 3),
        in_specs=[_bs((tr, ATTN_W), lambda i, c: (i, c))],
        out_specs=_bs((tr, ATTN_W), lambda i, c: (i, c)),
        out_shape=_sds((seq, 3 * ATTN_W), BF16),
        compiler_params=_cparams(2),
    )(proj)


def _shift_rows_down(u, prev, n):
    rows = u.shape[0]
    ridx = lax.broadcasted_iota(jnp.int32, u.shape, 0)
    out = pltpu.roll(u, n, 0)
    for r in range(n):
        out = jnp.where(ridx == r, prev[8 - n + r:8 - n + r + 1, :], out)
    del rows
    return out


def _shift_rows_up(u, nxt, n):
    rows = u.shape[0]
    ridx = lax.broadcasted_iota(jnp.int32, u.shape, 0)
    out = pltpu.roll(u, rows - n, 0)
    for r in range(n):
        out = jnp.where(ridx == rows - n + r, nxt[r:r + 1, :], out)
    return out


CONV_COL0 = 3


def _conv_fwd(proj, w_conv, seq, tr):
    hb = tr // 8

    def body(cb_ref, cc_ref, cu_ref, ccp_ref, cup_ref, w_ref, e_ref, d_ref):
        i = pl.program_id(0)
        u = cc_ref[...] * cu_ref[...]
        up = jnp.where(i > 0, ccp_ref[...] * cup_ref[...], 0.0)
        w = w_ref[...]
        d = w[0:1, :] * _shift_rows_down(u, up, 2) + w[1:2, :] * _shift_rows_down(u, up, 1) + w[2:3, :] * u
        d_ref[...] = d
        e_ref[...] = (cb_ref[...] * d).astype(BF16)

    prev = lambda c: (lambda i: (jnp.maximum(i * hb - 1, 0), c))
    return pl.pallas_call(
        body, name="conv_fwd", grid=(seq // tr,),
        in_specs=[_bs((tr, CONV_W), lambda i: (i, CONV_COL0)),
                  _bs((tr, CONV_W), lambda i: (i, CONV_COL0 + 1)),
                  _bs((tr, CONV_W), lambda i: (i, CONV_COL0 + 2)),
                  _bs((8, CONV_W), prev(CONV_COL0 + 1)),
                  _bs((8, CONV_W), prev(CONV_COL0 + 2)),
                  _bs((3, CONV_W), lambda i: (0, 0))],
        out_specs=[_bs((tr, CONV_W), lambda i: (i, 0)), _bs((tr, CONV_W), lambda i: (i, 0))],
        out_shape=[_sds((seq, CONV_W), BF16), _sds((seq, CONV_W), F32)],
        compiler_params=_cparams(1),
    )(proj, proj, proj, proj, proj, w_conv)


def _conv_bwd(proj, de, d, w_conv, seq, tr):
    hb = tr // 8
    nblk = seq // tr

    def body(cb_ref, cc_ref, cu_ref, ccp_ref, cup_ref, cbn_ref, de_ref, den_ref, d_ref, w_ref, o_ref, dw_ref):
        i = pl.program_id(0)
        cc, cu, cb = cc_ref[...], cu_ref[...], cb_ref[...]
        u = cc * cu
        up = jnp.where(i > 0, ccp_ref[...] * cup_ref[...], 0.0)
        u1 = _shift_rows_down(u, up, 1)
        u2 = _shift_rows_down(u, up, 2)
        de_ = de_ref[...]
        dd = de_ * cb
        ddn = jnp.where(i < nblk - 1, den_ref[...] * cbn_ref[...], 0.0)
        w = w_ref[...]
        du = w[2:3, :] * dd + w[1:2, :] * _shift_rows_up(dd, ddn, 1) + w[0:1, :] * _shift_rows_up(dd, ddn, 2)
        o_ref[:, 0:CONV_W] = (de_ * d_ref[...]).astype(BF16)
        o_ref[:, CONV_W:2 * CONV_W] = (du * cu).astype(BF16)
        o_ref[:, 2 * CONV_W:3 * CONV_W] = (du * cc).astype(BF16)
        ridx = lax.broadcasted_iota(jnp.int32, (8, CONV_W), 0)
        dw0 = jnp.sum(dd * u2, axis=0, keepdims=True)
        dw1 = jnp.sum(dd * u1, axis=0, keepdims=True)
        dw2 = jnp.sum(dd * u, axis=0, keepdims=True)
        dw_ref[...] = jnp.where(ridx == 0, dw0, jnp.where(ridx == 1, dw1, jnp.where(ridx == 2, dw2, 0.0)))

    prev = lambda c: (lambda i: (jnp.maximum(i * hb - 1, 0), c))
    nxt = lambda c: (lambda i: (jnp.minimum((i + 1) * hb, seq // 8 - 1), c))
    return pl.pallas_call(
        body, name="conv_bwd", grid=(nblk,),
        in_specs=[_bs((tr, CONV_W), lambda i: (i, CONV_COL0)),
                  _bs((tr, CONV_W), lambda i: (i, CONV_COL0 + 1)),
                  _bs((tr, CONV_W), lambda i: (i, CONV_COL0 + 2)),
                  _bs((8, CONV_W), prev(CONV_COL0 + 1)),
                  _bs((8, CONV_W), prev(CONV_COL0 + 2)),
                  _bs((8, CONV_W), nxt(CONV_COL0)),
                  _bs((tr, CONV_W), lambda i: (i, 0)),
                  _bs((8, CONV_W), nxt(0)),
                  _bs((tr, CONV_W), lambda i: (i, 0)),
                  _bs((3, CONV_W), lambda i: (0, 0))],
        out_specs=[_bs((tr, 3 * CONV_W), lambda i: (i, 0)), _bs((None, 8, CONV_W), lambda i: (i, 0, 0))],
        out_shape=[_sds((seq, 3 * CONV_W), BF16), _sds((nblk, 8, CONV_W), F32)],
        compiler_params=_cparams(1),
    )(proj, proj, proj, proj, proj, proj, de, de, d, w_conv)


def _nt(a, b):
    return lax.dot_general(a, b, _DIMS["nt"], preferred_element_type=F32)


def _tn(a, b):
    return lax.dot_general(a, b, _DIMS["tn"], preferred_element_type=F32)


def _nn(a, b):
    return lax.dot_general(a, b, _DIMS["nn"], preferred_element_type=F32)


def _log_gates(z):
    lse = jnp.log(1.0 + jnp.exp(-jnp.abs(z)))
    log_beta = jnp.minimum(z, 0.0) - lse
    return log_beta, log_beta - z


DEAD_LOG_WEIGHT = -110.0


def _first_live_tile(i, scores, causal, tot_sc):
    def row_totals(zs, diag):
        out = []
        for z in zs:
            keep = _log_gates(z)[1]
            if diag:
                keep = jnp.where(causal, keep, 0.0)
            out.append(jnp.sum(keep, axis=-1, keepdims=True))
        return out

    def alive():
        return jnp.max(jnp.maximum(tot_sc[0], tot_sc[1])) > DEAD_LOG_WEIGHT

    for h, t in enumerate(row_totals(scores(i), True)):
        tot_sc[h] = t

    def step(c):
        for h, t in enumerate(row_totals(scores(c[0]), False)):
            tot_sc[h] = tot_sc[h] + t
        return c[0] - 1, alive()

    j_end, _ = lax.while_loop(lambda c: jnp.logical_and(c[0] >= 0, c[1]), step, (i - 1, alive()))
    return j_end + 1


def _attn_fwd(qkv, seq):
    blk = ATT_BLK
    nq = seq // blk
    npair = N_HEADS // 2

    def body(q_ref, k_ref, v_ref, o_ref, z0_sc, z1_sc, w0_sc, w1_sc, tot_sc, acc_sc):
        i = pl.program_id(1)
        is_a = lax.broadcasted_iota(jnp.int32, (1, LANES), 1) < HEAD_DIM
        q2 = q_ref[...]
        zero = jnp.zeros_like(q2)
        qs = (jnp.where(is_a, q2, zero), jnp.where(is_a, zero, q2))
        row = lax.broadcasted_iota(jnp.int32, (blk, blk), 0)
        col = lax.broadcasted_iota(jnp.int32, (blk, blk), 1)
        tri = (row > col).astype(BF16)
        causal = col < row

        def scores(j):
            k2 = k_ref[pl.ds(pl.multiple_of(j * blk, blk), blk), :]
            return [_nt(qs[h], k2) for h in range(2)]

        def weights(zs, tot, diag):
            gates = [_log_gates(z) for z in zs]
            keeps = [jnp.where(causal, g[1], 0.0) if diag else g[1] for g in gates]
            sums = [_nn(lk.astype(BF16), tri) for lk in keeps]
            ws, new_tot = [], []
            for h in range(2):
                w = jnp.exp(gates[h][0] + (tot[h] + sums[h]))
                if diag:
                    w = jnp.where(causal, w, 0.0)
                ws.append(w.astype(BF16))
                new_tot.append(tot[h] + jnp.sum(keeps[h], axis=-1, keepdims=True))
            return ws, new_tot

        def values(ws, j, acc):
            v2 = v_ref[pl.ds(pl.multiple_of(j * blk, blk), blk), :]
            return acc + jnp.where(is_a, _nn(ws[0], v2), _nn(ws[1], v2))

        z_bufs, w_bufs = (z0_sc, z1_sc), (w0_sc, w1_sc)

        def put(ref, vals):
            for h in range(2):
                ref[h] = vals[h]

        first = _first_live_tile(i, scores, causal, tot_sc)
        trips = i - first

        def trip(j, s):
            acc_sc[...] = values((w_bufs[s][0], w_bufs[s][1]), j + 1, acc_sc[...])
            put(z_bufs[1 - s], scores(jnp.maximum(j - 1, first)))
            ws, tot = weights((z_bufs[s][0], z_bufs[s][1]), [tot_sc[0], tot_sc[1]], False)
            put(w_bufs[1 - s], ws)
            put(tot_sc, tot)

        zero_col = jnp.zeros((blk, 1), F32)
        ws, tot = weights(scores(i), [zero_col, zero_col], True)
        put(w0_sc, ws)
        put(tot_sc, tot)
        put(z0_sc, scores(jnp.maximum(i - 1, first)))
        acc_sc[...] = jnp.zeros_like(acc_sc)

        def two_trips(pp, carry):
            j = i - 1 - 2 * pp
            trip(j, 0)
            trip(j - 1, 1)
            return carry

        lax.fori_loop(0, trips // 2, two_trips, 0)
        odd = trips % 2 == 1

        @pl.when(odd)
        def _():
            trip(first, 0)

        @pl.when(odd)
        def _():
            o_ref[...] = values((w1_sc[0], w1_sc[1]), first, acc_sc[...]).astype(BF16)

        @pl.when(jnp.logical_not(odd))
        def _():
            o_ref[...] = values((w0_sc[0], w0_sc[1]), first, acc_sc[...]).astype(BF16)

    return pl.pallas_call(
        body, name="attn_fwd", grid=(npair, nq),
        in_specs=[_bs((blk, LANES), lambda p, i: (i, p)),
                  _bs((seq, LANES), lambda p, i: (0, npair + p)),
                  _bs((seq, LANES), lambda p, i: (0, 2 * npair + p))],
        out_specs=_bs((blk, LANES), lambda p, i: (i, p)),
        out_shape=_sds((seq, ATTN_W), BF16),
        scratch_shapes=[pltpu.VMEM((2, blk, blk), F32), pltpu.VMEM((2, blk, blk), F32),
                        pltpu.VMEM((2, blk, blk), BF16), pltpu.VMEM((2, blk, blk), BF16),
                        pltpu.VMEM((2, blk, 1), F32), pltpu.VMEM((blk, LANES), F32)],
        compiler_params=_cparams(2),
    )(qkv, qkv, qkv)


def _attn_bwd(qkv, do, seq):
    blk = ATT_BLK
    nq = seq // blk
    npair = N_HEADS // 2

    def body(q_ref, k_ref, v_ref, do_ref, dq_ref, dk_ref, dv_ref,
             prod0_sc, prod1_sc, pend0_sc, pend1_sc, ltot_sc, cum_sc, pre_sc, dq_sc):
        i = pl.program_id(1)

        @pl.when(i == 0)
        def _():
            dk_ref[...] = jnp.zeros_like(dk_ref)
            dv_ref[...] = jnp.zeros_like(dv_ref)

        is_a = lax.broadcasted_iota(jnp.int32, (1, LANES), 1) < HEAD_DIM
        q2 = q_ref[...]
        do2 = do_ref[...]
        zero = jnp.zeros_like(q2)
        qs = (jnp.where(is_a, q2, zero), jnp.where(is_a, zero, q2))
        dos = (jnp.where(is_a, do2, zero), jnp.where(is_a, zero, do2))
        row = lax.broadcasted_iota(jnp.int32, (blk, blk), 0)
        col = lax.broadcasted_iota(jnp.int32, (blk, blk), 1)
        tri_after = (row > col).astype(BF16)
        tri_excl = (row < col).astype(BF16)
        causal = col < row

        prod_bufs, pend_bufs = (prod0_sc, prod1_sc), (pend0_sc, pend1_sc)

        def scores(j):
            k2 = k_ref[pl.ds(pl.multiple_of(j * blk, blk), blk), :]
            return [_nt(qs[h], k2) for h in range(2)]

        def products(j):
            v2 = v_ref[pl.ds(pl.multiple_of(j * blk, blk), blk), :]
            return scores(j) + [_nt(dos[h], v2) for h in range(2)]

        def local_grads(prods, diag):
            zs, dws = prods[:2], prods[2:]
            gates = [_log_gates(z) for z in zs]
            keeps = [jnp.where(causal, g[1], 0.0) if diag else g[1] for g in gates]
            sums = [_nn(lk.astype(BF16), tri_after) for lk in keeps]
            ws, gs = [], []
            for h in range(2):
                cum = cum_sc[h] + jnp.sum(keeps[h], axis=-1, keepdims=True)
                cum_sc[h] = cum
                w = jnp.exp(gates[h][0] + ((ltot_sc[h] - cum) + sums[h]))
                if diag:
                    w = jnp.where(causal, w, 0.0)
                ws.append(w)
                gs.append(dws[h] * w)
            befores = [_nn(g.astype(BF16), tri_excl) for g in gs]
            dzs = []
            for h in range(2):
                beta = jnp.exp(gates[h][0])
                dz = gs[h] * (1.0 - beta) - (pre_sc[h] + befores[h]) * beta
                if diag:
                    dz = jnp.where(causal, dz, 0.0)
                dzs.append(dz.astype(BF16))
                pre_sc[h] = pre_sc[h] + jnp.sum(gs[h], axis=-1, keepdims=True)
            return [w.astype(BF16) for w in ws] + dzs

        def grad_matmuls(pend, j):
            off = pl.multiple_of(j * blk, blk)
            k2 = k_ref[pl.ds(off, blk), :]
            dq_sc[...] += jnp.where(is_a, _nn(pend[2], k2), _nn(pend[3], k2))
            dk_ref[pl.ds(off, blk), :] += jnp.where(is_a, _tn(pend[2], q2), _tn(pend[3], q2))
            dv_ref[pl.ds(off, blk), :] += jnp.where(is_a, _tn(pend[0], do2), _tn(pend[1], do2))

        def put(ref, vals):
            for n, val in enumerate(vals):
                ref[n] = val

        def take(ref):
            return [ref[n] for n in range(4)]

        def trip(j, s):
            grad_matmuls(take(pend_bufs[s]), jnp.maximum(j - 1, 0))
            put(prod_bufs[1 - s], products(j + 1))
            put(pend_bufs[1 - s], local_grads(take(prod_bufs[s]), False))

        first = _first_live_tile(i, scores, causal, ltot_sc)
        cum_sc[...] = jnp.zeros_like(cum_sc)
        pre_sc[...] = jnp.zeros_like(pre_sc)
        dq_sc[...] = jnp.zeros_like(dq_sc)
        pend0_sc[...] = jnp.zeros_like(pend0_sc)
        put(prod0_sc, products(first))

        def two_trips(pp, carry):
            trip(first + 2 * pp, 0)
            trip(first + 2 * pp + 1, 1)
            return carry

        trips = i - first
        lax.fori_loop(0, trips // 2, two_trips, 0)
        odd = trips % 2 == 1

        @pl.when(odd)
        def _():
            trip(i - 1, 0)

        def finish(s):
            grad_matmuls(take(pend_bufs[s]), jnp.maximum(i - 1, 0))
            grad_matmuls(local_grads(take(prod_bufs[s]), True), i)
            dq_ref[...] = dq_sc[...] * Q_SCALE

        @pl.when(odd)
        def _():
            finish(1)

        @pl.when(jnp.logical_not(odd))
        def _():
            finish(0)

    qmap = lambda p, i: (i, p)
    return pl.pallas_call(
        body, name="attn_bwd", grid=(npair, nq),
        in_specs=[_bs((blk, LANES), qmap),
                  _bs((seq, LANES), lambda p, i: (0, npair + p)),
                  _bs((seq, LANES), lambda p, i: (0, 2 * npair + p)),
                  _bs((blk, LANES), qmap)],
        out_specs=[_bs((blk, LANES), qmap),
                   _bs((seq, LANES), lambda p, i: (0, p)),
                   _bs((seq, LANES), lambda p, i: (0, p))],
        out_shape=[_sds((seq, ATTN_W), F32)] * 3,
        scratch_shapes=[pltpu.VMEM((4, blk, blk), F32), pltpu.VMEM((4, blk, blk), F32),
                        pltpu.VMEM((4, blk, blk), BF16), pltpu.VMEM((4, blk, blk), BF16),
                        pltpu.VMEM((2, blk, 1), F32), pltpu.VMEM((2, blk, 1), F32), pltpu.VMEM((2, blk, 1), F32),
                        pltpu.VMEM((blk, LANES), F32)],
        compiler_params=_cparams(2),
    )(qkv, qkv, qkv, do)


def _elementwise(name, fn, ins, out_dtypes):
    rows, cols = ins[0].shape
    tr = rows
    for cand in (512, 256, 128, 64, 32, 16, 8):
        if rows % cand == 0 and cand * cols * 4 <= 2 * 1024 * 1024:
            tr = cand
            break
    n_in = len(ins)

    def body(*refs):
        res = fn(*[r[...] for r in refs[:n_in]])
        for r, val in zip(refs[n_in:], res):
            r[...] = val.astype(r.dtype)

    spec = _bs((tr, cols), lambda i: (i, 0))
    return pl.pallas_call(
        body, name=name, grid=(rows // tr,),
        in_specs=[spec] * n_in, out_specs=[spec] * len(out_dtypes),
        out_shape=[_sds((rows, cols), dt) for dt in out_dtypes],
        compiler_params=_cparams(1),
    )(*ins)


def _adamw_fn(w, g, m, v):
    m = ADAM_B1 * m + (1.0 - ADAM_B1) * g
    v = ADAM_B2 * v + (1.0 - ADAM_B2) * (g * g)
    m_hat = m / (1.0 - ADAM_B1 ** ADAM_STEP)
    v_hat = v / (1.0 - ADAM_B2 ** ADAM_STEP)
    delta = -ADAM_LR * (m_hat / (jnp.sqrt(v_hat) + ADAM_EPS) + ADAM_WD * w)
    return delta, m, v


def _adamw(name, w, g, m, v):
    shape = w.shape
    as2d = lambda a: a.reshape(-1, shape[-1])
    delta, nm, nv = _elementwise(name, _adamw_fn, [as2d(w), as2d(g), as2d(m), as2d(v)], [F32, F32, F32])
    return delta.reshape(shape), nm.reshape(shape), nv.reshape(shape)


def _place():
    x, y, c = lax.axis_index("x"), lax.axis_index("y"), lax.axis_index("c")
    chips = [(1 - x, y), (x, 1 - y), (1 - x, 1 - y)]
    return x, y, c, chips


ANY = pl.BlockSpec(memory_space=pl.ANY)
VMEM_WHOLE = pl.BlockSpec(memory_space=pltpu.VMEM)


def _allgather_weights(shards):
    n = len(shards)

    def body(*refs):
        src, dst = refs[:n], refs[n:2 * n]
        send_sems, recv_sems, local_sems = refs[2 * n:]
        x, y, c, chips = _place()
        me, sibling, mychip = (x, y, c), (x, y, 1 - c), 2 * x + y

        def piece(w, chip, half):
            hr = src[w].shape[0] // 2
            return dst[w].at[chip, pl.ds(half * hr, hr)]

        def copy(w, k, src_ref, dst_ref, to):
            return pltpu.make_async_remote_copy(src_ref=src_ref, dst_ref=dst_ref, send_sem=send_sems.at[w, k],
                                                recv_sem=recv_sems.at[w, k], device_id=to, device_id_type=MESH)

        started, local = [], []
        for w in range(n):
            hr = src[w].shape[0] // 2
            own = pltpu.make_async_copy(src[w], dst[w].at[mychip], local_sems.at[w])
            own.start()
            local.append(own)
            for r, (cx, cy) in enumerate(chips):
                cp = copy(w, r, src[w].at[pl.ds(c * hr, hr)], piece(w, mychip, c), (cx, cy, c))
                cp.start()
                started.append(cp)
        for w in range(n):
            for r, (cx, cy) in enumerate(chips):
                landed = piece(w, 2 * cx + cy, c)
                copy(w, r, landed, landed, me).wait_recv()
                fwd = copy(w, 3 + r, landed, landed, sibling)
                fwd.start()
                started.append(fwd)
        for w in range(n):
            for r, (cx, cy) in enumerate(chips):
                from_sib = piece(w, 2 * cx + cy, 1 - c)
                copy(w, 3 + r, from_sib, from_sib, me).wait_recv()
        for cp in local:
            cp.wait()
        for cp in started:
            cp.wait_send()

    return pl.pallas_call(
        body, name="allgather_weights",
        in_specs=[VMEM_WHOLE] * n, out_specs=[VMEM_WHOLE] * n,
        out_shape=[_sds((N_CHIPS,) + s.shape, s.dtype) for s in shards],
        scratch_shapes=[pltpu.SemaphoreType.DMA((n, 6)), pltpu.SemaphoreType.DMA((n, 6)),
                        pltpu.SemaphoreType.DMA((n,))],
        compiler_params=pltpu.CompilerParams(vmem_limit_bytes=VMEM_LIMIT),
    )(*shards)


SUM_ROWS = 64


def _rs_pair_sum(name, grads):
    n = len(grads)

    def body(*refs):
        g, out = refs[:n], refs[n:2 * n]
        stage, land, keep = refs[2 * n:3 * n], refs[3 * n:4 * n], refs[4 * n:5 * n]
        send_sems, recv_sems, stage_sems, keep_sems = refs[5 * n:]
        x, y, c, _ = _place()
        sibling = (x, y, 1 - c)
        loads = []
        for w in range(n):
            hr = g[w].shape[1] // 2
            st = pltpu.make_async_copy(g[w].at[:, pl.ds((1 - c) * hr, hr)], stage[w], stage_sems.at[w])
            kp = pltpu.make_async_copy(g[w].at[:, pl.ds(c * hr, hr)], keep[w], keep_sems.at[w])
            st.start()
            kp.start()
            loads.append((st, kp))
        gives = []
        for w in range(n):
            loads[w][0].wait()
            give = pltpu.make_async_remote_copy(src_ref=stage[w], dst_ref=land[w], send_sem=send_sems.at[w],
                                                recv_sem=recv_sems.at[w], device_id=sibling, device_id_type=MESH)
            give.start()
            gives.append(give)
        for w in range(n):
            loads[w][1].wait()
            gives[w].wait_recv()
            nb = g[w].shape[1] // 2 // SUM_ROWS

            def add(idx, carry, w=w, nb=nb):
                k, r = idx // nb, pl.multiple_of((idx % nb) * SUM_ROWS, SUM_ROWS)
                rows = pl.ds(r, SUM_ROWS)
                out[w][k, rows, :] = (keep[w][k, rows, :] + land[w][k, rows, :]).astype(BF16)
                return carry

            lax.fori_loop(0, N_CHIPS * nb, add, 0)
        for give in gives:
            give.wait_send()

    half = [(N_CHIPS, a.shape[1] // 2, a.shape[2]) for a in grads]
    bufs = [pltpu.VMEM(s, F32) for s in half]
    sems = pltpu.SemaphoreType.DMA((n,))
    return pl.pallas_call(
        body, name=name,
        in_specs=[ANY] * n, out_specs=[VMEM_WHOLE] * n, out_shape=[_sds(s, BF16) for s in half],
        scratch_shapes=bufs + bufs + bufs + [sems, sems, sems, sems],
        compiler_params=pltpu.CompilerParams(vmem_limit_bytes=VMEM_LIMIT),
    )(*grads)


def _rs_exchange_join(parts):
    n = len(parts)

    def body(*refs):
        t, full, got = refs[:n], refs[n:2 * n], refs[2 * n:3 * n]
        send_sems, recv_sems = refs[3 * n:]
        x, y, c, chips = _place()
        mychip, sibling = 2 * x + y, (x, y, 1 - c)
        sends = []
        for w in range(n):
            for r, (cx, cy) in enumerate(chips):
                cp = pltpu.make_async_remote_copy(src_ref=t[w].at[2 * cx + cy], dst_ref=got[w].at[r],
                                                  send_sem=send_sems.at[w, r], recv_sem=recv_sems.at[w, r],
                                                  device_id=(cx, cy, c), device_id_type=MESH)
                cp.start()
                sends.append(cp)
        for w in range(n):
            hr = t[w].shape[1]
            for r in range(3):
                pltpu.make_async_remote_copy(src_ref=got[w].at[r], dst_ref=got[w].at[r], send_sem=send_sems.at[w, r],
                                             recv_sem=recv_sems.at[w, r], device_id=sibling,
                                             device_id_type=MESH).wait_recv()

            def add(idx, carry, w=w, hr=hr):
                r = pl.multiple_of(idx * SUM_ROWS, SUM_ROWS)
                rows = pl.ds(r, SUM_ROWS)
                f = lambda v: v.astype(F32)
                total = ((f(t[w][mychip, rows, :]) + f(got[w][0, rows, :])) + f(got[w][1, rows, :])) \
                    + f(got[w][2, rows, :])
                full[w][pl.ds(pl.multiple_of(c * hr + r, SUM_ROWS), SUM_ROWS), :] = total
                return carry

            lax.fori_loop(0, hr // SUM_ROWS, add, 0)
            mine = full[w].at[pl.ds(c * hr, hr)]
            give = pltpu.make_async_remote_copy(src_ref=mine, dst_ref=mine, send_sem=send_sems.at[w, 3],
                                                recv_sem=recv_sems.at[w, 3], device_id=sibling, device_id_type=MESH)
            give.start()
            sends.append(give)
        for w in range(n):
            hr = t[w].shape[1]
            theirs = full[w].at[pl.ds((1 - c) * hr, hr)]
            pltpu.make_async_remote_copy(src_ref=theirs, dst_ref=theirs, send_sem=send_sems.at[w, 3],
                                         recv_sem=recv_sems.at[w, 3], device_id=sibling, device_id_type=MESH).wait_recv()
        for cp in sends:
            cp.wait_send()

    return pl.pallas_call(
        body, name="rs_exchange_join",
        in_specs=[VMEM_WHOLE] * n, out_specs=[VMEM_WHOLE] * n,
        out_shape=[_sds((2 * a.shape[1], a.shape[2]), F32) for a in parts],
        scratch_shapes=[pltpu.VMEM((3,) + a.shape[1:], a.dtype) for a in parts]
        + [pltpu.SemaphoreType.DMA((n, 4)), pltpu.SemaphoreType.DMA((n, 4))],
        compiler_params=pltpu.CompilerParams(vmem_limit_bytes=VMEM_LIMIT),
    )(*parts)


def _small_allreduce(loss_p, dg_parts, dbg_a, dbg_c, dwc):
    ins = [loss_p] + list(dg_parts) + [dbg_a, dbg_c, dwc]
    n_in = len(ins)
    vmem = pl.BlockSpec(memory_space=pltpu.VMEM)

    def body(*refs):
        in_refs = refs[:n_in]
        out_ref, vec, buf, send_sems, recv_sems = refs[n_in:]
        x, y, c, _ = _place()
        me = 4 * x + 2 * y + c
        vec[...] = jnp.zeros_like(vec)
        vec[0:1, :] = jnp.sum(in_refs[0][...], axis=0)
        for r in range(5):
            vec[1 + r:2 + r, :] = jnp.sum(in_refs[1 + r][...], axis=0)
        vec[6:7, :] = jnp.sum(in_refs[6][...], axis=0)
        vec[7:8, :] = jnp.sum(in_refs[7][...], axis=0)
        vec[8:16, 0:CONV_W] = jnp.sum(in_refs[8][...], axis=0)
        buf[pl.ds(me, 1)] = vec[...][None]
        copies = []
        for r in range(1, 8):
            fx, fy, fc = (r >> 2) & 1, (r >> 1) & 1, r & 1
            to = (1 - x if fx else x, 1 - y if fy else y, 1 - c if fc else c)
            cp = pltpu.make_async_remote_copy(src_ref=vec, dst_ref=buf.at[me], send_sem=send_sems.at[r - 1],
                                              recv_sem=recv_sems.at[r - 1], device_id=to, device_id_type=MESH)
            cp.start()
            copies.append(cp)
        for cp in copies:
            cp.wait()
        total = buf[0]
        for s in range(1, 8):
            total = total + buf[s]
        out_ref[...] = total
        out_ref[0:1, :] = jnp.broadcast_to(jnp.sum(total[0:1, :], axis=-1, keepdims=True), (1, D_MODEL))

    return pl.pallas_call(
        body, name="small_allreduce",
        in_specs=[vmem] * n_in, out_specs=vmem, out_shape=_sds((SMALL_ROWS, D_MODEL), F32),
        scratch_shapes=[pltpu.VMEM((SMALL_ROWS, D_MODEL), F32), pltpu.VMEM((8, SMALL_ROWS, D_MODEL), F32),
                        pltpu.SemaphoreType.DMA((7,)), pltpu.SemaphoreType.DMA((7,))],
    )(*ins)


def _local_step(x, p, tgt, g, b_gate, w_conv, wf):
    seq = x.shape[0]
    tm = min(seq, 1024)
    th = min(seq, 512)
    ni, nh = seq // tm, seq // th
    g_pre_mix, g_post_mix, g_pre_mlp, g_post_mlp, g_ple = g
    w_in, w_ao, w_co, w_o, w_up, w_down, w_pg, w_pp = wf
    D = D_MODEL
    vec = lambda a, blk=0: (a, _bs((1, D), lambda i, j, k: (0, blk)))
    rows_i = lambda a, t, blk=0: (a, _bs((t, D), lambda i, j, k: (i, blk)))
    rows_k = lambda a, t, blk=0: (a, _bs((t, D), lambda i, j, k: (k, blk)))
    part = lambda n: (_sds((n, 1, D), F32), _bs((None, 1, D), lambda i, j, k: (i, 0, 0)))
    full2 = lambda a: (a, _bs(a.shape, lambda i, j, k: (0, 0)))

    normed = lambda xb, gb: (_rms(xb, gb).astype(BF16),) * 2
    keep_a = lambda t: [(_sds((seq, D), BF16), _bs((t, D), lambda i, j, k: (i, 0)))]
    proj, h1 = _mm("proj_in", "nn", (ni, 4, 1),
                   a_ins=[rows_i(x, tm), vec(g_pre_mix)], a_fn=normed,
                   b_ins=[(w_in, _bs((None, D, 1280), lambda i, j, k: (j, 0, 0)))], b_fn=_ident,
                   outs=[(_sds((seq, D_IN), F32), _bs((tm, 1280), lambda i, j, k: (i, j)))],
                   acc_shape=(tm, 1280), a_cache=((tm, D), BF16), a_outs=keep_a(tm))
    qkv = _qkv_cast(proj, seq, tm)
    o = _attn_fwd(qkv, seq)
    (y_attn,) = _mm("attn_out", "nn", (ni, 1, 1),
                    a_ins=[(o, _bs((tm, ATTN_W), lambda i, j, k: (i, 0)))], a_fn=_ident,
                    b_ins=[full2(w_ao)], b_fn=_ident,
                    outs=[(_sds((seq, D), F32), _bs((tm, D), lambda i, j, k: (i, 0)))], acc_shape=(tm, D))
    e, d = _conv_fwd(proj, w_conv, seq, tm)
    (y_conv,) = _mm("conv_out", "nn", (ni, 1, 1),
                    a_ins=[(e, _bs((tm, CONV_W), lambda i, j, k: (i, 0)))], a_fn=_ident,
                    b_ins=[full2(w_co)], b_fn=_ident,
                    outs=[(_sds((seq, D), F32), _bs((tm, D), lambda i, j, k: (i, 0)))], acc_shape=(tm, D))

    def mix_fn(ga, gc, ya, yc, ba, bc):
        return ((_sig(ga + ba) * ya + _sig(gc + bc) * yc).astype(BF16),) * 2

    def post_mix(acc, xb, gb):
        return acc, xb + _rms(acc, gb)

    mix_ins = lambda rows: [rows(proj, th, 3), rows(proj, th, 4), rows(y_attn, th), rows(y_conv, th),
                            vec(b_gate, 0), vec(b_gate, 1)]
    mixed, x1, mixin = _mm("mix_out", "nn", (nh, 1, 1),
                           a_ins=mix_ins(rows_i), a_fn=mix_fn, b_ins=[full2(w_o)], b_fn=_ident,
                           epi_ins=[rows_i(x, th), vec(g_post_mix)], epi_fn=post_mix,
                           outs=[(_sds((seq, D), F32), _bs((th, D), lambda i, j, k: (i, 0)))] * 2,
                           acc_shape=(th, D), a_cache=((th, D), BF16), a_outs=keep_a(th))
    up, h2 = _mm("mlp_up", "nn", (ni, 4, 1),
                 a_ins=[rows_i(x1, tm), vec(g_pre_mlp)], a_fn=normed,
                 b_ins=[(w_up, _bs((None, D, D), lambda i, j, k: (j, 0, 0)))], b_fn=_ident,
                 outs=[(_sds((seq, D_FF), BF16), _bs((tm, D), lambda i, j, k: (i, j)))],
                 acc_shape=(tm, D), a_cache=((tm, D), BF16), a_outs=keep_a(tm))

    def relu2(ub):
        r = jnp.maximum(ub.astype(F32), 0.0)
        return (r * r).astype(BF16)

    f, x2 = _mm("mlp_down", "nn", (ni, 1, 4),
                a_ins=[(up, _bs((tm, D), lambda i, j, k: (i, k)))], a_fn=relu2,
                b_ins=[(w_down, _bs((D, D), lambda i, j, k: (k, 0)))], b_fn=_ident,
                epi_ins=[rows_i(x1, tm), vec(g_post_mlp)], epi_fn=post_mix,
                outs=[(_sds((seq, D), F32), _bs((tm, D), lambda i, j, k: (i, 0)))] * 2, acc_shape=(tm, D))
    (pp,) = _mm("ple_proj", "nn", (ni, 1, 1),
                a_ins=[(p, _bs((tm, PLE_DIM), lambda i, j, k: (i, 0)))], a_fn=_to_bf16,
                b_ins=[full2(w_pp)], b_fn=_ident,
                outs=[(_sds((seq, D), F32), _bs((tm, D), lambda i, j, k: (i, 0)))], acc_shape=(tm, D))

    def head(acc, x2b, ppb, tb):
        pg = _sig(acc)
        err = x2b + pg * ppb - tb
        return pg, err * (1.0 / D), jnp.sum(err * err, axis=0, keepdims=True) * (0.5 / D)

    pg, dx3, loss_p, h3 = _mm("ple_gate_loss", "nn", (nh, 1, 1),
                              a_ins=[rows_i(x2, th), vec(g_ple)], a_fn=normed,
                              b_ins=[full2(w_pg)], b_fn=_ident,
                              epi_ins=[rows_i(x2, th), rows_i(pp, th), rows_i(tgt, th)], epi_fn=head,
                              outs=[(_sds((seq, D), F32), _bs((th, D), lambda i, j, k: (i, 0)))] * 2 + [part(nh)],
                              acc_shape=(th, D), a_cache=((th, D), BF16), a_outs=keep_a(th))

    (dw_pp,) = _mm("dw_ple_proj", "tn", (1, 1, nh),
                   a_ins=[(p, _bs((th, PLE_DIM), lambda i, j, k: (k, 0)))], a_fn=_to_bf16,
                   b_ins=[rows_k(dx3, th), rows_k(pg, th)], b_fn=lambda a, b: (a * b).astype(BF16),
                   outs=[(_sds((PLE_DIM, D), F32), _bs((PLE_DIM, D), lambda i, j, k: (0, 0)))],
                   acc_shape=(PLE_DIM, D))

    def dpre_fn(dx3b, ppb, pgb):
        return (dx3b * ppb * pgb * (1.0 - pgb)).astype(BF16)

    def ple_norm_bwd(acc, x2b, dx3b, gb):
        dxn, dg = _rms_bwd(x2b, gb, acc)
        return dx3b + dxn, dg

    dx2, dg_ple_p, dpre = _mm("d_ple_gate", "nt", (nh, 1, 1),
                              a_ins=[rows_i(dx3, th), rows_i(pp, th), rows_i(pg, th)],
                              a_fn=lambda a, b, c: (dpre_fn(a, b, c),) * 2,
                              b_ins=[full2(w_pg)], b_fn=_ident,
                              epi_ins=[rows_i(x2, th), rows_i(dx3, th), vec(g_ple)], epi_fn=ple_norm_bwd,
                              outs=[(_sds((seq, D), F32), _bs((th, D), lambda i, j, k: (i, 0))), part(nh)],
                              acc_shape=(th, D), a_cache=((th, D), BF16),
                              a_outs=[(_sds((seq, D), BF16), _bs((th, D), lambda i, j, k: (i, 0)))])
    (dw_pg,) = _mm("dw_ple_gate", "tn", (1, 1, ni),
                   a_ins=[rows_k(h3, tm)], a_fn=_ident, b_ins=[rows_k(dpre, tm)], b_fn=_ident,
                   outs=[(_sds((D, D), F32), _bs((D, D), lambda i, j, k: (0, 0)))], acc_shape=(D, D))

    def df_fn(fb, dx2b, gb):
        dfb, dg = _rms_bwd(fb, gb, dx2b)
        dfb = dfb.astype(BF16)
        return dfb, dfb, dg

    def dup_fn(acc, ub):
        return (acc * (2.0 * jnp.maximum(ub.astype(F32), 0.0)),)

    dup, df, dg_post_mlp_p = _mm("d_mlp_down", "nt", (ni, 4, 1),
                                 a_ins=[rows_i(f, tm), rows_i(dx2, tm), vec(g_post_mlp)], a_fn=df_fn,
                                 b_ins=[(w_down, _bs((D, D), lambda i, j, k: (j, 0)))], b_fn=_ident,
                                 epi_ins=[(up, _bs((tm, D), lambda i, j, k: (i, j)))], epi_fn=dup_fn,
                                 outs=[(_sds((seq, D_FF), BF16), _bs((tm, D), lambda i, j, k: (i, j)))],
                                 acc_shape=(tm, D), a_cache=((tm, D), BF16),
                                 a_outs=[(_sds((seq, D), BF16), _bs((tm, D), lambda i, j, k: (i, 0))), part(ni)])
    (dw_down,) = _mm("dw_mlp_down", "tn", (4, 1, ni),
                     a_ins=[(up, _bs((tm, D), lambda i, j, k: (k, i)))], a_fn=relu2,
                     b_ins=[rows_k(df, tm)], b_fn=_ident,
                     outs=[(_sds((D_FF, D), F32), _bs((D, D), lambda i, j, k: (i, 0)))], acc_shape=(D, D))
    (dw_up,) = _mm("dw_mlp_up", "tn", (1, 4, ni),
                   a_ins=[rows_k(h2, tm)], a_fn=_ident,
                   b_ins=[(dup, _bs((tm, D), lambda i, j, k: (k, j)))], b_fn=_ident,
                   outs=[(_sds((N_CHIPS, D, D), F32), _bs((None, D, D), lambda i, j, k: (j, 0, 0)))],
                   acc_shape=(D, D))

    def mlp_norm_bwd(acc, x1b, dx2b, mixedb, g_mlp, g_mix):
        dxn, dg_mlp = _rms_bwd(x1b, g_mlp, acc)
        dx1b = dx2b + dxn
        dmixedb, dg_mix = _rms_bwd(mixedb, g_mix, dx1b)
        return dx1b, dmixedb, dg_mlp, dg_mix

    dx1, dmixed, dg_pre_mlp_p, dg_post_mix_p = _mm(
        "d_mlp_up", "nt", (nh, 1, 4),
        a_ins=[(dup, _bs((th, D), lambda i, j, k: (i, k)))], a_fn=_ident,
        b_ins=[(w_up, _bs((None, D, D), lambda i, j, k: (k, 0, 0)))], b_fn=_ident,
        epi_ins=[rows_i(x1, th), rows_i(dx2, th), rows_i(mixed, th), vec(g_pre_mlp), vec(g_post_mix)],
        epi_fn=mlp_norm_bwd,
        outs=[(_sds((seq, D), F32), _bs((th, D), lambda i, j, k: (i, 0))),
              (_sds((seq, D), BF16), _bs((th, D), lambda i, j, k: (i, 0))), part(nh), part(nh)],
        acc_shape=(th, D))
    (dw_o,) = _mm("dw_mix_out", "tn", (1, 1, ni),
                  a_ins=[rows_k(mixin, tm)], a_fn=_ident, b_ins=[rows_k(dmixed, tm)], b_fn=_ident,
                  outs=[(_sds((D, D), F32), _bs((D, D), lambda i, j, k: (0, 0)))], acc_shape=(D, D))

    def gate_bwd(acc, ga, gc, ya, yc, ba, bc):
        sa, sc = _sig(ga + ba), _sig(gc + bc)
        dga = acc * ya * sa * (1.0 - sa)
        dgc = acc * yc * sc * (1.0 - sc)
        return (acc * sa, acc * sc, jnp.concatenate([dga, dgc], axis=1),
                jnp.sum(dga, axis=0, keepdims=True), jnp.sum(dgc, axis=0, keepdims=True))

    dya, dyc, dgate, dbg_a_p, dbg_c_p = _mm(
        "d_mix_out", "nt", (nh, 1, 1),
        a_ins=[rows_i(dmixed, th)], a_fn=_ident, b_ins=[full2(w_o)], b_fn=_ident,
        epi_ins=mix_ins(rows_i), epi_fn=gate_bwd,
        outs=[(_sds((seq, D), BF16), _bs((th, D), lambda i, j, k: (i, 0)))] * 2
             + [(_sds((seq, 2 * D), BF16), _bs((th, 2 * D), lambda i, j, k: (i, 0))), part(nh), part(nh)],
        acc_shape=(th, D))
    (dw_ao,) = _mm("dw_attn_out", "tn", (1, 1, nh),
                   a_ins=[(o, _bs((th, ATTN_W), lambda i, j, k: (k, 0)))], a_fn=_ident,
                   b_ins=[rows_k(dya, th)], b_fn=_ident,
                   outs=[(_sds((ATTN_W, D), F32), _bs((ATTN_W, D), lambda i, j, k: (0, 0)))], acc_shape=(ATTN_W, D))
    (do,) = _mm("d_attn_out", "nt", (ni, 1, 1),
                a_ins=[rows_i(dya, tm)], a_fn=_ident, b_ins=[full2(w_ao)], b_fn=_ident,
                outs=[(_sds((seq, ATTN_W), BF16), _bs((tm, ATTN_W), lambda i, j, k: (i, 0)))],
                acc_shape=(tm, ATTN_W))
    dq, dk, dv = _attn_bwd(qkv, do, seq)
    (dw_co,) = _mm("dw_conv_out", "tn", (1, 1, nh),
                   a_ins=[(e, _bs((th, CONV_W), lambda i, j, k: (k, 0)))], a_fn=_ident,
                   b_ins=[rows_k(dyc, th)], b_fn=_ident,
                   outs=[(_sds((CONV_W, D), F32), _bs((CONV_W, D), lambda i, j, k: (0, 0)))], acc_shape=(CONV_W, D))
    (de,) = _mm("d_conv_out", "nt", (ni, 1, 1),
                a_ins=[rows_i(dyc, tm)], a_fn=_ident, b_ins=[full2(w_co)], b_fn=_ident,
                outs=[(_sds((seq, CONV_W), F32), _bs((tm, CONV_W), lambda i, j, k: (i, 0)))],
                acc_shape=(tm, CONV_W))
    dconv, dwc_p = _conv_bwd(proj, de, d, w_conv, seq, tm)
    dproj = jnp.concatenate([dq.astype(BF16), dk.astype(BF16), dv.astype(BF16), dconv, dgate], axis=1)
    (dw_in,) = _mm("dw_proj_in", "tn", (1, 4, ni),
                   a_ins=[rows_k(h1, tm)], a_fn=_ident,
                   b_ins=[(dproj, _bs((tm, 1280), lambda i, j, k: (k, j)))], b_fn=_ident,
                   outs=[(_sds((N_CHIPS, D, 1280), F32), _bs((None, D, 1280), lambda i, j, k: (j, 0, 0)))],
                   acc_shape=(D, 1280))

    def in_norm_bwd(acc, xb, dx1b, gb):
        dxn, dg = _rms_bwd(xb, gb, acc)
        return dx1b + dxn, dg

    grad_x, dg_pre_mix_p = _mm("d_proj_in", "nt", (ni, 1, 4),
                               a_ins=[(dproj, _bs((tm, 1280), lambda i, j, k: (i, k)))], a_fn=_ident,
                               b_ins=[(w_in, _bs((None, D, 1280), lambda i, j, k: (k, 0, 0)))], b_fn=_ident,
                               epi_ins=[rows_i(x, tm), rows_i(dx1, tm), vec(g_pre_mix)], epi_fn=in_norm_bwd,
                               outs=[(_sds((seq, D), F32), _bs((tm, D), lambda i, j, k: (i, 0))), part(ni)],
                               acc_shape=(tm, D))

    chip_major = lambda a: a.reshape(a.shape[0], N_CHIPS, a.shape[1] // N_CHIPS).transpose(1, 0, 2)
    big = [dw_in, chip_major(dw_ao), chip_major(dw_co), dw_o.reshape(N_CHIPS, D // N_CHIPS, D), dw_up,
           dw_down.reshape(N_CHIPS, D_FF // N_CHIPS, D), dw_pg.reshape(N_CHIPS, D // N_CHIPS, D), chip_major(dw_pp)]
    small = (loss_p, [dg_pre_mix_p, dg_post_mix_p, dg_pre_mlp_p, dg_post_mlp_p, dg_ple_p], dbg_a_p, dbg_c_p, dwc_p)
    return grad_x, big, small


RS_GROUPS = ((0,), (4,), (5,), (1, 2, 3, 6, 7))


def _reduce_scatter(big):
    pair = [None] * len(big)
    for gi, group in enumerate(RS_GROUPS):
        for w, s in zip(group, _rs_pair_sum(f"rs_pair_sum_{gi}", [big[w] for w in group])):
            pair[w] = s
    return _rs_exchange_join(pair)


def kernel(x, p, g_pre_mix, w_in, b_gate, w_conv, w_attn_out, w_conv_out, w_o, g_post_mix, g_pre_mlp, w_up, w_down, g_post_mlp, g_ple, w_ple_gate, w_ple_proj, loss_target, m_g_pre_mix, m_w_in, m_b_gate, m_w_conv, m_w_attn_out, m_w_conv_out, m_w_o, m_g_post_mix, m_g_pre_mlp, m_w_up, m_w_down, m_g_post_mlp, m_g_ple, m_w_ple_gate, m_w_ple_proj, v_g_pre_mix, v_w_in, v_b_gate, v_w_conv, v_w_attn_out, v_w_conv_out, v_w_o, v_g_post_mix, v_g_pre_mlp, v_w_up, v_w_down, v_g_post_mlp, v_g_ple, v_w_ple_gate, v_w_ple_proj):
    mats = [w_in, w_attn_out, w_conv_out, w_o, w_up, w_down, w_ple_gate, w_ple_proj]
    mats_m = [m_w_in, m_w_attn_out, m_w_conv_out, m_w_o, m_w_up, m_w_down, m_w_ple_gate, m_w_ple_proj]
    mats_v = [v_w_in, v_w_attn_out, v_w_conv_out, v_w_o, v_w_up, v_w_down, v_w_ple_gate, v_w_ple_proj]
    gains = [g_pre_mix, g_post_mix, g_pre_mlp, g_post_mlp, g_ple]
    gains_m = [m_g_pre_mix, m_g_post_mix, m_g_pre_mlp, m_g_post_mlp, m_g_ple]
    gains_v = [v_g_pre_mix, v_g_post_mix, v_g_pre_mlp, v_g_post_mlp, v_g_ple]

    taps = jnp.concatenate([w_conv[0], jnp.zeros((CONV_PAD_ROWS - 3, LANES), F32)], axis=0)
    gathered = _allgather_weights([w[0].astype(BF16) for w in mats] + [taps])
    cols_joined = lambda a: a.transpose(1, 0, 2).reshape(a.shape[1], N_CHIPS * a.shape[2])
    rows_joined = lambda a: a.reshape(N_CHIPS * a.shape[1], a.shape[2])
    wf = [gathered[0], cols_joined(gathered[1]), cols_joined(gathered[2]), rows_joined(gathered[3]), gathered[4],
          rows_joined(gathered[5]), rows_joined(gathered[6]), cols_joined(gathered[7])]
    w_conv_full = cols_joined(gathered[8])[0:3, :]
    chip = 2 * lax.axis_index("x") + lax.axis_index("y")

    grad_x, big, small = _local_step(x[0], p[0, 0], loss_target[0], gains, b_gate, w_conv_full, wf)

    shard_grads = _reduce_scatter(big)
    red = _small_allreduce(*small)
    loss = red[0, 0]
    grad_gains = [red[1 + r:2 + r, :] for r in range(5)]
    grad_b_gate = jnp.concatenate([red[6:7, :], red[7:8, :]], axis=1)
    grad_w_conv = lax.dynamic_slice(red[8:11, :], (0, chip * LANES), (3, LANES))[None]

    grads_big = [gr.reshape(w.shape) for gr, w in zip(shard_grads, mats)]
    upd_big = [_adamw(f"adamw_{i}", w, gr, m, v) for i, (w, gr, m, v) in enumerate(zip(mats, grads_big, mats_m, mats_v))]
    pack = lambda vs, bg: jnp.concatenate(list(vs) + [bg.reshape(2, D_MODEL), jnp.zeros((1, D_MODEL), F32)], axis=0)
    upd_small = _adamw("adamw_small", pack(gains, b_gate), pack(grad_gains, grad_b_gate),
                       pack(gains_m, m_b_gate), pack(gains_v, v_b_gate))
    upd_conv = _adamw("adamw_conv", w_conv, grad_w_conv, m_w_conv, v_w_conv)

    def small_out(a, which):
        gains_out = [a[r:r + 1, :] for r in range(5)]
        return gains_out, a[5:7, :].reshape(1, 2 * D_MODEL)

    def ordered(g_pre_mix_, big_, b_gate_, conv_, g_rest):
        return [g_pre_mix_, big_[0], b_gate_, conv_, big_[1], big_[2], big_[3], g_rest[0], g_rest[1], big_[4], big_[5],
                g_rest[2], g_rest[3], big_[6], big_[7]]

    outs = [loss, grad_x[None]]
    outs += ordered(grad_gains[0], grads_big, grad_b_gate, grad_w_conv, grad_gains[1:])
    for which in range(3):
        g_out, b_out = small_out(upd_small[which], which)
        outs += ordered(g_out[0], [u[which] for u in upd_big], b_out, upd_conv[which], g_out[1:])
    return tuple(outs)
```

```python
import functools

import jax
import jax.numpy as jnp
from jax import lax
from jax.experimental import pallas as pl
from jax.experimental.pallas import tpu as pltpu

F32 = jnp.float32
BF16 = jnp.bfloat16
MESH = pl.DeviceIdType.MESH

D_MODEL = 1024
N_HEADS = 8
HEAD_DIM = 64
ATTN_W = N_HEADS * HEAD_DIM
CONV_W = 512
D_FF = 4096
PLE_DIM = 256
D_IN = 5120
N_CHIPS = 4
EPS = 1e-6
Q_SCALE = HEAD_DIM ** -0.5

ADAM_LR = 0.001
ADAM_B1 = 0.9
ADAM_B2 = 0.999
ADAM_EPS = 1e-08
ADAM_WD = 0.01
ADAM_STEP = 10

V7X_VMEM_BYTES = 64 * 1024 * 1024
VMEM_LIMIT = V7X_VMEM_BYTES - 8 * 1024 * 1024
LANES = 128
ATT_BLK = 256
SMALL_ROWS = 16
CONV_PAD_ROWS = 16


def _cparams(n_grid):
    return pltpu.CompilerParams(dimension_semantics=("arbitrary",) * n_grid, vmem_limit_bytes=VMEM_LIMIT)


def _bs(shape, fn):
    return pl.BlockSpec(shape, fn)


def _rms_stats(xf):
    return lax.rsqrt(jnp.mean(xf * xf, axis=-1, keepdims=True) + EPS)


def _rms(xf, g):
    return xf * _rms_stats(xf) * g


def _rms_bwd(xf, g, dy):
    r = _rms_stats(xf)
    xh = xf * r
    dyg = dy * g
    dx = r * (dyg - xh * jnp.mean(dyg * xh, axis=-1, keepdims=True))
    return dx, jnp.sum(dy * xh, axis=0, keepdims=True)


def _sig(z):
    return 1.0 / (1.0 + jnp.exp(-z))


def _ident(a):
    return a


def _to_bf16(a):
    return a.astype(BF16)


_DIMS = {"nn": (((1,), (0,)), ((), ())), "nt": (((1,), (1,)), ((), ())), "tn": (((0,), (0,)), ((), ()))}


def _mm(name, mode, grid, a_ins, a_fn, b_ins, b_fn, outs, acc_shape, epi_ins=(), epi_fn=None,
        a_cache=None, a_outs=()):
    nk = grid[2]
    na, nb, ne, no, nao = len(a_ins), len(b_ins), len(epi_ins), len(outs), len(a_outs)
    assert a_cache is None or nk == 1
    assert not a_outs or a_cache is not None
    dims = _DIMS[mode]
    if epi_fn is None:
        epi_fn = lambda acc: (acc,)

    def body(*refs):
        a_refs = refs[:na]
        b_refs = refs[na:na + nb]
        e_refs = refs[na + nb:na + nb + ne]
        o_refs = refs[na + nb + ne:na + nb + ne + no]
        ao_refs = refs[na + nb + ne + no:na + nb + ne + no + nao]
        scratch = list(refs[na + nb + ne + no + nao:])
        acc_ref = scratch.pop(0) if nk > 1 else None
        a_sc = scratch.pop(0) if a_cache is not None else None
        j = pl.program_id(1)
        k = pl.program_id(2)

        def finish(acc):
            res = epi_fn(acc, *[r[...] for r in e_refs])
            for r, val in zip(o_refs, res):
                r[...] = val.astype(r.dtype)

        if a_sc is not None:
            @pl.when(j == 0)
            def _():
                res = a_fn(*[r[...] for r in a_refs])
                if nao:
                    for r, val in zip(ao_refs, res[1:]):
                        r[...] = val.astype(r.dtype)
                    res = res[0]
                a_sc[...] = res
            a = a_sc[...]
        else:
            a = a_fn(*[r[...] for r in a_refs])
        b = b_fn(*[r[...] for r in b_refs])
        prod = lax.dot_general(a, b, dims, preferred_element_type=F32)
        if nk == 1:
            finish(prod)
        else:
            @pl.when(k == 0)
            def _():
                acc_ref[...] = prod

            @pl.when(k > 0)
            def _():
                acc_ref[...] += prod

            @pl.when(k == nk - 1)
            def _():
                finish(acc_ref[...])

    scratch_shapes = []
    if nk > 1:
        scratch_shapes.append(pltpu.VMEM(acc_shape, F32))
    if a_cache is not None:
        scratch_shapes.append(pltpu.VMEM(*a_cache))
    all_outs = list(outs) + list(a_outs)
    res = pl.pallas_call(
        body, name=name, grid=grid,
        in_specs=[s for _, s in a_ins] + [s for _, s in b_ins] + [s for _, s in epi_ins],
        out_specs=[s for _, s in all_outs],
        out_shape=[o for o, _ in all_outs],
        scratch_shapes=scratch_shapes,
        compiler_params=_cparams(3),
    )(*[a for a, _ in a_ins], *[a for a, _ in b_ins], *[a for a, _ in epi_ins])
    return res


def _sds(shape, dtype):
    return jax.ShapeDtypeStruct(shape, dtype)


def _shift_rows_down(u, prev, n):
    rows = u.shape[0]
    ridx = lax.broadcasted_iota(jnp.int32, u.shape, 0)
    out = pltpu.roll(u, n, 0)
    for r in range(n):
        out = jnp.where(ridx == r, prev[8 - n + r:8 - n + r + 1, :], out)
    del rows
    return out


def _shift_rows_up(u, nxt, n):
    rows = u.shape[0]
    ridx = lax.broadcasted_iota(jnp.int32, u.shape, 0)
    out = pltpu.roll(u, rows - n, 0)
    for r in range(n):
        out = jnp.where(ridx == rows - n + r, nxt[r:r + 1, :], out)
    return out


CONV_COL0 = 3


def _conv_fwd(proj, w_conv, seq, tr):
    hb = tr // 8

    def body(cb_ref, cc_ref, cu_ref, ccp_ref, cup_ref, w_ref, e_ref, d_ref):
        i = pl.program_id(0)
        u = cc_ref[...] * cu_ref[...]
        up = jnp.where(i > 0, ccp_ref[...] * cup_ref[...], 0.0)
        w = w_ref[...]
        d = w[0:1, :] * _shift_rows_down(u, up, 2) + w[1:2, :] * _shift_rows_down(u, up, 1) + w[2:3, :] * u
        d_ref[...] = d
        e_ref[...] = (cb_ref[...] * d).astype(BF16)

    prev = lambda c: (lambda i: (jnp.maximum(i * hb - 1, 0), c))
    return pl.pallas_call(
        body, name="conv_fwd", grid=(seq // tr,),
        in_specs=[_bs((tr, CONV_W), lambda i: (i, CONV_COL0)),
                  _bs((tr, CONV_W), lambda i: (i, CONV_COL0 + 1)),
                  _bs((tr, CONV_W), lambda i: (i, CONV_COL0 + 2)),
                  _bs((8, CONV_W), prev(CONV_COL0 + 1)),
                  _bs((8, CONV_W), prev(CONV_COL0 + 2)),
                  _bs((3, CONV_W), lambda i: (0, 0))],
        out_specs=[_bs((tr, CONV_W), lambda i: (i, 0)), _bs((tr, CONV_W), lambda i: (i, 0))],
        out_shape=[_sds((seq, CONV_W), BF16), _sds((seq, CONV_W), F32)],
        compiler_params=_cparams(1),
    )(proj, proj, proj, proj, proj, w_conv)


def _conv_bwd(proj, de, d, w_conv, seq, tr):
    hb = tr // 8
    nblk = seq // tr

    def body(cb_ref, cc_ref, cu_ref, ccp_ref, cup_ref, cbn_ref, de_ref, den_ref, d_ref, w_ref, o_ref, dw_ref):
        i = pl.program_id(0)
        cc, cu, cb = cc_ref[...], cu_ref[...], cb_ref[...]
        u = cc * cu
        up = jnp.where(i > 0, ccp_ref[...] * cup_ref[...], 0.0)
        u1 = _shift_rows_down(u, up, 1)
        u2 = _shift_rows_down(u, up, 2)
        de_ = de_ref[...]
        dd = de_ * cb
        ddn = jnp.where(i < nblk - 1, den_ref[...] * cbn_ref[...], 0.0)
        w = w_ref[...]
        du = w[2:3, :] * dd + w[1:2, :] * _shift_rows_up(dd, ddn, 1) + w[0:1, :] * _shift_rows_up(dd, ddn, 2)
        o_ref[:, 0:CONV_W] = (de_ * d_ref[...]).astype(BF16)
        o_ref[:, CONV_W:2 * CONV_W] = (du * cu).astype(BF16)
        o_ref[:, 2 * CONV_W:3 * CONV_W] = (du * cc).astype(BF16)
        ridx = lax.broadcasted_iota(jnp.int32, (8, CONV_W), 0)
        dw0 = jnp.sum(dd * u2, axis=0, keepdims=True)
        dw1 = jnp.sum(dd * u1, axis=0, keepdims=True)
        dw2 = jnp.sum(dd * u, axis=0, keepdims=True)
        dw_ref[...] = jnp.where(ridx == 0, dw0, jnp.where(ridx == 1, dw1, jnp.where(ridx == 2, dw2, 0.0)))

    prev = lambda c: (lambda i: (jnp.maximum(i * hb - 1, 0), c))
    nxt = lambda c: (lambda i: (jnp.minimum((i + 1) * hb, seq // 8 - 1), c))
    return pl.pallas_call(
        body, name="conv_bwd", grid=(nblk,),
        in_specs=[_bs((tr, CONV_W), lambda i: (i, CONV_COL0)),
                  _bs((tr, CONV_W), lambda i: (i, CONV_COL0 + 1)),
                  _bs((tr, CONV_W), lambda i: (i, CONV_COL0 + 2)),
                  _bs((8, CONV_W), prev(CONV_COL0 + 1)),
                  _bs((8, CONV_W), prev(CONV_COL0 + 2)),
                  _bs((8, CONV_W), nxt(CONV_COL0)),
                  _bs((tr, CONV_W), lambda i: (i, 0)),
                  _bs((8, CONV_W), nxt(0)),
                  _bs((tr, CONV_W), lambda i: (i, 0)),
                  _bs((3, CONV_W), lambda i: (0, 0))],
        out_specs=[_bs((tr, 3 * CONV_W), lambda i: (i, 0)), _bs((None, 8, CONV_W), lambda i: (i, 0, 0))],
        out_shape=[_sds((seq, 3 * CONV_W), BF16), _sds((nblk, 8, CONV_W), F32)],
        compiler_params=_cparams(1),
    )(proj, proj, proj, proj, proj, proj, de, de, d, w_conv)


def _nt(a, b):
    return lax.dot_general(a, b, _DIMS["nt"], preferred_element_type=F32)


def _tn(a, b):
    return lax.dot_general(a, b, _DIMS["tn"], preferred_element_type=F32)


def _nn(a, b):
    return lax.dot_general(a, b, _DIMS["nn"], preferred_element_type=F32)


def _log_gates(z):
    lse = jnp.log(1.0 + jnp.exp(-jnp.abs(z)))
    log_beta = jnp.minimum(z, 0.0) - lse
    return log_beta, log_beta - z


DEAD_LOG_WEIGHT = -110.0


def _first_live_tile(start, scores, live_sc):
    def alive():
        return jnp.max(jnp.maximum(live_sc[0], live_sc[1])) > DEAD_LOG_WEIGHT

    def step(c):
        for h, z in enumerate(scores(c[0])):
            live_sc[h] = live_sc[h] + jnp.sum(_log_gates(z)[1], axis=-1, keepdims=True)
        return c[0] - 1, alive()

    j_end, _ = lax.while_loop(lambda c: jnp.logical_and(c[0] >= 0, c[1]), step, (start, alive()))
    return j_end + 1


def _attn_fwd(proj, seq):
    blk = ATT_BLK
    nq = seq // blk
    npair = N_HEADS // 2

    def body(q_ref, k_ref, v_ref, o_ref, z0_sc, z1_sc, w0_sc, w1_sc, tot_sc, live_sc, acc_sc):
        i = pl.program_id(1)
        is_a = lax.broadcasted_iota(jnp.int32, (1, LANES), 1) < HEAD_DIM
        q2 = (q_ref[...] * Q_SCALE).astype(BF16)
        zero = jnp.zeros_like(q2)
        qs = (jnp.where(is_a, q2, zero), jnp.where(is_a, zero, q2))
        row = lax.broadcasted_iota(jnp.int32, (blk, blk), 0)
        col = lax.broadcasted_iota(jnp.int32, (blk, blk), 1)
        tri = (row > col).astype(BF16)
        causal = col < row

        def tile_of(ref, j):
            return ref[pl.ds(pl.multiple_of(j * blk, blk), blk), :].astype(BF16)

        def scores(j):
            k2 = tile_of(k_ref, j)
            return [_nt(qs[h], k2) for h in range(2)]

        has_left = i > 0
        left = jnp.maximum(i - 1, 0)
        g_d = [_log_gates(z) for z in scores(i)]
        g_l = [_log_gates(z) for z in scores(left)]
        keep_d = [jnp.where(causal, g[1], 0.0) for g in g_d]
        keep_l = [jnp.where(has_left, g[1], 0.0) for g in g_l]
        suf_d = [_nn(lk.astype(BF16), tri) for lk in keep_d]
        suf_l = [_nn(lk.astype(BF16), tri) for lk in keep_l]
        v_d, v_l = tile_of(v_ref, i), tile_of(v_ref, left)
        pv = []
        for h in range(2):
            sum_d = jnp.sum(keep_d[h], axis=-1, keepdims=True)
            w_d = jnp.where(causal, jnp.exp(g_d[h][0] + suf_d[h]), 0.0)
            w_l = jnp.where(has_left, jnp.exp(g_l[h][0] + (sum_d + suf_l[h])), 0.0)
            pv.append(_nn(w_d.astype(BF16), v_d) + _nn(w_l.astype(BF16), v_l))
            tot_sc[h] = sum_d + jnp.sum(keep_l[h], axis=-1, keepdims=True)
        acc_sc[...] = jnp.where(is_a, pv[0], pv[1])

        live_sc[...] = tot_sc[...]
        first = _first_live_tile(i - 2, scores, live_sc)
        trips = i - 1 - first
        z_bufs, w_bufs = (z0_sc, z1_sc), (w0_sc, w1_sc)

        def put(ref, vals):
            for h in range(2):
                ref[h] = vals[h]

        def weights(zs):
            gates = [_log_gates(z) for z in zs]
            sums = [_nn(g[1].astype(BF16), tri) for g in gates]
            ws = []
            for h in range(2):
                ws.append(jnp.exp(gates[h][0] + (tot_sc[h] + sums[h])).astype(BF16))
                tot_sc[h] = tot_sc[h] + jnp.sum(gates[h][1], axis=-1, keepdims=True)
            return ws

        def add_values(w_buf, j):
            v2 = tile_of(v_ref, j)
            acc_sc[...] += jnp.where(is_a, _nn(w_buf[0], v2), _nn(w_buf[1], v2))

        def trip(j, s):
            add_values(w_bufs[s], j + 1)
            put(z_bufs[1 - s], scores(jnp.maximum(j - 1, first)))
            put(w_bufs[1 - s], weights((z_bufs[s][0], z_bufs[s][1])))

        @pl.when(trips > 0)
        def _():
            put(z0_sc, scores(i - 2))
            w0_sc[...] = jnp.zeros_like(w0_sc)

            def two_trips(pp, carry):
                j = i - 2 - 2 * pp
                trip(j, 0)
                trip(j - 1, 1)
                return carry

            lax.fori_loop(0, trips // 2, two_trips, 0)
            odd = trips % 2 == 1

            @pl.when(odd)
            def _():
                trip(first, 0)
                add_values(w1_sc, first)

            @pl.when(jnp.logical_not(odd))
            def _():
                add_values(w0_sc, first)

        o_ref[...] = acc_sc[...].astype(BF16)

    return pl.pallas_call(
        body, name="attn_fwd", grid=(npair, nq),
        in_specs=[_bs((blk, LANES), lambda p, i: (i, p)),
                  _bs((seq, LANES), lambda p, i: (0, npair + p)),
                  _bs((seq, LANES), lambda p, i: (0, 2 * npair + p))],
        out_specs=_bs((blk, LANES), lambda p, i: (i, p)),
        out_shape=_sds((seq, ATTN_W), BF16),
        scratch_shapes=[pltpu.VMEM((2, blk, blk), F32), pltpu.VMEM((2, blk, blk), F32),
                        pltpu.VMEM((2, blk, blk), BF16), pltpu.VMEM((2, blk, blk), BF16),
                        pltpu.VMEM((2, blk, 1), F32), pltpu.VMEM((2, blk, 1), F32), pltpu.VMEM((blk, LANES), F32)],
        compiler_params=_cparams(2),
    )(proj, proj, proj)


def _attn_bwd(proj, do, seq):
    blk = ATT_BLK
    nq = seq // blk
    npair = N_HEADS // 2

    def body(q_ref, k_ref, v_ref, do_ref, dq_ref, dk_ref, dv_ref,
             prod0_sc, prod1_sc, pend0_sc, pend1_sc, tot_sc, live_sc, cum_sc, pre_sc, dq_sc):
        i = pl.program_id(1)

        @pl.when(i == 0)
        def _():
            dk_ref[...] = jnp.zeros_like(dk_ref)
            dv_ref[...] = jnp.zeros_like(dv_ref)

        is_a = lax.broadcasted_iota(jnp.int32, (1, LANES), 1) < HEAD_DIM
        q2 = (q_ref[...] * Q_SCALE).astype(BF16)
        do2 = do_ref[...]
        zero = jnp.zeros_like(q2)
        qs = (jnp.where(is_a, q2, zero), jnp.where(is_a, zero, q2))
        dos = (jnp.where(is_a, do2, zero), jnp.where(is_a, zero, do2))
        row = lax.broadcasted_iota(jnp.int32, (blk, blk), 0)
        col = lax.broadcasted_iota(jnp.int32, (blk, blk), 1)
        tri_after = (row > col).astype(BF16)
        tri_excl = (row < col).astype(BF16)
        causal = col < row

        def tile_of(ref, j):
            return ref[pl.ds(pl.multiple_of(j * blk, blk), blk), :].astype(BF16)

        def scores(j):
            k2 = tile_of(k_ref, j)
            return [_nt(qs[h], k2) for h in range(2)]

        def products(j):
            v2 = tile_of(v_ref, j)
            return scores(j) + [_nt(dos[h], v2) for h in range(2)]

        def row_sum(a):
            return jnp.sum(a, axis=-1, keepdims=True)

        def grad_matmuls(ws, dzs, j):
            rows = pl.ds(pl.multiple_of(j * blk, blk), blk)
            k2 = tile_of(k_ref, j)
            dq_sc[...] += jnp.where(is_a, _nn(dzs[0], k2), _nn(dzs[1], k2))
            dk_ref[rows, :] += jnp.where(is_a, _tn(dzs[0], q2), _tn(dzs[1], q2))
            if ws is not None:
                dv_ref[rows, :] += jnp.where(is_a, _tn(ws[0], do2), _tn(ws[1], do2))

        has_left = i > 0
        left = jnp.maximum(i - 1, 0)
        p_d, p_l = products(i), products(left)
        g_d = [_log_gates(z) for z in p_d[:2]]
        g_l = [_log_gates(z) for z in p_l[:2]]
        keep_d = [jnp.where(causal, g[1], 0.0) for g in g_d]
        keep_l = [jnp.where(has_left, g[1], 0.0) for g in g_l]
        suf_d = [_nn(lk.astype(BF16), tri_after) for lk in keep_d]
        suf_l = [_nn(lk.astype(BF16), tri_after) for lk in keep_l]
        w_d, w_l, gg_d, gg_l = [], [], [], []
        for h in range(2):
            sum_d = row_sum(keep_d[h])
            w_d.append(jnp.where(causal, jnp.exp(g_d[h][0] + suf_d[h]), 0.0))
            w_l.append(jnp.where(has_left, jnp.exp(g_l[h][0] + (sum_d + suf_l[h])), 0.0))
            gg_d.append(p_d[2 + h] * w_d[h])
            gg_l.append(p_l[2 + h] * w_l[h])
            tot_sc[h] = sum_d + row_sum(keep_l[h])
        before_d = [_nn(g.astype(BF16), tri_excl) for g in gg_d]
        before_l = [_nn(g.astype(BF16), tri_excl) for g in gg_l]
        dz_d, dz_l = [], []
        for h in range(2):
            beta_d, beta_l = jnp.exp(g_d[h][0]), jnp.exp(g_l[h][0])
            dz = gg_l[h] * (1.0 - beta_l) - before_l[h] * beta_l
            dz_l.append(jnp.where(has_left, dz, 0.0).astype(BF16))
            dz = gg_d[h] * (1.0 - beta_d) - (row_sum(gg_l[h]) + before_d[h]) * beta_d
            dz_d.append(jnp.where(causal, dz, 0.0).astype(BF16))
        dq_sc[...] = jnp.zeros_like(dq_sc)
        grad_matmuls([w.astype(BF16) for w in w_l], dz_l, left)
        grad_matmuls([w.astype(BF16) for w in w_d], dz_d, i)

        live_sc[...] = tot_sc[...]
        first = _first_live_tile(i - 2, scores, live_sc)
        trips = i - 1 - first
        prod_bufs, pend_bufs = (prod0_sc, prod1_sc), (pend0_sc, pend1_sc)

        def local_grads(prods):
            zs, dws = prods[:2], prods[2:]
            gates = [_log_gates(z) for z in zs]
            sums = [_nn(g[1].astype(BF16), tri_after) for g in gates]
            ws, gs = [], []
            for h in range(2):
                cum = cum_sc[h] + row_sum(gates[h][1])
                cum_sc[h] = cum
                ws.append(jnp.exp(gates[h][0] + ((live_sc[h] - cum) + sums[h])))
                gs.append(dws[h] * ws[h])
            befores = [_nn(g.astype(BF16), tri_excl) for g in gs]
            dzs = []
            for h in range(2):
                beta = jnp.exp(gates[h][0])
                dzs.append((gs[h] * (1.0 - beta) - (pre_sc[h] + befores[h]) * beta).astype(BF16))
                pre_sc[h] = pre_sc[h] + row_sum(gs[h])
            return [w.astype(BF16) for w in ws] + dzs

        def put(ref, vals):
            for n, val in enumerate(vals):
                ref[n] = val

        def flush(pend, j):
            grad_matmuls([pend[0], pend[1]], [pend[2], pend[3]], j)

        def trip(j, s):
            flush(pend_bufs[s], jnp.maximum(j - 1, first))
            put(prod_bufs[1 - s], products(j + 1))
            put(pend_bufs[1 - s], local_grads([prod_bufs[s][n] for n in range(4)]))

        def earlier_keys_share(j, mask):
            dzs = []
            for h, z in enumerate(scores(j)):
                beta = jnp.exp(_log_gates(z)[0])
                dzs.append(jnp.where(mask, -pre_sc[h] * beta, 0.0).astype(BF16))
            grad_matmuls(None, dzs, j)

        @pl.when(trips > 0)
        def _():
            cum_sc[...] = jnp.zeros_like(cum_sc)
            pre_sc[...] = jnp.zeros_like(pre_sc)
            pend0_sc[...] = jnp.zeros_like(pend0_sc)
            put(prod0_sc, products(first))

            def two_trips(pp, carry):
                trip(first + 2 * pp, 0)
                trip(first + 2 * pp + 1, 1)
                return carry

            lax.fori_loop(0, trips // 2, two_trips, 0)
            odd = trips % 2 == 1

            @pl.when(odd)
            def _():
                trip(i - 2, 0)
                flush(pend1_sc, i - 2)

            @pl.when(jnp.logical_not(odd))
            def _():
                flush(pend0_sc, i - 2)

            earlier_keys_share(i - 1, True)
            earlier_keys_share(i, causal)

        dq_ref[...] = dq_sc[...] * Q_SCALE

    qmap = lambda p, i: (i, p)
    return pl.pallas_call(
        body, name="attn_bwd", grid=(npair, nq),
        in_specs=[_bs((blk, LANES), qmap),
                  _bs((seq, LANES), lambda p, i: (0, npair + p)),
                  _bs((seq, LANES), lambda p, i: (0, 2 * npair + p)),
                  _bs((blk, LANES), qmap)],
        out_specs=[_bs((blk, LANES), qmap),
                   _bs((seq, LANES), lambda p, i: (0, p)),
                   _bs((seq, LANES), lambda p, i: (0, p))],
        out_shape=[_sds((seq, ATTN_W), F32)] * 3,
        scratch_shapes=[pltpu.VMEM((4, blk, blk), F32), pltpu.VMEM((4, blk, blk), F32),
                        pltpu.VMEM((4, blk, blk), BF16), pltpu.VMEM((4, blk, blk), BF16),
                        pltpu.VMEM((2, blk, 1), F32), pltpu.VMEM((2, blk, 1), F32), pltpu.VMEM((2, blk, 1), F32),
                        pltpu.VMEM((2, blk, 1), F32), pltpu.VMEM((blk, LANES), F32)],
        compiler_params=_cparams(2),
    )(proj, proj, proj, do)


def _elementwise(name, fn, ins, out_dtypes):
    rows, cols = ins[0].shape
    tr = rows
    for cand in (512, 256, 128, 64, 32, 16, 8):
        if rows % cand == 0 and cand * cols * 4 <= 2 * 1024 * 1024:
            tr = cand
            break
    n_in = len(ins)

    def body(*refs):
        res = fn(*[r[...] for r in refs[:n_in]])
        for r, val in zip(refs[n_in:], res):
            r[...] = val.astype(r.dtype)

    spec = _bs((tr, cols), lambda i: (i, 0))
    return pl.pallas_call(
        body, name=name, grid=(rows // tr,),
        in_specs=[spec] * n_in, out_specs=[spec] * len(out_dtypes),
        out_shape=[_sds((rows, cols), dt) for dt in out_dtypes],
        compiler_params=_cparams(1),
    )(*ins)


def _adamw_fn(w, g, m, v):
    m = ADAM_B1 * m + (1.0 - ADAM_B1) * g
    v = ADAM_B2 * v + (1.0 - ADAM_B2) * (g * g)
    m_hat = m / (1.0 - ADAM_B1 ** ADAM_STEP)
    v_hat = v / (1.0 - ADAM_B2 ** ADAM_STEP)
    delta = -ADAM_LR * (m_hat / (jnp.sqrt(v_hat) + ADAM_EPS) + ADAM_WD * w)
    return delta, m, v


def _adamw(name, w, g, m, v):
    shape = w.shape
    as2d = lambda a: a.reshape(-1, shape[-1])
    delta, nm, nv = _elementwise(name, _adamw_fn, [as2d(w), as2d(g), as2d(m), as2d(v)], [F32, F32, F32])
    return delta.reshape(shape), nm.reshape(shape), nv.reshape(shape)


def _place():
    x, y, c = lax.axis_index("x"), lax.axis_index("y"), lax.axis_index("c")
    chips = [(1 - x, y), (x, 1 - y), (1 - x, 1 - y)]
    return x, y, c, chips


ANY = pl.BlockSpec(memory_space=pl.ANY)
VMEM_WHOLE = pl.BlockSpec(memory_space=pltpu.VMEM)


def _allgather_weights(shards):
    n = len(shards)

    def body(*refs):
        src, dst = refs[:n], refs[n:2 * n]
        send_sems, recv_sems, local_sems = refs[2 * n:]
        x, y, c, chips = _place()
        me, sibling, mychip = (x, y, c), (x, y, 1 - c), 2 * x + y

        def piece(w, chip, half):
            hr = src[w].shape[0] // 2
            return dst[w].at[chip, pl.ds(half * hr, hr)]

        def copy(w, k, src_ref, dst_ref, to):
            return pltpu.make_async_remote_copy(src_ref=src_ref, dst_ref=dst_ref, send_sem=send_sems.at[w, k],
                                                recv_sem=recv_sems.at[w, k], device_id=to, device_id_type=MESH)

        started, local = [], []
        for w in range(n):
            hr = src[w].shape[0] // 2
            own = pltpu.make_async_copy(src[w], dst[w].at[mychip], local_sems.at[w])
            own.start()
            local.append(own)
            for r, (cx, cy) in enumerate(chips):
                cp = copy(w, r, src[w].at[pl.ds(c * hr, hr)], piece(w, mychip, c), (cx, cy, c))
                cp.start()
                started.append(cp)
        for w in range(n):
            for r, (cx, cy) in enumerate(chips):
                landed = piece(w, 2 * cx + cy, c)
                copy(w, r, landed, landed, me).wait_recv()
                fwd = copy(w, 3 + r, landed, landed, sibling)
                fwd.start()
                started.append(fwd)
        for w in range(n):
            for r, (cx, cy) in enumerate(chips):
                from_sib = piece(w, 2 * cx + cy, 1 - c)
                copy(w, 3 + r, from_sib, from_sib, me).wait_recv()
        for cp in local:
            cp.wait()
        for cp in started:
            cp.wait_send()

    return pl.pallas_call(
        body, name="allgather_weights",
        in_specs=[VMEM_WHOLE] * n, out_specs=[VMEM_WHOLE] * n,
        out_shape=[_sds((N_CHIPS,) + s.shape, s.dtype) for s in shards],
        scratch_shapes=[pltpu.SemaphoreType.DMA((n, 6)), pltpu.SemaphoreType.DMA((n, 6)),
                        pltpu.SemaphoreType.DMA((n,))],
        compiler_params=pltpu.CompilerParams(vmem_limit_bytes=VMEM_LIMIT),
    )(*shards)


SUM_ROWS = 64


def _rs_pair_sum(name, grads):
    n = len(grads)

    def body(*refs):
        g, out = refs[:n], refs[n:2 * n]
        stage, land, keep = refs[2 * n:3 * n], refs[3 * n:4 * n], refs[4 * n:5 * n]
        send_sems, recv_sems, stage_sems, keep_sems = refs[5 * n:]
        x, y, c, _ = _place()
        sibling = (x, y, 1 - c)
        loads = []
        for w in range(n):
            hr = g[w].shape[1] // 2
            st = pltpu.make_async_copy(g[w].at[:, pl.ds((1 - c) * hr, hr)], stage[w], stage_sems.at[w])
            kp = pltpu.make_async_copy(g[w].at[:, pl.ds(c * hr, hr)], keep[w], keep_sems.at[w])
            st.start()
            kp.start()
            loads.append((st, kp))
        gives = []
        for w in range(n):
            loads[w][0].wait()
            give = pltpu.make_async_remote_copy(src_ref=stage[w], dst_ref=land[w], send_sem=send_sems.at[w],
                                                recv_sem=recv_sems.at[w], device_id=sibling, device_id_type=MESH)
            give.start()
            gives.append(give)
        for w in range(n):
            loads[w][1].wait()
            gives[w].wait_recv()
            nb = g[w].shape[1] // 2 // SUM_ROWS

            def add(idx, carry, w=w, nb=nb):
                k, r = idx // nb, pl.multiple_of((idx % nb) * SUM_ROWS, SUM_ROWS)
                rows = pl.ds(r, SUM_ROWS)
                out[w][k, rows, :] = (keep[w][k, rows, :] + land[w][k, rows, :]).astype(BF16)
                return carry

            lax.fori_loop(0, N_CHIPS * nb, add, 0)
        for give in gives:
            give.wait_send()

    half = [(N_CHIPS, a.shape[1] // 2, a.shape[2]) for a in grads]
    bufs = [pltpu.VMEM(s, F32) for s in half]
    sems = pltpu.SemaphoreType.DMA((n,))
    return pl.pallas_call(
        body, name=name,
        in_specs=[ANY] * n, out_specs=[VMEM_WHOLE] * n, out_shape=[_sds(s, BF16) for s in half],
        scratch_shapes=bufs + bufs + bufs + [sems, sems, sems, sems],
        compiler_params=pltpu.CompilerParams(vmem_limit_bytes=VMEM_LIMIT),
    )(*grads)


def _rs_exchange_join(parts):
    n = len(parts)

    def body(*refs):
        t, full, got = refs[:n], refs[n:2 * n], refs[2 * n:3 * n]
        send_sems, recv_sems = refs[3 * n:]
        x, y, c, chips = _place()
        mychip, sibling = 2 * x + y, (x, y, 1 - c)
        sends = []
        for w in range(n):
            for r, (cx, cy) in enumerate(chips):
                cp = pltpu.make_async_remote_copy(src_ref=t[w].at[2 * cx + cy], dst_ref=got[w].at[r],
                                                  send_sem=send_sems.at[w, r], recv_sem=recv_sems.at[w, r],
                                                  device_id=(cx, cy, c), device_id_type=MESH)
                cp.start()
                sends.append(cp)
        for w in range(n):
            hr = t[w].shape[1]
            for r in range(3):
                pltpu.make_async_remote_copy(src_ref=got[w].at[r], dst_ref=got[w].at[r], send_sem=send_sems.at[w, r],
                                             recv_sem=recv_sems.at[w, r], device_id=sibling,
                                             device_id_type=MESH).wait_recv()

            def add(idx, carry, w=w, hr=hr):
                r = pl.multiple_of(idx * SUM_ROWS, SUM_ROWS)
                rows = pl.ds(r, SUM_ROWS)
                f = lambda v: v.astype(F32)
                total = ((f(t[w][mychip, rows, :]) + f(got[w][0, rows, :])) + f(got[w][1, rows, :])) \
                    + f(got[w][2, rows, :])
                full[w][pl.ds(pl.multiple_of(c * hr + r, SUM_ROWS), SUM_ROWS), :] = total
                return carry

            lax.fori_loop(0, hr // SUM_ROWS, add, 0)
            mine = full[w].at[pl.ds(c * hr, hr)]
            give = pltpu.make_async_remote_copy(src_ref=mine, dst_ref=mine, send_sem=send_sems.at[w, 3],
                                                recv_sem=recv_sems.at[w, 3], device_id=sibling, device_id_type=MESH)
            give.start()
            sends.append(give)
        for w in range(n):
            hr = t[w].shape[1]
            theirs = full[w].at[pl.ds((1 - c) * hr, hr)]
            pltpu.make_async_remote_copy(src_ref=theirs, dst_ref=theirs, send_sem=send_sems.at[w, 3],
                                         recv_sem=recv_sems.at[w, 3], device_id=sibling, device_id_type=MESH).wait_recv()
        for cp in sends:
            cp.wait_send()

    return pl.pallas_call(
        body, name="rs_exchange_join",
        in_specs=[VMEM_WHOLE] * n, out_specs=[VMEM_WHOLE] * n,
        out_shape=[_sds((2 * a.shape[1], a.shape[2]), F32) for a in parts],
        scratch_shapes=[pltpu.VMEM((3,) + a.shape[1:], a.dtype) for a in parts]
        + [pltpu.SemaphoreType.DMA((n, 4)), pltpu.SemaphoreType.DMA((n, 4))],
        compiler_params=pltpu.CompilerParams(vmem_limit_bytes=VMEM_LIMIT),
    )(*parts)


def _small_allreduce(loss_p, dg_parts, dbg_a, dbg_c, dwc):
    ins = [loss_p] + list(dg_parts) + [dbg_a, dbg_c, dwc]
    n_in = len(ins)
    vmem = pl.BlockSpec(memory_space=pltpu.VMEM)

    def body(*refs):
        in_refs = refs[:n_in]
        out_ref, vec, buf, send_sems, recv_sems = refs[n_in:]
        x, y, c, _ = _place()
        me = 4 * x + 2 * y + c
        vec[...] = jnp.zeros_like(vec)
        vec[0:1, :] = jnp.sum(in_refs[0][...], axis=0)
        for r in range(5):
            vec[1 + r:2 + r, :] = jnp.sum(in_refs[1 + r][...], axis=0)
        vec[6:7, :] = jnp.sum(in_refs[6][...], axis=0)
        vec[7:8, :] = jnp.sum(in_refs[7][...], axis=0)
        vec[8:16, 0:CONV_W] = jnp.sum(in_refs[8][...], axis=0)
        buf[pl.ds(me, 1)] = vec[...][None]
        copies = []
        for r in range(1, 8):
            fx, fy, fc = (r >> 2) & 1, (r >> 1) & 1, r & 1
            to = (1 - x if fx else x, 1 - y if fy else y, 1 - c if fc else c)
            cp = pltpu.make_async_remote_copy(src_ref=vec, dst_ref=buf.at[me], send_sem=send_sems.at[r - 1],
                                              recv_sem=recv_sems.at[r - 1], device_id=to, device_id_type=MESH)
            cp.start()
            copies.append(cp)
        for cp in copies:
            cp.wait()
        total = buf[0]
        for s in range(1, 8):
            total = total + buf[s]
        out_ref[...] = total
        out_ref[0:1, :] = jnp.broadcast_to(jnp.sum(total[0:1, :], axis=-1, keepdims=True), (1, D_MODEL))

    return pl.pallas_call(
        body, name="small_allreduce",
        in_specs=[vmem] * n_in, out_specs=vmem, out_shape=_sds((SMALL_ROWS, D_MODEL), F32),
        scratch_shapes=[pltpu.VMEM((SMALL_ROWS, D_MODEL), F32), pltpu.VMEM((8, SMALL_ROWS, D_MODEL), F32),
                        pltpu.SemaphoreType.DMA((7,)), pltpu.SemaphoreType.DMA((7,))],
    )(*ins)


def _local_step(x, p, tgt, g, b_gate, w_conv, wf):
    seq = x.shape[0]
    tm = min(seq, 1024)
    th = min(seq, 512)
    ni, nh = seq // tm, seq // th
    g_pre_mix, g_post_mix, g_pre_mlp, g_post_mlp, g_ple = g
    w_in, w_ao, w_co, w_o, w_up, w_down, w_pg, w_pp = wf
    D = D_MODEL
    vec = lambda a, blk=0: (a, _bs((1, D), lambda i, j, k: (0, blk)))
    rows_i = lambda a, t, blk=0: (a, _bs((t, D), lambda i, j, k: (i, blk)))
    rows_k = lambda a, t, blk=0: (a, _bs((t, D), lambda i, j, k: (k, blk)))
    part = lambda n: (_sds((n, 1, D), F32), _bs((None, 1, D), lambda i, j, k: (i, 0, 0)))
    full2 = lambda a: (a, _bs(a.shape, lambda i, j, k: (0, 0)))

    normed = lambda xb, gb: (_rms(xb, gb).astype(BF16),) * 2
    keep_a = lambda t: [(_sds((seq, D), BF16), _bs((t, D), lambda i, j, k: (i, 0)))]
    proj, h1 = _mm("proj_in", "nn", (ni, 4, 1),
                   a_ins=[rows_i(x, tm), vec(g_pre_mix)], a_fn=normed,
                   b_ins=[(w_in, _bs((None, D, 1280), lambda i, j, k: (j, 0, 0)))], b_fn=_ident,
                   outs=[(_sds((seq, D_IN), F32), _bs((tm, 1280), lambda i, j, k: (i, j)))],
                   acc_shape=(tm, 1280), a_cache=((tm, D), BF16), a_outs=keep_a(tm))
    o = _attn_fwd(proj, seq)
    (y_attn,) = _mm("attn_out", "nn", (ni, 1, 1),
                    a_ins=[(o, _bs((tm, ATTN_W), lambda i, j, k: (i, 0)))], a_fn=_ident,
                    b_ins=[full2(w_ao)], b_fn=_ident,
                    outs=[(_sds((seq, D), F32), _bs((tm, D), lambda i, j, k: (i, 0)))], acc_shape=(tm, D))
    e, d = _conv_fwd(proj, w_conv, seq, tm)
    (y_conv,) = _mm("conv_out", "nn", (ni, 1, 1),
                    a_ins=[(e, _bs((tm, CONV_W), lambda i, j, k: (i, 0)))], a_fn=_ident,
                    b_ins=[full2(w_co)], b_fn=_ident,
                    outs=[(_sds((seq, D), F32), _bs((tm, D), lambda i, j, k: (i, 0)))], acc_shape=(tm, D))

    def mix_fn(ga, gc, ya, yc, ba, bc):
        return ((_sig(ga + ba) * ya + _sig(gc + bc) * yc).astype(BF16),) * 2

    def post_mix(acc, xb, gb):
        return acc, xb + _rms(acc, gb)

    mix_ins = lambda rows: [rows(proj, th, 3), rows(proj, th, 4), rows(y_attn, th), rows(y_conv, th),
                            vec(b_gate, 0), vec(b_gate, 1)]
    mixed, x1, mixin = _mm("mix_out", "nn", (nh, 1, 1),
                           a_ins=mix_ins(rows_i), a_fn=mix_fn, b_ins=[full2(w_o)], b_fn=_ident,
                           epi_ins=[rows_i(x, th), vec(g_post_mix)], epi_fn=post_mix,
                           outs=[(_sds((seq, D), F32), _bs((th, D), lambda i, j, k: (i, 0)))] * 2,
                           acc_shape=(th, D), a_cache=((th, D), BF16), a_outs=keep_a(th))
    up, h2 = _mm("mlp_up", "nn", (ni, 4, 1),
                 a_ins=[rows_i(x1, tm), vec(g_pre_mlp)], a_fn=normed,
                 b_ins=[(w_up, _bs((None, D, D), lambda i, j, k: (j, 0, 0)))], b_fn=_ident,
                 outs=[(_sds((seq, D_FF), BF16), _bs((tm, D), lambda i, j, k: (i, j)))],
                 acc_shape=(tm, D), a_cache=((tm, D), BF16), a_outs=keep_a(tm))

    def relu2(ub):
        r = jnp.maximum(ub.astype(F32), 0.0)
        return (r * r).astype(BF16)

    f, x2 = _mm("mlp_down", "nn", (ni, 1, 4),
                a_ins=[(up, _bs((tm, D), lambda i, j, k: (i, k)))], a_fn=relu2,
                b_ins=[(w_down, _bs((D, D), lambda i, j, k: (k, 0)))], b_fn=_ident,
                epi_ins=[rows_i(x1, tm), vec(g_post_mlp)], epi_fn=post_mix,
                outs=[(_sds((seq, D), F32), _bs((tm, D), lambda i, j, k: (i, 0)))] * 2, acc_shape=(tm, D))
    (pp,) = _mm("ple_proj", "nn", (ni, 1, 1),
                a_ins=[(p, _bs((tm, PLE_DIM), lambda i, j, k: (i, 0)))], a_fn=_to_bf16,
                b_ins=[full2(w_pp)], b_fn=_ident,
                outs=[(_sds((seq, D), F32), _bs((tm, D), lambda i, j, k: (i, 0)))], acc_shape=(tm, D))

    def head(acc, x2b, ppb, tb):
        pg = _sig(acc)
        err = x2b + pg * ppb - tb
        return pg, err * (1.0 / D), jnp.sum(err * err, axis=0, keepdims=True) * (0.5 / D)

    pg, dx3, loss_p, h3 = _mm("ple_gate_loss", "nn", (nh, 1, 1),
                              a_ins=[rows_i(x2, th), vec(g_ple)], a_fn=normed,
                              b_ins=[full2(w_pg)], b_fn=_ident,
                              epi_ins=[rows_i(x2, th), rows_i(pp, th), rows_i(tgt, th)], epi_fn=head,
                              outs=[(_sds((seq, D), F32), _bs((th, D), lambda i, j, k: (i, 0)))] * 2 + [part(nh)],
                              acc_shape=(th, D), a_cache=((th, D), BF16), a_outs=keep_a(th))

    (dw_pp,) = _mm("dw_ple_proj", "tn", (1, 1, nh),
                   a_ins=[(p, _bs((th, PLE_DIM), lambda i, j, k: (k, 0)))], a_fn=_to_bf16,
                   b_ins=[rows_k(dx3, th), rows_k(pg, th)], b_fn=lambda a, b: (a * b).astype(BF16),
                   outs=[(_sds((PLE_DIM, D), F32), _bs((PLE_DIM, D), lambda i, j, k: (0, 0)))],
                   acc_shape=(PLE_DIM, D))

    def dpre_fn(dx3b, ppb, pgb):
        return (dx3b * ppb * pgb * (1.0 - pgb)).astype(BF16)

    def ple_norm_bwd(acc, x2b, dx3b, gb):
        dxn, dg = _rms_bwd(x2b, gb, acc)
        return dx3b + dxn, dg

    dx2, dg_ple_p, dpre = _mm("d_ple_gate", "nt", (nh, 1, 1),
                              a_ins=[rows_i(dx3, th), rows_i(pp, th), rows_i(pg, th)],
                              a_fn=lambda a, b, c: (dpre_fn(a, b, c),) * 2,
                              b_ins=[full2(w_pg)], b_fn=_ident,
                              epi_ins=[rows_i(x2, th), rows_i(dx3, th), vec(g_ple)], epi_fn=ple_norm_bwd,
                              outs=[(_sds((seq, D), F32), _bs((th, D), lambda i, j, k: (i, 0))), part(nh)],
                              acc_shape=(th, D), a_cache=((th, D), BF16),
                              a_outs=[(_sds((seq, D), BF16), _bs((th, D), lambda i, j, k: (i, 0)))])
    (dw_pg,) = _mm("dw_ple_gate", "tn", (1, 1, ni),
                   a_ins=[rows_k(h3, tm)], a_fn=_ident, b_ins=[rows_k(dpre, tm)], b_fn=_ident,
                   outs=[(_sds((D, D), F32), _bs((D, D), lambda i, j, k: (0, 0)))], acc_shape=(D, D))

    def df_fn(fb, dx2b, gb):
        dfb, dg = _rms_bwd(fb, gb, dx2b)
        dfb = dfb.astype(BF16)
        return dfb, dfb, dg

    def dup_fn(acc, ub):
        return (acc * (2.0 * jnp.maximum(ub.astype(F32), 0.0)),)

    dup, df, dg_post_mlp_p = _mm("d_mlp_down", "nt", (ni, 4, 1),
                                 a_ins=[rows_i(f, tm), rows_i(dx2, tm), vec(g_post_mlp)], a_fn=df_fn,
                                 b_ins=[(w_down, _bs((D, D), lambda i, j, k: (j, 0)))], b_fn=_ident,
                                 epi_ins=[(up, _bs((tm, D), lambda i, j, k: (i, j)))], epi_fn=dup_fn,
                                 outs=[(_sds((seq, D_FF), BF16), _bs((tm, D), lambda i, j, k: (i, j)))],
                                 acc_shape=(tm, D), a_cache=((tm, D), BF16),
                                 a_outs=[(_sds((seq, D), BF16), _bs((tm, D), lambda i, j, k: (i, 0))), part(ni)])
    (dw_down,) = _mm("dw_mlp_down", "tn", (4, 1, ni),
                     a_ins=[(up, _bs((tm, D), lambda i, j, k: (k, i)))], a_fn=relu2,
                     b_ins=[rows_k(df, tm)], b_fn=_ident,
                     outs=[(_sds((D_FF, D), F32), _bs((D, D), lambda i, j, k: (i, 0)))], acc_shape=(D, D))
    (dw_up,) = _mm("dw_mlp_up", "tn", (1, 4, ni),
                   a_ins=[rows_k(h2, tm)], a_fn=_ident,
                   b_ins=[(dup, _bs((tm, D), lambda i, j, k: (k, j)))], b_fn=_ident,
                   outs=[(_sds((N_CHIPS, D, D), F32), _bs((None, D, D), lambda i, j, k: (j, 0, 0)))],
                   acc_shape=(D, D))

    def mlp_norm_bwd(acc, x1b, dx2b, mixedb, g_mlp, g_mix):
        dxn, dg_mlp = _rms_bwd(x1b, g_mlp, acc)
        dx1b = dx2b + dxn
        dmixedb, dg_mix = _rms_bwd(mixedb, g_mix, dx1b)
        return dx1b, dmixedb, dg_mlp, dg_mix

    dx1, dmixed, dg_pre_mlp_p, dg_post_mix_p = _mm(
        "d_mlp_up", "nt", (nh, 1, 4),
        a_ins=[(dup, _bs((th, D), lambda i, j, k: (i, k)))], a_fn=_ident,
        b_ins=[(w_up, _bs((None, D, D), lambda i, j, k: (k, 0, 0)))], b_fn=_ident,
        epi_ins=[rows_i(x1, th), rows_i(dx2, th), rows_i(mixed, th), vec(g_pre_mlp), vec(g_post_mix)],
        epi_fn=mlp_norm_bwd,
        outs=[(_sds((seq, D), F32), _bs((th, D), lambda i, j, k: (i, 0))),
              (_sds((seq, D), BF16), _bs((th, D), lambda i, j, k: (i, 0))), part(nh), part(nh)],
        acc_shape=(th, D))
    (dw_o,) = _mm("dw_mix_out", "tn", (1, 1, ni),
                  a_ins=[rows_k(mixin, tm)], a_fn=_ident, b_ins=[rows_k(dmixed, tm)], b_fn=_ident,
                  outs=[(_sds((D, D), F32), _bs((D, D), lambda i, j, k: (0, 0)))], acc_shape=(D, D))

    def gate_bwd(acc, ga, gc, ya, yc, ba, bc):
        sa, sc = _sig(ga + ba), _sig(gc + bc)
        dga = acc * ya * sa * (1.0 - sa)
        dgc = acc * yc * sc * (1.0 - sc)
        return (acc * sa, acc * sc, jnp.concatenate([dga, dgc], axis=1),
                jnp.sum(dga, axis=0, keepdims=True), jnp.sum(dgc, axis=0, keepdims=True))

    dya, dyc, dgate, dbg_a_p, dbg_c_p = _mm(
        "d_mix_out", "nt", (nh, 1, 1),
        a_ins=[rows_i(dmixed, th)], a_fn=_ident, b_ins=[full2(w_o)], b_fn=_ident,
        epi_ins=mix_ins(rows_i), epi_fn=gate_bwd,
        outs=[(_sds((seq, D), BF16), _bs((th, D), lambda i, j, k: (i, 0)))] * 2
             + [(_sds((seq, 2 * D), BF16), _bs((th, 2 * D), lambda i, j, k: (i, 0))), part(nh), part(nh)],
        acc_shape=(th, D))
    (dw_ao,) = _mm("dw_attn_out", "tn", (1, 1, nh),
                   a_ins=[(o, _bs((th, ATTN_W), lambda i, j, k: (k, 0)))], a_fn=_ident,
                   b_ins=[rows_k(dya, th)], b_fn=_ident,
                   outs=[(_sds((ATTN_W, D), F32), _bs((ATTN_W, D), lambda i, j, k: (0, 0)))], acc_shape=(ATTN_W, D))
    (do,) = _mm("d_attn_out", "nt", (ni, 1, 1),
                a_ins=[rows_i(dya, tm)], a_fn=_ident, b_ins=[full2(w_ao)], b_fn=_ident,
                outs=[(_sds((seq, ATTN_W), BF16), _bs((tm, ATTN_W), lambda i, j, k: (i, 0)))],
                acc_shape=(tm, ATTN_W))
    dq, dk, dv = _attn_bwd(proj, do, seq)
    (dw_co,) = _mm("dw_conv_out", "tn", (1, 1, nh),
                   a_ins=[(e, _bs((th, CONV_W), lambda i, j, k: (k, 0)))], a_fn=_ident,
                   b_ins=[rows_k(dyc, th)], b_fn=_ident,
                   outs=[(_sds((CONV_W, D), F32), _bs((CONV_W, D), lambda i, j, k: (0, 0)))], acc_shape=(CONV_W, D))
    (de,) = _mm("d_conv_out", "nt", (ni, 1, 1),
                a_ins=[rows_i(dyc, tm)], a_fn=_ident, b_ins=[full2(w_co)], b_fn=_ident,
                outs=[(_sds((seq, CONV_W), F32), _bs((tm, CONV_W), lambda i, j, k: (i, 0)))],
                acc_shape=(tm, CONV_W))
    dconv, dwc_p = _conv_bwd(proj, de, d, w_conv, seq, tm)
    dproj = jnp.concatenate([dq.astype(BF16), dk.astype(BF16), dv.astype(BF16), dconv, dgate], axis=1)
    (dw_in,) = _mm("dw_proj_in", "tn", (1, 4, ni),
                   a_ins=[rows_k(h1, tm)], a_fn=_ident,
                   b_ins=[(dproj, _bs((tm, 1280), lambda i, j, k: (k, j)))], b_fn=_ident,
                   outs=[(_sds((N_CHIPS, D, 1280), F32), _bs((None, D, 1280), lambda i, j, k: (j, 0, 0)))],
                   acc_shape=(D, 1280))

    def in_norm_bwd(acc, xb, dx1b, gb):
        dxn, dg = _rms_bwd(xb, gb, acc)
        return dx1b + dxn, dg

    grad_x, dg_pre_mix_p = _mm("d_proj_in", "nt", (ni, 1, 4),
                               a_ins=[(dproj, _bs((tm, 1280), lambda i, j, k: (i, k)))], a_fn=_ident,
                               b_ins=[(w_in, _bs((None, D, 1280), lambda i, j, k: (k, 0, 0)))], b_fn=_ident,
                               epi_ins=[rows_i(x, tm), rows_i(dx1, tm), vec(g_pre_mix)], epi_fn=in_norm_bwd,
                               outs=[(_sds((seq, D), F32), _bs((tm, D), lambda i, j, k: (i, 0))), part(ni)],
                               acc_shape=(tm, D))

    chip_major = lambda a: a.reshape(a.shape[0], N_CHIPS, a.shape[1] // N_CHIPS).transpose(1, 0, 2)
    big = [dw_in, chip_major(dw_ao), chip_major(dw_co), dw_o.reshape(N_CHIPS, D // N_CHIPS, D), dw_up,
           dw_down.reshape(N_CHIPS, D_FF // N_CHIPS, D), dw_pg.reshape(N_CHIPS, D // N_CHIPS, D), chip_major(dw_pp)]
    small = (loss_p, [dg_pre_mix_p, dg_post_mix_p, dg_pre_mlp_p, dg_post_mlp_p, dg_ple_p], dbg_a_p, dbg_c_p, dwc_p)
    return grad_x, big, small


RS_GROUPS = ((0,), (4,), (5,), (1, 2, 3, 6, 7))


def _reduce_scatter(big):
    pair = [None] * len(big)
    for gi, group in enumerate(RS_GROUPS):
        for w, s in zip(group, _rs_pair_sum(f"rs_pair_sum_{gi}", [big[w] for w in group])):
            pair[w] = s
    return _rs_exchange_join(pair)


def kernel(x, p, g_pre_mix, w_in, b_gate, w_conv, w_attn_out, w_conv_out, w_o, g_post_mix, g_pre_mlp, w_up, w_down, g_post_mlp, g_ple, w_ple_gate, w_ple_proj, loss_target, m_g_pre_mix, m_w_in, m_b_gate, m_w_conv, m_w_attn_out, m_w_conv_out, m_w_o, m_g_post_mix, m_g_pre_mlp, m_w_up, m_w_down, m_g_post_mlp, m_g_ple, m_w_ple_gate, m_w_ple_proj, v_g_pre_mix, v_w_in, v_b_gate, v_w_conv, v_w_attn_out, v_w_conv_out, v_w_o, v_g_post_mix, v_g_pre_mlp, v_w_up, v_w_down, v_g_post_mlp, v_g_ple, v_w_ple_gate, v_w_ple_proj):
    mats = [w_in, w_attn_out, w_conv_out, w_o, w_up, w_down, w_ple_gate, w_ple_proj]
    mats_m = [m_w_in, m_w_attn_out, m_w_conv_out, m_w_o, m_w_up, m_w_down, m_w_ple_gate, m_w_ple_proj]
    mats_v = [v_w_in, v_w_attn_out, v_w_conv_out, v_w_o, v_w_up, v_w_down, v_w_ple_gate, v_w_ple_proj]
    gains = [g_pre_mix, g_post_mix, g_pre_mlp, g_post_mlp, g_ple]
    gains_m = [m_g_pre_mix, m_g_post_mix, m_g_pre_mlp, m_g_post_mlp, m_g_ple]
    gains_v = [v_g_pre_mix, v_g_post_mix, v_g_pre_mlp, v_g_post_mlp, v_g_ple]

    taps = jnp.concatenate([w_conv[0], jnp.zeros((CONV_PAD_ROWS - 3, LANES), F32)], axis=0)
    gathered = _allgather_weights([w[0].astype(BF16) for w in mats] + [taps])
    cols_joined = lambda a: a.transpose(1, 0, 2).reshape(a.shape[1], N_CHIPS * a.shape[2])
    rows_joined = lambda a: a.reshape(N_CHIPS * a.shape[1], a.shape[2])
    wf = [gathered[0], cols_joined(gathered[1]), cols_joined(gathered[2]), rows_joined(gathered[3]), gathered[4],
          rows_joined(gathered[5]), rows_joined(gathered[6]), cols_joined(gathered[7])]
    w_conv_full = cols_joined(gathered[8])[0:3, :]
    chip = 2 * lax.axis_index("x") + lax.axis_index("y")

    grad_x, big, small = _local_step(x[0], p[0, 0], loss_target[0], gains, b_gate, w_conv_full, wf)

    shard_grads = _reduce_scatter(big)
    red = _small_allreduce(*small)
    loss = red[0, 0]
    grad_gains = [red[1 + r:2 + r, :] for r in range(5)]
    grad_b_gate = jnp.concatenate([red[6:7, :], red[7:8, :]], axis=1)
    grad_w_conv = lax.dynamic_slice(red[8:11, :], (0, chip * LANES), (3, LANES))[None]

    grads_big = [gr.reshape(w.shape) for gr, w in zip(shard_grads, mats)]
    upd_big = [_adamw(f"adamw_{i}", w, gr, m, v) for i, (w, gr, m, v) in enumerate(zip(mats, grads_big, mats_m, mats_v))]
    pack = lambda vs, bg: jnp.concatenate(list(vs) + [bg.reshape(2, D_MODEL), jnp.zeros((1, D_MODEL), F32)], axis=0)
    upd_small = _adamw("adamw_small", pack(gains, b_gate), pack(grad_gains, grad_b_gate),
                       pack(gains_m, m_b_gate), pack(gains_v, v_b_gate))
    upd_conv = _adamw("adamw_conv", w_conv, grad_w_conv, m_w_conv, v_w_conv)

    def small_out(a, which):
        gains_out = [a[r:r + 1, :] for r in range(5)]
        return gains_out, a[5:7, :].reshape(1, 2 * D_MODEL)

    def ordered(g_pre_mix_, big_, b_gate_, conv_, g_rest):
        return [g_pre_mix_, big_[0], b_gate_, conv_, big_[1], big_[2], big_[3], g_rest[0], g_rest[1], big_[4], big_[5],
                g_rest[2], g_rest[3], big_[6], big_[7]]

    outs = [loss, grad_x[None]]
    outs += ordered(grad_gains[0], grads_big, grad_b_gate, grad_w_conv, grad_gains[1:])
    for which in range(3):
        g_out, b_out = small_out(upd_small[which], which)
        outs += ordered(g_out[0], [u[which] for u in upd_big], b_out, upd_conv[which], g_out[1:])
    return tuple(outs)
```

```python
import functools

import jax
import jax.numpy as jnp
from jax import lax
from jax.experimental import pallas as pl
from jax.experimental.pallas import tpu as pltpu

F32 = jnp.float32
BF16 = jnp.bfloat16
MESH = pl.DeviceIdType.MESH

D_MODEL = 1024
N_HEADS = 8
HEAD_DIM = 64
ATTN_W = N_HEADS * HEAD_DIM
CONV_W = 512
D_FF = 4096
PLE_DIM = 256
D_IN = 5120
N_CHIPS = 4
EPS = 1e-6
Q_SCALE = HEAD_DIM ** -0.5

ADAM_LR = 0.001
ADAM_B1 = 0.9
ADAM_B2 = 0.999
ADAM_EPS = 1e-08
ADAM_WD = 0.01
ADAM_STEP = 10

V7X_VMEM_BYTES = 64 * 1024 * 1024
VMEM_LIMIT = V7X_VMEM_BYTES - 8 * 1024 * 1024
LANES = 128
ATT_BLK = 256
SMALL_ROWS = 16
CONV_PAD_ROWS = 16


def _cparams(n_grid):
    return pltpu.CompilerParams(dimension_semantics=("arbitrary",) * n_grid, vmem_limit_bytes=VMEM_LIMIT)


def _bs(shape, fn):
    return pl.BlockSpec(shape, fn)


def _rms_stats(xf):
    return lax.rsqrt(jnp.mean(xf * xf, axis=-1, keepdims=True) + EPS)


def _rms(xf, g):
    return xf * _rms_stats(xf) * g


def _rms_bwd(xf, g, dy):
    r = _rms_stats(xf)
    xh = xf * r
    dyg = dy * g
    dx = r * (dyg - xh * jnp.mean(dyg * xh, axis=-1, keepdims=True))
    return dx, jnp.sum(dy * xh, axis=0, keepdims=True)


def _sig(z):
    return 1.0 / (1.0 + jnp.exp(-z))


def _ident(a):
    return a


def _to_bf16(a):
    return a.astype(BF16)


_DIMS = {"nn": (((1,), (0,)), ((), ())), "nt": (((1,), (1,)), ((), ())), "tn": (((0,), (0,)), ((), ()))}


def _mm(name, mode, grid, a_ins, a_fn, b_ins, b_fn, outs, acc_shape, epi_ins=(), epi_fn=None,
        a_cache=None, a_outs=()):
    nk = grid[2]
    na, nb, ne, no, nao = len(a_ins), len(b_ins), len(epi_ins), len(outs), len(a_outs)
    assert a_cache is None or nk == 1
    assert not a_outs or a_cache is not None
    dims = _DIMS[mode]
    if epi_fn is None:
        epi_fn = lambda acc: (acc,)

    def body(*refs):
        a_refs = refs[:na]
        b_refs = refs[na:na + nb]
        e_refs = refs[na + nb:na + nb + ne]
        o_refs = refs[na + nb + ne:na + nb + ne + no]
        ao_refs = refs[na + nb + ne + no:na + nb + ne + no + nao]
        scratch = list(refs[na + nb + ne + no + nao:])
        acc_ref = scratch.pop(0) if nk > 1 else None
        a_sc = scratch.pop(0) if a_cache is not None else None
        j = pl.program_id(1)
        k = pl.program_id(2)

        def finish(acc):
            res = epi_fn(acc, *[r[...] for r in e_refs])
            for r, val in zip(o_refs, res):
                r[...] = val.astype(r.dtype)

        if a_sc is not None:
            @pl.when(j == 0)
            def _():
                res = a_fn(*[r[...] for r in a_refs])
                if nao:
                    for r, val in zip(ao_refs, res[1:]):
                        r[...] = val.astype(r.dtype)
                    res = res[0]
                a_sc[...] = res
            a = a_sc[...]
        else:
            a = a_fn(*[r[...] for r in a_refs])
        b = b_fn(*[r[...] for r in b_refs])
        prod = lax.dot_general(a, b, dims, preferred_element_type=F32)
        if nk == 1:
            finish(prod)
        else:
            @pl.when(k == 0)
            def _():
                acc_ref[...] = prod

            @pl.when(k > 0)
            def _():
                acc_ref[...] += prod

            @pl.when(k == nk - 1)
            def _():
                finish(acc_ref[...])

    scratch_shapes = []
    if nk > 1:
        scratch_shapes.append(pltpu.VMEM(acc_shape, F32))
    if a_cache is not None:
        scratch_shapes.append(pltpu.VMEM(*a_cache))
    all_outs = list(outs) + list(a_outs)
    res = pl.pallas_call(
        body, name=name, grid=grid,
        in_specs=[s for _, s in a_ins] + [s for _, s in b_ins] + [s for _, s in epi_ins],
        out_specs=[s for _, s in all_outs],
        out_shape=[o for o, _ in all_outs],
        scratch_shapes=scratch_shapes,
        compiler_params=_cparams(3),
    )(*[a for a, _ in a_ins], *[a for a, _ in b_ins], *[a for a, _ in epi_ins])
    return res


def _sds(shape, dtype):
    return jax.ShapeDtypeStruct(shape, dtype)


def _shift_rows_down(u, prev, n):
    rows = u.shape[0]
    ridx = lax.broadcasted_iota(jnp.int32, u.shape, 0)
    out = pltpu.roll(u, n, 0)
    for r in range(n):
        out = jnp.where(ridx == r, prev[8 - n + r:8 - n + r + 1, :], out)
    del rows
    return out


def _shift_rows_up(u, nxt, n):
    rows = u.shape[0]
    ridx = lax.broadcasted_iota(jnp.int32, u.shape, 0)
    out = pltpu.roll(u, rows - n, 0)
    for r in range(n):
        out = jnp.where(ridx == rows - n + r, nxt[r:r + 1, :], out)
    return out


CONV_COL0 = 3


def _conv_fwd(proj, w_conv, seq, tr):
    hb = tr // 8

    def body(cb_ref, cc_ref, cu_ref, ccp_ref, cup_ref, w_ref, e_ref, d_ref):
        i = pl.program_id(0)
        u = cc_ref[...] * cu_ref[...]
        up = jnp.where(i > 0, ccp_ref[...] * cup_ref[...], 0.0)
        w = w_ref[...]
        d = w[0:1, :] * _shift_rows_down(u, up, 2) + w[1:2, :] * _shift_rows_down(u, up, 1) + w[2:3, :] * u
        d_ref[...] = d
        e_ref[...] = (cb_ref[...] * d).astype(BF16)

    prev = lambda c: (lambda i: (jnp.maximum(i * hb - 1, 0), c))
    return pl.pallas_call(
        body, name="conv_fwd", grid=(seq // tr,),
        in_specs=[_bs((tr, CONV_W), lambda i: (i, CONV_COL0)),
                  _bs((tr, CONV_W), lambda i: (i, CONV_COL0 + 1)),
                  _bs((tr, CONV_W), lambda i: (i, CONV_COL0 + 2)),
                  _bs((8, CONV_W), prev(CONV_COL0 + 1)),
                  _bs((8, CONV_W), prev(CONV_COL0 + 2)),
                  _bs((3, CONV_W), lambda i: (0, 0))],
        out_specs=[_bs((tr, CONV_W), lambda i: (i, 0)), _bs((tr, CONV_W), lambda i: (i, 0))],
        out_shape=[_sds((seq, CONV_W), BF16), _sds((seq, CONV_W), F32)],
        compiler_params=_cparams(1),
    )(proj, proj, proj, proj, proj, w_conv)


def _conv_bwd(proj, de, d, w_conv, seq, tr):
    hb = tr // 8
    nblk = seq // tr

    def body(cb_ref, cc_ref, cu_ref, ccp_ref, cup_ref, cbn_ref, de_ref, den_ref, d_ref, w_ref, o_ref, dw_ref):
        i = pl.program_id(0)
        cc, cu, cb = cc_ref[...], cu_ref[...], cb_ref[...]
        u = cc * cu
        up = jnp.where(i > 0, ccp_ref[...] * cup_ref[...], 0.0)
        u1 = _shift_rows_down(u, up, 1)
        u2 = _shift_rows_down(u, up, 2)
        de_ = de_ref[...]
        dd = de_ * cb
        ddn = jnp.where(i < nblk - 1, den_ref[...] * cbn_ref[...], 0.0)
        w = w_ref[...]
        du = w[2:3, :] * dd + w[1:2, :] * _shift_rows_up(dd, ddn, 1) + w[0:1, :] * _shift_rows_up(dd, ddn, 2)
        o_ref[:, 0:CONV_W] = (de_ * d_ref[...]).astype(BF16)
        o_ref[:, CONV_W:2 * CONV_W] = (du * cu).astype(BF16)
        o_ref[:, 2 * CONV_W:3 * CONV_W] = (du * cc).astype(BF16)
        ridx = lax.broadcasted_iota(jnp.int32, (8, CONV_W), 0)
        dw0 = jnp.sum(dd * u2, axis=0, keepdims=True)
        dw1 = jnp.sum(dd * u1, axis=0, keepdims=True)
        dw2 = jnp.sum(dd * u, axis=0, keepdims=True)
        dw_ref[...] = jnp.where(ridx == 0, dw0, jnp.where(ridx == 1, dw1, jnp.where(ridx == 2, dw2, 0.0)))

    prev = lambda c: (lambda i: (jnp.maximum(i * hb - 1, 0), c))
    nxt = lambda c: (lambda i: (jnp.minimum((i + 1) * hb, seq // 8 - 1), c))
    return pl.pallas_call(
        body, name="conv_bwd", grid=(nblk,),
        in_specs=[_bs((tr, CONV_W), lambda i: (i, CONV_COL0)),
                  _bs((tr, CONV_W), lambda i: (i, CONV_COL0 + 1)),
                  _bs((tr, CONV_W), lambda i: (i, CONV_COL0 + 2)),
                  _bs((8, CONV_W), prev(CONV_COL0 + 1)),
                  _bs((8, CONV_W), prev(CONV_COL0 + 2)),
                  _bs((8, CONV_W), nxt(CONV_COL0)),
                  _bs((tr, CONV_W), lambda i: (i, 0)),
                  _bs((8, CONV_W), nxt(0)),
                  _bs((tr, CONV_W), lambda i: (i, 0)),
                  _bs((3, CONV_W), lambda i: (0, 0))],
        out_specs=[_bs((tr, 3 * CONV_W), lambda i: (i, 0)), _bs((None, 8, CONV_W), lambda i: (i, 0, 0))],
        out_shape=[_sds((seq, 3 * CONV_W), BF16), _sds((nblk, 8, CONV_W), F32)],
        compiler_params=_cparams(1),
    )(proj, proj, proj, proj, proj, proj, de, de, d, w_conv)


def _nt(a, b):
    return lax.dot_general(a, b, _DIMS["nt"], preferred_element_type=F32)


def _tn(a, b):
    return lax.dot_general(a, b, _DIMS["tn"], preferred_element_type=F32)


def _nn(a, b):
    return lax.dot_general(a, b, _DIMS["nn"], preferred_element_type=F32)


def _log_gates(z):
    lse = jnp.log(1.0 + jnp.exp(-jnp.abs(z)))
    log_beta = jnp.minimum(z, 0.0) - lse
    return log_beta, log_beta - z


DEAD_LOG_WEIGHT = -110.0


def _first_live_tile(start, scores, live_sc):
    def alive():
        return jnp.max(jnp.maximum(live_sc[0], live_sc[1])) > DEAD_LOG_WEIGHT

    def step(c):
        for h, z in enumerate(scores(c[0])):
            live_sc[h] = live_sc[h] + jnp.sum(_log_gates(z)[1], axis=-1, keepdims=True)
        return c[0] - 1, alive()

    j_end, _ = lax.while_loop(lambda c: jnp.logical_and(c[0] >= 0, c[1]), step, (start, alive()))
    return j_end + 1


def _attn_fwd(proj, seq):
    blk = ATT_BLK
    nq = seq // blk
    npair = N_HEADS // 2

    def body(q_ref, k_ref, v_ref, o_ref, z0_sc, z1_sc, w0_sc, w1_sc, tot_sc, live_sc, acc_sc):
        i = pl.program_id(1)
        is_a = lax.broadcasted_iota(jnp.int32, (1, LANES), 1) < HEAD_DIM
        q2 = (q_ref[...] * Q_SCALE).astype(BF16)
        zero = jnp.zeros_like(q2)
        qs = (jnp.where(is_a, q2, zero), jnp.where(is_a, zero, q2))
        row = lax.broadcasted_iota(jnp.int32, (blk, blk), 0)
        col = lax.broadcasted_iota(jnp.int32, (blk, blk), 1)
        tri = (row > col).astype(BF16)
        causal = col < row

        def tile_of(ref, j):
            return ref[pl.ds(pl.multiple_of(j * blk, blk), blk), :].astype(BF16)

        def scores(j):
            k2 = tile_of(k_ref, j)
            return [_nt(qs[h], k2) for h in range(2)]

        has_left = i > 0
        left = jnp.maximum(i - 1, 0)
        g_d = [_log_gates(z) for z in scores(i)]
        g_l = [_log_gates(z) for z in scores(left)]
        keep_d = [jnp.where(causal, g[1], 0.0) for g in g_d]
        keep_l = [jnp.where(has_left, g[1], 0.0) for g in g_l]
        suf_d = [_nn(lk.astype(BF16), tri) for lk in keep_d]
        suf_l = [_nn(lk.astype(BF16), tri) for lk in keep_l]
        v_d, v_l = tile_of(v_ref, i), tile_of(v_ref, left)
        pv = []
        for h in range(2):
            sum_d = jnp.sum(keep_d[h], axis=-1, keepdims=True)
            w_d = jnp.where(causal, jnp.exp(g_d[h][0] + suf_d[h]), 0.0)
            w_l = jnp.where(has_left, jnp.exp(g_l[h][0] + (sum_d + suf_l[h])), 0.0)
            pv.append(_nn(w_d.astype(BF16), v_d) + _nn(w_l.astype(BF16), v_l))
            tot_sc[h] = sum_d + jnp.sum(keep_l[h], axis=-1, keepdims=True)
        acc_sc[...] = jnp.where(is_a, pv[0], pv[1])

        live_sc[...] = tot_sc[...]
        first = _first_live_tile(i - 2, scores, live_sc)
        trips = i - 1 - first
        z_bufs, w_bufs = (z0_sc, z1_sc), (w0_sc, w1_sc)

        def put(ref, vals):
            for h in range(2):
                ref[h] = vals[h]

        def weights(zs):
            gates = [_log_gates(z) for z in zs]
            sums = [_nn(g[1].astype(BF16), tri) for g in gates]
            ws = []
            for h in range(2):
                ws.append(jnp.exp(gates[h][0] + (tot_sc[h] + sums[h])).astype(BF16))
                tot_sc[h] = tot_sc[h] + jnp.sum(gates[h][1], axis=-1, keepdims=True)
            return ws

        def add_values(w_buf, j):
            v2 = tile_of(v_ref, j)
            acc_sc[...] += jnp.where(is_a, _nn(w_buf[0], v2), _nn(w_buf[1], v2))

        def trip(j, s):
            add_values(w_bufs[s], j + 1)
            put(z_bufs[1 - s], scores(jnp.maximum(j - 1, first)))
            put(w_bufs[1 - s], weights((z_bufs[s][0], z_bufs[s][1])))

        @pl.when(trips > 0)
        def _():
            put(z0_sc, scores(i - 2))
            w0_sc[...] = jnp.zeros_like(w0_sc)

            def two_trips(pp, carry):
                j = i - 2 - 2 * pp
                trip(j, 0)
                trip(j - 1, 1)
                return carry

            lax.fori_loop(0, trips // 2, two_trips, 0)
            odd = trips % 2 == 1

            @pl.when(odd)
            def _():
                trip(first, 0)
                add_values(w1_sc, first)

            @pl.when(jnp.logical_not(odd))
            def _():
                add_values(w0_sc, first)

        o_ref[...] = acc_sc[...].astype(BF16)

    return pl.pallas_call(
        body, name="attn_fwd", grid=(npair, nq),
        in_specs=[_bs((blk, LANES), lambda p, i: (i, p)),
                  _bs((seq, LANES), lambda p, i: (0, npair + p)),
                  _bs((seq, LANES), lambda p, i: (0, 2 * npair + p))],
        out_specs=_bs((blk, LANES), lambda p, i: (i, p)),
        out_shape=_sds((seq, ATTN_W), BF16),
        scratch_shapes=[pltpu.VMEM((2, blk, blk), F32), pltpu.VMEM((2, blk, blk), F32),
                        pltpu.VMEM((2, blk, blk), BF16), pltpu.VMEM((2, blk, blk), BF16),
                        pltpu.VMEM((2, blk, 1), F32), pltpu.VMEM((2, blk, 1), F32), pltpu.VMEM((blk, LANES), F32)],
        compiler_params=_cparams(2),
    )(proj, proj, proj)


def _attn_bwd(proj, do, seq):
    blk = ATT_BLK
    nq = seq // blk
    npair = N_HEADS // 2

    def body(q_ref, k_ref, v_ref, do_ref, dq_ref, dk_ref, dv_ref,
             prod0_sc, prod1_sc, pend0_sc, pend1_sc, tot_sc, live_sc, cum_sc, pre_sc, dq_sc):
        i = pl.program_id(1)

        @pl.when(i == 0)
        def _():
            dk_ref[...] = jnp.zeros_like(dk_ref)
            dv_ref[...] = jnp.zeros_like(dv_ref)

        is_a = lax.broadcasted_iota(jnp.int32, (1, LANES), 1) < HEAD_DIM
        q2 = (q_ref[...] * Q_SCALE).astype(BF16)
        do2 = do_ref[...]
        zero = jnp.zeros_like(q2)
        qs = (jnp.where(is_a, q2, zero), jnp.where(is_a, zero, q2))
        dos = (jnp.where(is_a, do2, zero), jnp.where(is_a, zero, do2))
        row = lax.broadcasted_iota(jnp.int32, (blk, blk), 0)
        col = lax.broadcasted_iota(jnp.int32, (blk, blk), 1)
        tri_after = (row > col).astype(BF16)
        tri_excl = (row < col).astype(BF16)
        causal = col < row

        def tile_of(ref, j):
            return ref[pl.ds(pl.multiple_of(j * blk, blk), blk), :].astype(BF16)

        def scores(j):
            k2 = tile_of(k_ref, j)
            return [_nt(qs[h], k2) for h in range(2)]

        def products(j):
            v2 = tile_of(v_ref, j)
            return scores(j) + [_nt(dos[h], v2) for h in range(2)]

        def row_sum(a):
            return jnp.sum(a, axis=-1, keepdims=True)

        def grad_matmuls(ws, dzs, j):
            rows = pl.ds(pl.multiple_of(j * blk, blk), blk)
            k2 = tile_of(k_ref, j)
            dq_sc[...] += jnp.where(is_a, _nn(dzs[0], k2), _nn(dzs[1], k2))
            dk_ref[rows, :] += jnp.where(is_a, _tn(dzs[0], q2), _tn(dzs[1], q2))
            if ws is not None:
                dv_ref[rows, :] += jnp.where(is_a, _tn(ws[0], do2), _tn(ws[1], do2))

        has_left = i > 0
        left = jnp.maximum(i - 1, 0)
        p_d, p_l = products(i), products(left)
        g_d = [_log_gates(z) for z in p_d[:2]]
        g_l = [_log_gates(z) for z in p_l[:2]]
        keep_d = [jnp.where(causal, g[1], 0.0) for g in g_d]
        keep_l = [jnp.where(has_left, g[1], 0.0) for g in g_l]
        suf_d = [_nn(lk.astype(BF16), tri_after) for lk in keep_d]
        suf_l = [_nn(lk.astype(BF16), tri_after) for lk in keep_l]
        w_d, w_l, gg_d, gg_l = [], [], [], []
        for h in range(2):
            sum_d = row_sum(keep_d[h])
            w_d.append(jnp.where(causal, jnp.exp(g_d[h][0] + suf_d[h]), 0.0))
            w_l.append(jnp.where(has_left, jnp.exp(g_l[h][0] + (sum_d + suf_l[h])), 0.0))
            gg_d.append(p_d[2 + h] * w_d[h])
            gg_l.append(p_l[2 + h] * w_l[h])
            tot_sc[h] = sum_d + row_sum(keep_l[h])
        before_d = [_nn(g.astype(BF16), tri_excl) for g in gg_d]
        before_l = [_nn(g.astype(BF16), tri_excl) for g in gg_l]
        dz_d, dz_l = [], []
        for h in range(2):
            beta_d, beta_l = jnp.exp(g_d[h][0]), jnp.exp(g_l[h][0])
            dz = gg_l[h] * (1.0 - beta_l) - before_l[h] * beta_l
            dz_l.append(jnp.where(has_left, dz, 0.0).astype(BF16))
            dz = gg_d[h] * (1.0 - beta_d) - (row_sum(gg_l[h]) + before_d[h]) * beta_d
            dz_d.append(jnp.where(causal, dz, 0.0).astype(BF16))
        dq_sc[...] = jnp.zeros_like(dq_sc)
        grad_matmuls([w.astype(BF16) for w in w_l], dz_l, left)
        grad_matmuls([w.astype(BF16) for w in w_d], dz_d, i)

        live_sc[...] = tot_sc[...]
        first = _first_live_tile(i - 2, scores, live_sc)
        trips = i - 1 - first
        prod_bufs, pend_bufs = (prod0_sc, prod1_sc), (pend0_sc, pend1_sc)

        def local_grads(prods):
            zs, dws = prods[:2], prods[2:]
            gates = [_log_gates(z) for z in zs]
            sums = [_nn(g[1].astype(BF16), tri_after) for g in gates]
            ws, gs = [], []
            for h in range(2):
                cum = cum_sc[h] + row_sum(gates[h][1])
                cum_sc[h] = cum
                ws.append(jnp.exp(gates[h][0] + ((live_sc[h] - cum) + sums[h])))
                gs.append(dws[h] * ws[h])
            befores = [_nn(g.astype(BF16), tri_excl) for g in gs]
            dzs = []
            for h in range(2):
                beta = jnp.exp(gates[h][0])
                dzs.append((gs[h] * (1.0 - beta) - (pre_sc[h] + befores[h]) * beta).astype(BF16))
                pre_sc[h] = pre_sc[h] + row_sum(gs[h])
            return [w.astype(BF16) for w in ws] + dzs

        def put(ref, vals):
            for n, val in enumerate(vals):
                ref[n] = val

        def flush(pend, j):
            grad_matmuls([pend[0], pend[1]], [pend[2], pend[3]], j)

        def trip(j, s):
            flush(pend_bufs[s], jnp.maximum(j - 1, first))
            put(prod_bufs[1 - s], products(j + 1))
            put(pend_bufs[1 - s], local_grads([prod_bufs[s][n] for n in range(4)]))

        def earlier_keys_share(j, mask):
            dzs = []
            for h, z in enumerate(scores(j)):
                beta = jnp.exp(_log_gates(z)[0])
                dzs.append(jnp.where(mask, -pre_sc[h] * beta, 0.0).astype(BF16))
            grad_matmuls(None, dzs, j)

        @pl.when(trips > 0)
        def _():
            cum_sc[...] = jnp.zeros_like(cum_sc)
            pre_sc[...] = jnp.zeros_like(pre_sc)
            pend0_sc[...] = jnp.zeros_like(pend0_sc)
            put(prod0_sc, products(first))

            def two_trips(pp, carry):
                trip(first + 2 * pp, 0)
                trip(first + 2 * pp + 1, 1)
                return carry

            lax.fori_loop(0, trips // 2, two_trips, 0)
            odd = trips % 2 == 1

            @pl.when(odd)
            def _():
                trip(i - 2, 0)
                flush(pend1_sc, i - 2)

            @pl.when(jnp.logical_not(odd))
            def _():
                flush(pend0_sc, i - 2)

            earlier_keys_share(i - 1, True)
            earlier_keys_share(i, causal)

        dq_ref[...] = dq_sc[...] * Q_SCALE

    qmap = lambda p, i: (i, p)
    return pl.pallas_call(
        body, name="attn_bwd", grid=(npair, nq),
        in_specs=[_bs((blk, LANES), qmap),
                  _bs((seq, LANES), lambda p, i: (0, npair + p)),
                  _bs((seq, LANES), lambda p, i: (0, 2 * npair + p)),
                  _bs((blk, LANES), qmap)],
        out_specs=[_bs((blk, LANES), qmap),
                   _bs((seq, LANES), lambda p, i: (0, p)),
                   _bs((seq, LANES), lambda p, i: (0, p))],
        out_shape=[_sds((seq, ATTN_W), F32)] * 3,
        scratch_shapes=[pltpu.VMEM((4, blk, blk), F32), pltpu.VMEM((4, blk, blk), F32),
                        pltpu.VMEM((4, blk, blk), BF16), pltpu.VMEM((4, blk, blk), BF16),
                        pltpu.VMEM((2, blk, 1), F32), pltpu.VMEM((2, blk, 1), F32), pltpu.VMEM((2, blk, 1), F32),
                        pltpu.VMEM((2, blk, 1), F32), pltpu.VMEM((blk, LANES), F32)],
        compiler_params=_cparams(2),
    )(proj, proj, proj, do)


def _elementwise(name, fn, ins, out_dtypes):
    rows, cols = ins[0].shape
    tr = rows
    for cand in (512, 256, 128, 64, 32, 16, 8):
        if rows % cand == 0 and cand * cols * 4 <= 2 * 1024 * 1024:
            tr = cand
            break
    n_in = len(ins)

    def body(*refs):
        res = fn(*[r[...] for r in refs[:n_in]])
        for r, val in zip(refs[n_in:], res):
            r[...] = val.astype(r.dtype)

    spec = _bs((tr, cols), lambda i: (i, 0))
    return pl.pallas_call(
        body, name=name, grid=(rows // tr,),
        in_specs=[spec] * n_in, out_specs=[spec] * len(out_dtypes),
        out_shape=[_sds((rows, cols), dt) for dt in out_dtypes],
        compiler_params=_cparams(1),
    )(*ins)


def _adamw_fn(w, g, m, v):
    m = ADAM_B1 * m + (1.0 - ADAM_B1) * g
    v = ADAM_B2 * v + (1.0 - ADAM_B2) * (g * g)
    m_hat = m / (1.0 - ADAM_B1 ** ADAM_STEP)
    v_hat = v / (1.0 - ADAM_B2 ** ADAM_STEP)
    delta = -ADAM_LR * (m_hat / (jnp.sqrt(v_hat) + ADAM_EPS) + ADAM_WD * w)
    return delta, m, v


def _adamw(name, w, g, m, v):
    shape = w.shape
    as2d = lambda a: a.reshape(-1, shape[-1])
    delta, nm, nv = _elementwise(name, _adamw_fn, [as2d(w), as2d(g), as2d(m), as2d(v)], [F32, F32, F32])
    return delta.reshape(shape), nm.reshape(shape), nv.reshape(shape)


def _place():
    x, y, c = lax.axis_index("x"), lax.axis_index("y"), lax.axis_index("c")
    chips = [(1 - x, y), (x, 1 - y), (1 - x, 1 - y)]
    return x, y, c, chips


ANY = pl.BlockSpec(memory_space=pl.ANY)
VMEM_WHOLE = pl.BlockSpec(memory_space=pltpu.VMEM)


def _allgather_weights(shards):
    n = len(shards)

    def body(*refs):
        src, dst = refs[:n], refs[n:2 * n]
        send_sems, recv_sems, local_sems = refs[2 * n:]
        x, y, c, chips = _place()
        me, sibling, mychip = (x, y, c), (x, y, 1 - c), 2 * x + y

        def piece(w, chip, half):
            hr = src[w].shape[0] // 2
            return dst[w].at[chip, pl.ds(half * hr, hr)]

        def copy(w, k, src_ref, dst_ref, to):
            return pltpu.make_async_remote_copy(src_ref=src_ref, dst_ref=dst_ref, send_sem=send_sems.at[w, k],
                                                recv_sem=recv_sems.at[w, k], device_id=to, device_id_type=MESH)

        started, local = [], []
        for w in range(n):
            hr = src[w].shape[0] // 2
            own = pltpu.make_async_copy(src[w], dst[w].at[mychip], local_sems.at[w])
            own.start()
            local.append(own)
            for r, (cx, cy) in enumerate(chips):
                cp = copy(w, r, src[w].at[pl.ds(c * hr, hr)], piece(w, mychip, c), (cx, cy, c))
                cp.start()
                started.append(cp)
        for w in range(n):
            for r, (cx, cy) in enumerate(chips):
                landed = piece(w, 2 * cx + cy, c)
                copy(w, r, landed, landed, me).wait_recv()
                fwd = copy(w, 3 + r, landed, landed, sibling)
                fwd.start()
                started.append(fwd)
        for w in range(n):
            for r, (cx, cy) in enumerate(chips):
                from_sib = piece(w, 2 * cx + cy, 1 - c)
                copy(w, 3 + r, from_sib, from_sib, me).wait_recv()
        for cp in local:
            cp.wait()
        for cp in started:
            cp.wait_send()

    return pl.pallas_call(
        body, name="allgather_weights",
        in_specs=[VMEM_WHOLE] * n, out_specs=[VMEM_WHOLE] * n,
        out_shape=[_sds((N_CHIPS,) + s.shape, s.dtype) for s in shards],
        scratch_shapes=[pltpu.SemaphoreType.DMA((n, 6)), pltpu.SemaphoreType.DMA((n, 6)),
                        pltpu.SemaphoreType.DMA((n,))],
        compiler_params=pltpu.CompilerParams(vmem_limit_bytes=VMEM_LIMIT),
    )(*shards)


SUM_ROWS = 64


def _rs_pair_sum(name, grads):
    n = len(grads)

    def body(*refs):
        g, out = refs[:n], refs[n:2 * n]
        stage, land, keep = refs[2 * n:3 * n], refs[3 * n:4 * n], refs[4 * n:5 * n]
        send_sems, recv_sems, stage_sems, keep_sems = refs[5 * n:]
        x, y, c, _ = _place()
        sibling = (x, y, 1 - c)
        loads = []
        for w in range(n):
            hr = g[w].shape[1] // 2
            st = pltpu.make_async_copy(g[w].at[:, pl.ds((1 - c) * hr, hr)], stage[w], stage_sems.at[w])
            kp = pltpu.make_async_copy(g[w].at[:, pl.ds(c * hr, hr)], keep[w], keep_sems.at[w])
            st.start()
            kp.start()
            loads.append((st, kp))
        gives = []
        for w in range(n):
            loads[w][0].wait()
            give = pltpu.make_async_remote_copy(src_ref=stage[w], dst_ref=land[w], send_sem=send_sems.at[w],
                                                recv_sem=recv_sems.at[w], device_id=sibling, device_id_type=MESH)
            give.start()
            gives.append(give)
        for w in range(n):
            loads[w][1].wait()
            gives[w].wait_recv()
            nb = g[w].shape[1] // 2 // SUM_ROWS

            def add(idx, carry, w=w, nb=nb):
                k, r = idx // nb, pl.multiple_of((idx % nb) * SUM_ROWS, SUM_ROWS)
                rows = pl.ds(r, SUM_ROWS)
                out[w][k, rows, :] = (keep[w][k, rows, :] + land[w][k, rows, :]).astype(BF16)
                return carry

            lax.fori_loop(0, N_CHIPS * nb, add, 0)
        for give in gives:
            give.wait_send()

    half = [(N_CHIPS, a.shape[1] // 2, a.shape[2]) for a in grads]
    bufs = [pltpu.VMEM(s, F32) for s in half]
    sems = pltpu.SemaphoreType.DMA((n,))
    return pl.pallas_call(
        body, name=name,
        in_specs=[ANY] * n, out_specs=[VMEM_WHOLE] * n, out_shape=[_sds(s, BF16) for s in half],
        scratch_shapes=bufs + bufs + bufs + [sems, sems, sems, sems],
        compiler_params=pltpu.CompilerParams(vmem_limit_bytes=VMEM_LIMIT),
    )(*grads)


def _rs_exchange_join(parts):
    n = len(parts)

    def body(*refs):
        t, full, got = refs[:n], refs[n:2 * n], refs[2 * n:3 * n]
        send_sems, recv_sems = refs[3 * n:]
        x, y, c, chips = _place()
        mychip, sibling = 2 * x + y, (x, y, 1 - c)
        sends = []
        for w in range(n):
            for r, (cx, cy) in enumerate(chips):
                cp = pltpu.make_async_remote_copy(src_ref=t[w].at[2 * cx + cy], dst_ref=got[w].at[r],
                                                  send_sem=send_sems.at[w, r], recv_sem=recv_sems.at[w, r],
                                                  device_id=(cx, cy, c), device_id_type=MESH)
                cp.start()
                sends.append(cp)
        for w in range(n):
            hr = t[w].shape[1]
            for r in range(3):
                pltpu.make_async_remote_copy(src_ref=got[w].at[r], dst_ref=got[w].at[r], send_sem=send_sems.at[w, r],
                                             recv_sem=recv_sems.at[w, r], device_id=sibling,
                                             device_id_type=MESH).wait_recv()

            def add(idx, carry, w=w, hr=hr):
                r = pl.multiple_of(idx * SUM_ROWS, SUM_ROWS)
                rows = pl.ds(r, SUM_ROWS)
                f = lambda v: v.astype(F32)
                total = ((f(t[w][mychip, rows, :]) + f(got[w][0, rows, :])) + f(got[w][1, rows, :])) \
                    + f(got[w][2, rows, :])
                full[w][pl.ds(pl.multiple_of(c * hr + r, SUM_ROWS), SUM_ROWS), :] = total
                return carry

            lax.fori_loop(0, hr // SUM_ROWS, add, 0)
            mine = full[w].at[pl.ds(c * hr, hr)]
            give = pltpu.make_async_remote_copy(src_ref=mine, dst_ref=mine, send_sem=send_sems.at[w, 3],
                                                recv_sem=recv_sems.at[w, 3], device_id=sibling, device_id_type=MESH)
            give.start()
            sends.append(give)
        for w in range(n):
            hr = t[w].shape[1]
            theirs = full[w].at[pl.ds((1 - c) * hr, hr)]
            pltpu.make_async_remote_copy(src_ref=theirs, dst_ref=theirs, send_sem=send_sems.at[w, 3],
                                         recv_sem=recv_sems.at[w, 3], device_id=sibling, device_id_type=MESH).wait_recv()
        for cp in sends:
            cp.wait_send()

    return pl.pallas_call(
        body, name="rs_exchange_join",
        in_specs=[VMEM_WHOLE] * n, out_specs=[VMEM_WHOLE] * n,
        out_shape=[_sds((2 * a.shape[1], a.shape[2]), F32) for a in parts],
        scratch_shapes=[pltpu.VMEM((3,) + a.shape[1:], a.dtype) for a in parts]
        + [pltpu.SemaphoreType.DMA((n, 4)), pltpu.SemaphoreType.DMA((n, 4))],
        compiler_params=pltpu.CompilerParams(vmem_limit_bytes=VMEM_LIMIT),
    )(*parts)


def _small_allreduce(loss_p, dg_parts, dbg_a, dbg_c, dwc):
    ins = [loss_p] + list(dg_parts) + [dbg_a, dbg_c, dwc]
    n_in = len(ins)
    vmem = pl.BlockSpec(memory_space=pltpu.VMEM)

    def body(*refs):
        in_refs = refs[:n_in]
        out_ref, vec, buf, send_sems, recv_sems = refs[n_in:]
        x, y, c, _ = _place()
        me = 4 * x + 2 * y + c
        vec[...] = jnp.zeros_like(vec)
        vec[0:1, :] = jnp.sum(in_refs[0][...], axis=0)
        for r in range(5):
            vec[1 + r:2 + r, :] = jnp.sum(in_refs[1 + r][...], axis=0)
        vec[6:7, :] = jnp.sum(in_refs[6][...], axis=0)
        vec[7:8, :] = jnp.sum(in_refs[7][...], axis=0)
        vec[8:16, 0:CONV_W] = jnp.sum(in_refs[8][...], axis=0)
        buf[pl.ds(me, 1)] = vec[...][None]
        copies = []
        for r in range(1, 8):
            fx, fy, fc = (r >> 2) & 1, (r >> 1) & 1, r & 1
            to = (1 - x if fx else x, 1 - y if fy else y, 1 - c if fc else c)
            cp = pltpu.make_async_remote_copy(src_ref=vec, dst_ref=buf.at[me], send_sem=send_sems.at[r - 1],
                                              recv_sem=recv_sems.at[r - 1], device_id=to, device_id_type=MESH)
            cp.start()
            copies.append(cp)
        for cp in copies:
            cp.wait()
        total = buf[0]
        for s in range(1, 8):
            total = total + buf[s]
        out_ref[...] = total
        out_ref[0:1, :] = jnp.broadcast_to(jnp.sum(total[0:1, :], axis=-1, keepdims=True), (1, D_MODEL))

    return pl.pallas_call(
        body, name="small_allreduce",
        in_specs=[vmem] * n_in, out_specs=vmem, out_shape=_sds((SMALL_ROWS, D_MODEL), F32),
        scratch_shapes=[pltpu.VMEM((SMALL_ROWS, D_MODEL), F32), pltpu.VMEM((8, SMALL_ROWS, D_MODEL), F32),
                        pltpu.SemaphoreType.DMA((7,)), pltpu.SemaphoreType.DMA((7,))],
    )(*ins)


def _local_step(x, p, tgt, g, b_gate, w_conv, wf):
    seq = x.shape[0]
    tm = min(seq, 1024)
    th = min(seq, 512)
    tl = min(seq, 2048)
    ni, nh, nl = seq // tm, seq // th, seq // tl
    g_pre_mix, g_post_mix, g_pre_mlp, g_post_mlp, g_ple = g
    w_in, w_ao, w_co, w_o, w_up, w_down, w_pg, w_pp, w_in_nat, w_up_nat = wf
    D = D_MODEL
    vec = lambda a, blk=0: (a, _bs((1, D), lambda i, j, k: (0, blk)))
    rows_i = lambda a, t, blk=0: (a, _bs((t, D), lambda i, j, k: (i, blk)))
    rows_k = lambda a, t, blk=0: (a, _bs((t, D), lambda i, j, k: (k, blk)))
    part = lambda n: (_sds((n, 1, D), F32), _bs((None, 1, D), lambda i, j, k: (i, 0, 0)))
    full2 = lambda a: (a, _bs(a.shape, lambda i, j, k: (0, 0)))

    normed = lambda xb, gb: (_rms(xb, gb).astype(BF16),) * 2
    keep_a = lambda t: [(_sds((seq, D), BF16), _bs((t, D), lambda i, j, k: (i, 0)))]
    proj, h1 = _mm("proj_in", "nn", (ni, 4, 1),
                   a_ins=[rows_i(x, tm), vec(g_pre_mix)], a_fn=normed,
                   b_ins=[(w_in, _bs((None, D, 1280), lambda i, j, k: (j, 0, 0)))], b_fn=_ident,
                   outs=[(_sds((seq, D_IN), F32), _bs((tm, 1280), lambda i, j, k: (i, j)))],
                   acc_shape=(tm, 1280), a_cache=((tm, D), BF16), a_outs=keep_a(tm))
    o = _attn_fwd(proj, seq)
    (y_attn,) = _mm("attn_out", "nn", (ni, 1, 1),
                    a_ins=[(o, _bs((tm, ATTN_W), lambda i, j, k: (i, 0)))], a_fn=_ident,
                    b_ins=[full2(w_ao)], b_fn=_ident,
                    outs=[(_sds((seq, D), F32), _bs((tm, D), lambda i, j, k: (i, 0)))], acc_shape=(tm, D))
    e, d = _conv_fwd(proj, w_conv, seq, tm)
    (y_conv,) = _mm("conv_out", "nn", (ni, 1, 1),
                    a_ins=[(e, _bs((tm, CONV_W), lambda i, j, k: (i, 0)))], a_fn=_ident,
                    b_ins=[full2(w_co)], b_fn=_ident,
                    outs=[(_sds((seq, D), F32), _bs((tm, D), lambda i, j, k: (i, 0)))], acc_shape=(tm, D))

    def mix_fn(ga, gc, ya, yc, ba, bc):
        return ((_sig(ga + ba) * ya + _sig(gc + bc) * yc).astype(BF16),) * 2

    def post_mix(acc, xb, gb):
        return acc, xb + _rms(acc, gb)

    mix_ins = lambda rows: [rows(proj, th, 3), rows(proj, th, 4), rows(y_attn, th), rows(y_conv, th),
                            vec(b_gate, 0), vec(b_gate, 1)]
    mixed, x1, mixin = _mm("mix_out", "nn", (nh, 1, 1),
                           a_ins=mix_ins(rows_i), a_fn=mix_fn, b_ins=[full2(w_o)], b_fn=_ident,
                           epi_ins=[rows_i(x, th), vec(g_post_mix)], epi_fn=post_mix,
                           outs=[(_sds((seq, D), F32), _bs((th, D), lambda i, j, k: (i, 0)))] * 2,
                           acc_shape=(th, D), a_cache=((th, D), BF16), a_outs=keep_a(th))
    up, h2 = _mm("mlp_up", "nn", (ni, 4, 1),
                 a_ins=[rows_i(x1, tm), vec(g_pre_mlp)], a_fn=normed,
                 b_ins=[(w_up, _bs((None, D, D), lambda i, j, k: (j, 0, 0)))], b_fn=_ident,
                 outs=[(_sds((seq, D_FF), BF16), _bs((tm, D), lambda i, j, k: (i, j)))],
                 acc_shape=(tm, D), a_cache=((tm, D), BF16), a_outs=keep_a(tm))

    def relu2(ub):
        r = jnp.maximum(ub.astype(F32), 0.0)
        return (r * r).astype(BF16)

    f, x2 = _mm("mlp_down", "nn", (nh, 1, 1),
                a_ins=[(up, _bs((th, D_FF), lambda i, j, k: (i, 0)))], a_fn=relu2,
                b_ins=[full2(w_down)], b_fn=_ident,
                epi_ins=[rows_i(x1, th), vec(g_post_mlp)], epi_fn=post_mix,
                outs=[(_sds((seq, D), F32), _bs((th, D), lambda i, j, k: (i, 0)))] * 2, acc_shape=(th, D))
    (pp,) = _mm("ple_proj", "nn", (ni, 1, 1),
                a_ins=[(p, _bs((tm, PLE_DIM), lambda i, j, k: (i, 0)))], a_fn=_to_bf16,
                b_ins=[full2(w_pp)], b_fn=_ident,
                outs=[(_sds((seq, D), F32), _bs((tm, D), lambda i, j, k: (i, 0)))], acc_shape=(tm, D))

    def head(acc, x2b, ppb, tb):
        pg = _sig(acc)
        err = x2b + pg * ppb - tb
        return pg, err * (1.0 / D), jnp.sum(err * err, axis=0, keepdims=True) * (0.5 / D)

    pg, dx3, loss_p, h3 = _mm("ple_gate_loss", "nn", (nh, 1, 1),
                              a_ins=[rows_i(x2, th), vec(g_ple)], a_fn=normed,
                              b_ins=[full2(w_pg)], b_fn=_ident,
                              epi_ins=[rows_i(x2, th), rows_i(pp, th), rows_i(tgt, th)], epi_fn=head,
                              outs=[(_sds((seq, D), F32), _bs((th, D), lambda i, j, k: (i, 0)))] * 2 + [part(nh)],
                              acc_shape=(th, D), a_cache=((th, D), BF16), a_outs=keep_a(th))

    (dw_pp,) = _mm("dw_ple_proj", "tn", (1, 1, nh),
                   a_ins=[(p, _bs((th, PLE_DIM), lambda i, j, k: (k, 0)))], a_fn=_to_bf16,
                   b_ins=[rows_k(dx3, th), rows_k(pg, th)], b_fn=lambda a, b: (a * b).astype(BF16),
                   outs=[(_sds((PLE_DIM, D), F32), _bs((PLE_DIM, D), lambda i, j, k: (0, 0)))],
                   acc_shape=(PLE_DIM, D))

    def dpre_fn(dx3b, ppb, pgb):
        return (dx3b * ppb * pgb * (1.0 - pgb)).astype(BF16)

    def ple_norm_bwd(acc, x2b, dx3b, gb):
        dxn, dg = _rms_bwd(x2b, gb, acc)
        return dx3b + dxn, dg

    dx2, dg_ple_p, dpre = _mm("d_ple_gate", "nt", (nh, 1, 1),
                              a_ins=[rows_i(dx3, th), rows_i(pp, th), rows_i(pg, th)],
                              a_fn=lambda a, b, c: (dpre_fn(a, b, c),) * 2,
                              b_ins=[full2(w_pg)], b_fn=_ident,
                              epi_ins=[rows_i(x2, th), rows_i(dx3, th), vec(g_ple)], epi_fn=ple_norm_bwd,
                              outs=[(_sds((seq, D), F32), _bs((th, D), lambda i, j, k: (i, 0))), part(nh)],
                              acc_shape=(th, D), a_cache=((th, D), BF16),
                              a_outs=[(_sds((seq, D), BF16), _bs((th, D), lambda i, j, k: (i, 0)))])
    (dw_pg,) = _mm("dw_ple_gate", "tn", (1, 1, ni),
                   a_ins=[rows_k(h3, tm)], a_fn=_ident, b_ins=[rows_k(dpre, tm)], b_fn=_ident,
                   outs=[(_sds((D, D), F32), _bs((D, D), lambda i, j, k: (0, 0)))], acc_shape=(D, D))

    def df_fn(fb, dx2b, gb):
        dfb, dg = _rms_bwd(fb, gb, dx2b)
        dfb = dfb.astype(BF16)
        return dfb, dfb, dg

    def dup_fn(acc, ub):
        return (acc * (2.0 * jnp.maximum(ub.astype(F32), 0.0)),)

    dup, df, dg_post_mlp_p = _mm("d_mlp_down", "nt", (ni, 4, 1),
                                 a_ins=[rows_i(f, tm), rows_i(dx2, tm), vec(g_post_mlp)], a_fn=df_fn,
                                 b_ins=[(w_down, _bs((D, D), lambda i, j, k: (j, 0)))], b_fn=_ident,
                                 epi_ins=[(up, _bs((tm, D), lambda i, j, k: (i, j)))], epi_fn=dup_fn,
                                 outs=[(_sds((seq, D_FF), BF16), _bs((tm, D), lambda i, j, k: (i, j)))],
                                 acc_shape=(tm, D), a_cache=((tm, D), BF16),
                                 a_outs=[(_sds((seq, D), BF16), _bs((tm, D), lambda i, j, k: (i, 0))), part(ni)])
    (dw_down,) = _mm("dw_mlp_down", "tn", (4, 1, nl),
                     a_ins=[(up, _bs((tl, D), lambda i, j, k: (k, i)))], a_fn=relu2,
                     b_ins=[rows_k(df, tl)], b_fn=_ident,
                     outs=[(_sds((D_FF, D), F32), _bs((D, D), lambda i, j, k: (i, 0)))], acc_shape=(D, D))
    (dw_up,) = _mm("dw_mlp_up", "tn", (1, 4, nl),
                   a_ins=[rows_k(h2, tl)], a_fn=_ident,
                   b_ins=[(dup, _bs((tl, D), lambda i, j, k: (k, j)))], b_fn=_ident,
                   outs=[(_sds((N_CHIPS, D, D), F32), _bs((None, D, D), lambda i, j, k: (j, 0, 0)))],
                   acc_shape=(D, D))

    def mlp_norm_bwd(acc, x1b, dx2b, mixedb, g_mlp, g_mix):
        dxn, dg_mlp = _rms_bwd(x1b, g_mlp, acc)
        dx1b = dx2b + dxn
        dmixedb, dg_mix = _rms_bwd(mixedb, g_mix, dx1b)
        return dx1b, dmixedb, dg_mlp, dg_mix

    dx1, dmixed, dg_pre_mlp_p, dg_post_mix_p = _mm(
        "d_mlp_up", "nt", (nh, 1, 1),
        a_ins=[(dup, _bs((th, D_FF), lambda i, j, k: (i, 0)))], a_fn=_ident,
        b_ins=[full2(w_up_nat)], b_fn=_ident,
        epi_ins=[rows_i(x1, th), rows_i(dx2, th), rows_i(mixed, th), vec(g_pre_mlp), vec(g_post_mix)],
        epi_fn=mlp_norm_bwd,
        outs=[(_sds((seq, D), F32), _bs((th, D), lambda i, j, k: (i, 0))),
              (_sds((seq, D), BF16), _bs((th, D), lambda i, j, k: (i, 0))), part(nh), part(nh)],
        acc_shape=(th, D))
    (dw_o,) = _mm("dw_mix_out", "tn", (1, 1, ni),
                  a_ins=[rows_k(mixin, tm)], a_fn=_ident, b_ins=[rows_k(dmixed, tm)], b_fn=_ident,
                  outs=[(_sds((D, D), F32), _bs((D, D), lambda i, j, k: (0, 0)))], acc_shape=(D, D))

    def gate_bwd(acc, ga, gc, ya, yc, ba, bc):
        sa, sc = _sig(ga + ba), _sig(gc + bc)
        dga = acc * ya * sa * (1.0 - sa)
        dgc = acc * yc * sc * (1.0 - sc)
        return (acc * sa, acc * sc, jnp.concatenate([dga, dgc], axis=1),
                jnp.sum(dga, axis=0, keepdims=True), jnp.sum(dgc, axis=0, keepdims=True))

    dya, dyc, dgate, dbg_a_p, dbg_c_p = _mm(
        "d_mix_out", "nt", (nh, 1, 1),
        a_ins=[rows_i(dmixed, th)], a_fn=_ident, b_ins=[full2(w_o)], b_fn=_ident,
        epi_ins=mix_ins(rows_i), epi_fn=gate_bwd,
        outs=[(_sds((seq, D), BF16), _bs((th, D), lambda i, j, k: (i, 0)))] * 2
             + [(_sds((seq, 2 * D), BF16), _bs((th, 2 * D), lambda i, j, k: (i, 0))), part(nh), part(nh)],
        acc_shape=(th, D))
    (dw_ao,) = _mm("dw_attn_out", "tn", (1, 1, nh),
                   a_ins=[(o, _bs((th, ATTN_W), lambda i, j, k: (k, 0)))], a_fn=_ident,
                   b_ins=[rows_k(dya, th)], b_fn=_ident,
                   outs=[(_sds((ATTN_W, D), F32), _bs((ATTN_W, D), lambda i, j, k: (0, 0)))], acc_shape=(ATTN_W, D))
    (do,) = _mm("d_attn_out", "nt", (ni, 1, 1),
                a_ins=[rows_i(dya, tm)], a_fn=_ident, b_ins=[full2(w_ao)], b_fn=_ident,
                outs=[(_sds((seq, ATTN_W), BF16), _bs((tm, ATTN_W), lambda i, j, k: (i, 0)))],
                acc_shape=(tm, ATTN_W))
    dq, dk, dv = _attn_bwd(proj, do, seq)
    (dw_co,) = _mm("dw_conv_out", "tn", (1, 1, nh),
                   a_ins=[(e, _bs((th, CONV_W), lambda i, j, k: (k, 0)))], a_fn=_ident,
                   b_ins=[rows_k(dyc, th)], b_fn=_ident,
                   outs=[(_sds((CONV_W, D), F32), _bs((CONV_W, D), lambda i, j, k: (0, 0)))], acc_shape=(CONV_W, D))
    (de,) = _mm("d_conv_out", "nt", (ni, 1, 1),
                a_ins=[rows_i(dyc, tm)], a_fn=_ident, b_ins=[full2(w_co)], b_fn=_ident,
                outs=[(_sds((seq, CONV_W), F32), _bs((tm, CONV_W), lambda i, j, k: (i, 0)))],
                acc_shape=(tm, CONV_W))
    dconv, dwc_p = _conv_bwd(proj, de, d, w_conv, seq, tm)
    dproj = jnp.concatenate([dq.astype(BF16), dk.astype(BF16), dv.astype(BF16), dconv, dgate], axis=1)
    (dw_in,) = _mm("dw_proj_in", "tn", (1, 4, nl),
                   a_ins=[rows_k(h1, tl)], a_fn=_ident,
                   b_ins=[(dproj, _bs((tl, 1280), lambda i, j, k: (k, j)))], b_fn=_ident,
                   outs=[(_sds((N_CHIPS, D, 1280), F32), _bs((None, D, 1280), lambda i, j, k: (j, 0, 0)))],
                   acc_shape=(D, 1280))

    def in_norm_bwd(acc, xb, dx1b, gb):
        dxn, dg = _rms_bwd(xb, gb, acc)
        return dx1b + dxn, dg

    grad_x, dg_pre_mix_p = _mm("d_proj_in", "nt", (nh, 1, 1),
                               a_ins=[(dproj, _bs((th, D_IN), lambda i, j, k: (i, 0)))], a_fn=_ident,
                               b_ins=[full2(w_in_nat)], b_fn=_ident,
                               epi_ins=[rows_i(x, th), rows_i(dx1, th), vec(g_pre_mix)], epi_fn=in_norm_bwd,
                               outs=[(_sds((seq, D), F32), _bs((th, D), lambda i, j, k: (i, 0))), part(nh)],
                               acc_shape=(th, D))

    chip_major = lambda a: a.reshape(a.shape[0], N_CHIPS, a.shape[1] // N_CHIPS).transpose(1, 0, 2)
    big = [dw_in, chip_major(dw_ao), chip_major(dw_co), dw_o.reshape(N_CHIPS, D // N_CHIPS, D), dw_up,
           dw_down.reshape(N_CHIPS, D_FF // N_CHIPS, D), dw_pg.reshape(N_CHIPS, D // N_CHIPS, D), chip_major(dw_pp)]
    small = (loss_p, [dg_pre_mix_p, dg_post_mix_p, dg_pre_mlp_p, dg_post_mlp_p, dg_ple_p], dbg_a_p, dbg_c_p, dwc_p)
    return grad_x, big, small


RS_GROUPS = ((0,), (4,), (5,), (1, 2, 3, 6, 7))


def _reduce_scatter(big):
    pair = [None] * len(big)
    for gi, group in enumerate(RS_GROUPS):
        for w, s in zip(group, _rs_pair_sum(f"rs_pair_sum_{gi}", [big[w] for w in group])):
            pair[w] = s
    return _rs_exchange_join(pair)


def kernel(x, p, g_pre_mix, w_in, b_gate, w_conv, w_attn_out, w_conv_out, w_o, g_post_mix, g_pre_mlp, w_up, w_down, g_post_mlp, g_ple, w_ple_gate, w_ple_proj, loss_target, m_g_pre_mix, m_w_in, m_b_gate, m_w_conv, m_w_attn_out, m_w_conv_out, m_w_o, m_g_post_mix, m_g_pre_mlp, m_w_up, m_w_down, m_g_post_mlp, m_g_ple, m_w_ple_gate, m_w_ple_proj, v_g_pre_mix, v_w_in, v_b_gate, v_w_conv, v_w_attn_out, v_w_conv_out, v_w_o, v_g_post_mix, v_g_pre_mlp, v_w_up, v_w_down, v_g_post_mlp, v_g_ple, v_w_ple_gate, v_w_ple_proj):
    mats = [w_in, w_attn_out, w_conv_out, w_o, w_up, w_down, w_ple_gate, w_ple_proj]
    mats_m = [m_w_in, m_w_attn_out, m_w_conv_out, m_w_o, m_w_up, m_w_down, m_w_ple_gate, m_w_ple_proj]
    mats_v = [v_w_in, v_w_attn_out, v_w_conv_out, v_w_o, v_w_up, v_w_down, v_w_ple_gate, v_w_ple_proj]
    gains = [g_pre_mix, g_post_mix, g_pre_mlp, g_post_mlp, g_ple]
    gains_m = [m_g_pre_mix, m_g_post_mix, m_g_pre_mlp, m_g_post_mlp, m_g_ple]
    gains_v = [v_g_pre_mix, v_g_post_mix, v_g_pre_mlp, v_g_post_mlp, v_g_ple]

    taps = jnp.concatenate([w_conv[0], jnp.zeros((CONV_PAD_ROWS - 3, LANES), F32)], axis=0)
    gathered = _allgather_weights([w[0].astype(BF16) for w in mats] + [taps])
    cols_joined = lambda a: a.transpose(1, 0, 2).reshape(a.shape[1], N_CHIPS * a.shape[2])
    rows_joined = lambda a: a.reshape(N_CHIPS * a.shape[1], a.shape[2])
    wf = [gathered[0], cols_joined(gathered[1]), cols_joined(gathered[2]), rows_joined(gathered[3]), gathered[4],
          rows_joined(gathered[5]), rows_joined(gathered[6]), cols_joined(gathered[7]),
          cols_joined(gathered[0]), cols_joined(gathered[4])]
    w_conv_full = cols_joined(gathered[8])[0:3, :]
    chip = 2 * lax.axis_index("x") + lax.axis_index("y")

    grad_x, big, small = _local_step(x[0], p[0, 0], loss_target[0], gains, b_gate, w_conv_full, wf)

    shard_grads = _reduce_scatter(big)
    red = _small_allreduce(*small)
    loss = red[0, 0]
    grad_gains = [red[1 + r:2 + r, :] for r in range(5)]
    grad_b_gate = jnp.concatenate([red[6:7, :], red[7:8, :]], axis=1)
    grad_w_conv = lax.dynamic_slice(red[8:11, :], (0, chip * LANES), (3, LANES))[None]

    grads_big = [gr.reshape(w.shape) for gr, w in zip(shard_grads, mats)]
    upd_big = [_adamw(f"adamw_{i}", w, gr, m, v) for i, (w, gr, m, v) in enumerate(zip(mats, grads_big, mats_m, mats_v))]
    pack = lambda vs, bg: jnp.concatenate(list(vs) + [bg.reshape(2, D_MODEL), jnp.zeros((1, D_MODEL), F32)], axis=0)
    upd_small = _adamw("adamw_small", pack(gains, b_gate), pack(grad_gains, grad_b_gate),
                       pack(gains_m, m_b_gate), pack(gains_v, v_b_gate))
    upd_conv = _adamw("adamw_conv", w_conv, grad_w_conv, m_w_conv, v_w_conv)

    def small_out(a, which):
        gains_out = [a[r:r + 1, :] for r in range(5)]
        return gains_out, a[5:7, :].reshape(1, 2 * D_MODEL)

    def ordered(g_pre_mix_, big_, b_gate_, conv_, g_rest):
        return [g_pre_mix_, big_[0], b_gate_, conv_, big_[1], big_[2], big_[3], g_rest[0], g_rest[1], big_[4], big_[5],
                g_rest[2], g_rest[3], big_[6], big_[7]]

    outs = [loss, grad_x[None]]
    outs += ordered(grad_gains[0], grads_big, grad_b_gate, grad_w_conv, grad_gains[1:])
    for which in range(3):
        g_out, b_out = small_out(upd_small[which], which)
        outs += ordered(g_out[0], [u[which] for u in upd_big], b_out, upd_conv[which], g_out[1:])
    return tuple(outs)
```

```python
import functools

import jax
import jax.numpy as jnp
from jax import lax
from jax.experimental import pallas as pl
from jax.experimental.pallas import tpu as pltpu

F32 = jnp.float32
BF16 = jnp.bfloat16
MESH = pl.DeviceIdType.MESH

D_MODEL = 1024
N_HEADS = 8
HEAD_DIM = 64
ATTN_W = N_HEADS * HEAD_DIM
CONV_W = 512
D_FF = 4096
PLE_DIM = 256
D_IN = 5120
N_CHIPS = 4
EPS = 1e-6
Q_SCALE = HEAD_DIM ** -0.5

ADAM_LR = 0.001
ADAM_B1 = 0.9
ADAM_B2 = 0.999
ADAM_EPS = 1e-08
ADAM_WD = 0.01
ADAM_STEP = 10

V7X_VMEM_BYTES = 64 * 1024 * 1024
VMEM_LIMIT = V7X_VMEM_BYTES - 8 * 1024 * 1024
LANES = 128
ATT_BLK = 256
SMALL_ROWS = 16
CONV_PAD_ROWS = 16


def _cparams(n_grid):
    return pltpu.CompilerParams(dimension_semantics=("arbitrary",) * n_grid, vmem_limit_bytes=VMEM_LIMIT)


def _bs(shape, fn):
    return pl.BlockSpec(shape, fn)


def _rms_stats(xf):
    return lax.rsqrt(jnp.mean(xf * xf, axis=-1, keepdims=True) + EPS)


def _rms(xf, g):
    return xf * _rms_stats(xf) * g


def _rms_bwd(xf, g, dy):
    r = _rms_stats(xf)
    xh = xf * r
    dyg = dy * g
    dx = r * (dyg - xh * jnp.mean(dyg * xh, axis=-1, keepdims=True))
    return dx, jnp.sum(dy * xh, axis=0, keepdims=True)


def _sig(z):
    return 1.0 / (1.0 + jnp.exp(-z))


def _ident(a):
    return a


def _to_bf16(a):
    return a.astype(BF16)


_DIMS = {"nn": (((1,), (0,)), ((), ())), "nt": (((1,), (1,)), ((), ())), "tn": (((0,), (0,)), ((), ()))}


def _mm(name, mode, grid, a_ins, a_fn, b_ins, b_fn, outs, acc_shape, epi_ins=(), epi_fn=None,
        a_cache=None, a_outs=(), epi_a=()):
    nk = grid[2]
    na, nb, ne, no, nao = len(a_ins), len(b_ins), len(epi_ins), len(outs), len(a_outs)
    assert a_cache is None or nk == 1
    assert not a_outs or a_cache is not None
    dims = _DIMS[mode]
    if epi_fn is None:
        epi_fn = lambda acc: (acc,)

    def body(*refs):
        a_refs = refs[:na]
        b_refs = refs[na:na + nb]
        e_refs = refs[na + nb:na + nb + ne]
        o_refs = refs[na + nb + ne:na + nb + ne + no]
        ao_refs = refs[na + nb + ne + no:na + nb + ne + no + nao]
        scratch = list(refs[na + nb + ne + no + nao:])
        acc_ref = scratch.pop(0) if nk > 1 else None
        a_sc = scratch.pop(0) if a_cache is not None else None
        j = pl.program_id(1)
        k = pl.program_id(2)

        def finish(acc):
            res = epi_fn(acc, *[a_refs[t][...] for t in epi_a], *[r[...] for r in e_refs])
            for r, val in zip(o_refs, res):
                r[...] = val.astype(r.dtype)

        if a_sc is not None:
            @pl.when(j == 0)
            def _():
                res = a_fn(*[r[...] for r in a_refs])
                if nao:
                    for r, val in zip(ao_refs, res[1:]):
                        r[...] = val.astype(r.dtype)
                    res = res[0]
                a_sc[...] = res
            a = a_sc[...]
        else:
            a = a_fn(*[r[...] for r in a_refs])
        b = b_fn(*[r[...] for r in b_refs])
        prod = lax.dot_general(a, b, dims, preferred_element_type=F32)
        if nk == 1:
            finish(prod)
        else:
            @pl.when(k == 0)
            def _():
                acc_ref[...] = prod

            @pl.when(k > 0)
            def _():
                acc_ref[...] += prod

            @pl.when(k == nk - 1)
            def _():
                finish(acc_ref[...])

    scratch_shapes = []
    if nk > 1:
        scratch_shapes.append(pltpu.VMEM(acc_shape, F32))
    if a_cache is not None:
        scratch_shapes.append(pltpu.VMEM(*a_cache))
    all_outs = list(outs) + list(a_outs)
    res = pl.pallas_call(
        body, name=name, grid=grid,
        in_specs=[s for _, s in a_ins] + [s for _, s in b_ins] + [s for _, s in epi_ins],
        out_specs=[s for _, s in all_outs],
        out_shape=[o for o, _ in all_outs],
        scratch_shapes=scratch_shapes,
        compiler_params=_cparams(3),
    )(*[a for a, _ in a_ins], *[a for a, _ in b_ins], *[a for a, _ in epi_ins])
    return res


def _sds(shape, dtype):
    return jax.ShapeDtypeStruct(shape, dtype)


def _shift_rows_down(u, prev, n):
    rows = u.shape[0]
    ridx = lax.broadcasted_iota(jnp.int32, u.shape, 0)
    out = pltpu.roll(u, n, 0)
    for r in range(n):
        out = jnp.where(ridx == r, prev[8 - n + r:8 - n + r + 1, :], out)
    del rows
    return out


def _shift_rows_up(u, nxt, n):
    rows = u.shape[0]
    ridx = lax.broadcasted_iota(jnp.int32, u.shape, 0)
    out = pltpu.roll(u, rows - n, 0)
    for r in range(n):
        out = jnp.where(ridx == rows - n + r, nxt[r:r + 1, :], out)
    return out


CONV_COL0 = 3


def _conv_fwd(proj, w_conv, seq, tr):
    hb = tr // 8

    def body(cb_ref, cc_ref, cu_ref, ccp_ref, cup_ref, w_ref, e_ref, d_ref):
        i = pl.program_id(0)
        u = cc_ref[...] * cu_ref[...]
        up = jnp.where(i > 0, ccp_ref[...] * cup_ref[...], 0.0)
        w = w_ref[...]
        d = w[0:1, :] * _shift_rows_down(u, up, 2) + w[1:2, :] * _shift_rows_down(u, up, 1) + w[2:3, :] * u
        d_ref[...] = d
        e_ref[...] = (cb_ref[...] * d).astype(BF16)

    prev = lambda c: (lambda i: (jnp.maximum(i * hb - 1, 0), c))
    return pl.pallas_call(
        body, name="conv_fwd", grid=(seq // tr,),
        in_specs=[_bs((tr, CONV_W), lambda i: (i, CONV_COL0)),
                  _bs((tr, CONV_W), lambda i: (i, CONV_COL0 + 1)),
                  _bs((tr, CONV_W), lambda i: (i, CONV_COL0 + 2)),
                  _bs((8, CONV_W), prev(CONV_COL0 + 1)),
                  _bs((8, CONV_W), prev(CONV_COL0 + 2)),
                  _bs((3, CONV_W), lambda i: (0, 0))],
        out_specs=[_bs((tr, CONV_W), lambda i: (i, 0)), _bs((tr, CONV_W), lambda i: (i, 0))],
        out_shape=[_sds((seq, CONV_W), BF16), _sds((seq, CONV_W), F32)],
        compiler_params=_cparams(1),
    )(proj, proj, proj, proj, proj, w_conv)


def _conv_bwd(proj, de, d, w_conv, seq, tr):
    hb = tr // 8
    nblk = seq // tr

    def body(cb_ref, cc_ref, cu_ref, ccp_ref, cup_ref, cbn_ref, de_ref, den_ref, d_ref, w_ref, o_ref, dw_ref):
        i = pl.program_id(0)
        cc, cu, cb = cc_ref[...], cu_ref[...], cb_ref[...]
        u = cc * cu
        up = jnp.where(i > 0, ccp_ref[...] * cup_ref[...], 0.0)
        u1 = _shift_rows_down(u, up, 1)
        u2 = _shift_rows_down(u, up, 2)
        de_ = de_ref[...]
        dd = de_ * cb
        ddn = jnp.where(i < nblk - 1, den_ref[...] * cbn_ref[...], 0.0)
        w = w_ref[...]
        du = w[2:3, :] * dd + w[1:2, :] * _shift_rows_up(dd, ddn, 1) + w[0:1, :] * _shift_rows_up(dd, ddn, 2)
        o_ref[:, 0:CONV_W] = (de_ * d_ref[...]).astype(BF16)
        o_ref[:, CONV_W:2 * CONV_W] = (du * cu).astype(BF16)
        o_ref[:, 2 * CONV_W:3 * CONV_W] = (du * cc).astype(BF16)
        ridx = lax.broadcasted_iota(jnp.int32, (8, CONV_W), 0)
        dw0 = jnp.sum(dd * u2, axis=0, keepdims=True)
        dw1 = jnp.sum(dd * u1, axis=0, keepdims=True)
        dw2 = jnp.sum(dd * u, axis=0, keepdims=True)
        dw_ref[...] = jnp.where(ridx == 0, dw0, jnp.where(ridx == 1, dw1, jnp.where(ridx == 2, dw2, 0.0)))

    prev = lambda c: (lambda i: (jnp.maximum(i * hb - 1, 0), c))
    nxt = lambda c: (lambda i: (jnp.minimum((i + 1) * hb, seq // 8 - 1), c))
    return pl.pallas_call(
        body, name="conv_bwd", grid=(nblk,),
        in_specs=[_bs((tr, CONV_W), lambda i: (i, CONV_COL0)),
                  _bs((tr, CONV_W), lambda i: (i, CONV_COL0 + 1)),
                  _bs((tr, CONV_W), lambda i: (i, CONV_COL0 + 2)),
                  _bs((8, CONV_W), prev(CONV_COL0 + 1)),
                  _bs((8, CONV_W), prev(CONV_COL0 + 2)),
                  _bs((8, CONV_W), nxt(CONV_COL0)),
                  _bs((tr, CONV_W), lambda i: (i, 0)),
                  _bs((8, CONV_W), nxt(0)),
                  _bs((tr, CONV_W), lambda i: (i, 0)),
                  _bs((3, CONV_W), lambda i: (0, 0))],
        out_specs=[_bs((tr, 3 * CONV_W), lambda i: (i, 0)), _bs((None, 8, CONV_W), lambda i: (i, 0, 0))],
        out_shape=[_sds((seq, 3 * CONV_W), BF16), _sds((nblk, 8, CONV_W), F32)],
        compiler_params=_cparams(1),
    )(proj, proj, proj, proj, proj, proj, de, de, d, w_conv)


def _nt(a, b):
    return lax.dot_general(a, b, _DIMS["nt"], preferred_element_type=F32)


def _tn(a, b):
    return lax.dot_general(a, b, _DIMS["tn"], preferred_element_type=F32)


def _nn(a, b):
    return lax.dot_general(a, b, _DIMS["nn"], preferred_element_type=F32)


def _log_gates(z):
    lse = jnp.log(1.0 + jnp.exp(-jnp.abs(z)))
    log_beta = jnp.minimum(z, 0.0) - lse
    return log_beta, log_beta - z


DEAD_LOG_WEIGHT = -110.0


def _first_live_tile(start, scores, live_sc):
    def alive():
        return jnp.max(jnp.maximum(live_sc[0], live_sc[1])) > DEAD_LOG_WEIGHT

    def step(c):
        for h, z in enumerate(scores(c[0])):
            live_sc[h] = live_sc[h] + jnp.sum(_log_gates(z)[1], axis=-1, keepdims=True)
        return c[0] - 1, alive()

    j_end, _ = lax.while_loop(lambda c: jnp.logical_and(c[0] >= 0, c[1]), step, (start, alive()))
    return j_end + 1


def _attn_fwd(proj, seq):
    blk = ATT_BLK
    nq = seq // blk
    npair = N_HEADS // 2

    def body(q_ref, k_ref, v_ref, o_ref, z0_sc, z1_sc, w0_sc, w1_sc, tot_sc, live_sc, acc_sc):
        i = pl.program_id(1)
        is_a = lax.broadcasted_iota(jnp.int32, (1, LANES), 1) < HEAD_DIM
        q2 = (q_ref[...] * Q_SCALE).astype(BF16)
        zero = jnp.zeros_like(q2)
        qs = (jnp.where(is_a, q2, zero), jnp.where(is_a, zero, q2))
        row = lax.broadcasted_iota(jnp.int32, (blk, blk), 0)
        col = lax.broadcasted_iota(jnp.int32, (blk, blk), 1)
        tri = (row > col).astype(BF16)
        causal = col < row

        def tile_of(ref, j):
            return ref[pl.ds(pl.multiple_of(j * blk, blk), blk), :].astype(BF16)

        def scores(j):
            k2 = tile_of(k_ref, j)
            return [_nt(qs[h], k2) for h in range(2)]

        has_left = i > 0
        left = jnp.maximum(i - 1, 0)
        g_d = [_log_gates(z) for z in scores(i)]
        g_l = [_log_gates(z) for z in scores(left)]
        keep_d = [jnp.where(causal, g[1], 0.0) for g in g_d]
        keep_l = [jnp.where(has_left, g[1], 0.0) for g in g_l]
        suf_d = [_nn(lk.astype(BF16), tri) for lk in keep_d]
        suf_l = [_nn(lk.astype(BF16), tri) for lk in keep_l]
        v_d, v_l = tile_of(v_ref, i), tile_of(v_ref, left)
        pv = []
        for h in range(2):
            sum_d = jnp.sum(keep_d[h], axis=-1, keepdims=True)
            w_d = jnp.where(causal, jnp.exp(g_d[h][0] + suf_d[h]), 0.0)
            w_l = jnp.where(has_left, jnp.exp(g_l[h][0] + (sum_d + suf_l[h])), 0.0)
            pv.append(_nn(w_d.astype(BF16), v_d) + _nn(w_l.astype(BF16), v_l))
            tot_sc[h] = sum_d + jnp.sum(keep_l[h], axis=-1, keepdims=True)
        acc_sc[...] = jnp.where(is_a, pv[0], pv[1])

        live_sc[...] = tot_sc[...]
        first = _first_live_tile(i - 2, scores, live_sc)
        trips = i - 1 - first
        z_bufs, w_bufs = (z0_sc, z1_sc), (w0_sc, w1_sc)

        def put(ref, vals):
            for h in range(2):
                ref[h] = vals[h]

        def weights(zs):
            gates = [_log_gates(z) for z in zs]
            sums = [_nn(g[1].astype(BF16), tri) for g in gates]
            ws = []
            for h in range(2):
                ws.append(jnp.exp(gates[h][0] + (tot_sc[h] + sums[h])).astype(BF16))
                tot_sc[h] = tot_sc[h] + jnp.sum(gates[h][1], axis=-1, keepdims=True)
            return ws

        def add_values(w_buf, j):
            v2 = tile_of(v_ref, j)
            acc_sc[...] += jnp.where(is_a, _nn(w_buf[0], v2), _nn(w_buf[1], v2))

        def trip(j, s):
            add_values(w_bufs[s], j + 1)
            put(z_bufs[1 - s], scores(jnp.maximum(j - 1, first)))
            put(w_bufs[1 - s], weights((z_bufs[s][0], z_bufs[s][1])))

        @pl.when(trips > 0)
        def _():
            put(z0_sc, scores(i - 2))
            w0_sc[...] = jnp.zeros_like(w0_sc)

            def two_trips(pp, carry):
                j = i - 2 - 2 * pp
                trip(j, 0)
                trip(j - 1, 1)
                return carry

            lax.fori_loop(0, trips // 2, two_trips, 0)
            odd = trips % 2 == 1

            @pl.when(odd)
            def _():
                trip(first, 0)
                add_values(w1_sc, first)

            @pl.when(jnp.logical_not(odd))
            def _():
                add_values(w0_sc, first)

        o_ref[...] = acc_sc[...].astype(BF16)

    return pl.pallas_call(
        body, name="attn_fwd", grid=(npair, nq),
        in_specs=[_bs((blk, LANES), lambda p, i: (i, p)),
                  _bs((seq, LANES), lambda p, i: (0, npair + p)),
                  _bs((seq, LANES), lambda p, i: (0, 2 * npair + p))],
        out_specs=_bs((blk, LANES), lambda p, i: (i, p)),
        out_shape=_sds((seq, ATTN_W), BF16),
        scratch_shapes=[pltpu.VMEM((2, blk, blk), F32), pltpu.VMEM((2, blk, blk), F32),
                        pltpu.VMEM((2, blk, blk), BF16), pltpu.VMEM((2, blk, blk), BF16),
                        pltpu.VMEM((2, blk, 1), F32), pltpu.VMEM((2, blk, 1), F32), pltpu.VMEM((blk, LANES), F32)],
        compiler_params=_cparams(2),
    )(proj, proj, proj)


def _attn_bwd(proj, do, seq):
    blk = ATT_BLK
    nq = seq // blk
    npair = N_HEADS // 2

    def body(q_ref, k_ref, v_ref, do_ref, dq_ref, dk_ref, dv_ref,
             prod0_sc, prod1_sc, pend0_sc, pend1_sc, tot_sc, live_sc, cum_sc, pre_sc, dq_sc):
        i = pl.program_id(1)

        @pl.when(i == 0)
        def _():
            dk_ref[...] = jnp.zeros_like(dk_ref)
            dv_ref[...] = jnp.zeros_like(dv_ref)

        is_a = lax.broadcasted_iota(jnp.int32, (1, LANES), 1) < HEAD_DIM
        q2 = (q_ref[...] * Q_SCALE).astype(BF16)
        do2 = do_ref[...]
        zero = jnp.zeros_like(q2)
        qs = (jnp.where(is_a, q2, zero), jnp.where(is_a, zero, q2))
        dos = (jnp.where(is_a, do2, zero), jnp.where(is_a, zero, do2))
        row = lax.broadcasted_iota(jnp.int32, (blk, blk), 0)
        col = lax.broadcasted_iota(jnp.int32, (blk, blk), 1)
        tri_after = (row > col).astype(BF16)
        tri_excl = (row < col).astype(BF16)
        causal = col < row

        def tile_of(ref, j):
            return ref[pl.ds(pl.multiple_of(j * blk, blk), blk), :].astype(BF16)

        def scores(j):
            k2 = tile_of(k_ref, j)
            return [_nt(qs[h], k2) for h in range(2)]

        def products(j):
            v2 = tile_of(v_ref, j)
            return scores(j) + [_nt(dos[h], v2) for h in range(2)]

        def row_sum(a):
            return jnp.sum(a, axis=-1, keepdims=True)

        def grad_matmuls(ws, dzs, j):
            rows = pl.ds(pl.multiple_of(j * blk, blk), blk)
            k2 = tile_of(k_ref, j)
            dq_sc[...] += jnp.where(is_a, _nn(dzs[0], k2), _nn(dzs[1], k2))
            dk_ref[rows, :] += jnp.where(is_a, _tn(dzs[0], q2), _tn(dzs[1], q2))
            if ws is not None:
                dv_ref[rows, :] += jnp.where(is_a, _tn(ws[0], do2), _tn(ws[1], do2))

        has_left = i > 0
        left = jnp.maximum(i - 1, 0)
        p_d, p_l = products(i), products(left)
        g_d = [_log_gates(z) for z in p_d[:2]]
        g_l = [_log_gates(z) for z in p_l[:2]]
        keep_d = [jnp.where(causal, g[1], 0.0) for g in g_d]
        keep_l = [jnp.where(has_left, g[1], 0.0) for g in g_l]
        suf_d = [_nn(lk.astype(BF16), tri_after) for lk in keep_d]
        suf_l = [_nn(lk.astype(BF16), tri_after) for lk in keep_l]
        w_d, w_l, gg_d, gg_l = [], [], [], []
        for h in range(2):
            sum_d = row_sum(keep_d[h])
            w_d.append(jnp.where(causal, jnp.exp(g_d[h][0] + suf_d[h]), 0.0))
            w_l.append(jnp.where(has_left, jnp.exp(g_l[h][0] + (sum_d + suf_l[h])), 0.0))
            gg_d.append(p_d[2 + h] * w_d[h])
            gg_l.append(p_l[2 + h] * w_l[h])
            tot_sc[h] = sum_d + row_sum(keep_l[h])
        before_d = [_nn(g.astype(BF16), tri_excl) for g in gg_d]
        before_l = [_nn(g.astype(BF16), tri_excl) for g in gg_l]
        dz_d, dz_l = [], []
        for h in range(2):
            beta_d, beta_l = jnp.exp(g_d[h][0]), jnp.exp(g_l[h][0])
            dz = gg_l[h] * (1.0 - beta_l) - before_l[h] * beta_l
            dz_l.append(jnp.where(has_left, dz, 0.0).astype(BF16))
            dz = gg_d[h] * (1.0 - beta_d) - (row_sum(gg_l[h]) + before_d[h]) * beta_d
            dz_d.append(jnp.where(causal, dz, 0.0).astype(BF16))
        dq_sc[...] = jnp.zeros_like(dq_sc)
        grad_matmuls([w.astype(BF16) for w in w_l], dz_l, left)
        grad_matmuls([w.astype(BF16) for w in w_d], dz_d, i)

        live_sc[...] = tot_sc[...]
        first = _first_live_tile(i - 2, scores, live_sc)
        trips = i - 1 - first
        prod_bufs, pend_bufs = (prod0_sc, prod1_sc), (pend0_sc, pend1_sc)

        def local_grads(prods):
            zs, dws = prods[:2], prods[2:]
            gates = [_log_gates(z) for z in zs]
            sums = [_nn(g[1].astype(BF16), tri_after) for g in gates]
            ws, gs = [], []
            for h in range(2):
                cum = cum_sc[h] + row_sum(gates[h][1])
                cum_sc[h] = cum
                ws.append(jnp.exp(gates[h][0] + ((live_sc[h] - cum) + sums[h])))
                gs.append(dws[h] * ws[h])
            befores = [_nn(g.astype(BF16), tri_excl) for g in gs]
            dzs = []
            for h in range(2):
                beta = jnp.exp(gates[h][0])
                dzs.append((gs[h] * (1.0 - beta) - (pre_sc[h] + befores[h]) * beta).astype(BF16))
                pre_sc[h] = pre_sc[h] + row_sum(gs[h])
            return [w.astype(BF16) for w in ws] + dzs

        def put(ref, vals):
            for n, val in enumerate(vals):
                ref[n] = val

        def flush(pend, j):
            grad_matmuls([pend[0], pend[1]], [pend[2], pend[3]], j)

        def trip(j, s):
            flush(pend_bufs[s], jnp.maximum(j - 1, first))
            put(prod_bufs[1 - s], products(j + 1))
            put(pend_bufs[1 - s], local_grads([prod_bufs[s][n] for n in range(4)]))

        def earlier_keys_share(j, mask):
            dzs = []
            for h, z in enumerate(scores(j)):
                beta = jnp.exp(_log_gates(z)[0])
                dzs.append(jnp.where(mask, -pre_sc[h] * beta, 0.0).astype(BF16))
            grad_matmuls(None, dzs, j)

        @pl.when(trips > 0)
        def _():
            cum_sc[...] = jnp.zeros_like(cum_sc)
            pre_sc[...] = jnp.zeros_like(pre_sc)
            pend0_sc[...] = jnp.zeros_like(pend0_sc)
            put(prod0_sc, products(first))

            def two_trips(pp, carry):
                trip(first + 2 * pp, 0)
                trip(first + 2 * pp + 1, 1)
                return carry

            lax.fori_loop(0, trips // 2, two_trips, 0)
            odd = trips % 2 == 1

            @pl.when(odd)
            def _():
                trip(i - 2, 0)
                flush(pend1_sc, i - 2)

            @pl.when(jnp.logical_not(odd))
            def _():
                flush(pend0_sc, i - 2)

            earlier_keys_share(i - 1, True)
            earlier_keys_share(i, causal)

        dq_ref[...] = dq_sc[...] * Q_SCALE

    qmap = lambda p, i: (i, p)
    return pl.pallas_call(
        body, name="attn_bwd", grid=(npair, nq),
        in_specs=[_bs((blk, LANES), qmap),
                  _bs((seq, LANES), lambda p, i: (0, npair + p)),
                  _bs((seq, LANES), lambda p, i: (0, 2 * npair + p)),
                  _bs((blk, LANES), qmap)],
        out_specs=[_bs((blk, LANES), qmap),
                   _bs((seq, LANES), lambda p, i: (0, p)),
                   _bs((seq, LANES), lambda p, i: (0, p))],
        out_shape=[_sds((seq, ATTN_W), F32)] * 3,
        scratch_shapes=[pltpu.VMEM((4, blk, blk), F32), pltpu.VMEM((4, blk, blk), F32),
                        pltpu.VMEM((4, blk, blk), BF16), pltpu.VMEM((4, blk, blk), BF16),
                        pltpu.VMEM((2, blk, 1), F32), pltpu.VMEM((2, blk, 1), F32), pltpu.VMEM((2, blk, 1), F32),
                        pltpu.VMEM((2, blk, 1), F32), pltpu.VMEM((blk, LANES), F32)],
        compiler_params=_cparams(2),
    )(proj, proj, proj, do)


def _elementwise(name, fn, ins, out_dtypes):
    rows, cols = ins[0].shape
    tr = rows
    for cand in (512, 256, 128, 64, 32, 16, 8):
        if rows % cand == 0 and cand * cols * 4 <= 2 * 1024 * 1024:
            tr = cand
            break
    n_in = len(ins)

    def body(*refs):
        res = fn(*[r[...] for r in refs[:n_in]])
        for r, val in zip(refs[n_in:], res):
            r[...] = val.astype(r.dtype)

    spec = _bs((tr, cols), lambda i: (i, 0))
    return pl.pallas_call(
        body, name=name, grid=(rows // tr,),
        in_specs=[spec] * n_in, out_specs=[spec] * len(out_dtypes),
        out_shape=[_sds((rows, cols), dt) for dt in out_dtypes],
        compiler_params=_cparams(1),
    )(*ins)


def _adamw_fn(w, g, m, v):
    m = ADAM_B1 * m + (1.0 - ADAM_B1) * g
    v = ADAM_B2 * v + (1.0 - ADAM_B2) * (g * g)
    m_hat = m / (1.0 - ADAM_B1 ** ADAM_STEP)
    v_hat = v / (1.0 - ADAM_B2 ** ADAM_STEP)
    delta = -ADAM_LR * (m_hat / (jnp.sqrt(v_hat) + ADAM_EPS) + ADAM_WD * w)
    return delta, m, v


def _adamw(name, w, g, m, v):
    shape = w.shape
    as2d = lambda a: a.reshape(-1, shape[-1])
    delta, nm, nv = _elementwise(name, _adamw_fn, [as2d(w), as2d(g), as2d(m), as2d(v)], [F32, F32, F32])
    return delta.reshape(shape), nm.reshape(shape), nv.reshape(shape)


def _place():
    x, y, c = lax.axis_index("x"), lax.axis_index("y"), lax.axis_index("c")
    chips = [(1 - x, y), (x, 1 - y), (1 - x, 1 - y)]
    return x, y, c, chips


ANY = pl.BlockSpec(memory_space=pl.ANY)
VMEM_WHOLE = pl.BlockSpec(memory_space=pltpu.VMEM)


def _allgather_weights(shards):
    n = len(shards)

    def body(*refs):
        src, dst = refs[:n], refs[n:2 * n]
        send_sems, recv_sems, local_sems = refs[2 * n:]
        x, y, c, chips = _place()
        me, sibling, mychip = (x, y, c), (x, y, 1 - c), 2 * x + y

        def piece(w, chip, half):
            hr = src[w].shape[0] // 2
            return dst[w].at[chip, pl.ds(half * hr, hr)]

        def copy(w, k, src_ref, dst_ref, to):
            return pltpu.make_async_remote_copy(src_ref=src_ref, dst_ref=dst_ref, send_sem=send_sems.at[w, k],
                                                recv_sem=recv_sems.at[w, k], device_id=to, device_id_type=MESH)

        started, local = [], []
        for w in range(n):
            hr = src[w].shape[0] // 2
            own = pltpu.make_async_copy(src[w], dst[w].at[mychip], local_sems.at[w])
            own.start()
            local.append(own)
            for r, (cx, cy) in enumerate(chips):
                cp = copy(w, r, src[w].at[pl.ds(c * hr, hr)], piece(w, mychip, c), (cx, cy, c))
                cp.start()
                started.append(cp)
        for w in range(n):
            for r, (cx, cy) in enumerate(chips):
                landed = piece(w, 2 * cx + cy, c)
                copy(w, r, landed, landed, me).wait_recv()
                fwd = copy(w, 3 + r, landed, landed, sibling)
                fwd.start()
                started.append(fwd)
        for w in range(n):
            for r, (cx, cy) in enumerate(chips):
                from_sib = piece(w, 2 * cx + cy, 1 - c)
                copy(w, 3 + r, from_sib, from_sib, me).wait_recv()
        for cp in local:
            cp.wait()
        for cp in started:
            cp.wait_send()

    return pl.pallas_call(
        body, name="allgather_weights",
        in_specs=[VMEM_WHOLE] * n, out_specs=[VMEM_WHOLE] * n,
        out_shape=[_sds((N_CHIPS,) + s.shape, s.dtype) for s in shards],
        scratch_shapes=[pltpu.SemaphoreType.DMA((n, 6)), pltpu.SemaphoreType.DMA((n, 6)),
                        pltpu.SemaphoreType.DMA((n,))],
        compiler_params=pltpu.CompilerParams(vmem_limit_bytes=VMEM_LIMIT),
    )(*shards)


SUM_ROWS = 64


def _rs_pair_sum(name, grads):
    n = len(grads)

    def body(*refs):
        g, out = refs[:n], refs[n:2 * n]
        stage, land, keep = refs[2 * n:3 * n], refs[3 * n:4 * n], refs[4 * n:5 * n]
        send_sems, recv_sems, stage_sems, keep_sems = refs[5 * n:]
        x, y, c, _ = _place()
        sibling = (x, y, 1 - c)
        loads = []
        for w in range(n):
            hr = g[w].shape[1] // 2
            st = pltpu.make_async_copy(g[w].at[:, pl.ds((1 - c) * hr, hr)], stage[w], stage_sems.at[w])
            kp = pltpu.make_async_copy(g[w].at[:, pl.ds(c * hr, hr)], keep[w], keep_sems.at[w])
            st.start()
            kp.start()
            loads.append((st, kp))
        gives = []
        for w in range(n):
            loads[w][0].wait()
            give = pltpu.make_async_remote_copy(src_ref=stage[w], dst_ref=land[w], send_sem=send_sems.at[w],
                                                recv_sem=recv_sems.at[w], device_id=sibling, device_id_type=MESH)
            give.start()
            gives.append(give)
        for w in range(n):
            loads[w][1].wait()
            gives[w].wait_recv()
            nb = g[w].shape[1] // 2 // SUM_ROWS

            def add(idx, carry, w=w, nb=nb):
                k, r = idx // nb, pl.multiple_of((idx % nb) * SUM_ROWS, SUM_ROWS)
                rows = pl.ds(r, SUM_ROWS)
                out[w][k, rows, :] = (keep[w][k, rows, :] + land[w][k, rows, :]).astype(BF16)
                return carry

            lax.fori_loop(0, N_CHIPS * nb, add, 0)
        for give in gives:
            give.wait_send()

    half = [(N_CHIPS, a.shape[1] // 2, a.shape[2]) for a in grads]
    bufs = [pltpu.VMEM(s, F32) for s in half]
    sems = pltpu.SemaphoreType.DMA((n,))
    return pl.pallas_call(
        body, name=name,
        in_specs=[ANY] * n, out_specs=[VMEM_WHOLE] * n, out_shape=[_sds(s, BF16) for s in half],
        scratch_shapes=bufs + bufs + bufs + [sems, sems, sems, sems],
        compiler_params=pltpu.CompilerParams(vmem_limit_bytes=VMEM_LIMIT),
    )(*grads)


def _rs_exchange_join(parts):
    n = len(parts)

    def body(*refs):
        t, full, got = refs[:n], refs[n:2 * n], refs[2 * n:3 * n]
        send_sems, recv_sems = refs[3 * n:]
        x, y, c, chips = _place()
        mychip, sibling = 2 * x + y, (x, y, 1 - c)
        sends = []
        for w in range(n):
            for r, (cx, cy) in enumerate(chips):
                cp = pltpu.make_async_remote_copy(src_ref=t[w].at[2 * cx + cy], dst_ref=got[w].at[r],
                                                  send_sem=send_sems.at[w, r], recv_sem=recv_sems.at[w, r],
                                                  device_id=(cx, cy, c), device_id_type=MESH)
                cp.start()
                sends.append(cp)
        for w in range(n):
            hr = t[w].shape[1]
            for r in range(3):
                pltpu.make_async_remote_copy(src_ref=got[w].at[r], dst_ref=got[w].at[r], send_sem=send_sems.at[w, r],
                                             recv_sem=recv_sems.at[w, r], device_id=sibling,
                                             device_id_type=MESH).wait_recv()

            def add(idx, carry, w=w, hr=hr):
                r = pl.multiple_of(idx * SUM_ROWS, SUM_ROWS)
                rows = pl.ds(r, SUM_ROWS)
                f = lambda v: v.astype(F32)
                total = ((f(t[w][mychip, rows, :]) + f(got[w][0, rows, :])) + f(got[w][1, rows, :])) \
                    + f(got[w][2, rows, :])
                full[w][pl.ds(pl.multiple_of(c * hr + r, SUM_ROWS), SUM_ROWS), :] = total
                return carry

            lax.fori_loop(0, hr // SUM_ROWS, add, 0)
            mine = full[w].at[pl.ds(c * hr, hr)]
            give = pltpu.make_async_remote_copy(src_ref=mine, dst_ref=mine, send_sem=send_sems.at[w, 3],
                                                recv_sem=recv_sems.at[w, 3], device_id=sibling, device_id_type=MESH)
            give.start()
            sends.append(give)
        for w in range(n):
            hr = t[w].shape[1]
            theirs = full[w].at[pl.ds((1 - c) * hr, hr)]
            pltpu.make_async_remote_copy(src_ref=theirs, dst_ref=theirs, send_sem=send_sems.at[w, 3],
                                         recv_sem=recv_sems.at[w, 3], device_id=sibling, device_id_type=MESH).wait_recv()
        for cp in sends:
            cp.wait_send()

    return pl.pallas_call(
        body, name="rs_exchange_join",
        in_specs=[VMEM_WHOLE] * n, out_specs=[VMEM_WHOLE] * n,
        out_shape=[_sds((2 * a.shape[1], a.shape[2]), F32) for a in parts],
        scratch_shapes=[pltpu.VMEM((3,) + a.shape[1:], a.dtype) for a in parts]
        + [pltpu.SemaphoreType.DMA((n, 4)), pltpu.SemaphoreType.DMA((n, 4))],
        compiler_params=pltpu.CompilerParams(vmem_limit_bytes=VMEM_LIMIT),
    )(*parts)


def _small_allreduce(loss_p, dg_parts, dbg_a, dbg_c, dwc):
    ins = [loss_p] + list(dg_parts) + [dbg_a, dbg_c, dwc]
    n_in = len(ins)
    vmem = pl.BlockSpec(memory_space=pltpu.VMEM)

    def body(*refs):
        in_refs = refs[:n_in]
        out_ref, vec, buf, send_sems, recv_sems = refs[n_in:]
        x, y, c, _ = _place()
        me = 4 * x + 2 * y + c
        vec[...] = jnp.zeros_like(vec)
        vec[0:1, :] = jnp.sum(in_refs[0][...], axis=0)
        for r in range(5):
            vec[1 + r:2 + r, :] = jnp.sum(in_refs[1 + r][...], axis=0)
        vec[6:7, :] = jnp.sum(in_refs[6][...], axis=0)
        vec[7:8, :] = jnp.sum(in_refs[7][...], axis=0)
        vec[8:16, 0:CONV_W] = jnp.sum(in_refs[8][...], axis=0)
        buf[pl.ds(me, 1)] = vec[...][None]
        copies = []
        for r in range(1, 8):
            fx, fy, fc = (r >> 2) & 1, (r >> 1) & 1, r & 1
            to = (1 - x if fx else x, 1 - y if fy else y, 1 - c if fc else c)
            cp = pltpu.make_async_remote_copy(src_ref=vec, dst_ref=buf.at[me], send_sem=send_sems.at[r - 1],
                                              recv_sem=recv_sems.at[r - 1], device_id=to, device_id_type=MESH)
            cp.start()
            copies.append(cp)
        for cp in copies:
            cp.wait()
        total = buf[0]
        for s in range(1, 8):
            total = total + buf[s]
        out_ref[...] = total
        out_ref[0:1, :] = jnp.broadcast_to(jnp.sum(total[0:1, :], axis=-1, keepdims=True), (1, D_MODEL))

    return pl.pallas_call(
        body, name="small_allreduce",
        in_specs=[vmem] * n_in, out_specs=vmem, out_shape=_sds((SMALL_ROWS, D_MODEL), F32),
        scratch_shapes=[pltpu.VMEM((SMALL_ROWS, D_MODEL), F32), pltpu.VMEM((8, SMALL_ROWS, D_MODEL), F32),
                        pltpu.SemaphoreType.DMA((7,)), pltpu.SemaphoreType.DMA((7,))],
    )(*ins)


def _local_step(x, p, tgt, g, b_gate, w_conv, wf):
    seq = x.shape[0]
    tm = min(seq, 1024)
    th = min(seq, 512)
    tl = min(seq, 2048)
    ni, nh, nl = seq // tm, seq // th, seq // tl
    g_pre_mix, g_post_mix, g_pre_mlp, g_post_mlp, g_ple = g
    w_in, w_ao, w_co, w_o, w_up, w_down, w_pg, w_pp, w_in_nat, w_up_nat = wf
    D = D_MODEL
    vec = lambda a, blk=0: (a, _bs((1, D), lambda i, j, k: (0, blk)))
    rows_i = lambda a, t, blk=0: (a, _bs((t, D), lambda i, j, k: (i, blk)))
    rows_k = lambda a, t, blk=0: (a, _bs((t, D), lambda i, j, k: (k, blk)))
    part = lambda n: (_sds((n, 1, D), F32), _bs((None, 1, D), lambda i, j, k: (i, 0, 0)))
    full2 = lambda a: (a, _bs(a.shape, lambda i, j, k: (0, 0)))

    normed = lambda xb, gb: (_rms(xb, gb).astype(BF16),) * 2
    keep_a = lambda t: [(_sds((seq, D), BF16), _bs((t, D), lambda i, j, k: (i, 0)))]
    proj, h1 = _mm("proj_in", "nn", (ni, 4, 1),
                   a_ins=[rows_i(x, tm), vec(g_pre_mix)], a_fn=normed,
                   b_ins=[(w_in, _bs((None, D, 1280), lambda i, j, k: (j, 0, 0)))], b_fn=_ident,
                   outs=[(_sds((seq, D_IN), F32), _bs((tm, 1280), lambda i, j, k: (i, j)))],
                   acc_shape=(tm, 1280), a_cache=((tm, D), BF16), a_outs=keep_a(tm))
    o = _attn_fwd(proj, seq)
    (y_attn,) = _mm("attn_out", "nn", (ni, 1, 1),
                    a_ins=[(o, _bs((tm, ATTN_W), lambda i, j, k: (i, 0)))], a_fn=_ident,
                    b_ins=[full2(w_ao)], b_fn=_ident,
                    outs=[(_sds((seq, D), BF16), _bs((tm, D), lambda i, j, k: (i, 0)))], acc_shape=(tm, D))
    e, d = _conv_fwd(proj, w_conv, seq, tm)
    (y_conv,) = _mm("conv_out", "nn", (ni, 1, 1),
                    a_ins=[(e, _bs((tm, CONV_W), lambda i, j, k: (i, 0)))], a_fn=_ident,
                    b_ins=[full2(w_co)], b_fn=_ident,
                    outs=[(_sds((seq, D), BF16), _bs((tm, D), lambda i, j, k: (i, 0)))], acc_shape=(tm, D))

    def mix_fn(ga, gc, ya, yc, ba, bc):
        return ((_sig(ga + ba) * ya.astype(F32) + _sig(gc + bc) * yc.astype(F32)).astype(BF16),) * 2

    def post_mix(acc, xb, gb):
        return acc, xb + _rms(acc, gb)

    mix_ins = lambda rows: [rows(proj, th, 3), rows(proj, th, 4), rows(y_attn, th), rows(y_conv, th),
                            vec(b_gate, 0), vec(b_gate, 1)]
    mixed, x1, mixin = _mm("mix_out", "nn", (nh, 1, 1),
                           a_ins=mix_ins(rows_i), a_fn=mix_fn, b_ins=[full2(w_o)], b_fn=_ident,
                           epi_ins=[rows_i(x, th), vec(g_post_mix)], epi_fn=post_mix,
                           outs=[(_sds((seq, D), F32), _bs((th, D), lambda i, j, k: (i, 0)))] * 2,
                           acc_shape=(th, D), a_cache=((th, D), BF16), a_outs=keep_a(th))
    up, h2 = _mm("mlp_up", "nn", (nh, 1, 1),
                 a_ins=[rows_i(x1, th), vec(g_pre_mlp)], a_fn=normed,
                 b_ins=[full2(w_up_nat)], b_fn=_ident,
                 outs=[(_sds((seq, D_FF), BF16), _bs((th, D_FF), lambda i, j, k: (i, 0)))],
                 acc_shape=(th, D_FF), a_cache=((th, D), BF16), a_outs=keep_a(th))

    def relu2(ub):
        r = jnp.maximum(ub.astype(F32), 0.0)
        return (r * r).astype(BF16)

    f, x2 = _mm("mlp_down", "nn", (nh, 1, 1),
                a_ins=[(up, _bs((th, D_FF), lambda i, j, k: (i, 0)))], a_fn=relu2,
                b_ins=[full2(w_down)], b_fn=_ident,
                epi_ins=[rows_i(x1, th), vec(g_post_mlp)], epi_fn=post_mix,
                outs=[(_sds((seq, D), F32), _bs((th, D), lambda i, j, k: (i, 0)))] * 2, acc_shape=(th, D))
    (pp,) = _mm("ple_proj", "nn", (ni, 1, 1),
                a_ins=[(p, _bs((tm, PLE_DIM), lambda i, j, k: (i, 0)))], a_fn=_to_bf16,
                b_ins=[full2(w_pp)], b_fn=_ident,
                outs=[(_sds((seq, D), F32), _bs((tm, D), lambda i, j, k: (i, 0)))], acc_shape=(tm, D))

    def head(acc, x2b, ppb, tb):
        pg = _sig(acc)
        err = x2b + pg * ppb - tb
        return pg, err * (1.0 / D), jnp.sum(err * err, axis=0, keepdims=True) * (0.5 / D)

    pg, dx3, loss_p, h3 = _mm("ple_gate_loss", "nn", (nh, 1, 1),
                              a_ins=[rows_i(x2, th), vec(g_ple)], a_fn=normed,
                              b_ins=[full2(w_pg)], b_fn=_ident,
                              epi_a=(0,), epi_ins=[rows_i(pp, th), rows_i(tgt, th)], epi_fn=head,
                              outs=[(_sds((seq, D), BF16), _bs((th, D), lambda i, j, k: (i, 0))),
                                    (_sds((seq, D), F32), _bs((th, D), lambda i, j, k: (i, 0))), part(nh)],
                              acc_shape=(th, D), a_cache=((th, D), BF16), a_outs=keep_a(th))

    (dw_pp,) = _mm("dw_ple_proj", "tn", (1, 1, nh),
                   a_ins=[(p, _bs((th, PLE_DIM), lambda i, j, k: (k, 0)))], a_fn=_to_bf16,
                   b_ins=[rows_k(dx3, th), rows_k(pg, th)], b_fn=lambda a, b: (a * b.astype(F32)).astype(BF16),
                   outs=[(_sds((PLE_DIM, D), F32), _bs((PLE_DIM, D), lambda i, j, k: (0, 0)))],
                   acc_shape=(PLE_DIM, D))

    def dpre_fn(dx3b, ppb, pgb):
        pgf = pgb.astype(F32)
        return (dx3b * ppb * pgf * (1.0 - pgf)).astype(BF16)

    def ple_norm_bwd(acc, dx3b, x2b, gb):
        dxn, dg = _rms_bwd(x2b, gb, acc)
        return dx3b + dxn, dg

    dx2, dg_ple_p, dpre = _mm("d_ple_gate", "nt", (nh, 1, 1),
                              a_ins=[rows_i(dx3, th), rows_i(pp, th), rows_i(pg, th)],
                              a_fn=lambda a, b, c: (dpre_fn(a, b, c),) * 2,
                              b_ins=[full2(w_pg)], b_fn=_ident,
                              epi_a=(0,), epi_ins=[rows_i(x2, th), vec(g_ple)], epi_fn=ple_norm_bwd,
                              outs=[(_sds((seq, D), F32), _bs((th, D), lambda i, j, k: (i, 0))), part(nh)],
                              acc_shape=(th, D), a_cache=((th, D), BF16),
                              a_outs=[(_sds((seq, D), BF16), _bs((th, D), lambda i, j, k: (i, 0)))])
    (dw_pg,) = _mm("dw_ple_gate", "tn", (1, 1, ni),
                   a_ins=[rows_k(h3, tm)], a_fn=_ident, b_ins=[rows_k(dpre, tm)], b_fn=_ident,
                   outs=[(_sds((D, D), F32), _bs((D, D), lambda i, j, k: (0, 0)))], acc_shape=(D, D))

    def df_fn(fb, dx2b, gb):
        dfb, dg = _rms_bwd(fb, gb, dx2b)
        dfb = dfb.astype(BF16)
        return dfb, dfb, dg

    def dup_fn(acc, ub):
        return (acc * (2.0 * jnp.maximum(ub.astype(F32), 0.0)),)

    dup, df, dg_post_mlp_p = _mm("d_mlp_down", "nt", (ni, 4, 1),
                                 a_ins=[rows_i(f, tm), rows_i(dx2, tm), vec(g_post_mlp)], a_fn=df_fn,
                                 b_ins=[(w_down, _bs((D, D), lambda i, j, k: (j, 0)))], b_fn=_ident,
                                 epi_ins=[(up, _bs((tm, D), lambda i, j, k: (i, j)))], epi_fn=dup_fn,
                                 outs=[(_sds((seq, D_FF), BF16), _bs((tm, D), lambda i, j, k: (i, j)))],
                                 acc_shape=(tm, D), a_cache=((tm, D), BF16),
                                 a_outs=[(_sds((seq, D), BF16), _bs((tm, D), lambda i, j, k: (i, 0))), part(ni)])
    (dw_down,) = _mm("dw_mlp_down", "tn", (4, 1, nl),
                     a_ins=[(up, _bs((tl, D), lambda i, j, k: (k, i)))], a_fn=relu2,
                     b_ins=[rows_k(df, tl)], b_fn=_ident,
                     outs=[(_sds((D_FF, D), F32), _bs((D, D), lambda i, j, k: (i, 0)))], acc_shape=(D, D))
    (dw_up,) = _mm("dw_mlp_up", "tn", (1, 4, nl),
                   a_ins=[rows_k(h2, tl)], a_fn=_ident,
                   b_ins=[(dup, _bs((tl, D), lambda i, j, k: (k, j)))], b_fn=_ident,
                   outs=[(_sds((N_CHIPS, D, D), F32), _bs((None, D, D), lambda i, j, k: (j, 0, 0)))],
                   acc_shape=(D, D))

    def mlp_norm_bwd(acc, x1b, dx2b, mixedb, g_mlp, g_mix):
        dxn, dg_mlp = _rms_bwd(x1b, g_mlp, acc)
        dx1b = dx2b + dxn
        dmixedb, dg_mix = _rms_bwd(mixedb, g_mix, dx1b)
        return dx1b, dmixedb, dg_mlp, dg_mix

    dx1, dmixed, dg_pre_mlp_p, dg_post_mix_p = _mm(
        "d_mlp_up", "nt", (nh, 1, 1),
        a_ins=[(dup, _bs((th, D_FF), lambda i, j, k: (i, 0)))], a_fn=_ident,
        b_ins=[full2(w_up_nat)], b_fn=_ident,
        epi_ins=[rows_i(x1, th), rows_i(dx2, th), rows_i(mixed, th), vec(g_pre_mlp), vec(g_post_mix)],
        epi_fn=mlp_norm_bwd,
        outs=[(_sds((seq, D), F32), _bs((th, D), lambda i, j, k: (i, 0))),
              (_sds((seq, D), BF16), _bs((th, D), lambda i, j, k: (i, 0))), part(nh), part(nh)],
        acc_shape=(th, D))
    (dw_o,) = _mm("dw_mix_out", "tn", (1, 1, ni),
                  a_ins=[rows_k(mixin, tm)], a_fn=_ident, b_ins=[rows_k(dmixed, tm)], b_fn=_ident,
                  outs=[(_sds((D, D), F32), _bs((D, D), lambda i, j, k: (0, 0)))], acc_shape=(D, D))

    def gate_bwd(acc, ga, gc, ya, yc, ba, bc):
        sa, sc = _sig(ga + ba), _sig(gc + bc)
        dga = acc * ya.astype(F32) * sa * (1.0 - sa)
        dgc = acc * yc.astype(F32) * sc * (1.0 - sc)
        return (acc * sa, acc * sc, jnp.concatenate([dga, dgc], axis=1),
                jnp.sum(dga, axis=0, keepdims=True), jnp.sum(dgc, axis=0, keepdims=True))

    dya, dyc, dgate, dbg_a_p, dbg_c_p = _mm(
        "d_mix_out", "nt", (nh, 1, 1),
        a_ins=[rows_i(dmixed, th)], a_fn=_ident, b_ins=[full2(w_o)], b_fn=_ident,
        epi_ins=mix_ins(rows_i), epi_fn=gate_bwd,
        outs=[(_sds((seq, D), BF16), _bs((th, D), lambda i, j, k: (i, 0)))] * 2
             + [(_sds((seq, 2 * D), BF16), _bs((th, 2 * D), lambda i, j, k: (i, 0))), part(nh), part(nh)],
        acc_shape=(th, D))
    (dw_ao,) = _mm("dw_attn_out", "tn", (1, 1, nh),
                   a_ins=[(o, _bs((th, ATTN_W), lambda i, j, k: (k, 0)))], a_fn=_ident,
                   b_ins=[rows_k(dya, th)], b_fn=_ident,
                   outs=[(_sds((ATTN_W, D), F32), _bs((ATTN_W, D), lambda i, j, k: (0, 0)))], acc_shape=(ATTN_W, D))
    (do,) = _mm("d_attn_out", "nt", (ni, 1, 1),
                a_ins=[rows_i(dya, tm)], a_fn=_ident, b_ins=[full2(w_ao)], b_fn=_ident,
                outs=[(_sds((seq, ATTN_W), BF16), _bs((tm, ATTN_W), lambda i, j, k: (i, 0)))],
                acc_shape=(tm, ATTN_W))
    dq, dk, dv = _attn_bwd(proj, do, seq)
    (dw_co,) = _mm("dw_conv_out", "tn", (1, 1, nh),
                   a_ins=[(e, _bs((th, CONV_W), lambda i, j, k: (k, 0)))], a_fn=_ident,
                   b_ins=[rows_k(dyc, th)], b_fn=_ident,
                   outs=[(_sds((CONV_W, D), F32), _bs((CONV_W, D), lambda i, j, k: (0, 0)))], acc_shape=(CONV_W, D))
    (de,) = _mm("d_conv_out", "nt", (ni, 1, 1),
                a_ins=[rows_i(dyc, tm)], a_fn=_ident, b_ins=[full2(w_co)], b_fn=_ident,
                outs=[(_sds((seq, CONV_W), F32), _bs((tm, CONV_W), lambda i, j, k: (i, 0)))],
                acc_shape=(tm, CONV_W))
    dconv, dwc_p = _conv_bwd(proj, de, d, w_conv, seq, tm)
    dproj = jnp.concatenate([dq.astype(BF16), dk.astype(BF16), dv.astype(BF16), dconv, dgate], axis=1)
    (dw_in,) = _mm("dw_proj_in", "tn", (1, 4, nl),
                   a_ins=[rows_k(h1, tl)], a_fn=_ident,
                   b_ins=[(dproj, _bs((tl, 1280), lambda i, j, k: (k, j)))], b_fn=_ident,
                   outs=[(_sds((N_CHIPS, D, 1280), F32), _bs((None, D, 1280), lambda i, j, k: (j, 0, 0)))],
                   acc_shape=(D, 1280))

    def in_norm_bwd(acc, xb, dx1b, gb):
        dxn, dg = _rms_bwd(xb, gb, acc)
        return dx1b + dxn, dg

    grad_x, dg_pre_mix_p = _mm("d_proj_in", "nt", (nh, 1, 1),
                               a_ins=[(dproj, _bs((th, D_IN), lambda i, j, k: (i, 0)))], a_fn=_ident,
                               b_ins=[full2(w_in_nat)], b_fn=_ident,
                               epi_ins=[rows_i(x, th), rows_i(dx1, th), vec(g_pre_mix)], epi_fn=in_norm_bwd,
                               outs=[(_sds((seq, D), F32), _bs((th, D), lambda i, j, k: (i, 0))), part(nh)],
                               acc_shape=(th, D))

    chip_major = lambda a: a.reshape(a.shape[0], N_CHIPS, a.shape[1] // N_CHIPS).transpose(1, 0, 2)
    big = [dw_in, chip_major(dw_ao), chip_major(dw_co), dw_o.reshape(N_CHIPS, D // N_CHIPS, D), dw_up,
           dw_down.reshape(N_CHIPS, D_FF // N_CHIPS, D), dw_pg.reshape(N_CHIPS, D // N_CHIPS, D), chip_major(dw_pp)]
    small = (loss_p, [dg_pre_mix_p, dg_post_mix_p, dg_pre_mlp_p, dg_post_mlp_p, dg_ple_p], dbg_a_p, dbg_c_p, dwc_p)
    return grad_x, big, small


RS_GROUPS = ((0,), (4,), (5,), (1, 2, 3, 6, 7))


def _reduce_scatter(big):
    pair = [None] * len(big)
    for gi, group in enumerate(RS_GROUPS):
        for w, s in zip(group, _rs_pair_sum(f"rs_pair_sum_{gi}", [big[w] for w in group])):
            pair[w] = s
    return _rs_exchange_join(pair)


def kernel(x, p, g_pre_mix, w_in, b_gate, w_conv, w_attn_out, w_conv_out, w_o, g_post_mix, g_pre_mlp, w_up, w_down, g_post_mlp, g_ple, w_ple_gate, w_ple_proj, loss_target, m_g_pre_mix, m_w_in, m_b_gate, m_w_conv, m_w_attn_out, m_w_conv_out, m_w_o, m_g_post_mix, m_g_pre_mlp, m_w_up, m_w_down, m_g_post_mlp, m_g_ple, m_w_ple_gate, m_w_ple_proj, v_g_pre_mix, v_w_in, v_b_gate, v_w_conv, v_w_attn_out, v_w_conv_out, v_w_o, v_g_post_mix, v_g_pre_mlp, v_w_up, v_w_down, v_g_post_mlp, v_g_ple, v_w_ple_gate, v_w_ple_proj):
    mats = [w_in, w_attn_out, w_conv_out, w_o, w_up, w_down, w_ple_gate, w_ple_proj]
    mats_m = [m_w_in, m_w_attn_out, m_w_conv_out, m_w_o, m_w_up, m_w_down, m_w_ple_gate, m_w_ple_proj]
    mats_v = [v_w_in, v_w_attn_out, v_w_conv_out, v_w_o, v_w_up, v_w_down, v_w_ple_gate, v_w_ple_proj]
    gains = [g_pre_mix, g_post_mix, g_pre_mlp, g_post_mlp, g_ple]
    gains_m = [m_g_pre_mix, m_g_post_mix, m_g_pre_mlp, m_g_post_mlp, m_g_ple]
    gains_v = [v_g_pre_mix, v_g_post_mix, v_g_pre_mlp, v_g_post_mlp, v_g_ple]

    taps = jnp.concatenate([w_conv[0], jnp.zeros((CONV_PAD_ROWS - 3, LANES), F32)], axis=0)
    gathered = _allgather_weights([w[0].astype(BF16) for w in mats] + [taps])
    cols_joined = lambda a: a.transpose(1, 0, 2).reshape(a.shape[1], N_CHIPS * a.shape[2])
    rows_joined = lambda a: a.reshape(N_CHIPS * a.shape[1], a.shape[2])
    wf = [gathered[0], cols_joined(gathered[1]), cols_joined(gathered[2]), rows_joined(gathered[3]), gathered[4],
          rows_joined(gathered[5]), rows_joined(gathered[6]), cols_joined(gathered[7]),
          cols_joined(gathered[0]), cols_joined(gathered[4])]
    w_conv_full = cols_joined(gathered[8])[0:3, :]
    chip = 2 * lax.axis_index("x") + lax.axis_index("y")

    grad_x, big, small = _local_step(x[0], p[0, 0], loss_target[0], gains, b_gate, w_conv_full, wf)

    shard_grads = _reduce_scatter(big)
    red = _small_allreduce(*small)
    loss = red[0, 0]
    grad_gains = [red[1 + r:2 + r, :] for r in range(5)]
    grad_b_gate = jnp.concatenate([red[6:7, :], red[7:8, :]], axis=1)
    grad_w_conv = lax.dynamic_slice(red[8:11, :], (0, chip * LANES), (3, LANES))[None]

    grads_big = [gr.reshape(w.shape) for gr, w in zip(shard_grads, mats)]
    upd_big = [_adamw(f"adamw_{i}", w, gr, m, v) for i, (w, gr, m, v) in enumerate(zip(mats, grads_big, mats_m, mats_v))]
    pack = lambda vs, bg: jnp.concatenate(list(vs) + [bg.reshape(2, D_MODEL), jnp.zeros((1, D_MODEL), F32)], axis=0)
    upd_small = _adamw("adamw_small", pack(gains, b_gate), pack(grad_gains, grad_b_gate),
                       pack(gains_m, m_b_gate), pack(gains_v, v_b_gate))
    upd_conv = _adamw("adamw_conv", w_conv, grad_w_conv, m_w_conv, v_w_conv)

    def small_out(a, which):
        gains_out = [a[r:r + 1, :] for r in range(5)]
        return gains_out, a[5:7, :].reshape(1, 2 * D_MODEL)

    def ordered(g_pre_mix_, big_, b_gate_, conv_, g_rest):
        return [g_pre_mix_, big_[0], b_gate_, conv_, big_[1], big_[2], big_[3], g_rest[0], g_rest[1], big_[4], big_[5],
                g_rest[2], g_rest[3], big_[6], big_[7]]

    outs = [loss, grad_x[None]]
    outs += ordered(grad_gains[0], grads_big, grad_b_gate, grad_w_conv, grad_gains[1:])
    for which in range(3):
        g_out, b_out = small_out(upd_small[which], which)
        outs += ordered(g_out[0], [u[which] for u in upd_big], b_out, upd_conv[which], g_out[1:])
    return tuple(outs)
```

```python
import functools

import jax
import jax.numpy as jnp
from jax import lax
from jax.experimental import pallas as pl
from jax.experimental.pallas import tpu as pltpu

F32 = jnp.float32
BF16 = jnp.bfloat16
MESH = pl.DeviceIdType.MESH

D_MODEL = 1024
N_HEADS = 8
HEAD_DIM = 64
ATTN_W = N_HEADS * HEAD_DIM
CONV_W = 512
D_FF = 4096
PLE_DIM = 256
D_IN = 5120
N_CHIPS = 4
EPS = 1e-6
Q_SCALE = HEAD_DIM ** -0.5

ADAM_LR = 0.001
ADAM_B1 = 0.9
ADAM_B2 = 0.999
ADAM_EPS = 1e-08
ADAM_WD = 0.01
ADAM_STEP = 10

V7X_VMEM_BYTES = 64 * 1024 * 1024
VMEM_LIMIT = V7X_VMEM_BYTES - 8 * 1024 * 1024
LANES = 128
ATT_BLK = 256
SMALL_ROWS = 16
CONV_PAD_ROWS = 16


def _cparams(n_grid):
    return pltpu.CompilerParams(dimension_semantics=("arbitrary",) * n_grid, vmem_limit_bytes=VMEM_LIMIT)


def _bs(shape, fn):
    return pl.BlockSpec(shape, fn)


def _rms_stats(xf):
    return lax.rsqrt(jnp.mean(xf * xf, axis=-1, keepdims=True) + EPS)


def _rms(xf, g):
    return xf * _rms_stats(xf) * g


def _rms_bwd(xf, g, dy):
    r = _rms_stats(xf)
    xh = xf * r
    dyg = dy * g
    dx = r * (dyg - xh * jnp.mean(dyg * xh, axis=-1, keepdims=True))
    return dx, jnp.sum(dy * xh, axis=0, keepdims=True)


def _sig(z):
    return 1.0 / (1.0 + jnp.exp(-z))


def _ident(a):
    return a


def _to_bf16(a):
    return a.astype(BF16)


_DIMS = {"nn": (((1,), (0,)), ((), ())), "nt": (((1,), (1,)), ((), ())), "tn": (((0,), (0,)), ((), ()))}


def _mm(name, mode, grid, a_ins, a_fn, b_ins, b_fn, outs, acc_shape, epi_ins=(), epi_fn=None,
        a_cache=None, a_outs=(), epi_a=()):
    nk = grid[2]
    na, nb, ne, no, nao = len(a_ins), len(b_ins), len(epi_ins), len(outs), len(a_outs)
    assert a_cache is None or nk == 1
    assert not a_outs or a_cache is not None
    dims = _DIMS[mode]
    if epi_fn is None:
        epi_fn = lambda acc: (acc,)

    def body(*refs):
        a_refs = refs[:na]
        b_refs = refs[na:na + nb]
        e_refs = refs[na + nb:na + nb + ne]
        o_refs = refs[na + nb + ne:na + nb + ne + no]
        ao_refs = refs[na + nb + ne + no:na + nb + ne + no + nao]
        scratch = list(refs[na + nb + ne + no + nao:])
        acc_ref = scratch.pop(0) if nk > 1 else None
        a_sc = scratch.pop(0) if a_cache is not None else None
        j = pl.program_id(1)
        k = pl.program_id(2)

        def finish(acc):
            res = epi_fn(acc, *[a_refs[t][...] for t in epi_a], *[r[...] for r in e_refs])
            for r, val in zip(o_refs, res):
                r[...] = val.astype(r.dtype)

        if a_sc is not None:
            @pl.when(j == 0)
            def _():
                res = a_fn(*[r[...] for r in a_refs])
                if nao:
                    for r, val in zip(ao_refs, res[1:]):
                        r[...] = val.astype(r.dtype)
                    res = res[0]
                a_sc[...] = res
            a = a_sc[...]
        else:
            a = a_fn(*[r[...] for r in a_refs])
        b = b_fn(*[r[...] for r in b_refs])
        prod = lax.dot_general(a, b, dims, preferred_element_type=F32)
        if nk == 1:
            finish(prod)
        else:
            @pl.when(k == 0)
            def _():
                acc_ref[...] = prod

            @pl.when(k > 0)
            def _():
                acc_ref[...] += prod

            @pl.when(k == nk - 1)
            def _():
                finish(acc_ref[...])

    scratch_shapes = []
    if nk > 1:
        scratch_shapes.append(pltpu.VMEM(acc_shape, F32))
    if a_cache is not None:
        scratch_shapes.append(pltpu.VMEM(*a_cache))
    all_outs = list(outs) + list(a_outs)
    res = pl.pallas_call(
        body, name=name, grid=grid,
        in_specs=[s for _, s in a_ins] + [s for _, s in b_ins] + [s for _, s in epi_ins],
        out_specs=[s for _, s in all_outs],
        out_shape=[o for o, _ in all_outs],
        scratch_shapes=scratch_shapes,
        compiler_params=_cparams(3),
    )(*[a for a, _ in a_ins], *[a for a, _ in b_ins], *[a for a, _ in epi_ins])
    return res


def _sds(shape, dtype):
    return jax.ShapeDtypeStruct(shape, dtype)


def _shift_rows_down(u, prev, n):
    rows = u.shape[0]
    ridx = lax.broadcasted_iota(jnp.int32, u.shape, 0)
    out = pltpu.roll(u, n, 0)
    for r in range(n):
        out = jnp.where(ridx == r, prev[8 - n + r:8 - n + r + 1, :], out)
    del rows
    return out


def _shift_rows_up(u, nxt, n):
    rows = u.shape[0]
    ridx = lax.broadcasted_iota(jnp.int32, u.shape, 0)
    out = pltpu.roll(u, rows - n, 0)
    for r in range(n):
        out = jnp.where(ridx == rows - n + r, nxt[r:r + 1, :], out)
    return out


CONV_COL0 = 3


def _conv_fwd(proj, w_conv, seq, tr):
    hb = tr // 8

    def body(cb_ref, cc_ref, cu_ref, ccp_ref, cup_ref, w_ref, e_ref, d_ref):
        i = pl.program_id(0)
        u = cc_ref[...] * cu_ref[...]
        up = jnp.where(i > 0, ccp_ref[...] * cup_ref[...], 0.0)
        w = w_ref[...]
        d = w[0:1, :] * _shift_rows_down(u, up, 2) + w[1:2, :] * _shift_rows_down(u, up, 1) + w[2:3, :] * u
        d_ref[...] = d
        e_ref[...] = (cb_ref[...] * d).astype(BF16)

    prev = lambda c: (lambda i: (jnp.maximum(i * hb - 1, 0), c))
    return pl.pallas_call(
        body, name="conv_fwd", grid=(seq // tr,),
        in_specs=[_bs((tr, CONV_W), lambda i: (i, CONV_COL0)),
                  _bs((tr, CONV_W), lambda i: (i, CONV_COL0 + 1)),
                  _bs((tr, CONV_W), lambda i: (i, CONV_COL0 + 2)),
                  _bs((8, CONV_W), prev(CONV_COL0 + 1)),
                  _bs((8, CONV_W), prev(CONV_COL0 + 2)),
                  _bs((3, CONV_W), lambda i: (0, 0))],
        out_specs=[_bs((tr, CONV_W), lambda i: (i, 0)), _bs((tr, CONV_W), lambda i: (i, 0))],
        out_shape=[_sds((seq, CONV_W), BF16), _sds((seq, CONV_W), F32)],
        compiler_params=_cparams(1),
    )(proj, proj, proj, proj, proj, w_conv)


def _conv_bwd(proj, de, d, w_conv, seq, tr):
    hb = tr // 8
    nblk = seq // tr

    def body(cb_ref, cc_ref, cu_ref, ccp_ref, cup_ref, cbn_ref, de_ref, den_ref, d_ref, w_ref, o_ref, dw_ref):
        i = pl.program_id(0)
        cc, cu, cb = cc_ref[...], cu_ref[...], cb_ref[...]
        u = cc * cu
        up = jnp.where(i > 0, ccp_ref[...] * cup_ref[...], 0.0)
        u1 = _shift_rows_down(u, up, 1)
        u2 = _shift_rows_down(u, up, 2)
        de_ = de_ref[...]
        dd = de_ * cb
        ddn = jnp.where(i < nblk - 1, den_ref[...] * cbn_ref[...], 0.0)
        w = w_ref[...]
        du = w[2:3, :] * dd + w[1:2, :] * _shift_rows_up(dd, ddn, 1) + w[0:1, :] * _shift_rows_up(dd, ddn, 2)
        o_ref[:, 0:CONV_W] = (de_ * d_ref[...]).astype(BF16)
        o_ref[:, CONV_W:2 * CONV_W] = (du * cu).astype(BF16)
        o_ref[:, 2 * CONV_W:3 * CONV_W] = (du * cc).astype(BF16)
        ridx = lax.broadcasted_iota(jnp.int32, (8, CONV_W), 0)
        dw0 = jnp.sum(dd * u2, axis=0, keepdims=True)
        dw1 = jnp.sum(dd * u1, axis=0, keepdims=True)
        dw2 = jnp.sum(dd * u, axis=0, keepdims=True)
        dw_ref[...] = jnp.where(ridx == 0, dw0, jnp.where(ridx == 1, dw1, jnp.where(ridx == 2, dw2, 0.0)))

    prev = lambda c: (lambda i: (jnp.maximum(i * hb - 1, 0), c))
    nxt = lambda c: (lambda i: (jnp.minimum((i + 1) * hb, seq // 8 - 1), c))
    return pl.pallas_call(
        body, name="conv_bwd", grid=(nblk,),
        in_specs=[_bs((tr, CONV_W), lambda i: (i, CONV_COL0)),
                  _bs((tr, CONV_W), lambda i: (i, CONV_COL0 + 1)),
                  _bs((tr, CONV_W), lambda i: (i, CONV_COL0 + 2)),
                  _bs((8, CONV_W), prev(CONV_COL0 + 1)),
                  _bs((8, CONV_W), prev(CONV_COL0 + 2)),
                  _bs((8, CONV_W), nxt(CONV_COL0)),
                  _bs((tr, CONV_W), lambda i: (i, 0)),
                  _bs((8, CONV_W), nxt(0)),
                  _bs((tr, CONV_W), lambda i: (i, 0)),
                  _bs((3, CONV_W), lambda i: (0, 0))],
        out_specs=[_bs((tr, 3 * CONV_W), lambda i: (i, 0)), _bs((None, 8, CONV_W), lambda i: (i, 0, 0))],
        out_shape=[_sds((seq, 3 * CONV_W), BF16), _sds((nblk, 8, CONV_W), F32)],
        compiler_params=_cparams(1),
    )(proj, proj, proj, proj, proj, proj, de, de, d, w_conv)


def _nt(a, b):
    return lax.dot_general(a, b, _DIMS["nt"], preferred_element_type=F32)


def _tn(a, b):
    return lax.dot_general(a, b, _DIMS["tn"], preferred_element_type=F32)


def _nn(a, b):
    return lax.dot_general(a, b, _DIMS["nn"], preferred_element_type=F32)


def _log_gates(z):
    lse = jnp.log(1.0 + jnp.exp(-jnp.abs(z)))
    log_beta = jnp.minimum(z, 0.0) - lse
    return log_beta, log_beta - z


DEAD_LOG_WEIGHT = -110.0


def _first_live_tile(start, scores, live_sc):
    def alive():
        return jnp.max(jnp.maximum(live_sc[0], live_sc[1])) > DEAD_LOG_WEIGHT

    def step(c):
        for h, z in enumerate(scores(c[0])):
            live_sc[h] = live_sc[h] + jnp.sum(_log_gates(z)[1], axis=-1, keepdims=True)
        return c[0] - 1, alive()

    j_end, _ = lax.while_loop(lambda c: jnp.logical_and(c[0] >= 0, c[1]), step, (start, alive()))
    return j_end + 1


def _attn_fwd(proj, seq):
    blk = ATT_BLK
    nq = seq // blk
    npair = N_HEADS // 2

    def body(q_ref, k_ref, v_ref, o_ref, z0_sc, z1_sc, w0_sc, w1_sc, tot_sc, live_sc, acc_sc):
        i = pl.program_id(1)
        is_a = lax.broadcasted_iota(jnp.int32, (1, LANES), 1) < HEAD_DIM
        q2 = (q_ref[...] * Q_SCALE).astype(BF16)
        zero = jnp.zeros_like(q2)
        qs = (jnp.where(is_a, q2, zero), jnp.where(is_a, zero, q2))
        row = lax.broadcasted_iota(jnp.int32, (blk, blk), 0)
        col = lax.broadcasted_iota(jnp.int32, (blk, blk), 1)
        tri = (row > col).astype(BF16)
        causal = col < row

        def tile_of(ref, j):
            return ref[pl.ds(pl.multiple_of(j * blk, blk), blk), :].astype(BF16)

        def scores(j):
            k2 = tile_of(k_ref, j)
            return [_nt(qs[h], k2) for h in range(2)]

        has_left = i > 0
        left = jnp.maximum(i - 1, 0)
        g_d = [_log_gates(z) for z in scores(i)]
        g_l = [_log_gates(z) for z in scores(left)]
        keep_d = [jnp.where(causal, g[1], 0.0) for g in g_d]
        keep_l = [jnp.where(has_left, g[1], 0.0) for g in g_l]
        suf_d = [_nn(lk.astype(BF16), tri) for lk in keep_d]
        suf_l = [_nn(lk.astype(BF16), tri) for lk in keep_l]
        v_d, v_l = tile_of(v_ref, i), tile_of(v_ref, left)
        pv = []
        for h in range(2):
            sum_d = jnp.sum(keep_d[h], axis=-1, keepdims=True)
            w_d = jnp.where(causal, jnp.exp(g_d[h][0] + suf_d[h]), 0.0)
            w_l = jnp.where(has_left, jnp.exp(g_l[h][0] + (sum_d + suf_l[h])), 0.0)
            pv.append(_nn(w_d.astype(BF16), v_d) + _nn(w_l.astype(BF16), v_l))
            tot_sc[h] = sum_d + jnp.sum(keep_l[h], axis=-1, keepdims=True)
        acc_sc[...] = jnp.where(is_a, pv[0], pv[1])

        live_sc[...] = tot_sc[...]
        first = _first_live_tile(i - 2, scores, live_sc)
        trips = i - 1 - first
        z_bufs, w_bufs = (z0_sc, z1_sc), (w0_sc, w1_sc)

        def put(ref, vals):
            for h in range(2):
                ref[h] = vals[h]

        def weights(zs):
            gates = [_log_gates(z) for z in zs]
            sums = [_nn(g[1].astype(BF16), tri) for g in gates]
            ws = []
            for h in range(2):
                ws.append(jnp.exp(gates[h][0] + (tot_sc[h] + sums[h])).astype(BF16))
                tot_sc[h] = tot_sc[h] + jnp.sum(gates[h][1], axis=-1, keepdims=True)
            return ws

        def add_values(w_buf, j):
            v2 = tile_of(v_ref, j)
            acc_sc[...] += jnp.where(is_a, _nn(w_buf[0], v2), _nn(w_buf[1], v2))

        def trip(j, s):
            add_values(w_bufs[s], j + 1)
            put(z_bufs[1 - s], scores(jnp.maximum(j - 1, first)))
            put(w_bufs[1 - s], weights((z_bufs[s][0], z_bufs[s][1])))

        @pl.when(trips > 0)
        def _():
            put(z0_sc, scores(i - 2))
            w0_sc[...] = jnp.zeros_like(w0_sc)

            def two_trips(pp, carry):
                j = i - 2 - 2 * pp
                trip(j, 0)
                trip(j - 1, 1)
                return carry

            lax.fori_loop(0, trips // 2, two_trips, 0)
            odd = trips % 2 == 1

            @pl.when(odd)
            def _():
                trip(first, 0)
                add_values(w1_sc, first)

            @pl.when(jnp.logical_not(odd))
            def _():
                add_values(w0_sc, first)

        o_ref[...] = acc_sc[...].astype(BF16)

    return pl.pallas_call(
        body, name="attn_fwd", grid=(npair, nq),
        in_specs=[_bs((blk, LANES), lambda p, i: (i, p)),
                  _bs((seq, LANES), lambda p, i: (0, npair + p)),
                  _bs((seq, LANES), lambda p, i: (0, 2 * npair + p))],
        out_specs=_bs((blk, LANES), lambda p, i: (i, p)),
        out_shape=_sds((seq, ATTN_W), BF16),
        scratch_shapes=[pltpu.VMEM((2, blk, blk), F32), pltpu.VMEM((2, blk, blk), F32),
                        pltpu.VMEM((2, blk, blk), BF16), pltpu.VMEM((2, blk, blk), BF16),
                        pltpu.VMEM((2, blk, 1), F32), pltpu.VMEM((2, blk, 1), F32), pltpu.VMEM((blk, LANES), F32)],
        compiler_params=_cparams(2),
    )(proj, proj, proj)


def _attn_bwd(proj, do, seq):
    blk = ATT_BLK
    nq = seq // blk
    npair = N_HEADS // 2

    def body(q_ref, k_ref, v_ref, do_ref, dq_ref, dk_ref, dv_ref,
             prod0_sc, prod1_sc, pend0_sc, pend1_sc, tot_sc, live_sc, cum_sc, pre_sc, dq_sc):
        i = pl.program_id(1)

        @pl.when(i == 0)
        def _():
            dk_ref[...] = jnp.zeros_like(dk_ref)
            dv_ref[...] = jnp.zeros_like(dv_ref)

        is_a = lax.broadcasted_iota(jnp.int32, (1, LANES), 1) < HEAD_DIM
        q2 = (q_ref[...] * Q_SCALE).astype(BF16)
        do2 = do_ref[...]
        zero = jnp.zeros_like(q2)
        qs = (jnp.where(is_a, q2, zero), jnp.where(is_a, zero, q2))
        dos = (jnp.where(is_a, do2, zero), jnp.where(is_a, zero, do2))
        row = lax.broadcasted_iota(jnp.int32, (blk, blk), 0)
        col = lax.broadcasted_iota(jnp.int32, (blk, blk), 1)
        tri_after = (row > col).astype(BF16)
        tri_excl = (row < col).astype(BF16)
        causal = col < row

        def tile_of(ref, j):
            return ref[pl.ds(pl.multiple_of(j * blk, blk), blk), :].astype(BF16)

        def scores(j):
            k2 = tile_of(k_ref, j)
            return [_nt(qs[h], k2) for h in range(2)]

        def products(j):
            v2 = tile_of(v_ref, j)
            return scores(j) + [_nt(dos[h], v2) for h in range(2)]

        def row_sum(a):
            return jnp.sum(a, axis=-1, keepdims=True)

        def grad_matmuls(ws, dzs, j):
            rows = pl.ds(pl.multiple_of(j * blk, blk), blk)
            k2 = tile_of(k_ref, j)
            dq_sc[...] += jnp.where(is_a, _nn(dzs[0], k2), _nn(dzs[1], k2))
            dk_ref[rows, :] += jnp.where(is_a, _tn(dzs[0], q2), _tn(dzs[1], q2))
            if ws is not None:
                dv_ref[rows, :] += jnp.where(is_a, _tn(ws[0], do2), _tn(ws[1], do2))

        has_left = i > 0
        left = jnp.maximum(i - 1, 0)
        p_d, p_l = products(i), products(left)
        g_d = [_log_gates(z) for z in p_d[:2]]
        g_l = [_log_gates(z) for z in p_l[:2]]
        keep_d = [jnp.where(causal, g[1], 0.0) for g in g_d]
        keep_l = [jnp.where(has_left, g[1], 0.0) for g in g_l]
        suf_d = [_nn(lk.astype(BF16), tri_after) for lk in keep_d]
        suf_l = [_nn(lk.astype(BF16), tri_after) for lk in keep_l]
        w_d, w_l, gg_d, gg_l = [], [], [], []
        for h in range(2):
            sum_d = row_sum(keep_d[h])
            w_d.append(jnp.where(causal, jnp.exp(g_d[h][0] + suf_d[h]), 0.0))
            w_l.append(jnp.where(has_left, jnp.exp(g_l[h][0] + (sum_d + suf_l[h])), 0.0))
            gg_d.append(p_d[2 + h] * w_d[h])
            gg_l.append(p_l[2 + h] * w_l[h])
            tot_sc[h] = sum_d + row_sum(keep_l[h])
        before_d = [_nn(g.astype(BF16), tri_excl) for g in gg_d]
        before_l = [_nn(g.astype(BF16), tri_excl) for g in gg_l]
        dz_d, dz_l = [], []
        for h in range(2):
            beta_d, beta_l = jnp.exp(g_d[h][0]), jnp.exp(g_l[h][0])
            dz = gg_l[h] * (1.0 - beta_l) - before_l[h] * beta_l
            dz_l.append(jnp.where(has_left, dz, 0.0).astype(BF16))
            dz = gg_d[h] * (1.0 - beta_d) - (row_sum(gg_l[h]) + before_d[h]) * beta_d
            dz_d.append(jnp.where(causal, dz, 0.0).astype(BF16))
        dq_sc[...] = jnp.zeros_like(dq_sc)
        grad_matmuls([w.astype(BF16) for w in w_l], dz_l, left)
        grad_matmuls([w.astype(BF16) for w in w_d], dz_d, i)

        live_sc[...] = tot_sc[...]
        first = _first_live_tile(i - 2, scores, live_sc)
        trips = i - 1 - first
        prod_bufs, pend_bufs = (prod0_sc, prod1_sc), (pend0_sc, pend1_sc)

        def local_grads(prods):
            zs, dws = prods[:2], prods[2:]
            gates = [_log_gates(z) for z in zs]
            sums = [_nn(g[1].astype(BF16), tri_after) for g in gates]
            ws, gs = [], []
            for h in range(2):
                cum = cum_sc[h] + row_sum(gates[h][1])
                cum_sc[h] = cum
                ws.append(jnp.exp(gates[h][0] + ((live_sc[h] - cum) + sums[h])))
                gs.append(dws[h] * ws[h])
            befores = [_nn(g.astype(BF16), tri_excl) for g in gs]
            dzs = []
            for h in range(2):
                beta = jnp.exp(gates[h][0])
                dzs.append((gs[h] * (1.0 - beta) - (pre_sc[h] + befores[h]) * beta).astype(BF16))
                pre_sc[h] = pre_sc[h] + row_sum(gs[h])
            return [w.astype(BF16) for w in ws] + dzs

        def put(ref, vals):
            for n, val in enumerate(vals):
                ref[n] = val

        def flush(pend, j):
            grad_matmuls([pend[0], pend[1]], [pend[2], pend[3]], j)

        def trip(j, s):
            flush(pend_bufs[s], jnp.maximum(j - 1, first))
            put(prod_bufs[1 - s], products(j + 1))
            put(pend_bufs[1 - s], local_grads([prod_bufs[s][n] for n in range(4)]))

        def earlier_keys_share(j, mask):
            dzs = []
            for h, z in enumerate(scores(j)):
                beta = jnp.exp(_log_gates(z)[0])
                dzs.append(jnp.where(mask, -pre_sc[h] * beta, 0.0).astype(BF16))
            grad_matmuls(None, dzs, j)

        @pl.when(trips > 0)
        def _():
            cum_sc[...] = jnp.zeros_like(cum_sc)
            pre_sc[...] = jnp.zeros_like(pre_sc)
            pend0_sc[...] = jnp.zeros_like(pend0_sc)
            put(prod0_sc, products(first))

            def two_trips(pp, carry):
                trip(first + 2 * pp, 0)
                trip(first + 2 * pp + 1, 1)
                return carry

            lax.fori_loop(0, trips // 2, two_trips, 0)
            odd = trips % 2 == 1

            @pl.when(odd)
            def _():
                trip(i - 2, 0)
                flush(pend1_sc, i - 2)

            @pl.when(jnp.logical_not(odd))
            def _():
                flush(pend0_sc, i - 2)

            earlier_keys_share(i - 1, True)
            earlier_keys_share(i, causal)

        dq_ref[...] = dq_sc[...] * Q_SCALE

    qmap = lambda p, i: (i, p)
    return pl.pallas_call(
        body, name="attn_bwd", grid=(npair, nq),
        in_specs=[_bs((blk, LANES), qmap),
                  _bs((seq, LANES), lambda p, i: (0, npair + p)),
                  _bs((seq, LANES), lambda p, i: (0, 2 * npair + p)),
                  _bs((blk, LANES), qmap)],
        out_specs=[_bs((blk, LANES), qmap),
                   _bs((seq, LANES), lambda p, i: (0, p)),
                   _bs((seq, LANES), lambda p, i: (0, p))],
        out_shape=[_sds((seq, ATTN_W), F32)] * 3,
        scratch_shapes=[pltpu.VMEM((4, blk, blk), F32), pltpu.VMEM((4, blk, blk), F32),
                        pltpu.VMEM((4, blk, blk), BF16), pltpu.VMEM((4, blk, blk), BF16),
                        pltpu.VMEM((2, blk, 1), F32), pltpu.VMEM((2, blk, 1), F32), pltpu.VMEM((2, blk, 1), F32),
                        pltpu.VMEM((2, blk, 1), F32), pltpu.VMEM((blk, LANES), F32)],
        compiler_params=_cparams(2),
    )(proj, proj, proj, do)


def _elementwise(name, fn, ins, out_dtypes):
    rows, cols = ins[0].shape
    tr = rows
    for cand in (512, 256, 128, 64, 32, 16, 8):
        if rows % cand == 0 and cand * cols * 4 <= 2 * 1024 * 1024:
            tr = cand
            break
    n_in = len(ins)

    def body(*refs):
        res = fn(*[r[...] for r in refs[:n_in]])
        for r, val in zip(refs[n_in:], res):
            r[...] = val.astype(r.dtype)

    spec = _bs((tr, cols), lambda i: (i, 0))
    return pl.pallas_call(
        body, name=name, grid=(rows // tr,),
        in_specs=[spec] * n_in, out_specs=[spec] * len(out_dtypes),
        out_shape=[_sds((rows, cols), dt) for dt in out_dtypes],
        compiler_params=_cparams(1),
    )(*ins)


def _adamw_fn(w, g, m, v):
    m = ADAM_B1 * m + (1.0 - ADAM_B1) * g
    v = ADAM_B2 * v + (1.0 - ADAM_B2) * (g * g)
    m_hat = m / (1.0 - ADAM_B1 ** ADAM_STEP)
    v_hat = v / (1.0 - ADAM_B2 ** ADAM_STEP)
    delta = -ADAM_LR * (m_hat / (jnp.sqrt(v_hat) + ADAM_EPS) + ADAM_WD * w)
    return delta, m, v


def _adamw(name, w, g, m, v):
    shape = w.shape
    as2d = lambda a: a.reshape(-1, shape[-1])
    delta, nm, nv = _elementwise(name, _adamw_fn, [as2d(w), as2d(g), as2d(m), as2d(v)], [F32, F32, F32])
    return delta.reshape(shape), nm.reshape(shape), nv.reshape(shape)


def _place():
    x, y, c = lax.axis_index("x"), lax.axis_index("y"), lax.axis_index("c")
    chips = [(1 - x, y), (x, 1 - y), (1 - x, 1 - y)]
    return x, y, c, chips


ANY = pl.BlockSpec(memory_space=pl.ANY)
VMEM_WHOLE = pl.BlockSpec(memory_space=pltpu.VMEM)


def _allgather_weights(shards):
    n = len(shards)

    def body(*refs):
        src, dst = refs[:n], refs[n:2 * n]
        send_sems, recv_sems, local_sems = refs[2 * n:]
        x, y, c, chips = _place()
        me, sibling, mychip = (x, y, c), (x, y, 1 - c), 2 * x + y

        def piece(w, chip, half):
            hr = src[w].shape[0] // 2
            return dst[w].at[chip, pl.ds(half * hr, hr)]

        def copy(w, k, src_ref, dst_ref, to):
            return pltpu.make_async_remote_copy(src_ref=src_ref, dst_ref=dst_ref, send_sem=send_sems.at[w, k],
                                                recv_sem=recv_sems.at[w, k], device_id=to, device_id_type=MESH)

        started, local = [], []
        for w in range(n):
            hr = src[w].shape[0] // 2
            own = pltpu.make_async_copy(src[w], dst[w].at[mychip], local_sems.at[w])
            own.start()
            local.append(own)
            for r, (cx, cy) in enumerate(chips):
                cp = copy(w, r, src[w].at[pl.ds(c * hr, hr)], piece(w, mychip, c), (cx, cy, c))
                cp.start()
                started.append(cp)
        for w in range(n):
            for r, (cx, cy) in enumerate(chips):
                landed = piece(w, 2 * cx + cy, c)
                copy(w, r, landed, landed, me).wait_recv()
                fwd = copy(w, 3 + r, landed, landed, sibling)
                fwd.start()
                started.append(fwd)
        for w in range(n):
            for r, (cx, cy) in enumerate(chips):
                from_sib = piece(w, 2 * cx + cy, 1 - c)
                copy(w, 3 + r, from_sib, from_sib, me).wait_recv()
        for cp in local:
            cp.wait()
        for cp in started:
            cp.wait_send()

    return pl.pallas_call(
        body, name="allgather_weights",
        in_specs=[VMEM_WHOLE] * n, out_specs=[VMEM_WHOLE] * n,
        out_shape=[_sds((N_CHIPS,) + s.shape, s.dtype) for s in shards],
        scratch_shapes=[pltpu.SemaphoreType.DMA((n, 6)), pltpu.SemaphoreType.DMA((n, 6)),
                        pltpu.SemaphoreType.DMA((n,))],
        compiler_params=pltpu.CompilerParams(vmem_limit_bytes=VMEM_LIMIT),
    )(*shards)


SUM_ROWS = 64


def _rs_pair_sum(name, grads):
    n = len(grads)

    def body(*refs):
        g, out = refs[:n], refs[n:2 * n]
        stage, land, keep = refs[2 * n:3 * n], refs[3 * n:4 * n], refs[4 * n:5 * n]
        send_sems, recv_sems, stage_sems, keep_sems = refs[5 * n:]
        x, y, c, _ = _place()
        sibling = (x, y, 1 - c)
        loads = []
        for w in range(n):
            hr = g[w].shape[1] // 2
            st = pltpu.make_async_copy(g[w].at[:, pl.ds((1 - c) * hr, hr)], stage[w], stage_sems.at[w])
            kp = pltpu.make_async_copy(g[w].at[:, pl.ds(c * hr, hr)], keep[w], keep_sems.at[w])
            st.start()
            kp.start()
            loads.append((st, kp))
        gives = []
        for w in range(n):
            loads[w][0].wait()
            give = pltpu.make_async_remote_copy(src_ref=stage[w], dst_ref=land[w], send_sem=send_sems.at[w],
                                                recv_sem=recv_sems.at[w], device_id=sibling, device_id_type=MESH)
            give.start()
            gives.append(give)
        for w in range(n):
            loads[w][1].wait()
            gives[w].wait_recv()
            nb = g[w].shape[1] // 2 // SUM_ROWS

            def add(idx, carry, w=w, nb=nb):
                k, r = idx // nb, pl.multiple_of((idx % nb) * SUM_ROWS, SUM_ROWS)
                rows = pl.ds(r, SUM_ROWS)
                out[w][k, rows, :] = (keep[w][k, rows, :] + land[w][k, rows, :]).astype(BF16)
                return carry

            lax.fori_loop(0, N_CHIPS * nb, add, 0)
        for give in gives:
            give.wait_send()

    half = [(N_CHIPS, a.shape[1] // 2, a.shape[2]) for a in grads]
    bufs = [pltpu.VMEM(s, F32) for s in half]
    sems = pltpu.SemaphoreType.DMA((n,))
    return pl.pallas_call(
        body, name=name,
        in_specs=[ANY] * n, out_specs=[VMEM_WHOLE] * n, out_shape=[_sds(s, BF16) for s in half],
        scratch_shapes=bufs + bufs + bufs + [sems, sems, sems, sems],
        compiler_params=pltpu.CompilerParams(vmem_limit_bytes=VMEM_LIMIT),
    )(*grads)


def _rs_exchange_join(parts):
    n = len(parts)

    def body(*refs):
        t, full, got = refs[:n], refs[n:2 * n], refs[2 * n:3 * n]
        send_sems, recv_sems = refs[3 * n:]
        x, y, c, chips = _place()
        mychip, sibling = 2 * x + y, (x, y, 1 - c)
        sends = []
        for w in range(n):
            for r, (cx, cy) in enumerate(chips):
                cp = pltpu.make_async_remote_copy(src_ref=t[w].at[2 * cx + cy], dst_ref=got[w].at[r],
                                                  send_sem=send_sems.at[w, r], recv_sem=recv_sems.at[w, r],
                                                  device_id=(cx, cy, c), device_id_type=MESH)
                cp.start()
                sends.append(cp)
        for w in range(n):
            hr = t[w].shape[1]
            for r in range(3):
                pltpu.make_async_remote_copy(src_ref=got[w].at[r], dst_ref=got[w].at[r], send_sem=send_sems.at[w, r],
                                             recv_sem=recv_sems.at[w, r], device_id=sibling,
                                             device_id_type=MESH).wait_recv()

            def add(idx, carry, w=w, hr=hr):
                r = pl.multiple_of(idx * SUM_ROWS, SUM_ROWS)
                rows = pl.ds(r, SUM_ROWS)
                f = lambda v: v.astype(F32)
                total = ((f(t[w][mychip, rows, :]) + f(got[w][0, rows, :])) + f(got[w][1, rows, :])) \
                    + f(got[w][2, rows, :])
                full[w][pl.ds(pl.multiple_of(c * hr + r, SUM_ROWS), SUM_ROWS), :] = total
                return carry

            lax.fori_loop(0, hr // SUM_ROWS, add, 0)
            mine = full[w].at[pl.ds(c * hr, hr)]
            give = pltpu.make_async_remote_copy(src_ref=mine, dst_ref=mine, send_sem=send_sems.at[w, 3],
                                                recv_sem=recv_sems.at[w, 3], device_id=sibling, device_id_type=MESH)
            give.start()
            sends.append(give)
        for w in range(n):
            hr = t[w].shape[1]
            theirs = full[w].at[pl.ds((1 - c) * hr, hr)]
            pltpu.make_async_remote_copy(src_ref=theirs, dst_ref=theirs, send_sem=send_sems.at[w, 3],
                                         recv_sem=recv_sems.at[w, 3], device_id=sibling, device_id_type=MESH).wait_recv()
        for cp in sends:
            cp.wait_send()

    return pl.pallas_call(
        body, name="rs_exchange_join",
        in_specs=[VMEM_WHOLE] * n, out_specs=[VMEM_WHOLE] * n,
        out_shape=[_sds((2 * a.shape[1], a.shape[2]), F32) for a in parts],
        scratch_shapes=[pltpu.VMEM((3,) + a.shape[1:], a.dtype) for a in parts]
        + [pltpu.SemaphoreType.DMA((n, 4)), pltpu.SemaphoreType.DMA((n, 4))],
        compiler_params=pltpu.CompilerParams(vmem_limit_bytes=VMEM_LIMIT),
    )(*parts)


def _small_allreduce(loss_p, dg_parts, dbg_a, dbg_c, dwc):
    ins = [loss_p] + list(dg_parts) + [dbg_a, dbg_c, dwc]
    n_in = len(ins)
    vmem = pl.BlockSpec(memory_space=pltpu.VMEM)

    def body(*refs):
        in_refs = refs[:n_in]
        out_ref, vec, buf, send_sems, recv_sems = refs[n_in:]
        x, y, c, _ = _place()
        me = 4 * x + 2 * y + c
        vec[...] = jnp.zeros_like(vec)
        vec[0:1, :] = jnp.sum(in_refs[0][...], axis=0)
        for r in range(5):
            vec[1 + r:2 + r, :] = jnp.sum(in_refs[1 + r][...], axis=0)
        vec[6:7, :] = jnp.sum(in_refs[6][...], axis=0)
        vec[7:8, :] = jnp.sum(in_refs[7][...], axis=0)
        vec[8:16, 0:CONV_W] = jnp.sum(in_refs[8][...], axis=0)
        buf[pl.ds(me, 1)] = vec[...][None]
        copies = []
        for r in range(1, 8):
            fx, fy, fc = (r >> 2) & 1, (r >> 1) & 1, r & 1
            to = (1 - x if fx else x, 1 - y if fy else y, 1 - c if fc else c)
            cp = pltpu.make_async_remote_copy(src_ref=vec, dst_ref=buf.at[me], send_sem=send_sems.at[r - 1],
                                              recv_sem=recv_sems.at[r - 1], device_id=to, device_id_type=MESH)
            cp.start()
            copies.append(cp)
        for cp in copies:
            cp.wait()
        total = buf[0]
        for s in range(1, 8):
            total = total + buf[s]
        out_ref[...] = total
        out_ref[0:1, :] = jnp.broadcast_to(jnp.sum(total[0:1, :], axis=-1, keepdims=True), (1, D_MODEL))

    return pl.pallas_call(
        body, name="small_allreduce",
        in_specs=[vmem] * n_in, out_specs=vmem, out_shape=_sds((SMALL_ROWS, D_MODEL), F32),
        scratch_shapes=[pltpu.VMEM((SMALL_ROWS, D_MODEL), F32), pltpu.VMEM((8, SMALL_ROWS, D_MODEL), F32),
                        pltpu.SemaphoreType.DMA((7,)), pltpu.SemaphoreType.DMA((7,))],
    )(*ins)


def _local_step(x, p, tgt, g, b_gate, w_conv, wf):
    seq = x.shape[0]
    tm = min(seq, 1024)
    th = min(seq, 512)
    tl = min(seq, 2048)
    ni, nh, nl = seq // tm, seq // th, seq // tl
    g_pre_mix, g_post_mix, g_pre_mlp, g_post_mlp, g_ple = g
    w_in, w_ao, w_co, w_o, w_up, w_down, w_pg, w_pp, w_in_nat, w_up_nat = wf
    D = D_MODEL
    vec = lambda a, blk=0: (a, _bs((1, D), lambda i, j, k: (0, blk)))
    rows_i = lambda a, t, blk=0: (a, _bs((t, D), lambda i, j, k: (i, blk)))
    rows_k = lambda a, t, blk=0: (a, _bs((t, D), lambda i, j, k: (k, blk)))
    part = lambda n: (_sds((n, 1, D), F32), _bs((None, 1, D), lambda i, j, k: (i, 0, 0)))
    full2 = lambda a: (a, _bs(a.shape, lambda i, j, k: (0, 0)))

    normed = lambda xb, gb: (_rms(xb, gb).astype(BF16),) * 2
    keep_a = lambda t: [(_sds((seq, D), BF16), _bs((t, D), lambda i, j, k: (i, 0)))]
    proj, h1 = _mm("proj_in", "nn", (ni, 4, 1),
                   a_ins=[rows_i(x, tm), vec(g_pre_mix)], a_fn=normed,
                   b_ins=[(w_in, _bs((None, D, 1280), lambda i, j, k: (j, 0, 0)))], b_fn=_ident,
                   outs=[(_sds((seq, D_IN), F32), _bs((tm, 1280), lambda i, j, k: (i, j)))],
                   acc_shape=(tm, 1280), a_cache=((tm, D), BF16), a_outs=keep_a(tm))
    o = _attn_fwd(proj, seq)
    (y_attn,) = _mm("attn_out", "nn", (ni, 1, 1),
                    a_ins=[(o, _bs((tm, ATTN_W), lambda i, j, k: (i, 0)))], a_fn=_ident,
                    b_ins=[full2(w_ao)], b_fn=_ident,
                    outs=[(_sds((seq, D), BF16), _bs((tm, D), lambda i, j, k: (i, 0)))], acc_shape=(tm, D))
    e, d = _conv_fwd(proj, w_conv, seq, tm)
    (y_conv,) = _mm("conv_out", "nn", (ni, 1, 1),
                    a_ins=[(e, _bs((tm, CONV_W), lambda i, j, k: (i, 0)))], a_fn=_ident,
                    b_ins=[full2(w_co)], b_fn=_ident,
                    outs=[(_sds((seq, D), BF16), _bs((tm, D), lambda i, j, k: (i, 0)))], acc_shape=(tm, D))

    def mix_fn(ga, gc, ya, yc, ba, bc):
        return ((_sig(ga + ba) * ya.astype(F32) + _sig(gc + bc) * yc.astype(F32)).astype(BF16),) * 2

    def post_mix(acc, xb, gb):
        return acc, xb + _rms(acc, gb)

    mix_ins = lambda rows: [rows(proj, th, 3), rows(proj, th, 4), rows(y_attn, th), rows(y_conv, th),
                            vec(b_gate, 0), vec(b_gate, 1)]
    mixed, x1, mixin = _mm("mix_out", "nn", (nh, 1, 1),
                           a_ins=mix_ins(rows_i), a_fn=mix_fn, b_ins=[full2(w_o)], b_fn=_ident,
                           epi_ins=[rows_i(x, th), vec(g_post_mix)], epi_fn=post_mix,
                           outs=[(_sds((seq, D), F32), _bs((th, D), lambda i, j, k: (i, 0)))] * 2,
                           acc_shape=(th, D), a_cache=((th, D), BF16), a_outs=keep_a(th))
    up, h2 = _mm("mlp_up", "nn", (nh, 1, 1),
                 a_ins=[rows_i(x1, th), vec(g_pre_mlp)], a_fn=normed,
                 b_ins=[full2(w_up_nat)], b_fn=_ident,
                 outs=[(_sds((seq, D_FF), BF16), _bs((th, D_FF), lambda i, j, k: (i, 0)))],
                 acc_shape=(th, D_FF), a_cache=((th, D), BF16), a_outs=keep_a(th))

    def relu2(ub):
        r = jnp.maximum(ub.astype(F32), 0.0)
        return (r * r).astype(BF16)

    f, x2 = _mm("mlp_down", "nn", (nh, 1, 1),
                a_ins=[(up, _bs((th, D_FF), lambda i, j, k: (i, 0)))], a_fn=relu2,
                b_ins=[full2(w_down)], b_fn=_ident,
                epi_ins=[rows_i(x1, th), vec(g_post_mlp)], epi_fn=post_mix,
                outs=[(_sds((seq, D), F32), _bs((th, D), lambda i, j, k: (i, 0)))] * 2, acc_shape=(th, D))
    (pp,) = _mm("ple_proj", "nn", (ni, 1, 1),
                a_ins=[(p, _bs((tm, PLE_DIM), lambda i, j, k: (i, 0)))], a_fn=_to_bf16,
                b_ins=[full2(w_pp)], b_fn=_ident,
                outs=[(_sds((seq, D), F32), _bs((tm, D), lambda i, j, k: (i, 0)))], acc_shape=(tm, D))

    def head(acc, x2b, ppb, tb):
        pg = _sig(acc)
        err = x2b + pg * ppb - tb
        return pg, err * (1.0 / D), jnp.sum(err * err, axis=0, keepdims=True) * (0.5 / D)

    pg, dx3, loss_p, h3 = _mm("ple_gate_loss", "nn", (nh, 1, 1),
                              a_ins=[rows_i(x2, th), vec(g_ple)], a_fn=normed,
                              b_ins=[full2(w_pg)], b_fn=_ident,
                              epi_a=(0,), epi_ins=[rows_i(pp, th), rows_i(tgt, th)], epi_fn=head,
                              outs=[(_sds((seq, D), BF16), _bs((th, D), lambda i, j, k: (i, 0))),
                                    (_sds((seq, D), F32), _bs((th, D), lambda i, j, k: (i, 0))), part(nh)],
                              acc_shape=(th, D), a_cache=((th, D), BF16), a_outs=keep_a(th))

    (dw_pp,) = _mm("dw_ple_proj", "tn", (1, 1, nh),
                   a_ins=[(p, _bs((th, PLE_DIM), lambda i, j, k: (k, 0)))], a_fn=_to_bf16,
                   b_ins=[rows_k(dx3, th), rows_k(pg, th)], b_fn=lambda a, b: (a * b.astype(F32)).astype(BF16),
                   outs=[(_sds((PLE_DIM, D), F32), _bs((PLE_DIM, D), lambda i, j, k: (0, 0)))],
                   acc_shape=(PLE_DIM, D))

    def dpre_fn(dx3b, ppb, pgb):
        pgf = pgb.astype(F32)
        return (dx3b * ppb * pgf * (1.0 - pgf)).astype(BF16)

    def ple_norm_bwd(acc, dx3b, x2b, gb, fb, g_mlp):
        dxn, dg = _rms_bwd(x2b, gb, acc)
        dx2b = dx3b + dxn
        dfb, dg_mlp = _rms_bwd(fb, g_mlp, dx2b)
        return dx2b, dg, dfb, dg_mlp

    dx2, dg_ple_p, df, dg_post_mlp_p, dpre = _mm(
        "d_ple_gate", "nt", (nh, 1, 1),
        a_ins=[rows_i(dx3, th), rows_i(pp, th), rows_i(pg, th)], a_fn=lambda a, b, c: (dpre_fn(a, b, c),) * 2,
        b_ins=[full2(w_pg)], b_fn=_ident,
        epi_a=(0,), epi_ins=[rows_i(x2, th), vec(g_ple), rows_i(f, th), vec(g_post_mlp)], epi_fn=ple_norm_bwd,
        outs=[(_sds((seq, D), F32), _bs((th, D), lambda i, j, k: (i, 0))), part(nh),
              (_sds((seq, D), BF16), _bs((th, D), lambda i, j, k: (i, 0))), part(nh)],
        acc_shape=(th, D), a_cache=((th, D), BF16),
        a_outs=[(_sds((seq, D), BF16), _bs((th, D), lambda i, j, k: (i, 0)))])
    (dw_pg,) = _mm("dw_ple_gate", "tn", (1, 1, ni),
                   a_ins=[rows_k(h3, tm)], a_fn=_ident, b_ins=[rows_k(dpre, tm)], b_fn=_ident,
                   outs=[(_sds((D, D), F32), _bs((D, D), lambda i, j, k: (0, 0)))], acc_shape=(D, D))

    def dup_fn(acc, ub):
        return (acc * (2.0 * jnp.maximum(ub.astype(F32), 0.0)),)

    (dup,) = _mm("d_mlp_down", "nt", (nh, 1, 1),
                 a_ins=[rows_i(df, th)], a_fn=_ident, b_ins=[full2(w_down)], b_fn=_ident,
                 epi_ins=[(up, _bs((th, D_FF), lambda i, j, k: (i, 0)))], epi_fn=dup_fn,
                 outs=[(_sds((seq, D_FF), BF16), _bs((th, D_FF), lambda i, j, k: (i, 0)))],
                 acc_shape=(th, D_FF))
    (dw_down,) = _mm("dw_mlp_down", "tn", (4, 1, nl),
                     a_ins=[(up, _bs((tl, D), lambda i, j, k: (k, i)))], a_fn=relu2,
                     b_ins=[rows_k(df, tl)], b_fn=_ident,
                     outs=[(_sds((D_FF, D), F32), _bs((D, D), lambda i, j, k: (i, 0)))], acc_shape=(D, D))
    (dw_up,) = _mm("dw_mlp_up", "tn", (1, 4, nl),
                   a_ins=[rows_k(h2, tl)], a_fn=_ident,
                   b_ins=[(dup, _bs((tl, D), lambda i, j, k: (k, j)))], b_fn=_ident,
                   outs=[(_sds((N_CHIPS, D, D), F32), _bs((None, D, D), lambda i, j, k: (j, 0, 0)))],
                   acc_shape=(D, D))

    def mlp_norm_bwd(acc, x1b, dx2b, mixedb, g_mlp, g_mix):
        dxn, dg_mlp = _rms_bwd(x1b, g_mlp, acc)
        dx1b = dx2b + dxn
        dmixedb, dg_mix = _rms_bwd(mixedb, g_mix, dx1b)
        return dx1b, dmixedb, dg_mlp, dg_mix

    dx1, dmixed, dg_pre_mlp_p, dg_post_mix_p = _mm(
        "d_mlp_up", "nt", (nh, 1, 1),
        a_ins=[(dup, _bs((th, D_FF), lambda i, j, k: (i, 0)))], a_fn=_ident,
        b_ins=[full2(w_up_nat)], b_fn=_ident,
        epi_ins=[rows_i(x1, th), rows_i(dx2, th), rows_i(mixed, th), vec(g_pre_mlp), vec(g_post_mix)],
        epi_fn=mlp_norm_bwd,
        outs=[(_sds((seq, D), F32), _bs((th, D), lambda i, j, k: (i, 0))),
              (_sds((seq, D), BF16), _bs((th, D), lambda i, j, k: (i, 0))), part(nh), part(nh)],
        acc_shape=(th, D))
    (dw_o,) = _mm("dw_mix_out", "tn", (1, 1, ni),
                  a_ins=[rows_k(mixin, tm)], a_fn=_ident, b_ins=[rows_k(dmixed, tm)], b_fn=_ident,
                  outs=[(_sds((D, D), F32), _bs((D, D), lambda i, j, k: (0, 0)))], acc_shape=(D, D))

    def gate_bwd(acc, ga, gc, ya, yc, ba, bc):
        sa, sc = _sig(ga + ba), _sig(gc + bc)
        dga = acc * ya.astype(F32) * sa * (1.0 - sa)
        dgc = acc * yc.astype(F32) * sc * (1.0 - sc)
        return (acc * sa, acc * sc, jnp.concatenate([dga, dgc], axis=1),
                jnp.sum(dga, axis=0, keepdims=True), jnp.sum(dgc, axis=0, keepdims=True))

    dya, dyc, dgate, dbg_a_p, dbg_c_p = _mm(
        "d_mix_out", "nt", (nh, 1, 1),
        a_ins=[rows_i(dmixed, th)], a_fn=_ident, b_ins=[full2(w_o)], b_fn=_ident,
        epi_ins=mix_ins(rows_i), epi_fn=gate_bwd,
        outs=[(_sds((seq, D), BF16), _bs((th, D), lambda i, j, k: (i, 0)))] * 2
             + [(_sds((seq, 2 * D), BF16), _bs((th, 2 * D), lambda i, j, k: (i, 0))), part(nh), part(nh)],
        acc_shape=(th, D))
    (dw_ao,) = _mm("dw_attn_out", "tn", (1, 1, nh),
                   a_ins=[(o, _bs((th, ATTN_W), lambda i, j, k: (k, 0)))], a_fn=_ident,
                   b_ins=[rows_k(dya, th)], b_fn=_ident,
                   outs=[(_sds((ATTN_W, D), F32), _bs((ATTN_W, D), lambda i, j, k: (0, 0)))], acc_shape=(ATTN_W, D))
    (do,) = _mm("d_attn_out", "nt", (ni, 1, 1),
                a_ins=[rows_i(dya, tm)], a_fn=_ident, b_ins=[full2(w_ao)], b_fn=_ident,
                outs=[(_sds((seq, ATTN_W), BF16), _bs((tm, ATTN_W), lambda i, j, k: (i, 0)))],
                acc_shape=(tm, ATTN_W))
    dq, dk, dv = _attn_bwd(proj, do, seq)
    (dw_co,) = _mm("dw_conv_out", "tn", (1, 1, nh),
                   a_ins=[(e, _bs((th, CONV_W), lambda i, j, k: (k, 0)))], a_fn=_ident,
                   b_ins=[rows_k(dyc, th)], b_fn=_ident,
                   outs=[(_sds((CONV_W, D), F32), _bs((CONV_W, D), lambda i, j, k: (0, 0)))], acc_shape=(CONV_W, D))
    (de,) = _mm("d_conv_out", "nt", (ni, 1, 1),
                a_ins=[rows_i(dyc, tm)], a_fn=_ident, b_ins=[full2(w_co)], b_fn=_ident,
                outs=[(_sds((seq, CONV_W), F32), _bs((tm, CONV_W), lambda i, j, k: (i, 0)))],
                acc_shape=(tm, CONV_W))
    dconv, dwc_p = _conv_bwd(proj, de, d, w_conv, seq, tm)
    qkv_w = 3 * ATTN_W
    join_bf16 = lambda *blocks: jnp.concatenate([b.astype(BF16) for b in blocks], axis=1)
    piece = lambda a, t, rows, blk=0: (a, _bs((t, a.shape[1]), (lambda i, j, k: (k, blk)) if rows == "k"
                                             else (lambda i, j, k: (i, blk))))
    (dw_in_qkv,) = _mm("dw_proj_in_qkv", "tn", (1, 1, ni),
                       a_ins=[rows_k(h1, tm)], a_fn=_ident,
                       b_ins=[piece(dq, tm, "k"), piece(dk, tm, "k"), piece(dv, tm, "k")], b_fn=join_bf16,
                       outs=[(_sds((D, qkv_w), F32), _bs((D, qkv_w), lambda i, j, k: (0, 0)))], acc_shape=(D, qkv_w))
    (dw_in_conv,) = _mm("dw_proj_in_conv", "tn", (1, 1, nl),
                        a_ins=[rows_k(h1, tl)], a_fn=_ident, b_ins=[piece(dconv, tl, "k")], b_fn=_ident,
                        outs=[(_sds((D, 3 * CONV_W), F32), _bs((D, 3 * CONV_W), lambda i, j, k: (0, 0)))],
                        acc_shape=(D, 3 * CONV_W))
    (dw_in_gate,) = _mm("dw_proj_in_gate", "tn", (1, 2, nl),
                        a_ins=[rows_k(h1, tl)], a_fn=_ident,
                        b_ins=[(dgate, _bs((tl, D), lambda i, j, k: (k, j)))], b_fn=_ident,
                        outs=[(_sds((D, 2 * D), F32), _bs((D, D), lambda i, j, k: (0, j)))], acc_shape=(D, D))
    dw_in = jnp.concatenate([dw_in_qkv, dw_in_conv, dw_in_gate], axis=1)

    def in_norm_bwd(acc, xb, dx1b, gb):
        dxn, dg = _rms_bwd(xb, gb, acc)
        return dx1b + dxn, dg

    grad_x, dg_pre_mix_p = _mm("d_proj_in", "nt", (nh, 1, 1),
                               a_ins=[piece(dq, th, "i"), piece(dk, th, "i"), piece(dv, th, "i"),
                                      piece(dconv, th, "i"), piece(dgate, th, "i")], a_fn=join_bf16,
                               b_ins=[full2(w_in_nat)], b_fn=_ident,
                               epi_ins=[rows_i(x, th), rows_i(dx1, th), vec(g_pre_mix)], epi_fn=in_norm_bwd,
                               outs=[(_sds((seq, D), F32), _bs((th, D), lambda i, j, k: (i, 0))), part(nh)],
                               acc_shape=(th, D))

    chip_major = lambda a: a.reshape(a.shape[0], N_CHIPS, a.shape[1] // N_CHIPS).transpose(1, 0, 2)
    big = [chip_major(dw_in), chip_major(dw_ao), chip_major(dw_co), dw_o.reshape(N_CHIPS, D // N_CHIPS, D), dw_up,
           dw_down.reshape(N_CHIPS, D_FF // N_CHIPS, D), dw_pg.reshape(N_CHIPS, D // N_CHIPS, D), chip_major(dw_pp)]
    small = (loss_p, [dg_pre_mix_p, dg_post_mix_p, dg_pre_mlp_p, dg_post_mlp_p, dg_ple_p], dbg_a_p, dbg_c_p, dwc_p)
    return grad_x, big, small


RS_GROUPS = ((0,), (4,), (5,), (1, 2, 3, 6, 7))


def _reduce_scatter(big):
    pair = [None] * len(big)
    for gi, group in enumerate(RS_GROUPS):
        for w, s in zip(group, _rs_pair_sum(f"rs_pair_sum_{gi}", [big[w] for w in group])):
            pair[w] = s
    return _rs_exchange_join(pair)


def kernel(x, p, g_pre_mix, w_in, b_gate, w_conv, w_attn_out, w_conv_out, w_o, g_post_mix, g_pre_mlp, w_up, w_down, g_post_mlp, g_ple, w_ple_gate, w_ple_proj, loss_target, m_g_pre_mix, m_w_in, m_b_gate, m_w_conv, m_w_attn_out, m_w_conv_out, m_w_o, m_g_post_mix, m_g_pre_mlp, m_w_up, m_w_down, m_g_post_mlp, m_g_ple, m_w_ple_gate, m_w_ple_proj, v_g_pre_mix, v_w_in, v_b_gate, v_w_conv, v_w_attn_out, v_w_conv_out, v_w_o, v_g_post_mix, v_g_pre_mlp, v_w_up, v_w_down, v_g_post_mlp, v_g_ple, v_w_ple_gate, v_w_ple_proj):
    mats = [w_in, w_attn_out, w_conv_out, w_o, w_up, w_down, w_ple_gate, w_ple_proj]
    mats_m = [m_w_in, m_w_attn_out, m_w_conv_out, m_w_o, m_w_up, m_w_down, m_w_ple_gate, m_w_ple_proj]
    mats_v = [v_w_in, v_w_attn_out, v_w_conv_out, v_w_o, v_w_up, v_w_down, v_w_ple_gate, v_w_ple_proj]
    gains = [g_pre_mix, g_post_mix, g_pre_mlp, g_post_mlp, g_ple]
    gains_m = [m_g_pre_mix, m_g_post_mix, m_g_pre_mlp, m_g_post_mlp, m_g_ple]
    gains_v = [v_g_pre_mix, v_g_post_mix, v_g_pre_mlp, v_g_post_mlp, v_g_ple]

    taps = jnp.concatenate([w_conv[0], jnp.zeros((CONV_PAD_ROWS - 3, LANES), F32)], axis=0)
    gathered = _allgather_weights([w[0].astype(BF16) for w in mats] + [taps])
    cols_joined = lambda a: a.transpose(1, 0, 2).reshape(a.shape[1], N_CHIPS * a.shape[2])
    rows_joined = lambda a: a.reshape(N_CHIPS * a.shape[1], a.shape[2])
    wf = [gathered[0], cols_joined(gathered[1]), cols_joined(gathered[2]), rows_joined(gathered[3]), gathered[4],
          rows_joined(gathered[5]), rows_joined(gathered[6]), cols_joined(gathered[7]),
          cols_joined(gathered[0]), cols_joined(gathered[4])]
    w_conv_full = cols_joined(gathered[8])[0:3, :]
    chip = 2 * lax.axis_index("x") + lax.axis_index("y")

    grad_x, big, small = _local_step(x[0], p[0, 0], loss_target[0], gains, b_gate, w_conv_full, wf)

    shard_grads = _reduce_scatter(big)
    red = _small_allreduce(*small)
    loss = red[0, 0]
    grad_gains = [red[1 + r:2 + r, :] for r in range(5)]
    grad_b_gate = jnp.concatenate([red[6:7, :], red[7:8, :]], axis=1)
    grad_w_conv = lax.dynamic_slice(red[8:11, :], (0, chip * LANES), (3, LANES))[None]

    grads_big = [gr.reshape(w.shape) for gr, w in zip(shard_grads, mats)]
    upd_big = [_adamw(f"adamw_{i}", w, gr, m, v) for i, (w, gr, m, v) in enumerate(zip(mats, grads_big, mats_m, mats_v))]
    pack = lambda vs, bg: jnp.concatenate(list(vs) + [bg.reshape(2, D_MODEL), jnp.zeros((1, D_MODEL), F32)], axis=0)
    upd_small = _adamw("adamw_small", pack(gains, b_gate), pack(grad_gains, grad_b_gate),
                       pack(gains_m, m_b_gate), pack(gains_v, v_b_gate))
    upd_conv = _adamw("adamw_conv", w_conv, grad_w_conv, m_w_conv, v_w_conv)

    def small_out(a, which):
        gains_out = [a[r:r + 1, :] for r in range(5)]
        return gains_out, a[5:7, :].reshape(1, 2 * D_MODEL)

    def ordered(g_pre_mix_, big_, b_gate_, conv_, g_rest):
        return [g_pre_mix_, big_[0], b_gate_, conv_, big_[1], big_[2], big_[3], g_rest[0], g_rest[1], big_[4], big_[5],
                g_rest[2], g_rest[3], big_[6], big_[7]]

    outs = [loss, grad_x[None]]
    outs += ordered(grad_gains[0], grads_big, grad_b_gate, grad_w_conv, grad_gains[1:])
    for which in range(3):
        g_out, b_out = small_out(upd_small[which], which)
        outs += ordered(g_out[0], [u[which] for u in upd_big], b_out, upd_conv[which], g_out[1:])
    return tuple(outs)
```

```python
import functools

import jax
import jax.numpy as jnp
from jax import lax
from jax.experimental import pallas as pl
from jax.experimental.pallas import tpu as pltpu

F32 = jnp.float32
BF16 = jnp.bfloat16
MESH = pl.DeviceIdType.MESH

D_MODEL = 1024
N_HEADS = 8
HEAD_DIM = 64
ATTN_W = N_HEADS * HEAD_DIM
CONV_W = 512
D_FF = 4096
PLE_DIM = 256
D_IN = 5120
N_CHIPS = 4
EPS = 1e-6
Q_SCALE = HEAD_DIM ** -0.5

ADAM_LR = 0.001
ADAM_B1 = 0.9
ADAM_B2 = 0.999
ADAM_EPS = 1e-08
ADAM_WD = 0.01
ADAM_STEP = 10

V7X_VMEM_BYTES = 64 * 1024 * 1024
VMEM_LIMIT = V7X_VMEM_BYTES - 8 * 1024 * 1024
LANES = 128
ATT_BLK = 256
SMALL_ROWS = 16
CONV_PAD_ROWS = 16


def _cparams(n_grid):
    return pltpu.CompilerParams(dimension_semantics=("arbitrary",) * n_grid, vmem_limit_bytes=VMEM_LIMIT)


def _bs(shape, fn):
    return pl.BlockSpec(shape, fn)


def _rms_stats(xf):
    return lax.rsqrt(jnp.mean(xf * xf, axis=-1, keepdims=True) + EPS)


def _rms(xf, g):
    return xf * _rms_stats(xf) * g


def _rms_bwd(xf, g, dy):
    r = _rms_stats(xf)
    xh = xf * r
    dyg = dy * g
    dx = r * (dyg - xh * jnp.mean(dyg * xh, axis=-1, keepdims=True))
    return dx, jnp.sum(dy * xh, axis=0, keepdims=True)


def _sig(z):
    return 1.0 / (1.0 + jnp.exp(-z))


def _ident(a):
    return a


def _to_bf16(a):
    return a.astype(BF16)


_DIMS = {"nn": (((1,), (0,)), ((), ())), "nt": (((1,), (1,)), ((), ())), "tn": (((0,), (0,)), ((), ()))}


def _mm(name, mode, grid, a_ins, a_fn, b_ins, b_fn, outs, acc_shape, epi_ins=(), epi_fn=None,
        a_cache=None, a_outs=(), epi_a=()):
    nk = grid[2]
    na, nb, ne, no, nao = len(a_ins), len(b_ins), len(epi_ins), len(outs), len(a_outs)
    assert a_cache is None or nk == 1
    assert not a_outs or a_cache is not None
    dims = _DIMS[mode]
    if epi_fn is None:
        epi_fn = lambda acc: (acc,)

    def body(*refs):
        a_refs = refs[:na]
        b_refs = refs[na:na + nb]
        e_refs = refs[na + nb:na + nb + ne]
        o_refs = refs[na + nb + ne:na + nb + ne + no]
        ao_refs = refs[na + nb + ne + no:na + nb + ne + no + nao]
        scratch = list(refs[na + nb + ne + no + nao:])
        acc_ref = scratch.pop(0) if nk > 1 else None
        a_sc = scratch.pop(0) if a_cache is not None else None
        j = pl.program_id(1)
        k = pl.program_id(2)

        def finish(acc):
            res = epi_fn(acc, *[a_refs[t][...] for t in epi_a], *[r[...] for r in e_refs])
            for r, val in zip(o_refs, res):
                r[...] = val.astype(r.dtype)

        if a_sc is not None:
            @pl.when(j == 0)
            def _():
                res = a_fn(*[r[...] for r in a_refs])
                if nao:
                    for r, val in zip(ao_refs, res[1:]):
                        r[...] = val.astype(r.dtype)
                    res = res[0]
                a_sc[...] = res
            a = a_sc[...]
        else:
            a = a_fn(*[r[...] for r in a_refs])
        b = b_fn(*[r[...] for r in b_refs])
        prod = lax.dot_general(a, b, dims, preferred_element_type=F32)
        if nk == 1:
            finish(prod)
        else:
            @pl.when(k == 0)
            def _():
                acc_ref[...] = prod

            @pl.when(k > 0)
            def _():
                acc_ref[...] += prod

            @pl.when(k == nk - 1)
            def _():
                finish(acc_ref[...])

    scratch_shapes = []
    if nk > 1:
        scratch_shapes.append(pltpu.VMEM(acc_shape, F32))
    if a_cache is not None:
        scratch_shapes.append(pltpu.VMEM(*a_cache))
    all_outs = list(outs) + list(a_outs)
    res = pl.pallas_call(
        body, name=name, grid=grid,
        in_specs=[s for _, s in a_ins] + [s for _, s in b_ins] + [s for _, s in epi_ins],
        out_specs=[s for _, s in all_outs],
        out_shape=[o for o, _ in all_outs],
        scratch_shapes=scratch_shapes,
        compiler_params=_cparams(3),
    )(*[a for a, _ in a_ins], *[a for a, _ in b_ins], *[a for a, _ in epi_ins])
    return res


def _sds(shape, dtype):
    return jax.ShapeDtypeStruct(shape, dtype)


def _shift_rows_down(u, prev, n):
    rows = u.shape[0]
    ridx = lax.broadcasted_iota(jnp.int32, u.shape, 0)
    out = pltpu.roll(u, n, 0)
    for r in range(n):
        out = jnp.where(ridx == r, prev[8 - n + r:8 - n + r + 1, :], out)
    del rows
    return out


def _shift_rows_up(u, nxt, n):
    rows = u.shape[0]
    ridx = lax.broadcasted_iota(jnp.int32, u.shape, 0)
    out = pltpu.roll(u, rows - n, 0)
    for r in range(n):
        out = jnp.where(ridx == rows - n + r, nxt[r:r + 1, :], out)
    return out


CONV_COL0 = 3


def _conv_fwd(proj, w_conv, seq, tr):
    hb = tr // 8

    def body(cb_ref, cc_ref, cu_ref, ccp_ref, cup_ref, w_ref, e_ref, d_ref):
        i = pl.program_id(0)
        u = cc_ref[...] * cu_ref[...]
        up = jnp.where(i > 0, ccp_ref[...] * cup_ref[...], 0.0)
        w = w_ref[...]
        d = w[0:1, :] * _shift_rows_down(u, up, 2) + w[1:2, :] * _shift_rows_down(u, up, 1) + w[2:3, :] * u
        d_ref[...] = d
        e_ref[...] = (cb_ref[...] * d).astype(BF16)

    prev = lambda c: (lambda i: (jnp.maximum(i * hb - 1, 0), c))
    return pl.pallas_call(
        body, name="conv_fwd", grid=(seq // tr,),
        in_specs=[_bs((tr, CONV_W), lambda i: (i, CONV_COL0)),
                  _bs((tr, CONV_W), lambda i: (i, CONV_COL0 + 1)),
                  _bs((tr, CONV_W), lambda i: (i, CONV_COL0 + 2)),
                  _bs((8, CONV_W), prev(CONV_COL0 + 1)),
                  _bs((8, CONV_W), prev(CONV_COL0 + 2)),
                  _bs((3, CONV_W), lambda i: (0, 0))],
        out_specs=[_bs((tr, CONV_W), lambda i: (i, 0)), _bs((tr, CONV_W), lambda i: (i, 0))],
        out_shape=[_sds((seq, CONV_W), BF16), _sds((seq, CONV_W), F32)],
        compiler_params=_cparams(1),
    )(proj, proj, proj, proj, proj, w_conv)


def _conv_bwd(proj, de, d, w_conv, seq, tr):
    hb = tr // 8
    nblk = seq // tr

    def body(cb_ref, cc_ref, cu_ref, ccp_ref, cup_ref, cbn_ref, de_ref, den_ref, d_ref, w_ref, o_ref, dw_ref):
        i = pl.program_id(0)
        cc, cu, cb = cc_ref[...], cu_ref[...], cb_ref[...]
        u = cc * cu
        up = jnp.where(i > 0, ccp_ref[...] * cup_ref[...], 0.0)
        u1 = _shift_rows_down(u, up, 1)
        u2 = _shift_rows_down(u, up, 2)
        de_ = de_ref[...]
        dd = de_ * cb
        ddn = jnp.where(i < nblk - 1, den_ref[...] * cbn_ref[...], 0.0)
        w = w_ref[...]
        du = w[2:3, :] * dd + w[1:2, :] * _shift_rows_up(dd, ddn, 1) + w[0:1, :] * _shift_rows_up(dd, ddn, 2)
        o_ref[:, 0:CONV_W] = (de_ * d_ref[...]).astype(BF16)
        o_ref[:, CONV_W:2 * CONV_W] = (du * cu).astype(BF16)
        o_ref[:, 2 * CONV_W:3 * CONV_W] = (du * cc).astype(BF16)
        ridx = lax.broadcasted_iota(jnp.int32, (8, CONV_W), 0)
        dw0 = jnp.sum(dd * u2, axis=0, keepdims=True)
        dw1 = jnp.sum(dd * u1, axis=0, keepdims=True)
        dw2 = jnp.sum(dd * u, axis=0, keepdims=True)
        dw_ref[...] = jnp.where(ridx == 0, dw0, jnp.where(ridx == 1, dw1, jnp.where(ridx == 2, dw2, 0.0)))

    prev = lambda c: (lambda i: (jnp.maximum(i * hb - 1, 0), c))
    nxt = lambda c: (lambda i: (jnp.minimum((i + 1) * hb, seq // 8 - 1), c))
    return pl.pallas_call(
        body, name="conv_bwd", grid=(nblk,),
        in_specs=[_bs((tr, CONV_W), lambda i: (i, CONV_COL0)),
                  _bs((tr, CONV_W), lambda i: (i, CONV_COL0 + 1)),
                  _bs((tr, CONV_W), lambda i: (i, CONV_COL0 + 2)),
                  _bs((8, CONV_W), prev(CONV_COL0 + 1)),
                  _bs((8, CONV_W), prev(CONV_COL0 + 2)),
                  _bs((8, CONV_W), nxt(CONV_COL0)),
                  _bs((tr, CONV_W), lambda i: (i, 0)),
                  _bs((8, CONV_W), nxt(0)),
                  _bs((tr, CONV_W), lambda i: (i, 0)),
                  _bs((3, CONV_W), lambda i: (0, 0))],
        out_specs=[_bs((tr, 3 * CONV_W), lambda i: (i, 0)), _bs((None, 8, CONV_W), lambda i: (i, 0, 0))],
        out_shape=[_sds((seq, 3 * CONV_W), BF16), _sds((nblk, 8, CONV_W), F32)],
        compiler_params=_cparams(1),
    )(proj, proj, proj, proj, proj, proj, de, de, d, w_conv)


def _nt(a, b):
    return lax.dot_general(a, b, _DIMS["nt"], preferred_element_type=F32)


def _tn(a, b):
    return lax.dot_general(a, b, _DIMS["tn"], preferred_element_type=F32)


def _nn(a, b):
    return lax.dot_general(a, b, _DIMS["nn"], preferred_element_type=F32)


def _log_gates(z):
    lse = jnp.log(1.0 + jnp.exp(-jnp.abs(z)))
    log_beta = jnp.minimum(z, 0.0) - lse
    return log_beta, log_beta - z


DEAD_LOG_WEIGHT = -110.0


def _first_live_tile(start, scores, live_sc):
    def alive():
        return jnp.max(jnp.maximum(live_sc[0], live_sc[1])) > DEAD_LOG_WEIGHT

    def step(c):
        for h, z in enumerate(scores(c[0])):
            live_sc[h] = live_sc[h] + jnp.sum(_log_gates(z)[1], axis=-1, keepdims=True)
        return c[0] - 1, alive()

    j_end, _ = lax.while_loop(lambda c: jnp.logical_and(c[0] >= 0, c[1]), step, (start, alive()))
    return j_end + 1


def _attn_fwd(proj, seq):
    blk = ATT_BLK
    nq = seq // blk
    npair = N_HEADS // 2

    def body(q_ref, k_ref, v_ref, o_ref, z0_sc, z1_sc, w0_sc, w1_sc, tot_sc, live_sc, acc_sc):
        i = pl.program_id(1)
        is_a = lax.broadcasted_iota(jnp.int32, (1, LANES), 1) < HEAD_DIM
        q2 = (q_ref[...] * Q_SCALE).astype(BF16)
        zero = jnp.zeros_like(q2)
        qs = (jnp.where(is_a, q2, zero), jnp.where(is_a, zero, q2))
        row = lax.broadcasted_iota(jnp.int32, (blk, blk), 0)
        col = lax.broadcasted_iota(jnp.int32, (blk, blk), 1)
        tri = (row > col).astype(BF16)
        causal = col < row

        def tile_of(ref, j):
            return ref[pl.ds(pl.multiple_of(j * blk, blk), blk), :].astype(BF16)

        def scores(j):
            k2 = tile_of(k_ref, j)
            return [_nt(qs[h], k2) for h in range(2)]

        has_left = i > 0
        left = jnp.maximum(i - 1, 0)
        g_d = [_log_gates(z) for z in scores(i)]
        g_l = [_log_gates(z) for z in scores(left)]
        keep_d = [jnp.where(causal, g[1], 0.0) for g in g_d]
        keep_l = [jnp.where(has_left, g[1], 0.0) for g in g_l]
        suf_d = [_nn(lk.astype(BF16), tri) for lk in keep_d]
        suf_l = [_nn(lk.astype(BF16), tri) for lk in keep_l]
        v_d, v_l = tile_of(v_ref, i), tile_of(v_ref, left)
        pv = []
        for h in range(2):
            sum_d = jnp.sum(keep_d[h], axis=-1, keepdims=True)
            w_d = jnp.where(causal, jnp.exp(g_d[h][0] + suf_d[h]), 0.0)
            w_l = jnp.where(has_left, jnp.exp(g_l[h][0] + (sum_d + suf_l[h])), 0.0)
            pv.append(_nn(w_d.astype(BF16), v_d) + _nn(w_l.astype(BF16), v_l))
            tot_sc[h] = sum_d + jnp.sum(keep_l[h], axis=-1, keepdims=True)
        acc_sc[...] = jnp.where(is_a, pv[0], pv[1])

        live_sc[...] = tot_sc[...]
        first = _first_live_tile(i - 2, scores, live_sc)
        trips = i - 1 - first
        z_bufs, w_bufs = (z0_sc, z1_sc), (w0_sc, w1_sc)

        def put(ref, vals):
            for h in range(2):
                ref[h] = vals[h]

        def weights(zs):
            gates = [_log_gates(z) for z in zs]
            sums = [_nn(g[1].astype(BF16), tri) for g in gates]
            ws = []
            for h in range(2):
                ws.append(jnp.exp(gates[h][0] + (tot_sc[h] + sums[h])).astype(BF16))
                tot_sc[h] = tot_sc[h] + jnp.sum(gates[h][1], axis=-1, keepdims=True)
            return ws

        def add_values(w_buf, j):
            v2 = tile_of(v_ref, j)
            acc_sc[...] += jnp.where(is_a, _nn(w_buf[0], v2), _nn(w_buf[1], v2))

        def trip(j, s):
            add_values(w_bufs[s], j + 1)
            put(z_bufs[1 - s], scores(jnp.maximum(j - 1, first)))
            put(w_bufs[1 - s], weights((z_bufs[s][0], z_bufs[s][1])))

        @pl.when(trips > 0)
        def _():
            put(z0_sc, scores(i - 2))
            w0_sc[...] = jnp.zeros_like(w0_sc)

            def two_trips(pp, carry):
                j = i - 2 - 2 * pp
                trip(j, 0)
                trip(j - 1, 1)
                return carry

            lax.fori_loop(0, trips // 2, two_trips, 0)
            odd = trips % 2 == 1

            @pl.when(odd)
            def _():
                trip(first, 0)
                add_values(w1_sc, first)

            @pl.when(jnp.logical_not(odd))
            def _():
                add_values(w0_sc, first)

        o_ref[...] = acc_sc[...].astype(BF16)

    return pl.pallas_call(
        body, name="attn_fwd", grid=(npair, nq),
        in_specs=[_bs((blk, LANES), lambda p, i: (i, p)),
                  _bs((seq, LANES), lambda p, i: (0, npair + p)),
                  _bs((seq, LANES), lambda p, i: (0, 2 * npair + p))],
        out_specs=_bs((blk, LANES), lambda p, i: (i, p)),
        out_shape=_sds((seq, ATTN_W), BF16),
        scratch_shapes=[pltpu.VMEM((2, blk, blk), F32), pltpu.VMEM((2, blk, blk), F32),
                        pltpu.VMEM((2, blk, blk), BF16), pltpu.VMEM((2, blk, blk), BF16),
                        pltpu.VMEM((2, blk, 1), F32), pltpu.VMEM((2, blk, 1), F32), pltpu.VMEM((blk, LANES), F32)],
        compiler_params=_cparams(2),
    )(proj, proj, proj)


def _attn_bwd(proj, do, seq):
    blk = ATT_BLK
    nq = seq // blk
    npair = N_HEADS // 2

    def body(q_ref, k_ref, v_ref, do_ref, dq_ref, dk_ref, dv_ref,
             prod0_sc, prod1_sc, pend0_sc, pend1_sc, tot_sc, live_sc, cum_sc, pre_sc, dq_sc):
        i = pl.program_id(1)

        @pl.when(i == 0)
        def _():
            dk_ref[...] = jnp.zeros_like(dk_ref)
            dv_ref[...] = jnp.zeros_like(dv_ref)

        is_a = lax.broadcasted_iota(jnp.int32, (1, LANES), 1) < HEAD_DIM
        q2 = (q_ref[...] * Q_SCALE).astype(BF16)
        do2 = do_ref[...]
        zero = jnp.zeros_like(q2)
        qs = (jnp.where(is_a, q2, zero), jnp.where(is_a, zero, q2))
        dos = (jnp.where(is_a, do2, zero), jnp.where(is_a, zero, do2))
        row = lax.broadcasted_iota(jnp.int32, (blk, blk), 0)
        col = lax.broadcasted_iota(jnp.int32, (blk, blk), 1)
        tri_after = (row > col).astype(BF16)
        tri_excl = (row < col).astype(BF16)
        causal = col < row

        def tile_of(ref, j):
            return ref[pl.ds(pl.multiple_of(j * blk, blk), blk), :].astype(BF16)

        def scores(j):
            k2 = tile_of(k_ref, j)
            return [_nt(qs[h], k2) for h in range(2)]

        def products(j):
            v2 = tile_of(v_ref, j)
            return scores(j) + [_nt(dos[h], v2) for h in range(2)]

        def row_sum(a):
            return jnp.sum(a, axis=-1, keepdims=True)

        def grad_matmuls(ws, dzs, j):
            rows = pl.ds(pl.multiple_of(j * blk, blk), blk)
            k2 = tile_of(k_ref, j)
            dq_sc[...] += jnp.where(is_a, _nn(dzs[0], k2), _nn(dzs[1], k2))
            dk_ref[rows, :] += jnp.where(is_a, _tn(dzs[0], q2), _tn(dzs[1], q2))
            if ws is not None:
                dv_ref[rows, :] += jnp.where(is_a, _tn(ws[0], do2), _tn(ws[1], do2))

        has_left = i > 0
        left = jnp.maximum(i - 1, 0)
        p_d, p_l = products(i), products(left)
        g_d = [_log_gates(z) for z in p_d[:2]]
        g_l = [_log_gates(z) for z in p_l[:2]]
        keep_d = [jnp.where(causal, g[1], 0.0) for g in g_d]
        keep_l = [jnp.where(has_left, g[1], 0.0) for g in g_l]
        suf_d = [_nn(lk.astype(BF16), tri_after) for lk in keep_d]
        suf_l = [_nn(lk.astype(BF16), tri_after) for lk in keep_l]
        w_d, w_l, gg_d, gg_l = [], [], [], []
        for h in range(2):
            sum_d = row_sum(keep_d[h])
            w_d.append(jnp.where(causal, jnp.exp(g_d[h][0] + suf_d[h]), 0.0))
            w_l.append(jnp.where(has_left, jnp.exp(g_l[h][0] + (sum_d + suf_l[h])), 0.0))
            gg_d.append(p_d[2 + h] * w_d[h])
            gg_l.append(p_l[2 + h] * w_l[h])
            tot_sc[h] = sum_d + row_sum(keep_l[h])
        before_d = [_nn(g.astype(BF16), tri_excl) for g in gg_d]
        before_l = [_nn(g.astype(BF16), tri_excl) for g in gg_l]
        dz_d, dz_l = [], []
        for h in range(2):
            beta_d, beta_l = jnp.exp(g_d[h][0]), jnp.exp(g_l[h][0])
            dz = gg_l[h] * (1.0 - beta_l) - before_l[h] * beta_l
            dz_l.append(jnp.where(has_left, dz, 0.0).astype(BF16))
            dz = gg_d[h] * (1.0 - beta_d) - (row_sum(gg_l[h]) + before_d[h]) * beta_d
            dz_d.append(jnp.where(causal, dz, 0.0).astype(BF16))
        dq_sc[...] = jnp.zeros_like(dq_sc)
        grad_matmuls([w.astype(BF16) for w in w_l], dz_l, left)
        grad_matmuls([w.astype(BF16) for w in w_d], dz_d, i)

        live_sc[...] = tot_sc[...]
        first = _first_live_tile(i - 2, scores, live_sc)
        trips = i - 1 - first
        prod_bufs, pend_bufs = (prod0_sc, prod1_sc), (pend0_sc, pend1_sc)

        def local_grads(prods):
            zs, dws = prods[:2], prods[2:]
            gates = [_log_gates(z) for z in zs]
            sums = [_nn(g[1].astype(BF16), tri_after) for g in gates]
            ws, gs = [], []
            for h in range(2):
                cum = cum_sc[h] + row_sum(gates[h][1])
                cum_sc[h] = cum
                ws.append(jnp.exp(gates[h][0] + ((live_sc[h] - cum) + sums[h])))
                gs.append(dws[h] * ws[h])
            befores = [_nn(g.astype(BF16), tri_excl) for g in gs]
            dzs = []
            for h in range(2):
                beta = jnp.exp(gates[h][0])
                dzs.append((gs[h] * (1.0 - beta) - (pre_sc[h] + befores[h]) * beta).astype(BF16))
                pre_sc[h] = pre_sc[h] + row_sum(gs[h])
            return [w.astype(BF16) for w in ws] + dzs

        def put(ref, vals):
            for n, val in enumerate(vals):
                ref[n] = val

        def flush(pend, j):
            grad_matmuls([pend[0], pend[1]], [pend[2], pend[3]], j)

        def trip(j, s):
            flush(pend_bufs[s], jnp.maximum(j - 1, first))
            put(prod_bufs[1 - s], products(j + 1))
            put(pend_bufs[1 - s], local_grads([prod_bufs[s][n] for n in range(4)]))

        def earlier_keys_share(j, mask):
            dzs = []
            for h, z in enumerate(scores(j)):
                beta = jnp.exp(_log_gates(z)[0])
                dzs.append(jnp.where(mask, -pre_sc[h] * beta, 0.0).astype(BF16))
            grad_matmuls(None, dzs, j)

        @pl.when(trips > 0)
        def _():
            cum_sc[...] = jnp.zeros_like(cum_sc)
            pre_sc[...] = jnp.zeros_like(pre_sc)
            pend0_sc[...] = jnp.zeros_like(pend0_sc)
            put(prod0_sc, products(first))

            def two_trips(pp, carry):
                trip(first + 2 * pp, 0)
                trip(first + 2 * pp + 1, 1)
                return carry

            lax.fori_loop(0, trips // 2, two_trips, 0)
            odd = trips % 2 == 1

            @pl.when(odd)
            def _():
                trip(i - 2, 0)
                flush(pend1_sc, i - 2)

            @pl.when(jnp.logical_not(odd))
            def _():
                flush(pend0_sc, i - 2)

            earlier_keys_share(i - 1, True)
            earlier_keys_share(i, causal)

        dq_ref[...] = dq_sc[...] * Q_SCALE

    qmap = lambda p, i: (i, p)
    return pl.pallas_call(
        body, name="attn_bwd", grid=(npair, nq),
        in_specs=[_bs((blk, LANES), qmap),
                  _bs((seq, LANES), lambda p, i: (0, npair + p)),
                  _bs((seq, LANES), lambda p, i: (0, 2 * npair + p)),
                  _bs((blk, LANES), qmap)],
        out_specs=[_bs((blk, LANES), qmap),
                   _bs((seq, LANES), lambda p, i: (0, p)),
                   _bs((seq, LANES), lambda p, i: (0, p))],
        out_shape=[_sds((seq, ATTN_W), F32)] * 3,
        scratch_shapes=[pltpu.VMEM((4, blk, blk), F32), pltpu.VMEM((4, blk, blk), F32),
                        pltpu.VMEM((4, blk, blk), BF16), pltpu.VMEM((4, blk, blk), BF16),
                        pltpu.VMEM((2, blk, 1), F32), pltpu.VMEM((2, blk, 1), F32), pltpu.VMEM((2, blk, 1), F32),
                        pltpu.VMEM((2, blk, 1), F32), pltpu.VMEM((blk, LANES), F32)],
        compiler_params=_cparams(2),
    )(proj, proj, proj, do)


def _elementwise(name, fn, ins, out_dtypes):
    rows, cols = ins[0].shape
    tr = rows
    for cand in (512, 256, 128, 64, 32, 16, 8):
        if rows % cand == 0 and cand * cols * 4 <= 2 * 1024 * 1024:
            tr = cand
            break
    n_in = len(ins)

    def body(*refs):
        res = fn(*[r[...] for r in refs[:n_in]])
        for r, val in zip(refs[n_in:], res):
            r[...] = val.astype(r.dtype)

    spec = _bs((tr, cols), lambda i: (i, 0))
    return pl.pallas_call(
        body, name=name, grid=(rows // tr,),
        in_specs=[spec] * n_in, out_specs=[spec] * len(out_dtypes),
        out_shape=[_sds((rows, cols), dt) for dt in out_dtypes],
        compiler_params=_cparams(1),
    )(*ins)


def _adamw_fn(w, g, m, v):
    m = ADAM_B1 * m + (1.0 - ADAM_B1) * g
    v = ADAM_B2 * v + (1.0 - ADAM_B2) * (g * g)
    m_hat = m / (1.0 - ADAM_B1 ** ADAM_STEP)
    v_hat = v / (1.0 - ADAM_B2 ** ADAM_STEP)
    delta = -ADAM_LR * (m_hat / (jnp.sqrt(v_hat) + ADAM_EPS) + ADAM_WD * w)
    return delta, m, v


def _adamw(name, w, g, m, v):
    shape = w.shape
    as2d = lambda a: a.reshape(-1, shape[-1])
    delta, nm, nv = _elementwise(name, _adamw_fn, [as2d(w), as2d(g), as2d(m), as2d(v)], [F32, F32, F32])
    return delta.reshape(shape), nm.reshape(shape), nv.reshape(shape)


def _place():
    x, y, c = lax.axis_index("x"), lax.axis_index("y"), lax.axis_index("c")
    chips = [(1 - x, y), (x, 1 - y), (1 - x, 1 - y)]
    return x, y, c, chips


ANY = pl.BlockSpec(memory_space=pl.ANY)
VMEM_WHOLE = pl.BlockSpec(memory_space=pltpu.VMEM)


def _allgather_weights(shards):
    n = len(shards)

    def body(*refs):
        src, dst = refs[:n], refs[n:2 * n]
        send_sems, recv_sems, local_sems = refs[2 * n:]
        x, y, c, chips = _place()
        me, sibling, mychip = (x, y, c), (x, y, 1 - c), 2 * x + y

        x_nbr, y_nbr, diag = 2 * (1 - x) + y, 2 * x + (1 - y), 2 * (1 - x) + (1 - y)
        to_x, to_y = (1 - x, y, c), (x, 1 - y, c)

        def parts(w):
            hr = src[w].shape[0] // 2
            first = hr // 2 if hr % 32 == 0 else hr
            return first, hr - first

        def rows_of(w, chip, half, route):
            hr = src[w].shape[0] // 2
            first, second = parts(w)
            start, size = {0: (0, hr), 1: (0, hr), 2: (0, first), 3: (first, second)}[route]
            return dst[w].at[chip, pl.ds(half * hr + start, size)]

        def copy(w, k, src_ref, dst_ref, to):
            return pltpu.make_async_remote_copy(src_ref=src_ref, dst_ref=dst_ref, send_sem=send_sems.at[w, k],
                                                recv_sem=recv_sems.at[w, k], device_id=to, device_id_type=MESH)

        def landed(w, route):
            chip = {0: x_nbr, 1: y_nbr, 2: diag, 3: diag}[route]
            return rows_of(w, chip, c, route), chip

        def routes(w):
            return (0, 1, 2, 3) if parts(w)[1] else (0, 1, 2)

        started, local = [], []
        for w in range(n):
            hr = src[w].shape[0] // 2
            own = pltpu.make_async_copy(src[w], dst[w].at[mychip], local_sems.at[w])
            own.start()
            local.append(own)
            mine = src[w].at[pl.ds(c * hr, hr)]
            for route, to in ((0, to_x), (1, to_y)):
                cp = copy(w, route, mine, rows_of(w, mychip, c, route), to)
                cp.start()
                started.append(cp)

        def pass_on(w, route):
            got, chip = landed(w, route)
            copy(w, route, got, got, me).wait_recv()
            if route == 1:
                part = rows_of(w, chip, c, 2)
                started.append(copy(w, 2, part, part, to_x))
                started[-1].start()
            if route == 0 and parts(w)[1]:
                part = rows_of(w, chip, c, 3)
                started.append(copy(w, 3, part, part, to_y))
                started[-1].start()
            started.append(copy(w, 4 + route, got, got, sibling))
            started[-1].start()

        for w in range(n):
            pass_on(w, 1)
            pass_on(w, 0)
        for w in range(n):
            for route in routes(w)[2:]:
                pass_on(w, route)
        for w in range(n):
            for route in routes(w):
                chip = landed(w, route)[1]
                from_sib = rows_of(w, chip, 1 - c, route)
                copy(w, 4 + route, from_sib, from_sib, me).wait_recv()
        for cp in local:
            cp.wait()
        for cp in started:
            cp.wait_send()

    return pl.pallas_call(
        body, name="allgather_weights",
        in_specs=[VMEM_WHOLE] * n, out_specs=[VMEM_WHOLE] * n,
        out_shape=[_sds((N_CHIPS,) + s.shape, s.dtype) for s in shards],
        scratch_shapes=[pltpu.SemaphoreType.DMA((n, 8)), pltpu.SemaphoreType.DMA((n, 8)),
                        pltpu.SemaphoreType.DMA((n,))],
        compiler_params=pltpu.CompilerParams(vmem_limit_bytes=VMEM_LIMIT),
    )(*shards)


SUM_ROWS = 64


def _rs_pair_sum(name, grads):
    n = len(grads)

    def body(*refs):
        g, out = refs[:n], refs[n:2 * n]
        stage, land, keep = refs[2 * n:3 * n], refs[3 * n:4 * n], refs[4 * n:5 * n]
        send_sems, recv_sems, stage_sems, keep_sems = refs[5 * n:]
        x, y, c, _ = _place()
        sibling = (x, y, 1 - c)
        loads = []
        for w in range(n):
            hr = g[w].shape[1] // 2
            st = pltpu.make_async_copy(g[w].at[:, pl.ds((1 - c) * hr, hr)], stage[w], stage_sems.at[w])
            kp = pltpu.make_async_copy(g[w].at[:, pl.ds(c * hr, hr)], keep[w], keep_sems.at[w])
            st.start()
            kp.start()
            loads.append((st, kp))
        gives = []
        for w in range(n):
            loads[w][0].wait()
            give = pltpu.make_async_remote_copy(src_ref=stage[w], dst_ref=land[w], send_sem=send_sems.at[w],
                                                recv_sem=recv_sems.at[w], device_id=sibling, device_id_type=MESH)
            give.start()
            gives.append(give)
        for w in range(n):
            loads[w][1].wait()
            gives[w].wait_recv()
            nb = g[w].shape[1] // 2 // SUM_ROWS

            def add(idx, carry, w=w, nb=nb):
                k, r = idx // nb, pl.multiple_of((idx % nb) * SUM_ROWS, SUM_ROWS)
                rows = pl.ds(r, SUM_ROWS)
                out[w][k, rows, :] = (keep[w][k, rows, :] + land[w][k, rows, :]).astype(BF16)
                return carry

            lax.fori_loop(0, N_CHIPS * nb, add, 0)
        for give in gives:
            give.wait_send()

    half = [(N_CHIPS, a.shape[1] // 2, a.shape[2]) for a in grads]
    bufs = [pltpu.VMEM(s, F32) for s in half]
    sems = pltpu.SemaphoreType.DMA((n,))
    return pl.pallas_call(
        body, name=name,
        in_specs=[ANY] * n, out_specs=[VMEM_WHOLE] * n, out_shape=[_sds(s, BF16) for s in half],
        scratch_shapes=bufs + bufs + bufs + [sems, sems, sems, sems],
        compiler_params=pltpu.CompilerParams(vmem_limit_bytes=VMEM_LIMIT),
    )(*grads)


def _rs_exchange_join(parts):
    n = len(parts)

    def body(*refs):
        t, full, got = refs[:n], refs[n:2 * n], refs[2 * n:3 * n]
        send_sems, recv_sems = refs[3 * n:]
        x, y, c, chips = _place()
        mychip, sibling = 2 * x + y, (x, y, 1 - c)
        sends = []
        for w in range(n):
            for r, (cx, cy) in enumerate(chips):
                cp = pltpu.make_async_remote_copy(src_ref=t[w].at[2 * cx + cy], dst_ref=got[w].at[r],
                                                  send_sem=send_sems.at[w, r], recv_sem=recv_sems.at[w, r],
                                                  device_id=(cx, cy, c), device_id_type=MESH)
                cp.start()
                sends.append(cp)
        for w in range(n):
            hr = t[w].shape[1]
            for r in range(3):
                pltpu.make_async_remote_copy(src_ref=got[w].at[r], dst_ref=got[w].at[r], send_sem=send_sems.at[w, r],
                                             recv_sem=recv_sems.at[w, r], device_id=sibling,
                                             device_id_type=MESH).wait_recv()

            def add(idx, carry, w=w, hr=hr):
                r = pl.multiple_of(idx * SUM_ROWS, SUM_ROWS)
                rows = pl.ds(r, SUM_ROWS)
                f = lambda v: v.astype(F32)
                total = ((f(t[w][mychip, rows, :]) + f(got[w][0, rows, :])) + f(got[w][1, rows, :])) \
                    + f(got[w][2, rows, :])
                full[w][pl.ds(pl.multiple_of(c * hr + r, SUM_ROWS), SUM_ROWS), :] = total
                return carry

            lax.fori_loop(0, hr // SUM_ROWS, add, 0)
            mine = full[w].at[pl.ds(c * hr, hr)]
            give = pltpu.make_async_remote_copy(src_ref=mine, dst_ref=mine, send_sem=send_sems.at[w, 3],
                                                recv_sem=recv_sems.at[w, 3], device_id=sibling, device_id_type=MESH)
            give.start()
            sends.append(give)
        for w in range(n):
            hr = t[w].shape[1]
            theirs = full[w].at[pl.ds((1 - c) * hr, hr)]
            pltpu.make_async_remote_copy(src_ref=theirs, dst_ref=theirs, send_sem=send_sems.at[w, 3],
                                         recv_sem=recv_sems.at[w, 3], device_id=sibling, device_id_type=MESH).wait_recv()
        for cp in sends:
            cp.wait_send()

    return pl.pallas_call(
        body, name="rs_exchange_join",
        in_specs=[VMEM_WHOLE] * n, out_specs=[VMEM_WHOLE] * n,
        out_shape=[_sds((2 * a.shape[1], a.shape[2]), F32) for a in parts],
        scratch_shapes=[pltpu.VMEM((3,) + a.shape[1:], a.dtype) for a in parts]
        + [pltpu.SemaphoreType.DMA((n, 4)), pltpu.SemaphoreType.DMA((n, 4))],
        compiler_params=pltpu.CompilerParams(vmem_limit_bytes=VMEM_LIMIT),
    )(*parts)


def _small_allreduce(loss_p, dg_parts, dbg_a, dbg_c, dwc):
    ins = [loss_p] + list(dg_parts) + [dbg_a, dbg_c, dwc]
    n_in = len(ins)
    vmem = pl.BlockSpec(memory_space=pltpu.VMEM)

    def body(*refs):
        in_refs = refs[:n_in]
        out_ref, vec, buf, send_sems, recv_sems = refs[n_in:]
        x, y, c, _ = _place()
        me = 4 * x + 2 * y + c
        vec[...] = jnp.zeros_like(vec)
        vec[0:1, :] = jnp.sum(in_refs[0][...], axis=0)
        for r in range(5):
            vec[1 + r:2 + r, :] = jnp.sum(in_refs[1 + r][...], axis=0)
        vec[6:7, :] = jnp.sum(in_refs[6][...], axis=0)
        vec[7:8, :] = jnp.sum(in_refs[7][...], axis=0)
        vec[8:16, 0:CONV_W] = jnp.sum(in_refs[8][...], axis=0)
        buf[pl.ds(me, 1)] = vec[...][None]
        copies = []
        for r in range(1, 8):
            fx, fy, fc = (r >> 2) & 1, (r >> 1) & 1, r & 1
            to = (1 - x if fx else x, 1 - y if fy else y, 1 - c if fc else c)
            cp = pltpu.make_async_remote_copy(src_ref=vec, dst_ref=buf.at[me], send_sem=send_sems.at[r - 1],
                                              recv_sem=recv_sems.at[r - 1], device_id=to, device_id_type=MESH)
            cp.start()
            copies.append(cp)
        for cp in copies:
            cp.wait()
        total = buf[0]
        for s in range(1, 8):
            total = total + buf[s]
        out_ref[...] = total
        out_ref[0:1, :] = jnp.broadcast_to(jnp.sum(total[0:1, :], axis=-1, keepdims=True), (1, D_MODEL))

    return pl.pallas_call(
        body, name="small_allreduce",
        in_specs=[vmem] * n_in, out_specs=vmem, out_shape=_sds((SMALL_ROWS, D_MODEL), F32),
        scratch_shapes=[pltpu.VMEM((SMALL_ROWS, D_MODEL), F32), pltpu.VMEM((8, SMALL_ROWS, D_MODEL), F32),
                        pltpu.SemaphoreType.DMA((7,)), pltpu.SemaphoreType.DMA((7,))],
    )(*ins)


def _local_step(x, p, tgt, g, b_gate, w_conv, wf):
    seq = x.shape[0]
    tm = min(seq, 1024)
    th = min(seq, 512)
    tl = min(seq, 2048)
    ni, nh, nl = seq // tm, seq // th, seq // tl
    g_pre_mix, g_post_mix, g_pre_mlp, g_post_mlp, g_ple = g
    w_in, w_ao, w_co, w_o, w_up, w_down, w_pg, w_pp, w_in_nat, w_up_nat = wf
    D = D_MODEL
    vec = lambda a, blk=0: (a, _bs((1, D), lambda i, j, k: (0, blk)))
    rows_i = lambda a, t, blk=0: (a, _bs((t, D), lambda i, j, k: (i, blk)))
    rows_k = lambda a, t, blk=0: (a, _bs((t, D), lambda i, j, k: (k, blk)))
    part = lambda n: (_sds((n, 1, D), F32), _bs((None, 1, D), lambda i, j, k: (i, 0, 0)))
    full2 = lambda a: (a, _bs(a.shape, lambda i, j, k: (0, 0)))

    normed = lambda xb, gb: (_rms(xb, gb).astype(BF16),) * 2
    keep_a = lambda t: [(_sds((seq, D), BF16), _bs((t, D), lambda i, j, k: (i, 0)))]
    proj, h1 = _mm("proj_in", "nn", (ni, 4, 1),
                   a_ins=[rows_i(x, tm), vec(g_pre_mix)], a_fn=normed,
                   b_ins=[(w_in, _bs((None, D, 1280), lambda i, j, k: (j, 0, 0)))], b_fn=_ident,
                   outs=[(_sds((seq, D_IN), F32), _bs((tm, 1280), lambda i, j, k: (i, j)))],
                   acc_shape=(tm, 1280), a_cache=((tm, D), BF16), a_outs=keep_a(tm))
    o = _attn_fwd(proj, seq)
    (y_attn,) = _mm("attn_out", "nn", (ni, 1, 1),
                    a_ins=[(o, _bs((tm, ATTN_W), lambda i, j, k: (i, 0)))], a_fn=_ident,
                    b_ins=[full2(w_ao)], b_fn=_ident,
                    outs=[(_sds((seq, D), BF16), _bs((tm, D), lambda i, j, k: (i, 0)))], acc_shape=(tm, D))
    e, d = _conv_fwd(proj, w_conv, seq, tm)
    (y_conv,) = _mm("conv_out", "nn", (ni, 1, 1),
                    a_ins=[(e, _bs((tm, CONV_W), lambda i, j, k: (i, 0)))], a_fn=_ident,
                    b_ins=[full2(w_co)], b_fn=_ident,
                    outs=[(_sds((seq, D), BF16), _bs((tm, D), lambda i, j, k: (i, 0)))], acc_shape=(tm, D))

    def mix_fn(ga, gc, ya, yc, ba, bc):
        return ((_sig(ga + ba) * ya.astype(F32) + _sig(gc + bc) * yc.astype(F32)).astype(BF16),) * 2

    def post_mix(acc, xb, gb):
        return acc, xb + _rms(acc, gb)

    mix_ins = lambda rows: [rows(proj, th, 3), rows(proj, th, 4), rows(y_attn, th), rows(y_conv, th),
                            vec(b_gate, 0), vec(b_gate, 1)]
    mixed, x1, mixin = _mm("mix_out", "nn", (nh, 1, 1),
                           a_ins=mix_ins(rows_i), a_fn=mix_fn, b_ins=[full2(w_o)], b_fn=_ident,
                           epi_ins=[rows_i(x, th), vec(g_post_mix)], epi_fn=post_mix,
                           outs=[(_sds((seq, D), F32), _bs((th, D), lambda i, j, k: (i, 0)))] * 2,
                           acc_shape=(th, D), a_cache=((th, D), BF16), a_outs=keep_a(th))
    up, h2 = _mm("mlp_up", "nn", (nh, 1, 1),
                 a_ins=[rows_i(x1, th), vec(g_pre_mlp)], a_fn=normed,
                 b_ins=[full2(w_up_nat)], b_fn=_ident,
                 outs=[(_sds((seq, D_FF), BF16), _bs((th, D_FF), lambda i, j, k: (i, 0)))],
                 acc_shape=(th, D_FF), a_cache=((th, D), BF16), a_outs=keep_a(th))

    def relu2(ub):
        r = jnp.maximum(ub.astype(F32), 0.0)
        return (r * r).astype(BF16)

    f, x2 = _mm("mlp_down", "nn", (nh, 1, 1),
                a_ins=[(up, _bs((th, D_FF), lambda i, j, k: (i, 0)))], a_fn=relu2,
                b_ins=[full2(w_down)], b_fn=_ident,
                epi_ins=[rows_i(x1, th), vec(g_post_mlp)], epi_fn=post_mix,
                outs=[(_sds((seq, D), F32), _bs((th, D), lambda i, j, k: (i, 0)))] * 2, acc_shape=(th, D))
    (pp,) = _mm("ple_proj", "nn", (ni, 1, 1),
                a_ins=[(p, _bs((tm, PLE_DIM), lambda i, j, k: (i, 0)))], a_fn=_to_bf16,
                b_ins=[full2(w_pp)], b_fn=_ident,
                outs=[(_sds((seq, D), F32), _bs((tm, D), lambda i, j, k: (i, 0)))], acc_shape=(tm, D))

    def head(acc, x2b, ppb, tb):
        pg = _sig(acc)
        err = x2b + pg * ppb - tb
        return pg, err * (1.0 / D), jnp.sum(err * err, axis=0, keepdims=True) * (0.5 / D)

    pg, dx3, loss_p, h3 = _mm("ple_gate_loss", "nn", (nh, 1, 1),
                              a_ins=[rows_i(x2, th), vec(g_ple)], a_fn=normed,
                              b_ins=[full2(w_pg)], b_fn=_ident,
                              epi_a=(0,), epi_ins=[rows_i(pp, th), rows_i(tgt, th)], epi_fn=head,
                              outs=[(_sds((seq, D), BF16), _bs((th, D), lambda i, j, k: (i, 0))),
                                    (_sds((seq, D), F32), _bs((th, D), lambda i, j, k: (i, 0))), part(nh)],
                              acc_shape=(th, D), a_cache=((th, D), BF16), a_outs=keep_a(th))

    (dw_pp,) = _mm("dw_ple_proj", "tn", (1, 1, nh),
                   a_ins=[(p, _bs((th, PLE_DIM), lambda i, j, k: (k, 0)))], a_fn=_to_bf16,
                   b_ins=[rows_k(dx3, th), rows_k(pg, th)], b_fn=lambda a, b: (a * b.astype(F32)).astype(BF16),
                   outs=[(_sds((PLE_DIM, D), F32), _bs((PLE_DIM, D), lambda i, j, k: (0, 0)))],
                   acc_shape=(PLE_DIM, D))

    def dpre_fn(dx3b, ppb, pgb):
        pgf = pgb.astype(F32)
        return (dx3b * ppb * pgf * (1.0 - pgf)).astype(BF16)

    def ple_norm_bwd(acc, dx3b, x2b, gb, fb, g_mlp):
        dxn, dg = _rms_bwd(x2b, gb, acc)
        dx2b = dx3b + dxn
        dfb, dg_mlp = _rms_bwd(fb, g_mlp, dx2b)
        return dx2b, dg, dfb, dg_mlp

    dx2, dg_ple_p, df, dg_post_mlp_p, dpre = _mm(
        "d_ple_gate", "nt", (nh, 1, 1),
        a_ins=[rows_i(dx3, th), rows_i(pp, th), rows_i(pg, th)], a_fn=lambda a, b, c: (dpre_fn(a, b, c),) * 2,
        b_ins=[full2(w_pg)], b_fn=_ident,
        epi_a=(0,), epi_ins=[rows_i(x2, th), vec(g_ple), rows_i(f, th), vec(g_post_mlp)], epi_fn=ple_norm_bwd,
        outs=[(_sds((seq, D), F32), _bs((th, D), lambda i, j, k: (i, 0))), part(nh),
              (_sds((seq, D), BF16), _bs((th, D), lambda i, j, k: (i, 0))), part(nh)],
        acc_shape=(th, D), a_cache=((th, D), BF16),
        a_outs=[(_sds((seq, D), BF16), _bs((th, D), lambda i, j, k: (i, 0)))])
    (dw_pg,) = _mm("dw_ple_gate", "tn", (1, 1, ni),
                   a_ins=[rows_k(h3, tm)], a_fn=_ident, b_ins=[rows_k(dpre, tm)], b_fn=_ident,
                   outs=[(_sds((D, D), F32), _bs((D, D), lambda i, j, k: (0, 0)))], acc_shape=(D, D))

    def dup_fn(acc, ub):
        return (acc * (2.0 * jnp.maximum(ub.astype(F32), 0.0)),)

    (dup,) = _mm("d_mlp_down", "nt", (nh, 1, 1),
                 a_ins=[rows_i(df, th)], a_fn=_ident, b_ins=[full2(w_down)], b_fn=_ident,
                 epi_ins=[(up, _bs((th, D_FF), lambda i, j, k: (i, 0)))], epi_fn=dup_fn,
                 outs=[(_sds((seq, D_FF), BF16), _bs((th, D_FF), lambda i, j, k: (i, 0)))],
                 acc_shape=(th, D_FF))
    (dw_down,) = _mm("dw_mlp_down", "tn", (4, 1, nl),
                     a_ins=[(up, _bs((tl, D), lambda i, j, k: (k, i)))], a_fn=relu2,
                     b_ins=[rows_k(df, tl)], b_fn=_ident,
                     outs=[(_sds((D_FF, D), F32), _bs((D, D), lambda i, j, k: (i, 0)))], acc_shape=(D, D))
    (dw_up,) = _mm("dw_mlp_up", "tn", (1, 4, nl),
                   a_ins=[rows_k(h2, tl)], a_fn=_ident,
                   b_ins=[(dup, _bs((tl, D), lambda i, j, k: (k, j)))], b_fn=_ident,
                   outs=[(_sds((N_CHIPS, D, D), F32), _bs((None, D, D), lambda i, j, k: (j, 0, 0)))],
                   acc_shape=(D, D))

    def mlp_norm_bwd(acc, x1b, dx2b, mixedb, g_mlp, g_mix):
        dxn, dg_mlp = _rms_bwd(x1b, g_mlp, acc)
        dx1b = dx2b + dxn
        dmixedb, dg_mix = _rms_bwd(mixedb, g_mix, dx1b)
        return dx1b, dmixedb, dg_mlp, dg_mix

    dx1, dmixed, dg_pre_mlp_p, dg_post_mix_p = _mm(
        "d_mlp_up", "nt", (nh, 1, 1),
        a_ins=[(dup, _bs((th, D_FF), lambda i, j, k: (i, 0)))], a_fn=_ident,
        b_ins=[full2(w_up_nat)], b_fn=_ident,
        epi_ins=[rows_i(x1, th), rows_i(dx2, th), rows_i(mixed, th), vec(g_pre_mlp), vec(g_post_mix)],
        epi_fn=mlp_norm_bwd,
        outs=[(_sds((seq, D), F32), _bs((th, D), lambda i, j, k: (i, 0))),
              (_sds((seq, D), BF16), _bs((th, D), lambda i, j, k: (i, 0))), part(nh), part(nh)],
        acc_shape=(th, D))
    (dw_o,) = _mm("dw_mix_out", "tn", (1, 1, ni),
                  a_ins=[rows_k(mixin, tm)], a_fn=_ident, b_ins=[rows_k(dmixed, tm)], b_fn=_ident,
                  outs=[(_sds((D, D), F32), _bs((D, D), lambda i, j, k: (0, 0)))], acc_shape=(D, D))

    def gate_bwd(acc, ga, gc, ya, yc, ba, bc):
        sa, sc = _sig(ga + ba), _sig(gc + bc)
        dga = acc * ya.astype(F32) * sa * (1.0 - sa)
        dgc = acc * yc.astype(F32) * sc * (1.0 - sc)
        return (acc * sa, acc * sc, jnp.concatenate([dga, dgc], axis=1),
                jnp.sum(dga, axis=0, keepdims=True), jnp.sum(dgc, axis=0, keepdims=True))

    dya, dyc, dgate, dbg_a_p, dbg_c_p = _mm(
        "d_mix_out", "nt", (nh, 1, 1),
        a_ins=[rows_i(dmixed, th)], a_fn=_ident, b_ins=[full2(w_o)], b_fn=_ident,
        epi_ins=mix_ins(rows_i), epi_fn=gate_bwd,
        outs=[(_sds((seq, D), BF16), _bs((th, D), lambda i, j, k: (i, 0)))] * 2
             + [(_sds((seq, 2 * D), BF16), _bs((th, 2 * D), lambda i, j, k: (i, 0))), part(nh), part(nh)],
        acc_shape=(th, D))
    (dw_ao,) = _mm("dw_attn_out", "tn", (1, 1, nh),
                   a_ins=[(o, _bs((th, ATTN_W), lambda i, j, k: (k, 0)))], a_fn=_ident,
                   b_ins=[rows_k(dya, th)], b_fn=_ident,
                   outs=[(_sds((ATTN_W, D), F32), _bs((ATTN_W, D), lambda i, j, k: (0, 0)))], acc_shape=(ATTN_W, D))
    (do,) = _mm("d_attn_out", "nt", (ni, 1, 1),
                a_ins=[rows_i(dya, tm)], a_fn=_ident, b_ins=[full2(w_ao)], b_fn=_ident,
                outs=[(_sds((seq, ATTN_W), BF16), _bs((tm, ATTN_W), lambda i, j, k: (i, 0)))],
                acc_shape=(tm, ATTN_W))
    dq, dk, dv = _attn_bwd(proj, do, seq)
    (dw_co,) = _mm("dw_conv_out", "tn", (1, 1, nh),
                   a_ins=[(e, _bs((th, CONV_W), lambda i, j, k: (k, 0)))], a_fn=_ident,
                   b_ins=[rows_k(dyc, th)], b_fn=_ident,
                   outs=[(_sds((CONV_W, D), F32), _bs((CONV_W, D), lambda i, j, k: (0, 0)))], acc_shape=(CONV_W, D))
    (de,) = _mm("d_conv_out", "nt", (ni, 1, 1),
                a_ins=[rows_i(dyc, tm)], a_fn=_ident, b_ins=[full2(w_co)], b_fn=_ident,
                outs=[(_sds((seq, CONV_W), F32), _bs((tm, CONV_W), lambda i, j, k: (i, 0)))],
                acc_shape=(tm, CONV_W))
    dconv, dwc_p = _conv_bwd(proj, de, d, w_conv, seq, tm)
    qkv_w = 3 * ATTN_W
    join_bf16 = lambda *blocks: jnp.concatenate([b.astype(BF16) for b in blocks], axis=1)
    piece = lambda a, t, rows, blk=0: (a, _bs((t, a.shape[1]), (lambda i, j, k: (k, blk)) if rows == "k"
                                             else (lambda i, j, k: (i, blk))))
    (dw_in_qkv,) = _mm("dw_proj_in_qkv", "tn", (1, 1, ni),
                       a_ins=[rows_k(h1, tm)], a_fn=_ident,
                       b_ins=[piece(dq, tm, "k"), piece(dk, tm, "k"), piece(dv, tm, "k")], b_fn=join_bf16,
                       outs=[(_sds((D, qkv_w), F32), _bs((D, qkv_w), lambda i, j, k: (0, 0)))], acc_shape=(D, qkv_w))
    (dw_in_conv,) = _mm("dw_proj_in_conv", "tn", (1, 1, nl),
                        a_ins=[rows_k(h1, tl)], a_fn=_ident, b_ins=[piece(dconv, tl, "k")], b_fn=_ident,
                        outs=[(_sds((D, 3 * CONV_W), F32), _bs((D, 3 * CONV_W), lambda i, j, k: (0, 0)))],
                        acc_shape=(D, 3 * CONV_W))
    (dw_in_gate,) = _mm("dw_proj_in_gate", "tn", (1, 2, nl),
                        a_ins=[rows_k(h1, tl)], a_fn=_ident,
                        b_ins=[(dgate, _bs((tl, D), lambda i, j, k: (k, j)))], b_fn=_ident,
                        outs=[(_sds((D, 2 * D), F32), _bs((D, D), lambda i, j, k: (0, j)))], acc_shape=(D, D))
    dw_in = jnp.concatenate([dw_in_qkv, dw_in_conv, dw_in_gate], axis=1)

    def in_norm_bwd(acc, xb, dx1b, gb):
        dxn, dg = _rms_bwd(xb, gb, acc)
        return dx1b + dxn, dg

    grad_x, dg_pre_mix_p = _mm("d_proj_in", "nt", (nh, 1, 1),
                               a_ins=[piece(dq, th, "i"), piece(dk, th, "i"), piece(dv, th, "i"),
                                      piece(dconv, th, "i"), piece(dgate, th, "i")], a_fn=join_bf16,
                               b_ins=[full2(w_in_nat)], b_fn=_ident,
                               epi_ins=[rows_i(x, th), rows_i(dx1, th), vec(g_pre_mix)], epi_fn=in_norm_bwd,
                               outs=[(_sds((seq, D), F32), _bs((th, D), lambda i, j, k: (i, 0))), part(nh)],
                               acc_shape=(th, D))

    chip_major = lambda a: a.reshape(a.shape[0], N_CHIPS, a.shape[1] // N_CHIPS).transpose(1, 0, 2)
    big = [chip_major(dw_in), chip_major(dw_ao), chip_major(dw_co), dw_o.reshape(N_CHIPS, D // N_CHIPS, D), dw_up,
           dw_down.reshape(N_CHIPS, D_FF // N_CHIPS, D), dw_pg.reshape(N_CHIPS, D // N_CHIPS, D), chip_major(dw_pp)]
    small = (loss_p, [dg_pre_mix_p, dg_post_mix_p, dg_pre_mlp_p, dg_post_mlp_p, dg_ple_p], dbg_a_p, dbg_c_p, dwc_p)
    return grad_x, big, small


RS_GROUPS = ((0,), (4,), (5,), (1, 2, 3, 6, 7))


def _reduce_scatter(big):
    pair = [None] * len(big)
    for gi, group in enumerate(RS_GROUPS):
        for w, s in zip(group, _rs_pair_sum(f"rs_pair_sum_{gi}", [big[w] for w in group])):
            pair[w] = s
    return _rs_exchange_join(pair)


def kernel(x, p, g_pre_mix, w_in, b_gate, w_conv, w_attn_out, w_conv_out, w_o, g_post_mix, g_pre_mlp, w_up, w_down, g_post_mlp, g_ple, w_ple_gate, w_ple_proj, loss_target, m_g_pre_mix, m_w_in, m_b_gate, m_w_conv, m_w_attn_out, m_w_conv_out, m_w_o, m_g_post_mix, m_g_pre_mlp, m_w_up, m_w_down, m_g_post_mlp, m_g_ple, m_w_ple_gate, m_w_ple_proj, v_g_pre_mix, v_w_in, v_b_gate, v_w_conv, v_w_attn_out, v_w_conv_out, v_w_o, v_g_post_mix, v_g_pre_mlp, v_w_up, v_w_down, v_g_post_mlp, v_g_ple, v_w_ple_gate, v_w_ple_proj):
    mats = [w_in, w_attn_out, w_conv_out, w_o, w_up, w_down, w_ple_gate, w_ple_proj]
    mats_m = [m_w_in, m_w_attn_out, m_w_conv_out, m_w_o, m_w_up, m_w_down, m_w_ple_gate, m_w_ple_proj]
    mats_v = [v_w_in, v_w_attn_out, v_w_conv_out, v_w_o, v_w_up, v_w_down, v_w_ple_gate, v_w_ple_proj]
    gains = [g_pre_mix, g_post_mix, g_pre_mlp, g_post_mlp, g_ple]
    gains_m = [m_g_pre_mix, m_g_post_mix, m_g_pre_mlp, m_g_post_mlp, m_g_ple]
    gains_v = [v_g_pre_mix, v_g_post_mix, v_g_pre_mlp, v_g_post_mlp, v_g_ple]

    taps = jnp.concatenate([w_conv[0], jnp.zeros((CONV_PAD_ROWS - 3, LANES), F32)], axis=0)
    gathered = _allgather_weights([w[0].astype(BF16) for w in mats] + [taps])
    cols_joined = lambda a: a.transpose(1, 0, 2).reshape(a.shape[1], N_CHIPS * a.shape[2])
    rows_joined = lambda a: a.reshape(N_CHIPS * a.shape[1], a.shape[2])
    wf = [gathered[0], cols_joined(gathered[1]), cols_joined(gathered[2]), rows_joined(gathered[3]), gathered[4],
          rows_joined(gathered[5]), rows_joined(gathered[6]), cols_joined(gathered[7]),
          cols_joined(gathered[0]), cols_joined(gathered[4])]
    w_conv_full = cols_joined(gathered[8])[0:3, :]
    chip = 2 * lax.axis_index("x") + lax.axis_index("y")

    grad_x, big, small = _local_step(x[0], p[0, 0], loss_target[0], gains, b_gate, w_conv_full, wf)

    shard_grads = _reduce_scatter(big)
    red = _small_allreduce(*small)
    loss = red[0, 0]
    grad_gains = [red[1 + r:2 + r, :] for r in range(5)]
    grad_b_gate = jnp.concatenate([red[6:7, :], red[7:8, :]], axis=1)
    grad_w_conv = lax.dynamic_slice(red[8:11, :], (0, chip * LANES), (3, LANES))[None]

    grads_big = [gr.reshape(w.shape) for gr, w in zip(shard_grads, mats)]
    upd_big = [_adamw(f"adamw_{i}", w, gr, m, v) for i, (w, gr, m, v) in enumerate(zip(mats, grads_big, mats_m, mats_v))]
    pack = lambda vs, bg: jnp.concatenate(list(vs) + [bg.reshape(2, D_MODEL), jnp.zeros((1, D_MODEL), F32)], axis=0)
    upd_small = _adamw("adamw_small", pack(gains, b_gate), pack(grad_gains, grad_b_gate),
                       pack(gains_m, m_b_gate), pack(gains_v, v_b_gate))
    upd_conv = _adamw("adamw_conv", w_conv, grad_w_conv, m_w_conv, v_w_conv)

    def small_out(a, which):
        gains_out = [a[r:r + 1, :] for r in range(5)]
        return gains_out, a[5:7, :].reshape(1, 2 * D_MODEL)

    def ordered(g_pre_mix_, big_, b_gate_, conv_, g_rest):
        return [g_pre_mix_, big_[0], b_gate_, conv_, big_[1], big_[2], big_[3], g_rest[0], g_rest[1], big_[4], big_[5],
                g_rest[2], g_rest[3], big_[6], big_[7]]

    outs = [loss, grad_x[None]]
    outs += ordered(grad_gains[0], grads_big, grad_b_gate, grad_w_conv, grad_gains[1:])
    for which in range(3):
        g_out, b_out = small_out(upd_small[which], which)
        outs += ordered(g_out[0], [u[which] for u in upd_big], b_out, upd_conv[which], g_out[1:])
    return tuple(outs)
```

```python
import functools

import jax
import jax.numpy as jnp
from jax import lax
from jax.experimental import pallas as pl
from jax.experimental.pallas import tpu as pltpu

F32 = jnp.float32
BF16 = jnp.bfloat16
MESH = pl.DeviceIdType.MESH

D_MODEL = 1024
N_HEADS = 8
HEAD_DIM = 64
ATTN_W = N_HEADS * HEAD_DIM
CONV_W = 512
D_FF = 4096
PLE_DIM = 256
D_IN = 5120
N_CHIPS = 4
EPS = 1e-6
Q_SCALE = HEAD_DIM ** -0.5

ADAM_LR = 0.001
ADAM_B1 = 0.9
ADAM_B2 = 0.999
ADAM_EPS = 1e-08
ADAM_WD = 0.01
ADAM_STEP = 10

V7X_VMEM_BYTES = 64 * 1024 * 1024
VMEM_LIMIT = V7X_VMEM_BYTES - 8 * 1024 * 1024
LANES = 128
ATT_BLK = 256
SMALL_ROWS = 16
CONV_PAD_ROWS = 16


def _cparams(n_grid):
    return pltpu.CompilerParams(dimension_semantics=("arbitrary",) * n_grid, vmem_limit_bytes=VMEM_LIMIT)


def _bs(shape, fn):
    return pl.BlockSpec(shape, fn)


def _rms_stats(xf):
    return lax.rsqrt(jnp.mean(xf * xf, axis=-1, keepdims=True) + EPS)


def _rms(xf, g):
    return xf * _rms_stats(xf) * g


def _rms_bwd(xf, g, dy):
    r = _rms_stats(xf)
    xh = xf * r
    dyg = dy * g
    dx = r * (dyg - xh * jnp.mean(dyg * xh, axis=-1, keepdims=True))
    return dx, jnp.sum(dy * xh, axis=0, keepdims=True)


def _sig(z):
    return 1.0 / (1.0 + jnp.exp(-z))


def _ident(a):
    return a


def _to_bf16(a):
    return a.astype(BF16)


_DIMS = {"nn": (((1,), (0,)), ((), ())), "nt": (((1,), (1,)), ((), ())), "tn": (((0,), (0,)), ((), ()))}


def _mm(name, mode, grid, a_ins, a_fn, b_ins, b_fn, outs, acc_shape, epi_ins=(), epi_fn=None,
        a_cache=None, a_outs=(), epi_a=()):
    nk = grid[2]
    na, nb, ne, no, nao = len(a_ins), len(b_ins), len(epi_ins), len(outs), len(a_outs)
    assert a_cache is None or nk == 1
    assert not a_outs or a_cache is not None
    dims = _DIMS[mode]
    if epi_fn is None:
        epi_fn = lambda acc: (acc,)

    def body(*refs):
        a_refs = refs[:na]
        b_refs = refs[na:na + nb]
        e_refs = refs[na + nb:na + nb + ne]
        o_refs = refs[na + nb + ne:na + nb + ne + no]
        ao_refs = refs[na + nb + ne + no:na + nb + ne + no + nao]
        scratch = list(refs[na + nb + ne + no + nao:])
        acc_ref = scratch.pop(0) if nk > 1 else None
        a_sc = scratch.pop(0) if a_cache is not None else None
        j = pl.program_id(1)
        k = pl.program_id(2)

        def finish(acc):
            res = epi_fn(acc, *[a_refs[t][...] for t in epi_a], *[r[...] for r in e_refs])
            for r, val in zip(o_refs, res):
                r[...] = val.astype(r.dtype)

        if a_sc is not None:
            @pl.when(j == 0)
            def _():
                res = a_fn(*[r[...] for r in a_refs])
                if nao:
                    for r, val in zip(ao_refs, res[1:]):
                        r[...] = val.astype(r.dtype)
                    res = res[0]
                a_sc[...] = res
            a = a_sc[...]
        else:
            a = a_fn(*[r[...] for r in a_refs])
        b = b_fn(*[r[...] for r in b_refs])
        prod = lax.dot_general(a, b, dims, preferred_element_type=F32)
        if nk == 1:
            finish(prod)
        else:
            @pl.when(k == 0)
            def _():
                acc_ref[...] = prod

            @pl.when(k > 0)
            def _():
                acc_ref[...] += prod

            @pl.when(k == nk - 1)
            def _():
                finish(acc_ref[...])

    scratch_shapes = []
    if nk > 1:
        scratch_shapes.append(pltpu.VMEM(acc_shape, F32))
    if a_cache is not None:
        scratch_shapes.append(pltpu.VMEM(*a_cache))
    all_outs = list(outs) + list(a_outs)
    res = pl.pallas_call(
        body, name=name, grid=grid,
        in_specs=[s for _, s in a_ins] + [s for _, s in b_ins] + [s for _, s in epi_ins],
        out_specs=[s for _, s in all_outs],
        out_shape=[o for o, _ in all_outs],
        scratch_shapes=scratch_shapes,
        compiler_params=_cparams(3),
    )(*[a for a, _ in a_ins], *[a for a, _ in b_ins], *[a for a, _ in epi_ins])
    return res


def _sds(shape, dtype):
    return jax.ShapeDtypeStruct(shape, dtype)


def _shift_rows_down(u, prev, n):
    rows = u.shape[0]
    ridx = lax.broadcasted_iota(jnp.int32, u.shape, 0)
    out = pltpu.roll(u, n, 0)
    for r in range(n):
        out = jnp.where(ridx == r, prev[8 - n + r:8 - n + r + 1, :], out)
    del rows
    return out


def _shift_rows_up(u, nxt, n):
    rows = u.shape[0]
    ridx = lax.broadcasted_iota(jnp.int32, u.shape, 0)
    out = pltpu.roll(u, rows - n, 0)
    for r in range(n):
        out = jnp.where(ridx == rows - n + r, nxt[r:r + 1, :], out)
    return out


CONV_COL0 = 3


def _conv_fwd(proj, w_conv, seq, tr):
    hb = tr // 8

    def body(cb_ref, cc_ref, cu_ref, ccp_ref, cup_ref, w_ref, e_ref, d_ref):
        i = pl.program_id(0)
        u = cc_ref[...] * cu_ref[...]
        up = jnp.where(i > 0, ccp_ref[...] * cup_ref[...], 0.0)
        w = w_ref[...]
        d = w[0:1, :] * _shift_rows_down(u, up, 2) + w[1:2, :] * _shift_rows_down(u, up, 1) + w[2:3, :] * u
        d_ref[...] = d
        e_ref[...] = (cb_ref[...] * d).astype(BF16)

    prev = lambda c: (lambda i: (jnp.maximum(i * hb - 1, 0), c))
    return pl.pallas_call(
        body, name="conv_fwd", grid=(seq // tr,),
        in_specs=[_bs((tr, CONV_W), lambda i: (i, CONV_COL0)),
                  _bs((tr, CONV_W), lambda i: (i, CONV_COL0 + 1)),
                  _bs((tr, CONV_W), lambda i: (i, CONV_COL0 + 2)),
                  _bs((8, CONV_W), prev(CONV_COL0 + 1)),
                  _bs((8, CONV_W), prev(CONV_COL0 + 2)),
                  _bs((3, CONV_W), lambda i: (0, 0))],
        out_specs=[_bs((tr, CONV_W), lambda i: (i, 0)), _bs((tr, CONV_W), lambda i: (i, 0))],
        out_shape=[_sds((seq, CONV_W), BF16), _sds((seq, CONV_W), F32)],
        compiler_params=_cparams(1),
    )(proj, proj, proj, proj, proj, w_conv)


def _conv_bwd(proj, de, d, w_conv, seq, tr):
    hb = tr // 8
    nblk = seq // tr

    def body(cb_ref, cc_ref, cu_ref, ccp_ref, cup_ref, cbn_ref, de_ref, den_ref, d_ref, w_ref, o_ref, dw_ref):
        i = pl.program_id(0)
        cc, cu, cb = cc_ref[...], cu_ref[...], cb_ref[...]
        u = cc * cu
        up = jnp.where(i > 0, ccp_ref[...] * cup_ref[...], 0.0)
        u1 = _shift_rows_down(u, up, 1)
        u2 = _shift_rows_down(u, up, 2)
        de_ = de_ref[...]
        dd = de_ * cb
        ddn = jnp.where(i < nblk - 1, den_ref[...] * cbn_ref[...], 0.0)
        w = w_ref[...]
        du = w[2:3, :] * dd + w[1:2, :] * _shift_rows_up(dd, ddn, 1) + w[0:1, :] * _shift_rows_up(dd, ddn, 2)
        o_ref[:, 0:CONV_W] = (de_ * d_ref[...]).astype(BF16)
        o_ref[:, CONV_W:2 * CONV_W] = (du * cu).astype(BF16)
        o_ref[:, 2 * CONV_W:3 * CONV_W] = (du * cc).astype(BF16)
        ridx = lax.broadcasted_iota(jnp.int32, (8, CONV_W), 0)
        dw0 = jnp.sum(dd * u2, axis=0, keepdims=True)
        dw1 = jnp.sum(dd * u1, axis=0, keepdims=True)
        dw2 = jnp.sum(dd * u, axis=0, keepdims=True)
        dw_ref[...] = jnp.where(ridx == 0, dw0, jnp.where(ridx == 1, dw1, jnp.where(ridx == 2, dw2, 0.0)))

    prev = lambda c: (lambda i: (jnp.maximum(i * hb - 1, 0), c))
    nxt = lambda c: (lambda i: (jnp.minimum((i + 1) * hb, seq // 8 - 1), c))
    return pl.pallas_call(
        body, name="conv_bwd", grid=(nblk,),
        in_specs=[_bs((tr, CONV_W), lambda i: (i, CONV_COL0)),
                  _bs((tr, CONV_W), lambda i: (i, CONV_COL0 + 1)),
                  _bs((tr, CONV_W), lambda i: (i, CONV_COL0 + 2)),
                  _bs((8, CONV_W), prev(CONV_COL0 + 1)),
                  _bs((8, CONV_W), prev(CONV_COL0 + 2)),
                  _bs((8, CONV_W), nxt(CONV_COL0)),
                  _bs((tr, CONV_W), lambda i: (i, 0)),
                  _bs((8, CONV_W), nxt(0)),
                  _bs((tr, CONV_W), lambda i: (i, 0)),
                  _bs((3, CONV_W), lambda i: (0, 0))],
        out_specs=[_bs((tr, 3 * CONV_W), lambda i: (i, 0)), _bs((None, 8, CONV_W), lambda i: (i, 0, 0))],
        out_shape=[_sds((seq, 3 * CONV_W), BF16), _sds((nblk, 8, CONV_W), F32)],
        compiler_params=_cparams(1),
    )(proj, proj, proj, proj, proj, proj, de, de, d, w_conv)


def _nt(a, b):
    return lax.dot_general(a, b, _DIMS["nt"], preferred_element_type=F32)


def _tn(a, b):
    return lax.dot_general(a, b, _DIMS["tn"], preferred_element_type=F32)


def _nn(a, b):
    return lax.dot_general(a, b, _DIMS["nn"], preferred_element_type=F32)


def _log_gates(z):
    lse = jnp.log(1.0 + jnp.exp(-jnp.abs(z)))
    log_beta = jnp.minimum(z, 0.0) - lse
    return log_beta, log_beta - z


DEAD_LOG_WEIGHT = -110.0
NO_TILE = -1e30


def _first_live_tile(start, scores, live_sc):
    def alive():
        return jnp.max(jnp.maximum(live_sc[0], live_sc[1])) > DEAD_LOG_WEIGHT

    def step(c):
        for h, z in enumerate(scores(c[0])):
            live_sc[h] = live_sc[h] + jnp.sum(_log_gates(z)[1], axis=-1, keepdims=True)
        return c[0] - 1, alive()

    j_end, _ = lax.while_loop(lambda c: jnp.logical_and(c[0] >= 0, c[1]), step, (start, alive()))
    return j_end + 1


def _attn_fwd(proj, seq):
    blk = ATT_BLK
    nq = seq // blk
    npair = N_HEADS // 2

    def body(q_ref, k_ref, v_ref, o_ref, z0_sc, z1_sc, w0_sc, w1_sc, tot_sc, live_sc, acc_sc):
        i = pl.program_id(1)
        is_a = lax.broadcasted_iota(jnp.int32, (1, LANES), 1) < HEAD_DIM
        q2 = (q_ref[...] * Q_SCALE).astype(BF16)
        zero = jnp.zeros_like(q2)
        qs = (jnp.where(is_a, q2, zero), jnp.where(is_a, zero, q2))
        row = lax.broadcasted_iota(jnp.int32, (blk, blk), 0)
        col = lax.broadcasted_iota(jnp.int32, (blk, blk), 1)
        tri = (row > col).astype(BF16)
        causal = col < row

        def tile_of(ref, j):
            return ref[pl.ds(pl.multiple_of(j * blk, blk), blk), :].astype(BF16)

        def scores(j):
            k2 = tile_of(k_ref, j)
            return [_nt(qs[h], k2) for h in range(2)]

        has_left = i > 0
        left = jnp.maximum(i - 1, 0)
        g_d = [_log_gates(z) for z in scores(i)]
        g_l = [_log_gates(z) for z in scores(left)]
        keep_d = [jnp.where(causal, g[1], 0.0) for g in g_d]
        suf_d = [_nn(lk.astype(BF16), tri) for lk in keep_d]
        suf_l = [_nn(g[1].astype(BF16), tri) for g in g_l]
        v_d, v_l = tile_of(v_ref, i), tile_of(v_ref, left)
        pv = []
        for h in range(2):
            sum_d = jnp.sum(keep_d[h], axis=-1, keepdims=True)
            w_d = jnp.where(causal, jnp.exp(g_d[h][0] + suf_d[h]), 0.0)
            w_l = jnp.exp(g_l[h][0] + (jnp.where(has_left, sum_d, NO_TILE) + suf_l[h]))
            pv.append(_nn(w_d.astype(BF16), v_d) + _nn(w_l.astype(BF16), v_l))
            tot_sc[h] = sum_d + jnp.sum(g_l[h][1], axis=-1, keepdims=True)
        acc_sc[...] = jnp.where(is_a, pv[0], pv[1])

        live_sc[...] = tot_sc[...]
        first = _first_live_tile(i - 2, scores, live_sc)
        trips = i - 1 - first
        z_bufs, w_bufs = (z0_sc, z1_sc), (w0_sc, w1_sc)

        def put(ref, vals):
            for h in range(2):
                ref[h] = vals[h]

        def weights(zs):
            gates = [_log_gates(z) for z in zs]
            sums = [_nn(g[1].astype(BF16), tri) for g in gates]
            ws = []
            for h in range(2):
                ws.append(jnp.exp(gates[h][0] + (tot_sc[h] + sums[h])).astype(BF16))
                tot_sc[h] = tot_sc[h] + jnp.sum(gates[h][1], axis=-1, keepdims=True)
            return ws

        def add_values(w_buf, j):
            v2 = tile_of(v_ref, j)
            acc_sc[...] += jnp.where(is_a, _nn(w_buf[0], v2), _nn(w_buf[1], v2))

        def trip(j, s):
            add_values(w_bufs[s], j + 1)
            put(z_bufs[1 - s], scores(jnp.maximum(j - 1, first)))
            put(w_bufs[1 - s], weights((z_bufs[s][0], z_bufs[s][1])))

        @pl.when(trips > 0)
        def _():
            put(z0_sc, scores(i - 2))
            w0_sc[...] = jnp.zeros_like(w0_sc)

            def two_trips(pp, carry):
                j = i - 2 - 2 * pp
                trip(j, 0)
                trip(j - 1, 1)
                return carry

            lax.fori_loop(0, trips // 2, two_trips, 0)
            odd = trips % 2 == 1

            @pl.when(odd)
            def _():
                trip(first, 0)
                add_values(w1_sc, first)

            @pl.when(jnp.logical_not(odd))
            def _():
                add_values(w0_sc, first)

        o_ref[...] = acc_sc[...].astype(BF16)

    return pl.pallas_call(
        body, name="attn_fwd", grid=(npair, nq),
        in_specs=[_bs((blk, LANES), lambda p, i: (i, p)),
                  _bs((seq, LANES), lambda p, i: (0, npair + p)),
                  _bs((seq, LANES), lambda p, i: (0, 2 * npair + p))],
        out_specs=_bs((blk, LANES), lambda p, i: (i, p)),
        out_shape=_sds((seq, ATTN_W), BF16),
        scratch_shapes=[pltpu.VMEM((2, blk, blk), F32), pltpu.VMEM((2, blk, blk), F32),
                        pltpu.VMEM((2, blk, blk), BF16), pltpu.VMEM((2, blk, blk), BF16),
                        pltpu.VMEM((2, blk, 1), F32), pltpu.VMEM((2, blk, 1), F32), pltpu.VMEM((blk, LANES), F32)],
        compiler_params=_cparams(2),
    )(proj, proj, proj)


def _attn_bwd(proj, do, seq):
    blk = ATT_BLK
    nq = seq // blk
    npair = N_HEADS // 2

    def body(q_ref, k_ref, v_ref, do_ref, dq_ref, dk_ref, dv_ref,
             prod0_sc, prod1_sc, pend0_sc, pend1_sc, tot_sc, live_sc, cum_sc, pre_sc, dq_sc):
        i = pl.program_id(1)

        @pl.when(i == 0)
        def _():
            dk_ref[...] = jnp.zeros_like(dk_ref)
            dv_ref[...] = jnp.zeros_like(dv_ref)

        is_a = lax.broadcasted_iota(jnp.int32, (1, LANES), 1) < HEAD_DIM
        q2 = (q_ref[...] * Q_SCALE).astype(BF16)
        do2 = do_ref[...]
        zero = jnp.zeros_like(q2)
        qs = (jnp.where(is_a, q2, zero), jnp.where(is_a, zero, q2))
        dos = (jnp.where(is_a, do2, zero), jnp.where(is_a, zero, do2))
        row = lax.broadcasted_iota(jnp.int32, (blk, blk), 0)
        col = lax.broadcasted_iota(jnp.int32, (blk, blk), 1)
        tri_after = (row > col).astype(BF16)
        tri_excl = (row < col).astype(BF16)
        causal = col < row

        def tile_of(ref, j):
            return ref[pl.ds(pl.multiple_of(j * blk, blk), blk), :].astype(BF16)

        def scores(j):
            k2 = tile_of(k_ref, j)
            return [_nt(qs[h], k2) for h in range(2)]

        def products(j):
            v2 = tile_of(v_ref, j)
            return scores(j) + [_nt(dos[h], v2) for h in range(2)]

        def row_sum(a):
            return jnp.sum(a, axis=-1, keepdims=True)

        def grad_matmuls(ws, dzs, j):
            rows = pl.ds(pl.multiple_of(j * blk, blk), blk)
            k2 = tile_of(k_ref, j)
            dq_sc[...] += jnp.where(is_a, _nn(dzs[0], k2), _nn(dzs[1], k2))
            dk_ref[rows, :] += jnp.where(is_a, _tn(dzs[0], q2), _tn(dzs[1], q2))
            if ws is not None:
                dv_ref[rows, :] += jnp.where(is_a, _tn(ws[0], do2), _tn(ws[1], do2))

        has_left = i > 0
        left = jnp.maximum(i - 1, 0)
        p_d, p_l = products(i), products(left)
        g_d = [_log_gates(z) for z in p_d[:2]]
        g_l = [_log_gates(z) for z in p_l[:2]]
        keep_d = [jnp.where(causal, g[1], 0.0) for g in g_d]
        suf_d = [_nn(lk.astype(BF16), tri_after) for lk in keep_d]
        suf_l = [_nn(g[1].astype(BF16), tri_after) for g in g_l]
        w_d, w_l, gg_d, gg_l = [], [], [], []
        for h in range(2):
            sum_d = row_sum(keep_d[h])
            w_d.append(jnp.where(causal, jnp.exp(g_d[h][0] + suf_d[h]), 0.0))
            w_l.append(jnp.exp(g_l[h][0] + (jnp.where(has_left, sum_d, NO_TILE) + suf_l[h])))
            gg_d.append(p_d[2 + h] * w_d[h])
            gg_l.append(p_l[2 + h] * w_l[h])
            tot_sc[h] = sum_d + row_sum(g_l[h][1])
        before_d = [_nn(g.astype(BF16), tri_excl) for g in gg_d]
        before_l = [_nn(g.astype(BF16), tri_excl) for g in gg_l]
        dz_d, dz_l = [], []
        for h in range(2):
            beta_d, beta_l = jnp.exp(g_d[h][0]), jnp.exp(g_l[h][0])
            dz_l.append((gg_l[h] * (1.0 - beta_l) - before_l[h] * beta_l).astype(BF16))
            dz = gg_d[h] * (1.0 - beta_d) - (row_sum(gg_l[h]) + before_d[h]) * beta_d
            dz_d.append(jnp.where(causal, dz, 0.0).astype(BF16))
        dq_sc[...] = jnp.zeros_like(dq_sc)
        grad_matmuls([w.astype(BF16) for w in w_l], dz_l, left)
        grad_matmuls([w.astype(BF16) for w in w_d], dz_d, i)

        live_sc[...] = tot_sc[...]
        first = _first_live_tile(i - 2, scores, live_sc)
        trips = i - 1 - first
        prod_bufs, pend_bufs = (prod0_sc, prod1_sc), (pend0_sc, pend1_sc)

        def local_grads(prods):
            zs, dws = prods[:2], prods[2:]
            gates = [_log_gates(z) for z in zs]
            sums = [_nn(g[1].astype(BF16), tri_after) for g in gates]
            ws, gs = [], []
            for h in range(2):
                cum = cum_sc[h] + row_sum(gates[h][1])
                cum_sc[h] = cum
                ws.append(jnp.exp(gates[h][0] + ((live_sc[h] - cum) + sums[h])))
                gs.append(dws[h] * ws[h])
            befores = [_nn(g.astype(BF16), tri_excl) for g in gs]
            dzs = []
            for h in range(2):
                beta = jnp.exp(gates[h][0])
                dzs.append((gs[h] * (1.0 - beta) - (pre_sc[h] + befores[h]) * beta).astype(BF16))
                pre_sc[h] = pre_sc[h] + row_sum(gs[h])
            return [w.astype(BF16) for w in ws] + dzs

        def put(ref, vals):
            for n, val in enumerate(vals):
                ref[n] = val

        def flush(pend, j):
            grad_matmuls([pend[0], pend[1]], [pend[2], pend[3]], j)

        def trip(j, s):
            flush(pend_bufs[s], jnp.maximum(j - 1, first))
            put(prod_bufs[1 - s], products(j + 1))
            put(pend_bufs[1 - s], local_grads([prod_bufs[s][n] for n in range(4)]))

        def earlier_keys_share(j, mask):
            dzs = []
            for h, z in enumerate(scores(j)):
                beta = jnp.exp(_log_gates(z)[0])
                dzs.append(jnp.where(mask, -pre_sc[h] * beta, 0.0).astype(BF16))
            grad_matmuls(None, dzs, j)

        @pl.when(trips > 0)
        def _():
            cum_sc[...] = jnp.zeros_like(cum_sc)
            pre_sc[...] = jnp.zeros_like(pre_sc)
            pend0_sc[...] = jnp.zeros_like(pend0_sc)
            put(prod0_sc, products(first))

            def two_trips(pp, carry):
                trip(first + 2 * pp, 0)
                trip(first + 2 * pp + 1, 1)
                return carry

            lax.fori_loop(0, trips // 2, two_trips, 0)
            odd = trips % 2 == 1

            @pl.when(odd)
            def _():
                trip(i - 2, 0)
                flush(pend1_sc, i - 2)

            @pl.when(jnp.logical_not(odd))
            def _():
                flush(pend0_sc, i - 2)

            earlier_keys_share(i - 1, True)
            earlier_keys_share(i, causal)

        dq_ref[...] = dq_sc[...] * Q_SCALE

    qmap = lambda p, i: (i, p)
    return pl.pallas_call(
        body, name="attn_bwd", grid=(npair, nq),
        in_specs=[_bs((blk, LANES), qmap),
                  _bs((seq, LANES), lambda p, i: (0, npair + p)),
                  _bs((seq, LANES), lambda p, i: (0, 2 * npair + p)),
                  _bs((blk, LANES), qmap)],
        out_specs=[_bs((blk, LANES), qmap),
                   _bs((seq, LANES), lambda p, i: (0, p)),
                   _bs((seq, LANES), lambda p, i: (0, p))],
        out_shape=[_sds((seq, ATTN_W), F32)] * 3,
        scratch_shapes=[pltpu.VMEM((4, blk, blk), F32), pltpu.VMEM((4, blk, blk), F32),
                        pltpu.VMEM((4, blk, blk), BF16), pltpu.VMEM((4, blk, blk), BF16),
                        pltpu.VMEM((2, blk, 1), F32), pltpu.VMEM((2, blk, 1), F32), pltpu.VMEM((2, blk, 1), F32),
                        pltpu.VMEM((2, blk, 1), F32), pltpu.VMEM((blk, LANES), F32)],
        compiler_params=_cparams(2),
    )(proj, proj, proj, do)


def _elementwise(name, fn, ins, out_dtypes):
    rows, cols = ins[0].shape
    tr = rows
    for cand in (512, 256, 128, 64, 32, 16, 8):
        if rows % cand == 0 and cand * cols * 4 <= 2 * 1024 * 1024:
            tr = cand
            break
    n_in = len(ins)

    def body(*refs):
        res = fn(*[r[...] for r in refs[:n_in]])
        for r, val in zip(refs[n_in:], res):
            r[...] = val.astype(r.dtype)

    spec = _bs((tr, cols), lambda i: (i, 0))
    return pl.pallas_call(
        body, name=name, grid=(rows // tr,),
        in_specs=[spec] * n_in, out_specs=[spec] * len(out_dtypes),
        out_shape=[_sds((rows, cols), dt) for dt in out_dtypes],
        compiler_params=_cparams(1),
    )(*ins)


def _adamw_fn(w, g, m, v):
    m = ADAM_B1 * m + (1.0 - ADAM_B1) * g
    v = ADAM_B2 * v + (1.0 - ADAM_B2) * (g * g)
    m_hat = m / (1.0 - ADAM_B1 ** ADAM_STEP)
    v_hat = v / (1.0 - ADAM_B2 ** ADAM_STEP)
    delta = -ADAM_LR * (m_hat / (jnp.sqrt(v_hat) + ADAM_EPS) + ADAM_WD * w)
    return delta, m, v


def _adamw(name, w, g, m, v):
    shape = w.shape
    as2d = lambda a: a.reshape(-1, shape[-1])
    delta, nm, nv = _elementwise(name, _adamw_fn, [as2d(w), as2d(g), as2d(m), as2d(v)], [F32, F32, F32])
    return delta.reshape(shape), nm.reshape(shape), nv.reshape(shape)


def _place():
    x, y, c = lax.axis_index("x"), lax.axis_index("y"), lax.axis_index("c")
    chips = [(1 - x, y), (x, 1 - y), (1 - x, 1 - y)]
    return x, y, c, chips


ANY = pl.BlockSpec(memory_space=pl.ANY)
VMEM_WHOLE = pl.BlockSpec(memory_space=pltpu.VMEM)


def _allgather_weights(shards):
    n = len(shards)

    def body(*refs):
        src, dst = refs[:n], refs[n:2 * n]
        send_sems, recv_sems, local_sems = refs[2 * n:]
        x, y, c, chips = _place()
        me, sibling, mychip = (x, y, c), (x, y, 1 - c), 2 * x + y

        x_nbr, y_nbr, diag = 2 * (1 - x) + y, 2 * x + (1 - y), 2 * (1 - x) + (1 - y)
        to_x, to_y = (1 - x, y, c), (x, 1 - y, c)

        def parts(w):
            hr = src[w].shape[0] // 2
            first = hr // 2 if hr % 32 == 0 else hr
            return first, hr - first

        def rows_of(w, chip, half, route):
            hr = src[w].shape[0] // 2
            first, second = parts(w)
            start, size = {0: (0, hr), 1: (0, hr), 2: (0, first), 3: (first, second)}[route]
            return dst[w].at[chip, pl.ds(half * hr + start, size)]

        def copy(w, k, src_ref, dst_ref, to):
            return pltpu.make_async_remote_copy(src_ref=src_ref, dst_ref=dst_ref, send_sem=send_sems.at[w, k],
                                                recv_sem=recv_sems.at[w, k], device_id=to, device_id_type=MESH)

        def landed(w, route):
            chip = {0: x_nbr, 1: y_nbr, 2: diag, 3: diag}[route]
            return rows_of(w, chip, c, route), chip

        def routes(w):
            return (0, 1, 2, 3) if parts(w)[1] else (0, 1, 2)

        started, local = [], []
        for w in range(n):
            hr = src[w].shape[0] // 2
            own = pltpu.make_async_copy(src[w], dst[w].at[mychip], local_sems.at[w])
            own.start()
            local.append(own)
            mine = src[w].at[pl.ds(c * hr, hr)]
            for route, to in ((0, to_x), (1, to_y)):
                cp = copy(w, route, mine, rows_of(w, mychip, c, route), to)
                cp.start()
                started.append(cp)

        def pass_on(w, route):
            got, chip = landed(w, route)
            copy(w, route, got, got, me).wait_recv()
            if route == 1:
                part = rows_of(w, chip, c, 2)
                started.append(copy(w, 2, part, part, to_x))
                started[-1].start()
            if route == 0 and parts(w)[1]:
                part = rows_of(w, chip, c, 3)
                started.append(copy(w, 3, part, part, to_y))
                started[-1].start()
            started.append(copy(w, 4 + route, got, got, sibling))
            started[-1].start()

        for w in range(n):
            pass_on(w, 1)
            pass_on(w, 0)
        for w in range(n):
            for route in routes(w)[2:]:
                pass_on(w, route)
        for w in range(n):
            for route in routes(w):
                chip = landed(w, route)[1]
                from_sib = rows_of(w, chip, 1 - c, route)
                copy(w, 4 + route, from_sib, from_sib, me).wait_recv()
        for cp in local:
            cp.wait()
        for cp in started:
            cp.wait_send()

    return pl.pallas_call(
        body, name="allgather_weights",
        in_specs=[VMEM_WHOLE] * n, out_specs=[VMEM_WHOLE] * n,
        out_shape=[_sds((N_CHIPS,) + s.shape, s.dtype) for s in shards],
        scratch_shapes=[pltpu.SemaphoreType.DMA((n, 8)), pltpu.SemaphoreType.DMA((n, 8)),
                        pltpu.SemaphoreType.DMA((n,))],
        compiler_params=pltpu.CompilerParams(vmem_limit_bytes=VMEM_LIMIT),
    )(*shards)


SUM_ROWS = 64


def _rs_pair_sum(name, grads):
    n = len(grads)

    def body(*refs):
        g, out = refs[:n], refs[n:2 * n]
        stage, land, keep = refs[2 * n:3 * n], refs[3 * n:4 * n], refs[4 * n:5 * n]
        send_sems, recv_sems, stage_sems, keep_sems = refs[5 * n:]
        x, y, c, _ = _place()
        sibling = (x, y, 1 - c)
        loads = []
        for w in range(n):
            hr = g[w].shape[1] // 2
            st = pltpu.make_async_copy(g[w].at[:, pl.ds((1 - c) * hr, hr)], stage[w], stage_sems.at[w])
            kp = pltpu.make_async_copy(g[w].at[:, pl.ds(c * hr, hr)], keep[w], keep_sems.at[w])
            st.start()
            kp.start()
            loads.append((st, kp))
        gives = []
        for w in range(n):
            loads[w][0].wait()
            give = pltpu.make_async_remote_copy(src_ref=stage[w], dst_ref=land[w], send_sem=send_sems.at[w],
                                                recv_sem=recv_sems.at[w], device_id=sibling, device_id_type=MESH)
            give.start()
            gives.append(give)
        for w in range(n):
            loads[w][1].wait()
            gives[w].wait_recv()
            nb = g[w].shape[1] // 2 // SUM_ROWS

            def add(idx, carry, w=w, nb=nb):
                k, r = idx // nb, pl.multiple_of((idx % nb) * SUM_ROWS, SUM_ROWS)
                rows = pl.ds(r, SUM_ROWS)
                out[w][k, rows, :] = (keep[w][k, rows, :] + land[w][k, rows, :]).astype(BF16)
                return carry

            lax.fori_loop(0, N_CHIPS * nb, add, 0)
        for give in gives:
            give.wait_send()

    half = [(N_CHIPS, a.shape[1] // 2, a.shape[2]) for a in grads]
    bufs = [pltpu.VMEM(s, F32) for s in half]
    sems = pltpu.SemaphoreType.DMA((n,))
    return pl.pallas_call(
        body, name=name,
        in_specs=[ANY] * n, out_specs=[VMEM_WHOLE] * n, out_shape=[_sds(s, BF16) for s in half],
        scratch_shapes=bufs + bufs + bufs + [sems, sems, sems, sems],
        compiler_params=pltpu.CompilerParams(vmem_limit_bytes=VMEM_LIMIT),
    )(*grads)


def _rs_exchange_join(parts):
    n = len(parts)

    def body(*refs):
        t, full, got = refs[:n], refs[n:2 * n], refs[2 * n:3 * n]
        send_sems, recv_sems = refs[3 * n:]
        x, y, c, chips = _place()
        mychip, sibling = 2 * x + y, (x, y, 1 - c)
        sends = []
        for w in range(n):
            for r, (cx, cy) in enumerate(chips):
                cp = pltpu.make_async_remote_copy(src_ref=t[w].at[2 * cx + cy], dst_ref=got[w].at[r],
                                                  send_sem=send_sems.at[w, r], recv_sem=recv_sems.at[w, r],
                                                  device_id=(cx, cy, c), device_id_type=MESH)
                cp.start()
                sends.append(cp)
        for w in range(n):
            hr = t[w].shape[1]
            for r in range(3):
                pltpu.make_async_remote_copy(src_ref=got[w].at[r], dst_ref=got[w].at[r], send_sem=send_sems.at[w, r],
                                             recv_sem=recv_sems.at[w, r], device_id=sibling,
                                             device_id_type=MESH).wait_recv()

            def add(idx, carry, w=w, hr=hr):
                r = pl.multiple_of(idx * SUM_ROWS, SUM_ROWS)
                rows = pl.ds(r, SUM_ROWS)
                f = lambda v: v.astype(F32)
                total = ((f(t[w][mychip, rows, :]) + f(got[w][0, rows, :])) + f(got[w][1, rows, :])) \
                    + f(got[w][2, rows, :])
                full[w][pl.ds(pl.multiple_of(c * hr + r, SUM_ROWS), SUM_ROWS), :] = total
                return carry

            lax.fori_loop(0, hr // SUM_ROWS, add, 0)
            mine = full[w].at[pl.ds(c * hr, hr)]
            give = pltpu.make_async_remote_copy(src_ref=mine, dst_ref=mine, send_sem=send_sems.at[w, 3],
                                                recv_sem=recv_sems.at[w, 3], device_id=sibling, device_id_type=MESH)
            give.start()
            sends.append(give)
        for w in range(n):
            hr = t[w].shape[1]
            theirs = full[w].at[pl.ds((1 - c) * hr, hr)]
            pltpu.make_async_remote_copy(src_ref=theirs, dst_ref=theirs, send_sem=send_sems.at[w, 3],
                                         recv_sem=recv_sems.at[w, 3], device_id=sibling, device_id_type=MESH).wait_recv()
        for cp in sends:
            cp.wait_send()

    return pl.pallas_call(
        body, name="rs_exchange_join",
        in_specs=[VMEM_WHOLE] * n, out_specs=[VMEM_WHOLE] * n,
        out_shape=[_sds((2 * a.shape[1], a.shape[2]), F32) for a in parts],
        scratch_shapes=[pltpu.VMEM((3,) + a.shape[1:], a.dtype) for a in parts]
        + [pltpu.SemaphoreType.DMA((n, 4)), pltpu.SemaphoreType.DMA((n, 4))],
        compiler_params=pltpu.CompilerParams(vmem_limit_bytes=VMEM_LIMIT),
    )(*parts)


def _small_allreduce(loss_p, dg_parts, dbg_a, dbg_c, dwc):
    ins = [loss_p] + list(dg_parts) + [dbg_a, dbg_c, dwc]
    n_in = len(ins)
    vmem = pl.BlockSpec(memory_space=pltpu.VMEM)

    def body(*refs):
        in_refs = refs[:n_in]
        out_ref, vec, buf, send_sems, recv_sems = refs[n_in:]
        x, y, c, _ = _place()
        me = 4 * x + 2 * y + c
        vec[...] = jnp.zeros_like(vec)
        vec[0:1, :] = jnp.sum(in_refs[0][...], axis=0)
        for r in range(5):
            vec[1 + r:2 + r, :] = jnp.sum(in_refs[1 + r][...], axis=0)
        vec[6:7, :] = jnp.sum(in_refs[6][...], axis=0)
        vec[7:8, :] = jnp.sum(in_refs[7][...], axis=0)
        vec[8:16, 0:CONV_W] = jnp.sum(in_refs[8][...], axis=0)
        buf[pl.ds(me, 1)] = vec[...][None]
        copies = []
        for r in range(1, 8):
            fx, fy, fc = (r >> 2) & 1, (r >> 1) & 1, r & 1
            to = (1 - x if fx else x, 1 - y if fy else y, 1 - c if fc else c)
            cp = pltpu.make_async_remote_copy(src_ref=vec, dst_ref=buf.at[me], send_sem=send_sems.at[r - 1],
                                              recv_sem=recv_sems.at[r - 1], device_id=to, device_id_type=MESH)
            cp.start()
            copies.append(cp)
        for cp in copies:
            cp.wait()
        total = buf[0]
        for s in range(1, 8):
            total = total + buf[s]
        out_ref[...] = total
        out_ref[0:1, :] = jnp.broadcast_to(jnp.sum(total[0:1, :], axis=-1, keepdims=True), (1, D_MODEL))

    return pl.pallas_call(
        body, name="small_allreduce",
        in_specs=[vmem] * n_in, out_specs=vmem, out_shape=_sds((SMALL_ROWS, D_MODEL), F32),
        scratch_shapes=[pltpu.VMEM((SMALL_ROWS, D_MODEL), F32), pltpu.VMEM((8, SMALL_ROWS, D_MODEL), F32),
                        pltpu.SemaphoreType.DMA((7,)), pltpu.SemaphoreType.DMA((7,))],
    )(*ins)


def _local_step(x, p, tgt, g, b_gate, w_conv, wf):
    seq = x.shape[0]
    tm = min(seq, 1024)
    th = min(seq, 512)
    tl = min(seq, 2048)
    ni, nh, nl = seq // tm, seq // th, seq // tl
    g_pre_mix, g_post_mix, g_pre_mlp, g_post_mlp, g_ple = g
    w_in, w_ao, w_co, w_o, w_up, w_down, w_pg, w_pp, w_in_nat, w_up_nat = wf
    D = D_MODEL
    vec = lambda a, blk=0: (a, _bs((1, D), lambda i, j, k: (0, blk)))
    rows_i = lambda a, t, blk=0: (a, _bs((t, D), lambda i, j, k: (i, blk)))
    rows_k = lambda a, t, blk=0: (a, _bs((t, D), lambda i, j, k: (k, blk)))
    part = lambda n: (_sds((n, 1, D), F32), _bs((None, 1, D), lambda i, j, k: (i, 0, 0)))
    full2 = lambda a: (a, _bs(a.shape, lambda i, j, k: (0, 0)))

    normed = lambda xb, gb: (_rms(xb, gb).astype(BF16),) * 2
    keep_a = lambda t: [(_sds((seq, D), BF16), _bs((t, D), lambda i, j, k: (i, 0)))]
    proj, h1 = _mm("proj_in", "nn", (nh, 1, 1),
                   a_ins=[rows_i(x, th), vec(g_pre_mix)], a_fn=normed,
                   b_ins=[full2(w_in_nat)], b_fn=_ident,
                   outs=[(_sds((seq, D_IN), F32), _bs((th, D_IN), lambda i, j, k: (i, 0)))],
                   acc_shape=(th, D_IN), a_cache=((th, D), BF16), a_outs=keep_a(th))
    o = _attn_fwd(proj, seq)
    (y_attn,) = _mm("attn_out", "nn", (ni, 1, 1),
                    a_ins=[(o, _bs((tm, ATTN_W), lambda i, j, k: (i, 0)))], a_fn=_ident,
                    b_ins=[full2(w_ao)], b_fn=_ident,
                    outs=[(_sds((seq, D), BF16), _bs((tm, D), lambda i, j, k: (i, 0)))], acc_shape=(tm, D))
    e, d = _conv_fwd(proj, w_conv, seq, tm)
    (y_conv,) = _mm("conv_out", "nn", (ni, 1, 1),
                    a_ins=[(e, _bs((tm, CONV_W), lambda i, j, k: (i, 0)))], a_fn=_ident,
                    b_ins=[full2(w_co)], b_fn=_ident,
                    outs=[(_sds((seq, D), BF16), _bs((tm, D), lambda i, j, k: (i, 0)))], acc_shape=(tm, D))

    def mix_fn(ga, gc, ya, yc, ba, bc):
        return ((_sig(ga + ba) * ya.astype(F32) + _sig(gc + bc) * yc.astype(F32)).astype(BF16),) * 2

    def post_mix(acc, xb, gb):
        return acc, xb + _rms(acc, gb)

    mix_ins = lambda rows: [rows(proj, th, 3), rows(proj, th, 4), rows(y_attn, th), rows(y_conv, th),
                            vec(b_gate, 0), vec(b_gate, 1)]
    mixed, x1, mixin = _mm("mix_out", "nn", (nh, 1, 1),
                           a_ins=mix_ins(rows_i), a_fn=mix_fn, b_ins=[full2(w_o)], b_fn=_ident,
                           epi_ins=[rows_i(x, th), vec(g_post_mix)], epi_fn=post_mix,
                           outs=[(_sds((seq, D), F32), _bs((th, D), lambda i, j, k: (i, 0)))] * 2,
                           acc_shape=(th, D), a_cache=((th, D), BF16), a_outs=keep_a(th))
    up, h2 = _mm("mlp_up", "nn", (nh, 1, 1),
                 a_ins=[rows_i(x1, th), vec(g_pre_mlp)], a_fn=normed,
                 b_ins=[full2(w_up_nat)], b_fn=_ident,
                 outs=[(_sds((seq, D_FF), BF16), _bs((th, D_FF), lambda i, j, k: (i, 0)))],
                 acc_shape=(th, D_FF), a_cache=((th, D), BF16), a_outs=keep_a(th))

    def relu2(ub):
        r = jnp.maximum(ub.astype(F32), 0.0)
        return (r * r).astype(BF16)

    f, x2 = _mm("mlp_down", "nn", (nh, 1, 1),
                a_ins=[(up, _bs((th, D_FF), lambda i, j, k: (i, 0)))], a_fn=relu2,
                b_ins=[full2(w_down)], b_fn=_ident,
                epi_ins=[rows_i(x1, th), vec(g_post_mlp)], epi_fn=post_mix,
                outs=[(_sds((seq, D), F32), _bs((th, D), lambda i, j, k: (i, 0)))] * 2, acc_shape=(th, D))
    (pp,) = _mm("ple_proj", "nn", (ni, 1, 1),
                a_ins=[(p, _bs((tm, PLE_DIM), lambda i, j, k: (i, 0)))], a_fn=_to_bf16,
                b_ins=[full2(w_pp)], b_fn=_ident,
                outs=[(_sds((seq, D), F32), _bs((tm, D), lambda i, j, k: (i, 0)))], acc_shape=(tm, D))

    def head(acc, x2b, ppb, tb):
        pg = _sig(acc)
        err = x2b + pg * ppb - tb
        return pg, err * (1.0 / D), jnp.sum(err * err, axis=0, keepdims=True) * (0.5 / D)

    pg, dx3, loss_p, h3 = _mm("ple_gate_loss", "nn", (nh, 1, 1),
                              a_ins=[rows_i(x2, th), vec(g_ple)], a_fn=normed,
                              b_ins=[full2(w_pg)], b_fn=_ident,
                              epi_a=(0,), epi_ins=[rows_i(pp, th), rows_i(tgt, th)], epi_fn=head,
                              outs=[(_sds((seq, D), BF16), _bs((th, D), lambda i, j, k: (i, 0))),
                                    (_sds((seq, D), F32), _bs((th, D), lambda i, j, k: (i, 0))), part(nh)],
                              acc_shape=(th, D), a_cache=((th, D), BF16), a_outs=keep_a(th))

    (dw_pp,) = _mm("dw_ple_proj", "tn", (1, 1, nh),
                   a_ins=[(p, _bs((th, PLE_DIM), lambda i, j, k: (k, 0)))], a_fn=_to_bf16,
                   b_ins=[rows_k(dx3, th), rows_k(pg, th)], b_fn=lambda a, b: (a * b.astype(F32)).astype(BF16),
                   outs=[(_sds((PLE_DIM, D), F32), _bs((PLE_DIM, D), lambda i, j, k: (0, 0)))],
                   acc_shape=(PLE_DIM, D))

    def dpre_fn(dx3b, ppb, pgb):
        pgf = pgb.astype(F32)
        return (dx3b * ppb * pgf * (1.0 - pgf)).astype(BF16)

    def ple_norm_bwd(acc, dx3b, x2b, gb, fb, g_mlp):
        dxn, dg = _rms_bwd(x2b, gb, acc)
        dx2b = dx3b + dxn
        dfb, dg_mlp = _rms_bwd(fb, g_mlp, dx2b)
        return dx2b, dg, dfb, dg_mlp

    dx2, dg_ple_p, df, dg_post_mlp_p, dpre = _mm(
        "d_ple_gate", "nt", (nh, 1, 1),
        a_ins=[rows_i(dx3, th), rows_i(pp, th), rows_i(pg, th)], a_fn=lambda a, b, c: (dpre_fn(a, b, c),) * 2,
        b_ins=[full2(w_pg)], b_fn=_ident,
        epi_a=(0,), epi_ins=[rows_i(x2, th), vec(g_ple), rows_i(f, th), vec(g_post_mlp)], epi_fn=ple_norm_bwd,
        outs=[(_sds((seq, D), F32), _bs((th, D), lambda i, j, k: (i, 0))), part(nh),
              (_sds((seq, D), BF16), _bs((th, D), lambda i, j, k: (i, 0))), part(nh)],
        acc_shape=(th, D), a_cache=((th, D), BF16),
        a_outs=[(_sds((seq, D), BF16), _bs((th, D), lambda i, j, k: (i, 0)))])
    (dw_pg,) = _mm("dw_ple_gate", "tn", (1, 1, ni),
                   a_ins=[rows_k(h3, tm)], a_fn=_ident, b_ins=[rows_k(dpre, tm)], b_fn=_ident,
                   outs=[(_sds((D, D), F32), _bs((D, D), lambda i, j, k: (0, 0)))], acc_shape=(D, D))

    def dup_fn(acc, ub):
        return (acc * (2.0 * jnp.maximum(ub.astype(F32), 0.0)),)

    (dup,) = _mm("d_mlp_down", "nt", (nh, 1, 1),
                 a_ins=[rows_i(df, th)], a_fn=_ident, b_ins=[full2(w_down)], b_fn=_ident,
                 epi_ins=[(up, _bs((th, D_FF), lambda i, j, k: (i, 0)))], epi_fn=dup_fn,
                 outs=[(_sds((seq, D_FF), BF16), _bs((th, D_FF), lambda i, j, k: (i, 0)))],
                 acc_shape=(th, D_FF))
    (dw_down,) = _mm("dw_mlp_down", "tn", (4, 1, nl),
                     a_ins=[(up, _bs((tl, D), lambda i, j, k: (k, i)))], a_fn=relu2,
                     b_ins=[rows_k(df, tl)], b_fn=_ident,
                     outs=[(_sds((D_FF, D), F32), _bs((D, D), lambda i, j, k: (i, 0)))], acc_shape=(D, D))
    (dw_up,) = _mm("dw_mlp_up", "tn", (1, 4, nl),
                   a_ins=[rows_k(h2, tl)], a_fn=_ident,
                   b_ins=[(dup, _bs((tl, D), lambda i, j, k: (k, j)))], b_fn=_ident,
                   outs=[(_sds((N_CHIPS, D, D), F32), _bs((None, D, D), lambda i, j, k: (j, 0, 0)))],
                   acc_shape=(D, D))

    def mlp_norm_bwd(acc, x1b, dx2b, mixedb, g_mlp, g_mix):
        dxn, dg_mlp = _rms_bwd(x1b, g_mlp, acc)
        dx1b = dx2b + dxn
        dmixedb, dg_mix = _rms_bwd(mixedb, g_mix, dx1b)
        return dx1b, dmixedb, dg_mlp, dg_mix

    dx1, dmixed, dg_pre_mlp_p, dg_post_mix_p = _mm(
        "d_mlp_up", "nt", (nh, 1, 1),
        a_ins=[(dup, _bs((th, D_FF), lambda i, j, k: (i, 0)))], a_fn=_ident,
        b_ins=[full2(w_up_nat)], b_fn=_ident,
        epi_ins=[rows_i(x1, th), rows_i(dx2, th), rows_i(mixed, th), vec(g_pre_mlp), vec(g_post_mix)],
        epi_fn=mlp_norm_bwd,
        outs=[(_sds((seq, D), F32), _bs((th, D), lambda i, j, k: (i, 0))),
              (_sds((seq, D), BF16), _bs((th, D), lambda i, j, k: (i, 0))), part(nh), part(nh)],
        acc_shape=(th, D))
    (dw_o,) = _mm("dw_mix_out", "tn", (1, 1, ni),
                  a_ins=[rows_k(mixin, tm)], a_fn=_ident, b_ins=[rows_k(dmixed, tm)], b_fn=_ident,
                  outs=[(_sds((D, D), F32), _bs((D, D), lambda i, j, k: (0, 0)))], acc_shape=(D, D))

    def gate_bwd(acc, ga, gc, ya, yc, ba, bc):
        sa, sc = _sig(ga + ba), _sig(gc + bc)
        dga = acc * ya.astype(F32) * sa * (1.0 - sa)
        dgc = acc * yc.astype(F32) * sc * (1.0 - sc)
        return (acc * sa, acc * sc, jnp.concatenate([dga, dgc], axis=1),
                jnp.sum(dga, axis=0, keepdims=True), jnp.sum(dgc, axis=0, keepdims=True))

    dya, dyc, dgate, dbg_a_p, dbg_c_p = _mm(
        "d_mix_out", "nt", (nh, 1, 1),
        a_ins=[rows_i(dmixed, th)], a_fn=_ident, b_ins=[full2(w_o)], b_fn=_ident,
        epi_ins=mix_ins(rows_i), epi_fn=gate_bwd,
        outs=[(_sds((seq, D), BF16), _bs((th, D), lambda i, j, k: (i, 0)))] * 2
             + [(_sds((seq, 2 * D), BF16), _bs((th, 2 * D), lambda i, j, k: (i, 0))), part(nh), part(nh)],
        acc_shape=(th, D))
    (dw_ao,) = _mm("dw_attn_out", "tn", (1, 1, nh),
                   a_ins=[(o, _bs((th, ATTN_W), lambda i, j, k: (k, 0)))], a_fn=_ident,
                   b_ins=[rows_k(dya, th)], b_fn=_ident,
                   outs=[(_sds((ATTN_W, D), F32), _bs((ATTN_W, D), lambda i, j, k: (0, 0)))], acc_shape=(ATTN_W, D))
    (do,) = _mm("d_attn_out", "nt", (ni, 1, 1),
                a_ins=[rows_i(dya, tm)], a_fn=_ident, b_ins=[full2(w_ao)], b_fn=_ident,
                outs=[(_sds((seq, ATTN_W), BF16), _bs((tm, ATTN_W), lambda i, j, k: (i, 0)))],
                acc_shape=(tm, ATTN_W))
    dq, dk, dv = _attn_bwd(proj, do, seq)
    (dw_co,) = _mm("dw_conv_out", "tn", (1, 1, nh),
                   a_ins=[(e, _bs((th, CONV_W), lambda i, j, k: (k, 0)))], a_fn=_ident,
                   b_ins=[rows_k(dyc, th)], b_fn=_ident,
                   outs=[(_sds((CONV_W, D), F32), _bs((CONV_W, D), lambda i, j, k: (0, 0)))], acc_shape=(CONV_W, D))
    (de,) = _mm("d_conv_out", "nt", (ni, 1, 1),
                a_ins=[rows_i(dyc, tm)], a_fn=_ident, b_ins=[full2(w_co)], b_fn=_ident,
                outs=[(_sds((seq, CONV_W), F32), _bs((tm, CONV_W), lambda i, j, k: (i, 0)))],
                acc_shape=(tm, CONV_W))
    dconv, dwc_p = _conv_bwd(proj, de, d, w_conv, seq, tm)
    qkv_w = 3 * ATTN_W
    join_bf16 = lambda *blocks: jnp.concatenate([b.astype(BF16) for b in blocks], axis=1)
    piece = lambda a, t, rows, blk=0: (a, _bs((t, a.shape[1]), (lambda i, j, k: (k, blk)) if rows == "k"
                                             else (lambda i, j, k: (i, blk))))
    (dw_in_qkv,) = _mm("dw_proj_in_qkv", "tn", (1, 1, ni),
                       a_ins=[rows_k(h1, tm)], a_fn=_ident,
                       b_ins=[piece(dq, tm, "k"), piece(dk, tm, "k"), piece(dv, tm, "k")], b_fn=join_bf16,
                       outs=[(_sds((D, qkv_w), F32), _bs((D, qkv_w), lambda i, j, k: (0, 0)))], acc_shape=(D, qkv_w))
    (dw_in_conv,) = _mm("dw_proj_in_conv", "tn", (1, 1, nl),
                        a_ins=[rows_k(h1, tl)], a_fn=_ident, b_ins=[piece(dconv, tl, "k")], b_fn=_ident,
                        outs=[(_sds((D, 3 * CONV_W), F32), _bs((D, 3 * CONV_W), lambda i, j, k: (0, 0)))],
                        acc_shape=(D, 3 * CONV_W))
    (dw_in_gate,) = _mm("dw_proj_in_gate", "tn", (1, 2, nl),
                        a_ins=[rows_k(h1, tl)], a_fn=_ident,
                        b_ins=[(dgate, _bs((tl, D), lambda i, j, k: (k, j)))], b_fn=_ident,
                        outs=[(_sds((D, 2 * D), F32), _bs((D, D), lambda i, j, k: (0, j)))], acc_shape=(D, D))
    dw_in = jnp.concatenate([dw_in_qkv, dw_in_conv, dw_in_gate], axis=1)

    def in_norm_bwd(acc, xb, dx1b, gb):
        dxn, dg = _rms_bwd(xb, gb, acc)
        return dx1b + dxn, dg

    grad_x, dg_pre_mix_p = _mm("d_proj_in", "nt", (nh, 1, 1),
                               a_ins=[piece(dq, th, "i"), piece(dk, th, "i"), piece(dv, th, "i"),
                                      piece(dconv, th, "i"), piece(dgate, th, "i")], a_fn=join_bf16,
                               b_ins=[full2(w_in_nat)], b_fn=_ident,
                               epi_ins=[rows_i(x, th), rows_i(dx1, th), vec(g_pre_mix)], epi_fn=in_norm_bwd,
                               outs=[(_sds((seq, D), F32), _bs((th, D), lambda i, j, k: (i, 0))), part(nh)],
                               acc_shape=(th, D))

    chip_major = lambda a: a.reshape(a.shape[0], N_CHIPS, a.shape[1] // N_CHIPS).transpose(1, 0, 2)
    big = [chip_major(dw_in), chip_major(dw_ao), chip_major(dw_co), dw_o.reshape(N_CHIPS, D // N_CHIPS, D), dw_up,
           dw_down.reshape(N_CHIPS, D_FF // N_CHIPS, D), dw_pg.reshape(N_CHIPS, D // N_CHIPS, D), chip_major(dw_pp)]
    small = (loss_p, [dg_pre_mix_p, dg_post_mix_p, dg_pre_mlp_p, dg_post_mlp_p, dg_ple_p], dbg_a_p, dbg_c_p, dwc_p)
    return grad_x, big, small


RS_GROUPS = ((0,), (4,), (5,), (1, 2, 3, 6, 7))


def _reduce_scatter(big):
    pair = [None] * len(big)
    for gi, group in enumerate(RS_GROUPS):
        for w, s in zip(group, _rs_pair_sum(f"rs_pair_sum_{gi}", [big[w] for w in group])):
            pair[w] = s
    return _rs_exchange_join(pair)


def kernel(x, p, g_pre_mix, w_in, b_gate, w_conv, w_attn_out, w_conv_out, w_o, g_post_mix, g_pre_mlp, w_up, w_down, g_post_mlp, g_ple, w_ple_gate, w_ple_proj, loss_target, m_g_pre_mix, m_w_in, m_b_gate, m_w_conv, m_w_attn_out, m_w_conv_out, m_w_o, m_g_post_mix, m_g_pre_mlp, m_w_up, m_w_down, m_g_post_mlp, m_g_ple, m_w_ple_gate, m_w_ple_proj, v_g_pre_mix, v_w_in, v_b_gate, v_w_conv, v_w_attn_out, v_w_conv_out, v_w_o, v_g_post_mix, v_g_pre_mlp, v_w_up, v_w_down, v_g_post_mlp, v_g_ple, v_w_ple_gate, v_w_ple_proj):
    mats = [w_in, w_attn_out, w_conv_out, w_o, w_up, w_down, w_ple_gate, w_ple_proj]
    mats_m = [m_w_in, m_w_attn_out, m_w_conv_out, m_w_o, m_w_up, m_w_down, m_w_ple_gate, m_w_ple_proj]
    mats_v = [v_w_in, v_w_attn_out, v_w_conv_out, v_w_o, v_w_up, v_w_down, v_w_ple_gate, v_w_ple_proj]
    gains = [g_pre_mix, g_post_mix, g_pre_mlp, g_post_mlp, g_ple]
    gains_m = [m_g_pre_mix, m_g_post_mix, m_g_pre_mlp, m_g_post_mlp, m_g_ple]
    gains_v = [v_g_pre_mix, v_g_post_mix, v_g_pre_mlp, v_g_post_mlp, v_g_ple]

    taps = jnp.concatenate([w_conv[0], jnp.zeros((CONV_PAD_ROWS - 3, LANES), F32)], axis=0)
    gathered = _allgather_weights([w[0].astype(BF16) for w in mats] + [taps])
    cols_joined = lambda a: a.transpose(1, 0, 2).reshape(a.shape[1], N_CHIPS * a.shape[2])
    rows_joined = lambda a: a.reshape(N_CHIPS * a.shape[1], a.shape[2])
    wf = [gathered[0], cols_joined(gathered[1]), cols_joined(gathered[2]), rows_joined(gathered[3]), gathered[4],
          rows_joined(gathered[5]), rows_joined(gathered[6]), cols_joined(gathered[7]),
          cols_joined(gathered[0]), cols_joined(gathered[4])]
    w_conv_full = cols_joined(gathered[8])[0:3, :]
    chip = 2 * lax.axis_index("x") + lax.axis_index("y")

    grad_x, big, small = _local_step(x[0], p[0, 0], loss_target[0], gains, b_gate, w_conv_full, wf)

    shard_grads = _reduce_scatter(big)
    red = _small_allreduce(*small)
    loss = red[0, 0]
    grad_gains = [red[1 + r:2 + r, :] for r in range(5)]
    grad_b_gate = jnp.concatenate([red[6:7, :], red[7:8, :]], axis=1)
    grad_w_conv = lax.dynamic_slice(red[8:11, :], (0, chip * LANES), (3, LANES))[None]

    grads_big = [gr.reshape(w.shape) for gr, w in zip(shard_grads, mats)]
    upd_big = [_adamw(f"adamw_{i}", w, gr, m, v) for i, (w, gr, m, v) in enumerate(zip(mats, grads_big, mats_m, mats_v))]
    pack = lambda vs, bg: jnp.concatenate(list(vs) + [bg.reshape(2, D_MODEL), jnp.zeros((1, D_MODEL), F32)], axis=0)
    upd_small = _adamw("adamw_small", pack(gains, b_gate), pack(grad_gains, grad_b_gate),
                       pack(gains_m, m_b_gate), pack(gains_v, v_b_gate))
    upd_conv = _adamw("adamw_conv", w_conv, grad_w_conv, m_w_conv, v_w_conv)

    def small_out(a, which):
        gains_out = [a[r:r + 1, :] for r in range(5)]
        return gains_out, a[5:7, :].reshape(1, 2 * D_MODEL)

    def ordered(g_pre_mix_, big_, b_gate_, conv_, g_rest):
        return [g_pre_mix_, big_[0], b_gate_, conv_, big_[1], big_[2], big_[3], g_rest[0], g_rest[1], big_[4], big_[5],
                g_rest[2], g_rest[3], big_[6], big_[7]]

    outs = [loss, grad_x[None]]
    outs += ordered(grad_gains[0], grads_big, grad_b_gate, grad_w_conv, grad_gains[1:])
    for which in range(3):
        g_out, b_out = small_out(upd_small[which], which)
        outs += ordered(g_out[0], [u[which] for u in upd_big], b_out, upd_conv[which], g_out[1:])
    return tuple(outs)
```

```python
import functools

import jax
import jax.numpy as jnp
from jax import lax
from jax.experimental import pallas as pl
from jax.experimental.pallas import tpu as pltpu

F32 = jnp.float32
BF16 = jnp.bfloat16
MESH = pl.DeviceIdType.MESH

D_MODEL = 1024
N_HEADS = 8
HEAD_DIM = 64
ATTN_W = N_HEADS * HEAD_DIM
CONV_W = 512
D_FF = 4096
PLE_DIM = 256
D_IN = 5120
N_CHIPS = 4
EPS = 1e-6
Q_SCALE = HEAD_DIM ** -0.5

ADAM_LR = 0.001
ADAM_B1 = 0.9
ADAM_B2 = 0.999
ADAM_EPS = 1e-08
ADAM_WD = 0.01
ADAM_STEP = 10

V7X_VMEM_BYTES = 64 * 1024 * 1024
VMEM_LIMIT = V7X_VMEM_BYTES - 8 * 1024 * 1024
LANES = 128
ATT_BLK = 256
SMALL_ROWS = 16
CONV_PAD_ROWS = 16


def _cparams(n_grid):
    return pltpu.CompilerParams(dimension_semantics=("arbitrary",) * n_grid, vmem_limit_bytes=VMEM_LIMIT)


def _bs(shape, fn):
    return pl.BlockSpec(shape, fn)


def _rms_stats(xf):
    return lax.rsqrt(jnp.mean(xf * xf, axis=-1, keepdims=True) + EPS)


def _rms(xf, g):
    return xf * _rms_stats(xf) * g


def _rms_bwd(xf, g, dy):
    r = _rms_stats(xf)
    xh = xf * r
    dyg = dy * g
    dx = r * (dyg - xh * jnp.mean(dyg * xh, axis=-1, keepdims=True))
    return dx, jnp.sum(dy * xh, axis=0, keepdims=True)


def _sig(z):
    return 1.0 / (1.0 + jnp.exp(-z))


def _ident(a):
    return a


def _to_bf16(a):
    return a.astype(BF16)


_DIMS = {"nn": (((1,), (0,)), ((), ())), "nt": (((1,), (1,)), ((), ())), "tn": (((0,), (0,)), ((), ()))}


def _mm(name, mode, grid, a_ins, a_fn, b_ins, b_fn, outs, acc_shape, epi_ins=(), epi_fn=None,
        a_cache=None, a_outs=(), epi_a=()):
    nk = grid[2]
    na, nb, ne, no, nao = len(a_ins), len(b_ins), len(epi_ins), len(outs), len(a_outs)
    assert a_cache is None or nk == 1
    assert not a_outs or a_cache is not None
    dims = _DIMS[mode]
    if epi_fn is None:
        epi_fn = lambda acc: (acc,)

    def body(*refs):
        a_refs = refs[:na]
        b_refs = refs[na:na + nb]
        e_refs = refs[na + nb:na + nb + ne]
        o_refs = refs[na + nb + ne:na + nb + ne + no]
        ao_refs = refs[na + nb + ne + no:na + nb + ne + no + nao]
        scratch = list(refs[na + nb + ne + no + nao:])
        acc_ref = scratch.pop(0) if nk > 1 else None
        a_sc = scratch.pop(0) if a_cache is not None else None
        j = pl.program_id(1)
        k = pl.program_id(2)

        def finish(acc):
            res = epi_fn(acc, *[a_refs[t][...] for t in epi_a], *[r[...] for r in e_refs])
            for r, val in zip(o_refs, res):
                r[...] = val.astype(r.dtype)

        if a_sc is not None:
            @pl.when(j == 0)
            def _():
                res = a_fn(*[r[...] for r in a_refs])
                if nao:
                    for r, val in zip(ao_refs, res[1:]):
                        r[...] = val.astype(r.dtype)
                    res = res[0]
                a_sc[...] = res
            a = a_sc[...]
        else:
            a = a_fn(*[r[...] for r in a_refs])
        b = b_fn(*[r[...] for r in b_refs])
        prod = lax.dot_general(a, b, dims, preferred_element_type=F32)
        if nk == 1:
            finish(prod)
        else:
            @pl.when(k == 0)
            def _():
                acc_ref[...] = prod

            @pl.when(k > 0)
            def _():
                acc_ref[...] += prod

            @pl.when(k == nk - 1)
            def _():
                finish(acc_ref[...])

    scratch_shapes = []
    if nk > 1:
        scratch_shapes.append(pltpu.VMEM(acc_shape, F32))
    if a_cache is not None:
        scratch_shapes.append(pltpu.VMEM(*a_cache))
    all_outs = list(outs) + list(a_outs)
    res = pl.pallas_call(
        body, name=name, grid=grid,
        in_specs=[s for _, s in a_ins] + [s for _, s in b_ins] + [s for _, s in epi_ins],
        out_specs=[s for _, s in all_outs],
        out_shape=[o for o, _ in all_outs],
        scratch_shapes=scratch_shapes,
        compiler_params=_cparams(3),
    )(*[a for a, _ in a_ins], *[a for a, _ in b_ins], *[a for a, _ in epi_ins])
    return res


def _sds(shape, dtype):
    return jax.ShapeDtypeStruct(shape, dtype)


def _shift_rows_down(u, prev, n):
    rows = u.shape[0]
    ridx = lax.broadcasted_iota(jnp.int32, u.shape, 0)
    out = pltpu.roll(u, n, 0)
    for r in range(n):
        out = jnp.where(ridx == r, prev[8 - n + r:8 - n + r + 1, :], out)
    del rows
    return out


def _shift_rows_up(u, nxt, n):
    rows = u.shape[0]
    ridx = lax.broadcasted_iota(jnp.int32, u.shape, 0)
    out = pltpu.roll(u, rows - n, 0)
    for r in range(n):
        out = jnp.where(ridx == rows - n + r, nxt[r:r + 1, :], out)
    return out


CONV_COL0 = 3


def _conv_fwd(proj, w_conv, seq, tr):
    hb = tr // 8

    def body(cb_ref, cc_ref, cu_ref, ccp_ref, cup_ref, w_ref, e_ref, d_ref):
        i = pl.program_id(0)
        u = cc_ref[...] * cu_ref[...]
        up = jnp.where(i > 0, ccp_ref[...] * cup_ref[...], 0.0)
        w = w_ref[...]
        d = w[0:1, :] * _shift_rows_down(u, up, 2) + w[1:2, :] * _shift_rows_down(u, up, 1) + w[2:3, :] * u
        d_ref[...] = d
        e_ref[...] = (cb_ref[...] * d).astype(BF16)

    prev = lambda c: (lambda i: (jnp.maximum(i * hb - 1, 0), c))
    return pl.pallas_call(
        body, name="conv_fwd", grid=(seq // tr,),
        in_specs=[_bs((tr, CONV_W), lambda i: (i, CONV_COL0)),
                  _bs((tr, CONV_W), lambda i: (i, CONV_COL0 + 1)),
                  _bs((tr, CONV_W), lambda i: (i, CONV_COL0 + 2)),
                  _bs((8, CONV_W), prev(CONV_COL0 + 1)),
                  _bs((8, CONV_W), prev(CONV_COL0 + 2)),
                  _bs((3, CONV_W), lambda i: (0, 0))],
        out_specs=[_bs((tr, CONV_W), lambda i: (i, 0)), _bs((tr, CONV_W), lambda i: (i, 0))],
        out_shape=[_sds((seq, CONV_W), BF16), _sds((seq, CONV_W), F32)],
        compiler_params=_cparams(1),
    )(proj, proj, proj, proj, proj, w_conv)


def _conv_bwd(proj, de, d, w_conv, seq, tr):
    hb = tr // 8
    nblk = seq // tr

    def body(cb_ref, cc_ref, cu_ref, ccp_ref, cup_ref, cbn_ref, de_ref, den_ref, d_ref, w_ref, o_ref, dw_ref):
        i = pl.program_id(0)
        cc, cu, cb = cc_ref[...], cu_ref[...], cb_ref[...]
        u = cc * cu
        up = jnp.where(i > 0, ccp_ref[...] * cup_ref[...], 0.0)
        u1 = _shift_rows_down(u, up, 1)
        u2 = _shift_rows_down(u, up, 2)
        de_ = de_ref[...]
        dd = de_ * cb
        ddn = jnp.where(i < nblk - 1, den_ref[...] * cbn_ref[...], 0.0)
        w = w_ref[...]
        du = w[2:3, :] * dd + w[1:2, :] * _shift_rows_up(dd, ddn, 1) + w[0:1, :] * _shift_rows_up(dd, ddn, 2)
        o_ref[:, 0:CONV_W] = (de_ * d_ref[...]).astype(BF16)
        o_ref[:, CONV_W:2 * CONV_W] = (du * cu).astype(BF16)
        o_ref[:, 2 * CONV_W:3 * CONV_W] = (du * cc).astype(BF16)
        ridx = lax.broadcasted_iota(jnp.int32, (8, CONV_W), 0)
        dw0 = jnp.sum(dd * u2, axis=0, keepdims=True)
        dw1 = jnp.sum(dd * u1, axis=0, keepdims=True)
        dw2 = jnp.sum(dd * u, axis=0, keepdims=True)
        dw_ref[...] = jnp.where(ridx == 0, dw0, jnp.where(ridx == 1, dw1, jnp.where(ridx == 2, dw2, 0.0)))

    prev = lambda c: (lambda i: (jnp.maximum(i * hb - 1, 0), c))
    nxt = lambda c: (lambda i: (jnp.minimum((i + 1) * hb, seq // 8 - 1), c))
    return pl.pallas_call(
        body, name="conv_bwd", grid=(nblk,),
        in_specs=[_bs((tr, CONV_W), lambda i: (i, CONV_COL0)),
                  _bs((tr, CONV_W), lambda i: (i, CONV_COL0 + 1)),
                  _bs((tr, CONV_W), lambda i: (i, CONV_COL0 + 2)),
                  _bs((8, CONV_W), prev(CONV_COL0 + 1)),
                  _bs((8, CONV_W), prev(CONV_COL0 + 2)),
                  _bs((8, CONV_W), nxt(CONV_COL0)),
                  _bs((tr, CONV_W), lambda i: (i, 0)),
                  _bs((8, CONV_W), nxt(0)),
                  _bs((tr, CONV_W), lambda i: (i, 0)),
                  _bs((3, CONV_W), lambda i: (0, 0))],
        out_specs=[_bs((tr, 3 * CONV_W), lambda i: (i, 0)), _bs((None, 8, CONV_W), lambda i: (i, 0, 0))],
        out_shape=[_sds((seq, 3 * CONV_W), BF16), _sds((nblk, 8, CONV_W), F32)],
        compiler_params=_cparams(1),
    )(proj, proj, proj, proj, proj, proj, de, de, d, w_conv)


def _nt(a, b):
    return lax.dot_general(a, b, _DIMS["nt"], preferred_element_type=F32)


def _tn(a, b):
    return lax.dot_general(a, b, _DIMS["tn"], preferred_element_type=F32)


def _nn(a, b):
    return lax.dot_general(a, b, _DIMS["nn"], preferred_element_type=F32)


def _log_gates(z):
    lse = jnp.log(1.0 + jnp.exp(-jnp.abs(z)))
    log_beta = jnp.minimum(z, 0.0) - lse
    return log_beta, log_beta - z


DEAD_LOG_WEIGHT = -110.0
NO_TILE = -1e30


def _first_live_tile(start, scores, live_sc):
    def alive():
        return jnp.max(jnp.maximum(live_sc[0], live_sc[1])) > DEAD_LOG_WEIGHT

    def step(c):
        for h, z in enumerate(scores(c[0])):
            live_sc[h] = live_sc[h] + jnp.sum(_log_gates(z)[1], axis=-1, keepdims=True)
        return c[0] - 1, alive()

    j_end, _ = lax.while_loop(lambda c: jnp.logical_and(c[0] >= 0, c[1]), step, (start, alive()))
    return j_end + 1


def _attn_fwd(proj, seq):
    blk = ATT_BLK
    nq = seq // blk
    npair = N_HEADS // 2

    def body(q_ref, k_ref, v_ref, o_ref, z0_sc, z1_sc, w0_sc, w1_sc, tot_sc, live_sc, acc_sc):
        i = pl.program_id(1)
        is_a = lax.broadcasted_iota(jnp.int32, (1, LANES), 1) < HEAD_DIM
        q2 = (q_ref[...] * Q_SCALE).astype(BF16)
        zero = jnp.zeros_like(q2)
        qs = (jnp.where(is_a, q2, zero), jnp.where(is_a, zero, q2))
        row = lax.broadcasted_iota(jnp.int32, (blk, blk), 0)
        col = lax.broadcasted_iota(jnp.int32, (blk, blk), 1)
        tri = (row > col).astype(BF16)
        causal = col < row

        def tile_of(ref, j):
            return ref[pl.ds(pl.multiple_of(j * blk, blk), blk), :].astype(BF16)

        def scores(j):
            k2 = tile_of(k_ref, j)
            return [_nt(qs[h], k2) for h in range(2)]

        has_left = i > 0
        left = jnp.maximum(i - 1, 0)
        g_d = [_log_gates(z) for z in scores(i)]
        g_l = [_log_gates(z) for z in scores(left)]
        keep_d = [jnp.where(causal, g[1], 0.0) for g in g_d]
        suf_d = [_nn(lk.astype(BF16), tri) for lk in keep_d]
        suf_l = [_nn(g[1].astype(BF16), tri) for g in g_l]
        v_d, v_l = tile_of(v_ref, i), tile_of(v_ref, left)
        pv = []
        for h in range(2):
            sum_d = jnp.sum(keep_d[h], axis=-1, keepdims=True)
            w_d = jnp.where(causal, jnp.exp(g_d[h][0] + suf_d[h]), 0.0)
            w_l = jnp.exp(g_l[h][0] + (jnp.where(has_left, sum_d, NO_TILE) + suf_l[h]))
            pv.append(_nn(w_d.astype(BF16), v_d) + _nn(w_l.astype(BF16), v_l))
            tot_sc[h] = sum_d + jnp.sum(g_l[h][1], axis=-1, keepdims=True)
        acc_sc[...] = jnp.where(is_a, pv[0], pv[1])

        live_sc[...] = tot_sc[...]
        first = _first_live_tile(i - 2, scores, live_sc)
        trips = i - 1 - first
        z_bufs, w_bufs = (z0_sc, z1_sc), (w0_sc, w1_sc)

        def put(ref, vals):
            for h in range(2):
                ref[h] = vals[h]

        def weights(zs):
            gates = [_log_gates(z) for z in zs]
            sums = [_nn(g[1].astype(BF16), tri) for g in gates]
            ws = []
            for h in range(2):
                ws.append(jnp.exp(gates[h][0] + (tot_sc[h] + sums[h])).astype(BF16))
                tot_sc[h] = tot_sc[h] + jnp.sum(gates[h][1], axis=-1, keepdims=True)
            return ws

        def add_values(w_buf, j):
            v2 = tile_of(v_ref, j)
            acc_sc[...] += jnp.where(is_a, _nn(w_buf[0], v2), _nn(w_buf[1], v2))

        def trip(j, s):
            add_values(w_bufs[s], j + 1)
            put(z_bufs[1 - s], scores(jnp.maximum(j - 1, first)))
            put(w_bufs[1 - s], weights((z_bufs[s][0], z_bufs[s][1])))

        @pl.when(trips > 0)
        def _():
            put(z0_sc, scores(i - 2))
            w0_sc[...] = jnp.zeros_like(w0_sc)

            def two_trips(pp, carry):
                j = i - 2 - 2 * pp
                trip(j, 0)
                trip(j - 1, 1)
                return carry

            lax.fori_loop(0, trips // 2, two_trips, 0)
            odd = trips % 2 == 1

            @pl.when(odd)
            def _():
                trip(first, 0)
                add_values(w1_sc, first)

            @pl.when(jnp.logical_not(odd))
            def _():
                add_values(w0_sc, first)

        o_ref[...] = acc_sc[...].astype(BF16)

    return pl.pallas_call(
        body, name="attn_fwd", grid=(npair, nq),
        in_specs=[_bs((blk, LANES), lambda p, i: (i, p)),
                  _bs((seq, LANES), lambda p, i: (0, npair + p)),
                  _bs((seq, LANES), lambda p, i: (0, 2 * npair + p))],
        out_specs=_bs((blk, LANES), lambda p, i: (i, p)),
        out_shape=_sds((seq, ATTN_W), BF16),
        scratch_shapes=[pltpu.VMEM((2, blk, blk), F32), pltpu.VMEM((2, blk, blk), F32),
                        pltpu.VMEM((2, blk, blk), BF16), pltpu.VMEM((2, blk, blk), BF16),
                        pltpu.VMEM((2, blk, 1), F32), pltpu.VMEM((2, blk, 1), F32), pltpu.VMEM((blk, LANES), F32)],
        compiler_params=_cparams(2),
    )(proj, proj, proj)


def _attn_bwd(proj, do, seq):
    blk = ATT_BLK
    nq = seq // blk
    npair = N_HEADS // 2

    def body(q_ref, k_ref, v_ref, do_ref, dq_ref, dk_ref, dv_ref,
             prod0_sc, prod1_sc, pend0_sc, pend1_sc, tot_sc, live_sc, cum_sc, pre_sc, dq_sc):
        i = pl.program_id(1)

        @pl.when(i == 0)
        def _():
            dk_ref[...] = jnp.zeros_like(dk_ref)
            dv_ref[...] = jnp.zeros_like(dv_ref)

        is_a = lax.broadcasted_iota(jnp.int32, (1, LANES), 1) < HEAD_DIM
        q2 = (q_ref[...] * Q_SCALE).astype(BF16)
        do2 = do_ref[...]
        zero = jnp.zeros_like(q2)
        qs = (jnp.where(is_a, q2, zero), jnp.where(is_a, zero, q2))
        dos = (jnp.where(is_a, do2, zero), jnp.where(is_a, zero, do2))
        row = lax.broadcasted_iota(jnp.int32, (blk, blk), 0)
        col = lax.broadcasted_iota(jnp.int32, (blk, blk), 1)
        tri_after = (row > col).astype(BF16)
        tri_excl = (row < col).astype(BF16)
        causal = col < row

        def tile_of(ref, j):
            return ref[pl.ds(pl.multiple_of(j * blk, blk), blk), :].astype(BF16)

        def scores(j):
            k2 = tile_of(k_ref, j)
            return [_nt(qs[h], k2) for h in range(2)]

        def products(j):
            v2 = tile_of(v_ref, j)
            return scores(j) + [_nt(dos[h], v2) for h in range(2)]

        def row_sum(a):
            return jnp.sum(a, axis=-1, keepdims=True)

        def grad_matmuls(ws, dzs, j):
            rows = pl.ds(pl.multiple_of(j * blk, blk), blk)
            k2 = tile_of(k_ref, j)
            dq_sc[...] += jnp.where(is_a, _nn(dzs[0], k2), _nn(dzs[1], k2))
            dk_ref[rows, :] += jnp.where(is_a, _tn(dzs[0], q2), _tn(dzs[1], q2))
            if ws is not None:
                dv_ref[rows, :] += jnp.where(is_a, _tn(ws[0], do2), _tn(ws[1], do2))

        has_left = i > 0
        left = jnp.maximum(i - 1, 0)
        p_d, p_l = products(i), products(left)
        g_d = [_log_gates(z) for z in p_d[:2]]
        g_l = [_log_gates(z) for z in p_l[:2]]
        keep_d = [jnp.where(causal, g[1], 0.0) for g in g_d]
        suf_d = [_nn(lk.astype(BF16), tri_after) for lk in keep_d]
        suf_l = [_nn(g[1].astype(BF16), tri_after) for g in g_l]
        w_d, w_l, gg_d, gg_l = [], [], [], []
        for h in range(2):
            sum_d = row_sum(keep_d[h])
            w_d.append(jnp.where(causal, jnp.exp(g_d[h][0] + suf_d[h]), 0.0))
            w_l.append(jnp.exp(g_l[h][0] + (jnp.where(has_left, sum_d, NO_TILE) + suf_l[h])))
            gg_d.append(p_d[2 + h] * w_d[h])
            gg_l.append(p_l[2 + h] * w_l[h])
            tot_sc[h] = sum_d + row_sum(g_l[h][1])
        before_d = [_nn(g.astype(BF16), tri_excl) for g in gg_d]
        before_l = [_nn(g.astype(BF16), tri_excl) for g in gg_l]
        dz_d, dz_l = [], []
        for h in range(2):
            beta_d, beta_l = jnp.exp(g_d[h][0]), jnp.exp(g_l[h][0])
            dz_l.append((gg_l[h] * (1.0 - beta_l) - before_l[h] * beta_l).astype(BF16))
            dz = gg_d[h] * (1.0 - beta_d) - (row_sum(gg_l[h]) + before_d[h]) * beta_d
            dz_d.append(jnp.where(causal, dz, 0.0).astype(BF16))
        dq_sc[...] = jnp.zeros_like(dq_sc)
        grad_matmuls([w.astype(BF16) for w in w_l], dz_l, left)
        grad_matmuls([w.astype(BF16) for w in w_d], dz_d, i)

        live_sc[...] = tot_sc[...]
        first = _first_live_tile(i - 2, scores, live_sc)
        trips = i - 1 - first
        prod_bufs, pend_bufs = (prod0_sc, prod1_sc), (pend0_sc, pend1_sc)

        def local_grads(prods):
            zs, dws = prods[:2], prods[2:]
            gates = [_log_gates(z) for z in zs]
            sums = [_nn(g[1].astype(BF16), tri_after) for g in gates]
            ws, gs = [], []
            for h in range(2):
                cum = cum_sc[h] + row_sum(gates[h][1])
                cum_sc[h] = cum
                ws.append(jnp.exp(gates[h][0] + ((live_sc[h] - cum) + sums[h])))
                gs.append(dws[h] * ws[h])
            befores = [_nn(g.astype(BF16), tri_excl) for g in gs]
            dzs = []
            for h in range(2):
                beta = jnp.exp(gates[h][0])
                dzs.append((gs[h] * (1.0 - beta) - (pre_sc[h] + befores[h]) * beta).astype(BF16))
                pre_sc[h] = pre_sc[h] + row_sum(gs[h])
            return [w.astype(BF16) for w in ws] + dzs

        def put(ref, vals):
            for n, val in enumerate(vals):
                ref[n] = val

        def flush(pend, j):
            grad_matmuls([pend[0], pend[1]], [pend[2], pend[3]], j)

        def trip(j, s):
            flush(pend_bufs[s], jnp.maximum(j - 1, first))
            put(prod_bufs[1 - s], products(j + 1))
            put(pend_bufs[1 - s], local_grads([prod_bufs[s][n] for n in range(4)]))

        def earlier_keys_share(j, mask):
            dzs = []
            for h, z in enumerate(scores(j)):
                beta = jnp.exp(_log_gates(z)[0])
                dzs.append(jnp.where(mask, -pre_sc[h] * beta, 0.0).astype(BF16))
            grad_matmuls(None, dzs, j)

        @pl.when(trips > 0)
        def _():
            cum_sc[...] = jnp.zeros_like(cum_sc)
            pre_sc[...] = jnp.zeros_like(pre_sc)
            pend0_sc[...] = jnp.zeros_like(pend0_sc)
            put(prod0_sc, products(first))

            def two_trips(pp, carry):
                trip(first + 2 * pp, 0)
                trip(first + 2 * pp + 1, 1)
                return carry

            lax.fori_loop(0, trips // 2, two_trips, 0)
            odd = trips % 2 == 1

            @pl.when(odd)
            def _():
                trip(i - 2, 0)
                flush(pend1_sc, i - 2)

            @pl.when(jnp.logical_not(odd))
            def _():
                flush(pend0_sc, i - 2)

            earlier_keys_share(i - 1, True)
            earlier_keys_share(i, causal)

        dq_ref[...] = dq_sc[...] * Q_SCALE

    qmap = lambda p, i: (i, p)
    return pl.pallas_call(
        body, name="attn_bwd", grid=(npair, nq),
        in_specs=[_bs((blk, LANES), qmap),
                  _bs((seq, LANES), lambda p, i: (0, npair + p)),
                  _bs((seq, LANES), lambda p, i: (0, 2 * npair + p)),
                  _bs((blk, LANES), qmap)],
        out_specs=[_bs((blk, LANES), qmap),
                   _bs((seq, LANES), lambda p, i: (0, p)),
                   _bs((seq, LANES), lambda p, i: (0, p))],
        out_shape=[_sds((seq, ATTN_W), F32)] * 3,
        scratch_shapes=[pltpu.VMEM((4, blk, blk), F32), pltpu.VMEM((4, blk, blk), F32),
                        pltpu.VMEM((4, blk, blk), BF16), pltpu.VMEM((4, blk, blk), BF16),
                        pltpu.VMEM((2, blk, 1), F32), pltpu.VMEM((2, blk, 1), F32), pltpu.VMEM((2, blk, 1), F32),
                        pltpu.VMEM((2, blk, 1), F32), pltpu.VMEM((blk, LANES), F32)],
        compiler_params=_cparams(2),
    )(proj, proj, proj, do)


def _elementwise(name, fn, ins, out_dtypes):
    rows, cols = ins[0].shape
    tr = rows
    for cand in (512, 256, 128, 64, 32, 16, 8):
        if rows % cand == 0 and cand * cols * 4 <= 2 * 1024 * 1024:
            tr = cand
            break
    n_in = len(ins)

    def body(*refs):
        res = fn(*[r[...] for r in refs[:n_in]])
        for r, val in zip(refs[n_in:], res):
            r[...] = val.astype(r.dtype)

    spec = _bs((tr, cols), lambda i: (i, 0))
    return pl.pallas_call(
        body, name=name, grid=(rows // tr,),
        in_specs=[spec] * n_in, out_specs=[spec] * len(out_dtypes),
        out_shape=[_sds((rows, cols), dt) for dt in out_dtypes],
        compiler_params=_cparams(1),
    )(*ins)


def _adamw_fn(w, g, m, v):
    m = ADAM_B1 * m + (1.0 - ADAM_B1) * g
    v = ADAM_B2 * v + (1.0 - ADAM_B2) * (g * g)
    m_hat = m / (1.0 - ADAM_B1 ** ADAM_STEP)
    v_hat = v / (1.0 - ADAM_B2 ** ADAM_STEP)
    delta = -ADAM_LR * (m_hat / (jnp.sqrt(v_hat) + ADAM_EPS) + ADAM_WD * w)
    return delta, m, v


def _adamw(name, w, g, m, v):
    shape = w.shape
    as2d = lambda a: a.reshape(-1, shape[-1])
    delta, nm, nv = _elementwise(name, _adamw_fn, [as2d(w), as2d(g), as2d(m), as2d(v)], [F32, F32, F32])
    return delta.reshape(shape), nm.reshape(shape), nv.reshape(shape)


def _place():
    x, y, c = lax.axis_index("x"), lax.axis_index("y"), lax.axis_index("c")
    chips = [(1 - x, y), (x, 1 - y), (1 - x, 1 - y)]
    return x, y, c, chips


ANY = pl.BlockSpec(memory_space=pl.ANY)
VMEM_WHOLE = pl.BlockSpec(memory_space=pltpu.VMEM)


def _allgather_weights(shards):
    n = len(shards)

    def body(*refs):
        src, dst = refs[:n], refs[n:2 * n]
        send_sems, recv_sems, local_sems = refs[2 * n:]
        x, y, c, chips = _place()
        me, sibling, mychip = (x, y, c), (x, y, 1 - c), 2 * x + y

        x_nbr, y_nbr, diag = 2 * (1 - x) + y, 2 * x + (1 - y), 2 * (1 - x) + (1 - y)
        to_x, to_y = (1 - x, y, c), (x, 1 - y, c)

        def parts(w):
            hr = src[w].shape[0] // 2
            first = hr // 2 if hr % 32 == 0 else hr
            return first, hr - first

        def rows_of(w, chip, half, route):
            hr = src[w].shape[0] // 2
            first, second = parts(w)
            start, size = {0: (0, hr), 1: (0, hr), 2: (0, first), 3: (first, second)}[route]
            return dst[w].at[chip, pl.ds(half * hr + start, size)]

        def copy(w, k, src_ref, dst_ref, to):
            return pltpu.make_async_remote_copy(src_ref=src_ref, dst_ref=dst_ref, send_sem=send_sems.at[w, k],
                                                recv_sem=recv_sems.at[w, k], device_id=to, device_id_type=MESH)

        def landed(w, route):
            chip = {0: x_nbr, 1: y_nbr, 2: diag, 3: diag}[route]
            return rows_of(w, chip, c, route), chip

        def routes(w):
            return (0, 1, 2, 3) if parts(w)[1] else (0, 1, 2)

        started, local = [], []
        for w in range(n):
            hr = src[w].shape[0] // 2
            own = pltpu.make_async_copy(src[w], dst[w].at[mychip], local_sems.at[w])
            own.start()
            local.append(own)
            mine = src[w].at[pl.ds(c * hr, hr)]
            for route, to in ((0, to_x), (1, to_y)):
                cp = copy(w, route, mine, rows_of(w, mychip, c, route), to)
                cp.start()
                started.append(cp)

        def pass_on(w, route):
            got, chip = landed(w, route)
            copy(w, route, got, got, me).wait_recv()
            if route == 1:
                part = rows_of(w, chip, c, 2)
                started.append(copy(w, 2, part, part, to_x))
                started[-1].start()
            if route == 0 and parts(w)[1]:
                part = rows_of(w, chip, c, 3)
                started.append(copy(w, 3, part, part, to_y))
                started[-1].start()
            started.append(copy(w, 4 + route, got, got, sibling))
            started[-1].start()

        for w in range(n):
            pass_on(w, 1)
            pass_on(w, 0)
        for w in range(n):
            for route in routes(w)[2:]:
                pass_on(w, route)
        for w in range(n):
            for route in routes(w):
                chip = landed(w, route)[1]
                from_sib = rows_of(w, chip, 1 - c, route)
                copy(w, 4 + route, from_sib, from_sib, me).wait_recv()
        for cp in local:
            cp.wait()
        for cp in started:
            cp.wait_send()

    return pl.pallas_call(
        body, name="allgather_weights",
        in_specs=[VMEM_WHOLE] * n, out_specs=[VMEM_WHOLE] * n,
        out_shape=[_sds((N_CHIPS,) + s.shape, s.dtype) for s in shards],
        scratch_shapes=[pltpu.SemaphoreType.DMA((n, 8)), pltpu.SemaphoreType.DMA((n, 8)),
                        pltpu.SemaphoreType.DMA((n,))],
        compiler_params=pltpu.CompilerParams(vmem_limit_bytes=VMEM_LIMIT),
    )(*shards)


SUM_ROWS = 64


def _rs_pair_sum(name, grads):
    n = len(grads)

    def body(*refs):
        g, out = refs[:n], refs[n:2 * n]
        stage, land, keep = refs[2 * n:3 * n], refs[3 * n:4 * n], refs[4 * n:5 * n]
        send_sems, recv_sems, stage_sems, keep_sems = refs[5 * n:]
        x, y, c, _ = _place()
        sibling = (x, y, 1 - c)
        loads = []
        for w in range(n):
            hr = g[w].shape[1] // 2
            st = pltpu.make_async_copy(g[w].at[:, pl.ds((1 - c) * hr, hr)], stage[w], stage_sems.at[w])
            kp = pltpu.make_async_copy(g[w].at[:, pl.ds(c * hr, hr)], keep[w], keep_sems.at[w])
            st.start()
            kp.start()
            loads.append((st, kp))
        gives = []
        for w in range(n):
            loads[w][0].wait()
            give = pltpu.make_async_remote_copy(src_ref=stage[w], dst_ref=land[w], send_sem=send_sems.at[w],
                                                recv_sem=recv_sems.at[w], device_id=sibling, device_id_type=MESH)
            give.start()
            gives.append(give)
        for w in range(n):
            loads[w][1].wait()
            gives[w].wait_recv()
            nb = g[w].shape[1] // 2 // SUM_ROWS

            def add(idx, carry, w=w, nb=nb):
                k, r = idx // nb, pl.multiple_of((idx % nb) * SUM_ROWS, SUM_ROWS)
                rows = pl.ds(r, SUM_ROWS)
                out[w][k, rows, :] = (keep[w][k, rows, :] + land[w][k, rows, :]).astype(BF16)
                return carry

            lax.fori_loop(0, N_CHIPS * nb, add, 0)
        for give in gives:
            give.wait_send()

    half = [(N_CHIPS, a.shape[1] // 2, a.shape[2]) for a in grads]
    bufs = [pltpu.VMEM(s, F32) for s in half]
    sems = pltpu.SemaphoreType.DMA((n,))
    return pl.pallas_call(
        body, name=name,
        in_specs=[ANY] * n, out_specs=[VMEM_WHOLE] * n, out_shape=[_sds(s, BF16) for s in half],
        scratch_shapes=bufs + bufs + bufs + [sems, sems, sems, sems],
        compiler_params=pltpu.CompilerParams(vmem_limit_bytes=VMEM_LIMIT),
    )(*grads)


def _rs_exchange_join(parts):
    n = len(parts)

    def body(*refs):
        t, full = refs[:n], refs[n:2 * n]
        got_x, got_y, pass_on, got_2 = (refs[m * n:(m + 1) * n] for m in range(2, 6))
        send_sems, recv_sems = refs[6 * n:]
        x, y, c, _ = _place()
        mychip, sibling = 2 * x + y, (x, y, 1 - c)
        x_nbr, y_nbr, diag = 2 * (1 - x) + y, 2 * x + (1 - y), 2 * (1 - x) + (1 - y)
        to_x, to_y = (1 - x, y, c), (x, 1 - y, c)
        sends = []

        def copy(w, k, src_ref, dst_ref, to):
            return pltpu.make_async_remote_copy(src_ref=src_ref, dst_ref=dst_ref, send_sem=send_sems.at[w, k],
                                                recv_sem=recv_sems.at[w, k], device_id=to, device_id_type=MESH)

        def start(cp):
            cp.start()
            sends.append(cp)

        def add_rows(w, count, fn):
            def step(idx, carry):
                fn(pl.ds(pl.multiple_of(idx * SUM_ROWS, SUM_ROWS), SUM_ROWS), pl.multiple_of(idx * SUM_ROWS, SUM_ROWS))
                return carry
            lax.fori_loop(0, count // SUM_ROWS, step, 0)

        f32 = lambda v: v.astype(F32)
        for w in range(n):
            ha = t[w].shape[1] // 2
            part_a, part_b = pl.ds(0, ha), pl.ds(ha, ha)
            start(copy(w, 0, t[w].at[x_nbr, part_a], got_x[w].at[0], to_x))
            start(copy(w, 1, t[w].at[diag, part_a], got_x[w].at[1], to_x))
            start(copy(w, 2, t[w].at[y_nbr, part_b], got_y[w].at[0], to_y))
            start(copy(w, 3, t[w].at[diag, part_b], got_y[w].at[1], to_y))
        for w in range(n):
            hr = t[w].shape[1]
            ha = hr // 2
            for k in (0, 1):
                copy(w, k, got_x[w].at[k], got_x[w].at[k], to_x).wait_recv()

            def sum_a(rows, r, w=w, hr=hr):
                full[w][pl.ds(pl.multiple_of(c * hr + r, SUM_ROWS), SUM_ROWS), :] = \
                    f32(t[w][mychip, rows, :]) + f32(got_x[w][0, rows, :])
                pass_on[w][rows, :] = (f32(t[w][y_nbr, rows, :]) + f32(got_x[w][1, rows, :])).astype(BF16)

            add_rows(w, ha, sum_a)
            start(copy(w, 4, pass_on[w].at[pl.ds(0, ha)], got_2[w].at[pl.ds(0, ha)], to_y))
            for k in (2, 3):
                copy(w, k, got_y[w].at[k - 2], got_y[w].at[k - 2], to_y).wait_recv()

            def sum_b(rows, r, w=w, hr=hr, ha=ha):
                lower = pl.ds(pl.multiple_of(ha + r, SUM_ROWS), SUM_ROWS)
                full[w][pl.ds(pl.multiple_of(c * hr + ha + r, SUM_ROWS), SUM_ROWS), :] = \
                    f32(t[w][mychip, lower, :]) + f32(got_y[w][0, rows, :])
                pass_on[w][lower, :] = (f32(t[w][x_nbr, lower, :]) + f32(got_y[w][1, rows, :])).astype(BF16)

            add_rows(w, ha, sum_b)
            start(copy(w, 5, pass_on[w].at[pl.ds(ha, ha)], got_2[w].at[pl.ds(ha, ha)], to_x))
        for w in range(n):
            hr = t[w].shape[1]
            ha = hr // 2
            copy(w, 4, got_2[w].at[pl.ds(0, ha)], got_2[w].at[pl.ds(0, ha)], to_y).wait_recv()
            copy(w, 5, got_2[w].at[pl.ds(ha, ha)], got_2[w].at[pl.ds(ha, ha)], to_x).wait_recv()

            def finish(rows, r, w=w, hr=hr):
                out_rows = pl.ds(pl.multiple_of(c * hr + r, SUM_ROWS), SUM_ROWS)
                full[w][out_rows, :] = full[w][out_rows, :] + f32(got_2[w][rows, :])

            add_rows(w, hr, finish)
            mine = full[w].at[pl.ds(c * hr, hr)]
            start(copy(w, 6, mine, mine, sibling))
        for w in range(n):
            hr = t[w].shape[1]
            theirs = full[w].at[pl.ds((1 - c) * hr, hr)]
            copy(w, 6, theirs, theirs, sibling).wait_recv()
        for cp in sends:
            cp.wait_send()

    half = lambda a: pltpu.VMEM((2, a.shape[1] // 2, a.shape[2]), a.dtype)
    whole = lambda a: pltpu.VMEM(a.shape[1:], a.dtype)
    return pl.pallas_call(
        body, name="rs_exchange_join",
        in_specs=[VMEM_WHOLE] * n, out_specs=[VMEM_WHOLE] * n,
        out_shape=[_sds((2 * a.shape[1], a.shape[2]), F32) for a in parts],
        scratch_shapes=[half(a) for a in parts] + [half(a) for a in parts] + [whole(a) for a in parts]
        + [whole(a) for a in parts] + [pltpu.SemaphoreType.DMA((n, 7)), pltpu.SemaphoreType.DMA((n, 7))],
        compiler_params=pltpu.CompilerParams(vmem_limit_bytes=VMEM_LIMIT),
    )(*parts)


def _small_allreduce(loss_p, dg_parts, dbg_a, dbg_c, dwc):
    ins = [loss_p] + list(dg_parts) + [dbg_a, dbg_c, dwc]
    n_in = len(ins)
    vmem = pl.BlockSpec(memory_space=pltpu.VMEM)

    def body(*refs):
        in_refs = refs[:n_in]
        out_ref, vec, buf, send_sems, recv_sems = refs[n_in:]
        x, y, c, _ = _place()
        me = 4 * x + 2 * y + c
        vec[...] = jnp.zeros_like(vec)
        vec[0:1, :] = jnp.sum(in_refs[0][...], axis=0)
        for r in range(5):
            vec[1 + r:2 + r, :] = jnp.sum(in_refs[1 + r][...], axis=0)
        vec[6:7, :] = jnp.sum(in_refs[6][...], axis=0)
        vec[7:8, :] = jnp.sum(in_refs[7][...], axis=0)
        vec[8:16, 0:CONV_W] = jnp.sum(in_refs[8][...], axis=0)
        buf[pl.ds(me, 1)] = vec[...][None]
        copies = []
        for r in range(1, 8):
            fx, fy, fc = (r >> 2) & 1, (r >> 1) & 1, r & 1
            to = (1 - x if fx else x, 1 - y if fy else y, 1 - c if fc else c)
            cp = pltpu.make_async_remote_copy(src_ref=vec, dst_ref=buf.at[me], send_sem=send_sems.at[r - 1],
                                              recv_sem=recv_sems.at[r - 1], device_id=to, device_id_type=MESH)
            cp.start()
            copies.append(cp)
        for cp in copies:
            cp.wait()
        total = buf[0]
        for s in range(1, 8):
            total = total + buf[s]
        out_ref[...] = total
        out_ref[0:1, :] = jnp.broadcast_to(jnp.sum(total[0:1, :], axis=-1, keepdims=True), (1, D_MODEL))

    return pl.pallas_call(
        body, name="small_allreduce",
        in_specs=[vmem] * n_in, out_specs=vmem, out_shape=_sds((SMALL_ROWS, D_MODEL), F32),
        scratch_shapes=[pltpu.VMEM((SMALL_ROWS, D_MODEL), F32), pltpu.VMEM((8, SMALL_ROWS, D_MODEL), F32),
                        pltpu.SemaphoreType.DMA((7,)), pltpu.SemaphoreType.DMA((7,))],
    )(*ins)


def _local_step(x, p, tgt, g, b_gate, w_conv, wf):
    seq = x.shape[0]
    tm = min(seq, 1024)
    th = min(seq, 512)
    tl = min(seq, 2048)
    ni, nh, nl = seq // tm, seq // th, seq // tl
    g_pre_mix, g_post_mix, g_pre_mlp, g_post_mlp, g_ple = g
    w_in, w_ao, w_co, w_o, w_up, w_down, w_pg, w_pp, w_in_nat, w_up_nat = wf
    D = D_MODEL
    vec = lambda a, blk=0: (a, _bs((1, D), lambda i, j, k: (0, blk)))
    rows_i = lambda a, t, blk=0: (a, _bs((t, D), lambda i, j, k: (i, blk)))
    rows_k = lambda a, t, blk=0: (a, _bs((t, D), lambda i, j, k: (k, blk)))
    part = lambda n: (_sds((n, 1, D), F32), _bs((None, 1, D), lambda i, j, k: (i, 0, 0)))
    full2 = lambda a: (a, _bs(a.shape, lambda i, j, k: (0, 0)))

    normed = lambda xb, gb: (_rms(xb, gb).astype(BF16),) * 2
    keep_a = lambda t: [(_sds((seq, D), BF16), _bs((t, D), lambda i, j, k: (i, 0)))]
    main_w = D_IN - 2 * D
    proj, gates, h1 = _mm("proj_in", "nn", (nh, 1, 1),
                          a_ins=[rows_i(x, th), vec(g_pre_mix)], a_fn=normed,
                          b_ins=[full2(w_in_nat)], b_fn=_ident,
                          epi_fn=lambda acc: (acc[:, :main_w], acc[:, main_w:]),
                          outs=[(_sds((seq, main_w), F32), _bs((th, main_w), lambda i, j, k: (i, 0))),
                                (_sds((seq, 2 * D), BF16), _bs((th, 2 * D), lambda i, j, k: (i, 0)))],
                          acc_shape=(th, D_IN), a_cache=((th, D), BF16), a_outs=keep_a(th))
    o = _attn_fwd(proj, seq)
    (y_attn,) = _mm("attn_out", "nn", (ni, 1, 1),
                    a_ins=[(o, _bs((tm, ATTN_W), lambda i, j, k: (i, 0)))], a_fn=_ident,
                    b_ins=[full2(w_ao)], b_fn=_ident,
                    outs=[(_sds((seq, D), BF16), _bs((tm, D), lambda i, j, k: (i, 0)))], acc_shape=(tm, D))
    e, d = _conv_fwd(proj, w_conv, seq, tm)
    (y_conv,) = _mm("conv_out", "nn", (ni, 1, 1),
                    a_ins=[(e, _bs((tm, CONV_W), lambda i, j, k: (i, 0)))], a_fn=_ident,
                    b_ins=[full2(w_co)], b_fn=_ident,
                    outs=[(_sds((seq, D), BF16), _bs((tm, D), lambda i, j, k: (i, 0)))], acc_shape=(tm, D))

    def gate_values(ga, gc, ba, bc):
        return _sig(ga.astype(F32) + ba), _sig(gc.astype(F32) + bc)

    def mix_fn(ga, gc, ya, yc, ba, bc):
        sa, sc = gate_values(ga, gc, ba, bc)
        return ((sa * ya.astype(F32) + sc * yc.astype(F32)).astype(BF16),) * 2

    def post_mix(acc, xb, gb):
        return acc, xb + _rms(acc, gb)

    mix_ins = lambda rows: [rows(gates, th, 0), rows(gates, th, 1), rows(y_attn, th), rows(y_conv, th),
                            vec(b_gate, 0), vec(b_gate, 1)]
    mixed, x1, mixin = _mm("mix_out", "nn", (nh, 1, 1),
                           a_ins=mix_ins(rows_i), a_fn=mix_fn, b_ins=[full2(w_o)], b_fn=_ident,
                           epi_ins=[rows_i(x, th), vec(g_post_mix)], epi_fn=post_mix,
                           outs=[(_sds((seq, D), F32), _bs((th, D), lambda i, j, k: (i, 0)))] * 2,
                           acc_shape=(th, D), a_cache=((th, D), BF16), a_outs=keep_a(th))
    up, h2 = _mm("mlp_up", "nn", (nh, 1, 1),
                 a_ins=[rows_i(x1, th), vec(g_pre_mlp)], a_fn=normed,
                 b_ins=[full2(w_up_nat)], b_fn=_ident,
                 outs=[(_sds((seq, D_FF), BF16), _bs((th, D_FF), lambda i, j, k: (i, 0)))],
                 acc_shape=(th, D_FF), a_cache=((th, D), BF16), a_outs=keep_a(th))

    def relu2(ub):
        r = jnp.maximum(ub.astype(F32), 0.0)
        return (r * r).astype(BF16)

    f, x2 = _mm("mlp_down", "nn", (nh, 1, 1),
                a_ins=[(up, _bs((th, D_FF), lambda i, j, k: (i, 0)))], a_fn=relu2,
                b_ins=[full2(w_down)], b_fn=_ident,
                epi_ins=[rows_i(x1, th), vec(g_post_mlp)], epi_fn=post_mix,
                outs=[(_sds((seq, D), F32), _bs((th, D), lambda i, j, k: (i, 0)))] * 2, acc_shape=(th, D))
    (pp,) = _mm("ple_proj", "nn", (ni, 1, 1),
                a_ins=[(p, _bs((tm, PLE_DIM), lambda i, j, k: (i, 0)))], a_fn=_to_bf16,
                b_ins=[full2(w_pp)], b_fn=_ident,
                outs=[(_sds((seq, D), F32), _bs((tm, D), lambda i, j, k: (i, 0)))], acc_shape=(tm, D))

    def head(acc, x2b, ppb, tb):
        pg = _sig(acc)
        err = x2b + pg * ppb - tb
        return pg, err * (1.0 / D), jnp.sum(err * err, axis=0, keepdims=True) * (0.5 / D)

    pg, dx3, loss_p, h3 = _mm("ple_gate_loss", "nn", (nh, 1, 1),
                              a_ins=[rows_i(x2, th), vec(g_ple)], a_fn=normed,
                              b_ins=[full2(w_pg)], b_fn=_ident,
                              epi_a=(0,), epi_ins=[rows_i(pp, th), rows_i(tgt, th)], epi_fn=head,
                              outs=[(_sds((seq, D), BF16), _bs((th, D), lambda i, j, k: (i, 0))),
                                    (_sds((seq, D), F32), _bs((th, D), lambda i, j, k: (i, 0))), part(nh)],
                              acc_shape=(th, D), a_cache=((th, D), BF16), a_outs=keep_a(th))

    (dw_pp,) = _mm("dw_ple_proj", "tn", (1, 1, nh),
                   a_ins=[(p, _bs((th, PLE_DIM), lambda i, j, k: (k, 0)))], a_fn=_to_bf16,
                   b_ins=[rows_k(dx3, th), rows_k(pg, th)], b_fn=lambda a, b: (a * b.astype(F32)).astype(BF16),
                   outs=[(_sds((PLE_DIM, D), F32), _bs((PLE_DIM, D), lambda i, j, k: (0, 0)))],
                   acc_shape=(PLE_DIM, D))

    def dpre_fn(dx3b, ppb, pgb):
        pgf = pgb.astype(F32)
        return (dx3b * ppb * pgf * (1.0 - pgf)).astype(BF16)

    def ple_norm_bwd(acc, dx3b, x2b, gb, fb, g_mlp):
        dxn, dg = _rms_bwd(x2b, gb, acc)
        dx2b = dx3b + dxn
        dfb, dg_mlp = _rms_bwd(fb, g_mlp, dx2b)
        return dx2b, dg, dfb, dg_mlp

    dx2, dg_ple_p, df, dg_post_mlp_p, dpre = _mm(
        "d_ple_gate", "nt", (nh, 1, 1),
        a_ins=[rows_i(dx3, th), rows_i(pp, th), rows_i(pg, th)], a_fn=lambda a, b, c: (dpre_fn(a, b, c),) * 2,
        b_ins=[full2(w_pg)], b_fn=_ident,
        epi_a=(0,), epi_ins=[rows_i(x2, th), vec(g_ple), rows_i(f, th), vec(g_post_mlp)], epi_fn=ple_norm_bwd,
        outs=[(_sds((seq, D), F32), _bs((th, D), lambda i, j, k: (i, 0))), part(nh),
              (_sds((seq, D), BF16), _bs((th, D), lambda i, j, k: (i, 0))), part(nh)],
        acc_shape=(th, D), a_cache=((th, D), BF16),
        a_outs=[(_sds((seq, D), BF16), _bs((th, D), lambda i, j, k: (i, 0)))])
    (dw_pg,) = _mm("dw_ple_gate", "tn", (1, 1, ni),
                   a_ins=[rows_k(h3, tm)], a_fn=_ident, b_ins=[rows_k(dpre, tm)], b_fn=_ident,
                   outs=[(_sds((D, D), F32), _bs((D, D), lambda i, j, k: (0, 0)))], acc_shape=(D, D))

    def dup_fn(acc, ub):
        return (acc * (2.0 * jnp.maximum(ub.astype(F32), 0.0)),)

    (dup,) = _mm("d_mlp_down", "nt", (nh, 1, 1),
                 a_ins=[rows_i(df, th)], a_fn=_ident, b_ins=[full2(w_down)], b_fn=_ident,
                 epi_ins=[(up, _bs((th, D_FF), lambda i, j, k: (i, 0)))], epi_fn=dup_fn,
                 outs=[(_sds((seq, D_FF), BF16), _bs((th, D_FF), lambda i, j, k: (i, 0)))],
                 acc_shape=(th, D_FF))
    (dw_down,) = _mm("dw_mlp_down", "tn", (4, 1, nl),
                     a_ins=[(up, _bs((tl, D), lambda i, j, k: (k, i)))], a_fn=relu2,
                     b_ins=[rows_k(df, tl)], b_fn=_ident,
                     outs=[(_sds((D_FF, D), F32), _bs((D, D), lambda i, j, k: (i, 0)))], acc_shape=(D, D))
    (dw_up,) = _mm("dw_mlp_up", "tn", (1, 4, nl),
                   a_ins=[rows_k(h2, tl)], a_fn=_ident,
                   b_ins=[(dup, _bs((tl, D), lambda i, j, k: (k, j)))], b_fn=_ident,
                   outs=[(_sds((N_CHIPS, D, D), F32), _bs((None, D, D), lambda i, j, k: (j, 0, 0)))],
                   acc_shape=(D, D))

    def mlp_norm_bwd(acc, x1b, dx2b, mixedb, g_mlp, g_mix):
        dxn, dg_mlp = _rms_bwd(x1b, g_mlp, acc)
        dx1b = dx2b + dxn
        dmixedb, dg_mix = _rms_bwd(mixedb, g_mix, dx1b)
        return dx1b, dmixedb, dg_mlp, dg_mix

    dx1, dmixed, dg_pre_mlp_p, dg_post_mix_p = _mm(
        "d_mlp_up", "nt", (nh, 1, 1),
        a_ins=[(dup, _bs((th, D_FF), lambda i, j, k: (i, 0)))], a_fn=_ident,
        b_ins=[full2(w_up_nat)], b_fn=_ident,
        epi_ins=[rows_i(x1, th), rows_i(dx2, th), rows_i(mixed, th), vec(g_pre_mlp), vec(g_post_mix)],
        epi_fn=mlp_norm_bwd,
        outs=[(_sds((seq, D), F32), _bs((th, D), lambda i, j, k: (i, 0))),
              (_sds((seq, D), BF16), _bs((th, D), lambda i, j, k: (i, 0))), part(nh), part(nh)],
        acc_shape=(th, D))
    (dw_o,) = _mm("dw_mix_out", "tn", (1, 1, ni),
                  a_ins=[rows_k(mixin, tm)], a_fn=_ident, b_ins=[rows_k(dmixed, tm)], b_fn=_ident,
                  outs=[(_sds((D, D), F32), _bs((D, D), lambda i, j, k: (0, 0)))], acc_shape=(D, D))

    def gate_bwd(acc, ga, gc, ya, yc, ba, bc):
        sa, sc = gate_values(ga, gc, ba, bc)
        dga = acc * ya.astype(F32) * sa * (1.0 - sa)
        dgc = acc * yc.astype(F32) * sc * (1.0 - sc)
        return (acc * sa, acc * sc, jnp.concatenate([dga, dgc], axis=1),
                jnp.sum(dga, axis=0, keepdims=True), jnp.sum(dgc, axis=0, keepdims=True))

    dya, dyc, dgate, dbg_a_p, dbg_c_p = _mm(
        "d_mix_out", "nt", (nh, 1, 1),
        a_ins=[rows_i(dmixed, th)], a_fn=_ident, b_ins=[full2(w_o)], b_fn=_ident,
        epi_ins=mix_ins(rows_i), epi_fn=gate_bwd,
        outs=[(_sds((seq, D), BF16), _bs((th, D), lambda i, j, k: (i, 0)))] * 2
             + [(_sds((seq, 2 * D), BF16), _bs((th, 2 * D), lambda i, j, k: (i, 0))), part(nh), part(nh)],
        acc_shape=(th, D))
    (dw_ao,) = _mm("dw_attn_out", "tn", (1, 1, nh),
                   a_ins=[(o, _bs((th, ATTN_W), lambda i, j, k: (k, 0)))], a_fn=_ident,
                   b_ins=[rows_k(dya, th)], b_fn=_ident,
                   outs=[(_sds((ATTN_W, D), F32), _bs((ATTN_W, D), lambda i, j, k: (0, 0)))], acc_shape=(ATTN_W, D))
    (do,) = _mm("d_attn_out", "nt", (ni, 1, 1),
                a_ins=[rows_i(dya, tm)], a_fn=_ident, b_ins=[full2(w_ao)], b_fn=_ident,
                outs=[(_sds((seq, ATTN_W), BF16), _bs((tm, ATTN_W), lambda i, j, k: (i, 0)))],
                acc_shape=(tm, ATTN_W))
    dq, dk, dv = _attn_bwd(proj, do, seq)
    (dw_co,) = _mm("dw_conv_out", "tn", (1, 1, nh),
                   a_ins=[(e, _bs((th, CONV_W), lambda i, j, k: (k, 0)))], a_fn=_ident,
                   b_ins=[rows_k(dyc, th)], b_fn=_ident,
                   outs=[(_sds((CONV_W, D), F32), _bs((CONV_W, D), lambda i, j, k: (0, 0)))], acc_shape=(CONV_W, D))
    (de,) = _mm("d_conv_out", "nt", (ni, 1, 1),
                a_ins=[rows_i(dyc, tm)], a_fn=_ident, b_ins=[full2(w_co)], b_fn=_ident,
                outs=[(_sds((seq, CONV_W), F32), _bs((tm, CONV_W), lambda i, j, k: (i, 0)))],
                acc_shape=(tm, CONV_W))
    dconv, dwc_p = _conv_bwd(proj, de, d, w_conv, seq, tm)
    qkv_w = 3 * ATTN_W
    join_bf16 = lambda *blocks: jnp.concatenate([b.astype(BF16) for b in blocks], axis=1)
    piece = lambda a, t, rows, blk=0: (a, _bs((t, a.shape[1]), (lambda i, j, k: (k, blk)) if rows == "k"
                                             else (lambda i, j, k: (i, blk))))
    (dw_in_qkv,) = _mm("dw_proj_in_qkv", "tn", (1, 1, ni),
                       a_ins=[rows_k(h1, tm)], a_fn=_ident,
                       b_ins=[piece(dq, tm, "k"), piece(dk, tm, "k"), piece(dv, tm, "k")], b_fn=join_bf16,
                       outs=[(_sds((D, qkv_w), F32), _bs((D, qkv_w), lambda i, j, k: (0, 0)))], acc_shape=(D, qkv_w))
    (dw_in_conv,) = _mm("dw_proj_in_conv", "tn", (1, 1, nl),
                        a_ins=[rows_k(h1, tl)], a_fn=_ident, b_ins=[piece(dconv, tl, "k")], b_fn=_ident,
                        outs=[(_sds((D, 3 * CONV_W), F32), _bs((D, 3 * CONV_W), lambda i, j, k: (0, 0)))],
                        acc_shape=(D, 3 * CONV_W))
    (dw_in_gate,) = _mm("dw_proj_in_gate", "tn", (1, 2, nl),
                        a_ins=[rows_k(h1, tl)], a_fn=_ident,
                        b_ins=[(dgate, _bs((tl, D), lambda i, j, k: (k, j)))], b_fn=_ident,
                        outs=[(_sds((D, 2 * D), F32), _bs((D, D), lambda i, j, k: (0, j)))], acc_shape=(D, D))
    dw_in = jnp.concatenate([dw_in_qkv, dw_in_conv, dw_in_gate], axis=1)

    def in_norm_bwd(acc, xb, dx1b, gb):
        dxn, dg = _rms_bwd(xb, gb, acc)
        return dx1b + dxn, dg

    grad_x, dg_pre_mix_p = _mm("d_proj_in", "nt", (nh, 1, 1),
                               a_ins=[piece(dq, th, "i"), piece(dk, th, "i"), piece(dv, th, "i"),
                                      piece(dconv, th, "i"), piece(dgate, th, "i")], a_fn=join_bf16,
                               b_ins=[full2(w_in_nat)], b_fn=_ident,
                               epi_ins=[rows_i(x, th), rows_i(dx1, th), vec(g_pre_mix)], epi_fn=in_norm_bwd,
                               outs=[(_sds((seq, D), F32), _bs((th, D), lambda i, j, k: (i, 0))), part(nh)],
                               acc_shape=(th, D))

    chip_major = lambda a: a.reshape(a.shape[0], N_CHIPS, a.shape[1] // N_CHIPS).transpose(1, 0, 2)
    big = [chip_major(dw_in), chip_major(dw_ao), chip_major(dw_co), dw_o.reshape(N_CHIPS, D // N_CHIPS, D), dw_up,
           dw_down.reshape(N_CHIPS, D_FF // N_CHIPS, D), dw_pg.reshape(N_CHIPS, D // N_CHIPS, D), chip_major(dw_pp)]
    small = (loss_p, [dg_pre_mix_p, dg_post_mix_p, dg_pre_mlp_p, dg_post_mlp_p, dg_ple_p], dbg_a_p, dbg_c_p, dwc_p)
    return grad_x, big, small


RS_GROUPS = ((0,), (4,), (5,), (1, 2, 3, 6, 7))


def _reduce_scatter(big):
    pair = [None] * len(big)
    for gi, group in enumerate(RS_GROUPS):
        for w, s in zip(group, _rs_pair_sum(f"rs_pair_sum_{gi}", [big[w] for w in group])):
            pair[w] = s
    return _rs_exchange_join(pair)


def kernel(x, p, g_pre_mix, w_in, b_gate, w_conv, w_attn_out, w_conv_out, w_o, g_post_mix, g_pre_mlp, w_up, w_down, g_post_mlp, g_ple, w_ple_gate, w_ple_proj, loss_target, m_g_pre_mix, m_w_in, m_b_gate, m_w_conv, m_w_attn_out, m_w_conv_out, m_w_o, m_g_post_mix, m_g_pre_mlp, m_w_up, m_w_down, m_g_post_mlp, m_g_ple, m_w_ple_gate, m_w_ple_proj, v_g_pre_mix, v_w_in, v_b_gate, v_w_conv, v_w_attn_out, v_w_conv_out, v_w_o, v_g_post_mix, v_g_pre_mlp, v_w_up, v_w_down, v_g_post_mlp, v_g_ple, v_w_ple_gate, v_w_ple_proj):
    mats = [w_in, w_attn_out, w_conv_out, w_o, w_up, w_down, w_ple_gate, w_ple_proj]
    mats_m = [m_w_in, m_w_attn_out, m_w_conv_out, m_w_o, m_w_up, m_w_down, m_w_ple_gate, m_w_ple_proj]
    mats_v = [v_w_in, v_w_attn_out, v_w_conv_out, v_w_o, v_w_up, v_w_down, v_w_ple_gate, v_w_ple_proj]
    gains = [g_pre_mix, g_post_mix, g_pre_mlp, g_post_mlp, g_ple]
    gains_m = [m_g_pre_mix, m_g_post_mix, m_g_pre_mlp, m_g_post_mlp, m_g_ple]
    gains_v = [v_g_pre_mix, v_g_post_mix, v_g_pre_mlp, v_g_post_mlp, v_g_ple]

    taps = jnp.concatenate([w_conv[0], jnp.zeros((CONV_PAD_ROWS - 3, LANES), F32)], axis=0)
    gathered = _allgather_weights([w[0].astype(BF16) for w in mats] + [taps])
    cols_joined = lambda a: a.transpose(1, 0, 2).reshape(a.shape[1], N_CHIPS * a.shape[2])
    rows_joined = lambda a: a.reshape(N_CHIPS * a.shape[1], a.shape[2])
    wf = [gathered[0], cols_joined(gathered[1]), cols_joined(gathered[2]), rows_joined(gathered[3]), gathered[4],
          rows_joined(gathered[5]), rows_joined(gathered[6]), cols_joined(gathered[7]),
          cols_joined(gathered[0]), cols_joined(gathered[4])]
    w_conv_full = cols_joined(gathered[8])[0:3, :]
    chip = 2 * lax.axis_index("x") + lax.axis_index("y")

    grad_x, big, small = _local_step(x[0], p[0, 0], loss_target[0], gains, b_gate, w_conv_full, wf)

    shard_grads = _reduce_scatter(big)
    red = _small_allreduce(*small)
    loss = red[0, 0]
    grad_gains = [red[1 + r:2 + r, :] for r in range(5)]
    grad_b_gate = jnp.concatenate([red[6:7, :], red[7:8, :]], axis=1)
    grad_w_conv = lax.dynamic_slice(red[8:11, :], (0, chip * LANES), (3, LANES))[None]

    grads_big = [gr.reshape(w.shape) for gr, w in zip(shard_grads, mats)]
    upd_big = [_adamw(f"adamw_{i}", w, gr, m, v) for i, (w, gr, m, v) in enumerate(zip(mats, grads_big, mats_m, mats_v))]
    pack = lambda vs, bg: jnp.concatenate(list(vs) + [bg.reshape(2, D_MODEL), jnp.zeros((1, D_MODEL), F32)], axis=0)
    upd_small = _adamw("adamw_small", pack(gains, b_gate), pack(grad_gains, grad_b_gate),
                       pack(gains_m, m_b_gate), pack(gains_v, v_b_gate))
    upd_conv = _adamw("adamw_conv", w_conv, grad_w_conv, m_w_conv, v_w_conv)

    def small_out(a, which):
        gains_out = [a[r:r + 1, :] for r in range(5)]
        return gains_out, a[5:7, :].reshape(1, 2 * D_MODEL)

    def ordered(g_pre_mix_, big_, b_gate_, conv_, g_rest):
        return [g_pre_mix_, big_[0], b_gate_, conv_, big_[1], big_[2], big_[3], g_rest[0], g_rest[1], big_[4], big_[5],
                g_rest[2], g_rest[3], big_[6], big_[7]]

    outs = [loss, grad_x[None]]
    outs += ordered(grad_gains[0], grads_big, grad_b_gate, grad_w_conv, grad_gains[1:])
    for which in range(3):
        g_out, b_out = small_out(upd_small[which], which)
        outs += ordered(g_out[0], [u[which] for u in upd_big], b_out, upd_conv[which], g_out[1:])
    return tuple(outs)
```

```python
import functools

import jax
import jax.numpy as jnp
from jax import lax
from jax.experimental import pallas as pl
from jax.experimental.pallas import tpu as pltpu

F32 = jnp.float32
BF16 = jnp.bfloat16
MESH = pl.DeviceIdType.MESH

D_MODEL = 1024
N_HEADS = 8
HEAD_DIM = 64
ATTN_W = N_HEADS * HEAD_DIM
CONV_W = 512
D_FF = 4096
PLE_DIM = 256
D_IN = 5120
N_CHIPS = 4
EPS = 1e-6
Q_SCALE = HEAD_DIM ** -0.5

ADAM_LR = 0.001
ADAM_B1 = 0.9
ADAM_B2 = 0.999
ADAM_EPS = 1e-08
ADAM_WD = 0.01
ADAM_STEP = 10

V7X_VMEM_BYTES = 64 * 1024 * 1024
VMEM_LIMIT = V7X_VMEM_BYTES - 8 * 1024 * 1024
LANES = 128
ATT_BLK = 256
SMALL_ROWS = 16
CONV_PAD_ROWS = 16


def _cparams(n_grid):
    return pltpu.CompilerParams(dimension_semantics=("arbitrary",) * n_grid, vmem_limit_bytes=VMEM_LIMIT)


def _bs(shape, fn):
    return pl.BlockSpec(shape, fn)


def _rms_stats(xf):
    return lax.rsqrt(jnp.mean(xf * xf, axis=-1, keepdims=True) + EPS)


def _rms(xf, g):
    return xf * _rms_stats(xf) * g


def _rms_bwd(xf, g, dy):
    r = _rms_stats(xf)
    xh = xf * r
    dyg = dy * g
    dx = r * (dyg - xh * jnp.mean(dyg * xh, axis=-1, keepdims=True))
    return dx, jnp.sum(dy * xh, axis=0, keepdims=True)


def _sig(z):
    return 1.0 / (1.0 + jnp.exp(-z))


def _ident(a):
    return a


def _to_bf16(a):
    return a.astype(BF16)


_DIMS = {"nn": (((1,), (0,)), ((), ())), "nt": (((1,), (1,)), ((), ())), "tn": (((0,), (0,)), ((), ()))}


def _mm(name, mode, grid, a_ins, a_fn, b_ins, b_fn, outs, acc_shape, epi_ins=(), epi_fn=None,
        a_cache=None, a_outs=(), epi_a=()):
    nk = grid[2]
    na, nb, ne, no, nao = len(a_ins), len(b_ins), len(epi_ins), len(outs), len(a_outs)
    assert a_cache is None or nk == 1
    assert not a_outs or a_cache is not None
    dims = _DIMS[mode]
    if epi_fn is None:
        epi_fn = lambda acc: (acc,)

    def body(*refs):
        a_refs = refs[:na]
        b_refs = refs[na:na + nb]
        e_refs = refs[na + nb:na + nb + ne]
        o_refs = refs[na + nb + ne:na + nb + ne + no]
        ao_refs = refs[na + nb + ne + no:na + nb + ne + no + nao]
        scratch = list(refs[na + nb + ne + no + nao:])
        acc_ref = scratch.pop(0) if nk > 1 else None
        a_sc = scratch.pop(0) if a_cache is not None else None
        j = pl.program_id(1)
        k = pl.program_id(2)

        def finish(acc):
            res = epi_fn(acc, *[a_refs[t][...] for t in epi_a], *[r[...] for r in e_refs])
            for r, val in zip(o_refs, res):
                r[...] = val.astype(r.dtype)

        if a_sc is not None:
            @pl.when(j == 0)
            def _():
                res = a_fn(*[r[...] for r in a_refs])
                if nao:
                    for r, val in zip(ao_refs, res[1:]):
                        r[...] = val.astype(r.dtype)
                    res = res[0]
                a_sc[...] = res
            a = a_sc[...]
        else:
            a = a_fn(*[r[...] for r in a_refs])
        b = b_fn(*[r[...] for r in b_refs])
        prod = lax.dot_general(a, b, dims, preferred_element_type=F32)
        if nk == 1:
            finish(prod)
        else:
            @pl.when(k == 0)
            def _():
                acc_ref[...] = prod

            @pl.when(k > 0)
            def _():
                acc_ref[...] += prod

            @pl.when(k == nk - 1)
            def _():
                finish(acc_ref[...])

    scratch_shapes = []
    if nk > 1:
        scratch_shapes.append(pltpu.VMEM(acc_shape, F32))
    if a_cache is not None:
        scratch_shapes.append(pltpu.VMEM(*a_cache))
    all_outs = list(outs) + list(a_outs)
    res = pl.pallas_call(
        body, name=name, grid=grid,
        in_specs=[s for _, s in a_ins] + [s for _, s in b_ins] + [s for _, s in epi_ins],
        out_specs=[s for _, s in all_outs],
        out_shape=[o for o, _ in all_outs],
        scratch_shapes=scratch_shapes,
        compiler_params=_cparams(3),
    )(*[a for a, _ in a_ins], *[a for a, _ in b_ins], *[a for a, _ in epi_ins])
    return res


def _sds(shape, dtype):
    return jax.ShapeDtypeStruct(shape, dtype)


def _nt(a, b):
    return lax.dot_general(a, b, _DIMS["nt"], preferred_element_type=F32)


def _tn(a, b):
    return lax.dot_general(a, b, _DIMS["tn"], preferred_element_type=F32)


def _nn(a, b):
    return lax.dot_general(a, b, _DIMS["nn"], preferred_element_type=F32)


def _ple_head(x2, p, tgt, f, g_ple, g_post_mlp, w_pg, w_pp, seq, tr):
    nblk = seq // tr
    D = D_MODEL

    def body(x2_ref, p_ref, t_ref, f_ref, gp_ref, gm_ref, wpg_ref, wpp_ref,
             dx2_ref, df_ref, dpre_ref, h3_ref, dpp_ref, loss_ref, dgp_ref, dgm_ref):
        x2b, gp, wpg = x2_ref[...], gp_ref[...], wpg_ref[...]
        h3 = _rms(x2b, gp).astype(BF16)
        h3_ref[...] = h3
        gate = _sig(_nn(h3, wpg))
        pp = _nn(p_ref[...].astype(BF16), wpp_ref[...])
        err = x2b + gate * pp - t_ref[...]
        loss_ref[...] = jnp.sum(err * err, axis=0, keepdims=True) * (0.5 / D)
        dx3 = err * (1.0 / D)
        dpp_ref[...] = (dx3 * gate).astype(BF16)
        dpre = (dx3 * pp * gate * (1.0 - gate)).astype(BF16)
        dpre_ref[...] = dpre
        dxn, dgp = _rms_bwd(x2b, gp, _nt(dpre, wpg))
        dx2 = dx3 + dxn
        dx2_ref[...] = dx2
        dgp_ref[...] = dgp
        dfb, dgm = _rms_bwd(f_ref[...], gm_ref[...], dx2)
        df_ref[...] = dfb.astype(BF16)
        dgm_ref[...] = dgm

    rows = _bs((tr, D), lambda i: (i, 0))
    vec = _bs((1, D), lambda i: (0, 0))
    part = _bs((None, 1, D), lambda i: (i, 0, 0))
    return pl.pallas_call(
        body, name="ple_head", grid=(nblk,),
        in_specs=[rows, _bs((tr, PLE_DIM), lambda i: (i, 0)), rows, rows, vec, vec,
                  _bs((D, D), lambda i: (0, 0)), _bs((PLE_DIM, D), lambda i: (0, 0))],
        out_specs=[rows] * 5 + [part] * 3,
        out_shape=[_sds((seq, D), F32)] + [_sds((seq, D), BF16)] * 4 + [_sds((nblk, 1, D), F32)] * 3,
        compiler_params=_cparams(1),
    )(x2, p, tgt, f, g_ple, g_post_mlp, w_pg, w_pp)


def _shift_rows_down(u, prev, n):
    rows = u.shape[0]
    ridx = lax.broadcasted_iota(jnp.int32, u.shape, 0)
    out = pltpu.roll(u, n, 0)
    for r in range(n):
        out = jnp.where(ridx == r, prev[8 - n + r:8 - n + r + 1, :], out)
    del rows
    return out


def _shift_rows_up(u, nxt, n):
    rows = u.shape[0]
    ridx = lax.broadcasted_iota(jnp.int32, u.shape, 0)
    out = pltpu.roll(u, rows - n, 0)
    for r in range(n):
        out = jnp.where(ridx == rows - n + r, nxt[r:r + 1, :], out)
    return out


CONV_COL0 = 3


def _conv_fwd(proj, w_conv, seq, tr):
    hb = tr // 8

    def body(cb_ref, cc_ref, cu_ref, ccp_ref, cup_ref, w_ref, e_ref, d_ref):
        i = pl.program_id(0)
        u = cc_ref[...] * cu_ref[...]
        up = jnp.where(i > 0, ccp_ref[...] * cup_ref[...], 0.0)
        w = w_ref[...]
        d = w[0:1, :] * _shift_rows_down(u, up, 2) + w[1:2, :] * _shift_rows_down(u, up, 1) + w[2:3, :] * u
        d_ref[...] = d
        e_ref[...] = (cb_ref[...] * d).astype(BF16)

    prev = lambda c: (lambda i: (jnp.maximum(i * hb - 1, 0), c))
    return pl.pallas_call(
        body, name="conv_fwd", grid=(seq // tr,),
        in_specs=[_bs((tr, CONV_W), lambda i: (i, CONV_COL0)),
                  _bs((tr, CONV_W), lambda i: (i, CONV_COL0 + 1)),
                  _bs((tr, CONV_W), lambda i: (i, CONV_COL0 + 2)),
                  _bs((8, CONV_W), prev(CONV_COL0 + 1)),
                  _bs((8, CONV_W), prev(CONV_COL0 + 2)),
                  _bs((3, CONV_W), lambda i: (0, 0))],
        out_specs=[_bs((tr, CONV_W), lambda i: (i, 0)), _bs((tr, CONV_W), lambda i: (i, 0))],
        out_shape=[_sds((seq, CONV_W), BF16), _sds((seq, CONV_W), F32)],
        compiler_params=_cparams(1),
    )(proj, proj, proj, proj, proj, w_conv)


def _conv_bwd(proj, de, d, w_conv, seq, tr):
    hb = tr // 8
    nblk = seq // tr

    def body(cb_ref, cc_ref, cu_ref, ccp_ref, cup_ref, cbn_ref, de_ref, den_ref, d_ref, w_ref, o_ref, dw_ref):
        i = pl.program_id(0)
        cc, cu, cb = cc_ref[...], cu_ref[...], cb_ref[...]
        u = cc * cu
        up = jnp.where(i > 0, ccp_ref[...] * cup_ref[...], 0.0)
        u1 = _shift_rows_down(u, up, 1)
        u2 = _shift_rows_down(u, up, 2)
        de_ = de_ref[...]
        dd = de_ * cb
        ddn = jnp.where(i < nblk - 1, den_ref[...] * cbn_ref[...], 0.0)
        w = w_ref[...]
        du = w[2:3, :] * dd + w[1:2, :] * _shift_rows_up(dd, ddn, 1) + w[0:1, :] * _shift_rows_up(dd, ddn, 2)
        o_ref[:, 0:CONV_W] = (de_ * d_ref[...]).astype(BF16)
        o_ref[:, CONV_W:2 * CONV_W] = (du * cu).astype(BF16)
        o_ref[:, 2 * CONV_W:3 * CONV_W] = (du * cc).astype(BF16)
        ridx = lax.broadcasted_iota(jnp.int32, (8, CONV_W), 0)
        dw0 = jnp.sum(dd * u2, axis=0, keepdims=True)
        dw1 = jnp.sum(dd * u1, axis=0, keepdims=True)
        dw2 = jnp.sum(dd * u, axis=0, keepdims=True)
        dw_ref[...] = jnp.where(ridx == 0, dw0, jnp.where(ridx == 1, dw1, jnp.where(ridx == 2, dw2, 0.0)))

    prev = lambda c: (lambda i: (jnp.maximum(i * hb - 1, 0), c))
    nxt = lambda c: (lambda i: (jnp.minimum((i + 1) * hb, seq // 8 - 1), c))
    return pl.pallas_call(
        body, name="conv_bwd", grid=(nblk,),
        in_specs=[_bs((tr, CONV_W), lambda i: (i, CONV_COL0)),
                  _bs((tr, CONV_W), lambda i: (i, CONV_COL0 + 1)),
                  _bs((tr, CONV_W), lambda i: (i, CONV_COL0 + 2)),
                  _bs((8, CONV_W), prev(CONV_COL0 + 1)),
                  _bs((8, CONV_W), prev(CONV_COL0 + 2)),
                  _bs((8, CONV_W), nxt(CONV_COL0)),
                  _bs((tr, CONV_W), lambda i: (i, 0)),
                  _bs((8, CONV_W), nxt(0)),
                  _bs((tr, CONV_W), lambda i: (i, 0)),
                  _bs((3, CONV_W), lambda i: (0, 0))],
        out_specs=[_bs((tr, 3 * CONV_W), lambda i: (i, 0)), _bs((None, 8, CONV_W), lambda i: (i, 0, 0))],
        out_shape=[_sds((seq, 3 * CONV_W), BF16), _sds((nblk, 8, CONV_W), F32)],
        compiler_params=_cparams(1),
    )(proj, proj, proj, proj, proj, proj, de, de, d, w_conv)


def _log_gates(z):
    lse = jnp.log(1.0 + jnp.exp(-jnp.abs(z)))
    log_beta = jnp.minimum(z, 0.0) - lse
    return log_beta, log_beta - z


DEAD_LOG_WEIGHT = -110.0
NO_TILE = -1e30


def _first_live_tile(start, scores, live_sc):
    def alive():
        return jnp.max(jnp.maximum(live_sc[0], live_sc[1])) > DEAD_LOG_WEIGHT

    def step(c):
        for h, z in enumerate(scores(c[0])):
            live_sc[h] = live_sc[h] + jnp.sum(_log_gates(z)[1], axis=-1, keepdims=True)
        return c[0] - 1, alive()

    j_end, _ = lax.while_loop(lambda c: jnp.logical_and(c[0] >= 0, c[1]), step, (start, alive()))
    return j_end + 1


def _attn_fwd(proj, seq):
    blk = ATT_BLK
    nq = seq // blk
    npair = N_HEADS // 2

    def body(q_ref, k_ref, v_ref, o_ref, z0_sc, z1_sc, w0_sc, w1_sc, tot_sc, live_sc, acc_sc):
        i = pl.program_id(1)
        is_a = lax.broadcasted_iota(jnp.int32, (1, LANES), 1) < HEAD_DIM
        q2 = (q_ref[...] * Q_SCALE).astype(BF16)
        zero = jnp.zeros_like(q2)
        qs = (jnp.where(is_a, q2, zero), jnp.where(is_a, zero, q2))
        row = lax.broadcasted_iota(jnp.int32, (blk, blk), 0)
        col = lax.broadcasted_iota(jnp.int32, (blk, blk), 1)
        tri = (row > col).astype(BF16)
        causal = col < row

        def tile_of(ref, j):
            return ref[pl.ds(pl.multiple_of(j * blk, blk), blk), :].astype(BF16)

        def scores(j):
            k2 = tile_of(k_ref, j)
            return [_nt(qs[h], k2) for h in range(2)]

        has_left = i > 0
        left = jnp.maximum(i - 1, 0)
        g_d = [_log_gates(z) for z in scores(i)]
        g_l = [_log_gates(z) for z in scores(left)]
        keep_d = [jnp.where(causal, g[1], 0.0) for g in g_d]
        suf_d = [_nn(lk.astype(BF16), tri) for lk in keep_d]
        suf_l = [_nn(g[1].astype(BF16), tri) for g in g_l]
        v_d, v_l = tile_of(v_ref, i), tile_of(v_ref, left)
        pv = []
        for h in range(2):
            sum_d = jnp.sum(keep_d[h], axis=-1, keepdims=True)
            w_d = jnp.where(causal, jnp.exp(g_d[h][0] + suf_d[h]), 0.0)
            w_l = jnp.exp(g_l[h][0] + (jnp.where(has_left, sum_d, NO_TILE) + suf_l[h]))
            pv.append(_nn(w_d.astype(BF16), v_d) + _nn(w_l.astype(BF16), v_l))
            tot_sc[h] = sum_d + jnp.sum(g_l[h][1], axis=-1, keepdims=True)
        acc_sc[...] = jnp.where(is_a, pv[0], pv[1])

        live_sc[...] = tot_sc[...]
        first = _first_live_tile(i - 2, scores, live_sc)
        trips = i - 1 - first
        z_bufs, w_bufs = (z0_sc, z1_sc), (w0_sc, w1_sc)

        def put(ref, vals):
            for h in range(2):
                ref[h] = vals[h]

        def weights(zs):
            gates = [_log_gates(z) for z in zs]
            sums = [_nn(g[1].astype(BF16), tri) for g in gates]
            ws = []
            for h in range(2):
                ws.append(jnp.exp(gates[h][0] + (tot_sc[h] + sums[h])).astype(BF16))
                tot_sc[h] = tot_sc[h] + jnp.sum(gates[h][1], axis=-1, keepdims=True)
            return ws

        def add_values(w_buf, j):
            v2 = tile_of(v_ref, j)
            acc_sc[...] += jnp.where(is_a, _nn(w_buf[0], v2), _nn(w_buf[1], v2))

        def trip(j, s):
            add_values(w_bufs[s], j + 1)
            put(z_bufs[1 - s], scores(jnp.maximum(j - 1, first)))
            put(w_bufs[1 - s], weights((z_bufs[s][0], z_bufs[s][1])))

        @pl.when(trips > 0)
        def _():
            put(z0_sc, scores(i - 2))
            w0_sc[...] = jnp.zeros_like(w0_sc)

            def two_trips(pp, carry):
                j = i - 2 - 2 * pp
                trip(j, 0)
                trip(j - 1, 1)
                return carry

            lax.fori_loop(0, trips // 2, two_trips, 0)
            odd = trips % 2 == 1

            @pl.when(odd)
            def _():
                trip(first, 0)
                add_values(w1_sc, first)

            @pl.when(jnp.logical_not(odd))
            def _():
                add_values(w0_sc, first)

        o_ref[...] = acc_sc[...].astype(BF16)

    return pl.pallas_call(
        body, name="attn_fwd", grid=(npair, nq),
        in_specs=[_bs((blk, LANES), lambda p, i: (i, p)),
                  _bs((seq, LANES), lambda p, i: (0, npair + p)),
                  _bs((seq, LANES), lambda p, i: (0, 2 * npair + p))],
        out_specs=_bs((blk, LANES), lambda p, i: (i, p)),
        out_shape=_sds((seq, ATTN_W), BF16),
        scratch_shapes=[pltpu.VMEM((2, blk, blk), F32), pltpu.VMEM((2, blk, blk), F32),
                        pltpu.VMEM((2, blk, blk), BF16), pltpu.VMEM((2, blk, blk), BF16),
                        pltpu.VMEM((2, blk, 1), F32), pltpu.VMEM((2, blk, 1), F32), pltpu.VMEM((blk, LANES), F32)],
        compiler_params=_cparams(2),
    )(proj, proj, proj)


def _attn_bwd(proj, do, seq):
    blk = ATT_BLK
    nq = seq // blk
    npair = N_HEADS // 2

    def body(q_ref, k_ref, v_ref, do_ref, dq_ref, dk_ref, dv_ref,
             prod0_sc, prod1_sc, pend0_sc, pend1_sc, tot_sc, live_sc, cum_sc, pre_sc, dq_sc):
        i = pl.program_id(1)

        @pl.when(i == 0)
        def _():
            dk_ref[...] = jnp.zeros_like(dk_ref)
            dv_ref[...] = jnp.zeros_like(dv_ref)

        is_a = lax.broadcasted_iota(jnp.int32, (1, LANES), 1) < HEAD_DIM
        q2 = (q_ref[...] * Q_SCALE).astype(BF16)
        do2 = do_ref[...]
        zero = jnp.zeros_like(q2)
        qs = (jnp.where(is_a, q2, zero), jnp.where(is_a, zero, q2))
        dos = (jnp.where(is_a, do2, zero), jnp.where(is_a, zero, do2))
        row = lax.broadcasted_iota(jnp.int32, (blk, blk), 0)
        col = lax.broadcasted_iota(jnp.int32, (blk, blk), 1)
        tri_after = (row > col).astype(BF16)
        tri_excl = (row < col).astype(BF16)
        causal = col < row

        def tile_of(ref, j):
            return ref[pl.ds(pl.multiple_of(j * blk, blk), blk), :].astype(BF16)

        def scores(j):
            k2 = tile_of(k_ref, j)
            return [_nt(qs[h], k2) for h in range(2)]

        def products(j):
            v2 = tile_of(v_ref, j)
            return scores(j) + [_nt(dos[h], v2) for h in range(2)]

        def row_sum(a):
            return jnp.sum(a, axis=-1, keepdims=True)

        def grad_matmuls(ws, dzs, j):
            rows = pl.ds(pl.multiple_of(j * blk, blk), blk)
            k2 = tile_of(k_ref, j)
            dq_sc[...] += jnp.where(is_a, _nn(dzs[0], k2), _nn(dzs[1], k2))
            dk_ref[rows, :] += jnp.where(is_a, _tn(dzs[0], q2), _tn(dzs[1], q2))
            if ws is not None:
                dv_ref[rows, :] += jnp.where(is_a, _tn(ws[0], do2), _tn(ws[1], do2))

        has_left = i > 0
        left = jnp.maximum(i - 1, 0)
        p_d, p_l = products(i), products(left)
        g_d = [_log_gates(z) for z in p_d[:2]]
        g_l = [_log_gates(z) for z in p_l[:2]]
        keep_d = [jnp.where(causal, g[1], 0.0) for g in g_d]
        suf_d = [_nn(lk.astype(BF16), tri_after) for lk in keep_d]
        suf_l = [_nn(g[1].astype(BF16), tri_after) for g in g_l]
        w_d, w_l, gg_d, gg_l = [], [], [], []
        for h in range(2):
            sum_d = row_sum(keep_d[h])
            w_d.append(jnp.where(causal, jnp.exp(g_d[h][0] + suf_d[h]), 0.0))
            w_l.append(jnp.exp(g_l[h][0] + (jnp.where(has_left, sum_d, NO_TILE) + suf_l[h])))
            gg_d.append(p_d[2 + h] * w_d[h])
            gg_l.append(p_l[2 + h] * w_l[h])
            tot_sc[h] = sum_d + row_sum(g_l[h][1])
        before_d = [_nn(g.astype(BF16), tri_excl) for g in gg_d]
        before_l = [_nn(g.astype(BF16), tri_excl) for g in gg_l]
        dz_d, dz_l = [], []
        for h in range(2):
            beta_d, beta_l = jnp.exp(g_d[h][0]), jnp.exp(g_l[h][0])
            dz_l.append((gg_l[h] * (1.0 - beta_l) - before_l[h] * beta_l).astype(BF16))
            dz = gg_d[h] * (1.0 - beta_d) - (row_sum(gg_l[h]) + before_d[h]) * beta_d
            dz_d.append(jnp.where(causal, dz, 0.0).astype(BF16))
        dq_sc[...] = jnp.zeros_like(dq_sc)
        grad_matmuls([w.astype(BF16) for w in w_l], dz_l, left)
        grad_matmuls([w.astype(BF16) for w in w_d], dz_d, i)

        live_sc[...] = tot_sc[...]
        first = _first_live_tile(i - 2, scores, live_sc)
        trips = i - 1 - first
        prod_bufs, pend_bufs = (prod0_sc, prod1_sc), (pend0_sc, pend1_sc)

        def local_grads(prods):
            zs, dws = prods[:2], prods[2:]
            gates = [_log_gates(z) for z in zs]
            sums = [_nn(g[1].astype(BF16), tri_after) for g in gates]
            ws, gs = [], []
            for h in range(2):
                cum = cum_sc[h] + row_sum(gates[h][1])
                cum_sc[h] = cum
                ws.append(jnp.exp(gates[h][0] + ((live_sc[h] - cum) + sums[h])))
                gs.append(dws[h] * ws[h])
            befores = [_nn(g.astype(BF16), tri_excl) for g in gs]
            dzs = []
            for h in range(2):
                beta = jnp.exp(gates[h][0])
                dzs.append((gs[h] * (1.0 - beta) - (pre_sc[h] + befores[h]) * beta).astype(BF16))
                pre_sc[h] = pre_sc[h] + row_sum(gs[h])
            return [w.astype(BF16) for w in ws] + dzs

        def put(ref, vals):
            for n, val in enumerate(vals):
                ref[n] = val

        def flush(pend, j):
            grad_matmuls([pend[0], pend[1]], [pend[2], pend[3]], j)

        def trip(j, s):
            flush(pend_bufs[s], jnp.maximum(j - 1, first))
            put(prod_bufs[1 - s], products(j + 1))
            put(pend_bufs[1 - s], local_grads([prod_bufs[s][n] for n in range(4)]))

        def earlier_keys_share(j, mask):
            dzs = []
            for h, z in enumerate(scores(j)):
                beta = jnp.exp(_log_gates(z)[0])
                dzs.append(jnp.where(mask, -pre_sc[h] * beta, 0.0).astype(BF16))
            grad_matmuls(None, dzs, j)

        @pl.when(trips > 0)
        def _():
            cum_sc[...] = jnp.zeros_like(cum_sc)
            pre_sc[...] = jnp.zeros_like(pre_sc)
            pend0_sc[...] = jnp.zeros_like(pend0_sc)
            put(prod0_sc, products(first))

            def two_trips(pp, carry):
                trip(first + 2 * pp, 0)
                trip(first + 2 * pp + 1, 1)
                return carry

            lax.fori_loop(0, trips // 2, two_trips, 0)
            odd = trips % 2 == 1

            @pl.when(odd)
            def _():
                trip(i - 2, 0)
                flush(pend1_sc, i - 2)

            @pl.when(jnp.logical_not(odd))
            def _():
                flush(pend0_sc, i - 2)

            earlier_keys_share(i - 1, True)
            earlier_keys_share(i, causal)

        dq_ref[...] = dq_sc[...] * Q_SCALE

    qmap = lambda p, i: (i, p)
    return pl.pallas_call(
        body, name="attn_bwd", grid=(npair, nq),
        in_specs=[_bs((blk, LANES), qmap),
                  _bs((seq, LANES), lambda p, i: (0, npair + p)),
                  _bs((seq, LANES), lambda p, i: (0, 2 * npair + p)),
                  _bs((blk, LANES), qmap)],
        out_specs=[_bs((blk, LANES), qmap),
                   _bs((seq, LANES), lambda p, i: (0, p)),
                   _bs((seq, LANES), lambda p, i: (0, p))],
        out_shape=[_sds((seq, ATTN_W), F32)] * 3,
        scratch_shapes=[pltpu.VMEM((4, blk, blk), F32), pltpu.VMEM((4, blk, blk), F32),
                        pltpu.VMEM((4, blk, blk), BF16), pltpu.VMEM((4, blk, blk), BF16),
                        pltpu.VMEM((2, blk, 1), F32), pltpu.VMEM((2, blk, 1), F32), pltpu.VMEM((2, blk, 1), F32),
                        pltpu.VMEM((2, blk, 1), F32), pltpu.VMEM((blk, LANES), F32)],
        compiler_params=_cparams(2),
    )(proj, proj, proj, do)


def _elementwise(name, fn, ins, out_dtypes):
    rows, cols = ins[0].shape
    tr = rows
    for cand in (512, 256, 128, 64, 32, 16, 8):
        if rows % cand == 0 and cand * cols * 4 <= 2 * 1024 * 1024:
            tr = cand
            break
    n_in = len(ins)

    def body(*refs):
        res = fn(*[r[...] for r in refs[:n_in]])
        for r, val in zip(refs[n_in:], res):
            r[...] = val.astype(r.dtype)

    spec = _bs((tr, cols), lambda i: (i, 0))
    return pl.pallas_call(
        body, name=name, grid=(rows // tr,),
        in_specs=[spec] * n_in, out_specs=[spec] * len(out_dtypes),
        out_shape=[_sds((rows, cols), dt) for dt in out_dtypes],
        compiler_params=_cparams(1),
    )(*ins)


def _adamw_fn(w, g, m, v):
    m = ADAM_B1 * m + (1.0 - ADAM_B1) * g
    v = ADAM_B2 * v + (1.0 - ADAM_B2) * (g * g)
    m_hat = m / (1.0 - ADAM_B1 ** ADAM_STEP)
    v_hat = v / (1.0 - ADAM_B2 ** ADAM_STEP)
    delta = -ADAM_LR * (m_hat / (jnp.sqrt(v_hat) + ADAM_EPS) + ADAM_WD * w)
    return delta, m, v


def _adamw(name, w, g, m, v):
    shape = w.shape
    as2d = lambda a: a.reshape(-1, shape[-1])
    delta, nm, nv = _elementwise(name, _adamw_fn, [as2d(w), as2d(g), as2d(m), as2d(v)], [F32, F32, F32])
    return delta.reshape(shape), nm.reshape(shape), nv.reshape(shape)


def _place():
    x, y, c = lax.axis_index("x"), lax.axis_index("y"), lax.axis_index("c")
    chips = [(1 - x, y), (x, 1 - y), (1 - x, 1 - y)]
    return x, y, c, chips


ANY = pl.BlockSpec(memory_space=pl.ANY)
VMEM_WHOLE = pl.BlockSpec(memory_space=pltpu.VMEM)


def _allgather_weights(shards):
    n = len(shards)

    def body(*refs):
        src, dst = refs[:n], refs[n:2 * n]
        send_sems, recv_sems, local_sems = refs[2 * n:]
        x, y, c, chips = _place()
        me, sibling, mychip = (x, y, c), (x, y, 1 - c), 2 * x + y

        x_nbr, y_nbr, diag = 2 * (1 - x) + y, 2 * x + (1 - y), 2 * (1 - x) + (1 - y)
        to_x, to_y = (1 - x, y, c), (x, 1 - y, c)

        def parts(w):
            hr = src[w].shape[0] // 2
            first = hr // 2 if hr % 32 == 0 else hr
            return first, hr - first

        def rows_of(w, chip, half, route):
            hr = src[w].shape[0] // 2
            first, second = parts(w)
            start, size = {0: (0, hr), 1: (0, hr), 2: (0, first), 3: (first, second)}[route]
            return dst[w].at[chip, pl.ds(half * hr + start, size)]

        def copy(w, k, src_ref, dst_ref, to):
            return pltpu.make_async_remote_copy(src_ref=src_ref, dst_ref=dst_ref, send_sem=send_sems.at[w, k],
                                                recv_sem=recv_sems.at[w, k], device_id=to, device_id_type=MESH)

        def landed(w, route):
            chip = {0: x_nbr, 1: y_nbr, 2: diag, 3: diag}[route]
            return rows_of(w, chip, c, route), chip

        def routes(w):
            return (0, 1, 2, 3) if parts(w)[1] else (0, 1, 2)

        started, local = [], []
        for w in range(n):
            hr = src[w].shape[0] // 2
            own = pltpu.make_async_copy(src[w], dst[w].at[mychip], local_sems.at[w])
            own.start()
            local.append(own)
            mine = src[w].at[pl.ds(c * hr, hr)]
            for route, to in ((0, to_x), (1, to_y)):
                cp = copy(w, route, mine, rows_of(w, mychip, c, route), to)
                cp.start()
                started.append(cp)

        def pass_on(w, route):
            got, chip = landed(w, route)
            copy(w, route, got, got, me).wait_recv()
            if route == 1:
                part = rows_of(w, chip, c, 2)
                started.append(copy(w, 2, part, part, to_x))
                started[-1].start()
            if route == 0 and parts(w)[1]:
                part = rows_of(w, chip, c, 3)
                started.append(copy(w, 3, part, part, to_y))
                started[-1].start()
            started.append(copy(w, 4 + route, got, got, sibling))
            started[-1].start()

        for w in range(n):
            pass_on(w, 1)
            pass_on(w, 0)
        for w in range(n):
            for route in routes(w)[2:]:
                pass_on(w, route)
        for w in range(n):
            for route in routes(w):
                chip = landed(w, route)[1]
                from_sib = rows_of(w, chip, 1 - c, route)
                copy(w, 4 + route, from_sib, from_sib, me).wait_recv()
        for cp in local:
            cp.wait()
        for cp in started:
            cp.wait_send()

    return pl.pallas_call(
        body, name="allgather_weights",
        in_specs=[VMEM_WHOLE] * n, out_specs=[VMEM_WHOLE] * n,
        out_shape=[_sds((N_CHIPS,) + s.shape, s.dtype) for s in shards],
        scratch_shapes=[pltpu.SemaphoreType.DMA((n, 8)), pltpu.SemaphoreType.DMA((n, 8)),
                        pltpu.SemaphoreType.DMA((n,))],
        compiler_params=pltpu.CompilerParams(vmem_limit_bytes=VMEM_LIMIT),
    )(*shards)


SUM_ROWS = 64


def _rs_pair_sum(name, grads):
    n = len(grads)

    def body(*refs):
        g, out = refs[:n], refs[n:2 * n]
        stage, land, keep = refs[2 * n:3 * n], refs[3 * n:4 * n], refs[4 * n:5 * n]
        send_sems, recv_sems, stage_sems, keep_sems = refs[5 * n:]
        x, y, c, _ = _place()
        sibling = (x, y, 1 - c)
        loads = []
        for w in range(n):
            hr = g[w].shape[1] // 2
            st = pltpu.make_async_copy(g[w].at[:, pl.ds((1 - c) * hr, hr)], stage[w], stage_sems.at[w])
            kp = pltpu.make_async_copy(g[w].at[:, pl.ds(c * hr, hr)], keep[w], keep_sems.at[w])
            st.start()
            kp.start()
            loads.append((st, kp))
        gives = []
        for w in range(n):
            loads[w][0].wait()
            give = pltpu.make_async_remote_copy(src_ref=stage[w], dst_ref=land[w], send_sem=send_sems.at[w],
                                                recv_sem=recv_sems.at[w], device_id=sibling, device_id_type=MESH)
            give.start()
            gives.append(give)
        for w in range(n):
            loads[w][1].wait()
            gives[w].wait_recv()
            nb = g[w].shape[1] // 2 // SUM_ROWS

            def add(idx, carry, w=w, nb=nb):
                k, r = idx // nb, pl.multiple_of((idx % nb) * SUM_ROWS, SUM_ROWS)
                rows = pl.ds(r, SUM_ROWS)
                out[w][k, rows, :] = (keep[w][k, rows, :] + land[w][k, rows, :]).astype(BF16)
                return carry

            lax.fori_loop(0, N_CHIPS * nb, add, 0)
        for give in gives:
            give.wait_send()

    half = [(N_CHIPS, a.shape[1] // 2, a.shape[2]) for a in grads]
    bufs = [pltpu.VMEM(s, F32) for s in half]
    sems = pltpu.SemaphoreType.DMA((n,))
    return pl.pallas_call(
        body, name=name,
        in_specs=[ANY] * n, out_specs=[VMEM_WHOLE] * n, out_shape=[_sds(s, BF16) for s in half],
        scratch_shapes=bufs + bufs + bufs + [sems, sems, sems, sems],
        compiler_params=pltpu.CompilerParams(vmem_limit_bytes=VMEM_LIMIT),
    )(*grads)


def _rs_exchange_join(parts):
    n = len(parts)

    def body(*refs):
        t, full = refs[:n], refs[n:2 * n]
        got_x, got_y, pass_on, got_2 = (refs[m * n:(m + 1) * n] for m in range(2, 6))
        send_sems, recv_sems = refs[6 * n:]
        x, y, c, _ = _place()
        mychip, sibling = 2 * x + y, (x, y, 1 - c)
        x_nbr, y_nbr, diag = 2 * (1 - x) + y, 2 * x + (1 - y), 2 * (1 - x) + (1 - y)
        to_x, to_y = (1 - x, y, c), (x, 1 - y, c)
        sends = []

        def copy(w, k, src_ref, dst_ref, to):
            return pltpu.make_async_remote_copy(src_ref=src_ref, dst_ref=dst_ref, send_sem=send_sems.at[w, k],
                                                recv_sem=recv_sems.at[w, k], device_id=to, device_id_type=MESH)

        def start(cp):
            cp.start()
            sends.append(cp)

        def add_rows(w, count, fn):
            def step(idx, carry):
                fn(pl.ds(pl.multiple_of(idx * SUM_ROWS, SUM_ROWS), SUM_ROWS), pl.multiple_of(idx * SUM_ROWS, SUM_ROWS))
                return carry
            lax.fori_loop(0, count // SUM_ROWS, step, 0)

        f32 = lambda v: v.astype(F32)
        for w in range(n):
            ha = t[w].shape[1] // 2
            part_a, part_b = pl.ds(0, ha), pl.ds(ha, ha)
            start(copy(w, 0, t[w].at[x_nbr, part_a], got_x[w].at[0], to_x))
            start(copy(w, 1, t[w].at[diag, part_a], got_x[w].at[1], to_x))
            start(copy(w, 2, t[w].at[y_nbr, part_b], got_y[w].at[0], to_y))
            start(copy(w, 3, t[w].at[diag, part_b], got_y[w].at[1], to_y))
        for w in range(n):
            hr = t[w].shape[1]
            ha = hr // 2
            for k in (0, 1):
                copy(w, k, got_x[w].at[k], got_x[w].at[k], to_x).wait_recv()

            def sum_a(rows, r, w=w, hr=hr):
                full[w][pl.ds(pl.multiple_of(c * hr + r, SUM_ROWS), SUM_ROWS), :] = \
                    f32(t[w][mychip, rows, :]) + f32(got_x[w][0, rows, :])
                pass_on[w][rows, :] = (f32(t[w][y_nbr, rows, :]) + f32(got_x[w][1, rows, :])).astype(BF16)

            add_rows(w, ha, sum_a)
            start(copy(w, 4, pass_on[w].at[pl.ds(0, ha)], got_2[w].at[pl.ds(0, ha)], to_y))
            for k in (2, 3):
                copy(w, k, got_y[w].at[k - 2], got_y[w].at[k - 2], to_y).wait_recv()

            def sum_b(rows, r, w=w, hr=hr, ha=ha):
                lower = pl.ds(pl.multiple_of(ha + r, SUM_ROWS), SUM_ROWS)
                full[w][pl.ds(pl.multiple_of(c * hr + ha + r, SUM_ROWS), SUM_ROWS), :] = \
                    f32(t[w][mychip, lower, :]) + f32(got_y[w][0, rows, :])
                pass_on[w][lower, :] = (f32(t[w][x_nbr, lower, :]) + f32(got_y[w][1, rows, :])).astype(BF16)

            add_rows(w, ha, sum_b)
            start(copy(w, 5, pass_on[w].at[pl.ds(ha, ha)], got_2[w].at[pl.ds(ha, ha)], to_x))
        for w in range(n):
            hr = t[w].shape[1]
            ha = hr // 2
            copy(w, 4, got_2[w].at[pl.ds(0, ha)], got_2[w].at[pl.ds(0, ha)], to_y).wait_recv()
            copy(w, 5, got_2[w].at[pl.ds(ha, ha)], got_2[w].at[pl.ds(ha, ha)], to_x).wait_recv()

            def finish(rows, r, w=w, hr=hr):
                out_rows = pl.ds(pl.multiple_of(c * hr + r, SUM_ROWS), SUM_ROWS)
                full[w][out_rows, :] = full[w][out_rows, :] + f32(got_2[w][rows, :])

            add_rows(w, hr, finish)
            mine = full[w].at[pl.ds(c * hr, hr)]
            start(copy(w, 6, mine, mine, sibling))
        for w in range(n):
            hr = t[w].shape[1]
            theirs = full[w].at[pl.ds((1 - c) * hr, hr)]
            copy(w, 6, theirs, theirs, sibling).wait_recv()
        for cp in sends:
            cp.wait_send()

    half = lambda a: pltpu.VMEM((2, a.shape[1] // 2, a.shape[2]), a.dtype)
    whole = lambda a: pltpu.VMEM(a.shape[1:], a.dtype)
    return pl.pallas_call(
        body, name="rs_exchange_join",
        in_specs=[VMEM_WHOLE] * n, out_specs=[VMEM_WHOLE] * n,
        out_shape=[_sds((2 * a.shape[1], a.shape[2]), F32) for a in parts],
        scratch_shapes=[half(a) for a in parts] + [half(a) for a in parts] + [whole(a) for a in parts]
        + [whole(a) for a in parts] + [pltpu.SemaphoreType.DMA((n, 7)), pltpu.SemaphoreType.DMA((n, 7))],
        compiler_params=pltpu.CompilerParams(vmem_limit_bytes=VMEM_LIMIT),
    )(*parts)


def _small_allreduce(loss_p, dg_parts, dbg_a, dbg_c, dwc):
    ins = [loss_p] + list(dg_parts) + [dbg_a, dbg_c, dwc]
    n_in = len(ins)
    vmem = pl.BlockSpec(memory_space=pltpu.VMEM)

    def body(*refs):
        in_refs = refs[:n_in]
        out_ref, vec, buf, send_sems, recv_sems = refs[n_in:]
        x, y, c, _ = _place()
        me = 4 * x + 2 * y + c
        vec[...] = jnp.zeros_like(vec)
        vec[0:1, :] = jnp.sum(in_refs[0][...], axis=0)
        for r in range(5):
            vec[1 + r:2 + r, :] = jnp.sum(in_refs[1 + r][...], axis=0)
        vec[6:7, :] = jnp.sum(in_refs[6][...], axis=0)
        vec[7:8, :] = jnp.sum(in_refs[7][...], axis=0)
        vec[8:16, 0:CONV_W] = jnp.sum(in_refs[8][...], axis=0)
        buf[pl.ds(me, 1)] = vec[...][None]
        copies = []
        for r in range(1, 8):
            fx, fy, fc = (r >> 2) & 1, (r >> 1) & 1, r & 1
            to = (1 - x if fx else x, 1 - y if fy else y, 1 - c if fc else c)
            cp = pltpu.make_async_remote_copy(src_ref=vec, dst_ref=buf.at[me], send_sem=send_sems.at[r - 1],
                                              recv_sem=recv_sems.at[r - 1], device_id=to, device_id_type=MESH)
            cp.start()
            copies.append(cp)
        for cp in copies:
            cp.wait()
        total = buf[0]
        for s in range(1, 8):
            total = total + buf[s]
        out_ref[...] = total
        out_ref[0:1, :] = jnp.broadcast_to(jnp.sum(total[0:1, :], axis=-1, keepdims=True), (1, D_MODEL))

    return pl.pallas_call(
        body, name="small_allreduce",
        in_specs=[vmem] * n_in, out_specs=vmem, out_shape=_sds((SMALL_ROWS, D_MODEL), F32),
        scratch_shapes=[pltpu.VMEM((SMALL_ROWS, D_MODEL), F32), pltpu.VMEM((8, SMALL_ROWS, D_MODEL), F32),
                        pltpu.SemaphoreType.DMA((7,)), pltpu.SemaphoreType.DMA((7,))],
    )(*ins)


def _local_step(x, p, tgt, g, b_gate, w_conv, wf):
    seq = x.shape[0]
    tm = min(seq, 1024)
    th = min(seq, 512)
    tl = min(seq, 2048)
    ni, nh, nl = seq // tm, seq // th, seq // tl
    g_pre_mix, g_post_mix, g_pre_mlp, g_post_mlp, g_ple = g
    w_in, w_ao, w_co, w_o, w_up, w_down, w_pg, w_pp, w_in_nat, w_up_nat = wf
    D = D_MODEL
    vec = lambda a, blk=0: (a, _bs((1, D), lambda i, j, k: (0, blk)))
    rows_i = lambda a, t, blk=0: (a, _bs((t, D), lambda i, j, k: (i, blk)))
    rows_k = lambda a, t, blk=0: (a, _bs((t, D), lambda i, j, k: (k, blk)))
    part = lambda n: (_sds((n, 1, D), F32), _bs((None, 1, D), lambda i, j, k: (i, 0, 0)))
    full2 = lambda a: (a, _bs(a.shape, lambda i, j, k: (0, 0)))

    normed = lambda xb, gb: (_rms(xb, gb).astype(BF16),) * 2
    keep_a = lambda t: [(_sds((seq, D), BF16), _bs((t, D), lambda i, j, k: (i, 0)))]
    main_w = D_IN - 2 * D
    proj, gates, h1 = _mm("proj_in", "nn", (nh, 1, 1),
                          a_ins=[rows_i(x, th), vec(g_pre_mix)], a_fn=normed,
                          b_ins=[full2(w_in_nat)], b_fn=_ident,
                          epi_fn=lambda acc: (acc[:, :main_w], acc[:, main_w:]),
                          outs=[(_sds((seq, main_w), F32), _bs((th, main_w), lambda i, j, k: (i, 0))),
                                (_sds((seq, 2 * D), BF16), _bs((th, 2 * D), lambda i, j, k: (i, 0)))],
                          acc_shape=(th, D_IN), a_cache=((th, D), BF16), a_outs=keep_a(th))
    o = _attn_fwd(proj, seq)
    (y_attn,) = _mm("attn_out", "nn", (ni, 1, 1),
                    a_ins=[(o, _bs((tm, ATTN_W), lambda i, j, k: (i, 0)))], a_fn=_ident,
                    b_ins=[full2(w_ao)], b_fn=_ident,
                    outs=[(_sds((seq, D), BF16), _bs((tm, D), lambda i, j, k: (i, 0)))], acc_shape=(tm, D))
    e, d = _conv_fwd(proj, w_conv, seq, tm)
    (y_conv,) = _mm("conv_out", "nn", (ni, 1, 1),
                    a_ins=[(e, _bs((tm, CONV_W), lambda i, j, k: (i, 0)))], a_fn=_ident,
                    b_ins=[full2(w_co)], b_fn=_ident,
                    outs=[(_sds((seq, D), BF16), _bs((tm, D), lambda i, j, k: (i, 0)))], acc_shape=(tm, D))

    def gate_values(ga, gc, ba, bc):
        return _sig(ga.astype(F32) + ba), _sig(gc.astype(F32) + bc)

    def mix_fn(ga, gc, ya, yc, ba, bc):
        sa, sc = gate_values(ga, gc, ba, bc)
        return ((sa * ya.astype(F32) + sc * yc.astype(F32)).astype(BF16),) * 2

    def post_mix(acc, xb, gb):
        return acc, xb + _rms(acc, gb)

    mix_ins = lambda rows: [rows(gates, th, 0), rows(gates, th, 1), rows(y_attn, th), rows(y_conv, th),
                            vec(b_gate, 0), vec(b_gate, 1)]
    mixed, x1, mixin = _mm("mix_out", "nn", (nh, 1, 1),
                           a_ins=mix_ins(rows_i), a_fn=mix_fn, b_ins=[full2(w_o)], b_fn=_ident,
                           epi_ins=[rows_i(x, th), vec(g_post_mix)], epi_fn=post_mix,
                           outs=[(_sds((seq, D), F32), _bs((th, D), lambda i, j, k: (i, 0)))] * 2,
                           acc_shape=(th, D), a_cache=((th, D), BF16), a_outs=keep_a(th))
    up, h2 = _mm("mlp_up", "nn", (nh, 1, 1),
                 a_ins=[rows_i(x1, th), vec(g_pre_mlp)], a_fn=normed,
                 b_ins=[full2(w_up_nat)], b_fn=_ident,
                 outs=[(_sds((seq, D_FF), BF16), _bs((th, D_FF), lambda i, j, k: (i, 0)))],
                 acc_shape=(th, D_FF), a_cache=((th, D), BF16), a_outs=keep_a(th))

    def relu2(ub):
        r = jnp.maximum(ub.astype(F32), 0.0)
        return (r * r).astype(BF16)

    f, x2 = _mm("mlp_down", "nn", (nh, 1, 1),
                a_ins=[(up, _bs((th, D_FF), lambda i, j, k: (i, 0)))], a_fn=relu2,
                b_ins=[full2(w_down)], b_fn=_ident,
                epi_ins=[rows_i(x1, th), vec(g_post_mlp)], epi_fn=post_mix,
                outs=[(_sds((seq, D), F32), _bs((th, D), lambda i, j, k: (i, 0)))] * 2, acc_shape=(th, D))
    dx2, df, dpre, h3, dpp, loss_p, dg_ple_p, dg_post_mlp_p = _ple_head(
        x2, p, tgt, f, g_ple, g_post_mlp, w_pg, w_pp, seq, th)

    (dw_pp,) = _mm("dw_ple_proj", "tn", (1, 1, nh),
                   a_ins=[(p, _bs((th, PLE_DIM), lambda i, j, k: (k, 0)))], a_fn=_to_bf16,
                   b_ins=[rows_k(dpp, th)], b_fn=_ident,
                   outs=[(_sds((PLE_DIM, D), F32), _bs((PLE_DIM, D), lambda i, j, k: (0, 0)))],
                   acc_shape=(PLE_DIM, D))
    (dw_pg,) = _mm("dw_ple_gate", "tn", (1, 1, ni),
                   a_ins=[rows_k(h3, tm)], a_fn=_ident, b_ins=[rows_k(dpre, tm)], b_fn=_ident,
                   outs=[(_sds((D, D), F32), _bs((D, D), lambda i, j, k: (0, 0)))], acc_shape=(D, D))

    def dup_fn(acc, ub):
        return (acc * (2.0 * jnp.maximum(ub.astype(F32), 0.0)),)

    (dup,) = _mm("d_mlp_down", "nt", (nh, 1, 1),
                 a_ins=[rows_i(df, th)], a_fn=_ident, b_ins=[full2(w_down)], b_fn=_ident,
                 epi_ins=[(up, _bs((th, D_FF), lambda i, j, k: (i, 0)))], epi_fn=dup_fn,
                 outs=[(_sds((seq, D_FF), BF16), _bs((th, D_FF), lambda i, j, k: (i, 0)))],
                 acc_shape=(th, D_FF))
    (dw_down,) = _mm("dw_mlp_down", "tn", (4, 1, nl),
                     a_ins=[(up, _bs((tl, D), lambda i, j, k: (k, i)))], a_fn=relu2,
                     b_ins=[rows_k(df, tl)], b_fn=_ident,
                     outs=[(_sds((D_FF, D), F32), _bs((D, D), lambda i, j, k: (i, 0)))], acc_shape=(D, D))
    (dw_up,) = _mm("dw_mlp_up", "tn", (1, 4, nl),
                   a_ins=[rows_k(h2, tl)], a_fn=_ident,
                   b_ins=[(dup, _bs((tl, D), lambda i, j, k: (k, j)))], b_fn=_ident,
                   outs=[(_sds((N_CHIPS, D, D), F32), _bs((None, D, D), lambda i, j, k: (j, 0, 0)))],
                   acc_shape=(D, D))

    def mlp_norm_bwd(acc, x1b, dx2b, mixedb, g_mlp, g_mix):
        dxn, dg_mlp = _rms_bwd(x1b, g_mlp, acc)
        dx1b = dx2b + dxn
        dmixedb, dg_mix = _rms_bwd(mixedb, g_mix, dx1b)
        return dx1b, dmixedb, dg_mlp, dg_mix

    dx1, dmixed, dg_pre_mlp_p, dg_post_mix_p = _mm(
        "d_mlp_up", "nt", (nh, 1, 1),
        a_ins=[(dup, _bs((th, D_FF), lambda i, j, k: (i, 0)))], a_fn=_ident,
        b_ins=[full2(w_up_nat)], b_fn=_ident,
        epi_ins=[rows_i(x1, th), rows_i(dx2, th), rows_i(mixed, th), vec(g_pre_mlp), vec(g_post_mix)],
        epi_fn=mlp_norm_bwd,
        outs=[(_sds((seq, D), F32), _bs((th, D), lambda i, j, k: (i, 0))),
              (_sds((seq, D), BF16), _bs((th, D), lambda i, j, k: (i, 0))), part(nh), part(nh)],
        acc_shape=(th, D))
    (dw_o,) = _mm("dw_mix_out", "tn", (1, 1, ni),
                  a_ins=[rows_k(mixin, tm)], a_fn=_ident, b_ins=[rows_k(dmixed, tm)], b_fn=_ident,
                  outs=[(_sds((D, D), F32), _bs((D, D), lambda i, j, k: (0, 0)))], acc_shape=(D, D))

    def gate_bwd(acc, ga, gc, ya, yc, ba, bc):
        sa, sc = gate_values(ga, gc, ba, bc)
        dga = acc * ya.astype(F32) * sa * (1.0 - sa)
        dgc = acc * yc.astype(F32) * sc * (1.0 - sc)
        return (acc * sa, acc * sc, jnp.concatenate([dga, dgc], axis=1),
                jnp.sum(dga, axis=0, keepdims=True), jnp.sum(dgc, axis=0, keepdims=True))

    dya, dyc, dgate, dbg_a_p, dbg_c_p = _mm(
        "d_mix_out", "nt", (nh, 1, 1),
        a_ins=[rows_i(dmixed, th)], a_fn=_ident, b_ins=[full2(w_o)], b_fn=_ident,
        epi_ins=mix_ins(rows_i), epi_fn=gate_bwd,
        outs=[(_sds((seq, D), BF16), _bs((th, D), lambda i, j, k: (i, 0)))] * 2
             + [(_sds((seq, 2 * D), BF16), _bs((th, 2 * D), lambda i, j, k: (i, 0))), part(nh), part(nh)],
        acc_shape=(th, D))
    (dw_ao,) = _mm("dw_attn_out", "tn", (1, 1, nh),
                   a_ins=[(o, _bs((th, ATTN_W), lambda i, j, k: (k, 0)))], a_fn=_ident,
                   b_ins=[rows_k(dya, th)], b_fn=_ident,
                   outs=[(_sds((ATTN_W, D), F32), _bs((ATTN_W, D), lambda i, j, k: (0, 0)))], acc_shape=(ATTN_W, D))
    (do,) = _mm("d_attn_out", "nt", (ni, 1, 1),
                a_ins=[rows_i(dya, tm)], a_fn=_ident, b_ins=[full2(w_ao)], b_fn=_ident,
                outs=[(_sds((seq, ATTN_W), BF16), _bs((tm, ATTN_W), lambda i, j, k: (i, 0)))],
                acc_shape=(tm, ATTN_W))
    dq, dk, dv = _attn_bwd(proj, do, seq)
    (dw_co,) = _mm("dw_conv_out", "tn", (1, 1, nh),
                   a_ins=[(e, _bs((th, CONV_W), lambda i, j, k: (k, 0)))], a_fn=_ident,
                   b_ins=[rows_k(dyc, th)], b_fn=_ident,
                   outs=[(_sds((CONV_W, D), F32), _bs((CONV_W, D), lambda i, j, k: (0, 0)))], acc_shape=(CONV_W, D))
    (de,) = _mm("d_conv_out", "nt", (ni, 1, 1),
                a_ins=[rows_i(dyc, tm)], a_fn=_ident, b_ins=[full2(w_co)], b_fn=_ident,
                outs=[(_sds((seq, CONV_W), F32), _bs((tm, CONV_W), lambda i, j, k: (i, 0)))],
                acc_shape=(tm, CONV_W))
    dconv, dwc_p = _conv_bwd(proj, de, d, w_conv, seq, tm)
    qkv_w = 3 * ATTN_W
    join_bf16 = lambda *blocks: jnp.concatenate([b.astype(BF16) for b in blocks], axis=1)
    piece = lambda a, t, rows, blk=0: (a, _bs((t, a.shape[1]), (lambda i, j, k: (k, blk)) if rows == "k"
                                             else (lambda i, j, k: (i, blk))))
    (dw_in_qkv,) = _mm("dw_proj_in_qkv", "tn", (1, 1, ni),
                       a_ins=[rows_k(h1, tm)], a_fn=_ident,
                       b_ins=[piece(dq, tm, "k"), piece(dk, tm, "k"), piece(dv, tm, "k")], b_fn=join_bf16,
                       outs=[(_sds((D, qkv_w), F32), _bs((D, qkv_w), lambda i, j, k: (0, 0)))], acc_shape=(D, qkv_w))
    (dw_in_conv,) = _mm("dw_proj_in_conv", "tn", (1, 1, nl),
                        a_ins=[rows_k(h1, tl)], a_fn=_ident, b_ins=[piece(dconv, tl, "k")], b_fn=_ident,
                        outs=[(_sds((D, 3 * CONV_W), F32), _bs((D, 3 * CONV_W), lambda i, j, k: (0, 0)))],
                        acc_shape=(D, 3 * CONV_W))
    (dw_in_gate,) = _mm("dw_proj_in_gate", "tn", (1, 2, nl),
                        a_ins=[rows_k(h1, tl)], a_fn=_ident,
                        b_ins=[(dgate, _bs((tl, D), lambda i, j, k: (k, j)))], b_fn=_ident,
                        outs=[(_sds((D, 2 * D), F32), _bs((D, D), lambda i, j, k: (0, j)))], acc_shape=(D, D))
    dw_in = jnp.concatenate([dw_in_qkv, dw_in_conv, dw_in_gate], axis=1)

    def in_norm_bwd(acc, xb, dx1b, gb):
        dxn, dg = _rms_bwd(xb, gb, acc)
        return dx1b + dxn, dg

    grad_x, dg_pre_mix_p = _mm("d_proj_in", "nt", (nh, 1, 1),
                               a_ins=[piece(dq, th, "i"), piece(dk, th, "i"), piece(dv, th, "i"),
                                      piece(dconv, th, "i"), piece(dgate, th, "i")], a_fn=join_bf16,
                               b_ins=[full2(w_in_nat)], b_fn=_ident,
                               epi_ins=[rows_i(x, th), rows_i(dx1, th), vec(g_pre_mix)], epi_fn=in_norm_bwd,
                               outs=[(_sds((seq, D), F32), _bs((th, D), lambda i, j, k: (i, 0))), part(nh)],
                               acc_shape=(th, D))

    chip_major = lambda a: a.reshape(a.shape[0], N_CHIPS, a.shape[1] // N_CHIPS).transpose(1, 0, 2)
    big = [chip_major(dw_in), chip_major(dw_ao), chip_major(dw_co), dw_o.reshape(N_CHIPS, D // N_CHIPS, D), dw_up,
           dw_down.reshape(N_CHIPS, D_FF // N_CHIPS, D), dw_pg.reshape(N_CHIPS, D // N_CHIPS, D), chip_major(dw_pp)]
    small = (loss_p, [dg_pre_mix_p, dg_post_mix_p, dg_pre_mlp_p, dg_post_mlp_p, dg_ple_p], dbg_a_p, dbg_c_p, dwc_p)
    return grad_x, big, small


RS_GROUPS = ((0,), (4,), (5,), (1, 2, 3, 6, 7))


def _reduce_scatter(big):
    pair = [None] * len(big)
    for gi, group in enumerate(RS_GROUPS):
        for w, s in zip(group, _rs_pair_sum(f"rs_pair_sum_{gi}", [big[w] for w in group])):
            pair[w] = s
    return _rs_exchange_join(pair)


def kernel(x, p, g_pre_mix, w_in, b_gate, w_conv, w_attn_out, w_conv_out, w_o, g_post_mix, g_pre_mlp, w_up, w_down, g_post_mlp, g_ple, w_ple_gate, w_ple_proj, loss_target, m_g_pre_mix, m_w_in, m_b_gate, m_w_conv, m_w_attn_out, m_w_conv_out, m_w_o, m_g_post_mix, m_g_pre_mlp, m_w_up, m_w_down, m_g_post_mlp, m_g_ple, m_w_ple_gate, m_w_ple_proj, v_g_pre_mix, v_w_in, v_b_gate, v_w_conv, v_w_attn_out, v_w_conv_out, v_w_o, v_g_post_mix, v_g_pre_mlp, v_w_up, v_w_down, v_g_post_mlp, v_g_ple, v_w_ple_gate, v_w_ple_proj):
    mats = [w_in, w_attn_out, w_conv_out, w_o, w_up, w_down, w_ple_gate, w_ple_proj]
    mats_m = [m_w_in, m_w_attn_out, m_w_conv_out, m_w_o, m_w_up, m_w_down, m_w_ple_gate, m_w_ple_proj]
    mats_v = [v_w_in, v_w_attn_out, v_w_conv_out, v_w_o, v_w_up, v_w_down, v_w_ple_gate, v_w_ple_proj]
    gains = [g_pre_mix, g_post_mix, g_pre_mlp, g_post_mlp, g_ple]
    gains_m = [m_g_pre_mix, m_g_post_mix, m_g_pre_mlp, m_g_post_mlp, m_g_ple]
    gains_v = [v_g_pre_mix, v_g_post_mix, v_g_pre_mlp, v_g_post_mlp, v_g_ple]

    taps = jnp.concatenate([w_conv[0], jnp.zeros((CONV_PAD_ROWS - 3, LANES), F32)], axis=0)
    gathered = _allgather_weights([w[0].astype(BF16) for w in mats] + [taps])
    cols_joined = lambda a: a.transpose(1, 0, 2).reshape(a.shape[1], N_CHIPS * a.shape[2])
    rows_joined = lambda a: a.reshape(N_CHIPS * a.shape[1], a.shape[2])
    wf = [gathered[0], cols_joined(gathered[1]), cols_joined(gathered[2]), rows_joined(gathered[3]), gathered[4],
          rows_joined(gathered[5]), rows_joined(gathered[6]), cols_joined(gathered[7]),
          cols_joined(gathered[0]), cols_joined(gathered[4])]
    w_conv_full = cols_joined(gathered[8])[0:3, :]
    chip = 2 * lax.axis_index("x") + lax.axis_index("y")

    grad_x, big, small = _local_step(x[0], p[0, 0], loss_target[0], gains, b_gate, w_conv_full, wf)

    shard_grads = _reduce_scatter(big)
    red = _small_allreduce(*small)
    loss = red[0, 0]
    grad_gains = [red[1 + r:2 + r, :] for r in range(5)]
    grad_b_gate = jnp.concatenate([red[6:7, :], red[7:8, :]], axis=1)
    grad_w_conv = lax.dynamic_slice(red[8:11, :], (0, chip * LANES), (3, LANES))[None]

    grads_big = [gr.reshape(w.shape) for gr, w in zip(shard_grads, mats)]
    upd_big = [_adamw(f"adamw_{i}", w, gr, m, v) for i, (w, gr, m, v) in enumerate(zip(mats, grads_big, mats_m, mats_v))]
    pack = lambda vs, bg: jnp.concatenate(list(vs) + [bg.reshape(2, D_MODEL), jnp.zeros((1, D_MODEL), F32)], axis=0)
    upd_small = _adamw("adamw_small", pack(gains, b_gate), pack(grad_gains, grad_b_gate),
                       pack(gains_m, m_b_gate), pack(gains_v, v_b_gate))
    upd_conv = _adamw("adamw_conv", w_conv, grad_w_conv, m_w_conv, v_w_conv)

    def small_out(a, which):
        gains_out = [a[r:r + 1, :] for r in range(5)]
        return gains_out, a[5:7, :].reshape(1, 2 * D_MODEL)

    def ordered(g_pre_mix_, big_, b_gate_, conv_, g_rest):
        return [g_pre_mix_, big_[0], b_gate_, conv_, big_[1], big_[2], big_[3], g_rest[0], g_rest[1], big_[4], big_[5],
                g_rest[2], g_rest[3], big_[6], big_[7]]

    outs = [loss, grad_x[None]]
    outs += ordered(grad_gains[0], grads_big, grad_b_gate, grad_w_conv, grad_gains[1:])
    for which in range(3):
        g_out, b_out = small_out(upd_small[which], which)
        outs += ordered(g_out[0], [u[which] for u in upd_big], b_out, upd_conv[which], g_out[1:])
    return tuple(outs)
```

```python
import functools

import jax
import jax.numpy as jnp
from jax import lax
from jax.experimental import pallas as pl
from jax.experimental.pallas import tpu as pltpu

F32 = jnp.float32
BF16 = jnp.bfloat16
MESH = pl.DeviceIdType.MESH

D_MODEL = 1024
N_HEADS = 8
HEAD_DIM = 64
ATTN_W = N_HEADS * HEAD_DIM
CONV_W = 512
D_FF = 4096
PLE_DIM = 256
D_IN = 5120
N_CHIPS = 4
EPS = 1e-6
Q_SCALE = HEAD_DIM ** -0.5

ADAM_LR = 0.001
ADAM_B1 = 0.9
ADAM_B2 = 0.999
ADAM_EPS = 1e-08
ADAM_WD = 0.01
ADAM_STEP = 10

V7X_VMEM_BYTES = 64 * 1024 * 1024
VMEM_LIMIT = V7X_VMEM_BYTES - 8 * 1024 * 1024
LANES = 128
ATT_BLK = 256
SMALL_ROWS = 16
CONV_PAD_ROWS = 16


def _cparams(n_grid):
    return pltpu.CompilerParams(dimension_semantics=("arbitrary",) * n_grid, vmem_limit_bytes=VMEM_LIMIT)


def _bs(shape, fn):
    return pl.BlockSpec(shape, fn)


def _rms_stats(xf):
    return lax.rsqrt(jnp.mean(xf * xf, axis=-1, keepdims=True) + EPS)


def _rms(xf, g):
    return xf * _rms_stats(xf) * g


def _rms_bwd(xf, g, dy):
    r = _rms_stats(xf)
    xh = xf * r
    dyg = dy * g
    dx = r * (dyg - xh * jnp.mean(dyg * xh, axis=-1, keepdims=True))
    return dx, jnp.sum(dy * xh, axis=0, keepdims=True)


def _sig(z):
    return 1.0 / (1.0 + jnp.exp(-z))


def _ident(a):
    return a


def _to_bf16(a):
    return a.astype(BF16)


_DIMS = {"nn": (((1,), (0,)), ((), ())), "nt": (((1,), (1,)), ((), ())), "tn": (((0,), (0,)), ((), ()))}


def _mm(name, mode, grid, a_ins, a_fn, b_ins, b_fn, outs, acc_shape, epi_ins=(), epi_fn=None,
        a_cache=None, a_outs=(), epi_a=()):
    nk = grid[2]
    na, nb, ne, no, nao = len(a_ins), len(b_ins), len(epi_ins), len(outs), len(a_outs)
    assert a_cache is None or nk == 1
    assert not a_outs or a_cache is not None
    dims = _DIMS[mode]
    if epi_fn is None:
        epi_fn = lambda acc: (acc,)

    def body(*refs):
        a_refs = refs[:na]
        b_refs = refs[na:na + nb]
        e_refs = refs[na + nb:na + nb + ne]
        o_refs = refs[na + nb + ne:na + nb + ne + no]
        ao_refs = refs[na + nb + ne + no:na + nb + ne + no + nao]
        scratch = list(refs[na + nb + ne + no + nao:])
        acc_ref = scratch.pop(0) if nk > 1 else None
        a_sc = scratch.pop(0) if a_cache is not None else None
        j = pl.program_id(1)
        k = pl.program_id(2)

        def finish(acc):
            res = epi_fn(acc, *[a_refs[t][...] for t in epi_a], *[r[...] for r in e_refs])
            for r, val in zip(o_refs, res):
                r[...] = val.astype(r.dtype)

        if a_sc is not None:
            @pl.when(j == 0)
            def _():
                res = a_fn(*[r[...] for r in a_refs])
                if nao:
                    for r, val in zip(ao_refs, res[1:]):
                        r[...] = val.astype(r.dtype)
                    res = res[0]
                a_sc[...] = res
            a = a_sc[...]
        else:
            a = a_fn(*[r[...] for r in a_refs])
        b = b_fn(*[r[...] for r in b_refs])
        prod = lax.dot_general(a, b, dims, preferred_element_type=F32)
        if nk == 1:
            finish(prod)
        else:
            @pl.when(k == 0)
            def _():
                acc_ref[...] = prod

            @pl.when(k > 0)
            def _():
                acc_ref[...] += prod

            @pl.when(k == nk - 1)
            def _():
                finish(acc_ref[...])

    scratch_shapes = []
    if nk > 1:
        scratch_shapes.append(pltpu.VMEM(acc_shape, F32))
    if a_cache is not None:
        scratch_shapes.append(pltpu.VMEM(*a_cache))
    all_outs = list(outs) + list(a_outs)
    res = pl.pallas_call(
        body, name=name, grid=grid,
        in_specs=[s for _, s in a_ins] + [s for _, s in b_ins] + [s for _, s in epi_ins],
        out_specs=[s for _, s in all_outs],
        out_shape=[o for o, _ in all_outs],
        scratch_shapes=scratch_shapes,
        compiler_params=_cparams(3),
    )(*[a for a, _ in a_ins], *[a for a, _ in b_ins], *[a for a, _ in epi_ins])
    return res


def _sds(shape, dtype):
    return jax.ShapeDtypeStruct(shape, dtype)


def _nt(a, b):
    return lax.dot_general(a, b, _DIMS["nt"], preferred_element_type=F32)


def _tn(a, b):
    return lax.dot_general(a, b, _DIMS["tn"], preferred_element_type=F32)


def _nn(a, b):
    return lax.dot_general(a, b, _DIMS["nn"], preferred_element_type=F32)


def _mlp_down_ple_head(up, x1, p, tgt, g_ple, g_post_mlp, w_down, w_pg, w_pp, seq, tr):
    nblk = seq // tr
    D = D_MODEL

    def body(up_ref, x1_ref, p_ref, t_ref, gp_ref, gm_ref, wd_ref, wpg_ref, wpp_ref,
             dx2_ref, df_ref, dpre_ref, h3_ref, dpp_ref, loss_ref, dgp_ref, dgm_ref):
        gp, gm, wpg = gp_ref[...], gm_ref[...], wpg_ref[...]
        hidden = jnp.maximum(up_ref[...].astype(F32), 0.0)
        fb = _nn((hidden * hidden).astype(BF16), wd_ref[...])
        x2b = x1_ref[...] + _rms(fb, gm)
        h3 = _rms(x2b, gp).astype(BF16)
        h3_ref[...] = h3
        gate = _sig(_nn(h3, wpg))
        pp = _nn(p_ref[...].astype(BF16), wpp_ref[...])
        err = x2b + gate * pp - t_ref[...]
        loss_ref[...] = jnp.sum(err * err, axis=0, keepdims=True) * (0.5 / D)
        dx3 = err * (1.0 / D)
        dpp_ref[...] = (dx3 * gate).astype(BF16)
        dpre = (dx3 * pp * gate * (1.0 - gate)).astype(BF16)
        dpre_ref[...] = dpre
        dxn, dgp = _rms_bwd(x2b, gp, _nt(dpre, wpg))
        dx2 = dx3 + dxn
        dx2_ref[...] = dx2
        dgp_ref[...] = dgp
        dfb, dgm = _rms_bwd(fb, gm, dx2)
        df_ref[...] = dfb.astype(BF16)
        dgm_ref[...] = dgm

    rows = _bs((tr, D), lambda i: (i, 0))
    vec = _bs((1, D), lambda i: (0, 0))
    part = _bs((None, 1, D), lambda i: (i, 0, 0))
    return pl.pallas_call(
        body, name="mlp_down_ple_head", grid=(nblk,),
        in_specs=[_bs((tr, D_FF), lambda i: (i, 0)), rows, _bs((tr, PLE_DIM), lambda i: (i, 0)), rows, vec, vec,
                  _bs((D_FF, D), lambda i: (0, 0)), _bs((D, D), lambda i: (0, 0)), _bs((PLE_DIM, D), lambda i: (0, 0))],
        out_specs=[rows] * 5 + [part] * 3,
        out_shape=[_sds((seq, D), F32)] + [_sds((seq, D), BF16)] * 4 + [_sds((nblk, 1, D), F32)] * 3,
        compiler_params=_cparams(1),
    )(up, x1, p, tgt, g_ple, g_post_mlp, w_down, w_pg, w_pp)


def _shift_rows_down(u, prev, n):
    rows = u.shape[0]
    ridx = lax.broadcasted_iota(jnp.int32, u.shape, 0)
    out = pltpu.roll(u, n, 0)
    for r in range(n):
        out = jnp.where(ridx == r, prev[8 - n + r:8 - n + r + 1, :], out)
    del rows
    return out


def _shift_rows_up(u, nxt, n):
    rows = u.shape[0]
    ridx = lax.broadcasted_iota(jnp.int32, u.shape, 0)
    out = pltpu.roll(u, rows - n, 0)
    for r in range(n):
        out = jnp.where(ridx == rows - n + r, nxt[r:r + 1, :], out)
    return out


CONV_COL0 = 3


def _conv_fwd(proj, w_conv, seq, tr):
    hb = tr // 8

    def body(cb_ref, cc_ref, cu_ref, ccp_ref, cup_ref, w_ref, e_ref, d_ref):
        i = pl.program_id(0)
        u = cc_ref[...] * cu_ref[...]
        up = jnp.where(i > 0, ccp_ref[...] * cup_ref[...], 0.0)
        w = w_ref[...]
        d = w[0:1, :] * _shift_rows_down(u, up, 2) + w[1:2, :] * _shift_rows_down(u, up, 1) + w[2:3, :] * u
        d_ref[...] = d
        e_ref[...] = (cb_ref[...] * d).astype(BF16)

    prev = lambda c: (lambda i: (jnp.maximum(i * hb - 1, 0), c))
    return pl.pallas_call(
        body, name="conv_fwd", grid=(seq // tr,),
        in_specs=[_bs((tr, CONV_W), lambda i: (i, CONV_COL0)),
                  _bs((tr, CONV_W), lambda i: (i, CONV_COL0 + 1)),
                  _bs((tr, CONV_W), lambda i: (i, CONV_COL0 + 2)),
                  _bs((8, CONV_W), prev(CONV_COL0 + 1)),
                  _bs((8, CONV_W), prev(CONV_COL0 + 2)),
                  _bs((3, CONV_W), lambda i: (0, 0))],
        out_specs=[_bs((tr, CONV_W), lambda i: (i, 0)), _bs((tr, CONV_W), lambda i: (i, 0))],
        out_shape=[_sds((seq, CONV_W), BF16), _sds((seq, CONV_W), F32)],
        compiler_params=_cparams(1),
    )(proj, proj, proj, proj, proj, w_conv)


def _conv_bwd(proj, de, d, w_conv, seq, tr):
    hb = tr // 8
    nblk = seq // tr

    def body(cb_ref, cc_ref, cu_ref, ccp_ref, cup_ref, cbn_ref, de_ref, den_ref, d_ref, w_ref, o_ref, dw_ref):
        i = pl.program_id(0)
        cc, cu, cb = cc_ref[...], cu_ref[...], cb_ref[...]
        u = cc * cu
        up = jnp.where(i > 0, ccp_ref[...] * cup_ref[...], 0.0)
        u1 = _shift_rows_down(u, up, 1)
        u2 = _shift_rows_down(u, up, 2)
        de_ = de_ref[...]
        dd = de_ * cb
        ddn = jnp.where(i < nblk - 1, den_ref[...] * cbn_ref[...], 0.0)
        w = w_ref[...]
        du = w[2:3, :] * dd + w[1:2, :] * _shift_rows_up(dd, ddn, 1) + w[0:1, :] * _shift_rows_up(dd, ddn, 2)
        o_ref[:, 0:CONV_W] = (de_ * d_ref[...]).astype(BF16)
        o_ref[:, CONV_W:2 * CONV_W] = (du * cu).astype(BF16)
        o_ref[:, 2 * CONV_W:3 * CONV_W] = (du * cc).astype(BF16)
        ridx = lax.broadcasted_iota(jnp.int32, (8, CONV_W), 0)
        dw0 = jnp.sum(dd * u2, axis=0, keepdims=True)
        dw1 = jnp.sum(dd * u1, axis=0, keepdims=True)
        dw2 = jnp.sum(dd * u, axis=0, keepdims=True)
        dw_ref[...] = jnp.where(ridx == 0, dw0, jnp.where(ridx == 1, dw1, jnp.where(ridx == 2, dw2, 0.0)))

    prev = lambda c: (lambda i: (jnp.maximum(i * hb - 1, 0), c))
    nxt = lambda c: (lambda i: (jnp.minimum((i + 1) * hb, seq // 8 - 1), c))
    return pl.pallas_call(
        body, name="conv_bwd", grid=(nblk,),
        in_specs=[_bs((tr, CONV_W), lambda i: (i, CONV_COL0)),
                  _bs((tr, CONV_W), lambda i: (i, CONV_COL0 + 1)),
                  _bs((tr, CONV_W), lambda i: (i, CONV_COL0 + 2)),
                  _bs((8, CONV_W), prev(CONV_COL0 + 1)),
                  _bs((8, CONV_W), prev(CONV_COL0 + 2)),
                  _bs((8, CONV_W), nxt(CONV_COL0)),
                  _bs((tr, CONV_W), lambda i: (i, 0)),
                  _bs((8, CONV_W), nxt(0)),
                  _bs((tr, CONV_W), lambda i: (i, 0)),
                  _bs((3, CONV_W), lambda i: (0, 0))],
        out_specs=[_bs((tr, 3 * CONV_W), lambda i: (i, 0)), _bs((None, 8, CONV_W), lambda i: (i, 0, 0))],
        out_shape=[_sds((seq, 3 * CONV_W), BF16), _sds((nblk, 8, CONV_W), F32)],
        compiler_params=_cparams(1),
    )(proj, proj, proj, proj, proj, proj, de, de, d, w_conv)


def _log_gates(z):
    lse = jnp.log(1.0 + jnp.exp(-jnp.abs(z)))
    log_beta = jnp.minimum(z, 0.0) - lse
    return log_beta, log_beta - z


DEAD_LOG_WEIGHT = -110.0
NO_TILE = -1e30


def _first_live_tile(start, scores, live_sc):
    def alive():
        return jnp.max(jnp.maximum(live_sc[0], live_sc[1])) > DEAD_LOG_WEIGHT

    def step(c):
        for h, z in enumerate(scores(c[0])):
            live_sc[h] = live_sc[h] + jnp.sum(_log_gates(z)[1], axis=-1, keepdims=True)
        return c[0] - 1, alive()

    j_end, _ = lax.while_loop(lambda c: jnp.logical_and(c[0] >= 0, c[1]), step, (start, alive()))
    return j_end + 1


def _attn_fwd(proj, seq):
    blk = ATT_BLK
    nq = seq // blk
    npair = N_HEADS // 2

    def body(q_ref, k_ref, v_ref, o_ref, z0_sc, z1_sc, w0_sc, w1_sc, tot_sc, live_sc, acc_sc):
        i = pl.program_id(1)
        is_a = lax.broadcasted_iota(jnp.int32, (1, LANES), 1) < HEAD_DIM
        q2 = (q_ref[...] * Q_SCALE).astype(BF16)
        zero = jnp.zeros_like(q2)
        qs = (jnp.where(is_a, q2, zero), jnp.where(is_a, zero, q2))
        row = lax.broadcasted_iota(jnp.int32, (blk, blk), 0)
        col = lax.broadcasted_iota(jnp.int32, (blk, blk), 1)
        tri = (row > col).astype(BF16)
        causal = col < row

        def tile_of(ref, j):
            return ref[pl.ds(pl.multiple_of(j * blk, blk), blk), :].astype(BF16)

        def scores(j):
            k2 = tile_of(k_ref, j)
            return [_nt(qs[h], k2) for h in range(2)]

        has_left = i > 0
        left = jnp.maximum(i - 1, 0)
        g_d = [_log_gates(z) for z in scores(i)]
        g_l = [_log_gates(z) for z in scores(left)]
        keep_d = [jnp.where(causal, g[1], 0.0) for g in g_d]
        suf_d = [_nn(lk.astype(BF16), tri) for lk in keep_d]
        suf_l = [_nn(g[1].astype(BF16), tri) for g in g_l]
        v_d, v_l = tile_of(v_ref, i), tile_of(v_ref, left)
        pv = []
        for h in range(2):
            sum_d = jnp.sum(keep_d[h], axis=-1, keepdims=True)
            w_d = jnp.where(causal, jnp.exp(g_d[h][0] + suf_d[h]), 0.0)
            w_l = jnp.exp(g_l[h][0] + (jnp.where(has_left, sum_d, NO_TILE) + suf_l[h]))
            pv.append(_nn(w_d.astype(BF16), v_d) + _nn(w_l.astype(BF16), v_l))
            tot_sc[h] = sum_d + jnp.sum(g_l[h][1], axis=-1, keepdims=True)
        acc_sc[...] = jnp.where(is_a, pv[0], pv[1])

        live_sc[...] = tot_sc[...]
        first = _first_live_tile(i - 2, scores, live_sc)
        trips = i - 1 - first
        z_bufs, w_bufs = (z0_sc, z1_sc), (w0_sc, w1_sc)

        def put(ref, vals):
            for h in range(2):
                ref[h] = vals[h]

        def weights(zs):
            gates = [_log_gates(z) for z in zs]
            sums = [_nn(g[1].astype(BF16), tri) for g in gates]
            ws = []
            for h in range(2):
                ws.append(jnp.exp(gates[h][0] + (tot_sc[h] + sums[h])).astype(BF16))
                tot_sc[h] = tot_sc[h] + jnp.sum(gates[h][1], axis=-1, keepdims=True)
            return ws

        def add_values(w_buf, j):
            v2 = tile_of(v_ref, j)
            acc_sc[...] += jnp.where(is_a, _nn(w_buf[0], v2), _nn(w_buf[1], v2))

        def trip(j, s):
            add_values(w_bufs[s], j + 1)
            put(z_bufs[1 - s], scores(jnp.maximum(j - 1, first)))
            put(w_bufs[1 - s], weights((z_bufs[s][0], z_bufs[s][1])))

        @pl.when(trips > 0)
        def _():
            put(z0_sc, scores(i - 2))
            w0_sc[...] = jnp.zeros_like(w0_sc)

            def two_trips(pp, carry):
                j = i - 2 - 2 * pp
                trip(j, 0)
                trip(j - 1, 1)
                return carry

            lax.fori_loop(0, trips // 2, two_trips, 0)
            odd = trips % 2 == 1

            @pl.when(odd)
            def _():
                trip(first, 0)
                add_values(w1_sc, first)

            @pl.when(jnp.logical_not(odd))
            def _():
                add_values(w0_sc, first)

        o_ref[...] = acc_sc[...].astype(BF16)

    return pl.pallas_call(
        body, name="attn_fwd", grid=(npair, nq),
        in_specs=[_bs((blk, LANES), lambda p, i: (i, p)),
                  _bs((seq, LANES), lambda p, i: (0, npair + p)),
                  _bs((seq, LANES), lambda p, i: (0, 2 * npair + p))],
        out_specs=_bs((blk, LANES), lambda p, i: (i, p)),
        out_shape=_sds((seq, ATTN_W), BF16),
        scratch_shapes=[pltpu.VMEM((2, blk, blk), F32), pltpu.VMEM((2, blk, blk), F32),
                        pltpu.VMEM((2, blk, blk), BF16), pltpu.VMEM((2, blk, blk), BF16),
                        pltpu.VMEM((2, blk, 1), F32), pltpu.VMEM((2, blk, 1), F32), pltpu.VMEM((blk, LANES), F32)],
        compiler_params=_cparams(2),
    )(proj, proj, proj)


def _attn_bwd(proj, do, seq):
    blk = ATT_BLK
    nq = seq // blk
    npair = N_HEADS // 2

    def body(q_ref, k_ref, v_ref, do_ref, dq_ref, dk_ref, dv_ref,
             prod0_sc, prod1_sc, pend0_sc, pend1_sc, tot_sc, live_sc, cum_sc, pre_sc, dq_sc):
        i = pl.program_id(1)

        @pl.when(i == 0)
        def _():
            dk_ref[...] = jnp.zeros_like(dk_ref)
            dv_ref[...] = jnp.zeros_like(dv_ref)

        is_a = lax.broadcasted_iota(jnp.int32, (1, LANES), 1) < HEAD_DIM
        q2 = (q_ref[...] * Q_SCALE).astype(BF16)
        do2 = do_ref[...]
        zero = jnp.zeros_like(q2)
        qs = (jnp.where(is_a, q2, zero), jnp.where(is_a, zero, q2))
        dos = (jnp.where(is_a, do2, zero), jnp.where(is_a, zero, do2))
        row = lax.broadcasted_iota(jnp.int32, (blk, blk), 0)
        col = lax.broadcasted_iota(jnp.int32, (blk, blk), 1)
        tri_after = (row > col).astype(BF16)
        tri_excl = (row < col).astype(BF16)
        causal = col < row

        def tile_of(ref, j):
            return ref[pl.ds(pl.multiple_of(j * blk, blk), blk), :].astype(BF16)

        def scores(j):
            k2 = tile_of(k_ref, j)
            return [_nt(qs[h], k2) for h in range(2)]

        def products(j):
            v2 = tile_of(v_ref, j)
            return scores(j) + [_nt(dos[h], v2) for h in range(2)]

        def row_sum(a):
            return jnp.sum(a, axis=-1, keepdims=True)

        def grad_matmuls(ws, dzs, j):
            rows = pl.ds(pl.multiple_of(j * blk, blk), blk)
            k2 = tile_of(k_ref, j)
            dq_sc[...] += jnp.where(is_a, _nn(dzs[0], k2), _nn(dzs[1], k2))
            dk_ref[rows, :] += jnp.where(is_a, _tn(dzs[0], q2), _tn(dzs[1], q2))
            if ws is not None:
                dv_ref[rows, :] += jnp.where(is_a, _tn(ws[0], do2), _tn(ws[1], do2))

        has_left = i > 0
        left = jnp.maximum(i - 1, 0)
        p_d, p_l = products(i), products(left)
        g_d = [_log_gates(z) for z in p_d[:2]]
        g_l = [_log_gates(z) for z in p_l[:2]]
        keep_d = [jnp.where(causal, g[1], 0.0) for g in g_d]
        suf_d = [_nn(lk.astype(BF16), tri_after) for lk in keep_d]
        suf_l = [_nn(g[1].astype(BF16), tri_after) for g in g_l]
        w_d, w_l, gg_d, gg_l = [], [], [], []
        for h in range(2):
            sum_d = row_sum(keep_d[h])
            w_d.append(jnp.where(causal, jnp.exp(g_d[h][0] + suf_d[h]), 0.0))
            w_l.append(jnp.exp(g_l[h][0] + (jnp.where(has_left, sum_d, NO_TILE) + suf_l[h])))
            gg_d.append(p_d[2 + h] * w_d[h])
            gg_l.append(p_l[2 + h] * w_l[h])
            tot_sc[h] = sum_d + row_sum(g_l[h][1])
        before_d = [_nn(g.astype(BF16), tri_excl) for g in gg_d]
        before_l = [_nn(g.astype(BF16), tri_excl) for g in gg_l]
        dz_d, dz_l = [], []
        for h in range(2):
            beta_d, beta_l = jnp.exp(g_d[h][0]), jnp.exp(g_l[h][0])
            dz_l.append((gg_l[h] * (1.0 - beta_l) - before_l[h] * beta_l).astype(BF16))
            dz = gg_d[h] * (1.0 - beta_d) - (row_sum(gg_l[h]) + before_d[h]) * beta_d
            dz_d.append(jnp.where(causal, dz, 0.0).astype(BF16))
        dq_sc[...] = jnp.zeros_like(dq_sc)
        grad_matmuls([w.astype(BF16) for w in w_l], dz_l, left)
        grad_matmuls([w.astype(BF16) for w in w_d], dz_d, i)

        live_sc[...] = tot_sc[...]
        first = _first_live_tile(i - 2, scores, live_sc)
        trips = i - 1 - first
        prod_bufs, pend_bufs = (prod0_sc, prod1_sc), (pend0_sc, pend1_sc)

        def local_grads(prods):
            zs, dws = prods[:2], prods[2:]
            gates = [_log_gates(z) for z in zs]
            sums = [_nn(g[1].astype(BF16), tri_after) for g in gates]
            ws, gs = [], []
            for h in range(2):
                cum = cum_sc[h] + row_sum(gates[h][1])
                cum_sc[h] = cum
                ws.append(jnp.exp(gates[h][0] + ((live_sc[h] - cum) + sums[h])))
                gs.append(dws[h] * ws[h])
            befores = [_nn(g.astype(BF16), tri_excl) for g in gs]
            dzs = []
            for h in range(2):
                beta = jnp.exp(gates[h][0])
                dzs.append((gs[h] * (1.0 - beta) - (pre_sc[h] + befores[h]) * beta).astype(BF16))
                pre_sc[h] = pre_sc[h] + row_sum(gs[h])
            return [w.astype(BF16) for w in ws] + dzs

        def put(ref, vals):
            for n, val in enumerate(vals):
                ref[n] = val

        def flush(pend, j):
            grad_matmuls([pend[0], pend[1]], [pend[2], pend[3]], j)

        def trip(j, s):
            flush(pend_bufs[s], jnp.maximum(j - 1, first))
            put(prod_bufs[1 - s], products(j + 1))
            put(pend_bufs[1 - s], local_grads([prod_bufs[s][n] for n in range(4)]))

        def earlier_keys_share(j, mask):
            dzs = []
            for h, z in enumerate(scores(j)):
                beta = jnp.exp(_log_gates(z)[0])
                dzs.append(jnp.where(mask, -pre_sc[h] * beta, 0.0).astype(BF16))
            grad_matmuls(None, dzs, j)

        @pl.when(trips > 0)
        def _():
            cum_sc[...] = jnp.zeros_like(cum_sc)
            pre_sc[...] = jnp.zeros_like(pre_sc)
            pend0_sc[...] = jnp.zeros_like(pend0_sc)
            put(prod0_sc, products(first))

            def two_trips(pp, carry):
                trip(first + 2 * pp, 0)
                trip(first + 2 * pp + 1, 1)
                return carry

            lax.fori_loop(0, trips // 2, two_trips, 0)
            odd = trips % 2 == 1

            @pl.when(odd)
            def _():
                trip(i - 2, 0)
                flush(pend1_sc, i - 2)

            @pl.when(jnp.logical_not(odd))
            def _():
                flush(pend0_sc, i - 2)

            earlier_keys_share(i - 1, True)
            earlier_keys_share(i, causal)

        dq_ref[...] = dq_sc[...] * Q_SCALE

    qmap = lambda p, i: (i, p)
    return pl.pallas_call(
        body, name="attn_bwd", grid=(npair, nq),
        in_specs=[_bs((blk, LANES), qmap),
                  _bs((seq, LANES), lambda p, i: (0, npair + p)),
                  _bs((seq, LANES), lambda p, i: (0, 2 * npair + p)),
                  _bs((blk, LANES), qmap)],
        out_specs=[_bs((blk, LANES), qmap),
                   _bs((seq, LANES), lambda p, i: (0, p)),
                   _bs((seq, LANES), lambda p, i: (0, p))],
        out_shape=[_sds((seq, ATTN_W), F32)] * 3,
        scratch_shapes=[pltpu.VMEM((4, blk, blk), F32), pltpu.VMEM((4, blk, blk), F32),
                        pltpu.VMEM((4, blk, blk), BF16), pltpu.VMEM((4, blk, blk), BF16),
                        pltpu.VMEM((2, blk, 1), F32), pltpu.VMEM((2, blk, 1), F32), pltpu.VMEM((2, blk, 1), F32),
                        pltpu.VMEM((2, blk, 1), F32), pltpu.VMEM((blk, LANES), F32)],
        compiler_params=_cparams(2),
    )(proj, proj, proj, do)


def _elementwise(name, fn, ins, out_dtypes):
    rows, cols = ins[0].shape
    tr = rows
    for cand in (512, 256, 128, 64, 32, 16, 8):
        if rows % cand == 0 and cand * cols * 4 <= 2 * 1024 * 1024:
            tr = cand
            break
    n_in = len(ins)

    def body(*refs):
        res = fn(*[r[...] for r in refs[:n_in]])
        for r, val in zip(refs[n_in:], res):
            r[...] = val.astype(r.dtype)

    spec = _bs((tr, cols), lambda i: (i, 0))
    return pl.pallas_call(
        body, name=name, grid=(rows // tr,),
        in_specs=[spec] * n_in, out_specs=[spec] * len(out_dtypes),
        out_shape=[_sds((rows, cols), dt) for dt in out_dtypes],
        compiler_params=_cparams(1),
    )(*ins)


def _adamw_fn(w, g, m, v):
    m = ADAM_B1 * m + (1.0 - ADAM_B1) * g
    v = ADAM_B2 * v + (1.0 - ADAM_B2) * (g * g)
    m_hat = m / (1.0 - ADAM_B1 ** ADAM_STEP)
    v_hat = v / (1.0 - ADAM_B2 ** ADAM_STEP)
    delta = -ADAM_LR * (m_hat / (jnp.sqrt(v_hat) + ADAM_EPS) + ADAM_WD * w)
    return delta, m, v


def _adamw(name, w, g, m, v):
    shape = w.shape
    as2d = lambda a: a.reshape(-1, shape[-1])
    delta, nm, nv = _elementwise(name, _adamw_fn, [as2d(w), as2d(g), as2d(m), as2d(v)], [F32, F32, F32])
    return delta.reshape(shape), nm.reshape(shape), nv.reshape(shape)


def _place():
    x, y, c = lax.axis_index("x"), lax.axis_index("y"), lax.axis_index("c")
    chips = [(1 - x, y), (x, 1 - y), (1 - x, 1 - y)]
    return x, y, c, chips


ANY = pl.BlockSpec(memory_space=pl.ANY)
VMEM_WHOLE = pl.BlockSpec(memory_space=pltpu.VMEM)


def _allgather_weights(shards):
    n = len(shards)

    def body(*refs):
        src, dst = refs[:n], refs[n:2 * n]
        send_sems, recv_sems, local_sems = refs[2 * n:]
        x, y, c, chips = _place()
        me, sibling, mychip = (x, y, c), (x, y, 1 - c), 2 * x + y

        x_nbr, y_nbr, diag = 2 * (1 - x) + y, 2 * x + (1 - y), 2 * (1 - x) + (1 - y)
        to_x, to_y = (1 - x, y, c), (x, 1 - y, c)

        def parts(w):
            hr = src[w].shape[0] // 2
            first = hr // 2 if hr % 32 == 0 else hr
            return first, hr - first

        def rows_of(w, chip, half, route):
            hr = src[w].shape[0] // 2
            first, second = parts(w)
            start, size = {0: (0, hr), 1: (0, hr), 2: (0, first), 3: (first, second)}[route]
            return dst[w].at[chip, pl.ds(half * hr + start, size)]

        def copy(w, k, src_ref, dst_ref, to):
            return pltpu.make_async_remote_copy(src_ref=src_ref, dst_ref=dst_ref, send_sem=send_sems.at[w, k],
                                                recv_sem=recv_sems.at[w, k], device_id=to, device_id_type=MESH)

        def landed(w, route):
            chip = {0: x_nbr, 1: y_nbr, 2: diag, 3: diag}[route]
            return rows_of(w, chip, c, route), chip

        def routes(w):
            return (0, 1, 2, 3) if parts(w)[1] else (0, 1, 2)

        started, local = [], []
        for w in range(n):
            hr = src[w].shape[0] // 2
            own = pltpu.make_async_copy(src[w], dst[w].at[mychip], local_sems.at[w])
            own.start()
            local.append(own)
            mine = src[w].at[pl.ds(c * hr, hr)]
            for route, to in ((0, to_x), (1, to_y)):
                cp = copy(w, route, mine, rows_of(w, mychip, c, route), to)
                cp.start()
                started.append(cp)

        def pass_on(w, route):
            got, chip = landed(w, route)
            copy(w, route, got, got, me).wait_recv()
            if route == 1:
                part = rows_of(w, chip, c, 2)
                started.append(copy(w, 2, part, part, to_x))
                started[-1].start()
            if route == 0 and parts(w)[1]:
                part = rows_of(w, chip, c, 3)
                started.append(copy(w, 3, part, part, to_y))
                started[-1].start()
            started.append(copy(w, 4 + route, got, got, sibling))
            started[-1].start()

        for w in range(n):
            pass_on(w, 1)
            pass_on(w, 0)
        for w in range(n):
            for route in routes(w)[2:]:
                pass_on(w, route)
        for w in range(n):
            for route in routes(w):
                chip = landed(w, route)[1]
                from_sib = rows_of(w, chip, 1 - c, route)
                copy(w, 4 + route, from_sib, from_sib, me).wait_recv()
        for cp in local:
            cp.wait()
        for cp in started:
            cp.wait_send()

    return pl.pallas_call(
        body, name="allgather_weights",
        in_specs=[VMEM_WHOLE] * n, out_specs=[VMEM_WHOLE] * n,
        out_shape=[_sds((N_CHIPS,) + s.shape, s.dtype) for s in shards],
        scratch_shapes=[pltpu.SemaphoreType.DMA((n, 8)), pltpu.SemaphoreType.DMA((n, 8)),
                        pltpu.SemaphoreType.DMA((n,))],
        compiler_params=pltpu.CompilerParams(vmem_limit_bytes=VMEM_LIMIT),
    )(*shards)


SUM_ROWS = 64


def _rs_pair_sum(name, grads):
    n = len(grads)

    def body(*refs):
        g, out = refs[:n], refs[n:2 * n]
        stage, land, keep = refs[2 * n:3 * n], refs[3 * n:4 * n], refs[4 * n:5 * n]
        send_sems, recv_sems, stage_sems, keep_sems = refs[5 * n:]
        x, y, c, _ = _place()
        sibling = (x, y, 1 - c)
        loads = []
        for w in range(n):
            hr = g[w].shape[1] // 2
            st = pltpu.make_async_copy(g[w].at[:, pl.ds((1 - c) * hr, hr)], stage[w], stage_sems.at[w])
            kp = pltpu.make_async_copy(g[w].at[:, pl.ds(c * hr, hr)], keep[w], keep_sems.at[w])
            st.start()
            kp.start()
            loads.append((st, kp))
        gives = []
        for w in range(n):
            loads[w][0].wait()
            give = pltpu.make_async_remote_copy(src_ref=stage[w], dst_ref=land[w], send_sem=send_sems.at[w],
                                                recv_sem=recv_sems.at[w], device_id=sibling, device_id_type=MESH)
            give.start()
            gives.append(give)
        for w in range(n):
            loads[w][1].wait()
            gives[w].wait_recv()
            nb = g[w].shape[1] // 2 // SUM_ROWS

            def add(idx, carry, w=w, nb=nb):
                k, r = idx // nb, pl.multiple_of((idx % nb) * SUM_ROWS, SUM_ROWS)
                rows = pl.ds(r, SUM_ROWS)
                out[w][k, rows, :] = (keep[w][k, rows, :] + land[w][k, rows, :]).astype(BF16)
                return carry

            lax.fori_loop(0, N_CHIPS * nb, add, 0)
        for give in gives:
            give.wait_send()

    half = [(N_CHIPS, a.shape[1] // 2, a.shape[2]) for a in grads]
    bufs = [pltpu.VMEM(s, F32) for s in half]
    sems = pltpu.SemaphoreType.DMA((n,))
    return pl.pallas_call(
        body, name=name,
        in_specs=[ANY] * n, out_specs=[VMEM_WHOLE] * n, out_shape=[_sds(s, BF16) for s in half],
        scratch_shapes=bufs + bufs + bufs + [sems, sems, sems, sems],
        compiler_params=pltpu.CompilerParams(vmem_limit_bytes=VMEM_LIMIT),
    )(*grads)


def _rs_exchange_join(parts):
    n = len(parts)

    def body(*refs):
        t, full = refs[:n], refs[n:2 * n]
        got_x, got_y, pass_on, got_2 = (refs[m * n:(m + 1) * n] for m in range(2, 6))
        send_sems, recv_sems = refs[6 * n:]
        x, y, c, _ = _place()
        mychip, sibling = 2 * x + y, (x, y, 1 - c)
        x_nbr, y_nbr, diag = 2 * (1 - x) + y, 2 * x + (1 - y), 2 * (1 - x) + (1 - y)
        to_x, to_y = (1 - x, y, c), (x, 1 - y, c)
        sends = []

        def copy(w, k, src_ref, dst_ref, to):
            return pltpu.make_async_remote_copy(src_ref=src_ref, dst_ref=dst_ref, send_sem=send_sems.at[w, k],
                                                recv_sem=recv_sems.at[w, k], device_id=to, device_id_type=MESH)

        def start(cp):
            cp.start()
            sends.append(cp)

        def add_rows(w, count, fn):
            def step(idx, carry):
                fn(pl.ds(pl.multiple_of(idx * SUM_ROWS, SUM_ROWS), SUM_ROWS), pl.multiple_of(idx * SUM_ROWS, SUM_ROWS))
                return carry
            lax.fori_loop(0, count // SUM_ROWS, step, 0)

        f32 = lambda v: v.astype(F32)
        for w in range(n):
            ha = t[w].shape[1] // 2
            part_a, part_b = pl.ds(0, ha), pl.ds(ha, ha)
            start(copy(w, 0, t[w].at[x_nbr, part_a], got_x[w].at[0], to_x))
            start(copy(w, 1, t[w].at[diag, part_a], got_x[w].at[1], to_x))
            start(copy(w, 2, t[w].at[y_nbr, part_b], got_y[w].at[0], to_y))
            start(copy(w, 3, t[w].at[diag, part_b], got_y[w].at[1], to_y))
        for w in range(n):
            hr = t[w].shape[1]
            ha = hr // 2
            for k in (0, 1):
                copy(w, k, got_x[w].at[k], got_x[w].at[k], to_x).wait_recv()

            def sum_a(rows, r, w=w, hr=hr):
                full[w][pl.ds(pl.multiple_of(c * hr + r, SUM_ROWS), SUM_ROWS), :] = \
                    f32(t[w][mychip, rows, :]) + f32(got_x[w][0, rows, :])
                pass_on[w][rows, :] = (f32(t[w][y_nbr, rows, :]) + f32(got_x[w][1, rows, :])).astype(BF16)

            add_rows(w, ha, sum_a)
            start(copy(w, 4, pass_on[w].at[pl.ds(0, ha)], got_2[w].at[pl.ds(0, ha)], to_y))
            for k in (2, 3):
                copy(w, k, got_y[w].at[k - 2], got_y[w].at[k - 2], to_y).wait_recv()

            def sum_b(rows, r, w=w, hr=hr, ha=ha):
                lower = pl.ds(pl.multiple_of(ha + r, SUM_ROWS), SUM_ROWS)
                full[w][pl.ds(pl.multiple_of(c * hr + ha + r, SUM_ROWS), SUM_ROWS), :] = \
                    f32(t[w][mychip, lower, :]) + f32(got_y[w][0, rows, :])
                pass_on[w][lower, :] = (f32(t[w][x_nbr, lower, :]) + f32(got_y[w][1, rows, :])).astype(BF16)

            add_rows(w, ha, sum_b)
            start(copy(w, 5, pass_on[w].at[pl.ds(ha, ha)], got_2[w].at[pl.ds(ha, ha)], to_x))
        for w in range(n):
            hr = t[w].shape[1]
            ha = hr // 2
            copy(w, 4, got_2[w].at[pl.ds(0, ha)], got_2[w].at[pl.ds(0, ha)], to_y).wait_recv()
            copy(w, 5, got_2[w].at[pl.ds(ha, ha)], got_2[w].at[pl.ds(ha, ha)], to_x).wait_recv()

            def finish(rows, r, w=w, hr=hr):
                out_rows = pl.ds(pl.multiple_of(c * hr + r, SUM_ROWS), SUM_ROWS)
                full[w][out_rows, :] = full[w][out_rows, :] + f32(got_2[w][rows, :])

            add_rows(w, hr, finish)
            mine = full[w].at[pl.ds(c * hr, hr)]
            start(copy(w, 6, mine, mine, sibling))
        for w in range(n):
            hr = t[w].shape[1]
            theirs = full[w].at[pl.ds((1 - c) * hr, hr)]
            copy(w, 6, theirs, theirs, sibling).wait_recv()
        for cp in sends:
            cp.wait_send()

    half = lambda a: pltpu.VMEM((2, a.shape[1] // 2, a.shape[2]), a.dtype)
    whole = lambda a: pltpu.VMEM(a.shape[1:], a.dtype)
    return pl.pallas_call(
        body, name="rs_exchange_join",
        in_specs=[VMEM_WHOLE] * n, out_specs=[VMEM_WHOLE] * n,
        out_shape=[_sds((2 * a.shape[1], a.shape[2]), F32) for a in parts],
        scratch_shapes=[half(a) for a in parts] + [half(a) for a in parts] + [whole(a) for a in parts]
        + [whole(a) for a in parts] + [pltpu.SemaphoreType.DMA((n, 7)), pltpu.SemaphoreType.DMA((n, 7))],
        compiler_params=pltpu.CompilerParams(vmem_limit_bytes=VMEM_LIMIT),
    )(*parts)


def _small_allreduce(loss_p, dg_parts, dbg_a, dbg_c, dwc):
    ins = [loss_p] + list(dg_parts) + [dbg_a, dbg_c, dwc]
    n_in = len(ins)
    vmem = pl.BlockSpec(memory_space=pltpu.VMEM)

    def body(*refs):
        in_refs = refs[:n_in]
        out_ref, vec, buf, send_sems, recv_sems = refs[n_in:]
        x, y, c, _ = _place()
        me = 4 * x + 2 * y + c
        vec[...] = jnp.zeros_like(vec)
        vec[0:1, :] = jnp.sum(in_refs[0][...], axis=0)
        for r in range(5):
            vec[1 + r:2 + r, :] = jnp.sum(in_refs[1 + r][...], axis=0)
        vec[6:7, :] = jnp.sum(in_refs[6][...], axis=0)
        vec[7:8, :] = jnp.sum(in_refs[7][...], axis=0)
        vec[8:16, 0:CONV_W] = jnp.sum(in_refs[8][...], axis=0)
        buf[pl.ds(me, 1)] = vec[...][None]
        copies = []
        for r in range(1, 8):
            fx, fy, fc = (r >> 2) & 1, (r >> 1) & 1, r & 1
            to = (1 - x if fx else x, 1 - y if fy else y, 1 - c if fc else c)
            cp = pltpu.make_async_remote_copy(src_ref=vec, dst_ref=buf.at[me], send_sem=send_sems.at[r - 1],
                                              recv_sem=recv_sems.at[r - 1], device_id=to, device_id_type=MESH)
            cp.start()
            copies.append(cp)
        for cp in copies:
            cp.wait()
        total = buf[0]
        for s in range(1, 8):
            total = total + buf[s]
        out_ref[...] = total
        out_ref[0:1, :] = jnp.broadcast_to(jnp.sum(total[0:1, :], axis=-1, keepdims=True), (1, D_MODEL))

    return pl.pallas_call(
        body, name="small_allreduce",
        in_specs=[vmem] * n_in, out_specs=vmem, out_shape=_sds((SMALL_ROWS, D_MODEL), F32),
        scratch_shapes=[pltpu.VMEM((SMALL_ROWS, D_MODEL), F32), pltpu.VMEM((8, SMALL_ROWS, D_MODEL), F32),
                        pltpu.SemaphoreType.DMA((7,)), pltpu.SemaphoreType.DMA((7,))],
    )(*ins)


def _local_step(x, p, tgt, g, b_gate, w_conv, wf):
    seq = x.shape[0]
    tm = min(seq, 1024)
    th = min(seq, 512)
    tl = min(seq, 2048)
    ni, nh, nl = seq // tm, seq // th, seq // tl
    g_pre_mix, g_post_mix, g_pre_mlp, g_post_mlp, g_ple = g
    w_in, w_ao, w_co, w_o, w_up, w_down, w_pg, w_pp, w_in_nat, w_up_nat = wf
    D = D_MODEL
    vec = lambda a, blk=0: (a, _bs((1, D), lambda i, j, k: (0, blk)))
    rows_i = lambda a, t, blk=0: (a, _bs((t, D), lambda i, j, k: (i, blk)))
    rows_k = lambda a, t, blk=0: (a, _bs((t, D), lambda i, j, k: (k, blk)))
    part = lambda n: (_sds((n, 1, D), F32), _bs((None, 1, D), lambda i, j, k: (i, 0, 0)))
    full2 = lambda a: (a, _bs(a.shape, lambda i, j, k: (0, 0)))

    normed = lambda xb, gb: (_rms(xb, gb).astype(BF16),) * 2
    keep_a = lambda t: [(_sds((seq, D), BF16), _bs((t, D), lambda i, j, k: (i, 0)))]
    main_w = D_IN - 2 * D
    proj, gates, h1 = _mm("proj_in", "nn", (nh, 1, 1),
                          a_ins=[rows_i(x, th), vec(g_pre_mix)], a_fn=normed,
                          b_ins=[full2(w_in_nat)], b_fn=_ident,
                          epi_fn=lambda acc: (acc[:, :main_w], acc[:, main_w:]),
                          outs=[(_sds((seq, main_w), F32), _bs((th, main_w), lambda i, j, k: (i, 0))),
                                (_sds((seq, 2 * D), BF16), _bs((th, 2 * D), lambda i, j, k: (i, 0)))],
                          acc_shape=(th, D_IN), a_cache=((th, D), BF16), a_outs=keep_a(th))
    o = _attn_fwd(proj, seq)
    (y_attn,) = _mm("attn_out", "nn", (ni, 1, 1),
                    a_ins=[(o, _bs((tm, ATTN_W), lambda i, j, k: (i, 0)))], a_fn=_ident,
                    b_ins=[full2(w_ao)], b_fn=_ident,
                    outs=[(_sds((seq, D), BF16), _bs((tm, D), lambda i, j, k: (i, 0)))], acc_shape=(tm, D))
    e, d = _conv_fwd(proj, w_conv, seq, tm)
    (y_conv,) = _mm("conv_out", "nn", (ni, 1, 1),
                    a_ins=[(e, _bs((tm, CONV_W), lambda i, j, k: (i, 0)))], a_fn=_ident,
                    b_ins=[full2(w_co)], b_fn=_ident,
                    outs=[(_sds((seq, D), BF16), _bs((tm, D), lambda i, j, k: (i, 0)))], acc_shape=(tm, D))

    def gate_values(ga, gc, ba, bc):
        return _sig(ga.astype(F32) + ba), _sig(gc.astype(F32) + bc)

    def mix_fn(ga, gc, ya, yc, ba, bc):
        sa, sc = gate_values(ga, gc, ba, bc)
        return ((sa * ya.astype(F32) + sc * yc.astype(F32)).astype(BF16),) * 2

    def post_mix(acc, xb, gb):
        return acc, xb + _rms(acc, gb)

    mix_ins = lambda rows: [rows(gates, th, 0), rows(gates, th, 1), rows(y_attn, th), rows(y_conv, th),
                            vec(b_gate, 0), vec(b_gate, 1)]
    mixed, x1, mixin = _mm("mix_out", "nn", (nh, 1, 1),
                           a_ins=mix_ins(rows_i), a_fn=mix_fn, b_ins=[full2(w_o)], b_fn=_ident,
                           epi_ins=[rows_i(x, th), vec(g_post_mix)], epi_fn=post_mix,
                           outs=[(_sds((seq, D), F32), _bs((th, D), lambda i, j, k: (i, 0)))] * 2,
                           acc_shape=(th, D), a_cache=((th, D), BF16), a_outs=keep_a(th))
    up, h2 = _mm("mlp_up", "nn", (nh, 1, 1),
                 a_ins=[rows_i(x1, th), vec(g_pre_mlp)], a_fn=normed,
                 b_ins=[full2(w_up_nat)], b_fn=_ident,
                 outs=[(_sds((seq, D_FF), BF16), _bs((th, D_FF), lambda i, j, k: (i, 0)))],
                 acc_shape=(th, D_FF), a_cache=((th, D), BF16), a_outs=keep_a(th))

    def relu2(ub):
        r = jnp.maximum(ub.astype(F32), 0.0)
        return (r * r).astype(BF16)

    dx2, df, dpre, h3, dpp, loss_p, dg_ple_p, dg_post_mlp_p = _mlp_down_ple_head(
        up, x1, p, tgt, g_ple, g_post_mlp, w_down, w_pg, w_pp, seq, th)

    (dw_pp,) = _mm("dw_ple_proj", "tn", (1, 1, nh),
                   a_ins=[(p, _bs((th, PLE_DIM), lambda i, j, k: (k, 0)))], a_fn=_to_bf16,
                   b_ins=[rows_k(dpp, th)], b_fn=_ident,
                   outs=[(_sds((PLE_DIM, D), F32), _bs((PLE_DIM, D), lambda i, j, k: (0, 0)))],
                   acc_shape=(PLE_DIM, D))
    (dw_pg,) = _mm("dw_ple_gate", "tn", (1, 1, ni),
                   a_ins=[rows_k(h3, tm)], a_fn=_ident, b_ins=[rows_k(dpre, tm)], b_fn=_ident,
                   outs=[(_sds((D, D), F32), _bs((D, D), lambda i, j, k: (0, 0)))], acc_shape=(D, D))

    def dup_fn(acc, ub):
        return (acc * (2.0 * jnp.maximum(ub.astype(F32), 0.0)),)

    (dup,) = _mm("d_mlp_down", "nt", (nh, 1, 1),
                 a_ins=[rows_i(df, th)], a_fn=_ident, b_ins=[full2(w_down)], b_fn=_ident,
                 epi_ins=[(up, _bs((th, D_FF), lambda i, j, k: (i, 0)))], epi_fn=dup_fn,
                 outs=[(_sds((seq, D_FF), BF16), _bs((th, D_FF), lambda i, j, k: (i, 0)))],
                 acc_shape=(th, D_FF))
    (dw_down,) = _mm("dw_mlp_down", "tn", (4, 1, nl),
                     a_ins=[(up, _bs((tl, D), lambda i, j, k: (k, i)))], a_fn=relu2,
                     b_ins=[rows_k(df, tl)], b_fn=_ident,
                     outs=[(_sds((D_FF, D), F32), _bs((D, D), lambda i, j, k: (i, 0)))], acc_shape=(D, D))
    (dw_up,) = _mm("dw_mlp_up", "tn", (1, 4, nl),
                   a_ins=[rows_k(h2, tl)], a_fn=_ident,
                   b_ins=[(dup, _bs((tl, D), lambda i, j, k: (k, j)))], b_fn=_ident,
                   outs=[(_sds((N_CHIPS, D, D), F32), _bs((None, D, D), lambda i, j, k: (j, 0, 0)))],
                   acc_shape=(D, D))

    def mlp_norm_bwd(acc, x1b, dx2b, mixedb, g_mlp, g_mix):
        dxn, dg_mlp = _rms_bwd(x1b, g_mlp, acc)
        dx1b = dx2b + dxn
        dmixedb, dg_mix = _rms_bwd(mixedb, g_mix, dx1b)
        return dx1b, dmixedb, dg_mlp, dg_mix

    dx1, dmixed, dg_pre_mlp_p, dg_post_mix_p = _mm(
        "d_mlp_up", "nt", (nh, 1, 1),
        a_ins=[(dup, _bs((th, D_FF), lambda i, j, k: (i, 0)))], a_fn=_ident,
        b_ins=[full2(w_up_nat)], b_fn=_ident,
        epi_ins=[rows_i(x1, th), rows_i(dx2, th), rows_i(mixed, th), vec(g_pre_mlp), vec(g_post_mix)],
        epi_fn=mlp_norm_bwd,
        outs=[(_sds((seq, D), F32), _bs((th, D), lambda i, j, k: (i, 0))),
              (_sds((seq, D), BF16), _bs((th, D), lambda i, j, k: (i, 0))), part(nh), part(nh)],
        acc_shape=(th, D))
    (dw_o,) = _mm("dw_mix_out", "tn", (1, 1, ni),
                  a_ins=[rows_k(mixin, tm)], a_fn=_ident, b_ins=[rows_k(dmixed, tm)], b_fn=_ident,
                  outs=[(_sds((D, D), F32), _bs((D, D), lambda i, j, k: (0, 0)))], acc_shape=(D, D))

    def gate_bwd(acc, ga, gc, ya, yc, ba, bc, wao, wco):
        sa, sc = gate_values(ga, gc, ba, bc)
        dga = acc * ya.astype(F32) * sa * (1.0 - sa)
        dgc = acc * yc.astype(F32) * sc * (1.0 - sc)
        dya, dyc = (acc * sa).astype(BF16), (acc * sc).astype(BF16)
        return (dya, dyc, jnp.concatenate([dga, dgc], axis=1), _nt(dya, wao), _nt(dyc, wco),
                jnp.sum(dga, axis=0, keepdims=True), jnp.sum(dgc, axis=0, keepdims=True))

    dya, dyc, dgate, do, de, dbg_a_p, dbg_c_p = _mm(
        "d_mix_out", "nt", (nh, 1, 1),
        a_ins=[rows_i(dmixed, th)], a_fn=_ident, b_ins=[full2(w_o)], b_fn=_ident,
        epi_ins=mix_ins(rows_i) + [full2(w_ao), full2(w_co)], epi_fn=gate_bwd,
        outs=[(_sds((seq, D), BF16), _bs((th, D), lambda i, j, k: (i, 0)))] * 2
             + [(_sds((seq, 2 * D), BF16), _bs((th, 2 * D), lambda i, j, k: (i, 0))),
                (_sds((seq, ATTN_W), BF16), _bs((th, ATTN_W), lambda i, j, k: (i, 0))),
                (_sds((seq, CONV_W), F32), _bs((th, CONV_W), lambda i, j, k: (i, 0))), part(nh), part(nh)],
        acc_shape=(th, D))
    (dw_ao,) = _mm("dw_attn_out", "tn", (1, 1, nh),
                   a_ins=[(o, _bs((th, ATTN_W), lambda i, j, k: (k, 0)))], a_fn=_ident,
                   b_ins=[rows_k(dya, th)], b_fn=_ident,
                   outs=[(_sds((ATTN_W, D), F32), _bs((ATTN_W, D), lambda i, j, k: (0, 0)))], acc_shape=(ATTN_W, D))
    dq, dk, dv = _attn_bwd(proj, do, seq)
    (dw_co,) = _mm("dw_conv_out", "tn", (1, 1, nh),
                   a_ins=[(e, _bs((th, CONV_W), lambda i, j, k: (k, 0)))], a_fn=_ident,
                   b_ins=[rows_k(dyc, th)], b_fn=_ident,
                   outs=[(_sds((CONV_W, D), F32), _bs((CONV_W, D), lambda i, j, k: (0, 0)))], acc_shape=(CONV_W, D))
    dconv, dwc_p = _conv_bwd(proj, de, d, w_conv, seq, tm)
    qkv_w = 3 * ATTN_W
    join_bf16 = lambda *blocks: jnp.concatenate([b.astype(BF16) for b in blocks], axis=1)
    piece = lambda a, t, rows, blk=0: (a, _bs((t, a.shape[1]), (lambda i, j, k: (k, blk)) if rows == "k"
                                             else (lambda i, j, k: (i, blk))))
    (dw_in_qkv,) = _mm("dw_proj_in_qkv", "tn", (1, 1, ni),
                       a_ins=[rows_k(h1, tm)], a_fn=_ident,
                       b_ins=[piece(dq, tm, "k"), piece(dk, tm, "k"), piece(dv, tm, "k")], b_fn=join_bf16,
                       outs=[(_sds((D, qkv_w), F32), _bs((D, qkv_w), lambda i, j, k: (0, 0)))], acc_shape=(D, qkv_w))
    (dw_in_conv,) = _mm("dw_proj_in_conv", "tn", (1, 1, nl),
                        a_ins=[rows_k(h1, tl)], a_fn=_ident, b_ins=[piece(dconv, tl, "k")], b_fn=_ident,
                        outs=[(_sds((D, 3 * CONV_W), F32), _bs((D, 3 * CONV_W), lambda i, j, k: (0, 0)))],
                        acc_shape=(D, 3 * CONV_W))
    (dw_in_gate,) = _mm("dw_proj_in_gate", "tn", (1, 2, nl),
                        a_ins=[rows_k(h1, tl)], a_fn=_ident,
                        b_ins=[(dgate, _bs((tl, D), lambda i, j, k: (k, j)))], b_fn=_ident,
                        outs=[(_sds((D, 2 * D), F32), _bs((D, D), lambda i, j, k: (0, j)))], acc_shape=(D, D))
    dw_in = jnp.concatenate([dw_in_qkv, dw_in_conv, dw_in_gate], axis=1)

    def in_norm_bwd(acc, xb, dx1b, gb):
        dxn, dg = _rms_bwd(xb, gb, acc)
        return dx1b + dxn, dg

    grad_x, dg_pre_mix_p = _mm("d_proj_in", "nt", (nh, 1, 1),
                               a_ins=[piece(dq, th, "i"), piece(dk, th, "i"), piece(dv, th, "i"),
                                      piece(dconv, th, "i"), piece(dgate, th, "i")], a_fn=join_bf16,
                               b_ins=[full2(w_in_nat)], b_fn=_ident,
                               epi_ins=[rows_i(x, th), rows_i(dx1, th), vec(g_pre_mix)], epi_fn=in_norm_bwd,
                               outs=[(_sds((seq, D), F32), _bs((th, D), lambda i, j, k: (i, 0))), part(nh)],
                               acc_shape=(th, D))

    chip_major = lambda a: a.reshape(a.shape[0], N_CHIPS, a.shape[1] // N_CHIPS).transpose(1, 0, 2)
    big = [chip_major(dw_in), chip_major(dw_ao), chip_major(dw_co), dw_o.reshape(N_CHIPS, D // N_CHIPS, D), dw_up,
           dw_down.reshape(N_CHIPS, D_FF // N_CHIPS, D), dw_pg.reshape(N_CHIPS, D // N_CHIPS, D), chip_major(dw_pp)]
    small = (loss_p, [dg_pre_mix_p, dg_post_mix_p, dg_pre_mlp_p, dg_post_mlp_p, dg_ple_p], dbg_a_p, dbg_c_p, dwc_p)
    return grad_x, big, small


RS_GROUPS = ((0,), (4,), (5,), (1, 2, 3, 6, 7))


def _reduce_scatter(big):
    pair = [None] * len(big)
    for gi, group in enumerate(RS_GROUPS):
        for w, s in zip(group, _rs_pair_sum(f"rs_pair_sum_{gi}", [big[w] for w in group])):
            pair[w] = s
    return _rs_exchange_join(pair)


def kernel(x, p, g_pre_mix, w_in, b_gate, w_conv, w_attn_out, w_conv_out, w_o, g_post_mix, g_pre_mlp, w_up, w_down, g_post_mlp, g_ple, w_ple_gate, w_ple_proj, loss_target, m_g_pre_mix, m_w_in, m_b_gate, m_w_conv, m_w_attn_out, m_w_conv_out, m_w_o, m_g_post_mix, m_g_pre_mlp, m_w_up, m_w_down, m_g_post_mlp, m_g_ple, m_w_ple_gate, m_w_ple_proj, v_g_pre_mix, v_w_in, v_b_gate, v_w_conv, v_w_attn_out, v_w_conv_out, v_w_o, v_g_post_mix, v_g_pre_mlp, v_w_up, v_w_down, v_g_post_mlp, v_g_ple, v_w_ple_gate, v_w_ple_proj):
    mats = [w_in, w_attn_out, w_conv_out, w_o, w_up, w_down, w_ple_gate, w_ple_proj]
    mats_m = [m_w_in, m_w_attn_out, m_w_conv_out, m_w_o, m_w_up, m_w_down, m_w_ple_gate, m_w_ple_proj]
    mats_v = [v_w_in, v_w_attn_out, v_w_conv_out, v_w_o, v_w_up, v_w_down, v_w_ple_gate, v_w_ple_proj]
    gains = [g_pre_mix, g_post_mix, g_pre_mlp, g_post_mlp, g_ple]
    gains_m = [m_g_pre_mix, m_g_post_mix, m_g_pre_mlp, m_g_post_mlp, m_g_ple]
    gains_v = [v_g_pre_mix, v_g_post_mix, v_g_pre_mlp, v_g_post_mlp, v_g_ple]

    taps = jnp.concatenate([w_conv[0], jnp.zeros((CONV_PAD_ROWS - 3, LANES), F32)], axis=0)
    gathered = _allgather_weights([w[0].astype(BF16) for w in mats] + [taps])
    cols_joined = lambda a: a.transpose(1, 0, 2).reshape(a.shape[1], N_CHIPS * a.shape[2])
    rows_joined = lambda a: a.reshape(N_CHIPS * a.shape[1], a.shape[2])
    wf = [gathered[0], cols_joined(gathered[1]), cols_joined(gathered[2]), rows_joined(gathered[3]), gathered[4],
          rows_joined(gathered[5]), rows_joined(gathered[6]), cols_joined(gathered[7]),
          cols_joined(gathered[0]), cols_joined(gathered[4])]
    w_conv_full = cols_joined(gathered[8])[0:3, :]
    chip = 2 * lax.axis_index("x") + lax.axis_index("y")

    grad_x, big, small = _local_step(x[0], p[0, 0], loss_target[0], gains, b_gate, w_conv_full, wf)

    shard_grads = _reduce_scatter(big)
    red = _small_allreduce(*small)
    loss = red[0, 0]
    grad_gains = [red[1 + r:2 + r, :] for r in range(5)]
    grad_b_gate = jnp.concatenate([red[6:7, :], red[7:8, :]], axis=1)
    grad_w_conv = lax.dynamic_slice(red[8:11, :], (0, chip * LANES), (3, LANES))[None]

    grads_big = [gr.reshape(w.shape) for gr, w in zip(shard_grads, mats)]
    upd_big = [_adamw(f"adamw_{i}", w, gr, m, v) for i, (w, gr, m, v) in enumerate(zip(mats, grads_big, mats_m, mats_v))]
    pack = lambda vs, bg: jnp.concatenate(list(vs) + [bg.reshape(2, D_MODEL), jnp.zeros((1, D_MODEL), F32)], axis=0)
    upd_small = _adamw("adamw_small", pack(gains, b_gate), pack(grad_gains, grad_b_gate),
                       pack(gains_m, m_b_gate), pack(gains_v, v_b_gate))
    upd_conv = _adamw("adamw_conv", w_conv, grad_w_conv, m_w_conv, v_w_conv)

    def small_out(a, which):
        gains_out = [a[r:r + 1, :] for r in range(5)]
        return gains_out, a[5:7, :].reshape(1, 2 * D_MODEL)

    def ordered(g_pre_mix_, big_, b_gate_, conv_, g_rest):
        return [g_pre_mix_, big_[0], b_gate_, conv_, big_[1], big_[2], big_[3], g_rest[0], g_rest[1], big_[4], big_[5],
                g_rest[2], g_rest[3], big_[6], big_[7]]

    outs = [loss, grad_x[None]]
    outs += ordered(grad_gains[0], grads_big, grad_b_gate, grad_w_conv, grad_gains[1:])
    for which in range(3):
        g_out, b_out = small_out(upd_small[which], which)
        outs += ordered(g_out[0], [u[which] for u in upd_big], b_out, upd_conv[which], g_out[1:])
    return tuple(outs)
```

```python
import functools

import jax
import jax.numpy as jnp
from jax import lax
from jax.experimental import pallas as pl
from jax.experimental.pallas import tpu as pltpu

F32 = jnp.float32
BF16 = jnp.bfloat16
MESH = pl.DeviceIdType.MESH

D_MODEL = 1024
N_HEADS = 8
HEAD_DIM = 64
ATTN_W = N_HEADS * HEAD_DIM
CONV_W = 512
D_FF = 4096
PLE_DIM = 256
D_IN = 5120
N_CHIPS = 4
EPS = 1e-6
Q_SCALE = HEAD_DIM ** -0.5

ADAM_LR = 0.001
ADAM_B1 = 0.9
ADAM_B2 = 0.999
ADAM_EPS = 1e-08
ADAM_WD = 0.01
ADAM_STEP = 10

V7X_VMEM_BYTES = 64 * 1024 * 1024
VMEM_LIMIT = V7X_VMEM_BYTES - 8 * 1024 * 1024
LANES = 128
ATT_BLK = 256
SMALL_ROWS = 16
CONV_PAD_ROWS = 16


def _cparams(n_grid):
    return pltpu.CompilerParams(dimension_semantics=("arbitrary",) * n_grid, vmem_limit_bytes=VMEM_LIMIT)


def _bs(shape, fn):
    return pl.BlockSpec(shape, fn)


def _rms_stats(xf):
    return lax.rsqrt(jnp.mean(xf * xf, axis=-1, keepdims=True) + EPS)


def _rms(xf, g):
    return xf * _rms_stats(xf) * g


def _rms_bwd(xf, g, dy):
    r = _rms_stats(xf)
    xh = xf * r
    dyg = dy * g
    dx = r * (dyg - xh * jnp.mean(dyg * xh, axis=-1, keepdims=True))
    return dx, jnp.sum(dy * xh, axis=0, keepdims=True)


def _sig(z):
    return 1.0 / (1.0 + jnp.exp(-z))


def _ident(a):
    return a


def _to_bf16(a):
    return a.astype(BF16)


_DIMS = {"nn": (((1,), (0,)), ((), ())), "nt": (((1,), (1,)), ((), ())), "tn": (((0,), (0,)), ((), ()))}


def _mm(name, mode, grid, a_ins, a_fn, b_ins, b_fn, outs, acc_shape, epi_ins=(), epi_fn=None,
        a_cache=None, a_outs=(), epi_a=()):
    nk = grid[2]
    na, nb, ne, no, nao = len(a_ins), len(b_ins), len(epi_ins), len(outs), len(a_outs)
    assert a_cache is None or nk == 1
    assert not a_outs or a_cache is not None
    dims = _DIMS[mode]
    if epi_fn is None:
        epi_fn = lambda acc: (acc,)

    def body(*refs):
        a_refs = refs[:na]
        b_refs = refs[na:na + nb]
        e_refs = refs[na + nb:na + nb + ne]
        o_refs = refs[na + nb + ne:na + nb + ne + no]
        ao_refs = refs[na + nb + ne + no:na + nb + ne + no + nao]
        scratch = list(refs[na + nb + ne + no + nao:])
        acc_ref = scratch.pop(0) if nk > 1 else None
        a_sc = scratch.pop(0) if a_cache is not None else None
        j = pl.program_id(1)
        k = pl.program_id(2)

        def finish(acc):
            res = epi_fn(acc, *[a_refs[t][...] for t in epi_a], *[r[...] for r in e_refs])
            for r, val in zip(o_refs, res):
                r[...] = val.astype(r.dtype)

        if a_sc is not None:
            @pl.when(j == 0)
            def _():
                res = a_fn(*[r[...] for r in a_refs])
                if nao:
                    for r, val in zip(ao_refs, res[1:]):
                        r[...] = val.astype(r.dtype)
                    res = res[0]
                a_sc[...] = res
            a = a_sc[...]
        else:
            a = a_fn(*[r[...] for r in a_refs])
        b = b_fn(*[r[...] for r in b_refs])
        prod = lax.dot_general(a, b, dims, preferred_element_type=F32)
        if nk == 1:
            finish(prod)
        else:
            @pl.when(k == 0)
            def _():
                acc_ref[...] = prod

            @pl.when(k > 0)
            def _():
                acc_ref[...] += prod

            @pl.when(k == nk - 1)
            def _():
                finish(acc_ref[...])

    scratch_shapes = []
    if nk > 1:
        scratch_shapes.append(pltpu.VMEM(acc_shape, F32))
    if a_cache is not None:
        scratch_shapes.append(pltpu.VMEM(*a_cache))
    all_outs = list(outs) + list(a_outs)
    res = pl.pallas_call(
        body, name=name, grid=grid,
        in_specs=[s for _, s in a_ins] + [s for _, s in b_ins] + [s for _, s in epi_ins],
        out_specs=[s for _, s in all_outs],
        out_shape=[o for o, _ in all_outs],
        scratch_shapes=scratch_shapes,
        compiler_params=_cparams(3),
    )(*[a for a, _ in a_ins], *[a for a, _ in b_ins], *[a for a, _ in epi_ins])
    return res


def _sds(shape, dtype):
    return jax.ShapeDtypeStruct(shape, dtype)


def _nt(a, b):
    return lax.dot_general(a, b, _DIMS["nt"], preferred_element_type=F32)


def _tn(a, b):
    return lax.dot_general(a, b, _DIMS["tn"], preferred_element_type=F32)


def _nn(a, b):
    return lax.dot_general(a, b, _DIMS["nn"], preferred_element_type=F32)


def _mlp_down_ple_head(up, x1, p, tgt, g_ple, g_post_mlp, w_down, w_pg, w_pp, seq, tr):
    nblk = seq // tr
    D = D_MODEL

    def body(up_ref, x1_ref, p_ref, t_ref, gp_ref, gm_ref, wd_ref, wpg_ref, wpp_ref,
             dx2_ref, df_ref, dpre_ref, h3_ref, dpp_ref, loss_ref, dgp_ref, dgm_ref):
        gp, gm, wpg = gp_ref[...], gm_ref[...], wpg_ref[...]
        hidden = jnp.maximum(up_ref[...].astype(F32), 0.0)
        fb = _nn((hidden * hidden).astype(BF16), wd_ref[...])
        x2b = x1_ref[...] + _rms(fb, gm)
        h3 = _rms(x2b, gp).astype(BF16)
        h3_ref[...] = h3
        gate = _sig(_nn(h3, wpg))
        pp = _nn(p_ref[...].astype(BF16), wpp_ref[...])
        err = x2b + gate * pp - t_ref[...]
        loss_ref[...] = jnp.sum(err * err, axis=0, keepdims=True) * (0.5 / D)
        dx3 = err * (1.0 / D)
        dpp_ref[...] = (dx3 * gate).astype(BF16)
        dpre = (dx3 * pp * gate * (1.0 - gate)).astype(BF16)
        dpre_ref[...] = dpre
        dxn, dgp = _rms_bwd(x2b, gp, _nt(dpre, wpg))
        dx2 = dx3 + dxn
        dx2_ref[...] = dx2
        dgp_ref[...] = dgp
        dfb, dgm = _rms_bwd(fb, gm, dx2)
        df_ref[...] = dfb.astype(BF16)
        dgm_ref[...] = dgm

    rows = _bs((tr, D), lambda i: (i, 0))
    vec = _bs((1, D), lambda i: (0, 0))
    part = _bs((None, 1, D), lambda i: (i, 0, 0))
    return pl.pallas_call(
        body, name="mlp_down_ple_head", grid=(nblk,),
        in_specs=[_bs((tr, D_FF), lambda i: (i, 0)), rows, _bs((tr, PLE_DIM), lambda i: (i, 0)), rows, vec, vec,
                  _bs((D_FF, D), lambda i: (0, 0)), _bs((D, D), lambda i: (0, 0)), _bs((PLE_DIM, D), lambda i: (0, 0))],
        out_specs=[rows] * 5 + [part] * 3,
        out_shape=[_sds((seq, D), F32)] + [_sds((seq, D), BF16)] * 4 + [_sds((nblk, 1, D), F32)] * 3,
        compiler_params=_cparams(1),
    )(up, x1, p, tgt, g_ple, g_post_mlp, w_down, w_pg, w_pp)


def _shift_rows_down(u, prev, n):
    rows = u.shape[0]
    ridx = lax.broadcasted_iota(jnp.int32, u.shape, 0)
    out = pltpu.roll(u, n, 0)
    for r in range(n):
        out = jnp.where(ridx == r, prev[8 - n + r:8 - n + r + 1, :], out)
    del rows
    return out


def _shift_rows_up(u, nxt, n):
    rows = u.shape[0]
    ridx = lax.broadcasted_iota(jnp.int32, u.shape, 0)
    out = pltpu.roll(u, rows - n, 0)
    for r in range(n):
        out = jnp.where(ridx == rows - n + r, nxt[r:r + 1, :], out)
    return out


CONV_COL0 = 3


def _conv_fwd(proj, w_conv, seq, tr):
    hb = tr // 8

    def body(cb_ref, cc_ref, cu_ref, ccp_ref, cup_ref, w_ref, e_ref):
        i = pl.program_id(0)
        u = cc_ref[...] * cu_ref[...]
        up = jnp.where(i > 0, ccp_ref[...] * cup_ref[...], 0.0)
        w = w_ref[...]
        d = w[0:1, :] * _shift_rows_down(u, up, 2) + w[1:2, :] * _shift_rows_down(u, up, 1) + w[2:3, :] * u
        e_ref[...] = (cb_ref[...] * d).astype(BF16)

    prev = lambda c: (lambda i: (jnp.maximum(i * hb - 1, 0), c))
    return pl.pallas_call(
        body, name="conv_fwd", grid=(seq // tr,),
        in_specs=[_bs((tr, CONV_W), lambda i: (i, CONV_COL0)),
                  _bs((tr, CONV_W), lambda i: (i, CONV_COL0 + 1)),
                  _bs((tr, CONV_W), lambda i: (i, CONV_COL0 + 2)),
                  _bs((8, CONV_W), prev(CONV_COL0 + 1)),
                  _bs((8, CONV_W), prev(CONV_COL0 + 2)),
                  _bs((3, CONV_W), lambda i: (0, 0))],
        out_specs=_bs((tr, CONV_W), lambda i: (i, 0)),
        out_shape=_sds((seq, CONV_W), BF16),
        compiler_params=_cparams(1),
    )(proj, proj, proj, proj, proj, w_conv)


def _conv_bwd(proj, de, w_conv, seq, tr):
    hb = tr // 8
    nblk = seq // tr

    def body(cb_ref, cc_ref, cu_ref, ccp_ref, cup_ref, cbn_ref, de_ref, den_ref, w_ref, o_ref, dw_ref):
        i = pl.program_id(0)
        cc, cu, cb = cc_ref[...], cu_ref[...], cb_ref[...]
        u = cc * cu
        up = jnp.where(i > 0, ccp_ref[...] * cup_ref[...], 0.0)
        u1 = _shift_rows_down(u, up, 1)
        u2 = _shift_rows_down(u, up, 2)
        de_ = de_ref[...]
        dd = de_ * cb
        ddn = jnp.where(i < nblk - 1, den_ref[...] * cbn_ref[...], 0.0)
        w = w_ref[...]
        du = w[2:3, :] * dd + w[1:2, :] * _shift_rows_up(dd, ddn, 1) + w[0:1, :] * _shift_rows_up(dd, ddn, 2)
        o_ref[:, 0:CONV_W] = (de_ * (w[0:1, :] * u2 + w[1:2, :] * u1 + w[2:3, :] * u)).astype(BF16)
        o_ref[:, CONV_W:2 * CONV_W] = (du * cu).astype(BF16)
        o_ref[:, 2 * CONV_W:3 * CONV_W] = (du * cc).astype(BF16)
        ridx = lax.broadcasted_iota(jnp.int32, (8, CONV_W), 0)
        dw0 = jnp.sum(dd * u2, axis=0, keepdims=True)
        dw1 = jnp.sum(dd * u1, axis=0, keepdims=True)
        dw2 = jnp.sum(dd * u, axis=0, keepdims=True)
        dw_ref[...] = jnp.where(ridx == 0, dw0, jnp.where(ridx == 1, dw1, jnp.where(ridx == 2, dw2, 0.0)))

    prev = lambda c: (lambda i: (jnp.maximum(i * hb - 1, 0), c))
    nxt = lambda c: (lambda i: (jnp.minimum((i + 1) * hb, seq // 8 - 1), c))
    return pl.pallas_call(
        body, name="conv_bwd", grid=(nblk,),
        in_specs=[_bs((tr, CONV_W), lambda i: (i, CONV_COL0)),
                  _bs((tr, CONV_W), lambda i: (i, CONV_COL0 + 1)),
                  _bs((tr, CONV_W), lambda i: (i, CONV_COL0 + 2)),
                  _bs((8, CONV_W), prev(CONV_COL0 + 1)),
                  _bs((8, CONV_W), prev(CONV_COL0 + 2)),
                  _bs((8, CONV_W), nxt(CONV_COL0)),
                  _bs((tr, CONV_W), lambda i: (i, 0)),
                  _bs((8, CONV_W), nxt(0)),
                  _bs((3, CONV_W), lambda i: (0, 0))],
        out_specs=[_bs((tr, 3 * CONV_W), lambda i: (i, 0)), _bs((None, 8, CONV_W), lambda i: (i, 0, 0))],
        out_shape=[_sds((seq, 3 * CONV_W), BF16), _sds((nblk, 8, CONV_W), F32)],
        compiler_params=_cparams(1),
    )(proj, proj, proj, proj, proj, proj, de, de, w_conv)


def _log_gates(z):
    lse = jnp.log(1.0 + jnp.exp(-jnp.abs(z)))
    log_beta = jnp.minimum(z, 0.0) - lse
    return log_beta, log_beta - z


DEAD_LOG_WEIGHT = -110.0
NO_TILE = -1e30


def _first_live_tile(start, scores, live_sc):
    def alive():
        return jnp.max(jnp.maximum(live_sc[0], live_sc[1])) > DEAD_LOG_WEIGHT

    def step(c):
        for h, z in enumerate(scores(c[0])):
            live_sc[h] = live_sc[h] + jnp.sum(_log_gates(z)[1], axis=-1, keepdims=True)
        return c[0] - 1, alive()

    j_end, _ = lax.while_loop(lambda c: jnp.logical_and(c[0] >= 0, c[1]), step, (start, alive()))
    return j_end + 1


def _attn_fwd(proj, seq):
    blk = ATT_BLK
    nq = seq // blk
    npair = N_HEADS // 2

    def body(q_ref, k_ref, v_ref, o_ref, z0_sc, z1_sc, w0_sc, w1_sc, tot_sc, live_sc, acc_sc):
        i = pl.program_id(1)
        is_a = lax.broadcasted_iota(jnp.int32, (1, LANES), 1) < HEAD_DIM
        q2 = (q_ref[...] * Q_SCALE).astype(BF16)
        zero = jnp.zeros_like(q2)
        qs = (jnp.where(is_a, q2, zero), jnp.where(is_a, zero, q2))
        row = lax.broadcasted_iota(jnp.int32, (blk, blk), 0)
        col = lax.broadcasted_iota(jnp.int32, (blk, blk), 1)
        tri = (row > col).astype(BF16)
        causal = col < row

        def tile_of(ref, j):
            return ref[pl.ds(pl.multiple_of(j * blk, blk), blk), :].astype(BF16)

        def scores(j):
            k2 = tile_of(k_ref, j)
            return [_nt(qs[h], k2) for h in range(2)]

        has_left = i > 0
        left = jnp.maximum(i - 1, 0)
        g_d = [_log_gates(z) for z in scores(i)]
        g_l = [_log_gates(z) for z in scores(left)]
        keep_d = [jnp.where(causal, g[1], 0.0) for g in g_d]
        suf_d = [_nn(lk.astype(BF16), tri) for lk in keep_d]
        suf_l = [_nn(g[1].astype(BF16), tri) for g in g_l]
        v_d, v_l = tile_of(v_ref, i), tile_of(v_ref, left)
        pv = []
        for h in range(2):
            sum_d = jnp.sum(keep_d[h], axis=-1, keepdims=True)
            w_d = jnp.where(causal, jnp.exp(g_d[h][0] + suf_d[h]), 0.0)
            w_l = jnp.exp(g_l[h][0] + (jnp.where(has_left, sum_d, NO_TILE) + suf_l[h]))
            pv.append(_nn(w_d.astype(BF16), v_d) + _nn(w_l.astype(BF16), v_l))
            tot_sc[h] = sum_d + jnp.sum(g_l[h][1], axis=-1, keepdims=True)
        acc_sc[...] = jnp.where(is_a, pv[0], pv[1])

        live_sc[...] = tot_sc[...]
        first = _first_live_tile(i - 2, scores, live_sc)
        trips = i - 1 - first
        z_bufs, w_bufs = (z0_sc, z1_sc), (w0_sc, w1_sc)

        def put(ref, vals):
            for h in range(2):
                ref[h] = vals[h]

        def weights(zs):
            gates = [_log_gates(z) for z in zs]
            sums = [_nn(g[1].astype(BF16), tri) for g in gates]
            ws = []
            for h in range(2):
                ws.append(jnp.exp(gates[h][0] + (tot_sc[h] + sums[h])).astype(BF16))
                tot_sc[h] = tot_sc[h] + jnp.sum(gates[h][1], axis=-1, keepdims=True)
            return ws

        def add_values(w_buf, j):
            v2 = tile_of(v_ref, j)
            acc_sc[...] += jnp.where(is_a, _nn(w_buf[0], v2), _nn(w_buf[1], v2))

        def trip(j, s):
            add_values(w_bufs[s], j + 1)
            put(z_bufs[1 - s], scores(jnp.maximum(j - 1, first)))
            put(w_bufs[1 - s], weights((z_bufs[s][0], z_bufs[s][1])))

        @pl.when(trips > 0)
        def _():
            put(z0_sc, scores(i - 2))
            w0_sc[...] = jnp.zeros_like(w0_sc)

            def two_trips(pp, carry):
                j = i - 2 - 2 * pp
                trip(j, 0)
                trip(j - 1, 1)
                return carry

            lax.fori_loop(0, trips // 2, two_trips, 0)
            odd = trips % 2 == 1

            @pl.when(odd)
            def _():
                trip(first, 0)
                add_values(w1_sc, first)

            @pl.when(jnp.logical_not(odd))
            def _():
                add_values(w0_sc, first)

        o_ref[...] = acc_sc[...].astype(BF16)

    return pl.pallas_call(
        body, name="attn_fwd", grid=(npair, nq),
        in_specs=[_bs((blk, LANES), lambda p, i: (i, p)),
                  _bs((seq, LANES), lambda p, i: (0, npair + p)),
                  _bs((seq, LANES), lambda p, i: (0, 2 * npair + p))],
        out_specs=_bs((blk, LANES), lambda p, i: (i, p)),
        out_shape=_sds((seq, ATTN_W), BF16),
        scratch_shapes=[pltpu.VMEM((2, blk, blk), F32), pltpu.VMEM((2, blk, blk), F32),
                        pltpu.VMEM((2, blk, blk), BF16), pltpu.VMEM((2, blk, blk), BF16),
                        pltpu.VMEM((2, blk, 1), F32), pltpu.VMEM((2, blk, 1), F32), pltpu.VMEM((blk, LANES), F32)],
        compiler_params=_cparams(2),
    )(proj, proj, proj)


def _attn_bwd(proj, do, seq):
    blk = ATT_BLK
    nq = seq // blk
    npair = N_HEADS // 2

    def body(q_ref, k_ref, v_ref, do_ref, dq_ref, dk_ref, dv_ref,
             prod0_sc, prod1_sc, pend0_sc, pend1_sc, tot_sc, live_sc, cum_sc, pre_sc, dq_sc):
        i = pl.program_id(1)

        @pl.when(i == 0)
        def _():
            dk_ref[...] = jnp.zeros_like(dk_ref)
            dv_ref[...] = jnp.zeros_like(dv_ref)

        is_a = lax.broadcasted_iota(jnp.int32, (1, LANES), 1) < HEAD_DIM
        q2 = (q_ref[...] * Q_SCALE).astype(BF16)
        do2 = do_ref[...]
        zero = jnp.zeros_like(q2)
        qs = (jnp.where(is_a, q2, zero), jnp.where(is_a, zero, q2))
        dos = (jnp.where(is_a, do2, zero), jnp.where(is_a, zero, do2))
        row = lax.broadcasted_iota(jnp.int32, (blk, blk), 0)
        col = lax.broadcasted_iota(jnp.int32, (blk, blk), 1)
        tri_after = (row > col).astype(BF16)
        tri_excl = (row < col).astype(BF16)
        causal = col < row

        def tile_of(ref, j):
            return ref[pl.ds(pl.multiple_of(j * blk, blk), blk), :].astype(BF16)

        def scores(j):
            k2 = tile_of(k_ref, j)
            return [_nt(qs[h], k2) for h in range(2)]

        def products(j):
            v2 = tile_of(v_ref, j)
            return scores(j) + [_nt(dos[h], v2) for h in range(2)]

        def row_sum(a):
            return jnp.sum(a, axis=-1, keepdims=True)

        def grad_matmuls(ws, dzs, j):
            rows = pl.ds(pl.multiple_of(j * blk, blk), blk)
            k2 = tile_of(k_ref, j)
            dq_sc[...] += jnp.where(is_a, _nn(dzs[0], k2), _nn(dzs[1], k2))
            dk_ref[rows, :] += jnp.where(is_a, _tn(dzs[0], q2), _tn(dzs[1], q2))
            if ws is not None:
                dv_ref[rows, :] += jnp.where(is_a, _tn(ws[0], do2), _tn(ws[1], do2))

        has_left = i > 0
        left = jnp.maximum(i - 1, 0)
        p_d, p_l = products(i), products(left)
        g_d = [_log_gates(z) for z in p_d[:2]]
        g_l = [_log_gates(z) for z in p_l[:2]]
        keep_d = [jnp.where(causal, g[1], 0.0) for g in g_d]
        suf_d = [_nn(lk.astype(BF16), tri_after) for lk in keep_d]
        suf_l = [_nn(g[1].astype(BF16), tri_after) for g in g_l]
        w_d, w_l, gg_d, gg_l = [], [], [], []
        for h in range(2):
            sum_d = row_sum(keep_d[h])
            w_d.append(jnp.where(causal, jnp.exp(g_d[h][0] + suf_d[h]), 0.0))
            w_l.append(jnp.exp(g_l[h][0] + (jnp.where(has_left, sum_d, NO_TILE) + suf_l[h])))
            gg_d.append(p_d[2 + h] * w_d[h])
            gg_l.append(p_l[2 + h] * w_l[h])
            tot_sc[h] = sum_d + row_sum(g_l[h][1])
        before_d = [_nn(g.astype(BF16), tri_excl) for g in gg_d]
        before_l = [_nn(g.astype(BF16), tri_excl) for g in gg_l]
        dz_d, dz_l = [], []
        for h in range(2):
            beta_d, beta_l = jnp.exp(g_d[h][0]), jnp.exp(g_l[h][0])
            dz_l.append((gg_l[h] * (1.0 - beta_l) - before_l[h] * beta_l).astype(BF16))
            dz = gg_d[h] * (1.0 - beta_d) - (row_sum(gg_l[h]) + before_d[h]) * beta_d
            dz_d.append(jnp.where(causal, dz, 0.0).astype(BF16))
        dq_sc[...] = jnp.zeros_like(dq_sc)
        grad_matmuls([w.astype(BF16) for w in w_l], dz_l, left)
        grad_matmuls([w.astype(BF16) for w in w_d], dz_d, i)

        live_sc[...] = tot_sc[...]
        first = _first_live_tile(i - 2, scores, live_sc)
        trips = i - 1 - first
        prod_bufs, pend_bufs = (prod0_sc, prod1_sc), (pend0_sc, pend1_sc)

        def local_grads(prods):
            zs, dws = prods[:2], prods[2:]
            gates = [_log_gates(z) for z in zs]
            sums = [_nn(g[1].astype(BF16), tri_after) for g in gates]
            ws, gs = [], []
            for h in range(2):
                cum = cum_sc[h] + row_sum(gates[h][1])
                cum_sc[h] = cum
                ws.append(jnp.exp(gates[h][0] + ((live_sc[h] - cum) + sums[h])))
                gs.append(dws[h] * ws[h])
            befores = [_nn(g.astype(BF16), tri_excl) for g in gs]
            dzs = []
            for h in range(2):
                beta = jnp.exp(gates[h][0])
                dzs.append((gs[h] * (1.0 - beta) - (pre_sc[h] + befores[h]) * beta).astype(BF16))
                pre_sc[h] = pre_sc[h] + row_sum(gs[h])
            return [w.astype(BF16) for w in ws] + dzs

        def put(ref, vals):
            for n, val in enumerate(vals):
                ref[n] = val

        def flush(pend, j):
            grad_matmuls([pend[0], pend[1]], [pend[2], pend[3]], j)

        def trip(j, s):
            flush(pend_bufs[s], jnp.maximum(j - 1, first))
            put(prod_bufs[1 - s], products(j + 1))
            put(pend_bufs[1 - s], local_grads([prod_bufs[s][n] for n in range(4)]))

        def earlier_keys_share(j, mask):
            dzs = []
            for h, z in enumerate(scores(j)):
                beta = jnp.exp(_log_gates(z)[0])
                dzs.append(jnp.where(mask, -pre_sc[h] * beta, 0.0).astype(BF16))
            grad_matmuls(None, dzs, j)

        @pl.when(trips > 0)
        def _():
            cum_sc[...] = jnp.zeros_like(cum_sc)
            pre_sc[...] = jnp.zeros_like(pre_sc)
            pend0_sc[...] = jnp.zeros_like(pend0_sc)
            put(prod0_sc, products(first))

            def two_trips(pp, carry):
                trip(first + 2 * pp, 0)
                trip(first + 2 * pp + 1, 1)
                return carry

            lax.fori_loop(0, trips // 2, two_trips, 0)
            odd = trips % 2 == 1

            @pl.when(odd)
            def _():
                trip(i - 2, 0)
                flush(pend1_sc, i - 2)

            @pl.when(jnp.logical_not(odd))
            def _():
                flush(pend0_sc, i - 2)

            earlier_keys_share(i - 1, True)
            earlier_keys_share(i, causal)

        dq_ref[...] = dq_sc[...] * Q_SCALE

    qmap = lambda p, i: (i, p)
    return pl.pallas_call(
        body, name="attn_bwd", grid=(npair, nq),
        in_specs=[_bs((blk, LANES), qmap),
                  _bs((seq, LANES), lambda p, i: (0, npair + p)),
                  _bs((seq, LANES), lambda p, i: (0, 2 * npair + p)),
                  _bs((blk, LANES), qmap)],
        out_specs=[_bs((blk, LANES), qmap),
                   _bs((seq, LANES), lambda p, i: (0, p)),
                   _bs((seq, LANES), lambda p, i: (0, p))],
        out_shape=[_sds((seq, ATTN_W), F32)] * 3,
        scratch_shapes=[pltpu.VMEM((4, blk, blk), F32), pltpu.VMEM((4, blk, blk), F32),
                        pltpu.VMEM((4, blk, blk), BF16), pltpu.VMEM((4, blk, blk), BF16),
                        pltpu.VMEM((2, blk, 1), F32), pltpu.VMEM((2, blk, 1), F32), pltpu.VMEM((2, blk, 1), F32),
                        pltpu.VMEM((2, blk, 1), F32), pltpu.VMEM((blk, LANES), F32)],
        compiler_params=_cparams(2),
    )(proj, proj, proj, do)


def _elementwise(name, fn, ins, out_dtypes):
    rows, cols = ins[0].shape
    tr = rows
    for cand in (512, 256, 128, 64, 32, 16, 8):
        if rows % cand == 0 and cand * cols * 4 <= 2 * 1024 * 1024:
            tr = cand
            break
    n_in = len(ins)

    def body(*refs):
        res = fn(*[r[...] for r in refs[:n_in]])
        for r, val in zip(refs[n_in:], res):
            r[...] = val.astype(r.dtype)

    spec = _bs((tr, cols), lambda i: (i, 0))
    return pl.pallas_call(
        body, name=name, grid=(rows // tr,),
        in_specs=[spec] * n_in, out_specs=[spec] * len(out_dtypes),
        out_shape=[_sds((rows, cols), dt) for dt in out_dtypes],
        compiler_params=_cparams(1),
    )(*ins)


def _adamw_fn(w, g, m, v):
    m = ADAM_B1 * m + (1.0 - ADAM_B1) * g
    v = ADAM_B2 * v + (1.0 - ADAM_B2) * (g * g)
    m_hat = m / (1.0 - ADAM_B1 ** ADAM_STEP)
    v_hat = v / (1.0 - ADAM_B2 ** ADAM_STEP)
    delta = -ADAM_LR * (m_hat / (jnp.sqrt(v_hat) + ADAM_EPS) + ADAM_WD * w)
    return delta, m, v


def _adamw(name, w, g, m, v):
    shape = w.shape
    as2d = lambda a: a.reshape(-1, shape[-1])
    delta, nm, nv = _elementwise(name, _adamw_fn, [as2d(w), as2d(g), as2d(m), as2d(v)], [F32, F32, F32])
    return delta.reshape(shape), nm.reshape(shape), nv.reshape(shape)


def _place():
    x, y, c = lax.axis_index("x"), lax.axis_index("y"), lax.axis_index("c")
    chips = [(1 - x, y), (x, 1 - y), (1 - x, 1 - y)]
    return x, y, c, chips


ANY = pl.BlockSpec(memory_space=pl.ANY)
VMEM_WHOLE = pl.BlockSpec(memory_space=pltpu.VMEM)


def _allgather_weights(shards):
    n = len(shards)

    def body(*refs):
        src, dst = refs[:n], refs[n:2 * n]
        send_sems, recv_sems, local_sems = refs[2 * n:]
        x, y, c, chips = _place()
        me, sibling, mychip = (x, y, c), (x, y, 1 - c), 2 * x + y

        x_nbr, y_nbr, diag = 2 * (1 - x) + y, 2 * x + (1 - y), 2 * (1 - x) + (1 - y)
        to_x, to_y = (1 - x, y, c), (x, 1 - y, c)

        def parts(w):
            hr = src[w].shape[0] // 2
            first = hr // 2 if hr % 32 == 0 else hr
            return first, hr - first

        def rows_of(w, chip, half, route):
            hr = src[w].shape[0] // 2
            first, second = parts(w)
            start, size = {0: (0, hr), 1: (0, hr), 2: (0, first), 3: (first, second)}[route]
            return dst[w].at[chip, pl.ds(half * hr + start, size)]

        def copy(w, k, src_ref, dst_ref, to):
            return pltpu.make_async_remote_copy(src_ref=src_ref, dst_ref=dst_ref, send_sem=send_sems.at[w, k],
                                                recv_sem=recv_sems.at[w, k], device_id=to, device_id_type=MESH)

        def landed(w, route):
            chip = {0: x_nbr, 1: y_nbr, 2: diag, 3: diag}[route]
            return rows_of(w, chip, c, route), chip

        def routes(w):
            return (0, 1, 2, 3) if parts(w)[1] else (0, 1, 2)

        started, local = [], []
        for w in range(n):
            hr = src[w].shape[0] // 2
            own = pltpu.make_async_copy(src[w], dst[w].at[mychip], local_sems.at[w])
            own.start()
            local.append(own)
            mine = src[w].at[pl.ds(c * hr, hr)]
            for route, to in ((0, to_x), (1, to_y)):
                cp = copy(w, route, mine, rows_of(w, mychip, c, route), to)
                cp.start()
                started.append(cp)

        def pass_on(w, route):
            got, chip = landed(w, route)
            copy(w, route, got, got, me).wait_recv()
            if route == 1:
                part = rows_of(w, chip, c, 2)
                started.append(copy(w, 2, part, part, to_x))
                started[-1].start()
            if route == 0 and parts(w)[1]:
                part = rows_of(w, chip, c, 3)
                started.append(copy(w, 3, part, part, to_y))
                started[-1].start()
            started.append(copy(w, 4 + route, got, got, sibling))
            started[-1].start()

        for w in range(n):
            pass_on(w, 1)
            pass_on(w, 0)
        for w in range(n):
            for route in routes(w)[2:]:
                pass_on(w, route)
        for w in range(n):
            for route in routes(w):
                chip = landed(w, route)[1]
                from_sib = rows_of(w, chip, 1 - c, route)
                copy(w, 4 + route, from_sib, from_sib, me).wait_recv()
        for cp in local:
            cp.wait()
        for cp in started:
            cp.wait_send()

    return pl.pallas_call(
        body, name="allgather_weights",
        in_specs=[VMEM_WHOLE] * n, out_specs=[VMEM_WHOLE] * n,
        out_shape=[_sds((N_CHIPS,) + s.shape, s.dtype) for s in shards],
        scratch_shapes=[pltpu.SemaphoreType.DMA((n, 8)), pltpu.SemaphoreType.DMA((n, 8)),
                        pltpu.SemaphoreType.DMA((n,))],
        compiler_params=pltpu.CompilerParams(vmem_limit_bytes=VMEM_LIMIT),
    )(*shards)


SUM_ROWS = 64


def _rs_pair_sum(name, grads):
    n = len(grads)

    def body(*refs):
        g, out = refs[:n], refs[n:2 * n]
        stage, land, keep = refs[2 * n:3 * n], refs[3 * n:4 * n], refs[4 * n:5 * n]
        send_sems, recv_sems, stage_sems, keep_sems = refs[5 * n:]
        x, y, c, _ = _place()
        sibling = (x, y, 1 - c)
        loads = []
        for w in range(n):
            hr = g[w].shape[1] // 2
            st = pltpu.make_async_copy(g[w].at[:, pl.ds((1 - c) * hr, hr)], stage[w], stage_sems.at[w])
            kp = pltpu.make_async_copy(g[w].at[:, pl.ds(c * hr, hr)], keep[w], keep_sems.at[w])
            st.start()
            kp.start()
            loads.append((st, kp))
        gives = []
        for w in range(n):
            loads[w][0].wait()
            give = pltpu.make_async_remote_copy(src_ref=stage[w], dst_ref=land[w], send_sem=send_sems.at[w],
                                                recv_sem=recv_sems.at[w], device_id=sibling, device_id_type=MESH)
            give.start()
            gives.append(give)
        for w in range(n):
            loads[w][1].wait()
            gives[w].wait_recv()
            nb = g[w].shape[1] // 2 // SUM_ROWS

            def add(idx, carry, w=w, nb=nb):
                k, r = idx // nb, pl.multiple_of((idx % nb) * SUM_ROWS, SUM_ROWS)
                rows = pl.ds(r, SUM_ROWS)
                out[w][k, rows, :] = (keep[w][k, rows, :] + land[w][k, rows, :]).astype(BF16)
                return carry

            lax.fori_loop(0, N_CHIPS * nb, add, 0)
        for give in gives:
            give.wait_send()

    half = [(N_CHIPS, a.shape[1] // 2, a.shape[2]) for a in grads]
    bufs = [pltpu.VMEM(s, F32) for s in half]
    sems = pltpu.SemaphoreType.DMA((n,))
    return pl.pallas_call(
        body, name=name,
        in_specs=[ANY] * n, out_specs=[VMEM_WHOLE] * n, out_shape=[_sds(s, BF16) for s in half],
        scratch_shapes=bufs + bufs + bufs + [sems, sems, sems, sems],
        compiler_params=pltpu.CompilerParams(vmem_limit_bytes=VMEM_LIMIT),
    )(*grads)


def _rs_exchange_join(parts):
    n = len(parts)

    def body(*refs):
        t, full = refs[:n], refs[n:2 * n]
        got_x, got_y, pass_on, got_2 = (refs[m * n:(m + 1) * n] for m in range(2, 6))
        send_sems, recv_sems = refs[6 * n:]
        x, y, c, _ = _place()
        mychip, sibling = 2 * x + y, (x, y, 1 - c)
        x_nbr, y_nbr, diag = 2 * (1 - x) + y, 2 * x + (1 - y), 2 * (1 - x) + (1 - y)
        to_x, to_y = (1 - x, y, c), (x, 1 - y, c)
        sends = []

        def copy(w, k, src_ref, dst_ref, to):
            return pltpu.make_async_remote_copy(src_ref=src_ref, dst_ref=dst_ref, send_sem=send_sems.at[w, k],
                                                recv_sem=recv_sems.at[w, k], device_id=to, device_id_type=MESH)

        def start(cp):
            cp.start()
            sends.append(cp)

        def add_rows(w, count, fn):
            def step(idx, carry):
                fn(pl.ds(pl.multiple_of(idx * SUM_ROWS, SUM_ROWS), SUM_ROWS), pl.multiple_of(idx * SUM_ROWS, SUM_ROWS))
                return carry
            lax.fori_loop(0, count // SUM_ROWS, step, 0)

        f32 = lambda v: v.astype(F32)
        for w in range(n):
            ha = t[w].shape[1] // 2
            part_a, part_b = pl.ds(0, ha), pl.ds(ha, ha)
            start(copy(w, 0, t[w].at[x_nbr, part_a], got_x[w].at[0], to_x))
            start(copy(w, 1, t[w].at[diag, part_a], got_x[w].at[1], to_x))
            start(copy(w, 2, t[w].at[y_nbr, part_b], got_y[w].at[0], to_y))
            start(copy(w, 3, t[w].at[diag, part_b], got_y[w].at[1], to_y))
        for w in range(n):
            hr = t[w].shape[1]
            ha = hr // 2
            for k in (0, 1):
                copy(w, k, got_x[w].at[k], got_x[w].at[k], to_x).wait_recv()

            def sum_a(rows, r, w=w, hr=hr):
                full[w][pl.ds(pl.multiple_of(c * hr + r, SUM_ROWS), SUM_ROWS), :] = \
                    f32(t[w][mychip, rows, :]) + f32(got_x[w][0, rows, :])
                pass_on[w][rows, :] = (f32(t[w][y_nbr, rows, :]) + f32(got_x[w][1, rows, :])).astype(BF16)

            add_rows(w, ha, sum_a)
            start(copy(w, 4, pass_on[w].at[pl.ds(0, ha)], got_2[w].at[pl.ds(0, ha)], to_y))
            for k in (2, 3):
                copy(w, k, got_y[w].at[k - 2], got_y[w].at[k - 2], to_y).wait_recv()

            def sum_b(rows, r, w=w, hr=hr, ha=ha):
                lower = pl.ds(pl.multiple_of(ha + r, SUM_ROWS), SUM_ROWS)
                full[w][pl.ds(pl.multiple_of(c * hr + ha + r, SUM_ROWS), SUM_ROWS), :] = \
                    f32(t[w][mychip, lower, :]) + f32(got_y[w][0, rows, :])
                pass_on[w][lower, :] = (f32(t[w][x_nbr, lower, :]) + f32(got_y[w][1, rows, :])).astype(BF16)

            add_rows(w, ha, sum_b)
            start(copy(w, 5, pass_on[w].at[pl.ds(ha, ha)], got_2[w].at[pl.ds(ha, ha)], to_x))
        for w in range(n):
            hr = t[w].shape[1]
            ha = hr // 2
            copy(w, 4, got_2[w].at[pl.ds(0, ha)], got_2[w].at[pl.ds(0, ha)], to_y).wait_recv()
            copy(w, 5, got_2[w].at[pl.ds(ha, ha)], got_2[w].at[pl.ds(ha, ha)], to_x).wait_recv()

            def finish(rows, r, w=w, hr=hr):
                out_rows = pl.ds(pl.multiple_of(c * hr + r, SUM_ROWS), SUM_ROWS)
                full[w][out_rows, :] = full[w][out_rows, :] + f32(got_2[w][rows, :])

            add_rows(w, hr, finish)
            mine = full[w].at[pl.ds(c * hr, hr)]
            start(copy(w, 6, mine, mine, sibling))
        for w in range(n):
            hr = t[w].shape[1]
            theirs = full[w].at[pl.ds((1 - c) * hr, hr)]
            copy(w, 6, theirs, theirs, sibling).wait_recv()
        for cp in sends:
            cp.wait_send()

    half = lambda a: pltpu.VMEM((2, a.shape[1] // 2, a.shape[2]), a.dtype)
    whole = lambda a: pltpu.VMEM(a.shape[1:], a.dtype)
    return pl.pallas_call(
        body, name="rs_exchange_join",
        in_specs=[VMEM_WHOLE] * n, out_specs=[VMEM_WHOLE] * n,
        out_shape=[_sds((2 * a.shape[1], a.shape[2]), F32) for a in parts],
        scratch_shapes=[half(a) for a in parts] + [half(a) for a in parts] + [whole(a) for a in parts]
        + [whole(a) for a in parts] + [pltpu.SemaphoreType.DMA((n, 7)), pltpu.SemaphoreType.DMA((n, 7))],
        compiler_params=pltpu.CompilerParams(vmem_limit_bytes=VMEM_LIMIT),
    )(*parts)


def _small_allreduce(loss_p, dg_parts, dbg_a, dbg_c, dwc):
    ins = [loss_p] + list(dg_parts) + [dbg_a, dbg_c, dwc]
    n_in = len(ins)
    vmem = pl.BlockSpec(memory_space=pltpu.VMEM)

    def body(*refs):
        in_refs = refs[:n_in]
        out_ref, vec, buf, send_sems, recv_sems = refs[n_in:]
        x, y, c, _ = _place()
        me = 4 * x + 2 * y + c
        vec[...] = jnp.zeros_like(vec)
        vec[0:1, :] = jnp.sum(in_refs[0][...], axis=0)
        for r in range(5):
            vec[1 + r:2 + r, :] = jnp.sum(in_refs[1 + r][...], axis=0)
        vec[6:7, :] = jnp.sum(in_refs[6][...], axis=0)
        vec[7:8, :] = jnp.sum(in_refs[7][...], axis=0)
        vec[8:16, 0:CONV_W] = jnp.sum(in_refs[8][...], axis=0)
        buf[pl.ds(me, 1)] = vec[...][None]
        copies = []
        for r in range(1, 8):
            fx, fy, fc = (r >> 2) & 1, (r >> 1) & 1, r & 1
            to = (1 - x if fx else x, 1 - y if fy else y, 1 - c if fc else c)
            cp = pltpu.make_async_remote_copy(src_ref=vec, dst_ref=buf.at[me], send_sem=send_sems.at[r - 1],
                                              recv_sem=recv_sems.at[r - 1], device_id=to, device_id_type=MESH)
            cp.start()
            copies.append(cp)
        for cp in copies:
            cp.wait()
        total = buf[0]
        for s in range(1, 8):
            total = total + buf[s]
        out_ref[...] = total
        out_ref[0:1, :] = jnp.broadcast_to(jnp.sum(total[0:1, :], axis=-1, keepdims=True), (1, D_MODEL))

    return pl.pallas_call(
        body, name="small_allreduce",
        in_specs=[vmem] * n_in, out_specs=vmem, out_shape=_sds((SMALL_ROWS, D_MODEL), F32),
        scratch_shapes=[pltpu.VMEM((SMALL_ROWS, D_MODEL), F32), pltpu.VMEM((8, SMALL_ROWS, D_MODEL), F32),
                        pltpu.SemaphoreType.DMA((7,)), pltpu.SemaphoreType.DMA((7,))],
    )(*ins)


def _local_step(x, p, tgt, g, b_gate, w_conv, wf):
    seq = x.shape[0]
    tm = min(seq, 1024)
    th = min(seq, 512)
    tl = min(seq, 2048)
    ni, nh, nl = seq // tm, seq // th, seq // tl
    g_pre_mix, g_post_mix, g_pre_mlp, g_post_mlp, g_ple = g
    w_in, w_ao, w_co, w_o, w_up, w_down, w_pg, w_pp, w_in_nat, w_up_nat = wf
    D = D_MODEL
    vec = lambda a, blk=0: (a, _bs((1, D), lambda i, j, k: (0, blk)))
    rows_i = lambda a, t, blk=0: (a, _bs((t, D), lambda i, j, k: (i, blk)))
    rows_k = lambda a, t, blk=0: (a, _bs((t, D), lambda i, j, k: (k, blk)))
    part = lambda n: (_sds((n, 1, D), F32), _bs((None, 1, D), lambda i, j, k: (i, 0, 0)))
    full2 = lambda a: (a, _bs(a.shape, lambda i, j, k: (0, 0)))

    normed = lambda xb, gb: (_rms(xb, gb).astype(BF16),) * 2
    keep_a = lambda t: [(_sds((seq, D), BF16), _bs((t, D), lambda i, j, k: (i, 0)))]
    main_w = D_IN - 2 * D
    proj, gates, h1 = _mm("proj_in", "nn", (nh, 1, 1),
                          a_ins=[rows_i(x, th), vec(g_pre_mix)], a_fn=normed,
                          b_ins=[full2(w_in_nat)], b_fn=_ident,
                          epi_fn=lambda acc: (acc[:, :main_w], acc[:, main_w:]),
                          outs=[(_sds((seq, main_w), F32), _bs((th, main_w), lambda i, j, k: (i, 0))),
                                (_sds((seq, 2 * D), BF16), _bs((th, 2 * D), lambda i, j, k: (i, 0)))],
                          acc_shape=(th, D_IN), a_cache=((th, D), BF16), a_outs=keep_a(th))
    o = _attn_fwd(proj, seq)
    e = _conv_fwd(proj, w_conv, seq, tm)

    def gate_values(ga, gc, ba, bc):
        return _sig(ga.astype(F32) + ba), _sig(gc.astype(F32) + bc)

    def mix_fn(ga, gc, ob, eb, ba, bc, wao, wco):
        sa, sc = gate_values(ga, gc, ba, bc)
        ya, yc = _nn(ob, wao).astype(BF16), _nn(eb, wco).astype(BF16)
        mix = (sa * ya.astype(F32) + sc * yc.astype(F32)).astype(BF16)
        return mix, mix, ya, yc

    def post_mix(acc, xb, gb):
        return acc, xb + _rms(acc, gb)

    half_rows = lambda a: (a, _bs((th, a.shape[1]), lambda i, j, k: (i, 0)))
    mixed, x1, mixin, y_attn, y_conv = _mm(
        "mix_out", "nn", (nh, 1, 1),
        a_ins=[rows_i(gates, th, 0), rows_i(gates, th, 1), half_rows(o), half_rows(e), vec(b_gate, 0), vec(b_gate, 1),
               full2(w_ao), full2(w_co)], a_fn=mix_fn, b_ins=[full2(w_o)], b_fn=_ident,
        epi_ins=[rows_i(x, th), vec(g_post_mix)], epi_fn=post_mix,
        outs=[(_sds((seq, D), BF16), _bs((th, D), lambda i, j, k: (i, 0))),
              (_sds((seq, D), F32), _bs((th, D), lambda i, j, k: (i, 0)))],
        acc_shape=(th, D), a_cache=((th, D), BF16), a_outs=keep_a(th) * 3)
    mix_ins = lambda rows: [rows(gates, th, 0), rows(gates, th, 1), rows(y_attn, th), rows(y_conv, th),
                            vec(b_gate, 0), vec(b_gate, 1)]
    up, h2 = _mm("mlp_up", "nn", (nh, 1, 1),
                 a_ins=[rows_i(x1, th), vec(g_pre_mlp)], a_fn=normed,
                 b_ins=[full2(w_up_nat)], b_fn=_ident,
                 outs=[(_sds((seq, D_FF), BF16), _bs((th, D_FF), lambda i, j, k: (i, 0)))],
                 acc_shape=(th, D_FF), a_cache=((th, D), BF16), a_outs=keep_a(th))

    def relu2(ub):
        r = jnp.maximum(ub.astype(F32), 0.0)
        return (r * r).astype(BF16)

    dx2, df, dpre, h3, dpp, loss_p, dg_ple_p, dg_post_mlp_p = _mlp_down_ple_head(
        up, x1, p, tgt, g_ple, g_post_mlp, w_down, w_pg, w_pp, seq, th)

    (dw_pp,) = _mm("dw_ple_proj", "tn", (1, 1, nh),
                   a_ins=[(p, _bs((th, PLE_DIM), lambda i, j, k: (k, 0)))], a_fn=_to_bf16,
                   b_ins=[rows_k(dpp, th)], b_fn=_ident,
                   outs=[(_sds((PLE_DIM, D), F32), _bs((PLE_DIM, D), lambda i, j, k: (0, 0)))],
                   acc_shape=(PLE_DIM, D))
    (dw_pg,) = _mm("dw_ple_gate", "tn", (1, 1, nl),
                   a_ins=[rows_k(h3, tl)], a_fn=_ident, b_ins=[rows_k(dpre, tl)], b_fn=_ident,
                   outs=[(_sds((D, D), F32), _bs((D, D), lambda i, j, k: (0, 0)))], acc_shape=(D, D))

    def dup_fn(acc, ub):
        return (acc * (2.0 * jnp.maximum(ub.astype(F32), 0.0)),)

    (dup,) = _mm("d_mlp_down", "nt", (nh, 1, 1),
                 a_ins=[rows_i(df, th)], a_fn=_ident, b_ins=[full2(w_down)], b_fn=_ident,
                 epi_ins=[(up, _bs((th, D_FF), lambda i, j, k: (i, 0)))], epi_fn=dup_fn,
                 outs=[(_sds((seq, D_FF), BF16), _bs((th, D_FF), lambda i, j, k: (i, 0)))],
                 acc_shape=(th, D_FF))
    (dw_down,) = _mm("dw_mlp_down", "tn", (4, 1, nl),
                     a_ins=[(up, _bs((tl, D), lambda i, j, k: (k, i)))], a_fn=relu2,
                     b_ins=[rows_k(df, tl)], b_fn=_ident,
                     outs=[(_sds((D_FF, D), F32), _bs((D, D), lambda i, j, k: (i, 0)))], acc_shape=(D, D))
    (dw_up,) = _mm("dw_mlp_up", "tn", (1, 4, nl),
                   a_ins=[rows_k(h2, tl)], a_fn=_ident,
                   b_ins=[(dup, _bs((tl, D), lambda i, j, k: (k, j)))], b_fn=_ident,
                   outs=[(_sds((N_CHIPS, D, D), F32), _bs((None, D, D), lambda i, j, k: (j, 0, 0)))],
                   acc_shape=(D, D))

    def mlp_norm_bwd(acc, x1b, dx2b, mixedb, g_mlp, g_mix):
        dxn, dg_mlp = _rms_bwd(x1b, g_mlp, acc)
        dx1b = dx2b + dxn
        dmixedb, dg_mix = _rms_bwd(mixedb.astype(F32), g_mix, dx1b)
        return dx1b, dmixedb, dg_mlp, dg_mix

    dx1, dmixed, dg_pre_mlp_p, dg_post_mix_p = _mm(
        "d_mlp_up", "nt", (nh, 1, 1),
        a_ins=[(dup, _bs((th, D_FF), lambda i, j, k: (i, 0)))], a_fn=_ident,
        b_ins=[full2(w_up_nat)], b_fn=_ident,
        epi_ins=[rows_i(x1, th), rows_i(dx2, th), rows_i(mixed, th), vec(g_pre_mlp), vec(g_post_mix)],
        epi_fn=mlp_norm_bwd,
        outs=[(_sds((seq, D), F32), _bs((th, D), lambda i, j, k: (i, 0))),
              (_sds((seq, D), BF16), _bs((th, D), lambda i, j, k: (i, 0))), part(nh), part(nh)],
        acc_shape=(th, D))
    (dw_o,) = _mm("dw_mix_out", "tn", (1, 1, nl),
                  a_ins=[rows_k(mixin, tl)], a_fn=_ident, b_ins=[rows_k(dmixed, tl)], b_fn=_ident,
                  outs=[(_sds((D, D), F32), _bs((D, D), lambda i, j, k: (0, 0)))], acc_shape=(D, D))

    def gate_bwd(acc, ga, gc, ya, yc, ba, bc, wao, wco):
        sa, sc = gate_values(ga, gc, ba, bc)
        dga = acc * ya.astype(F32) * sa * (1.0 - sa)
        dgc = acc * yc.astype(F32) * sc * (1.0 - sc)
        dya, dyc = (acc * sa).astype(BF16), (acc * sc).astype(BF16)
        return (dya, dyc, jnp.concatenate([dga, dgc], axis=1), _nt(dya, wao), _nt(dyc, wco),
                jnp.sum(dga, axis=0, keepdims=True), jnp.sum(dgc, axis=0, keepdims=True))

    dya, dyc, dgate, do, de, dbg_a_p, dbg_c_p = _mm(
        "d_mix_out", "nt", (nh, 1, 1),
        a_ins=[rows_i(dmixed, th)], a_fn=_ident, b_ins=[full2(w_o)], b_fn=_ident,
        epi_ins=mix_ins(rows_i) + [full2(w_ao), full2(w_co)], epi_fn=gate_bwd,
        outs=[(_sds((seq, D), BF16), _bs((th, D), lambda i, j, k: (i, 0)))] * 2
             + [(_sds((seq, 2 * D), BF16), _bs((th, 2 * D), lambda i, j, k: (i, 0))),
                (_sds((seq, ATTN_W), BF16), _bs((th, ATTN_W), lambda i, j, k: (i, 0))),
                (_sds((seq, CONV_W), F32), _bs((th, CONV_W), lambda i, j, k: (i, 0))), part(nh), part(nh)],
        acc_shape=(th, D))
    (dw_ao,) = _mm("dw_attn_out", "tn", (1, 1, nh),
                   a_ins=[(o, _bs((th, ATTN_W), lambda i, j, k: (k, 0)))], a_fn=_ident,
                   b_ins=[rows_k(dya, th)], b_fn=_ident,
                   outs=[(_sds((ATTN_W, D), F32), _bs((ATTN_W, D), lambda i, j, k: (0, 0)))], acc_shape=(ATTN_W, D))
    dq, dk, dv = _attn_bwd(proj, do, seq)
    (dw_co,) = _mm("dw_conv_out", "tn", (1, 1, nh),
                   a_ins=[(e, _bs((th, CONV_W), lambda i, j, k: (k, 0)))], a_fn=_ident,
                   b_ins=[rows_k(dyc, th)], b_fn=_ident,
                   outs=[(_sds((CONV_W, D), F32), _bs((CONV_W, D), lambda i, j, k: (0, 0)))], acc_shape=(CONV_W, D))
    dconv, dwc_p = _conv_bwd(proj, de, w_conv, seq, tm)
    qkv_w = 3 * ATTN_W
    join_bf16 = lambda *blocks: jnp.concatenate([b.astype(BF16) for b in blocks], axis=1)
    piece = lambda a, t, rows, blk=0: (a, _bs((t, a.shape[1]), (lambda i, j, k: (k, blk)) if rows == "k"
                                             else (lambda i, j, k: (i, blk))))
    (dw_in_qkv,) = _mm("dw_proj_in_qkv", "tn", (1, 1, ni),
                       a_ins=[rows_k(h1, tm)], a_fn=_ident,
                       b_ins=[piece(dq, tm, "k"), piece(dk, tm, "k"), piece(dv, tm, "k")], b_fn=join_bf16,
                       outs=[(_sds((D, qkv_w), F32), _bs((D, qkv_w), lambda i, j, k: (0, 0)))], acc_shape=(D, qkv_w))
    (dw_in_conv,) = _mm("dw_proj_in_conv", "tn", (1, 1, nl),
                        a_ins=[rows_k(h1, tl)], a_fn=_ident, b_ins=[piece(dconv, tl, "k")], b_fn=_ident,
                        outs=[(_sds((D, 3 * CONV_W), F32), _bs((D, 3 * CONV_W), lambda i, j, k: (0, 0)))],
                        acc_shape=(D, 3 * CONV_W))
    (dw_in_gate,) = _mm("dw_proj_in_gate", "tn", (1, 2, nl),
                        a_ins=[rows_k(h1, tl)], a_fn=_ident,
                        b_ins=[(dgate, _bs((tl, D), lambda i, j, k: (k, j)))], b_fn=_ident,
                        outs=[(_sds((D, 2 * D), F32), _bs((D, D), lambda i, j, k: (0, j)))], acc_shape=(D, D))
    dw_in = jnp.concatenate([dw_in_qkv, dw_in_conv, dw_in_gate], axis=1)

    def in_norm_bwd(acc, xb, dx1b, gb):
        dxn, dg = _rms_bwd(xb, gb, acc)
        return dx1b + dxn, dg

    grad_x, dg_pre_mix_p = _mm("d_proj_in", "nt", (nh, 1, 1),
                               a_ins=[piece(dq, th, "i"), piece(dk, th, "i"), piece(dv, th, "i"),
                                      piece(dconv, th, "i"), piece(dgate, th, "i")], a_fn=join_bf16,
                               b_ins=[full2(w_in_nat)], b_fn=_ident,
                               epi_ins=[rows_i(x, th), rows_i(dx1, th), vec(g_pre_mix)], epi_fn=in_norm_bwd,
                               outs=[(_sds((seq, D), F32), _bs((th, D), lambda i, j, k: (i, 0))), part(nh)],
                               acc_shape=(th, D))

    chip_major = lambda a: a.reshape(a.shape[0], N_CHIPS, a.shape[1] // N_CHIPS).transpose(1, 0, 2)
    big = [chip_major(dw_in), chip_major(dw_ao), chip_major(dw_co), dw_o.reshape(N_CHIPS, D // N_CHIPS, D), dw_up,
           dw_down.reshape(N_CHIPS, D_FF // N_CHIPS, D), dw_pg.reshape(N_CHIPS, D // N_CHIPS, D), chip_major(dw_pp)]
    small = (loss_p, [dg_pre_mix_p, dg_post_mix_p, dg_pre_mlp_p, dg_post_mlp_p, dg_ple_p], dbg_a_p, dbg_c_p, dwc_p)
    return grad_x, big, small


RS_GROUPS = ((0,), (4,), (5,), (1, 2, 3, 6, 7))


def _reduce_scatter(big):
    pair = [None] * len(big)
    for gi, group in enumerate(RS_GROUPS):
        for w, s in zip(group, _rs_pair_sum(f"rs_pair_sum_{gi}", [big[w] for w in group])):
            pair[w] = s
    return _rs_exchange_join(pair)


def kernel(x, p, g_pre_mix, w_in, b_gate, w_conv, w_attn_out, w_conv_out, w_o, g_post_mix, g_pre_mlp, w_up, w_down, g_post_mlp, g_ple, w_ple_gate, w_ple_proj, loss_target, m_g_pre_mix, m_w_in, m_b_gate, m_w_conv, m_w_attn_out, m_w_conv_out, m_w_o, m_g_post_mix, m_g_pre_mlp, m_w_up, m_w_down, m_g_post_mlp, m_g_ple, m_w_ple_gate, m_w_ple_proj, v_g_pre_mix, v_w_in, v_b_gate, v_w_conv, v_w_attn_out, v_w_conv_out, v_w_o, v_g_post_mix, v_g_pre_mlp, v_w_up, v_w_down, v_g_post_mlp, v_g_ple, v_w_ple_gate, v_w_ple_proj):
    mats = [w_in, w_attn_out, w_conv_out, w_o, w_up, w_down, w_ple_gate, w_ple_proj]
    mats_m = [m_w_in, m_w_attn_out, m_w_conv_out, m_w_o, m_w_up, m_w_down, m_w_ple_gate, m_w_ple_proj]
    mats_v = [v_w_in, v_w_attn_out, v_w_conv_out, v_w_o, v_w_up, v_w_down, v_w_ple_gate, v_w_ple_proj]
    gains = [g_pre_mix, g_post_mix, g_pre_mlp, g_post_mlp, g_ple]
    gains_m = [m_g_pre_mix, m_g_post_mix, m_g_pre_mlp, m_g_post_mlp, m_g_ple]
    gains_v = [v_g_pre_mix, v_g_post_mix, v_g_pre_mlp, v_g_post_mlp, v_g_ple]

    taps = jnp.concatenate([w_conv[0], jnp.zeros((CONV_PAD_ROWS - 3, LANES), F32)], axis=0)
    gathered = _allgather_weights([w[0].astype(BF16) for w in mats] + [taps])
    cols_joined = lambda a: a.transpose(1, 0, 2).reshape(a.shape[1], N_CHIPS * a.shape[2])
    rows_joined = lambda a: a.reshape(N_CHIPS * a.shape[1], a.shape[2])
    wf = [gathered[0], cols_joined(gathered[1]), cols_joined(gathered[2]), rows_joined(gathered[3]), gathered[4],
          rows_joined(gathered[5]), rows_joined(gathered[6]), cols_joined(gathered[7]),
          cols_joined(gathered[0]), cols_joined(gathered[4])]
    w_conv_full = cols_joined(gathered[8])[0:3, :]
    chip = 2 * lax.axis_index("x") + lax.axis_index("y")

    grad_x, big, small = _local_step(x[0], p[0, 0], loss_target[0], gains, b_gate, w_conv_full, wf)

    shard_grads = _reduce_scatter(big)
    red = _small_allreduce(*small)
    loss = red[0, 0]
    grad_gains = [red[1 + r:2 + r, :] for r in range(5)]
    grad_b_gate = jnp.concatenate([red[6:7, :], red[7:8, :]], axis=1)
    grad_w_conv = lax.dynamic_slice(red[8:11, :], (0, chip * LANES), (3, LANES))[None]

    grads_big = [gr.reshape(w.shape) for gr, w in zip(shard_grads, mats)]
    upd_big = [_adamw(f"adamw_{i}", w, gr, m, v) for i, (w, gr, m, v) in enumerate(zip(mats, grads_big, mats_m, mats_v))]
    pack = lambda vs, bg: jnp.concatenate(list(vs) + [bg.reshape(2, D_MODEL), jnp.zeros((1, D_MODEL), F32)], axis=0)
    upd_small = _adamw("adamw_small", pack(gains, b_gate), pack(grad_gains, grad_b_gate),
                       pack(gains_m, m_b_gate), pack(gains_v, v_b_gate))
    upd_conv = _adamw("adamw_conv", w_conv, grad_w_conv, m_w_conv, v_w_conv)

    def small_out(a, which):
        gains_out = [a[r:r + 1, :] for r in range(5)]
        return gains_out, a[5:7, :].reshape(1, 2 * D_MODEL)

    def ordered(g_pre_mix_, big_, b_gate_, conv_, g_rest):
        return [g_pre_mix_, big_[0], b_gate_, conv_, big_[1], big_[2], big_[3], g_rest[0], g_rest[1], big_[4], big_[5],
                g_rest[2], g_rest[3], big_[6], big_[7]]

    outs = [loss, grad_x[None]]
    outs += ordered(grad_gains[0], grads_big, grad_b_gate, grad_w_conv, grad_gains[1:])
    for which in range(3):
        g_out, b_out = small_out(upd_small[which], which)
        outs += ordered(g_out[0], [u[which] for u in upd_big], b_out, upd_conv[which], g_out[1:])
    return tuple(outs)
```

```python
import functools

import jax
import jax.numpy as jnp
from jax import lax
from jax.experimental import pallas as pl
from jax.experimental.pallas import tpu as pltpu

F32 = jnp.float32
BF16 = jnp.bfloat16
MESH = pl.DeviceIdType.MESH

D_MODEL = 1024
N_HEADS = 8
HEAD_DIM = 64
ATTN_W = N_HEADS * HEAD_DIM
CONV_W = 512
D_FF = 4096
PLE_DIM = 256
D_IN = 5120
N_CHIPS = 4
EPS = 1e-6
Q_SCALE = HEAD_DIM ** -0.5

ADAM_LR = 0.001
ADAM_B1 = 0.9
ADAM_B2 = 0.999
ADAM_EPS = 1e-08
ADAM_WD = 0.01
ADAM_STEP = 10

V7X_VMEM_BYTES = 64 * 1024 * 1024
VMEM_LIMIT = V7X_VMEM_BYTES - 8 * 1024 * 1024
LANES = 128
ATT_BLK = 256
SMALL_ROWS = 16
CONV_PAD_ROWS = 16


def _cparams(n_grid):
    return pltpu.CompilerParams(dimension_semantics=("arbitrary",) * n_grid, vmem_limit_bytes=VMEM_LIMIT)


def _bs(shape, fn):
    return pl.BlockSpec(shape, fn)


def _rms_stats(xf):
    return lax.rsqrt(jnp.mean(xf * xf, axis=-1, keepdims=True) + EPS)


def _rms(xf, g):
    return xf * _rms_stats(xf) * g


def _rms_bwd(xf, g, dy):
    r = _rms_stats(xf)
    xh = xf * r
    dyg = dy * g
    dx = r * (dyg - xh * jnp.mean(dyg * xh, axis=-1, keepdims=True))
    return dx, jnp.sum(dy * xh, axis=0, keepdims=True)


def _sig(z):
    return 1.0 / (1.0 + jnp.exp(-z))


def _ident(a):
    return a


def _to_bf16(a):
    return a.astype(BF16)


_DIMS = {"nn": (((1,), (0,)), ((), ())), "nt": (((1,), (1,)), ((), ())), "tn": (((0,), (0,)), ((), ()))}


def _mm(name, mode, grid, a_ins, a_fn, b_ins, b_fn, outs, acc_shape, epi_ins=(), epi_fn=None,
        a_cache=None, a_outs=(), epi_a=()):
    nk = grid[2]
    na, nb, ne, no, nao = len(a_ins), len(b_ins), len(epi_ins), len(outs), len(a_outs)
    assert a_cache is None or nk == 1
    assert not a_outs or a_cache is not None
    dims = _DIMS[mode]
    if epi_fn is None:
        epi_fn = lambda acc: (acc,)

    def body(*refs):
        a_refs = refs[:na]
        b_refs = refs[na:na + nb]
        e_refs = refs[na + nb:na + nb + ne]
        o_refs = refs[na + nb + ne:na + nb + ne + no]
        ao_refs = refs[na + nb + ne + no:na + nb + ne + no + nao]
        scratch = list(refs[na + nb + ne + no + nao:])
        acc_ref = scratch.pop(0) if nk > 1 else None
        a_sc = scratch.pop(0) if a_cache is not None else None
        j = pl.program_id(1)
        k = pl.program_id(2)

        def finish(acc):
            res = epi_fn(acc, *[a_refs[t][...] for t in epi_a], *[r[...] for r in e_refs])
            for r, val in zip(o_refs, res):
                r[...] = val.astype(r.dtype)

        if a_sc is not None:
            @pl.when(j == 0)
            def _():
                res = a_fn(*[r[...] for r in a_refs])
                if nao:
                    for r, val in zip(ao_refs, res[1:]):
                        r[...] = val.astype(r.dtype)
                    res = res[0]
                a_sc[...] = res
            a = a_sc[...]
        else:
            a = a_fn(*[r[...] for r in a_refs])
        b = b_fn(*[r[...] for r in b_refs])
        prod = lax.dot_general(a, b, dims, preferred_element_type=F32)
        if nk == 1:
            finish(prod)
        else:
            @pl.when(k == 0)
            def _():
                acc_ref[...] = prod

            @pl.when(k > 0)
            def _():
                acc_ref[...] += prod

            @pl.when(k == nk - 1)
            def _():
                finish(acc_ref[...])

    scratch_shapes = []
    if nk > 1:
        scratch_shapes.append(pltpu.VMEM(acc_shape, F32))
    if a_cache is not None:
        scratch_shapes.append(pltpu.VMEM(*a_cache))
    all_outs = list(outs) + list(a_outs)
    res = pl.pallas_call(
        body, name=name, grid=grid,
        in_specs=[s for _, s in a_ins] + [s for _, s in b_ins] + [s for _, s in epi_ins],
        out_specs=[s for _, s in all_outs],
        out_shape=[o for o, _ in all_outs],
        scratch_shapes=scratch_shapes,
        compiler_params=_cparams(3),
    )(*[a for a, _ in a_ins], *[a for a, _ in b_ins], *[a for a, _ in epi_ins])
    return res


def _sds(shape, dtype):
    return jax.ShapeDtypeStruct(shape, dtype)


def _nt(a, b):
    return lax.dot_general(a, b, _DIMS["nt"], preferred_element_type=F32)


def _tn(a, b):
    return lax.dot_general(a, b, _DIMS["tn"], preferred_element_type=F32)


def _nn(a, b):
    return lax.dot_general(a, b, _DIMS["nn"], preferred_element_type=F32)


def _mlp_down_ple_head(up, x1, p, tgt, g_ple, g_post_mlp, w_down, w_pg, w_pp, seq, tr):
    nblk = seq // tr
    D = D_MODEL

    def body(up_ref, x1_ref, p_ref, t_ref, gp_ref, gm_ref, wd_ref, wpg_ref, wpp_ref,
             dx2_ref, df_ref, dpre_ref, h3_ref, dpp_ref, loss_ref, dgp_ref, dgm_ref):
        gp, gm, wpg, wpp = gp_ref[...], gm_ref[...], wpg_ref[...], wpp_ref[...]
        hidden = jnp.maximum(up_ref[...].astype(F32), 0.0)
        f_all = _nn((hidden * hidden).astype(BF16), wd_ref[...])
        halves = [pl.ds(0, tr // 2), pl.ds(tr // 2, tr // 2)]
        fb = [f_all[:tr // 2], f_all[tr // 2:]]
        x2b = [x1_ref[r, :] + _rms(fb[s], gm) for s, r in enumerate(halves)]
        h3 = [_rms(x, gp).astype(BF16) for x in x2b]
        gate = [_sig(_nn(h, wpg)) for h in h3]
        pp = [_nn(p_ref[r, :].astype(BF16), wpp) for r in halves]
        err = [x2b[s] + gate[s] * pp[s] - t_ref[r, :] for s, r in enumerate(halves)]
        dx3 = [e * (1.0 / D) for e in err]
        dpre = [(dx3[s] * pp[s] * gate[s] * (1.0 - gate[s])).astype(BF16) for s in range(2)]
        dh3 = [_nt(d, wpg) for d in dpre]
        loss, dgp_sum, dgm_sum = 0.0, 0.0, 0.0
        for s, r in enumerate(halves):
            h3_ref[r, :] = h3[s]
            dpp_ref[r, :] = (dx3[s] * gate[s]).astype(BF16)
            dpre_ref[r, :] = dpre[s]
            dxn, dgp = _rms_bwd(x2b[s], gp, dh3[s])
            dx2 = dx3[s] + dxn
            dx2_ref[r, :] = dx2
            dfb, dgm = _rms_bwd(fb[s], gm, dx2)
            df_ref[r, :] = dfb.astype(BF16)
            loss = loss + jnp.sum(err[s] * err[s], axis=0, keepdims=True)
            dgp_sum, dgm_sum = dgp_sum + dgp, dgm_sum + dgm
        loss_ref[...] = loss * (0.5 / D)
        dgp_ref[...] = dgp_sum
        dgm_ref[...] = dgm_sum

    rows = _bs((tr, D), lambda i: (i, 0))
    vec = _bs((1, D), lambda i: (0, 0))
    part = _bs((None, 1, D), lambda i: (i, 0, 0))
    return pl.pallas_call(
        body, name="mlp_down_ple_head", grid=(nblk,),
        in_specs=[_bs((tr, D_FF), lambda i: (i, 0)), rows, _bs((tr, PLE_DIM), lambda i: (i, 0)), rows, vec, vec,
                  _bs((D_FF, D), lambda i: (0, 0)), _bs((D, D), lambda i: (0, 0)), _bs((PLE_DIM, D), lambda i: (0, 0))],
        out_specs=[rows] * 5 + [part] * 3,
        out_shape=[_sds((seq, D), F32)] + [_sds((seq, D), BF16)] * 4 + [_sds((nblk, 1, D), F32)] * 3,
        compiler_params=_cparams(1),
    )(up, x1, p, tgt, g_ple, g_post_mlp, w_down, w_pg, w_pp)


def _shift_rows_down(u, prev, n):
    rows = u.shape[0]
    ridx = lax.broadcasted_iota(jnp.int32, u.shape, 0)
    out = pltpu.roll(u, n, 0)
    for r in range(n):
        out = jnp.where(ridx == r, prev[8 - n + r:8 - n + r + 1, :], out)
    del rows
    return out


def _shift_rows_up(u, nxt, n):
    rows = u.shape[0]
    ridx = lax.broadcasted_iota(jnp.int32, u.shape, 0)
    out = pltpu.roll(u, rows - n, 0)
    for r in range(n):
        out = jnp.where(ridx == rows - n + r, nxt[r:r + 1, :], out)
    return out


CONV_COL0 = 3


def _conv_fwd(proj, w_conv, seq, tr):
    hb = tr // 8

    def body(cb_ref, cc_ref, cu_ref, ccp_ref, cup_ref, w_ref, e_ref):
        i = pl.program_id(0)
        u = cc_ref[...] * cu_ref[...]
        up = jnp.where(i > 0, ccp_ref[...] * cup_ref[...], 0.0)
        w = w_ref[...]
        d = w[0:1, :] * _shift_rows_down(u, up, 2) + w[1:2, :] * _shift_rows_down(u, up, 1) + w[2:3, :] * u
        e_ref[...] = (cb_ref[...] * d).astype(BF16)

    prev = lambda c: (lambda i: (jnp.maximum(i * hb - 1, 0), c))
    return pl.pallas_call(
        body, name="conv_fwd", grid=(seq // tr,),
        in_specs=[_bs((tr, CONV_W), lambda i: (i, CONV_COL0)),
                  _bs((tr, CONV_W), lambda i: (i, CONV_COL0 + 1)),
                  _bs((tr, CONV_W), lambda i: (i, CONV_COL0 + 2)),
                  _bs((8, CONV_W), prev(CONV_COL0 + 1)),
                  _bs((8, CONV_W), prev(CONV_COL0 + 2)),
                  _bs((3, CONV_W), lambda i: (0, 0))],
        out_specs=_bs((tr, CONV_W), lambda i: (i, 0)),
        out_shape=_sds((seq, CONV_W), BF16),
        compiler_params=_cparams(1),
    )(proj, proj, proj, proj, proj, w_conv)


def _conv_bwd(proj, de, w_conv, seq, tr):
    hb = tr // 8
    nblk = seq // tr

    def body(cb_ref, cc_ref, cu_ref, ccp_ref, cup_ref, cbn_ref, de_ref, den_ref, w_ref, o_ref, dw_ref):
        i = pl.program_id(0)
        cc, cu, cb = cc_ref[...], cu_ref[...], cb_ref[...]
        u = cc * cu
        up = jnp.where(i > 0, ccp_ref[...] * cup_ref[...], 0.0)
        u1 = _shift_rows_down(u, up, 1)
        u2 = _shift_rows_down(u, up, 2)
        de_ = de_ref[...]
        dd = de_ * cb
        ddn = jnp.where(i < nblk - 1, den_ref[...] * cbn_ref[...], 0.0)
        w = w_ref[...]
        du = w[2:3, :] * dd + w[1:2, :] * _shift_rows_up(dd, ddn, 1) + w[0:1, :] * _shift_rows_up(dd, ddn, 2)
        o_ref[:, 0:CONV_W] = (de_ * (w[0:1, :] * u2 + w[1:2, :] * u1 + w[2:3, :] * u)).astype(BF16)
        o_ref[:, CONV_W:2 * CONV_W] = (du * cu).astype(BF16)
        o_ref[:, 2 * CONV_W:3 * CONV_W] = (du * cc).astype(BF16)
        ridx = lax.broadcasted_iota(jnp.int32, (8, CONV_W), 0)
        dw0 = jnp.sum(dd * u2, axis=0, keepdims=True)
        dw1 = jnp.sum(dd * u1, axis=0, keepdims=True)
        dw2 = jnp.sum(dd * u, axis=0, keepdims=True)
        dw_ref[...] = jnp.where(ridx == 0, dw0, jnp.where(ridx == 1, dw1, jnp.where(ridx == 2, dw2, 0.0)))

    prev = lambda c: (lambda i: (jnp.maximum(i * hb - 1, 0), c))
    nxt = lambda c: (lambda i: (jnp.minimum((i + 1) * hb, seq // 8 - 1), c))
    return pl.pallas_call(
        body, name="conv_bwd", grid=(nblk,),
        in_specs=[_bs((tr, CONV_W), lambda i: (i, CONV_COL0)),
                  _bs((tr, CONV_W), lambda i: (i, CONV_COL0 + 1)),
                  _bs((tr, CONV_W), lambda i: (i, CONV_COL0 + 2)),
                  _bs((8, CONV_W), prev(CONV_COL0 + 1)),
                  _bs((8, CONV_W), prev(CONV_COL0 + 2)),
                  _bs((8, CONV_W), nxt(CONV_COL0)),
                  _bs((tr, CONV_W), lambda i: (i, 0)),
                  _bs((8, CONV_W), nxt(0)),
                  _bs((3, CONV_W), lambda i: (0, 0))],
        out_specs=[_bs((tr, 3 * CONV_W), lambda i: (i, 0)), _bs((None, 8, CONV_W), lambda i: (i, 0, 0))],
        out_shape=[_sds((seq, 3 * CONV_W), BF16), _sds((nblk, 8, CONV_W), F32)],
        compiler_params=_cparams(1),
    )(proj, proj, proj, proj, proj, proj, de, de, w_conv)


def _log_gates(z):
    lse = jnp.log(1.0 + jnp.exp(-jnp.abs(z)))
    log_beta = jnp.minimum(z, 0.0) - lse
    return log_beta, log_beta - z


DEAD_LOG_WEIGHT = -110.0
NO_TILE = -1e30


def _first_live_tile(start, scores, live_sc):
    def alive():
        return jnp.max(jnp.maximum(live_sc[0], live_sc[1])) > DEAD_LOG_WEIGHT

    def step(c):
        for h, z in enumerate(scores(c[0])):
            live_sc[h] = live_sc[h] + jnp.sum(_log_gates(z)[1], axis=-1, keepdims=True)
        return c[0] - 1, alive()

    j_end, _ = lax.while_loop(lambda c: jnp.logical_and(c[0] >= 0, c[1]), step, (start, alive()))
    return j_end + 1


def _attn_fwd(proj, seq):
    blk = ATT_BLK
    nq = seq // blk
    npair = N_HEADS // 2

    def body(q_ref, k_ref, v_ref, o_ref, z0_sc, z1_sc, w0_sc, w1_sc, tot_sc, live_sc, acc_sc):
        i = pl.program_id(1)
        is_a = lax.broadcasted_iota(jnp.int32, (1, LANES), 1) < HEAD_DIM
        q2 = (q_ref[...] * Q_SCALE).astype(BF16)
        zero = jnp.zeros_like(q2)
        qs = (jnp.where(is_a, q2, zero), jnp.where(is_a, zero, q2))
        row = lax.broadcasted_iota(jnp.int32, (blk, blk), 0)
        col = lax.broadcasted_iota(jnp.int32, (blk, blk), 1)
        tri = (row > col).astype(BF16)
        causal = col < row

        def tile_of(ref, j):
            return ref[pl.ds(pl.multiple_of(j * blk, blk), blk), :].astype(BF16)

        def scores(j):
            k2 = tile_of(k_ref, j)
            return [_nt(qs[h], k2) for h in range(2)]

        has_left = i > 0
        left = jnp.maximum(i - 1, 0)
        g_d = [_log_gates(z) for z in scores(i)]
        g_l = [_log_gates(z) for z in scores(left)]
        keep_d = [jnp.where(causal, g[1], 0.0) for g in g_d]
        suf_d = [_nn(lk.astype(BF16), tri) for lk in keep_d]
        suf_l = [_nn(g[1].astype(BF16), tri) for g in g_l]
        v_d, v_l = tile_of(v_ref, i), tile_of(v_ref, left)
        pv = []
        for h in range(2):
            sum_d = jnp.sum(keep_d[h], axis=-1, keepdims=True)
            w_d = jnp.where(causal, jnp.exp(g_d[h][0] + suf_d[h]), 0.0)
            w_l = jnp.exp(g_l[h][0] + (jnp.where(has_left, sum_d, NO_TILE) + suf_l[h]))
            pv.append(_nn(w_d.astype(BF16), v_d) + _nn(w_l.astype(BF16), v_l))
            tot_sc[h] = sum_d + jnp.sum(g_l[h][1], axis=-1, keepdims=True)
        acc_sc[...] = jnp.where(is_a, pv[0], pv[1])

        live_sc[...] = tot_sc[...]
        first = _first_live_tile(i - 2, scores, live_sc)
        trips = i - 1 - first
        z_bufs, w_bufs = (z0_sc, z1_sc), (w0_sc, w1_sc)

        def put(ref, vals):
            for h in range(2):
                ref[h] = vals[h]

        def weights(zs):
            gates = [_log_gates(z) for z in zs]
            sums = [_nn(g[1].astype(BF16), tri) for g in gates]
            ws = []
            for h in range(2):
                ws.append(jnp.exp(gates[h][0] + (tot_sc[h] + sums[h])).astype(BF16))
                tot_sc[h] = tot_sc[h] + jnp.sum(gates[h][1], axis=-1, keepdims=True)
            return ws

        def add_values(w_buf, j):
            v2 = tile_of(v_ref, j)
            acc_sc[...] += jnp.where(is_a, _nn(w_buf[0], v2), _nn(w_buf[1], v2))

        def trip(j, s):
            add_values(w_bufs[s], j + 1)
            put(z_bufs[1 - s], scores(jnp.maximum(j - 1, first)))
            put(w_bufs[1 - s], weights((z_bufs[s][0], z_bufs[s][1])))

        @pl.when(trips > 0)
        def _():
            put(z0_sc, scores(i - 2))
            w0_sc[...] = jnp.zeros_like(w0_sc)

            def two_trips(pp, carry):
                j = i - 2 - 2 * pp
                trip(j, 0)
                trip(j - 1, 1)
                return carry

            lax.fori_loop(0, trips // 2, two_trips, 0)
            odd = trips % 2 == 1

            @pl.when(odd)
            def _():
                trip(first, 0)
                add_values(w1_sc, first)

            @pl.when(jnp.logical_not(odd))
            def _():
                add_values(w0_sc, first)

        o_ref[...] = acc_sc[...].astype(BF16)

    return pl.pallas_call(
        body, name="attn_fwd", grid=(npair, nq),
        in_specs=[_bs((blk, LANES), lambda p, i: (i, p)),
                  _bs((seq, LANES), lambda p, i: (0, npair + p)),
                  _bs((seq, LANES), lambda p, i: (0, 2 * npair + p))],
        out_specs=_bs((blk, LANES), lambda p, i: (i, p)),
        out_shape=_sds((seq, ATTN_W), BF16),
        scratch_shapes=[pltpu.VMEM((2, blk, blk), F32), pltpu.VMEM((2, blk, blk), F32),
                        pltpu.VMEM((2, blk, blk), BF16), pltpu.VMEM((2, blk, blk), BF16),
                        pltpu.VMEM((2, blk, 1), F32), pltpu.VMEM((2, blk, 1), F32), pltpu.VMEM((blk, LANES), F32)],
        compiler_params=_cparams(2),
    )(proj, proj, proj)


def _attn_bwd(proj, do, seq):
    blk = ATT_BLK
    nq = seq // blk
    npair = N_HEADS // 2

    def body(q_ref, k_ref, v_ref, do_ref, dq_ref, dk_ref, dv_ref,
             prod0_sc, prod1_sc, pend0_sc, pend1_sc, tot_sc, live_sc, cum_sc, pre_sc, dq_sc):
        i = pl.program_id(1)

        @pl.when(i == 0)
        def _():
            dk_ref[...] = jnp.zeros_like(dk_ref)
            dv_ref[...] = jnp.zeros_like(dv_ref)

        is_a = lax.broadcasted_iota(jnp.int32, (1, LANES), 1) < HEAD_DIM
        q2 = (q_ref[...] * Q_SCALE).astype(BF16)
        do2 = do_ref[...]
        zero = jnp.zeros_like(q2)
        qs = (jnp.where(is_a, q2, zero), jnp.where(is_a, zero, q2))
        dos = (jnp.where(is_a, do2, zero), jnp.where(is_a, zero, do2))
        row = lax.broadcasted_iota(jnp.int32, (blk, blk), 0)
        col = lax.broadcasted_iota(jnp.int32, (blk, blk), 1)
        tri_after = (row > col).astype(BF16)
        tri_excl = (row < col).astype(BF16)
        causal = col < row

        def tile_of(ref, j):
            return ref[pl.ds(pl.multiple_of(j * blk, blk), blk), :].astype(BF16)

        def scores(j):
            k2 = tile_of(k_ref, j)
            return [_nt(qs[h], k2) for h in range(2)]

        def products(j):
            v2 = tile_of(v_ref, j)
            return scores(j) + [_nt(dos[h], v2) for h in range(2)]

        def row_sum(a):
            return jnp.sum(a, axis=-1, keepdims=True)

        def grad_matmuls(ws, dzs, j):
            rows = pl.ds(pl.multiple_of(j * blk, blk), blk)
            k2 = tile_of(k_ref, j)
            dq_sc[...] += jnp.where(is_a, _nn(dzs[0], k2), _nn(dzs[1], k2))
            dk_ref[rows, :] += jnp.where(is_a, _tn(dzs[0], q2), _tn(dzs[1], q2))
            if ws is not None:
                dv_ref[rows, :] += jnp.where(is_a, _tn(ws[0], do2), _tn(ws[1], do2))

        has_left = i > 0
        left = jnp.maximum(i - 1, 0)
        p_d, p_l = products(i), products(left)
        g_d = [_log_gates(z) for z in p_d[:2]]
        g_l = [_log_gates(z) for z in p_l[:2]]
        keep_d = [jnp.where(causal, g[1], 0.0) for g in g_d]
        suf_d = [_nn(lk.astype(BF16), tri_after) for lk in keep_d]
        suf_l = [_nn(g[1].astype(BF16), tri_after) for g in g_l]
        w_d, w_l, gg_d, gg_l = [], [], [], []
        for h in range(2):
            sum_d = row_sum(keep_d[h])
            w_d.append(jnp.where(causal, jnp.exp(g_d[h][0] + suf_d[h]), 0.0))
            w_l.append(jnp.exp(g_l[h][0] + (jnp.where(has_left, sum_d, NO_TILE) + suf_l[h])))
            gg_d.append(p_d[2 + h] * w_d[h])
            gg_l.append(p_l[2 + h] * w_l[h])
            tot_sc[h] = sum_d + row_sum(g_l[h][1])
        before_d = [_nn(g.astype(BF16), tri_excl) for g in gg_d]
        before_l = [_nn(g.astype(BF16), tri_excl) for g in gg_l]
        dz_d, dz_l = [], []
        for h in range(2):
            beta_d, beta_l = jnp.exp(g_d[h][0]), jnp.exp(g_l[h][0])
            dz_l.append((gg_l[h] * (1.0 - beta_l) - before_l[h] * beta_l).astype(BF16))
            dz = gg_d[h] * (1.0 - beta_d) - (row_sum(gg_l[h]) + before_d[h]) * beta_d
            dz_d.append(jnp.where(causal, dz, 0.0).astype(BF16))
        dq_sc[...] = jnp.zeros_like(dq_sc)
        grad_matmuls([w.astype(BF16) for w in w_l], dz_l, left)
        grad_matmuls([w.astype(BF16) for w in w_d], dz_d, i)

        live_sc[...] = tot_sc[...]
        first = _first_live_tile(i - 2, scores, live_sc)
        trips = i - 1 - first
        prod_bufs, pend_bufs = (prod0_sc, prod1_sc), (pend0_sc, pend1_sc)

        def local_grads(prods):
            zs, dws = prods[:2], prods[2:]
            gates = [_log_gates(z) for z in zs]
            sums = [_nn(g[1].astype(BF16), tri_after) for g in gates]
            ws, gs = [], []
            for h in range(2):
                cum = cum_sc[h] + row_sum(gates[h][1])
                cum_sc[h] = cum
                ws.append(jnp.exp(gates[h][0] + ((live_sc[h] - cum) + sums[h])))
                gs.append(dws[h] * ws[h])
            befores = [_nn(g.astype(BF16), tri_excl) for g in gs]
            dzs = []
            for h in range(2):
                beta = jnp.exp(gates[h][0])
                dzs.append((gs[h] * (1.0 - beta) - (pre_sc[h] + befores[h]) * beta).astype(BF16))
                pre_sc[h] = pre_sc[h] + row_sum(gs[h])
            return [w.astype(BF16) for w in ws] + dzs

        def put(ref, vals):
            for n, val in enumerate(vals):
                ref[n] = val

        def flush(pend, j):
            grad_matmuls([pend[0], pend[1]], [pend[2], pend[3]], j)

        def trip(j, s):
            flush(pend_bufs[s], jnp.maximum(j - 1, first))
            put(prod_bufs[1 - s], products(j + 1))
            put(pend_bufs[1 - s], local_grads([prod_bufs[s][n] for n in range(4)]))

        def earlier_keys_share(j, mask):
            dzs = []
            for h, z in enumerate(scores(j)):
                beta = jnp.exp(_log_gates(z)[0])
                dzs.append(jnp.where(mask, -pre_sc[h] * beta, 0.0).astype(BF16))
            grad_matmuls(None, dzs, j)

        @pl.when(trips > 0)
        def _():
            cum_sc[...] = jnp.zeros_like(cum_sc)
            pre_sc[...] = jnp.zeros_like(pre_sc)
            pend0_sc[...] = jnp.zeros_like(pend0_sc)
            put(prod0_sc, products(first))

            def two_trips(pp, carry):
                trip(first + 2 * pp, 0)
                trip(first + 2 * pp + 1, 1)
                return carry

            lax.fori_loop(0, trips // 2, two_trips, 0)
            odd = trips % 2 == 1

            @pl.when(odd)
            def _():
                trip(i - 2, 0)
                flush(pend1_sc, i - 2)

            @pl.when(jnp.logical_not(odd))
            def _():
                flush(pend0_sc, i - 2)

            earlier_keys_share(i - 1, True)
            earlier_keys_share(i, causal)

        dq_ref[...] = dq_sc[...] * Q_SCALE

    qmap = lambda p, i: (i, p)
    return pl.pallas_call(
        body, name="attn_bwd", grid=(npair, nq),
        in_specs=[_bs((blk, LANES), qmap),
                  _bs((seq, LANES), lambda p, i: (0, npair + p)),
                  _bs((seq, LANES), lambda p, i: (0, 2 * npair + p)),
                  _bs((blk, LANES), qmap)],
        out_specs=[_bs((blk, LANES), qmap),
                   _bs((seq, LANES), lambda p, i: (0, p)),
                   _bs((seq, LANES), lambda p, i: (0, p))],
        out_shape=[_sds((seq, ATTN_W), F32)] * 3,
        scratch_shapes=[pltpu.VMEM((4, blk, blk), F32), pltpu.VMEM((4, blk, blk), F32),
                        pltpu.VMEM((4, blk, blk), BF16), pltpu.VMEM((4, blk, blk), BF16),
                        pltpu.VMEM((2, blk, 1), F32), pltpu.VMEM((2, blk, 1), F32), pltpu.VMEM((2, blk, 1), F32),
                        pltpu.VMEM((2, blk, 1), F32), pltpu.VMEM((blk, LANES), F32)],
        compiler_params=_cparams(2),
    )(proj, proj, proj, do)


def _elementwise(name, fn, ins, out_dtypes):
    rows, cols = ins[0].shape
    tr = rows
    for cand in (512, 256, 128, 64, 32, 16, 8):
        if rows % cand == 0 and cand * cols * 4 <= 2 * 1024 * 1024:
            tr = cand
            break
    n_in = len(ins)

    def body(*refs):
        res = fn(*[r[...] for r in refs[:n_in]])
        for r, val in zip(refs[n_in:], res):
            r[...] = val.astype(r.dtype)

    spec = _bs((tr, cols), lambda i: (i, 0))
    return pl.pallas_call(
        body, name=name, grid=(rows // tr,),
        in_specs=[spec] * n_in, out_specs=[spec] * len(out_dtypes),
        out_shape=[_sds((rows, cols), dt) for dt in out_dtypes],
        compiler_params=_cparams(1),
    )(*ins)


def _adamw_fn(w, g, m, v):
    m = ADAM_B1 * m + (1.0 - ADAM_B1) * g
    v = ADAM_B2 * v + (1.0 - ADAM_B2) * (g * g)
    m_hat = m / (1.0 - ADAM_B1 ** ADAM_STEP)
    v_hat = v / (1.0 - ADAM_B2 ** ADAM_STEP)
    delta = -ADAM_LR * (m_hat / (jnp.sqrt(v_hat) + ADAM_EPS) + ADAM_WD * w)
    return delta, m, v


def _adamw(name, w, g, m, v):
    shape = w.shape
    as2d = lambda a: a.reshape(-1, shape[-1])
    delta, nm, nv = _elementwise(name, _adamw_fn, [as2d(w), as2d(g), as2d(m), as2d(v)], [F32, F32, F32])
    return delta.reshape(shape), nm.reshape(shape), nv.reshape(shape)


def _place():
    x, y, c = lax.axis_index("x"), lax.axis_index("y"), lax.axis_index("c")
    chips = [(1 - x, y), (x, 1 - y), (1 - x, 1 - y)]
    return x, y, c, chips


ANY = pl.BlockSpec(memory_space=pl.ANY)
VMEM_WHOLE = pl.BlockSpec(memory_space=pltpu.VMEM)


def _allgather_weights(shards):
    n = len(shards)

    def body(*refs):
        src, dst = refs[:n], refs[n:2 * n]
        send_sems, recv_sems, local_sems = refs[2 * n:]
        x, y, c, chips = _place()
        me, sibling, mychip = (x, y, c), (x, y, 1 - c), 2 * x + y

        x_nbr, y_nbr, diag = 2 * (1 - x) + y, 2 * x + (1 - y), 2 * (1 - x) + (1 - y)
        to_x, to_y = (1 - x, y, c), (x, 1 - y, c)

        def parts(w):
            hr = src[w].shape[0] // 2
            first = hr // 2 if hr % 32 == 0 else hr
            return first, hr - first

        def rows_of(w, chip, half, route):
            hr = src[w].shape[0] // 2
            first, second = parts(w)
            start, size = {0: (0, hr), 1: (0, hr), 2: (0, first), 3: (first, second)}[route]
            return dst[w].at[chip, pl.ds(half * hr + start, size)]

        def copy(w, k, src_ref, dst_ref, to):
            return pltpu.make_async_remote_copy(src_ref=src_ref, dst_ref=dst_ref, send_sem=send_sems.at[w, k],
                                                recv_sem=recv_sems.at[w, k], device_id=to, device_id_type=MESH)

        def landed(w, route):
            chip = {0: x_nbr, 1: y_nbr, 2: diag, 3: diag}[route]
            return rows_of(w, chip, c, route), chip

        def routes(w):
            return (0, 1, 2, 3) if parts(w)[1] else (0, 1, 2)

        started, local = [], []
        for w in range(n):
            hr = src[w].shape[0] // 2
            own = pltpu.make_async_copy(src[w], dst[w].at[mychip], local_sems.at[w])
            own.start()
            local.append(own)
            mine = src[w].at[pl.ds(c * hr, hr)]
            for route, to in ((0, to_x), (1, to_y)):
                cp = copy(w, route, mine, rows_of(w, mychip, c, route), to)
                cp.start()
                started.append(cp)

        def pass_on(w, route):
            got, chip = landed(w, route)
            copy(w, route, got, got, me).wait_recv()
            if route == 1:
                part = rows_of(w, chip, c, 2)
                started.append(copy(w, 2, part, part, to_x))
                started[-1].start()
            if route == 0 and parts(w)[1]:
                part = rows_of(w, chip, c, 3)
                started.append(copy(w, 3, part, part, to_y))
                started[-1].start()
            started.append(copy(w, 4 + route, got, got, sibling))
            started[-1].start()

        for w in range(n):
            pass_on(w, 1)
            pass_on(w, 0)
        for w in range(n):
            for route in routes(w)[2:]:
                pass_on(w, route)
        for w in range(n):
            for route in routes(w):
                chip = landed(w, route)[1]
                from_sib = rows_of(w, chip, 1 - c, route)
                copy(w, 4 + route, from_sib, from_sib, me).wait_recv()
        for cp in local:
            cp.wait()
        for cp in started:
            cp.wait_send()

    return pl.pallas_call(
        body, name="allgather_weights",
        in_specs=[VMEM_WHOLE] * n, out_specs=[VMEM_WHOLE] * n,
        out_shape=[_sds((N_CHIPS,) + s.shape, s.dtype) for s in shards],
        scratch_shapes=[pltpu.SemaphoreType.DMA((n, 8)), pltpu.SemaphoreType.DMA((n, 8)),
                        pltpu.SemaphoreType.DMA((n,))],
        compiler_params=pltpu.CompilerParams(vmem_limit_bytes=VMEM_LIMIT),
    )(*shards)


SUM_ROWS = 64


def _rs_pair_sum(name, grads):
    n = len(grads)

    def body(*refs):
        g, out = refs[:n], refs[n:2 * n]
        stage, land, keep = refs[2 * n:3 * n], refs[3 * n:4 * n], refs[4 * n:5 * n]
        send_sems, recv_sems, stage_sems, keep_sems = refs[5 * n:]
        x, y, c, _ = _place()
        sibling = (x, y, 1 - c)
        loads = []
        for w in range(n):
            hr = g[w].shape[1] // 2
            st = pltpu.make_async_copy(g[w].at[:, pl.ds((1 - c) * hr, hr)], stage[w], stage_sems.at[w])
            kp = pltpu.make_async_copy(g[w].at[:, pl.ds(c * hr, hr)], keep[w], keep_sems.at[w])
            st.start()
            kp.start()
            loads.append((st, kp))
        gives = []
        for w in range(n):
            loads[w][0].wait()
            give = pltpu.make_async_remote_copy(src_ref=stage[w], dst_ref=land[w], send_sem=send_sems.at[w],
                                                recv_sem=recv_sems.at[w], device_id=sibling, device_id_type=MESH)
            give.start()
            gives.append(give)
        for w in range(n):
            loads[w][1].wait()
            gives[w].wait_recv()
            nb = g[w].shape[1] // 2 // SUM_ROWS

            def add(idx, carry, w=w, nb=nb):
                k, r = idx // nb, pl.multiple_of((idx % nb) * SUM_ROWS, SUM_ROWS)
                rows = pl.ds(r, SUM_ROWS)
                out[w][k, rows, :] = (keep[w][k, rows, :] + land[w][k, rows, :]).astype(BF16)
                return carry

            lax.fori_loop(0, N_CHIPS * nb, add, 0)
        for give in gives:
            give.wait_send()

    half = [(N_CHIPS, a.shape[1] // 2, a.shape[2]) for a in grads]
    bufs = [pltpu.VMEM(s, F32) for s in half]
    sems = pltpu.SemaphoreType.DMA((n,))
    return pl.pallas_call(
        body, name=name,
        in_specs=[ANY] * n, out_specs=[VMEM_WHOLE] * n, out_shape=[_sds(s, BF16) for s in half],
        scratch_shapes=bufs + bufs + bufs + [sems, sems, sems, sems],
        compiler_params=pltpu.CompilerParams(vmem_limit_bytes=VMEM_LIMIT),
    )(*grads)


def _rs_exchange_join(parts):
    n = len(parts)

    def body(*refs):
        t, full = refs[:n], refs[n:2 * n]
        got_x, got_y, pass_on, got_2 = (refs[m * n:(m + 1) * n] for m in range(2, 6))
        send_sems, recv_sems = refs[6 * n:]
        x, y, c, _ = _place()
        mychip, sibling = 2 * x + y, (x, y, 1 - c)
        x_nbr, y_nbr, diag = 2 * (1 - x) + y, 2 * x + (1 - y), 2 * (1 - x) + (1 - y)
        to_x, to_y = (1 - x, y, c), (x, 1 - y, c)
        sends = []

        def copy(w, k, src_ref, dst_ref, to):
            return pltpu.make_async_remote_copy(src_ref=src_ref, dst_ref=dst_ref, send_sem=send_sems.at[w, k],
                                                recv_sem=recv_sems.at[w, k], device_id=to, device_id_type=MESH)

        def start(cp):
            cp.start()
            sends.append(cp)

        def add_rows(w, count, fn):
            def step(idx, carry):
                fn(pl.ds(pl.multiple_of(idx * SUM_ROWS, SUM_ROWS), SUM_ROWS), pl.multiple_of(idx * SUM_ROWS, SUM_ROWS))
                return carry
            lax.fori_loop(0, count // SUM_ROWS, step, 0)

        f32 = lambda v: v.astype(F32)
        for w in range(n):
            ha = t[w].shape[1] // 2
            part_a, part_b = pl.ds(0, ha), pl.ds(ha, ha)
            start(copy(w, 0, t[w].at[x_nbr, part_a], got_x[w].at[0], to_x))
            start(copy(w, 1, t[w].at[diag, part_a], got_x[w].at[1], to_x))
            start(copy(w, 2, t[w].at[y_nbr, part_b], got_y[w].at[0], to_y))
            start(copy(w, 3, t[w].at[diag, part_b], got_y[w].at[1], to_y))
        for w in range(n):
            hr = t[w].shape[1]
            ha = hr // 2
            for k in (0, 1):
                copy(w, k, got_x[w].at[k], got_x[w].at[k], to_x).wait_recv()

            def sum_a(rows, r, w=w, hr=hr):
                full[w][pl.ds(pl.multiple_of(c * hr + r, SUM_ROWS), SUM_ROWS), :] = \
                    f32(t[w][mychip, rows, :]) + f32(got_x[w][0, rows, :])
                pass_on[w][rows, :] = (f32(t[w][y_nbr, rows, :]) + f32(got_x[w][1, rows, :])).astype(BF16)

            add_rows(w, ha, sum_a)
            start(copy(w, 4, pass_on[w].at[pl.ds(0, ha)], got_2[w].at[pl.ds(0, ha)], to_y))
            for k in (2, 3):
                copy(w, k, got_y[w].at[k - 2], got_y[w].at[k - 2], to_y).wait_recv()

            def sum_b(rows, r, w=w, hr=hr, ha=ha):
                lower = pl.ds(pl.multiple_of(ha + r, SUM_ROWS), SUM_ROWS)
                full[w][pl.ds(pl.multiple_of(c * hr + ha + r, SUM_ROWS), SUM_ROWS), :] = \
                    f32(t[w][mychip, lower, :]) + f32(got_y[w][0, rows, :])
                pass_on[w][lower, :] = (f32(t[w][x_nbr, lower, :]) + f32(got_y[w][1, rows, :])).astype(BF16)

            add_rows(w, ha, sum_b)
            start(copy(w, 5, pass_on[w].at[pl.ds(ha, ha)], got_2[w].at[pl.ds(ha, ha)], to_x))
        for w in range(n):
            hr = t[w].shape[1]
            ha = hr // 2
            copy(w, 4, got_2[w].at[pl.ds(0, ha)], got_2[w].at[pl.ds(0, ha)], to_y).wait_recv()
            copy(w, 5, got_2[w].at[pl.ds(ha, ha)], got_2[w].at[pl.ds(ha, ha)], to_x).wait_recv()

            def finish(rows, r, w=w, hr=hr):
                out_rows = pl.ds(pl.multiple_of(c * hr + r, SUM_ROWS), SUM_ROWS)
                full[w][out_rows, :] = full[w][out_rows, :] + f32(got_2[w][rows, :])

            add_rows(w, hr, finish)
            mine = full[w].at[pl.ds(c * hr, hr)]
            start(copy(w, 6, mine, mine, sibling))
        for w in range(n):
            hr = t[w].shape[1]
            theirs = full[w].at[pl.ds((1 - c) * hr, hr)]
            copy(w, 6, theirs, theirs, sibling).wait_recv()
        for cp in sends:
            cp.wait_send()

    half = lambda a: pltpu.VMEM((2, a.shape[1] // 2, a.shape[2]), a.dtype)
    whole = lambda a: pltpu.VMEM(a.shape[1:], a.dtype)
    return pl.pallas_call(
        body, name="rs_exchange_join",
        in_specs=[VMEM_WHOLE] * n, out_specs=[VMEM_WHOLE] * n,
        out_shape=[_sds((2 * a.shape[1], a.shape[2]), F32) for a in parts],
        scratch_shapes=[half(a) for a in parts] + [half(a) for a in parts] + [whole(a) for a in parts]
        + [whole(a) for a in parts] + [pltpu.SemaphoreType.DMA((n, 7)), pltpu.SemaphoreType.DMA((n, 7))],
        compiler_params=pltpu.CompilerParams(vmem_limit_bytes=VMEM_LIMIT),
    )(*parts)


def _small_allreduce(loss_p, dg_parts, dbg_a, dbg_c, dwc):
    ins = [loss_p] + list(dg_parts) + [dbg_a, dbg_c, dwc]
    n_in = len(ins)
    vmem = pl.BlockSpec(memory_space=pltpu.VMEM)

    def body(*refs):
        in_refs = refs[:n_in]
        out_ref, vec, buf, send_sems, recv_sems = refs[n_in:]
        x, y, c, _ = _place()
        me = 4 * x + 2 * y + c
        vec[...] = jnp.zeros_like(vec)
        vec[0:1, :] = jnp.sum(in_refs[0][...], axis=0)
        for r in range(5):
            vec[1 + r:2 + r, :] = jnp.sum(in_refs[1 + r][...], axis=0)
        vec[6:7, :] = jnp.sum(in_refs[6][...], axis=0)
        vec[7:8, :] = jnp.sum(in_refs[7][...], axis=0)
        vec[8:16, 0:CONV_W] = jnp.sum(in_refs[8][...], axis=0)
        buf[pl.ds(me, 1)] = vec[...][None]
        copies = []
        for r in range(1, 8):
            fx, fy, fc = (r >> 2) & 1, (r >> 1) & 1, r & 1
            to = (1 - x if fx else x, 1 - y if fy else y, 1 - c if fc else c)
            cp = pltpu.make_async_remote_copy(src_ref=vec, dst_ref=buf.at[me], send_sem=send_sems.at[r - 1],
                                              recv_sem=recv_sems.at[r - 1], device_id=to, device_id_type=MESH)
            cp.start()
            copies.append(cp)
        for cp in copies:
            cp.wait()
        total = buf[0]
        for s in range(1, 8):
            total = total + buf[s]
        out_ref[...] = total
        out_ref[0:1, :] = jnp.broadcast_to(jnp.sum(total[0:1, :], axis=-1, keepdims=True), (1, D_MODEL))

    return pl.pallas_call(
        body, name="small_allreduce",
        in_specs=[vmem] * n_in, out_specs=vmem, out_shape=_sds((SMALL_ROWS, D_MODEL), F32),
        scratch_shapes=[pltpu.VMEM((SMALL_ROWS, D_MODEL), F32), pltpu.VMEM((8, SMALL_ROWS, D_MODEL), F32),
                        pltpu.SemaphoreType.DMA((7,)), pltpu.SemaphoreType.DMA((7,))],
    )(*ins)


def _local_step(x, p, tgt, g, b_gate, w_conv, wf):
    seq = x.shape[0]
    tm = min(seq, 1024)
    th = min(seq, 512)
    tl = min(seq, 2048)
    ni, nh, nl = seq // tm, seq // th, seq // tl
    g_pre_mix, g_post_mix, g_pre_mlp, g_post_mlp, g_ple = g
    w_in, w_ao, w_co, w_o, w_up, w_down, w_pg, w_pp, w_in_nat, w_up_nat = wf
    D = D_MODEL
    vec = lambda a, blk=0: (a, _bs((1, D), lambda i, j, k: (0, blk)))
    rows_i = lambda a, t, blk=0: (a, _bs((t, D), lambda i, j, k: (i, blk)))
    rows_k = lambda a, t, blk=0: (a, _bs((t, D), lambda i, j, k: (k, blk)))
    part = lambda n: (_sds((n, 1, D), F32), _bs((None, 1, D), lambda i, j, k: (i, 0, 0)))
    full2 = lambda a: (a, _bs(a.shape, lambda i, j, k: (0, 0)))

    normed = lambda xb, gb: (_rms(xb, gb).astype(BF16),) * 2
    keep_a = lambda t: [(_sds((seq, D), BF16), _bs((t, D), lambda i, j, k: (i, 0)))]
    main_w = D_IN - 2 * D
    proj, gates, h1 = _mm("proj_in", "nn", (nh, 1, 1),
                          a_ins=[rows_i(x, th), vec(g_pre_mix)], a_fn=normed,
                          b_ins=[full2(w_in_nat)], b_fn=_ident,
                          epi_fn=lambda acc: (acc[:, :main_w], acc[:, main_w:]),
                          outs=[(_sds((seq, main_w), F32), _bs((th, main_w), lambda i, j, k: (i, 0))),
                                (_sds((seq, 2 * D), BF16), _bs((th, 2 * D), lambda i, j, k: (i, 0)))],
                          acc_shape=(th, D_IN), a_cache=((th, D), BF16), a_outs=keep_a(th))
    o = _attn_fwd(proj, seq)
    e = _conv_fwd(proj, w_conv, seq, tm)

    def gate_values(ga, gc, ba, bc):
        return _sig(ga.astype(F32) + ba), _sig(gc.astype(F32) + bc)

    def mix_fn(ga, gc, ob, eb, ba, bc, wao, wco):
        sa, sc = gate_values(ga, gc, ba, bc)
        ya, yc = _nn(ob, wao).astype(BF16), _nn(eb, wco).astype(BF16)
        mix = (sa * ya.astype(F32) + sc * yc.astype(F32)).astype(BF16)
        return mix, mix, ya, yc

    def post_mix(acc, xb, gb):
        return acc, xb + _rms(acc, gb)

    half_rows = lambda a: (a, _bs((th, a.shape[1]), lambda i, j, k: (i, 0)))
    mixed, x1, mixin, y_attn, y_conv = _mm(
        "mix_out", "nn", (nh, 1, 1),
        a_ins=[rows_i(gates, th, 0), rows_i(gates, th, 1), half_rows(o), half_rows(e), vec(b_gate, 0), vec(b_gate, 1),
               full2(w_ao), full2(w_co)], a_fn=mix_fn, b_ins=[full2(w_o)], b_fn=_ident,
        epi_ins=[rows_i(x, th), vec(g_post_mix)], epi_fn=post_mix,
        outs=[(_sds((seq, D), BF16), _bs((th, D), lambda i, j, k: (i, 0))),
              (_sds((seq, D), F32), _bs((th, D), lambda i, j, k: (i, 0)))],
        acc_shape=(th, D), a_cache=((th, D), BF16), a_outs=keep_a(th) * 3)
    mix_ins = lambda rows: [rows(gates, th, 0), rows(gates, th, 1), rows(y_attn, th), rows(y_conv, th),
                            vec(b_gate, 0), vec(b_gate, 1)]
    up, h2 = _mm("mlp_up", "nn", (nh, 1, 1),
                 a_ins=[rows_i(x1, th), vec(g_pre_mlp)], a_fn=normed,
                 b_ins=[full2(w_up_nat)], b_fn=_ident,
                 outs=[(_sds((seq, D_FF), BF16), _bs((th, D_FF), lambda i, j, k: (i, 0)))],
                 acc_shape=(th, D_FF), a_cache=((th, D), BF16), a_outs=keep_a(th))

    def relu2(ub):
        r = jnp.maximum(ub.astype(F32), 0.0)
        return (r * r).astype(BF16)

    dx2, df, dpre, h3, dpp, loss_p, dg_ple_p, dg_post_mlp_p = _mlp_down_ple_head(
        up, x1, p, tgt, g_ple, g_post_mlp, w_down, w_pg, w_pp, seq, th)

    (dw_pp,) = _mm("dw_ple_proj", "tn", (1, 1, nh),
                   a_ins=[(p, _bs((th, PLE_DIM), lambda i, j, k: (k, 0)))], a_fn=_to_bf16,
                   b_ins=[rows_k(dpp, th)], b_fn=_ident,
                   outs=[(_sds((PLE_DIM, D), F32), _bs((PLE_DIM, D), lambda i, j, k: (0, 0)))],
                   acc_shape=(PLE_DIM, D))
    (dw_pg,) = _mm("dw_ple_gate", "tn", (1, 1, nl),
                   a_ins=[rows_k(h3, tl)], a_fn=_ident, b_ins=[rows_k(dpre, tl)], b_fn=_ident,
                   outs=[(_sds((D, D), F32), _bs((D, D), lambda i, j, k: (0, 0)))], acc_shape=(D, D))

    def dup_fn(acc, ub):
        return (acc * (2.0 * jnp.maximum(ub.astype(F32), 0.0)),)

    (dup,) = _mm("d_mlp_down", "nt", (nh, 1, 1),
                 a_ins=[rows_i(df, th)], a_fn=_ident, b_ins=[full2(w_down)], b_fn=_ident,
                 epi_ins=[(up, _bs((th, D_FF), lambda i, j, k: (i, 0)))], epi_fn=dup_fn,
                 outs=[(_sds((seq, D_FF), BF16), _bs((th, D_FF), lambda i, j, k: (i, 0)))],
                 acc_shape=(th, D_FF))
    (dw_down,) = _mm("dw_mlp_down", "tn", (4, 1, nl),
                     a_ins=[(up, _bs((tl, D), lambda i, j, k: (k, i)))], a_fn=relu2,
                     b_ins=[rows_k(df, tl)], b_fn=_ident,
                     outs=[(_sds((D_FF, D), F32), _bs((D, D), lambda i, j, k: (i, 0)))], acc_shape=(D, D))
    (dw_up,) = _mm("dw_mlp_up", "tn", (1, 4, nl),
                   a_ins=[rows_k(h2, tl)], a_fn=_ident,
                   b_ins=[(dup, _bs((tl, D), lambda i, j, k: (k, j)))], b_fn=_ident,
                   outs=[(_sds((N_CHIPS, D, D), F32), _bs((None, D, D), lambda i, j, k: (j, 0, 0)))],
                   acc_shape=(D, D))

    def mlp_norm_bwd(acc, x1b, dx2b, mixedb, g_mlp, g_mix):
        dxn, dg_mlp = _rms_bwd(x1b, g_mlp, acc)
        dx1b = dx2b + dxn
        dmixedb, dg_mix = _rms_bwd(mixedb.astype(F32), g_mix, dx1b)
        return dx1b, dmixedb, dg_mlp, dg_mix

    dx1, dmixed, dg_pre_mlp_p, dg_post_mix_p = _mm(
        "d_mlp_up", "nt", (nh, 1, 1),
        a_ins=[(dup, _bs((th, D_FF), lambda i, j, k: (i, 0)))], a_fn=_ident,
        b_ins=[full2(w_up_nat)], b_fn=_ident,
        epi_ins=[rows_i(x1, th), rows_i(dx2, th), rows_i(mixed, th), vec(g_pre_mlp), vec(g_post_mix)],
        epi_fn=mlp_norm_bwd,
        outs=[(_sds((seq, D), F32), _bs((th, D), lambda i, j, k: (i, 0))),
              (_sds((seq, D), BF16), _bs((th, D), lambda i, j, k: (i, 0))), part(nh), part(nh)],
        acc_shape=(th, D))
    (dw_o,) = _mm("dw_mix_out", "tn", (1, 1, nl),
                  a_ins=[rows_k(mixin, tl)], a_fn=_ident, b_ins=[rows_k(dmixed, tl)], b_fn=_ident,
                  outs=[(_sds((D, D), F32), _bs((D, D), lambda i, j, k: (0, 0)))], acc_shape=(D, D))

    def gate_bwd(acc, ga, gc, ya, yc, ba, bc, wao, wco):
        sa, sc = gate_values(ga, gc, ba, bc)
        dga = acc * ya.astype(F32) * sa * (1.0 - sa)
        dgc = acc * yc.astype(F32) * sc * (1.0 - sc)
        dya, dyc = (acc * sa).astype(BF16), (acc * sc).astype(BF16)
        return (dya, dyc, jnp.concatenate([dga, dgc], axis=1), _nt(dya, wao), _nt(dyc, wco),
                jnp.sum(dga, axis=0, keepdims=True), jnp.sum(dgc, axis=0, keepdims=True))

    dya, dyc, dgate, do, de, dbg_a_p, dbg_c_p = _mm(
        "d_mix_out", "nt", (nh, 1, 1),
        a_ins=[rows_i(dmixed, th)], a_fn=_ident, b_ins=[full2(w_o)], b_fn=_ident,
        epi_ins=mix_ins(rows_i) + [full2(w_ao), full2(w_co)], epi_fn=gate_bwd,
        outs=[(_sds((seq, D), BF16), _bs((th, D), lambda i, j, k: (i, 0)))] * 2
             + [(_sds((seq, 2 * D), BF16), _bs((th, 2 * D), lambda i, j, k: (i, 0))),
                (_sds((seq, ATTN_W), BF16), _bs((th, ATTN_W), lambda i, j, k: (i, 0))),
                (_sds((seq, CONV_W), F32), _bs((th, CONV_W), lambda i, j, k: (i, 0))), part(nh), part(nh)],
        acc_shape=(th, D))
    (dw_ao,) = _mm("dw_attn_out", "tn", (1, 1, nh),
                   a_ins=[(o, _bs((th, ATTN_W), lambda i, j, k: (k, 0)))], a_fn=_ident,
                   b_ins=[rows_k(dya, th)], b_fn=_ident,
                   outs=[(_sds((ATTN_W, D), F32), _bs((ATTN_W, D), lambda i, j, k: (0, 0)))], acc_shape=(ATTN_W, D))
    dq, dk, dv = _attn_bwd(proj, do, seq)
    (dw_co,) = _mm("dw_conv_out", "tn", (1, 1, nh),
                   a_ins=[(e, _bs((th, CONV_W), lambda i, j, k: (k, 0)))], a_fn=_ident,
                   b_ins=[rows_k(dyc, th)], b_fn=_ident,
                   outs=[(_sds((CONV_W, D), F32), _bs((CONV_W, D), lambda i, j, k: (0, 0)))], acc_shape=(CONV_W, D))
    dconv, dwc_p = _conv_bwd(proj, de, w_conv, seq, tm)
    qkv_w = 3 * ATTN_W
    join_bf16 = lambda *blocks: jnp.concatenate([b.astype(BF16) for b in blocks], axis=1)
    piece = lambda a, t, rows, blk=0: (a, _bs((t, a.shape[1]), (lambda i, j, k: (k, blk)) if rows == "k"
                                             else (lambda i, j, k: (i, blk))))
    (dw_in_qkv,) = _mm("dw_proj_in_qkv", "tn", (1, 1, ni),
                       a_ins=[rows_k(h1, tm)], a_fn=_ident,
                       b_ins=[piece(dq, tm, "k"), piece(dk, tm, "k"), piece(dv, tm, "k")], b_fn=join_bf16,
                       outs=[(_sds((D, qkv_w), F32), _bs((D, qkv_w), lambda i, j, k: (0, 0)))], acc_shape=(D, qkv_w))
    (dw_in_conv,) = _mm("dw_proj_in_conv", "tn", (1, 1, nl),
                        a_ins=[rows_k(h1, tl)], a_fn=_ident, b_ins=[piece(dconv, tl, "k")], b_fn=_ident,
                        outs=[(_sds((D, 3 * CONV_W), F32), _bs((D, 3 * CONV_W), lambda i, j, k: (0, 0)))],
                        acc_shape=(D, 3 * CONV_W))
    (dw_in_gate,) = _mm("dw_proj_in_gate", "tn", (1, 2, nl),
                        a_ins=[rows_k(h1, tl)], a_fn=_ident,
                        b_ins=[(dgate, _bs((tl, D), lambda i, j, k: (k, j)))], b_fn=_ident,
                        outs=[(_sds((D, 2 * D), F32), _bs((D, D), lambda i, j, k: (0, j)))], acc_shape=(D, D))
    dw_in = jnp.concatenate([dw_in_qkv, dw_in_conv, dw_in_gate], axis=1)

    def in_norm_bwd(acc, xb, dx1b, gb):
        dxn, dg = _rms_bwd(xb, gb, acc)
        return dx1b + dxn, dg

    grad_x, dg_pre_mix_p = _mm("d_proj_in", "nt", (nh, 1, 1),
                               a_ins=[piece(dq, th, "i"), piece(dk, th, "i"), piece(dv, th, "i"),
                                      piece(dconv, th, "i"), piece(dgate, th, "i")], a_fn=join_bf16,
                               b_ins=[full2(w_in_nat)], b_fn=_ident,
                               epi_ins=[rows_i(x, th), rows_i(dx1, th), vec(g_pre_mix)], epi_fn=in_norm_bwd,
                               outs=[(_sds((seq, D), F32), _bs((th, D), lambda i, j, k: (i, 0))), part(nh)],
                               acc_shape=(th, D))

    chip_major = lambda a: a.reshape(a.shape[0], N_CHIPS, a.shape[1] // N_CHIPS).transpose(1, 0, 2)
    big = [chip_major(dw_in), chip_major(dw_ao), chip_major(dw_co), dw_o.reshape(N_CHIPS, D // N_CHIPS, D), dw_up,
           dw_down.reshape(N_CHIPS, D_FF // N_CHIPS, D), dw_pg.reshape(N_CHIPS, D // N_CHIPS, D), chip_major(dw_pp)]
    small = (loss_p, [dg_pre_mix_p, dg_post_mix_p, dg_pre_mlp_p, dg_post_mlp_p, dg_ple_p], dbg_a_p, dbg_c_p, dwc_p)
    return grad_x, big, small


RS_GROUPS = ((0,), (4,), (5,), (1, 2, 3, 6, 7))


def _reduce_scatter(big):
    pair = [None] * len(big)
    for gi, group in enumerate(RS_GROUPS):
        for w, s in zip(group, _rs_pair_sum(f"rs_pair_sum_{gi}", [big[w] for w in group])):
            pair[w] = s
    return _rs_exchange_join(pair)


def kernel(x, p, g_pre_mix, w_in, b_gate, w_conv, w_attn_out, w_conv_out, w_o, g_post_mix, g_pre_mlp, w_up, w_down, g_post_mlp, g_ple, w_ple_gate, w_ple_proj, loss_target, m_g_pre_mix, m_w_in, m_b_gate, m_w_conv, m_w_attn_out, m_w_conv_out, m_w_o, m_g_post_mix, m_g_pre_mlp, m_w_up, m_w_down, m_g_post_mlp, m_g_ple, m_w_ple_gate, m_w_ple_proj, v_g_pre_mix, v_w_in, v_b_gate, v_w_conv, v_w_attn_out, v_w_conv_out, v_w_o, v_g_post_mix, v_g_pre_mlp, v_w_up, v_w_down, v_g_post_mlp, v_g_ple, v_w_ple_gate, v_w_ple_proj):
    mats = [w_in, w_attn_out, w_conv_out, w_o, w_up, w_down, w_ple_gate, w_ple_proj]
    mats_m = [m_w_in, m_w_attn_out, m_w_conv_out, m_w_o, m_w_up, m_w_down, m_w_ple_gate, m_w_ple_proj]
    mats_v = [v_w_in, v_w_attn_out, v_w_conv_out, v_w_o, v_w_up, v_w_down, v_w_ple_gate, v_w_ple_proj]
    gains = [g_pre_mix, g_post_mix, g_pre_mlp, g_post_mlp, g_ple]
    gains_m = [m_g_pre_mix, m_g_post_mix, m_g_pre_mlp, m_g_post_mlp, m_g_ple]
    gains_v = [v_g_pre_mix, v_g_post_mix, v_g_pre_mlp, v_g_post_mlp, v_g_ple]

    taps = jnp.concatenate([w_conv[0], jnp.zeros((CONV_PAD_ROWS - 3, LANES), F32)], axis=0)
    gathered = _allgather_weights([w[0].astype(BF16) for w in mats] + [taps])
    cols_joined = lambda a: a.transpose(1, 0, 2).reshape(a.shape[1], N_CHIPS * a.shape[2])
    rows_joined = lambda a: a.reshape(N_CHIPS * a.shape[1], a.shape[2])
    wf = [gathered[0], cols_joined(gathered[1]), cols_joined(gathered[2]), rows_joined(gathered[3]), gathered[4],
          rows_joined(gathered[5]), rows_joined(gathered[6]), cols_joined(gathered[7]),
          cols_joined(gathered[0]), cols_joined(gathered[4])]
    w_conv_full = cols_joined(gathered[8])[0:3, :]
    chip = 2 * lax.axis_index("x") + lax.axis_index("y")

    grad_x, big, small = _local_step(x[0], p[0, 0], loss_target[0], gains, b_gate, w_conv_full, wf)

    shard_grads = _reduce_scatter(big)
    red = _small_allreduce(*small)
    loss = red[0, 0]
    grad_gains = [red[1 + r:2 + r, :] for r in range(5)]
    grad_b_gate = jnp.concatenate([red[6:7, :], red[7:8, :]], axis=1)
    grad_w_conv = lax.dynamic_slice(red[8:11, :], (0, chip * LANES), (3, LANES))[None]

    grads_big = [gr.reshape(w.shape) for gr, w in zip(shard_grads, mats)]
    upd_big = [_adamw(f"adamw_{i}", w, gr, m, v) for i, (w, gr, m, v) in enumerate(zip(mats, grads_big, mats_m, mats_v))]
    pack = lambda vs, bg: jnp.concatenate(list(vs) + [bg.reshape(2, D_MODEL), jnp.zeros((1, D_MODEL), F32)], axis=0)
    upd_small = _adamw("adamw_small", pack(gains, b_gate), pack(grad_gains, grad_b_gate),
                       pack(gains_m, m_b_gate), pack(gains_v, v_b_gate))
    upd_conv = _adamw("adamw_conv", w_conv, grad_w_conv, m_w_conv, v_w_conv)

    def small_out(a, which):
        gains_out = [a[r:r + 1, :] for r in range(5)]
        return gains_out, a[5:7, :].reshape(1, 2 * D_MODEL)

    def ordered(g_pre_mix_, big_, b_gate_, conv_, g_rest):
        return [g_pre_mix_, big_[0], b_gate_, conv_, big_[1], big_[2], big_[3], g_rest[0], g_rest[1], big_[4], big_[5],
                g_rest[2], g_rest[3], big_[6], big_[7]]

    outs = [loss, grad_x[None]]
    outs += ordered(grad_gains[0], grads_big, grad_b_gate, grad_w_conv, grad_gains[1:])
    for which in range(3):
        g_out, b_out = small_out(upd_small[which], which)
        outs += ordered(g_out[0], [u[which] for u in upd_big], b_out, upd_conv[which], g_out[1:])
    return tuple(outs)
```

```python
import functools

import jax
import jax.numpy as jnp
from jax import lax
from jax.experimental import pallas as pl
from jax.experimental.pallas import tpu as pltpu

F32 = jnp.float32
BF16 = jnp.bfloat16
MESH = pl.DeviceIdType.MESH

D_MODEL = 1024
N_HEADS = 8
HEAD_DIM = 64
ATTN_W = N_HEADS * HEAD_DIM
CONV_W = 512
D_FF = 4096
PLE_DIM = 256
D_IN = 5120
N_CHIPS = 4
EPS = 1e-6
Q_SCALE = HEAD_DIM ** -0.5

ADAM_LR = 0.001
ADAM_B1 = 0.9
ADAM_B2 = 0.999
ADAM_EPS = 1e-08
ADAM_WD = 0.01
ADAM_STEP = 10

V7X_VMEM_BYTES = 64 * 1024 * 1024
VMEM_LIMIT = V7X_VMEM_BYTES - 8 * 1024 * 1024
LANES = 128
ATT_BLK = 256
SMALL_ROWS = 16
CONV_PAD_ROWS = 16


def _cparams(n_grid):
    return pltpu.CompilerParams(dimension_semantics=("arbitrary",) * n_grid, vmem_limit_bytes=VMEM_LIMIT)


def _bs(shape, fn):
    return pl.BlockSpec(shape, fn)


def _rms_stats(xf):
    return lax.rsqrt(jnp.mean(xf * xf, axis=-1, keepdims=True) + EPS)


def _rms(xf, g):
    return xf * _rms_stats(xf) * g


def _rms_bwd(xf, g, dy):
    r = _rms_stats(xf)
    xh = xf * r
    dyg = dy * g
    dx = r * (dyg - xh * jnp.mean(dyg * xh, axis=-1, keepdims=True))
    return dx, jnp.sum(dy * xh, axis=0, keepdims=True)


def _sig(z):
    return 1.0 / (1.0 + jnp.exp(-z))


def _ident(a):
    return a


def _to_bf16(a):
    return a.astype(BF16)


_DIMS = {"nn": (((1,), (0,)), ((), ())), "nt": (((1,), (1,)), ((), ())), "tn": (((0,), (0,)), ((), ()))}


def _mm(name, mode, grid, a_ins, a_fn, b_ins, b_fn, outs, acc_shape, epi_ins=(), epi_fn=None,
        a_cache=None, a_outs=(), epi_a=()):
    nk = grid[2]
    na, nb, ne, no, nao = len(a_ins), len(b_ins), len(epi_ins), len(outs), len(a_outs)
    assert a_cache is None or nk == 1
    assert not a_outs or a_cache is not None
    dims = _DIMS[mode]
    if epi_fn is None:
        epi_fn = lambda acc: (acc,)

    def body(*refs):
        a_refs = refs[:na]
        b_refs = refs[na:na + nb]
        e_refs = refs[na + nb:na + nb + ne]
        o_refs = refs[na + nb + ne:na + nb + ne + no]
        ao_refs = refs[na + nb + ne + no:na + nb + ne + no + nao]
        scratch = list(refs[na + nb + ne + no + nao:])
        acc_ref = scratch.pop(0) if nk > 1 else None
        a_sc = scratch.pop(0) if a_cache is not None else None
        j = pl.program_id(1)
        k = pl.program_id(2)

        def finish(acc):
            res = epi_fn(acc, *[a_refs[t][...] for t in epi_a], *[r[...] for r in e_refs])
            for r, val in zip(o_refs, res):
                r[...] = val.astype(r.dtype)

        if a_sc is not None:
            @pl.when(j == 0)
            def _():
                res = a_fn(*[r[...] for r in a_refs])
                if nao:
                    for r, val in zip(ao_refs, res[1:]):
                        r[...] = val.astype(r.dtype)
                    res = res[0]
                a_sc[...] = res
            a = a_sc[...]
        else:
            a = a_fn(*[r[...] for r in a_refs])
        b = b_fn(*[r[...] for r in b_refs])
        prod = lax.dot_general(a, b, dims, preferred_element_type=F32)
        if nk == 1:
            finish(prod)
        else:
            @pl.when(k == 0)
            def _():
                acc_ref[...] = prod

            @pl.when(k > 0)
            def _():
                acc_ref[...] += prod

            @pl.when(k == nk - 1)
            def _():
                finish(acc_ref[...])

    scratch_shapes = []
    if nk > 1:
        scratch_shapes.append(pltpu.VMEM(acc_shape, F32))
    if a_cache is not None:
        scratch_shapes.append(pltpu.VMEM(*a_cache))
    all_outs = list(outs) + list(a_outs)
    res = pl.pallas_call(
        body, name=name, grid=grid,
        in_specs=[s for _, s in a_ins] + [s for _, s in b_ins] + [s for _, s in epi_ins],
        out_specs=[s for _, s in all_outs],
        out_shape=[o for o, _ in all_outs],
        scratch_shapes=scratch_shapes,
        compiler_params=_cparams(3),
    )(*[a for a, _ in a_ins], *[a for a, _ in b_ins], *[a for a, _ in epi_ins])
    return res


def _sds(shape, dtype):
    return jax.ShapeDtypeStruct(shape, dtype)


def _nt(a, b):
    return lax.dot_general(a, b, _DIMS["nt"], preferred_element_type=F32)


def _tn(a, b):
    return lax.dot_general(a, b, _DIMS["tn"], preferred_element_type=F32)


def _nn(a, b):
    return lax.dot_general(a, b, _DIMS["nn"], preferred_element_type=F32)


def _mlp_down_ple_head(up, x1, p, tgt, g_ple, g_post_mlp, w_down, w_pg, w_pp, seq, tr):
    nblk = seq // tr
    D = D_MODEL

    def body(up_ref, x1_ref, p_ref, t_ref, gp_ref, gm_ref, wd_ref, wpg_ref, wpp_ref,
             dx2_ref, df_ref, dpre_ref, h3_ref, dpp_ref, loss_ref, dgp_ref, dgm_ref):
        gp, gm, wpg, wpp = gp_ref[...], gm_ref[...], wpg_ref[...], wpp_ref[...]
        halves = [pl.ds(0, tr // 2), pl.ds(tr // 2, tr // 2)]
        w_down = wd_ref[...]
        fb = []
        for r in halves:
            hidden = jnp.maximum(up_ref[r, :].astype(F32), 0.0)
            fb.append(_nn((hidden * hidden).astype(BF16), w_down))
        x2b = [x1_ref[r, :] + _rms(fb[s], gm) for s, r in enumerate(halves)]
        h3 = [_rms(x, gp).astype(BF16) for x in x2b]
        gate = [_sig(_nn(h, wpg)) for h in h3]
        pp = [_nn(p_ref[r, :].astype(BF16), wpp) for r in halves]
        err = [x2b[s] + gate[s] * pp[s] - t_ref[r, :] for s, r in enumerate(halves)]
        dx3 = [e * (1.0 / D) for e in err]
        dpre = [(dx3[s] * pp[s] * gate[s] * (1.0 - gate[s])).astype(BF16) for s in range(2)]
        dh3 = [_nt(d, wpg) for d in dpre]
        loss, dgp_sum, dgm_sum = 0.0, 0.0, 0.0
        for s, r in enumerate(halves):
            h3_ref[r, :] = h3[s]
            dpp_ref[r, :] = (dx3[s] * gate[s]).astype(BF16)
            dpre_ref[r, :] = dpre[s]
            dxn, dgp = _rms_bwd(x2b[s], gp, dh3[s])
            dx2 = dx3[s] + dxn
            dx2_ref[r, :] = dx2
            dfb, dgm = _rms_bwd(fb[s], gm, dx2)
            df_ref[r, :] = dfb.astype(BF16)
            loss = loss + jnp.sum(err[s] * err[s], axis=0, keepdims=True)
            dgp_sum, dgm_sum = dgp_sum + dgp, dgm_sum + dgm
        loss_ref[...] = loss * (0.5 / D)
        dgp_ref[...] = dgp_sum
        dgm_ref[...] = dgm_sum

    rows = _bs((tr, D), lambda i: (i, 0))
    vec = _bs((1, D), lambda i: (0, 0))
    part = _bs((None, 1, D), lambda i: (i, 0, 0))
    return pl.pallas_call(
        body, name="mlp_down_ple_head", grid=(nblk,),
        in_specs=[_bs((tr, D_FF), lambda i: (i, 0)), rows, _bs((tr, PLE_DIM), lambda i: (i, 0)), rows, vec, vec,
                  _bs((D_FF, D), lambda i: (0, 0)), _bs((D, D), lambda i: (0, 0)), _bs((PLE_DIM, D), lambda i: (0, 0))],
        out_specs=[rows] * 5 + [part] * 3,
        out_shape=[_sds((seq, D), F32)] + [_sds((seq, D), BF16)] * 4 + [_sds((nblk, 1, D), F32)] * 3,
        compiler_params=_cparams(1),
    )(up, x1, p, tgt, g_ple, g_post_mlp, w_down, w_pg, w_pp)


def _shift_rows_down(u, prev, n):
    rows = u.shape[0]
    ridx = lax.broadcasted_iota(jnp.int32, u.shape, 0)
    out = pltpu.roll(u, n, 0)
    for r in range(n):
        out = jnp.where(ridx == r, prev[8 - n + r:8 - n + r + 1, :], out)
    del rows
    return out


def _shift_rows_up(u, nxt, n):
    rows = u.shape[0]
    ridx = lax.broadcasted_iota(jnp.int32, u.shape, 0)
    out = pltpu.roll(u, rows - n, 0)
    for r in range(n):
        out = jnp.where(ridx == rows - n + r, nxt[r:r + 1, :], out)
    return out


CONV_COL0 = 3


def _conv_fwd(proj, w_conv, seq, tr):
    hb = tr // 8

    def body(cb_ref, cc_ref, cu_ref, ccp_ref, cup_ref, w_ref, e_ref):
        i = pl.program_id(0)
        u = cc_ref[...] * cu_ref[...]
        up = jnp.where(i > 0, ccp_ref[...] * cup_ref[...], 0.0)
        w = w_ref[...]
        d = w[0:1, :] * _shift_rows_down(u, up, 2) + w[1:2, :] * _shift_rows_down(u, up, 1) + w[2:3, :] * u
        e_ref[...] = (cb_ref[...] * d).astype(BF16)

    prev = lambda c: (lambda i: (jnp.maximum(i * hb - 1, 0), c))
    return pl.pallas_call(
        body, name="conv_fwd", grid=(seq // tr,),
        in_specs=[_bs((tr, CONV_W), lambda i: (i, CONV_COL0)),
                  _bs((tr, CONV_W), lambda i: (i, CONV_COL0 + 1)),
                  _bs((tr, CONV_W), lambda i: (i, CONV_COL0 + 2)),
                  _bs((8, CONV_W), prev(CONV_COL0 + 1)),
                  _bs((8, CONV_W), prev(CONV_COL0 + 2)),
                  _bs((3, CONV_W), lambda i: (0, 0))],
        out_specs=_bs((tr, CONV_W), lambda i: (i, 0)),
        out_shape=_sds((seq, CONV_W), BF16),
        compiler_params=_cparams(1),
    )(proj, proj, proj, proj, proj, w_conv)


def _conv_bwd(proj, de, w_conv, seq, tr):
    hb = tr // 8
    nblk = seq // tr

    def body(cb_ref, cc_ref, cu_ref, ccp_ref, cup_ref, cbn_ref, de_ref, den_ref, w_ref, o_ref, dw_ref):
        i = pl.program_id(0)
        cc, cu, cb = cc_ref[...], cu_ref[...], cb_ref[...]
        u = cc * cu
        up = jnp.where(i > 0, ccp_ref[...] * cup_ref[...], 0.0)
        u1 = _shift_rows_down(u, up, 1)
        u2 = _shift_rows_down(u, up, 2)
        de_ = de_ref[...]
        dd = de_ * cb
        ddn = jnp.where(i < nblk - 1, den_ref[...] * cbn_ref[...], 0.0)
        w = w_ref[...]
        du = w[2:3, :] * dd + w[1:2, :] * _shift_rows_up(dd, ddn, 1) + w[0:1, :] * _shift_rows_up(dd, ddn, 2)
        o_ref[:, 0:CONV_W] = (de_ * (w[0:1, :] * u2 + w[1:2, :] * u1 + w[2:3, :] * u)).astype(BF16)
        o_ref[:, CONV_W:2 * CONV_W] = (du * cu).astype(BF16)
        o_ref[:, 2 * CONV_W:3 * CONV_W] = (du * cc).astype(BF16)
        ridx = lax.broadcasted_iota(jnp.int32, (8, CONV_W), 0)
        dw0 = jnp.sum(dd * u2, axis=0, keepdims=True)
        dw1 = jnp.sum(dd * u1, axis=0, keepdims=True)
        dw2 = jnp.sum(dd * u, axis=0, keepdims=True)
        dw_ref[...] = jnp.where(ridx == 0, dw0, jnp.where(ridx == 1, dw1, jnp.where(ridx == 2, dw2, 0.0)))

    prev = lambda c: (lambda i: (jnp.maximum(i * hb - 1, 0), c))
    nxt = lambda c: (lambda i: (jnp.minimum((i + 1) * hb, seq // 8 - 1), c))
    return pl.pallas_call(
        body, name="conv_bwd", grid=(nblk,),
        in_specs=[_bs((tr, CONV_W), lambda i: (i, CONV_COL0)),
                  _bs((tr, CONV_W), lambda i: (i, CONV_COL0 + 1)),
                  _bs((tr, CONV_W), lambda i: (i, CONV_COL0 + 2)),
                  _bs((8, CONV_W), prev(CONV_COL0 + 1)),
                  _bs((8, CONV_W), prev(CONV_COL0 + 2)),
                  _bs((8, CONV_W), nxt(CONV_COL0)),
                  _bs((tr, CONV_W), lambda i: (i, 0)),
                  _bs((8, CONV_W), nxt(0)),
                  _bs((3, CONV_W), lambda i: (0, 0))],
        out_specs=[_bs((tr, 3 * CONV_W), lambda i: (i, 0)), _bs((None, 8, CONV_W), lambda i: (i, 0, 0))],
        out_shape=[_sds((seq, 3 * CONV_W), BF16), _sds((nblk, 8, CONV_W), F32)],
        compiler_params=_cparams(1),
    )(proj, proj, proj, proj, proj, proj, de, de, w_conv)


def _log_gates(z):
    lse = jnp.log(1.0 + jnp.exp(-jnp.abs(z)))
    log_beta = jnp.minimum(z, 0.0) - lse
    return log_beta, log_beta - z


DEAD_LOG_WEIGHT = -110.0
NO_TILE = -1e30


def _first_live_tile(start, scores, live_sc):
    def alive():
        return jnp.max(jnp.maximum(live_sc[0], live_sc[1])) > DEAD_LOG_WEIGHT

    def step(c):
        for h, z in enumerate(scores(c[0])):
            live_sc[h] = live_sc[h] + jnp.sum(_log_gates(z)[1], axis=-1, keepdims=True)
        return c[0] - 1, alive()

    j_end, _ = lax.while_loop(lambda c: jnp.logical_and(c[0] >= 0, c[1]), step, (start, alive()))
    return j_end + 1


def _attn_fwd(proj, seq):
    blk = ATT_BLK
    nq = seq // blk
    npair = N_HEADS // 2

    def body(q_ref, qn_ref, k_ref, v_ref, o_ref, z0_sc, z1_sc, w0_sc, w1_sc, tot_sc, live_sc, acc_sc, early_sc):
        i = pl.program_id(1)
        is_a = lax.broadcasted_iota(jnp.int32, (1, LANES), 1) < HEAD_DIM
        q2 = (q_ref[...] * Q_SCALE).astype(BF16)
        zero = jnp.zeros_like(q2)
        qs = (jnp.where(is_a, q2, zero), jnp.where(is_a, zero, q2))
        row = lax.broadcasted_iota(jnp.int32, (blk, blk), 0)
        col = lax.broadcasted_iota(jnp.int32, (blk, blk), 1)
        tri = (row > col).astype(BF16)
        causal = col < row

        def tile_of(ref, j):
            return ref[pl.ds(pl.multiple_of(j * blk, blk), blk), :].astype(BF16)

        def scores(j):
            k2 = tile_of(k_ref, j)
            return [_nt(qs[h], k2) for h in range(2)]

        has_left = i > 0
        left = jnp.maximum(i - 1, 0)

        def early_scores(q_block, diag_tile):
            q_heads = (jnp.where(is_a, q_block, zero), jnp.where(is_a, zero, q_block))
            for n, j in enumerate((diag_tile, jnp.maximum(diag_tile - 1, 0))):
                k2 = tile_of(k_ref, j)
                for h in range(2):
                    early_sc[2 * n + h] = _nt(q_heads[h], k2)

        @pl.when(i == 0)
        def _():
            early_scores(q2, i)

        g_d = [_log_gates(early_sc[h]) for h in range(2)]
        g_l = [_log_gates(early_sc[2 + h]) for h in range(2)]
        early_scores((qn_ref[...] * Q_SCALE).astype(BF16), jnp.minimum(i + 1, nq - 1))
        keep_d = [jnp.where(causal, g[1], 0.0) for g in g_d]
        suf_d = [_nn(lk.astype(BF16), tri) for lk in keep_d]
        suf_l = [_nn(g[1].astype(BF16), tri) for g in g_l]
        v_d, v_l = tile_of(v_ref, i), tile_of(v_ref, left)
        pv = []
        for h in range(2):
            sum_d = jnp.sum(keep_d[h], axis=-1, keepdims=True)
            w_d = jnp.where(causal, jnp.exp(g_d[h][0] + suf_d[h]), 0.0)
            w_l = jnp.exp(g_l[h][0] + (jnp.where(has_left, sum_d, NO_TILE) + suf_l[h]))
            pv.append(_nn(w_d.astype(BF16), v_d) + _nn(w_l.astype(BF16), v_l))
            tot_sc[h] = sum_d + jnp.sum(g_l[h][1], axis=-1, keepdims=True)
        acc_sc[...] = jnp.where(is_a, pv[0], pv[1])

        live_sc[...] = tot_sc[...]
        first = _first_live_tile(i - 2, scores, live_sc)
        trips = i - 1 - first
        z_bufs, w_bufs = (z0_sc, z1_sc), (w0_sc, w1_sc)

        def put(ref, vals):
            for h in range(2):
                ref[h] = vals[h]

        def weights(zs):
            gates = [_log_gates(z) for z in zs]
            sums = [_nn(g[1].astype(BF16), tri) for g in gates]
            ws = []
            for h in range(2):
                ws.append(jnp.exp(gates[h][0] + (tot_sc[h] + sums[h])).astype(BF16))
                tot_sc[h] = tot_sc[h] + jnp.sum(gates[h][1], axis=-1, keepdims=True)
            return ws

        def add_values(w_buf, j):
            v2 = tile_of(v_ref, j)
            acc_sc[...] += jnp.where(is_a, _nn(w_buf[0], v2), _nn(w_buf[1], v2))

        def trip(j, s):
            add_values(w_bufs[s], j + 1)
            put(z_bufs[1 - s], scores(jnp.maximum(j - 1, first)))
            put(w_bufs[1 - s], weights((z_bufs[s][0], z_bufs[s][1])))

        @pl.when(trips > 0)
        def _():
            put(z0_sc, scores(i - 2))
            w0_sc[...] = jnp.zeros_like(w0_sc)

            def two_trips(pp, carry):
                j = i - 2 - 2 * pp
                trip(j, 0)
                trip(j - 1, 1)
                return carry

            lax.fori_loop(0, trips // 2, two_trips, 0)
            odd = trips % 2 == 1

            @pl.when(odd)
            def _():
                trip(first, 0)
                add_values(w1_sc, first)

            @pl.when(jnp.logical_not(odd))
            def _():
                add_values(w0_sc, first)

        o_ref[...] = acc_sc[...].astype(BF16)

    return pl.pallas_call(
        body, name="attn_fwd", grid=(npair, nq),
        in_specs=[_bs((blk, LANES), lambda p, i: (i, p)),
                  _bs((blk, LANES), lambda p, i: (jnp.minimum(i + 1, nq - 1), p)),
                  _bs((seq, LANES), lambda p, i: (0, npair + p)),
                  _bs((seq, LANES), lambda p, i: (0, 2 * npair + p))],
        out_specs=_bs((blk, LANES), lambda p, i: (i, p)),
        out_shape=_sds((seq, ATTN_W), BF16),
        scratch_shapes=[pltpu.VMEM((2, blk, blk), F32), pltpu.VMEM((2, blk, blk), F32),
                        pltpu.VMEM((2, blk, blk), BF16), pltpu.VMEM((2, blk, blk), BF16),
                        pltpu.VMEM((2, blk, 1), F32), pltpu.VMEM((2, blk, 1), F32), pltpu.VMEM((blk, LANES), F32),
                        pltpu.VMEM((4, blk, blk), F32)],
        compiler_params=_cparams(2),
    )(proj, proj, proj, proj)


def _attn_bwd(proj, do, seq):
    blk = ATT_BLK
    nq = seq // blk
    npair = N_HEADS // 2

    def body(q_ref, qn_ref, k_ref, v_ref, do_ref, don_ref, dq_ref, dk_ref, dv_ref,
             prod0_sc, prod1_sc, pend0_sc, pend1_sc, tot_sc, live_sc, cum_sc, pre_sc, dq_sc, early_sc):
        i = pl.program_id(1)

        @pl.when(i == 0)
        def _():
            dk_ref[...] = jnp.zeros_like(dk_ref)
            dv_ref[...] = jnp.zeros_like(dv_ref)

        is_a = lax.broadcasted_iota(jnp.int32, (1, LANES), 1) < HEAD_DIM
        q2 = (q_ref[...] * Q_SCALE).astype(BF16)
        do2 = do_ref[...]
        zero = jnp.zeros_like(q2)
        qs = (jnp.where(is_a, q2, zero), jnp.where(is_a, zero, q2))
        dos = (jnp.where(is_a, do2, zero), jnp.where(is_a, zero, do2))
        row = lax.broadcasted_iota(jnp.int32, (blk, blk), 0)
        col = lax.broadcasted_iota(jnp.int32, (blk, blk), 1)
        tri_after = (row > col).astype(BF16)
        tri_excl = (row < col).astype(BF16)
        causal = col < row

        def tile_of(ref, j):
            return ref[pl.ds(pl.multiple_of(j * blk, blk), blk), :].astype(BF16)

        def scores(j):
            k2 = tile_of(k_ref, j)
            return [_nt(qs[h], k2) for h in range(2)]

        def products(j):
            v2 = tile_of(v_ref, j)
            return scores(j) + [_nt(dos[h], v2) for h in range(2)]

        def row_sum(a):
            return jnp.sum(a, axis=-1, keepdims=True)

        def grad_matmuls(ws, dzs, j):
            rows = pl.ds(pl.multiple_of(j * blk, blk), blk)
            k2 = tile_of(k_ref, j)
            dq_sc[...] += jnp.where(is_a, _nn(dzs[0], k2), _nn(dzs[1], k2))
            dk_ref[rows, :] += jnp.where(is_a, _tn(dzs[0], q2), _tn(dzs[1], q2))
            if ws is not None:
                dv_ref[rows, :] += jnp.where(is_a, _tn(ws[0], do2), _tn(ws[1], do2))

        has_left = i > 0
        left = jnp.maximum(i - 1, 0)

        def early_products(q_block, do_block, diag_tile):
            q_heads = (jnp.where(is_a, q_block, zero), jnp.where(is_a, zero, q_block))
            do_heads = (jnp.where(is_a, do_block, zero), jnp.where(is_a, zero, do_block))
            for n, j in enumerate((diag_tile, jnp.maximum(diag_tile - 1, 0))):
                k2, v2 = tile_of(k_ref, j), tile_of(v_ref, j)
                for h in range(2):
                    early_sc[4 * n + h] = _nt(q_heads[h], k2)
                    early_sc[4 * n + 2 + h] = _nt(do_heads[h], v2)

        @pl.when(i == 0)
        def _():
            early_products(q2, do2, i)

        p_d = [early_sc[n] for n in range(4)]
        p_l = [early_sc[4 + n] for n in range(4)]
        g_d = [_log_gates(z) for z in p_d[:2]]
        g_l = [_log_gates(z) for z in p_l[:2]]
        early_products((qn_ref[...] * Q_SCALE).astype(BF16), don_ref[...], jnp.minimum(i + 1, nq - 1))
        keep_d = [jnp.where(causal, g[1], 0.0) for g in g_d]
        suf_d = [_nn(lk.astype(BF16), tri_after) for lk in keep_d]
        suf_l = [_nn(g[1].astype(BF16), tri_after) for g in g_l]
        w_d, w_l, gg_d, gg_l = [], [], [], []
        for h in range(2):
            sum_d = row_sum(keep_d[h])
            w_d.append(jnp.where(causal, jnp.exp(g_d[h][0] + suf_d[h]), 0.0))
            w_l.append(jnp.exp(g_l[h][0] + (jnp.where(has_left, sum_d, NO_TILE) + suf_l[h])))
            gg_d.append(p_d[2 + h] * w_d[h])
            gg_l.append(p_l[2 + h] * w_l[h])
            tot_sc[h] = sum_d + row_sum(g_l[h][1])
        before_d = [_nn(g.astype(BF16), tri_excl) for g in gg_d]
        before_l = [_nn(g.astype(BF16), tri_excl) for g in gg_l]
        dz_d, dz_l = [], []
        for h in range(2):
            beta_d, beta_l = jnp.exp(g_d[h][0]), jnp.exp(g_l[h][0])
            dz_l.append((gg_l[h] * (1.0 - beta_l) - before_l[h] * beta_l).astype(BF16))
            dz = gg_d[h] * (1.0 - beta_d) - (row_sum(gg_l[h]) + before_d[h]) * beta_d
            dz_d.append(jnp.where(causal, dz, 0.0).astype(BF16))
        dq_sc[...] = jnp.zeros_like(dq_sc)
        grad_matmuls([w.astype(BF16) for w in w_l], dz_l, left)
        grad_matmuls([w.astype(BF16) for w in w_d], dz_d, i)

        live_sc[...] = tot_sc[...]
        first = _first_live_tile(i - 2, scores, live_sc)
        trips = i - 1 - first
        prod_bufs, pend_bufs = (prod0_sc, prod1_sc), (pend0_sc, pend1_sc)

        def local_grads(prods):
            zs, dws = prods[:2], prods[2:]
            gates = [_log_gates(z) for z in zs]
            sums = [_nn(g[1].astype(BF16), tri_after) for g in gates]
            ws, gs = [], []
            for h in range(2):
                cum = cum_sc[h] + row_sum(gates[h][1])
                cum_sc[h] = cum
                ws.append(jnp.exp(gates[h][0] + ((live_sc[h] - cum) + sums[h])))
                gs.append(dws[h] * ws[h])
            befores = [_nn(g.astype(BF16), tri_excl) for g in gs]
            dzs = []
            for h in range(2):
                beta = jnp.exp(gates[h][0])
                dzs.append((gs[h] * (1.0 - beta) - (pre_sc[h] + befores[h]) * beta).astype(BF16))
                pre_sc[h] = pre_sc[h] + row_sum(gs[h])
            return [w.astype(BF16) for w in ws] + dzs

        def put(ref, vals):
            for n, val in enumerate(vals):
                ref[n] = val

        def flush(pend, j):
            grad_matmuls([pend[0], pend[1]], [pend[2], pend[3]], j)

        def trip(j, s):
            flush(pend_bufs[s], jnp.maximum(j - 1, first))
            put(prod_bufs[1 - s], products(j + 1))
            put(pend_bufs[1 - s], local_grads([prod_bufs[s][n] for n in range(4)]))

        def earlier_keys_share(j, mask):
            dzs = []
            for h, z in enumerate(scores(j)):
                beta = jnp.exp(_log_gates(z)[0])
                dzs.append(jnp.where(mask, -pre_sc[h] * beta, 0.0).astype(BF16))
            grad_matmuls(None, dzs, j)

        @pl.when(trips > 0)
        def _():
            cum_sc[...] = jnp.zeros_like(cum_sc)
            pre_sc[...] = jnp.zeros_like(pre_sc)
            pend0_sc[...] = jnp.zeros_like(pend0_sc)
            put(prod0_sc, products(first))

            def two_trips(pp, carry):
                trip(first + 2 * pp, 0)
                trip(first + 2 * pp + 1, 1)
                return carry

            lax.fori_loop(0, trips // 2, two_trips, 0)
            odd = trips % 2 == 1

            @pl.when(odd)
            def _():
                trip(i - 2, 0)
                flush(pend1_sc, i - 2)

            @pl.when(jnp.logical_not(odd))
            def _():
                flush(pend0_sc, i - 2)

            earlier_keys_share(i - 1, True)
            earlier_keys_share(i, causal)

        dq_ref[...] = dq_sc[...] * Q_SCALE

    qmap = lambda p, i: (i, p)
    next_map = lambda p, i: (jnp.minimum(i + 1, nq - 1), p)
    return pl.pallas_call(
        body, name="attn_bwd", grid=(npair, nq),
        in_specs=[_bs((blk, LANES), qmap), _bs((blk, LANES), next_map),
                  _bs((seq, LANES), lambda p, i: (0, npair + p)),
                  _bs((seq, LANES), lambda p, i: (0, 2 * npair + p)),
                  _bs((blk, LANES), qmap), _bs((blk, LANES), next_map)],
        out_specs=[_bs((blk, LANES), qmap),
                   _bs((seq, LANES), lambda p, i: (0, p)),
                   _bs((seq, LANES), lambda p, i: (0, p))],
        out_shape=[_sds((seq, ATTN_W), F32)] * 3,
        scratch_shapes=[pltpu.VMEM((4, blk, blk), F32), pltpu.VMEM((4, blk, blk), F32),
                        pltpu.VMEM((4, blk, blk), BF16), pltpu.VMEM((4, blk, blk), BF16),
                        pltpu.VMEM((2, blk, 1), F32), pltpu.VMEM((2, blk, 1), F32), pltpu.VMEM((2, blk, 1), F32),
                        pltpu.VMEM((2, blk, 1), F32), pltpu.VMEM((blk, LANES), F32),
                        pltpu.VMEM((8, blk, blk), F32)],
        compiler_params=_cparams(2),
    )(proj, proj, proj, proj, do, do)


def _elementwise(name, fn, ins, out_dtypes):
    rows, cols = ins[0].shape
    tr = rows
    for cand in (512, 256, 128, 64, 32, 16, 8):
        if rows % cand == 0 and cand * cols * 4 <= 2 * 1024 * 1024:
            tr = cand
            break
    n_in = len(ins)

    def body(*refs):
        res = fn(*[r[...] for r in refs[:n_in]])
        for r, val in zip(refs[n_in:], res):
            r[...] = val.astype(r.dtype)

    spec = _bs((tr, cols), lambda i: (i, 0))
    return pl.pallas_call(
        body, name=name, grid=(rows // tr,),
        in_specs=[spec] * n_in, out_specs=[spec] * len(out_dtypes),
        out_shape=[_sds((rows, cols), dt) for dt in out_dtypes],
        compiler_params=_cparams(1),
    )(*ins)


def _adamw_fn(w, g, m, v):
    m = ADAM_B1 * m + (1.0 - ADAM_B1) * g
    v = ADAM_B2 * v + (1.0 - ADAM_B2) * (g * g)
    m_hat = m / (1.0 - ADAM_B1 ** ADAM_STEP)
    v_hat = v / (1.0 - ADAM_B2 ** ADAM_STEP)
    delta = -ADAM_LR * (m_hat / (jnp.sqrt(v_hat) + ADAM_EPS) + ADAM_WD * w)
    return delta, m, v


def _adamw(name, w, g, m, v):
    shape = w.shape
    as2d = lambda a: a.reshape(-1, shape[-1])
    delta, nm, nv = _elementwise(name, _adamw_fn, [as2d(w), as2d(g), as2d(m), as2d(v)], [F32, F32, F32])
    return delta.reshape(shape), nm.reshape(shape), nv.reshape(shape)


def _place():
    x, y, c = lax.axis_index("x"), lax.axis_index("y"), lax.axis_index("c")
    chips = [(1 - x, y), (x, 1 - y), (1 - x, 1 - y)]
    return x, y, c, chips


ANY = pl.BlockSpec(memory_space=pl.ANY)
VMEM_WHOLE = pl.BlockSpec(memory_space=pltpu.VMEM)


def _allgather_weights(shards):
    n = len(shards)

    def body(*refs):
        src, dst = refs[:n], refs[n:2 * n]
        send_sems, recv_sems, local_sems = refs[2 * n:]
        x, y, c, chips = _place()
        me, sibling, mychip = (x, y, c), (x, y, 1 - c), 2 * x + y

        x_nbr, y_nbr, diag = 2 * (1 - x) + y, 2 * x + (1 - y), 2 * (1 - x) + (1 - y)
        to_x, to_y = (1 - x, y, c), (x, 1 - y, c)

        def parts(w):
            hr = src[w].shape[0] // 2
            first = hr // 2 if hr % 32 == 0 else hr
            return first, hr - first

        def rows_of(w, chip, half, route):
            hr = src[w].shape[0] // 2
            first, second = parts(w)
            start, size = {0: (0, hr), 1: (0, hr), 2: (0, first), 3: (first, second)}[route]
            return dst[w].at[chip, pl.ds(half * hr + start, size)]

        def copy(w, k, src_ref, dst_ref, to):
            return pltpu.make_async_remote_copy(src_ref=src_ref, dst_ref=dst_ref, send_sem=send_sems.at[w, k],
                                                recv_sem=recv_sems.at[w, k], device_id=to, device_id_type=MESH)

        def landed(w, route):
            chip = {0: x_nbr, 1: y_nbr, 2: diag, 3: diag}[route]
            return rows_of(w, chip, c, route), chip

        def routes(w):
            return (0, 1, 2, 3) if parts(w)[1] else (0, 1, 2)

        started, local = [], []
        for w in range(n):
            hr = src[w].shape[0] // 2
            own = pltpu.make_async_copy(src[w], dst[w].at[mychip], local_sems.at[w])
            own.start()
            local.append(own)
            mine = src[w].at[pl.ds(c * hr, hr)]
            for route, to in ((0, to_x), (1, to_y)):
                cp = copy(w, route, mine, rows_of(w, mychip, c, route), to)
                cp.start()
                started.append(cp)

        def pass_on(w, route):
            got, chip = landed(w, route)
            copy(w, route, got, got, me).wait_recv()
            if route == 1:
                part = rows_of(w, chip, c, 2)
                started.append(copy(w, 2, part, part, to_x))
                started[-1].start()
            if route == 0 and parts(w)[1]:
                part = rows_of(w, chip, c, 3)
                started.append(copy(w, 3, part, part, to_y))
                started[-1].start()
            started.append(copy(w, 4 + route, got, got, sibling))
            started[-1].start()

        for w in range(n):
            pass_on(w, 1)
            pass_on(w, 0)
        for w in range(n):
            for route in routes(w)[2:]:
                pass_on(w, route)
        for w in range(n):
            for route in routes(w):
                chip = landed(w, route)[1]
                from_sib = rows_of(w, chip, 1 - c, route)
                copy(w, 4 + route, from_sib, from_sib, me).wait_recv()
        for cp in local:
            cp.wait()
        for cp in started:
            cp.wait_send()

    return pl.pallas_call(
        body, name="allgather_weights",
        in_specs=[VMEM_WHOLE] * n, out_specs=[VMEM_WHOLE] * n,
        out_shape=[_sds((N_CHIPS,) + s.shape, s.dtype) for s in shards],
        scratch_shapes=[pltpu.SemaphoreType.DMA((n, 8)), pltpu.SemaphoreType.DMA((n, 8)),
                        pltpu.SemaphoreType.DMA((n,))],
        compiler_params=pltpu.CompilerParams(vmem_limit_bytes=VMEM_LIMIT),
    )(*shards)


SUM_ROWS = 64


def _rs_pair_sum(name, grads):
    n = len(grads)

    def body(*refs):
        g, out = refs[:n], refs[n:2 * n]
        stage, land, keep = refs[2 * n:3 * n], refs[3 * n:4 * n], refs[4 * n:5 * n]
        send_sems, recv_sems, stage_sems, keep_sems = refs[5 * n:]
        x, y, c, _ = _place()
        sibling = (x, y, 1 - c)
        loads = []
        for w in range(n):
            hr = g[w].shape[1] // 2
            st = pltpu.make_async_copy(g[w].at[:, pl.ds((1 - c) * hr, hr)], stage[w], stage_sems.at[w])
            kp = pltpu.make_async_copy(g[w].at[:, pl.ds(c * hr, hr)], keep[w], keep_sems.at[w])
            st.start()
            kp.start()
            loads.append((st, kp))
        gives = []
        for w in range(n):
            loads[w][0].wait()
            give = pltpu.make_async_remote_copy(src_ref=stage[w], dst_ref=land[w], send_sem=send_sems.at[w],
                                                recv_sem=recv_sems.at[w], device_id=sibling, device_id_type=MESH)
            give.start()
            gives.append(give)
        for w in range(n):
            loads[w][1].wait()
            gives[w].wait_recv()
            nb = g[w].shape[1] // 2 // SUM_ROWS

            def add(idx, carry, w=w, nb=nb):
                k, r = idx // nb, pl.multiple_of((idx % nb) * SUM_ROWS, SUM_ROWS)
                rows = pl.ds(r, SUM_ROWS)
                out[w][k, rows, :] = (keep[w][k, rows, :] + land[w][k, rows, :]).astype(BF16)
                return carry

            lax.fori_loop(0, N_CHIPS * nb, add, 0)
        for give in gives:
            give.wait_send()

    half = [(N_CHIPS, a.shape[1] // 2, a.shape[2]) for a in grads]
    bufs = [pltpu.VMEM(s, F32) for s in half]
    sems = pltpu.SemaphoreType.DMA((n,))
    return pl.pallas_call(
        body, name=name,
        in_specs=[ANY] * n, out_specs=[VMEM_WHOLE] * n, out_shape=[_sds(s, BF16) for s in half],
        scratch_shapes=bufs + bufs + bufs + [sems, sems, sems, sems],
        compiler_params=pltpu.CompilerParams(vmem_limit_bytes=VMEM_LIMIT),
    )(*grads)


def _rs_exchange_join(parts):
    n = len(parts)

    def body(*refs):
        t, full = refs[:n], refs[n:2 * n]
        got_x, got_y, pass_on, got_2 = (refs[m * n:(m + 1) * n] for m in range(2, 6))
        send_sems, recv_sems = refs[6 * n:]
        x, y, c, _ = _place()
        mychip, sibling = 2 * x + y, (x, y, 1 - c)
        x_nbr, y_nbr, diag = 2 * (1 - x) + y, 2 * x + (1 - y), 2 * (1 - x) + (1 - y)
        to_x, to_y = (1 - x, y, c), (x, 1 - y, c)
        sends = []

        def copy(w, k, src_ref, dst_ref, to):
            return pltpu.make_async_remote_copy(src_ref=src_ref, dst_ref=dst_ref, send_sem=send_sems.at[w, k],
                                                recv_sem=recv_sems.at[w, k], device_id=to, device_id_type=MESH)

        def start(cp):
            cp.start()
            sends.append(cp)

        def add_rows(w, count, fn):
            def step(idx, carry):
                fn(pl.ds(pl.multiple_of(idx * SUM_ROWS, SUM_ROWS), SUM_ROWS), pl.multiple_of(idx * SUM_ROWS, SUM_ROWS))
                return carry
            lax.fori_loop(0, count // SUM_ROWS, step, 0)

        f32 = lambda v: v.astype(F32)
        for w in range(n):
            ha = t[w].shape[1] // 2
            part_a, part_b = pl.ds(0, ha), pl.ds(ha, ha)
            start(copy(w, 0, t[w].at[x_nbr, part_a], got_x[w].at[0], to_x))
            start(copy(w, 1, t[w].at[diag, part_a], got_x[w].at[1], to_x))
            start(copy(w, 2, t[w].at[y_nbr, part_b], got_y[w].at[0], to_y))
            start(copy(w, 3, t[w].at[diag, part_b], got_y[w].at[1], to_y))
        for w in range(n):
            hr = t[w].shape[1]
            ha = hr // 2
            for k in (0, 1):
                copy(w, k, got_x[w].at[k], got_x[w].at[k], to_x).wait_recv()

            def sum_a(rows, r, w=w, hr=hr):
                full[w][pl.ds(pl.multiple_of(c * hr + r, SUM_ROWS), SUM_ROWS), :] = \
                    f32(t[w][mychip, rows, :]) + f32(got_x[w][0, rows, :])
                pass_on[w][rows, :] = (f32(t[w][y_nbr, rows, :]) + f32(got_x[w][1, rows, :])).astype(BF16)

            add_rows(w, ha, sum_a)
            start(copy(w, 4, pass_on[w].at[pl.ds(0, ha)], got_2[w].at[pl.ds(0, ha)], to_y))
            for k in (2, 3):
                copy(w, k, got_y[w].at[k - 2], got_y[w].at[k - 2], to_y).wait_recv()

            def sum_b(rows, r, w=w, hr=hr, ha=ha):
                lower = pl.ds(pl.multiple_of(ha + r, SUM_ROWS), SUM_ROWS)
                full[w][pl.ds(pl.multiple_of(c * hr + ha + r, SUM_ROWS), SUM_ROWS), :] = \
                    f32(t[w][mychip, lower, :]) + f32(got_y[w][0, rows, :])
                pass_on[w][lower, :] = (f32(t[w][x_nbr, lower, :]) + f32(got_y[w][1, rows, :])).astype(BF16)

            add_rows(w, ha, sum_b)
            start(copy(w, 5, pass_on[w].at[pl.ds(ha, ha)], got_2[w].at[pl.ds(ha, ha)], to_x))
        for w in range(n):
            hr = t[w].shape[1]
            ha = hr // 2
            copy(w, 4, got_2[w].at[pl.ds(0, ha)], got_2[w].at[pl.ds(0, ha)], to_y).wait_recv()
            copy(w, 5, got_2[w].at[pl.ds(ha, ha)], got_2[w].at[pl.ds(ha, ha)], to_x).wait_recv()

            def finish(rows, r, w=w, hr=hr):
                out_rows = pl.ds(pl.multiple_of(c * hr + r, SUM_ROWS), SUM_ROWS)
                full[w][out_rows, :] = full[w][out_rows, :] + f32(got_2[w][rows, :])

            add_rows(w, hr, finish)
            mine = full[w].at[pl.ds(c * hr, hr)]
            start(copy(w, 6, mine, mine, sibling))
        for w in range(n):
            hr = t[w].shape[1]
            theirs = full[w].at[pl.ds((1 - c) * hr, hr)]
            copy(w, 6, theirs, theirs, sibling).wait_recv()
        for cp in sends:
            cp.wait_send()

    half = lambda a: pltpu.VMEM((2, a.shape[1] // 2, a.shape[2]), a.dtype)
    whole = lambda a: pltpu.VMEM(a.shape[1:], a.dtype)
    return pl.pallas_call(
        body, name="rs_exchange_join",
        in_specs=[VMEM_WHOLE] * n, out_specs=[VMEM_WHOLE] * n,
        out_shape=[_sds((2 * a.shape[1], a.shape[2]), F32) for a in parts],
        scratch_shapes=[half(a) for a in parts] + [half(a) for a in parts] + [whole(a) for a in parts]
        + [whole(a) for a in parts] + [pltpu.SemaphoreType.DMA((n, 7)), pltpu.SemaphoreType.DMA((n, 7))],
        compiler_params=pltpu.CompilerParams(vmem_limit_bytes=VMEM_LIMIT),
    )(*parts)


def _small_allreduce(loss_p, dg_parts, dbg_a, dbg_c, dwc):
    ins = [loss_p] + list(dg_parts) + [dbg_a, dbg_c, dwc]
    n_in = len(ins)
    vmem = pl.BlockSpec(memory_space=pltpu.VMEM)

    def body(*refs):
        in_refs = refs[:n_in]
        out_ref, vec, buf, send_sems, recv_sems = refs[n_in:]
        x, y, c, _ = _place()
        me = 4 * x + 2 * y + c
        vec[...] = jnp.zeros_like(vec)
        vec[0:1, :] = jnp.sum(in_refs[0][...], axis=0)
        for r in range(5):
            vec[1 + r:2 + r, :] = jnp.sum(in_refs[1 + r][...], axis=0)
        vec[6:7, :] = jnp.sum(in_refs[6][...], axis=0)
        vec[7:8, :] = jnp.sum(in_refs[7][...], axis=0)
        vec[8:16, 0:CONV_W] = jnp.sum(in_refs[8][...], axis=0)
        buf[pl.ds(me, 1)] = vec[...][None]
        copies = []
        for r in range(1, 8):
            fx, fy, fc = (r >> 2) & 1, (r >> 1) & 1, r & 1
            to = (1 - x if fx else x, 1 - y if fy else y, 1 - c if fc else c)
            cp = pltpu.make_async_remote_copy(src_ref=vec, dst_ref=buf.at[me], send_sem=send_sems.at[r - 1],
                                              recv_sem=recv_sems.at[r - 1], device_id=to, device_id_type=MESH)
            cp.start()
            copies.append(cp)
        for cp in copies:
            cp.wait()
        total = buf[0]
        for s in range(1, 8):
            total = total + buf[s]
        out_ref[...] = total
        out_ref[0:1, :] = jnp.broadcast_to(jnp.sum(total[0:1, :], axis=-1, keepdims=True), (1, D_MODEL))

    return pl.pallas_call(
        body, name="small_allreduce",
        in_specs=[vmem] * n_in, out_specs=vmem, out_shape=_sds((SMALL_ROWS, D_MODEL), F32),
        scratch_shapes=[pltpu.VMEM((SMALL_ROWS, D_MODEL), F32), pltpu.VMEM((8, SMALL_ROWS, D_MODEL), F32),
                        pltpu.SemaphoreType.DMA((7,)), pltpu.SemaphoreType.DMA((7,))],
    )(*ins)


def _local_step(x, p, tgt, g, b_gate, w_conv, wf):
    seq = x.shape[0]
    tm = min(seq, 1024)
    th = min(seq, 512)
    tl = min(seq, 2048)
    ni, nh, nl = seq // tm, seq // th, seq // tl
    g_pre_mix, g_post_mix, g_pre_mlp, g_post_mlp, g_ple = g
    w_in, w_ao, w_co, w_o, w_up, w_down, w_pg, w_pp, w_in_nat, w_up_nat = wf
    D = D_MODEL
    vec = lambda a, blk=0: (a, _bs((1, D), lambda i, j, k: (0, blk)))
    rows_i = lambda a, t, blk=0: (a, _bs((t, D), lambda i, j, k: (i, blk)))
    rows_k = lambda a, t, blk=0: (a, _bs((t, D), lambda i, j, k: (k, blk)))
    part = lambda n: (_sds((n, 1, D), F32), _bs((None, 1, D), lambda i, j, k: (i, 0, 0)))
    full2 = lambda a: (a, _bs(a.shape, lambda i, j, k: (0, 0)))

    normed = lambda xb, gb: (_rms(xb, gb).astype(BF16),) * 2
    keep_a = lambda t: [(_sds((seq, D), BF16), _bs((t, D), lambda i, j, k: (i, 0)))]
    main_w = D_IN - 2 * D
    proj, gates, h1 = _mm("proj_in", "nn", (nh, 1, 1),
                          a_ins=[rows_i(x, th), vec(g_pre_mix)], a_fn=normed,
                          b_ins=[full2(w_in_nat)], b_fn=_ident,
                          epi_fn=lambda acc: (acc[:, :main_w], acc[:, main_w:]),
                          outs=[(_sds((seq, main_w), F32), _bs((th, main_w), lambda i, j, k: (i, 0))),
                                (_sds((seq, 2 * D), BF16), _bs((th, 2 * D), lambda i, j, k: (i, 0)))],
                          acc_shape=(th, D_IN), a_cache=((th, D), BF16), a_outs=keep_a(th))
    o = _attn_fwd(proj, seq)
    e = _conv_fwd(proj, w_conv, seq, tm)

    def gate_values(ga, gc, ba, bc):
        return _sig(ga.astype(F32) + ba), _sig(gc.astype(F32) + bc)

    def mix_fn(ga, gc, ob, eb, ba, bc, wao, wco):
        sa, sc = gate_values(ga, gc, ba, bc)
        ya, yc = _nn(ob, wao).astype(BF16), _nn(eb, wco).astype(BF16)
        mix = (sa * ya.astype(F32) + sc * yc.astype(F32)).astype(BF16)
        return mix, mix, ya, yc

    def post_mix(acc, xb, gb):
        return acc, xb + _rms(acc, gb)

    half_rows = lambda a: (a, _bs((th, a.shape[1]), lambda i, j, k: (i, 0)))
    mixed, x1, mixin, y_attn, y_conv = _mm(
        "mix_out", "nn", (nh, 1, 1),
        a_ins=[rows_i(gates, th, 0), rows_i(gates, th, 1), half_rows(o), half_rows(e), vec(b_gate, 0), vec(b_gate, 1),
               full2(w_ao), full2(w_co)], a_fn=mix_fn, b_ins=[full2(w_o)], b_fn=_ident,
        epi_ins=[rows_i(x, th), vec(g_post_mix)], epi_fn=post_mix,
        outs=[(_sds((seq, D), BF16), _bs((th, D), lambda i, j, k: (i, 0))),
              (_sds((seq, D), F32), _bs((th, D), lambda i, j, k: (i, 0)))],
        acc_shape=(th, D), a_cache=((th, D), BF16), a_outs=keep_a(th) * 3)
    mix_ins = lambda rows: [rows(gates, th, 0), rows(gates, th, 1), rows(y_attn, th), rows(y_conv, th),
                            vec(b_gate, 0), vec(b_gate, 1)]
    up, h2 = _mm("mlp_up", "nn", (nh, 1, 1),
                 a_ins=[rows_i(x1, th), vec(g_pre_mlp)], a_fn=normed,
                 b_ins=[full2(w_up_nat)], b_fn=_ident,
                 outs=[(_sds((seq, D_FF), BF16), _bs((th, D_FF), lambda i, j, k: (i, 0)))],
                 acc_shape=(th, D_FF), a_cache=((th, D), BF16), a_outs=keep_a(th))

    def relu2(ub):
        r = jnp.maximum(ub.astype(F32), 0.0)
        return (r * r).astype(BF16)

    dx2, df, dpre, h3, dpp, loss_p, dg_ple_p, dg_post_mlp_p = _mlp_down_ple_head(
        up, x1, p, tgt, g_ple, g_post_mlp, w_down, w_pg, w_pp, seq, th)

    (dw_pp,) = _mm("dw_ple_proj", "tn", (1, 1, nh),
                   a_ins=[(p, _bs((th, PLE_DIM), lambda i, j, k: (k, 0)))], a_fn=_to_bf16,
                   b_ins=[rows_k(dpp, th)], b_fn=_ident,
                   outs=[(_sds((PLE_DIM, D), F32), _bs((PLE_DIM, D), lambda i, j, k: (0, 0)))],
                   acc_shape=(PLE_DIM, D))
    (dw_pg,) = _mm("dw_ple_gate", "tn", (1, 1, nl),
                   a_ins=[rows_k(h3, tl)], a_fn=_ident, b_ins=[rows_k(dpre, tl)], b_fn=_ident,
                   outs=[(_sds((D, D), F32), _bs((D, D), lambda i, j, k: (0, 0)))], acc_shape=(D, D))

    def dup_fn(acc, ub):
        return (acc * (2.0 * jnp.maximum(ub.astype(F32), 0.0)),)

    (dup,) = _mm("d_mlp_down", "nt", (nh, 1, 1),
                 a_ins=[rows_i(df, th)], a_fn=_ident, b_ins=[full2(w_down)], b_fn=_ident,
                 epi_ins=[(up, _bs((th, D_FF), lambda i, j, k: (i, 0)))], epi_fn=dup_fn,
                 outs=[(_sds((seq, D_FF), BF16), _bs((th, D_FF), lambda i, j, k: (i, 0)))],
                 acc_shape=(th, D_FF))
    (dw_down,) = _mm("dw_mlp_down", "tn", (4, 1, nl),
                     a_ins=[(up, _bs((tl, D), lambda i, j, k: (k, i)))], a_fn=relu2,
                     b_ins=[rows_k(df, tl)], b_fn=_ident,
                     outs=[(_sds((D_FF, D), F32), _bs((D, D), lambda i, j, k: (i, 0)))], acc_shape=(D, D))
    (dw_up,) = _mm("dw_mlp_up", "tn", (1, 4, nl),
                   a_ins=[rows_k(h2, tl)], a_fn=_ident,
                   b_ins=[(dup, _bs((tl, D), lambda i, j, k: (k, j)))], b_fn=_ident,
                   outs=[(_sds((N_CHIPS, D, D), F32), _bs((None, D, D), lambda i, j, k: (j, 0, 0)))],
                   acc_shape=(D, D))

    def mlp_norm_bwd(acc, x1b, dx2b, mixedb, g_mlp, g_mix):
        dxn, dg_mlp = _rms_bwd(x1b, g_mlp, acc)
        dx1b = dx2b + dxn
        dmixedb, dg_mix = _rms_bwd(mixedb.astype(F32), g_mix, dx1b)
        return dx1b, dmixedb, dg_mlp, dg_mix

    dx1, dmixed, dg_pre_mlp_p, dg_post_mix_p = _mm(
        "d_mlp_up", "nt", (nh, 1, 1),
        a_ins=[(dup, _bs((th, D_FF), lambda i, j, k: (i, 0)))], a_fn=_ident,
        b_ins=[full2(w_up_nat)], b_fn=_ident,
        epi_ins=[rows_i(x1, th), rows_i(dx2, th), rows_i(mixed, th), vec(g_pre_mlp), vec(g_post_mix)],
        epi_fn=mlp_norm_bwd,
        outs=[(_sds((seq, D), F32), _bs((th, D), lambda i, j, k: (i, 0))),
              (_sds((seq, D), BF16), _bs((th, D), lambda i, j, k: (i, 0))), part(nh), part(nh)],
        acc_shape=(th, D))
    (dw_o,) = _mm("dw_mix_out", "tn", (1, 1, nl),
                  a_ins=[rows_k(mixin, tl)], a_fn=_ident, b_ins=[rows_k(dmixed, tl)], b_fn=_ident,
                  outs=[(_sds((D, D), F32), _bs((D, D), lambda i, j, k: (0, 0)))], acc_shape=(D, D))

    def gate_bwd(acc, ga, gc, ya, yc, ba, bc, wao, wco):
        sa, sc = gate_values(ga, gc, ba, bc)
        dga = acc * ya.astype(F32) * sa * (1.0 - sa)
        dgc = acc * yc.astype(F32) * sc * (1.0 - sc)
        dya, dyc = (acc * sa).astype(BF16), (acc * sc).astype(BF16)
        return (dya, dyc, jnp.concatenate([dga, dgc], axis=1), _nt(dya, wao), _nt(dyc, wco),
                jnp.sum(dga, axis=0, keepdims=True), jnp.sum(dgc, axis=0, keepdims=True))

    dya, dyc, dgate, do, de, dbg_a_p, dbg_c_p = _mm(
        "d_mix_out", "nt", (nh, 1, 1),
        a_ins=[rows_i(dmixed, th)], a_fn=_ident, b_ins=[full2(w_o)], b_fn=_ident,
        epi_ins=mix_ins(rows_i) + [full2(w_ao), full2(w_co)], epi_fn=gate_bwd,
        outs=[(_sds((seq, D), BF16), _bs((th, D), lambda i, j, k: (i, 0)))] * 2
             + [(_sds((seq, 2 * D), BF16), _bs((th, 2 * D), lambda i, j, k: (i, 0))),
                (_sds((seq, ATTN_W), BF16), _bs((th, ATTN_W), lambda i, j, k: (i, 0))),
                (_sds((seq, CONV_W), F32), _bs((th, CONV_W), lambda i, j, k: (i, 0))), part(nh), part(nh)],
        acc_shape=(th, D))
    (dw_ao,) = _mm("dw_attn_out", "tn", (1, 1, nh),
                   a_ins=[(o, _bs((th, ATTN_W), lambda i, j, k: (k, 0)))], a_fn=_ident,
                   b_ins=[rows_k(dya, th)], b_fn=_ident,
                   outs=[(_sds((ATTN_W, D), F32), _bs((ATTN_W, D), lambda i, j, k: (0, 0)))], acc_shape=(ATTN_W, D))
    dq, dk, dv = _attn_bwd(proj, do, seq)
    (dw_co,) = _mm("dw_conv_out", "tn", (1, 1, nh),
                   a_ins=[(e, _bs((th, CONV_W), lambda i, j, k: (k, 0)))], a_fn=_ident,
                   b_ins=[rows_k(dyc, th)], b_fn=_ident,
                   outs=[(_sds((CONV_W, D), F32), _bs((CONV_W, D), lambda i, j, k: (0, 0)))], acc_shape=(CONV_W, D))
    dconv, dwc_p = _conv_bwd(proj, de, w_conv, seq, tm)
    qkv_w = 3 * ATTN_W
    join_bf16 = lambda *blocks: jnp.concatenate([b.astype(BF16) for b in blocks], axis=1)
    piece = lambda a, t, rows, blk=0: (a, _bs((t, a.shape[1]), (lambda i, j, k: (k, blk)) if rows == "k"
                                             else (lambda i, j, k: (i, blk))))
    (dw_in_qkv,) = _mm("dw_proj_in_qkv", "tn", (1, 1, ni),
                       a_ins=[rows_k(h1, tm)], a_fn=_ident,
                       b_ins=[piece(dq, tm, "k"), piece(dk, tm, "k"), piece(dv, tm, "k")], b_fn=join_bf16,
                       outs=[(_sds((D, qkv_w), F32), _bs((D, qkv_w), lambda i, j, k: (0, 0)))], acc_shape=(D, qkv_w))
    (dw_in_conv,) = _mm("dw_proj_in_conv", "tn", (1, 1, nl),
                        a_ins=[rows_k(h1, tl)], a_fn=_ident, b_ins=[piece(dconv, tl, "k")], b_fn=_ident,
                        outs=[(_sds((D, 3 * CONV_W), F32), _bs((D, 3 * CONV_W), lambda i, j, k: (0, 0)))],
                        acc_shape=(D, 3 * CONV_W))
    (dw_in_gate,) = _mm("dw_proj_in_gate", "tn", (1, 2, nl),
                        a_ins=[rows_k(h1, tl)], a_fn=_ident,
                        b_ins=[(dgate, _bs((tl, D), lambda i, j, k: (k, j)))], b_fn=_ident,
                        outs=[(_sds((D, 2 * D), F32), _bs((D, D), lambda i, j, k: (0, j)))], acc_shape=(D, D))
    dw_in = jnp.concatenate([dw_in_qkv, dw_in_conv, dw_in_gate], axis=1)

    def in_norm_bwd(acc, xb, dx1b, gb):
        dxn, dg = _rms_bwd(xb, gb, acc)
        return dx1b + dxn, dg

    grad_x, dg_pre_mix_p = _mm("d_proj_in", "nt", (nh, 1, 1),
                               a_ins=[piece(dq, th, "i"), piece(dk, th, "i"), piece(dv, th, "i"),
                                      piece(dconv, th, "i"), piece(dgate, th, "i")], a_fn=join_bf16,
                               b_ins=[full2(w_in_nat)], b_fn=_ident,
                               epi_ins=[rows_i(x, th), rows_i(dx1, th), vec(g_pre_mix)], epi_fn=in_norm_bwd,
                               outs=[(_sds((seq, D), F32), _bs((th, D), lambda i, j, k: (i, 0))), part(nh)],
                               acc_shape=(th, D))

    chip_major = lambda a: a.reshape(a.shape[0], N_CHIPS, a.shape[1] // N_CHIPS).transpose(1, 0, 2)
    big = [chip_major(dw_in), chip_major(dw_ao), chip_major(dw_co), dw_o.reshape(N_CHIPS, D // N_CHIPS, D), dw_up,
           dw_down.reshape(N_CHIPS, D_FF // N_CHIPS, D), dw_pg.reshape(N_CHIPS, D // N_CHIPS, D), chip_major(dw_pp)]
    small = (loss_p, [dg_pre_mix_p, dg_post_mix_p, dg_pre_mlp_p, dg_post_mlp_p, dg_ple_p], dbg_a_p, dbg_c_p, dwc_p)
    return grad_x, big, small


RS_GROUPS = ((0,), (4,), (5,), (1, 2, 3, 6, 7))


def _reduce_scatter(big):
    pair = [None] * len(big)
    for gi, group in enumerate(RS_GROUPS):
        for w, s in zip(group, _rs_pair_sum(f"rs_pair_sum_{gi}", [big[w] for w in group])):
            pair[w] = s
    return _rs_exchange_join(pair)


def kernel(x, p, g_pre_mix, w_in, b_gate, w_conv, w_attn_out, w_conv_out, w_o, g_post_mix, g_pre_mlp, w_up, w_down, g_post_mlp, g_ple, w_ple_gate, w_ple_proj, loss_target, m_g_pre_mix, m_w_in, m_b_gate, m_w_conv, m_w_attn_out, m_w_conv_out, m_w_o, m_g_post_mix, m_g_pre_mlp, m_w_up, m_w_down, m_g_post_mlp, m_g_ple, m_w_ple_gate, m_w_ple_proj, v_g_pre_mix, v_w_in, v_b_gate, v_w_conv, v_w_attn_out, v_w_conv_out, v_w_o, v_g_post_mix, v_g_pre_mlp, v_w_up, v_w_down, v_g_post_mlp, v_g_ple, v_w_ple_gate, v_w_ple_proj):
    mats = [w_in, w_attn_out, w_conv_out, w_o, w_up, w_down, w_ple_gate, w_ple_proj]
    mats_m = [m_w_in, m_w_attn_out, m_w_conv_out, m_w_o, m_w_up, m_w_down, m_w_ple_gate, m_w_ple_proj]
    mats_v = [v_w_in, v_w_attn_out, v_w_conv_out, v_w_o, v_w_up, v_w_down, v_w_ple_gate, v_w_ple_proj]
    gains = [g_pre_mix, g_post_mix, g_pre_mlp, g_post_mlp, g_ple]
    gains_m = [m_g_pre_mix, m_g_post_mix, m_g_pre_mlp, m_g_post_mlp, m_g_ple]
    gains_v = [v_g_pre_mix, v_g_post_mix, v_g_pre_mlp, v_g_post_mlp, v_g_ple]

    taps = jnp.concatenate([w_conv[0], jnp.zeros((CONV_PAD_ROWS - 3, LANES), F32)], axis=0)
    gathered = _allgather_weights([w[0].astype(BF16) for w in mats] + [taps])
    cols_joined = lambda a: a.transpose(1, 0, 2).reshape(a.shape[1], N_CHIPS * a.shape[2])
    rows_joined = lambda a: a.reshape(N_CHIPS * a.shape[1], a.shape[2])
    wf = [gathered[0], cols_joined(gathered[1]), cols_joined(gathered[2]), rows_joined(gathered[3]), gathered[4],
          rows_joined(gathered[5]), rows_joined(gathered[6]), cols_joined(gathered[7]),
          cols_joined(gathered[0]), cols_joined(gathered[4])]
    w_conv_full = cols_joined(gathered[8])[0:3, :]
    chip = 2 * lax.axis_index("x") + lax.axis_index("y")

    grad_x, big, small = _local_step(x[0], p[0, 0], loss_target[0], gains, b_gate, w_conv_full, wf)

    shard_grads = _reduce_scatter(big)
    red = _small_allreduce(*small)
    loss = red[0, 0]
    grad_gains = [red[1 + r:2 + r, :] for r in range(5)]
    grad_b_gate = jnp.concatenate([red[6:7, :], red[7:8, :]], axis=1)
    grad_w_conv = lax.dynamic_slice(red[8:11, :], (0, chip * LANES), (3, LANES))[None]

    grads_big = [gr.reshape(w.shape) for gr, w in zip(shard_grads, mats)]
    upd_big = [_adamw(f"adamw_{i}", w, gr, m, v) for i, (w, gr, m, v) in enumerate(zip(mats, grads_big, mats_m, mats_v))]
    pack = lambda vs, bg: jnp.concatenate(list(vs) + [bg.reshape(2, D_MODEL), jnp.zeros((1, D_MODEL), F32)], axis=0)
    upd_small = _adamw("adamw_small", pack(gains, b_gate), pack(grad_gains, grad_b_gate),
                       pack(gains_m, m_b_gate), pack(gains_v, v_b_gate))
    upd_conv = _adamw("adamw_conv", w_conv, grad_w_conv, m_w_conv, v_w_conv)

    def small_out(a, which):
        gains_out = [a[r:r + 1, :] for r in range(5)]
        return gains_out, a[5:7, :].reshape(1, 2 * D_MODEL)

    def ordered(g_pre_mix_, big_, b_gate_, conv_, g_rest):
        return [g_pre_mix_, big_[0], b_gate_, conv_, big_[1], big_[2], big_[3], g_rest[0], g_rest[1], big_[4], big_[5],
                g_rest[2], g_rest[3], big_[6], big_[7]]

    outs = [loss, grad_x[None]]
    outs += ordered(grad_gains[0], grads_big, grad_b_gate, grad_w_conv, grad_gains[1:])
    for which in range(3):
        g_out, b_out = small_out(upd_small[which], which)
        outs += ordered(g_out[0], [u[which] for u in upd_big], b_out, upd_conv[which], g_out[1:])
    return tuple(outs)
```

```python
import functools

import jax
import jax.numpy as jnp
from jax import lax
from jax.experimental import pallas as pl
from jax.experimental.pallas import tpu as pltpu

F32 = jnp.float32
BF16 = jnp.bfloat16
MESH = pl.DeviceIdType.MESH

D_MODEL = 1024
N_HEADS = 8
HEAD_DIM = 64
ATTN_W = N_HEADS * HEAD_DIM
CONV_W = 512
D_FF = 4096
PLE_DIM = 256
D_IN = 5120
N_CHIPS = 4
EPS = 1e-6
Q_SCALE = HEAD_DIM ** -0.5

ADAM_LR = 0.001
ADAM_B1 = 0.9
ADAM_B2 = 0.999
ADAM_EPS = 1e-08
ADAM_WD = 0.01
ADAM_STEP = 10

V7X_VMEM_BYTES = 64 * 1024 * 1024
VMEM_LIMIT = V7X_VMEM_BYTES - 8 * 1024 * 1024
LANES = 128
ATT_BLK = 256
SMALL_ROWS = 16
CONV_PAD_ROWS = 16


def _cparams(n_grid):
    return pltpu.CompilerParams(dimension_semantics=("arbitrary",) * n_grid, vmem_limit_bytes=VMEM_LIMIT)


def _bs(shape, fn):
    return pl.BlockSpec(shape, fn)


def _rms_stats(xf):
    return lax.rsqrt(jnp.mean(xf * xf, axis=-1, keepdims=True) + EPS)


def _rms(xf, g):
    return xf * _rms_stats(xf) * g


def _rms_bwd(xf, g, dy):
    r = _rms_stats(xf)
    xh = xf * r
    dyg = dy * g
    dx = r * (dyg - xh * jnp.mean(dyg * xh, axis=-1, keepdims=True))
    return dx, jnp.sum(dy * xh, axis=0, keepdims=True)


def _sig(z):
    return 1.0 / (1.0 + jnp.exp(-z))


def _ident(a):
    return a


def _to_bf16(a):
    return a.astype(BF16)


_DIMS = {"nn": (((1,), (0,)), ((), ())), "nt": (((1,), (1,)), ((), ())), "tn": (((0,), (0,)), ((), ()))}


def _mm(name, mode, grid, a_ins, a_fn, b_ins, b_fn, outs, acc_shape, epi_ins=(), epi_fn=None,
        a_cache=None, a_outs=(), epi_a=()):
    nk = grid[2]
    na, nb, ne, no, nao = len(a_ins), len(b_ins), len(epi_ins), len(outs), len(a_outs)
    assert a_cache is None or nk == 1
    assert not a_outs or a_cache is not None
    dims = _DIMS[mode]
    if epi_fn is None:
        epi_fn = lambda acc: (acc,)

    def body(*refs):
        a_refs = refs[:na]
        b_refs = refs[na:na + nb]
        e_refs = refs[na + nb:na + nb + ne]
        o_refs = refs[na + nb + ne:na + nb + ne + no]
        ao_refs = refs[na + nb + ne + no:na + nb + ne + no + nao]
        scratch = list(refs[na + nb + ne + no + nao:])
        acc_ref = scratch.pop(0) if nk > 1 else None
        a_sc = scratch.pop(0) if a_cache is not None else None
        j = pl.program_id(1)
        k = pl.program_id(2)

        def finish(acc):
            res = epi_fn(acc, *[a_refs[t][...] for t in epi_a], *[r[...] for r in e_refs])
            for r, val in zip(o_refs, res):
                r[...] = val.astype(r.dtype)

        if a_sc is not None:
            @pl.when(j == 0)
            def _():
                res = a_fn(*[r[...] for r in a_refs])
                if nao:
                    for r, val in zip(ao_refs, res[1:]):
                        r[...] = val.astype(r.dtype)
                    res = res[0]
                a_sc[...] = res
            a = a_sc[...]
        else:
            a = a_fn(*[r[...] for r in a_refs])
        b = b_fn(*[r[...] for r in b_refs])
        prod = lax.dot_general(a, b, dims, preferred_element_type=F32)
        if nk == 1:
            finish(prod)
        else:
            @pl.when(k == 0)
            def _():
                acc_ref[...] = prod

            @pl.when(k > 0)
            def _():
                acc_ref[...] += prod

            @pl.when(k == nk - 1)
            def _():
                finish(acc_ref[...])

    scratch_shapes = []
    if nk > 1:
        scratch_shapes.append(pltpu.VMEM(acc_shape, F32))
    if a_cache is not None:
        scratch_shapes.append(pltpu.VMEM(*a_cache))
    all_outs = list(outs) + list(a_outs)
    res = pl.pallas_call(
        body, name=name, grid=grid,
        in_specs=[s for _, s in a_ins] + [s for _, s in b_ins] + [s for _, s in epi_ins],
        out_specs=[s for _, s in all_outs],
        out_shape=[o for o, _ in all_outs],
        scratch_shapes=scratch_shapes,
        compiler_params=_cparams(3),
    )(*[a for a, _ in a_ins], *[a for a, _ in b_ins], *[a for a, _ in epi_ins])
    return res


def _sds(shape, dtype):
    return jax.ShapeDtypeStruct(shape, dtype)


def _nt(a, b):
    return lax.dot_general(a, b, _DIMS["nt"], preferred_element_type=F32)


def _tn(a, b):
    return lax.dot_general(a, b, _DIMS["tn"], preferred_element_type=F32)


def _nn(a, b):
    return lax.dot_general(a, b, _DIMS["nn"], preferred_element_type=F32)


def _mlp_down_ple_head(up, x1, p, tgt, g_ple, g_post_mlp, w_down, w_pg, w_pp, seq, tr):
    nblk = seq // tr
    D = D_MODEL

    def body(up_ref, x1_ref, p_ref, t_ref, gp_ref, gm_ref, wd_ref, wpg_ref, wpp_ref,
             dx2_ref, df_ref, dpre_ref, h3_ref, dpp_ref, loss_ref, dgp_ref, dgm_ref):
        gp, gm, wpg, wpp = gp_ref[...], gm_ref[...], wpg_ref[...], wpp_ref[...]
        halves = [pl.ds(0, tr // 2), pl.ds(tr // 2, tr // 2)]
        w_down = wd_ref[...]
        fb = []
        for r in halves:
            hidden = jnp.maximum(up_ref[r, :].astype(F32), 0.0)
            fb.append(_nn((hidden * hidden).astype(BF16), w_down))
        x2b = [x1_ref[r, :] + _rms(fb[s], gm) for s, r in enumerate(halves)]
        h3 = [_rms(x, gp).astype(BF16) for x in x2b]
        gate = [_sig(_nn(h, wpg)) for h in h3]
        pp = [_nn(p_ref[r, :].astype(BF16), wpp) for r in halves]
        err = [x2b[s] + gate[s] * pp[s] - t_ref[r, :] for s, r in enumerate(halves)]
        dx3 = [e * (1.0 / D) for e in err]
        dpre = [(dx3[s] * pp[s] * gate[s] * (1.0 - gate[s])).astype(BF16) for s in range(2)]
        dh3 = [_nt(d, wpg) for d in dpre]
        loss, dgp_sum, dgm_sum = 0.0, 0.0, 0.0
        for s, r in enumerate(halves):
            h3_ref[r, :] = h3[s]
            dpp_ref[r, :] = (dx3[s] * gate[s]).astype(BF16)
            dpre_ref[r, :] = dpre[s]
            dxn, dgp = _rms_bwd(x2b[s], gp, dh3[s])
            dx2 = dx3[s] + dxn
            dx2_ref[r, :] = dx2
            dfb, dgm = _rms_bwd(fb[s], gm, dx2)
            df_ref[r, :] = dfb.astype(BF16)
            loss = loss + jnp.sum(err[s] * err[s], axis=0, keepdims=True)
            dgp_sum, dgm_sum = dgp_sum + dgp, dgm_sum + dgm
        loss_ref[...] = loss * (0.5 / D)
        dgp_ref[...] = dgp_sum
        dgm_ref[...] = dgm_sum

    rows = _bs((tr, D), lambda i: (i, 0))
    vec = _bs((1, D), lambda i: (0, 0))
    part = _bs((None, 1, D), lambda i: (i, 0, 0))
    return pl.pallas_call(
        body, name="mlp_down_ple_head", grid=(nblk,),
        in_specs=[_bs((tr, D_FF), lambda i: (i, 0)), rows, _bs((tr, PLE_DIM), lambda i: (i, 0)), rows, vec, vec,
                  _bs((D_FF, D), lambda i: (0, 0)), _bs((D, D), lambda i: (0, 0)), _bs((PLE_DIM, D), lambda i: (0, 0))],
        out_specs=[rows] * 5 + [part] * 3,
        out_shape=[_sds((seq, D), F32)] + [_sds((seq, D), BF16)] * 4 + [_sds((nblk, 1, D), F32)] * 3,
        compiler_params=_cparams(1),
    )(up, x1, p, tgt, g_ple, g_post_mlp, w_down, w_pg, w_pp)


def _shift_rows_down(u, prev, n):
    rows = u.shape[0]
    ridx = lax.broadcasted_iota(jnp.int32, u.shape, 0)
    out = pltpu.roll(u, n, 0)
    for r in range(n):
        out = jnp.where(ridx == r, prev[8 - n + r:8 - n + r + 1, :], out)
    del rows
    return out


def _shift_rows_up(u, nxt, n):
    rows = u.shape[0]
    ridx = lax.broadcasted_iota(jnp.int32, u.shape, 0)
    out = pltpu.roll(u, rows - n, 0)
    for r in range(n):
        out = jnp.where(ridx == rows - n + r, nxt[r:r + 1, :], out)
    return out


CONV_COL0 = 3


def _conv_fwd(proj, w_conv, seq, tr):
    hb = tr // 8

    def body(cb_ref, cc_ref, cu_ref, ccp_ref, cup_ref, w_ref, e_ref):
        i = pl.program_id(0)
        u = cc_ref[...] * cu_ref[...]
        up = jnp.where(i > 0, ccp_ref[...] * cup_ref[...], 0.0)
        w = w_ref[...]
        d = w[0:1, :] * _shift_rows_down(u, up, 2) + w[1:2, :] * _shift_rows_down(u, up, 1) + w[2:3, :] * u
        e_ref[...] = (cb_ref[...] * d).astype(BF16)

    prev = lambda c: (lambda i: (jnp.maximum(i * hb - 1, 0), c))
    return pl.pallas_call(
        body, name="conv_fwd", grid=(seq // tr,),
        in_specs=[_bs((tr, CONV_W), lambda i: (i, CONV_COL0)),
                  _bs((tr, CONV_W), lambda i: (i, CONV_COL0 + 1)),
                  _bs((tr, CONV_W), lambda i: (i, CONV_COL0 + 2)),
                  _bs((8, CONV_W), prev(CONV_COL0 + 1)),
                  _bs((8, CONV_W), prev(CONV_COL0 + 2)),
                  _bs((3, CONV_W), lambda i: (0, 0))],
        out_specs=_bs((tr, CONV_W), lambda i: (i, 0)),
        out_shape=_sds((seq, CONV_W), BF16),
        compiler_params=_cparams(1),
    )(proj, proj, proj, proj, proj, w_conv)


def _conv_bwd(proj, de, w_conv, seq, tr):
    hb = tr // 8
    nblk = seq // tr

    def body(cb_ref, cc_ref, cu_ref, ccp_ref, cup_ref, cbn_ref, de_ref, den_ref, w_ref, o_ref, dw_ref):
        i = pl.program_id(0)
        cc, cu, cb = cc_ref[...], cu_ref[...], cb_ref[...]
        u = cc * cu
        up = jnp.where(i > 0, ccp_ref[...] * cup_ref[...], 0.0)
        u1 = _shift_rows_down(u, up, 1)
        u2 = _shift_rows_down(u, up, 2)
        de_ = de_ref[...]
        dd = de_ * cb
        ddn = jnp.where(i < nblk - 1, den_ref[...] * cbn_ref[...], 0.0)
        w = w_ref[...]
        du = w[2:3, :] * dd + w[1:2, :] * _shift_rows_up(dd, ddn, 1) + w[0:1, :] * _shift_rows_up(dd, ddn, 2)
        o_ref[:, 0:CONV_W] = (de_ * (w[0:1, :] * u2 + w[1:2, :] * u1 + w[2:3, :] * u)).astype(BF16)
        o_ref[:, CONV_W:2 * CONV_W] = (du * cu).astype(BF16)
        o_ref[:, 2 * CONV_W:3 * CONV_W] = (du * cc).astype(BF16)
        ridx = lax.broadcasted_iota(jnp.int32, (8, CONV_W), 0)
        dw0 = jnp.sum(dd * u2, axis=0, keepdims=True)
        dw1 = jnp.sum(dd * u1, axis=0, keepdims=True)
        dw2 = jnp.sum(dd * u, axis=0, keepdims=True)
        dw_ref[...] = jnp.where(ridx == 0, dw0, jnp.where(ridx == 1, dw1, jnp.where(ridx == 2, dw2, 0.0)))

    prev = lambda c: (lambda i: (jnp.maximum(i * hb - 1, 0), c))
    nxt = lambda c: (lambda i: (jnp.minimum((i + 1) * hb, seq // 8 - 1), c))
    return pl.pallas_call(
        body, name="conv_bwd", grid=(nblk,),
        in_specs=[_bs((tr, CONV_W), lambda i: (i, CONV_COL0)),
                  _bs((tr, CONV_W), lambda i: (i, CONV_COL0 + 1)),
                  _bs((tr, CONV_W), lambda i: (i, CONV_COL0 + 2)),
                  _bs((8, CONV_W), prev(CONV_COL0 + 1)),
                  _bs((8, CONV_W), prev(CONV_COL0 + 2)),
                  _bs((8, CONV_W), nxt(CONV_COL0)),
                  _bs((tr, CONV_W), lambda i: (i, 0)),
                  _bs((8, CONV_W), nxt(0)),
                  _bs((3, CONV_W), lambda i: (0, 0))],
        out_specs=[_bs((tr, 3 * CONV_W), lambda i: (i, 0)), _bs((None, 8, CONV_W), lambda i: (i, 0, 0))],
        out_shape=[_sds((seq, 3 * CONV_W), BF16), _sds((nblk, 8, CONV_W), F32)],
        compiler_params=_cparams(1),
    )(proj, proj, proj, proj, proj, proj, de, de, w_conv)


def _log_gates(z):
    lse = jnp.log(1.0 + jnp.exp(-jnp.abs(z)))
    log_beta = jnp.minimum(z, 0.0) - lse
    return log_beta, log_beta - z


DEAD_LOG_WEIGHT = -110.0
NO_TILE = -1e30


def _first_live_tile(start, scores, live_sc):
    def alive():
        return jnp.max(jnp.maximum(live_sc[0], live_sc[1])) > DEAD_LOG_WEIGHT

    def step(c):
        for h, z in enumerate(scores(c[0])):
            live_sc[h] = live_sc[h] + jnp.sum(_log_gates(z)[1], axis=-1, keepdims=True)
        return c[0] - 1, alive()

    j_end, _ = lax.while_loop(lambda c: jnp.logical_and(c[0] >= 0, c[1]), step, (start, alive()))
    return j_end + 1


def _attn_fwd(proj, seq):
    blk = ATT_BLK
    nq = seq // blk
    npair = N_HEADS // 2

    def body(q_ref, k_ref, v_ref, o_ref, z0_sc, z1_sc, w0_sc, w1_sc, tot_sc, live_sc, acc_sc):
        i = pl.program_id(1)
        is_a = lax.broadcasted_iota(jnp.int32, (1, LANES), 1) < HEAD_DIM
        q2 = (q_ref[...] * Q_SCALE).astype(BF16)
        zero = jnp.zeros_like(q2)
        qs = (jnp.where(is_a, q2, zero), jnp.where(is_a, zero, q2))
        row = lax.broadcasted_iota(jnp.int32, (blk, blk), 0)
        col = lax.broadcasted_iota(jnp.int32, (blk, blk), 1)
        tri = (row > col).astype(BF16)
        causal = col < row

        def tile_of(ref, j):
            return ref[pl.ds(pl.multiple_of(j * blk, blk), blk), :].astype(BF16)

        def scores(j):
            k2 = tile_of(k_ref, j)
            return [_nt(qs[h], k2) for h in range(2)]

        has_left = i > 0
        left = jnp.maximum(i - 1, 0)

        g_d = [_log_gates(z) for z in scores(i)]
        g_l = [_log_gates(z) for z in scores(left)]
        keep_d = [jnp.where(causal, g[1], 0.0) for g in g_d]
        suf_d = [_nn(lk.astype(BF16), tri) for lk in keep_d]
        suf_l = [_nn(g[1].astype(BF16), tri) for g in g_l]
        v_d, v_l = tile_of(v_ref, i), tile_of(v_ref, left)
        pv = []
        for h in range(2):
            sum_d = jnp.sum(keep_d[h], axis=-1, keepdims=True)
            w_d = jnp.where(causal, jnp.exp(g_d[h][0] + suf_d[h]), 0.0)
            w_l = jnp.exp(g_l[h][0] + (jnp.where(has_left, sum_d, NO_TILE) + suf_l[h]))
            pv.append(_nn(w_d.astype(BF16), v_d) + _nn(w_l.astype(BF16), v_l))
            tot_sc[h] = sum_d + jnp.sum(g_l[h][1], axis=-1, keepdims=True)
        acc_sc[...] = jnp.where(is_a, pv[0], pv[1])

        live_sc[...] = tot_sc[...]
        first = _first_live_tile(i - 2, scores, live_sc)
        trips = i - 1 - first
        z_bufs, w_bufs = (z0_sc, z1_sc), (w0_sc, w1_sc)

        def put(ref, vals):
            for h in range(2):
                ref[h] = vals[h]

        def weights(zs):
            gates = [_log_gates(z) for z in zs]
            sums = [_nn(g[1].astype(BF16), tri) for g in gates]
            ws = []
            for h in range(2):
                ws.append(jnp.exp(gates[h][0] + (tot_sc[h] + sums[h])).astype(BF16))
                tot_sc[h] = tot_sc[h] + jnp.sum(gates[h][1], axis=-1, keepdims=True)
            return ws

        def add_values(w_buf, j):
            v2 = tile_of(v_ref, j)
            acc_sc[...] += jnp.where(is_a, _nn(w_buf[0], v2), _nn(w_buf[1], v2))

        def trip(j, s):
            add_values(w_bufs[s], j + 1)
            put(z_bufs[1 - s], scores(jnp.maximum(j - 1, first)))
            put(w_bufs[1 - s], weights((z_bufs[s][0], z_bufs[s][1])))

        @pl.when(trips > 0)
        def _():
            put(z0_sc, scores(i - 2))
            w0_sc[...] = jnp.zeros_like(w0_sc)

            def two_trips(pp, carry):
                j = i - 2 - 2 * pp
                trip(j, 0)
                trip(j - 1, 1)
                return carry

            lax.fori_loop(0, trips // 2, two_trips, 0)
            odd = trips % 2 == 1

            @pl.when(odd)
            def _():
                trip(first, 0)
                add_values(w1_sc, first)

            @pl.when(jnp.logical_not(odd))
            def _():
                add_values(w0_sc, first)

        o_ref[...] = acc_sc[...].astype(BF16)

    return pl.pallas_call(
        body, name="attn_fwd", grid=(npair, nq),
        in_specs=[_bs((blk, LANES), lambda p, i: (i, p)),
                  _bs((seq, LANES), lambda p, i: (0, npair + p)),
                  _bs((seq, LANES), lambda p, i: (0, 2 * npair + p))],
        out_specs=_bs((blk, LANES), lambda p, i: (i, p)),
        out_shape=_sds((seq, ATTN_W), BF16),
        scratch_shapes=[pltpu.VMEM((2, blk, blk), F32), pltpu.VMEM((2, blk, blk), F32),
                        pltpu.VMEM((2, blk, blk), BF16), pltpu.VMEM((2, blk, blk), BF16),
                        pltpu.VMEM((2, blk, 1), F32), pltpu.VMEM((2, blk, 1), F32), pltpu.VMEM((blk, LANES), F32)],
        compiler_params=_cparams(2),
    )(proj, proj, proj)


def _attn_bwd(proj, do, seq):
    blk = ATT_BLK
    nq = seq // blk
    npair = N_HEADS // 2

    def body(q_ref, k_ref, v_ref, do_ref, dq_ref, dk_ref, dv_ref,
             prod0_sc, prod1_sc, pend0_sc, pend1_sc, tot_sc, live_sc, cum_sc, pre_sc, dq_sc):
        i = pl.program_id(1)

        @pl.when(i == 0)
        def _():
            dk_ref[...] = jnp.zeros_like(dk_ref)
            dv_ref[...] = jnp.zeros_like(dv_ref)

        is_a = lax.broadcasted_iota(jnp.int32, (1, LANES), 1) < HEAD_DIM
        q2 = (q_ref[...] * Q_SCALE).astype(BF16)
        do2 = do_ref[...]
        zero = jnp.zeros_like(q2)
        qs = (jnp.where(is_a, q2, zero), jnp.where(is_a, zero, q2))
        dos = (jnp.where(is_a, do2, zero), jnp.where(is_a, zero, do2))
        row = lax.broadcasted_iota(jnp.int32, (blk, blk), 0)
        col = lax.broadcasted_iota(jnp.int32, (blk, blk), 1)
        tri_after = (row > col).astype(BF16)
        tri_excl = (row < col).astype(BF16)
        causal = col < row

        def tile_of(ref, j):
            return ref[pl.ds(pl.multiple_of(j * blk, blk), blk), :].astype(BF16)

        def scores(j):
            k2 = tile_of(k_ref, j)
            return [_nt(qs[h], k2) for h in range(2)]

        def products(j):
            v2 = tile_of(v_ref, j)
            return scores(j) + [_nt(dos[h], v2) for h in range(2)]

        def row_sum(a):
            return jnp.sum(a, axis=-1, keepdims=True)

        def grad_matmuls(ws, dzs, j):
            rows = pl.ds(pl.multiple_of(j * blk, blk), blk)
            k2 = tile_of(k_ref, j)
            dq_sc[...] += jnp.where(is_a, _nn(dzs[0], k2), _nn(dzs[1], k2))
            dk_ref[rows, :] += jnp.where(is_a, _tn(dzs[0], q2), _tn(dzs[1], q2))
            if ws is not None:
                dv_ref[rows, :] += jnp.where(is_a, _tn(ws[0], do2), _tn(ws[1], do2))

        has_left = i > 0
        left = jnp.maximum(i - 1, 0)

        p_d, p_l = products(i), products(left)
        g_d = [_log_gates(z) for z in p_d[:2]]
        g_l = [_log_gates(z) for z in p_l[:2]]
        keep_d = [jnp.where(causal, g[1], 0.0) for g in g_d]
        suf_d = [_nn(lk.astype(BF16), tri_after) for lk in keep_d]
        suf_l = [_nn(g[1].astype(BF16), tri_after) for g in g_l]
        w_d, w_l, gg_d, gg_l = [], [], [], []
        for h in range(2):
            sum_d = row_sum(keep_d[h])
            w_d.append(jnp.where(causal, jnp.exp(g_d[h][0] + suf_d[h]), 0.0))
            w_l.append(jnp.exp(g_l[h][0] + (jnp.where(has_left, sum_d, NO_TILE) + suf_l[h])))
            gg_d.append(p_d[2 + h] * w_d[h])
            gg_l.append(p_l[2 + h] * w_l[h])
            tot_sc[h] = sum_d + row_sum(g_l[h][1])
        before_d = [_nn(g.astype(BF16), tri_excl) for g in gg_d]
        before_l = [_nn(g.astype(BF16), tri_excl) for g in gg_l]
        dz_d, dz_l = [], []
        for h in range(2):
            beta_d, beta_l = jnp.exp(g_d[h][0]), jnp.exp(g_l[h][0])
            dz_l.append((gg_l[h] * (1.0 - beta_l) - before_l[h] * beta_l).astype(BF16))
            dz = gg_d[h] * (1.0 - beta_d) - (row_sum(gg_l[h]) + before_d[h]) * beta_d
            dz_d.append(jnp.where(causal, dz, 0.0).astype(BF16))
        dq_sc[...] = jnp.zeros_like(dq_sc)
        grad_matmuls([w.astype(BF16) for w in w_l], dz_l, left)
        grad_matmuls([w.astype(BF16) for w in w_d], dz_d, i)

        live_sc[...] = tot_sc[...]
        first = _first_live_tile(i - 2, scores, live_sc)
        trips = i - 1 - first
        prod_bufs, pend_bufs = (prod0_sc, prod1_sc), (pend0_sc, pend1_sc)

        def local_grads(prods):
            zs, dws = prods[:2], prods[2:]
            gates = [_log_gates(z) for z in zs]
            sums = [_nn(g[1].astype(BF16), tri_after) for g in gates]
            ws, gs = [], []
            for h in range(2):
                cum = cum_sc[h] + row_sum(gates[h][1])
                cum_sc[h] = cum
                ws.append(jnp.exp(gates[h][0] + ((live_sc[h] - cum) + sums[h])))
                gs.append(dws[h] * ws[h])
            befores = [_nn(g.astype(BF16), tri_excl) for g in gs]
            dzs = []
            for h in range(2):
                beta = jnp.exp(gates[h][0])
                dzs.append((gs[h] * (1.0 - beta) - (pre_sc[h] + befores[h]) * beta).astype(BF16))
                pre_sc[h] = pre_sc[h] + row_sum(gs[h])
            return [w.astype(BF16) for w in ws] + dzs

        def put(ref, vals):
            for n, val in enumerate(vals):
                ref[n] = val

        def flush(pend, j):
            grad_matmuls([pend[0], pend[1]], [pend[2], pend[3]], j)

        def trip(j, s):
            flush(pend_bufs[s], jnp.maximum(j - 1, first))
            put(prod_bufs[1 - s], products(j + 1))
            put(pend_bufs[1 - s], local_grads([prod_bufs[s][n] for n in range(4)]))

        def earlier_keys_share(j, mask):
            dzs = []
            for h, z in enumerate(scores(j)):
                beta = jnp.exp(_log_gates(z)[0])
                dzs.append(jnp.where(mask, -pre_sc[h] * beta, 0.0).astype(BF16))
            grad_matmuls(None, dzs, j)

        @pl.when(trips > 0)
        def _():
            cum_sc[...] = jnp.zeros_like(cum_sc)
            pre_sc[...] = jnp.zeros_like(pre_sc)
            pend0_sc[...] = jnp.zeros_like(pend0_sc)
            put(prod0_sc, products(first))

            def two_trips(pp, carry):
                trip(first + 2 * pp, 0)
                trip(first + 2 * pp + 1, 1)
                return carry

            lax.fori_loop(0, trips // 2, two_trips, 0)
            odd = trips % 2 == 1

            @pl.when(odd)
            def _():
                trip(i - 2, 0)
                flush(pend1_sc, i - 2)

            @pl.when(jnp.logical_not(odd))
            def _():
                flush(pend0_sc, i - 2)

            earlier_keys_share(i - 1, True)
            earlier_keys_share(i, causal)

        dq_ref[...] = dq_sc[...] * Q_SCALE

    qmap = lambda p, i: (i, p)
    return pl.pallas_call(
        body, name="attn_bwd", grid=(npair, nq),
        in_specs=[_bs((blk, LANES), qmap),
                  _bs((seq, LANES), lambda p, i: (0, npair + p)),
                  _bs((seq, LANES), lambda p, i: (0, 2 * npair + p)),
                  _bs((blk, LANES), qmap)],
        out_specs=[_bs((blk, LANES), qmap),
                   _bs((seq, LANES), lambda p, i: (0, p)),
                   _bs((seq, LANES), lambda p, i: (0, p))],
        out_shape=[_sds((seq, ATTN_W), F32)] * 3,
        scratch_shapes=[pltpu.VMEM((4, blk, blk), F32), pltpu.VMEM((4, blk, blk), F32),
                        pltpu.VMEM((4, blk, blk), BF16), pltpu.VMEM((4, blk, blk), BF16),
                        pltpu.VMEM((2, blk, 1), F32), pltpu.VMEM((2, blk, 1), F32), pltpu.VMEM((2, blk, 1), F32),
                        pltpu.VMEM((2, blk, 1), F32), pltpu.VMEM((blk, LANES), F32)],
        compiler_params=_cparams(2),
    )(proj, proj, proj, do)


def _elementwise(name, fn, ins, out_dtypes):
    rows, cols = ins[0].shape
    tr = rows
    for cand in (512, 256, 128, 64, 32, 16, 8):
        if rows % cand == 0 and cand * cols * 4 <= 2 * 1024 * 1024:
            tr = cand
            break
    n_in = len(ins)

    def body(*refs):
        res = fn(*[r[...] for r in refs[:n_in]])
        for r, val in zip(refs[n_in:], res):
            r[...] = val.astype(r.dtype)

    spec = _bs((tr, cols), lambda i: (i, 0))
    return pl.pallas_call(
        body, name=name, grid=(rows // tr,),
        in_specs=[spec] * n_in, out_specs=[spec] * len(out_dtypes),
        out_shape=[_sds((rows, cols), dt) for dt in out_dtypes],
        compiler_params=_cparams(1),
    )(*ins)


def _adamw_fn(w, g, m, v):
    m = ADAM_B1 * m + (1.0 - ADAM_B1) * g
    v = ADAM_B2 * v + (1.0 - ADAM_B2) * (g * g)
    m_hat = m / (1.0 - ADAM_B1 ** ADAM_STEP)
    v_hat = v / (1.0 - ADAM_B2 ** ADAM_STEP)
    delta = -ADAM_LR * (m_hat / (jnp.sqrt(v_hat) + ADAM_EPS) + ADAM_WD * w)
    return delta, m, v


def _adamw(name, w, g, m, v):
    shape = w.shape
    as2d = lambda a: a.reshape(-1, shape[-1])
    delta, nm, nv = _elementwise(name, _adamw_fn, [as2d(w), as2d(g), as2d(m), as2d(v)], [F32, F32, F32])
    return delta.reshape(shape), nm.reshape(shape), nv.reshape(shape)


def _place():
    x, y, c = lax.axis_index("x"), lax.axis_index("y"), lax.axis_index("c")
    chips = [(1 - x, y), (x, 1 - y), (1 - x, 1 - y)]
    return x, y, c, chips


ANY = pl.BlockSpec(memory_space=pl.ANY)
VMEM_WHOLE = pl.BlockSpec(memory_space=pltpu.VMEM)


def _allgather_weights(shards):
    n = len(shards)

    def body(*refs):
        src, dst = refs[:n], refs[n:2 * n]
        send_sems, recv_sems, local_sems = refs[2 * n:]
        x, y, c, chips = _place()
        me, sibling, mychip = (x, y, c), (x, y, 1 - c), 2 * x + y

        x_nbr, y_nbr, diag = 2 * (1 - x) + y, 2 * x + (1 - y), 2 * (1 - x) + (1 - y)
        to_x, to_y = (1 - x, y, c), (x, 1 - y, c)

        def parts(w):
            hr = src[w].shape[0] // 2
            first = hr // 2 if hr % 32 == 0 else hr
            return first, hr - first

        def rows_of(w, chip, half, route):
            hr = src[w].shape[0] // 2
            first, second = parts(w)
            start, size = {0: (0, hr), 1: (0, hr), 2: (0, first), 3: (first, second)}[route]
            return dst[w].at[chip, pl.ds(half * hr + start, size)]

        def copy(w, k, src_ref, dst_ref, to):
            return pltpu.make_async_remote_copy(src_ref=src_ref, dst_ref=dst_ref, send_sem=send_sems.at[w, k],
                                                recv_sem=recv_sems.at[w, k], device_id=to, device_id_type=MESH)

        def landed(w, route):
            chip = {0: x_nbr, 1: y_nbr, 2: diag, 3: diag}[route]
            return rows_of(w, chip, c, route), chip

        def routes(w):
            return (0, 1, 2, 3) if parts(w)[1] else (0, 1, 2)

        started, local = [], []
        for w in range(n):
            hr = src[w].shape[0] // 2
            own = pltpu.make_async_copy(src[w], dst[w].at[mychip], local_sems.at[w])
            own.start()
            local.append(own)
            mine = src[w].at[pl.ds(c * hr, hr)]
            for route, to in ((0, to_x), (1, to_y)):
                cp = copy(w, route, mine, rows_of(w, mychip, c, route), to)
                cp.start()
                started.append(cp)

        def pass_on(w, route):
            got, chip = landed(w, route)
            copy(w, route, got, got, me).wait_recv()
            if route == 1:
                part = rows_of(w, chip, c, 2)
                started.append(copy(w, 2, part, part, to_x))
                started[-1].start()
            if route == 0 and parts(w)[1]:
                part = rows_of(w, chip, c, 3)
                started.append(copy(w, 3, part, part, to_y))
                started[-1].start()
            started.append(copy(w, 4 + route, got, got, sibling))
            started[-1].start()

        for w in range(n):
            pass_on(w, 1)
            pass_on(w, 0)
        for w in range(n):
            for route in routes(w)[2:]:
                pass_on(w, route)
        for w in range(n):
            for route in routes(w):
                chip = landed(w, route)[1]
                from_sib = rows_of(w, chip, 1 - c, route)
                copy(w, 4 + route, from_sib, from_sib, me).wait_recv()
        for cp in local:
            cp.wait()
        for cp in started:
            cp.wait_send()

    return pl.pallas_call(
        body, name="allgather_weights",
        in_specs=[VMEM_WHOLE] * n, out_specs=[VMEM_WHOLE] * n,
        out_shape=[_sds((N_CHIPS,) + s.shape, s.dtype) for s in shards],
        scratch_shapes=[pltpu.SemaphoreType.DMA((n, 8)), pltpu.SemaphoreType.DMA((n, 8)),
                        pltpu.SemaphoreType.DMA((n,))],
        compiler_params=pltpu.CompilerParams(vmem_limit_bytes=VMEM_LIMIT),
    )(*shards)


SUM_ROWS = 64


def _rs_pair_sum(name, grads):
    n = len(grads)

    def body(*refs):
        g, out = refs[:n], refs[n:2 * n]
        stage, land, keep = refs[2 * n:3 * n], refs[3 * n:4 * n], refs[4 * n:5 * n]
        send_sems, recv_sems, stage_sems, keep_sems = refs[5 * n:]
        x, y, c, _ = _place()
        sibling = (x, y, 1 - c)
        loads = []
        for w in range(n):
            hr = g[w].shape[1] // 2
            st = pltpu.make_async_copy(g[w].at[:, pl.ds((1 - c) * hr, hr)], stage[w], stage_sems.at[w])
            kp = pltpu.make_async_copy(g[w].at[:, pl.ds(c * hr, hr)], keep[w], keep_sems.at[w])
            st.start()
            kp.start()
            loads.append((st, kp))
        gives = []
        for w in range(n):
            loads[w][0].wait()
            give = pltpu.make_async_remote_copy(src_ref=stage[w], dst_ref=land[w], send_sem=send_sems.at[w],
                                                recv_sem=recv_sems.at[w], device_id=sibling, device_id_type=MESH)
            give.start()
            gives.append(give)
        for w in range(n):
            loads[w][1].wait()
            gives[w].wait_recv()
            nb = g[w].shape[1] // 2 // SUM_ROWS

            def add(idx, carry, w=w, nb=nb):
                k, r = idx // nb, pl.multiple_of((idx % nb) * SUM_ROWS, SUM_ROWS)
                rows = pl.ds(r, SUM_ROWS)
                out[w][k, rows, :] = (keep[w][k, rows, :] + land[w][k, rows, :]).astype(BF16)
                return carry

            lax.fori_loop(0, N_CHIPS * nb, add, 0)
        for give in gives:
            give.wait_send()

    half = [(N_CHIPS, a.shape[1] // 2, a.shape[2]) for a in grads]
    bufs = [pltpu.VMEM(s, F32) for s in half]
    sems = pltpu.SemaphoreType.DMA((n,))
    return pl.pallas_call(
        body, name=name,
        in_specs=[ANY] * n, out_specs=[VMEM_WHOLE] * n, out_shape=[_sds(s, BF16) for s in half],
        scratch_shapes=bufs + bufs + bufs + [sems, sems, sems, sems],
        compiler_params=pltpu.CompilerParams(vmem_limit_bytes=VMEM_LIMIT),
    )(*grads)


def _rs_exchange_join(parts):
    n = len(parts)

    def body(*refs):
        t, full = refs[:n], refs[n:2 * n]
        got_x, got_y, pass_on, got_2 = (refs[m * n:(m + 1) * n] for m in range(2, 6))
        send_sems, recv_sems = refs[6 * n:]
        x, y, c, _ = _place()
        mychip, sibling = 2 * x + y, (x, y, 1 - c)
        x_nbr, y_nbr, diag = 2 * (1 - x) + y, 2 * x + (1 - y), 2 * (1 - x) + (1 - y)
        to_x, to_y = (1 - x, y, c), (x, 1 - y, c)
        sends = []

        def copy(w, k, src_ref, dst_ref, to):
            return pltpu.make_async_remote_copy(src_ref=src_ref, dst_ref=dst_ref, send_sem=send_sems.at[w, k],
                                                recv_sem=recv_sems.at[w, k], device_id=to, device_id_type=MESH)

        def start(cp):
            cp.start()
            sends.append(cp)

        def add_rows(w, count, fn):
            def step(idx, carry):
                fn(pl.ds(pl.multiple_of(idx * SUM_ROWS, SUM_ROWS), SUM_ROWS), pl.multiple_of(idx * SUM_ROWS, SUM_ROWS))
                return carry
            lax.fori_loop(0, count // SUM_ROWS, step, 0)

        f32 = lambda v: v.astype(F32)
        for w in range(n):
            ha = t[w].shape[1] // 2
            part_a, part_b = pl.ds(0, ha), pl.ds(ha, ha)
            start(copy(w, 0, t[w].at[x_nbr, part_a], got_x[w].at[0], to_x))
            start(copy(w, 1, t[w].at[diag, part_a], got_x[w].at[1], to_x))
            start(copy(w, 2, t[w].at[y_nbr, part_b], got_y[w].at[0], to_y))
            start(copy(w, 3, t[w].at[diag, part_b], got_y[w].at[1], to_y))
        for w in range(n):
            hr = t[w].shape[1]
            ha = hr // 2
            for k in (0, 1):
                copy(w, k, got_x[w].at[k], got_x[w].at[k], to_x).wait_recv()

            def sum_a(rows, r, w=w, hr=hr):
                full[w][pl.ds(pl.multiple_of(c * hr + r, SUM_ROWS), SUM_ROWS), :] = \
                    f32(t[w][mychip, rows, :]) + f32(got_x[w][0, rows, :])
                pass_on[w][rows, :] = (f32(t[w][y_nbr, rows, :]) + f32(got_x[w][1, rows, :])).astype(BF16)

            add_rows(w, ha, sum_a)
            start(copy(w, 4, pass_on[w].at[pl.ds(0, ha)], got_2[w].at[pl.ds(0, ha)], to_y))
            for k in (2, 3):
                copy(w, k, got_y[w].at[k - 2], got_y[w].at[k - 2], to_y).wait_recv()

            def sum_b(rows, r, w=w, hr=hr, ha=ha):
                lower = pl.ds(pl.multiple_of(ha + r, SUM_ROWS), SUM_ROWS)
                full[w][pl.ds(pl.multiple_of(c * hr + ha + r, SUM_ROWS), SUM_ROWS), :] = \
                    f32(t[w][mychip, lower, :]) + f32(got_y[w][0, rows, :])
                pass_on[w][lower, :] = (f32(t[w][x_nbr, lower, :]) + f32(got_y[w][1, rows, :])).astype(BF16)

            add_rows(w, ha, sum_b)
            start(copy(w, 5, pass_on[w].at[pl.ds(ha, ha)], got_2[w].at[pl.ds(ha, ha)], to_x))
        for w in range(n):
            hr = t[w].shape[1]
            ha = hr // 2
            copy(w, 4, got_2[w].at[pl.ds(0, ha)], got_2[w].at[pl.ds(0, ha)], to_y).wait_recv()
            copy(w, 5, got_2[w].at[pl.ds(ha, ha)], got_2[w].at[pl.ds(ha, ha)], to_x).wait_recv()

            def finish(rows, r, w=w, hr=hr):
                out_rows = pl.ds(pl.multiple_of(c * hr + r, SUM_ROWS), SUM_ROWS)
                full[w][out_rows, :] = full[w][out_rows, :] + f32(got_2[w][rows, :])

            add_rows(w, hr, finish)
            mine = full[w].at[pl.ds(c * hr, hr)]
            start(copy(w, 6, mine, mine, sibling))
        for w in range(n):
            hr = t[w].shape[1]
            theirs = full[w].at[pl.ds((1 - c) * hr, hr)]
            copy(w, 6, theirs, theirs, sibling).wait_recv()
        for cp in sends:
            cp.wait_send()

    half = lambda a: pltpu.VMEM((2, a.shape[1] // 2, a.shape[2]), a.dtype)
    whole = lambda a: pltpu.VMEM(a.shape[1:], a.dtype)
    return pl.pallas_call(
        body, name="rs_exchange_join",
        in_specs=[VMEM_WHOLE] * n, out_specs=[VMEM_WHOLE] * n,
        out_shape=[_sds((2 * a.shape[1], a.shape[2]), F32) for a in parts],
        scratch_shapes=[half(a) for a in parts] + [half(a) for a in parts] + [whole(a) for a in parts]
        + [whole(a) for a in parts] + [pltpu.SemaphoreType.DMA((n, 7)), pltpu.SemaphoreType.DMA((n, 7))],
        compiler_params=pltpu.CompilerParams(vmem_limit_bytes=VMEM_LIMIT),
    )(*parts)


def _small_allreduce(loss_p, dg_parts, dbg_a, dbg_c, dwc):
    ins = [loss_p] + list(dg_parts) + [dbg_a, dbg_c, dwc]
    n_in = len(ins)
    vmem = pl.BlockSpec(memory_space=pltpu.VMEM)

    def body(*refs):
        in_refs = refs[:n_in]
        out_ref, vec, buf, send_sems, recv_sems = refs[n_in:]
        x, y, c, _ = _place()
        me = 4 * x + 2 * y + c
        vec[...] = jnp.zeros_like(vec)
        vec[0:1, :] = jnp.sum(in_refs[0][...], axis=0)
        for r in range(5):
            vec[1 + r:2 + r, :] = jnp.sum(in_refs[1 + r][...], axis=0)
        vec[6:7, :] = jnp.sum(in_refs[6][...], axis=0)
        vec[7:8, :] = jnp.sum(in_refs[7][...], axis=0)
        vec[8:16, 0:CONV_W] = jnp.sum(in_refs[8][...], axis=0)
        buf[pl.ds(me, 1)] = vec[...][None]
        copies = []
        for r in range(1, 8):
            fx, fy, fc = (r >> 2) & 1, (r >> 1) & 1, r & 1
            to = (1 - x if fx else x, 1 - y if fy else y, 1 - c if fc else c)
            cp = pltpu.make_async_remote_copy(src_ref=vec, dst_ref=buf.at[me], send_sem=send_sems.at[r - 1],
                                              recv_sem=recv_sems.at[r - 1], device_id=to, device_id_type=MESH)
            cp.start()
            copies.append(cp)
        for cp in copies:
            cp.wait()
        total = buf[0]
        for s in range(1, 8):
            total = total + buf[s]
        out_ref[...] = total
        out_ref[0:1, :] = jnp.broadcast_to(jnp.sum(total[0:1, :], axis=-1, keepdims=True), (1, D_MODEL))

    return pl.pallas_call(
        body, name="small_allreduce",
        in_specs=[vmem] * n_in, out_specs=vmem, out_shape=_sds((SMALL_ROWS, D_MODEL), F32),
        scratch_shapes=[pltpu.VMEM((SMALL_ROWS, D_MODEL), F32), pltpu.VMEM((8, SMALL_ROWS, D_MODEL), F32),
                        pltpu.SemaphoreType.DMA((7,)), pltpu.SemaphoreType.DMA((7,))],
    )(*ins)


def _local_step(x, p, tgt, g, b_gate, w_conv, wf):
    seq = x.shape[0]
    tm = min(seq, 1024)
    th = min(seq, 512)
    tl = min(seq, 2048)
    ni, nh, nl = seq // tm, seq // th, seq // tl
    g_pre_mix, g_post_mix, g_pre_mlp, g_post_mlp, g_ple = g
    w_in, w_ao, w_co, w_o, w_up, w_down, w_pg, w_pp, w_in_nat, w_up_nat = wf
    D = D_MODEL
    vec = lambda a, blk=0: (a, _bs((1, D), lambda i, j, k: (0, blk)))
    rows_i = lambda a, t, blk=0: (a, _bs((t, D), lambda i, j, k: (i, blk)))
    rows_k = lambda a, t, blk=0: (a, _bs((t, D), lambda i, j, k: (k, blk)))
    part = lambda n: (_sds((n, 1, D), F32), _bs((None, 1, D), lambda i, j, k: (i, 0, 0)))
    full2 = lambda a: (a, _bs(a.shape, lambda i, j, k: (0, 0)))

    normed = lambda xb, gb: (_rms(xb, gb).astype(BF16),) * 2
    keep_a = lambda t: [(_sds((seq, D), BF16), _bs((t, D), lambda i, j, k: (i, 0)))]
    main_w = D_IN - 2 * D
    proj, gates, h1 = _mm("proj_in", "nn", (nh, 1, 1),
                          a_ins=[rows_i(x, th), vec(g_pre_mix)], a_fn=normed,
                          b_ins=[full2(w_in_nat)], b_fn=_ident,
                          epi_fn=lambda acc: (acc[:, :main_w], acc[:, main_w:]),
                          outs=[(_sds((seq, main_w), F32), _bs((th, main_w), lambda i, j, k: (i, 0))),
                                (_sds((seq, 2 * D), BF16), _bs((th, 2 * D), lambda i, j, k: (i, 0)))],
                          acc_shape=(th, D_IN), a_cache=((th, D), BF16), a_outs=keep_a(th))
    o = _attn_fwd(proj, seq)
    e = _conv_fwd(proj, w_conv, seq, tm)

    def gate_values(ga, gc, ba, bc):
        return _sig(ga.astype(F32) + ba), _sig(gc.astype(F32) + bc)

    def branch_outputs(ob, eb, wao, wco):
        return _nn(ob, wao).astype(BF16).astype(F32), _nn(eb, wco).astype(BF16).astype(F32)

    def mix_fn(ga, gc, ob, eb, ba, bc, wao, wco):
        sa, sc = gate_values(ga, gc, ba, bc)
        ya, yc = branch_outputs(ob, eb, wao, wco)
        return ((sa * ya + sc * yc).astype(BF16),) * 2

    def post_mix(acc, xb, gb):
        return acc, xb + _rms(acc, gb)

    half_rows = lambda a: (a, _bs((th, a.shape[1]), lambda i, j, k: (i, 0)))
    mix_ins = [rows_i(gates, th, 0), rows_i(gates, th, 1), half_rows(o), half_rows(e), vec(b_gate, 0), vec(b_gate, 1),
               full2(w_ao), full2(w_co)]
    mixed, x1, mixin = _mm(
        "mix_out", "nn", (nh, 1, 1),
        a_ins=mix_ins, a_fn=mix_fn, b_ins=[full2(w_o)], b_fn=_ident,
        epi_ins=[rows_i(x, th), vec(g_post_mix)], epi_fn=post_mix,
        outs=[(_sds((seq, D), BF16), _bs((th, D), lambda i, j, k: (i, 0))),
              (_sds((seq, D), F32), _bs((th, D), lambda i, j, k: (i, 0)))],
        acc_shape=(th, D), a_cache=((th, D), BF16), a_outs=keep_a(th))
    up, h2 = _mm("mlp_up", "nn", (nh, 1, 1),
                 a_ins=[rows_i(x1, th), vec(g_pre_mlp)], a_fn=normed,
                 b_ins=[full2(w_up_nat)], b_fn=_ident,
                 outs=[(_sds((seq, D_FF), BF16), _bs((th, D_FF), lambda i, j, k: (i, 0)))],
                 acc_shape=(th, D_FF), a_cache=((th, D), BF16), a_outs=keep_a(th))

    def relu2(ub):
        r = jnp.maximum(ub.astype(F32), 0.0)
        return (r * r).astype(BF16)

    dx2, df, dpre, h3, dpp, loss_p, dg_ple_p, dg_post_mlp_p = _mlp_down_ple_head(
        up, x1, p, tgt, g_ple, g_post_mlp, w_down, w_pg, w_pp, seq, th)

    (dw_pp,) = _mm("dw_ple_proj", "tn", (1, 1, nh),
                   a_ins=[(p, _bs((th, PLE_DIM), lambda i, j, k: (k, 0)))], a_fn=_to_bf16,
                   b_ins=[rows_k(dpp, th)], b_fn=_ident,
                   outs=[(_sds((PLE_DIM, D), F32), _bs((PLE_DIM, D), lambda i, j, k: (0, 0)))],
                   acc_shape=(PLE_DIM, D))
    (dw_pg,) = _mm("dw_ple_gate", "tn", (1, 1, nl),
                   a_ins=[rows_k(h3, tl)], a_fn=_ident, b_ins=[rows_k(dpre, tl)], b_fn=_ident,
                   outs=[(_sds((D, D), F32), _bs((D, D), lambda i, j, k: (0, 0)))], acc_shape=(D, D))

    def dup_fn(acc, ub):
        return (acc * (2.0 * jnp.maximum(ub.astype(F32), 0.0)),)

    (dup,) = _mm("d_mlp_down", "nt", (nh, 1, 1),
                 a_ins=[rows_i(df, th)], a_fn=_ident, b_ins=[full2(w_down)], b_fn=_ident,
                 epi_ins=[(up, _bs((th, D_FF), lambda i, j, k: (i, 0)))], epi_fn=dup_fn,
                 outs=[(_sds((seq, D_FF), BF16), _bs((th, D_FF), lambda i, j, k: (i, 0)))],
                 acc_shape=(th, D_FF))
    (dw_down,) = _mm("dw_mlp_down", "tn", (4, 1, nl),
                     a_ins=[(up, _bs((tl, D), lambda i, j, k: (k, i)))], a_fn=relu2,
                     b_ins=[rows_k(df, tl)], b_fn=_ident,
                     outs=[(_sds((D_FF, D), F32), _bs((D, D), lambda i, j, k: (i, 0)))], acc_shape=(D, D))
    (dw_up,) = _mm("dw_mlp_up", "tn", (1, 4, nl),
                   a_ins=[rows_k(h2, tl)], a_fn=_ident,
                   b_ins=[(dup, _bs((tl, D), lambda i, j, k: (k, j)))], b_fn=_ident,
                   outs=[(_sds((N_CHIPS, D, D), F32), _bs((None, D, D), lambda i, j, k: (j, 0, 0)))],
                   acc_shape=(D, D))

    def mlp_norm_bwd(acc, x1b, dx2b, mixedb, g_mlp, g_mix):
        dxn, dg_mlp = _rms_bwd(x1b, g_mlp, acc)
        dx1b = dx2b + dxn
        dmixedb, dg_mix = _rms_bwd(mixedb.astype(F32), g_mix, dx1b)
        return dx1b, dmixedb, dg_mlp, dg_mix

    dx1, dmixed, dg_pre_mlp_p, dg_post_mix_p = _mm(
        "d_mlp_up", "nt", (nh, 1, 1),
        a_ins=[(dup, _bs((th, D_FF), lambda i, j, k: (i, 0)))], a_fn=_ident,
        b_ins=[full2(w_up_nat)], b_fn=_ident,
        epi_ins=[rows_i(x1, th), rows_i(dx2, th), rows_i(mixed, th), vec(g_pre_mlp), vec(g_post_mix)],
        epi_fn=mlp_norm_bwd,
        outs=[(_sds((seq, D), F32), _bs((th, D), lambda i, j, k: (i, 0))),
              (_sds((seq, D), BF16), _bs((th, D), lambda i, j, k: (i, 0))), part(nh), part(nh)],
        acc_shape=(th, D))
    (dw_o,) = _mm("dw_mix_out", "tn", (1, 1, nl),
                  a_ins=[rows_k(mixin, tl)], a_fn=_ident, b_ins=[rows_k(dmixed, tl)], b_fn=_ident,
                  outs=[(_sds((D, D), F32), _bs((D, D), lambda i, j, k: (0, 0)))], acc_shape=(D, D))

    def gate_bwd(acc, ga, gc, ob, eb, ba, bc, wao, wco):
        sa, sc = gate_values(ga, gc, ba, bc)
        ya, yc = branch_outputs(ob, eb, wao, wco)
        dga = acc * ya * sa * (1.0 - sa)
        dgc = acc * yc * sc * (1.0 - sc)
        dya, dyc = (acc * sa).astype(BF16), (acc * sc).astype(BF16)
        return (dya, dyc, jnp.concatenate([dga, dgc], axis=1), _nt(dya, wao), _nt(dyc, wco),
                jnp.sum(dga, axis=0, keepdims=True), jnp.sum(dgc, axis=0, keepdims=True))

    dya, dyc, dgate, do, de, dbg_a_p, dbg_c_p = _mm(
        "d_mix_out", "nt", (nh, 1, 1),
        a_ins=[rows_i(dmixed, th)], a_fn=_ident, b_ins=[full2(w_o)], b_fn=_ident,
        epi_ins=mix_ins, epi_fn=gate_bwd,
        outs=[(_sds((seq, D), BF16), _bs((th, D), lambda i, j, k: (i, 0)))] * 2
             + [(_sds((seq, 2 * D), BF16), _bs((th, 2 * D), lambda i, j, k: (i, 0))),
                (_sds((seq, ATTN_W), BF16), _bs((th, ATTN_W), lambda i, j, k: (i, 0))),
                (_sds((seq, CONV_W), F32), _bs((th, CONV_W), lambda i, j, k: (i, 0))), part(nh), part(nh)],
        acc_shape=(th, D))
    (dw_ao,) = _mm("dw_attn_out", "tn", (1, 1, nh),
                   a_ins=[(o, _bs((th, ATTN_W), lambda i, j, k: (k, 0)))], a_fn=_ident,
                   b_ins=[rows_k(dya, th)], b_fn=_ident,
                   outs=[(_sds((ATTN_W, D), F32), _bs((ATTN_W, D), lambda i, j, k: (0, 0)))], acc_shape=(ATTN_W, D))
    dq, dk, dv = _attn_bwd(proj, do, seq)
    (dw_co,) = _mm("dw_conv_out", "tn", (1, 1, nh),
                   a_ins=[(e, _bs((th, CONV_W), lambda i, j, k: (k, 0)))], a_fn=_ident,
                   b_ins=[rows_k(dyc, th)], b_fn=_ident,
                   outs=[(_sds((CONV_W, D), F32), _bs((CONV_W, D), lambda i, j, k: (0, 0)))], acc_shape=(CONV_W, D))
    dconv, dwc_p = _conv_bwd(proj, de, w_conv, seq, tm)
    qkv_w = 3 * ATTN_W
    join_bf16 = lambda *blocks: jnp.concatenate([b.astype(BF16) for b in blocks], axis=1)
    piece = lambda a, t, rows, blk=0: (a, _bs((t, a.shape[1]), (lambda i, j, k: (k, blk)) if rows == "k"
                                             else (lambda i, j, k: (i, blk))))
    (dw_in_qkv,) = _mm("dw_proj_in_qkv", "tn", (1, 1, ni),
                       a_ins=[rows_k(h1, tm)], a_fn=_ident,
                       b_ins=[piece(dq, tm, "k"), piece(dk, tm, "k"), piece(dv, tm, "k")], b_fn=join_bf16,
                       outs=[(_sds((D, qkv_w), F32), _bs((D, qkv_w), lambda i, j, k: (0, 0)))], acc_shape=(D, qkv_w))
    (dw_in_conv,) = _mm("dw_proj_in_conv", "tn", (1, 1, nl),
                        a_ins=[rows_k(h1, tl)], a_fn=_ident, b_ins=[piece(dconv, tl, "k")], b_fn=_ident,
                        outs=[(_sds((D, 3 * CONV_W), F32), _bs((D, 3 * CONV_W), lambda i, j, k: (0, 0)))],
                        acc_shape=(D, 3 * CONV_W))
    (dw_in_gate,) = _mm("dw_proj_in_gate", "tn", (1, 2, nl),
                        a_ins=[rows_k(h1, tl)], a_fn=_ident,
                        b_ins=[(dgate, _bs((tl, D), lambda i, j, k: (k, j)))], b_fn=_ident,
                        outs=[(_sds((D, 2 * D), F32), _bs((D, D), lambda i, j, k: (0, j)))], acc_shape=(D, D))
    dw_in = jnp.concatenate([dw_in_qkv, dw_in_conv, dw_in_gate], axis=1)

    def in_norm_bwd(acc, xb, dx1b, gb):
        dxn, dg = _rms_bwd(xb, gb, acc)
        return dx1b + dxn, dg

    grad_x, dg_pre_mix_p = _mm("d_proj_in", "nt", (nh, 1, 1),
                               a_ins=[piece(dq, th, "i"), piece(dk, th, "i"), piece(dv, th, "i"),
                                      piece(dconv, th, "i"), piece(dgate, th, "i")], a_fn=join_bf16,
                               b_ins=[full2(w_in_nat)], b_fn=_ident,
                               epi_ins=[rows_i(x, th), rows_i(dx1, th), vec(g_pre_mix)], epi_fn=in_norm_bwd,
                               outs=[(_sds((seq, D), F32), _bs((th, D), lambda i, j, k: (i, 0))), part(nh)],
                               acc_shape=(th, D))

    chip_major = lambda a: a.reshape(a.shape[0], N_CHIPS, a.shape[1] // N_CHIPS).transpose(1, 0, 2)
    big = [chip_major(dw_in), chip_major(dw_ao), chip_major(dw_co), dw_o.reshape(N_CHIPS, D // N_CHIPS, D), dw_up,
           dw_down.reshape(N_CHIPS, D_FF // N_CHIPS, D), dw_pg.reshape(N_CHIPS, D // N_CHIPS, D), chip_major(dw_pp)]
    small = (loss_p, [dg_pre_mix_p, dg_post_mix_p, dg_pre_mlp_p, dg_post_mlp_p, dg_ple_p], dbg_a_p, dbg_c_p, dwc_p)
    return grad_x, big, small


RS_GROUPS = ((0,), (4,), (5,), (1, 2, 3, 6, 7))


def _reduce_scatter(big):
    pair = [None] * len(big)
    for gi, group in enumerate(RS_GROUPS):
        for w, s in zip(group, _rs_pair_sum(f"rs_pair_sum_{gi}", [big[w] for w in group])):
            pair[w] = s
    return _rs_exchange_join(pair)


def kernel(x, p, g_pre_mix, w_in, b_gate, w_conv, w_attn_out, w_conv_out, w_o, g_post_mix, g_pre_mlp, w_up, w_down, g_post_mlp, g_ple, w_ple_gate, w_ple_proj, loss_target, m_g_pre_mix, m_w_in, m_b_gate, m_w_conv, m_w_attn_out, m_w_conv_out, m_w_o, m_g_post_mix, m_g_pre_mlp, m_w_up, m_w_down, m_g_post_mlp, m_g_ple, m_w_ple_gate, m_w_ple_proj, v_g_pre_mix, v_w_in, v_b_gate, v_w_conv, v_w_attn_out, v_w_conv_out, v_w_o, v_g_post_mix, v_g_pre_mlp, v_w_up, v_w_down, v_g_post_mlp, v_g_ple, v_w_ple_gate, v_w_ple_proj):
    mats = [w_in, w_attn_out, w_conv_out, w_o, w_up, w_down, w_ple_gate, w_ple_proj]
    mats_m = [m_w_in, m_w_attn_out, m_w_conv_out, m_w_o, m_w_up, m_w_down, m_w_ple_gate, m_w_ple_proj]
    mats_v = [v_w_in, v_w_attn_out, v_w_conv_out, v_w_o, v_w_up, v_w_down, v_w_ple_gate, v_w_ple_proj]
    gains = [g_pre_mix, g_post_mix, g_pre_mlp, g_post_mlp, g_ple]
    gains_m = [m_g_pre_mix, m_g_post_mix, m_g_pre_mlp, m_g_post_mlp, m_g_ple]
    gains_v = [v_g_pre_mix, v_g_post_mix, v_g_pre_mlp, v_g_post_mlp, v_g_ple]

    taps = jnp.concatenate([w_conv[0], jnp.zeros((CONV_PAD_ROWS - 3, LANES), F32)], axis=0)
    gathered = _allgather_weights([w[0].astype(BF16) for w in mats] + [taps])
    cols_joined = lambda a: a.transpose(1, 0, 2).reshape(a.shape[1], N_CHIPS * a.shape[2])
    rows_joined = lambda a: a.reshape(N_CHIPS * a.shape[1], a.shape[2])
    wf = [gathered[0], cols_joined(gathered[1]), cols_joined(gathered[2]), rows_joined(gathered[3]), gathered[4],
          rows_joined(gathered[5]), rows_joined(gathered[6]), cols_joined(gathered[7]),
          cols_joined(gathered[0]), cols_joined(gathered[4])]
    w_conv_full = cols_joined(gathered[8])[0:3, :]
    chip = 2 * lax.axis_index("x") + lax.axis_index("y")

    grad_x, big, small = _local_step(x[0], p[0, 0], loss_target[0], gains, b_gate, w_conv_full, wf)

    shard_grads = _reduce_scatter(big)
    red = _small_allreduce(*small)
    loss = red[0, 0]
    grad_gains = [red[1 + r:2 + r, :] for r in range(5)]
    grad_b_gate = jnp.concatenate([red[6:7, :], red[7:8, :]], axis=1)
    grad_w_conv = lax.dynamic_slice(red[8:11, :], (0, chip * LANES), (3, LANES))[None]

    grads_big = [gr.reshape(w.shape) for gr, w in zip(shard_grads, mats)]
    upd_big = [_adamw(f"adamw_{i}", w, gr, m, v) for i, (w, gr, m, v) in enumerate(zip(mats, grads_big, mats_m, mats_v))]
    pack = lambda vs, bg: jnp.concatenate(list(vs) + [bg.reshape(2, D_MODEL), jnp.zeros((1, D_MODEL), F32)], axis=0)
    upd_small = _adamw("adamw_small", pack(gains, b_gate), pack(grad_gains, grad_b_gate),
                       pack(gains_m, m_b_gate), pack(gains_v, v_b_gate))
    upd_conv = _adamw("adamw_conv", w_conv, grad_w_conv, m_w_conv, v_w_conv)

    def small_out(a, which):
        gains_out = [a[r:r + 1, :] for r in range(5)]
        return gains_out, a[5:7, :].reshape(1, 2 * D_MODEL)

    def ordered(g_pre_mix_, big_, b_gate_, conv_, g_rest):
        return [g_pre_mix_, big_[0], b_gate_, conv_, big_[1], big_[2], big_[3], g_rest[0], g_rest[1], big_[4], big_[5],
                g_rest[2], g_rest[3], big_[6], big_[7]]

    outs = [loss, grad_x[None]]
    outs += ordered(grad_gains[0], grads_big, grad_b_gate, grad_w_conv, grad_gains[1:])
    for which in range(3):
        g_out, b_out = small_out(upd_small[which], which)
        outs += ordered(g_out[0], [u[which] for u in upd_big], b_out, upd_conv[which], g_out[1:])
    return tuple(outs)
```

```python
import functools

import jax
import jax.numpy as jnp
from jax import lax
from jax.experimental import pallas as pl
from jax.experimental.pallas import tpu as pltpu

F32 = jnp.float32
BF16 = jnp.bfloat16
MESH = pl.DeviceIdType.MESH

D_MODEL = 1024
N_HEADS = 8
HEAD_DIM = 64
ATTN_W = N_HEADS * HEAD_DIM
CONV_W = 512
D_FF = 4096
PLE_DIM = 256
D_IN = 5120
N_CHIPS = 4
EPS = 1e-6
Q_SCALE = HEAD_DIM ** -0.5

ADAM_LR = 0.001
ADAM_B1 = 0.9
ADAM_B2 = 0.999
ADAM_EPS = 1e-08
ADAM_WD = 0.01
ADAM_STEP = 10

V7X_VMEM_BYTES = 64 * 1024 * 1024
VMEM_LIMIT = V7X_VMEM_BYTES - 8 * 1024 * 1024
LANES = 128
ATT_BLK = 256
SMALL_ROWS = 16
CONV_PAD_ROWS = 16


def _cparams(n_grid):
    return pltpu.CompilerParams(dimension_semantics=("arbitrary",) * n_grid, vmem_limit_bytes=VMEM_LIMIT)


def _bs(shape, fn):
    return pl.BlockSpec(shape, fn)


def _rms_stats(xf):
    return lax.rsqrt(jnp.mean(xf * xf, axis=-1, keepdims=True) + EPS)


def _rms(xf, g):
    return xf * _rms_stats(xf) * g


def _rms_bwd(xf, g, dy):
    r = _rms_stats(xf)
    xh = xf * r
    dyg = dy * g
    dx = r * (dyg - xh * jnp.mean(dyg * xh, axis=-1, keepdims=True))
    return dx, jnp.sum(dy * xh, axis=0, keepdims=True)


def _sig(z):
    return 1.0 / (1.0 + jnp.exp(-z))


def _ident(a):
    return a


def _to_bf16(a):
    return a.astype(BF16)


_DIMS = {"nn": (((1,), (0,)), ((), ())), "nt": (((1,), (1,)), ((), ())), "tn": (((0,), (0,)), ((), ()))}


def _mm(name, mode, grid, a_ins, a_fn, b_ins, b_fn, outs, acc_shape, epi_ins=(), epi_fn=None,
        a_cache=None, a_outs=(), epi_a=()):
    nk = grid[2]
    na, nb, ne, no, nao = len(a_ins), len(b_ins), len(epi_ins), len(outs), len(a_outs)
    assert a_cache is None or nk == 1
    assert not a_outs or a_cache is not None
    dims = _DIMS[mode]
    if epi_fn is None:
        epi_fn = lambda acc: (acc,)

    def body(*refs):
        a_refs = refs[:na]
        b_refs = refs[na:na + nb]
        e_refs = refs[na + nb:na + nb + ne]
        o_refs = refs[na + nb + ne:na + nb + ne + no]
        ao_refs = refs[na + nb + ne + no:na + nb + ne + no + nao]
        scratch = list(refs[na + nb + ne + no + nao:])
        acc_ref = scratch.pop(0) if nk > 1 else None
        a_sc = scratch.pop(0) if a_cache is not None else None
        j = pl.program_id(1)
        k = pl.program_id(2)

        def finish(acc):
            res = epi_fn(acc, *[a_refs[t][...] for t in epi_a], *[r[...] for r in e_refs])
            for r, val in zip(o_refs, res):
                r[...] = val.astype(r.dtype)

        if a_sc is not None:
            @pl.when(j == 0)
            def _():
                res = a_fn(*[r[...] for r in a_refs])
                if nao:
                    for r, val in zip(ao_refs, res[1:]):
                        r[...] = val.astype(r.dtype)
                    res = res[0]
                a_sc[...] = res
            a = a_sc[...]
        else:
            a = a_fn(*[r[...] for r in a_refs])
        b = b_fn(*[r[...] for r in b_refs])
        prod = lax.dot_general(a, b, dims, preferred_element_type=F32)
        if nk == 1:
            finish(prod)
        else:
            @pl.when(k == 0)
            def _():
                acc_ref[...] = prod

            @pl.when(k > 0)
            def _():
                acc_ref[...] += prod

            @pl.when(k == nk - 1)
            def _():
                finish(acc_ref[...])

    scratch_shapes = []
    if nk > 1:
        scratch_shapes.append(pltpu.VMEM(acc_shape, F32))
    if a_cache is not None:
        scratch_shapes.append(pltpu.VMEM(*a_cache))
    all_outs = list(outs) + list(a_outs)
    res = pl.pallas_call(
        body, name=name, grid=grid,
        in_specs=[s for _, s in a_ins] + [s for _, s in b_ins] + [s for _, s in epi_ins],
        out_specs=[s for _, s in all_outs],
        out_shape=[o for o, _ in all_outs],
        scratch_shapes=scratch_shapes,
        compiler_params=_cparams(3),
    )(*[a for a, _ in a_ins], *[a for a, _ in b_ins], *[a for a, _ in epi_ins])
    return res


def _sds(shape, dtype):
    return jax.ShapeDtypeStruct(shape, dtype)


def _nt(a, b):
    return lax.dot_general(a, b, _DIMS["nt"], preferred_element_type=F32)


def _tn(a, b):
    return lax.dot_general(a, b, _DIMS["tn"], preferred_element_type=F32)


def _nn(a, b):
    return lax.dot_general(a, b, _DIMS["nn"], preferred_element_type=F32)


def _mlp_down_ple_head(up, x1, p, tgt, g_ple, g_post_mlp, w_down, w_pg, w_pp, seq, tr):
    nblk = seq // tr
    D = D_MODEL

    def body(up_ref, x1_ref, p_ref, t_ref, gp_ref, gm_ref, wd_ref, wpg_ref, wpp_ref,
             dx2_ref, df_ref, dpre_ref, h3_ref, dpp_ref, loss_ref, dgp_ref, dgm_ref):
        gp, gm, wpg, wpp = gp_ref[...], gm_ref[...], wpg_ref[...], wpp_ref[...]
        halves = [pl.ds(0, tr // 2), pl.ds(tr // 2, tr // 2)]
        w_down = wd_ref[...]
        fb = []
        for r in halves:
            hidden = jnp.maximum(up_ref[r, :].astype(F32), 0.0)
            fb.append(_nn((hidden * hidden).astype(BF16), w_down))
        x2b = [x1_ref[r, :] + _rms(fb[s], gm) for s, r in enumerate(halves)]
        h3 = [_rms(x, gp).astype(BF16) for x in x2b]
        gate = [_sig(_nn(h, wpg)) for h in h3]
        pp = [_nn(p_ref[r, :].astype(BF16), wpp) for r in halves]
        err = [x2b[s] + gate[s] * pp[s] - t_ref[r, :] for s, r in enumerate(halves)]
        dx3 = [e * (1.0 / D) for e in err]
        dpre = [(dx3[s] * pp[s] * gate[s] * (1.0 - gate[s])).astype(BF16) for s in range(2)]
        dh3 = [_nt(d, wpg) for d in dpre]
        loss, dgp_sum, dgm_sum = 0.0, 0.0, 0.0
        for s, r in enumerate(halves):
            h3_ref[r, :] = h3[s]
            dpp_ref[r, :] = (dx3[s] * gate[s]).astype(BF16)
            dpre_ref[r, :] = dpre[s]
            dxn, dgp = _rms_bwd(x2b[s], gp, dh3[s])
            dx2 = dx3[s] + dxn
            dx2_ref[r, :] = dx2
            dfb, dgm = _rms_bwd(fb[s], gm, dx2)
            df_ref[r, :] = dfb.astype(BF16)
            loss = loss + jnp.sum(err[s] * err[s], axis=0, keepdims=True)
            dgp_sum, dgm_sum = dgp_sum + dgp, dgm_sum + dgm
        loss_ref[...] = loss * (0.5 / D)
        dgp_ref[...] = dgp_sum
        dgm_ref[...] = dgm_sum

    rows = _bs((tr, D), lambda i: (i, 0))
    vec = _bs((1, D), lambda i: (0, 0))
    part = _bs((None, 1, D), lambda i: (i, 0, 0))
    return pl.pallas_call(
        body, name="mlp_down_ple_head", grid=(nblk,),
        in_specs=[_bs((tr, D_FF), lambda i: (i, 0)), rows, _bs((tr, PLE_DIM), lambda i: (i, 0)), rows, vec, vec,
                  _bs((D_FF, D), lambda i: (0, 0)), _bs((D, D), lambda i: (0, 0)), _bs((PLE_DIM, D), lambda i: (0, 0))],
        out_specs=[rows] * 5 + [part] * 3,
        out_shape=[_sds((seq, D), F32)] + [_sds((seq, D), BF16)] * 4 + [_sds((nblk, 1, D), F32)] * 3,
        compiler_params=_cparams(1),
    )(up, x1, p, tgt, g_ple, g_post_mlp, w_down, w_pg, w_pp)


def _shift_rows_down(u, prev, n):
    rows = u.shape[0]
    ridx = lax.broadcasted_iota(jnp.int32, u.shape, 0)
    out = pltpu.roll(u, n, 0)
    for r in range(n):
        out = jnp.where(ridx == r, prev[8 - n + r:8 - n + r + 1, :], out)
    del rows
    return out


def _shift_rows_up(u, nxt, n):
    rows = u.shape[0]
    ridx = lax.broadcasted_iota(jnp.int32, u.shape, 0)
    out = pltpu.roll(u, rows - n, 0)
    for r in range(n):
        out = jnp.where(ridx == rows - n + r, nxt[r:r + 1, :], out)
    return out


CONV_COL0 = 3


def _conv_fwd(proj, w_conv, seq, tr):
    hb = tr // 8

    def body(cb_ref, cc_ref, cu_ref, ccp_ref, cup_ref, w_ref, e_ref):
        i = pl.program_id(0)
        u = cc_ref[...] * cu_ref[...]
        up = jnp.where(i > 0, ccp_ref[...] * cup_ref[...], 0.0)
        w = w_ref[...]
        d = w[0:1, :] * _shift_rows_down(u, up, 2) + w[1:2, :] * _shift_rows_down(u, up, 1) + w[2:3, :] * u
        e_ref[...] = (cb_ref[...] * d).astype(BF16)

    prev = lambda c: (lambda i: (jnp.maximum(i * hb - 1, 0), c))
    return pl.pallas_call(
        body, name="conv_fwd", grid=(seq // tr,),
        in_specs=[_bs((tr, CONV_W), lambda i: (i, CONV_COL0)),
                  _bs((tr, CONV_W), lambda i: (i, CONV_COL0 + 1)),
                  _bs((tr, CONV_W), lambda i: (i, CONV_COL0 + 2)),
                  _bs((8, CONV_W), prev(CONV_COL0 + 1)),
                  _bs((8, CONV_W), prev(CONV_COL0 + 2)),
                  _bs((3, CONV_W), lambda i: (0, 0))],
        out_specs=_bs((tr, CONV_W), lambda i: (i, 0)),
        out_shape=_sds((seq, CONV_W), BF16),
        compiler_params=_cparams(1),
    )(proj, proj, proj, proj, proj, w_conv)


def _conv_bwd(proj, de, w_conv, seq, tr):
    hb = tr // 8
    nblk = seq // tr

    def body(cb_ref, cc_ref, cu_ref, ccp_ref, cup_ref, cbn_ref, de_ref, den_ref, w_ref, o_ref, dw_ref):
        i = pl.program_id(0)
        cc, cu, cb = cc_ref[...], cu_ref[...], cb_ref[...]
        u = cc * cu
        up = jnp.where(i > 0, ccp_ref[...] * cup_ref[...], 0.0)
        u1 = _shift_rows_down(u, up, 1)
        u2 = _shift_rows_down(u, up, 2)
        de_ = de_ref[...]
        dd = de_ * cb
        ddn = jnp.where(i < nblk - 1, den_ref[...] * cbn_ref[...], 0.0)
        w = w_ref[...]
        du = w[2:3, :] * dd + w[1:2, :] * _shift_rows_up(dd, ddn, 1) + w[0:1, :] * _shift_rows_up(dd, ddn, 2)
        o_ref[:, 0:CONV_W] = (de_ * (w[0:1, :] * u2 + w[1:2, :] * u1 + w[2:3, :] * u)).astype(BF16)
        o_ref[:, CONV_W:2 * CONV_W] = (du * cu).astype(BF16)
        o_ref[:, 2 * CONV_W:3 * CONV_W] = (du * cc).astype(BF16)
        ridx = lax.broadcasted_iota(jnp.int32, (8, CONV_W), 0)
        dw0 = jnp.sum(dd * u2, axis=0, keepdims=True)
        dw1 = jnp.sum(dd * u1, axis=0, keepdims=True)
        dw2 = jnp.sum(dd * u, axis=0, keepdims=True)
        dw_ref[...] = jnp.where(ridx == 0, dw0, jnp.where(ridx == 1, dw1, jnp.where(ridx == 2, dw2, 0.0)))

    prev = lambda c: (lambda i: (jnp.maximum(i * hb - 1, 0), c))
    nxt = lambda c: (lambda i: (jnp.minimum((i + 1) * hb, seq // 8 - 1), c))
    return pl.pallas_call(
        body, name="conv_bwd", grid=(nblk,),
        in_specs=[_bs((tr, CONV_W), lambda i: (i, CONV_COL0)),
                  _bs((tr, CONV_W), lambda i: (i, CONV_COL0 + 1)),
                  _bs((tr, CONV_W), lambda i: (i, CONV_COL0 + 2)),
                  _bs((8, CONV_W), prev(CONV_COL0 + 1)),
                  _bs((8, CONV_W), prev(CONV_COL0 + 2)),
                  _bs((8, CONV_W), nxt(CONV_COL0)),
                  _bs((tr, CONV_W), lambda i: (i, 0)),
                  _bs((8, CONV_W), nxt(0)),
                  _bs((3, CONV_W), lambda i: (0, 0))],
        out_specs=[_bs((tr, 3 * CONV_W), lambda i: (i, 0)), _bs((None, 8, CONV_W), lambda i: (i, 0, 0))],
        out_shape=[_sds((seq, 3 * CONV_W), BF16), _sds((nblk, 8, CONV_W), F32)],
        compiler_params=_cparams(1),
    )(proj, proj, proj, proj, proj, proj, de, de, w_conv)


def _log_gates(z):
    lse = jnp.log(1.0 + jnp.exp(-jnp.abs(z)))
    log_beta = jnp.minimum(z, 0.0) - lse
    return log_beta, log_beta - z


DEAD_LOG_WEIGHT = -1e30
NO_TILE = -1e30


def _first_live_tile(start, scores, live_sc):
    def alive():
        return jnp.max(jnp.maximum(live_sc[0], live_sc[1])) > DEAD_LOG_WEIGHT

    def step(c):
        for h, z in enumerate(scores(c[0])):
            live_sc[h] = live_sc[h] + jnp.sum(_log_gates(z)[1], axis=-1, keepdims=True)
        return c[0] - 1, alive()

    j_end, _ = lax.while_loop(lambda c: jnp.logical_and(c[0] >= 0, c[1]), step, (start, alive()))
    return j_end + 1


def _attn_fwd(proj, seq):
    blk = ATT_BLK
    nq = seq // blk
    npair = N_HEADS // 2

    def body(q_ref, k_ref, v_ref, o_ref, z0_sc, z1_sc, w0_sc, w1_sc, tot_sc, live_sc, acc_sc):
        i = pl.program_id(1)
        is_a = lax.broadcasted_iota(jnp.int32, (1, LANES), 1) < HEAD_DIM
        q2 = (q_ref[...] * Q_SCALE).astype(BF16)
        zero = jnp.zeros_like(q2)
        qs = (jnp.where(is_a, q2, zero), jnp.where(is_a, zero, q2))
        row = lax.broadcasted_iota(jnp.int32, (blk, blk), 0)
        col = lax.broadcasted_iota(jnp.int32, (blk, blk), 1)
        tri = (row > col).astype(BF16)
        causal = col < row

        def tile_of(ref, j):
            return ref[pl.ds(pl.multiple_of(j * blk, blk), blk), :].astype(BF16)

        def scores(j):
            k2 = tile_of(k_ref, j)
            return [_nt(qs[h], k2) for h in range(2)]

        has_left = i > 0
        left = jnp.maximum(i - 1, 0)

        g_d = [_log_gates(z) for z in scores(i)]
        g_l = [_log_gates(z) for z in scores(left)]
        keep_d = [jnp.where(causal, g[1], 0.0) for g in g_d]
        suf_d = [_nn(lk.astype(BF16), tri) for lk in keep_d]
        suf_l = [_nn(g[1].astype(BF16), tri) for g in g_l]
        v_d, v_l = tile_of(v_ref, i), tile_of(v_ref, left)
        pv = []
        for h in range(2):
            sum_d = jnp.sum(keep_d[h], axis=-1, keepdims=True)
            w_d = jnp.where(causal, jnp.exp(g_d[h][0] + suf_d[h]), 0.0)
            w_l = jnp.exp(g_l[h][0] + (jnp.where(has_left, sum_d, NO_TILE) + suf_l[h]))
            pv.append(_nn(w_d.astype(BF16), v_d) + _nn(w_l.astype(BF16), v_l))
            tot_sc[h] = sum_d + jnp.sum(g_l[h][1], axis=-1, keepdims=True)
        acc_sc[...] = jnp.where(is_a, pv[0], pv[1])

        live_sc[...] = tot_sc[...]
        first = _first_live_tile(i - 2, scores, live_sc)
        trips = i - 1 - first
        z_bufs, w_bufs = (z0_sc, z1_sc), (w0_sc, w1_sc)

        def put(ref, vals):
            for h in range(2):
                ref[h] = vals[h]

        def weights(zs):
            gates = [_log_gates(z) for z in zs]
            sums = [_nn(g[1].astype(BF16), tri) for g in gates]
            ws = []
            for h in range(2):
                ws.append(jnp.exp(gates[h][0] + (tot_sc[h] + sums[h])).astype(BF16))
                tot_sc[h] = tot_sc[h] + jnp.sum(gates[h][1], axis=-1, keepdims=True)
            return ws

        def add_values(w_buf, j):
            v2 = tile_of(v_ref, j)
            acc_sc[...] += jnp.where(is_a, _nn(w_buf[0], v2), _nn(w_buf[1], v2))

        def trip(j, s):
            add_values(w_bufs[s], j + 1)
            put(z_bufs[1 - s], scores(jnp.maximum(j - 1, first)))
            put(w_bufs[1 - s], weights((z_bufs[s][0], z_bufs[s][1])))

        @pl.when(trips > 0)
        def _():
            put(z0_sc, scores(i - 2))
            w0_sc[...] = jnp.zeros_like(w0_sc)

            def two_trips(pp, carry):
                j = i - 2 - 2 * pp
                trip(j, 0)
                trip(j - 1, 1)
                return carry

            lax.fori_loop(0, trips // 2, two_trips, 0)
            odd = trips % 2 == 1

            @pl.when(odd)
            def _():
                trip(first, 0)
                add_values(w1_sc, first)

            @pl.when(jnp.logical_not(odd))
            def _():
                add_values(w0_sc, first)

        o_ref[...] = acc_sc[...].astype(BF16)

    return pl.pallas_call(
        body, name="attn_fwd", grid=(npair, nq),
        in_specs=[_bs((blk, LANES), lambda p, i: (i, p)),
                  _bs((seq, LANES), lambda p, i: (0, npair + p)),
                  _bs((seq, LANES), lambda p, i: (0, 2 * npair + p))],
        out_specs=_bs((blk, LANES), lambda p, i: (i, p)),
        out_shape=_sds((seq, ATTN_W), BF16),
        scratch_shapes=[pltpu.VMEM((2, blk, blk), F32), pltpu.VMEM((2, blk, blk), F32),
                        pltpu.VMEM((2, blk, blk), BF16), pltpu.VMEM((2, blk, blk), BF16),
                        pltpu.VMEM((2, blk, 1), F32), pltpu.VMEM((2, blk, 1), F32), pltpu.VMEM((blk, LANES), F32)],
        compiler_params=_cparams(2),
    )(proj, proj, proj)


def _attn_bwd(proj, do, seq):
    blk = ATT_BLK
    nq = seq // blk
    npair = N_HEADS // 2

    def body(q_ref, k_ref, v_ref, do_ref, dq_ref, dk_ref, dv_ref,
             prod0_sc, prod1_sc, pend0_sc, pend1_sc, tot_sc, live_sc, cum_sc, pre_sc, dq_sc):
        i = pl.program_id(1)

        @pl.when(i == 0)
        def _():
            dk_ref[...] = jnp.zeros_like(dk_ref)
            dv_ref[...] = jnp.zeros_like(dv_ref)

        is_a = lax.broadcasted_iota(jnp.int32, (1, LANES), 1) < HEAD_DIM
        q2 = (q_ref[...] * Q_SCALE).astype(BF16)
        do2 = do_ref[...]
        zero = jnp.zeros_like(q2)
        qs = (jnp.where(is_a, q2, zero), jnp.where(is_a, zero, q2))
        dos = (jnp.where(is_a, do2, zero), jnp.where(is_a, zero, do2))
        row = lax.broadcasted_iota(jnp.int32, (blk, blk), 0)
        col = lax.broadcasted_iota(jnp.int32, (blk, blk), 1)
        tri_after = (row > col).astype(BF16)
        tri_excl = (row < col).astype(BF16)
        causal = col < row

        def tile_of(ref, j):
            return ref[pl.ds(pl.multiple_of(j * blk, blk), blk), :].astype(BF16)

        def scores(j):
            k2 = tile_of(k_ref, j)
            return [_nt(qs[h], k2) for h in range(2)]

        def products(j):
            v2 = tile_of(v_ref, j)
            return scores(j) + [_nt(dos[h], v2) for h in range(2)]

        def row_sum(a):
            return jnp.sum(a, axis=-1, keepdims=True)

        def grad_matmuls(ws, dzs, j):
            rows = pl.ds(pl.multiple_of(j * blk, blk), blk)
            k2 = tile_of(k_ref, j)
            dq_sc[...] += jnp.where(is_a, _nn(dzs[0], k2), _nn(dzs[1], k2))
            dk_ref[rows, :] += jnp.where(is_a, _tn(dzs[0], q2), _tn(dzs[1], q2))
            if ws is not None:
                dv_ref[rows, :] += jnp.where(is_a, _tn(ws[0], do2), _tn(ws[1], do2))

        has_left = i > 0
        left = jnp.maximum(i - 1, 0)

        p_d, p_l = products(i), products(left)
        g_d = [_log_gates(z) for z in p_d[:2]]
        g_l = [_log_gates(z) for z in p_l[:2]]
        keep_d = [jnp.where(causal, g[1], 0.0) for g in g_d]
        suf_d = [_nn(lk.astype(BF16), tri_after) for lk in keep_d]
        suf_l = [_nn(g[1].astype(BF16), tri_after) for g in g_l]
        w_d, w_l, gg_d, gg_l = [], [], [], []
        for h in range(2):
            sum_d = row_sum(keep_d[h])
            w_d.append(jnp.where(causal, jnp.exp(g_d[h][0] + suf_d[h]), 0.0))
            w_l.append(jnp.exp(g_l[h][0] + (jnp.where(has_left, sum_d, NO_TILE) + suf_l[h])))
            gg_d.append(p_d[2 + h] * w_d[h])
            gg_l.append(p_l[2 + h] * w_l[h])
            tot_sc[h] = sum_d + row_sum(g_l[h][1])
        before_d = [_nn(g.astype(BF16), tri_excl) for g in gg_d]
        before_l = [_nn(g.astype(BF16), tri_excl) for g in gg_l]
        dz_d, dz_l = [], []
        for h in range(2):
            beta_d, beta_l = jnp.exp(g_d[h][0]), jnp.exp(g_l[h][0])
            dz_l.append((gg_l[h] * (1.0 - beta_l) - before_l[h] * beta_l).astype(BF16))
            dz = gg_d[h] * (1.0 - beta_d) - (row_sum(gg_l[h]) + before_d[h]) * beta_d
            dz_d.append(jnp.where(causal, dz, 0.0).astype(BF16))
        dq_sc[...] = jnp.zeros_like(dq_sc)
        grad_matmuls([w.astype(BF16) for w in w_l], dz_l, left)
        grad_matmuls([w.astype(BF16) for w in w_d], dz_d, i)

        live_sc[...] = tot_sc[...]
        first = _first_live_tile(i - 2, scores, live_sc)
        trips = i - 1 - first
        prod_bufs, pend_bufs = (prod0_sc, prod1_sc), (pend0_sc, pend1_sc)

        def local_grads(prods):
            zs, dws = prods[:2], prods[2:]
            gates = [_log_gates(z) for z in zs]
            sums = [_nn(g[1].astype(BF16), tri_after) for g in gates]
            ws, gs = [], []
            for h in range(2):
                cum = cum_sc[h] + row_sum(gates[h][1])
                cum_sc[h] = cum
                ws.append(jnp.exp(gates[h][0] + ((live_sc[h] - cum) + sums[h])))
                gs.append(dws[h] * ws[h])
            befores = [_nn(g.astype(BF16), tri_excl) for g in gs]
            dzs = []
            for h in range(2):
                beta = jnp.exp(gates[h][0])
                dzs.append((gs[h] * (1.0 - beta) - (pre_sc[h] + befores[h]) * beta).astype(BF16))
                pre_sc[h] = pre_sc[h] + row_sum(gs[h])
            return [w.astype(BF16) for w in ws] + dzs

        def put(ref, vals):
            for n, val in enumerate(vals):
                ref[n] = val

        def flush(pend, j):
            grad_matmuls([pend[0], pend[1]], [pend[2], pend[3]], j)

        def trip(j, s):
            flush(pend_bufs[s], jnp.maximum(j - 1, first))
            put(prod_bufs[1 - s], products(j + 1))
            put(pend_bufs[1 - s], local_grads([prod_bufs[s][n] for n in range(4)]))

        def earlier_keys_share(j, mask):
            dzs = []
            for h, z in enumerate(scores(j)):
                beta = jnp.exp(_log_gates(z)[0])
                dzs.append(jnp.where(mask, -pre_sc[h] * beta, 0.0).astype(BF16))
            grad_matmuls(None, dzs, j)

        @pl.when(trips > 0)
        def _():
            cum_sc[...] = jnp.zeros_like(cum_sc)
            pre_sc[...] = jnp.zeros_like(pre_sc)
            pend0_sc[...] = jnp.zeros_like(pend0_sc)
            put(prod0_sc, products(first))

            def two_trips(pp, carry):
                trip(first + 2 * pp, 0)
                trip(first + 2 * pp + 1, 1)
                return carry

            lax.fori_loop(0, trips // 2, two_trips, 0)
            odd = trips % 2 == 1

            @pl.when(odd)
            def _():
                trip(i - 2, 0)
                flush(pend1_sc, i - 2)

            @pl.when(jnp.logical_not(odd))
            def _():
                flush(pend0_sc, i - 2)

            earlier_keys_share(i - 1, True)
            earlier_keys_share(i, causal)

        dq_ref[...] = dq_sc[...] * Q_SCALE

    qmap = lambda p, i: (i, p)
    return pl.pallas_call(
        body, name="attn_bwd", grid=(npair, nq),
        in_specs=[_bs((blk, LANES), qmap),
                  _bs((seq, LANES), lambda p, i: (0, npair + p)),
                  _bs((seq, LANES), lambda p, i: (0, 2 * npair + p)),
                  _bs((blk, LANES), qmap)],
        out_specs=[_bs((blk, LANES), qmap),
                   _bs((seq, LANES), lambda p, i: (0, p)),
                   _bs((seq, LANES), lambda p, i: (0, p))],
        out_shape=[_sds((seq, ATTN_W), F32)] * 3,
        scratch_shapes=[pltpu.VMEM((4, blk, blk), F32), pltpu.VMEM((4, blk, blk), F32),
                        pltpu.VMEM((4, blk, blk), BF16), pltpu.VMEM((4, blk, blk), BF16),
                        pltpu.VMEM((2, blk, 1), F32), pltpu.VMEM((2, blk, 1), F32), pltpu.VMEM((2, blk, 1), F32),
                        pltpu.VMEM((2, blk, 1), F32), pltpu.VMEM((blk, LANES), F32)],
        compiler_params=_cparams(2),
    )(proj, proj, proj, do)


def _elementwise(name, fn, ins, out_dtypes):
    rows, cols = ins[0].shape
    tr = rows
    for cand in (512, 256, 128, 64, 32, 16, 8):
        if rows % cand == 0 and cand * cols * 4 <= 2 * 1024 * 1024:
            tr = cand
            break
    n_in = len(ins)

    def body(*refs):
        res = fn(*[r[...] for r in refs[:n_in]])
        for r, val in zip(refs[n_in:], res):
            r[...] = val.astype(r.dtype)

    spec = _bs((tr, cols), lambda i: (i, 0))
    return pl.pallas_call(
        body, name=name, grid=(rows // tr,),
        in_specs=[spec] * n_in, out_specs=[spec] * len(out_dtypes),
        out_shape=[_sds((rows, cols), dt) for dt in out_dtypes],
        compiler_params=_cparams(1),
    )(*ins)


def _adamw_fn(w, g, m, v):
    m = ADAM_B1 * m + (1.0 - ADAM_B1) * g
    v = ADAM_B2 * v + (1.0 - ADAM_B2) * (g * g)
    m_hat = m / (1.0 - ADAM_B1 ** ADAM_STEP)
    v_hat = v / (1.0 - ADAM_B2 ** ADAM_STEP)
    delta = -ADAM_LR * (m_hat / (jnp.sqrt(v_hat) + ADAM_EPS) + ADAM_WD * w)
    return delta, m, v


def _adamw(name, w, g, m, v):
    shape = w.shape
    as2d = lambda a: a.reshape(-1, shape[-1])
    delta, nm, nv = _elementwise(name, _adamw_fn, [as2d(w), as2d(g), as2d(m), as2d(v)], [F32, F32, F32])
    return delta.reshape(shape), nm.reshape(shape), nv.reshape(shape)


def _place():
    x, y, c = lax.axis_index("x"), lax.axis_index("y"), lax.axis_index("c")
    chips = [(1 - x, y), (x, 1 - y), (1 - x, 1 - y)]
    return x, y, c, chips


ANY = pl.BlockSpec(memory_space=pl.ANY)
VMEM_WHOLE = pl.BlockSpec(memory_space=pltpu.VMEM)


def _allgather_weights(shards):
    n = len(shards)

    def body(*refs):
        src, dst = refs[:n], refs[n:2 * n]
        send_sems, recv_sems, local_sems = refs[2 * n:]
        x, y, c, chips = _place()
        me, sibling, mychip = (x, y, c), (x, y, 1 - c), 2 * x + y

        x_nbr, y_nbr, diag = 2 * (1 - x) + y, 2 * x + (1 - y), 2 * (1 - x) + (1 - y)
        to_x, to_y = (1 - x, y, c), (x, 1 - y, c)

        def parts(w):
            hr = src[w].shape[0] // 2
            first = hr // 2 if hr % 32 == 0 else hr
            return first, hr - first

        def rows_of(w, chip, half, route):
            hr = src[w].shape[0] // 2
            first, second = parts(w)
            start, size = {0: (0, hr), 1: (0, hr), 2: (0, first), 3: (first, second)}[route]
            return dst[w].at[chip, pl.ds(half * hr + start, size)]

        def copy(w, k, src_ref, dst_ref, to):
            return pltpu.make_async_remote_copy(src_ref=src_ref, dst_ref=dst_ref, send_sem=send_sems.at[w, k],
                                                recv_sem=recv_sems.at[w, k], device_id=to, device_id_type=MESH)

        def landed(w, route):
            chip = {0: x_nbr, 1: y_nbr, 2: diag, 3: diag}[route]
            return rows_of(w, chip, c, route), chip

        def routes(w):
            return (0, 1, 2, 3) if parts(w)[1] else (0, 1, 2)

        started, local = [], []
        for w in range(n):
            hr = src[w].shape[0] // 2
            own = pltpu.make_async_copy(src[w], dst[w].at[mychip], local_sems.at[w])
            own.start()
            local.append(own)
            mine = src[w].at[pl.ds(c * hr, hr)]
            for route, to in ((0, to_x), (1, to_y)):
                cp = copy(w, route, mine, rows_of(w, mychip, c, route), to)
                cp.start()
                started.append(cp)

        def pass_on(w, route):
            got, chip = landed(w, route)
            copy(w, route, got, got, me).wait_recv()
            if route == 1:
                part = rows_of(w, chip, c, 2)
                started.append(copy(w, 2, part, part, to_x))
                started[-1].start()
            if route == 0 and parts(w)[1]:
                part = rows_of(w, chip, c, 3)
                started.append(copy(w, 3, part, part, to_y))
                started[-1].start()
            started.append(copy(w, 4 + route, got, got, sibling))
            started[-1].start()

        for w in range(n):
            pass_on(w, 1)
            pass_on(w, 0)
        for w in range(n):
            for route in routes(w)[2:]:
                pass_on(w, route)
        for w in range(n):
            for route in routes(w):
                chip = landed(w, route)[1]
                from_sib = rows_of(w, chip, 1 - c, route)
                copy(w, 4 + route, from_sib, from_sib, me).wait_recv()
        for cp in local:
            cp.wait()
        for cp in started:
            cp.wait_send()

    return pl.pallas_call(
        body, name="allgather_weights",
        in_specs=[VMEM_WHOLE] * n, out_specs=[VMEM_WHOLE] * n,
        out_shape=[_sds((N_CHIPS,) + s.shape, s.dtype) for s in shards],
        scratch_shapes=[pltpu.SemaphoreType.DMA((n, 8)), pltpu.SemaphoreType.DMA((n, 8)),
                        pltpu.SemaphoreType.DMA((n,))],
        compiler_params=pltpu.CompilerParams(vmem_limit_bytes=VMEM_LIMIT),
    )(*shards)


SUM_ROWS = 64


def _rs_pair_sum(name, grads):
    n = len(grads)

    def body(*refs):
        g, out = refs[:n], refs[n:2 * n]
        stage, land, keep = refs[2 * n:3 * n], refs[3 * n:4 * n], refs[4 * n:5 * n]
        send_sems, recv_sems, stage_sems, keep_sems = refs[5 * n:]
        x, y, c, _ = _place()
        sibling = (x, y, 1 - c)
        loads = []
        for w in range(n):
            hr = g[w].shape[1] // 2
            st = pltpu.make_async_copy(g[w].at[:, pl.ds((1 - c) * hr, hr)], stage[w], stage_sems.at[w])
            kp = pltpu.make_async_copy(g[w].at[:, pl.ds(c * hr, hr)], keep[w], keep_sems.at[w])
            st.start()
            kp.start()
            loads.append((st, kp))
        gives = []
        for w in range(n):
            loads[w][0].wait()
            give = pltpu.make_async_remote_copy(src_ref=stage[w], dst_ref=land[w], send_sem=send_sems.at[w],
                                                recv_sem=recv_sems.at[w], device_id=sibling, device_id_type=MESH)
            give.start()
            gives.append(give)
        for w in range(n):
            loads[w][1].wait()
            gives[w].wait_recv()
            nb = g[w].shape[1] // 2 // SUM_ROWS

            def add(idx, carry, w=w, nb=nb):
                k, r = idx // nb, pl.multiple_of((idx % nb) * SUM_ROWS, SUM_ROWS)
                rows = pl.ds(r, SUM_ROWS)
                out[w][k, rows, :] = (keep[w][k, rows, :] + land[w][k, rows, :]).astype(BF16)
                return carry

            lax.fori_loop(0, N_CHIPS * nb, add, 0)
        for give in gives:
            give.wait_send()

    half = [(N_CHIPS, a.shape[1] // 2, a.shape[2]) for a in grads]
    bufs = [pltpu.VMEM(s, F32) for s in half]
    sems = pltpu.SemaphoreType.DMA((n,))
    return pl.pallas_call(
        body, name=name,
        in_specs=[ANY] * n, out_specs=[VMEM_WHOLE] * n, out_shape=[_sds(s, BF16) for s in half],
        scratch_shapes=bufs + bufs + bufs + [sems, sems, sems, sems],
        compiler_params=pltpu.CompilerParams(vmem_limit_bytes=VMEM_LIMIT),
    )(*grads)


def _rs_exchange_join(parts):
    n = len(parts)

    def body(*refs):
        t, full = refs[:n], refs[n:2 * n]
        got_x, got_y, pass_on, got_2 = (refs[m * n:(m + 1) * n] for m in range(2, 6))
        send_sems, recv_sems = refs[6 * n:]
        x, y, c, _ = _place()
        mychip, sibling = 2 * x + y, (x, y, 1 - c)
        x_nbr, y_nbr, diag = 2 * (1 - x) + y, 2 * x + (1 - y), 2 * (1 - x) + (1 - y)
        to_x, to_y = (1 - x, y, c), (x, 1 - y, c)
        sends = []

        def copy(w, k, src_ref, dst_ref, to):
            return pltpu.make_async_remote_copy(src_ref=src_ref, dst_ref=dst_ref, send_sem=send_sems.at[w, k],
                                                recv_sem=recv_sems.at[w, k], device_id=to, device_id_type=MESH)

        def start(cp):
            cp.start()
            sends.append(cp)

        def add_rows(w, count, fn):
            def step(idx, carry):
                fn(pl.ds(pl.multiple_of(idx * SUM_ROWS, SUM_ROWS), SUM_ROWS), pl.multiple_of(idx * SUM_ROWS, SUM_ROWS))
                return carry
            lax.fori_loop(0, count // SUM_ROWS, step, 0)

        f32 = lambda v: v.astype(F32)
        for w in range(n):
            ha = t[w].shape[1] // 2
            part_a, part_b = pl.ds(0, ha), pl.ds(ha, ha)
            start(copy(w, 0, t[w].at[x_nbr, part_a], got_x[w].at[0], to_x))
            start(copy(w, 1, t[w].at[diag, part_a], got_x[w].at[1], to_x))
            start(copy(w, 2, t[w].at[y_nbr, part_b], got_y[w].at[0], to_y))
            start(copy(w, 3, t[w].at[diag, part_b], got_y[w].at[1], to_y))
        for w in range(n):
            hr = t[w].shape[1]
            ha = hr // 2
            for k in (0, 1):
                copy(w, k, got_x[w].at[k], got_x[w].at[k], to_x).wait_recv()

            def sum_a(rows, r, w=w, hr=hr):
                full[w][pl.ds(pl.multiple_of(c * hr + r, SUM_ROWS), SUM_ROWS), :] = \
                    f32(t[w][mychip, rows, :]) + f32(got_x[w][0, rows, :])
                pass_on[w][rows, :] = (f32(t[w][y_nbr, rows, :]) + f32(got_x[w][1, rows, :])).astype(BF16)

            add_rows(w, ha, sum_a)
            start(copy(w, 4, pass_on[w].at[pl.ds(0, ha)], got_2[w].at[pl.ds(0, ha)], to_y))
            for k in (2, 3):
                copy(w, k, got_y[w].at[k - 2], got_y[w].at[k - 2], to_y).wait_recv()

            def sum_b(rows, r, w=w, hr=hr, ha=ha):
                lower = pl.ds(pl.multiple_of(ha + r, SUM_ROWS), SUM_ROWS)
                full[w][pl.ds(pl.multiple_of(c * hr + ha + r, SUM_ROWS), SUM_ROWS), :] = \
                    f32(t[w][mychip, lower, :]) + f32(got_y[w][0, rows, :])
                pass_on[w][lower, :] = (f32(t[w][x_nbr, lower, :]) + f32(got_y[w][1, rows, :])).astype(BF16)

            add_rows(w, ha, sum_b)
            start(copy(w, 5, pass_on[w].at[pl.ds(ha, ha)], got_2[w].at[pl.ds(ha, ha)], to_x))
        for w in range(n):
            hr = t[w].shape[1]
            ha = hr // 2
            copy(w, 4, got_2[w].at[pl.ds(0, ha)], got_2[w].at[pl.ds(0, ha)], to_y).wait_recv()
            copy(w, 5, got_2[w].at[pl.ds(ha, ha)], got_2[w].at[pl.ds(ha, ha)], to_x).wait_recv()

            def finish(rows, r, w=w, hr=hr):
                out_rows = pl.ds(pl.multiple_of(c * hr + r, SUM_ROWS), SUM_ROWS)
                full[w][out_rows, :] = full[w][out_rows, :] + f32(got_2[w][rows, :])

            add_rows(w, hr, finish)
            mine = full[w].at[pl.ds(c * hr, hr)]
            start(copy(w, 6, mine, mine, sibling))
        for w in range(n):
            hr = t[w].shape[1]
            theirs = full[w].at[pl.ds((1 - c) * hr, hr)]
            copy(w, 6, theirs, theirs, sibling).wait_recv()
        for cp in sends:
            cp.wait_send()

    half = lambda a: pltpu.VMEM((2, a.shape[1] // 2, a.shape[2]), a.dtype)
    whole = lambda a: pltpu.VMEM(a.shape[1:], a.dtype)
    return pl.pallas_call(
        body, name="rs_exchange_join",
        in_specs=[VMEM_WHOLE] * n, out_specs=[VMEM_WHOLE] * n,
        out_shape=[_sds((2 * a.shape[1], a.shape[2]), F32) for a in parts],
        scratch_shapes=[half(a) for a in parts] + [half(a) for a in parts] + [whole(a) for a in parts]
        + [whole(a) for a in parts] + [pltpu.SemaphoreType.DMA((n, 7)), pltpu.SemaphoreType.DMA((n, 7))],
        compiler_params=pltpu.CompilerParams(vmem_limit_bytes=VMEM_LIMIT),
    )(*parts)


def _small_allreduce(loss_p, dg_parts, dbg_a, dbg_c, dwc):
    ins = [loss_p] + list(dg_parts) + [dbg_a, dbg_c, dwc]
    n_in = len(ins)
    vmem = pl.BlockSpec(memory_space=pltpu.VMEM)

    def body(*refs):
        in_refs = refs[:n_in]
        out_ref, vec, buf, send_sems, recv_sems = refs[n_in:]
        x, y, c, _ = _place()
        me = 4 * x + 2 * y + c
        vec[...] = jnp.zeros_like(vec)
        vec[0:1, :] = jnp.sum(in_refs[0][...], axis=0)
        for r in range(5):
            vec[1 + r:2 + r, :] = jnp.sum(in_refs[1 + r][...], axis=0)
        vec[6:7, :] = jnp.sum(in_refs[6][...], axis=0)
        vec[7:8, :] = jnp.sum(in_refs[7][...], axis=0)
        vec[8:16, 0:CONV_W] = jnp.sum(in_refs[8][...], axis=0)
        buf[pl.ds(me, 1)] = vec[...][None]
        copies = []
        for r in range(1, 8):
            fx, fy, fc = (r >> 2) & 1, (r >> 1) & 1, r & 1
            to = (1 - x if fx else x, 1 - y if fy else y, 1 - c if fc else c)
            cp = pltpu.make_async_remote_copy(src_ref=vec, dst_ref=buf.at[me], send_sem=send_sems.at[r - 1],
                                              recv_sem=recv_sems.at[r - 1], device_id=to, device_id_type=MESH)
            cp.start()
            copies.append(cp)
        for cp in copies:
            cp.wait()
        total = buf[0]
        for s in range(1, 8):
            total = total + buf[s]
        out_ref[...] = total
        out_ref[0:1, :] = jnp.broadcast_to(jnp.sum(total[0:1, :], axis=-1, keepdims=True), (1, D_MODEL))

    return pl.pallas_call(
        body, name="small_allreduce",
        in_specs=[vmem] * n_in, out_specs=vmem, out_shape=_sds((SMALL_ROWS, D_MODEL), F32),
        scratch_shapes=[pltpu.VMEM((SMALL_ROWS, D_MODEL), F32), pltpu.VMEM((8, SMALL_ROWS, D_MODEL), F32),
                        pltpu.SemaphoreType.DMA((7,)), pltpu.SemaphoreType.DMA((7,))],
    )(*ins)


def _local_step(x, p, tgt, g, b_gate, w_conv, wf):
    seq = x.shape[0]
    tm = min(seq, 1024)
    th = min(seq, 512)
    tl = min(seq, 2048)
    ni, nh, nl = seq // tm, seq // th, seq // tl
    g_pre_mix, g_post_mix, g_pre_mlp, g_post_mlp, g_ple = g
    w_in, w_ao, w_co, w_o, w_up, w_down, w_pg, w_pp, w_in_nat, w_up_nat = wf
    D = D_MODEL
    vec = lambda a, blk=0: (a, _bs((1, D), lambda i, j, k: (0, blk)))
    rows_i = lambda a, t, blk=0: (a, _bs((t, D), lambda i, j, k: (i, blk)))
    rows_k = lambda a, t, blk=0: (a, _bs((t, D), lambda i, j, k: (k, blk)))
    part = lambda n: (_sds((n, 1, D), F32), _bs((None, 1, D), lambda i, j, k: (i, 0, 0)))
    full2 = lambda a: (a, _bs(a.shape, lambda i, j, k: (0, 0)))

    normed = lambda xb, gb: (_rms(xb, gb).astype(BF16),) * 2
    keep_a = lambda t: [(_sds((seq, D), BF16), _bs((t, D), lambda i, j, k: (i, 0)))]
    main_w = D_IN - 2 * D
    proj, gates, h1 = _mm("proj_in", "nn", (nh, 1, 1),
                          a_ins=[rows_i(x, th), vec(g_pre_mix)], a_fn=normed,
                          b_ins=[full2(w_in_nat)], b_fn=_ident,
                          epi_fn=lambda acc: (acc[:, :main_w], acc[:, main_w:]),
                          outs=[(_sds((seq, main_w), F32), _bs((th, main_w), lambda i, j, k: (i, 0))),
                                (_sds((seq, 2 * D), BF16), _bs((th, 2 * D), lambda i, j, k: (i, 0)))],
                          acc_shape=(th, D_IN), a_cache=((th, D), BF16), a_outs=keep_a(th))
    o = _attn_fwd(proj, seq)
    e = _conv_fwd(proj, w_conv, seq, tm)

    def gate_values(ga, gc, ba, bc):
        return _sig(ga.astype(F32) + ba), _sig(gc.astype(F32) + bc)

    def branch_outputs(ob, eb, wao, wco):
        return _nn(ob, wao).astype(BF16).astype(F32), _nn(eb, wco).astype(BF16).astype(F32)

    def mix_fn(ga, gc, ob, eb, ba, bc, wao, wco):
        sa, sc = gate_values(ga, gc, ba, bc)
        ya, yc = branch_outputs(ob, eb, wao, wco)
        return ((sa * ya + sc * yc).astype(BF16),) * 2

    def post_mix(acc, xb, gb):
        return acc, xb + _rms(acc, gb)

    half_rows = lambda a: (a, _bs((th, a.shape[1]), lambda i, j, k: (i, 0)))
    mix_ins = [rows_i(gates, th, 0), rows_i(gates, th, 1), half_rows(o), half_rows(e), vec(b_gate, 0), vec(b_gate, 1),
               full2(w_ao), full2(w_co)]
    mixed, x1, mixin = _mm(
        "mix_out", "nn", (nh, 1, 1),
        a_ins=mix_ins, a_fn=mix_fn, b_ins=[full2(w_o)], b_fn=_ident,
        epi_ins=[rows_i(x, th), vec(g_post_mix)], epi_fn=post_mix,
        outs=[(_sds((seq, D), BF16), _bs((th, D), lambda i, j, k: (i, 0))),
              (_sds((seq, D), F32), _bs((th, D), lambda i, j, k: (i, 0)))],
        acc_shape=(th, D), a_cache=((th, D), BF16), a_outs=keep_a(th))
    up, h2 = _mm("mlp_up", "nn", (nh, 1, 1),
                 a_ins=[rows_i(x1, th), vec(g_pre_mlp)], a_fn=normed,
                 b_ins=[full2(w_up_nat)], b_fn=_ident,
                 outs=[(_sds((seq, D_FF), BF16), _bs((th, D_FF), lambda i, j, k: (i, 0)))],
                 acc_shape=(th, D_FF), a_cache=((th, D), BF16), a_outs=keep_a(th))

    def relu2(ub):
        r = jnp.maximum(ub.astype(F32), 0.0)
        return (r * r).astype(BF16)

    dx2, df, dpre, h3, dpp, loss_p, dg_ple_p, dg_post_mlp_p = _mlp_down_ple_head(
        up, x1, p, tgt, g_ple, g_post_mlp, w_down, w_pg, w_pp, seq, th)

    (dw_pp,) = _mm("dw_ple_proj", "tn", (1, 1, nh),
                   a_ins=[(p, _bs((th, PLE_DIM), lambda i, j, k: (k, 0)))], a_fn=_to_bf16,
                   b_ins=[rows_k(dpp, th)], b_fn=_ident,
                   outs=[(_sds((PLE_DIM, D), F32), _bs((PLE_DIM, D), lambda i, j, k: (0, 0)))],
                   acc_shape=(PLE_DIM, D))
    (dw_pg,) = _mm("dw_ple_gate", "tn", (1, 1, nl),
                   a_ins=[rows_k(h3, tl)], a_fn=_ident, b_ins=[rows_k(dpre, tl)], b_fn=_ident,
                   outs=[(_sds((D, D), F32), _bs((D, D), lambda i, j, k: (0, 0)))], acc_shape=(D, D))

    def dup_fn(acc, ub):
        return (acc * (2.0 * jnp.maximum(ub.astype(F32), 0.0)),)

    (dup,) = _mm("d_mlp_down", "nt", (nh, 1, 1),
                 a_ins=[rows_i(df, th)], a_fn=_ident, b_ins=[full2(w_down)], b_fn=_ident,
                 epi_ins=[(up, _bs((th, D_FF), lambda i, j, k: (i, 0)))], epi_fn=dup_fn,
                 outs=[(_sds((seq, D_FF), BF16), _bs((th, D_FF), lambda i, j, k: (i, 0)))],
                 acc_shape=(th, D_FF))
    (dw_down,) = _mm("dw_mlp_down", "tn", (4, 1, nl),
                     a_ins=[(up, _bs((tl, D), lambda i, j, k: (k, i)))], a_fn=relu2,
                     b_ins=[rows_k(df, tl)], b_fn=_ident,
                     outs=[(_sds((D_FF, D), F32), _bs((D, D), lambda i, j, k: (i, 0)))], acc_shape=(D, D))
    (dw_up,) = _mm("dw_mlp_up", "tn", (1, 4, nl),
                   a_ins=[rows_k(h2, tl)], a_fn=_ident,
                   b_ins=[(dup, _bs((tl, D), lambda i, j, k: (k, j)))], b_fn=_ident,
                   outs=[(_sds((N_CHIPS, D, D), F32), _bs((None, D, D), lambda i, j, k: (j, 0, 0)))],
                   acc_shape=(D, D))

    def mlp_norm_bwd(acc, x1b, dx2b, mixedb, g_mlp, g_mix):
        dxn, dg_mlp = _rms_bwd(x1b, g_mlp, acc)
        dx1b = dx2b + dxn
        dmixedb, dg_mix = _rms_bwd(mixedb.astype(F32), g_mix, dx1b)
        return dx1b, dmixedb, dg_mlp, dg_mix

    dx1, dmixed, dg_pre_mlp_p, dg_post_mix_p = _mm(
        "d_mlp_up", "nt", (nh, 1, 1),
        a_ins=[(dup, _bs((th, D_FF), lambda i, j, k: (i, 0)))], a_fn=_ident,
        b_ins=[full2(w_up_nat)], b_fn=_ident,
        epi_ins=[rows_i(x1, th), rows_i(dx2, th), rows_i(mixed, th), vec(g_pre_mlp), vec(g_post_mix)],
        epi_fn=mlp_norm_bwd,
        outs=[(_sds((seq, D), F32), _bs((th, D), lambda i, j, k: (i, 0))),
              (_sds((seq, D), BF16), _bs((th, D), lambda i, j, k: (i, 0))), part(nh), part(nh)],
        acc_shape=(th, D))
    (dw_o,) = _mm("dw_mix_out", "tn", (1, 1, nl),
                  a_ins=[rows_k(mixin, tl)], a_fn=_ident, b_ins=[rows_k(dmixed, tl)], b_fn=_ident,
                  outs=[(_sds((D, D), F32), _bs((D, D), lambda i, j, k: (0, 0)))], acc_shape=(D, D))

    def gate_bwd(acc, ga, gc, ob, eb, ba, bc, wao, wco):
        sa, sc = gate_values(ga, gc, ba, bc)
        ya, yc = branch_outputs(ob, eb, wao, wco)
        dga = acc * ya * sa * (1.0 - sa)
        dgc = acc * yc * sc * (1.0 - sc)
        dya, dyc = (acc * sa).astype(BF16), (acc * sc).astype(BF16)
        return (dya, dyc, jnp.concatenate([dga, dgc], axis=1), _nt(dya, wao), _nt(dyc, wco),
                jnp.sum(dga, axis=0, keepdims=True), jnp.sum(dgc, axis=0, keepdims=True))

    dya, dyc, dgate, do, de, dbg_a_p, dbg_c_p = _mm(
        "d_mix_out", "nt", (nh, 1, 1),
        a_ins=[rows_i(dmixed, th)], a_fn=_ident, b_ins=[full2(w_o)], b_fn=_ident,
        epi_ins=mix_ins, epi_fn=gate_bwd,
        outs=[(_sds((seq, D), BF16), _bs((th, D), lambda i, j, k: (i, 0)))] * 2
             + [(_sds((seq, 2 * D), BF16), _bs((th, 2 * D), lambda i, j, k: (i, 0))),
                (_sds((seq, ATTN_W), BF16), _bs((th, ATTN_W), lambda i, j, k: (i, 0))),
                (_sds((seq, CONV_W), F32), _bs((th, CONV_W), lambda i, j, k: (i, 0))), part(nh), part(nh)],
        acc_shape=(th, D))
    (dw_ao,) = _mm("dw_attn_out", "tn", (1, 1, nh),
                   a_ins=[(o, _bs((th, ATTN_W), lambda i, j, k: (k, 0)))], a_fn=_ident,
                   b_ins=[rows_k(dya, th)], b_fn=_ident,
                   outs=[(_sds((ATTN_W, D), F32), _bs((ATTN_W, D), lambda i, j, k: (0, 0)))], acc_shape=(ATTN_W, D))
    dq, dk, dv = _attn_bwd(proj, do, seq)
    (dw_co,) = _mm("dw_conv_out", "tn", (1, 1, nh),
                   a_ins=[(e, _bs((th, CONV_W), lambda i, j, k: (k, 0)))], a_fn=_ident,
                   b_ins=[rows_k(dyc, th)], b_fn=_ident,
                   outs=[(_sds((CONV_W, D), F32), _bs((CONV_W, D), lambda i, j, k: (0, 0)))], acc_shape=(CONV_W, D))
    dconv, dwc_p = _conv_bwd(proj, de, w_conv, seq, tm)
    qkv_w = 3 * ATTN_W
    join_bf16 = lambda *blocks: jnp.concatenate([b.astype(BF16) for b in blocks], axis=1)
    piece = lambda a, t, rows, blk=0: (a, _bs((t, a.shape[1]), (lambda i, j, k: (k, blk)) if rows == "k"
                                             else (lambda i, j, k: (i, blk))))
    (dw_in_qkv,) = _mm("dw_proj_in_qkv", "tn", (1, 1, ni),
                       a_ins=[rows_k(h1, tm)], a_fn=_ident,
                       b_ins=[piece(dq, tm, "k"), piece(dk, tm, "k"), piece(dv, tm, "k")], b_fn=join_bf16,
                       outs=[(_sds((D, qkv_w), F32), _bs((D, qkv_w), lambda i, j, k: (0, 0)))], acc_shape=(D, qkv_w))
    (dw_in_conv,) = _mm("dw_proj_in_conv", "tn", (1, 1, nl),
                        a_ins=[rows_k(h1, tl)], a_fn=_ident, b_ins=[piece(dconv, tl, "k")], b_fn=_ident,
                        outs=[(_sds((D, 3 * CONV_W), F32), _bs((D, 3 * CONV_W), lambda i, j, k: (0, 0)))],
                        acc_shape=(D, 3 * CONV_W))
    (dw_in_gate,) = _mm("dw_proj_in_gate", "tn", (1, 2, nl),
                        a_ins=[rows_k(h1, tl)], a_fn=_ident,
                        b_ins=[(dgate, _bs((tl, D), lambda i, j, k: (k, j)))], b_fn=_ident,
                        outs=[(_sds((D, 2 * D), F32), _bs((D, D), lambda i, j, k: (0, j)))], acc_shape=(D, D))
    dw_in = jnp.concatenate([dw_in_qkv, dw_in_conv, dw_in_gate], axis=1)

    def in_norm_bwd(acc, xb, dx1b, gb):
        dxn, dg = _rms_bwd(xb, gb, acc)
        return dx1b + dxn, dg

    grad_x, dg_pre_mix_p = _mm("d_proj_in", "nt", (nh, 1, 1),
                               a_ins=[piece(dq, th, "i"), piece(dk, th, "i"), piece(dv, th, "i"),
                                      piece(dconv, th, "i"), piece(dgate, th, "i")], a_fn=join_bf16,
                               b_ins=[full2(w_in_nat)], b_fn=_ident,
                               epi_ins=[rows_i(x, th), rows_i(dx1, th), vec(g_pre_mix)], epi_fn=in_norm_bwd,
                               outs=[(_sds((seq, D), F32), _bs((th, D), lambda i, j, k: (i, 0))), part(nh)],
                               acc_shape=(th, D))

    chip_major = lambda a: a.reshape(a.shape[0], N_CHIPS, a.shape[1] // N_CHIPS).transpose(1, 0, 2)
    big = [chip_major(dw_in), chip_major(dw_ao), chip_major(dw_co), dw_o.reshape(N_CHIPS, D // N_CHIPS, D), dw_up,
           dw_down.reshape(N_CHIPS, D_FF // N_CHIPS, D), dw_pg.reshape(N_CHIPS, D // N_CHIPS, D), chip_major(dw_pp)]
    small = (loss_p, [dg_pre_mix_p, dg_post_mix_p, dg_pre_mlp_p, dg_post_mlp_p, dg_ple_p], dbg_a_p, dbg_c_p, dwc_p)
    return grad_x, big, small


RS_GROUPS = ((0,), (4,), (5,), (1, 2, 3, 6, 7))


def _reduce_scatter(big):
    pair = [None] * len(big)
    for gi, group in enumerate(RS_GROUPS):
        for w, s in zip(group, _rs_pair_sum(f"rs_pair_sum_{gi}", [big[w] for w in group])):
            pair[w] = s
    return _rs_exchange_join(pair)


def kernel(x, p, g_pre_mix, w_in, b_gate, w_conv, w_attn_out, w_conv_out, w_o, g_post_mix, g_pre_mlp, w_up, w_down, g_post_mlp, g_ple, w_ple_gate, w_ple_proj, loss_target, m_g_pre_mix, m_w_in, m_b_gate, m_w_conv, m_w_attn_out, m_w_conv_out, m_w_o, m_g_post_mix, m_g_pre_mlp, m_w_up, m_w_down, m_g_post_mlp, m_g_ple, m_w_ple_gate, m_w_ple_proj, v_g_pre_mix, v_w_in, v_b_gate, v_w_conv, v_w_attn_out, v_w_conv_out, v_w_o, v_g_post_mix, v_g_pre_mlp, v_w_up, v_w_down, v_g_post_mlp, v_g_ple, v_w_ple_gate, v_w_ple_proj):
    mats = [w_in, w_attn_out, w_conv_out, w_o, w_up, w_down, w_ple_gate, w_ple_proj]
    mats_m = [m_w_in, m_w_attn_out, m_w_conv_out, m_w_o, m_w_up, m_w_down, m_w_ple_gate, m_w_ple_proj]
    mats_v = [v_w_in, v_w_attn_out, v_w_conv_out, v_w_o, v_w_up, v_w_down, v_w_ple_gate, v_w_ple_proj]
    gains = [g_pre_mix, g_post_mix, g_pre_mlp, g_post_mlp, g_ple]
    gains_m = [m_g_pre_mix, m_g_post_mix, m_g_pre_mlp, m_g_post_mlp, m_g_ple]
    gains_v = [v_g_pre_mix, v_g_post_mix, v_g_pre_mlp, v_g_post_mlp, v_g_ple]

    taps = jnp.concatenate([w_conv[0], jnp.zeros((CONV_PAD_ROWS - 3, LANES), F32)], axis=0)
    gathered = _allgather_weights([w[0].astype(BF16) for w in mats] + [taps])
    cols_joined = lambda a: a.transpose(1, 0, 2).reshape(a.shape[1], N_CHIPS * a.shape[2])
    rows_joined = lambda a: a.reshape(N_CHIPS * a.shape[1], a.shape[2])
    wf = [gathered[0], cols_joined(gathered[1]), cols_joined(gathered[2]), rows_joined(gathered[3]), gathered[4],
          rows_joined(gathered[5]), rows_joined(gathered[6]), cols_joined(gathered[7]),
          cols_joined(gathered[0]), cols_joined(gathered[4])]
    w_conv_full = cols_joined(gathered[8])[0:3, :]
    chip = 2 * lax.axis_index("x") + lax.axis_index("y")

    grad_x, big, small = _local_step(x[0], p[0, 0], loss_target[0], gains, b_gate, w_conv_full, wf)

    shard_grads = _reduce_scatter(big)
    red = _small_allreduce(*small)
    loss = red[0, 0]
    grad_gains = [red[1 + r:2 + r, :] for r in range(5)]
    grad_b_gate = jnp.concatenate([red[6:7, :], red[7:8, :]], axis=1)
    grad_w_conv = lax.dynamic_slice(red[8:11, :], (0, chip * LANES), (3, LANES))[None]

    grads_big = [gr.reshape(w.shape) for gr, w in zip(shard_grads, mats)]
    upd_big = [_adamw(f"adamw_{i}", w, gr, m, v) for i, (w, gr, m, v) in enumerate(zip(mats, grads_big, mats_m, mats_v))]
    pack = lambda vs, bg: jnp.concatenate(list(vs) + [bg.reshape(2, D_MODEL), jnp.zeros((1, D_MODEL), F32)], axis=0)
    upd_small = _adamw("adamw_small", pack(gains, b_gate), pack(grad_gains, grad_b_gate),
                       pack(gains_m, m_b_gate), pack(gains_v, v_b_gate))
    upd_conv = _adamw("adamw_conv", w_conv, grad_w_conv, m_w_conv, v_w_conv)

    def small_out(a, which):
        gains_out = [a[r:r + 1, :] for r in range(5)]
        return gains_out, a[5:7, :].reshape(1, 2 * D_MODEL)

    def ordered(g_pre_mix_, big_, b_gate_, conv_, g_rest):
        return [g_pre_mix_, big_[0], b_gate_, conv_, big_[1], big_[2], big_[3], g_rest[0], g_rest[1], big_[4], big_[5],
                g_rest[2], g_rest[3], big_[6], big_[7]]

    outs = [loss, grad_x[None]]
    outs += ordered(grad_gains[0], grads_big, grad_b_gate, grad_w_conv, grad_gains[1:])
    for which in range(3):
        g_out, b_out = small_out(upd_small[which], which)
        outs += ordered(g_out[0], [u[which] for u in upd_big], b_out, upd_conv[which], g_out[1:])
    return tuple(outs)
```

```python
import jax
import jax.numpy as jnp
from jax import lax
from jax.experimental import pallas as pl
from jax.experimental.pallas import tpu as pltpu

F32 = jnp.float32
BF16 = jnp.bfloat16
MESH = pl.DeviceIdType.MESH

D_MODEL = 1024
N_HEADS = 8
HEAD_DIM = 64
ATTN_W = N_HEADS * HEAD_DIM
CONV_W = 512
D_FF = 4096
PLE_DIM = 256
D_IN = 5120
N_CHIPS = 4
EPS = 1e-6
Q_SCALE = HEAD_DIM ** -0.5

ADAM_LR = 0.001
ADAM_B1 = 0.9
ADAM_B2 = 0.999
ADAM_EPS = 1e-08
ADAM_WD = 0.01
ADAM_STEP = 10

V7X_VMEM_BYTES = 64 * 1024 * 1024
VMEM_LIMIT = V7X_VMEM_BYTES - 8 * 1024 * 1024
LANES = 128
ATT_BLK = 256
SMALL_ROWS = 16
CONV_PAD_ROWS = 16


def _cparams(n_grid):
    return pltpu.CompilerParams(dimension_semantics=("arbitrary",) * n_grid, vmem_limit_bytes=VMEM_LIMIT)


def _bs(shape, fn):
    return pl.BlockSpec(shape, fn)


def _rms_stats(xf):
    return lax.rsqrt(jnp.mean(xf * xf, axis=-1, keepdims=True) + EPS)


def _rms(xf, g):
    return xf * _rms_stats(xf) * g


def _rms_bwd(xf, g, dy):
    r = _rms_stats(xf)
    xh = xf * r
    dyg = dy * g
    dx = r * (dyg - xh * jnp.mean(dyg * xh, axis=-1, keepdims=True))
    return dx, jnp.sum(dy * xh, axis=0, keepdims=True)


def _sig(z):
    return 1.0 / (1.0 + jnp.exp(-z))


def _ident(a):
    return a


def _to_bf16(a):
    return a.astype(BF16)


_DIMS = {"nn": (((1,), (0,)), ((), ())), "nt": (((1,), (1,)), ((), ())), "tn": (((0,), (0,)), ((), ()))}


def _mm(name, mode, grid, a_ins, a_fn, b_ins, b_fn, outs, acc_shape, epi_ins=(), epi_fn=None,
        a_cache=None, a_outs=(), epi_a=()):
    nk = grid[2]
    na, nb, ne, no, nao = len(a_ins), len(b_ins), len(epi_ins), len(outs), len(a_outs)
    assert a_cache is None or nk == 1
    assert not a_outs or a_cache is not None
    dims = _DIMS[mode]
    if epi_fn is None:
        epi_fn = lambda acc: (acc,)

    def body(*refs):
        a_refs = refs[:na]
        b_refs = refs[na:na + nb]
        e_refs = refs[na + nb:na + nb + ne]
        o_refs = refs[na + nb + ne:na + nb + ne + no]
        ao_refs = refs[na + nb + ne + no:na + nb + ne + no + nao]
        scratch = list(refs[na + nb + ne + no + nao:])
        acc_ref = scratch.pop(0) if nk > 1 else None
        a_sc = scratch.pop(0) if a_cache is not None else None
        j = pl.program_id(1)
        k = pl.program_id(2)

        def finish(acc):
            res = epi_fn(acc, *[a_refs[t][...] for t in epi_a], *[r[...] for r in e_refs])
            for r, val in zip(o_refs, res):
                r[...] = val.astype(r.dtype)

        if a_sc is not None:
            @pl.when(j == 0)
            def _():
                res = a_fn(*[r[...] for r in a_refs])
                if nao:
                    for r, val in zip(ao_refs, res[1:]):
                        r[...] = val.astype(r.dtype)
                    res = res[0]
                a_sc[...] = res
            a = a_sc[...]
        else:
            a = a_fn(*[r[...] for r in a_refs])
        b = b_fn(*[r[...] for r in b_refs])
        prod = lax.dot_general(a, b, dims, preferred_element_type=F32)
        if nk == 1:
            finish(prod)
        else:
            @pl.when(k == 0)
            def _():
                acc_ref[...] = prod

            @pl.when(k > 0)
            def _():
                acc_ref[...] += prod

            @pl.when(k == nk - 1)
            def _():
                finish(acc_ref[...])

    scratch_shapes = []
    if nk > 1:
        scratch_shapes.append(pltpu.VMEM(acc_shape, F32))
    if a_cache is not None:
        scratch_shapes.append(pltpu.VMEM(*a_cache))
    all_outs = list(outs) + list(a_outs)
    res = pl.pallas_call(
        body, name=name, grid=grid,
        in_specs=[s for _, s in a_ins] + [s for _, s in b_ins] + [s for _, s in epi_ins],
        out_specs=[s for _, s in all_outs],
        out_shape=[o for o, _ in all_outs],
        scratch_shapes=scratch_shapes,
        compiler_params=_cparams(3),
    )(*[a for a, _ in a_ins], *[a for a, _ in b_ins], *[a for a, _ in epi_ins])
    return res


def _sds(shape, dtype):
    return jax.ShapeDtypeStruct(shape, dtype)


def _nt(a, b):
    return lax.dot_general(a, b, _DIMS["nt"], preferred_element_type=F32)


def _tn(a, b):
    return lax.dot_general(a, b, _DIMS["tn"], preferred_element_type=F32)


def _nn(a, b):
    return lax.dot_general(a, b, _DIMS["nn"], preferred_element_type=F32)


def _mlp_down_ple_head(up, x1, p, tgt, g_ple, g_post_mlp, w_down, w_pg, w_pp, seq, tr):
    nblk = seq // tr
    D = D_MODEL

    def body(up_ref, x1_ref, p_ref, t_ref, gp_ref, gm_ref, wd_ref, wpg_ref, wpp_ref,
             dx2_ref, df_ref, dpre_ref, h3_ref, dpp_ref, loss_ref, dgp_ref, dgm_ref):
        gp, gm, wpg, wpp = gp_ref[...], gm_ref[...], wpg_ref[...], wpp_ref[...]
        halves = [pl.ds(0, tr // 2), pl.ds(tr // 2, tr // 2)]
        w_down = wd_ref[...]
        fb = []
        for r in halves:
            hidden = jnp.maximum(up_ref[r, :].astype(F32), 0.0)
            fb.append(_nn((hidden * hidden).astype(BF16), w_down))
        x2b = [x1_ref[r, :] + _rms(fb[s], gm) for s, r in enumerate(halves)]
        h3 = [_rms(x, gp).astype(BF16) for x in x2b]
        gate = [_sig(_nn(h, wpg)) for h in h3]
        pp = [_nn(p_ref[r, :].astype(BF16), wpp) for r in halves]
        err = [x2b[s] + gate[s] * pp[s] - t_ref[r, :] for s, r in enumerate(halves)]
        dx3 = [e * (1.0 / D) for e in err]
        dpre = [(dx3[s] * pp[s] * gate[s] * (1.0 - gate[s])).astype(BF16) for s in range(2)]
        dh3 = [_nt(d, wpg) for d in dpre]
        loss, dgp_sum, dgm_sum = 0.0, 0.0, 0.0
        for s, r in enumerate(halves):
            h3_ref[r, :] = h3[s]
            dpp_ref[r, :] = (dx3[s] * gate[s]).astype(BF16)
            dpre_ref[r, :] = dpre[s]
            dxn, dgp = _rms_bwd(x2b[s], gp, dh3[s])
            dx2 = dx3[s] + dxn
            dx2_ref[r, :] = dx2
            dfb, dgm = _rms_bwd(fb[s], gm, dx2)
            df_ref[r, :] = dfb.astype(BF16)
            loss = loss + jnp.sum(err[s] * err[s], axis=0, keepdims=True)
            dgp_sum, dgm_sum = dgp_sum + dgp, dgm_sum + dgm
        loss_ref[...] = loss * (0.5 / D)
        dgp_ref[...] = dgp_sum
        dgm_ref[...] = dgm_sum

    rows = _bs((tr, D), lambda i: (i, 0))
    vec = _bs((1, D), lambda i: (0, 0))
    part = _bs((None, 1, D), lambda i: (i, 0, 0))
    return pl.pallas_call(
        body, name="mlp_down_ple_head", grid=(nblk,),
        in_specs=[_bs((tr, D_FF), lambda i: (i, 0)), rows, _bs((tr, PLE_DIM), lambda i: (i, 0)), rows, vec, vec,
                  _bs((D_FF, D), lambda i: (0, 0)), _bs((D, D), lambda i: (0, 0)), _bs((PLE_DIM, D), lambda i: (0, 0))],
        out_specs=[rows] * 5 + [part] * 3,
        out_shape=[_sds((seq, D), F32)] + [_sds((seq, D), BF16)] * 4 + [_sds((nblk, 1, D), F32)] * 3,
        compiler_params=_cparams(1),
    )(up, x1, p, tgt, g_ple, g_post_mlp, w_down, w_pg, w_pp)


def _shift_rows_down(u, prev, n):
    rows = u.shape[0]
    ridx = lax.broadcasted_iota(jnp.int32, u.shape, 0)
    out = pltpu.roll(u, n, 0)
    for r in range(n):
        out = jnp.where(ridx == r, prev[8 - n + r:8 - n + r + 1, :], out)
    del rows
    return out


def _shift_rows_up(u, nxt, n):
    rows = u.shape[0]
    ridx = lax.broadcasted_iota(jnp.int32, u.shape, 0)
    out = pltpu.roll(u, rows - n, 0)
    for r in range(n):
        out = jnp.where(ridx == rows - n + r, nxt[r:r + 1, :], out)
    return out


CONV_COL0 = 3


def _conv_fwd(proj, w_conv, seq, tr):
    hb = tr // 8

    def body(cb_ref, cc_ref, cu_ref, ccp_ref, cup_ref, w_ref, e_ref):
        i = pl.program_id(0)
        u = cc_ref[...] * cu_ref[...]
        up = jnp.where(i > 0, ccp_ref[...] * cup_ref[...], 0.0)
        w = w_ref[...]
        d = w[0:1, :] * _shift_rows_down(u, up, 2) + w[1:2, :] * _shift_rows_down(u, up, 1) + w[2:3, :] * u
        e_ref[...] = (cb_ref[...] * d).astype(BF16)

    prev = lambda c: (lambda i: (jnp.maximum(i * hb - 1, 0), c))
    return pl.pallas_call(
        body, name="conv_fwd", grid=(seq // tr,),
        in_specs=[_bs((tr, CONV_W), lambda i: (i, CONV_COL0)),
                  _bs((tr, CONV_W), lambda i: (i, CONV_COL0 + 1)),
                  _bs((tr, CONV_W), lambda i: (i, CONV_COL0 + 2)),
                  _bs((8, CONV_W), prev(CONV_COL0 + 1)),
                  _bs((8, CONV_W), prev(CONV_COL0 + 2)),
                  _bs((3, CONV_W), lambda i: (0, 0))],
        out_specs=_bs((tr, CONV_W), lambda i: (i, 0)),
        out_shape=_sds((seq, CONV_W), BF16),
        compiler_params=_cparams(1),
    )(proj, proj, proj, proj, proj, w_conv)


def _conv_bwd(proj, de, w_conv, seq, tr):
    hb = tr // 8
    nblk = seq // tr

    def body(cb_ref, cc_ref, cu_ref, ccp_ref, cup_ref, cbn_ref, de_ref, den_ref, w_ref, o_ref, dw_ref):
        i = pl.program_id(0)
        cc, cu, cb = cc_ref[...], cu_ref[...], cb_ref[...]
        u = cc * cu
        up = jnp.where(i > 0, ccp_ref[...] * cup_ref[...], 0.0)
        u1 = _shift_rows_down(u, up, 1)
        u2 = _shift_rows_down(u, up, 2)
        de_ = de_ref[...]
        dd = de_ * cb
        ddn = jnp.where(i < nblk - 1, den_ref[...] * cbn_ref[...], 0.0)
        w = w_ref[...]
        du = w[2:3, :] * dd + w[1:2, :] * _shift_rows_up(dd, ddn, 1) + w[0:1, :] * _shift_rows_up(dd, ddn, 2)
        o_ref[:, 0:CONV_W] = (de_ * (w[0:1, :] * u2 + w[1:2, :] * u1 + w[2:3, :] * u)).astype(BF16)
        o_ref[:, CONV_W:2 * CONV_W] = (du * cu).astype(BF16)
        o_ref[:, 2 * CONV_W:3 * CONV_W] = (du * cc).astype(BF16)
        ridx = lax.broadcasted_iota(jnp.int32, (8, CONV_W), 0)
        dw0 = jnp.sum(dd * u2, axis=0, keepdims=True)
        dw1 = jnp.sum(dd * u1, axis=0, keepdims=True)
        dw2 = jnp.sum(dd * u, axis=0, keepdims=True)
        dw_ref[...] = jnp.where(ridx == 0, dw0, jnp.where(ridx == 1, dw1, jnp.where(ridx == 2, dw2, 0.0)))

    prev = lambda c: (lambda i: (jnp.maximum(i * hb - 1, 0), c))
    nxt = lambda c: (lambda i: (jnp.minimum((i + 1) * hb, seq // 8 - 1), c))
    return pl.pallas_call(
        body, name="conv_bwd", grid=(nblk,),
        in_specs=[_bs((tr, CONV_W), lambda i: (i, CONV_COL0)),
                  _bs((tr, CONV_W), lambda i: (i, CONV_COL0 + 1)),
                  _bs((tr, CONV_W), lambda i: (i, CONV_COL0 + 2)),
                  _bs((8, CONV_W), prev(CONV_COL0 + 1)),
                  _bs((8, CONV_W), prev(CONV_COL0 + 2)),
                  _bs((8, CONV_W), nxt(CONV_COL0)),
                  _bs((tr, CONV_W), lambda i: (i, 0)),
                  _bs((8, CONV_W), nxt(0)),
                  _bs((3, CONV_W), lambda i: (0, 0))],
        out_specs=[_bs((tr, 3 * CONV_W), lambda i: (i, 0)), _bs((None, 8, CONV_W), lambda i: (i, 0, 0))],
        out_shape=[_sds((seq, 3 * CONV_W), BF16), _sds((nblk, 8, CONV_W), F32)],
        compiler_params=_cparams(1),
    )(proj, proj, proj, proj, proj, proj, de, de, w_conv)


def _log_gates(z):
    lse = jnp.log(1.0 + jnp.exp(-jnp.abs(z)))
    log_beta = jnp.minimum(z, 0.0) - lse
    return log_beta, log_beta - z


DEAD_LOG_WEIGHT = -110.0
NO_TILE = -1e30


def _first_live_tile(start, scores, live_sc):
    def alive():
        return jnp.max(jnp.maximum(live_sc[0], live_sc[1])) > DEAD_LOG_WEIGHT

    def step(c):
        for h, z in enumerate(scores(c[0])):
            live_sc[h] = live_sc[h] + jnp.sum(_log_gates(z)[1], axis=-1, keepdims=True)
        return c[0] - 1, alive()

    j_end, _ = lax.while_loop(lambda c: jnp.logical_and(c[0] >= 0, c[1]), step, (start, alive()))
    return j_end + 1


def _attn_fwd(proj, seq):
    blk = ATT_BLK
    nq = seq // blk
    npair = N_HEADS // 2

    def body(q_ref, k_ref, v_ref, o_ref, z0_sc, z1_sc, w0_sc, w1_sc, tot_sc, acc_sc):
        i = pl.program_id(1)
        is_a = lax.broadcasted_iota(jnp.int32, (1, LANES), 1) < HEAD_DIM
        q2 = (q_ref[...] * Q_SCALE).astype(BF16)
        zero = jnp.zeros_like(q2)
        qs = (jnp.where(is_a, q2, zero), jnp.where(is_a, zero, q2))
        row = lax.broadcasted_iota(jnp.int32, (blk, blk), 0)
        col = lax.broadcasted_iota(jnp.int32, (blk, blk), 1)
        tri = (row > col).astype(BF16)
        causal = col < row

        def tile_of(ref, j):
            return ref[pl.ds(pl.multiple_of(j * blk, blk), blk), :].astype(BF16)

        def scores(j):
            k2 = tile_of(k_ref, j)
            return [_nt(qs[h], k2) for h in range(2)]

        has_left = i > 0
        left = jnp.maximum(i - 1, 0)

        g_d = [_log_gates(z) for z in scores(i)]
        g_l = [_log_gates(z) for z in scores(left)]
        keep_d = [jnp.where(causal, g[1], 0.0) for g in g_d]
        suf_d = [_nn(lk.astype(BF16), tri) for lk in keep_d]
        suf_l = [_nn(g[1].astype(BF16), tri) for g in g_l]
        v_d, v_l = tile_of(v_ref, i), tile_of(v_ref, left)
        pv = []
        for h in range(2):
            sum_d = jnp.sum(keep_d[h], axis=-1, keepdims=True)
            w_d = jnp.where(causal, jnp.exp(g_d[h][0] + suf_d[h]), 0.0)
            w_l = jnp.exp(g_l[h][0] + (jnp.where(has_left, sum_d, NO_TILE) + suf_l[h]))
            pv.append(_nn(w_d.astype(BF16), v_d) + _nn(w_l.astype(BF16), v_l))
            tot_sc[h] = sum_d + jnp.sum(g_l[h][1], axis=-1, keepdims=True)
        acc_sc[...] = jnp.where(is_a, pv[0], pv[1])

        z_bufs, w_bufs = (z0_sc, z1_sc), (w0_sc, w1_sc)

        def alive():
            return jnp.max(jnp.maximum(tot_sc[0], tot_sc[1])) > DEAD_LOG_WEIGHT

        def put(ref, vals):
            for h in range(2):
                ref[h] = vals[h]

        def weights(zs):
            gates = [_log_gates(z) for z in zs]
            sums = [_nn(g[1].astype(BF16), tri) for g in gates]
            ws = []
            for h in range(2):
                ws.append(jnp.exp(gates[h][0] + (tot_sc[h] + sums[h])).astype(BF16))
                tot_sc[h] = tot_sc[h] + jnp.sum(gates[h][1], axis=-1, keepdims=True)
            return ws

        def add_values(w_buf, j):
            v2 = tile_of(v_ref, j)
            acc_sc[...] += jnp.where(is_a, _nn(w_buf[0], v2), _nn(w_buf[1], v2))

        def trip(j, s):
            add_values(w_bufs[s], j + 1)
            put(z_bufs[1 - s], scores(jnp.maximum(j - 1, 0)))
            put(w_bufs[1 - s], weights((z_bufs[s][0], z_bufs[s][1])))

        @pl.when(jnp.logical_and(i >= 2, alive()))
        def _():
            put(z0_sc, scores(i - 2))
            w0_sc[...] = jnp.zeros_like(w0_sc)

            def two_trips(c):
                trip(c[0], 0)
                trip(c[0] - 1, 1)
                return c[0] - 2, alive()

            j_next, still = lax.while_loop(lambda c: jnp.logical_and(c[0] >= 1, c[1]), two_trips, (i - 2, i >= 2))
            one_left = jnp.logical_and(j_next == 0, still)

            @pl.when(one_left)
            def _():
                trip(0, 0)
                add_values(w1_sc, 0)

            @pl.when(jnp.logical_not(one_left))
            def _():
                add_values(w0_sc, j_next + 1)

        o_ref[...] = acc_sc[...].astype(BF16)

    return pl.pallas_call(
        body, name="attn_fwd", grid=(npair, nq),
        in_specs=[_bs((blk, LANES), lambda p, i: (i, p)),
                  _bs((seq, LANES), lambda p, i: (0, npair + p)),
                  _bs((seq, LANES), lambda p, i: (0, 2 * npair + p))],
        out_specs=_bs((blk, LANES), lambda p, i: (i, p)),
        out_shape=_sds((seq, ATTN_W), BF16),
        scratch_shapes=[pltpu.VMEM((2, blk, blk), F32), pltpu.VMEM((2, blk, blk), F32),
                        pltpu.VMEM((2, blk, blk), BF16), pltpu.VMEM((2, blk, blk), BF16),
                        pltpu.VMEM((2, blk, 1), F32), pltpu.VMEM((blk, LANES), F32)],
        compiler_params=_cparams(2),
    )(proj, proj, proj)


def _attn_bwd(proj, do, seq):
    blk = ATT_BLK
    nq = seq // blk
    npair = N_HEADS // 2

    def body(q_ref, k_ref, v_ref, do_ref, dq_ref, dk_ref, dv_ref,
             prod0_sc, prod1_sc, pend0_sc, pend1_sc, tot_sc, live_sc, cum_sc, pre_sc, dq_sc):
        i = pl.program_id(1)

        @pl.when(i == 0)
        def _():
            dk_ref[...] = jnp.zeros_like(dk_ref)
            dv_ref[...] = jnp.zeros_like(dv_ref)

        is_a = lax.broadcasted_iota(jnp.int32, (1, LANES), 1) < HEAD_DIM
        q2 = (q_ref[...] * Q_SCALE).astype(BF16)
        do2 = do_ref[...]
        zero = jnp.zeros_like(q2)
        qs = (jnp.where(is_a, q2, zero), jnp.where(is_a, zero, q2))
        dos = (jnp.where(is_a, do2, zero), jnp.where(is_a, zero, do2))
        row = lax.broadcasted_iota(jnp.int32, (blk, blk), 0)
        col = lax.broadcasted_iota(jnp.int32, (blk, blk), 1)
        tri_after = (row > col).astype(BF16)
        tri_excl = (row < col).astype(BF16)
        causal = col < row

        def tile_of(ref, j):
            return ref[pl.ds(pl.multiple_of(j * blk, blk), blk), :].astype(BF16)

        def scores(j):
            k2 = tile_of(k_ref, j)
            return [_nt(qs[h], k2) for h in range(2)]

        def products(j):
            v2 = tile_of(v_ref, j)
            return scores(j) + [_nt(dos[h], v2) for h in range(2)]

        def row_sum(a):
            return jnp.sum(a, axis=-1, keepdims=True)

        def grad_matmuls(ws, dzs, j):
            rows = pl.ds(pl.multiple_of(j * blk, blk), blk)
            k2 = tile_of(k_ref, j)
            dq_sc[...] += jnp.where(is_a, _nn(dzs[0], k2), _nn(dzs[1], k2))
            dk_ref[rows, :] += jnp.where(is_a, _tn(dzs[0], q2), _tn(dzs[1], q2))
            if ws is not None:
                dv_ref[rows, :] += jnp.where(is_a, _tn(ws[0], do2), _tn(ws[1], do2))

        has_left = i > 0
        left = jnp.maximum(i - 1, 0)

        p_d, p_l = products(i), products(left)
        g_d = [_log_gates(z) for z in p_d[:2]]
        g_l = [_log_gates(z) for z in p_l[:2]]
        keep_d = [jnp.where(causal, g[1], 0.0) for g in g_d]
        suf_d = [_nn(lk.astype(BF16), tri_after) for lk in keep_d]
        suf_l = [_nn(g[1].astype(BF16), tri_after) for g in g_l]
        w_d, w_l, gg_d, gg_l = [], [], [], []
        for h in range(2):
            sum_d = row_sum(keep_d[h])
            w_d.append(jnp.where(causal, jnp.exp(g_d[h][0] + suf_d[h]), 0.0))
            w_l.append(jnp.exp(g_l[h][0] + (jnp.where(has_left, sum_d, NO_TILE) + suf_l[h])))
            gg_d.append(p_d[2 + h] * w_d[h])
            gg_l.append(p_l[2 + h] * w_l[h])
            tot_sc[h] = sum_d + row_sum(g_l[h][1])
        before_d = [_nn(g.astype(BF16), tri_excl) for g in gg_d]
        before_l = [_nn(g.astype(BF16), tri_excl) for g in gg_l]
        dz_d, dz_l = [], []
        for h in range(2):
            beta_d, beta_l = jnp.exp(g_d[h][0]), jnp.exp(g_l[h][0])
            dz_l.append((gg_l[h] * (1.0 - beta_l) - before_l[h] * beta_l).astype(BF16))
            dz = gg_d[h] * (1.0 - beta_d) - (row_sum(gg_l[h]) + before_d[h]) * beta_d
            dz_d.append(jnp.where(causal, dz, 0.0).astype(BF16))
        dq_sc[...] = jnp.zeros_like(dq_sc)
        grad_matmuls([w.astype(BF16) for w in w_l], dz_l, left)
        grad_matmuls([w.astype(BF16) for w in w_d], dz_d, i)

        live_sc[...] = tot_sc[...]
        first = _first_live_tile(i - 2, scores, live_sc)
        trips = i - 1 - first
        prod_bufs, pend_bufs = (prod0_sc, prod1_sc), (pend0_sc, pend1_sc)

        def local_grads(prods):
            zs, dws = prods[:2], prods[2:]
            gates = [_log_gates(z) for z in zs]
            sums = [_nn(g[1].astype(BF16), tri_after) for g in gates]
            ws, gs = [], []
            for h in range(2):
                cum = cum_sc[h] + row_sum(gates[h][1])
                cum_sc[h] = cum
                ws.append(jnp.exp(gates[h][0] + ((live_sc[h] - cum) + sums[h])))
                gs.append(dws[h] * ws[h])
            befores = [_nn(g.astype(BF16), tri_excl) for g in gs]
            dzs = []
            for h in range(2):
                beta = jnp.exp(gates[h][0])
                dzs.append((gs[h] * (1.0 - beta) - (pre_sc[h] + befores[h]) * beta).astype(BF16))
                pre_sc[h] = pre_sc[h] + row_sum(gs[h])
            return [w.astype(BF16) for w in ws] + dzs

        def put(ref, vals):
            for n, val in enumerate(vals):
                ref[n] = val

        def flush(pend, j):
            grad_matmuls([pend[0], pend[1]], [pend[2], pend[3]], j)

        def trip(j, s):
            flush(pend_bufs[s], jnp.maximum(j - 1, first))
            put(prod_bufs[1 - s], products(j + 1))
            put(pend_bufs[1 - s], local_grads([prod_bufs[s][n] for n in range(4)]))

        def earlier_keys_share(j, mask):
            dzs = []
            for h, z in enumerate(scores(j)):
                beta = jnp.exp(_log_gates(z)[0])
                dzs.append(jnp.where(mask, -pre_sc[h] * beta, 0.0).astype(BF16))
            grad_matmuls(None, dzs, j)

        @pl.when(trips > 0)
        def _():
            cum_sc[...] = jnp.zeros_like(cum_sc)
            pre_sc[...] = jnp.zeros_like(pre_sc)
            pend0_sc[...] = jnp.zeros_like(pend0_sc)
            put(prod0_sc, products(first))

            def two_trips(pp, carry):
                trip(first + 2 * pp, 0)
                trip(first + 2 * pp + 1, 1)
                return carry

            lax.fori_loop(0, trips // 2, two_trips, 0)
            odd = trips % 2 == 1

            @pl.when(odd)
            def _():
                trip(i - 2, 0)
                flush(pend1_sc, i - 2)

            @pl.when(jnp.logical_not(odd))
            def _():
                flush(pend0_sc, i - 2)

            earlier_keys_share(i - 1, True)
            earlier_keys_share(i, causal)

        dq_ref[...] = dq_sc[...] * Q_SCALE

    qmap = lambda p, i: (i, p)
    return pl.pallas_call(
        body, name="attn_bwd", grid=(npair, nq),
        in_specs=[_bs((blk, LANES), qmap),
                  _bs((seq, LANES), lambda p, i: (0, npair + p)),
                  _bs((seq, LANES), lambda p, i: (0, 2 * npair + p)),
                  _bs((blk, LANES), qmap)],
        out_specs=[_bs((blk, LANES), qmap),
                   _bs((seq, LANES), lambda p, i: (0, p)),
                   _bs((seq, LANES), lambda p, i: (0, p))],
        out_shape=[_sds((seq, ATTN_W), F32)] * 3,
        scratch_shapes=[pltpu.VMEM((4, blk, blk), F32), pltpu.VMEM((4, blk, blk), F32),
                        pltpu.VMEM((4, blk, blk), BF16), pltpu.VMEM((4, blk, blk), BF16),
                        pltpu.VMEM((2, blk, 1), F32), pltpu.VMEM((2, blk, 1), F32), pltpu.VMEM((2, blk, 1), F32),
                        pltpu.VMEM((2, blk, 1), F32), pltpu.VMEM((blk, LANES), F32)],
        compiler_params=_cparams(2),
    )(proj, proj, proj, do)


def _elementwise(name, fn, ins, out_dtypes):
    rows, cols = ins[0].shape
    tr = rows
    for cand in (512, 256, 128, 64, 32, 16, 8):
        if rows % cand == 0 and cand * cols * 4 <= 2 * 1024 * 1024:
            tr = cand
            break
    n_in = len(ins)

    def body(*refs):
        res = fn(*[r[...] for r in refs[:n_in]])
        for r, val in zip(refs[n_in:], res):
            r[...] = val.astype(r.dtype)

    spec = _bs((tr, cols), lambda i: (i, 0))
    return pl.pallas_call(
        body, name=name, grid=(rows // tr,),
        in_specs=[spec] * n_in, out_specs=[spec] * len(out_dtypes),
        out_shape=[_sds((rows, cols), dt) for dt in out_dtypes],
        compiler_params=_cparams(1),
    )(*ins)


def _adamw_fn(w, g, m, v):
    m = ADAM_B1 * m + (1.0 - ADAM_B1) * g
    v = ADAM_B2 * v + (1.0 - ADAM_B2) * (g * g)
    m_hat = m / (1.0 - ADAM_B1 ** ADAM_STEP)
    v_hat = v / (1.0 - ADAM_B2 ** ADAM_STEP)
    delta = -ADAM_LR * (m_hat / (jnp.sqrt(v_hat) + ADAM_EPS) + ADAM_WD * w)
    return delta, m, v


def _adamw(name, w, g, m, v):
    shape = w.shape
    as2d = lambda a: a.reshape(-1, shape[-1])
    delta, nm, nv = _elementwise(name, _adamw_fn, [as2d(w), as2d(g), as2d(m), as2d(v)], [F32, F32, F32])
    return delta.reshape(shape), nm.reshape(shape), nv.reshape(shape)


def _place():
    return lax.axis_index("x"), lax.axis_index("y"), lax.axis_index("c")


ANY = pl.BlockSpec(memory_space=pl.ANY)
VMEM_WHOLE = pl.BlockSpec(memory_space=pltpu.VMEM)


def _allgather_weights(shards):
    n = len(shards)

    def body(*refs):
        src, dst = refs[:n], refs[n:2 * n]
        send_sems, recv_sems, local_sems = refs[2 * n:]
        x, y, c = _place()
        me, sibling, mychip = (x, y, c), (x, y, 1 - c), 2 * x + y

        x_nbr, y_nbr, diag = 2 * (1 - x) + y, 2 * x + (1 - y), 2 * (1 - x) + (1 - y)
        to_x, to_y = (1 - x, y, c), (x, 1 - y, c)

        def parts(w):
            hr = src[w].shape[0] // 2
            first = hr // 2 if hr % 32 == 0 else hr
            return first, hr - first

        def rows_of(w, chip, half, route):
            hr = src[w].shape[0] // 2
            first, second = parts(w)
            start, size = {0: (0, hr), 1: (0, hr), 2: (0, first), 3: (first, second)}[route]
            return dst[w].at[chip, pl.ds(half * hr + start, size)]

        def copy(w, k, src_ref, dst_ref, to):
            return pltpu.make_async_remote_copy(src_ref=src_ref, dst_ref=dst_ref, send_sem=send_sems.at[w, k],
                                                recv_sem=recv_sems.at[w, k], device_id=to, device_id_type=MESH)

        def landed(w, route):
            chip = {0: x_nbr, 1: y_nbr, 2: diag, 3: diag}[route]
            return rows_of(w, chip, c, route), chip

        def routes(w):
            return (0, 1, 2, 3) if parts(w)[1] else (0, 1, 2)

        started, local = [], []
        for w in range(n):
            hr = src[w].shape[0] // 2
            own = pltpu.make_async_copy(src[w], dst[w].at[mychip], local_sems.at[w])
            own.start()
            local.append(own)
            mine = src[w].at[pl.ds(c * hr, hr)]
            for route, to in ((0, to_x), (1, to_y)):
                cp = copy(w, route, mine, rows_of(w, mychip, c, route), to)
                cp.start()
                started.append(cp)

        def pass_on(w, route):
            got, chip = landed(w, route)
            copy(w, route, got, got, me).wait_recv()
            if route == 1:
                part = rows_of(w, chip, c, 2)
                started.append(copy(w, 2, part, part, to_x))
                started[-1].start()
            if route == 0 and parts(w)[1]:
                part = rows_of(w, chip, c, 3)
                started.append(copy(w, 3, part, part, to_y))
                started[-1].start()
            started.append(copy(w, 4 + route, got, got, sibling))
            started[-1].start()

        for w in range(n):
            pass_on(w, 1)
            pass_on(w, 0)
        for w in range(n):
            for route in routes(w)[2:]:
                pass_on(w, route)
        for w in range(n):
            for route in routes(w):
                chip = landed(w, route)[1]
                from_sib = rows_of(w, chip, 1 - c, route)
                copy(w, 4 + route, from_sib, from_sib, me).wait_recv()
        for cp in local:
            cp.wait()
        for cp in started:
            cp.wait_send()

    return pl.pallas_call(
        body, name="allgather_weights",
        in_specs=[VMEM_WHOLE] * n, out_specs=[VMEM_WHOLE] * n,
        out_shape=[_sds((N_CHIPS,) + s.shape, s.dtype) for s in shards],
        scratch_shapes=[pltpu.SemaphoreType.DMA((n, 8)), pltpu.SemaphoreType.DMA((n, 8)),
                        pltpu.SemaphoreType.DMA((n,))],
        compiler_params=pltpu.CompilerParams(vmem_limit_bytes=VMEM_LIMIT),
    )(*shards)


SUM_ROWS = 64


def _rs_pair_sum(name, grads):
    n = len(grads)

    def body(*refs):
        g, out = refs[:n], refs[n:2 * n]
        stage, land, keep = refs[2 * n:3 * n], refs[3 * n:4 * n], refs[4 * n:5 * n]
        send_sems, recv_sems, stage_sems, keep_sems = refs[5 * n:]
        x, y, c = _place()
        sibling = (x, y, 1 - c)
        loads = []
        for w in range(n):
            hr = g[w].shape[1] // 2
            st = pltpu.make_async_copy(g[w].at[:, pl.ds((1 - c) * hr, hr)], stage[w], stage_sems.at[w])
            kp = pltpu.make_async_copy(g[w].at[:, pl.ds(c * hr, hr)], keep[w], keep_sems.at[w])
            st.start()
            kp.start()
            loads.append((st, kp))
        gives = []
        for w in range(n):
            loads[w][0].wait()
            give = pltpu.make_async_remote_copy(src_ref=stage[w], dst_ref=land[w], send_sem=send_sems.at[w],
                                                recv_sem=recv_sems.at[w], device_id=sibling, device_id_type=MESH)
            give.start()
            gives.append(give)
        for w in range(n):
            loads[w][1].wait()
            gives[w].wait_recv()
            nb = g[w].shape[1] // 2 // SUM_ROWS

            def add(idx, carry, w=w, nb=nb):
                k, r = idx // nb, pl.multiple_of((idx % nb) * SUM_ROWS, SUM_ROWS)
                rows = pl.ds(r, SUM_ROWS)
                out[w][k, rows, :] = (keep[w][k, rows, :] + land[w][k, rows, :]).astype(BF16)
                return carry

            lax.fori_loop(0, N_CHIPS * nb, add, 0)
        for give in gives:
            give.wait_send()

    half = [(N_CHIPS, a.shape[1] // 2, a.shape[2]) for a in grads]
    bufs = [pltpu.VMEM(s, F32) for s in half]
    sems = pltpu.SemaphoreType.DMA((n,))
    return pl.pallas_call(
        body, name=name,
        in_specs=[ANY] * n, out_specs=[VMEM_WHOLE] * n, out_shape=[_sds(s, BF16) for s in half],
        scratch_shapes=bufs + bufs + bufs + [sems, sems, sems, sems],
        compiler_params=pltpu.CompilerParams(vmem_limit_bytes=VMEM_LIMIT),
    )(*grads)


def _rs_exchange_join(parts):
    n = len(parts)

    def body(*refs):
        t, full = refs[:n], refs[n:2 * n]
        got_x, got_y, pass_on, got_2 = (refs[m * n:(m + 1) * n] for m in range(2, 6))
        send_sems, recv_sems = refs[6 * n:]
        x, y, c = _place()
        mychip, sibling = 2 * x + y, (x, y, 1 - c)
        x_nbr, y_nbr, diag = 2 * (1 - x) + y, 2 * x + (1 - y), 2 * (1 - x) + (1 - y)
        to_x, to_y = (1 - x, y, c), (x, 1 - y, c)
        sends = []

        def copy(w, k, src_ref, dst_ref, to):
            return pltpu.make_async_remote_copy(src_ref=src_ref, dst_ref=dst_ref, send_sem=send_sems.at[w, k],
                                                recv_sem=recv_sems.at[w, k], device_id=to, device_id_type=MESH)

        def start(cp):
            cp.start()
            sends.append(cp)

        def add_rows(w, count, fn):
            def step(idx, carry):
                fn(pl.ds(pl.multiple_of(idx * SUM_ROWS, SUM_ROWS), SUM_ROWS), pl.multiple_of(idx * SUM_ROWS, SUM_ROWS))
                return carry
            lax.fori_loop(0, count // SUM_ROWS, step, 0)

        f32 = lambda v: v.astype(F32)
        for w in range(n):
            ha = t[w].shape[1] // 2
            part_a, part_b = pl.ds(0, ha), pl.ds(ha, ha)
            start(copy(w, 0, t[w].at[x_nbr, part_a], got_x[w].at[0], to_x))
            start(copy(w, 1, t[w].at[diag, part_a], got_x[w].at[1], to_x))
            start(copy(w, 2, t[w].at[y_nbr, part_b], got_y[w].at[0], to_y))
            start(copy(w, 3, t[w].at[diag, part_b], got_y[w].at[1], to_y))
        for w in range(n):
            hr = t[w].shape[1]
            ha = hr // 2
            for k in (0, 1):
                copy(w, k, got_x[w].at[k], got_x[w].at[k], to_x).wait_recv()

            def sum_a(rows, r, w=w, hr=hr):
                full[w][pl.ds(pl.multiple_of(c * hr + r, SUM_ROWS), SUM_ROWS), :] = \
                    f32(t[w][mychip, rows, :]) + f32(got_x[w][0, rows, :])
                pass_on[w][rows, :] = (f32(t[w][y_nbr, rows, :]) + f32(got_x[w][1, rows, :])).astype(BF16)

            add_rows(w, ha, sum_a)
            start(copy(w, 4, pass_on[w].at[pl.ds(0, ha)], got_2[w].at[pl.ds(0, ha)], to_y))
            for k in (2, 3):
                copy(w, k, got_y[w].at[k - 2], got_y[w].at[k - 2], to_y).wait_recv()

            def sum_b(rows, r, w=w, hr=hr, ha=ha):
                lower = pl.ds(pl.multiple_of(ha + r, SUM_ROWS), SUM_ROWS)
                full[w][pl.ds(pl.multiple_of(c * hr + ha + r, SUM_ROWS), SUM_ROWS), :] = \
                    f32(t[w][mychip, lower, :]) + f32(got_y[w][0, rows, :])
                pass_on[w][lower, :] = (f32(t[w][x_nbr, lower, :]) + f32(got_y[w][1, rows, :])).astype(BF16)

            add_rows(w, ha, sum_b)
            start(copy(w, 5, pass_on[w].at[pl.ds(ha, ha)], got_2[w].at[pl.ds(ha, ha)], to_x))
        for w in range(n):
            hr = t[w].shape[1]
            ha = hr // 2
            copy(w, 4, got_2[w].at[pl.ds(0, ha)], got_2[w].at[pl.ds(0, ha)], to_y).wait_recv()
            copy(w, 5, got_2[w].at[pl.ds(ha, ha)], got_2[w].at[pl.ds(ha, ha)], to_x).wait_recv()

            def finish(rows, r, w=w, hr=hr):
                out_rows = pl.ds(pl.multiple_of(c * hr + r, SUM_ROWS), SUM_ROWS)
                full[w][out_rows, :] = full[w][out_rows, :] + f32(got_2[w][rows, :])

            add_rows(w, hr, finish)
            mine = full[w].at[pl.ds(c * hr, hr)]
            start(copy(w, 6, mine, mine, sibling))
        for w in range(n):
            hr = t[w].shape[1]
            theirs = full[w].at[pl.ds((1 - c) * hr, hr)]
            copy(w, 6, theirs, theirs, sibling).wait_recv()
        for cp in sends:
            cp.wait_send()

    half = lambda a: pltpu.VMEM((2, a.shape[1] // 2, a.shape[2]), a.dtype)
    whole = lambda a: pltpu.VMEM(a.shape[1:], a.dtype)
    return pl.pallas_call(
        body, name="rs_exchange_join",
        in_specs=[VMEM_WHOLE] * n, out_specs=[VMEM_WHOLE] * n,
        out_shape=[_sds((2 * a.shape[1], a.shape[2]), F32) for a in parts],
        scratch_shapes=[half(a) for a in parts] + [half(a) for a in parts] + [whole(a) for a in parts]
        + [whole(a) for a in parts] + [pltpu.SemaphoreType.DMA((n, 7)), pltpu.SemaphoreType.DMA((n, 7))],
        compiler_params=pltpu.CompilerParams(vmem_limit_bytes=VMEM_LIMIT),
    )(*parts)


def _small_allreduce(loss_p, dg_parts, dbg_a, dbg_c, dwc):
    ins = [loss_p] + list(dg_parts) + [dbg_a, dbg_c, dwc]
    n_in = len(ins)
    vmem = pl.BlockSpec(memory_space=pltpu.VMEM)

    def body(*refs):
        in_refs = refs[:n_in]
        out_ref, vec, buf, send_sems, recv_sems = refs[n_in:]
        x, y, c = _place()
        me = 4 * x + 2 * y + c
        vec[...] = jnp.zeros_like(vec)
        vec[0:1, :] = jnp.sum(in_refs[0][...], axis=0)
        for r in range(5):
            vec[1 + r:2 + r, :] = jnp.sum(in_refs[1 + r][...], axis=0)
        vec[6:7, :] = jnp.sum(in_refs[6][...], axis=0)
        vec[7:8, :] = jnp.sum(in_refs[7][...], axis=0)
        vec[8:16, 0:CONV_W] = jnp.sum(in_refs[8][...], axis=0)
        buf[pl.ds(me, 1)] = vec[...][None]
        copies = []
        for r in range(1, 8):
            fx, fy, fc = (r >> 2) & 1, (r >> 1) & 1, r & 1
            to = (1 - x if fx else x, 1 - y if fy else y, 1 - c if fc else c)
            cp = pltpu.make_async_remote_copy(src_ref=vec, dst_ref=buf.at[me], send_sem=send_sems.at[r - 1],
                                              recv_sem=recv_sems.at[r - 1], device_id=to, device_id_type=MESH)
            cp.start()
            copies.append(cp)
        for cp in copies:
            cp.wait()
        total = buf[0]
        for s in range(1, 8):
            total = total + buf[s]
        out_ref[...] = total
        out_ref[0:1, :] = jnp.broadcast_to(jnp.sum(total[0:1, :], axis=-1, keepdims=True), (1, D_MODEL))

    return pl.pallas_call(
        body, name="small_allreduce",
        in_specs=[vmem] * n_in, out_specs=vmem, out_shape=_sds((SMALL_ROWS, D_MODEL), F32),
        scratch_shapes=[pltpu.VMEM((SMALL_ROWS, D_MODEL), F32), pltpu.VMEM((8, SMALL_ROWS, D_MODEL), F32),
                        pltpu.SemaphoreType.DMA((7,)), pltpu.SemaphoreType.DMA((7,))],
    )(*ins)


def _local_step(x, p, tgt, g, b_gate, w_conv, wf):
    seq = x.shape[0]
    tm = min(seq, 1024)
    th = min(seq, 512)
    tl = min(seq, 2048)
    ni, nh, nl = seq // tm, seq // th, seq // tl
    g_pre_mix, g_post_mix, g_pre_mlp, g_post_mlp, g_ple = g
    w_in, w_ao, w_co, w_o, w_up, w_down, w_pg, w_pp, w_in_nat, w_up_nat = wf
    D = D_MODEL
    vec = lambda a, blk=0: (a, _bs((1, D), lambda i, j, k: (0, blk)))
    rows_i = lambda a, t, blk=0: (a, _bs((t, D), lambda i, j, k: (i, blk)))
    rows_k = lambda a, t, blk=0: (a, _bs((t, D), lambda i, j, k: (k, blk)))
    part = lambda n: (_sds((n, 1, D), F32), _bs((None, 1, D), lambda i, j, k: (i, 0, 0)))
    full2 = lambda a: (a, _bs(a.shape, lambda i, j, k: (0, 0)))

    normed = lambda xb, gb: (_rms(xb, gb).astype(BF16),) * 2
    keep_a = lambda t: [(_sds((seq, D), BF16), _bs((t, D), lambda i, j, k: (i, 0)))]
    main_w = D_IN - 2 * D
    proj, gates, h1 = _mm("proj_in", "nn", (nh, 1, 1),
                          a_ins=[rows_i(x, th), vec(g_pre_mix)], a_fn=normed,
                          b_ins=[full2(w_in_nat)], b_fn=_ident,
                          epi_fn=lambda acc: (acc[:, :main_w], acc[:, main_w:]),
                          outs=[(_sds((seq, main_w), F32), _bs((th, main_w), lambda i, j, k: (i, 0))),
                                (_sds((seq, 2 * D), BF16), _bs((th, 2 * D), lambda i, j, k: (i, 0)))],
                          acc_shape=(th, D_IN), a_cache=((th, D), BF16), a_outs=keep_a(th))
    o = _attn_fwd(proj, seq)
    e = _conv_fwd(proj, w_conv, seq, tm)

    def gate_values(ga, gc, ba, bc):
        return _sig(ga.astype(F32) + ba), _sig(gc.astype(F32) + bc)

    def branch_outputs(ob, eb, wao, wco):
        return _nn(ob, wao).astype(BF16).astype(F32), _nn(eb, wco).astype(BF16).astype(F32)

    def mix_fn(ga, gc, ob, eb, ba, bc, wao, wco):
        sa, sc = gate_values(ga, gc, ba, bc)
        ya, yc = branch_outputs(ob, eb, wao, wco)
        return ((sa * ya + sc * yc).astype(BF16),) * 2

    def post_mix(acc, xb, gb):
        return acc, xb + _rms(acc, gb)

    half_rows = lambda a: (a, _bs((th, a.shape[1]), lambda i, j, k: (i, 0)))
    mix_ins = [rows_i(gates, th, 0), rows_i(gates, th, 1), half_rows(o), half_rows(e), vec(b_gate, 0), vec(b_gate, 1),
               full2(w_ao), full2(w_co)]
    mixed, x1, mixin = _mm(
        "mix_out", "nn", (nh, 1, 1),
        a_ins=mix_ins, a_fn=mix_fn, b_ins=[full2(w_o)], b_fn=_ident,
        epi_ins=[rows_i(x, th), vec(g_post_mix)], epi_fn=post_mix,
        outs=[(_sds((seq, D), BF16), _bs((th, D), lambda i, j, k: (i, 0))),
              (_sds((seq, D), F32), _bs((th, D), lambda i, j, k: (i, 0)))],
        acc_shape=(th, D), a_cache=((th, D), BF16), a_outs=keep_a(th))
    up, h2 = _mm("mlp_up", "nn", (nh, 1, 1),
                 a_ins=[rows_i(x1, th), vec(g_pre_mlp)], a_fn=normed,
                 b_ins=[full2(w_up_nat)], b_fn=_ident,
                 outs=[(_sds((seq, D_FF), BF16), _bs((th, D_FF), lambda i, j, k: (i, 0)))],
                 acc_shape=(th, D_FF), a_cache=((th, D), BF16), a_outs=keep_a(th))

    def relu2(ub):
        r = jnp.maximum(ub.astype(F32), 0.0)
        return (r * r).astype(BF16)

    dx2, df, dpre, h3, dpp, loss_p, dg_ple_p, dg_post_mlp_p = _mlp_down_ple_head(
        up, x1, p, tgt, g_ple, g_post_mlp, w_down, w_pg, w_pp, seq, th)

    (dw_pp,) = _mm("dw_ple_proj", "tn", (1, 1, nh),
                   a_ins=[(p, _bs((th, PLE_DIM), lambda i, j, k: (k, 0)))], a_fn=_to_bf16,
                   b_ins=[rows_k(dpp, th)], b_fn=_ident,
                   outs=[(_sds((PLE_DIM, D), F32), _bs((PLE_DIM, D), lambda i, j, k: (0, 0)))],
                   acc_shape=(PLE_DIM, D))
    (dw_pg,) = _mm("dw_ple_gate", "tn", (1, 1, nl),
                   a_ins=[rows_k(h3, tl)], a_fn=_ident, b_ins=[rows_k(dpre, tl)], b_fn=_ident,
                   outs=[(_sds((D, D), F32), _bs((D, D), lambda i, j, k: (0, 0)))], acc_shape=(D, D))

    def dup_fn(acc, ub):
        return (acc * (2.0 * jnp.maximum(ub.astype(F32), 0.0)),)

    (dup,) = _mm("d_mlp_down", "nt", (nh, 1, 1),
                 a_ins=[rows_i(df, th)], a_fn=_ident, b_ins=[full2(w_down)], b_fn=_ident,
                 epi_ins=[(up, _bs((th, D_FF), lambda i, j, k: (i, 0)))], epi_fn=dup_fn,
                 outs=[(_sds((seq, D_FF), BF16), _bs((th, D_FF), lambda i, j, k: (i, 0)))],
                 acc_shape=(th, D_FF))
    (dw_down,) = _mm("dw_mlp_down", "tn", (4, 1, nl),
                     a_ins=[(up, _bs((tl, D), lambda i, j, k: (k, i)))], a_fn=relu2,
                     b_ins=[rows_k(df, tl)], b_fn=_ident,
                     outs=[(_sds((D_FF, D), F32), _bs((D, D), lambda i, j, k: (i, 0)))], acc_shape=(D, D))
    (dw_up,) = _mm("dw_mlp_up", "tn", (1, 4, nl),
                   a_ins=[rows_k(h2, tl)], a_fn=_ident,
                   b_ins=[(dup, _bs((tl, D), lambda i, j, k: (k, j)))], b_fn=_ident,
                   outs=[(_sds((N_CHIPS, D, D), F32), _bs((None, D, D), lambda i, j, k: (j, 0, 0)))],
                   acc_shape=(D, D))

    def mlp_norm_bwd(acc, x1b, dx2b, mixedb, g_mlp, g_mix):
        dxn, dg_mlp = _rms_bwd(x1b, g_mlp, acc)
        dx1b = dx2b + dxn
        dmixedb, dg_mix = _rms_bwd(mixedb.astype(F32), g_mix, dx1b)
        return dx1b, dmixedb, dg_mlp, dg_mix

    dx1, dmixed, dg_pre_mlp_p, dg_post_mix_p = _mm(
        "d_mlp_up", "nt", (nh, 1, 1),
        a_ins=[(dup, _bs((th, D_FF), lambda i, j, k: (i, 0)))], a_fn=_ident,
        b_ins=[full2(w_up_nat)], b_fn=_ident,
        epi_ins=[rows_i(x1, th), rows_i(dx2, th), rows_i(mixed, th), vec(g_pre_mlp), vec(g_post_mix)],
        epi_fn=mlp_norm_bwd,
        outs=[(_sds((seq, D), F32), _bs((th, D), lambda i, j, k: (i, 0))),
              (_sds((seq, D), BF16), _bs((th, D), lambda i, j, k: (i, 0))), part(nh), part(nh)],
        acc_shape=(th, D))
    (dw_o,) = _mm("dw_mix_out", "tn", (1, 1, nl),
                  a_ins=[rows_k(mixin, tl)], a_fn=_ident, b_ins=[rows_k(dmixed, tl)], b_fn=_ident,
                  outs=[(_sds((D, D), F32), _bs((D, D), lambda i, j, k: (0, 0)))], acc_shape=(D, D))

    def gate_bwd(acc, ga, gc, ob, eb, ba, bc, wao, wco):
        sa, sc = gate_values(ga, gc, ba, bc)
        ya, yc = branch_outputs(ob, eb, wao, wco)
        dga = acc * ya * sa * (1.0 - sa)
        dgc = acc * yc * sc * (1.0 - sc)
        dya, dyc = (acc * sa).astype(BF16), (acc * sc).astype(BF16)
        return (dya, dyc, jnp.concatenate([dga, dgc], axis=1), _nt(dya, wao), _nt(dyc, wco),
                jnp.sum(dga, axis=0, keepdims=True), jnp.sum(dgc, axis=0, keepdims=True))

    dya, dyc, dgate, do, de, dbg_a_p, dbg_c_p = _mm(
        "d_mix_out", "nt", (nh, 1, 1),
        a_ins=[rows_i(dmixed, th)], a_fn=_ident, b_ins=[full2(w_o)], b_fn=_ident,
        epi_ins=mix_ins, epi_fn=gate_bwd,
        outs=[(_sds((seq, D), BF16), _bs((th, D), lambda i, j, k: (i, 0)))] * 2
             + [(_sds((seq, 2 * D), BF16), _bs((th, 2 * D), lambda i, j, k: (i, 0))),
                (_sds((seq, ATTN_W), BF16), _bs((th, ATTN_W), lambda i, j, k: (i, 0))),
                (_sds((seq, CONV_W), F32), _bs((th, CONV_W), lambda i, j, k: (i, 0))), part(nh), part(nh)],
        acc_shape=(th, D))
    (dw_ao,) = _mm("dw_attn_out", "tn", (1, 1, nh),
                   a_ins=[(o, _bs((th, ATTN_W), lambda i, j, k: (k, 0)))], a_fn=_ident,
                   b_ins=[rows_k(dya, th)], b_fn=_ident,
                   outs=[(_sds((ATTN_W, D), F32), _bs((ATTN_W, D), lambda i, j, k: (0, 0)))], acc_shape=(ATTN_W, D))
    dq, dk, dv = _attn_bwd(proj, do, seq)
    (dw_co,) = _mm("dw_conv_out", "tn", (1, 1, nh),
                   a_ins=[(e, _bs((th, CONV_W), lambda i, j, k: (k, 0)))], a_fn=_ident,
                   b_ins=[rows_k(dyc, th)], b_fn=_ident,
                   outs=[(_sds((CONV_W, D), F32), _bs((CONV_W, D), lambda i, j, k: (0, 0)))], acc_shape=(CONV_W, D))
    dconv, dwc_p = _conv_bwd(proj, de, w_conv, seq, tm)
    qkv_w = 3 * ATTN_W
    join_bf16 = lambda *blocks: jnp.concatenate([b.astype(BF16) for b in blocks], axis=1)
    piece = lambda a, t, rows, blk=0: (a, _bs((t, a.shape[1]), (lambda i, j, k: (k, blk)) if rows == "k"
                                             else (lambda i, j, k: (i, blk))))
    (dw_in_qkv,) = _mm("dw_proj_in_qkv", "tn", (1, 1, ni),
                       a_ins=[rows_k(h1, tm)], a_fn=_ident,
                       b_ins=[piece(dq, tm, "k"), piece(dk, tm, "k"), piece(dv, tm, "k")], b_fn=join_bf16,
                       outs=[(_sds((D, qkv_w), F32), _bs((D, qkv_w), lambda i, j, k: (0, 0)))], acc_shape=(D, qkv_w))
    (dw_in_conv,) = _mm("dw_proj_in_conv", "tn", (1, 1, nl),
                        a_ins=[rows_k(h1, tl)], a_fn=_ident, b_ins=[piece(dconv, tl, "k")], b_fn=_ident,
                        outs=[(_sds((D, 3 * CONV_W), F32), _bs((D, 3 * CONV_W), lambda i, j, k: (0, 0)))],
                        acc_shape=(D, 3 * CONV_W))
    (dw_in_gate,) = _mm("dw_proj_in_gate", "tn", (1, 2, nl),
                        a_ins=[rows_k(h1, tl)], a_fn=_ident,
                        b_ins=[(dgate, _bs((tl, D), lambda i, j, k: (k, j)))], b_fn=_ident,
                        outs=[(_sds((D, 2 * D), F32), _bs((D, D), lambda i, j, k: (0, j)))], acc_shape=(D, D))
    dw_in = jnp.concatenate([dw_in_qkv, dw_in_conv, dw_in_gate], axis=1)

    def in_norm_bwd(acc, xb, dx1b, gb):
        dxn, dg = _rms_bwd(xb, gb, acc)
        return dx1b + dxn, dg

    grad_x, dg_pre_mix_p = _mm("d_proj_in", "nt", (nh, 1, 1),
                               a_ins=[piece(dq, th, "i"), piece(dk, th, "i"), piece(dv, th, "i"),
                                      piece(dconv, th, "i"), piece(dgate, th, "i")], a_fn=join_bf16,
                               b_ins=[full2(w_in_nat)], b_fn=_ident,
                               epi_ins=[rows_i(x, th), rows_i(dx1, th), vec(g_pre_mix)], epi_fn=in_norm_bwd,
                               outs=[(_sds((seq, D), F32), _bs((th, D), lambda i, j, k: (i, 0))), part(nh)],
                               acc_shape=(th, D))

    chip_major = lambda a: a.reshape(a.shape[0], N_CHIPS, a.shape[1] // N_CHIPS).transpose(1, 0, 2)
    big = [chip_major(dw_in), chip_major(dw_ao), chip_major(dw_co), dw_o.reshape(N_CHIPS, D // N_CHIPS, D), dw_up,
           dw_down.reshape(N_CHIPS, D_FF // N_CHIPS, D), dw_pg.reshape(N_CHIPS, D // N_CHIPS, D), chip_major(dw_pp)]
    small = (loss_p, [dg_pre_mix_p, dg_post_mix_p, dg_pre_mlp_p, dg_post_mlp_p, dg_ple_p], dbg_a_p, dbg_c_p, dwc_p)
    return grad_x, big, small


RS_GROUPS = ((0,), (4,), (5,), (1, 2, 3, 6, 7))


def _reduce_scatter(big):
    pair = [None] * len(big)
    for gi, group in enumerate(RS_GROUPS):
        for w, s in zip(group, _rs_pair_sum(f"rs_pair_sum_{gi}", [big[w] for w in group])):
            pair[w] = s
    return _rs_exchange_join(pair)


def kernel(x, p, g_pre_mix, w_in, b_gate, w_conv, w_attn_out, w_conv_out, w_o, g_post_mix, g_pre_mlp, w_up, w_down, g_post_mlp, g_ple, w_ple_gate, w_ple_proj, loss_target, m_g_pre_mix, m_w_in, m_b_gate, m_w_conv, m_w_attn_out, m_w_conv_out, m_w_o, m_g_post_mix, m_g_pre_mlp, m_w_up, m_w_down, m_g_post_mlp, m_g_ple, m_w_ple_gate, m_w_ple_proj, v_g_pre_mix, v_w_in, v_b_gate, v_w_conv, v_w_attn_out, v_w_conv_out, v_w_o, v_g_post_mix, v_g_pre_mlp, v_w_up, v_w_down, v_g_post_mlp, v_g_ple, v_w_ple_gate, v_w_ple_proj):
    mats = [w_in, w_attn_out, w_conv_out, w_o, w_up, w_down, w_ple_gate, w_ple_proj]
    mats_m = [m_w_in, m_w_attn_out, m_w_conv_out, m_w_o, m_w_up, m_w_down, m_w_ple_gate, m_w_ple_proj]
    mats_v = [v_w_in, v_w_attn_out, v_w_conv_out, v_w_o, v_w_up, v_w_down, v_w_ple_gate, v_w_ple_proj]
    gains = [g_pre_mix, g_post_mix, g_pre_mlp, g_post_mlp, g_ple]
    gains_m = [m_g_pre_mix, m_g_post_mix, m_g_pre_mlp, m_g_post_mlp, m_g_ple]
    gains_v = [v_g_pre_mix, v_g_post_mix, v_g_pre_mlp, v_g_post_mlp, v_g_ple]

    taps = jnp.concatenate([w_conv[0], jnp.zeros((CONV_PAD_ROWS - 3, LANES), F32)], axis=0)
    gathered = _allgather_weights([w[0].astype(BF16) for w in mats] + [taps])
    cols_joined = lambda a: a.transpose(1, 0, 2).reshape(a.shape[1], N_CHIPS * a.shape[2])
    rows_joined = lambda a: a.reshape(N_CHIPS * a.shape[1], a.shape[2])
    wf = [gathered[0], cols_joined(gathered[1]), cols_joined(gathered[2]), rows_joined(gathered[3]), gathered[4],
          rows_joined(gathered[5]), rows_joined(gathered[6]), cols_joined(gathered[7]),
          cols_joined(gathered[0]), cols_joined(gathered[4])]
    w_conv_full = cols_joined(gathered[8])[0:3, :]
    chip = 2 * lax.axis_index("x") + lax.axis_index("y")

    grad_x, big, small = _local_step(x[0], p[0, 0], loss_target[0], gains, b_gate, w_conv_full, wf)

    shard_grads = _reduce_scatter(big)
    red = _small_allreduce(*small)
    loss = red[0, 0]
    grad_gains = [red[1 + r:2 + r, :] for r in range(5)]
    grad_b_gate = jnp.concatenate([red[6:7, :], red[7:8, :]], axis=1)
    grad_w_conv = lax.dynamic_slice(red[8:11, :], (0, chip * LANES), (3, LANES))[None]

    grads_big = [gr.reshape(w.shape) for gr, w in zip(shard_grads, mats)]
    upd_big = [_adamw(f"adamw_{i}", w, gr, m, v) for i, (w, gr, m, v) in enumerate(zip(mats, grads_big, mats_m, mats_v))]
    pack = lambda vs, bg: jnp.concatenate(list(vs) + [bg.reshape(2, D_MODEL), jnp.zeros((1, D_MODEL), F32)], axis=0)
    upd_small = _adamw("adamw_small", pack(gains, b_gate), pack(grad_gains, grad_b_gate),
                       pack(gains_m, m_b_gate), pack(gains_v, v_b_gate))
    upd_conv = _adamw("adamw_conv", w_conv, grad_w_conv, m_w_conv, v_w_conv)

    def small_out(a, which):
        gains_out = [a[r:r + 1, :] for r in range(5)]
        return gains_out, a[5:7, :].reshape(1, 2 * D_MODEL)

    def ordered(g_pre_mix_, big_, b_gate_, conv_, g_rest):
        return [g_pre_mix_, big_[0], b_gate_, conv_, big_[1], big_[2], big_[3], g_rest[0], g_rest[1], big_[4], big_[5],
                g_rest[2], g_rest[3], big_[6], big_[7]]

    outs = [loss, grad_x[None]]
    outs += ordered(grad_gains[0], grads_big, grad_b_gate, grad_w_conv, grad_gains[1:])
    for which in range(3):
        g_out, b_out = small_out(upd_small[which], which)
        outs += ordered(g_out[0], [u[which] for u in upd_big], b_out, upd_conv[which], g_out[1:])
    return tuple(outs)
```

```python
import jax
import jax.numpy as jnp
from jax import lax
from jax.experimental import pallas as pl
from jax.experimental.pallas import tpu as pltpu

F32 = jnp.float32
BF16 = jnp.bfloat16
MESH = pl.DeviceIdType.MESH

D_MODEL = 1024
N_HEADS = 8
HEAD_DIM = 64
ATTN_W = N_HEADS * HEAD_DIM
CONV_W = 512
D_FF = 4096
PLE_DIM = 256
D_IN = 5120
N_CHIPS = 4
EPS = 1e-6
Q_SCALE = HEAD_DIM ** -0.5

ADAM_LR = 0.001
ADAM_B1 = 0.9
ADAM_B2 = 0.999
ADAM_EPS = 1e-08
ADAM_WD = 0.01
ADAM_STEP = 10

V7X_VMEM_BYTES = 64 * 1024 * 1024
VMEM_LIMIT = V7X_VMEM_BYTES - 8 * 1024 * 1024
LANES = 128
ATT_BLK = 256
SMALL_ROWS = 16
CONV_PAD_ROWS = 16


def _cparams(n_grid):
    return pltpu.CompilerParams(dimension_semantics=("arbitrary",) * n_grid, vmem_limit_bytes=VMEM_LIMIT)


def _bs(shape, fn):
    return pl.BlockSpec(shape, fn)


def _rms_stats(xf):
    return lax.rsqrt(jnp.mean(xf * xf, axis=-1, keepdims=True) + EPS)


def _rms(xf, g):
    return xf * _rms_stats(xf) * g


def _rms_bwd(xf, g, dy):
    r = _rms_stats(xf)
    xh = xf * r
    dyg = dy * g
    dx = r * (dyg - xh * jnp.mean(dyg * xh, axis=-1, keepdims=True))
    return dx, jnp.sum(dy * xh, axis=0, keepdims=True)


def _sig(z):
    return 1.0 / (1.0 + jnp.exp(-z))


def _ident(a):
    return a


def _to_bf16(a):
    return a.astype(BF16)


_DIMS = {"nn": (((1,), (0,)), ((), ())), "nt": (((1,), (1,)), ((), ())), "tn": (((0,), (0,)), ((), ()))}


def _mm(name, mode, grid, a_ins, a_fn, b_ins, b_fn, outs, acc_shape, epi_ins=(), epi_fn=None,
        a_cache=None, a_outs=(), epi_a=()):
    nk = grid[2]
    na, nb, ne, no, nao = len(a_ins), len(b_ins), len(epi_ins), len(outs), len(a_outs)
    assert a_cache is None or nk == 1
    assert not a_outs or a_cache is not None
    dims = _DIMS[mode]
    if epi_fn is None:
        epi_fn = lambda acc: (acc,)

    def body(*refs):
        a_refs = refs[:na]
        b_refs = refs[na:na + nb]
        e_refs = refs[na + nb:na + nb + ne]
        o_refs = refs[na + nb + ne:na + nb + ne + no]
        ao_refs = refs[na + nb + ne + no:na + nb + ne + no + nao]
        scratch = list(refs[na + nb + ne + no + nao:])
        acc_ref = scratch.pop(0) if nk > 1 else None
        a_sc = scratch.pop(0) if a_cache is not None else None
        j = pl.program_id(1)
        k = pl.program_id(2)

        def finish(acc):
            res = epi_fn(acc, *[a_refs[t][...] for t in epi_a], *[r[...] for r in e_refs])
            for r, val in zip(o_refs, res):
                r[...] = val.astype(r.dtype)

        if a_sc is not None:
            @pl.when(j == 0)
            def _():
                res = a_fn(*[r[...] for r in a_refs])
                if nao:
                    for r, val in zip(ao_refs, res[1:]):
                        r[...] = val.astype(r.dtype)
                    res = res[0]
                a_sc[...] = res
            a = a_sc[...]
        else:
            a = a_fn(*[r[...] for r in a_refs])
        b = b_fn(*[r[...] for r in b_refs])
        prod = lax.dot_general(a, b, dims, preferred_element_type=F32)
        if nk == 1:
            finish(prod)
        else:
            @pl.when(k == 0)
            def _():
                acc_ref[...] = prod

            @pl.when(k > 0)
            def _():
                acc_ref[...] += prod

            @pl.when(k == nk - 1)
            def _():
                finish(acc_ref[...])

    scratch_shapes = []
    if nk > 1:
        scratch_shapes.append(pltpu.VMEM(acc_shape, F32))
    if a_cache is not None:
        scratch_shapes.append(pltpu.VMEM(*a_cache))
    all_outs = list(outs) + list(a_outs)
    res = pl.pallas_call(
        body, name=name, grid=grid,
        in_specs=[s for _, s in a_ins] + [s for _, s in b_ins] + [s for _, s in epi_ins],
        out_specs=[s for _, s in all_outs],
        out_shape=[o for o, _ in all_outs],
        scratch_shapes=scratch_shapes,
        compiler_params=_cparams(3),
    )(*[a for a, _ in a_ins], *[a for a, _ in b_ins], *[a for a, _ in epi_ins])
    return res


def _sds(shape, dtype):
    return jax.ShapeDtypeStruct(shape, dtype)


def _nt(a, b):
    return lax.dot_general(a, b, _DIMS["nt"], preferred_element_type=F32)


def _tn(a, b):
    return lax.dot_general(a, b, _DIMS["tn"], preferred_element_type=F32)


def _nn(a, b):
    return lax.dot_general(a, b, _DIMS["nn"], preferred_element_type=F32)


def _mlp_down_ple_head(up, x1, p, tgt, g_ple, g_post_mlp, w_down, w_pg, w_pp, seq, tr):
    nblk = seq // tr
    D = D_MODEL

    def body(up_ref, x1_ref, p_ref, t_ref, gp_ref, gm_ref, wd_ref, wpg_ref, wpp_ref,
             dx2_ref, df_ref, dpre_ref, h3_ref, dpp_ref, loss_ref, dgp_ref, dgm_ref):
        gp, gm, wpg, wpp = gp_ref[...], gm_ref[...], wpg_ref[...], wpp_ref[...]
        halves = [pl.ds(0, tr // 2), pl.ds(tr // 2, tr // 2)]
        w_down = wd_ref[...]
        fb = []
        for r in halves:
            hidden = jnp.maximum(up_ref[r, :].astype(F32), 0.0)
            fb.append(_nn((hidden * hidden).astype(BF16), w_down))
        x2b = [x1_ref[r, :] + _rms(fb[s], gm) for s, r in enumerate(halves)]
        h3 = [_rms(x, gp).astype(BF16) for x in x2b]
        gate = [_sig(_nn(h, wpg)) for h in h3]
        pp = [_nn(p_ref[r, :].astype(BF16), wpp) for r in halves]
        err = [x2b[s] + gate[s] * pp[s] - t_ref[r, :] for s, r in enumerate(halves)]
        dx3 = [e * (1.0 / D) for e in err]
        dpre = [(dx3[s] * pp[s] * gate[s] * (1.0 - gate[s])).astype(BF16) for s in range(2)]
        dh3 = [_nt(d, wpg) for d in dpre]
        loss, dgp_sum, dgm_sum = 0.0, 0.0, 0.0
        for s, r in enumerate(halves):
            h3_ref[r, :] = h3[s]
            dpp_ref[r, :] = (dx3[s] * gate[s]).astype(BF16)
            dpre_ref[r, :] = dpre[s]
            dxn, dgp = _rms_bwd(x2b[s], gp, dh3[s])
            dx2 = dx3[s] + dxn
            dx2_ref[r, :] = dx2
            dfb, dgm = _rms_bwd(fb[s], gm, dx2)
            df_ref[r, :] = dfb.astype(BF16)
            loss = loss + jnp.sum(err[s] * err[s], axis=0, keepdims=True)
            dgp_sum, dgm_sum = dgp_sum + dgp, dgm_sum + dgm
        loss_ref[...] = loss * (0.5 / D)
        dgp_ref[...] = dgp_sum
        dgm_ref[...] = dgm_sum

    rows = _bs((tr, D), lambda i: (i, 0))
    vec = _bs((1, D), lambda i: (0, 0))
    part = _bs((None, 1, D), lambda i: (i, 0, 0))
    return pl.pallas_call(
        body, name="mlp_down_ple_head", grid=(nblk,),
        in_specs=[_bs((tr, D_FF), lambda i: (i, 0)), rows, _bs((tr, PLE_DIM), lambda i: (i, 0)), rows, vec, vec,
                  _bs((D_FF, D), lambda i: (0, 0)), _bs((D, D), lambda i: (0, 0)), _bs((PLE_DIM, D), lambda i: (0, 0))],
        out_specs=[rows] * 5 + [part] * 3,
        out_shape=[_sds((seq, D), F32)] + [_sds((seq, D), BF16)] * 4 + [_sds((nblk, 1, D), F32)] * 3,
        compiler_params=_cparams(1),
    )(up, x1, p, tgt, g_ple, g_post_mlp, w_down, w_pg, w_pp)


def _shift_rows_down(u, prev, n):
    rows = u.shape[0]
    ridx = lax.broadcasted_iota(jnp.int32, u.shape, 0)
    out = pltpu.roll(u, n, 0)
    for r in range(n):
        out = jnp.where(ridx == r, prev[8 - n + r:8 - n + r + 1, :], out)
    del rows
    return out


def _shift_rows_up(u, nxt, n):
    rows = u.shape[0]
    ridx = lax.broadcasted_iota(jnp.int32, u.shape, 0)
    out = pltpu.roll(u, rows - n, 0)
    for r in range(n):
        out = jnp.where(ridx == rows - n + r, nxt[r:r + 1, :], out)
    return out


CONV_COL0 = 3


def _conv_fwd(proj, w_conv, seq, tr):
    hb = tr // 8

    def body(cb_ref, cc_ref, cu_ref, ccp_ref, cup_ref, w_ref, e_ref):
        i = pl.program_id(0)
        u = cc_ref[...] * cu_ref[...]
        up = jnp.where(i > 0, ccp_ref[...] * cup_ref[...], 0.0)
        w = w_ref[...]
        d = w[0:1, :] * _shift_rows_down(u, up, 2) + w[1:2, :] * _shift_rows_down(u, up, 1) + w[2:3, :] * u
        e_ref[...] = (cb_ref[...] * d).astype(BF16)

    prev = lambda c: (lambda i: (jnp.maximum(i * hb - 1, 0), c))
    return pl.pallas_call(
        body, name="conv_fwd", grid=(seq // tr,),
        in_specs=[_bs((tr, CONV_W), lambda i: (i, CONV_COL0)),
                  _bs((tr, CONV_W), lambda i: (i, CONV_COL0 + 1)),
                  _bs((tr, CONV_W), lambda i: (i, CONV_COL0 + 2)),
                  _bs((8, CONV_W), prev(CONV_COL0 + 1)),
                  _bs((8, CONV_W), prev(CONV_COL0 + 2)),
                  _bs((3, CONV_W), lambda i: (0, 0))],
        out_specs=_bs((tr, CONV_W), lambda i: (i, 0)),
        out_shape=_sds((seq, CONV_W), BF16),
        compiler_params=_cparams(1),
    )(proj, proj, proj, proj, proj, w_conv)


def _conv_bwd(proj, de, w_conv, seq, tr):
    hb = tr // 8
    nblk = seq // tr

    def body(cb_ref, cc_ref, cu_ref, ccp_ref, cup_ref, cbn_ref, de_ref, den_ref, w_ref, o_ref, dw_ref):
        i = pl.program_id(0)
        cc, cu, cb = cc_ref[...], cu_ref[...], cb_ref[...]
        u = cc * cu
        up = jnp.where(i > 0, ccp_ref[...] * cup_ref[...], 0.0)
        u1 = _shift_rows_down(u, up, 1)
        u2 = _shift_rows_down(u, up, 2)
        de_ = de_ref[...]
        dd = de_ * cb
        ddn = jnp.where(i < nblk - 1, den_ref[...] * cbn_ref[...], 0.0)
        w = w_ref[...]
        du = w[2:3, :] * dd + w[1:2, :] * _shift_rows_up(dd, ddn, 1) + w[0:1, :] * _shift_rows_up(dd, ddn, 2)
        o_ref[:, 0:CONV_W] = (de_ * (w[0:1, :] * u2 + w[1:2, :] * u1 + w[2:3, :] * u)).astype(BF16)
        o_ref[:, CONV_W:2 * CONV_W] = (du * cu).astype(BF16)
        o_ref[:, 2 * CONV_W:3 * CONV_W] = (du * cc).astype(BF16)
        ridx = lax.broadcasted_iota(jnp.int32, (8, CONV_W), 0)
        dw0 = jnp.sum(dd * u2, axis=0, keepdims=True)
        dw1 = jnp.sum(dd * u1, axis=0, keepdims=True)
        dw2 = jnp.sum(dd * u, axis=0, keepdims=True)
        dw_ref[...] = jnp.where(ridx == 0, dw0, jnp.where(ridx == 1, dw1, jnp.where(ridx == 2, dw2, 0.0)))

    prev = lambda c: (lambda i: (jnp.maximum(i * hb - 1, 0), c))
    nxt = lambda c: (lambda i: (jnp.minimum((i + 1) * hb, seq // 8 - 1), c))
    return pl.pallas_call(
        body, name="conv_bwd", grid=(nblk,),
        in_specs=[_bs((tr, CONV_W), lambda i: (i, CONV_COL0)),
                  _bs((tr, CONV_W), lambda i: (i, CONV_COL0 + 1)),
                  _bs((tr, CONV_W), lambda i: (i, CONV_COL0 + 2)),
                  _bs((8, CONV_W), prev(CONV_COL0 + 1)),
                  _bs((8, CONV_W), prev(CONV_COL0 + 2)),
                  _bs((8, CONV_W), nxt(CONV_COL0)),
                  _bs((tr, CONV_W), lambda i: (i, 0)),
                  _bs((8, CONV_W), nxt(0)),
                  _bs((3, CONV_W), lambda i: (0, 0))],
        out_specs=[_bs((tr, 3 * CONV_W), lambda i: (i, 0)), _bs((None, 8, CONV_W), lambda i: (i, 0, 0))],
        out_shape=[_sds((seq, 3 * CONV_W), BF16), _sds((nblk, 8, CONV_W), F32)],
        compiler_params=_cparams(1),
    )(proj, proj, proj, proj, proj, proj, de, de, w_conv)


def _log_gates(z):
    lse = jnp.log(1.0 + jnp.exp(-jnp.abs(z)))
    log_beta = jnp.minimum(z, 0.0) - lse
    return log_beta, log_beta - z


DEAD_LOG_WEIGHT = -1e30
NO_TILE = -1e30


def _first_live_tile(start, scores, live_sc):
    def alive():
        return jnp.max(jnp.maximum(live_sc[0], live_sc[1])) > DEAD_LOG_WEIGHT

    def step(c):
        for h, z in enumerate(scores(c[0])):
            live_sc[h] = live_sc[h] + jnp.sum(_log_gates(z)[1], axis=-1, keepdims=True)
        return c[0] - 1, alive()

    j_end, _ = lax.while_loop(lambda c: jnp.logical_and(c[0] >= 0, c[1]), step, (start, alive()))
    return j_end + 1


def _attn_fwd(proj, seq):
    blk = ATT_BLK
    nq = seq // blk
    npair = N_HEADS // 2

    def body(q_ref, k_ref, v_ref, o_ref, z0_sc, z1_sc, w0_sc, w1_sc, tot_sc, acc_sc):
        i = pl.program_id(1)
        is_a = lax.broadcasted_iota(jnp.int32, (1, LANES), 1) < HEAD_DIM
        q2 = (q_ref[...] * Q_SCALE).astype(BF16)
        zero = jnp.zeros_like(q2)
        qs = (jnp.where(is_a, q2, zero), jnp.where(is_a, zero, q2))
        row = lax.broadcasted_iota(jnp.int32, (blk, blk), 0)
        col = lax.broadcasted_iota(jnp.int32, (blk, blk), 1)
        tri = (row > col).astype(BF16)
        causal = col < row

        def tile_of(ref, j):
            return ref[pl.ds(pl.multiple_of(j * blk, blk), blk), :].astype(BF16)

        def scores(j):
            k2 = tile_of(k_ref, j)
            return [_nt(qs[h], k2) for h in range(2)]

        has_left = i > 0
        left = jnp.maximum(i - 1, 0)

        g_d = [_log_gates(z) for z in scores(i)]
        g_l = [_log_gates(z) for z in scores(left)]
        keep_d = [jnp.where(causal, g[1], 0.0) for g in g_d]
        suf_d = [_nn(lk.astype(BF16), tri) for lk in keep_d]
        suf_l = [_nn(g[1].astype(BF16), tri) for g in g_l]
        v_d, v_l = tile_of(v_ref, i), tile_of(v_ref, left)
        pv = []
        for h in range(2):
            sum_d = jnp.sum(keep_d[h], axis=-1, keepdims=True)
            w_d = jnp.where(causal, jnp.exp(g_d[h][0] + suf_d[h]), 0.0)
            w_l = jnp.exp(g_l[h][0] + (jnp.where(has_left, sum_d, NO_TILE) + suf_l[h]))
            pv.append(_nn(w_d.astype(BF16), v_d) + _nn(w_l.astype(BF16), v_l))
            tot_sc[h] = sum_d + jnp.sum(g_l[h][1], axis=-1, keepdims=True)
        acc_sc[...] = jnp.where(is_a, pv[0], pv[1])

        z_bufs, w_bufs = (z0_sc, z1_sc), (w0_sc, w1_sc)

        def alive():
            return jnp.max(jnp.maximum(tot_sc[0], tot_sc[1])) > DEAD_LOG_WEIGHT

        def put(ref, vals):
            for h in range(2):
                ref[h] = vals[h]

        def weights(zs):
            gates = [_log_gates(z) for z in zs]
            sums = [_nn(g[1].astype(BF16), tri) for g in gates]
            ws = []
            for h in range(2):
                ws.append(jnp.exp(gates[h][0] + (tot_sc[h] + sums[h])).astype(BF16))
                tot_sc[h] = tot_sc[h] + jnp.sum(gates[h][1], axis=-1, keepdims=True)
            return ws

        def add_values(w_buf, j):
            v2 = tile_of(v_ref, j)
            acc_sc[...] += jnp.where(is_a, _nn(w_buf[0], v2), _nn(w_buf[1], v2))

        def trip(j, s):
            add_values(w_bufs[s], j + 1)
            put(z_bufs[1 - s], scores(jnp.maximum(j - 1, 0)))
            put(w_bufs[1 - s], weights((z_bufs[s][0], z_bufs[s][1])))

        @pl.when(jnp.logical_and(i >= 2, alive()))
        def _():
            put(z0_sc, scores(i - 2))
            w0_sc[...] = jnp.zeros_like(w0_sc)

            def two_trips(c):
                trip(c[0], 0)
                trip(c[0] - 1, 1)
                return c[0] - 2, alive()

            j_next, still = lax.while_loop(lambda c: jnp.logical_and(c[0] >= 1, c[1]), two_trips, (i - 2, i >= 2))
            one_left = jnp.logical_and(j_next == 0, still)

            @pl.when(one_left)
            def _():
                trip(0, 0)
                add_values(w1_sc, 0)

            @pl.when(jnp.logical_not(one_left))
            def _():
                add_values(w0_sc, j_next + 1)

        o_ref[...] = acc_sc[...].astype(BF16)

    return pl.pallas_call(
        body, name="attn_fwd", grid=(npair, nq),
        in_specs=[_bs((blk, LANES), lambda p, i: (i, p)),
                  _bs((seq, LANES), lambda p, i: (0, npair + p)),
                  _bs((seq, LANES), lambda p, i: (0, 2 * npair + p))],
        out_specs=_bs((blk, LANES), lambda p, i: (i, p)),
        out_shape=_sds((seq, ATTN_W), BF16),
        scratch_shapes=[pltpu.VMEM((2, blk, blk), F32), pltpu.VMEM((2, blk, blk), F32),
                        pltpu.VMEM((2, blk, blk), BF16), pltpu.VMEM((2, blk, blk), BF16),
                        pltpu.VMEM((2, blk, 1), F32), pltpu.VMEM((blk, LANES), F32)],
        compiler_params=_cparams(2),
    )(proj, proj, proj)


def _attn_bwd(proj, do, seq):
    blk = ATT_BLK
    nq = seq // blk
    npair = N_HEADS // 2

    def body(q_ref, k_ref, v_ref, do_ref, dq_ref, dk_ref, dv_ref,
             prod0_sc, prod1_sc, pend0_sc, pend1_sc, tot_sc, live_sc, cum_sc, pre_sc, dq_sc):
        i = pl.program_id(1)

        @pl.when(i == 0)
        def _():
            dk_ref[...] = jnp.zeros_like(dk_ref)
            dv_ref[...] = jnp.zeros_like(dv_ref)

        is_a = lax.broadcasted_iota(jnp.int32, (1, LANES), 1) < HEAD_DIM
        q2 = (q_ref[...] * Q_SCALE).astype(BF16)
        do2 = do_ref[...]
        zero = jnp.zeros_like(q2)
        qs = (jnp.where(is_a, q2, zero), jnp.where(is_a, zero, q2))
        dos = (jnp.where(is_a, do2, zero), jnp.where(is_a, zero, do2))
        row = lax.broadcasted_iota(jnp.int32, (blk, blk), 0)
        col = lax.broadcasted_iota(jnp.int32, (blk, blk), 1)
        tri_after = (row > col).astype(BF16)
        tri_excl = (row < col).astype(BF16)
        causal = col < row

        def tile_of(ref, j):
            return ref[pl.ds(pl.multiple_of(j * blk, blk), blk), :].astype(BF16)

        def scores(j):
            k2 = tile_of(k_ref, j)
            return [_nt(qs[h], k2) for h in range(2)]

        def products(j):
            v2 = tile_of(v_ref, j)
            return scores(j) + [_nt(dos[h], v2) for h in range(2)]

        def row_sum(a):
            return jnp.sum(a, axis=-1, keepdims=True)

        def grad_matmuls(ws, dzs, j):
            rows = pl.ds(pl.multiple_of(j * blk, blk), blk)
            k2 = tile_of(k_ref, j)
            dq_sc[...] += jnp.where(is_a, _nn(dzs[0], k2), _nn(dzs[1], k2))
            dk_ref[rows, :] += jnp.where(is_a, _tn(dzs[0], q2), _tn(dzs[1], q2))
            if ws is not None:
                dv_ref[rows, :] += jnp.where(is_a, _tn(ws[0], do2), _tn(ws[1], do2))

        has_left = i > 0
        left = jnp.maximum(i - 1, 0)

        p_d, p_l = products(i), products(left)
        g_d = [_log_gates(z) for z in p_d[:2]]
        g_l = [_log_gates(z) for z in p_l[:2]]
        keep_d = [jnp.where(causal, g[1], 0.0) for g in g_d]
        suf_d = [_nn(lk.astype(BF16), tri_after) for lk in keep_d]
        suf_l = [_nn(g[1].astype(BF16), tri_after) for g in g_l]
        w_d, w_l, gg_d, gg_l = [], [], [], []
        for h in range(2):
            sum_d = row_sum(keep_d[h])
            w_d.append(jnp.where(causal, jnp.exp(g_d[h][0] + suf_d[h]), 0.0))
            w_l.append(jnp.exp(g_l[h][0] + (jnp.where(has_left, sum_d, NO_TILE) + suf_l[h])))
            gg_d.append(p_d[2 + h] * w_d[h])
            gg_l.append(p_l[2 + h] * w_l[h])
            tot_sc[h] = sum_d + row_sum(g_l[h][1])
        before_d = [_nn(g.astype(BF16), tri_excl) for g in gg_d]
        before_l = [_nn(g.astype(BF16), tri_excl) for g in gg_l]
        dz_d, dz_l = [], []
        for h in range(2):
            beta_d, beta_l = jnp.exp(g_d[h][0]), jnp.exp(g_l[h][0])
            dz_l.append((gg_l[h] * (1.0 - beta_l) - before_l[h] * beta_l).astype(BF16))
            dz = gg_d[h] * (1.0 - beta_d) - (row_sum(gg_l[h]) + before_d[h]) * beta_d
            dz_d.append(jnp.where(causal, dz, 0.0).astype(BF16))
        dq_sc[...] = jnp.zeros_like(dq_sc)
        grad_matmuls([w.astype(BF16) for w in w_l], dz_l, left)
        grad_matmuls([w.astype(BF16) for w in w_d], dz_d, i)

        live_sc[...] = tot_sc[...]
        first = _first_live_tile(i - 2, scores, live_sc)
        trips = i - 1 - first
        prod_bufs, pend_bufs = (prod0_sc, prod1_sc), (pend0_sc, pend1_sc)

        def local_grads(prods):
            zs, dws = prods[:2], prods[2:]
            gates = [_log_gates(z) for z in zs]
            sums = [_nn(g[1].astype(BF16), tri_after) for g in gates]
            ws, gs = [], []
            for h in range(2):
                cum = cum_sc[h] + row_sum(gates[h][1])
                cum_sc[h] = cum
                ws.append(jnp.exp(gates[h][0] + ((live_sc[h] - cum) + sums[h])))
                gs.append(dws[h] * ws[h])
            befores = [_nn(g.astype(BF16), tri_excl) for g in gs]
            dzs = []
            for h in range(2):
                beta = jnp.exp(gates[h][0])
                dzs.append((gs[h] * (1.0 - beta) - (pre_sc[h] + befores[h]) * beta).astype(BF16))
                pre_sc[h] = pre_sc[h] + row_sum(gs[h])
            return [w.astype(BF16) for w in ws] + dzs

        def put(ref, vals):
            for n, val in enumerate(vals):
                ref[n] = val

        def flush(pend, j):
            grad_matmuls([pend[0], pend[1]], [pend[2], pend[3]], j)

        def trip(j, s):
            flush(pend_bufs[s], jnp.maximum(j - 1, first))
            put(prod_bufs[1 - s], products(j + 1))
            put(pend_bufs[1 - s], local_grads([prod_bufs[s][n] for n in range(4)]))

        def earlier_keys_share(j, mask):
            dzs = []
            for h, z in enumerate(scores(j)):
                beta = jnp.exp(_log_gates(z)[0])
                dzs.append(jnp.where(mask, -pre_sc[h] * beta, 0.0).astype(BF16))
            grad_matmuls(None, dzs, j)

        @pl.when(trips > 0)
        def _():
            cum_sc[...] = jnp.zeros_like(cum_sc)
            pre_sc[...] = jnp.zeros_like(pre_sc)
            pend0_sc[...] = jnp.zeros_like(pend0_sc)
            put(prod0_sc, products(first))

            def two_trips(pp, carry):
                trip(first + 2 * pp, 0)
                trip(first + 2 * pp + 1, 1)
                return carry

            lax.fori_loop(0, trips // 2, two_trips, 0)
            odd = trips % 2 == 1

            @pl.when(odd)
            def _():
                trip(i - 2, 0)
                flush(pend1_sc, i - 2)

            @pl.when(jnp.logical_not(odd))
            def _():
                flush(pend0_sc, i - 2)

            earlier_keys_share(i - 1, True)
            earlier_keys_share(i, causal)

        dq_ref[...] = dq_sc[...] * Q_SCALE

    qmap = lambda p, i: (i, p)
    return pl.pallas_call(
        body, name="attn_bwd", grid=(npair, nq),
        in_specs=[_bs((blk, LANES), qmap),
                  _bs((seq, LANES), lambda p, i: (0, npair + p)),
                  _bs((seq, LANES), lambda p, i: (0, 2 * npair + p)),
                  _bs((blk, LANES), qmap)],
        out_specs=[_bs((blk, LANES), qmap),
                   _bs((seq, LANES), lambda p, i: (0, p)),
                   _bs((seq, LANES), lambda p, i: (0, p))],
        out_shape=[_sds((seq, ATTN_W), F32)] * 3,
        scratch_shapes=[pltpu.VMEM((4, blk, blk), F32), pltpu.VMEM((4, blk, blk), F32),
                        pltpu.VMEM((4, blk, blk), BF16), pltpu.VMEM((4, blk, blk), BF16),
                        pltpu.VMEM((2, blk, 1), F32), pltpu.VMEM((2, blk, 1), F32), pltpu.VMEM((2, blk, 1), F32),
                        pltpu.VMEM((2, blk, 1), F32), pltpu.VMEM((blk, LANES), F32)],
        compiler_params=_cparams(2),
    )(proj, proj, proj, do)


def _elementwise(name, fn, ins, out_dtypes):
    rows, cols = ins[0].shape
    tr = rows
    for cand in (512, 256, 128, 64, 32, 16, 8):
        if rows % cand == 0 and cand * cols * 4 <= 2 * 1024 * 1024:
            tr = cand
            break
    n_in = len(ins)

    def body(*refs):
        res = fn(*[r[...] for r in refs[:n_in]])
        for r, val in zip(refs[n_in:], res):
            r[...] = val.astype(r.dtype)

    spec = _bs((tr, cols), lambda i: (i, 0))
    return pl.pallas_call(
        body, name=name, grid=(rows // tr,),
        in_specs=[spec] * n_in, out_specs=[spec] * len(out_dtypes),
        out_shape=[_sds((rows, cols), dt) for dt in out_dtypes],
        compiler_params=_cparams(1),
    )(*ins)


def _adamw_fn(w, g, m, v):
    m = ADAM_B1 * m + (1.0 - ADAM_B1) * g
    v = ADAM_B2 * v + (1.0 - ADAM_B2) * (g * g)
    m_hat = m / (1.0 - ADAM_B1 ** ADAM_STEP)
    v_hat = v / (1.0 - ADAM_B2 ** ADAM_STEP)
    delta = -ADAM_LR * (m_hat / (jnp.sqrt(v_hat) + ADAM_EPS) + ADAM_WD * w)
    return delta, m, v


def _adamw(name, w, g, m, v):
    shape = w.shape
    as2d = lambda a: a.reshape(-1, shape[-1])
    delta, nm, nv = _elementwise(name, _adamw_fn, [as2d(w), as2d(g), as2d(m), as2d(v)], [F32, F32, F32])
    return delta.reshape(shape), nm.reshape(shape), nv.reshape(shape)


def _place():
    return lax.axis_index("x"), lax.axis_index("y"), lax.axis_index("c")


ANY = pl.BlockSpec(memory_space=pl.ANY)
VMEM_WHOLE = pl.BlockSpec(memory_space=pltpu.VMEM)


def _allgather_weights(shards):
    n = len(shards)

    def body(*refs):
        src, dst = refs[:n], refs[n:2 * n]
        send_sems, recv_sems, local_sems = refs[2 * n:]
        x, y, c = _place()
        me, sibling, mychip = (x, y, c), (x, y, 1 - c), 2 * x + y

        x_nbr, y_nbr, diag = 2 * (1 - x) + y, 2 * x + (1 - y), 2 * (1 - x) + (1 - y)
        to_x, to_y = (1 - x, y, c), (x, 1 - y, c)

        def parts(w):
            hr = src[w].shape[0] // 2
            first = hr // 2 if hr % 32 == 0 else hr
            return first, hr - first

        def rows_of(w, chip, half, route):
            hr = src[w].shape[0] // 2
            first, second = parts(w)
            start, size = {0: (0, hr), 1: (0, hr), 2: (0, first), 3: (first, second)}[route]
            return dst[w].at[chip, pl.ds(half * hr + start, size)]

        def copy(w, k, src_ref, dst_ref, to):
            return pltpu.make_async_remote_copy(src_ref=src_ref, dst_ref=dst_ref, send_sem=send_sems.at[w, k],
                                                recv_sem=recv_sems.at[w, k], device_id=to, device_id_type=MESH)

        def landed(w, route):
            chip = {0: x_nbr, 1: y_nbr, 2: diag, 3: diag}[route]
            return rows_of(w, chip, c, route), chip

        def routes(w):
            return (0, 1, 2, 3) if parts(w)[1] else (0, 1, 2)

        started, local = [], []
        for w in range(n):
            hr = src[w].shape[0] // 2
            own = pltpu.make_async_copy(src[w], dst[w].at[mychip], local_sems.at[w])
            own.start()
            local.append(own)
            mine = src[w].at[pl.ds(c * hr, hr)]
            for route, to in ((0, to_x), (1, to_y)):
                cp = copy(w, route, mine, rows_of(w, mychip, c, route), to)
                cp.start()
                started.append(cp)

        def pass_on(w, route):
            got, chip = landed(w, route)
            copy(w, route, got, got, me).wait_recv()
            if route == 1:
                part = rows_of(w, chip, c, 2)
                started.append(copy(w, 2, part, part, to_x))
                started[-1].start()
            if route == 0 and parts(w)[1]:
                part = rows_of(w, chip, c, 3)
                started.append(copy(w, 3, part, part, to_y))
                started[-1].start()
            started.append(copy(w, 4 + route, got, got, sibling))
            started[-1].start()

        for w in range(n):
            pass_on(w, 1)
            pass_on(w, 0)
        for w in range(n):
            for route in routes(w)[2:]:
                pass_on(w, route)
        for w in range(n):
            for route in routes(w):
                chip = landed(w, route)[1]
                from_sib = rows_of(w, chip, 1 - c, route)
                copy(w, 4 + route, from_sib, from_sib, me).wait_recv()
        for cp in local:
            cp.wait()
        for cp in started:
            cp.wait_send()

    return pl.pallas_call(
        body, name="allgather_weights",
        in_specs=[VMEM_WHOLE] * n, out_specs=[VMEM_WHOLE] * n,
        out_shape=[_sds((N_CHIPS,) + s.shape, s.dtype) for s in shards],
        scratch_shapes=[pltpu.SemaphoreType.DMA((n, 8)), pltpu.SemaphoreType.DMA((n, 8)),
                        pltpu.SemaphoreType.DMA((n,))],
        compiler_params=pltpu.CompilerParams(vmem_limit_bytes=VMEM_LIMIT),
    )(*shards)


SUM_ROWS = 64


def _rs_pair_sum(name, grads):
    n = len(grads)

    def body(*refs):
        g, out = refs[:n], refs[n:2 * n]
        stage, land, keep = refs[2 * n:3 * n], refs[3 * n:4 * n], refs[4 * n:5 * n]
        send_sems, recv_sems, stage_sems, keep_sems = refs[5 * n:]
        x, y, c = _place()
        sibling = (x, y, 1 - c)
        loads = []
        for w in range(n):
            hr = g[w].shape[1] // 2
            st = pltpu.make_async_copy(g[w].at[:, pl.ds((1 - c) * hr, hr)], stage[w], stage_sems.at[w])
            kp = pltpu.make_async_copy(g[w].at[:, pl.ds(c * hr, hr)], keep[w], keep_sems.at[w])
            st.start()
            kp.start()
            loads.append((st, kp))
        gives = []
        for w in range(n):
            loads[w][0].wait()
            give = pltpu.make_async_remote_copy(src_ref=stage[w], dst_ref=land[w], send_sem=send_sems.at[w],
                                                recv_sem=recv_sems.at[w], device_id=sibling, device_id_type=MESH)
            give.start()
            gives.append(give)
        for w in range(n):
            loads[w][1].wait()
            gives[w].wait_recv()
            nb = g[w].shape[1] // 2 // SUM_ROWS

            def add(idx, carry, w=w, nb=nb):
                k, r = idx // nb, pl.multiple_of((idx % nb) * SUM_ROWS, SUM_ROWS)
                rows = pl.ds(r, SUM_ROWS)
                out[w][k, rows, :] = (keep[w][k, rows, :] + land[w][k, rows, :]).astype(BF16)
                return carry

            lax.fori_loop(0, N_CHIPS * nb, add, 0)
        for give in gives:
            give.wait_send()

    half = [(N_CHIPS, a.shape[1] // 2, a.shape[2]) for a in grads]
    bufs = [pltpu.VMEM(s, F32) for s in half]
    sems = pltpu.SemaphoreType.DMA((n,))
    return pl.pallas_call(
        body, name=name,
        in_specs=[ANY] * n, out_specs=[VMEM_WHOLE] * n, out_shape=[_sds(s, BF16) for s in half],
        scratch_shapes=bufs + bufs + bufs + [sems, sems, sems, sems],
        compiler_params=pltpu.CompilerParams(vmem_limit_bytes=VMEM_LIMIT),
    )(*grads)


def _rs_exchange_join(parts):
    n = len(parts)

    def body(*refs):
        t, full = refs[:n], refs[n:2 * n]
        got_x, got_y, pass_on, got_2 = (refs[m * n:(m + 1) * n] for m in range(2, 6))
        send_sems, recv_sems = refs[6 * n:]
        x, y, c = _place()
        mychip, sibling = 2 * x + y, (x, y, 1 - c)
        x_nbr, y_nbr, diag = 2 * (1 - x) + y, 2 * x + (1 - y), 2 * (1 - x) + (1 - y)
        to_x, to_y = (1 - x, y, c), (x, 1 - y, c)
        sends = []

        def copy(w, k, src_ref, dst_ref, to):
            return pltpu.make_async_remote_copy(src_ref=src_ref, dst_ref=dst_ref, send_sem=send_sems.at[w, k],
                                                recv_sem=recv_sems.at[w, k], device_id=to, device_id_type=MESH)

        def start(cp):
            cp.start()
            sends.append(cp)

        def add_rows(w, count, fn):
            def step(idx, carry):
                fn(pl.ds(pl.multiple_of(idx * SUM_ROWS, SUM_ROWS), SUM_ROWS), pl.multiple_of(idx * SUM_ROWS, SUM_ROWS))
                return carry
            lax.fori_loop(0, count // SUM_ROWS, step, 0)

        f32 = lambda v: v.astype(F32)
        for w in range(n):
            ha = t[w].shape[1] // 2
            part_a, part_b = pl.ds(0, ha), pl.ds(ha, ha)
            start(copy(w, 0, t[w].at[x_nbr, part_a], got_x[w].at[0], to_x))
            start(copy(w, 1, t[w].at[diag, part_a], got_x[w].at[1], to_x))
            start(copy(w, 2, t[w].at[y_nbr, part_b], got_y[w].at[0], to_y))
            start(copy(w, 3, t[w].at[diag, part_b], got_y[w].at[1], to_y))
        for w in range(n):
            hr = t[w].shape[1]
            ha = hr // 2
            for k in (0, 1):
                copy(w, k, got_x[w].at[k], got_x[w].at[k], to_x).wait_recv()

            def sum_a(rows, r, w=w, hr=hr):
                full[w][pl.ds(pl.multiple_of(c * hr + r, SUM_ROWS), SUM_ROWS), :] = \
                    f32(t[w][mychip, rows, :]) + f32(got_x[w][0, rows, :])
                pass_on[w][rows, :] = (f32(t[w][y_nbr, rows, :]) + f32(got_x[w][1, rows, :])).astype(BF16)

            add_rows(w, ha, sum_a)
            start(copy(w, 4, pass_on[w].at[pl.ds(0, ha)], got_2[w].at[pl.ds(0, ha)], to_y))
            for k in (2, 3):
                copy(w, k, got_y[w].at[k - 2], got_y[w].at[k - 2], to_y).wait_recv()

            def sum_b(rows, r, w=w, hr=hr, ha=ha):
                lower = pl.ds(pl.multiple_of(ha + r, SUM_ROWS), SUM_ROWS)
                full[w][pl.ds(pl.multiple_of(c * hr + ha + r, SUM_ROWS), SUM_ROWS), :] = \
                    f32(t[w][mychip, lower, :]) + f32(got_y[w][0, rows, :])
                pass_on[w][lower, :] = (f32(t[w][x_nbr, lower, :]) + f32(got_y[w][1, rows, :])).astype(BF16)

            add_rows(w, ha, sum_b)
            start(copy(w, 5, pass_on[w].at[pl.ds(ha, ha)], got_2[w].at[pl.ds(ha, ha)], to_x))
        for w in range(n):
            hr = t[w].shape[1]
            ha = hr // 2
            copy(w, 4, got_2[w].at[pl.ds(0, ha)], got_2[w].at[pl.ds(0, ha)], to_y).wait_recv()
            copy(w, 5, got_2[w].at[pl.ds(ha, ha)], got_2[w].at[pl.ds(ha, ha)], to_x).wait_recv()

            def finish(rows, r, w=w, hr=hr):
                out_rows = pl.ds(pl.multiple_of(c * hr + r, SUM_ROWS), SUM_ROWS)
                full[w][out_rows, :] = full[w][out_rows, :] + f32(got_2[w][rows, :])

            add_rows(w, hr, finish)
            mine = full[w].at[pl.ds(c * hr, hr)]
            start(copy(w, 6, mine, mine, sibling))
        for w in range(n):
            hr = t[w].shape[1]
            theirs = full[w].at[pl.ds((1 - c) * hr, hr)]
            copy(w, 6, theirs, theirs, sibling).wait_recv()
        for cp in sends:
            cp.wait_send()

    half = lambda a: pltpu.VMEM((2, a.shape[1] // 2, a.shape[2]), a.dtype)
    whole = lambda a: pltpu.VMEM(a.shape[1:], a.dtype)
    return pl.pallas_call(
        body, name="rs_exchange_join",
        in_specs=[VMEM_WHOLE] * n, out_specs=[VMEM_WHOLE] * n,
        out_shape=[_sds((2 * a.shape[1], a.shape[2]), F32) for a in parts],
        scratch_shapes=[half(a) for a in parts] + [half(a) for a in parts] + [whole(a) for a in parts]
        + [whole(a) for a in parts] + [pltpu.SemaphoreType.DMA((n, 7)), pltpu.SemaphoreType.DMA((n, 7))],
        compiler_params=pltpu.CompilerParams(vmem_limit_bytes=VMEM_LIMIT),
    )(*parts)


def _small_allreduce(loss_p, dg_parts, dbg_a, dbg_c, dwc):
    ins = [loss_p] + list(dg_parts) + [dbg_a, dbg_c, dwc]
    n_in = len(ins)
    vmem = pl.BlockSpec(memory_space=pltpu.VMEM)

    def body(*refs):
        in_refs = refs[:n_in]
        out_ref, vec, buf, send_sems, recv_sems = refs[n_in:]
        x, y, c = _place()
        me = 4 * x + 2 * y + c
        vec[...] = jnp.zeros_like(vec)
        vec[0:1, :] = jnp.sum(in_refs[0][...], axis=0)
        for r in range(5):
            vec[1 + r:2 + r, :] = jnp.sum(in_refs[1 + r][...], axis=0)
        vec[6:7, :] = jnp.sum(in_refs[6][...], axis=0)
        vec[7:8, :] = jnp.sum(in_refs[7][...], axis=0)
        vec[8:16, 0:CONV_W] = jnp.sum(in_refs[8][...], axis=0)
        buf[pl.ds(me, 1)] = vec[...][None]
        copies = []
        for r in range(1, 8):
            fx, fy, fc = (r >> 2) & 1, (r >> 1) & 1, r & 1
            to = (1 - x if fx else x, 1 - y if fy else y, 1 - c if fc else c)
            cp = pltpu.make_async_remote_copy(src_ref=vec, dst_ref=buf.at[me], send_sem=send_sems.at[r - 1],
                                              recv_sem=recv_sems.at[r - 1], device_id=to, device_id_type=MESH)
            cp.start()
            copies.append(cp)
        for cp in copies:
            cp.wait()
        total = buf[0]
        for s in range(1, 8):
            total = total + buf[s]
        out_ref[...] = total
        out_ref[0:1, :] = jnp.broadcast_to(jnp.sum(total[0:1, :], axis=-1, keepdims=True), (1, D_MODEL))

    return pl.pallas_call(
        body, name="small_allreduce",
        in_specs=[vmem] * n_in, out_specs=vmem, out_shape=_sds((SMALL_ROWS, D_MODEL), F32),
        scratch_shapes=[pltpu.VMEM((SMALL_ROWS, D_MODEL), F32), pltpu.VMEM((8, SMALL_ROWS, D_MODEL), F32),
                        pltpu.SemaphoreType.DMA((7,)), pltpu.SemaphoreType.DMA((7,))],
    )(*ins)


def _local_step(x, p, tgt, g, b_gate, w_conv, wf):
    seq = x.shape[0]
    tm = min(seq, 1024)
    th = min(seq, 512)
    tl = min(seq, 2048)
    ni, nh, nl = seq // tm, seq // th, seq // tl
    g_pre_mix, g_post_mix, g_pre_mlp, g_post_mlp, g_ple = g
    w_in, w_ao, w_co, w_o, w_up, w_down, w_pg, w_pp, w_in_nat, w_up_nat = wf
    D = D_MODEL
    vec = lambda a, blk=0: (a, _bs((1, D), lambda i, j, k: (0, blk)))
    rows_i = lambda a, t, blk=0: (a, _bs((t, D), lambda i, j, k: (i, blk)))
    rows_k = lambda a, t, blk=0: (a, _bs((t, D), lambda i, j, k: (k, blk)))
    part = lambda n: (_sds((n, 1, D), F32), _bs((None, 1, D), lambda i, j, k: (i, 0, 0)))
    full2 = lambda a: (a, _bs(a.shape, lambda i, j, k: (0, 0)))

    normed = lambda xb, gb: (_rms(xb, gb).astype(BF16),) * 2
    keep_a = lambda t: [(_sds((seq, D), BF16), _bs((t, D), lambda i, j, k: (i, 0)))]
    main_w = D_IN - 2 * D
    proj, gates, h1 = _mm("proj_in", "nn", (nh, 1, 1),
                          a_ins=[rows_i(x, th), vec(g_pre_mix)], a_fn=normed,
                          b_ins=[full2(w_in_nat)], b_fn=_ident,
                          epi_fn=lambda acc: (acc[:, :main_w], acc[:, main_w:]),
                          outs=[(_sds((seq, main_w), F32), _bs((th, main_w), lambda i, j, k: (i, 0))),
                                (_sds((seq, 2 * D), BF16), _bs((th, 2 * D), lambda i, j, k: (i, 0)))],
                          acc_shape=(th, D_IN), a_cache=((th, D), BF16), a_outs=keep_a(th))
    o = _attn_fwd(proj, seq)
    e = _conv_fwd(proj, w_conv, seq, tm)

    def gate_values(ga, gc, ba, bc):
        return _sig(ga.astype(F32) + ba), _sig(gc.astype(F32) + bc)

    def branch_outputs(ob, eb, wao, wco):
        return _nn(ob, wao).astype(BF16).astype(F32), _nn(eb, wco).astype(BF16).astype(F32)

    def mix_fn(ga, gc, ob, eb, ba, bc, wao, wco):
        sa, sc = gate_values(ga, gc, ba, bc)
        ya, yc = branch_outputs(ob, eb, wao, wco)
        return ((sa * ya + sc * yc).astype(BF16),) * 2

    def post_mix(acc, xb, gb):
        return acc, xb + _rms(acc, gb)

    half_rows = lambda a: (a, _bs((th, a.shape[1]), lambda i, j, k: (i, 0)))
    mix_ins = [rows_i(gates, th, 0), rows_i(gates, th, 1), half_rows(o), half_rows(e), vec(b_gate, 0), vec(b_gate, 1),
               full2(w_ao), full2(w_co)]
    mixed, x1, mixin = _mm(
        "mix_out", "nn", (nh, 1, 1),
        a_ins=mix_ins, a_fn=mix_fn, b_ins=[full2(w_o)], b_fn=_ident,
        epi_ins=[rows_i(x, th), vec(g_post_mix)], epi_fn=post_mix,
        outs=[(_sds((seq, D), BF16), _bs((th, D), lambda i, j, k: (i, 0))),
              (_sds((seq, D), F32), _bs((th, D), lambda i, j, k: (i, 0)))],
        acc_shape=(th, D), a_cache=((th, D), BF16), a_outs=keep_a(th))
    up, h2 = _mm("mlp_up", "nn", (nh, 1, 1),
                 a_ins=[rows_i(x1, th), vec(g_pre_mlp)], a_fn=normed,
                 b_ins=[full2(w_up_nat)], b_fn=_ident,
                 outs=[(_sds((seq, D_FF), BF16), _bs((th, D_FF), lambda i, j, k: (i, 0)))],
                 acc_shape=(th, D_FF), a_cache=((th, D), BF16), a_outs=keep_a(th))

    def relu2(ub):
        r = jnp.maximum(ub.astype(F32), 0.0)
        return (r * r).astype(BF16)

    dx2, df, dpre, h3, dpp, loss_p, dg_ple_p, dg_post_mlp_p = _mlp_down_ple_head(
        up, x1, p, tgt, g_ple, g_post_mlp, w_down, w_pg, w_pp, seq, th)

    (dw_pp,) = _mm("dw_ple_proj", "tn", (1, 1, nh),
                   a_ins=[(p, _bs((th, PLE_DIM), lambda i, j, k: (k, 0)))], a_fn=_to_bf16,
                   b_ins=[rows_k(dpp, th)], b_fn=_ident,
                   outs=[(_sds((PLE_DIM, D), F32), _bs((PLE_DIM, D), lambda i, j, k: (0, 0)))],
                   acc_shape=(PLE_DIM, D))
    (dw_pg,) = _mm("dw_ple_gate", "tn", (1, 1, nl),
                   a_ins=[rows_k(h3, tl)], a_fn=_ident, b_ins=[rows_k(dpre, tl)], b_fn=_ident,
                   outs=[(_sds((D, D), F32), _bs((D, D), lambda i, j, k: (0, 0)))], acc_shape=(D, D))

    def dup_fn(acc, ub):
        return (acc * (2.0 * jnp.maximum(ub.astype(F32), 0.0)),)

    (dup,) = _mm("d_mlp_down", "nt", (nh, 1, 1),
                 a_ins=[rows_i(df, th)], a_fn=_ident, b_ins=[full2(w_down)], b_fn=_ident,
                 epi_ins=[(up, _bs((th, D_FF), lambda i, j, k: (i, 0)))], epi_fn=dup_fn,
                 outs=[(_sds((seq, D_FF), BF16), _bs((th, D_FF), lambda i, j, k: (i, 0)))],
                 acc_shape=(th, D_FF))
    (dw_down,) = _mm("dw_mlp_down", "tn", (4, 1, nl),
                     a_ins=[(up, _bs((tl, D), lambda i, j, k: (k, i)))], a_fn=relu2,
                     b_ins=[rows_k(df, tl)], b_fn=_ident,
                     outs=[(_sds((D_FF, D), F32), _bs((D, D), lambda i, j, k: (i, 0)))], acc_shape=(D, D))
    (dw_up,) = _mm("dw_mlp_up", "tn", (1, 4, nl),
                   a_ins=[rows_k(h2, tl)], a_fn=_ident,
                   b_ins=[(dup, _bs((tl, D), lambda i, j, k: (k, j)))], b_fn=_ident,
                   outs=[(_sds((N_CHIPS, D, D), F32), _bs((None, D, D), lambda i, j, k: (j, 0, 0)))],
                   acc_shape=(D, D))

    def mlp_norm_bwd(acc, x1b, dx2b, mixedb, g_mlp, g_mix):
        dxn, dg_mlp = _rms_bwd(x1b, g_mlp, acc)
        dx1b = dx2b + dxn
        dmixedb, dg_mix = _rms_bwd(mixedb.astype(F32), g_mix, dx1b)
        return dx1b, dmixedb, dg_mlp, dg_mix

    dx1, dmixed, dg_pre_mlp_p, dg_post_mix_p = _mm(
        "d_mlp_up", "nt", (nh, 1, 1),
        a_ins=[(dup, _bs((th, D_FF), lambda i, j, k: (i, 0)))], a_fn=_ident,
        b_ins=[full2(w_up_nat)], b_fn=_ident,
        epi_ins=[rows_i(x1, th), rows_i(dx2, th), rows_i(mixed, th), vec(g_pre_mlp), vec(g_post_mix)],
        epi_fn=mlp_norm_bwd,
        outs=[(_sds((seq, D), F32), _bs((th, D), lambda i, j, k: (i, 0))),
              (_sds((seq, D), BF16), _bs((th, D), lambda i, j, k: (i, 0))), part(nh), part(nh)],
        acc_shape=(th, D))
    (dw_o,) = _mm("dw_mix_out", "tn", (1, 1, nl),
                  a_ins=[rows_k(mixin, tl)], a_fn=_ident, b_ins=[rows_k(dmixed, tl)], b_fn=_ident,
                  outs=[(_sds((D, D), F32), _bs((D, D), lambda i, j, k: (0, 0)))], acc_shape=(D, D))

    def gate_bwd(acc, ga, gc, ob, eb, ba, bc, wao, wco):
        sa, sc = gate_values(ga, gc, ba, bc)
        ya, yc = branch_outputs(ob, eb, wao, wco)
        dga = acc * ya * sa * (1.0 - sa)
        dgc = acc * yc * sc * (1.0 - sc)
        dya, dyc = (acc * sa).astype(BF16), (acc * sc).astype(BF16)
        return (dya, dyc, jnp.concatenate([dga, dgc], axis=1), _nt(dya, wao), _nt(dyc, wco),
                jnp.sum(dga, axis=0, keepdims=True), jnp.sum(dgc, axis=0, keepdims=True))

    dya, dyc, dgate, do, de, dbg_a_p, dbg_c_p = _mm(
        "d_mix_out", "nt", (nh, 1, 1),
        a_ins=[rows_i(dmixed, th)], a_fn=_ident, b_ins=[full2(w_o)], b_fn=_ident,
        epi_ins=mix_ins, epi_fn=gate_bwd,
        outs=[(_sds((seq, D), BF16), _bs((th, D), lambda i, j, k: (i, 0)))] * 2
             + [(_sds((seq, 2 * D), BF16), _bs((th, 2 * D), lambda i, j, k: (i, 0))),
                (_sds((seq, ATTN_W), BF16), _bs((th, ATTN_W), lambda i, j, k: (i, 0))),
                (_sds((seq, CONV_W), F32), _bs((th, CONV_W), lambda i, j, k: (i, 0))), part(nh), part(nh)],
        acc_shape=(th, D))
    (dw_ao,) = _mm("dw_attn_out", "tn", (1, 1, nh),
                   a_ins=[(o, _bs((th, ATTN_W), lambda i, j, k: (k, 0)))], a_fn=_ident,
                   b_ins=[rows_k(dya, th)], b_fn=_ident,
                   outs=[(_sds((ATTN_W, D), F32), _bs((ATTN_W, D), lambda i, j, k: (0, 0)))], acc_shape=(ATTN_W, D))
    dq, dk, dv = _attn_bwd(proj, do, seq)
    (dw_co,) = _mm("dw_conv_out", "tn", (1, 1, nh),
                   a_ins=[(e, _bs((th, CONV_W), lambda i, j, k: (k, 0)))], a_fn=_ident,
                   b_ins=[rows_k(dyc, th)], b_fn=_ident,
                   outs=[(_sds((CONV_W, D), F32), _bs((CONV_W, D), lambda i, j, k: (0, 0)))], acc_shape=(CONV_W, D))
    dconv, dwc_p = _conv_bwd(proj, de, w_conv, seq, tm)
    qkv_w = 3 * ATTN_W
    join_bf16 = lambda *blocks: jnp.concatenate([b.astype(BF16) for b in blocks], axis=1)
    piece = lambda a, t, rows, blk=0: (a, _bs((t, a.shape[1]), (lambda i, j, k: (k, blk)) if rows == "k"
                                             else (lambda i, j, k: (i, blk))))
    (dw_in_qkv,) = _mm("dw_proj_in_qkv", "tn", (1, 1, ni),
                       a_ins=[rows_k(h1, tm)], a_fn=_ident,
                       b_ins=[piece(dq, tm, "k"), piece(dk, tm, "k"), piece(dv, tm, "k")], b_fn=join_bf16,
                       outs=[(_sds((D, qkv_w), F32), _bs((D, qkv_w), lambda i, j, k: (0, 0)))], acc_shape=(D, qkv_w))
    (dw_in_conv,) = _mm("dw_proj_in_conv", "tn", (1, 1, nl),
                        a_ins=[rows_k(h1, tl)], a_fn=_ident, b_ins=[piece(dconv, tl, "k")], b_fn=_ident,
                        outs=[(_sds((D, 3 * CONV_W), F32), _bs((D, 3 * CONV_W), lambda i, j, k: (0, 0)))],
                        acc_shape=(D, 3 * CONV_W))
    (dw_in_gate,) = _mm("dw_proj_in_gate", "tn", (1, 2, nl),
                        a_ins=[rows_k(h1, tl)], a_fn=_ident,
                        b_ins=[(dgate, _bs((tl, D), lambda i, j, k: (k, j)))], b_fn=_ident,
                        outs=[(_sds((D, 2 * D), F32), _bs((D, D), lambda i, j, k: (0, j)))], acc_shape=(D, D))
    dw_in = jnp.concatenate([dw_in_qkv, dw_in_conv, dw_in_gate], axis=1)

    def in_norm_bwd(acc, xb, dx1b, gb):
        dxn, dg = _rms_bwd(xb, gb, acc)
        return dx1b + dxn, dg

    grad_x, dg_pre_mix_p = _mm("d_proj_in", "nt", (nh, 1, 1),
                               a_ins=[piece(dq, th, "i"), piece(dk, th, "i"), piece(dv, th, "i"),
                                      piece(dconv, th, "i"), piece(dgate, th, "i")], a_fn=join_bf16,
                               b_ins=[full2(w_in_nat)], b_fn=_ident,
                               epi_ins=[rows_i(x, th), rows_i(dx1, th), vec(g_pre_mix)], epi_fn=in_norm_bwd,
                               outs=[(_sds((seq, D), F32), _bs((th, D), lambda i, j, k: (i, 0))), part(nh)],
                               acc_shape=(th, D))

    chip_major = lambda a: a.reshape(a.shape[0], N_CHIPS, a.shape[1] // N_CHIPS).transpose(1, 0, 2)
    big = [chip_major(dw_in), chip_major(dw_ao), chip_major(dw_co), dw_o.reshape(N_CHIPS, D // N_CHIPS, D), dw_up,
           dw_down.reshape(N_CHIPS, D_FF // N_CHIPS, D), dw_pg.reshape(N_CHIPS, D // N_CHIPS, D), chip_major(dw_pp)]
    small = (loss_p, [dg_pre_mix_p, dg_post_mix_p, dg_pre_mlp_p, dg_post_mlp_p, dg_ple_p], dbg_a_p, dbg_c_p, dwc_p)
    return grad_x, big, small


RS_GROUPS = ((0,), (4,), (5,), (1, 2, 3, 6, 7))


def _reduce_scatter(big):
    pair = [None] * len(big)
    for gi, group in enumerate(RS_GROUPS):
        for w, s in zip(group, _rs_pair_sum(f"rs_pair_sum_{gi}", [big[w] for w in group])):
            pair[w] = s
    return _rs_exchange_join(pair)


def kernel(x, p, g_pre_mix, w_in, b_gate, w_conv, w_attn_out, w_conv_out, w_o, g_post_mix, g_pre_mlp, w_up, w_down, g_post_mlp, g_ple, w_ple_gate, w_ple_proj, loss_target, m_g_pre_mix, m_w_in, m_b_gate, m_w_conv, m_w_attn_out, m_w_conv_out, m_w_o, m_g_post_mix, m_g_pre_mlp, m_w_up, m_w_down, m_g_post_mlp, m_g_ple, m_w_ple_gate, m_w_ple_proj, v_g_pre_mix, v_w_in, v_b_gate, v_w_conv, v_w_attn_out, v_w_conv_out, v_w_o, v_g_post_mix, v_g_pre_mlp, v_w_up, v_w_down, v_g_post_mlp, v_g_ple, v_w_ple_gate, v_w_ple_proj):
    mats = [w_in, w_attn_out, w_conv_out, w_o, w_up, w_down, w_ple_gate, w_ple_proj]
    mats_m = [m_w_in, m_w_attn_out, m_w_conv_out, m_w_o, m_w_up, m_w_down, m_w_ple_gate, m_w_ple_proj]
    mats_v = [v_w_in, v_w_attn_out, v_w_conv_out, v_w_o, v_w_up, v_w_down, v_w_ple_gate, v_w_ple_proj]
    gains = [g_pre_mix, g_post_mix, g_pre_mlp, g_post_mlp, g_ple]
    gains_m = [m_g_pre_mix, m_g_post_mix, m_g_pre_mlp, m_g_post_mlp, m_g_ple]
    gains_v = [v_g_pre_mix, v_g_post_mix, v_g_pre_mlp, v_g_post_mlp, v_g_ple]

    taps = jnp.concatenate([w_conv[0], jnp.zeros((CONV_PAD_ROWS - 3, LANES), F32)], axis=0)
    gathered = _allgather_weights([w[0].astype(BF16) for w in mats] + [taps])
    cols_joined = lambda a: a.transpose(1, 0, 2).reshape(a.shape[1], N_CHIPS * a.shape[2])
    rows_joined = lambda a: a.reshape(N_CHIPS * a.shape[1], a.shape[2])
    wf = [gathered[0], cols_joined(gathered[1]), cols_joined(gathered[2]), rows_joined(gathered[3]), gathered[4],
          rows_joined(gathered[5]), rows_joined(gathered[6]), cols_joined(gathered[7]),
          cols_joined(gathered[0]), cols_joined(gathered[4])]
    w_conv_full = cols_joined(gathered[8])[0:3, :]
    chip = 2 * lax.axis_index("x") + lax.axis_index("y")

    grad_x, big, small = _local_step(x[0], p[0, 0], loss_target[0], gains, b_gate, w_conv_full, wf)

    shard_grads = _reduce_scatter(big)
    red = _small_allreduce(*small)
    loss = red[0, 0]
    grad_gains = [red[1 + r:2 + r, :] for r in range(5)]
    grad_b_gate = jnp.concatenate([red[6:7, :], red[7:8, :]], axis=1)
    grad_w_conv = lax.dynamic_slice(red[8:11, :], (0, chip * LANES), (3, LANES))[None]

    grads_big = [gr.reshape(w.shape) for gr, w in zip(shard_grads, mats)]
    upd_big = [_adamw(f"adamw_{i}", w, gr, m, v) for i, (w, gr, m, v) in enumerate(zip(mats, grads_big, mats_m, mats_v))]
    pack = lambda vs, bg: jnp.concatenate(list(vs) + [bg.reshape(2, D_MODEL), jnp.zeros((1, D_MODEL), F32)], axis=0)
    upd_small = _adamw("adamw_small", pack(gains, b_gate), pack(grad_gains, grad_b_gate),
                       pack(gains_m, m_b_gate), pack(gains_v, v_b_gate))
    upd_conv = _adamw("adamw_conv", w_conv, grad_w_conv, m_w_conv, v_w_conv)

    def small_out(a, which):
        gains_out = [a[r:r + 1, :] for r in range(5)]
        return gains_out, a[5:7, :].reshape(1, 2 * D_MODEL)

    def ordered(g_pre_mix_, big_, b_gate_, conv_, g_rest):
        return [g_pre_mix_, big_[0], b_gate_, conv_, big_[1], big_[2], big_[3], g_rest[0], g_rest[1], big_[4], big_[5],
                g_rest[2], g_rest[3], big_[6], big_[7]]

    outs = [loss, grad_x[None]]
    outs += ordered(grad_gains[0], grads_big, grad_b_gate, grad_w_conv, grad_gains[1:])
    for which in range(3):
        g_out, b_out = small_out(upd_small[which], which)
        outs += ordered(g_out[0], [u[which] for u in upd_big], b_out, upd_conv[which], g_out[1:])
    return tuple(outs)
```

```python
import jax
import jax.numpy as jnp
from jax import lax
from jax.experimental import pallas as pl
from jax.experimental.pallas import tpu as pltpu

F32 = jnp.float32
BF16 = jnp.bfloat16
MESH = pl.DeviceIdType.MESH

D_MODEL = 1024
N_HEADS = 8
HEAD_DIM = 64
ATTN_W = N_HEADS * HEAD_DIM
CONV_W = 512
D_FF = 4096
PLE_DIM = 256
D_IN = 5120
N_CHIPS = 4
EPS = 1e-6
Q_SCALE = HEAD_DIM ** -0.5

ADAM_LR = 0.001
ADAM_B1 = 0.9
ADAM_B2 = 0.999
ADAM_EPS = 1e-08
ADAM_WD = 0.01
ADAM_STEP = 10

V7X_VMEM_BYTES = 64 * 1024 * 1024
VMEM_LIMIT = V7X_VMEM_BYTES - 8 * 1024 * 1024
LANES = 128
ATT_BLK = 256
SMALL_ROWS = 16
CONV_PAD_ROWS = 16


def _cparams(n_grid):
    return pltpu.CompilerParams(dimension_semantics=("arbitrary",) * n_grid, vmem_limit_bytes=VMEM_LIMIT)


def _bs(shape, fn):
    return pl.BlockSpec(shape, fn)


def _rms_stats(xf):
    return lax.rsqrt(jnp.mean(xf * xf, axis=-1, keepdims=True) + EPS)


def _rms(xf, g):
    return xf * _rms_stats(xf) * g


def _rms_bwd(xf, g, dy):
    r = _rms_stats(xf)
    xh = xf * r
    dyg = dy * g
    dx = r * (dyg - xh * jnp.mean(dyg * xh, axis=-1, keepdims=True))
    return dx, jnp.sum(dy * xh, axis=0, keepdims=True)


def _sig(z):
    return 1.0 / (1.0 + jnp.exp(-z))


def _ident(a):
    return a


def _to_bf16(a):
    return a.astype(BF16)


_DIMS = {"nn": (((1,), (0,)), ((), ())), "nt": (((1,), (1,)), ((), ())), "tn": (((0,), (0,)), ((), ()))}


def _mm(name, mode, grid, a_ins, a_fn, b_ins, b_fn, outs, acc_shape, epi_ins=(), epi_fn=None,
        a_cache=None, a_outs=(), epi_a=()):
    nk = grid[2]
    na, nb, ne, no, nao = len(a_ins), len(b_ins), len(epi_ins), len(outs), len(a_outs)
    assert a_cache is None or nk == 1
    assert not a_outs or a_cache is not None
    dims = _DIMS[mode]
    if epi_fn is None:
        epi_fn = lambda acc: (acc,)

    def body(*refs):
        a_refs = refs[:na]
        b_refs = refs[na:na + nb]
        e_refs = refs[na + nb:na + nb + ne]
        o_refs = refs[na + nb + ne:na + nb + ne + no]
        ao_refs = refs[na + nb + ne + no:na + nb + ne + no + nao]
        scratch = list(refs[na + nb + ne + no + nao:])
        acc_ref = scratch.pop(0) if nk > 1 else None
        a_sc = scratch.pop(0) if a_cache is not None else None
        j = pl.program_id(1)
        k = pl.program_id(2)

        def finish(acc):
            res = epi_fn(acc, *[a_refs[t][...] for t in epi_a], *[r[...] for r in e_refs])
            for r, val in zip(o_refs, res):
                r[...] = val.astype(r.dtype)

        if a_sc is not None:
            @pl.when(j == 0)
            def _():
                res = a_fn(*[r[...] for r in a_refs])
                if nao:
                    for r, val in zip(ao_refs, res[1:]):
                        r[...] = val.astype(r.dtype)
                    res = res[0]
                a_sc[...] = res
            a = a_sc[...]
        else:
            a = a_fn(*[r[...] for r in a_refs])
        b = b_fn(*[r[...] for r in b_refs])
        prod = lax.dot_general(a, b, dims, preferred_element_type=F32)
        if nk == 1:
            finish(prod)
        else:
            @pl.when(k == 0)
            def _():
                acc_ref[...] = prod

            @pl.when(k > 0)
            def _():
                acc_ref[...] += prod

            @pl.when(k == nk - 1)
            def _():
                finish(acc_ref[...])

    scratch_shapes = []
    if nk > 1:
        scratch_shapes.append(pltpu.VMEM(acc_shape, F32))
    if a_cache is not None:
        scratch_shapes.append(pltpu.VMEM(*a_cache))
    all_outs = list(outs) + list(a_outs)
    res = pl.pallas_call(
        body, name=name, grid=grid,
        in_specs=[s for _, s in a_ins] + [s for _, s in b_ins] + [s for _, s in epi_ins],
        out_specs=[s for _, s in all_outs],
        out_shape=[o for o, _ in all_outs],
        scratch_shapes=scratch_shapes,
        compiler_params=_cparams(3),
    )(*[a for a, _ in a_ins], *[a for a, _ in b_ins], *[a for a, _ in epi_ins])
    return res


def _sds(shape, dtype):
    return jax.ShapeDtypeStruct(shape, dtype)


def _nt(a, b):
    return lax.dot_general(a, b, _DIMS["nt"], preferred_element_type=F32)


def _tn(a, b):
    return lax.dot_general(a, b, _DIMS["tn"], preferred_element_type=F32)


def _nn(a, b):
    return lax.dot_general(a, b, _DIMS["nn"], preferred_element_type=F32)


def _mlp_down_ple_head(up, x1, p, tgt, g_ple, g_post_mlp, w_down, w_pg, w_pp, seq, tr):
    nblk = seq // tr
    D = D_MODEL

    def body(up_ref, x1_ref, p_ref, t_ref, gp_ref, gm_ref, wd_ref, wpg_ref, wpp_ref,
             dx2_ref, df_ref, dpre_ref, h3_ref, dpp_ref, loss_ref, dgp_ref, dgm_ref):
        gp, gm, wpg, wpp = gp_ref[...], gm_ref[...], wpg_ref[...], wpp_ref[...]
        halves = [pl.ds(0, tr // 2), pl.ds(tr // 2, tr // 2)]
        w_down = wd_ref[...]
        fb = []
        for r in halves:
            hidden = jnp.maximum(up_ref[r, :].astype(F32), 0.0)
            fb.append(_nn((hidden * hidden).astype(BF16), w_down))
        x2b = [x1_ref[r, :] + _rms(fb[s], gm) for s, r in enumerate(halves)]
        h3 = [_rms(x, gp).astype(BF16) for x in x2b]
        gate = [_sig(_nn(h, wpg)) for h in h3]
        pp = [_nn(p_ref[r, :].astype(BF16), wpp) for r in halves]
        err = [x2b[s] + gate[s] * pp[s] - t_ref[r, :] for s, r in enumerate(halves)]
        dx3 = [e * (1.0 / D) for e in err]
        dpre = [(dx3[s] * pp[s] * gate[s] * (1.0 - gate[s])).astype(BF16) for s in range(2)]
        dh3 = [_nt(d, wpg) for d in dpre]
        loss, dgp_sum, dgm_sum = 0.0, 0.0, 0.0
        for s, r in enumerate(halves):
            h3_ref[r, :] = h3[s]
            dpp_ref[r, :] = (dx3[s] * gate[s]).astype(BF16)
            dpre_ref[r, :] = dpre[s]
            dxn, dgp = _rms_bwd(x2b[s], gp, dh3[s])
            dx2 = dx3[s] + dxn
            dx2_ref[r, :] = dx2
            dfb, dgm = _rms_bwd(fb[s], gm, dx2)
            df_ref[r, :] = dfb.astype(BF16)
            loss = loss + jnp.sum(err[s] * err[s], axis=0, keepdims=True)
            dgp_sum, dgm_sum = dgp_sum + dgp, dgm_sum + dgm
        loss_ref[...] = loss * (0.5 / D)
        dgp_ref[...] = dgp_sum
        dgm_ref[...] = dgm_sum

    rows = _bs((tr, D), lambda i: (i, 0))
    vec = _bs((1, D), lambda i: (0, 0))
    part = _bs((None, 1, D), lambda i: (i, 0, 0))
    return pl.pallas_call(
        body, name="mlp_down_ple_head", grid=(nblk,),
        in_specs=[_bs((tr, D_FF), lambda i: (i, 0)), rows, _bs((tr, PLE_DIM), lambda i: (i, 0)), rows, vec, vec,
                  _bs((D_FF, D), lambda i: (0, 0)), _bs((D, D), lambda i: (0, 0)), _bs((PLE_DIM, D), lambda i: (0, 0))],
        out_specs=[rows] * 5 + [part] * 3,
        out_shape=[_sds((seq, D), F32)] + [_sds((seq, D), BF16)] * 4 + [_sds((nblk, 1, D), F32)] * 3,
        compiler_params=_cparams(1),
    )(up, x1, p, tgt, g_ple, g_post_mlp, w_down, w_pg, w_pp)


def _shift_rows_down(u, prev, n):
    rows = u.shape[0]
    ridx = lax.broadcasted_iota(jnp.int32, u.shape, 0)
    out = pltpu.roll(u, n, 0)
    for r in range(n):
        out = jnp.where(ridx == r, prev[8 - n + r:8 - n + r + 1, :], out)
    del rows
    return out


def _shift_rows_up(u, nxt, n):
    rows = u.shape[0]
    ridx = lax.broadcasted_iota(jnp.int32, u.shape, 0)
    out = pltpu.roll(u, rows - n, 0)
    for r in range(n):
        out = jnp.where(ridx == rows - n + r, nxt[r:r + 1, :], out)
    return out


CONV_COL0 = 3


def _conv_fwd(proj, w_conv, seq, tr):
    hb = tr // 8

    def body(cb_ref, cc_ref, cu_ref, ccp_ref, cup_ref, w_ref, e_ref):
        i = pl.program_id(0)
        u = cc_ref[...] * cu_ref[...]
        up = jnp.where(i > 0, ccp_ref[...] * cup_ref[...], 0.0)
        w = w_ref[...]
        d = w[0:1, :] * _shift_rows_down(u, up, 2) + w[1:2, :] * _shift_rows_down(u, up, 1) + w[2:3, :] * u
        e_ref[...] = (cb_ref[...] * d).astype(BF16)

    prev = lambda c: (lambda i: (jnp.maximum(i * hb - 1, 0), c))
    return pl.pallas_call(
        body, name="conv_fwd", grid=(seq // tr,),
        in_specs=[_bs((tr, CONV_W), lambda i: (i, CONV_COL0)),
                  _bs((tr, CONV_W), lambda i: (i, CONV_COL0 + 1)),
                  _bs((tr, CONV_W), lambda i: (i, CONV_COL0 + 2)),
                  _bs((8, CONV_W), prev(CONV_COL0 + 1)),
                  _bs((8, CONV_W), prev(CONV_COL0 + 2)),
                  _bs((3, CONV_W), lambda i: (0, 0))],
        out_specs=_bs((tr, CONV_W), lambda i: (i, 0)),
        out_shape=_sds((seq, CONV_W), BF16),
        compiler_params=_cparams(1),
    )(proj, proj, proj, proj, proj, w_conv)


def _conv_bwd(proj, de, w_conv, seq, tr):
    hb = tr // 8
    nblk = seq // tr

    def body(cb_ref, cc_ref, cu_ref, ccp_ref, cup_ref, cbn_ref, de_ref, den_ref, w_ref, o_ref, dw_ref):
        i = pl.program_id(0)
        cc, cu, cb = cc_ref[...], cu_ref[...], cb_ref[...]
        u = cc * cu
        up = jnp.where(i > 0, ccp_ref[...] * cup_ref[...], 0.0)
        u1 = _shift_rows_down(u, up, 1)
        u2 = _shift_rows_down(u, up, 2)
        de_ = de_ref[...]
        dd = de_ * cb
        ddn = jnp.where(i < nblk - 1, den_ref[...] * cbn_ref[...], 0.0)
        w = w_ref[...]
        du = w[2:3, :] * dd + w[1:2, :] * _shift_rows_up(dd, ddn, 1) + w[0:1, :] * _shift_rows_up(dd, ddn, 2)
        o_ref[:, 0:CONV_W] = (de_ * (w[0:1, :] * u2 + w[1:2, :] * u1 + w[2:3, :] * u)).astype(BF16)
        o_ref[:, CONV_W:2 * CONV_W] = (du * cu).astype(BF16)
        o_ref[:, 2 * CONV_W:3 * CONV_W] = (du * cc).astype(BF16)
        ridx = lax.broadcasted_iota(jnp.int32, (8, CONV_W), 0)
        dw0 = jnp.sum(dd * u2, axis=0, keepdims=True)
        dw1 = jnp.sum(dd * u1, axis=0, keepdims=True)
        dw2 = jnp.sum(dd * u, axis=0, keepdims=True)
        dw_ref[...] = jnp.where(ridx == 0, dw0, jnp.where(ridx == 1, dw1, jnp.where(ridx == 2, dw2, 0.0)))

    prev = lambda c: (lambda i: (jnp.maximum(i * hb - 1, 0), c))
    nxt = lambda c: (lambda i: (jnp.minimum((i + 1) * hb, seq // 8 - 1), c))
    return pl.pallas_call(
        body, name="conv_bwd", grid=(nblk,),
        in_specs=[_bs((tr, CONV_W), lambda i: (i, CONV_COL0)),
                  _bs((tr, CONV_W), lambda i: (i, CONV_COL0 + 1)),
                  _bs((tr, CONV_W), lambda i: (i, CONV_COL0 + 2)),
                  _bs((8, CONV_W), prev(CONV_COL0 + 1)),
                  _bs((8, CONV_W), prev(CONV_COL0 + 2)),
                  _bs((8, CONV_W), nxt(CONV_COL0)),
                  _bs((tr, CONV_W), lambda i: (i, 0)),
                  _bs((8, CONV_W), nxt(0)),
                  _bs((3, CONV_W), lambda i: (0, 0))],
        out_specs=[_bs((tr, 3 * CONV_W), lambda i: (i, 0)), _bs((None, 8, CONV_W), lambda i: (i, 0, 0))],
        out_shape=[_sds((seq, 3 * CONV_W), BF16), _sds((nblk, 8, CONV_W), F32)],
        compiler_params=_cparams(1),
    )(proj, proj, proj, proj, proj, proj, de, de, w_conv)


def _log_gates(z):
    lse = jnp.log(1.0 + jnp.exp(-jnp.abs(z)))
    log_beta = jnp.minimum(z, 0.0) - lse
    return log_beta, log_beta - z


DEAD_LOG_WEIGHT = -110.0
NO_TILE = -1e30


def _first_live_tile(start, scores, live_sc):
    def alive():
        return jnp.max(jnp.maximum(live_sc[0], live_sc[1])) > DEAD_LOG_WEIGHT

    def step(c):
        for h, z in enumerate(scores(c[0])):
            live_sc[h] = live_sc[h] + jnp.sum(_log_gates(z)[1], axis=-1, keepdims=True)
        return c[0] - 1, alive()

    j_end, _ = lax.while_loop(lambda c: jnp.logical_and(c[0] >= 0, c[1]), step, (start, alive()))
    return j_end + 1


def _attn_fwd(proj, seq):
    blk = ATT_BLK
    nq = seq // blk
    npair = N_HEADS // 2

    def body(q_ref, k_ref, v_ref, o_ref, z0_sc, z1_sc, w0_sc, w1_sc, tot_sc, acc_sc):
        i = pl.program_id(1)
        is_a = lax.broadcasted_iota(jnp.int32, (1, LANES), 1) < HEAD_DIM
        q2 = (q_ref[...] * Q_SCALE).astype(BF16)
        zero = jnp.zeros_like(q2)
        qs = (jnp.where(is_a, q2, zero), jnp.where(is_a, zero, q2))
        row = lax.broadcasted_iota(jnp.int32, (blk, blk), 0)
        col = lax.broadcasted_iota(jnp.int32, (blk, blk), 1)
        tri = (row > col).astype(BF16)
        causal = col < row

        def tile_of(ref, j):
            return ref[pl.ds(pl.multiple_of(j * blk, blk), blk), :].astype(BF16)

        def scores(j):
            k2 = tile_of(k_ref, j)
            return [_nt(qs[h], k2) for h in range(2)]

        has_left = i > 0
        left = jnp.maximum(i - 1, 0)

        g_d = [_log_gates(z) for z in scores(i)]
        g_l = [_log_gates(z) for z in scores(left)]
        keep_d = [jnp.where(causal, g[1], 0.0) for g in g_d]
        suf_d = [_nn(lk.astype(BF16), tri) for lk in keep_d]
        suf_l = [_nn(g[1].astype(BF16), tri) for g in g_l]
        v_d, v_l = tile_of(v_ref, i), tile_of(v_ref, left)
        pv = []
        for h in range(2):
            sum_d = jnp.sum(keep_d[h], axis=-1, keepdims=True)
            w_d = jnp.where(causal, jnp.exp(g_d[h][0] + suf_d[h]), 0.0)
            w_l = jnp.exp(g_l[h][0] + (jnp.where(has_left, sum_d, NO_TILE) + suf_l[h]))
            pv.append(_nn(w_d.astype(BF16), v_d) + _nn(w_l.astype(BF16), v_l))
            tot_sc[h] = sum_d + jnp.sum(g_l[h][1], axis=-1, keepdims=True)
        acc_sc[...] = jnp.where(is_a, pv[0], pv[1])

        z_bufs, w_bufs = (z0_sc, z1_sc), (w0_sc, w1_sc)

        def alive():
            return jnp.max(jnp.maximum(tot_sc[0], tot_sc[1])) > DEAD_LOG_WEIGHT

        def put(ref, vals):
            for h in range(2):
                ref[h] = vals[h]

        def weights(zs):
            gates = [_log_gates(z) for z in zs]
            sums = [_nn(g[1].astype(BF16), tri) for g in gates]
            ws = []
            for h in range(2):
                ws.append(jnp.exp(gates[h][0] + (tot_sc[h] + sums[h])).astype(BF16))
                tot_sc[h] = tot_sc[h] + jnp.sum(gates[h][1], axis=-1, keepdims=True)
            return ws

        def add_values(w_buf, j):
            v2 = tile_of(v_ref, j)
            acc_sc[...] += jnp.where(is_a, _nn(w_buf[0], v2), _nn(w_buf[1], v2))

        def trip(j, s):
            add_values(w_bufs[s], j + 1)
            put(z_bufs[1 - s], scores(jnp.maximum(j - 1, 0)))
            put(w_bufs[1 - s], weights((z_bufs[s][0], z_bufs[s][1])))

        @pl.when(jnp.logical_and(i >= 2, alive()))
        def _():
            put(z0_sc, scores(i - 2))
            w0_sc[...] = jnp.zeros_like(w0_sc)

            def two_trips(c):
                trip(c[0], 0)
                trip(c[0] - 1, 1)
                return c[0] - 2, alive()

            j_next, still = lax.while_loop(lambda c: jnp.logical_and(c[0] >= 1, c[1]), two_trips, (i - 2, i >= 2))
            one_left = jnp.logical_and(j_next == 0, still)

            @pl.when(one_left)
            def _():
                trip(0, 0)
                add_values(w1_sc, 0)

            @pl.when(jnp.logical_not(one_left))
            def _():
                add_values(w0_sc, j_next + 1)

        o_ref[...] = acc_sc[...].astype(BF16)

    return pl.pallas_call(
        body, name="attn_fwd", grid=(npair, nq),
        in_specs=[_bs((blk, LANES), lambda p, i: (i, p)),
                  _bs((seq, LANES), lambda p, i: (0, npair + p)),
                  _bs((seq, LANES), lambda p, i: (0, 2 * npair + p))],
        out_specs=_bs((blk, LANES), lambda p, i: (i, p)),
        out_shape=_sds((seq, ATTN_W), BF16),
        scratch_shapes=[pltpu.VMEM((2, blk, blk), F32), pltpu.VMEM((2, blk, blk), F32),
                        pltpu.VMEM((2, blk, blk), BF16), pltpu.VMEM((2, blk, blk), BF16),
                        pltpu.VMEM((2, blk, 1), F32), pltpu.VMEM((blk, LANES), F32)],
        compiler_params=_cparams(2),
    )(proj, proj, proj)


def _attn_bwd(proj, do, seq):
    blk = ATT_BLK
    nq = seq // blk
    npair = N_HEADS // 2

    def body(q_ref, k_ref, v_ref, do_ref, dq_ref, dk_ref, dv_ref,
             prod0_sc, prod1_sc, pend0_sc, pend1_sc, tot_sc, live_sc, cum_sc, pre_sc, dq_sc):
        i = pl.program_id(1)

        @pl.when(i == 0)
        def _():
            dk_ref[...] = jnp.zeros_like(dk_ref)
            dv_ref[...] = jnp.zeros_like(dv_ref)

        is_a = lax.broadcasted_iota(jnp.int32, (1, LANES), 1) < HEAD_DIM
        q2 = (q_ref[...] * Q_SCALE).astype(BF16)
        do2 = do_ref[...]
        zero = jnp.zeros_like(q2)
        qs = (jnp.where(is_a, q2, zero), jnp.where(is_a, zero, q2))
        dos = (jnp.where(is_a, do2, zero), jnp.where(is_a, zero, do2))
        row = lax.broadcasted_iota(jnp.int32, (blk, blk), 0)
        col = lax.broadcasted_iota(jnp.int32, (blk, blk), 1)
        tri_after = (row > col).astype(BF16)
        tri_excl = (row < col).astype(BF16)
        causal = col < row

        def tile_of(ref, j):
            return ref[pl.ds(pl.multiple_of(j * blk, blk), blk), :].astype(BF16)

        def scores(j):
            k2 = tile_of(k_ref, j)
            return [_nt(qs[h], k2) for h in range(2)]

        def products(j):
            v2 = tile_of(v_ref, j)
            return scores(j) + [_nt(dos[h], v2) for h in range(2)]

        def row_sum(a):
            return jnp.sum(a, axis=-1, keepdims=True)

        def grad_matmuls(ws, dzs, j):
            rows = pl.ds(pl.multiple_of(j * blk, blk), blk)
            k2 = tile_of(k_ref, j)
            dq_sc[...] += jnp.where(is_a, _nn(dzs[0], k2), _nn(dzs[1], k2))
            dk_ref[rows, :] += jnp.where(is_a, _tn(dzs[0], q2), _tn(dzs[1], q2))
            if ws is not None:
                dv_ref[rows, :] += jnp.where(is_a, _tn(ws[0], do2), _tn(ws[1], do2))

        has_left = i > 0
        left = jnp.maximum(i - 1, 0)

        p_d, p_l = products(i), products(left)
        g_d = [_log_gates(z) for z in p_d[:2]]
        g_l = [_log_gates(z) for z in p_l[:2]]
        keep_d = [jnp.where(causal, g[1], 0.0) for g in g_d]
        suf_d = [_nn(lk.astype(BF16), tri_after) for lk in keep_d]
        suf_l = [_nn(g[1].astype(BF16), tri_after) for g in g_l]
        w_d, w_l, gg_d, gg_l = [], [], [], []
        for h in range(2):
            sum_d = row_sum(keep_d[h])
            w_d.append(jnp.where(causal, jnp.exp(g_d[h][0] + suf_d[h]), 0.0))
            w_l.append(jnp.exp(g_l[h][0] + (jnp.where(has_left, sum_d, NO_TILE) + suf_l[h])))
            gg_d.append(p_d[2 + h] * w_d[h])
            gg_l.append(p_l[2 + h] * w_l[h])
            tot_sc[h] = sum_d + row_sum(g_l[h][1])
        before_d = [_nn(g.astype(BF16), tri_excl) for g in gg_d]
        before_l = [_nn(g.astype(BF16), tri_excl) for g in gg_l]
        dz_d, dz_l = [], []
        for h in range(2):
            beta_d, beta_l = jnp.exp(g_d[h][0]), jnp.exp(g_l[h][0])
            dz_l.append((gg_l[h] * (1.0 - beta_l) - before_l[h] * beta_l).astype(BF16))
            dz = gg_d[h] * (1.0 - beta_d) - (row_sum(gg_l[h]) + before_d[h]) * beta_d
            dz_d.append(jnp.where(causal, dz, 0.0).astype(BF16))
        dq_sc[...] = jnp.zeros_like(dq_sc)
        grad_matmuls([w.astype(BF16) for w in w_l], dz_l, left)
        grad_matmuls([w.astype(BF16) for w in w_d], dz_d, i)

        live_sc[...] = tot_sc[...]
        first = _first_live_tile(i - 2, scores, live_sc)
        trips = i - 1 - first
        prod_bufs, pend_bufs = (prod0_sc, prod1_sc), (pend0_sc, pend1_sc)

        def local_grads(prods):
            zs, dws = prods[:2], prods[2:]
            gates = [_log_gates(z) for z in zs]
            sums = [_nn(g[1].astype(BF16), tri_after) for g in gates]
            ws, gs = [], []
            for h in range(2):
                cum = cum_sc[h] + row_sum(gates[h][1])
                cum_sc[h] = cum
                ws.append(jnp.exp(gates[h][0] + ((live_sc[h] - cum) + sums[h])))
                gs.append(dws[h] * ws[h])
            befores = [_nn(g.astype(BF16), tri_excl) for g in gs]
            dzs = []
            for h in range(2):
                beta = jnp.exp(gates[h][0])
                dzs.append((gs[h] * (1.0 - beta) - (pre_sc[h] + befores[h]) * beta).astype(BF16))
                pre_sc[h] = pre_sc[h] + row_sum(gs[h])
            return [w.astype(BF16) for w in ws] + dzs

        def put(ref, vals):
            for n, val in enumerate(vals):
                ref[n] = val

        def flush(pend, j):
            grad_matmuls([pend[0], pend[1]], [pend[2], pend[3]], j)

        def trip(j, s):
            flush(pend_bufs[s], jnp.maximum(j - 1, first))
            put(prod_bufs[1 - s], products(j + 1))
            put(pend_bufs[1 - s], local_grads([prod_bufs[s][n] for n in range(4)]))

        def earlier_keys_share(j, mask):
            dzs = []
            for h, z in enumerate(scores(j)):
                beta = jnp.exp(_log_gates(z)[0])
                dzs.append(jnp.where(mask, -pre_sc[h] * beta, 0.0).astype(BF16))
            grad_matmuls(None, dzs, j)

        @pl.when(trips > 0)
        def _():
            cum_sc[...] = jnp.zeros_like(cum_sc)
            pre_sc[...] = jnp.zeros_like(pre_sc)
            pend0_sc[...] = jnp.zeros_like(pend0_sc)
            put(prod0_sc, products(first))

            def two_trips(pp, carry):
                trip(first + 2 * pp, 0)
                trip(first + 2 * pp + 1, 1)
                return carry

            lax.fori_loop(0, trips // 2, two_trips, 0)
            odd = trips % 2 == 1

            @pl.when(odd)
            def _():
                trip(i - 2, 0)
                flush(pend1_sc, i - 2)

            @pl.when(jnp.logical_not(odd))
            def _():
                flush(pend0_sc, i - 2)

            earlier_keys_share(i - 1, True)
            earlier_keys_share(i, causal)

        dq_ref[...] = dq_sc[...] * Q_SCALE

    qmap = lambda p, i: (i, p)
    return pl.pallas_call(
        body, name="attn_bwd", grid=(npair, nq),
        in_specs=[_bs((blk, LANES), qmap),
                  _bs((seq, LANES), lambda p, i: (0, npair + p)),
                  _bs((seq, LANES), lambda p, i: (0, 2 * npair + p)),
                  _bs((blk, LANES), qmap)],
        out_specs=[_bs((blk, LANES), qmap),
                   _bs((seq, LANES), lambda p, i: (0, p)),
                   _bs((seq, LANES), lambda p, i: (0, p))],
        out_shape=[_sds((seq, ATTN_W), F32)] * 3,
        scratch_shapes=[pltpu.VMEM((4, blk, blk), F32), pltpu.VMEM((4, blk, blk), F32),
                        pltpu.VMEM((4, blk, blk), BF16), pltpu.VMEM((4, blk, blk), BF16),
                        pltpu.VMEM((2, blk, 1), F32), pltpu.VMEM((2, blk, 1), F32), pltpu.VMEM((2, blk, 1), F32),
                        pltpu.VMEM((2, blk, 1), F32), pltpu.VMEM((blk, LANES), F32)],
        compiler_params=_cparams(2),
    )(proj, proj, proj, do)


def _elementwise(name, fn, ins, out_dtypes):
    rows, cols = ins[0].shape
    tr = rows
    for cand in (512, 256, 128, 64, 32, 16, 8):
        if rows % cand == 0 and cand * cols * 4 <= 2 * 1024 * 1024:
            tr = cand
            break
    n_in = len(ins)

    def body(*refs):
        res = fn(*[r[...] for r in refs[:n_in]])
        for r, val in zip(refs[n_in:], res):
            r[...] = val.astype(r.dtype)

    spec = _bs((tr, cols), lambda i: (i, 0))
    return pl.pallas_call(
        body, name=name, grid=(rows // tr,),
        in_specs=[spec] * n_in, out_specs=[spec] * len(out_dtypes),
        out_shape=[_sds((rows, cols), dt) for dt in out_dtypes],
        compiler_params=_cparams(1),
    )(*ins)


def _adamw_fn(w, g, m, v):
    m = ADAM_B1 * m + (1.0 - ADAM_B1) * g
    v = ADAM_B2 * v + (1.0 - ADAM_B2) * (g * g)
    m_hat = m / (1.0 - ADAM_B1 ** ADAM_STEP)
    v_hat = v / (1.0 - ADAM_B2 ** ADAM_STEP)
    delta = -ADAM_LR * (m_hat / (jnp.sqrt(v_hat) + ADAM_EPS) + ADAM_WD * w)
    return delta, m, v


def _adamw(name, w, g, m, v):
    shape = w.shape
    as2d = lambda a: a.reshape(-1, shape[-1])
    delta, nm, nv = _elementwise(name, _adamw_fn, [as2d(w), as2d(g), as2d(m), as2d(v)], [F32, F32, F32])
    return delta.reshape(shape), nm.reshape(shape), nv.reshape(shape)


def _place():
    return lax.axis_index("x"), lax.axis_index("y"), lax.axis_index("c")


ANY = pl.BlockSpec(memory_space=pl.ANY)
VMEM_WHOLE = pl.BlockSpec(memory_space=pltpu.VMEM)


def _allgather_weights(shards):
    n = len(shards)

    def body(*refs):
        src, dst = refs[:n], refs[n:2 * n]
        send_sems, recv_sems, local_sems = refs[2 * n:]
        x, y, c = _place()
        me, sibling, mychip = (x, y, c), (x, y, 1 - c), 2 * x + y

        x_nbr, y_nbr, diag = 2 * (1 - x) + y, 2 * x + (1 - y), 2 * (1 - x) + (1 - y)
        to_x, to_y = (1 - x, y, c), (x, 1 - y, c)

        def parts(w):
            hr = src[w].shape[0] // 2
            first = hr // 2 if hr % 32 == 0 else hr
            return first, hr - first

        def rows_of(w, chip, half, route):
            hr = src[w].shape[0] // 2
            first, second = parts(w)
            start, size = {0: (0, hr), 1: (0, hr), 2: (0, first), 3: (first, second)}[route]
            return dst[w].at[chip, pl.ds(half * hr + start, size)]

        def copy(w, k, src_ref, dst_ref, to):
            return pltpu.make_async_remote_copy(src_ref=src_ref, dst_ref=dst_ref, send_sem=send_sems.at[w, k],
                                                recv_sem=recv_sems.at[w, k], device_id=to, device_id_type=MESH)

        def landed(w, route):
            chip = {0: x_nbr, 1: y_nbr, 2: diag, 3: diag}[route]
            return rows_of(w, chip, c, route), chip

        def routes(w):
            return (0, 1, 2, 3) if parts(w)[1] else (0, 1, 2)

        started, local = [], []
        for w in range(n):
            hr = src[w].shape[0] // 2
            own = pltpu.make_async_copy(src[w], dst[w].at[mychip], local_sems.at[w])
            own.start()
            local.append(own)
            mine = src[w].at[pl.ds(c * hr, hr)]
            for route, to in ((0, to_x), (1, to_y)):
                cp = copy(w, route, mine, rows_of(w, mychip, c, route), to)
                cp.start()
                started.append(cp)

        def pass_on(w, route):
            got, chip = landed(w, route)
            copy(w, route, got, got, me).wait_recv()
            if route == 1:
                part = rows_of(w, chip, c, 2)
                started.append(copy(w, 2, part, part, to_x))
                started[-1].start()
            if route == 0 and parts(w)[1]:
                part = rows_of(w, chip, c, 3)
                started.append(copy(w, 3, part, part, to_y))
                started[-1].start()
            started.append(copy(w, 4 + route, got, got, sibling))
            started[-1].start()

        for w in range(n):
            pass_on(w, 1)
            pass_on(w, 0)
        for w in range(n):
            for route in routes(w)[2:]:
                pass_on(w, route)
        for w in range(n):
            for route in routes(w):
                chip = landed(w, route)[1]
                from_sib = rows_of(w, chip, 1 - c, route)
                copy(w, 4 + route, from_sib, from_sib, me).wait_recv()
        for cp in local:
            cp.wait()
        for cp in started:
            cp.wait_send()

    return pl.pallas_call(
        body, name="allgather_weights",
        in_specs=[VMEM_WHOLE] * n, out_specs=[VMEM_WHOLE] * n,
        out_shape=[_sds((N_CHIPS,) + s.shape, s.dtype) for s in shards],
        scratch_shapes=[pltpu.SemaphoreType.DMA((n, 8)), pltpu.SemaphoreType.DMA((n, 8)),
                        pltpu.SemaphoreType.DMA((n,))],
        compiler_params=pltpu.CompilerParams(vmem_limit_bytes=VMEM_LIMIT),
    )(*shards)


SUM_ROWS = 64


def _rs_pair_sum(name, grads):
    n = len(grads)

    def body(*refs):
        g, out = refs[:n], refs[n:2 * n]
        stage, give16, land, keep = (refs[m * n:(m + 1) * n] for m in range(2, 6))
        send_sems, recv_sems, stage_sems, keep_sems = refs[6 * n:]
        x, y, c = _place()
        sibling = (x, y, 1 - c)

        def over_rows(w, fn):
            nb = g[w].shape[1] // 2 // SUM_ROWS

            def step(idx, carry):
                fn(idx // nb, pl.ds(pl.multiple_of((idx % nb) * SUM_ROWS, SUM_ROWS), SUM_ROWS))
                return carry

            lax.fori_loop(0, N_CHIPS * nb, step, 0)

        loads = []
        for w in range(n):
            hr = g[w].shape[1] // 2
            st = pltpu.make_async_copy(g[w].at[:, pl.ds((1 - c) * hr, hr)], stage[w], stage_sems.at[w])
            kp = pltpu.make_async_copy(g[w].at[:, pl.ds(c * hr, hr)], keep[w], keep_sems.at[w])
            st.start()
            kp.start()
            loads.append((st, kp))
        gives = []
        for w in range(n):
            loads[w][0].wait()

            def narrow(k, rows, w=w):
                give16[w][k, rows, :] = stage[w][k, rows, :].astype(BF16)

            over_rows(w, narrow)
            give = pltpu.make_async_remote_copy(src_ref=give16[w], dst_ref=land[w], send_sem=send_sems.at[w],
                                                recv_sem=recv_sems.at[w], device_id=sibling, device_id_type=MESH)
            give.start()
            gives.append(give)
        for w in range(n):
            loads[w][1].wait()
            gives[w].wait_recv()

            def add(k, rows, w=w):
                out[w][k, rows, :] = (keep[w][k, rows, :] + land[w][k, rows, :].astype(F32)).astype(BF16)

            over_rows(w, add)
        for give in gives:
            give.wait_send()

    half = [(N_CHIPS, a.shape[1] // 2, a.shape[2]) for a in grads]
    wide = [pltpu.VMEM(s, F32) for s in half]
    narrow_bufs = [pltpu.VMEM(s, BF16) for s in half]
    sems = pltpu.SemaphoreType.DMA((n,))
    return pl.pallas_call(
        body, name=name,
        in_specs=[ANY] * n, out_specs=[VMEM_WHOLE] * n, out_shape=[_sds(s, BF16) for s in half],
        scratch_shapes=wide + narrow_bufs + narrow_bufs + wide + [sems, sems, sems, sems],
        compiler_params=pltpu.CompilerParams(vmem_limit_bytes=VMEM_LIMIT),
    )(*grads)


def _rs_exchange_join(parts):
    n = len(parts)

    def body(*refs):
        t, full = refs[:n], refs[n:2 * n]
        got_x, got_y, pass_on, got_2 = (refs[m * n:(m + 1) * n] for m in range(2, 6))
        send_sems, recv_sems = refs[6 * n:]
        x, y, c = _place()
        mychip, sibling = 2 * x + y, (x, y, 1 - c)
        x_nbr, y_nbr, diag = 2 * (1 - x) + y, 2 * x + (1 - y), 2 * (1 - x) + (1 - y)
        to_x, to_y = (1 - x, y, c), (x, 1 - y, c)
        sends = []

        def copy(w, k, src_ref, dst_ref, to):
            return pltpu.make_async_remote_copy(src_ref=src_ref, dst_ref=dst_ref, send_sem=send_sems.at[w, k],
                                                recv_sem=recv_sems.at[w, k], device_id=to, device_id_type=MESH)

        def start(cp):
            cp.start()
            sends.append(cp)

        def add_rows(w, count, fn):
            def step(idx, carry):
                fn(pl.ds(pl.multiple_of(idx * SUM_ROWS, SUM_ROWS), SUM_ROWS), pl.multiple_of(idx * SUM_ROWS, SUM_ROWS))
                return carry
            lax.fori_loop(0, count // SUM_ROWS, step, 0)

        f32 = lambda v: v.astype(F32)
        for w in range(n):
            ha = t[w].shape[1] // 2
            part_a, part_b = pl.ds(0, ha), pl.ds(ha, ha)
            start(copy(w, 0, t[w].at[x_nbr, part_a], got_x[w].at[0], to_x))
            start(copy(w, 1, t[w].at[diag, part_a], got_x[w].at[1], to_x))
            start(copy(w, 2, t[w].at[y_nbr, part_b], got_y[w].at[0], to_y))
            start(copy(w, 3, t[w].at[diag, part_b], got_y[w].at[1], to_y))
        for w in range(n):
            hr = t[w].shape[1]
            ha = hr // 2
            for k in (0, 1):
                copy(w, k, got_x[w].at[k], got_x[w].at[k], to_x).wait_recv()

            def sum_a(rows, r, w=w, hr=hr):
                full[w][pl.ds(pl.multiple_of(c * hr + r, SUM_ROWS), SUM_ROWS), :] = \
                    f32(t[w][mychip, rows, :]) + f32(got_x[w][0, rows, :])
                pass_on[w][rows, :] = (f32(t[w][y_nbr, rows, :]) + f32(got_x[w][1, rows, :])).astype(BF16)

            add_rows(w, ha, sum_a)
            start(copy(w, 4, pass_on[w].at[pl.ds(0, ha)], got_2[w].at[pl.ds(0, ha)], to_y))
            for k in (2, 3):
                copy(w, k, got_y[w].at[k - 2], got_y[w].at[k - 2], to_y).wait_recv()

            def sum_b(rows, r, w=w, hr=hr, ha=ha):
                lower = pl.ds(pl.multiple_of(ha + r, SUM_ROWS), SUM_ROWS)
                full[w][pl.ds(pl.multiple_of(c * hr + ha + r, SUM_ROWS), SUM_ROWS), :] = \
                    f32(t[w][mychip, lower, :]) + f32(got_y[w][0, rows, :])
                pass_on[w][lower, :] = (f32(t[w][x_nbr, lower, :]) + f32(got_y[w][1, rows, :])).astype(BF16)

            add_rows(w, ha, sum_b)
            start(copy(w, 5, pass_on[w].at[pl.ds(ha, ha)], got_2[w].at[pl.ds(ha, ha)], to_x))
        for w in range(n):
            hr = t[w].shape[1]
            ha = hr // 2
            copy(w, 4, got_2[w].at[pl.ds(0, ha)], got_2[w].at[pl.ds(0, ha)], to_y).wait_recv()
            copy(w, 5, got_2[w].at[pl.ds(ha, ha)], got_2[w].at[pl.ds(ha, ha)], to_x).wait_recv()

            def finish(rows, r, w=w, hr=hr):
                out_rows = pl.ds(pl.multiple_of(c * hr + r, SUM_ROWS), SUM_ROWS)
                full[w][out_rows, :] = full[w][out_rows, :] + f32(got_2[w][rows, :])

            add_rows(w, hr, finish)
            mine = full[w].at[pl.ds(c * hr, hr)]
            start(copy(w, 6, mine, mine, sibling))
        for w in range(n):
            hr = t[w].shape[1]
            theirs = full[w].at[pl.ds((1 - c) * hr, hr)]
            copy(w, 6, theirs, theirs, sibling).wait_recv()
        for cp in sends:
            cp.wait_send()

    half = lambda a: pltpu.VMEM((2, a.shape[1] // 2, a.shape[2]), a.dtype)
    whole = lambda a: pltpu.VMEM(a.shape[1:], a.dtype)
    return pl.pallas_call(
        body, name="rs_exchange_join",
        in_specs=[VMEM_WHOLE] * n, out_specs=[VMEM_WHOLE] * n,
        out_shape=[_sds((2 * a.shape[1], a.shape[2]), F32) for a in parts],
        scratch_shapes=[half(a) for a in parts] + [half(a) for a in parts] + [whole(a) for a in parts]
        + [whole(a) for a in parts] + [pltpu.SemaphoreType.DMA((n, 7)), pltpu.SemaphoreType.DMA((n, 7))],
        compiler_params=pltpu.CompilerParams(vmem_limit_bytes=VMEM_LIMIT),
    )(*parts)


def _small_allreduce(loss_p, dg_parts, dbg_a, dbg_c, dwc):
    ins = [loss_p] + list(dg_parts) + [dbg_a, dbg_c, dwc]
    n_in = len(ins)
    vmem = pl.BlockSpec(memory_space=pltpu.VMEM)

    def body(*refs):
        in_refs = refs[:n_in]
        out_ref, vec, buf, send_sems, recv_sems = refs[n_in:]
        x, y, c = _place()
        me = 4 * x + 2 * y + c
        vec[...] = jnp.zeros_like(vec)
        vec[0:1, :] = jnp.sum(in_refs[0][...], axis=0)
        for r in range(5):
            vec[1 + r:2 + r, :] = jnp.sum(in_refs[1 + r][...], axis=0)
        vec[6:7, :] = jnp.sum(in_refs[6][...], axis=0)
        vec[7:8, :] = jnp.sum(in_refs[7][...], axis=0)
        vec[8:16, 0:CONV_W] = jnp.sum(in_refs[8][...], axis=0)
        buf[pl.ds(me, 1)] = vec[...][None]
        copies = []
        for r in range(1, 8):
            fx, fy, fc = (r >> 2) & 1, (r >> 1) & 1, r & 1
            to = (1 - x if fx else x, 1 - y if fy else y, 1 - c if fc else c)
            cp = pltpu.make_async_remote_copy(src_ref=vec, dst_ref=buf.at[me], send_sem=send_sems.at[r - 1],
                                              recv_sem=recv_sems.at[r - 1], device_id=to, device_id_type=MESH)
            cp.start()
            copies.append(cp)
        for cp in copies:
            cp.wait()
        total = buf[0]
        for s in range(1, 8):
            total = total + buf[s]
        out_ref[...] = total
        out_ref[0:1, :] = jnp.broadcast_to(jnp.sum(total[0:1, :], axis=-1, keepdims=True), (1, D_MODEL))

    return pl.pallas_call(
        body, name="small_allreduce",
        in_specs=[vmem] * n_in, out_specs=vmem, out_shape=_sds((SMALL_ROWS, D_MODEL), F32),
        scratch_shapes=[pltpu.VMEM((SMALL_ROWS, D_MODEL), F32), pltpu.VMEM((8, SMALL_ROWS, D_MODEL), F32),
                        pltpu.SemaphoreType.DMA((7,)), pltpu.SemaphoreType.DMA((7,))],
    )(*ins)


def _local_step(x, p, tgt, g, b_gate, w_conv, wf):
    seq = x.shape[0]
    tm = min(seq, 1024)
    th = min(seq, 512)
    tl = min(seq, 2048)
    ni, nh, nl = seq // tm, seq // th, seq // tl
    g_pre_mix, g_post_mix, g_pre_mlp, g_post_mlp, g_ple = g
    w_in, w_ao, w_co, w_o, w_up, w_down, w_pg, w_pp, w_in_nat, w_up_nat = wf
    D = D_MODEL
    vec = lambda a, blk=0: (a, _bs((1, D), lambda i, j, k: (0, blk)))
    rows_i = lambda a, t, blk=0: (a, _bs((t, D), lambda i, j, k: (i, blk)))
    rows_k = lambda a, t, blk=0: (a, _bs((t, D), lambda i, j, k: (k, blk)))
    part = lambda n: (_sds((n, 1, D), F32), _bs((None, 1, D), lambda i, j, k: (i, 0, 0)))
    full2 = lambda a: (a, _bs(a.shape, lambda i, j, k: (0, 0)))

    normed = lambda xb, gb: (_rms(xb, gb).astype(BF16),) * 2
    keep_a = lambda t: [(_sds((seq, D), BF16), _bs((t, D), lambda i, j, k: (i, 0)))]
    main_w = D_IN - 2 * D
    proj, gates, h1 = _mm("proj_in", "nn", (nh, 1, 1),
                          a_ins=[rows_i(x, th), vec(g_pre_mix)], a_fn=normed,
                          b_ins=[full2(w_in_nat)], b_fn=_ident,
                          epi_fn=lambda acc: (acc[:, :main_w], acc[:, main_w:]),
                          outs=[(_sds((seq, main_w), F32), _bs((th, main_w), lambda i, j, k: (i, 0))),
                                (_sds((seq, 2 * D), BF16), _bs((th, 2 * D), lambda i, j, k: (i, 0)))],
                          acc_shape=(th, D_IN), a_cache=((th, D), BF16), a_outs=keep_a(th))
    o = _attn_fwd(proj, seq)
    e = _conv_fwd(proj, w_conv, seq, tm)

    def gate_values(ga, gc, ba, bc):
        return _sig(ga.astype(F32) + ba), _sig(gc.astype(F32) + bc)

    def branch_outputs(ob, eb, wao, wco):
        return _nn(ob, wao).astype(BF16).astype(F32), _nn(eb, wco).astype(BF16).astype(F32)

    def mix_fn(ga, gc, ob, eb, ba, bc, wao, wco):
        sa, sc = gate_values(ga, gc, ba, bc)
        ya, yc = branch_outputs(ob, eb, wao, wco)
        return ((sa * ya + sc * yc).astype(BF16),) * 2

    def post_mix(acc, xb, gb):
        return acc, xb + _rms(acc, gb)

    half_rows = lambda a: (a, _bs((th, a.shape[1]), lambda i, j, k: (i, 0)))
    mix_ins = [rows_i(gates, th, 0), rows_i(gates, th, 1), half_rows(o), half_rows(e), vec(b_gate, 0), vec(b_gate, 1),
               full2(w_ao), full2(w_co)]
    mixed, x1, mixin = _mm(
        "mix_out", "nn", (nh, 1, 1),
        a_ins=mix_ins, a_fn=mix_fn, b_ins=[full2(w_o)], b_fn=_ident,
        epi_ins=[rows_i(x, th), vec(g_post_mix)], epi_fn=post_mix,
        outs=[(_sds((seq, D), BF16), _bs((th, D), lambda i, j, k: (i, 0))),
              (_sds((seq, D), F32), _bs((th, D), lambda i, j, k: (i, 0)))],
        acc_shape=(th, D), a_cache=((th, D), BF16), a_outs=keep_a(th))
    up, h2 = _mm("mlp_up", "nn", (nh, 1, 1),
                 a_ins=[rows_i(x1, th), vec(g_pre_mlp)], a_fn=normed,
                 b_ins=[full2(w_up_nat)], b_fn=_ident,
                 outs=[(_sds((seq, D_FF), BF16), _bs((th, D_FF), lambda i, j, k: (i, 0)))],
                 acc_shape=(th, D_FF), a_cache=((th, D), BF16), a_outs=keep_a(th))

    def relu2(ub):
        r = jnp.maximum(ub.astype(F32), 0.0)
        return (r * r).astype(BF16)

    dx2, df, dpre, h3, dpp, loss_p, dg_ple_p, dg_post_mlp_p = _mlp_down_ple_head(
        up, x1, p, tgt, g_ple, g_post_mlp, w_down, w_pg, w_pp, seq, th)

    (dw_pp,) = _mm("dw_ple_proj", "tn", (1, 1, nh),
                   a_ins=[(p, _bs((th, PLE_DIM), lambda i, j, k: (k, 0)))], a_fn=_to_bf16,
                   b_ins=[rows_k(dpp, th)], b_fn=_ident,
                   outs=[(_sds((PLE_DIM, D), F32), _bs((PLE_DIM, D), lambda i, j, k: (0, 0)))],
                   acc_shape=(PLE_DIM, D))
    (dw_pg,) = _mm("dw_ple_gate", "tn", (1, 1, nl),
                   a_ins=[rows_k(h3, tl)], a_fn=_ident, b_ins=[rows_k(dpre, tl)], b_fn=_ident,
                   outs=[(_sds((D, D), F32), _bs((D, D), lambda i, j, k: (0, 0)))], acc_shape=(D, D))

    def dup_fn(acc, ub):
        return (acc * (2.0 * jnp.maximum(ub.astype(F32), 0.0)),)

    (dup,) = _mm("d_mlp_down", "nt", (nh, 1, 1),
                 a_ins=[rows_i(df, th)], a_fn=_ident, b_ins=[full2(w_down)], b_fn=_ident,
                 epi_ins=[(up, _bs((th, D_FF), lambda i, j, k: (i, 0)))], epi_fn=dup_fn,
                 outs=[(_sds((seq, D_FF), BF16), _bs((th, D_FF), lambda i, j, k: (i, 0)))],
                 acc_shape=(th, D_FF))
    (dw_down,) = _mm("dw_mlp_down", "tn", (4, 1, nl),
                     a_ins=[(up, _bs((tl, D), lambda i, j, k: (k, i)))], a_fn=relu2,
                     b_ins=[rows_k(df, tl)], b_fn=_ident,
                     outs=[(_sds((D_FF, D), F32), _bs((D, D), lambda i, j, k: (i, 0)))], acc_shape=(D, D))
    (dw_up,) = _mm("dw_mlp_up", "tn", (1, 4, nl),
                   a_ins=[rows_k(h2, tl)], a_fn=_ident,
                   b_ins=[(dup, _bs((tl, D), lambda i, j, k: (k, j)))], b_fn=_ident,
                   outs=[(_sds((N_CHIPS, D, D), F32), _bs((None, D, D), lambda i, j, k: (j, 0, 0)))],
                   acc_shape=(D, D))

    def mlp_norm_bwd(acc, x1b, dx2b, mixedb, g_mlp, g_mix):
        dxn, dg_mlp = _rms_bwd(x1b, g_mlp, acc)
        dx1b = dx2b + dxn
        dmixedb, dg_mix = _rms_bwd(mixedb.astype(F32), g_mix, dx1b)
        return dx1b, dmixedb, dg_mlp, dg_mix

    dx1, dmixed, dg_pre_mlp_p, dg_post_mix_p = _mm(
        "d_mlp_up", "nt", (nh, 1, 1),
        a_ins=[(dup, _bs((th, D_FF), lambda i, j, k: (i, 0)))], a_fn=_ident,
        b_ins=[full2(w_up_nat)], b_fn=_ident,
        epi_ins=[rows_i(x1, th), rows_i(dx2, th), rows_i(mixed, th), vec(g_pre_mlp), vec(g_post_mix)],
        epi_fn=mlp_norm_bwd,
        outs=[(_sds((seq, D), F32), _bs((th, D), lambda i, j, k: (i, 0))),
              (_sds((seq, D), BF16), _bs((th, D), lambda i, j, k: (i, 0))), part(nh), part(nh)],
        acc_shape=(th, D))
    (dw_o,) = _mm("dw_mix_out", "tn", (1, 1, nl),
                  a_ins=[rows_k(mixin, tl)], a_fn=_ident, b_ins=[rows_k(dmixed, tl)], b_fn=_ident,
                  outs=[(_sds((D, D), F32), _bs((D, D), lambda i, j, k: (0, 0)))], acc_shape=(D, D))

    def gate_bwd(acc, ga, gc, ob, eb, ba, bc, wao, wco):
        sa, sc = gate_values(ga, gc, ba, bc)
        ya, yc = branch_outputs(ob, eb, wao, wco)
        dga = acc * ya * sa * (1.0 - sa)
        dgc = acc * yc * sc * (1.0 - sc)
        dya, dyc = (acc * sa).astype(BF16), (acc * sc).astype(BF16)
        return (dya, dyc, jnp.concatenate([dga, dgc], axis=1), _nt(dya, wao), _nt(dyc, wco),
                jnp.sum(dga, axis=0, keepdims=True), jnp.sum(dgc, axis=0, keepdims=True))

    dya, dyc, dgate, do, de, dbg_a_p, dbg_c_p = _mm(
        "d_mix_out", "nt", (nh, 1, 1),
        a_ins=[rows_i(dmixed, th)], a_fn=_ident, b_ins=[full2(w_o)], b_fn=_ident,
        epi_ins=mix_ins, epi_fn=gate_bwd,
        outs=[(_sds((seq, D), BF16), _bs((th, D), lambda i, j, k: (i, 0)))] * 2
             + [(_sds((seq, 2 * D), BF16), _bs((th, 2 * D), lambda i, j, k: (i, 0))),
                (_sds((seq, ATTN_W), BF16), _bs((th, ATTN_W), lambda i, j, k: (i, 0))),
                (_sds((seq, CONV_W), F32), _bs((th, CONV_W), lambda i, j, k: (i, 0))), part(nh), part(nh)],
        acc_shape=(th, D))
    (dw_ao,) = _mm("dw_attn_out", "tn", (1, 1, nh),
                   a_ins=[(o, _bs((th, ATTN_W), lambda i, j, k: (k, 0)))], a_fn=_ident,
                   b_ins=[rows_k(dya, th)], b_fn=_ident,
                   outs=[(_sds((ATTN_W, D), F32), _bs((ATTN_W, D), lambda i, j, k: (0, 0)))], acc_shape=(ATTN_W, D))
    dq, dk, dv = _attn_bwd(proj, do, seq)
    (dw_co,) = _mm("dw_conv_out", "tn", (1, 1, nh),
                   a_ins=[(e, _bs((th, CONV_W), lambda i, j, k: (k, 0)))], a_fn=_ident,
                   b_ins=[rows_k(dyc, th)], b_fn=_ident,
                   outs=[(_sds((CONV_W, D), F32), _bs((CONV_W, D), lambda i, j, k: (0, 0)))], acc_shape=(CONV_W, D))
    dconv, dwc_p = _conv_bwd(proj, de, w_conv, seq, tm)
    qkv_w = 3 * ATTN_W
    join_bf16 = lambda *blocks: jnp.concatenate([b.astype(BF16) for b in blocks], axis=1)
    piece = lambda a, t, rows, blk=0: (a, _bs((t, a.shape[1]), (lambda i, j, k: (k, blk)) if rows == "k"
                                             else (lambda i, j, k: (i, blk))))
    (dw_in_qkv,) = _mm("dw_proj_in_qkv", "tn", (1, 1, ni),
                       a_ins=[rows_k(h1, tm)], a_fn=_ident,
                       b_ins=[piece(dq, tm, "k"), piece(dk, tm, "k"), piece(dv, tm, "k")], b_fn=join_bf16,
                       outs=[(_sds((D, qkv_w), F32), _bs((D, qkv_w), lambda i, j, k: (0, 0)))], acc_shape=(D, qkv_w))
    (dw_in_conv,) = _mm("dw_proj_in_conv", "tn", (1, 1, nl),
                        a_ins=[rows_k(h1, tl)], a_fn=_ident, b_ins=[piece(dconv, tl, "k")], b_fn=_ident,
                        outs=[(_sds((D, 3 * CONV_W), F32), _bs((D, 3 * CONV_W), lambda i, j, k: (0, 0)))],
                        acc_shape=(D, 3 * CONV_W))
    (dw_in_gate,) = _mm("dw_proj_in_gate", "tn", (1, 2, nl),
                        a_ins=[rows_k(h1, tl)], a_fn=_ident,
                        b_ins=[(dgate, _bs((tl, D), lambda i, j, k: (k, j)))], b_fn=_ident,
                        outs=[(_sds((D, 2 * D), F32), _bs((D, D), lambda i, j, k: (0, j)))], acc_shape=(D, D))
    dw_in = jnp.concatenate([dw_in_qkv, dw_in_conv, dw_in_gate], axis=1)

    def in_norm_bwd(acc, xb, dx1b, gb):
        dxn, dg = _rms_bwd(xb, gb, acc)
        return dx1b + dxn, dg

    grad_x, dg_pre_mix_p = _mm("d_proj_in", "nt", (nh, 1, 1),
                               a_ins=[piece(dq, th, "i"), piece(dk, th, "i"), piece(dv, th, "i"),
                                      piece(dconv, th, "i"), piece(dgate, th, "i")], a_fn=join_bf16,
                               b_ins=[full2(w_in_nat)], b_fn=_ident,
                               epi_ins=[rows_i(x, th), rows_i(dx1, th), vec(g_pre_mix)], epi_fn=in_norm_bwd,
                               outs=[(_sds((seq, D), F32), _bs((th, D), lambda i, j, k: (i, 0))), part(nh)],
                               acc_shape=(th, D))

    chip_major = lambda a: a.reshape(a.shape[0], N_CHIPS, a.shape[1] // N_CHIPS).transpose(1, 0, 2)
    big = [chip_major(dw_in), chip_major(dw_ao), chip_major(dw_co), dw_o.reshape(N_CHIPS, D // N_CHIPS, D), dw_up,
           dw_down.reshape(N_CHIPS, D_FF // N_CHIPS, D), dw_pg.reshape(N_CHIPS, D // N_CHIPS, D), chip_major(dw_pp)]
    small = (loss_p, [dg_pre_mix_p, dg_post_mix_p, dg_pre_mlp_p, dg_post_mlp_p, dg_ple_p], dbg_a_p, dbg_c_p, dwc_p)
    return grad_x, big, small


RS_GROUPS = ((0,), (4,), (5,), (1, 2, 3, 6, 7))


def _reduce_scatter(big):
    pair = [None] * len(big)
    for gi, group in enumerate(RS_GROUPS):
        for w, s in zip(group, _rs_pair_sum(f"rs_pair_sum_{gi}", [big[w] for w in group])):
            pair[w] = s
    return _rs_exchange_join(pair)


def kernel(x, p, g_pre_mix, w_in, b_gate, w_conv, w_attn_out, w_conv_out, w_o, g_post_mix, g_pre_mlp, w_up, w_down, g_post_mlp, g_ple, w_ple_gate, w_ple_proj, loss_target, m_g_pre_mix, m_w_in, m_b_gate, m_w_conv, m_w_attn_out, m_w_conv_out, m_w_o, m_g_post_mix, m_g_pre_mlp, m_w_up, m_w_down, m_g_post_mlp, m_g_ple, m_w_ple_gate, m_w_ple_proj, v_g_pre_mix, v_w_in, v_b_gate, v_w_conv, v_w_attn_out, v_w_conv_out, v_w_o, v_g_post_mix, v_g_pre_mlp, v_w_up, v_w_down, v_g_post_mlp, v_g_ple, v_w_ple_gate, v_w_ple_proj):
    mats = [w_in, w_attn_out, w_conv_out, w_o, w_up, w_down, w_ple_gate, w_ple_proj]
    mats_m = [m_w_in, m_w_attn_out, m_w_conv_out, m_w_o, m_w_up, m_w_down, m_w_ple_gate, m_w_ple_proj]
    mats_v = [v_w_in, v_w_attn_out, v_w_conv_out, v_w_o, v_w_up, v_w_down, v_w_ple_gate, v_w_ple_proj]
    gains = [g_pre_mix, g_post_mix, g_pre_mlp, g_post_mlp, g_ple]
    gains_m = [m_g_pre_mix, m_g_post_mix, m_g_pre_mlp, m_g_post_mlp, m_g_ple]
    gains_v = [v_g_pre_mix, v_g_post_mix, v_g_pre_mlp, v_g_post_mlp, v_g_ple]

    taps = jnp.concatenate([w_conv[0], jnp.zeros((CONV_PAD_ROWS - 3, LANES), F32)], axis=0)
    gathered = _allgather_weights([w[0].astype(BF16) for w in mats] + [taps])
    cols_joined = lambda a: a.transpose(1, 0, 2).reshape(a.shape[1], N_CHIPS * a.shape[2])
    rows_joined = lambda a: a.reshape(N_CHIPS * a.shape[1], a.shape[2])
    wf = [gathered[0], cols_joined(gathered[1]), cols_joined(gathered[2]), rows_joined(gathered[3]), gathered[4],
          rows_joined(gathered[5]), rows_joined(gathered[6]), cols_joined(gathered[7]),
          cols_joined(gathered[0]), cols_joined(gathered[4])]
    w_conv_full = cols_joined(gathered[8])[0:3, :]
    chip = 2 * lax.axis_index("x") + lax.axis_index("y")

    grad_x, big, small = _local_step(x[0], p[0, 0], loss_target[0], gains, b_gate, w_conv_full, wf)

    shard_grads = _reduce_scatter(big)
    red = _small_allreduce(*small)
    loss = red[0, 0]
    grad_gains = [red[1 + r:2 + r, :] for r in range(5)]
    grad_b_gate = jnp.concatenate([red[6:7, :], red[7:8, :]], axis=1)
    grad_w_conv = lax.dynamic_slice(red[8:11, :], (0, chip * LANES), (3, LANES))[None]

    grads_big = [gr.reshape(w.shape) for gr, w in zip(shard_grads, mats)]
    upd_big = [_adamw(f"adamw_{i}", w, gr, m, v) for i, (w, gr, m, v) in enumerate(zip(mats, grads_big, mats_m, mats_v))]
    pack = lambda vs, bg: jnp.concatenate(list(vs) + [bg.reshape(2, D_MODEL), jnp.zeros((1, D_MODEL), F32)], axis=0)
    upd_small = _adamw("adamw_small", pack(gains, b_gate), pack(grad_gains, grad_b_gate),
                       pack(gains_m, m_b_gate), pack(gains_v, v_b_gate))
    upd_conv = _adamw("adamw_conv", w_conv, grad_w_conv, m_w_conv, v_w_conv)

    def small_out(a, which):
        gains_out = [a[r:r + 1, :] for r in range(5)]
        return gains_out, a[5:7, :].reshape(1, 2 * D_MODEL)

    def ordered(g_pre_mix_, big_, b_gate_, conv_, g_rest):
        return [g_pre_mix_, big_[0], b_gate_, conv_, big_[1], big_[2], big_[3], g_rest[0], g_rest[1], big_[4], big_[5],
                g_rest[2], g_rest[3], big_[6], big_[7]]

    outs = [loss, grad_x[None]]
    outs += ordered(grad_gains[0], grads_big, grad_b_gate, grad_w_conv, grad_gains[1:])
    for which in range(3):
        g_out, b_out = small_out(upd_small[which], which)
        outs += ordered(g_out[0], [u[which] for u in upd_big], b_out, upd_conv[which], g_out[1:])
    return tuple(outs)
```

```python
import jax
import jax.numpy as jnp
from jax import lax
from jax.experimental import pallas as pl
from jax.experimental.pallas import tpu as pltpu

F32 = jnp.float32
BF16 = jnp.bfloat16
MESH = pl.DeviceIdType.MESH

D_MODEL = 1024
N_HEADS = 8
HEAD_DIM = 64
ATTN_W = N_HEADS * HEAD_DIM
CONV_W = 512
D_FF = 4096
PLE_DIM = 256
D_IN = 5120
N_CHIPS = 4
EPS = 1e-6
Q_SCALE = HEAD_DIM ** -0.5

ADAM_LR = 0.001
ADAM_B1 = 0.9
ADAM_B2 = 0.999
ADAM_EPS = 1e-08
ADAM_WD = 0.01
ADAM_STEP = 10

V7X_VMEM_BYTES = 64 * 1024 * 1024
VMEM_LIMIT = V7X_VMEM_BYTES - 8 * 1024 * 1024
LANES = 128
ATT_BLK = 256
SMALL_ROWS = 16
CONV_PAD_ROWS = 16


def _cparams(n_grid):
    return pltpu.CompilerParams(dimension_semantics=("arbitrary",) * n_grid, vmem_limit_bytes=VMEM_LIMIT)


def _bs(shape, fn):
    return pl.BlockSpec(shape, fn)


def _rms_stats(xf):
    return lax.rsqrt(jnp.mean(xf * xf, axis=-1, keepdims=True) + EPS)


def _rms(xf, g):
    return xf * _rms_stats(xf) * g


def _rms_bwd(xf, g, dy):
    r = _rms_stats(xf)
    xh = xf * r
    dyg = dy * g
    dx = r * (dyg - xh * jnp.mean(dyg * xh, axis=-1, keepdims=True))
    return dx, jnp.sum(dy * xh, axis=0, keepdims=True)


def _sig(z):
    return 1.0 / (1.0 + jnp.exp(-z))


def _ident(a):
    return a


def _to_bf16(a):
    return a.astype(BF16)


_DIMS = {"nn": (((1,), (0,)), ((), ())), "nt": (((1,), (1,)), ((), ())), "tn": (((0,), (0,)), ((), ()))}


def _mm(name, mode, grid, a_ins, a_fn, b_ins, b_fn, outs, acc_shape, epi_ins=(), epi_fn=None,
        a_cache=None, a_outs=(), epi_a=()):
    nk = grid[2]
    na, nb, ne, no, nao = len(a_ins), len(b_ins), len(epi_ins), len(outs), len(a_outs)
    assert a_cache is None or nk == 1
    assert not a_outs or a_cache is not None
    dims = _DIMS[mode]
    if epi_fn is None:
        epi_fn = lambda acc: (acc,)

    def body(*refs):
        a_refs = refs[:na]
        b_refs = refs[na:na + nb]
        e_refs = refs[na + nb:na + nb + ne]
        o_refs = refs[na + nb + ne:na + nb + ne + no]
        ao_refs = refs[na + nb + ne + no:na + nb + ne + no + nao]
        scratch = list(refs[na + nb + ne + no + nao:])
        acc_ref = scratch.pop(0) if nk > 1 else None
        a_sc = scratch.pop(0) if a_cache is not None else None
        j = pl.program_id(1)
        k = pl.program_id(2)

        def finish(acc):
            res = epi_fn(acc, *[a_refs[t][...] for t in epi_a], *[r[...] for r in e_refs])
            for r, val in zip(o_refs, res):
                r[...] = val.astype(r.dtype)

        if a_sc is not None:
            @pl.when(j == 0)
            def _():
                res = a_fn(*[r[...] for r in a_refs])
                if nao:
                    for r, val in zip(ao_refs, res[1:]):
                        r[...] = val.astype(r.dtype)
                    res = res[0]
                a_sc[...] = res
            a = a_sc[...]
        else:
            a = a_fn(*[r[...] for r in a_refs])
        b = b_fn(*[r[...] for r in b_refs])
        prod = lax.dot_general(a, b, dims, preferred_element_type=F32)
        if nk == 1:
            finish(prod)
        else:
            @pl.when(k == 0)
            def _():
                acc_ref[...] = prod

            @pl.when(k > 0)
            def _():
                acc_ref[...] += prod

            @pl.when(k == nk - 1)
            def _():
                finish(acc_ref[...])

    scratch_shapes = []
    if nk > 1:
        scratch_shapes.append(pltpu.VMEM(acc_shape, F32))
    if a_cache is not None:
        scratch_shapes.append(pltpu.VMEM(*a_cache))
    all_outs = list(outs) + list(a_outs)
    res = pl.pallas_call(
        body, name=name, grid=grid,
        in_specs=[s for _, s in a_ins] + [s for _, s in b_ins] + [s for _, s in epi_ins],
        out_specs=[s for _, s in all_outs],
        out_shape=[o for o, _ in all_outs],
        scratch_shapes=scratch_shapes,
        compiler_params=_cparams(3),
    )(*[a for a, _ in a_ins], *[a for a, _ in b_ins], *[a for a, _ in epi_ins])
    return res


def _sds(shape, dtype):
    return jax.ShapeDtypeStruct(shape, dtype)


def _nt(a, b):
    return lax.dot_general(a, b, _DIMS["nt"], preferred_element_type=F32)


def _tn(a, b):
    return lax.dot_general(a, b, _DIMS["tn"], preferred_element_type=F32)


def _nn(a, b):
    return lax.dot_general(a, b, _DIMS["nn"], preferred_element_type=F32)


HEAD_PARTS = 2


def _mlp_down_ple_head(up, x1, p, tgt, g_ple, g_post_mlp, w_down, w_pg, w_pp, seq, tr):
    nblk = seq // tr
    D = D_MODEL

    def body(up_ref, x1_ref, p_ref, t_ref, gp_ref, gm_ref, wd_ref, wpg_ref, wpp_ref,
             dx2_ref, df_ref, dpre_ref, h3_ref, dpp_ref, loss_ref, dgp_ref, dgm_ref):
        gp, gm, wpg, wpp = gp_ref[...], gm_ref[...], wpg_ref[...], wpp_ref[...]
        halves = [pl.ds(n * (tr // HEAD_PARTS), tr // HEAD_PARTS) for n in range(HEAD_PARTS)]
        w_down = wd_ref[...]
        fb = []
        for r in halves:
            hidden = jnp.maximum(up_ref[r, :].astype(F32), 0.0)
            fb.append(_nn((hidden * hidden).astype(BF16), w_down))
        loss, dgp_sum, dgm_sum = 0.0, 0.0, 0.0
        for s, r in enumerate(halves):
            x2b = x1_ref[r, :] + _rms(fb[s], gm)
            h3 = _rms(x2b, gp).astype(BF16)
            gate = _sig(_nn(h3, wpg))
            pp = _nn(p_ref[r, :].astype(BF16), wpp)
            err = x2b + gate * pp - t_ref[r, :]
            dx3 = err * (1.0 / D)
            dpre = (dx3 * pp * gate * (1.0 - gate)).astype(BF16)
            h3_ref[r, :] = h3
            dpp_ref[r, :] = (dx3 * gate).astype(BF16)
            dpre_ref[r, :] = dpre
            dxn, dgp = _rms_bwd(x2b, gp, _nt(dpre, wpg))
            dx2 = dx3 + dxn
            dx2_ref[r, :] = dx2
            dfb, dgm = _rms_bwd(fb[s], gm, dx2)
            df_ref[r, :] = dfb.astype(BF16)
            loss = loss + jnp.sum(err * err, axis=0, keepdims=True)
            dgp_sum, dgm_sum = dgp_sum + dgp, dgm_sum + dgm
        loss_ref[...] = loss * (0.5 / D)
        dgp_ref[...] = dgp_sum
        dgm_ref[...] = dgm_sum

    rows = _bs((tr, D), lambda i: (i, 0))
    vec = _bs((1, D), lambda i: (0, 0))
    part = _bs((None, 1, D), lambda i: (i, 0, 0))
    return pl.pallas_call(
        body, name="mlp_down_ple_head", grid=(nblk,),
        in_specs=[_bs((tr, D_FF), lambda i: (i, 0)), rows, _bs((tr, PLE_DIM), lambda i: (i, 0)), rows, vec, vec,
                  _bs((D_FF, D), lambda i: (0, 0)), _bs((D, D), lambda i: (0, 0)), _bs((PLE_DIM, D), lambda i: (0, 0))],
        out_specs=[rows] * 5 + [part] * 3,
        out_shape=[_sds((seq, D), F32)] + [_sds((seq, D), BF16)] * 4 + [_sds((nblk, 1, D), F32)] * 3,
        compiler_params=_cparams(1),
    )(up, x1, p, tgt, g_ple, g_post_mlp, w_down, w_pg, w_pp)


def _shift_rows_down(u, prev, n):
    rows = u.shape[0]
    ridx = lax.broadcasted_iota(jnp.int32, u.shape, 0)
    out = pltpu.roll(u, n, 0)
    for r in range(n):
        out = jnp.where(ridx == r, prev[8 - n + r:8 - n + r + 1, :], out)
    del rows
    return out


def _shift_rows_up(u, nxt, n):
    rows = u.shape[0]
    ridx = lax.broadcasted_iota(jnp.int32, u.shape, 0)
    out = pltpu.roll(u, rows - n, 0)
    for r in range(n):
        out = jnp.where(ridx == rows - n + r, nxt[r:r + 1, :], out)
    return out


CONV_COL0 = 3


def _conv_fwd(proj, w_conv, seq, tr):
    hb = tr // 8

    def body(cb_ref, cc_ref, cu_ref, ccp_ref, cup_ref, w_ref, e_ref):
        i = pl.program_id(0)
        u = cc_ref[...] * cu_ref[...]
        up = jnp.where(i > 0, ccp_ref[...] * cup_ref[...], 0.0)
        w = w_ref[...]
        d = w[0:1, :] * _shift_rows_down(u, up, 2) + w[1:2, :] * _shift_rows_down(u, up, 1) + w[2:3, :] * u
        e_ref[...] = (cb_ref[...] * d).astype(BF16)

    prev = lambda c: (lambda i: (jnp.maximum(i * hb - 1, 0), c))
    return pl.pallas_call(
        body, name="conv_fwd", grid=(seq // tr,),
        in_specs=[_bs((tr, CONV_W), lambda i: (i, CONV_COL0)),
                  _bs((tr, CONV_W), lambda i: (i, CONV_COL0 + 1)),
                  _bs((tr, CONV_W), lambda i: (i, CONV_COL0 + 2)),
                  _bs((8, CONV_W), prev(CONV_COL0 + 1)),
                  _bs((8, CONV_W), prev(CONV_COL0 + 2)),
                  _bs((3, CONV_W), lambda i: (0, 0))],
        out_specs=_bs((tr, CONV_W), lambda i: (i, 0)),
        out_shape=_sds((seq, CONV_W), BF16),
        compiler_params=_cparams(1),
    )(proj, proj, proj, proj, proj, w_conv)


def _conv_bwd(proj, de, w_conv, seq, tr):
    hb = tr // 8
    nblk = seq // tr

    def body(cb_ref, cc_ref, cu_ref, ccp_ref, cup_ref, cbn_ref, de_ref, den_ref, w_ref, o_ref, dw_ref):
        i = pl.program_id(0)
        cc, cu, cb = cc_ref[...], cu_ref[...], cb_ref[...]
        u = cc * cu
        up = jnp.where(i > 0, ccp_ref[...] * cup_ref[...], 0.0)
        u1 = _shift_rows_down(u, up, 1)
        u2 = _shift_rows_down(u, up, 2)
        de_ = de_ref[...]
        dd = de_ * cb
        ddn = jnp.where(i < nblk - 1, den_ref[...] * cbn_ref[...], 0.0)
        w = w_ref[...]
        du = w[2:3, :] * dd + w[1:2, :] * _shift_rows_up(dd, ddn, 1) + w[0:1, :] * _shift_rows_up(dd, ddn, 2)
        o_ref[:, 0:CONV_W] = (de_ * (w[0:1, :] * u2 + w[1:2, :] * u1 + w[2:3, :] * u)).astype(BF16)
        o_ref[:, CONV_W:2 * CONV_W] = (du * cu).astype(BF16)
        o_ref[:, 2 * CONV_W:3 * CONV_W] = (du * cc).astype(BF16)
        ridx = lax.broadcasted_iota(jnp.int32, (8, CONV_W), 0)
        dw0 = jnp.sum(dd * u2, axis=0, keepdims=True)
        dw1 = jnp.sum(dd * u1, axis=0, keepdims=True)
        dw2 = jnp.sum(dd * u, axis=0, keepdims=True)
        dw_ref[...] = jnp.where(ridx == 0, dw0, jnp.where(ridx == 1, dw1, jnp.where(ridx == 2, dw2, 0.0)))

    prev = lambda c: (lambda i: (jnp.maximum(i * hb - 1, 0), c))
    nxt = lambda c: (lambda i: (jnp.minimum((i + 1) * hb, seq // 8 - 1), c))
    return pl.pallas_call(
        body, name="conv_bwd", grid=(nblk,),
        in_specs=[_bs((tr, CONV_W), lambda i: (i, CONV_COL0)),
                  _bs((tr, CONV_W), lambda i: (i, CONV_COL0 + 1)),
                  _bs((tr, CONV_W), lambda i: (i, CONV_COL0 + 2)),
                  _bs((8, CONV_W), prev(CONV_COL0 + 1)),
                  _bs((8, CONV_W), prev(CONV_COL0 + 2)),
                  _bs((8, CONV_W), nxt(CONV_COL0)),
                  _bs((tr, CONV_W), lambda i: (i, 0)),
                  _bs((8, CONV_W), nxt(0)),
                  _bs((3, CONV_W), lambda i: (0, 0))],
        out_specs=[_bs((tr, 3 * CONV_W), lambda i: (i, 0)), _bs((None, 8, CONV_W), lambda i: (i, 0, 0))],
        out_shape=[_sds((seq, 3 * CONV_W), BF16), _sds((nblk, 8, CONV_W), F32)],
        compiler_params=_cparams(1),
    )(proj, proj, proj, proj, proj, proj, de, de, w_conv)


def _log_gates(z):
    lse = jnp.log(1.0 + jnp.exp(-jnp.abs(z)))
    log_beta = jnp.minimum(z, 0.0) - lse
    return log_beta, log_beta - z


DEAD_LOG_WEIGHT = -110.0
NO_TILE = -1e30


def _first_live_tile(start, scores, live_sc):
    def alive():
        return jnp.max(jnp.maximum(live_sc[0], live_sc[1])) > DEAD_LOG_WEIGHT

    def step(c):
        for h, z in enumerate(scores(c[0])):
            live_sc[h] = live_sc[h] + jnp.sum(_log_gates(z)[1], axis=-1, keepdims=True)
        return c[0] - 1, alive()

    j_end, _ = lax.while_loop(lambda c: jnp.logical_and(c[0] >= 0, c[1]), step, (start, alive()))
    return j_end + 1


def _attn_fwd(proj, seq):
    blk = ATT_BLK
    nq = seq // blk
    npair = N_HEADS // 2

    def body(q_ref, k_ref, v_ref, o_ref, z0_sc, z1_sc, w0_sc, w1_sc, tot_sc, acc_sc):
        i = pl.program_id(1)
        is_a = lax.broadcasted_iota(jnp.int32, (1, LANES), 1) < HEAD_DIM
        q2 = (q_ref[...] * Q_SCALE).astype(BF16)
        zero = jnp.zeros_like(q2)
        qs = (jnp.where(is_a, q2, zero), jnp.where(is_a, zero, q2))
        row = lax.broadcasted_iota(jnp.int32, (blk, blk), 0)
        col = lax.broadcasted_iota(jnp.int32, (blk, blk), 1)
        tri = (row > col).astype(BF16)
        causal = col < row

        def tile_of(ref, j):
            return ref[pl.ds(pl.multiple_of(j * blk, blk), blk), :].astype(BF16)

        def scores(j):
            k2 = tile_of(k_ref, j)
            return [_nt(qs[h], k2) for h in range(2)]

        has_left = i > 0
        left = jnp.maximum(i - 1, 0)

        g_d = [_log_gates(z) for z in scores(i)]
        g_l = [_log_gates(z) for z in scores(left)]
        keep_d = [jnp.where(causal, g[1], 0.0) for g in g_d]
        suf_d = [_nn(lk.astype(BF16), tri) for lk in keep_d]
        suf_l = [_nn(g[1].astype(BF16), tri) for g in g_l]
        v_d, v_l = tile_of(v_ref, i), tile_of(v_ref, left)
        pv = []
        for h in range(2):
            sum_d = jnp.sum(keep_d[h], axis=-1, keepdims=True)
            w_d = jnp.where(causal, jnp.exp(g_d[h][0] + suf_d[h]), 0.0)
            w_l = jnp.exp(g_l[h][0] + (jnp.where(has_left, sum_d, NO_TILE) + suf_l[h]))
            pv.append(_nn(w_d.astype(BF16), v_d) + _nn(w_l.astype(BF16), v_l))
            tot_sc[h] = sum_d + jnp.sum(g_l[h][1], axis=-1, keepdims=True)
        acc_sc[...] = jnp.where(is_a, pv[0], pv[1])

        z_bufs, w_bufs = (z0_sc, z1_sc), (w0_sc, w1_sc)

        def alive():
            return jnp.max(jnp.maximum(tot_sc[0], tot_sc[1])) > DEAD_LOG_WEIGHT

        def put(ref, vals):
            for h in range(2):
                ref[h] = vals[h]

        def weights(zs):
            gates = [_log_gates(z) for z in zs]
            sums = [_nn(g[1].astype(BF16), tri) for g in gates]
            ws = []
            for h in range(2):
                ws.append(jnp.exp(gates[h][0] + (tot_sc[h] + sums[h])).astype(BF16))
                tot_sc[h] = tot_sc[h] + jnp.sum(gates[h][1], axis=-1, keepdims=True)
            return ws

        def add_values(w_buf, j):
            v2 = tile_of(v_ref, j)
            acc_sc[...] += jnp.where(is_a, _nn(w_buf[0], v2), _nn(w_buf[1], v2))

        def trip(j, s):
            add_values(w_bufs[s], j + 1)
            put(z_bufs[1 - s], scores(jnp.maximum(j - 1, 0)))
            put(w_bufs[1 - s], weights((z_bufs[s][0], z_bufs[s][1])))

        @pl.when(jnp.logical_and(i >= 2, alive()))
        def _():
            put(z0_sc, scores(i - 2))
            w0_sc[...] = jnp.zeros_like(w0_sc)

            def two_trips(c):
                trip(c[0], 0)
                trip(c[0] - 1, 1)
                return c[0] - 2, alive()

            j_next, still = lax.while_loop(lambda c: jnp.logical_and(c[0] >= 1, c[1]), two_trips, (i - 2, i >= 2))
            one_left = jnp.logical_and(j_next == 0, still)

            @pl.when(one_left)
            def _():
                trip(0, 0)
                add_values(w1_sc, 0)

            @pl.when(jnp.logical_not(one_left))
            def _():
                add_values(w0_sc, j_next + 1)

        o_ref[...] = acc_sc[...].astype(BF16)

    return pl.pallas_call(
        body, name="attn_fwd", grid=(npair, nq),
        in_specs=[_bs((blk, LANES), lambda p, i: (i, p)),
                  _bs((seq, LANES), lambda p, i: (0, npair + p)),
                  _bs((seq, LANES), lambda p, i: (0, 2 * npair + p))],
        out_specs=_bs((blk, LANES), lambda p, i: (i, p)),
        out_shape=_sds((seq, ATTN_W), BF16),
        scratch_shapes=[pltpu.VMEM((2, blk, blk), F32), pltpu.VMEM((2, blk, blk), F32),
                        pltpu.VMEM((2, blk, blk), BF16), pltpu.VMEM((2, blk, blk), BF16),
                        pltpu.VMEM((2, blk, 1), F32), pltpu.VMEM((blk, LANES), F32)],
        compiler_params=_cparams(2),
    )(proj, proj, proj)


def _attn_bwd(proj, do, seq):
    blk = ATT_BLK
    nq = seq // blk
    npair = N_HEADS // 2

    def body(q_ref, k_ref, v_ref, do_ref, dq_ref, dk_ref, dv_ref,
             prod0_sc, prod1_sc, pend0_sc, pend1_sc, tot_sc, live_sc, cum_sc, pre_sc, dq_sc):
        i = pl.program_id(1)

        @pl.when(i == 0)
        def _():
            dk_ref[...] = jnp.zeros_like(dk_ref)
            dv_ref[...] = jnp.zeros_like(dv_ref)

        is_a = lax.broadcasted_iota(jnp.int32, (1, LANES), 1) < HEAD_DIM
        q2 = (q_ref[...] * Q_SCALE).astype(BF16)
        do2 = do_ref[...]
        zero = jnp.zeros_like(q2)
        qs = (jnp.where(is_a, q2, zero), jnp.where(is_a, zero, q2))
        dos = (jnp.where(is_a, do2, zero), jnp.where(is_a, zero, do2))
        row = lax.broadcasted_iota(jnp.int32, (blk, blk), 0)
        col = lax.broadcasted_iota(jnp.int32, (blk, blk), 1)
        tri_after = (row > col).astype(BF16)
        tri_excl = (row < col).astype(BF16)
        causal = col < row

        def tile_of(ref, j):
            return ref[pl.ds(pl.multiple_of(j * blk, blk), blk), :].astype(BF16)

        def scores(j):
            k2 = tile_of(k_ref, j)
            return [_nt(qs[h], k2) for h in range(2)]

        def products(j):
            v2 = tile_of(v_ref, j)
            return scores(j) + [_nt(dos[h], v2) for h in range(2)]

        def row_sum(a):
            return jnp.sum(a, axis=-1, keepdims=True)

        def grad_matmuls(ws, dzs, j):
            rows = pl.ds(pl.multiple_of(j * blk, blk), blk)
            k2 = tile_of(k_ref, j)
            dq_sc[...] += jnp.where(is_a, _nn(dzs[0], k2), _nn(dzs[1], k2))
            dk_ref[rows, :] += jnp.where(is_a, _tn(dzs[0], q2), _tn(dzs[1], q2))
            if ws is not None:
                dv_ref[rows, :] += jnp.where(is_a, _tn(ws[0], do2), _tn(ws[1], do2))

        has_left = i > 0
        left = jnp.maximum(i - 1, 0)

        p_d, p_l = products(i), products(left)
        g_d = [_log_gates(z) for z in p_d[:2]]
        g_l = [_log_gates(z) for z in p_l[:2]]
        keep_d = [jnp.where(causal, g[1], 0.0) for g in g_d]
        suf_d = [_nn(lk.astype(BF16), tri_after) for lk in keep_d]
        suf_l = [_nn(g[1].astype(BF16), tri_after) for g in g_l]
        w_d, w_l, gg_d, gg_l = [], [], [], []
        for h in range(2):
            sum_d = row_sum(keep_d[h])
            w_d.append(jnp.where(causal, jnp.exp(g_d[h][0] + suf_d[h]), 0.0))
            w_l.append(jnp.exp(g_l[h][0] + (jnp.where(has_left, sum_d, NO_TILE) + suf_l[h])))
            gg_d.append(p_d[2 + h] * w_d[h])
            gg_l.append(p_l[2 + h] * w_l[h])
            tot_sc[h] = sum_d + row_sum(g_l[h][1])
        before_d = [_nn(g.astype(BF16), tri_excl) for g in gg_d]
        before_l = [_nn(g.astype(BF16), tri_excl) for g in gg_l]
        dz_d, dz_l = [], []
        for h in range(2):
            beta_d, beta_l = jnp.exp(g_d[h][0]), jnp.exp(g_l[h][0])
            dz_l.append((gg_l[h] * (1.0 - beta_l) - before_l[h] * beta_l).astype(BF16))
            dz = gg_d[h] * (1.0 - beta_d) - (row_sum(gg_l[h]) + before_d[h]) * beta_d
            dz_d.append(jnp.where(causal, dz, 0.0).astype(BF16))
        dq_sc[...] = jnp.zeros_like(dq_sc)
        grad_matmuls([w.astype(BF16) for w in w_l], dz_l, left)
        grad_matmuls([w.astype(BF16) for w in w_d], dz_d, i)

        live_sc[...] = tot_sc[...]
        first = _first_live_tile(i - 2, scores, live_sc)
        trips = i - 1 - first
        prod_bufs, pend_bufs = (prod0_sc, prod1_sc), (pend0_sc, pend1_sc)

        def local_grads(prods):
            zs, dws = prods[:2], prods[2:]
            gates = [_log_gates(z) for z in zs]
            sums = [_nn(g[1].astype(BF16), tri_after) for g in gates]
            ws, gs = [], []
            for h in range(2):
                cum = cum_sc[h] + row_sum(gates[h][1])
                cum_sc[h] = cum
                ws.append(jnp.exp(gates[h][0] + ((live_sc[h] - cum) + sums[h])))
                gs.append(dws[h] * ws[h])
            befores = [_nn(g.astype(BF16), tri_excl) for g in gs]
            dzs = []
            for h in range(2):
                beta = jnp.exp(gates[h][0])
                dzs.append((gs[h] * (1.0 - beta) - (pre_sc[h] + befores[h]) * beta).astype(BF16))
                pre_sc[h] = pre_sc[h] + row_sum(gs[h])
            return [w.astype(BF16) for w in ws] + dzs

        def put(ref, vals):
            for n, val in enumerate(vals):
                ref[n] = val

        def flush(pend, j):
            grad_matmuls([pend[0], pend[1]], [pend[2], pend[3]], j)

        def trip(j, s):
            flush(pend_bufs[s], jnp.maximum(j - 1, first))
            put(prod_bufs[1 - s], products(j + 1))
            put(pend_bufs[1 - s], local_grads([prod_bufs[s][n] for n in range(4)]))

        def earlier_keys_share(j, mask):
            dzs = []
            for h, z in enumerate(scores(j)):
                beta = jnp.exp(_log_gates(z)[0])
                dzs.append(jnp.where(mask, -pre_sc[h] * beta, 0.0).astype(BF16))
            grad_matmuls(None, dzs, j)

        @pl.when(trips > 0)
        def _():
            cum_sc[...] = jnp.zeros_like(cum_sc)
            pre_sc[...] = jnp.zeros_like(pre_sc)
            pend0_sc[...] = jnp.zeros_like(pend0_sc)
            put(prod0_sc, products(first))

            def two_trips(pp, carry):
                trip(first + 2 * pp, 0)
                trip(first + 2 * pp + 1, 1)
                return carry

            lax.fori_loop(0, trips // 2, two_trips, 0)
            odd = trips % 2 == 1

            @pl.when(odd)
            def _():
                trip(i - 2, 0)
                flush(pend1_sc, i - 2)

            @pl.when(jnp.logical_not(odd))
            def _():
                flush(pend0_sc, i - 2)

            earlier_keys_share(i - 1, True)
            earlier_keys_share(i, causal)

        dq_ref[...] = dq_sc[...] * Q_SCALE

    qmap = lambda p, i: (i, p)
    return pl.pallas_call(
        body, name="attn_bwd", grid=(npair, nq),
        in_specs=[_bs((blk, LANES), qmap),
                  _bs((seq, LANES), lambda p, i: (0, npair + p)),
                  _bs((seq, LANES), lambda p, i: (0, 2 * npair + p)),
                  _bs((blk, LANES), qmap)],
        out_specs=[_bs((blk, LANES), qmap),
                   _bs((seq, LANES), lambda p, i: (0, p)),
                   _bs((seq, LANES), lambda p, i: (0, p))],
        out_shape=[_sds((seq, ATTN_W), F32)] * 3,
        scratch_shapes=[pltpu.VMEM((4, blk, blk), F32), pltpu.VMEM((4, blk, blk), F32),
                        pltpu.VMEM((4, blk, blk), BF16), pltpu.VMEM((4, blk, blk), BF16),
                        pltpu.VMEM((2, blk, 1), F32), pltpu.VMEM((2, blk, 1), F32), pltpu.VMEM((2, blk, 1), F32),
                        pltpu.VMEM((2, blk, 1), F32), pltpu.VMEM((blk, LANES), F32)],
        compiler_params=_cparams(2),
    )(proj, proj, proj, do)


def _elementwise(name, fn, ins, out_dtypes):
    rows, cols = ins[0].shape
    tr = rows
    for cand in (512, 256, 128, 64, 32, 16, 8):
        if rows % cand == 0 and cand * cols * 4 <= 2 * 1024 * 1024:
            tr = cand
            break
    n_in = len(ins)

    def body(*refs):
        res = fn(*[r[...] for r in refs[:n_in]])
        for r, val in zip(refs[n_in:], res):
            r[...] = val.astype(r.dtype)

    spec = _bs((tr, cols), lambda i: (i, 0))
    return pl.pallas_call(
        body, name=name, grid=(rows // tr,),
        in_specs=[spec] * n_in, out_specs=[spec] * len(out_dtypes),
        out_shape=[_sds((rows, cols), dt) for dt in out_dtypes],
        compiler_params=_cparams(1),
    )(*ins)


def _adamw_fn(w, g, m, v):
    m = ADAM_B1 * m + (1.0 - ADAM_B1) * g
    v = ADAM_B2 * v + (1.0 - ADAM_B2) * (g * g)
    m_hat = m / (1.0 - ADAM_B1 ** ADAM_STEP)
    v_hat = v / (1.0 - ADAM_B2 ** ADAM_STEP)
    delta = -ADAM_LR * (m_hat / (jnp.sqrt(v_hat) + ADAM_EPS) + ADAM_WD * w)
    return delta, m, v


def _adamw(name, w, g, m, v):
    shape = w.shape
    as2d = lambda a: a.reshape(-1, shape[-1])
    delta, nm, nv = _elementwise(name, _adamw_fn, [as2d(w), as2d(g), as2d(m), as2d(v)], [F32, F32, F32])
    return delta.reshape(shape), nm.reshape(shape), nv.reshape(shape)


def _place():
    return lax.axis_index("x"), lax.axis_index("y"), lax.axis_index("c")


ANY = pl.BlockSpec(memory_space=pl.ANY)
VMEM_WHOLE = pl.BlockSpec(memory_space=pltpu.VMEM)


def _allgather_weights(shards):
    n = len(shards)

    def body(*refs):
        src, dst = refs[:n], refs[n:2 * n]
        send_sems, recv_sems, local_sems = refs[2 * n:]
        x, y, c = _place()
        me, sibling, mychip = (x, y, c), (x, y, 1 - c), 2 * x + y

        x_nbr, y_nbr, diag = 2 * (1 - x) + y, 2 * x + (1 - y), 2 * (1 - x) + (1 - y)
        to_x, to_y = (1 - x, y, c), (x, 1 - y, c)

        def parts(w):
            hr = src[w].shape[0] // 2
            first = hr // 2 if hr % 32 == 0 else hr
            return first, hr - first

        def rows_of(w, chip, half, route):
            hr = src[w].shape[0] // 2
            first, second = parts(w)
            start, size = {0: (0, hr), 1: (0, hr), 2: (0, first), 3: (first, second)}[route]
            return dst[w].at[chip, pl.ds(half * hr + start, size)]

        def copy(w, k, src_ref, dst_ref, to):
            return pltpu.make_async_remote_copy(src_ref=src_ref, dst_ref=dst_ref, send_sem=send_sems.at[w, k],
                                                recv_sem=recv_sems.at[w, k], device_id=to, device_id_type=MESH)

        def landed(w, route):
            chip = {0: x_nbr, 1: y_nbr, 2: diag, 3: diag}[route]
            return rows_of(w, chip, c, route), chip

        def routes(w):
            return (0, 1, 2, 3) if parts(w)[1] else (0, 1, 2)

        started, local = [], []
        for w in range(n):
            hr = src[w].shape[0] // 2
            own = pltpu.make_async_copy(src[w], dst[w].at[mychip], local_sems.at[w])
            own.start()
            local.append(own)
            mine = src[w].at[pl.ds(c * hr, hr)]
            for route, to in ((0, to_x), (1, to_y)):
                cp = copy(w, route, mine, rows_of(w, mychip, c, route), to)
                cp.start()
                started.append(cp)

        def pass_on(w, route):
            got, chip = landed(w, route)
            copy(w, route, got, got, me).wait_recv()
            if route == 1:
                part = rows_of(w, chip, c, 2)
                started.append(copy(w, 2, part, part, to_x))
                started[-1].start()
            if route == 0 and parts(w)[1]:
                part = rows_of(w, chip, c, 3)
                started.append(copy(w, 3, part, part, to_y))
                started[-1].start()
            started.append(copy(w, 4 + route, got, got, sibling))
            started[-1].start()

        for w in range(n):
            pass_on(w, 1)
            pass_on(w, 0)
        for w in range(n):
            for route in routes(w)[2:]:
                pass_on(w, route)
        for w in range(n):
            for route in routes(w):
                chip = landed(w, route)[1]
                from_sib = rows_of(w, chip, 1 - c, route)
                copy(w, 4 + route, from_sib, from_sib, me).wait_recv()
        for cp in local:
            cp.wait()
        for cp in started:
            cp.wait_send()

    return pl.pallas_call(
        body, name="allgather_weights",
        in_specs=[VMEM_WHOLE] * n, out_specs=[VMEM_WHOLE] * n,
        out_shape=[_sds((N_CHIPS,) + s.shape, s.dtype) for s in shards],
        scratch_shapes=[pltpu.SemaphoreType.DMA((n, 8)), pltpu.SemaphoreType.DMA((n, 8)),
                        pltpu.SemaphoreType.DMA((n,))],
        compiler_params=pltpu.CompilerParams(vmem_limit_bytes=VMEM_LIMIT),
    )(*shards)


SUM_ROWS = 64


def _rs_pair_sum(name, grads):
    n = len(grads)

    def body(*refs):
        g, out = refs[:n], refs[n:2 * n]
        stage, give16, land, keep = (refs[m * n:(m + 1) * n] for m in range(2, 6))
        send_sems, recv_sems, stage_sems, keep_sems = refs[6 * n:]
        x, y, c = _place()
        sibling = (x, y, 1 - c)

        def over_rows(w, fn):
            nb = g[w].shape[1] // 2 // SUM_ROWS

            def step(idx, carry):
                fn(idx // nb, pl.ds(pl.multiple_of((idx % nb) * SUM_ROWS, SUM_ROWS), SUM_ROWS))
                return carry

            lax.fori_loop(0, N_CHIPS * nb, step, 0)

        loads = []
        for w in range(n):
            hr = g[w].shape[1] // 2
            st = pltpu.make_async_copy(g[w].at[:, pl.ds((1 - c) * hr, hr)], stage[w], stage_sems.at[w])
            kp = pltpu.make_async_copy(g[w].at[:, pl.ds(c * hr, hr)], keep[w], keep_sems.at[w])
            st.start()
            kp.start()
            loads.append((st, kp))
        gives = []
        for w in range(n):
            loads[w][0].wait()

            def narrow(k, rows, w=w):
                give16[w][k, rows, :] = stage[w][k, rows, :].astype(BF16)

            over_rows(w, narrow)
            give = pltpu.make_async_remote_copy(src_ref=give16[w], dst_ref=land[w], send_sem=send_sems.at[w],
                                                recv_sem=recv_sems.at[w], device_id=sibling, device_id_type=MESH)
            give.start()
            gives.append(give)
        for w in range(n):
            loads[w][1].wait()
            gives[w].wait_recv()

            def add(k, rows, w=w):
                out[w][k, rows, :] = (keep[w][k, rows, :] + land[w][k, rows, :].astype(F32)).astype(BF16)

            over_rows(w, add)
        for give in gives:
            give.wait_send()

    half = [(N_CHIPS, a.shape[1] // 2, a.shape[2]) for a in grads]
    wide = [pltpu.VMEM(s, F32) for s in half]
    narrow_bufs = [pltpu.VMEM(s, BF16) for s in half]
    sems = pltpu.SemaphoreType.DMA((n,))
    return pl.pallas_call(
        body, name=name,
        in_specs=[ANY] * n, out_specs=[VMEM_WHOLE] * n, out_shape=[_sds(s, BF16) for s in half],
        scratch_shapes=wide + narrow_bufs + narrow_bufs + wide + [sems, sems, sems, sems],
        compiler_params=pltpu.CompilerParams(vmem_limit_bytes=VMEM_LIMIT),
    )(*grads)


def _rs_exchange_join(parts):
    n = len(parts)

    def body(*refs):
        t, full = refs[:n], refs[n:2 * n]
        got_x, got_y, pass_on, got_2 = (refs[m * n:(m + 1) * n] for m in range(2, 6))
        send_sems, recv_sems = refs[6 * n:]
        x, y, c = _place()
        mychip, sibling = 2 * x + y, (x, y, 1 - c)
        x_nbr, y_nbr, diag = 2 * (1 - x) + y, 2 * x + (1 - y), 2 * (1 - x) + (1 - y)
        to_x, to_y = (1 - x, y, c), (x, 1 - y, c)
        sends = []

        def copy(w, k, src_ref, dst_ref, to):
            return pltpu.make_async_remote_copy(src_ref=src_ref, dst_ref=dst_ref, send_sem=send_sems.at[w, k],
                                                recv_sem=recv_sems.at[w, k], device_id=to, device_id_type=MESH)

        def start(cp):
            cp.start()
            sends.append(cp)

        def add_rows(w, count, fn):
            def step(idx, carry):
                fn(pl.ds(pl.multiple_of(idx * SUM_ROWS, SUM_ROWS), SUM_ROWS), pl.multiple_of(idx * SUM_ROWS, SUM_ROWS))
                return carry
            lax.fori_loop(0, count // SUM_ROWS, step, 0)

        f32 = lambda v: v.astype(F32)
        for w in range(n):
            ha = t[w].shape[1] // 2
            part_a, part_b = pl.ds(0, ha), pl.ds(ha, ha)
            start(copy(w, 0, t[w].at[x_nbr, part_a], got_x[w].at[0], to_x))
            start(copy(w, 1, t[w].at[diag, part_a], got_x[w].at[1], to_x))
            start(copy(w, 2, t[w].at[y_nbr, part_b], got_y[w].at[0], to_y))
            start(copy(w, 3, t[w].at[diag, part_b], got_y[w].at[1], to_y))
        for w in range(n):
            hr = t[w].shape[1]
            ha = hr // 2
            for k in (0, 1):
                copy(w, k, got_x[w].at[k], got_x[w].at[k], to_x).wait_recv()

            def sum_a(rows, r, w=w, hr=hr):
                full[w][pl.ds(pl.multiple_of(c * hr + r, SUM_ROWS), SUM_ROWS), :] = \
                    f32(t[w][mychip, rows, :]) + f32(got_x[w][0, rows, :])
                pass_on[w][rows, :] = (f32(t[w][y_nbr, rows, :]) + f32(got_x[w][1, rows, :])).astype(BF16)

            add_rows(w, ha, sum_a)
            start(copy(w, 4, pass_on[w].at[pl.ds(0, ha)], got_2[w].at[pl.ds(0, ha)], to_y))
            for k in (2, 3):
                copy(w, k, got_y[w].at[k - 2], got_y[w].at[k - 2], to_y).wait_recv()

            def sum_b(rows, r, w=w, hr=hr, ha=ha):
                lower = pl.ds(pl.multiple_of(ha + r, SUM_ROWS), SUM_ROWS)
                full[w][pl.ds(pl.multiple_of(c * hr + ha + r, SUM_ROWS), SUM_ROWS), :] = \
                    f32(t[w][mychip, lower, :]) + f32(got_y[w][0, rows, :])
                pass_on[w][lower, :] = (f32(t[w][x_nbr, lower, :]) + f32(got_y[w][1, rows, :])).astype(BF16)

            add_rows(w, ha, sum_b)
            start(copy(w, 5, pass_on[w].at[pl.ds(ha, ha)], got_2[w].at[pl.ds(ha, ha)], to_x))
        for w in range(n):
            hr = t[w].shape[1]
            ha = hr // 2
            copy(w, 4, got_2[w].at[pl.ds(0, ha)], got_2[w].at[pl.ds(0, ha)], to_y).wait_recv()
            copy(w, 5, got_2[w].at[pl.ds(ha, ha)], got_2[w].at[pl.ds(ha, ha)], to_x).wait_recv()

            def finish(rows, r, w=w, hr=hr):
                out_rows = pl.ds(pl.multiple_of(c * hr + r, SUM_ROWS), SUM_ROWS)
                full[w][out_rows, :] = full[w][out_rows, :] + f32(got_2[w][rows, :])

            add_rows(w, hr, finish)
            mine = full[w].at[pl.ds(c * hr, hr)]
            start(copy(w, 6, mine, mine, sibling))
        for w in range(n):
            hr = t[w].shape[1]
            theirs = full[w].at[pl.ds((1 - c) * hr, hr)]
            copy(w, 6, theirs, theirs, sibling).wait_recv()
        for cp in sends:
            cp.wait_send()

    half = lambda a: pltpu.VMEM((2, a.shape[1] // 2, a.shape[2]), a.dtype)
    whole = lambda a: pltpu.VMEM(a.shape[1:], a.dtype)
    return pl.pallas_call(
        body, name="rs_exchange_join",
        in_specs=[VMEM_WHOLE] * n, out_specs=[VMEM_WHOLE] * n,
        out_shape=[_sds((2 * a.shape[1], a.shape[2]), F32) for a in parts],
        scratch_shapes=[half(a) for a in parts] + [half(a) for a in parts] + [whole(a) for a in parts]
        + [whole(a) for a in parts] + [pltpu.SemaphoreType.DMA((n, 7)), pltpu.SemaphoreType.DMA((n, 7))],
        compiler_params=pltpu.CompilerParams(vmem_limit_bytes=VMEM_LIMIT),
    )(*parts)


def _small_allreduce(loss_p, dg_parts, dbg_a, dbg_c, dwc):
    ins = [loss_p] + list(dg_parts) + [dbg_a, dbg_c, dwc]
    n_in = len(ins)
    vmem = pl.BlockSpec(memory_space=pltpu.VMEM)

    def body(*refs):
        in_refs = refs[:n_in]
        out_ref, vec, buf, send_sems, recv_sems = refs[n_in:]
        x, y, c = _place()
        me = 4 * x + 2 * y + c
        vec[...] = jnp.zeros_like(vec)
        vec[0:1, :] = jnp.sum(in_refs[0][...], axis=0)
        for r in range(5):
            vec[1 + r:2 + r, :] = jnp.sum(in_refs[1 + r][...], axis=0)
        vec[6:7, :] = jnp.sum(in_refs[6][...], axis=0)
        vec[7:8, :] = jnp.sum(in_refs[7][...], axis=0)
        vec[8:16, 0:CONV_W] = jnp.sum(in_refs[8][...], axis=0)
        buf[pl.ds(me, 1)] = vec[...][None]
        copies = []
        for r in range(1, 8):
            fx, fy, fc = (r >> 2) & 1, (r >> 1) & 1, r & 1
            to = (1 - x if fx else x, 1 - y if fy else y, 1 - c if fc else c)
            cp = pltpu.make_async_remote_copy(src_ref=vec, dst_ref=buf.at[me], send_sem=send_sems.at[r - 1],
                                              recv_sem=recv_sems.at[r - 1], device_id=to, device_id_type=MESH)
            cp.start()
            copies.append(cp)
        for cp in copies:
            cp.wait()
        total = buf[0]
        for s in range(1, 8):
            total = total + buf[s]
        out_ref[...] = total
        out_ref[0:1, :] = jnp.broadcast_to(jnp.sum(total[0:1, :], axis=-1, keepdims=True), (1, D_MODEL))

    return pl.pallas_call(
        body, name="small_allreduce",
        in_specs=[vmem] * n_in, out_specs=vmem, out_shape=_sds((SMALL_ROWS, D_MODEL), F32),
        scratch_shapes=[pltpu.VMEM((SMALL_ROWS, D_MODEL), F32), pltpu.VMEM((8, SMALL_ROWS, D_MODEL), F32),
                        pltpu.SemaphoreType.DMA((7,)), pltpu.SemaphoreType.DMA((7,))],
    )(*ins)


def _local_step(x, p, tgt, g, b_gate, w_conv, wf):
    seq = x.shape[0]
    tm = min(seq, 1024)
    th = min(seq, 512)
    tl = min(seq, 2048)
    ni, nh, nl = seq // tm, seq // th, seq // tl
    g_pre_mix, g_post_mix, g_pre_mlp, g_post_mlp, g_ple = g
    w_in_nat, w_ao, w_co, w_o, w_up_nat, w_down, w_pg, w_pp = wf
    D = D_MODEL
    vec = lambda a, blk=0: (a, _bs((1, D), lambda i, j, k: (0, blk)))
    rows_i = lambda a, t, blk=0: (a, _bs((t, D), lambda i, j, k: (i, blk)))
    rows_k = lambda a, t, blk=0: (a, _bs((t, D), lambda i, j, k: (k, blk)))
    part = lambda n: (_sds((n, 1, D), F32), _bs((None, 1, D), lambda i, j, k: (i, 0, 0)))
    full2 = lambda a: (a, _bs(a.shape, lambda i, j, k: (0, 0)))

    normed = lambda xb, gb: (_rms(xb, gb).astype(BF16),) * 2
    keep_a = lambda t: [(_sds((seq, D), BF16), _bs((t, D), lambda i, j, k: (i, 0)))]
    main_w = D_IN - 2 * D
    proj, gates, h1 = _mm("proj_in", "nn", (nh, 1, 1),
                          a_ins=[rows_i(x, th), vec(g_pre_mix)], a_fn=normed,
                          b_ins=[full2(w_in_nat)], b_fn=_ident,
                          epi_fn=lambda acc: (acc[:, :main_w], acc[:, main_w:]),
                          outs=[(_sds((seq, main_w), F32), _bs((th, main_w), lambda i, j, k: (i, 0))),
                                (_sds((seq, 2 * D), BF16), _bs((th, 2 * D), lambda i, j, k: (i, 0)))],
                          acc_shape=(th, D_IN), a_cache=((th, D), BF16), a_outs=keep_a(th))
    o = _attn_fwd(proj, seq)
    e = _conv_fwd(proj, w_conv, seq, tm)

    def gate_values(ga, gc, ba, bc):
        return _sig(ga.astype(F32) + ba), _sig(gc.astype(F32) + bc)

    def branch_outputs(ob, eb, wao, wco):
        return _nn(ob, wao).astype(BF16).astype(F32), _nn(eb, wco).astype(BF16).astype(F32)

    def mix_fn(ga, gc, ob, eb, ba, bc, wao, wco):
        sa, sc = gate_values(ga, gc, ba, bc)
        ya, yc = branch_outputs(ob, eb, wao, wco)
        return ((sa * ya + sc * yc).astype(BF16),) * 2

    def post_mix(acc, xb, gb):
        return acc, xb + _rms(acc, gb)

    half_rows = lambda a: (a, _bs((th, a.shape[1]), lambda i, j, k: (i, 0)))
    mix_ins = [rows_i(gates, th, 0), rows_i(gates, th, 1), half_rows(o), half_rows(e), vec(b_gate, 0), vec(b_gate, 1),
               full2(w_ao), full2(w_co)]
    mixed, x1, mixin = _mm(
        "mix_out", "nn", (nh, 1, 1),
        a_ins=mix_ins, a_fn=mix_fn, b_ins=[full2(w_o)], b_fn=_ident,
        epi_ins=[rows_i(x, th), vec(g_post_mix)], epi_fn=post_mix,
        outs=[(_sds((seq, D), BF16), _bs((th, D), lambda i, j, k: (i, 0))),
              (_sds((seq, D), F32), _bs((th, D), lambda i, j, k: (i, 0)))],
        acc_shape=(th, D), a_cache=((th, D), BF16), a_outs=keep_a(th))
    up, h2 = _mm("mlp_up", "nn", (nh, 1, 1),
                 a_ins=[rows_i(x1, th), vec(g_pre_mlp)], a_fn=normed,
                 b_ins=[full2(w_up_nat)], b_fn=_ident,
                 outs=[(_sds((seq, D_FF), BF16), _bs((th, D_FF), lambda i, j, k: (i, 0)))],
                 acc_shape=(th, D_FF), a_cache=((th, D), BF16), a_outs=keep_a(th))

    def relu2(ub):
        r = jnp.maximum(ub.astype(F32), 0.0)
        return (r * r).astype(BF16)

    dx2, df, dpre, h3, dpp, loss_p, dg_ple_p, dg_post_mlp_p = _mlp_down_ple_head(
        up, x1, p, tgt, g_ple, g_post_mlp, w_down, w_pg, w_pp, seq, th)

    (dw_pp,) = _mm("dw_ple_proj", "tn", (1, 1, nh),
                   a_ins=[(p, _bs((th, PLE_DIM), lambda i, j, k: (k, 0)))], a_fn=_to_bf16,
                   b_ins=[rows_k(dpp, th)], b_fn=_ident,
                   outs=[(_sds((PLE_DIM, D), F32), _bs((PLE_DIM, D), lambda i, j, k: (0, 0)))],
                   acc_shape=(PLE_DIM, D))
    (dw_pg,) = _mm("dw_ple_gate", "tn", (1, 1, nl),
                   a_ins=[rows_k(h3, tl)], a_fn=_ident, b_ins=[rows_k(dpre, tl)], b_fn=_ident,
                   outs=[(_sds((D, D), F32), _bs((D, D), lambda i, j, k: (0, 0)))], acc_shape=(D, D))

    def dup_fn(acc, ub):
        return (acc * (2.0 * jnp.maximum(ub.astype(F32), 0.0)),)

    (dup,) = _mm("d_mlp_down", "nt", (nh, 1, 1),
                 a_ins=[rows_i(df, th)], a_fn=_ident, b_ins=[full2(w_down)], b_fn=_ident,
                 epi_ins=[(up, _bs((th, D_FF), lambda i, j, k: (i, 0)))], epi_fn=dup_fn,
                 outs=[(_sds((seq, D_FF), BF16), _bs((th, D_FF), lambda i, j, k: (i, 0)))],
                 acc_shape=(th, D_FF))
    (dw_down,) = _mm("dw_mlp_down", "tn", (4, 1, nl),
                     a_ins=[(up, _bs((tl, D), lambda i, j, k: (k, i)))], a_fn=relu2,
                     b_ins=[rows_k(df, tl)], b_fn=_ident,
                     outs=[(_sds((D_FF, D), F32), _bs((D, D), lambda i, j, k: (i, 0)))], acc_shape=(D, D))
    (dw_up,) = _mm("dw_mlp_up", "tn", (1, 4, nl),
                   a_ins=[rows_k(h2, tl)], a_fn=_ident,
                   b_ins=[(dup, _bs((tl, D), lambda i, j, k: (k, j)))], b_fn=_ident,
                   outs=[(_sds((N_CHIPS, D, D), F32), _bs((None, D, D), lambda i, j, k: (j, 0, 0)))],
                   acc_shape=(D, D))

    def mlp_norm_bwd(acc, x1b, dx2b, mixedb, g_mlp, g_mix):
        dxn, dg_mlp = _rms_bwd(x1b, g_mlp, acc)
        dx1b = dx2b + dxn
        dmixedb, dg_mix = _rms_bwd(mixedb.astype(F32), g_mix, dx1b)
        return dx1b, dmixedb, dg_mlp, dg_mix

    dx1, dmixed, dg_pre_mlp_p, dg_post_mix_p = _mm(
        "d_mlp_up", "nt", (nh, 1, 1),
        a_ins=[(dup, _bs((th, D_FF), lambda i, j, k: (i, 0)))], a_fn=_ident,
        b_ins=[full2(w_up_nat)], b_fn=_ident,
        epi_ins=[rows_i(x1, th), rows_i(dx2, th), rows_i(mixed, th), vec(g_pre_mlp), vec(g_post_mix)],
        epi_fn=mlp_norm_bwd,
        outs=[(_sds((seq, D), F32), _bs((th, D), lambda i, j, k: (i, 0))),
              (_sds((seq, D), BF16), _bs((th, D), lambda i, j, k: (i, 0))), part(nh), part(nh)],
        acc_shape=(th, D))
    (dw_o,) = _mm("dw_mix_out", "tn", (1, 1, nl),
                  a_ins=[rows_k(mixin, tl)], a_fn=_ident, b_ins=[rows_k(dmixed, tl)], b_fn=_ident,
                  outs=[(_sds((D, D), F32), _bs((D, D), lambda i, j, k: (0, 0)))], acc_shape=(D, D))

    def gate_bwd(acc, ga, gc, ob, eb, ba, bc, wao, wco):
        sa, sc = gate_values(ga, gc, ba, bc)
        ya, yc = branch_outputs(ob, eb, wao, wco)
        dga = acc * ya * sa * (1.0 - sa)
        dgc = acc * yc * sc * (1.0 - sc)
        dya, dyc = (acc * sa).astype(BF16), (acc * sc).astype(BF16)
        return (dya, dyc, jnp.concatenate([dga, dgc], axis=1), _nt(dya, wao), _nt(dyc, wco),
                jnp.sum(dga, axis=0, keepdims=True), jnp.sum(dgc, axis=0, keepdims=True))

    dya, dyc, dgate, do, de, dbg_a_p, dbg_c_p = _mm(
        "d_mix_out", "nt", (nh, 1, 1),
        a_ins=[rows_i(dmixed, th)], a_fn=_ident, b_ins=[full2(w_o)], b_fn=_ident,
        epi_ins=mix_ins, epi_fn=gate_bwd,
        outs=[(_sds((seq, D), BF16), _bs((th, D), lambda i, j, k: (i, 0)))] * 2
             + [(_sds((seq, 2 * D), BF16), _bs((th, 2 * D), lambda i, j, k: (i, 0))),
                (_sds((seq, ATTN_W), BF16), _bs((th, ATTN_W), lambda i, j, k: (i, 0))),
                (_sds((seq, CONV_W), F32), _bs((th, CONV_W), lambda i, j, k: (i, 0))), part(nh), part(nh)],
        acc_shape=(th, D))
    (dw_ao,) = _mm("dw_attn_out", "tn", (1, 1, nh),
                   a_ins=[(o, _bs((th, ATTN_W), lambda i, j, k: (k, 0)))], a_fn=_ident,
                   b_ins=[rows_k(dya, th)], b_fn=_ident,
                   outs=[(_sds((ATTN_W, D), F32), _bs((ATTN_W, D), lambda i, j, k: (0, 0)))], acc_shape=(ATTN_W, D))
    dq, dk, dv = _attn_bwd(proj, do, seq)
    (dw_co,) = _mm("dw_conv_out", "tn", (1, 1, nh),
                   a_ins=[(e, _bs((th, CONV_W), lambda i, j, k: (k, 0)))], a_fn=_ident,
                   b_ins=[rows_k(dyc, th)], b_fn=_ident,
                   outs=[(_sds((CONV_W, D), F32), _bs((CONV_W, D), lambda i, j, k: (0, 0)))], acc_shape=(CONV_W, D))
    dconv, dwc_p = _conv_bwd(proj, de, w_conv, seq, tm)
    qkv_w = 3 * ATTN_W
    join_bf16 = lambda *blocks: jnp.concatenate([b.astype(BF16) for b in blocks], axis=1)
    piece = lambda a, t, rows, blk=0: (a, _bs((t, a.shape[1]), (lambda i, j, k: (k, blk)) if rows == "k"
                                             else (lambda i, j, k: (i, blk))))
    (dw_in_qkv,) = _mm("dw_proj_in_qkv", "tn", (1, 1, ni),
                       a_ins=[rows_k(h1, tm)], a_fn=_ident,
                       b_ins=[piece(dq, tm, "k"), piece(dk, tm, "k"), piece(dv, tm, "k")], b_fn=join_bf16,
                       outs=[(_sds((D, qkv_w), F32), _bs((D, qkv_w), lambda i, j, k: (0, 0)))], acc_shape=(D, qkv_w))
    (dw_in_conv,) = _mm("dw_proj_in_conv", "tn", (1, 1, nl),
                        a_ins=[rows_k(h1, tl)], a_fn=_ident, b_ins=[piece(dconv, tl, "k")], b_fn=_ident,
                        outs=[(_sds((D, 3 * CONV_W), F32), _bs((D, 3 * CONV_W), lambda i, j, k: (0, 0)))],
                        acc_shape=(D, 3 * CONV_W))
    (dw_in_gate,) = _mm("dw_proj_in_gate", "tn", (1, 2, nl),
                        a_ins=[rows_k(h1, tl)], a_fn=_ident,
                        b_ins=[(dgate, _bs((tl, D), lambda i, j, k: (k, j)))], b_fn=_ident,
                        outs=[(_sds((D, 2 * D), F32), _bs((D, D), lambda i, j, k: (0, j)))], acc_shape=(D, D))
    dw_in = jnp.concatenate([dw_in_qkv, dw_in_conv, dw_in_gate], axis=1)

    def in_norm_bwd(acc, xb, dx1b, gb):
        dxn, dg = _rms_bwd(xb, gb, acc)
        return dx1b + dxn, dg

    grad_x, dg_pre_mix_p = _mm("d_proj_in", "nt", (nh, 1, 1),
                               a_ins=[piece(dq, th, "i"), piece(dk, th, "i"), piece(dv, th, "i"),
                                      piece(dconv, th, "i"), piece(dgate, th, "i")], a_fn=join_bf16,
                               b_ins=[full2(w_in_nat)], b_fn=_ident,
                               epi_ins=[rows_i(x, th), rows_i(dx1, th), vec(g_pre_mix)], epi_fn=in_norm_bwd,
                               outs=[(_sds((seq, D), F32), _bs((th, D), lambda i, j, k: (i, 0))), part(nh)],
                               acc_shape=(th, D))

    chip_major = lambda a: a.reshape(a.shape[0], N_CHIPS, a.shape[1] // N_CHIPS).transpose(1, 0, 2)
    big = [chip_major(dw_in), chip_major(dw_ao), chip_major(dw_co), dw_o.reshape(N_CHIPS, D // N_CHIPS, D), dw_up,
           dw_down.reshape(N_CHIPS, D_FF // N_CHIPS, D), dw_pg.reshape(N_CHIPS, D // N_CHIPS, D), chip_major(dw_pp)]
    small = (loss_p, [dg_pre_mix_p, dg_post_mix_p, dg_pre_mlp_p, dg_post_mlp_p, dg_ple_p], dbg_a_p, dbg_c_p, dwc_p)
    return grad_x, big, small


RS_GROUPS = ((0,), (4,), (5,), (1, 2, 3, 6, 7))


def _reduce_scatter(big):
    pair = [None] * len(big)
    for gi, group in enumerate(RS_GROUPS):
        for w, s in zip(group, _rs_pair_sum(f"rs_pair_sum_{gi}", [big[w] for w in group])):
            pair[w] = s
    return _rs_exchange_join(pair)


def kernel(x, p, g_pre_mix, w_in, b_gate, w_conv, w_attn_out, w_conv_out, w_o, g_post_mix, g_pre_mlp, w_up, w_down, g_post_mlp, g_ple, w_ple_gate, w_ple_proj, loss_target, m_g_pre_mix, m_w_in, m_b_gate, m_w_conv, m_w_attn_out, m_w_conv_out, m_w_o, m_g_post_mix, m_g_pre_mlp, m_w_up, m_w_down, m_g_post_mlp, m_g_ple, m_w_ple_gate, m_w_ple_proj, v_g_pre_mix, v_w_in, v_b_gate, v_w_conv, v_w_attn_out, v_w_conv_out, v_w_o, v_g_post_mix, v_g_pre_mlp, v_w_up, v_w_down, v_g_post_mlp, v_g_ple, v_w_ple_gate, v_w_ple_proj):
    mats = [w_in, w_attn_out, w_conv_out, w_o, w_up, w_down, w_ple_gate, w_ple_proj]
    mats_m = [m_w_in, m_w_attn_out, m_w_conv_out, m_w_o, m_w_up, m_w_down, m_w_ple_gate, m_w_ple_proj]
    mats_v = [v_w_in, v_w_attn_out, v_w_conv_out, v_w_o, v_w_up, v_w_down, v_w_ple_gate, v_w_ple_proj]
    gains = [g_pre_mix, g_post_mix, g_pre_mlp, g_post_mlp, g_ple]
    gains_m = [m_g_pre_mix, m_g_post_mix, m_g_pre_mlp, m_g_post_mlp, m_g_ple]
    gains_v = [v_g_pre_mix, v_g_post_mix, v_g_pre_mlp, v_g_post_mlp, v_g_ple]

    taps = jnp.concatenate([w_conv[0], jnp.zeros((CONV_PAD_ROWS - 3, LANES), F32)], axis=0)
    gathered = _allgather_weights([w[0].astype(BF16) for w in mats] + [taps])
    cols_joined = lambda a: a.transpose(1, 0, 2).reshape(a.shape[1], N_CHIPS * a.shape[2])
    rows_joined = lambda a: a.reshape(N_CHIPS * a.shape[1], a.shape[2])
    col_sharded = (0, 1, 2, 4, 7)
    wf = [cols_joined(gathered[n]) if n in col_sharded else rows_joined(gathered[n]) for n in range(8)]
    w_conv_full = cols_joined(gathered[8])[0:3, :]
    chip = 2 * lax.axis_index("x") + lax.axis_index("y")

    grad_x, big, small = _local_step(x[0], p[0, 0], loss_target[0], gains, b_gate, w_conv_full, wf)

    shard_grads = _reduce_scatter(big)
    red = _small_allreduce(*small)
    loss = red[0, 0]
    grad_gains = [red[1 + r:2 + r, :] for r in range(5)]
    grad_b_gate = jnp.concatenate([red[6:7, :], red[7:8, :]], axis=1)
    grad_w_conv = lax.dynamic_slice(red[8:11, :], (0, chip * LANES), (3, LANES))[None]

    grads_big = [gr.reshape(w.shape) for gr, w in zip(shard_grads, mats)]
    upd_big = [_adamw(f"adamw_{i}", w, gr, m, v) for i, (w, gr, m, v) in enumerate(zip(mats, grads_big, mats_m, mats_v))]
    pack = lambda vs, bg: jnp.concatenate(list(vs) + [bg.reshape(2, D_MODEL), jnp.zeros((1, D_MODEL), F32)], axis=0)
    upd_small = _adamw("adamw_small", pack(gains, b_gate), pack(grad_gains, grad_b_gate),
                       pack(gains_m, m_b_gate), pack(gains_v, v_b_gate))
    upd_conv = _adamw("adamw_conv", w_conv, grad_w_conv, m_w_conv, v_w_conv)

    def small_out(a, which):
        gains_out = [a[r:r + 1, :] for r in range(5)]
        return gains_out, a[5:7, :].reshape(1, 2 * D_MODEL)

    def ordered(g_pre_mix_, big_, b_gate_, conv_, g_rest):
        return [g_pre_mix_, big_[0], b_gate_, conv_, big_[1], big_[2], big_[3], g_rest[0], g_rest[1], big_[4], big_[5],
                g_rest[2], g_rest[3], big_[6], big_[7]]

    outs = [loss, grad_x[None]]
    outs += ordered(grad_gains[0], grads_big, grad_b_gate, grad_w_conv, grad_gains[1:])
    for which in range(3):
        g_out, b_out = small_out(upd_small[which], which)
        outs += ordered(g_out[0], [u[which] for u in upd_big], b_out, upd_conv[which], g_out[1:])
    return tuple(outs)
```

```python
import jax
import jax.numpy as jnp
from jax import lax
from jax.experimental import pallas as pl
from jax.experimental.pallas import tpu as pltpu

F32 = jnp.float32
BF16 = jnp.bfloat16
MESH = pl.DeviceIdType.MESH

D_MODEL = 1024
N_HEADS = 8
HEAD_DIM = 64
ATTN_W = N_HEADS * HEAD_DIM
CONV_W = 512
D_FF = 4096
PLE_DIM = 256
D_IN = 5120
N_CHIPS = 4
EPS = 1e-6
Q_SCALE = HEAD_DIM ** -0.5

ADAM_LR = 0.001
ADAM_B1 = 0.9
ADAM_B2 = 0.999
ADAM_EPS = 1e-08
ADAM_WD = 0.01
ADAM_STEP = 10

V7X_VMEM_BYTES = 64 * 1024 * 1024
VMEM_LIMIT = V7X_VMEM_BYTES - 8 * 1024 * 1024
LANES = 128
ATT_BLK = 256
SMALL_ROWS = 16
CONV_PAD_ROWS = 16


def _cparams(n_grid):
    return pltpu.CompilerParams(dimension_semantics=("arbitrary",) * n_grid, vmem_limit_bytes=VMEM_LIMIT)


def _bs(shape, fn):
    return pl.BlockSpec(shape, fn)


def _rms_stats(xf):
    return lax.rsqrt(jnp.mean(xf * xf, axis=-1, keepdims=True) + EPS)


def _rms(xf, g):
    return xf * _rms_stats(xf) * g


def _rms_bwd(xf, g, dy):
    r = _rms_stats(xf)
    xh = xf * r
    dyg = dy * g
    dx = r * (dyg - xh * jnp.mean(dyg * xh, axis=-1, keepdims=True))
    return dx, jnp.sum(dy * xh, axis=0, keepdims=True)


def _sig(z):
    return 1.0 / (1.0 + jnp.exp(-z))


def _ident(a):
    return a


def _to_bf16(a):
    return a.astype(BF16)


_DIMS = {"nn": (((1,), (0,)), ((), ())), "nt": (((1,), (1,)), ((), ())), "tn": (((0,), (0,)), ((), ()))}


def _mm(name, mode, grid, a_ins, a_fn, b_ins, b_fn, outs, acc_shape, epi_ins=(), epi_fn=None,
        a_cache=None, a_outs=(), epi_a=()):
    nk = grid[2]
    na, nb, ne, no, nao = len(a_ins), len(b_ins), len(epi_ins), len(outs), len(a_outs)
    assert a_cache is None or nk == 1
    assert not a_outs or a_cache is not None
    dims = _DIMS[mode]
    if epi_fn is None:
        epi_fn = lambda acc: (acc,)

    def body(*refs):
        a_refs = refs[:na]
        b_refs = refs[na:na + nb]
        e_refs = refs[na + nb:na + nb + ne]
        o_refs = refs[na + nb + ne:na + nb + ne + no]
        ao_refs = refs[na + nb + ne + no:na + nb + ne + no + nao]
        scratch = list(refs[na + nb + ne + no + nao:])
        acc_ref = scratch.pop(0) if nk > 1 else None
        a_sc = scratch.pop(0) if a_cache is not None else None
        j = pl.program_id(1)
        k = pl.program_id(2)

        def finish(acc):
            res = epi_fn(acc, *[a_refs[t][...] for t in epi_a], *[r[...] for r in e_refs])
            for r, val in zip(o_refs, res):
                r[...] = val.astype(r.dtype)

        if a_sc is not None:
            @pl.when(j == 0)
            def _():
                res = a_fn(*[r[...] for r in a_refs])
                if nao:
                    for r, val in zip(ao_refs, res[1:]):
                        r[...] = val.astype(r.dtype)
                    res = res[0]
                a_sc[...] = res
            a = a_sc[...]
        else:
            a = a_fn(*[r[...] for r in a_refs])
        b = b_fn(*[r[...] for r in b_refs])
        prod = lax.dot_general(a, b, dims, preferred_element_type=F32)
        if nk == 1:
            finish(prod)
        else:
            @pl.when(k == 0)
            def _():
                acc_ref[...] = prod

            @pl.when(k > 0)
            def _():
                acc_ref[...] += prod

            @pl.when(k == nk - 1)
            def _():
                finish(acc_ref[...])

    scratch_shapes = []
    if nk > 1:
        scratch_shapes.append(pltpu.VMEM(acc_shape, F32))
    if a_cache is not None:
        scratch_shapes.append(pltpu.VMEM(*a_cache))
    all_outs = list(outs) + list(a_outs)
    res = pl.pallas_call(
        body, name=name, grid=grid,
        in_specs=[s for _, s in a_ins] + [s for _, s in b_ins] + [s for _, s in epi_ins],
        out_specs=[s for _, s in all_outs],
        out_shape=[o for o, _ in all_outs],
        scratch_shapes=scratch_shapes,
        compiler_params=_cparams(3),
    )(*[a for a, _ in a_ins], *[a for a, _ in b_ins], *[a for a, _ in epi_ins])
    return res


def _sds(shape, dtype):
    return jax.ShapeDtypeStruct(shape, dtype)


def _nt(a, b):
    return lax.dot_general(a, b, _DIMS["nt"], preferred_element_type=F32)


def _tn(a, b):
    return lax.dot_general(a, b, _DIMS["tn"], preferred_element_type=F32)


def _nn(a, b):
    return lax.dot_general(a, b, _DIMS["nn"], preferred_element_type=F32)


HEAD_PARTS = 2


def _mlp_down_ple_head(up, x1, p, tgt, g_ple, g_post_mlp, w_down, w_pg, w_pp, seq, tr):
    nblk = seq // tr
    D = D_MODEL

    def body(up_ref, x1_ref, p_ref, t_ref, gp_ref, gm_ref, wd_ref, wpg_ref, wpp_ref,
             dx2_ref, df_ref, dpre_ref, h3_ref, dpp_ref, loss_ref, dgp_ref, dgm_ref):
        gp, gm, wpg, wpp = gp_ref[...], gm_ref[...], wpg_ref[...], wpp_ref[...]
        halves = [pl.ds(n * (tr // HEAD_PARTS), tr // HEAD_PARTS) for n in range(HEAD_PARTS)]
        w_down = wd_ref[...]
        fb = []
        for r in halves:
            hidden = jnp.maximum(up_ref[r, :].astype(F32), 0.0)
            fb.append(_nn((hidden * hidden).astype(BF16), w_down))
        loss, dgp_sum, dgm_sum = 0.0, 0.0, 0.0
        for s, r in enumerate(halves):
            x2b = x1_ref[r, :] + _rms(fb[s], gm)
            h3 = _rms(x2b, gp).astype(BF16)
            gate = _sig(_nn(h3, wpg))
            pp = _nn(p_ref[r, :].astype(BF16), wpp)
            err = x2b + gate * pp - t_ref[r, :]
            dx3 = err * (1.0 / D)
            dpre = (dx3 * pp * gate * (1.0 - gate)).astype(BF16)
            h3_ref[r, :] = h3
            dpp_ref[r, :] = (dx3 * gate).astype(BF16)
            dpre_ref[r, :] = dpre
            dxn, dgp = _rms_bwd(x2b, gp, _nt(dpre, wpg))
            dx2 = dx3 + dxn
            dx2_ref[r, :] = dx2
            dfb, dgm = _rms_bwd(fb[s], gm, dx2)
            df_ref[r, :] = dfb.astype(BF16)
            loss = loss + jnp.sum(err * err, axis=0, keepdims=True)
            dgp_sum, dgm_sum = dgp_sum + dgp, dgm_sum + dgm
        loss_ref[...] = loss * (0.5 / D)
        dgp_ref[...] = dgp_sum
        dgm_ref[...] = dgm_sum

    rows = _bs((tr, D), lambda i: (i, 0))
    vec = _bs((1, D), lambda i: (0, 0))
    part = _bs((None, 1, D), lambda i: (i, 0, 0))
    return pl.pallas_call(
        body, name="mlp_down_ple_head", grid=(nblk,),
        in_specs=[_bs((tr, D_FF), lambda i: (i, 0)), rows, _bs((tr, PLE_DIM), lambda i: (i, 0)), rows, vec, vec,
                  _bs((D_FF, D), lambda i: (0, 0)), _bs((D, D), lambda i: (0, 0)), _bs((PLE_DIM, D), lambda i: (0, 0))],
        out_specs=[rows] * 5 + [part] * 3,
        out_shape=[_sds((seq, D), F32)] + [_sds((seq, D), BF16)] * 4 + [_sds((nblk, 1, D), F32)] * 3,
        compiler_params=_cparams(1),
    )(up, x1, p, tgt, g_ple, g_post_mlp, w_down, w_pg, w_pp)


def _shift_rows_down(u, prev, n):
    rows = u.shape[0]
    ridx = lax.broadcasted_iota(jnp.int32, u.shape, 0)
    out = pltpu.roll(u, n, 0)
    for r in range(n):
        out = jnp.where(ridx == r, prev[8 - n + r:8 - n + r + 1, :], out)
    del rows
    return out


def _shift_rows_up(u, nxt, n):
    rows = u.shape[0]
    ridx = lax.broadcasted_iota(jnp.int32, u.shape, 0)
    out = pltpu.roll(u, rows - n, 0)
    for r in range(n):
        out = jnp.where(ridx == rows - n + r, nxt[r:r + 1, :], out)
    return out


CONV_COL0 = 0


def _conv_fwd(proj, w_conv, seq, tr):
    hb = tr // 8

    def body(cb_ref, cc_ref, cu_ref, ccp_ref, cup_ref, w_ref, e_ref):
        i = pl.program_id(0)
        u = cc_ref[...] * cu_ref[...]
        up = jnp.where(i > 0, ccp_ref[...] * cup_ref[...], 0.0)
        w = w_ref[...]
        d = w[0:1, :] * _shift_rows_down(u, up, 2) + w[1:2, :] * _shift_rows_down(u, up, 1) + w[2:3, :] * u
        e_ref[...] = (cb_ref[...] * d).astype(BF16)

    prev = lambda c: (lambda i: (jnp.maximum(i * hb - 1, 0), c))
    return pl.pallas_call(
        body, name="conv_fwd", grid=(seq // tr,),
        in_specs=[_bs((tr, CONV_W), lambda i: (i, CONV_COL0)),
                  _bs((tr, CONV_W), lambda i: (i, CONV_COL0 + 1)),
                  _bs((tr, CONV_W), lambda i: (i, CONV_COL0 + 2)),
                  _bs((8, CONV_W), prev(CONV_COL0 + 1)),
                  _bs((8, CONV_W), prev(CONV_COL0 + 2)),
                  _bs((3, CONV_W), lambda i: (0, 0))],
        out_specs=_bs((tr, CONV_W), lambda i: (i, 0)),
        out_shape=_sds((seq, CONV_W), BF16),
        compiler_params=_cparams(1),
    )(proj, proj, proj, proj, proj, w_conv)


def _conv_bwd(proj, de, w_conv, seq, tr):
    hb = tr // 8
    nblk = seq // tr

    def body(cb_ref, cc_ref, cu_ref, ccp_ref, cup_ref, cbn_ref, de_ref, den_ref, w_ref, o_ref, dw_ref):
        i = pl.program_id(0)
        cc, cu, cb = cc_ref[...], cu_ref[...], cb_ref[...]
        u = cc * cu
        up = jnp.where(i > 0, ccp_ref[...] * cup_ref[...], 0.0)
        u1 = _shift_rows_down(u, up, 1)
        u2 = _shift_rows_down(u, up, 2)
        de_ = de_ref[...]
        dd = de_ * cb
        ddn = jnp.where(i < nblk - 1, den_ref[...] * cbn_ref[...], 0.0)
        w = w_ref[...]
        du = w[2:3, :] * dd + w[1:2, :] * _shift_rows_up(dd, ddn, 1) + w[0:1, :] * _shift_rows_up(dd, ddn, 2)
        o_ref[:, 0:CONV_W] = (de_ * (w[0:1, :] * u2 + w[1:2, :] * u1 + w[2:3, :] * u)).astype(BF16)
        o_ref[:, CONV_W:2 * CONV_W] = (du * cu).astype(BF16)
        o_ref[:, 2 * CONV_W:3 * CONV_W] = (du * cc).astype(BF16)
        ridx = lax.broadcasted_iota(jnp.int32, (8, CONV_W), 0)
        dw0 = jnp.sum(dd * u2, axis=0, keepdims=True)
        dw1 = jnp.sum(dd * u1, axis=0, keepdims=True)
        dw2 = jnp.sum(dd * u, axis=0, keepdims=True)
        dw_ref[...] = jnp.where(ridx == 0, dw0, jnp.where(ridx == 1, dw1, jnp.where(ridx == 2, dw2, 0.0)))

    prev = lambda c: (lambda i: (jnp.maximum(i * hb - 1, 0), c))
    nxt = lambda c: (lambda i: (jnp.minimum((i + 1) * hb, seq // 8 - 1), c))
    return pl.pallas_call(
        body, name="conv_bwd", grid=(nblk,),
        in_specs=[_bs((tr, CONV_W), lambda i: (i, CONV_COL0)),
                  _bs((tr, CONV_W), lambda i: (i, CONV_COL0 + 1)),
                  _bs((tr, CONV_W), lambda i: (i, CONV_COL0 + 2)),
                  _bs((8, CONV_W), prev(CONV_COL0 + 1)),
                  _bs((8, CONV_W), prev(CONV_COL0 + 2)),
                  _bs((8, CONV_W), nxt(CONV_COL0)),
                  _bs((tr, CONV_W), lambda i: (i, 0)),
                  _bs((8, CONV_W), nxt(0)),
                  _bs((3, CONV_W), lambda i: (0, 0))],
        out_specs=[_bs((tr, 3 * CONV_W), lambda i: (i, 0)), _bs((None, 8, CONV_W), lambda i: (i, 0, 0))],
        out_shape=[_sds((seq, 3 * CONV_W), BF16), _sds((nblk, 8, CONV_W), F32)],
        compiler_params=_cparams(1),
    )(proj, proj, proj, proj, proj, proj, de, de, w_conv)


def _log_gates(z):
    lse = jnp.log(1.0 + jnp.exp(-jnp.abs(z)))
    log_beta = jnp.minimum(z, 0.0) - lse
    return log_beta, log_beta - z


DEAD_LOG_WEIGHT = -110.0
NO_TILE = -1e30


def _first_live_tile(start, scores, live_sc):
    def alive():
        return jnp.max(jnp.maximum(live_sc[0], live_sc[1])) > DEAD_LOG_WEIGHT

    def step(c):
        for h, z in enumerate(scores(c[0])):
            live_sc[h] = live_sc[h] + jnp.sum(_log_gates(z)[1], axis=-1, keepdims=True)
        return c[0] - 1, alive()

    j_end, _ = lax.while_loop(lambda c: jnp.logical_and(c[0] >= 0, c[1]), step, (start, alive()))
    return j_end + 1


def _attn_fwd(proj, seq):
    blk = ATT_BLK
    nq = seq // blk
    npair = N_HEADS // 2

    def body(q_ref, k_ref, v_ref, o_ref, z0_sc, z1_sc, w0_sc, w1_sc, tot_sc, acc_sc):
        i = pl.program_id(1)
        is_a = lax.broadcasted_iota(jnp.int32, (1, LANES), 1) < HEAD_DIM
        q2 = (q_ref[...] * Q_SCALE).astype(BF16)
        zero = jnp.zeros_like(q2)
        qs = (jnp.where(is_a, q2, zero), jnp.where(is_a, zero, q2))
        row = lax.broadcasted_iota(jnp.int32, (blk, blk), 0)
        col = lax.broadcasted_iota(jnp.int32, (blk, blk), 1)
        tri = (row > col).astype(BF16)
        causal = col < row

        def tile_of(ref, j):
            return ref[pl.ds(pl.multiple_of(j * blk, blk), blk), :].astype(BF16)

        def scores(j):
            k2 = tile_of(k_ref, j)
            return [_nt(qs[h], k2) for h in range(2)]

        has_left = i > 0
        left = jnp.maximum(i - 1, 0)

        g_d = [_log_gates(z) for z in scores(i)]
        g_l = [_log_gates(z) for z in scores(left)]
        keep_d = [jnp.where(causal, g[1], 0.0) for g in g_d]
        suf_d = [_nn(lk.astype(BF16), tri) for lk in keep_d]
        suf_l = [_nn(g[1].astype(BF16), tri) for g in g_l]
        v_d, v_l = tile_of(v_ref, i), tile_of(v_ref, left)
        pv = []
        for h in range(2):
            sum_d = jnp.sum(keep_d[h], axis=-1, keepdims=True)
            w_d = jnp.where(causal, jnp.exp(g_d[h][0] + suf_d[h]), 0.0)
            w_l = jnp.exp(g_l[h][0] + (jnp.where(has_left, sum_d, NO_TILE) + suf_l[h]))
            pv.append(_nn(w_d.astype(BF16), v_d) + _nn(w_l.astype(BF16), v_l))
            tot_sc[h] = sum_d + jnp.sum(g_l[h][1], axis=-1, keepdims=True)
        acc_sc[...] = jnp.where(is_a, pv[0], pv[1])

        z_bufs, w_bufs = (z0_sc, z1_sc), (w0_sc, w1_sc)

        def alive():
            return jnp.max(jnp.maximum(tot_sc[0], tot_sc[1])) > DEAD_LOG_WEIGHT

        def put(ref, vals):
            for h in range(2):
                ref[h] = vals[h]

        def weights(zs):
            gates = [_log_gates(z) for z in zs]
            sums = [_nn(g[1].astype(BF16), tri) for g in gates]
            ws = []
            for h in range(2):
                ws.append(jnp.exp(gates[h][0] + (tot_sc[h] + sums[h])).astype(BF16))
                tot_sc[h] = tot_sc[h] + jnp.sum(gates[h][1], axis=-1, keepdims=True)
            return ws

        def add_values(w_buf, j):
            v2 = tile_of(v_ref, j)
            acc_sc[...] += jnp.where(is_a, _nn(w_buf[0], v2), _nn(w_buf[1], v2))

        def trip(j, s):
            add_values(w_bufs[s], j + 1)
            put(z_bufs[1 - s], scores(jnp.maximum(j - 1, 0)))
            put(w_bufs[1 - s], weights((z_bufs[s][0], z_bufs[s][1])))

        @pl.when(jnp.logical_and(i >= 2, alive()))
        def _():
            put(z0_sc, scores(i - 2))
            w0_sc[...] = jnp.zeros_like(w0_sc)

            def two_trips(c):
                trip(c[0], 0)
                trip(c[0] - 1, 1)
                return c[0] - 2, alive()

            j_next, still = lax.while_loop(lambda c: jnp.logical_and(c[0] >= 1, c[1]), two_trips, (i - 2, i >= 2))
            one_left = jnp.logical_and(j_next == 0, still)

            @pl.when(one_left)
            def _():
                trip(0, 0)
                add_values(w1_sc, 0)

            @pl.when(jnp.logical_not(one_left))
            def _():
                add_values(w0_sc, j_next + 1)

        o_ref[...] = acc_sc[...].astype(BF16)

    return pl.pallas_call(
        body, name="attn_fwd", grid=(npair, nq),
        in_specs=[_bs((blk, LANES), lambda p, i: (i, p)),
                  _bs((seq, LANES), lambda p, i: (0, npair + p)),
                  _bs((seq, LANES), lambda p, i: (0, 2 * npair + p))],
        out_specs=_bs((blk, LANES), lambda p, i: (i, p)),
        out_shape=_sds((seq, ATTN_W), BF16),
        scratch_shapes=[pltpu.VMEM((2, blk, blk), F32), pltpu.VMEM((2, blk, blk), F32),
                        pltpu.VMEM((2, blk, blk), BF16), pltpu.VMEM((2, blk, blk), BF16),
                        pltpu.VMEM((2, blk, 1), F32), pltpu.VMEM((blk, LANES), F32)],
        compiler_params=_cparams(2),
    )(proj, proj, proj)


def _attn_bwd(proj, do, seq):
    blk = ATT_BLK
    nq = seq // blk
    npair = N_HEADS // 2

    def body(q_ref, k_ref, v_ref, do_ref, dq_ref, dk_ref, dv_ref,
             prod0_sc, prod1_sc, pend0_sc, pend1_sc, tot_sc, live_sc, cum_sc, pre_sc, dq_sc):
        i = pl.program_id(1)

        @pl.when(i == 0)
        def _():
            dk_ref[...] = jnp.zeros_like(dk_ref)
            dv_ref[...] = jnp.zeros_like(dv_ref)

        is_a = lax.broadcasted_iota(jnp.int32, (1, LANES), 1) < HEAD_DIM
        q2 = (q_ref[...] * Q_SCALE).astype(BF16)
        do2 = do_ref[...]
        zero = jnp.zeros_like(q2)
        qs = (jnp.where(is_a, q2, zero), jnp.where(is_a, zero, q2))
        dos = (jnp.where(is_a, do2, zero), jnp.where(is_a, zero, do2))
        row = lax.broadcasted_iota(jnp.int32, (blk, blk), 0)
        col = lax.broadcasted_iota(jnp.int32, (blk, blk), 1)
        tri_after = (row > col).astype(BF16)
        tri_excl = (row < col).astype(BF16)
        causal = col < row

        def tile_of(ref, j):
            return ref[pl.ds(pl.multiple_of(j * blk, blk), blk), :].astype(BF16)

        def scores(j):
            k2 = tile_of(k_ref, j)
            return [_nt(qs[h], k2) for h in range(2)]

        def products(j):
            v2 = tile_of(v_ref, j)
            return scores(j) + [_nt(dos[h], v2) for h in range(2)]

        def row_sum(a):
            return jnp.sum(a, axis=-1, keepdims=True)

        def grad_matmuls(ws, dzs, j):
            rows = pl.ds(pl.multiple_of(j * blk, blk), blk)
            k2 = tile_of(k_ref, j)
            dq_sc[...] += jnp.where(is_a, _nn(dzs[0], k2), _nn(dzs[1], k2))
            dk_ref[rows, :] += jnp.where(is_a, _tn(dzs[0], q2), _tn(dzs[1], q2))
            if ws is not None:
                dv_ref[rows, :] += jnp.where(is_a, _tn(ws[0], do2), _tn(ws[1], do2))

        has_left = i > 0
        left = jnp.maximum(i - 1, 0)

        p_d, p_l = products(i), products(left)
        g_d = [_log_gates(z) for z in p_d[:2]]
        g_l = [_log_gates(z) for z in p_l[:2]]
        keep_d = [jnp.where(causal, g[1], 0.0) for g in g_d]
        suf_d = [_nn(lk.astype(BF16), tri_after) for lk in keep_d]
        suf_l = [_nn(g[1].astype(BF16), tri_after) for g in g_l]
        w_d, w_l, gg_d, gg_l = [], [], [], []
        for h in range(2):
            sum_d = row_sum(keep_d[h])
            w_d.append(jnp.where(causal, jnp.exp(g_d[h][0] + suf_d[h]), 0.0))
            w_l.append(jnp.exp(g_l[h][0] + (jnp.where(has_left, sum_d, NO_TILE) + suf_l[h])))
            gg_d.append(p_d[2 + h] * w_d[h])
            gg_l.append(p_l[2 + h] * w_l[h])
            tot_sc[h] = sum_d + row_sum(g_l[h][1])
        before_d = [_nn(g.astype(BF16), tri_excl) for g in gg_d]
        before_l = [_nn(g.astype(BF16), tri_excl) for g in gg_l]
        dz_d, dz_l = [], []
        for h in range(2):
            beta_d, beta_l = jnp.exp(g_d[h][0]), jnp.exp(g_l[h][0])
            dz_l.append((gg_l[h] * (1.0 - beta_l) - before_l[h] * beta_l).astype(BF16))
            dz = gg_d[h] * (1.0 - beta_d) - (row_sum(gg_l[h]) + before_d[h]) * beta_d
            dz_d.append(jnp.where(causal, dz, 0.0).astype(BF16))
        dq_sc[...] = jnp.zeros_like(dq_sc)
        grad_matmuls([w.astype(BF16) for w in w_l], dz_l, left)
        grad_matmuls([w.astype(BF16) for w in w_d], dz_d, i)

        live_sc[...] = tot_sc[...]
        first = _first_live_tile(i - 2, scores, live_sc)
        trips = i - 1 - first
        prod_bufs, pend_bufs = (prod0_sc, prod1_sc), (pend0_sc, pend1_sc)

        def local_grads(prods):
            zs, dws = prods[:2], prods[2:]
            gates = [_log_gates(z) for z in zs]
            sums = [_nn(g[1].astype(BF16), tri_after) for g in gates]
            ws, gs = [], []
            for h in range(2):
                cum = cum_sc[h] + row_sum(gates[h][1])
                cum_sc[h] = cum
                ws.append(jnp.exp(gates[h][0] + ((live_sc[h] - cum) + sums[h])))
                gs.append(dws[h] * ws[h])
            befores = [_nn(g.astype(BF16), tri_excl) for g in gs]
            dzs = []
            for h in range(2):
                beta = jnp.exp(gates[h][0])
                dzs.append((gs[h] * (1.0 - beta) - (pre_sc[h] + befores[h]) * beta).astype(BF16))
                pre_sc[h] = pre_sc[h] + row_sum(gs[h])
            return [w.astype(BF16) for w in ws] + dzs

        def put(ref, vals):
            for n, val in enumerate(vals):
                ref[n] = val

        def flush(pend, j):
            grad_matmuls([pend[0], pend[1]], [pend[2], pend[3]], j)

        def trip(j, s):
            flush(pend_bufs[s], jnp.maximum(j - 1, first))
            put(prod_bufs[1 - s], products(j + 1))
            put(pend_bufs[1 - s], local_grads([prod_bufs[s][n] for n in range(4)]))

        def earlier_keys_share(j, mask):
            dzs = []
            for h, z in enumerate(scores(j)):
                beta = jnp.exp(_log_gates(z)[0])
                dzs.append(jnp.where(mask, -pre_sc[h] * beta, 0.0).astype(BF16))
            grad_matmuls(None, dzs, j)

        @pl.when(trips > 0)
        def _():
            cum_sc[...] = jnp.zeros_like(cum_sc)
            pre_sc[...] = jnp.zeros_like(pre_sc)
            pend0_sc[...] = jnp.zeros_like(pend0_sc)
            put(prod0_sc, products(first))

            def two_trips(pp, carry):
                trip(first + 2 * pp, 0)
                trip(first + 2 * pp + 1, 1)
                return carry

            lax.fori_loop(0, trips // 2, two_trips, 0)
            odd = trips % 2 == 1

            @pl.when(odd)
            def _():
                trip(i - 2, 0)
                flush(pend1_sc, i - 2)

            @pl.when(jnp.logical_not(odd))
            def _():
                flush(pend0_sc, i - 2)

            earlier_keys_share(i - 1, True)
            earlier_keys_share(i, causal)

        dq_ref[...] = dq_sc[...] * Q_SCALE

    qmap = lambda p, i: (i, p)
    return pl.pallas_call(
        body, name="attn_bwd", grid=(npair, nq),
        in_specs=[_bs((blk, LANES), qmap),
                  _bs((seq, LANES), lambda p, i: (0, npair + p)),
                  _bs((seq, LANES), lambda p, i: (0, 2 * npair + p)),
                  _bs((blk, LANES), qmap)],
        out_specs=[_bs((blk, LANES), qmap),
                   _bs((seq, LANES), lambda p, i: (0, p)),
                   _bs((seq, LANES), lambda p, i: (0, p))],
        out_shape=[_sds((seq, ATTN_W), F32)] * 3,
        scratch_shapes=[pltpu.VMEM((4, blk, blk), F32), pltpu.VMEM((4, blk, blk), F32),
                        pltpu.VMEM((4, blk, blk), BF16), pltpu.VMEM((4, blk, blk), BF16),
                        pltpu.VMEM((2, blk, 1), F32), pltpu.VMEM((2, blk, 1), F32), pltpu.VMEM((2, blk, 1), F32),
                        pltpu.VMEM((2, blk, 1), F32), pltpu.VMEM((blk, LANES), F32)],
        compiler_params=_cparams(2),
    )(proj, proj, proj, do)


def _elementwise(name, fn, ins, out_dtypes):
    rows, cols = ins[0].shape
    tr = rows
    for cand in (512, 256, 128, 64, 32, 16, 8):
        if rows % cand == 0 and cand * cols * 4 <= 2 * 1024 * 1024:
            tr = cand
            break
    n_in = len(ins)

    def body(*refs):
        res = fn(*[r[...] for r in refs[:n_in]])
        for r, val in zip(refs[n_in:], res):
            r[...] = val.astype(r.dtype)

    spec = _bs((tr, cols), lambda i: (i, 0))
    return pl.pallas_call(
        body, name=name, grid=(rows // tr,),
        in_specs=[spec] * n_in, out_specs=[spec] * len(out_dtypes),
        out_shape=[_sds((rows, cols), dt) for dt in out_dtypes],
        compiler_params=_cparams(1),
    )(*ins)


def _adamw_fn(w, g, m, v):
    m = ADAM_B1 * m + (1.0 - ADAM_B1) * g
    v = ADAM_B2 * v + (1.0 - ADAM_B2) * (g * g)
    m_hat = m / (1.0 - ADAM_B1 ** ADAM_STEP)
    v_hat = v / (1.0 - ADAM_B2 ** ADAM_STEP)
    delta = -ADAM_LR * (m_hat / (jnp.sqrt(v_hat) + ADAM_EPS) + ADAM_WD * w)
    return delta, m, v


def _adamw(name, w, g, m, v):
    shape = w.shape
    as2d = lambda a: a.reshape(-1, shape[-1])
    delta, nm, nv = _elementwise(name, _adamw_fn, [as2d(w), as2d(g), as2d(m), as2d(v)], [F32, F32, F32])
    return delta.reshape(shape), nm.reshape(shape), nv.reshape(shape)


def _place():
    return lax.axis_index("x"), lax.axis_index("y"), lax.axis_index("c")


ANY = pl.BlockSpec(memory_space=pl.ANY)
VMEM_WHOLE = pl.BlockSpec(memory_space=pltpu.VMEM)


def _allgather_weights(shards):
    n = len(shards)

    def body(*refs):
        src, dst = refs[:n], refs[n:2 * n]
        send_sems, recv_sems, local_sems = refs[2 * n:]
        x, y, c = _place()
        me, sibling, mychip = (x, y, c), (x, y, 1 - c), 2 * x + y

        x_nbr, y_nbr, diag = 2 * (1 - x) + y, 2 * x + (1 - y), 2 * (1 - x) + (1 - y)
        to_x, to_y = (1 - x, y, c), (x, 1 - y, c)

        def parts(w):
            hr = src[w].shape[0] // 2
            first = hr // 2 if hr % 32 == 0 else hr
            return first, hr - first

        def rows_of(w, chip, half, route):
            hr = src[w].shape[0] // 2
            first, second = parts(w)
            start, size = {0: (0, hr), 1: (0, hr), 2: (0, first), 3: (first, second)}[route]
            return dst[w].at[chip, pl.ds(half * hr + start, size)]

        def copy(w, k, src_ref, dst_ref, to):
            return pltpu.make_async_remote_copy(src_ref=src_ref, dst_ref=dst_ref, send_sem=send_sems.at[w, k],
                                                recv_sem=recv_sems.at[w, k], device_id=to, device_id_type=MESH)

        def landed(w, route):
            chip = {0: x_nbr, 1: y_nbr, 2: diag, 3: diag}[route]
            return rows_of(w, chip, c, route), chip

        def routes(w):
            return (0, 1, 2, 3) if parts(w)[1] else (0, 1, 2)

        started, local = [], []
        for w in range(n):
            hr = src[w].shape[0] // 2
            own = pltpu.make_async_copy(src[w], dst[w].at[mychip], local_sems.at[w])
            own.start()
            local.append(own)
            mine = src[w].at[pl.ds(c * hr, hr)]
            for route, to in ((0, to_x), (1, to_y)):
                cp = copy(w, route, mine, rows_of(w, mychip, c, route), to)
                cp.start()
                started.append(cp)

        def pass_on(w, route):
            got, chip = landed(w, route)
            copy(w, route, got, got, me).wait_recv()
            if route == 1:
                part = rows_of(w, chip, c, 2)
                started.append(copy(w, 2, part, part, to_x))
                started[-1].start()
            if route == 0 and parts(w)[1]:
                part = rows_of(w, chip, c, 3)
                started.append(copy(w, 3, part, part, to_y))
                started[-1].start()
            started.append(copy(w, 4 + route, got, got, sibling))
            started[-1].start()

        for w in range(n):
            pass_on(w, 1)
            pass_on(w, 0)
        for w in range(n):
            for route in routes(w)[2:]:
                pass_on(w, route)
        for w in range(n):
            for route in routes(w):
                chip = landed(w, route)[1]
                from_sib = rows_of(w, chip, 1 - c, route)
                copy(w, 4 + route, from_sib, from_sib, me).wait_recv()
        for cp in local:
            cp.wait()
        for cp in started:
            cp.wait_send()

    return pl.pallas_call(
        body, name="allgather_weights",
        in_specs=[VMEM_WHOLE] * n, out_specs=[VMEM_WHOLE] * n,
        out_shape=[_sds((N_CHIPS,) + s.shape, s.dtype) for s in shards],
        scratch_shapes=[pltpu.SemaphoreType.DMA((n, 8)), pltpu.SemaphoreType.DMA((n, 8)),
                        pltpu.SemaphoreType.DMA((n,))],
        compiler_params=pltpu.CompilerParams(vmem_limit_bytes=VMEM_LIMIT),
    )(*shards)


SUM_ROWS = 64


def _rs_pair_sum(name, grads):
    n = len(grads)

    def body(*refs):
        g, out = refs[:n], refs[n:2 * n]
        stage, give16, land, keep = (refs[m * n:(m + 1) * n] for m in range(2, 6))
        send_sems, recv_sems, stage_sems, keep_sems = refs[6 * n:]
        x, y, c = _place()
        sibling = (x, y, 1 - c)

        def over_rows(w, fn):
            nb = g[w].shape[1] // 2 // SUM_ROWS

            def step(idx, carry):
                fn(idx // nb, pl.ds(pl.multiple_of((idx % nb) * SUM_ROWS, SUM_ROWS), SUM_ROWS))
                return carry

            lax.fori_loop(0, N_CHIPS * nb, step, 0)

        loads = []
        for w in range(n):
            hr = g[w].shape[1] // 2
            st = pltpu.make_async_copy(g[w].at[:, pl.ds((1 - c) * hr, hr)], stage[w], stage_sems.at[w])
            kp = pltpu.make_async_copy(g[w].at[:, pl.ds(c * hr, hr)], keep[w], keep_sems.at[w])
            st.start()
            kp.start()
            loads.append((st, kp))
        gives = []
        for w in range(n):
            loads[w][0].wait()

            def narrow(k, rows, w=w):
                give16[w][k, rows, :] = stage[w][k, rows, :].astype(BF16)

            over_rows(w, narrow)
            give = pltpu.make_async_remote_copy(src_ref=give16[w], dst_ref=land[w], send_sem=send_sems.at[w],
                                                recv_sem=recv_sems.at[w], device_id=sibling, device_id_type=MESH)
            give.start()
            gives.append(give)
        for w in range(n):
            loads[w][1].wait()
            gives[w].wait_recv()

            def add(k, rows, w=w):
                out[w][k, rows, :] = (keep[w][k, rows, :] + land[w][k, rows, :].astype(F32)).astype(BF16)

            over_rows(w, add)
        for give in gives:
            give.wait_send()

    half = [(N_CHIPS, a.shape[1] // 2, a.shape[2]) for a in grads]
    wide = [pltpu.VMEM(s, F32) for s in half]
    narrow_bufs = [pltpu.VMEM(s, BF16) for s in half]
    sems = pltpu.SemaphoreType.DMA((n,))
    return pl.pallas_call(
        body, name=name,
        in_specs=[ANY] * n, out_specs=[VMEM_WHOLE] * n, out_shape=[_sds(s, BF16) for s in half],
        scratch_shapes=wide + narrow_bufs + narrow_bufs + wide + [sems, sems, sems, sems],
        compiler_params=pltpu.CompilerParams(vmem_limit_bytes=VMEM_LIMIT),
    )(*grads)


def _rs_exchange_join(parts):
    n = len(parts)

    def body(*refs):
        t, full = refs[:n], refs[n:2 * n]
        got_x, got_y, pass_on, got_2 = (refs[m * n:(m + 1) * n] for m in range(2, 6))
        send_sems, recv_sems = refs[6 * n:]
        x, y, c = _place()
        mychip, sibling = 2 * x + y, (x, y, 1 - c)
        x_nbr, y_nbr, diag = 2 * (1 - x) + y, 2 * x + (1 - y), 2 * (1 - x) + (1 - y)
        to_x, to_y = (1 - x, y, c), (x, 1 - y, c)
        sends = []

        def copy(w, k, src_ref, dst_ref, to):
            return pltpu.make_async_remote_copy(src_ref=src_ref, dst_ref=dst_ref, send_sem=send_sems.at[w, k],
                                                recv_sem=recv_sems.at[w, k], device_id=to, device_id_type=MESH)

        def start(cp):
            cp.start()
            sends.append(cp)

        def add_rows(w, count, fn):
            def step(idx, carry):
                fn(pl.ds(pl.multiple_of(idx * SUM_ROWS, SUM_ROWS), SUM_ROWS), pl.multiple_of(idx * SUM_ROWS, SUM_ROWS))
                return carry
            lax.fori_loop(0, count // SUM_ROWS, step, 0)

        f32 = lambda v: v.astype(F32)
        for w in range(n):
            ha = t[w].shape[1] // 2
            part_a, part_b = pl.ds(0, ha), pl.ds(ha, ha)
            start(copy(w, 0, t[w].at[x_nbr, part_a], got_x[w].at[0], to_x))
            start(copy(w, 1, t[w].at[diag, part_a], got_x[w].at[1], to_x))
            start(copy(w, 2, t[w].at[y_nbr, part_b], got_y[w].at[0], to_y))
            start(copy(w, 3, t[w].at[diag, part_b], got_y[w].at[1], to_y))
        for w in range(n):
            hr = t[w].shape[1]
            ha = hr // 2
            for k in (0, 1):
                copy(w, k, got_x[w].at[k], got_x[w].at[k], to_x).wait_recv()

            def sum_a(rows, r, w=w, hr=hr):
                full[w][pl.ds(pl.multiple_of(c * hr + r, SUM_ROWS), SUM_ROWS), :] = \
                    f32(t[w][mychip, rows, :]) + f32(got_x[w][0, rows, :])
                pass_on[w][rows, :] = (f32(t[w][y_nbr, rows, :]) + f32(got_x[w][1, rows, :])).astype(BF16)

            add_rows(w, ha, sum_a)
            start(copy(w, 4, pass_on[w].at[pl.ds(0, ha)], got_2[w].at[pl.ds(0, ha)], to_y))
            for k in (2, 3):
                copy(w, k, got_y[w].at[k - 2], got_y[w].at[k - 2], to_y).wait_recv()

            def sum_b(rows, r, w=w, hr=hr, ha=ha):
                lower = pl.ds(pl.multiple_of(ha + r, SUM_ROWS), SUM_ROWS)
                full[w][pl.ds(pl.multiple_of(c * hr + ha + r, SUM_ROWS), SUM_ROWS), :] = \
                    f32(t[w][mychip, lower, :]) + f32(got_y[w][0, rows, :])
                pass_on[w][lower, :] = (f32(t[w][x_nbr, lower, :]) + f32(got_y[w][1, rows, :])).astype(BF16)

            add_rows(w, ha, sum_b)
            start(copy(w, 5, pass_on[w].at[pl.ds(ha, ha)], got_2[w].at[pl.ds(ha, ha)], to_x))
        for w in range(n):
            hr = t[w].shape[1]
            ha = hr // 2
            copy(w, 4, got_2[w].at[pl.ds(0, ha)], got_2[w].at[pl.ds(0, ha)], to_y).wait_recv()
            copy(w, 5, got_2[w].at[pl.ds(ha, ha)], got_2[w].at[pl.ds(ha, ha)], to_x).wait_recv()

            def finish(rows, r, w=w, hr=hr):
                out_rows = pl.ds(pl.multiple_of(c * hr + r, SUM_ROWS), SUM_ROWS)
                full[w][out_rows, :] = full[w][out_rows, :] + f32(got_2[w][rows, :])

            add_rows(w, hr, finish)
            mine = full[w].at[pl.ds(c * hr, hr)]
            start(copy(w, 6, mine, mine, sibling))
        for w in range(n):
            hr = t[w].shape[1]
            theirs = full[w].at[pl.ds((1 - c) * hr, hr)]
            copy(w, 6, theirs, theirs, sibling).wait_recv()
        for cp in sends:
            cp.wait_send()

    half = lambda a: pltpu.VMEM((2, a.shape[1] // 2, a.shape[2]), a.dtype)
    whole = lambda a: pltpu.VMEM(a.shape[1:], a.dtype)
    return pl.pallas_call(
        body, name="rs_exchange_join",
        in_specs=[VMEM_WHOLE] * n, out_specs=[VMEM_WHOLE] * n,
        out_shape=[_sds((2 * a.shape[1], a.shape[2]), F32) for a in parts],
        scratch_shapes=[half(a) for a in parts] + [half(a) for a in parts] + [whole(a) for a in parts]
        + [whole(a) for a in parts] + [pltpu.SemaphoreType.DMA((n, 7)), pltpu.SemaphoreType.DMA((n, 7))],
        compiler_params=pltpu.CompilerParams(vmem_limit_bytes=VMEM_LIMIT),
    )(*parts)


def _small_allreduce(loss_p, dg_parts, dbg_a, dbg_c, dwc):
    ins = [loss_p] + list(dg_parts) + [dbg_a, dbg_c, dwc]
    n_in = len(ins)
    vmem = pl.BlockSpec(memory_space=pltpu.VMEM)

    def body(*refs):
        in_refs = refs[:n_in]
        out_ref, vec, buf, send_sems, recv_sems = refs[n_in:]
        x, y, c = _place()
        me = 4 * x + 2 * y + c
        vec[...] = jnp.zeros_like(vec)
        vec[0:1, :] = jnp.sum(in_refs[0][...], axis=0)
        for r in range(5):
            vec[1 + r:2 + r, :] = jnp.sum(in_refs[1 + r][...], axis=0)
        vec[6:7, :] = jnp.sum(in_refs[6][...], axis=0)
        vec[7:8, :] = jnp.sum(in_refs[7][...], axis=0)
        vec[8:16, 0:CONV_W] = jnp.sum(in_refs[8][...], axis=0)
        buf[pl.ds(me, 1)] = vec[...][None]
        copies = []
        for r in range(1, 8):
            fx, fy, fc = (r >> 2) & 1, (r >> 1) & 1, r & 1
            to = (1 - x if fx else x, 1 - y if fy else y, 1 - c if fc else c)
            cp = pltpu.make_async_remote_copy(src_ref=vec, dst_ref=buf.at[me], send_sem=send_sems.at[r - 1],
                                              recv_sem=recv_sems.at[r - 1], device_id=to, device_id_type=MESH)
            cp.start()
            copies.append(cp)
        for cp in copies:
            cp.wait()
        total = buf[0]
        for s in range(1, 8):
            total = total + buf[s]
        out_ref[...] = total
        out_ref[0:1, :] = jnp.broadcast_to(jnp.sum(total[0:1, :], axis=-1, keepdims=True), (1, D_MODEL))

    return pl.pallas_call(
        body, name="small_allreduce",
        in_specs=[vmem] * n_in, out_specs=vmem, out_shape=_sds((SMALL_ROWS, D_MODEL), F32),
        scratch_shapes=[pltpu.VMEM((SMALL_ROWS, D_MODEL), F32), pltpu.VMEM((8, SMALL_ROWS, D_MODEL), F32),
                        pltpu.SemaphoreType.DMA((7,)), pltpu.SemaphoreType.DMA((7,))],
    )(*ins)


def _local_step(x, p, tgt, g, b_gate, w_conv, wf):
    seq = x.shape[0]
    tm = min(seq, 1024)
    th = min(seq, 512)
    tl = min(seq, 2048)
    ni, nh, nl = seq // tm, seq // th, seq // tl
    g_pre_mix, g_post_mix, g_pre_mlp, g_post_mlp, g_ple = g
    w_in_nat, w_ao, w_co, w_o, w_up_nat, w_down, w_pg, w_pp = wf
    D = D_MODEL
    vec = lambda a, blk=0: (a, _bs((1, D), lambda i, j, k: (0, blk)))
    rows_i = lambda a, t, blk=0: (a, _bs((t, D), lambda i, j, k: (i, blk)))
    rows_k = lambda a, t, blk=0: (a, _bs((t, D), lambda i, j, k: (k, blk)))
    part = lambda n: (_sds((n, 1, D), F32), _bs((None, 1, D), lambda i, j, k: (i, 0, 0)))
    full2 = lambda a: (a, _bs(a.shape, lambda i, j, k: (0, 0)))

    normed = lambda xb, gb: (_rms(xb, gb).astype(BF16),) * 2
    keep_a = lambda t: [(_sds((seq, D), BF16), _bs((t, D), lambda i, j, k: (i, 0)))]
    qkv_w, conv_w = 3 * ATTN_W, 3 * CONV_W
    qkv, proj_conv, gates, h1 = _mm(
        "proj_in", "nn", (nh, 1, 1),
        a_ins=[rows_i(x, th), vec(g_pre_mix)], a_fn=normed, b_ins=[full2(w_in_nat)], b_fn=_ident,
        epi_fn=lambda acc: (acc[:, :qkv_w], acc[:, qkv_w:qkv_w + conv_w], acc[:, qkv_w + conv_w:]),
        outs=[(_sds((seq, qkv_w), BF16), _bs((th, qkv_w), lambda i, j, k: (i, 0))),
              (_sds((seq, conv_w), F32), _bs((th, conv_w), lambda i, j, k: (i, 0))),
              (_sds((seq, 2 * D), BF16), _bs((th, 2 * D), lambda i, j, k: (i, 0)))],
        acc_shape=(th, D_IN), a_cache=((th, D), BF16), a_outs=keep_a(th))
    o = _attn_fwd(qkv, seq)
    e = _conv_fwd(proj_conv, w_conv, seq, tm)

    def gate_values(ga, gc, ba, bc):
        return _sig(ga.astype(F32) + ba), _sig(gc.astype(F32) + bc)

    def branch_outputs(ob, eb, wao, wco):
        return _nn(ob, wao).astype(BF16).astype(F32), _nn(eb, wco).astype(BF16).astype(F32)

    def mix_fn(ga, gc, ob, eb, ba, bc, wao, wco):
        sa, sc = gate_values(ga, gc, ba, bc)
        ya, yc = branch_outputs(ob, eb, wao, wco)
        return ((sa * ya + sc * yc).astype(BF16),) * 2

    def post_mix(acc, xb, gb):
        return acc, xb + _rms(acc, gb)

    half_rows = lambda a: (a, _bs((th, a.shape[1]), lambda i, j, k: (i, 0)))
    mix_ins = [rows_i(gates, th, 0), rows_i(gates, th, 1), half_rows(o), half_rows(e), vec(b_gate, 0), vec(b_gate, 1),
               full2(w_ao), full2(w_co)]
    mixed, x1, mixin = _mm(
        "mix_out", "nn", (nh, 1, 1),
        a_ins=mix_ins, a_fn=mix_fn, b_ins=[full2(w_o)], b_fn=_ident,
        epi_ins=[rows_i(x, th), vec(g_post_mix)], epi_fn=post_mix,
        outs=[(_sds((seq, D), BF16), _bs((th, D), lambda i, j, k: (i, 0))),
              (_sds((seq, D), F32), _bs((th, D), lambda i, j, k: (i, 0)))],
        acc_shape=(th, D), a_cache=((th, D), BF16), a_outs=keep_a(th))
    up, h2 = _mm("mlp_up", "nn", (nh, 1, 1),
                 a_ins=[rows_i(x1, th), vec(g_pre_mlp)], a_fn=normed,
                 b_ins=[full2(w_up_nat)], b_fn=_ident,
                 outs=[(_sds((seq, D_FF), BF16), _bs((th, D_FF), lambda i, j, k: (i, 0)))],
                 acc_shape=(th, D_FF), a_cache=((th, D), BF16), a_outs=keep_a(th))

    def relu2(ub):
        r = jnp.maximum(ub.astype(F32), 0.0)
        return (r * r).astype(BF16)

    dx2, df, dpre, h3, dpp, loss_p, dg_ple_p, dg_post_mlp_p = _mlp_down_ple_head(
        up, x1, p, tgt, g_ple, g_post_mlp, w_down, w_pg, w_pp, seq, th)

    (dw_pp,) = _mm("dw_ple_proj", "tn", (1, 1, nh),
                   a_ins=[(p, _bs((th, PLE_DIM), lambda i, j, k: (k, 0)))], a_fn=_to_bf16,
                   b_ins=[rows_k(dpp, th)], b_fn=_ident,
                   outs=[(_sds((PLE_DIM, D), F32), _bs((PLE_DIM, D), lambda i, j, k: (0, 0)))],
                   acc_shape=(PLE_DIM, D))
    (dw_pg,) = _mm("dw_ple_gate", "tn", (1, 1, nl),
                   a_ins=[rows_k(h3, tl)], a_fn=_ident, b_ins=[rows_k(dpre, tl)], b_fn=_ident,
                   outs=[(_sds((D, D), F32), _bs((D, D), lambda i, j, k: (0, 0)))], acc_shape=(D, D))

    def dup_fn(acc, ub):
        return (acc * (2.0 * jnp.maximum(ub.astype(F32), 0.0)),)

    (dup,) = _mm("d_mlp_down", "nt", (nh, 1, 1),
                 a_ins=[rows_i(df, th)], a_fn=_ident, b_ins=[full2(w_down)], b_fn=_ident,
                 epi_ins=[(up, _bs((th, D_FF), lambda i, j, k: (i, 0)))], epi_fn=dup_fn,
                 outs=[(_sds((seq, D_FF), BF16), _bs((th, D_FF), lambda i, j, k: (i, 0)))],
                 acc_shape=(th, D_FF))
    (dw_down,) = _mm("dw_mlp_down", "tn", (4, 1, nl),
                     a_ins=[(up, _bs((tl, D), lambda i, j, k: (k, i)))], a_fn=relu2,
                     b_ins=[rows_k(df, tl)], b_fn=_ident,
                     outs=[(_sds((D_FF, D), F32), _bs((D, D), lambda i, j, k: (i, 0)))], acc_shape=(D, D))
    (dw_up,) = _mm("dw_mlp_up", "tn", (1, 4, nl),
                   a_ins=[rows_k(h2, tl)], a_fn=_ident,
                   b_ins=[(dup, _bs((tl, D), lambda i, j, k: (k, j)))], b_fn=_ident,
                   outs=[(_sds((N_CHIPS, D, D), F32), _bs((None, D, D), lambda i, j, k: (j, 0, 0)))],
                   acc_shape=(D, D))

    def mlp_norm_bwd(acc, x1b, dx2b, mixedb, g_mlp, g_mix):
        dxn, dg_mlp = _rms_bwd(x1b, g_mlp, acc)
        dx1b = dx2b + dxn
        dmixedb, dg_mix = _rms_bwd(mixedb.astype(F32), g_mix, dx1b)
        return dx1b, dmixedb, dg_mlp, dg_mix

    dx1, dmixed, dg_pre_mlp_p, dg_post_mix_p = _mm(
        "d_mlp_up", "nt", (nh, 1, 1),
        a_ins=[(dup, _bs((th, D_FF), lambda i, j, k: (i, 0)))], a_fn=_ident,
        b_ins=[full2(w_up_nat)], b_fn=_ident,
        epi_ins=[rows_i(x1, th), rows_i(dx2, th), rows_i(mixed, th), vec(g_pre_mlp), vec(g_post_mix)],
        epi_fn=mlp_norm_bwd,
        outs=[(_sds((seq, D), F32), _bs((th, D), lambda i, j, k: (i, 0))),
              (_sds((seq, D), BF16), _bs((th, D), lambda i, j, k: (i, 0))), part(nh), part(nh)],
        acc_shape=(th, D))
    (dw_o,) = _mm("dw_mix_out", "tn", (1, 1, nl),
                  a_ins=[rows_k(mixin, tl)], a_fn=_ident, b_ins=[rows_k(dmixed, tl)], b_fn=_ident,
                  outs=[(_sds((D, D), F32), _bs((D, D), lambda i, j, k: (0, 0)))], acc_shape=(D, D))

    def gate_bwd(acc, ga, gc, ob, eb, ba, bc, wao, wco):
        sa, sc = gate_values(ga, gc, ba, bc)
        ya, yc = branch_outputs(ob, eb, wao, wco)
        dga = acc * ya * sa * (1.0 - sa)
        dgc = acc * yc * sc * (1.0 - sc)
        dya, dyc = (acc * sa).astype(BF16), (acc * sc).astype(BF16)
        return (dya, dyc, jnp.concatenate([dga, dgc], axis=1), _nt(dya, wao), _nt(dyc, wco),
                jnp.sum(dga, axis=0, keepdims=True), jnp.sum(dgc, axis=0, keepdims=True))

    dya, dyc, dgate, do, de, dbg_a_p, dbg_c_p = _mm(
        "d_mix_out", "nt", (nh, 1, 1),
        a_ins=[rows_i(dmixed, th)], a_fn=_ident, b_ins=[full2(w_o)], b_fn=_ident,
        epi_ins=mix_ins, epi_fn=gate_bwd,
        outs=[(_sds((seq, D), BF16), _bs((th, D), lambda i, j, k: (i, 0)))] * 2
             + [(_sds((seq, 2 * D), BF16), _bs((th, 2 * D), lambda i, j, k: (i, 0))),
                (_sds((seq, ATTN_W), BF16), _bs((th, ATTN_W), lambda i, j, k: (i, 0))),
                (_sds((seq, CONV_W), F32), _bs((th, CONV_W), lambda i, j, k: (i, 0))), part(nh), part(nh)],
        acc_shape=(th, D))
    (dw_ao,) = _mm("dw_attn_out", "tn", (1, 1, nh),
                   a_ins=[(o, _bs((th, ATTN_W), lambda i, j, k: (k, 0)))], a_fn=_ident,
                   b_ins=[rows_k(dya, th)], b_fn=_ident,
                   outs=[(_sds((ATTN_W, D), F32), _bs((ATTN_W, D), lambda i, j, k: (0, 0)))], acc_shape=(ATTN_W, D))
    dq, dk, dv = _attn_bwd(qkv, do, seq)
    (dw_co,) = _mm("dw_conv_out", "tn", (1, 1, nh),
                   a_ins=[(e, _bs((th, CONV_W), lambda i, j, k: (k, 0)))], a_fn=_ident,
                   b_ins=[rows_k(dyc, th)], b_fn=_ident,
                   outs=[(_sds((CONV_W, D), F32), _bs((CONV_W, D), lambda i, j, k: (0, 0)))], acc_shape=(CONV_W, D))
    dconv, dwc_p = _conv_bwd(proj_conv, de, w_conv, seq, tm)
    qkv_w = 3 * ATTN_W
    join_bf16 = lambda *blocks: jnp.concatenate([b.astype(BF16) for b in blocks], axis=1)
    piece = lambda a, t, rows, blk=0: (a, _bs((t, a.shape[1]), (lambda i, j, k: (k, blk)) if rows == "k"
                                             else (lambda i, j, k: (i, blk))))
    (dw_in_qkv,) = _mm("dw_proj_in_qkv", "tn", (1, 1, ni),
                       a_ins=[rows_k(h1, tm)], a_fn=_ident,
                       b_ins=[piece(dq, tm, "k"), piece(dk, tm, "k"), piece(dv, tm, "k")], b_fn=join_bf16,
                       outs=[(_sds((D, qkv_w), F32), _bs((D, qkv_w), lambda i, j, k: (0, 0)))], acc_shape=(D, qkv_w))
    (dw_in_conv,) = _mm("dw_proj_in_conv", "tn", (1, 1, nl),
                        a_ins=[rows_k(h1, tl)], a_fn=_ident, b_ins=[piece(dconv, tl, "k")], b_fn=_ident,
                        outs=[(_sds((D, 3 * CONV_W), F32), _bs((D, 3 * CONV_W), lambda i, j, k: (0, 0)))],
                        acc_shape=(D, 3 * CONV_W))
    (dw_in_gate,) = _mm("dw_proj_in_gate", "tn", (1, 2, nl),
                        a_ins=[rows_k(h1, tl)], a_fn=_ident,
                        b_ins=[(dgate, _bs((tl, D), lambda i, j, k: (k, j)))], b_fn=_ident,
                        outs=[(_sds((D, 2 * D), F32), _bs((D, D), lambda i, j, k: (0, j)))], acc_shape=(D, D))
    dw_in = jnp.concatenate([dw_in_qkv, dw_in_conv, dw_in_gate], axis=1)

    def in_norm_bwd(acc, xb, dx1b, gb):
        dxn, dg = _rms_bwd(xb, gb, acc)
        return dx1b + dxn, dg

    grad_x, dg_pre_mix_p = _mm("d_proj_in", "nt", (nh, 1, 1),
                               a_ins=[piece(dq, th, "i"), piece(dk, th, "i"), piece(dv, th, "i"),
                                      piece(dconv, th, "i"), piece(dgate, th, "i")], a_fn=join_bf16,
                               b_ins=[full2(w_in_nat)], b_fn=_ident,
                               epi_ins=[rows_i(x, th), rows_i(dx1, th), vec(g_pre_mix)], epi_fn=in_norm_bwd,
                               outs=[(_sds((seq, D), F32), _bs((th, D), lambda i, j, k: (i, 0))), part(nh)],
                               acc_shape=(th, D))

    chip_major = lambda a: a.reshape(a.shape[0], N_CHIPS, a.shape[1] // N_CHIPS).transpose(1, 0, 2)
    big = [chip_major(dw_in), chip_major(dw_ao), chip_major(dw_co), dw_o.reshape(N_CHIPS, D // N_CHIPS, D), dw_up,
           dw_down.reshape(N_CHIPS, D_FF // N_CHIPS, D), dw_pg.reshape(N_CHIPS, D // N_CHIPS, D), chip_major(dw_pp)]
    small = (loss_p, [dg_pre_mix_p, dg_post_mix_p, dg_pre_mlp_p, dg_post_mlp_p, dg_ple_p], dbg_a_p, dbg_c_p, dwc_p)
    return grad_x, big, small


RS_GROUPS = ((0,), (4,), (5,), (1, 2, 3, 6, 7))


def _reduce_scatter(big):
    pair = [None] * len(big)
    for gi, group in enumerate(RS_GROUPS):
        for w, s in zip(group, _rs_pair_sum(f"rs_pair_sum_{gi}", [big[w] for w in group])):
            pair[w] = s
    return _rs_exchange_join(pair)


def kernel(x, p, g_pre_mix, w_in, b_gate, w_conv, w_attn_out, w_conv_out, w_o, g_post_mix, g_pre_mlp, w_up, w_down, g_post_mlp, g_ple, w_ple_gate, w_ple_proj, loss_target, m_g_pre_mix, m_w_in, m_b_gate, m_w_conv, m_w_attn_out, m_w_conv_out, m_w_o, m_g_post_mix, m_g_pre_mlp, m_w_up, m_w_down, m_g_post_mlp, m_g_ple, m_w_ple_gate, m_w_ple_proj, v_g_pre_mix, v_w_in, v_b_gate, v_w_conv, v_w_attn_out, v_w_conv_out, v_w_o, v_g_post_mix, v_g_pre_mlp, v_w_up, v_w_down, v_g_post_mlp, v_g_ple, v_w_ple_gate, v_w_ple_proj):
    mats = [w_in, w_attn_out, w_conv_out, w_o, w_up, w_down, w_ple_gate, w_ple_proj]
    mats_m = [m_w_in, m_w_attn_out, m_w_conv_out, m_w_o, m_w_up, m_w_down, m_w_ple_gate, m_w_ple_proj]
    mats_v = [v_w_in, v_w_attn_out, v_w_conv_out, v_w_o, v_w_up, v_w_down, v_w_ple_gate, v_w_ple_proj]
    gains = [g_pre_mix, g_post_mix, g_pre_mlp, g_post_mlp, g_ple]
    gains_m = [m_g_pre_mix, m_g_post_mix, m_g_pre_mlp, m_g_post_mlp, m_g_ple]
    gains_v = [v_g_pre_mix, v_g_post_mix, v_g_pre_mlp, v_g_post_mlp, v_g_ple]

    taps = jnp.concatenate([w_conv[0], jnp.zeros((CONV_PAD_ROWS - 3, LANES), F32)], axis=0)
    gathered = _allgather_weights([w[0].astype(BF16) for w in mats] + [taps])
    cols_joined = lambda a: a.transpose(1, 0, 2).reshape(a.shape[1], N_CHIPS * a.shape[2])
    rows_joined = lambda a: a.reshape(N_CHIPS * a.shape[1], a.shape[2])
    col_sharded = (0, 1, 2, 4, 7)
    wf = [cols_joined(gathered[n]) if n in col_sharded else rows_joined(gathered[n]) for n in range(8)]
    w_conv_full = cols_joined(gathered[8])[0:3, :]
    chip = 2 * lax.axis_index("x") + lax.axis_index("y")

    grad_x, big, small = _local_step(x[0], p[0, 0], loss_target[0], gains, b_gate, w_conv_full, wf)

    shard_grads = _reduce_scatter(big)
    red = _small_allreduce(*small)
    loss = red[0, 0]
    grad_gains = [red[1 + r:2 + r, :] for r in range(5)]
    grad_b_gate = jnp.concatenate([red[6:7, :], red[7:8, :]], axis=1)
    grad_w_conv = lax.dynamic_slice(red[8:11, :], (0, chip * LANES), (3, LANES))[None]

    grads_big = [gr.reshape(w.shape) for gr, w in zip(shard_grads, mats)]
    upd_big = [_adamw(f"adamw_{i}", w, gr, m, v) for i, (w, gr, m, v) in enumerate(zip(mats, grads_big, mats_m, mats_v))]
    pack = lambda vs, bg: jnp.concatenate(list(vs) + [bg.reshape(2, D_MODEL), jnp.zeros((1, D_MODEL), F32)], axis=0)
    upd_small = _adamw("adamw_small", pack(gains, b_gate), pack(grad_gains, grad_b_gate),
                       pack(gains_m, m_b_gate), pack(gains_v, v_b_gate))
    upd_conv = _adamw("adamw_conv", w_conv, grad_w_conv, m_w_conv, v_w_conv)

    def small_out(a, which):
        gains_out = [a[r:r + 1, :] for r in range(5)]
        return gains_out, a[5:7, :].reshape(1, 2 * D_MODEL)

    def ordered(g_pre_mix_, big_, b_gate_, conv_, g_rest):
        return [g_pre_mix_, big_[0], b_gate_, conv_, big_[1], big_[2], big_[3], g_rest[0], g_rest[1], big_[4], big_[5],
                g_rest[2], g_rest[3], big_[6], big_[7]]

    outs = [loss, grad_x[None]]
    outs += ordered(grad_gains[0], grads_big, grad_b_gate, grad_w_conv, grad_gains[1:])
    for which in range(3):
        g_out, b_out = small_out(upd_small[which], which)
        outs += ordered(g_out[0], [u[which] for u in upd_big], b_out, upd_conv[which], g_out[1:])
    return tuple(outs)
```

```python
import jax
import jax.numpy as jnp
from jax import lax
from jax.experimental import pallas as pl
from jax.experimental.pallas import tpu as pltpu

F32 = jnp.float32
BF16 = jnp.bfloat16
MESH = pl.DeviceIdType.MESH

D_MODEL = 1024
N_HEADS = 8
HEAD_DIM = 64
ATTN_W = N_HEADS * HEAD_DIM
CONV_W = 512
D_FF = 4096
PLE_DIM = 256
D_IN = 5120
N_CHIPS = 4
EPS = 1e-6
Q_SCALE = HEAD_DIM ** -0.5

ADAM_LR = 0.001
ADAM_B1 = 0.9
ADAM_B2 = 0.999
ADAM_EPS = 1e-08
ADAM_WD = 0.01
ADAM_STEP = 10

V7X_VMEM_BYTES = 64 * 1024 * 1024
VMEM_LIMIT = V7X_VMEM_BYTES - 8 * 1024 * 1024
LANES = 128
ATT_BLK = 256
SMALL_ROWS = 16
CONV_PAD_ROWS = 16


def _cparams(n_grid):
    return pltpu.CompilerParams(dimension_semantics=("arbitrary",) * n_grid, vmem_limit_bytes=VMEM_LIMIT)


def _bs(shape, fn):
    return pl.BlockSpec(shape, fn)


def _rms_stats(xf):
    return lax.rsqrt(jnp.mean(xf * xf, axis=-1, keepdims=True) + EPS)


def _rms(xf, g):
    return xf * _rms_stats(xf) * g


def _rms_bwd(xf, g, dy):
    r = _rms_stats(xf)
    xh = xf * r
    dyg = dy * g
    dx = r * (dyg - xh * jnp.mean(dyg * xh, axis=-1, keepdims=True))
    return dx, jnp.sum(dy * xh, axis=0, keepdims=True)


def _sig(z):
    return 1.0 / (1.0 + jnp.exp(-z))


def _ident(a):
    return a


def _to_bf16(a):
    return a.astype(BF16)


_DIMS = {"nn": (((1,), (0,)), ((), ())), "nt": (((1,), (1,)), ((), ())), "tn": (((0,), (0,)), ((), ()))}


def _mm(name, mode, grid, a_ins, a_fn, b_ins, b_fn, outs, acc_shape, epi_ins=(), epi_fn=None,
        a_cache=None, a_outs=(), epi_a=()):
    nk = grid[2]
    na, nb, ne, no, nao = len(a_ins), len(b_ins), len(epi_ins), len(outs), len(a_outs)
    assert a_cache is None or nk == 1
    assert not a_outs or a_cache is not None
    dims = _DIMS[mode]
    if epi_fn is None:
        epi_fn = lambda acc: (acc,)

    def body(*refs):
        a_refs = refs[:na]
        b_refs = refs[na:na + nb]
        e_refs = refs[na + nb:na + nb + ne]
        o_refs = refs[na + nb + ne:na + nb + ne + no]
        ao_refs = refs[na + nb + ne + no:na + nb + ne + no + nao]
        scratch = list(refs[na + nb + ne + no + nao:])
        acc_ref = scratch.pop(0) if nk > 1 else None
        a_sc = scratch.pop(0) if a_cache is not None else None
        j = pl.program_id(1)
        k = pl.program_id(2)

        def finish(acc):
            res = epi_fn(acc, *[a_refs[t][...] for t in epi_a], *[r[...] for r in e_refs])
            for r, val in zip(o_refs, res):
                r[...] = val.astype(r.dtype)

        if a_sc is not None:
            @pl.when(j == 0)
            def _():
                res = a_fn(*[r[...] for r in a_refs])
                if nao:
                    for r, val in zip(ao_refs, res[1:]):
                        r[...] = val.astype(r.dtype)
                    res = res[0]
                a_sc[...] = res
            a = a_sc[...]
        else:
            a = a_fn(*[r[...] for r in a_refs])
        b = b_fn(*[r[...] for r in b_refs])
        prod = lax.dot_general(a, b, dims, preferred_element_type=F32)
        if nk == 1:
            finish(prod)
        else:
            @pl.when(k == 0)
            def _():
                acc_ref[...] = prod

            @pl.when(k > 0)
            def _():
                acc_ref[...] += prod

            @pl.when(k == nk - 1)
            def _():
                finish(acc_ref[...])

    scratch_shapes = []
    if nk > 1:
        scratch_shapes.append(pltpu.VMEM(acc_shape, F32))
    if a_cache is not None:
        scratch_shapes.append(pltpu.VMEM(*a_cache))
    all_outs = list(outs) + list(a_outs)
    res = pl.pallas_call(
        body, name=name, grid=grid,
        in_specs=[s for _, s in a_ins] + [s for _, s in b_ins] + [s for _, s in epi_ins],
        out_specs=[s for _, s in all_outs],
        out_shape=[o for o, _ in all_outs],
        scratch_shapes=scratch_shapes,
        compiler_params=_cparams(3),
    )(*[a for a, _ in a_ins], *[a for a, _ in b_ins], *[a for a, _ in epi_ins])
    return res


def _sds(shape, dtype):
    return jax.ShapeDtypeStruct(shape, dtype)


def _nt(a, b):
    return lax.dot_general(a, b, _DIMS["nt"], preferred_element_type=F32)


def _tn(a, b):
    return lax.dot_general(a, b, _DIMS["tn"], preferred_element_type=F32)


def _nn(a, b):
    return lax.dot_general(a, b, _DIMS["nn"], preferred_element_type=F32)


HEAD_PARTS = 2


def _mlp_down_ple_head(up, x1, p, tgt, g_ple, g_post_mlp, w_down, w_pg, w_pp, seq, tr):
    nblk = seq // tr
    D = D_MODEL

    def body(up_ref, x1_ref, p_ref, t_ref, gp_ref, gm_ref, wd_ref, wpg_ref, wpp_ref,
             dx2_ref, df_ref, dpre_ref, h3_ref, dpp_ref, loss_ref, dgp_ref, dgm_ref):
        gp, gm, wpg, wpp = gp_ref[...], gm_ref[...], wpg_ref[...], wpp_ref[...]
        halves = [pl.ds(n * (tr // HEAD_PARTS), tr // HEAD_PARTS) for n in range(HEAD_PARTS)]
        w_down = wd_ref[...]
        fb = []
        for r in halves:
            hidden = jnp.maximum(up_ref[r, :].astype(F32), 0.0)
            fb.append(_nn((hidden * hidden).astype(BF16), w_down))
        loss, dgp_sum, dgm_sum = 0.0, 0.0, 0.0
        for s, r in enumerate(halves):
            x2b = x1_ref[r, :] + _rms(fb[s], gm)
            h3 = _rms(x2b, gp).astype(BF16)
            gate = _sig(_nn(h3, wpg))
            pp = _nn(p_ref[r, :].astype(BF16), wpp)
            err = x2b + gate * pp - t_ref[r, :]
            dx3 = err * (1.0 / D)
            dpre = (dx3 * pp * gate * (1.0 - gate)).astype(BF16)
            h3_ref[r, :] = h3
            dpp_ref[r, :] = (dx3 * gate).astype(BF16)
            dpre_ref[r, :] = dpre
            dxn, dgp = _rms_bwd(x2b, gp, _nt(dpre, wpg))
            dx2 = dx3 + dxn
            dx2_ref[r, :] = dx2
            dfb, dgm = _rms_bwd(fb[s], gm, dx2)
            df_ref[r, :] = dfb.astype(BF16)
            loss = loss + jnp.sum(err * err, axis=0, keepdims=True)
            dgp_sum, dgm_sum = dgp_sum + dgp, dgm_sum + dgm
        loss_ref[...] = loss * (0.5 / D)
        dgp_ref[...] = dgp_sum
        dgm_ref[...] = dgm_sum

    rows = _bs((tr, D), lambda i: (i, 0))
    vec = _bs((1, D), lambda i: (0, 0))
    part = _bs((None, 1, D), lambda i: (i, 0, 0))
    return pl.pallas_call(
        body, name="mlp_down_ple_head", grid=(nblk,),
        in_specs=[_bs((tr, D_FF), lambda i: (i, 0)), rows, _bs((tr, PLE_DIM), lambda i: (i, 0)), rows, vec, vec,
                  _bs((D_FF, D), lambda i: (0, 0)), _bs((D, D), lambda i: (0, 0)), _bs((PLE_DIM, D), lambda i: (0, 0))],
        out_specs=[rows] * 5 + [part] * 3,
        out_shape=[_sds((seq, D), F32)] + [_sds((seq, D), BF16)] * 4 + [_sds((nblk, 1, D), F32)] * 3,
        compiler_params=_cparams(1),
    )(up, x1, p, tgt, g_ple, g_post_mlp, w_down, w_pg, w_pp)


def _shift_rows_down(u, prev, n):
    rows = u.shape[0]
    ridx = lax.broadcasted_iota(jnp.int32, u.shape, 0)
    out = pltpu.roll(u, n, 0)
    for r in range(n):
        out = jnp.where(ridx == r, prev[8 - n + r:8 - n + r + 1, :], out)
    del rows
    return out


def _shift_rows_up(u, nxt, n):
    rows = u.shape[0]
    ridx = lax.broadcasted_iota(jnp.int32, u.shape, 0)
    out = pltpu.roll(u, rows - n, 0)
    for r in range(n):
        out = jnp.where(ridx == rows - n + r, nxt[r:r + 1, :], out)
    return out


CONV_COL0 = 0


def _conv_fwd(proj, w_conv, seq, tr):
    hb = tr // 8

    def body(cb_ref, cc_ref, cu_ref, ccp_ref, cup_ref, w_ref, e_ref):
        i = pl.program_id(0)
        u = cc_ref[...] * cu_ref[...]
        up = jnp.where(i > 0, ccp_ref[...] * cup_ref[...], 0.0)
        w = w_ref[...]
        d = w[0:1, :] * _shift_rows_down(u, up, 2) + w[1:2, :] * _shift_rows_down(u, up, 1) + w[2:3, :] * u
        e_ref[...] = (cb_ref[...] * d).astype(BF16)

    prev = lambda c: (lambda i: (jnp.maximum(i * hb - 1, 0), c))
    return pl.pallas_call(
        body, name="conv_fwd", grid=(seq // tr,),
        in_specs=[_bs((tr, CONV_W), lambda i: (i, CONV_COL0)),
                  _bs((tr, CONV_W), lambda i: (i, CONV_COL0 + 1)),
                  _bs((tr, CONV_W), lambda i: (i, CONV_COL0 + 2)),
                  _bs((8, CONV_W), prev(CONV_COL0 + 1)),
                  _bs((8, CONV_W), prev(CONV_COL0 + 2)),
                  _bs((3, CONV_W), lambda i: (0, 0))],
        out_specs=_bs((tr, CONV_W), lambda i: (i, 0)),
        out_shape=_sds((seq, CONV_W), BF16),
        compiler_params=_cparams(1),
    )(proj, proj, proj, proj, proj, w_conv)


def _conv_bwd(proj, de, w_conv, seq, tr):
    hb = tr // 8
    nblk = seq // tr

    def body(cb_ref, cc_ref, cu_ref, ccp_ref, cup_ref, cbn_ref, de_ref, den_ref, w_ref, o_ref, dw_ref):
        i = pl.program_id(0)
        cc, cu, cb = cc_ref[...], cu_ref[...], cb_ref[...]
        u = cc * cu
        up = jnp.where(i > 0, ccp_ref[...] * cup_ref[...], 0.0)
        u1 = _shift_rows_down(u, up, 1)
        u2 = _shift_rows_down(u, up, 2)
        de_ = de_ref[...]
        dd = de_ * cb
        ddn = jnp.where(i < nblk - 1, den_ref[...] * cbn_ref[...], 0.0)
        w = w_ref[...]
        du = w[2:3, :] * dd + w[1:2, :] * _shift_rows_up(dd, ddn, 1) + w[0:1, :] * _shift_rows_up(dd, ddn, 2)
        o_ref[:, 0:CONV_W] = (de_ * (w[0:1, :] * u2 + w[1:2, :] * u1 + w[2:3, :] * u)).astype(BF16)
        o_ref[:, CONV_W:2 * CONV_W] = (du * cu).astype(BF16)
        o_ref[:, 2 * CONV_W:3 * CONV_W] = (du * cc).astype(BF16)
        ridx = lax.broadcasted_iota(jnp.int32, (8, CONV_W), 0)
        dw0 = jnp.sum(dd * u2, axis=0, keepdims=True)
        dw1 = jnp.sum(dd * u1, axis=0, keepdims=True)
        dw2 = jnp.sum(dd * u, axis=0, keepdims=True)
        dw_ref[...] = jnp.where(ridx == 0, dw0, jnp.where(ridx == 1, dw1, jnp.where(ridx == 2, dw2, 0.0)))

    prev = lambda c: (lambda i: (jnp.maximum(i * hb - 1, 0), c))
    nxt = lambda c: (lambda i: (jnp.minimum((i + 1) * hb, seq // 8 - 1), c))
    return pl.pallas_call(
        body, name="conv_bwd", grid=(nblk,),
        in_specs=[_bs((tr, CONV_W), lambda i: (i, CONV_COL0)),
                  _bs((tr, CONV_W), lambda i: (i, CONV_COL0 + 1)),
                  _bs((tr, CONV_W), lambda i: (i, CONV_COL0 + 2)),
                  _bs((8, CONV_W), prev(CONV_COL0 + 1)),
                  _bs((8, CONV_W), prev(CONV_COL0 + 2)),
                  _bs((8, CONV_W), nxt(CONV_COL0)),
                  _bs((tr, CONV_W), lambda i: (i, 0)),
                  _bs((8, CONV_W), nxt(0)),
                  _bs((3, CONV_W), lambda i: (0, 0))],
        out_specs=[_bs((tr, 3 * CONV_W), lambda i: (i, 0)), _bs((None, 8, CONV_W), lambda i: (i, 0, 0))],
        out_shape=[_sds((seq, 3 * CONV_W), BF16), _sds((nblk, 8, CONV_W), F32)],
        compiler_params=_cparams(1),
    )(proj, proj, proj, proj, proj, proj, de, de, w_conv)


def _log_gates(z):
    lse = jnp.log(1.0 + jnp.exp(-jnp.abs(z)))
    log_beta = jnp.minimum(z, 0.0) - lse
    return log_beta, log_beta - z


DEAD_LOG_WEIGHT = -110.0
NO_TILE = -1e30


def _first_live_tile(start, scores, live_sc):
    def alive():
        return jnp.max(jnp.maximum(live_sc[0], live_sc[1])) > DEAD_LOG_WEIGHT

    def step(c):
        for h, z in enumerate(scores(c[0])):
            live_sc[h] = live_sc[h] + jnp.sum(_log_gates(z)[1], axis=-1, keepdims=True)
        return c[0] - 1, alive()

    j_end, _ = lax.while_loop(lambda c: jnp.logical_and(c[0] >= 0, c[1]), step, (start, alive()))
    return j_end + 1


def _attn_fwd(proj, seq):
    blk = ATT_BLK
    nq = seq // blk
    npair = N_HEADS // 2

    def body(q_ref, k_ref, v_ref, o_ref, z0_sc, z1_sc, w0_sc, w1_sc, tot_sc, acc_sc):
        i = pl.program_id(1)
        is_a = lax.broadcasted_iota(jnp.int32, (1, LANES), 1) < HEAD_DIM
        q2 = (q_ref[...] * Q_SCALE).astype(BF16)
        zero = jnp.zeros_like(q2)
        qs = (jnp.where(is_a, q2, zero), jnp.where(is_a, zero, q2))
        row = lax.broadcasted_iota(jnp.int32, (blk, blk), 0)
        col = lax.broadcasted_iota(jnp.int32, (blk, blk), 1)
        tri = (row > col).astype(BF16)
        causal = col < row

        def tile_of(ref, j):
            return ref[pl.ds(pl.multiple_of(j * blk, blk), blk), :].astype(BF16)

        def scores(j):
            k2 = tile_of(k_ref, j)
            return [_nt(qs[h], k2) for h in range(2)]

        has_left = i > 0
        left = jnp.maximum(i - 1, 0)

        g_d = [_log_gates(z) for z in scores(i)]
        g_l = [_log_gates(z) for z in scores(left)]
        keep_d = [jnp.where(causal, g[1], 0.0) for g in g_d]
        suf_d = [_nn(lk.astype(BF16), tri) for lk in keep_d]
        suf_l = [_nn(g[1].astype(BF16), tri) for g in g_l]
        v_d, v_l = tile_of(v_ref, i), tile_of(v_ref, left)
        pv = []
        for h in range(2):
            sum_d = jnp.sum(keep_d[h], axis=-1, keepdims=True)
            w_d = jnp.where(causal, jnp.exp(g_d[h][0] + suf_d[h]), 0.0)
            w_l = jnp.exp(g_l[h][0] + (jnp.where(has_left, sum_d, NO_TILE) + suf_l[h]))
            pv.append(_nn(w_d.astype(BF16), v_d) + _nn(w_l.astype(BF16), v_l))
            tot_sc[h] = sum_d + jnp.sum(g_l[h][1], axis=-1, keepdims=True)
        acc_sc[...] = jnp.where(is_a, pv[0], pv[1])

        z_bufs, w_bufs = (z0_sc, z1_sc), (w0_sc, w1_sc)

        def alive():
            return jnp.max(jnp.maximum(tot_sc[0], tot_sc[1])) > DEAD_LOG_WEIGHT

        def put(ref, vals):
            for h in range(2):
                ref[h] = vals[h]

        def weights(zs):
            gates = [_log_gates(z) for z in zs]
            sums = [_nn(g[1].astype(BF16), tri) for g in gates]
            ws = []
            for h in range(2):
                ws.append(jnp.exp(gates[h][0] + (tot_sc[h] + sums[h])).astype(BF16))
                tot_sc[h] = tot_sc[h] + jnp.sum(gates[h][1], axis=-1, keepdims=True)
            return ws

        def add_values(w_buf, j):
            v2 = tile_of(v_ref, j)
            acc_sc[...] += jnp.where(is_a, _nn(w_buf[0], v2), _nn(w_buf[1], v2))

        def trip(j, s):
            add_values(w_bufs[s], j + 1)
            put(z_bufs[1 - s], scores(jnp.maximum(j - 1, 0)))
            put(w_bufs[1 - s], weights((z_bufs[s][0], z_bufs[s][1])))

        @pl.when(jnp.logical_and(i >= 2, alive()))
        def _():
            put(z0_sc, scores(i - 2))
            w0_sc[...] = jnp.zeros_like(w0_sc)

            def two_trips(c):
                trip(c[0], 0)
                trip(c[0] - 1, 1)
                return c[0] - 2, alive()

            j_next, still = lax.while_loop(lambda c: jnp.logical_and(c[0] >= 1, c[1]), two_trips, (i - 2, i >= 2))
            one_left = jnp.logical_and(j_next == 0, still)

            @pl.when(one_left)
            def _():
                trip(0, 0)
                add_values(w1_sc, 0)

            @pl.when(jnp.logical_not(one_left))
            def _():
                add_values(w0_sc, j_next + 1)

        o_ref[...] = acc_sc[...].astype(BF16)

    return pl.pallas_call(
        body, name="attn_fwd", grid=(npair, nq),
        in_specs=[_bs((blk, LANES), lambda p, i: (i, p)),
                  _bs((seq, LANES), lambda p, i: (0, npair + p)),
                  _bs((seq, LANES), lambda p, i: (0, 2 * npair + p))],
        out_specs=_bs((blk, LANES), lambda p, i: (i, p)),
        out_shape=_sds((seq, ATTN_W), BF16),
        scratch_shapes=[pltpu.VMEM((2, blk, blk), F32), pltpu.VMEM((2, blk, blk), F32),
                        pltpu.VMEM((2, blk, blk), BF16), pltpu.VMEM((2, blk, blk), BF16),
                        pltpu.VMEM((2, blk, 1), F32), pltpu.VMEM((blk, LANES), F32)],
        compiler_params=_cparams(2),
    )(proj, proj, proj)


def _attn_bwd(proj, do, seq):
    blk = ATT_BLK
    nq = seq // blk
    npair = N_HEADS // 2

    def body(q_ref, k_ref, v_ref, do_ref, dq_ref, dk_ref, dv_ref,
             prod0_sc, prod1_sc, pend0_sc, pend1_sc, tot_sc, live_sc, cum_sc, pre_sc, dq_sc):
        i = pl.program_id(1)

        @pl.when(i == 0)
        def _():
            dk_ref[...] = jnp.zeros_like(dk_ref)
            dv_ref[...] = jnp.zeros_like(dv_ref)

        is_a = lax.broadcasted_iota(jnp.int32, (1, LANES), 1) < HEAD_DIM
        q2 = (q_ref[...] * Q_SCALE).astype(BF16)
        do2 = do_ref[...]
        zero = jnp.zeros_like(q2)
        qs = (jnp.where(is_a, q2, zero), jnp.where(is_a, zero, q2))
        dos = (jnp.where(is_a, do2, zero), jnp.where(is_a, zero, do2))
        row = lax.broadcasted_iota(jnp.int32, (blk, blk), 0)
        col = lax.broadcasted_iota(jnp.int32, (blk, blk), 1)
        tri_after = (row > col).astype(BF16)
        tri_excl = (row < col).astype(BF16)
        causal = col < row

        def tile_of(ref, j):
            return ref[pl.ds(pl.multiple_of(j * blk, blk), blk), :].astype(BF16)

        def scores(j):
            k2 = tile_of(k_ref, j)
            return [_nt(qs[h], k2) for h in range(2)]

        def products(j):
            v2 = tile_of(v_ref, j)
            return scores(j) + [_nt(dos[h], v2) for h in range(2)]

        def row_sum(a):
            return jnp.sum(a, axis=-1, keepdims=True)

        def grad_matmuls(ws, dzs, j):
            rows = pl.ds(pl.multiple_of(j * blk, blk), blk)
            k2 = tile_of(k_ref, j)
            dq_sc[...] += jnp.where(is_a, _nn(dzs[0], k2), _nn(dzs[1], k2))
            dk_ref[rows, :] += jnp.where(is_a, _tn(dzs[0], q2), _tn(dzs[1], q2))
            if ws is not None:
                dv_ref[rows, :] += jnp.where(is_a, _tn(ws[0], do2), _tn(ws[1], do2))

        has_left = i > 0
        left = jnp.maximum(i - 1, 0)

        p_d, p_l = products(i), products(left)
        g_d = [_log_gates(z) for z in p_d[:2]]
        g_l = [_log_gates(z) for z in p_l[:2]]
        keep_d = [jnp.where(causal, g[1], 0.0) for g in g_d]
        suf_d = [_nn(lk.astype(BF16), tri_after) for lk in keep_d]
        suf_l = [_nn(g[1].astype(BF16), tri_after) for g in g_l]
        w_d, w_l, gg_d, gg_l = [], [], [], []
        for h in range(2):
            sum_d = row_sum(keep_d[h])
            w_d.append(jnp.where(causal, jnp.exp(g_d[h][0] + suf_d[h]), 0.0))
            w_l.append(jnp.exp(g_l[h][0] + (jnp.where(has_left, sum_d, NO_TILE) + suf_l[h])))
            gg_d.append(p_d[2 + h] * w_d[h])
            gg_l.append(p_l[2 + h] * w_l[h])
            tot_sc[h] = sum_d + row_sum(g_l[h][1])
        before_d = [_nn(g.astype(BF16), tri_excl) for g in gg_d]
        before_l = [_nn(g.astype(BF16), tri_excl) for g in gg_l]
        dz_d, dz_l = [], []
        for h in range(2):
            beta_d, beta_l = jnp.exp(g_d[h][0]), jnp.exp(g_l[h][0])
            dz_l.append((gg_l[h] * (1.0 - beta_l) - before_l[h] * beta_l).astype(BF16))
            dz = gg_d[h] * (1.0 - beta_d) - (row_sum(gg_l[h]) + before_d[h]) * beta_d
            dz_d.append(jnp.where(causal, dz, 0.0).astype(BF16))
        dq_sc[...] = jnp.zeros_like(dq_sc)
        grad_matmuls([w.astype(BF16) for w in w_l], dz_l, left)
        grad_matmuls([w.astype(BF16) for w in w_d], dz_d, i)

        live_sc[...] = tot_sc[...]
        first = _first_live_tile(i - 2, scores, live_sc)
        trips = i - 1 - first
        prod_bufs, pend_bufs = (prod0_sc, prod1_sc), (pend0_sc, pend1_sc)

        def local_grads(prods):
            zs, dws = prods[:2], prods[2:]
            gates = [_log_gates(z) for z in zs]
            sums = [_nn(g[1].astype(BF16), tri_after) for g in gates]
            ws, gs = [], []
            for h in range(2):
                cum = cum_sc[h] + row_sum(gates[h][1])
                cum_sc[h] = cum
                ws.append(jnp.exp(gates[h][0] + ((live_sc[h] - cum) + sums[h])))
                gs.append(dws[h] * ws[h])
            befores = [_nn(g.astype(BF16), tri_excl) for g in gs]
            dzs = []
            for h in range(2):
                beta = jnp.exp(gates[h][0])
                dzs.append((gs[h] * (1.0 - beta) - (pre_sc[h] + befores[h]) * beta).astype(BF16))
                pre_sc[h] = pre_sc[h] + row_sum(gs[h])
            return [w.astype(BF16) for w in ws] + dzs

        def put(ref, vals):
            for n, val in enumerate(vals):
                ref[n] = val

        def flush(pend, j):
            grad_matmuls([pend[0], pend[1]], [pend[2], pend[3]], j)

        def trip(j, s):
            flush(pend_bufs[s], jnp.maximum(j - 1, first))
            put(prod_bufs[1 - s], products(j + 1))
            put(pend_bufs[1 - s], local_grads([prod_bufs[s][n] for n in range(4)]))

        def earlier_keys_share(j, mask):
            dzs = []
            for h, z in enumerate(scores(j)):
                beta = jnp.exp(_log_gates(z)[0])
                dzs.append(jnp.where(mask, -pre_sc[h] * beta, 0.0).astype(BF16))
            grad_matmuls(None, dzs, j)

        @pl.when(trips > 0)
        def _():
            cum_sc[...] = jnp.zeros_like(cum_sc)
            pre_sc[...] = jnp.zeros_like(pre_sc)
            pend0_sc[...] = jnp.zeros_like(pend0_sc)
            put(prod0_sc, products(first))

            def two_trips(pp, carry):
                trip(first + 2 * pp, 0)
                trip(first + 2 * pp + 1, 1)
                return carry

            lax.fori_loop(0, trips // 2, two_trips, 0)
            odd = trips % 2 == 1

            @pl.when(odd)
            def _():
                trip(i - 2, 0)
                flush(pend1_sc, i - 2)

            @pl.when(jnp.logical_not(odd))
            def _():
                flush(pend0_sc, i - 2)

            earlier_keys_share(i - 1, True)
            earlier_keys_share(i, causal)

        dq_ref[...] = (dq_sc[...] * Q_SCALE).astype(BF16)

    qmap = lambda p, i: (i, p)
    return pl.pallas_call(
        body, name="attn_bwd", grid=(npair, nq),
        in_specs=[_bs((blk, LANES), qmap),
                  _bs((seq, LANES), lambda p, i: (0, npair + p)),
                  _bs((seq, LANES), lambda p, i: (0, 2 * npair + p)),
                  _bs((blk, LANES), qmap)],
        out_specs=[_bs((blk, LANES), qmap),
                   _bs((seq, LANES), lambda p, i: (0, p)),
                   _bs((seq, LANES), lambda p, i: (0, p))],
        out_shape=[_sds((seq, ATTN_W), BF16)] + [_sds((seq, ATTN_W), F32)] * 2,
        scratch_shapes=[pltpu.VMEM((4, blk, blk), F32), pltpu.VMEM((4, blk, blk), F32),
                        pltpu.VMEM((4, blk, blk), BF16), pltpu.VMEM((4, blk, blk), BF16),
                        pltpu.VMEM((2, blk, 1), F32), pltpu.VMEM((2, blk, 1), F32), pltpu.VMEM((2, blk, 1), F32),
                        pltpu.VMEM((2, blk, 1), F32), pltpu.VMEM((blk, LANES), F32)],
        compiler_params=_cparams(2),
    )(proj, proj, proj, do)


def _elementwise(name, fn, ins, out_dtypes):
    rows, cols = ins[0].shape
    tr = rows
    for cand in (512, 256, 128, 64, 32, 16, 8):
        if rows % cand == 0 and cand * cols * 4 <= 2 * 1024 * 1024:
            tr = cand
            break
    n_in = len(ins)

    def body(*refs):
        res = fn(*[r[...] for r in refs[:n_in]])
        for r, val in zip(refs[n_in:], res):
            r[...] = val.astype(r.dtype)

    spec = _bs((tr, cols), lambda i: (i, 0))
    return pl.pallas_call(
        body, name=name, grid=(rows // tr,),
        in_specs=[spec] * n_in, out_specs=[spec] * len(out_dtypes),
        out_shape=[_sds((rows, cols), dt) for dt in out_dtypes],
        compiler_params=_cparams(1),
    )(*ins)


def _adamw_fn(w, g, m, v):
    m = ADAM_B1 * m + (1.0 - ADAM_B1) * g
    v = ADAM_B2 * v + (1.0 - ADAM_B2) * (g * g)
    m_hat = m / (1.0 - ADAM_B1 ** ADAM_STEP)
    v_hat = v / (1.0 - ADAM_B2 ** ADAM_STEP)
    delta = -ADAM_LR * (m_hat / (jnp.sqrt(v_hat) + ADAM_EPS) + ADAM_WD * w)
    return delta, m, v


def _adamw(name, w, g, m, v):
    shape = w.shape
    as2d = lambda a: a.reshape(-1, shape[-1])
    delta, nm, nv = _elementwise(name, _adamw_fn, [as2d(w), as2d(g), as2d(m), as2d(v)], [F32, F32, F32])
    return delta.reshape(shape), nm.reshape(shape), nv.reshape(shape)


def _place():
    return lax.axis_index("x"), lax.axis_index("y"), lax.axis_index("c")


ANY = pl.BlockSpec(memory_space=pl.ANY)
VMEM_WHOLE = pl.BlockSpec(memory_space=pltpu.VMEM)


def _allgather_weights(shards):
    n = len(shards)

    def body(*refs):
        src, dst = refs[:n], refs[n:2 * n]
        send_sems, recv_sems, local_sems = refs[2 * n:]
        x, y, c = _place()
        me, sibling, mychip = (x, y, c), (x, y, 1 - c), 2 * x + y

        x_nbr, y_nbr, diag = 2 * (1 - x) + y, 2 * x + (1 - y), 2 * (1 - x) + (1 - y)
        to_x, to_y = (1 - x, y, c), (x, 1 - y, c)

        def parts(w):
            hr = src[w].shape[0] // 2
            first = hr // 2 if hr % 32 == 0 else hr
            return first, hr - first

        def rows_of(w, chip, half, route):
            hr = src[w].shape[0] // 2
            first, second = parts(w)
            start, size = {0: (0, hr), 1: (0, hr), 2: (0, first), 3: (first, second)}[route]
            return dst[w].at[chip, pl.ds(half * hr + start, size)]

        def copy(w, k, src_ref, dst_ref, to):
            return pltpu.make_async_remote_copy(src_ref=src_ref, dst_ref=dst_ref, send_sem=send_sems.at[w, k],
                                                recv_sem=recv_sems.at[w, k], device_id=to, device_id_type=MESH)

        def landed(w, route):
            chip = {0: x_nbr, 1: y_nbr, 2: diag, 3: diag}[route]
            return rows_of(w, chip, c, route), chip

        def routes(w):
            return (0, 1, 2, 3) if parts(w)[1] else (0, 1, 2)

        started, local = [], []
        for w in range(n):
            hr = src[w].shape[0] // 2
            own = pltpu.make_async_copy(src[w], dst[w].at[mychip], local_sems.at[w])
            own.start()
            local.append(own)
            mine = src[w].at[pl.ds(c * hr, hr)]
            for route, to in ((0, to_x), (1, to_y)):
                cp = copy(w, route, mine, rows_of(w, mychip, c, route), to)
                cp.start()
                started.append(cp)

        def pass_on(w, route):
            got, chip = landed(w, route)
            copy(w, route, got, got, me).wait_recv()
            if route == 1:
                part = rows_of(w, chip, c, 2)
                started.append(copy(w, 2, part, part, to_x))
                started[-1].start()
            if route == 0 and parts(w)[1]:
                part = rows_of(w, chip, c, 3)
                started.append(copy(w, 3, part, part, to_y))
                started[-1].start()
            started.append(copy(w, 4 + route, got, got, sibling))
            started[-1].start()

        for w in range(n):
            pass_on(w, 1)
            pass_on(w, 0)
        for w in range(n):
            for route in routes(w)[2:]:
                pass_on(w, route)
        for w in range(n):
            for route in routes(w):
                chip = landed(w, route)[1]
                from_sib = rows_of(w, chip, 1 - c, route)
                copy(w, 4 + route, from_sib, from_sib, me).wait_recv()
        for cp in local:
            cp.wait()
        for cp in started:
            cp.wait_send()

    return pl.pallas_call(
        body, name="allgather_weights",
        in_specs=[VMEM_WHOLE] * n, out_specs=[VMEM_WHOLE] * n,
        out_shape=[_sds((N_CHIPS,) + s.shape, s.dtype) for s in shards],
        scratch_shapes=[pltpu.SemaphoreType.DMA((n, 8)), pltpu.SemaphoreType.DMA((n, 8)),
                        pltpu.SemaphoreType.DMA((n,))],
        compiler_params=pltpu.CompilerParams(vmem_limit_bytes=VMEM_LIMIT),
    )(*shards)


SUM_ROWS = 64


def _rs_pair_sum(name, grads):
    n = len(grads)

    def body(*refs):
        g, out = refs[:n], refs[n:2 * n]
        stage, give16, land, keep = (refs[m * n:(m + 1) * n] for m in range(2, 6))
        send_sems, recv_sems, stage_sems, keep_sems = refs[6 * n:]
        x, y, c = _place()
        sibling = (x, y, 1 - c)

        def over_rows(w, fn):
            nb = g[w].shape[1] // 2 // SUM_ROWS

            def step(idx, carry):
                fn(idx // nb, pl.ds(pl.multiple_of((idx % nb) * SUM_ROWS, SUM_ROWS), SUM_ROWS))
                return carry

            lax.fori_loop(0, N_CHIPS * nb, step, 0)

        loads = []
        for w in range(n):
            hr = g[w].shape[1] // 2
            st = pltpu.make_async_copy(g[w].at[:, pl.ds((1 - c) * hr, hr)], stage[w], stage_sems.at[w])
            kp = pltpu.make_async_copy(g[w].at[:, pl.ds(c * hr, hr)], keep[w], keep_sems.at[w])
            st.start()
            kp.start()
            loads.append((st, kp))
        gives = []
        for w in range(n):
            loads[w][0].wait()

            def narrow(k, rows, w=w):
                give16[w][k, rows, :] = stage[w][k, rows, :].astype(BF16)

            over_rows(w, narrow)
            give = pltpu.make_async_remote_copy(src_ref=give16[w], dst_ref=land[w], send_sem=send_sems.at[w],
                                                recv_sem=recv_sems.at[w], device_id=sibling, device_id_type=MESH)
            give.start()
            gives.append(give)
        for w in range(n):
            loads[w][1].wait()
            gives[w].wait_recv()

            def add(k, rows, w=w):
                out[w][k, rows, :] = (keep[w][k, rows, :] + land[w][k, rows, :].astype(F32)).astype(BF16)

            over_rows(w, add)
        for give in gives:
            give.wait_send()

    half = [(N_CHIPS, a.shape[1] // 2, a.shape[2]) for a in grads]
    wide = [pltpu.VMEM(s, F32) for s in half]
    narrow_bufs = [pltpu.VMEM(s, BF16) for s in half]
    sems = pltpu.SemaphoreType.DMA((n,))
    return pl.pallas_call(
        body, name=name,
        in_specs=[ANY] * n, out_specs=[VMEM_WHOLE] * n, out_shape=[_sds(s, BF16) for s in half],
        scratch_shapes=wide + narrow_bufs + narrow_bufs + wide + [sems, sems, sems, sems],
        compiler_params=pltpu.CompilerParams(vmem_limit_bytes=VMEM_LIMIT),
    )(*grads)


def _rs_exchange_join(parts):
    n = len(parts)

    def body(*refs):
        t, full = refs[:n], refs[n:2 * n]
        got_x, got_y, pass_on, got_2 = (refs[m * n:(m + 1) * n] for m in range(2, 6))
        send_sems, recv_sems = refs[6 * n:]
        x, y, c = _place()
        mychip, sibling = 2 * x + y, (x, y, 1 - c)
        x_nbr, y_nbr, diag = 2 * (1 - x) + y, 2 * x + (1 - y), 2 * (1 - x) + (1 - y)
        to_x, to_y = (1 - x, y, c), (x, 1 - y, c)
        sends = []

        def copy(w, k, src_ref, dst_ref, to):
            return pltpu.make_async_remote_copy(src_ref=src_ref, dst_ref=dst_ref, send_sem=send_sems.at[w, k],
                                                recv_sem=recv_sems.at[w, k], device_id=to, device_id_type=MESH)

        def start(cp):
            cp.start()
            sends.append(cp)

        def add_rows(w, count, fn):
            def step(idx, carry):
                fn(pl.ds(pl.multiple_of(idx * SUM_ROWS, SUM_ROWS), SUM_ROWS), pl.multiple_of(idx * SUM_ROWS, SUM_ROWS))
                return carry
            lax.fori_loop(0, count // SUM_ROWS, step, 0)

        f32 = lambda v: v.astype(F32)
        for w in range(n):
            ha = t[w].shape[1] // 2
            part_a, part_b = pl.ds(0, ha), pl.ds(ha, ha)
            start(copy(w, 0, t[w].at[x_nbr, part_a], got_x[w].at[0], to_x))
            start(copy(w, 1, t[w].at[diag, part_a], got_x[w].at[1], to_x))
            start(copy(w, 2, t[w].at[y_nbr, part_b], got_y[w].at[0], to_y))
            start(copy(w, 3, t[w].at[diag, part_b], got_y[w].at[1], to_y))
        for w in range(n):
            hr = t[w].shape[1]
            ha = hr // 2
            for k in (0, 1):
                copy(w, k, got_x[w].at[k], got_x[w].at[k], to_x).wait_recv()

            def sum_a(rows, r, w=w, hr=hr):
                full[w][pl.ds(pl.multiple_of(c * hr + r, SUM_ROWS), SUM_ROWS), :] = \
                    f32(t[w][mychip, rows, :]) + f32(got_x[w][0, rows, :])
                pass_on[w][rows, :] = (f32(t[w][y_nbr, rows, :]) + f32(got_x[w][1, rows, :])).astype(BF16)

            add_rows(w, ha, sum_a)
            start(copy(w, 4, pass_on[w].at[pl.ds(0, ha)], got_2[w].at[pl.ds(0, ha)], to_y))
            for k in (2, 3):
                copy(w, k, got_y[w].at[k - 2], got_y[w].at[k - 2], to_y).wait_recv()

            def sum_b(rows, r, w=w, hr=hr, ha=ha):
                lower = pl.ds(pl.multiple_of(ha + r, SUM_ROWS), SUM_ROWS)
                full[w][pl.ds(pl.multiple_of(c * hr + ha + r, SUM_ROWS), SUM_ROWS), :] = \
                    f32(t[w][mychip, lower, :]) + f32(got_y[w][0, rows, :])
                pass_on[w][lower, :] = (f32(t[w][x_nbr, lower, :]) + f32(got_y[w][1, rows, :])).astype(BF16)

            add_rows(w, ha, sum_b)
            start(copy(w, 5, pass_on[w].at[pl.ds(ha, ha)], got_2[w].at[pl.ds(ha, ha)], to_x))
        for w in range(n):
            hr = t[w].shape[1]
            ha = hr // 2
            copy(w, 4, got_2[w].at[pl.ds(0, ha)], got_2[w].at[pl.ds(0, ha)], to_y).wait_recv()
            copy(w, 5, got_2[w].at[pl.ds(ha, ha)], got_2[w].at[pl.ds(ha, ha)], to_x).wait_recv()

            def finish(rows, r, w=w, hr=hr):
                out_rows = pl.ds(pl.multiple_of(c * hr + r, SUM_ROWS), SUM_ROWS)
                full[w][out_rows, :] = full[w][out_rows, :] + f32(got_2[w][rows, :])

            add_rows(w, hr, finish)
            mine = full[w].at[pl.ds(c * hr, hr)]
            start(copy(w, 6, mine, mine, sibling))
        for w in range(n):
            hr = t[w].shape[1]
            theirs = full[w].at[pl.ds((1 - c) * hr, hr)]
            copy(w, 6, theirs, theirs, sibling).wait_recv()
        for cp in sends:
            cp.wait_send()

    half = lambda a: pltpu.VMEM((2, a.shape[1] // 2, a.shape[2]), a.dtype)
    whole = lambda a: pltpu.VMEM(a.shape[1:], a.dtype)
    return pl.pallas_call(
        body, name="rs_exchange_join",
        in_specs=[VMEM_WHOLE] * n, out_specs=[VMEM_WHOLE] * n,
        out_shape=[_sds((2 * a.shape[1], a.shape[2]), F32) for a in parts],
        scratch_shapes=[half(a) for a in parts] + [half(a) for a in parts] + [whole(a) for a in parts]
        + [whole(a) for a in parts] + [pltpu.SemaphoreType.DMA((n, 7)), pltpu.SemaphoreType.DMA((n, 7))],
        compiler_params=pltpu.CompilerParams(vmem_limit_bytes=VMEM_LIMIT),
    )(*parts)


def _small_allreduce(loss_p, dg_parts, dbg_a, dbg_c, dwc):
    ins = [loss_p] + list(dg_parts) + [dbg_a, dbg_c, dwc]
    n_in = len(ins)
    vmem = pl.BlockSpec(memory_space=pltpu.VMEM)

    def body(*refs):
        in_refs = refs[:n_in]
        out_ref, vec, buf, send_sems, recv_sems = refs[n_in:]
        x, y, c = _place()
        me = 4 * x + 2 * y + c
        vec[...] = jnp.zeros_like(vec)
        vec[0:1, :] = jnp.sum(in_refs[0][...], axis=0)
        for r in range(5):
            vec[1 + r:2 + r, :] = jnp.sum(in_refs[1 + r][...], axis=0)
        vec[6:7, :] = jnp.sum(in_refs[6][...], axis=0)
        vec[7:8, :] = jnp.sum(in_refs[7][...], axis=0)
        vec[8:16, 0:CONV_W] = jnp.sum(in_refs[8][...], axis=0)
        buf[pl.ds(me, 1)] = vec[...][None]
        copies = []
        for r in range(1, 8):
            fx, fy, fc = (r >> 2) & 1, (r >> 1) & 1, r & 1
            to = (1 - x if fx else x, 1 - y if fy else y, 1 - c if fc else c)
            cp = pltpu.make_async_remote_copy(src_ref=vec, dst_ref=buf.at[me], send_sem=send_sems.at[r - 1],
                                              recv_sem=recv_sems.at[r - 1], device_id=to, device_id_type=MESH)
            cp.start()
            copies.append(cp)
        for cp in copies:
            cp.wait()
        total = buf[0]
        for s in range(1, 8):
            total = total + buf[s]
        out_ref[...] = total
        out_ref[0:1, :] = jnp.broadcast_to(jnp.sum(total[0:1, :], axis=-1, keepdims=True), (1, D_MODEL))

    return pl.pallas_call(
        body, name="small_allreduce",
        in_specs=[vmem] * n_in, out_specs=vmem, out_shape=_sds((SMALL_ROWS, D_MODEL), F32),
        scratch_shapes=[pltpu.VMEM((SMALL_ROWS, D_MODEL), F32), pltpu.VMEM((8, SMALL_ROWS, D_MODEL), F32),
                        pltpu.SemaphoreType.DMA((7,)), pltpu.SemaphoreType.DMA((7,))],
    )(*ins)


def _local_step(x, p, tgt, g, b_gate, w_conv, wf):
    seq = x.shape[0]
    tm = min(seq, 1024)
    th = min(seq, 512)
    tl = min(seq, 2048)
    ni, nh, nl = seq // tm, seq // th, seq // tl
    g_pre_mix, g_post_mix, g_pre_mlp, g_post_mlp, g_ple = g
    w_in_nat, w_ao, w_co, w_o, w_up_nat, w_down, w_pg, w_pp = wf
    D = D_MODEL
    vec = lambda a, blk=0: (a, _bs((1, D), lambda i, j, k: (0, blk)))
    rows_i = lambda a, t, blk=0: (a, _bs((t, D), lambda i, j, k: (i, blk)))
    rows_k = lambda a, t, blk=0: (a, _bs((t, D), lambda i, j, k: (k, blk)))
    part = lambda n: (_sds((n, 1, D), F32), _bs((None, 1, D), lambda i, j, k: (i, 0, 0)))
    full2 = lambda a: (a, _bs(a.shape, lambda i, j, k: (0, 0)))

    normed = lambda xb, gb: (_rms(xb, gb).astype(BF16),) * 2
    keep_a = lambda t: [(_sds((seq, D), BF16), _bs((t, D), lambda i, j, k: (i, 0)))]
    qkv_w, conv_w = 3 * ATTN_W, 3 * CONV_W
    qkv, proj_conv, gates, h1 = _mm(
        "proj_in", "nn", (nh, 1, 1),
        a_ins=[rows_i(x, th), vec(g_pre_mix)], a_fn=normed, b_ins=[full2(w_in_nat)], b_fn=_ident,
        epi_fn=lambda acc: (acc[:, :qkv_w], acc[:, qkv_w:qkv_w + conv_w], acc[:, qkv_w + conv_w:]),
        outs=[(_sds((seq, qkv_w), BF16), _bs((th, qkv_w), lambda i, j, k: (i, 0))),
              (_sds((seq, conv_w), F32), _bs((th, conv_w), lambda i, j, k: (i, 0))),
              (_sds((seq, 2 * D), BF16), _bs((th, 2 * D), lambda i, j, k: (i, 0)))],
        acc_shape=(th, D_IN), a_cache=((th, D), BF16), a_outs=keep_a(th))
    o = _attn_fwd(qkv, seq)
    e = _conv_fwd(proj_conv, w_conv, seq, tm)

    def gate_values(ga, gc, ba, bc):
        return _sig(ga.astype(F32) + ba), _sig(gc.astype(F32) + bc)

    def branch_outputs(ob, eb, wao, wco):
        return _nn(ob, wao).astype(BF16).astype(F32), _nn(eb, wco).astype(BF16).astype(F32)

    def mix_fn(ga, gc, ob, eb, ba, bc, wao, wco):
        sa, sc = gate_values(ga, gc, ba, bc)
        ya, yc = branch_outputs(ob, eb, wao, wco)
        return ((sa * ya + sc * yc).astype(BF16),) * 2

    def post_mix(acc, xb, gb):
        return acc, xb + _rms(acc, gb)

    half_rows = lambda a: (a, _bs((th, a.shape[1]), lambda i, j, k: (i, 0)))
    mix_ins = [rows_i(gates, th, 0), rows_i(gates, th, 1), half_rows(o), half_rows(e), vec(b_gate, 0), vec(b_gate, 1),
               full2(w_ao), full2(w_co)]
    mixed, x1, mixin = _mm(
        "mix_out", "nn", (nh, 1, 1),
        a_ins=mix_ins, a_fn=mix_fn, b_ins=[full2(w_o)], b_fn=_ident,
        epi_ins=[rows_i(x, th), vec(g_post_mix)], epi_fn=post_mix,
        outs=[(_sds((seq, D), BF16), _bs((th, D), lambda i, j, k: (i, 0))),
              (_sds((seq, D), F32), _bs((th, D), lambda i, j, k: (i, 0)))],
        acc_shape=(th, D), a_cache=((th, D), BF16), a_outs=keep_a(th))
    up, h2 = _mm("mlp_up", "nn", (nh, 1, 1),
                 a_ins=[rows_i(x1, th), vec(g_pre_mlp)], a_fn=normed,
                 b_ins=[full2(w_up_nat)], b_fn=_ident,
                 outs=[(_sds((seq, D_FF), BF16), _bs((th, D_FF), lambda i, j, k: (i, 0)))],
                 acc_shape=(th, D_FF), a_cache=((th, D), BF16), a_outs=keep_a(th))

    def relu2(ub):
        r = jnp.maximum(ub.astype(F32), 0.0)
        return (r * r).astype(BF16)

    dx2, df, dpre, h3, dpp, loss_p, dg_ple_p, dg_post_mlp_p = _mlp_down_ple_head(
        up, x1, p, tgt, g_ple, g_post_mlp, w_down, w_pg, w_pp, seq, th)

    (dw_pp,) = _mm("dw_ple_proj", "tn", (1, 1, nh),
                   a_ins=[(p, _bs((th, PLE_DIM), lambda i, j, k: (k, 0)))], a_fn=_to_bf16,
                   b_ins=[rows_k(dpp, th)], b_fn=_ident,
                   outs=[(_sds((PLE_DIM, D), F32), _bs((PLE_DIM, D), lambda i, j, k: (0, 0)))],
                   acc_shape=(PLE_DIM, D))
    (dw_pg,) = _mm("dw_ple_gate", "tn", (1, 1, nl),
                   a_ins=[rows_k(h3, tl)], a_fn=_ident, b_ins=[rows_k(dpre, tl)], b_fn=_ident,
                   outs=[(_sds((D, D), F32), _bs((D, D), lambda i, j, k: (0, 0)))], acc_shape=(D, D))

    def dup_fn(acc, ub):
        return (acc * (2.0 * jnp.maximum(ub.astype(F32), 0.0)),)

    (dup,) = _mm("d_mlp_down", "nt", (nh, 1, 1),
                 a_ins=[rows_i(df, th)], a_fn=_ident, b_ins=[full2(w_down)], b_fn=_ident,
                 epi_ins=[(up, _bs((th, D_FF), lambda i, j, k: (i, 0)))], epi_fn=dup_fn,
                 outs=[(_sds((seq, D_FF), BF16), _bs((th, D_FF), lambda i, j, k: (i, 0)))],
                 acc_shape=(th, D_FF))
    (dw_down,) = _mm("dw_mlp_down", "tn", (4, 1, nl),
                     a_ins=[(up, _bs((tl, D), lambda i, j, k: (k, i)))], a_fn=relu2,
                     b_ins=[rows_k(df, tl)], b_fn=_ident,
                     outs=[(_sds((D_FF, D), F32), _bs((D, D), lambda i, j, k: (i, 0)))], acc_shape=(D, D))
    (dw_up,) = _mm("dw_mlp_up", "tn", (1, 4, nl),
                   a_ins=[rows_k(h2, tl)], a_fn=_ident,
                   b_ins=[(dup, _bs((tl, D), lambda i, j, k: (k, j)))], b_fn=_ident,
                   outs=[(_sds((N_CHIPS, D, D), F32), _bs((None, D, D), lambda i, j, k: (j, 0, 0)))],
                   acc_shape=(D, D))

    def mlp_norm_bwd(acc, x1b, dx2b, mixedb, g_mlp, g_mix):
        dxn, dg_mlp = _rms_bwd(x1b, g_mlp, acc)
        dx1b = dx2b + dxn
        dmixedb, dg_mix = _rms_bwd(mixedb.astype(F32), g_mix, dx1b)
        return dx1b, dmixedb, dg_mlp, dg_mix

    dx1, dmixed, dg_pre_mlp_p, dg_post_mix_p = _mm(
        "d_mlp_up", "nt", (nh, 1, 1),
        a_ins=[(dup, _bs((th, D_FF), lambda i, j, k: (i, 0)))], a_fn=_ident,
        b_ins=[full2(w_up_nat)], b_fn=_ident,
        epi_ins=[rows_i(x1, th), rows_i(dx2, th), rows_i(mixed, th), vec(g_pre_mlp), vec(g_post_mix)],
        epi_fn=mlp_norm_bwd,
        outs=[(_sds((seq, D), F32), _bs((th, D), lambda i, j, k: (i, 0))),
              (_sds((seq, D), BF16), _bs((th, D), lambda i, j, k: (i, 0))), part(nh), part(nh)],
        acc_shape=(th, D))
    (dw_o,) = _mm("dw_mix_out", "tn", (1, 1, nl),
                  a_ins=[rows_k(mixin, tl)], a_fn=_ident, b_ins=[rows_k(dmixed, tl)], b_fn=_ident,
                  outs=[(_sds((D, D), F32), _bs((D, D), lambda i, j, k: (0, 0)))], acc_shape=(D, D))

    def gate_bwd(acc, ga, gc, ob, eb, ba, bc, wao, wco):
        sa, sc = gate_values(ga, gc, ba, bc)
        ya, yc = branch_outputs(ob, eb, wao, wco)
        dga = acc * ya * sa * (1.0 - sa)
        dgc = acc * yc * sc * (1.0 - sc)
        dya, dyc = (acc * sa).astype(BF16), (acc * sc).astype(BF16)
        return (dya, dyc, jnp.concatenate([dga, dgc], axis=1), _nt(dya, wao), _nt(dyc, wco),
                jnp.sum(dga, axis=0, keepdims=True), jnp.sum(dgc, axis=0, keepdims=True))

    dya, dyc, dgate, do, de, dbg_a_p, dbg_c_p = _mm(
        "d_mix_out", "nt", (nh, 1, 1),
        a_ins=[rows_i(dmixed, th)], a_fn=_ident, b_ins=[full2(w_o)], b_fn=_ident,
        epi_ins=mix_ins, epi_fn=gate_bwd,
        outs=[(_sds((seq, D), BF16), _bs((th, D), lambda i, j, k: (i, 0)))] * 2
             + [(_sds((seq, 2 * D), BF16), _bs((th, 2 * D), lambda i, j, k: (i, 0))),
                (_sds((seq, ATTN_W), BF16), _bs((th, ATTN_W), lambda i, j, k: (i, 0))),
                (_sds((seq, CONV_W), F32), _bs((th, CONV_W), lambda i, j, k: (i, 0))), part(nh), part(nh)],
        acc_shape=(th, D))
    (dw_ao,) = _mm("dw_attn_out", "tn", (1, 1, nh),
                   a_ins=[(o, _bs((th, ATTN_W), lambda i, j, k: (k, 0)))], a_fn=_ident,
                   b_ins=[rows_k(dya, th)], b_fn=_ident,
                   outs=[(_sds((ATTN_W, D), F32), _bs((ATTN_W, D), lambda i, j, k: (0, 0)))], acc_shape=(ATTN_W, D))
    dq, dk, dv = _attn_bwd(qkv, do, seq)
    (dw_co,) = _mm("dw_conv_out", "tn", (1, 1, nh),
                   a_ins=[(e, _bs((th, CONV_W), lambda i, j, k: (k, 0)))], a_fn=_ident,
                   b_ins=[rows_k(dyc, th)], b_fn=_ident,
                   outs=[(_sds((CONV_W, D), F32), _bs((CONV_W, D), lambda i, j, k: (0, 0)))], acc_shape=(CONV_W, D))
    dconv, dwc_p = _conv_bwd(proj_conv, de, w_conv, seq, tm)
    qkv_w = 3 * ATTN_W
    join_bf16 = lambda *blocks: jnp.concatenate([b.astype(BF16) for b in blocks], axis=1)
    piece = lambda a, t, rows, blk=0: (a, _bs((t, a.shape[1]), (lambda i, j, k: (k, blk)) if rows == "k"
                                             else (lambda i, j, k: (i, blk))))
    (dw_in_qkv,) = _mm("dw_proj_in_qkv", "tn", (1, 1, ni),
                       a_ins=[rows_k(h1, tm)], a_fn=_ident,
                       b_ins=[piece(dq, tm, "k"), piece(dk, tm, "k"), piece(dv, tm, "k")], b_fn=join_bf16,
                       outs=[(_sds((D, qkv_w), F32), _bs((D, qkv_w), lambda i, j, k: (0, 0)))], acc_shape=(D, qkv_w))
    (dw_in_conv,) = _mm("dw_proj_in_conv", "tn", (1, 1, nl),
                        a_ins=[rows_k(h1, tl)], a_fn=_ident, b_ins=[piece(dconv, tl, "k")], b_fn=_ident,
                        outs=[(_sds((D, 3 * CONV_W), F32), _bs((D, 3 * CONV_W), lambda i, j, k: (0, 0)))],
                        acc_shape=(D, 3 * CONV_W))
    (dw_in_gate,) = _mm("dw_proj_in_gate", "tn", (1, 2, nl),
                        a_ins=[rows_k(h1, tl)], a_fn=_ident,
                        b_ins=[(dgate, _bs((tl, D), lambda i, j, k: (k, j)))], b_fn=_ident,
                        outs=[(_sds((D, 2 * D), F32), _bs((D, D), lambda i, j, k: (0, j)))], acc_shape=(D, D))
    dw_in = jnp.concatenate([dw_in_qkv, dw_in_conv, dw_in_gate], axis=1)

    def in_norm_bwd(acc, xb, dx1b, gb):
        dxn, dg = _rms_bwd(xb, gb, acc)
        return dx1b + dxn, dg

    grad_x, dg_pre_mix_p = _mm("d_proj_in", "nt", (nh, 1, 1),
                               a_ins=[piece(dq, th, "i"), piece(dk, th, "i"), piece(dv, th, "i"),
                                      piece(dconv, th, "i"), piece(dgate, th, "i")], a_fn=join_bf16,
                               b_ins=[full2(w_in_nat)], b_fn=_ident,
                               epi_ins=[rows_i(x, th), rows_i(dx1, th), vec(g_pre_mix)], epi_fn=in_norm_bwd,
                               outs=[(_sds((seq, D), F32), _bs((th, D), lambda i, j, k: (i, 0))), part(nh)],
                               acc_shape=(th, D))

    chip_major = lambda a: a.reshape(a.shape[0], N_CHIPS, a.shape[1] // N_CHIPS).transpose(1, 0, 2)
    big = [chip_major(dw_in), chip_major(dw_ao), chip_major(dw_co), dw_o.reshape(N_CHIPS, D // N_CHIPS, D), dw_up,
           dw_down.reshape(N_CHIPS, D_FF // N_CHIPS, D), dw_pg.reshape(N_CHIPS, D // N_CHIPS, D), chip_major(dw_pp)]
    small = (loss_p, [dg_pre_mix_p, dg_post_mix_p, dg_pre_mlp_p, dg_post_mlp_p, dg_ple_p], dbg_a_p, dbg_c_p, dwc_p)
    return grad_x, big, small


RS_GROUPS = ((0,), (4,), (5,), (1, 2, 3, 6, 7))


def _reduce_scatter(big):
    pair = [None] * len(big)
    for gi, group in enumerate(RS_GROUPS):
        for w, s in zip(group, _rs_pair_sum(f"rs_pair_sum_{gi}", [big[w] for w in group])):
            pair[w] = s
    return _rs_exchange_join(pair)


def kernel(x, p, g_pre_mix, w_in, b_gate, w_conv, w_attn_out, w_conv_out, w_o, g_post_mix, g_pre_mlp, w_up, w_down, g_post_mlp, g_ple, w_ple_gate, w_ple_proj, loss_target, m_g_pre_mix, m_w_in, m_b_gate, m_w_conv, m_w_attn_out, m_w_conv_out, m_w_o, m_g_post_mix, m_g_pre_mlp, m_w_up, m_w_down, m_g_post_mlp, m_g_ple, m_w_ple_gate, m_w_ple_proj, v_g_pre_mix, v_w_in, v_b_gate, v_w_conv, v_w_attn_out, v_w_conv_out, v_w_o, v_g_post_mix, v_g_pre_mlp, v_w_up, v_w_down, v_g_post_mlp, v_g_ple, v_w_ple_gate, v_w_ple_proj):
    mats = [w_in, w_attn_out, w_conv_out, w_o, w_up, w_down, w_ple_gate, w_ple_proj]
    mats_m = [m_w_in, m_w_attn_out, m_w_conv_out, m_w_o, m_w_up, m_w_down, m_w_ple_gate, m_w_ple_proj]
    mats_v = [v_w_in, v_w_attn_out, v_w_conv_out, v_w_o, v_w_up, v_w_down, v_w_ple_gate, v_w_ple_proj]
    gains = [g_pre_mix, g_post_mix, g_pre_mlp, g_post_mlp, g_ple]
    gains_m = [m_g_pre_mix, m_g_post_mix, m_g_pre_mlp, m_g_post_mlp, m_g_ple]
    gains_v = [v_g_pre_mix, v_g_post_mix, v_g_pre_mlp, v_g_post_mlp, v_g_ple]

    taps = jnp.concatenate([w_conv[0], jnp.zeros((CONV_PAD_ROWS - 3, LANES), F32)], axis=0)
    gathered = _allgather_weights([w[0].astype(BF16) for w in mats] + [taps])
    cols_joined = lambda a: a.transpose(1, 0, 2).reshape(a.shape[1], N_CHIPS * a.shape[2])
    rows_joined = lambda a: a.reshape(N_CHIPS * a.shape[1], a.shape[2])
    col_sharded = (0, 1, 2, 4, 7)
    wf = [cols_joined(gathered[n]) if n in col_sharded else rows_joined(gathered[n]) for n in range(8)]
    w_conv_full = cols_joined(gathered[8])[0:3, :]
    chip = 2 * lax.axis_index("x") + lax.axis_index("y")

    grad_x, big, small = _local_step(x[0], p[0, 0], loss_target[0], gains, b_gate, w_conv_full, wf)

    shard_grads = _reduce_scatter(big)
    red = _small_allreduce(*small)
    loss = red[0, 0]
    grad_gains = [red[1 + r:2 + r, :] for r in range(5)]
    grad_b_gate = jnp.concatenate([red[6:7, :], red[7:8, :]], axis=1)
    grad_w_conv = lax.dynamic_slice(red[8:11, :], (0, chip * LANES), (3, LANES))[None]

    grads_big = [gr.reshape(w.shape) for gr, w in zip(shard_grads, mats)]
    upd_big = [_adamw(f"adamw_{i}", w, gr, m, v) for i, (w, gr, m, v) in enumerate(zip(mats, grads_big, mats_m, mats_v))]
    pack = lambda vs, bg: jnp.concatenate(list(vs) + [bg.reshape(2, D_MODEL), jnp.zeros((1, D_MODEL), F32)], axis=0)
    upd_small = _adamw("adamw_small", pack(gains, b_gate), pack(grad_gains, grad_b_gate),
                       pack(gains_m, m_b_gate), pack(gains_v, v_b_gate))
    upd_conv = _adamw("adamw_conv", w_conv, grad_w_conv, m_w_conv, v_w_conv)

    def small_out(a, which):
        gains_out = [a[r:r + 1, :] for r in range(5)]
        return gains_out, a[5:7, :].reshape(1, 2 * D_MODEL)

    def ordered(g_pre_mix_, big_, b_gate_, conv_, g_rest):
        return [g_pre_mix_, big_[0], b_gate_, conv_, big_[1], big_[2], big_[3], g_rest[0], g_rest[1], big_[4], big_[5],
                g_rest[2], g_rest[3], big_[6], big_[7]]

    outs = [loss, grad_x[None]]
    outs += ordered(grad_gains[0], grads_big, grad_b_gate, grad_w_conv, grad_gains[1:])
    for which in range(3):
        g_out, b_out = small_out(upd_small[which], which)
        outs += ordered(g_out[0], [u[which] for u in upd_big], b_out, upd_conv[which], g_out[1:])
    return tuple(outs)
```

```python
import jax
import jax.numpy as jnp
from jax import lax
from jax.experimental import pallas as pl
from jax.experimental.pallas import tpu as pltpu

F32 = jnp.float32
BF16 = jnp.bfloat16
MESH = pl.DeviceIdType.MESH

D_MODEL = 1024
N_HEADS = 8
HEAD_DIM = 64
ATTN_W = N_HEADS * HEAD_DIM
CONV_W = 512
D_FF = 4096
PLE_DIM = 256
D_IN = 5120
N_CHIPS = 4
EPS = 1e-6
Q_SCALE = HEAD_DIM ** -0.5

ADAM_LR = 0.001
ADAM_B1 = 0.9
ADAM_B2 = 0.999
ADAM_EPS = 1e-08
ADAM_WD = 0.01
ADAM_STEP = 10

V7X_VMEM_BYTES = 64 * 1024 * 1024
VMEM_LIMIT = V7X_VMEM_BYTES - 8 * 1024 * 1024
LANES = 128
ATT_BLK = 256
SMALL_ROWS = 16
CONV_PAD_ROWS = 16


def _cparams(n_grid):
    return pltpu.CompilerParams(dimension_semantics=("arbitrary",) * n_grid, vmem_limit_bytes=VMEM_LIMIT)


def _bs(shape, fn):
    return pl.BlockSpec(shape, fn)


def _rms_stats(xf):
    return lax.rsqrt(jnp.mean(xf * xf, axis=-1, keepdims=True) + EPS)


def _rms(xf, g):
    return xf * _rms_stats(xf) * g


def _rms_bwd(xf, g, dy):
    r = _rms_stats(xf)
    xh = xf * r
    dyg = dy * g
    dx = r * (dyg - xh * jnp.mean(dyg * xh, axis=-1, keepdims=True))
    return dx, jnp.sum(dy * xh, axis=0, keepdims=True)


def _sig(z):
    return 1.0 / (1.0 + jnp.exp(-z))


def _ident(a):
    return a


def _to_bf16(a):
    return a.astype(BF16)


_DIMS = {"nn": (((1,), (0,)), ((), ())), "nt": (((1,), (1,)), ((), ())), "tn": (((0,), (0,)), ((), ()))}


def _mm(name, mode, grid, a_ins, a_fn, b_ins, b_fn, outs, acc_shape, epi_ins=(), epi_fn=None,
        a_cache=None, a_outs=(), epi_a=(), row_halves=False, a_rowwise=None, epi_rowwise=None):
    nk = grid[2]
    na, nb, ne, no, nao = len(a_ins), len(b_ins), len(epi_ins), len(outs), len(a_outs)
    assert a_cache is None or nk == 1
    assert not a_outs or a_cache is not None
    assert not row_halves or (nk == 1 and grid[1] == 1 and mode in ("nn", "nt"))
    if row_halves:
        a_cache = None
    dims = _DIMS[mode]
    if epi_fn is None:
        epi_fn = lambda acc: (acc,)

    def body(*refs):
        a_refs = refs[:na]
        b_refs = refs[na:na + nb]
        e_refs = refs[na + nb:na + nb + ne]
        o_refs = refs[na + nb + ne:na + nb + ne + no]
        ao_refs = refs[na + nb + ne + no:na + nb + ne + no + nao]
        scratch = list(refs[na + nb + ne + no + nao:])
        acc_ref = scratch.pop(0) if nk > 1 else None
        a_sc = scratch.pop(0) if a_cache is not None else None
        j = pl.program_id(1)
        k = pl.program_id(2)

        def finish(acc):
            res = epi_fn(acc, *[a_refs[t][...] for t in epi_a], *[r[...] for r in e_refs])
            for r, val in zip(o_refs, res):
                r[...] = val.astype(r.dtype)

        if row_halves:
            rows = o_refs[0].shape[0]
            cut = lambda refs, flags: [r.shape[0] == rows if f is None else f
                                       for r, f in zip(refs, flags or [None] * len(refs))]
            a_cut, e_cut = cut(a_refs, a_rowwise), cut(e_refs, epi_rowwise)
            half = lambda r, s, c: r[s * (rows // 2):(s + 1) * (rows // 2), :] if c else r[...]

            def put(r, s, val):
                if r.shape[0] == rows:
                    r[s * (rows // 2):(s + 1) * (rows // 2), :] = val.astype(r.dtype)
                elif s == 0:
                    r[...] = val.astype(r.dtype)
                else:
                    r[...] += val.astype(r.dtype)

            b = b_fn(*[r[...] for r in b_refs])
            prods = []
            for s in range(2):
                res = a_fn(*[half(r, s, c) for r, c in zip(a_refs, a_cut)])
                if nao:
                    for r, val in zip(ao_refs, res[1:]):
                        put(r, s, val)
                    res = res[0]
                prods.append(lax.dot_general(res, b, dims, preferred_element_type=F32))
            for s in range(2):
                res = epi_fn(prods[s], *[half(a_refs[t], s, a_cut[t]) for t in epi_a],
                             *[half(r, s, c) for r, c in zip(e_refs, e_cut)])
                for r, val in zip(o_refs, res):
                    put(r, s, val)
            return

        if a_sc is not None:
            @pl.when(j == 0)
            def _():
                res = a_fn(*[r[...] for r in a_refs])
                if nao:
                    for r, val in zip(ao_refs, res[1:]):
                        r[...] = val.astype(r.dtype)
                    res = res[0]
                a_sc[...] = res
            a = a_sc[...]
        else:
            a = a_fn(*[r[...] for r in a_refs])
        b = b_fn(*[r[...] for r in b_refs])
        prod = lax.dot_general(a, b, dims, preferred_element_type=F32)
        if nk == 1:
            finish(prod)
        else:
            @pl.when(k == 0)
            def _():
                acc_ref[...] = prod

            @pl.when(k > 0)
            def _():
                acc_ref[...] += prod

            @pl.when(k == nk - 1)
            def _():
                finish(acc_ref[...])

    scratch_shapes = []
    if nk > 1:
        scratch_shapes.append(pltpu.VMEM(acc_shape, F32))
    if a_cache is not None:
        scratch_shapes.append(pltpu.VMEM(*a_cache))
    all_outs = list(outs) + list(a_outs)
    res = pl.pallas_call(
        body, name=name, grid=grid,
        in_specs=[s for _, s in a_ins] + [s for _, s in b_ins] + [s for _, s in epi_ins],
        out_specs=[s for _, s in all_outs],
        out_shape=[o for o, _ in all_outs],
        scratch_shapes=scratch_shapes,
        compiler_params=_cparams(3),
    )(*[a for a, _ in a_ins], *[a for a, _ in b_ins], *[a for a, _ in epi_ins])
    return res


def _sds(shape, dtype):
    return jax.ShapeDtypeStruct(shape, dtype)


def _nt(a, b):
    return lax.dot_general(a, b, _DIMS["nt"], preferred_element_type=F32)


def _tn(a, b):
    return lax.dot_general(a, b, _DIMS["tn"], preferred_element_type=F32)


def _nn(a, b):
    return lax.dot_general(a, b, _DIMS["nn"], preferred_element_type=F32)


HEAD_PARTS = 2


def _mlp_down_ple_head(up, x1, p, tgt, g_ple, g_post_mlp, w_down, w_pg, w_pp, seq, tr):
    nblk = seq // tr
    D = D_MODEL

    def body(up_ref, x1_ref, p_ref, t_ref, gp_ref, gm_ref, wd_ref, wpg_ref, wpp_ref,
             dx2_ref, df_ref, dpre_ref, h3_ref, dpp_ref, loss_ref, dgp_ref, dgm_ref):
        gp, gm, wpg, wpp = gp_ref[...], gm_ref[...], wpg_ref[...], wpp_ref[...]
        halves = [pl.ds(n * (tr // HEAD_PARTS), tr // HEAD_PARTS) for n in range(HEAD_PARTS)]
        w_down = wd_ref[...]
        fb = []
        for r in halves:
            hidden = jnp.maximum(up_ref[r, :].astype(F32), 0.0)
            fb.append(_nn((hidden * hidden).astype(BF16), w_down))
        loss, dgp_sum, dgm_sum = 0.0, 0.0, 0.0
        for s, r in enumerate(halves):
            x2b = x1_ref[r, :] + _rms(fb[s], gm)
            h3 = _rms(x2b, gp).astype(BF16)
            gate = _sig(_nn(h3, wpg))
            pp = _nn(p_ref[r, :].astype(BF16), wpp)
            err = x2b + gate * pp - t_ref[r, :]
            dx3 = err * (1.0 / D)
            dpre = (dx3 * pp * gate * (1.0 - gate)).astype(BF16)
            h3_ref[r, :] = h3
            dpp_ref[r, :] = (dx3 * gate).astype(BF16)
            dpre_ref[r, :] = dpre
            dxn, dgp = _rms_bwd(x2b, gp, _nt(dpre, wpg))
            dx2 = dx3 + dxn
            dx2_ref[r, :] = dx2
            dfb, dgm = _rms_bwd(fb[s], gm, dx2)
            df_ref[r, :] = dfb.astype(BF16)
            loss = loss + jnp.sum(err * err, axis=0, keepdims=True)
            dgp_sum, dgm_sum = dgp_sum + dgp, dgm_sum + dgm
        loss_ref[...] = loss * (0.5 / D)
        dgp_ref[...] = dgp_sum
        dgm_ref[...] = dgm_sum

    rows = _bs((tr, D), lambda i: (i, 0))
    vec = _bs((1, D), lambda i: (0, 0))
    part = _bs((None, 1, D), lambda i: (i, 0, 0))
    return pl.pallas_call(
        body, name="mlp_down_ple_head", grid=(nblk,),
        in_specs=[_bs((tr, D_FF), lambda i: (i, 0)), rows, _bs((tr, PLE_DIM), lambda i: (i, 0)), rows, vec, vec,
                  _bs((D_FF, D), lambda i: (0, 0)), _bs((D, D), lambda i: (0, 0)), _bs((PLE_DIM, D), lambda i: (0, 0))],
        out_specs=[rows] * 5 + [part] * 3,
        out_shape=[_sds((seq, D), F32)] + [_sds((seq, D), BF16)] * 4 + [_sds((nblk, 1, D), F32)] * 3,
        compiler_params=_cparams(1),
    )(up, x1, p, tgt, g_ple, g_post_mlp, w_down, w_pg, w_pp)


def _shift_rows_down(u, prev, n):
    rows = u.shape[0]
    ridx = lax.broadcasted_iota(jnp.int32, u.shape, 0)
    out = pltpu.roll(u, n, 0)
    for r in range(n):
        out = jnp.where(ridx == r, prev[8 - n + r:8 - n + r + 1, :], out)
    del rows
    return out


def _shift_rows_up(u, nxt, n):
    rows = u.shape[0]
    ridx = lax.broadcasted_iota(jnp.int32, u.shape, 0)
    out = pltpu.roll(u, rows - n, 0)
    for r in range(n):
        out = jnp.where(ridx == rows - n + r, nxt[r:r + 1, :], out)
    return out


CONV_COL0 = 0


def _conv_fwd(proj, w_conv, seq, tr):
    hb = tr // 8

    def body(cb_ref, cc_ref, cu_ref, ccp_ref, cup_ref, w_ref, e_ref):
        i = pl.program_id(0)
        u = cc_ref[...] * cu_ref[...]
        up = jnp.where(i > 0, ccp_ref[...] * cup_ref[...], 0.0)
        w = w_ref[...]
        d = w[0:1, :] * _shift_rows_down(u, up, 2) + w[1:2, :] * _shift_rows_down(u, up, 1) + w[2:3, :] * u
        e_ref[...] = (cb_ref[...] * d).astype(BF16)

    prev = lambda c: (lambda i: (jnp.maximum(i * hb - 1, 0), c))
    return pl.pallas_call(
        body, name="conv_fwd", grid=(seq // tr,),
        in_specs=[_bs((tr, CONV_W), lambda i: (i, CONV_COL0)),
                  _bs((tr, CONV_W), lambda i: (i, CONV_COL0 + 1)),
                  _bs((tr, CONV_W), lambda i: (i, CONV_COL0 + 2)),
                  _bs((8, CONV_W), prev(CONV_COL0 + 1)),
                  _bs((8, CONV_W), prev(CONV_COL0 + 2)),
                  _bs((3, CONV_W), lambda i: (0, 0))],
        out_specs=_bs((tr, CONV_W), lambda i: (i, 0)),
        out_shape=_sds((seq, CONV_W), BF16),
        compiler_params=_cparams(1),
    )(proj, proj, proj, proj, proj, w_conv)


def _conv_bwd(proj, de, w_conv, seq, tr):
    hb = tr // 8
    nblk = seq // tr

    def body(cb_ref, cc_ref, cu_ref, ccp_ref, cup_ref, cbn_ref, de_ref, den_ref, w_ref, o_ref, dw_ref):
        i = pl.program_id(0)
        cc, cu, cb = cc_ref[...], cu_ref[...], cb_ref[...]
        u = cc * cu
        up = jnp.where(i > 0, ccp_ref[...] * cup_ref[...], 0.0)
        u1 = _shift_rows_down(u, up, 1)
        u2 = _shift_rows_down(u, up, 2)
        de_ = de_ref[...]
        dd = de_ * cb
        ddn = jnp.where(i < nblk - 1, den_ref[...] * cbn_ref[...], 0.0)
        w = w_ref[...]
        du = w[2:3, :] * dd + w[1:2, :] * _shift_rows_up(dd, ddn, 1) + w[0:1, :] * _shift_rows_up(dd, ddn, 2)
        o_ref[:, 0:CONV_W] = (de_ * (w[0:1, :] * u2 + w[1:2, :] * u1 + w[2:3, :] * u)).astype(BF16)
        o_ref[:, CONV_W:2 * CONV_W] = (du * cu).astype(BF16)
        o_ref[:, 2 * CONV_W:3 * CONV_W] = (du * cc).astype(BF16)
        ridx = lax.broadcasted_iota(jnp.int32, (8, CONV_W), 0)
        dw0 = jnp.sum(dd * u2, axis=0, keepdims=True)
        dw1 = jnp.sum(dd * u1, axis=0, keepdims=True)
        dw2 = jnp.sum(dd * u, axis=0, keepdims=True)
        dw_ref[...] = jnp.where(ridx == 0, dw0, jnp.where(ridx == 1, dw1, jnp.where(ridx == 2, dw2, 0.0)))

    prev = lambda c: (lambda i: (jnp.maximum(i * hb - 1, 0), c))
    nxt = lambda c: (lambda i: (jnp.minimum((i + 1) * hb, seq // 8 - 1), c))
    return pl.pallas_call(
        body, name="conv_bwd", grid=(nblk,),
        in_specs=[_bs((tr, CONV_W), lambda i: (i, CONV_COL0)),
                  _bs((tr, CONV_W), lambda i: (i, CONV_COL0 + 1)),
                  _bs((tr, CONV_W), lambda i: (i, CONV_COL0 + 2)),
                  _bs((8, CONV_W), prev(CONV_COL0 + 1)),
                  _bs((8, CONV_W), prev(CONV_COL0 + 2)),
                  _bs((8, CONV_W), nxt(CONV_COL0)),
                  _bs((tr, CONV_W), lambda i: (i, 0)),
                  _bs((8, CONV_W), nxt(0)),
                  _bs((3, CONV_W), lambda i: (0, 0))],
        out_specs=[_bs((tr, 3 * CONV_W), lambda i: (i, 0)), _bs((None, 8, CONV_W), lambda i: (i, 0, 0))],
        out_shape=[_sds((seq, 3 * CONV_W), BF16), _sds((nblk, 8, CONV_W), F32)],
        compiler_params=_cparams(1),
    )(proj, proj, proj, proj, proj, proj, de, de, w_conv)


def _log_gates(z):
    lse = jnp.log(1.0 + jnp.exp(-jnp.abs(z)))
    log_beta = jnp.minimum(z, 0.0) - lse
    return log_beta, log_beta - z


DEAD_LOG_WEIGHT = -110.0
NO_TILE = -1e30


def _first_live_tile(start, scores, live_sc):
    def alive():
        return jnp.max(jnp.maximum(live_sc[0], live_sc[1])) > DEAD_LOG_WEIGHT

    def step(c):
        for h, z in enumerate(scores(c[0])):
            live_sc[h] = live_sc[h] + jnp.sum(_log_gates(z)[1], axis=-1, keepdims=True)
        return c[0] - 1, alive()

    j_end, _ = lax.while_loop(lambda c: jnp.logical_and(c[0] >= 0, c[1]), step, (start, alive()))
    return j_end + 1


def _attn_fwd(proj, seq):
    blk = ATT_BLK
    nq = seq // blk
    npair = N_HEADS // 2

    def body(q_ref, k_ref, v_ref, o_ref, z0_sc, z1_sc, w0_sc, w1_sc, tot_sc, acc_sc):
        i = pl.program_id(1)
        is_a = lax.broadcasted_iota(jnp.int32, (1, LANES), 1) < HEAD_DIM
        q2 = (q_ref[...] * Q_SCALE).astype(BF16)
        zero = jnp.zeros_like(q2)
        qs = (jnp.where(is_a, q2, zero), jnp.where(is_a, zero, q2))
        row = lax.broadcasted_iota(jnp.int32, (blk, blk), 0)
        col = lax.broadcasted_iota(jnp.int32, (blk, blk), 1)
        tri = (row > col).astype(BF16)
        causal = col < row

        def tile_of(ref, j):
            return ref[pl.ds(pl.multiple_of(j * blk, blk), blk), :].astype(BF16)

        def scores(j):
            k2 = tile_of(k_ref, j)
            return [_nt(qs[h], k2) for h in range(2)]

        has_left = i > 0
        left = jnp.maximum(i - 1, 0)

        g_d = [_log_gates(z) for z in scores(i)]
        g_l = [_log_gates(z) for z in scores(left)]
        keep_d = [jnp.where(causal, g[1], 0.0) for g in g_d]
        suf_d = [_nn(lk.astype(BF16), tri) for lk in keep_d]
        suf_l = [_nn(g[1].astype(BF16), tri) for g in g_l]
        v_d, v_l = tile_of(v_ref, i), tile_of(v_ref, left)
        pv = []
        for h in range(2):
            sum_d = jnp.sum(keep_d[h], axis=-1, keepdims=True)
            w_d = jnp.where(causal, jnp.exp(g_d[h][0] + suf_d[h]), 0.0)
            w_l = jnp.exp(g_l[h][0] + (jnp.where(has_left, sum_d, NO_TILE) + suf_l[h]))
            pv.append(_nn(w_d.astype(BF16), v_d) + _nn(w_l.astype(BF16), v_l))
            tot_sc[h] = sum_d + jnp.sum(g_l[h][1], axis=-1, keepdims=True)
        acc_sc[...] = jnp.where(is_a, pv[0], pv[1])

        z_bufs, w_bufs = (z0_sc, z1_sc), (w0_sc, w1_sc)

        def alive():
            return jnp.max(jnp.maximum(tot_sc[0], tot_sc[1])) > DEAD_LOG_WEIGHT

        def put(ref, vals):
            for h in range(2):
                ref[h] = vals[h]

        def weights(zs):
            gates = [_log_gates(z) for z in zs]
            sums = [_nn(g[1].astype(BF16), tri) for g in gates]
            ws = []
            for h in range(2):
                ws.append(jnp.exp(gates[h][0] + (tot_sc[h] + sums[h])).astype(BF16))
                tot_sc[h] = tot_sc[h] + jnp.sum(gates[h][1], axis=-1, keepdims=True)
            return ws

        def add_values(w_buf, j):
            v2 = tile_of(v_ref, j)
            acc_sc[...] += jnp.where(is_a, _nn(w_buf[0], v2), _nn(w_buf[1], v2))

        def trip(j, s):
            add_values(w_bufs[s], j + 1)
            put(z_bufs[1 - s], scores(jnp.maximum(j - 1, 0)))
            put(w_bufs[1 - s], weights((z_bufs[s][0], z_bufs[s][1])))

        @pl.when(jnp.logical_and(i >= 2, alive()))
        def _():
            put(z0_sc, scores(i - 2))
            w0_sc[...] = jnp.zeros_like(w0_sc)

            def two_trips(c):
                trip(c[0], 0)
                trip(c[0] - 1, 1)
                return c[0] - 2, alive()

            j_next, still = lax.while_loop(lambda c: jnp.logical_and(c[0] >= 1, c[1]), two_trips, (i - 2, i >= 2))
            one_left = jnp.logical_and(j_next == 0, still)

            @pl.when(one_left)
            def _():
                trip(0, 0)
                add_values(w1_sc, 0)

            @pl.when(jnp.logical_not(one_left))
            def _():
                add_values(w0_sc, j_next + 1)

        o_ref[...] = acc_sc[...].astype(BF16)

    return pl.pallas_call(
        body, name="attn_fwd", grid=(npair, nq),
        in_specs=[_bs((blk, LANES), lambda p, i: (i, p)),
                  _bs((seq, LANES), lambda p, i: (0, npair + p)),
                  _bs((seq, LANES), lambda p, i: (0, 2 * npair + p))],
        out_specs=_bs((blk, LANES), lambda p, i: (i, p)),
        out_shape=_sds((seq, ATTN_W), BF16),
        scratch_shapes=[pltpu.VMEM((2, blk, blk), F32), pltpu.VMEM((2, blk, blk), F32),
                        pltpu.VMEM((2, blk, blk), BF16), pltpu.VMEM((2, blk, blk), BF16),
                        pltpu.VMEM((2, blk, 1), F32), pltpu.VMEM((blk, LANES), F32)],
        compiler_params=_cparams(2),
    )(proj, proj, proj)


def _attn_bwd(proj, do, seq):
    blk = ATT_BLK
    nq = seq // blk
    npair = N_HEADS // 2

    def body(q_ref, k_ref, v_ref, do_ref, dq_ref, dk_ref, dv_ref,
             prod0_sc, prod1_sc, pend0_sc, pend1_sc, tot_sc, live_sc, cum_sc, pre_sc, dq_sc):
        i = pl.program_id(1)

        @pl.when(i == 0)
        def _():
            dk_ref[...] = jnp.zeros_like(dk_ref)
            dv_ref[...] = jnp.zeros_like(dv_ref)

        is_a = lax.broadcasted_iota(jnp.int32, (1, LANES), 1) < HEAD_DIM
        q2 = (q_ref[...] * Q_SCALE).astype(BF16)
        do2 = do_ref[...]
        zero = jnp.zeros_like(q2)
        qs = (jnp.where(is_a, q2, zero), jnp.where(is_a, zero, q2))
        dos = (jnp.where(is_a, do2, zero), jnp.where(is_a, zero, do2))
        row = lax.broadcasted_iota(jnp.int32, (blk, blk), 0)
        col = lax.broadcasted_iota(jnp.int32, (blk, blk), 1)
        tri_after = (row > col).astype(BF16)
        tri_excl = (row < col).astype(BF16)
        causal = col < row

        def tile_of(ref, j):
            return ref[pl.ds(pl.multiple_of(j * blk, blk), blk), :].astype(BF16)

        def scores(j):
            k2 = tile_of(k_ref, j)
            return [_nt(qs[h], k2) for h in range(2)]

        def products(j):
            v2 = tile_of(v_ref, j)
            return scores(j) + [_nt(dos[h], v2) for h in range(2)]

        def row_sum(a):
            return jnp.sum(a, axis=-1, keepdims=True)

        def grad_matmuls(ws, dzs, j):
            rows = pl.ds(pl.multiple_of(j * blk, blk), blk)
            k2 = tile_of(k_ref, j)
            dq_sc[...] += jnp.where(is_a, _nn(dzs[0], k2), _nn(dzs[1], k2))
            dk_ref[rows, :] += jnp.where(is_a, _tn(dzs[0], q2), _tn(dzs[1], q2))
            if ws is not None:
                dv_ref[rows, :] += jnp.where(is_a, _tn(ws[0], do2), _tn(ws[1], do2))

        has_left = i > 0
        left = jnp.maximum(i - 1, 0)

        p_d, p_l = products(i), products(left)
        g_d = [_log_gates(z) for z in p_d[:2]]
        g_l = [_log_gates(z) for z in p_l[:2]]
        keep_d = [jnp.where(causal, g[1], 0.0) for g in g_d]
        suf_d = [_nn(lk.astype(BF16), tri_after) for lk in keep_d]
        suf_l = [_nn(g[1].astype(BF16), tri_after) for g in g_l]
        w_d, w_l, gg_d, gg_l = [], [], [], []
        for h in range(2):
            sum_d = row_sum(keep_d[h])
            w_d.append(jnp.where(causal, jnp.exp(g_d[h][0] + suf_d[h]), 0.0))
            w_l.append(jnp.exp(g_l[h][0] + (jnp.where(has_left, sum_d, NO_TILE) + suf_l[h])))
            gg_d.append(p_d[2 + h] * w_d[h])
            gg_l.append(p_l[2 + h] * w_l[h])
            tot_sc[h] = sum_d + row_sum(g_l[h][1])
        before_d = [_nn(g.astype(BF16), tri_excl) for g in gg_d]
        before_l = [_nn(g.astype(BF16), tri_excl) for g in gg_l]
        dz_d, dz_l = [], []
        for h in range(2):
            beta_d, beta_l = jnp.exp(g_d[h][0]), jnp.exp(g_l[h][0])
            dz_l.append((gg_l[h] * (1.0 - beta_l) - before_l[h] * beta_l).astype(BF16))
            dz = gg_d[h] * (1.0 - beta_d) - (row_sum(gg_l[h]) + before_d[h]) * beta_d
            dz_d.append(jnp.where(causal, dz, 0.0).astype(BF16))
        dq_sc[...] = jnp.zeros_like(dq_sc)
        grad_matmuls([w.astype(BF16) for w in w_l], dz_l, left)
        grad_matmuls([w.astype(BF16) for w in w_d], dz_d, i)

        live_sc[...] = tot_sc[...]
        first = _first_live_tile(i - 2, scores, live_sc)
        trips = i - 1 - first
        prod_bufs, pend_bufs = (prod0_sc, prod1_sc), (pend0_sc, pend1_sc)

        def local_grads(prods):
            zs, dws = prods[:2], prods[2:]
            gates = [_log_gates(z) for z in zs]
            sums = [_nn(g[1].astype(BF16), tri_after) for g in gates]
            ws, gs = [], []
            for h in range(2):
                cum = cum_sc[h] + row_sum(gates[h][1])
                cum_sc[h] = cum
                ws.append(jnp.exp(gates[h][0] + ((live_sc[h] - cum) + sums[h])))
                gs.append(dws[h] * ws[h])
            befores = [_nn(g.astype(BF16), tri_excl) for g in gs]
            dzs = []
            for h in range(2):
                beta = jnp.exp(gates[h][0])
                dzs.append((gs[h] * (1.0 - beta) - (pre_sc[h] + befores[h]) * beta).astype(BF16))
                pre_sc[h] = pre_sc[h] + row_sum(gs[h])
            return [w.astype(BF16) for w in ws] + dzs

        def put(ref, vals):
            for n, val in enumerate(vals):
                ref[n] = val

        def flush(pend, j):
            grad_matmuls([pend[0], pend[1]], [pend[2], pend[3]], j)

        def trip(j, s):
            flush(pend_bufs[s], jnp.maximum(j - 1, first))
            put(prod_bufs[1 - s], products(j + 1))
            put(pend_bufs[1 - s], local_grads([prod_bufs[s][n] for n in range(4)]))

        def earlier_keys_share(j, mask):
            dzs = []
            for h, z in enumerate(scores(j)):
                beta = jnp.exp(_log_gates(z)[0])
                dzs.append(jnp.where(mask, -pre_sc[h] * beta, 0.0).astype(BF16))
            grad_matmuls(None, dzs, j)

        @pl.when(trips > 0)
        def _():
            cum_sc[...] = jnp.zeros_like(cum_sc)
            pre_sc[...] = jnp.zeros_like(pre_sc)
            pend0_sc[...] = jnp.zeros_like(pend0_sc)
            put(prod0_sc, products(first))

            def two_trips(pp, carry):
                trip(first + 2 * pp, 0)
                trip(first + 2 * pp + 1, 1)
                return carry

            lax.fori_loop(0, trips // 2, two_trips, 0)
            odd = trips % 2 == 1

            @pl.when(odd)
            def _():
                trip(i - 2, 0)
                flush(pend1_sc, i - 2)

            @pl.when(jnp.logical_not(odd))
            def _():
                flush(pend0_sc, i - 2)

            earlier_keys_share(i - 1, True)
            earlier_keys_share(i, causal)

        dq_ref[...] = (dq_sc[...] * Q_SCALE).astype(BF16)

    qmap = lambda p, i: (i, p)
    return pl.pallas_call(
        body, name="attn_bwd", grid=(npair, nq),
        in_specs=[_bs((blk, LANES), qmap),
                  _bs((seq, LANES), lambda p, i: (0, npair + p)),
                  _bs((seq, LANES), lambda p, i: (0, 2 * npair + p)),
                  _bs((blk, LANES), qmap)],
        out_specs=[_bs((blk, LANES), qmap),
                   _bs((seq, LANES), lambda p, i: (0, p)),
                   _bs((seq, LANES), lambda p, i: (0, p))],
        out_shape=[_sds((seq, ATTN_W), BF16)] + [_sds((seq, ATTN_W), F32)] * 2,
        scratch_shapes=[pltpu.VMEM((4, blk, blk), F32), pltpu.VMEM((4, blk, blk), F32),
                        pltpu.VMEM((4, blk, blk), BF16), pltpu.VMEM((4, blk, blk), BF16),
                        pltpu.VMEM((2, blk, 1), F32), pltpu.VMEM((2, blk, 1), F32), pltpu.VMEM((2, blk, 1), F32),
                        pltpu.VMEM((2, blk, 1), F32), pltpu.VMEM((blk, LANES), F32)],
        compiler_params=_cparams(2),
    )(proj, proj, proj, do)


def _elementwise(name, fn, ins, out_dtypes):
    rows, cols = ins[0].shape
    tr = rows
    for cand in (512, 256, 128, 64, 32, 16, 8):
        if rows % cand == 0 and cand * cols * 4 <= 2 * 1024 * 1024:
            tr = cand
            break
    n_in = len(ins)

    def body(*refs):
        res = fn(*[r[...] for r in refs[:n_in]])
        for r, val in zip(refs[n_in:], res):
            r[...] = val.astype(r.dtype)

    spec = _bs((tr, cols), lambda i: (i, 0))
    return pl.pallas_call(
        body, name=name, grid=(rows // tr,),
        in_specs=[spec] * n_in, out_specs=[spec] * len(out_dtypes),
        out_shape=[_sds((rows, cols), dt) for dt in out_dtypes],
        compiler_params=_cparams(1),
    )(*ins)


def _adamw_fn(w, g, m, v):
    m = ADAM_B1 * m + (1.0 - ADAM_B1) * g
    v = ADAM_B2 * v + (1.0 - ADAM_B2) * (g * g)
    m_hat = m / (1.0 - ADAM_B1 ** ADAM_STEP)
    v_hat = v / (1.0 - ADAM_B2 ** ADAM_STEP)
    delta = -ADAM_LR * (m_hat / (jnp.sqrt(v_hat) + ADAM_EPS) + ADAM_WD * w)
    return delta, m, v


def _adamw(name, w, g, m, v):
    shape = w.shape
    as2d = lambda a: a.reshape(-1, shape[-1])
    delta, nm, nv = _elementwise(name, _adamw_fn, [as2d(w), as2d(g), as2d(m), as2d(v)], [F32, F32, F32])
    return delta.reshape(shape), nm.reshape(shape), nv.reshape(shape)


def _place():
    return lax.axis_index("x"), lax.axis_index("y"), lax.axis_index("c")


ANY = pl.BlockSpec(memory_space=pl.ANY)
VMEM_WHOLE = pl.BlockSpec(memory_space=pltpu.VMEM)


def _allgather_weights(shards):
    n = len(shards)

    def body(*refs):
        src, dst = refs[:n], refs[n:2 * n]
        send_sems, recv_sems, local_sems = refs[2 * n:]
        x, y, c = _place()
        me, sibling, mychip = (x, y, c), (x, y, 1 - c), 2 * x + y

        x_nbr, y_nbr, diag = 2 * (1 - x) + y, 2 * x + (1 - y), 2 * (1 - x) + (1 - y)
        to_x, to_y = (1 - x, y, c), (x, 1 - y, c)

        def parts(w):
            hr = src[w].shape[0] // 2
            first = hr // 2 if hr % 32 == 0 else hr
            return first, hr - first

        def rows_of(w, chip, half, route):
            hr = src[w].shape[0] // 2
            first, second = parts(w)
            start, size = {0: (0, hr), 1: (0, hr), 2: (0, first), 3: (first, second)}[route]
            return dst[w].at[chip, pl.ds(half * hr + start, size)]

        def copy(w, k, src_ref, dst_ref, to):
            return pltpu.make_async_remote_copy(src_ref=src_ref, dst_ref=dst_ref, send_sem=send_sems.at[w, k],
                                                recv_sem=recv_sems.at[w, k], device_id=to, device_id_type=MESH)

        def landed(w, route):
            chip = {0: x_nbr, 1: y_nbr, 2: diag, 3: diag}[route]
            return rows_of(w, chip, c, route), chip

        def routes(w):
            return (0, 1, 2, 3) if parts(w)[1] else (0, 1, 2)

        started, local = [], []
        for w in range(n):
            hr = src[w].shape[0] // 2
            own = pltpu.make_async_copy(src[w], dst[w].at[mychip], local_sems.at[w])
            own.start()
            local.append(own)
            mine = src[w].at[pl.ds(c * hr, hr)]
            for route, to in ((0, to_x), (1, to_y)):
                cp = copy(w, route, mine, rows_of(w, mychip, c, route), to)
                cp.start()
                started.append(cp)

        def pass_on(w, route):
            got, chip = landed(w, route)
            copy(w, route, got, got, me).wait_recv()
            if route == 1:
                part = rows_of(w, chip, c, 2)
                started.append(copy(w, 2, part, part, to_x))
                started[-1].start()
            if route == 0 and parts(w)[1]:
                part = rows_of(w, chip, c, 3)
                started.append(copy(w, 3, part, part, to_y))
                started[-1].start()
            started.append(copy(w, 4 + route, got, got, sibling))
            started[-1].start()

        for w in range(n):
            pass_on(w, 1)
            pass_on(w, 0)
        for w in range(n):
            for route in routes(w)[2:]:
                pass_on(w, route)
        for w in range(n):
            for route in routes(w):
                chip = landed(w, route)[1]
                from_sib = rows_of(w, chip, 1 - c, route)
                copy(w, 4 + route, from_sib, from_sib, me).wait_recv()
        for cp in local:
            cp.wait()
        for cp in started:
            cp.wait_send()

    return pl.pallas_call(
        body, name="allgather_weights",
        in_specs=[VMEM_WHOLE] * n, out_specs=[VMEM_WHOLE] * n,
        out_shape=[_sds((N_CHIPS,) + s.shape, s.dtype) for s in shards],
        scratch_shapes=[pltpu.SemaphoreType.DMA((n, 8)), pltpu.SemaphoreType.DMA((n, 8)),
                        pltpu.SemaphoreType.DMA((n,))],
        compiler_params=pltpu.CompilerParams(vmem_limit_bytes=VMEM_LIMIT),
    )(*shards)


SUM_ROWS = 64


def _rs_pair_sum(name, grads):
    n = len(grads)

    def body(*refs):
        g, out = refs[:n], refs[n:2 * n]
        stage, give16, land, keep = (refs[m * n:(m + 1) * n] for m in range(2, 6))
        send_sems, recv_sems, stage_sems, keep_sems = refs[6 * n:]
        x, y, c = _place()
        sibling = (x, y, 1 - c)

        def over_rows(w, fn):
            nb = g[w].shape[1] // 2 // SUM_ROWS

            def step(idx, carry):
                fn(idx // nb, pl.ds(pl.multiple_of((idx % nb) * SUM_ROWS, SUM_ROWS), SUM_ROWS))
                return carry

            lax.fori_loop(0, N_CHIPS * nb, step, 0)

        loads = []
        for w in range(n):
            hr = g[w].shape[1] // 2
            st = pltpu.make_async_copy(g[w].at[:, pl.ds((1 - c) * hr, hr)], stage[w], stage_sems.at[w])
            kp = pltpu.make_async_copy(g[w].at[:, pl.ds(c * hr, hr)], keep[w], keep_sems.at[w])
            st.start()
            kp.start()
            loads.append((st, kp))
        gives = []
        for w in range(n):
            loads[w][0].wait()

            def narrow(k, rows, w=w):
                give16[w][k, rows, :] = stage[w][k, rows, :].astype(BF16)

            over_rows(w, narrow)
            give = pltpu.make_async_remote_copy(src_ref=give16[w], dst_ref=land[w], send_sem=send_sems.at[w],
                                                recv_sem=recv_sems.at[w], device_id=sibling, device_id_type=MESH)
            give.start()
            gives.append(give)
        for w in range(n):
            loads[w][1].wait()
            gives[w].wait_recv()

            def add(k, rows, w=w):
                out[w][k, rows, :] = (keep[w][k, rows, :] + land[w][k, rows, :].astype(F32)).astype(BF16)

            over_rows(w, add)
        for give in gives:
            give.wait_send()

    half = [(N_CHIPS, a.shape[1] // 2, a.shape[2]) for a in grads]
    wide = [pltpu.VMEM(s, F32) for s in half]
    narrow_bufs = [pltpu.VMEM(s, BF16) for s in half]
    sems = pltpu.SemaphoreType.DMA((n,))
    return pl.pallas_call(
        body, name=name,
        in_specs=[ANY] * n, out_specs=[VMEM_WHOLE] * n, out_shape=[_sds(s, BF16) for s in half],
        scratch_shapes=wide + narrow_bufs + narrow_bufs + wide + [sems, sems, sems, sems],
        compiler_params=pltpu.CompilerParams(vmem_limit_bytes=VMEM_LIMIT),
    )(*grads)


def _rs_exchange_join(parts):
    n = len(parts)

    def body(*refs):
        t, full = refs[:n], refs[n:2 * n]
        got_x, got_y, pass_on, got_2 = (refs[m * n:(m + 1) * n] for m in range(2, 6))
        send_sems, recv_sems = refs[6 * n:]
        x, y, c = _place()
        mychip, sibling = 2 * x + y, (x, y, 1 - c)
        x_nbr, y_nbr, diag = 2 * (1 - x) + y, 2 * x + (1 - y), 2 * (1 - x) + (1 - y)
        to_x, to_y = (1 - x, y, c), (x, 1 - y, c)
        sends = []

        def copy(w, k, src_ref, dst_ref, to):
            return pltpu.make_async_remote_copy(src_ref=src_ref, dst_ref=dst_ref, send_sem=send_sems.at[w, k],
                                                recv_sem=recv_sems.at[w, k], device_id=to, device_id_type=MESH)

        def start(cp):
            cp.start()
            sends.append(cp)

        def add_rows(w, count, fn):
            def step(idx, carry):
                fn(pl.ds(pl.multiple_of(idx * SUM_ROWS, SUM_ROWS), SUM_ROWS), pl.multiple_of(idx * SUM_ROWS, SUM_ROWS))
                return carry
            lax.fori_loop(0, count // SUM_ROWS, step, 0)

        f32 = lambda v: v.astype(F32)
        for w in range(n):
            ha = t[w].shape[1] // 2
            part_a, part_b = pl.ds(0, ha), pl.ds(ha, ha)
            start(copy(w, 0, t[w].at[x_nbr, part_a], got_x[w].at[0], to_x))
            start(copy(w, 1, t[w].at[diag, part_a], got_x[w].at[1], to_x))
            start(copy(w, 2, t[w].at[y_nbr, part_b], got_y[w].at[0], to_y))
            start(copy(w, 3, t[w].at[diag, part_b], got_y[w].at[1], to_y))
        for w in range(n):
            hr = t[w].shape[1]
            ha = hr // 2
            for k in (0, 1):
                copy(w, k, got_x[w].at[k], got_x[w].at[k], to_x).wait_recv()

            def sum_a(rows, r, w=w, hr=hr):
                full[w][pl.ds(pl.multiple_of(c * hr + r, SUM_ROWS), SUM_ROWS), :] = \
                    f32(t[w][mychip, rows, :]) + f32(got_x[w][0, rows, :])
                pass_on[w][rows, :] = (f32(t[w][y_nbr, rows, :]) + f32(got_x[w][1, rows, :])).astype(BF16)

            add_rows(w, ha, sum_a)
            start(copy(w, 4, pass_on[w].at[pl.ds(0, ha)], got_2[w].at[pl.ds(0, ha)], to_y))
            for k in (2, 3):
                copy(w, k, got_y[w].at[k - 2], got_y[w].at[k - 2], to_y).wait_recv()

            def sum_b(rows, r, w=w, hr=hr, ha=ha):
                lower = pl.ds(pl.multiple_of(ha + r, SUM_ROWS), SUM_ROWS)
                full[w][pl.ds(pl.multiple_of(c * hr + ha + r, SUM_ROWS), SUM_ROWS), :] = \
                    f32(t[w][mychip, lower, :]) + f32(got_y[w][0, rows, :])
                pass_on[w][lower, :] = (f32(t[w][x_nbr, lower, :]) + f32(got_y[w][1, rows, :])).astype(BF16)

            add_rows(w, ha, sum_b)
            start(copy(w, 5, pass_on[w].at[pl.ds(ha, ha)], got_2[w].at[pl.ds(ha, ha)], to_x))
        for w in range(n):
            hr = t[w].shape[1]
            ha = hr // 2
            copy(w, 4, got_2[w].at[pl.ds(0, ha)], got_2[w].at[pl.ds(0, ha)], to_y).wait_recv()
            copy(w, 5, got_2[w].at[pl.ds(ha, ha)], got_2[w].at[pl.ds(ha, ha)], to_x).wait_recv()

            def finish(rows, r, w=w, hr=hr):
                out_rows = pl.ds(pl.multiple_of(c * hr + r, SUM_ROWS), SUM_ROWS)
                full[w][out_rows, :] = full[w][out_rows, :] + f32(got_2[w][rows, :])

            add_rows(w, hr, finish)
            mine = full[w].at[pl.ds(c * hr, hr)]
            start(copy(w, 6, mine, mine, sibling))
        for w in range(n):
            hr = t[w].shape[1]
            theirs = full[w].at[pl.ds((1 - c) * hr, hr)]
            copy(w, 6, theirs, theirs, sibling).wait_recv()
        for cp in sends:
            cp.wait_send()

    half = lambda a: pltpu.VMEM((2, a.shape[1] // 2, a.shape[2]), a.dtype)
    whole = lambda a: pltpu.VMEM(a.shape[1:], a.dtype)
    return pl.pallas_call(
        body, name="rs_exchange_join",
        in_specs=[VMEM_WHOLE] * n, out_specs=[VMEM_WHOLE] * n,
        out_shape=[_sds((2 * a.shape[1], a.shape[2]), F32) for a in parts],
        scratch_shapes=[half(a) for a in parts] + [half(a) for a in parts] + [whole(a) for a in parts]
        + [whole(a) for a in parts] + [pltpu.SemaphoreType.DMA((n, 7)), pltpu.SemaphoreType.DMA((n, 7))],
        compiler_params=pltpu.CompilerParams(vmem_limit_bytes=VMEM_LIMIT),
    )(*parts)


def _small_allreduce(loss_p, dg_parts, dbg_a, dbg_c, dwc):
    ins = [loss_p] + list(dg_parts) + [dbg_a, dbg_c, dwc]
    n_in = len(ins)
    vmem = pl.BlockSpec(memory_space=pltpu.VMEM)

    def body(*refs):
        in_refs = refs[:n_in]
        out_ref, vec, buf, send_sems, recv_sems = refs[n_in:]
        x, y, c = _place()
        me = 4 * x + 2 * y + c
        vec[...] = jnp.zeros_like(vec)
        vec[0:1, :] = jnp.sum(in_refs[0][...], axis=0)
        for r in range(5):
            vec[1 + r:2 + r, :] = jnp.sum(in_refs[1 + r][...], axis=0)
        vec[6:7, :] = jnp.sum(in_refs[6][...], axis=0)
        vec[7:8, :] = jnp.sum(in_refs[7][...], axis=0)
        vec[8:16, 0:CONV_W] = jnp.sum(in_refs[8][...], axis=0)
        buf[pl.ds(me, 1)] = vec[...][None]
        copies = []
        for r in range(1, 8):
            fx, fy, fc = (r >> 2) & 1, (r >> 1) & 1, r & 1
            to = (1 - x if fx else x, 1 - y if fy else y, 1 - c if fc else c)
            cp = pltpu.make_async_remote_copy(src_ref=vec, dst_ref=buf.at[me], send_sem=send_sems.at[r - 1],
                                              recv_sem=recv_sems.at[r - 1], device_id=to, device_id_type=MESH)
            cp.start()
            copies.append(cp)
        for cp in copies:
            cp.wait()
        total = buf[0]
        for s in range(1, 8):
            total = total + buf[s]
        out_ref[...] = total
        out_ref[0:1, :] = jnp.broadcast_to(jnp.sum(total[0:1, :], axis=-1, keepdims=True), (1, D_MODEL))

    return pl.pallas_call(
        body, name="small_allreduce",
        in_specs=[vmem] * n_in, out_specs=vmem, out_shape=_sds((SMALL_ROWS, D_MODEL), F32),
        scratch_shapes=[pltpu.VMEM((SMALL_ROWS, D_MODEL), F32), pltpu.VMEM((8, SMALL_ROWS, D_MODEL), F32),
                        pltpu.SemaphoreType.DMA((7,)), pltpu.SemaphoreType.DMA((7,))],
    )(*ins)


def _local_step(x, p, tgt, g, b_gate, w_conv, wf):
    seq = x.shape[0]
    tm = min(seq, 1024)
    th = min(seq, 512)
    tl = min(seq, 2048)
    ni, nh, nl = seq // tm, seq // th, seq // tl
    g_pre_mix, g_post_mix, g_pre_mlp, g_post_mlp, g_ple = g
    w_in_nat, w_ao, w_co, w_o, w_up_nat, w_down, w_pg, w_pp = wf
    D = D_MODEL
    vec = lambda a, blk=0: (a, _bs((1, D), lambda i, j, k: (0, blk)))
    rows_i = lambda a, t, blk=0: (a, _bs((t, D), lambda i, j, k: (i, blk)))
    rows_k = lambda a, t, blk=0: (a, _bs((t, D), lambda i, j, k: (k, blk)))
    part = lambda n: (_sds((n, 1, D), F32), _bs((None, 1, D), lambda i, j, k: (i, 0, 0)))
    full2 = lambda a: (a, _bs(a.shape, lambda i, j, k: (0, 0)))

    normed = lambda xb, gb: (_rms(xb, gb).astype(BF16),) * 2
    keep_a = lambda t: [(_sds((seq, D), BF16), _bs((t, D), lambda i, j, k: (i, 0)))]
    qkv_w, conv_w = 3 * ATTN_W, 3 * CONV_W
    qkv, proj_conv, gates, h1 = _mm(
        "proj_in", "nn", (nh, 1, 1),
        a_ins=[rows_i(x, th), vec(g_pre_mix)], a_fn=normed, b_ins=[full2(w_in_nat)], b_fn=_ident,
        epi_fn=lambda acc: (acc[:, :qkv_w], acc[:, qkv_w:qkv_w + conv_w], acc[:, qkv_w + conv_w:]),
        outs=[(_sds((seq, qkv_w), BF16), _bs((th, qkv_w), lambda i, j, k: (i, 0))),
              (_sds((seq, conv_w), F32), _bs((th, conv_w), lambda i, j, k: (i, 0))),
              (_sds((seq, 2 * D), BF16), _bs((th, 2 * D), lambda i, j, k: (i, 0)))],
        acc_shape=(th, D_IN), a_cache=((th, D), BF16), a_outs=keep_a(th))
    o = _attn_fwd(qkv, seq)
    e = _conv_fwd(proj_conv, w_conv, seq, tm)

    def gate_values(ga, gc, ba, bc):
        return _sig(ga.astype(F32) + ba), _sig(gc.astype(F32) + bc)

    def branch_outputs(ob, eb, wao, wco):
        return _nn(ob, wao).astype(BF16).astype(F32), _nn(eb, wco).astype(BF16).astype(F32)

    def mix_fn(ga, gc, ob, eb, ba, bc, wao, wco):
        sa, sc = gate_values(ga, gc, ba, bc)
        ya, yc = branch_outputs(ob, eb, wao, wco)
        return ((sa * ya + sc * yc).astype(BF16),) * 2

    def post_mix(acc, xb, gb):
        return acc, xb + _rms(acc, gb)

    half_rows = lambda a: (a, _bs((th, a.shape[1]), lambda i, j, k: (i, 0)))
    mix_ins = [rows_i(gates, th, 0), rows_i(gates, th, 1), half_rows(o), half_rows(e), vec(b_gate, 0), vec(b_gate, 1),
               full2(w_ao), full2(w_co)]
    mixed, x1, mixin = _mm(
        "mix_out", "nn", (nh, 1, 1), row_halves=True, a_rowwise=[True] * 4 + [False] * 4,
        a_ins=mix_ins, a_fn=mix_fn, b_ins=[full2(w_o)], b_fn=_ident,
        epi_ins=[rows_i(x, th), vec(g_post_mix)], epi_fn=post_mix,
        outs=[(_sds((seq, D), BF16), _bs((th, D), lambda i, j, k: (i, 0))),
              (_sds((seq, D), F32), _bs((th, D), lambda i, j, k: (i, 0)))],
        acc_shape=(th, D), a_cache=((th, D), BF16), a_outs=keep_a(th))
    up, h2 = _mm("mlp_up", "nn", (nh, 1, 1),
                 a_ins=[rows_i(x1, th), vec(g_pre_mlp)], a_fn=normed,
                 b_ins=[full2(w_up_nat)], b_fn=_ident,
                 outs=[(_sds((seq, D_FF), BF16), _bs((th, D_FF), lambda i, j, k: (i, 0)))],
                 acc_shape=(th, D_FF), a_cache=((th, D), BF16), a_outs=keep_a(th))

    def relu2(ub):
        r = jnp.maximum(ub.astype(F32), 0.0)
        return (r * r).astype(BF16)

    dx2, df, dpre, h3, dpp, loss_p, dg_ple_p, dg_post_mlp_p = _mlp_down_ple_head(
        up, x1, p, tgt, g_ple, g_post_mlp, w_down, w_pg, w_pp, seq, th)

    (dw_pp,) = _mm("dw_ple_proj", "tn", (1, 1, nh),
                   a_ins=[(p, _bs((th, PLE_DIM), lambda i, j, k: (k, 0)))], a_fn=_to_bf16,
                   b_ins=[rows_k(dpp, th)], b_fn=_ident,
                   outs=[(_sds((PLE_DIM, D), F32), _bs((PLE_DIM, D), lambda i, j, k: (0, 0)))],
                   acc_shape=(PLE_DIM, D))
    (dw_pg,) = _mm("dw_ple_gate", "tn", (1, 1, nl),
                   a_ins=[rows_k(h3, tl)], a_fn=_ident, b_ins=[rows_k(dpre, tl)], b_fn=_ident,
                   outs=[(_sds((D, D), F32), _bs((D, D), lambda i, j, k: (0, 0)))], acc_shape=(D, D))

    def dup_fn(acc, ub):
        return (acc * (2.0 * jnp.maximum(ub.astype(F32), 0.0)),)

    (dup,) = _mm("d_mlp_down", "nt", (nh, 1, 1),
                 a_ins=[rows_i(df, th)], a_fn=_ident, b_ins=[full2(w_down)], b_fn=_ident,
                 epi_ins=[(up, _bs((th, D_FF), lambda i, j, k: (i, 0)))], epi_fn=dup_fn,
                 outs=[(_sds((seq, D_FF), BF16), _bs((th, D_FF), lambda i, j, k: (i, 0)))],
                 acc_shape=(th, D_FF))
    (dw_down,) = _mm("dw_mlp_down", "tn", (4, 1, nl),
                     a_ins=[(up, _bs((tl, D), lambda i, j, k: (k, i)))], a_fn=relu2,
                     b_ins=[rows_k(df, tl)], b_fn=_ident,
                     outs=[(_sds((D_FF, D), F32), _bs((D, D), lambda i, j, k: (i, 0)))], acc_shape=(D, D))
    (dw_up,) = _mm("dw_mlp_up", "tn", (1, 4, nl),
                   a_ins=[rows_k(h2, tl)], a_fn=_ident,
                   b_ins=[(dup, _bs((tl, D), lambda i, j, k: (k, j)))], b_fn=_ident,
                   outs=[(_sds((N_CHIPS, D, D), F32), _bs((None, D, D), lambda i, j, k: (j, 0, 0)))],
                   acc_shape=(D, D))

    def mlp_norm_bwd(acc, x1b, dx2b, mixedb, g_mlp, g_mix):
        dxn, dg_mlp = _rms_bwd(x1b, g_mlp, acc)
        dx1b = dx2b + dxn
        dmixedb, dg_mix = _rms_bwd(mixedb.astype(F32), g_mix, dx1b)
        return dx1b, dmixedb, dg_mlp, dg_mix

    dx1, dmixed, dg_pre_mlp_p, dg_post_mix_p = _mm(
        "d_mlp_up", "nt", (nh, 1, 1), row_halves=True,
        a_ins=[(dup, _bs((th, D_FF), lambda i, j, k: (i, 0)))], a_fn=_ident,
        b_ins=[full2(w_up_nat)], b_fn=_ident,
        epi_ins=[rows_i(x1, th), rows_i(dx2, th), rows_i(mixed, th), vec(g_pre_mlp), vec(g_post_mix)],
        epi_fn=mlp_norm_bwd,
        outs=[(_sds((seq, D), F32), _bs((th, D), lambda i, j, k: (i, 0))),
              (_sds((seq, D), BF16), _bs((th, D), lambda i, j, k: (i, 0))), part(nh), part(nh)],
        acc_shape=(th, D))
    (dw_o,) = _mm("dw_mix_out", "tn", (1, 1, nl),
                  a_ins=[rows_k(mixin, tl)], a_fn=_ident, b_ins=[rows_k(dmixed, tl)], b_fn=_ident,
                  outs=[(_sds((D, D), F32), _bs((D, D), lambda i, j, k: (0, 0)))], acc_shape=(D, D))

    def gate_bwd(acc, ga, gc, ob, eb, ba, bc, wao, wco):
        sa, sc = gate_values(ga, gc, ba, bc)
        ya, yc = branch_outputs(ob, eb, wao, wco)
        dga = acc * ya * sa * (1.0 - sa)
        dgc = acc * yc * sc * (1.0 - sc)
        dya, dyc = (acc * sa).astype(BF16), (acc * sc).astype(BF16)
        return (dya, dyc, jnp.concatenate([dga, dgc], axis=1), _nt(dya, wao), _nt(dyc, wco),
                jnp.sum(dga, axis=0, keepdims=True), jnp.sum(dgc, axis=0, keepdims=True))

    dya, dyc, dgate, do, de, dbg_a_p, dbg_c_p = _mm(
        "d_mix_out", "nt", (nh, 1, 1), row_halves=True, epi_rowwise=[True] * 4 + [False] * 4,
        a_ins=[rows_i(dmixed, th)], a_fn=_ident, b_ins=[full2(w_o)], b_fn=_ident,
        epi_ins=mix_ins, epi_fn=gate_bwd,
        outs=[(_sds((seq, D), BF16), _bs((th, D), lambda i, j, k: (i, 0)))] * 2
             + [(_sds((seq, 2 * D), BF16), _bs((th, 2 * D), lambda i, j, k: (i, 0))),
                (_sds((seq, ATTN_W), BF16), _bs((th, ATTN_W), lambda i, j, k: (i, 0))),
                (_sds((seq, CONV_W), F32), _bs((th, CONV_W), lambda i, j, k: (i, 0))), part(nh), part(nh)],
        acc_shape=(th, D))
    (dw_ao,) = _mm("dw_attn_out", "tn", (1, 1, nh),
                   a_ins=[(o, _bs((th, ATTN_W), lambda i, j, k: (k, 0)))], a_fn=_ident,
                   b_ins=[rows_k(dya, th)], b_fn=_ident,
                   outs=[(_sds((ATTN_W, D), F32), _bs((ATTN_W, D), lambda i, j, k: (0, 0)))], acc_shape=(ATTN_W, D))
    dq, dk, dv = _attn_bwd(qkv, do, seq)
    (dw_co,) = _mm("dw_conv_out", "tn", (1, 1, nh),
                   a_ins=[(e, _bs((th, CONV_W), lambda i, j, k: (k, 0)))], a_fn=_ident,
                   b_ins=[rows_k(dyc, th)], b_fn=_ident,
                   outs=[(_sds((CONV_W, D), F32), _bs((CONV_W, D), lambda i, j, k: (0, 0)))], acc_shape=(CONV_W, D))
    dconv, dwc_p = _conv_bwd(proj_conv, de, w_conv, seq, tm)
    qkv_w = 3 * ATTN_W
    join_bf16 = lambda *blocks: jnp.concatenate([b.astype(BF16) for b in blocks], axis=1)
    piece = lambda a, t, rows, blk=0: (a, _bs((t, a.shape[1]), (lambda i, j, k: (k, blk)) if rows == "k"
                                             else (lambda i, j, k: (i, blk))))
    (dw_in_qkv,) = _mm("dw_proj_in_qkv", "tn", (1, 1, ni),
                       a_ins=[rows_k(h1, tm)], a_fn=_ident,
                       b_ins=[piece(dq, tm, "k"), piece(dk, tm, "k"), piece(dv, tm, "k")], b_fn=join_bf16,
                       outs=[(_sds((D, qkv_w), F32), _bs((D, qkv_w), lambda i, j, k: (0, 0)))], acc_shape=(D, qkv_w))
    (dw_in_conv,) = _mm("dw_proj_in_conv", "tn", (1, 1, nl),
                        a_ins=[rows_k(h1, tl)], a_fn=_ident, b_ins=[piece(dconv, tl, "k")], b_fn=_ident,
                        outs=[(_sds((D, 3 * CONV_W), F32), _bs((D, 3 * CONV_W), lambda i, j, k: (0, 0)))],
                        acc_shape=(D, 3 * CONV_W))
    (dw_in_gate,) = _mm("dw_proj_in_gate", "tn", (1, 2, nl),
                        a_ins=[rows_k(h1, tl)], a_fn=_ident,
                        b_ins=[(dgate, _bs((tl, D), lambda i, j, k: (k, j)))], b_fn=_ident,
                        outs=[(_sds((D, 2 * D), F32), _bs((D, D), lambda i, j, k: (0, j)))], acc_shape=(D, D))
    dw_in = jnp.concatenate([dw_in_qkv, dw_in_conv, dw_in_gate], axis=1)

    def in_norm_bwd(acc, xb, dx1b, gb):
        dxn, dg = _rms_bwd(xb, gb, acc)
        return dx1b + dxn, dg

    grad_x, dg_pre_mix_p = _mm("d_proj_in", "nt", (nh, 1, 1), row_halves=True,
                               a_ins=[piece(dq, th, "i"), piece(dk, th, "i"), piece(dv, th, "i"),
                                      piece(dconv, th, "i"), piece(dgate, th, "i")], a_fn=join_bf16,
                               b_ins=[full2(w_in_nat)], b_fn=_ident,
                               epi_ins=[rows_i(x, th), rows_i(dx1, th), vec(g_pre_mix)], epi_fn=in_norm_bwd,
                               outs=[(_sds((seq, D), F32), _bs((th, D), lambda i, j, k: (i, 0))), part(nh)],
                               acc_shape=(th, D))

    chip_major = lambda a: a.reshape(a.shape[0], N_CHIPS, a.shape[1] // N_CHIPS).transpose(1, 0, 2)
    big = [chip_major(dw_in), chip_major(dw_ao), chip_major(dw_co), dw_o.reshape(N_CHIPS, D // N_CHIPS, D), dw_up,
           dw_down.reshape(N_CHIPS, D_FF // N_CHIPS, D), dw_pg.reshape(N_CHIPS, D // N_CHIPS, D), chip_major(dw_pp)]
    small = (loss_p, [dg_pre_mix_p, dg_post_mix_p, dg_pre_mlp_p, dg_post_mlp_p, dg_ple_p], dbg_a_p, dbg_c_p, dwc_p)
    return grad_x, big, small


RS_GROUPS = ((0,), (4,), (5,), (1, 2, 3, 6, 7))


def _reduce_scatter(big):
    pair = [None] * len(big)
    for gi, group in enumerate(RS_GROUPS):
        for w, s in zip(group, _rs_pair_sum(f"rs_pair_sum_{gi}", [big[w] for w in group])):
            pair[w] = s
    return _rs_exchange_join(pair)


def kernel(x, p, g_pre_mix, w_in, b_gate, w_conv, w_attn_out, w_conv_out, w_o, g_post_mix, g_pre_mlp, w_up, w_down, g_post_mlp, g_ple, w_ple_gate, w_ple_proj, loss_target, m_g_pre_mix, m_w_in, m_b_gate, m_w_conv, m_w_attn_out, m_w_conv_out, m_w_o, m_g_post_mix, m_g_pre_mlp, m_w_up, m_w_down, m_g_post_mlp, m_g_ple, m_w_ple_gate, m_w_ple_proj, v_g_pre_mix, v_w_in, v_b_gate, v_w_conv, v_w_attn_out, v_w_conv_out, v_w_o, v_g_post_mix, v_g_pre_mlp, v_w_up, v_w_down, v_g_post_mlp, v_g_ple, v_w_ple_gate, v_w_ple_proj):
    mats = [w_in, w_attn_out, w_conv_out, w_o, w_up, w_down, w_ple_gate, w_ple_proj]
    mats_m = [m_w_in, m_w_attn_out, m_w_conv_out, m_w_o, m_w_up, m_w_down, m_w_ple_gate, m_w_ple_proj]
    mats_v = [v_w_in, v_w_attn_out, v_w_conv_out, v_w_o, v_w_up, v_w_down, v_w_ple_gate, v_w_ple_proj]
    gains = [g_pre_mix, g_post_mix, g_pre_mlp, g_post_mlp, g_ple]
    gains_m = [m_g_pre_mix, m_g_post_mix, m_g_pre_mlp, m_g_post_mlp, m_g_ple]
    gains_v = [v_g_pre_mix, v_g_post_mix, v_g_pre_mlp, v_g_post_mlp, v_g_ple]

    taps = jnp.concatenate([w_conv[0], jnp.zeros((CONV_PAD_ROWS - 3, LANES), F32)], axis=0)
    gathered = _allgather_weights([w[0].astype(BF16) for w in mats] + [taps])
    cols_joined = lambda a: a.transpose(1, 0, 2).reshape(a.shape[1], N_CHIPS * a.shape[2])
    rows_joined = lambda a: a.reshape(N_CHIPS * a.shape[1], a.shape[2])
    col_sharded = (0, 1, 2, 4, 7)
    wf = [cols_joined(gathered[n]) if n in col_sharded else rows_joined(gathered[n]) for n in range(8)]
    w_conv_full = cols_joined(gathered[8])[0:3, :]
    chip = 2 * lax.axis_index("x") + lax.axis_index("y")

    grad_x, big, small = _local_step(x[0], p[0, 0], loss_target[0], gains, b_gate, w_conv_full, wf)

    shard_grads = _reduce_scatter(big)
    red = _small_allreduce(*small)
    loss = red[0, 0]
    grad_gains = [red[1 + r:2 + r, :] for r in range(5)]
    grad_b_gate = jnp.concatenate([red[6:7, :], red[7:8, :]], axis=1)
    grad_w_conv = lax.dynamic_slice(red[8:11, :], (0, chip * LANES), (3, LANES))[None]

    grads_big = [gr.reshape(w.shape) for gr, w in zip(shard_grads, mats)]
    upd_big = [_adamw(f"adamw_{i}", w, gr, m, v) for i, (w, gr, m, v) in enumerate(zip(mats, grads_big, mats_m, mats_v))]
    pack = lambda vs, bg: jnp.concatenate(list(vs) + [bg.reshape(2, D_MODEL), jnp.zeros((1, D_MODEL), F32)], axis=0)
    upd_small = _adamw("adamw_small", pack(gains, b_gate), pack(grad_gains, grad_b_gate),
                       pack(gains_m, m_b_gate), pack(gains_v, v_b_gate))
    upd_conv = _adamw("adamw_conv", w_conv, grad_w_conv, m_w_conv, v_w_conv)

    def small_out(a, which):
        gains_out = [a[r:r + 1, :] for r in range(5)]
        return gains_out, a[5:7, :].reshape(1, 2 * D_MODEL)

    def ordered(g_pre_mix_, big_, b_gate_, conv_, g_rest):
        return [g_pre_mix_, big_[0], b_gate_, conv_, big_[1], big_[2], big_[3], g_rest[0], g_rest[1], big_[4], big_[5],
                g_rest[2], g_rest[3], big_[6], big_[7]]

    outs = [loss, grad_x[None]]
    outs += ordered(grad_gains[0], grads_big, grad_b_gate, grad_w_conv, grad_gains[1:])
    for which in range(3):
        g_out, b_out = small_out(upd_small[which], which)
        outs += ordered(g_out[0], [u[which] for u in upd_big], b_out, upd_conv[which], g_out[1:])
    return tuple(outs)
```

```python
import jax
import jax.numpy as jnp
from jax import lax
from jax.experimental import pallas as pl
from jax.experimental.pallas import tpu as pltpu

F32 = jnp.float32
BF16 = jnp.bfloat16
MESH = pl.DeviceIdType.MESH

D_MODEL = 1024
N_HEADS = 8
HEAD_DIM = 64
ATTN_W = N_HEADS * HEAD_DIM
CONV_W = 512
D_FF = 4096
PLE_DIM = 256
D_IN = 5120
N_CHIPS = 4
EPS = 1e-6
Q_SCALE = HEAD_DIM ** -0.5

ADAM_LR = 0.001
ADAM_B1 = 0.9
ADAM_B2 = 0.999
ADAM_EPS = 1e-08
ADAM_WD = 0.01
ADAM_STEP = 10

V7X_VMEM_BYTES = 64 * 1024 * 1024
VMEM_LIMIT = V7X_VMEM_BYTES - 8 * 1024 * 1024
LANES = 128
ATT_BLK = 256
SMALL_ROWS = 16
CONV_PAD_ROWS = 16


def _cparams(n_grid):
    return pltpu.CompilerParams(dimension_semantics=("arbitrary",) * n_grid, vmem_limit_bytes=VMEM_LIMIT)


def _bs(shape, fn):
    return pl.BlockSpec(shape, fn)


def _rms_stats(xf):
    return lax.rsqrt(jnp.mean(xf * xf, axis=-1, keepdims=True) + EPS)


def _rms(xf, g):
    return xf * _rms_stats(xf) * g


def _rms_bwd(xf, g, dy):
    r = _rms_stats(xf)
    xh = xf * r
    dyg = dy * g
    dx = r * (dyg - xh * jnp.mean(dyg * xh, axis=-1, keepdims=True))
    return dx, jnp.sum(dy * xh, axis=0, keepdims=True)


def _sig(z):
    return 1.0 / (1.0 + jnp.exp(-z))


def _ident(a):
    return a


def _to_bf16(a):
    return a.astype(BF16)


_DIMS = {"nn": (((1,), (0,)), ((), ())), "nt": (((1,), (1,)), ((), ())), "tn": (((0,), (0,)), ((), ()))}


def _mm(name, mode, grid, a_ins, a_fn, b_ins, b_fn, outs, acc_shape, epi_ins=(), epi_fn=None,
        a_cache=None, a_outs=(), epi_a=()):
    nk = grid[2]
    na, nb, ne, no, nao = len(a_ins), len(b_ins), len(epi_ins), len(outs), len(a_outs)
    assert a_cache is None or nk == 1
    assert not a_outs or a_cache is not None
    dims = _DIMS[mode]
    if epi_fn is None:
        epi_fn = lambda acc: (acc,)

    def body(*refs):
        a_refs = refs[:na]
        b_refs = refs[na:na + nb]
        e_refs = refs[na + nb:na + nb + ne]
        o_refs = refs[na + nb + ne:na + nb + ne + no]
        ao_refs = refs[na + nb + ne + no:na + nb + ne + no + nao]
        scratch = list(refs[na + nb + ne + no + nao:])
        acc_ref = scratch.pop(0) if nk > 1 else None
        a_sc = scratch.pop(0) if a_cache is not None else None
        j = pl.program_id(1)
        k = pl.program_id(2)

        def finish(acc):
            res = epi_fn(acc, *[a_refs[t][...] for t in epi_a], *[r[...] for r in e_refs])
            for r, val in zip(o_refs, res):
                r[...] = val.astype(r.dtype)

        if a_sc is not None:
            @pl.when(j == 0)
            def _():
                res = a_fn(*[r[...] for r in a_refs])
                if nao:
                    for r, val in zip(ao_refs, res[1:]):
                        r[...] = val.astype(r.dtype)
                    res = res[0]
                a_sc[...] = res
            a = a_sc[...]
        else:
            a = a_fn(*[r[...] for r in a_refs])
        b = b_fn(*[r[...] for r in b_refs])
        prod = lax.dot_general(a, b, dims, preferred_element_type=F32)
        if nk == 1:
            finish(prod)
        else:
            @pl.when(k == 0)
            def _():
                acc_ref[...] = prod

            @pl.when(k > 0)
            def _():
                acc_ref[...] += prod

            @pl.when(k == nk - 1)
            def _():
                finish(acc_ref[...])

    scratch_shapes = []
    if nk > 1:
        scratch_shapes.append(pltpu.VMEM(acc_shape, F32))
    if a_cache is not None:
        scratch_shapes.append(pltpu.VMEM(*a_cache))
    all_outs = list(outs) + list(a_outs)
    res = pl.pallas_call(
        body, name=name, grid=grid,
        in_specs=[s for _, s in a_ins] + [s for _, s in b_ins] + [s for _, s in epi_ins],
        out_specs=[s for _, s in all_outs],
        out_shape=[o for o, _ in all_outs],
        scratch_shapes=scratch_shapes,
        compiler_params=_cparams(3),
    )(*[a for a, _ in a_ins], *[a for a, _ in b_ins], *[a for a, _ in epi_ins])
    return res


def _sds(shape, dtype):
    return jax.ShapeDtypeStruct(shape, dtype)


def _nt(a, b):
    return lax.dot_general(a, b, _DIMS["nt"], preferred_element_type=F32)


def _tn(a, b):
    return lax.dot_general(a, b, _DIMS["tn"], preferred_element_type=F32)


def _nn(a, b):
    return lax.dot_general(a, b, _DIMS["nn"], preferred_element_type=F32)


HEAD_PARTS = 2


def _mlp_down_ple_head(up, x1, p, tgt, g_ple, g_post_mlp, w_down, w_pg, w_pp, seq, tr):
    nblk = seq // tr
    D = D_MODEL

    def body(up_ref, x1_ref, p_ref, t_ref, gp_ref, gm_ref, wd_ref, wpg_ref, wpp_ref,
             dx2_ref, df_ref, dpre_ref, h3_ref, dpp_ref, loss_ref, dgp_ref, dgm_ref):
        gp, gm, wpg, wpp = gp_ref[...], gm_ref[...], wpg_ref[...], wpp_ref[...]
        halves = [pl.ds(n * (tr // HEAD_PARTS), tr // HEAD_PARTS) for n in range(HEAD_PARTS)]
        w_down = wd_ref[...]
        fb = []
        for r in halves:
            hidden = jnp.maximum(up_ref[r, :].astype(F32), 0.0)
            fb.append(_nn((hidden * hidden).astype(BF16), w_down))
        loss, dgp_sum, dgm_sum = 0.0, 0.0, 0.0
        for s, r in enumerate(halves):
            x2b = x1_ref[r, :] + _rms(fb[s], gm)
            h3 = _rms(x2b, gp).astype(BF16)
            gate = _sig(_nn(h3, wpg))
            pp = _nn(p_ref[r, :].astype(BF16), wpp)
            err = x2b + gate * pp - t_ref[r, :]
            dx3 = err * (1.0 / D)
            dpre = (dx3 * pp * gate * (1.0 - gate)).astype(BF16)
            h3_ref[r, :] = h3
            dpp_ref[r, :] = (dx3 * gate).astype(BF16)
            dpre_ref[r, :] = dpre
            dxn, dgp = _rms_bwd(x2b, gp, _nt(dpre, wpg))
            dx2 = dx3 + dxn
            dx2_ref[r, :] = dx2
            dfb, dgm = _rms_bwd(fb[s], gm, dx2)
            df_ref[r, :] = dfb.astype(BF16)
            loss = loss + jnp.sum(err * err, axis=0, keepdims=True)
            dgp_sum, dgm_sum = dgp_sum + dgp, dgm_sum + dgm
        loss_ref[...] = loss * (0.5 / D)
        dgp_ref[...] = dgp_sum
        dgm_ref[...] = dgm_sum

    rows = _bs((tr, D), lambda i: (i, 0))
    vec = _bs((1, D), lambda i: (0, 0))
    part = _bs((None, 1, D), lambda i: (i, 0, 0))
    return pl.pallas_call(
        body, name="mlp_down_ple_head", grid=(nblk,),
        in_specs=[_bs((tr, D_FF), lambda i: (i, 0)), rows, _bs((tr, PLE_DIM), lambda i: (i, 0)), rows, vec, vec,
                  _bs((D_FF, D), lambda i: (0, 0)), _bs((D, D), lambda i: (0, 0)), _bs((PLE_DIM, D), lambda i: (0, 0))],
        out_specs=[rows] * 5 + [part] * 3,
        out_shape=[_sds((seq, D), F32)] + [_sds((seq, D), BF16)] * 4 + [_sds((nblk, 1, D), F32)] * 3,
        compiler_params=_cparams(1),
    )(up, x1, p, tgt, g_ple, g_post_mlp, w_down, w_pg, w_pp)


def _shift_rows_down(u, prev, n):
    rows = u.shape[0]
    ridx = lax.broadcasted_iota(jnp.int32, u.shape, 0)
    out = pltpu.roll(u, n, 0)
    for r in range(n):
        out = jnp.where(ridx == r, prev[8 - n + r:8 - n + r + 1, :], out)
    del rows
    return out


def _shift_rows_up(u, nxt, n):
    rows = u.shape[0]
    ridx = lax.broadcasted_iota(jnp.int32, u.shape, 0)
    out = pltpu.roll(u, rows - n, 0)
    for r in range(n):
        out = jnp.where(ridx == rows - n + r, nxt[r:r + 1, :], out)
    return out


CONV_COL0 = 0


def _conv_fwd(proj, w_conv, seq, tr):
    hb = tr // 8

    def body(cb_ref, cc_ref, cu_ref, ccp_ref, cup_ref, w_ref, e_ref):
        i = pl.program_id(0)
        u = cc_ref[...] * cu_ref[...]
        up = jnp.where(i > 0, ccp_ref[...] * cup_ref[...], 0.0)
        w = w_ref[...]
        d = w[0:1, :] * _shift_rows_down(u, up, 2) + w[1:2, :] * _shift_rows_down(u, up, 1) + w[2:3, :] * u
        e_ref[...] = (cb_ref[...] * d).astype(BF16)

    prev = lambda c: (lambda i: (jnp.maximum(i * hb - 1, 0), c))
    return pl.pallas_call(
        body, name="conv_fwd", grid=(seq // tr,),
        in_specs=[_bs((tr, CONV_W), lambda i: (i, CONV_COL0)),
                  _bs((tr, CONV_W), lambda i: (i, CONV_COL0 + 1)),
                  _bs((tr, CONV_W), lambda i: (i, CONV_COL0 + 2)),
                  _bs((8, CONV_W), prev(CONV_COL0 + 1)),
                  _bs((8, CONV_W), prev(CONV_COL0 + 2)),
                  _bs((3, CONV_W), lambda i: (0, 0))],
        out_specs=_bs((tr, CONV_W), lambda i: (i, 0)),
        out_shape=_sds((seq, CONV_W), BF16),
        compiler_params=_cparams(1),
    )(proj, proj, proj, proj, proj, w_conv)


def _conv_bwd(proj, de, w_conv, seq, tr):
    hb = tr // 8
    nblk = seq // tr

    def body(cb_ref, cc_ref, cu_ref, ccp_ref, cup_ref, cbn_ref, de_ref, den_ref, w_ref, o_ref, dw_ref):
        i = pl.program_id(0)
        cc, cu, cb = cc_ref[...], cu_ref[...], cb_ref[...]
        u = cc * cu
        up = jnp.where(i > 0, ccp_ref[...] * cup_ref[...], 0.0)
        u1 = _shift_rows_down(u, up, 1)
        u2 = _shift_rows_down(u, up, 2)
        de_ = de_ref[...]
        dd = de_ * cb
        ddn = jnp.where(i < nblk - 1, den_ref[...] * cbn_ref[...], 0.0)
        w = w_ref[...]
        du = w[2:3, :] * dd + w[1:2, :] * _shift_rows_up(dd, ddn, 1) + w[0:1, :] * _shift_rows_up(dd, ddn, 2)
        o_ref[:, 0:CONV_W] = (de_ * (w[0:1, :] * u2 + w[1:2, :] * u1 + w[2:3, :] * u)).astype(BF16)
        o_ref[:, CONV_W:2 * CONV_W] = (du * cu).astype(BF16)
        o_ref[:, 2 * CONV_W:3 * CONV_W] = (du * cc).astype(BF16)
        ridx = lax.broadcasted_iota(jnp.int32, (8, CONV_W), 0)
        dw0 = jnp.sum(dd * u2, axis=0, keepdims=True)
        dw1 = jnp.sum(dd * u1, axis=0, keepdims=True)
        dw2 = jnp.sum(dd * u, axis=0, keepdims=True)
        dw_ref[...] = jnp.where(ridx == 0, dw0, jnp.where(ridx == 1, dw1, jnp.where(ridx == 2, dw2, 0.0)))

    prev = lambda c: (lambda i: (jnp.maximum(i * hb - 1, 0), c))
    nxt = lambda c: (lambda i: (jnp.minimum((i + 1) * hb, seq // 8 - 1), c))
    return pl.pallas_call(
        body, name="conv_bwd", grid=(nblk,),
        in_specs=[_bs((tr, CONV_W), lambda i: (i, CONV_COL0)),
                  _bs((tr, CONV_W), lambda i: (i, CONV_COL0 + 1)),
                  _bs((tr, CONV_W), lambda i: (i, CONV_COL0 + 2)),
                  _bs((8, CONV_W), prev(CONV_COL0 + 1)),
                  _bs((8, CONV_W), prev(CONV_COL0 + 2)),
                  _bs((8, CONV_W), nxt(CONV_COL0)),
                  _bs((tr, CONV_W), lambda i: (i, 0)),
                  _bs((8, CONV_W), nxt(0)),
                  _bs((3, CONV_W), lambda i: (0, 0))],
        out_specs=[_bs((tr, 3 * CONV_W), lambda i: (i, 0)), _bs((None, 8, CONV_W), lambda i: (i, 0, 0))],
        out_shape=[_sds((seq, 3 * CONV_W), BF16), _sds((nblk, 8, CONV_W), F32)],
        compiler_params=_cparams(1),
    )(proj, proj, proj, proj, proj, proj, de, de, w_conv)


def _log_gates(z):
    lse = jnp.log(1.0 + jnp.exp(-jnp.abs(z)))
    log_beta = jnp.minimum(z, 0.0) - lse
    return log_beta, log_beta - z


DEAD_LOG_WEIGHT = -110.0
NO_TILE = -1e30


def _first_live_tile(start, scores, live_sc):
    def alive():
        return jnp.max(jnp.maximum(live_sc[0], live_sc[1])) > DEAD_LOG_WEIGHT

    def step(c):
        for h, z in enumerate(scores(c[0])):
            live_sc[h] = live_sc[h] + jnp.sum(_log_gates(z)[1], axis=-1, keepdims=True)
        return c[0] - 1, alive()

    j_end, _ = lax.while_loop(lambda c: jnp.logical_and(c[0] >= 0, c[1]), step, (start, alive()))
    return j_end + 1


def _attn_fwd(proj, seq):
    blk = ATT_BLK
    nq = seq // blk
    npair = N_HEADS // 2

    def body(q_ref, k_ref, v_ref, o_ref, z0_sc, z1_sc, w0_sc, w1_sc, tot_sc, acc_sc):
        i = pl.program_id(1)
        is_a = lax.broadcasted_iota(jnp.int32, (1, LANES), 1) < HEAD_DIM
        q2 = (q_ref[...] * Q_SCALE).astype(BF16)
        zero = jnp.zeros_like(q2)
        qs = (jnp.where(is_a, q2, zero), jnp.where(is_a, zero, q2))
        row = lax.broadcasted_iota(jnp.int32, (blk, blk), 0)
        col = lax.broadcasted_iota(jnp.int32, (blk, blk), 1)
        tri = (row > col).astype(BF16)
        causal = col < row

        def tile_of(ref, j):
            return ref[pl.ds(pl.multiple_of(j * blk, blk), blk), :].astype(BF16)

        def scores(j):
            k2 = tile_of(k_ref, j)
            return [_nt(qs[h], k2) for h in range(2)]

        has_left = i > 0
        left = jnp.maximum(i - 1, 0)

        g_d = [_log_gates(z) for z in scores(i)]
        g_l = [_log_gates(z) for z in scores(left)]
        keep_d = [jnp.where(causal, g[1], 0.0) for g in g_d]
        suf_d = [_nn(lk.astype(BF16), tri) for lk in keep_d]
        suf_l = [_nn(g[1].astype(BF16), tri) for g in g_l]
        v_d, v_l = tile_of(v_ref, i), tile_of(v_ref, left)
        pv = []
        for h in range(2):
            sum_d = jnp.sum(keep_d[h], axis=-1, keepdims=True)
            w_d = jnp.where(causal, jnp.exp(g_d[h][0] + suf_d[h]), 0.0)
            w_l = jnp.exp(g_l[h][0] + (jnp.where(has_left, sum_d, NO_TILE) + suf_l[h]))
            pv.append(_nn(w_d.astype(BF16), v_d) + _nn(w_l.astype(BF16), v_l))
            tot_sc[h] = sum_d + jnp.sum(g_l[h][1], axis=-1, keepdims=True)
        acc_sc[...] = jnp.where(is_a, pv[0], pv[1])

        z_bufs, w_bufs = (z0_sc, z1_sc), (w0_sc, w1_sc)

        def alive():
            return jnp.max(jnp.maximum(tot_sc[0], tot_sc[1])) > DEAD_LOG_WEIGHT

        def put(ref, vals):
            for h in range(2):
                ref[h] = vals[h]

        def weights(zs):
            gates = [_log_gates(z) for z in zs]
            sums = [_nn(g[1].astype(BF16), tri) for g in gates]
            ws = []
            for h in range(2):
                ws.append(jnp.exp(gates[h][0] + (tot_sc[h] + sums[h])).astype(BF16))
                tot_sc[h] = tot_sc[h] + jnp.sum(gates[h][1], axis=-1, keepdims=True)
            return ws

        def add_values(w_buf, j):
            v2 = tile_of(v_ref, j)
            acc_sc[...] += jnp.where(is_a, _nn(w_buf[0], v2), _nn(w_buf[1], v2))

        def trip(j, s):
            add_values(w_bufs[s], j + 1)
            put(z_bufs[1 - s], scores(jnp.maximum(j - 1, 0)))
            put(w_bufs[1 - s], weights((z_bufs[s][0], z_bufs[s][1])))

        @pl.when(jnp.logical_and(i >= 2, alive()))
        def _():
            put(z0_sc, scores(i - 2))
            w0_sc[...] = jnp.zeros_like(w0_sc)

            def two_trips(c):
                trip(c[0], 0)
                trip(c[0] - 1, 1)
                return c[0] - 2, alive()

            j_next, still = lax.while_loop(lambda c: jnp.logical_and(c[0] >= 1, c[1]), two_trips, (i - 2, i >= 2))
            one_left = jnp.logical_and(j_next == 0, still)

            @pl.when(one_left)
            def _():
                trip(0, 0)
                add_values(w1_sc, 0)

            @pl.when(jnp.logical_not(one_left))
            def _():
                add_values(w0_sc, j_next + 1)

        o_ref[...] = acc_sc[...].astype(BF16)

    return pl.pallas_call(
        body, name="attn_fwd", grid=(npair, nq),
        in_specs=[_bs((blk, LANES), lambda p, i: (i, p)),
                  _bs((seq, LANES), lambda p, i: (0, npair + p)),
                  _bs((seq, LANES), lambda p, i: (0, 2 * npair + p))],
        out_specs=_bs((blk, LANES), lambda p, i: (i, p)),
        out_shape=_sds((seq, ATTN_W), BF16),
        scratch_shapes=[pltpu.VMEM((2, blk, blk), F32), pltpu.VMEM((2, blk, blk), F32),
                        pltpu.VMEM((2, blk, blk), BF16), pltpu.VMEM((2, blk, blk), BF16),
                        pltpu.VMEM((2, blk, 1), F32), pltpu.VMEM((blk, LANES), F32)],
        compiler_params=_cparams(2),
    )(proj, proj, proj)


def _attn_bwd(proj, do, seq):
    blk = ATT_BLK
    nq = seq // blk
    npair = N_HEADS // 2

    def body(q_ref, k_ref, v_ref, do_ref, dq_ref, dk_out, dv_out,
             prod0_sc, prod1_sc, pend0_sc, pend1_sc, tot_sc, live_sc, cum_sc, pre_sc, dq_sc, dk_ref, dv_ref):
        i = pl.program_id(1)

        @pl.when(i == 0)
        def _():
            dk_ref[...] = jnp.zeros_like(dk_ref)
            dv_ref[...] = jnp.zeros_like(dv_ref)

        is_a = lax.broadcasted_iota(jnp.int32, (1, LANES), 1) < HEAD_DIM
        q2 = (q_ref[...] * Q_SCALE).astype(BF16)
        do2 = do_ref[...]
        zero = jnp.zeros_like(q2)
        qs = (jnp.where(is_a, q2, zero), jnp.where(is_a, zero, q2))
        dos = (jnp.where(is_a, do2, zero), jnp.where(is_a, zero, do2))
        row = lax.broadcasted_iota(jnp.int32, (blk, blk), 0)
        col = lax.broadcasted_iota(jnp.int32, (blk, blk), 1)
        tri_after = (row > col).astype(BF16)
        tri_excl = (row < col).astype(BF16)
        causal = col < row

        def tile_of(ref, j):
            return ref[pl.ds(pl.multiple_of(j * blk, blk), blk), :].astype(BF16)

        def scores(j):
            k2 = tile_of(k_ref, j)
            return [_nt(qs[h], k2) for h in range(2)]

        def products(j):
            v2 = tile_of(v_ref, j)
            return scores(j) + [_nt(dos[h], v2) for h in range(2)]

        def row_sum(a):
            return jnp.sum(a, axis=-1, keepdims=True)

        def grad_matmuls(ws, dzs, j):
            rows = pl.ds(pl.multiple_of(j * blk, blk), blk)
            k2 = tile_of(k_ref, j)
            dq_sc[...] += jnp.where(is_a, _nn(dzs[0], k2), _nn(dzs[1], k2))
            dk_ref[rows, :] += jnp.where(is_a, _tn(dzs[0], q2), _tn(dzs[1], q2))
            if ws is not None:
                dv_ref[rows, :] += jnp.where(is_a, _tn(ws[0], do2), _tn(ws[1], do2))

        has_left = i > 0
        left = jnp.maximum(i - 1, 0)

        p_d, p_l = products(i), products(left)
        g_d = [_log_gates(z) for z in p_d[:2]]
        g_l = [_log_gates(z) for z in p_l[:2]]
        keep_d = [jnp.where(causal, g[1], 0.0) for g in g_d]
        suf_d = [_nn(lk.astype(BF16), tri_after) for lk in keep_d]
        suf_l = [_nn(g[1].astype(BF16), tri_after) for g in g_l]
        w_d, w_l, gg_d, gg_l = [], [], [], []
        for h in range(2):
            sum_d = row_sum(keep_d[h])
            w_d.append(jnp.where(causal, jnp.exp(g_d[h][0] + suf_d[h]), 0.0))
            w_l.append(jnp.exp(g_l[h][0] + (jnp.where(has_left, sum_d, NO_TILE) + suf_l[h])))
            gg_d.append(p_d[2 + h] * w_d[h])
            gg_l.append(p_l[2 + h] * w_l[h])
            tot_sc[h] = sum_d + row_sum(g_l[h][1])
        before_d = [_nn(g.astype(BF16), tri_excl) for g in gg_d]
        before_l = [_nn(g.astype(BF16), tri_excl) for g in gg_l]
        dz_d, dz_l = [], []
        for h in range(2):
            beta_d, beta_l = jnp.exp(g_d[h][0]), jnp.exp(g_l[h][0])
            dz_l.append((gg_l[h] * (1.0 - beta_l) - before_l[h] * beta_l).astype(BF16))
            dz = gg_d[h] * (1.0 - beta_d) - (row_sum(gg_l[h]) + before_d[h]) * beta_d
            dz_d.append(jnp.where(causal, dz, 0.0).astype(BF16))
        dq_sc[...] = jnp.zeros_like(dq_sc)
        grad_matmuls([w.astype(BF16) for w in w_l], dz_l, left)
        grad_matmuls([w.astype(BF16) for w in w_d], dz_d, i)

        live_sc[...] = tot_sc[...]
        first = _first_live_tile(i - 2, scores, live_sc)
        trips = i - 1 - first
        prod_bufs, pend_bufs = (prod0_sc, prod1_sc), (pend0_sc, pend1_sc)

        def local_grads(prods):
            zs, dws = prods[:2], prods[2:]
            gates = [_log_gates(z) for z in zs]
            sums = [_nn(g[1].astype(BF16), tri_after) for g in gates]
            ws, gs = [], []
            for h in range(2):
                cum = cum_sc[h] + row_sum(gates[h][1])
                cum_sc[h] = cum
                ws.append(jnp.exp(gates[h][0] + ((live_sc[h] - cum) + sums[h])))
                gs.append(dws[h] * ws[h])
            befores = [_nn(g.astype(BF16), tri_excl) for g in gs]
            dzs = []
            for h in range(2):
                beta = jnp.exp(gates[h][0])
                dzs.append((gs[h] * (1.0 - beta) - (pre_sc[h] + befores[h]) * beta).astype(BF16))
                pre_sc[h] = pre_sc[h] + row_sum(gs[h])
            return [w.astype(BF16) for w in ws] + dzs

        def put(ref, vals):
            for n, val in enumerate(vals):
                ref[n] = val

        def flush(pend, j):
            grad_matmuls([pend[0], pend[1]], [pend[2], pend[3]], j)

        def trip(j, s):
            flush(pend_bufs[s], jnp.maximum(j - 1, first))
            put(prod_bufs[1 - s], products(j + 1))
            put(pend_bufs[1 - s], local_grads([prod_bufs[s][n] for n in range(4)]))

        def earlier_keys_share(j, mask):
            dzs = []
            for h, z in enumerate(scores(j)):
                beta = jnp.exp(_log_gates(z)[0])
                dzs.append(jnp.where(mask, -pre_sc[h] * beta, 0.0).astype(BF16))
            grad_matmuls(None, dzs, j)

        @pl.when(trips > 0)
        def _():
            cum_sc[...] = jnp.zeros_like(cum_sc)
            pre_sc[...] = jnp.zeros_like(pre_sc)
            pend0_sc[...] = jnp.zeros_like(pend0_sc)
            put(prod0_sc, products(first))

            def two_trips(pp, carry):
                trip(first + 2 * pp, 0)
                trip(first + 2 * pp + 1, 1)
                return carry

            lax.fori_loop(0, trips // 2, two_trips, 0)
            odd = trips % 2 == 1

            @pl.when(odd)
            def _():
                trip(i - 2, 0)
                flush(pend1_sc, i - 2)

            @pl.when(jnp.logical_not(odd))
            def _():
                flush(pend0_sc, i - 2)

            earlier_keys_share(i - 1, True)
            earlier_keys_share(i, causal)

        dq_ref[...] = (dq_sc[...] * Q_SCALE).astype(BF16)

        @pl.when(i == nq - 1)
        def _():
            dk_out[...] = dk_ref[...].astype(BF16)
            dv_out[...] = dv_ref[...].astype(BF16)

    qmap = lambda p, i: (i, p)
    return pl.pallas_call(
        body, name="attn_bwd", grid=(npair, nq),
        in_specs=[_bs((blk, LANES), qmap),
                  _bs((seq, LANES), lambda p, i: (0, npair + p)),
                  _bs((seq, LANES), lambda p, i: (0, 2 * npair + p)),
                  _bs((blk, LANES), qmap)],
        out_specs=[_bs((blk, LANES), qmap),
                   _bs((seq, LANES), lambda p, i: (0, p)),
                   _bs((seq, LANES), lambda p, i: (0, p))],
        out_shape=[_sds((seq, ATTN_W), BF16)] * 3,
        scratch_shapes=[pltpu.VMEM((4, blk, blk), F32), pltpu.VMEM((4, blk, blk), F32),
                        pltpu.VMEM((4, blk, blk), BF16), pltpu.VMEM((4, blk, blk), BF16),
                        pltpu.VMEM((2, blk, 1), F32), pltpu.VMEM((2, blk, 1), F32), pltpu.VMEM((2, blk, 1), F32),
                        pltpu.VMEM((2, blk, 1), F32), pltpu.VMEM((blk, LANES), F32),
                        pltpu.VMEM((seq, LANES), F32), pltpu.VMEM((seq, LANES), F32)],
        compiler_params=_cparams(2),
    )(proj, proj, proj, do)


def _elementwise(name, fn, ins, out_dtypes):
    rows, cols = ins[0].shape
    tr = rows
    for cand in (512, 256, 128, 64, 32, 16, 8):
        if rows % cand == 0 and cand * cols * 4 <= 2 * 1024 * 1024:
            tr = cand
            break
    n_in = len(ins)

    def body(*refs):
        res = fn(*[r[...] for r in refs[:n_in]])
        for r, val in zip(refs[n_in:], res):
            r[...] = val.astype(r.dtype)

    spec = _bs((tr, cols), lambda i: (i, 0))
    return pl.pallas_call(
        body, name=name, grid=(rows // tr,),
        in_specs=[spec] * n_in, out_specs=[spec] * len(out_dtypes),
        out_shape=[_sds((rows, cols), dt) for dt in out_dtypes],
        compiler_params=_cparams(1),
    )(*ins)


def _adamw_fn(w, g, m, v):
    m = ADAM_B1 * m + (1.0 - ADAM_B1) * g
    v = ADAM_B2 * v + (1.0 - ADAM_B2) * (g * g)
    m_hat = m / (1.0 - ADAM_B1 ** ADAM_STEP)
    v_hat = v / (1.0 - ADAM_B2 ** ADAM_STEP)
    delta = -ADAM_LR * (m_hat / (jnp.sqrt(v_hat) + ADAM_EPS) + ADAM_WD * w)
    return delta, m, v


def _adamw(name, w, g, m, v):
    shape = w.shape
    as2d = lambda a: a.reshape(-1, shape[-1])
    delta, nm, nv = _elementwise(name, _adamw_fn, [as2d(w), as2d(g), as2d(m), as2d(v)], [F32, F32, F32])
    return delta.reshape(shape), nm.reshape(shape), nv.reshape(shape)


def _place():
    return lax.axis_index("x"), lax.axis_index("y"), lax.axis_index("c")


ANY = pl.BlockSpec(memory_space=pl.ANY)
VMEM_WHOLE = pl.BlockSpec(memory_space=pltpu.VMEM)


def _allgather_weights(shards):
    n = len(shards)

    def body(*refs):
        src, dst = refs[:n], refs[n:2 * n]
        send_sems, recv_sems, local_sems = refs[2 * n:]
        x, y, c = _place()
        me, sibling, mychip = (x, y, c), (x, y, 1 - c), 2 * x + y

        x_nbr, y_nbr, diag = 2 * (1 - x) + y, 2 * x + (1 - y), 2 * (1 - x) + (1 - y)
        to_x, to_y = (1 - x, y, c), (x, 1 - y, c)

        def parts(w):
            hr = src[w].shape[0] // 2
            first = hr // 2 if hr % 32 == 0 else hr
            return first, hr - first

        def rows_of(w, chip, half, route):
            hr = src[w].shape[0] // 2
            first, second = parts(w)
            start, size = {0: (0, hr), 1: (0, hr), 2: (0, first), 3: (first, second)}[route]
            return dst[w].at[chip, pl.ds(half * hr + start, size)]

        def copy(w, k, src_ref, dst_ref, to):
            return pltpu.make_async_remote_copy(src_ref=src_ref, dst_ref=dst_ref, send_sem=send_sems.at[w, k],
                                                recv_sem=recv_sems.at[w, k], device_id=to, device_id_type=MESH)

        def landed(w, route):
            chip = {0: x_nbr, 1: y_nbr, 2: diag, 3: diag}[route]
            return rows_of(w, chip, c, route), chip

        def routes(w):
            return (0, 1, 2, 3) if parts(w)[1] else (0, 1, 2)

        started, local = [], []
        for w in range(n):
            hr = src[w].shape[0] // 2
            own = pltpu.make_async_copy(src[w], dst[w].at[mychip], local_sems.at[w])
            own.start()
            local.append(own)
            mine = src[w].at[pl.ds(c * hr, hr)]
            for route, to in ((0, to_x), (1, to_y)):
                cp = copy(w, route, mine, rows_of(w, mychip, c, route), to)
                cp.start()
                started.append(cp)

        def pass_on(w, route):
            got, chip = landed(w, route)
            copy(w, route, got, got, me).wait_recv()
            if route == 1:
                part = rows_of(w, chip, c, 2)
                started.append(copy(w, 2, part, part, to_x))
                started[-1].start()
            if route == 0 and parts(w)[1]:
                part = rows_of(w, chip, c, 3)
                started.append(copy(w, 3, part, part, to_y))
                started[-1].start()
            started.append(copy(w, 4 + route, got, got, sibling))
            started[-1].start()

        for w in range(n):
            pass_on(w, 1)
            pass_on(w, 0)
        for w in range(n):
            for route in routes(w)[2:]:
                pass_on(w, route)
        for w in range(n):
            for route in routes(w):
                chip = landed(w, route)[1]
                from_sib = rows_of(w, chip, 1 - c, route)
                copy(w, 4 + route, from_sib, from_sib, me).wait_recv()
        for cp in local:
            cp.wait()
        for cp in started:
            cp.wait_send()

    return pl.pallas_call(
        body, name="allgather_weights",
        in_specs=[VMEM_WHOLE] * n, out_specs=[VMEM_WHOLE] * n,
        out_shape=[_sds((N_CHIPS,) + s.shape, s.dtype) for s in shards],
        scratch_shapes=[pltpu.SemaphoreType.DMA((n, 8)), pltpu.SemaphoreType.DMA((n, 8)),
                        pltpu.SemaphoreType.DMA((n,))],
        compiler_params=pltpu.CompilerParams(vmem_limit_bytes=VMEM_LIMIT),
    )(*shards)


SUM_ROWS = 64


def _rs_pair_sum(name, grads):
    n = len(grads)

    def body(*refs):
        g, out = refs[:n], refs[n:2 * n]
        stage, give16, land, keep = (refs[m * n:(m + 1) * n] for m in range(2, 6))
        send_sems, recv_sems, stage_sems, keep_sems = refs[6 * n:]
        x, y, c = _place()
        sibling = (x, y, 1 - c)

        def over_rows(w, fn):
            nb = g[w].shape[1] // 2 // SUM_ROWS

            def step(idx, carry):
                fn(idx // nb, pl.ds(pl.multiple_of((idx % nb) * SUM_ROWS, SUM_ROWS), SUM_ROWS))
                return carry

            lax.fori_loop(0, N_CHIPS * nb, step, 0)

        loads = []
        for w in range(n):
            hr = g[w].shape[1] // 2
            st = pltpu.make_async_copy(g[w].at[:, pl.ds((1 - c) * hr, hr)], stage[w], stage_sems.at[w])
            kp = pltpu.make_async_copy(g[w].at[:, pl.ds(c * hr, hr)], keep[w], keep_sems.at[w])
            st.start()
            kp.start()
            loads.append((st, kp))
        gives = []
        for w in range(n):
            loads[w][0].wait()

            def narrow(k, rows, w=w):
                give16[w][k, rows, :] = stage[w][k, rows, :].astype(BF16)

            over_rows(w, narrow)
            give = pltpu.make_async_remote_copy(src_ref=give16[w], dst_ref=land[w], send_sem=send_sems.at[w],
                                                recv_sem=recv_sems.at[w], device_id=sibling, device_id_type=MESH)
            give.start()
            gives.append(give)
        for w in range(n):
            loads[w][1].wait()
            gives[w].wait_recv()

            def add(k, rows, w=w):
                out[w][k, rows, :] = (keep[w][k, rows, :] + land[w][k, rows, :].astype(F32)).astype(BF16)

            over_rows(w, add)
        for give in gives:
            give.wait_send()

    half = [(N_CHIPS, a.shape[1] // 2, a.shape[2]) for a in grads]
    wide = [pltpu.VMEM(s, F32) for s in half]
    narrow_bufs = [pltpu.VMEM(s, BF16) for s in half]
    sems = pltpu.SemaphoreType.DMA((n,))
    return pl.pallas_call(
        body, name=name,
        in_specs=[ANY] * n, out_specs=[VMEM_WHOLE] * n, out_shape=[_sds(s, BF16) for s in half],
        scratch_shapes=wide + narrow_bufs + narrow_bufs + wide + [sems, sems, sems, sems],
        compiler_params=pltpu.CompilerParams(vmem_limit_bytes=VMEM_LIMIT),
    )(*grads)


def _rs_exchange_join(parts):
    n = len(parts)

    def body(*refs):
        t, full = refs[:n], refs[n:2 * n]
        got_x, got_y, pass_on, got_2 = (refs[m * n:(m + 1) * n] for m in range(2, 6))
        send_sems, recv_sems = refs[6 * n:]
        x, y, c = _place()
        mychip, sibling = 2 * x + y, (x, y, 1 - c)
        x_nbr, y_nbr, diag = 2 * (1 - x) + y, 2 * x + (1 - y), 2 * (1 - x) + (1 - y)
        to_x, to_y = (1 - x, y, c), (x, 1 - y, c)
        sends = []

        def copy(w, k, src_ref, dst_ref, to):
            return pltpu.make_async_remote_copy(src_ref=src_ref, dst_ref=dst_ref, send_sem=send_sems.at[w, k],
                                                recv_sem=recv_sems.at[w, k], device_id=to, device_id_type=MESH)

        def start(cp):
            cp.start()
            sends.append(cp)

        def add_rows(w, count, fn):
            def step(idx, carry):
                fn(pl.ds(pl.multiple_of(idx * SUM_ROWS, SUM_ROWS), SUM_ROWS), pl.multiple_of(idx * SUM_ROWS, SUM_ROWS))
                return carry
            lax.fori_loop(0, count // SUM_ROWS, step, 0)

        f32 = lambda v: v.astype(F32)
        for w in range(n):
            ha = t[w].shape[1] // 2
            part_a, part_b = pl.ds(0, ha), pl.ds(ha, ha)
            start(copy(w, 0, t[w].at[x_nbr, part_a], got_x[w].at[0], to_x))
            start(copy(w, 1, t[w].at[diag, part_a], got_x[w].at[1], to_x))
            start(copy(w, 2, t[w].at[y_nbr, part_b], got_y[w].at[0], to_y))
            start(copy(w, 3, t[w].at[diag, part_b], got_y[w].at[1], to_y))
        for w in range(n):
            hr = t[w].shape[1]
            ha = hr // 2
            for k in (0, 1):
                copy(w, k, got_x[w].at[k], got_x[w].at[k], to_x).wait_recv()

            def sum_a(rows, r, w=w, hr=hr):
                full[w][pl.ds(pl.multiple_of(c * hr + r, SUM_ROWS), SUM_ROWS), :] = \
                    f32(t[w][mychip, rows, :]) + f32(got_x[w][0, rows, :])
                pass_on[w][rows, :] = (f32(t[w][y_nbr, rows, :]) + f32(got_x[w][1, rows, :])).astype(BF16)

            add_rows(w, ha, sum_a)
            start(copy(w, 4, pass_on[w].at[pl.ds(0, ha)], got_2[w].at[pl.ds(0, ha)], to_y))
            for k in (2, 3):
                copy(w, k, got_y[w].at[k - 2], got_y[w].at[k - 2], to_y).wait_recv()

            def sum_b(rows, r, w=w, hr=hr, ha=ha):
                lower = pl.ds(pl.multiple_of(ha + r, SUM_ROWS), SUM_ROWS)
                full[w][pl.ds(pl.multiple_of(c * hr + ha + r, SUM_ROWS), SUM_ROWS), :] = \
                    f32(t[w][mychip, lower, :]) + f32(got_y[w][0, rows, :])
                pass_on[w][lower, :] = (f32(t[w][x_nbr, lower, :]) + f32(got_y[w][1, rows, :])).astype(BF16)

            add_rows(w, ha, sum_b)
            start(copy(w, 5, pass_on[w].at[pl.ds(ha, ha)], got_2[w].at[pl.ds(ha, ha)], to_x))
        for w in range(n):
            hr = t[w].shape[1]
            ha = hr // 2
            copy(w, 4, got_2[w].at[pl.ds(0, ha)], got_2[w].at[pl.ds(0, ha)], to_y).wait_recv()
            copy(w, 5, got_2[w].at[pl.ds(ha, ha)], got_2[w].at[pl.ds(ha, ha)], to_x).wait_recv()

            def finish(rows, r, w=w, hr=hr):
                out_rows = pl.ds(pl.multiple_of(c * hr + r, SUM_ROWS), SUM_ROWS)
                full[w][out_rows, :] = full[w][out_rows, :] + f32(got_2[w][rows, :])

            add_rows(w, hr, finish)
            mine = full[w].at[pl.ds(c * hr, hr)]
            start(copy(w, 6, mine, mine, sibling))
        for w in range(n):
            hr = t[w].shape[1]
            theirs = full[w].at[pl.ds((1 - c) * hr, hr)]
            copy(w, 6, theirs, theirs, sibling).wait_recv()
        for cp in sends:
            cp.wait_send()

    half = lambda a: pltpu.VMEM((2, a.shape[1] // 2, a.shape[2]), a.dtype)
    whole = lambda a: pltpu.VMEM(a.shape[1:], a.dtype)
    return pl.pallas_call(
        body, name="rs_exchange_join",
        in_specs=[VMEM_WHOLE] * n, out_specs=[VMEM_WHOLE] * n,
        out_shape=[_sds((2 * a.shape[1], a.shape[2]), F32) for a in parts],
        scratch_shapes=[half(a) for a in parts] + [half(a) for a in parts] + [whole(a) for a in parts]
        + [whole(a) for a in parts] + [pltpu.SemaphoreType.DMA((n, 7)), pltpu.SemaphoreType.DMA((n, 7))],
        compiler_params=pltpu.CompilerParams(vmem_limit_bytes=VMEM_LIMIT),
    )(*parts)


def _small_allreduce(loss_p, dg_parts, dbg_a, dbg_c, dwc):
    ins = [loss_p] + list(dg_parts) + [dbg_a, dbg_c, dwc]
    n_in = len(ins)
    vmem = pl.BlockSpec(memory_space=pltpu.VMEM)

    def body(*refs):
        in_refs = refs[:n_in]
        out_ref, vec, buf, send_sems, recv_sems = refs[n_in:]
        x, y, c = _place()
        me = 4 * x + 2 * y + c
        vec[...] = jnp.zeros_like(vec)
        vec[0:1, :] = jnp.sum(in_refs[0][...], axis=0)
        for r in range(5):
            vec[1 + r:2 + r, :] = jnp.sum(in_refs[1 + r][...], axis=0)
        vec[6:7, :] = jnp.sum(in_refs[6][...], axis=0)
        vec[7:8, :] = jnp.sum(in_refs[7][...], axis=0)
        vec[8:16, 0:CONV_W] = jnp.sum(in_refs[8][...], axis=0)
        buf[pl.ds(me, 1)] = vec[...][None]
        copies = []
        for r in range(1, 8):
            fx, fy, fc = (r >> 2) & 1, (r >> 1) & 1, r & 1
            to = (1 - x if fx else x, 1 - y if fy else y, 1 - c if fc else c)
            cp = pltpu.make_async_remote_copy(src_ref=vec, dst_ref=buf.at[me], send_sem=send_sems.at[r - 1],
                                              recv_sem=recv_sems.at[r - 1], device_id=to, device_id_type=MESH)
            cp.start()
            copies.append(cp)
        for cp in copies:
            cp.wait()
        total = buf[0]
        for s in range(1, 8):
            total = total + buf[s]
        out_ref[...] = total
        out_ref[0:1, :] = jnp.broadcast_to(jnp.sum(total[0:1, :], axis=-1, keepdims=True), (1, D_MODEL))

    return pl.pallas_call(
        body, name="small_allreduce",
        in_specs=[vmem] * n_in, out_specs=vmem, out_shape=_sds((SMALL_ROWS, D_MODEL), F32),
        scratch_shapes=[pltpu.VMEM((SMALL_ROWS, D_MODEL), F32), pltpu.VMEM((8, SMALL_ROWS, D_MODEL), F32),
                        pltpu.SemaphoreType.DMA((7,)), pltpu.SemaphoreType.DMA((7,))],
    )(*ins)


def _local_step(x, p, tgt, g, b_gate, w_conv, wf):
    seq = x.shape[0]
    tm = min(seq, 1024)
    th = min(seq, 512)
    tl = min(seq, 2048)
    ni, nh, nl = seq // tm, seq // th, seq // tl
    g_pre_mix, g_post_mix, g_pre_mlp, g_post_mlp, g_ple = g
    w_in_nat, w_ao, w_co, w_o, w_up_nat, w_down, w_pg, w_pp = wf
    D = D_MODEL
    vec = lambda a, blk=0: (a, _bs((1, D), lambda i, j, k: (0, blk)))
    rows_i = lambda a, t, blk=0: (a, _bs((t, D), lambda i, j, k: (i, blk)))
    rows_k = lambda a, t, blk=0: (a, _bs((t, D), lambda i, j, k: (k, blk)))
    part = lambda n: (_sds((n, 1, D), F32), _bs((None, 1, D), lambda i, j, k: (i, 0, 0)))
    full2 = lambda a: (a, _bs(a.shape, lambda i, j, k: (0, 0)))

    normed = lambda xb, gb: (_rms(xb, gb).astype(BF16),) * 2
    keep_a = lambda t: [(_sds((seq, D), BF16), _bs((t, D), lambda i, j, k: (i, 0)))]
    qkv_w, conv_w = 3 * ATTN_W, 3 * CONV_W
    qkv, proj_conv, gates, h1 = _mm(
        "proj_in", "nn", (nh, 1, 1),
        a_ins=[rows_i(x, th), vec(g_pre_mix)], a_fn=normed, b_ins=[full2(w_in_nat)], b_fn=_ident,
        epi_fn=lambda acc: (acc[:, :qkv_w], acc[:, qkv_w:qkv_w + conv_w], acc[:, qkv_w + conv_w:]),
        outs=[(_sds((seq, qkv_w), BF16), _bs((th, qkv_w), lambda i, j, k: (i, 0))),
              (_sds((seq, conv_w), F32), _bs((th, conv_w), lambda i, j, k: (i, 0))),
              (_sds((seq, 2 * D), BF16), _bs((th, 2 * D), lambda i, j, k: (i, 0)))],
        acc_shape=(th, D_IN), a_cache=((th, D), BF16), a_outs=keep_a(th))
    o = _attn_fwd(qkv, seq)
    e = _conv_fwd(proj_conv, w_conv, seq, tm)

    def gate_values(ga, gc, ba, bc):
        return _sig(ga.astype(F32) + ba), _sig(gc.astype(F32) + bc)

    def branch_outputs(ob, eb, wao, wco):
        return _nn(ob, wao).astype(BF16).astype(F32), _nn(eb, wco).astype(BF16).astype(F32)

    def mix_fn(ga, gc, ob, eb, ba, bc, wao, wco):
        sa, sc = gate_values(ga, gc, ba, bc)
        ya, yc = branch_outputs(ob, eb, wao, wco)
        return ((sa * ya + sc * yc).astype(BF16),) * 2

    def post_mix(acc, xb, gb):
        return acc, xb + _rms(acc, gb)

    half_rows = lambda a: (a, _bs((th, a.shape[1]), lambda i, j, k: (i, 0)))
    mix_ins = [rows_i(gates, th, 0), rows_i(gates, th, 1), half_rows(o), half_rows(e), vec(b_gate, 0), vec(b_gate, 1),
               full2(w_ao), full2(w_co)]
    mixed, x1, mixin = _mm(
        "mix_out", "nn", (nh, 1, 1),
        a_ins=mix_ins, a_fn=mix_fn, b_ins=[full2(w_o)], b_fn=_ident,
        epi_ins=[rows_i(x, th), vec(g_post_mix)], epi_fn=post_mix,
        outs=[(_sds((seq, D), BF16), _bs((th, D), lambda i, j, k: (i, 0))),
              (_sds((seq, D), F32), _bs((th, D), lambda i, j, k: (i, 0)))],
        acc_shape=(th, D), a_cache=((th, D), BF16), a_outs=keep_a(th))
    up, h2 = _mm("mlp_up", "nn", (nh, 1, 1),
                 a_ins=[rows_i(x1, th), vec(g_pre_mlp)], a_fn=normed,
                 b_ins=[full2(w_up_nat)], b_fn=_ident,
                 outs=[(_sds((seq, D_FF), BF16), _bs((th, D_FF), lambda i, j, k: (i, 0)))],
                 acc_shape=(th, D_FF), a_cache=((th, D), BF16), a_outs=keep_a(th))

    def relu2(ub):
        r = jnp.maximum(ub.astype(F32), 0.0)
        return (r * r).astype(BF16)

    dx2, df, dpre, h3, dpp, loss_p, dg_ple_p, dg_post_mlp_p = _mlp_down_ple_head(
        up, x1, p, tgt, g_ple, g_post_mlp, w_down, w_pg, w_pp, seq, th)

    (dw_pp,) = _mm("dw_ple_proj", "tn", (1, 1, nh),
                   a_ins=[(p, _bs((th, PLE_DIM), lambda i, j, k: (k, 0)))], a_fn=_to_bf16,
                   b_ins=[rows_k(dpp, th)], b_fn=_ident,
                   outs=[(_sds((PLE_DIM, D), F32), _bs((PLE_DIM, D), lambda i, j, k: (0, 0)))],
                   acc_shape=(PLE_DIM, D))
    (dw_pg,) = _mm("dw_ple_gate", "tn", (1, 1, nl),
                   a_ins=[rows_k(h3, tl)], a_fn=_ident, b_ins=[rows_k(dpre, tl)], b_fn=_ident,
                   outs=[(_sds((D, D), F32), _bs((D, D), lambda i, j, k: (0, 0)))], acc_shape=(D, D))

    def dup_fn(acc, ub):
        return (acc * (2.0 * jnp.maximum(ub.astype(F32), 0.0)),)

    (dup,) = _mm("d_mlp_down", "nt", (nh, 1, 1),
                 a_ins=[rows_i(df, th)], a_fn=_ident, b_ins=[full2(w_down)], b_fn=_ident,
                 epi_ins=[(up, _bs((th, D_FF), lambda i, j, k: (i, 0)))], epi_fn=dup_fn,
                 outs=[(_sds((seq, D_FF), BF16), _bs((th, D_FF), lambda i, j, k: (i, 0)))],
                 acc_shape=(th, D_FF))
    (dw_down,) = _mm("dw_mlp_down", "tn", (4, 1, nl),
                     a_ins=[(up, _bs((tl, D), lambda i, j, k: (k, i)))], a_fn=relu2,
                     b_ins=[rows_k(df, tl)], b_fn=_ident,
                     outs=[(_sds((D_FF, D), F32), _bs((D, D), lambda i, j, k: (i, 0)))], acc_shape=(D, D))
    (dw_up,) = _mm("dw_mlp_up", "tn", (1, 4, nl),
                   a_ins=[rows_k(h2, tl)], a_fn=_ident,
                   b_ins=[(dup, _bs((tl, D), lambda i, j, k: (k, j)))], b_fn=_ident,
                   outs=[(_sds((N_CHIPS, D, D), F32), _bs((None, D, D), lambda i, j, k: (j, 0, 0)))],
                   acc_shape=(D, D))

    def mlp_norm_bwd(acc, x1b, dx2b, mixedb, g_mlp, g_mix):
        dxn, dg_mlp = _rms_bwd(x1b, g_mlp, acc)
        dx1b = dx2b + dxn
        dmixedb, dg_mix = _rms_bwd(mixedb.astype(F32), g_mix, dx1b)
        return dx1b, dmixedb, dg_mlp, dg_mix

    dx1, dmixed, dg_pre_mlp_p, dg_post_mix_p = _mm(
        "d_mlp_up", "nt", (nh, 1, 1),
        a_ins=[(dup, _bs((th, D_FF), lambda i, j, k: (i, 0)))], a_fn=_ident,
        b_ins=[full2(w_up_nat)], b_fn=_ident,
        epi_ins=[rows_i(x1, th), rows_i(dx2, th), rows_i(mixed, th), vec(g_pre_mlp), vec(g_post_mix)],
        epi_fn=mlp_norm_bwd,
        outs=[(_sds((seq, D), F32), _bs((th, D), lambda i, j, k: (i, 0))),
              (_sds((seq, D), BF16), _bs((th, D), lambda i, j, k: (i, 0))), part(nh), part(nh)],
        acc_shape=(th, D))
    (dw_o,) = _mm("dw_mix_out", "tn", (1, 1, nl),
                  a_ins=[rows_k(mixin, tl)], a_fn=_ident, b_ins=[rows_k(dmixed, tl)], b_fn=_ident,
                  outs=[(_sds((D, D), F32), _bs((D, D), lambda i, j, k: (0, 0)))], acc_shape=(D, D))

    def gate_bwd(acc, ga, gc, ob, eb, ba, bc, wao, wco):
        sa, sc = gate_values(ga, gc, ba, bc)
        ya, yc = branch_outputs(ob, eb, wao, wco)
        dga = acc * ya * sa * (1.0 - sa)
        dgc = acc * yc * sc * (1.0 - sc)
        dya, dyc = (acc * sa).astype(BF16), (acc * sc).astype(BF16)
        return (dya, dyc, jnp.concatenate([dga, dgc], axis=1), _nt(dya, wao), _nt(dyc, wco),
                jnp.sum(dga, axis=0, keepdims=True), jnp.sum(dgc, axis=0, keepdims=True))

    dya, dyc, dgate, do, de, dbg_a_p, dbg_c_p = _mm(
        "d_mix_out", "nt", (nh, 1, 1),
        a_ins=[rows_i(dmixed, th)], a_fn=_ident, b_ins=[full2(w_o)], b_fn=_ident,
        epi_ins=mix_ins, epi_fn=gate_bwd,
        outs=[(_sds((seq, D), BF16), _bs((th, D), lambda i, j, k: (i, 0)))] * 2
             + [(_sds((seq, 2 * D), BF16), _bs((th, 2 * D), lambda i, j, k: (i, 0))),
                (_sds((seq, ATTN_W), BF16), _bs((th, ATTN_W), lambda i, j, k: (i, 0))),
                (_sds((seq, CONV_W), F32), _bs((th, CONV_W), lambda i, j, k: (i, 0))), part(nh), part(nh)],
        acc_shape=(th, D))
    (dw_ao,) = _mm("dw_attn_out", "tn", (1, 1, nh),
                   a_ins=[(o, _bs((th, ATTN_W), lambda i, j, k: (k, 0)))], a_fn=_ident,
                   b_ins=[rows_k(dya, th)], b_fn=_ident,
                   outs=[(_sds((ATTN_W, D), F32), _bs((ATTN_W, D), lambda i, j, k: (0, 0)))], acc_shape=(ATTN_W, D))
    dq, dk, dv = _attn_bwd(qkv, do, seq)
    (dw_co,) = _mm("dw_conv_out", "tn", (1, 1, nh),
                   a_ins=[(e, _bs((th, CONV_W), lambda i, j, k: (k, 0)))], a_fn=_ident,
                   b_ins=[rows_k(dyc, th)], b_fn=_ident,
                   outs=[(_sds((CONV_W, D), F32), _bs((CONV_W, D), lambda i, j, k: (0, 0)))], acc_shape=(CONV_W, D))
    dconv, dwc_p = _conv_bwd(proj_conv, de, w_conv, seq, tm)
    qkv_w = 3 * ATTN_W
    join_bf16 = lambda *blocks: jnp.concatenate([b.astype(BF16) for b in blocks], axis=1)
    piece = lambda a, t, rows, blk=0: (a, _bs((t, a.shape[1]), (lambda i, j, k: (k, blk)) if rows == "k"
                                             else (lambda i, j, k: (i, blk))))
    (dw_in_qkv,) = _mm("dw_proj_in_qkv", "tn", (1, 1, ni),
                       a_ins=[rows_k(h1, tm)], a_fn=_ident,
                       b_ins=[piece(dq, tm, "k"), piece(dk, tm, "k"), piece(dv, tm, "k")], b_fn=join_bf16,
                       outs=[(_sds((D, qkv_w), F32), _bs((D, qkv_w), lambda i, j, k: (0, 0)))], acc_shape=(D, qkv_w))
    (dw_in_conv,) = _mm("dw_proj_in_conv", "tn", (1, 1, nl),
                        a_ins=[rows_k(h1, tl)], a_fn=_ident, b_ins=[piece(dconv, tl, "k")], b_fn=_ident,
                        outs=[(_sds((D, 3 * CONV_W), F32), _bs((D, 3 * CONV_W), lambda i, j, k: (0, 0)))],
                        acc_shape=(D, 3 * CONV_W))
    (dw_in_gate,) = _mm("dw_proj_in_gate", "tn", (1, 2, nl),
                        a_ins=[rows_k(h1, tl)], a_fn=_ident,
                        b_ins=[(dgate, _bs((tl, D), lambda i, j, k: (k, j)))], b_fn=_ident,
                        outs=[(_sds((D, 2 * D), F32), _bs((D, D), lambda i, j, k: (0, j)))], acc_shape=(D, D))
    dw_in = jnp.concatenate([dw_in_qkv, dw_in_conv, dw_in_gate], axis=1)

    def in_norm_bwd(acc, xb, dx1b, gb):
        dxn, dg = _rms_bwd(xb, gb, acc)
        return dx1b + dxn, dg

    grad_x, dg_pre_mix_p = _mm("d_proj_in", "nt", (nh, 1, 1),
                               a_ins=[piece(dq, th, "i"), piece(dk, th, "i"), piece(dv, th, "i"),
                                      piece(dconv, th, "i"), piece(dgate, th, "i")], a_fn=join_bf16,
                               b_ins=[full2(w_in_nat)], b_fn=_ident,
                               epi_ins=[rows_i(x, th), rows_i(dx1, th), vec(g_pre_mix)], epi_fn=in_norm_bwd,
                               outs=[(_sds((seq, D), F32), _bs((th, D), lambda i, j, k: (i, 0))), part(nh)],
                               acc_shape=(th, D))

    chip_major = lambda a: a.reshape(a.shape[0], N_CHIPS, a.shape[1] // N_CHIPS).transpose(1, 0, 2)
    big = [chip_major(dw_in), chip_major(dw_ao), chip_major(dw_co), dw_o.reshape(N_CHIPS, D // N_CHIPS, D), dw_up,
           dw_down.reshape(N_CHIPS, D_FF // N_CHIPS, D), dw_pg.reshape(N_CHIPS, D // N_CHIPS, D), chip_major(dw_pp)]
    small = (loss_p, [dg_pre_mix_p, dg_post_mix_p, dg_pre_mlp_p, dg_post_mlp_p, dg_ple_p], dbg_a_p, dbg_c_p, dwc_p)
    return grad_x, big, small


RS_GROUPS = ((0,), (4,), (5,), (1, 2, 3, 6, 7))


def _reduce_scatter(big):
    pair = [None] * len(big)
    for gi, group in enumerate(RS_GROUPS):
        for w, s in zip(group, _rs_pair_sum(f"rs_pair_sum_{gi}", [big[w] for w in group])):
            pair[w] = s
    return _rs_exchange_join(pair)


def kernel(x, p, g_pre_mix, w_in, b_gate, w_conv, w_attn_out, w_conv_out, w_o, g_post_mix, g_pre_mlp, w_up, w_down, g_post_mlp, g_ple, w_ple_gate, w_ple_proj, loss_target, m_g_pre_mix, m_w_in, m_b_gate, m_w_conv, m_w_attn_out, m_w_conv_out, m_w_o, m_g_post_mix, m_g_pre_mlp, m_w_up, m_w_down, m_g_post_mlp, m_g_ple, m_w_ple_gate, m_w_ple_proj, v_g_pre_mix, v_w_in, v_b_gate, v_w_conv, v_w_attn_out, v_w_conv_out, v_w_o, v_g_post_mix, v_g_pre_mlp, v_w_up, v_w_down, v_g_post_mlp, v_g_ple, v_w_ple_gate, v_w_ple_proj):
    mats = [w_in, w_attn_out, w_conv_out, w_o, w_up, w_down, w_ple_gate, w_ple_proj]
    mats_m = [m_w_in, m_w_attn_out, m_w_conv_out, m_w_o, m_w_up, m_w_down, m_w_ple_gate, m_w_ple_proj]
    mats_v = [v_w_in, v_w_attn_out, v_w_conv_out, v_w_o, v_w_up, v_w_down, v_w_ple_gate, v_w_ple_proj]
    gains = [g_pre_mix, g_post_mix, g_pre_mlp, g_post_mlp, g_ple]
    gains_m = [m_g_pre_mix, m_g_post_mix, m_g_pre_mlp, m_g_post_mlp, m_g_ple]
    gains_v = [v_g_pre_mix, v_g_post_mix, v_g_pre_mlp, v_g_post_mlp, v_g_ple]

    taps = jnp.concatenate([w_conv[0], jnp.zeros((CONV_PAD_ROWS - 3, LANES), F32)], axis=0)
    gathered = _allgather_weights([w[0].astype(BF16) for w in mats] + [taps])
    cols_joined = lambda a: a.transpose(1, 0, 2).reshape(a.shape[1], N_CHIPS * a.shape[2])
    rows_joined = lambda a: a.reshape(N_CHIPS * a.shape[1], a.shape[2])
    col_sharded = (0, 1, 2, 4, 7)
    wf = [cols_joined(gathered[n]) if n in col_sharded else rows_joined(gathered[n]) for n in range(8)]
    w_conv_full = cols_joined(gathered[8])[0:3, :]
    chip = 2 * lax.axis_index("x") + lax.axis_index("y")

    grad_x, big, small = _local_step(x[0], p[0, 0], loss_target[0], gains, b_gate, w_conv_full, wf)

    shard_grads = _reduce_scatter(big)
    red = _small_allreduce(*small)
    loss = red[0, 0]
    grad_gains = [red[1 + r:2 + r, :] for r in range(5)]
    grad_b_gate = jnp.concatenate([red[6:7, :], red[7:8, :]], axis=1)
    grad_w_conv = lax.dynamic_slice(red[8:11, :], (0, chip * LANES), (3, LANES))[None]

    grads_big = [gr.reshape(w.shape) for gr, w in zip(shard_grads, mats)]
    upd_big = [_adamw(f"adamw_{i}", w, gr, m, v) for i, (w, gr, m, v) in enumerate(zip(mats, grads_big, mats_m, mats_v))]
    pack = lambda vs, bg: jnp.concatenate(list(vs) + [bg.reshape(2, D_MODEL), jnp.zeros((1, D_MODEL), F32)], axis=0)
    upd_small = _adamw("adamw_small", pack(gains, b_gate), pack(grad_gains, grad_b_gate),
                       pack(gains_m, m_b_gate), pack(gains_v, v_b_gate))
    upd_conv = _adamw("adamw_conv", w_conv, grad_w_conv, m_w_conv, v_w_conv)

    def small_out(a, which):
        gains_out = [a[r:r + 1, :] for r in range(5)]
        return gains_out, a[5:7, :].reshape(1, 2 * D_MODEL)

    def ordered(g_pre_mix_, big_, b_gate_, conv_, g_rest):
        return [g_pre_mix_, big_[0], b_gate_, conv_, big_[1], big_[2], big_[3], g_rest[0], g_rest[1], big_[4], big_[5],
                g_rest[2], g_rest[3], big_[6], big_[7]]

    outs = [loss, grad_x[None]]
    outs += ordered(grad_gains[0], grads_big, grad_b_gate, grad_w_conv, grad_gains[1:])
    for which in range(3):
        g_out, b_out = small_out(upd_small[which], which)
        outs += ordered(g_out[0], [u[which] for u in upd_big], b_out, upd_conv[which], g_out[1:])
    return tuple(outs)
```

```python
import jax
import jax.numpy as jnp
from jax import lax
from jax.experimental import pallas as pl
from jax.experimental.pallas import tpu as pltpu

F32 = jnp.float32
BF16 = jnp.bfloat16
MESH = pl.DeviceIdType.MESH

D_MODEL = 1024
N_HEADS = 8
HEAD_DIM = 64
ATTN_W = N_HEADS * HEAD_DIM
CONV_W = 512
D_FF = 4096
PLE_DIM = 256
D_IN = 5120
N_CHIPS = 4
EPS = 1e-6
Q_SCALE = HEAD_DIM ** -0.5

ADAM_LR = 0.001
ADAM_B1 = 0.9
ADAM_B2 = 0.999
ADAM_EPS = 1e-08
ADAM_WD = 0.01
ADAM_STEP = 10

V7X_VMEM_BYTES = 64 * 1024 * 1024
VMEM_LIMIT = V7X_VMEM_BYTES - 8 * 1024 * 1024
LANES = 128
ATT_BLK = 256
SMALL_ROWS = 16
CONV_PAD_ROWS = 16


def _cparams(n_grid):
    return pltpu.CompilerParams(dimension_semantics=("arbitrary",) * n_grid, vmem_limit_bytes=VMEM_LIMIT)


def _bs(shape, fn):
    return pl.BlockSpec(shape, fn)


def _rms_stats(xf):
    return lax.rsqrt(jnp.mean(xf * xf, axis=-1, keepdims=True) + EPS)


def _rms(xf, g):
    return xf * _rms_stats(xf) * g


def _rms_bwd(xf, g, dy):
    r = _rms_stats(xf)
    xh = xf * r
    dyg = dy * g
    dx = r * (dyg - xh * jnp.mean(dyg * xh, axis=-1, keepdims=True))
    return dx, jnp.sum(dy * xh, axis=0, keepdims=True)


def _sig(z):
    return 1.0 / (1.0 + jnp.exp(-z))


def _ident(a):
    return a


def _to_bf16(a):
    return a.astype(BF16)


_DIMS = {"nn": (((1,), (0,)), ((), ())), "nt": (((1,), (1,)), ((), ())), "tn": (((0,), (0,)), ((), ()))}


def _mm(name, mode, grid, a_ins, a_fn, b_ins, b_fn, outs, acc_shape, epi_ins=(), epi_fn=None,
        a_cache=None, a_outs=(), epi_a=()):
    nk = grid[2]
    na, nb, ne, no, nao = len(a_ins), len(b_ins), len(epi_ins), len(outs), len(a_outs)
    assert a_cache is None or nk == 1
    assert not a_outs or a_cache is not None
    dims = _DIMS[mode]
    if epi_fn is None:
        epi_fn = lambda acc: (acc,)

    def body(*refs):
        a_refs = refs[:na]
        b_refs = refs[na:na + nb]
        e_refs = refs[na + nb:na + nb + ne]
        o_refs = refs[na + nb + ne:na + nb + ne + no]
        ao_refs = refs[na + nb + ne + no:na + nb + ne + no + nao]
        scratch = list(refs[na + nb + ne + no + nao:])
        acc_ref = scratch.pop(0) if nk > 1 else None
        a_sc = scratch.pop(0) if a_cache is not None else None
        j = pl.program_id(1)
        k = pl.program_id(2)

        def finish(acc):
            res = epi_fn(acc, *[a_refs[t][...] for t in epi_a], *[r[...] for r in e_refs])
            for r, val in zip(o_refs, res):
                r[...] = val.astype(r.dtype)

        if a_sc is not None:
            @pl.when(j == 0)
            def _():
                res = a_fn(*[r[...] for r in a_refs])
                if nao:
                    for r, val in zip(ao_refs, res[1:]):
                        r[...] = val.astype(r.dtype)
                    res = res[0]
                a_sc[...] = res
            a = a_sc[...]
        else:
            a = a_fn(*[r[...] for r in a_refs])
        b = b_fn(*[r[...] for r in b_refs])
        prod = lax.dot_general(a, b, dims, preferred_element_type=F32)
        if nk == 1:
            finish(prod)
        else:
            @pl.when(k == 0)
            def _():
                acc_ref[...] = prod

            @pl.when(k > 0)
            def _():
                acc_ref[...] += prod

            @pl.when(k == nk - 1)
            def _():
                finish(acc_ref[...])

    scratch_shapes = []
    if nk > 1:
        scratch_shapes.append(pltpu.VMEM(acc_shape, F32))
    if a_cache is not None:
        scratch_shapes.append(pltpu.VMEM(*a_cache))
    all_outs = list(outs) + list(a_outs)
    res = pl.pallas_call(
        body, name=name, grid=grid,
        in_specs=[s for _, s in a_ins] + [s for _, s in b_ins] + [s for _, s in epi_ins],
        out_specs=[s for _, s in all_outs],
        out_shape=[o for o, _ in all_outs],
        scratch_shapes=scratch_shapes,
        compiler_params=_cparams(3),
    )(*[a for a, _ in a_ins], *[a for a, _ in b_ins], *[a for a, _ in epi_ins])
    return res


def _sds(shape, dtype):
    return jax.ShapeDtypeStruct(shape, dtype)


def _nt(a, b):
    return lax.dot_general(a, b, _DIMS["nt"], preferred_element_type=F32)


def _tn(a, b):
    return lax.dot_general(a, b, _DIMS["tn"], preferred_element_type=F32)


def _nn(a, b):
    return lax.dot_general(a, b, _DIMS["nn"], preferred_element_type=F32)


HEAD_PARTS = 2


def _mlp_down_ple_head(up, x1, p, tgt, g_ple, g_post_mlp, w_down, w_pg, w_pp, seq, tr):
    nblk = seq // tr
    D = D_MODEL

    def body(up_ref, x1_ref, p_ref, t_ref, gp_ref, gm_ref, wd_ref, wpg_ref, wpp_ref,
             dx2_ref, df_ref, dpre_ref, h3_ref, dpp_ref, loss_ref, dgp_ref, dgm_ref):
        gp, gm, wpg, wpp = gp_ref[...], gm_ref[...], wpg_ref[...], wpp_ref[...]
        halves = [pl.ds(n * (tr // HEAD_PARTS), tr // HEAD_PARTS) for n in range(HEAD_PARTS)]
        w_down = wd_ref[...]
        fb = []
        for r in halves:
            hidden = jnp.maximum(up_ref[r, :].astype(F32), 0.0)
            fb.append(_nn((hidden * hidden).astype(BF16), w_down))
        loss, dgp_sum, dgm_sum = 0.0, 0.0, 0.0
        for s, r in enumerate(halves):
            x2b = x1_ref[r, :] + _rms(fb[s], gm)
            h3 = _rms(x2b, gp).astype(BF16)
            gate = _sig(_nn(h3, wpg))
            pp = _nn(p_ref[r, :].astype(BF16), wpp)
            err = x2b + gate * pp - t_ref[r, :]
            dx3 = err * (1.0 / D)
            dpre = (dx3 * pp * gate * (1.0 - gate)).astype(BF16)
            h3_ref[r, :] = h3
            dpp_ref[r, :] = (dx3 * gate).astype(BF16)
            dpre_ref[r, :] = dpre
            dxn, dgp = _rms_bwd(x2b, gp, _nt(dpre, wpg))
            dx2 = dx3 + dxn
            dx2_ref[r, :] = dx2
            dfb, dgm = _rms_bwd(fb[s], gm, dx2)
            df_ref[r, :] = dfb.astype(BF16)
            loss = loss + jnp.sum(err * err, axis=0, keepdims=True)
            dgp_sum, dgm_sum = dgp_sum + dgp, dgm_sum + dgm
        loss_ref[...] = loss * (0.5 / D)
        dgp_ref[...] = dgp_sum
        dgm_ref[...] = dgm_sum

    rows = _bs((tr, D), lambda i: (i, 0))
    vec = _bs((1, D), lambda i: (0, 0))
    part = _bs((None, 1, D), lambda i: (i, 0, 0))
    return pl.pallas_call(
        body, name="mlp_down_ple_head", grid=(nblk,),
        in_specs=[_bs((tr, D_FF), lambda i: (i, 0)), rows, _bs((tr, PLE_DIM), lambda i: (i, 0)), rows, vec, vec,
                  _bs((D_FF, D), lambda i: (0, 0)), _bs((D, D), lambda i: (0, 0)), _bs((PLE_DIM, D), lambda i: (0, 0))],
        out_specs=[rows] * 5 + [part] * 3,
        out_shape=[_sds((seq, D), F32)] + [_sds((seq, D), BF16)] * 4 + [_sds((nblk, 1, D), F32)] * 3,
        compiler_params=_cparams(1),
    )(up, x1, p, tgt, g_ple, g_post_mlp, w_down, w_pg, w_pp)


def _shift_rows_down(u, prev, n):
    rows = u.shape[0]
    ridx = lax.broadcasted_iota(jnp.int32, u.shape, 0)
    out = pltpu.roll(u, n, 0)
    for r in range(n):
        out = jnp.where(ridx == r, prev[8 - n + r:8 - n + r + 1, :], out)
    del rows
    return out


def _shift_rows_up(u, nxt, n):
    rows = u.shape[0]
    ridx = lax.broadcasted_iota(jnp.int32, u.shape, 0)
    out = pltpu.roll(u, rows - n, 0)
    for r in range(n):
        out = jnp.where(ridx == rows - n + r, nxt[r:r + 1, :], out)
    return out


CONV_COL0 = 0


def _conv_fwd(proj, w_conv, seq, tr):
    hb = tr // 8

    def body(cb_ref, cc_ref, cu_ref, ccp_ref, cup_ref, w_ref, e_ref):
        i = pl.program_id(0)
        u = cc_ref[...] * cu_ref[...]
        up = jnp.where(i > 0, ccp_ref[...] * cup_ref[...], 0.0)
        w = w_ref[...]
        d = w[0:1, :] * _shift_rows_down(u, up, 2) + w[1:2, :] * _shift_rows_down(u, up, 1) + w[2:3, :] * u
        e_ref[...] = (cb_ref[...] * d).astype(BF16)

    prev = lambda c: (lambda i: (jnp.maximum(i * hb - 1, 0), c))
    return pl.pallas_call(
        body, name="conv_fwd", grid=(seq // tr,),
        in_specs=[_bs((tr, CONV_W), lambda i: (i, CONV_COL0)),
                  _bs((tr, CONV_W), lambda i: (i, CONV_COL0 + 1)),
                  _bs((tr, CONV_W), lambda i: (i, CONV_COL0 + 2)),
                  _bs((8, CONV_W), prev(CONV_COL0 + 1)),
                  _bs((8, CONV_W), prev(CONV_COL0 + 2)),
                  _bs((3, CONV_W), lambda i: (0, 0))],
        out_specs=_bs((tr, CONV_W), lambda i: (i, 0)),
        out_shape=_sds((seq, CONV_W), BF16),
        compiler_params=_cparams(1),
    )(proj, proj, proj, proj, proj, w_conv)


def _conv_bwd(proj, de, w_conv, seq, tr):
    hb = tr // 8
    nblk = seq // tr

    def body(cb_ref, cc_ref, cu_ref, ccp_ref, cup_ref, cbn_ref, de_ref, den_ref, w_ref, o_ref, dw_ref):
        i = pl.program_id(0)
        cc, cu, cb = cc_ref[...], cu_ref[...], cb_ref[...]
        u = cc * cu
        up = jnp.where(i > 0, ccp_ref[...] * cup_ref[...], 0.0)
        u1 = _shift_rows_down(u, up, 1)
        u2 = _shift_rows_down(u, up, 2)
        de_ = de_ref[...]
        dd = de_ * cb
        ddn = jnp.where(i < nblk - 1, den_ref[...] * cbn_ref[...], 0.0)
        w = w_ref[...]
        du = w[2:3, :] * dd + w[1:2, :] * _shift_rows_up(dd, ddn, 1) + w[0:1, :] * _shift_rows_up(dd, ddn, 2)
        o_ref[:, 0:CONV_W] = (de_ * (w[0:1, :] * u2 + w[1:2, :] * u1 + w[2:3, :] * u)).astype(BF16)
        o_ref[:, CONV_W:2 * CONV_W] = (du * cu).astype(BF16)
        o_ref[:, 2 * CONV_W:3 * CONV_W] = (du * cc).astype(BF16)
        ridx = lax.broadcasted_iota(jnp.int32, (8, CONV_W), 0)
        dw0 = jnp.sum(dd * u2, axis=0, keepdims=True)
        dw1 = jnp.sum(dd * u1, axis=0, keepdims=True)
        dw2 = jnp.sum(dd * u, axis=0, keepdims=True)
        dw_ref[...] = jnp.where(ridx == 0, dw0, jnp.where(ridx == 1, dw1, jnp.where(ridx == 2, dw2, 0.0)))

    prev = lambda c: (lambda i: (jnp.maximum(i * hb - 1, 0), c))
    nxt = lambda c: (lambda i: (jnp.minimum((i + 1) * hb, seq // 8 - 1), c))
    return pl.pallas_call(
        body, name="conv_bwd", grid=(nblk,),
        in_specs=[_bs((tr, CONV_W), lambda i: (i, CONV_COL0)),
                  _bs((tr, CONV_W), lambda i: (i, CONV_COL0 + 1)),
                  _bs((tr, CONV_W), lambda i: (i, CONV_COL0 + 2)),
                  _bs((8, CONV_W), prev(CONV_COL0 + 1)),
                  _bs((8, CONV_W), prev(CONV_COL0 + 2)),
                  _bs((8, CONV_W), nxt(CONV_COL0)),
                  _bs((tr, CONV_W), lambda i: (i, 0)),
                  _bs((8, CONV_W), nxt(0)),
                  _bs((3, CONV_W), lambda i: (0, 0))],
        out_specs=[_bs((tr, 3 * CONV_W), lambda i: (i, 0)), _bs((None, 8, CONV_W), lambda i: (i, 0, 0))],
        out_shape=[_sds((seq, 3 * CONV_W), BF16), _sds((nblk, 8, CONV_W), F32)],
        compiler_params=_cparams(1),
    )(proj, proj, proj, proj, proj, proj, de, de, w_conv)


def _log_gates(z):
    lse = jnp.log(1.0 + jnp.exp(-jnp.abs(z)))
    log_beta = jnp.minimum(z, 0.0) - lse
    return log_beta, log_beta - z


DEAD_LOG_WEIGHT = -110.0
NO_TILE = -1e30


def _first_live_tile(start, scores, live_sc):
    def alive():
        return jnp.max(jnp.maximum(live_sc[0], live_sc[1])) > DEAD_LOG_WEIGHT

    def step(c):
        for h, z in enumerate(scores(c[0])):
            live_sc[h] = live_sc[h] + jnp.sum(_log_gates(z)[1], axis=-1, keepdims=True)
        return c[0] - 1, alive()

    j_end, _ = lax.while_loop(lambda c: jnp.logical_and(c[0] >= 0, c[1]), step, (start, alive()))
    return j_end + 1


def _attn_fwd(proj, seq):
    blk = ATT_BLK
    nq = seq // blk
    npair = N_HEADS // 2

    def body(q_ref, k_ref, v_ref, o_ref, z0_sc, z1_sc, w0_sc, w1_sc, tot_sc, acc_sc):
        i = pl.program_id(1)
        is_a = lax.broadcasted_iota(jnp.int32, (1, LANES), 1) < HEAD_DIM
        q2 = (q_ref[...] * Q_SCALE).astype(BF16)
        zero = jnp.zeros_like(q2)
        qs = (jnp.where(is_a, q2, zero), jnp.where(is_a, zero, q2))
        row = lax.broadcasted_iota(jnp.int32, (blk, blk), 0)
        col = lax.broadcasted_iota(jnp.int32, (blk, blk), 1)
        tri = (row > col).astype(BF16)
        causal = col < row

        def tile_of(ref, j):
            return ref[pl.ds(pl.multiple_of(j * blk, blk), blk), :].astype(BF16)

        def scores(j):
            k2 = tile_of(k_ref, j)
            return [_nt(qs[h], k2) for h in range(2)]

        has_left = i > 0
        left = jnp.maximum(i - 1, 0)

        g_d = [_log_gates(z) for z in scores(i)]
        g_l = [_log_gates(z) for z in scores(left)]
        keep_d = [jnp.where(causal, g[1], 0.0) for g in g_d]
        suf_d = [_nn(lk.astype(BF16), tri) for lk in keep_d]
        suf_l = [_nn(g[1].astype(BF16), tri) for g in g_l]
        v_d, v_l = tile_of(v_ref, i), tile_of(v_ref, left)
        pv = []
        for h in range(2):
            sum_d = jnp.sum(keep_d[h], axis=-1, keepdims=True)
            w_d = jnp.where(causal, jnp.exp(g_d[h][0] + suf_d[h]), 0.0)
            w_l = jnp.exp(g_l[h][0] + (jnp.where(has_left, sum_d, NO_TILE) + suf_l[h]))
            pv.append(_nn(w_d.astype(BF16), v_d) + _nn(w_l.astype(BF16), v_l))
            tot_sc[h] = sum_d + jnp.sum(g_l[h][1], axis=-1, keepdims=True)
        acc_sc[...] = jnp.where(is_a, pv[0], pv[1])

        z_bufs, w_bufs = (z0_sc, z1_sc), (w0_sc, w1_sc)

        def alive():
            return jnp.max(jnp.maximum(tot_sc[0], tot_sc[1])) > DEAD_LOG_WEIGHT

        def put(ref, vals):
            for h in range(2):
                ref[h] = vals[h]

        def weights(zs):
            gates = [_log_gates(z) for z in zs]
            sums = [_nn(g[1].astype(BF16), tri) for g in gates]
            ws = []
            for h in range(2):
                ws.append(jnp.exp(gates[h][0] + (tot_sc[h] + sums[h])).astype(BF16))
                tot_sc[h] = tot_sc[h] + jnp.sum(gates[h][1], axis=-1, keepdims=True)
            return ws

        def add_values(w_buf, j):
            v2 = tile_of(v_ref, j)
            acc_sc[...] += jnp.where(is_a, _nn(w_buf[0], v2), _nn(w_buf[1], v2))

        def trip(j, s):
            add_values(w_bufs[s], j + 1)
            put(z_bufs[1 - s], scores(jnp.maximum(j - 1, 0)))
            put(w_bufs[1 - s], weights((z_bufs[s][0], z_bufs[s][1])))

        @pl.when(jnp.logical_and(i >= 2, alive()))
        def _():
            put(z0_sc, scores(i - 2))
            w0_sc[...] = jnp.zeros_like(w0_sc)

            def two_trips(c):
                trip(c[0], 0)
                trip(c[0] - 1, 1)
                return c[0] - 2, alive()

            j_next, still = lax.while_loop(lambda c: jnp.logical_and(c[0] >= 1, c[1]), two_trips, (i - 2, i >= 2))
            one_left = jnp.logical_and(j_next == 0, still)

            @pl.when(one_left)
            def _():
                trip(0, 0)
                add_values(w1_sc, 0)

            @pl.when(jnp.logical_not(one_left))
            def _():
                add_values(w0_sc, j_next + 1)

        o_ref[...] = acc_sc[...].astype(BF16)

    return pl.pallas_call(
        body, name="attn_fwd", grid=(npair, nq),
        in_specs=[_bs((blk, LANES), lambda p, i: (i, p)),
                  _bs((seq, LANES), lambda p, i: (0, npair + p)),
                  _bs((seq, LANES), lambda p, i: (0, 2 * npair + p))],
        out_specs=_bs((blk, LANES), lambda p, i: (i, p)),
        out_shape=_sds((seq, ATTN_W), BF16),
        scratch_shapes=[pltpu.VMEM((2, blk, blk), F32), pltpu.VMEM((2, blk, blk), F32),
                        pltpu.VMEM((2, blk, blk), BF16), pltpu.VMEM((2, blk, blk), BF16),
                        pltpu.VMEM((2, blk, 1), F32), pltpu.VMEM((blk, LANES), F32)],
        compiler_params=_cparams(2),
    )(proj, proj, proj)


def _attn_bwd(proj, do, seq):
    blk = ATT_BLK
    nq = seq // blk
    npair = N_HEADS // 2

    def body(q_ref, k_ref, v_ref, do_ref, dq_ref, dk_out, dv_out,
             prod0_sc, prod1_sc, pend0_sc, pend1_sc, tot_sc, live_sc, cum_sc, pre_sc, dq_sc, dk_ref, dv_ref):
        i = pl.program_id(1)

        @pl.when(i == 0)
        def _():
            dk_ref[...] = jnp.zeros_like(dk_ref)
            dv_ref[...] = jnp.zeros_like(dv_ref)

        is_a = lax.broadcasted_iota(jnp.int32, (1, LANES), 1) < HEAD_DIM
        q2 = (q_ref[...] * Q_SCALE).astype(BF16)
        do2 = do_ref[...]
        zero = jnp.zeros_like(q2)
        qs = (jnp.where(is_a, q2, zero), jnp.where(is_a, zero, q2))
        dos = (jnp.where(is_a, do2, zero), jnp.where(is_a, zero, do2))
        row = lax.broadcasted_iota(jnp.int32, (blk, blk), 0)
        col = lax.broadcasted_iota(jnp.int32, (blk, blk), 1)
        tri_after = (row > col).astype(BF16)
        tri_excl = (row < col).astype(BF16)
        causal = col < row

        def tile_of(ref, j):
            return ref[pl.ds(pl.multiple_of(j * blk, blk), blk), :].astype(BF16)

        def scores(j):
            k2 = tile_of(k_ref, j)
            return [_nt(qs[h], k2) for h in range(2)]

        def products(j):
            v2 = tile_of(v_ref, j)
            return scores(j) + [_nt(dos[h], v2) for h in range(2)]

        def row_sum(a):
            return jnp.sum(a, axis=-1, keepdims=True)

        def grad_matmuls(ws, dzs, j):
            rows = pl.ds(pl.multiple_of(j * blk, blk), blk)
            k2 = tile_of(k_ref, j)
            dq_sc[...] += jnp.where(is_a, _nn(dzs[0], k2), _nn(dzs[1], k2))
            dk_ref[rows, :] += jnp.where(is_a, _tn(dzs[0], q2), _tn(dzs[1], q2))
            if ws is not None:
                dv_ref[rows, :] += jnp.where(is_a, _tn(ws[0], do2), _tn(ws[1], do2))

        has_left = i > 0
        left = jnp.maximum(i - 1, 0)

        p_d, p_l = products(i), products(left)
        g_d = [_log_gates(z) for z in p_d[:2]]
        g_l = [_log_gates(z) for z in p_l[:2]]
        keep_d = [jnp.where(causal, g[1], 0.0) for g in g_d]
        suf_d = [_nn(lk.astype(BF16), tri_after) for lk in keep_d]
        suf_l = [_nn(g[1].astype(BF16), tri_after) for g in g_l]
        w_d, w_l, gg_d, gg_l = [], [], [], []
        for h in range(2):
            sum_d = row_sum(keep_d[h])
            w_d.append(jnp.where(causal, jnp.exp(g_d[h][0] + suf_d[h]), 0.0))
            w_l.append(jnp.exp(g_l[h][0] + (jnp.where(has_left, sum_d, NO_TILE) + suf_l[h])))
            gg_d.append(p_d[2 + h] * w_d[h])
            gg_l.append(p_l[2 + h] * w_l[h])
            tot_sc[h] = sum_d + row_sum(g_l[h][1])
        before_d = [_nn(g.astype(BF16), tri_excl) for g in gg_d]
        before_l = [_nn(g.astype(BF16), tri_excl) for g in gg_l]
        dz_d, dz_l = [], []
        for h in range(2):
            beta_d, beta_l = jnp.exp(g_d[h][0]), jnp.exp(g_l[h][0])
            dz_l.append((gg_l[h] * (1.0 - beta_l) - before_l[h] * beta_l).astype(BF16))
            dz = gg_d[h] * (1.0 - beta_d) - (row_sum(gg_l[h]) + before_d[h]) * beta_d
            dz_d.append(jnp.where(causal, dz, 0.0).astype(BF16))
        dq_sc[...] = jnp.zeros_like(dq_sc)
        grad_matmuls([w.astype(BF16) for w in w_l], dz_l, left)
        grad_matmuls([w.astype(BF16) for w in w_d], dz_d, i)

        live_sc[...] = tot_sc[...]
        first = _first_live_tile(i - 2, scores, live_sc)
        trips = i - 1 - first
        prod_bufs, pend_bufs = (prod0_sc, prod1_sc), (pend0_sc, pend1_sc)

        def local_grads(prods):
            zs, dws = prods[:2], prods[2:]
            gates = [_log_gates(z) for z in zs]
            sums = [_nn(g[1].astype(BF16), tri_after) for g in gates]
            ws, gs = [], []
            for h in range(2):
                cum = cum_sc[h] + row_sum(gates[h][1])
                cum_sc[h] = cum
                ws.append(jnp.exp(gates[h][0] + ((live_sc[h] - cum) + sums[h])))
                gs.append(dws[h] * ws[h])
            befores = [_nn(g.astype(BF16), tri_excl) for g in gs]
            dzs = []
            for h in range(2):
                beta = jnp.exp(gates[h][0])
                dzs.append((gs[h] * (1.0 - beta) - (pre_sc[h] + befores[h]) * beta).astype(BF16))
                pre_sc[h] = pre_sc[h] + row_sum(gs[h])
            return [w.astype(BF16) for w in ws] + dzs

        def put(ref, vals):
            for n, val in enumerate(vals):
                ref[n] = val

        def flush(pend, j):
            grad_matmuls([pend[0], pend[1]], [pend[2], pend[3]], j)

        def trip(j, s):
            flush(pend_bufs[s], jnp.maximum(j - 1, first))
            put(prod_bufs[1 - s], products(j + 1))
            put(pend_bufs[1 - s], local_grads([prod_bufs[s][n] for n in range(4)]))

        def earlier_keys_share(j, mask):
            dzs = []
            for h, z in enumerate(scores(j)):
                beta = jnp.exp(_log_gates(z)[0])
                dzs.append(jnp.where(mask, -pre_sc[h] * beta, 0.0).astype(BF16))
            grad_matmuls(None, dzs, j)

        @pl.when(trips > 0)
        def _():
            cum_sc[...] = jnp.zeros_like(cum_sc)
            pre_sc[...] = jnp.zeros_like(pre_sc)
            pend0_sc[...] = jnp.zeros_like(pend0_sc)
            put(prod0_sc, products(first))

            def two_trips(pp, carry):
                trip(first + 2 * pp, 0)
                trip(first + 2 * pp + 1, 1)
                return carry

            lax.fori_loop(0, trips // 2, two_trips, 0)
            odd = trips % 2 == 1

            @pl.when(odd)
            def _():
                trip(i - 2, 0)
                flush(pend1_sc, i - 2)

            @pl.when(jnp.logical_not(odd))
            def _():
                flush(pend0_sc, i - 2)

            earlier_keys_share(i - 1, True)
            earlier_keys_share(i, causal)

        dq_ref[...] = (dq_sc[...] * Q_SCALE).astype(BF16)

        @pl.when(i == nq - 1)
        def _():
            dk_out[...] = dk_ref[...].astype(BF16)
            dv_out[...] = dv_ref[...].astype(BF16)

    qmap = lambda p, i: (i, p)
    return pl.pallas_call(
        body, name="attn_bwd", grid=(npair, nq),
        in_specs=[_bs((blk, LANES), qmap),
                  _bs((seq, LANES), lambda p, i: (0, npair + p)),
                  _bs((seq, LANES), lambda p, i: (0, 2 * npair + p)),
                  _bs((blk, LANES), qmap)],
        out_specs=[_bs((blk, LANES), qmap),
                   _bs((seq, LANES), lambda p, i: (0, p)),
                   _bs((seq, LANES), lambda p, i: (0, p))],
        out_shape=[_sds((seq, ATTN_W), BF16)] * 3,
        scratch_shapes=[pltpu.VMEM((4, blk, blk), F32), pltpu.VMEM((4, blk, blk), F32),
                        pltpu.VMEM((4, blk, blk), BF16), pltpu.VMEM((4, blk, blk), BF16),
                        pltpu.VMEM((2, blk, 1), F32), pltpu.VMEM((2, blk, 1), F32), pltpu.VMEM((2, blk, 1), F32),
                        pltpu.VMEM((2, blk, 1), F32), pltpu.VMEM((blk, LANES), F32),
                        pltpu.VMEM((seq, LANES), F32), pltpu.VMEM((seq, LANES), F32)],
        compiler_params=_cparams(2),
    )(proj, proj, proj, do)


def _elementwise(name, fn, ins, out_dtypes):
    rows, cols = ins[0].shape
    tr = rows
    for cand in (512, 256, 128, 64, 32, 16, 8):
        if rows % cand == 0 and cand * cols * 4 <= 2 * 1024 * 1024:
            tr = cand
            break
    n_in = len(ins)

    def body(*refs):
        res = fn(*[r[...] for r in refs[:n_in]])
        for r, val in zip(refs[n_in:], res):
            r[...] = val.astype(r.dtype)

    spec = _bs((tr, cols), lambda i: (i, 0))
    return pl.pallas_call(
        body, name=name, grid=(rows // tr,),
        in_specs=[spec] * n_in, out_specs=[spec] * len(out_dtypes),
        out_shape=[_sds((rows, cols), dt) for dt in out_dtypes],
        compiler_params=_cparams(1),
    )(*ins)


def _adamw_fn(w, g, m, v):
    m = ADAM_B1 * m + (1.0 - ADAM_B1) * g
    v = ADAM_B2 * v + (1.0 - ADAM_B2) * (g * g)
    m_hat = m / (1.0 - ADAM_B1 ** ADAM_STEP)
    v_hat = v / (1.0 - ADAM_B2 ** ADAM_STEP)
    delta = -ADAM_LR * (m_hat / (jnp.sqrt(v_hat) + ADAM_EPS) + ADAM_WD * w)
    return delta, m, v


def _adamw(name, w, g, m, v):
    shape = w.shape
    as2d = lambda a: a.reshape(-1, shape[-1])
    delta, nm, nv = _elementwise(name, _adamw_fn, [as2d(w), as2d(g), as2d(m), as2d(v)], [F32, F32, F32])
    return delta.reshape(shape), nm.reshape(shape), nv.reshape(shape)


def _place():
    return lax.axis_index("x"), lax.axis_index("y"), lax.axis_index("c")


ANY = pl.BlockSpec(memory_space=pl.ANY)
VMEM_WHOLE = pl.BlockSpec(memory_space=pltpu.VMEM)


def _allgather_weights(shards):
    n = len(shards)

    def body(*refs):
        src, dst = refs[:n], refs[n:2 * n]
        send_sems, recv_sems, local_sems = refs[2 * n:]
        x, y, c = _place()
        me, sibling, mychip = (x, y, c), (x, y, 1 - c), 2 * x + y

        x_nbr, y_nbr, diag = 2 * (1 - x) + y, 2 * x + (1 - y), 2 * (1 - x) + (1 - y)
        to_x, to_y = (1 - x, y, c), (x, 1 - y, c)

        def parts(w):
            hr = src[w].shape[0] // 2
            first = hr // 2 if hr % 32 == 0 else hr
            return first, hr - first

        def rows_of(w, chip, half, route):
            hr = src[w].shape[0] // 2
            first, second = parts(w)
            start, size = {0: (0, hr), 1: (0, hr), 2: (0, first), 3: (first, second)}[route]
            return dst[w].at[chip, pl.ds(half * hr + start, size)]

        def copy(w, k, src_ref, dst_ref, to):
            return pltpu.make_async_remote_copy(src_ref=src_ref, dst_ref=dst_ref, send_sem=send_sems.at[w, k],
                                                recv_sem=recv_sems.at[w, k], device_id=to, device_id_type=MESH)

        def landed(w, route):
            chip = {0: x_nbr, 1: y_nbr, 2: diag, 3: diag}[route]
            return rows_of(w, chip, c, route), chip

        def routes(w):
            return (0, 1, 2, 3) if parts(w)[1] else (0, 1, 2)

        started, local = [], []
        for w in range(n):
            hr = src[w].shape[0] // 2
            own = pltpu.make_async_copy(src[w], dst[w].at[mychip], local_sems.at[w])
            own.start()
            local.append(own)
            mine = src[w].at[pl.ds(c * hr, hr)]
            for route, to in ((0, to_x), (1, to_y)):
                cp = copy(w, route, mine, rows_of(w, mychip, c, route), to)
                cp.start()
                started.append(cp)

        def pass_on(w, route):
            got, chip = landed(w, route)
            copy(w, route, got, got, me).wait_recv()
            if route == 1:
                part = rows_of(w, chip, c, 2)
                started.append(copy(w, 2, part, part, to_x))
                started[-1].start()
            if route == 0 and parts(w)[1]:
                part = rows_of(w, chip, c, 3)
                started.append(copy(w, 3, part, part, to_y))
                started[-1].start()
            started.append(copy(w, 4 + route, got, got, sibling))
            started[-1].start()

        for w in range(n):
            pass_on(w, 1)
            pass_on(w, 0)
        for w in range(n):
            for route in routes(w)[2:]:
                pass_on(w, route)
        for w in range(n):
            for route in routes(w):
                chip = landed(w, route)[1]
                from_sib = rows_of(w, chip, 1 - c, route)
                copy(w, 4 + route, from_sib, from_sib, me).wait_recv()
        for cp in local:
            cp.wait()
        for cp in started:
            cp.wait_send()

    return pl.pallas_call(
        body, name="allgather_weights",
        in_specs=[VMEM_WHOLE] * n, out_specs=[VMEM_WHOLE] * n,
        out_shape=[_sds((N_CHIPS,) + s.shape, s.dtype) for s in shards],
        scratch_shapes=[pltpu.SemaphoreType.DMA((n, 8)), pltpu.SemaphoreType.DMA((n, 8)),
                        pltpu.SemaphoreType.DMA((n,))],
        compiler_params=pltpu.CompilerParams(vmem_limit_bytes=VMEM_LIMIT),
    )(*shards)


SUM_ROWS = 64


def _rs_pair_sum(name, grads):
    n = len(grads)

    def body(*refs):
        g, out = refs[:n], refs[n:2 * n]
        stage, give16, land, keep = (refs[m * n:(m + 1) * n] for m in range(2, 6))
        send_sems, recv_sems, stage_sems, keep_sems = refs[6 * n:]
        x, y, c = _place()
        sibling = (x, y, 1 - c)

        def over_rows(w, fn):
            nb = g[w].shape[1] // 2 // SUM_ROWS

            def step(idx, carry):
                fn(idx // nb, pl.ds(pl.multiple_of((idx % nb) * SUM_ROWS, SUM_ROWS), SUM_ROWS))
                return carry

            lax.fori_loop(0, N_CHIPS * nb, step, 0)

        loads = []
        for w in range(n):
            hr = g[w].shape[1] // 2
            st = pltpu.make_async_copy(g[w].at[:, pl.ds((1 - c) * hr, hr)], stage[w], stage_sems.at[w])
            kp = pltpu.make_async_copy(g[w].at[:, pl.ds(c * hr, hr)], keep[w], keep_sems.at[w])
            st.start()
            kp.start()
            loads.append((st, kp))
        gives = []
        for w in range(n):
            loads[w][0].wait()

            def narrow(k, rows, w=w):
                give16[w][k, rows, :] = stage[w][k, rows, :].astype(BF16)

            over_rows(w, narrow)
            give = pltpu.make_async_remote_copy(src_ref=give16[w], dst_ref=land[w], send_sem=send_sems.at[w],
                                                recv_sem=recv_sems.at[w], device_id=sibling, device_id_type=MESH)
            give.start()
            gives.append(give)
        for w in range(n):
            loads[w][1].wait()
            gives[w].wait_recv()

            def add(k, rows, w=w):
                out[w][k, rows, :] = (keep[w][k, rows, :] + land[w][k, rows, :].astype(F32)).astype(BF16)

            over_rows(w, add)
        for give in gives:
            give.wait_send()

    half = [(N_CHIPS, a.shape[1] // 2, a.shape[2]) for a in grads]
    wide = [pltpu.VMEM(s, F32) for s in half]
    narrow_bufs = [pltpu.VMEM(s, BF16) for s in half]
    sems = pltpu.SemaphoreType.DMA((n,))
    return pl.pallas_call(
        body, name=name,
        in_specs=[ANY] * n, out_specs=[VMEM_WHOLE] * n, out_shape=[_sds(s, BF16) for s in half],
        scratch_shapes=wide + narrow_bufs + narrow_bufs + wide + [sems, sems, sems, sems],
        compiler_params=pltpu.CompilerParams(vmem_limit_bytes=VMEM_LIMIT),
    )(*grads)


def _rs_exchange_join(parts):
    n = len(parts)

    def body(*refs):
        t, full = refs[:n], refs[n:2 * n]
        got_x, got_y, pass_on, got_2 = (refs[m * n:(m + 1) * n] for m in range(2, 6))
        send_sems, recv_sems = refs[6 * n:]
        x, y, c = _place()
        mychip, sibling = 2 * x + y, (x, y, 1 - c)
        x_nbr, y_nbr, diag = 2 * (1 - x) + y, 2 * x + (1 - y), 2 * (1 - x) + (1 - y)
        to_x, to_y = (1 - x, y, c), (x, 1 - y, c)
        sends = []

        def copy(w, k, src_ref, dst_ref, to):
            return pltpu.make_async_remote_copy(src_ref=src_ref, dst_ref=dst_ref, send_sem=send_sems.at[w, k],
                                                recv_sem=recv_sems.at[w, k], device_id=to, device_id_type=MESH)

        def start(cp):
            cp.start()
            sends.append(cp)

        def add_rows(w, count, fn):
            def step(idx, carry):
                fn(pl.ds(pl.multiple_of(idx * SUM_ROWS, SUM_ROWS), SUM_ROWS), pl.multiple_of(idx * SUM_ROWS, SUM_ROWS))
                return carry
            lax.fori_loop(0, count // SUM_ROWS, step, 0)

        f32 = lambda v: v.astype(F32)
        for w in range(n):
            ha = t[w].shape[1] // 2
            part_a, part_b = pl.ds(0, ha), pl.ds(ha, ha)
            start(copy(w, 0, t[w].at[x_nbr, part_a], got_x[w].at[0], to_x))
            start(copy(w, 1, t[w].at[diag, part_a], got_x[w].at[1], to_x))
            start(copy(w, 2, t[w].at[y_nbr, part_b], got_y[w].at[0], to_y))
            start(copy(w, 3, t[w].at[diag, part_b], got_y[w].at[1], to_y))
        for w in range(n):
            hr = t[w].shape[1]
            ha = hr // 2
            for k in (0, 1):
                copy(w, k, got_x[w].at[k], got_x[w].at[k], to_x).wait_recv()

            def sum_a(rows, r, w=w, hr=hr):
                full[w][pl.ds(pl.multiple_of(c * hr + r, SUM_ROWS), SUM_ROWS), :] = \
                    f32(t[w][mychip, rows, :]) + f32(got_x[w][0, rows, :])
                pass_on[w][rows, :] = (f32(t[w][y_nbr, rows, :]) + f32(got_x[w][1, rows, :])).astype(BF16)

            add_rows(w, ha, sum_a)
            start(copy(w, 4, pass_on[w].at[pl.ds(0, ha)], got_2[w].at[pl.ds(0, ha)], to_y))
            for k in (2, 3):
                copy(w, k, got_y[w].at[k - 2], got_y[w].at[k - 2], to_y).wait_recv()

            def sum_b(rows, r, w=w, hr=hr, ha=ha):
                lower = pl.ds(pl.multiple_of(ha + r, SUM_ROWS), SUM_ROWS)
                full[w][pl.ds(pl.multiple_of(c * hr + ha + r, SUM_ROWS), SUM_ROWS), :] = \
                    f32(t[w][mychip, lower, :]) + f32(got_y[w][0, rows, :])
                pass_on[w][lower, :] = (f32(t[w][x_nbr, lower, :]) + f32(got_y[w][1, rows, :])).astype(BF16)

            add_rows(w, ha, sum_b)
            start(copy(w, 5, pass_on[w].at[pl.ds(ha, ha)], got_2[w].at[pl.ds(ha, ha)], to_x))
        for w in range(n):
            hr = t[w].shape[1]
            ha = hr // 2
            copy(w, 4, got_2[w].at[pl.ds(0, ha)], got_2[w].at[pl.ds(0, ha)], to_y).wait_recv()
            copy(w, 5, got_2[w].at[pl.ds(ha, ha)], got_2[w].at[pl.ds(ha, ha)], to_x).wait_recv()

            def finish(rows, r, w=w, hr=hr):
                out_rows = pl.ds(pl.multiple_of(c * hr + r, SUM_ROWS), SUM_ROWS)
                full[w][out_rows, :] = full[w][out_rows, :] + f32(got_2[w][rows, :])

            add_rows(w, hr, finish)
            mine = full[w].at[pl.ds(c * hr, hr)]
            start(copy(w, 6, mine, mine, sibling))
        for w in range(n):
            hr = t[w].shape[1]
            theirs = full[w].at[pl.ds((1 - c) * hr, hr)]
            copy(w, 6, theirs, theirs, sibling).wait_recv()
        for cp in sends:
            cp.wait_send()

    half = lambda a: pltpu.VMEM((2, a.shape[1] // 2, a.shape[2]), a.dtype)
    whole = lambda a: pltpu.VMEM(a.shape[1:], a.dtype)
    return pl.pallas_call(
        body, name="rs_exchange_join",
        in_specs=[VMEM_WHOLE] * n, out_specs=[VMEM_WHOLE] * n,
        out_shape=[_sds((2 * a.shape[1], a.shape[2]), F32) for a in parts],
        scratch_shapes=[half(a) for a in parts] + [half(a) for a in parts] + [whole(a) for a in parts]
        + [whole(a) for a in parts] + [pltpu.SemaphoreType.DMA((n, 7)), pltpu.SemaphoreType.DMA((n, 7))],
        compiler_params=pltpu.CompilerParams(vmem_limit_bytes=VMEM_LIMIT),
    )(*parts)


def _small_allreduce(loss_p, dg_parts, dbg_a, dbg_c, dwc):
    ins = [loss_p] + list(dg_parts) + [dbg_a, dbg_c, dwc]
    n_in = len(ins)
    vmem = pl.BlockSpec(memory_space=pltpu.VMEM)

    def body(*refs):
        in_refs = refs[:n_in]
        out_ref, vec, buf, send_sems, recv_sems = refs[n_in:]
        x, y, c = _place()
        me = 4 * x + 2 * y + c
        vec[...] = jnp.zeros_like(vec)
        vec[0:1, :] = jnp.sum(in_refs[0][...], axis=0)
        for r in range(5):
            vec[1 + r:2 + r, :] = jnp.sum(in_refs[1 + r][...], axis=0)
        vec[6:7, :] = jnp.sum(in_refs[6][...], axis=0)
        vec[7:8, :] = jnp.sum(in_refs[7][...], axis=0)
        vec[8:16, 0:CONV_W] = jnp.sum(in_refs[8][...], axis=0)
        buf[pl.ds(me, 1)] = vec[...][None]
        copies = []
        for r in range(1, 8):
            fx, fy, fc = (r >> 2) & 1, (r >> 1) & 1, r & 1
            to = (1 - x if fx else x, 1 - y if fy else y, 1 - c if fc else c)
            cp = pltpu.make_async_remote_copy(src_ref=vec, dst_ref=buf.at[me], send_sem=send_sems.at[r - 1],
                                              recv_sem=recv_sems.at[r - 1], device_id=to, device_id_type=MESH)
            cp.start()
            copies.append(cp)
        for cp in copies:
            cp.wait()
        total = buf[0]
        for s in range(1, 8):
            total = total + buf[s]
        out_ref[...] = total
        out_ref[0:1, :] = jnp.broadcast_to(jnp.sum(total[0:1, :], axis=-1, keepdims=True), (1, D_MODEL))

    return pl.pallas_call(
        body, name="small_allreduce",
        in_specs=[vmem] * n_in, out_specs=vmem, out_shape=_sds((SMALL_ROWS, D_MODEL), F32),
        scratch_shapes=[pltpu.VMEM((SMALL_ROWS, D_MODEL), F32), pltpu.VMEM((8, SMALL_ROWS, D_MODEL), F32),
                        pltpu.SemaphoreType.DMA((7,)), pltpu.SemaphoreType.DMA((7,))],
    )(*ins)


def _local_step(x, p, tgt, g, b_gate, w_conv, wf):
    seq = x.shape[0]
    tm = min(seq, 1024)
    th = min(seq, 512)
    tl = min(seq, 2048)
    ni, nh, nl = seq // tm, seq // th, seq // tl
    g_pre_mix, g_post_mix, g_pre_mlp, g_post_mlp, g_ple = g
    w_in_nat, w_ao, w_co, w_o, w_up_nat, w_down, w_pg, w_pp = wf
    D = D_MODEL
    vec = lambda a, blk=0: (a, _bs((1, D), lambda i, j, k: (0, blk)))
    rows_i = lambda a, t, blk=0: (a, _bs((t, D), lambda i, j, k: (i, blk)))
    rows_k = lambda a, t, blk=0: (a, _bs((t, D), lambda i, j, k: (k, blk)))
    part = lambda n: (_sds((n, 1, D), F32), _bs((None, 1, D), lambda i, j, k: (i, 0, 0)))
    full2 = lambda a: (a, _bs(a.shape, lambda i, j, k: (0, 0)))

    normed = lambda xb, gb: (_rms(xb, gb).astype(BF16),) * 2
    keep_a = lambda t: [(_sds((seq, D), BF16), _bs((t, D), lambda i, j, k: (i, 0)))]
    qkv_w, conv_w = 3 * ATTN_W, 3 * CONV_W
    qkv, proj_conv, gates, h1 = _mm(
        "proj_in", "nn", (nh, 1, 1),
        a_ins=[rows_i(x, th), vec(g_pre_mix)], a_fn=normed, b_ins=[full2(w_in_nat)], b_fn=_ident,
        epi_fn=lambda acc: (acc[:, :qkv_w], acc[:, qkv_w:qkv_w + conv_w], acc[:, qkv_w + conv_w:]),
        outs=[(_sds((seq, qkv_w), BF16), _bs((th, qkv_w), lambda i, j, k: (i, 0))),
              (_sds((seq, conv_w), F32), _bs((th, conv_w), lambda i, j, k: (i, 0))),
              (_sds((seq, 2 * D), BF16), _bs((th, 2 * D), lambda i, j, k: (i, 0)))],
        acc_shape=(th, D_IN), a_cache=((th, D), BF16), a_outs=keep_a(th))
    o = _attn_fwd(qkv, seq)
    e = _conv_fwd(proj_conv, w_conv, seq, tm)

    def gate_values(ga, gc, ba, bc):
        return _sig(ga.astype(F32) + ba), _sig(gc.astype(F32) + bc)

    def branch_outputs(ob, eb, wao, wco):
        return _nn(ob, wao).astype(BF16).astype(F32), _nn(eb, wco).astype(BF16).astype(F32)

    def mix_fn(ga, gc, ob, eb, ba, bc, wao, wco):
        sa, sc = gate_values(ga, gc, ba, bc)
        ya, yc = branch_outputs(ob, eb, wao, wco)
        return ((sa * ya + sc * yc).astype(BF16),) * 2

    def post_mix(acc, xb, gb):
        return acc, xb + _rms(acc, gb)

    half_rows = lambda a: (a, _bs((th, a.shape[1]), lambda i, j, k: (i, 0)))
    mix_ins = [rows_i(gates, th, 0), rows_i(gates, th, 1), half_rows(o), half_rows(e), vec(b_gate, 0), vec(b_gate, 1),
               full2(w_ao), full2(w_co)]
    mixed, x1, mixin = _mm(
        "mix_out", "nn", (nh, 1, 1),
        a_ins=mix_ins, a_fn=mix_fn, b_ins=[full2(w_o)], b_fn=_ident,
        epi_ins=[rows_i(x, th), vec(g_post_mix)], epi_fn=post_mix,
        outs=[(_sds((seq, D), BF16), _bs((th, D), lambda i, j, k: (i, 0))),
              (_sds((seq, D), F32), _bs((th, D), lambda i, j, k: (i, 0)))],
        acc_shape=(th, D), a_cache=((th, D), BF16), a_outs=keep_a(th))
    up, h2 = _mm("mlp_up", "nn", (nh, 1, 1),
                 a_ins=[rows_i(x1, th), vec(g_pre_mlp)], a_fn=normed,
                 b_ins=[full2(w_up_nat)], b_fn=_ident,
                 outs=[(_sds((seq, D_FF), BF16), _bs((th, D_FF), lambda i, j, k: (i, 0)))],
                 acc_shape=(th, D_FF), a_cache=((th, D), BF16), a_outs=keep_a(th))

    def relu2(ub):
        r = jnp.maximum(ub.astype(F32), 0.0)
        return (r * r).astype(BF16)

    dx2, df, dpre, h3, dpp, loss_p, dg_ple_p, dg_post_mlp_p = _mlp_down_ple_head(
        up, x1, p, tgt, g_ple, g_post_mlp, w_down, w_pg, w_pp, seq, th)

    (dw_pp,) = _mm("dw_ple_proj", "tn", (1, 1, nh),
                   a_ins=[(p, _bs((th, PLE_DIM), lambda i, j, k: (k, 0)))], a_fn=_to_bf16,
                   b_ins=[rows_k(dpp, th)], b_fn=_ident,
                   outs=[(_sds((PLE_DIM, D), F32), _bs((PLE_DIM, D), lambda i, j, k: (0, 0)))],
                   acc_shape=(PLE_DIM, D))
    (dw_pg,) = _mm("dw_ple_gate", "tn", (1, 1, nl),
                   a_ins=[rows_k(h3, tl)], a_fn=_ident, b_ins=[rows_k(dpre, tl)], b_fn=_ident,
                   outs=[(_sds((D, D), F32), _bs((D, D), lambda i, j, k: (0, 0)))], acc_shape=(D, D))

    def dup_fn(acc, ub):
        return (acc * (2.0 * jnp.maximum(ub.astype(F32), 0.0)),)

    (dup,) = _mm("d_mlp_down", "nt", (nh, 1, 1),
                 a_ins=[rows_i(df, th)], a_fn=_ident, b_ins=[full2(w_down)], b_fn=_ident,
                 epi_ins=[(up, _bs((th, D_FF), lambda i, j, k: (i, 0)))], epi_fn=dup_fn,
                 outs=[(_sds((seq, D_FF), BF16), _bs((th, D_FF), lambda i, j, k: (i, 0)))],
                 acc_shape=(th, D_FF))
    tx = min(seq, 4096)
    (dw_down,) = _mm("dw_mlp_down", "tn", (4, 1, seq // tx),
                     a_ins=[(up, _bs((tx, D), lambda i, j, k: (k, i)))], a_fn=relu2,
                     b_ins=[rows_k(df, tx)], b_fn=_ident,
                     outs=[(_sds((D_FF, D), F32), _bs((D, D), lambda i, j, k: (i, 0)))], acc_shape=(D, D))
    (dw_up,) = _mm("dw_mlp_up", "tn", (1, 4, seq // tx),
                   a_ins=[rows_k(h2, tx)], a_fn=_ident,
                   b_ins=[(dup, _bs((tx, D), lambda i, j, k: (k, j)))], b_fn=_ident,
                   outs=[(_sds((N_CHIPS, D, D), F32), _bs((None, D, D), lambda i, j, k: (j, 0, 0)))],
                   acc_shape=(D, D))

    def mlp_norm_bwd(acc, x1b, dx2b, mixedb, g_mlp, g_mix):
        dxn, dg_mlp = _rms_bwd(x1b, g_mlp, acc)
        dx1b = dx2b + dxn
        dmixedb, dg_mix = _rms_bwd(mixedb.astype(F32), g_mix, dx1b)
        return dx1b, dmixedb, dg_mlp, dg_mix

    dx1, dmixed, dg_pre_mlp_p, dg_post_mix_p = _mm(
        "d_mlp_up", "nt", (nh, 1, 1),
        a_ins=[(dup, _bs((th, D_FF), lambda i, j, k: (i, 0)))], a_fn=_ident,
        b_ins=[full2(w_up_nat)], b_fn=_ident,
        epi_ins=[rows_i(x1, th), rows_i(dx2, th), rows_i(mixed, th), vec(g_pre_mlp), vec(g_post_mix)],
        epi_fn=mlp_norm_bwd,
        outs=[(_sds((seq, D), F32), _bs((th, D), lambda i, j, k: (i, 0))),
              (_sds((seq, D), BF16), _bs((th, D), lambda i, j, k: (i, 0))), part(nh), part(nh)],
        acc_shape=(th, D))
    (dw_o,) = _mm("dw_mix_out", "tn", (1, 1, nl),
                  a_ins=[rows_k(mixin, tl)], a_fn=_ident, b_ins=[rows_k(dmixed, tl)], b_fn=_ident,
                  outs=[(_sds((D, D), F32), _bs((D, D), lambda i, j, k: (0, 0)))], acc_shape=(D, D))

    def gate_bwd(acc, ga, gc, ob, eb, ba, bc, wao, wco):
        sa, sc = gate_values(ga, gc, ba, bc)
        ya, yc = branch_outputs(ob, eb, wao, wco)
        dga = acc * ya * sa * (1.0 - sa)
        dgc = acc * yc * sc * (1.0 - sc)
        dya, dyc = (acc * sa).astype(BF16), (acc * sc).astype(BF16)
        return (dya, dyc, jnp.concatenate([dga, dgc], axis=1), _nt(dya, wao), _nt(dyc, wco),
                jnp.sum(dga, axis=0, keepdims=True), jnp.sum(dgc, axis=0, keepdims=True))

    dya, dyc, dgate, do, de, dbg_a_p, dbg_c_p = _mm(
        "d_mix_out", "nt", (nh, 1, 1),
        a_ins=[rows_i(dmixed, th)], a_fn=_ident, b_ins=[full2(w_o)], b_fn=_ident,
        epi_ins=mix_ins, epi_fn=gate_bwd,
        outs=[(_sds((seq, D), BF16), _bs((th, D), lambda i, j, k: (i, 0)))] * 2
             + [(_sds((seq, 2 * D), BF16), _bs((th, 2 * D), lambda i, j, k: (i, 0))),
                (_sds((seq, ATTN_W), BF16), _bs((th, ATTN_W), lambda i, j, k: (i, 0))),
                (_sds((seq, CONV_W), F32), _bs((th, CONV_W), lambda i, j, k: (i, 0))), part(nh), part(nh)],
        acc_shape=(th, D))
    (dw_ao,) = _mm("dw_attn_out", "tn", (1, 1, nh),
                   a_ins=[(o, _bs((th, ATTN_W), lambda i, j, k: (k, 0)))], a_fn=_ident,
                   b_ins=[rows_k(dya, th)], b_fn=_ident,
                   outs=[(_sds((ATTN_W, D), F32), _bs((ATTN_W, D), lambda i, j, k: (0, 0)))], acc_shape=(ATTN_W, D))
    dq, dk, dv = _attn_bwd(qkv, do, seq)
    (dw_co,) = _mm("dw_conv_out", "tn", (1, 1, nh),
                   a_ins=[(e, _bs((th, CONV_W), lambda i, j, k: (k, 0)))], a_fn=_ident,
                   b_ins=[rows_k(dyc, th)], b_fn=_ident,
                   outs=[(_sds((CONV_W, D), F32), _bs((CONV_W, D), lambda i, j, k: (0, 0)))], acc_shape=(CONV_W, D))
    dconv, dwc_p = _conv_bwd(proj_conv, de, w_conv, seq, tm)
    qkv_w = 3 * ATTN_W
    join_bf16 = lambda *blocks: jnp.concatenate([b.astype(BF16) for b in blocks], axis=1)
    piece = lambda a, t, rows, blk=0: (a, _bs((t, a.shape[1]), (lambda i, j, k: (k, blk)) if rows == "k"
                                             else (lambda i, j, k: (i, blk))))
    (dw_in_qkv,) = _mm("dw_proj_in_qkv", "tn", (1, 1, ni),
                       a_ins=[rows_k(h1, tm)], a_fn=_ident,
                       b_ins=[piece(dq, tm, "k"), piece(dk, tm, "k"), piece(dv, tm, "k")], b_fn=join_bf16,
                       outs=[(_sds((D, qkv_w), F32), _bs((D, qkv_w), lambda i, j, k: (0, 0)))], acc_shape=(D, qkv_w))
    (dw_in_conv,) = _mm("dw_proj_in_conv", "tn", (1, 1, nl),
                        a_ins=[rows_k(h1, tl)], a_fn=_ident, b_ins=[piece(dconv, tl, "k")], b_fn=_ident,
                        outs=[(_sds((D, 3 * CONV_W), F32), _bs((D, 3 * CONV_W), lambda i, j, k: (0, 0)))],
                        acc_shape=(D, 3 * CONV_W))
    (dw_in_gate,) = _mm("dw_proj_in_gate", "tn", (1, 2, nl),
                        a_ins=[rows_k(h1, tl)], a_fn=_ident,
                        b_ins=[(dgate, _bs((tl, D), lambda i, j, k: (k, j)))], b_fn=_ident,
                        outs=[(_sds((D, 2 * D), F32), _bs((D, D), lambda i, j, k: (0, j)))], acc_shape=(D, D))
    dw_in = jnp.concatenate([dw_in_qkv, dw_in_conv, dw_in_gate], axis=1)

    def in_norm_bwd(acc, xb, dx1b, gb):
        dxn, dg = _rms_bwd(xb, gb, acc)
        return dx1b + dxn, dg

    grad_x, dg_pre_mix_p = _mm("d_proj_in", "nt", (nh, 1, 1),
                               a_ins=[piece(dq, th, "i"), piece(dk, th, "i"), piece(dv, th, "i"),
                                      piece(dconv, th, "i"), piece(dgate, th, "i")], a_fn=join_bf16,
                               b_ins=[full2(w_in_nat)], b_fn=_ident,
                               epi_ins=[rows_i(x, th), rows_i(dx1, th), vec(g_pre_mix)], epi_fn=in_norm_bwd,
                               outs=[(_sds((seq, D), F32), _bs((th, D), lambda i, j, k: (i, 0))), part(nh)],
                               acc_shape=(th, D))

    chip_major = lambda a: a.reshape(a.shape[0], N_CHIPS, a.shape[1] // N_CHIPS).transpose(1, 0, 2)
    big = [chip_major(dw_in), chip_major(dw_ao), chip_major(dw_co), dw_o.reshape(N_CHIPS, D // N_CHIPS, D), dw_up,
           dw_down.reshape(N_CHIPS, D_FF // N_CHIPS, D), dw_pg.reshape(N_CHIPS, D // N_CHIPS, D), chip_major(dw_pp)]
    small = (loss_p, [dg_pre_mix_p, dg_post_mix_p, dg_pre_mlp_p, dg_post_mlp_p, dg_ple_p], dbg_a_p, dbg_c_p, dwc_p)
    return grad_x, big, small


RS_GROUPS = ((0,), (4,), (5,), (1, 2, 3, 6, 7))


def _reduce_scatter(big):
    pair = [None] * len(big)
    for gi, group in enumerate(RS_GROUPS):
        for w, s in zip(group, _rs_pair_sum(f"rs_pair_sum_{gi}", [big[w] for w in group])):
            pair[w] = s
    return _rs_exchange_join(pair)


def kernel(x, p, g_pre_mix, w_in, b_gate, w_conv, w_attn_out, w_conv_out, w_o, g_post_mix, g_pre_mlp, w_up, w_down, g_post_mlp, g_ple, w_ple_gate, w_ple_proj, loss_target, m_g_pre_mix, m_w_in, m_b_gate, m_w_conv, m_w_attn_out, m_w_conv_out, m_w_o, m_g_post_mix, m_g_pre_mlp, m_w_up, m_w_down, m_g_post_mlp, m_g_ple, m_w_ple_gate, m_w_ple_proj, v_g_pre_mix, v_w_in, v_b_gate, v_w_conv, v_w_attn_out, v_w_conv_out, v_w_o, v_g_post_mix, v_g_pre_mlp, v_w_up, v_w_down, v_g_post_mlp, v_g_ple, v_w_ple_gate, v_w_ple_proj):
    mats = [w_in, w_attn_out, w_conv_out, w_o, w_up, w_down, w_ple_gate, w_ple_proj]
    mats_m = [m_w_in, m_w_attn_out, m_w_conv_out, m_w_o, m_w_up, m_w_down, m_w_ple_gate, m_w_ple_proj]
    mats_v = [v_w_in, v_w_attn_out, v_w_conv_out, v_w_o, v_w_up, v_w_down, v_w_ple_gate, v_w_ple_proj]
    gains = [g_pre_mix, g_post_mix, g_pre_mlp, g_post_mlp, g_ple]
    gains_m = [m_g_pre_mix, m_g_post_mix, m_g_pre_mlp, m_g_post_mlp, m_g_ple]
    gains_v = [v_g_pre_mix, v_g_post_mix, v_g_pre_mlp, v_g_post_mlp, v_g_ple]

    taps = jnp.concatenate([w_conv[0], jnp.zeros((CONV_PAD_ROWS - 3, LANES), F32)], axis=0)
    gathered = _allgather_weights([w[0].astype(BF16) for w in mats] + [taps])
    cols_joined = lambda a: a.transpose(1, 0, 2).reshape(a.shape[1], N_CHIPS * a.shape[2])
    rows_joined = lambda a: a.reshape(N_CHIPS * a.shape[1], a.shape[2])
    col_sharded = (0, 1, 2, 4, 7)
    wf = [cols_joined(gathered[n]) if n in col_sharded else rows_joined(gathered[n]) for n in range(8)]
    w_conv_full = cols_joined(gathered[8])[0:3, :]
    chip = 2 * lax.axis_index("x") + lax.axis_index("y")

    grad_x, big, small = _local_step(x[0], p[0, 0], loss_target[0], gains, b_gate, w_conv_full, wf)

    shard_grads = _reduce_scatter(big)
    red = _small_allreduce(*small)
    loss = red[0, 0]
    grad_gains = [red[1 + r:2 + r, :] for r in range(5)]
    grad_b_gate = jnp.concatenate([red[6:7, :], red[7:8, :]], axis=1)
    grad_w_conv = lax.dynamic_slice(red[8:11, :], (0, chip * LANES), (3, LANES))[None]

    grads_big = [gr.reshape(w.shape) for gr, w in zip(shard_grads, mats)]
    upd_big = [_adamw(f"adamw_{i}", w, gr, m, v) for i, (w, gr, m, v) in enumerate(zip(mats, grads_big, mats_m, mats_v))]
    pack = lambda vs, bg: jnp.concatenate(list(vs) + [bg.reshape(2, D_MODEL), jnp.zeros((1, D_MODEL), F32)], axis=0)
    upd_small = _adamw("adamw_small", pack(gains, b_gate), pack(grad_gains, grad_b_gate),
                       pack(gains_m, m_b_gate), pack(gains_v, v_b_gate))
    upd_conv = _adamw("adamw_conv", w_conv, grad_w_conv, m_w_conv, v_w_conv)

    def small_out(a, which):
        gains_out = [a[r:r + 1, :] for r in range(5)]
        return gains_out, a[5:7, :].reshape(1, 2 * D_MODEL)

    def ordered(g_pre_mix_, big_, b_gate_, conv_, g_rest):
        return [g_pre_mix_, big_[0], b_gate_, conv_, big_[1], big_[2], big_[3], g_rest[0], g_rest[1], big_[4], big_[5],
                g_rest[2], g_rest[3], big_[6], big_[7]]

    outs = [loss, grad_x[None]]
    outs += ordered(grad_gains[0], grads_big, grad_b_gate, grad_w_conv, grad_gains[1:])
    for which in range(3):
        g_out, b_out = small_out(upd_small[which], which)
        outs += ordered(g_out[0], [u[which] for u in upd_big], b_out, upd_conv[which], g_out[1:])
    return tuple(outs)
```

```python
import jax
import jax.numpy as jnp
from jax import lax
from jax.experimental import pallas as pl
from jax.experimental.pallas import tpu as pltpu

F32 = jnp.float32
BF16 = jnp.bfloat16
MESH = pl.DeviceIdType.MESH

D_MODEL = 1024
N_HEADS = 8
HEAD_DIM = 64
ATTN_W = N_HEADS * HEAD_DIM
CONV_W = 512
D_FF = 4096
PLE_DIM = 256
D_IN = 5120
N_CHIPS = 4
EPS = 1e-6
Q_SCALE = HEAD_DIM ** -0.5

ADAM_LR = 0.001
ADAM_B1 = 0.9
ADAM_B2 = 0.999
ADAM_EPS = 1e-08
ADAM_WD = 0.01
ADAM_STEP = 10

V7X_VMEM_BYTES = 64 * 1024 * 1024
VMEM_LIMIT = V7X_VMEM_BYTES - 8 * 1024 * 1024
LANES = 128
ATT_BLK = 256
SMALL_ROWS = 16
CONV_PAD_ROWS = 16


def _cparams(n_grid):
    return pltpu.CompilerParams(dimension_semantics=("arbitrary",) * n_grid, vmem_limit_bytes=VMEM_LIMIT)


def _bs(shape, fn):
    return pl.BlockSpec(shape, fn)


def _rms_stats(xf):
    return lax.rsqrt(jnp.mean(xf * xf, axis=-1, keepdims=True) + EPS)


def _rms(xf, g):
    return xf * _rms_stats(xf) * g


def _rms_bwd(xf, g, dy):
    r = _rms_stats(xf)
    xh = xf * r
    dyg = dy * g
    dx = r * (dyg - xh * jnp.mean(dyg * xh, axis=-1, keepdims=True))
    return dx, jnp.sum(dy * xh, axis=0, keepdims=True)


def _sig(z):
    return 1.0 / (1.0 + jnp.exp(-z))


def _ident(a):
    return a


def _to_bf16(a):
    return a.astype(BF16)


_DIMS = {"nn": (((1,), (0,)), ((), ())), "nt": (((1,), (1,)), ((), ())), "tn": (((0,), (0,)), ((), ()))}


def _mm(name, mode, grid, a_ins, a_fn, b_ins, b_fn, outs, acc_shape, epi_ins=(), epi_fn=None,
        a_cache=None, a_outs=(), epi_a=()):
    nk = grid[2]
    na, nb, ne, no, nao = len(a_ins), len(b_ins), len(epi_ins), len(outs), len(a_outs)
    assert a_cache is None or nk == 1
    assert not a_outs or a_cache is not None
    dims = _DIMS[mode]
    if epi_fn is None:
        epi_fn = lambda acc: (acc,)

    def body(*refs):
        a_refs = refs[:na]
        b_refs = refs[na:na + nb]
        e_refs = refs[na + nb:na + nb + ne]
        o_refs = refs[na + nb + ne:na + nb + ne + no]
        ao_refs = refs[na + nb + ne + no:na + nb + ne + no + nao]
        scratch = list(refs[na + nb + ne + no + nao:])
        acc_ref = scratch.pop(0) if nk > 1 else None
        a_sc = scratch.pop(0) if a_cache is not None else None
        j = pl.program_id(1)
        k = pl.program_id(2)

        def finish(acc):
            res = epi_fn(acc, *[a_refs[t][...] for t in epi_a], *[r[...] for r in e_refs])
            for r, val in zip(o_refs, res):
                r[...] = val.astype(r.dtype)

        if a_sc is not None:
            @pl.when(j == 0)
            def _():
                res = a_fn(*[r[...] for r in a_refs])
                if nao:
                    for r, val in zip(ao_refs, res[1:]):
                        r[...] = val.astype(r.dtype)
                    res = res[0]
                a_sc[...] = res
            a = a_sc[...]
        else:
            a = a_fn(*[r[...] for r in a_refs])
        b = b_fn(*[r[...] for r in b_refs])
        prod = lax.dot_general(a, b, dims, preferred_element_type=F32)
        if nk == 1:
            finish(prod)
        else:
            @pl.when(k == 0)
            def _():
                acc_ref[...] = prod

            @pl.when(k > 0)
            def _():
                acc_ref[...] += prod

            @pl.when(k == nk - 1)
            def _():
                finish(acc_ref[...])

    scratch_shapes = []
    if nk > 1:
        scratch_shapes.append(pltpu.VMEM(acc_shape, F32))
    if a_cache is not None:
        scratch_shapes.append(pltpu.VMEM(*a_cache))
    all_outs = list(outs) + list(a_outs)
    res = pl.pallas_call(
        body, name=name, grid=grid,
        in_specs=[s for _, s in a_ins] + [s for _, s in b_ins] + [s for _, s in epi_ins],
        out_specs=[s for _, s in all_outs],
        out_shape=[o for o, _ in all_outs],
        scratch_shapes=scratch_shapes,
        compiler_params=_cparams(3),
    )(*[a for a, _ in a_ins], *[a for a, _ in b_ins], *[a for a, _ in epi_ins])
    return res


def _sds(shape, dtype):
    return jax.ShapeDtypeStruct(shape, dtype)


def _nt(a, b):
    return lax.dot_general(a, b, _DIMS["nt"], preferred_element_type=F32)


def _tn(a, b):
    return lax.dot_general(a, b, _DIMS["tn"], preferred_element_type=F32)


def _nn(a, b):
    return lax.dot_general(a, b, _DIMS["nn"], preferred_element_type=F32)


HEAD_PARTS = 2


def _mlp_down_ple_head(up, x1, p, tgt, g_ple, g_post_mlp, w_down, w_pg, w_pp, seq, tr):
    nblk = seq // tr
    D = D_MODEL

    def body(up_ref, x1_ref, p_ref, t_ref, gp_ref, gm_ref, wd_ref, wpg_ref, wpp_ref,
             dx2_ref, df_ref, dpre_ref, h3_ref, dpp_ref, loss_ref, dgp_ref, dgm_ref):
        gp, gm, wpg, wpp = gp_ref[...], gm_ref[...], wpg_ref[...], wpp_ref[...]
        halves = [pl.ds(n * (tr // HEAD_PARTS), tr // HEAD_PARTS) for n in range(HEAD_PARTS)]
        w_down = wd_ref[...]
        fb = []
        for r in halves:
            hidden = jnp.maximum(up_ref[r, :].astype(F32), 0.0)
            fb.append(_nn((hidden * hidden).astype(BF16), w_down))
        loss, dgp_sum, dgm_sum = 0.0, 0.0, 0.0
        for s, r in enumerate(halves):
            x2b = x1_ref[r, :] + _rms(fb[s], gm)
            h3 = _rms(x2b, gp).astype(BF16)
            gate = _sig(_nn(h3, wpg))
            pp = _nn(p_ref[r, :].astype(BF16), wpp)
            err = x2b + gate * pp - t_ref[r, :]
            dx3 = err * (1.0 / D)
            dpre = (dx3 * pp * gate * (1.0 - gate)).astype(BF16)
            h3_ref[r, :] = h3
            dpp_ref[r, :] = (dx3 * gate).astype(BF16)
            dpre_ref[r, :] = dpre
            dxn, dgp = _rms_bwd(x2b, gp, _nt(dpre, wpg))
            dx2 = dx3 + dxn
            dx2_ref[r, :] = dx2
            dfb, dgm = _rms_bwd(fb[s], gm, dx2)
            df_ref[r, :] = dfb.astype(BF16)
            loss = loss + jnp.sum(err * err, axis=0, keepdims=True)
            dgp_sum, dgm_sum = dgp_sum + dgp, dgm_sum + dgm
        loss_ref[...] = loss * (0.5 / D)
        dgp_ref[...] = dgp_sum
        dgm_ref[...] = dgm_sum

    rows = _bs((tr, D), lambda i: (i, 0))
    vec = _bs((1, D), lambda i: (0, 0))
    part = _bs((None, 1, D), lambda i: (i, 0, 0))
    return pl.pallas_call(
        body, name="mlp_down_ple_head", grid=(nblk,),
        in_specs=[_bs((tr, D_FF), lambda i: (i, 0)), rows, _bs((tr, PLE_DIM), lambda i: (i, 0)), rows, vec, vec,
                  _bs((D_FF, D), lambda i: (0, 0)), _bs((D, D), lambda i: (0, 0)), _bs((PLE_DIM, D), lambda i: (0, 0))],
        out_specs=[rows] * 5 + [part] * 3,
        out_shape=[_sds((seq, D), F32)] + [_sds((seq, D), BF16)] * 4 + [_sds((nblk, 1, D), F32)] * 3,
        compiler_params=_cparams(1),
    )(up, x1, p, tgt, g_ple, g_post_mlp, w_down, w_pg, w_pp)


def _shift_rows_down(u, prev, n):
    rows = u.shape[0]
    ridx = lax.broadcasted_iota(jnp.int32, u.shape, 0)
    out = pltpu.roll(u, n, 0)
    for r in range(n):
        out = jnp.where(ridx == r, prev[8 - n + r:8 - n + r + 1, :], out)
    del rows
    return out


def _shift_rows_up(u, nxt, n):
    rows = u.shape[0]
    ridx = lax.broadcasted_iota(jnp.int32, u.shape, 0)
    out = pltpu.roll(u, rows - n, 0)
    for r in range(n):
        out = jnp.where(ridx == rows - n + r, nxt[r:r + 1, :], out)
    return out


CONV_COL0 = 0


def _conv_fwd(proj, w_conv, seq, tr):
    hb = tr // 8

    def body(cb_ref, cc_ref, cu_ref, ccp_ref, cup_ref, w_ref, e_ref):
        i = pl.program_id(0)
        u = cc_ref[...] * cu_ref[...]
        up = jnp.where(i > 0, ccp_ref[...] * cup_ref[...], 0.0)
        w = w_ref[...]
        d = w[0:1, :] * _shift_rows_down(u, up, 2) + w[1:2, :] * _shift_rows_down(u, up, 1) + w[2:3, :] * u
        e_ref[...] = (cb_ref[...] * d).astype(BF16)

    prev = lambda c: (lambda i: (jnp.maximum(i * hb - 1, 0), c))
    return pl.pallas_call(
        body, name="conv_fwd", grid=(seq // tr,),
        in_specs=[_bs((tr, CONV_W), lambda i: (i, CONV_COL0)),
                  _bs((tr, CONV_W), lambda i: (i, CONV_COL0 + 1)),
                  _bs((tr, CONV_W), lambda i: (i, CONV_COL0 + 2)),
                  _bs((8, CONV_W), prev(CONV_COL0 + 1)),
                  _bs((8, CONV_W), prev(CONV_COL0 + 2)),
                  _bs((3, CONV_W), lambda i: (0, 0))],
        out_specs=_bs((tr, CONV_W), lambda i: (i, 0)),
        out_shape=_sds((seq, CONV_W), BF16),
        compiler_params=_cparams(1),
    )(proj, proj, proj, proj, proj, w_conv)


def _conv_bwd(proj, de, w_conv, seq, tr):
    hb = tr // 8
    nblk = seq // tr

    def body(cb_ref, cc_ref, cu_ref, ccp_ref, cup_ref, cbn_ref, de_ref, den_ref, w_ref, o_ref, dw_ref):
        i = pl.program_id(0)
        cc, cu, cb = cc_ref[...], cu_ref[...], cb_ref[...]
        u = cc * cu
        up = jnp.where(i > 0, ccp_ref[...] * cup_ref[...], 0.0)
        u1 = _shift_rows_down(u, up, 1)
        u2 = _shift_rows_down(u, up, 2)
        de_ = de_ref[...]
        dd = de_ * cb
        ddn = jnp.where(i < nblk - 1, den_ref[...] * cbn_ref[...], 0.0)
        w = w_ref[...]
        du = w[2:3, :] * dd + w[1:2, :] * _shift_rows_up(dd, ddn, 1) + w[0:1, :] * _shift_rows_up(dd, ddn, 2)
        o_ref[:, 0:CONV_W] = (de_ * (w[0:1, :] * u2 + w[1:2, :] * u1 + w[2:3, :] * u)).astype(BF16)
        o_ref[:, CONV_W:2 * CONV_W] = (du * cu).astype(BF16)
        o_ref[:, 2 * CONV_W:3 * CONV_W] = (du * cc).astype(BF16)
        ridx = lax.broadcasted_iota(jnp.int32, (8, CONV_W), 0)
        dw0 = jnp.sum(dd * u2, axis=0, keepdims=True)
        dw1 = jnp.sum(dd * u1, axis=0, keepdims=True)
        dw2 = jnp.sum(dd * u, axis=0, keepdims=True)
        dw_ref[...] = jnp.where(ridx == 0, dw0, jnp.where(ridx == 1, dw1, jnp.where(ridx == 2, dw2, 0.0)))

    prev = lambda c: (lambda i: (jnp.maximum(i * hb - 1, 0), c))
    nxt = lambda c: (lambda i: (jnp.minimum((i + 1) * hb, seq // 8 - 1), c))
    return pl.pallas_call(
        body, name="conv_bwd", grid=(nblk,),
        in_specs=[_bs((tr, CONV_W), lambda i: (i, CONV_COL0)),
                  _bs((tr, CONV_W), lambda i: (i, CONV_COL0 + 1)),
                  _bs((tr, CONV_W), lambda i: (i, CONV_COL0 + 2)),
                  _bs((8, CONV_W), prev(CONV_COL0 + 1)),
                  _bs((8, CONV_W), prev(CONV_COL0 + 2)),
                  _bs((8, CONV_W), nxt(CONV_COL0)),
                  _bs((tr, CONV_W), lambda i: (i, 0)),
                  _bs((8, CONV_W), nxt(0)),
                  _bs((3, CONV_W), lambda i: (0, 0))],
        out_specs=[_bs((tr, 3 * CONV_W), lambda i: (i, 0)), _bs((None, 8, CONV_W), lambda i: (i, 0, 0))],
        out_shape=[_sds((seq, 3 * CONV_W), BF16), _sds((nblk, 8, CONV_W), F32)],
        compiler_params=_cparams(1),
    )(proj, proj, proj, proj, proj, proj, de, de, w_conv)


def _log_gates(z):
    lse = jnp.log(1.0 + jnp.exp(-jnp.abs(z)))
    log_beta = jnp.minimum(z, 0.0) - lse
    return log_beta, log_beta - z


DEAD_LOG_WEIGHT = -110.0
NO_TILE = -1e30


def _first_live_tile(start, scores, live_sc):
    def alive():
        return jnp.max(jnp.maximum(live_sc[0], live_sc[1])) > DEAD_LOG_WEIGHT

    def step(c):
        for h, z in enumerate(scores(c[0])):
            live_sc[h] = live_sc[h] + jnp.sum(_log_gates(z)[1], axis=-1, keepdims=True)
        return c[0] - 1, alive()

    j_end, _ = lax.while_loop(lambda c: jnp.logical_and(c[0] >= 0, c[1]), step, (start, alive()))
    return j_end + 1


def _attn_fwd(proj, seq):
    blk = ATT_BLK
    nq = seq // blk
    npair = N_HEADS // 2

    def body(q_ref, k_ref, v_ref, o_ref, z0_sc, z1_sc, w0_sc, w1_sc, tot_sc, acc_sc):
        i = pl.program_id(1)
        is_a = lax.broadcasted_iota(jnp.int32, (1, LANES), 1) < HEAD_DIM
        q2 = (q_ref[...] * Q_SCALE).astype(BF16)
        zero = jnp.zeros_like(q2)
        qs = (jnp.where(is_a, q2, zero), jnp.where(is_a, zero, q2))
        row = lax.broadcasted_iota(jnp.int32, (blk, blk), 0)
        col = lax.broadcasted_iota(jnp.int32, (blk, blk), 1)
        tri = (row > col).astype(BF16)
        causal = col < row

        def tile_of(ref, j):
            return ref[pl.ds(pl.multiple_of(j * blk, blk), blk), :].astype(BF16)

        def scores(j):
            k2 = tile_of(k_ref, j)
            return [_nt(qs[h], k2) for h in range(2)]

        has_left = i > 0
        left = jnp.maximum(i - 1, 0)

        g_d = [_log_gates(z) for z in scores(i)]
        g_l = [_log_gates(z) for z in scores(left)]
        keep_d = [jnp.where(causal, g[1], 0.0) for g in g_d]
        suf_d = [_nn(lk.astype(BF16), tri) for lk in keep_d]
        suf_l = [_nn(g[1].astype(BF16), tri) for g in g_l]
        v_d, v_l = tile_of(v_ref, i), tile_of(v_ref, left)
        pv = []
        for h in range(2):
            sum_d = jnp.sum(keep_d[h], axis=-1, keepdims=True)
            w_d = jnp.where(causal, jnp.exp(g_d[h][0] + suf_d[h]), 0.0)
            w_l = jnp.exp(g_l[h][0] + (jnp.where(has_left, sum_d, NO_TILE) + suf_l[h]))
            pv.append(_nn(w_d.astype(BF16), v_d) + _nn(w_l.astype(BF16), v_l))
            tot_sc[h] = sum_d + jnp.sum(g_l[h][1], axis=-1, keepdims=True)
        acc_sc[...] = jnp.where(is_a, pv[0], pv[1])

        z_bufs, w_bufs = (z0_sc, z1_sc), (w0_sc, w1_sc)

        def alive():
            return jnp.max(jnp.maximum(tot_sc[0], tot_sc[1])) > DEAD_LOG_WEIGHT

        def put(ref, vals):
            for h in range(2):
                ref[h] = vals[h]

        def weights(zs):
            gates = [_log_gates(z) for z in zs]
            sums = [_nn(g[1].astype(BF16), tri) for g in gates]
            ws = []
            for h in range(2):
                ws.append(jnp.exp(gates[h][0] + (tot_sc[h] + sums[h])).astype(BF16))
                tot_sc[h] = tot_sc[h] + jnp.sum(gates[h][1], axis=-1, keepdims=True)
            return ws

        def add_values(w_buf, j):
            v2 = tile_of(v_ref, j)
            acc_sc[...] += jnp.where(is_a, _nn(w_buf[0], v2), _nn(w_buf[1], v2))

        def trip(j, s):
            add_values(w_bufs[s], j + 1)
            put(z_bufs[1 - s], scores(jnp.maximum(j - 1, 0)))
            put(w_bufs[1 - s], weights((z_bufs[s][0], z_bufs[s][1])))

        @pl.when(jnp.logical_and(i >= 2, alive()))
        def _():
            put(z0_sc, scores(i - 2))
            w0_sc[...] = jnp.zeros_like(w0_sc)

            def two_trips(c):
                trip(c[0], 0)
                trip(c[0] - 1, 1)
                return c[0] - 2, alive()

            j_next, still = lax.while_loop(lambda c: jnp.logical_and(c[0] >= 1, c[1]), two_trips, (i - 2, i >= 2))
            one_left = jnp.logical_and(j_next == 0, still)

            @pl.when(one_left)
            def _():
                trip(0, 0)
                add_values(w1_sc, 0)

            @pl.when(jnp.logical_not(one_left))
            def _():
                add_values(w0_sc, j_next + 1)

        o_ref[...] = acc_sc[...].astype(BF16)

    return pl.pallas_call(
        body, name="attn_fwd", grid=(npair, nq),
        in_specs=[_bs((blk, LANES), lambda p, i: (i, p)),
                  _bs((seq, LANES), lambda p, i: (0, npair + p)),
                  _bs((seq, LANES), lambda p, i: (0, 2 * npair + p))],
        out_specs=_bs((blk, LANES), lambda p, i: (i, p)),
        out_shape=_sds((seq, ATTN_W), BF16),
        scratch_shapes=[pltpu.VMEM((2, blk, blk), F32), pltpu.VMEM((2, blk, blk), F32),
                        pltpu.VMEM((2, blk, blk), BF16), pltpu.VMEM((2, blk, blk), BF16),
                        pltpu.VMEM((2, blk, 1), F32), pltpu.VMEM((blk, LANES), F32)],
        compiler_params=_cparams(2),
    )(proj, proj, proj)


def _attn_bwd(proj, do, seq):
    blk = ATT_BLK
    nq = seq // blk
    npair = N_HEADS // 2

    def body(q_ref, k_ref, v_ref, do_ref, dq_ref, dk_out, dv_out,
             prod0_sc, prod1_sc, pend0_sc, pend1_sc, tot_sc, live_sc, cum_sc, pre_sc, dq_sc, dk_ref, dv_ref):
        i = pl.program_id(1)

        @pl.when(i == 0)
        def _():
            dk_ref[...] = jnp.zeros_like(dk_ref)
            dv_ref[...] = jnp.zeros_like(dv_ref)

        is_a = lax.broadcasted_iota(jnp.int32, (1, LANES), 1) < HEAD_DIM
        q2 = (q_ref[...] * Q_SCALE).astype(BF16)
        do2 = do_ref[...]
        zero = jnp.zeros_like(q2)
        qs = (jnp.where(is_a, q2, zero), jnp.where(is_a, zero, q2))
        dos = (jnp.where(is_a, do2, zero), jnp.where(is_a, zero, do2))
        row = lax.broadcasted_iota(jnp.int32, (blk, blk), 0)
        col = lax.broadcasted_iota(jnp.int32, (blk, blk), 1)
        tri_after = (row > col).astype(BF16)
        tri_excl = (row < col).astype(BF16)
        causal = col < row

        def tile_of(ref, j):
            return ref[pl.ds(pl.multiple_of(j * blk, blk), blk), :].astype(BF16)

        def scores(j):
            k2 = tile_of(k_ref, j)
            return [_nt(qs[h], k2) for h in range(2)]

        def products(j):
            v2 = tile_of(v_ref, j)
            return scores(j) + [_nt(dos[h], v2) for h in range(2)]

        def row_sum(a):
            return jnp.sum(a, axis=-1, keepdims=True)

        def grad_matmuls(ws, dzs, j):
            rows = pl.ds(pl.multiple_of(j * blk, blk), blk)
            k2 = tile_of(k_ref, j)
            dq_sc[...] += jnp.where(is_a, _nn(dzs[0], k2), _nn(dzs[1], k2))
            dk_ref[rows, :] += jnp.where(is_a, _tn(dzs[0], q2), _tn(dzs[1], q2))
            if ws is not None:
                dv_ref[rows, :] += jnp.where(is_a, _tn(ws[0], do2), _tn(ws[1], do2))

        has_left = i > 0
        left = jnp.maximum(i - 1, 0)

        p_d, p_l = products(i), products(left)
        g_d = [_log_gates(z) for z in p_d[:2]]
        g_l = [_log_gates(z) for z in p_l[:2]]
        keep_d = [jnp.where(causal, g[1], 0.0) for g in g_d]
        suf_d = [_nn(lk.astype(BF16), tri_after) for lk in keep_d]
        suf_l = [_nn(g[1].astype(BF16), tri_after) for g in g_l]
        w_d, w_l, gg_d, gg_l = [], [], [], []
        for h in range(2):
            sum_d = row_sum(keep_d[h])
            w_d.append(jnp.where(causal, jnp.exp(g_d[h][0] + suf_d[h]), 0.0))
            w_l.append(jnp.exp(g_l[h][0] + (jnp.where(has_left, sum_d, NO_TILE) + suf_l[h])))
            gg_d.append(p_d[2 + h] * w_d[h])
            gg_l.append(p_l[2 + h] * w_l[h])
            tot_sc[h] = sum_d + row_sum(g_l[h][1])
        before_d = [_nn(g.astype(BF16), tri_excl) for g in gg_d]
        before_l = [_nn(g.astype(BF16), tri_excl) for g in gg_l]
        dz_d, dz_l = [], []
        for h in range(2):
            beta_d, beta_l = jnp.exp(g_d[h][0]), jnp.exp(g_l[h][0])
            dz_l.append((gg_l[h] * (1.0 - beta_l) - before_l[h] * beta_l).astype(BF16))
            dz = gg_d[h] * (1.0 - beta_d) - (row_sum(gg_l[h]) + before_d[h]) * beta_d
            dz_d.append(jnp.where(causal, dz, 0.0).astype(BF16))
        dq_sc[...] = jnp.zeros_like(dq_sc)
        grad_matmuls([w.astype(BF16) for w in w_l], dz_l, left)
        grad_matmuls([w.astype(BF16) for w in w_d], dz_d, i)

        live_sc[...] = tot_sc[...]
        first = _first_live_tile(i - 2, scores, live_sc)
        trips = i - 1 - first
        prod_bufs, pend_bufs = (prod0_sc, prod1_sc), (pend0_sc, pend1_sc)

        def local_grads(prods):
            zs, dws = prods[:2], prods[2:]
            gates = [_log_gates(z) for z in zs]
            sums = [_nn(g[1].astype(BF16), tri_after) for g in gates]
            ws, gs = [], []
            for h in range(2):
                cum = cum_sc[h] + row_sum(gates[h][1])
                cum_sc[h] = cum
                ws.append(jnp.exp(gates[h][0] + ((live_sc[h] - cum) + sums[h])))
                gs.append(dws[h] * ws[h])
            befores = [_nn(g.astype(BF16), tri_excl) for g in gs]
            dzs = []
            for h in range(2):
                beta = jnp.exp(gates[h][0])
                dzs.append((gs[h] * (1.0 - beta) - (pre_sc[h] + befores[h]) * beta).astype(BF16))
                pre_sc[h] = pre_sc[h] + row_sum(gs[h])
            return [w.astype(BF16) for w in ws] + dzs

        def put(ref, vals):
            for n, val in enumerate(vals):
                ref[n] = val

        def flush(pend, j):
            grad_matmuls([pend[0], pend[1]], [pend[2], pend[3]], j)

        def trip(j, s):
            flush(pend_bufs[s], jnp.maximum(j - 1, first))
            put(prod_bufs[1 - s], products(j + 1))
            put(pend_bufs[1 - s], local_grads([prod_bufs[s][n] for n in range(4)]))

        def earlier_keys_share(j, mask):
            dzs = []
            for h, z in enumerate(scores(j)):
                beta = jnp.exp(_log_gates(z)[0])
                dzs.append(jnp.where(mask, -pre_sc[h] * beta, 0.0).astype(BF16))
            grad_matmuls(None, dzs, j)

        @pl.when(trips > 0)
        def _():
            cum_sc[...] = jnp.zeros_like(cum_sc)
            pre_sc[...] = jnp.zeros_like(pre_sc)
            pend0_sc[...] = jnp.zeros_like(pend0_sc)
            put(prod0_sc, products(first))

            def two_trips(pp, carry):
                trip(first + 2 * pp, 0)
                trip(first + 2 * pp + 1, 1)
                return carry

            lax.fori_loop(0, trips // 2, two_trips, 0)
            odd = trips % 2 == 1

            @pl.when(odd)
            def _():
                trip(i - 2, 0)
                flush(pend1_sc, i - 2)

            @pl.when(jnp.logical_not(odd))
            def _():
                flush(pend0_sc, i - 2)

            earlier_keys_share(i - 1, True)
            earlier_keys_share(i, causal)

        dq_ref[...] = (dq_sc[...] * Q_SCALE).astype(BF16)

        @pl.when(i == nq - 1)
        def _():
            dk_out[...] = dk_ref[...].astype(BF16)
            dv_out[...] = dv_ref[...].astype(BF16)

    qmap = lambda p, i: (i, p)
    return pl.pallas_call(
        body, name="attn_bwd", grid=(npair, nq),
        in_specs=[_bs((blk, LANES), qmap),
                  _bs((seq, LANES), lambda p, i: (0, npair + p)),
                  _bs((seq, LANES), lambda p, i: (0, 2 * npair + p)),
                  _bs((blk, LANES), qmap)],
        out_specs=[_bs((blk, LANES), qmap),
                   _bs((seq, LANES), lambda p, i: (0, p)),
                   _bs((seq, LANES), lambda p, i: (0, p))],
        out_shape=[_sds((seq, ATTN_W), BF16)] * 3,
        scratch_shapes=[pltpu.VMEM((4, blk, blk), F32), pltpu.VMEM((4, blk, blk), F32),
                        pltpu.VMEM((4, blk, blk), BF16), pltpu.VMEM((4, blk, blk), BF16),
                        pltpu.VMEM((2, blk, 1), F32), pltpu.VMEM((2, blk, 1), F32), pltpu.VMEM((2, blk, 1), F32),
                        pltpu.VMEM((2, blk, 1), F32), pltpu.VMEM((blk, LANES), F32),
                        pltpu.VMEM((seq, LANES), F32), pltpu.VMEM((seq, LANES), F32)],
        compiler_params=_cparams(2),
    )(proj, proj, proj, do)


def _elementwise(name, fn, ins, out_dtypes):
    rows, cols = ins[0].shape
    tr = rows
    for cand in (512, 256, 128, 64, 32, 16, 8):
        if rows % cand == 0 and cand * cols * 4 <= 2 * 1024 * 1024:
            tr = cand
            break
    n_in = len(ins)

    def body(*refs):
        res = fn(*[r[...] for r in refs[:n_in]])
        for r, val in zip(refs[n_in:], res):
            r[...] = val.astype(r.dtype)

    spec = _bs((tr, cols), lambda i: (i, 0))
    return pl.pallas_call(
        body, name=name, grid=(rows // tr,),
        in_specs=[spec] * n_in, out_specs=[spec] * len(out_dtypes),
        out_shape=[_sds((rows, cols), dt) for dt in out_dtypes],
        compiler_params=_cparams(1),
    )(*ins)


def _adamw_fn(w, g, m, v):
    m = ADAM_B1 * m + (1.0 - ADAM_B1) * g
    v = ADAM_B2 * v + (1.0 - ADAM_B2) * (g * g)
    m_hat = m / (1.0 - ADAM_B1 ** ADAM_STEP)
    v_hat = v / (1.0 - ADAM_B2 ** ADAM_STEP)
    delta = -ADAM_LR * (m_hat / (jnp.sqrt(v_hat) + ADAM_EPS) + ADAM_WD * w)
    return delta, m, v


def _adamw(name, w, g, m, v):
    shape = w.shape
    as2d = lambda a: a.reshape(-1, shape[-1])
    delta, nm, nv = _elementwise(name, _adamw_fn, [as2d(w), as2d(g), as2d(m), as2d(v)], [F32, F32, F32])
    return delta.reshape(shape), nm.reshape(shape), nv.reshape(shape)


def _place():
    return lax.axis_index("x"), lax.axis_index("y"), lax.axis_index("c")


ANY = pl.BlockSpec(memory_space=pl.ANY)
VMEM_WHOLE = pl.BlockSpec(memory_space=pltpu.VMEM)


def _allgather_weights(shards):
    n = len(shards)

    def body(*refs):
        src, dst = refs[:n], refs[n:2 * n]
        send_sems, recv_sems, local_sems = refs[2 * n:]
        x, y, c = _place()
        me, sibling, mychip = (x, y, c), (x, y, 1 - c), 2 * x + y

        x_nbr, y_nbr, diag = 2 * (1 - x) + y, 2 * x + (1 - y), 2 * (1 - x) + (1 - y)
        to_x, to_y = (1 - x, y, c), (x, 1 - y, c)

        def parts(w):
            hr = src[w].shape[0] // 2
            first = hr // 2 if hr % 32 == 0 else hr
            return first, hr - first

        def rows_of(w, chip, half, route):
            hr = src[w].shape[0] // 2
            first, second = parts(w)
            start, size = {0: (0, hr), 1: (0, hr), 2: (0, first), 3: (first, second)}[route]
            return dst[w].at[chip, pl.ds(half * hr + start, size)]

        def copy(w, k, src_ref, dst_ref, to):
            return pltpu.make_async_remote_copy(src_ref=src_ref, dst_ref=dst_ref, send_sem=send_sems.at[w, k],
                                                recv_sem=recv_sems.at[w, k], device_id=to, device_id_type=MESH)

        def landed(w, route):
            chip = {0: x_nbr, 1: y_nbr, 2: diag, 3: diag}[route]
            return rows_of(w, chip, c, route), chip

        def routes(w):
            return (0, 1, 2, 3) if parts(w)[1] else (0, 1, 2)

        started, local = [], []
        for w in range(n):
            hr = src[w].shape[0] // 2
            own = pltpu.make_async_copy(src[w], dst[w].at[mychip], local_sems.at[w])
            own.start()
            local.append(own)
            mine = src[w].at[pl.ds(c * hr, hr)]
            for route, to in ((0, to_x), (1, to_y)):
                cp = copy(w, route, mine, rows_of(w, mychip, c, route), to)
                cp.start()
                started.append(cp)

        def pass_on(w, route):
            got, chip = landed(w, route)
            copy(w, route, got, got, me).wait_recv()
            if route == 1:
                part = rows_of(w, chip, c, 2)
                started.append(copy(w, 2, part, part, to_x))
                started[-1].start()
            if route == 0 and parts(w)[1]:
                part = rows_of(w, chip, c, 3)
                started.append(copy(w, 3, part, part, to_y))
                started[-1].start()
            started.append(copy(w, 4 + route, got, got, sibling))
            started[-1].start()

        for w in range(n):
            pass_on(w, 1)
            pass_on(w, 0)
        for w in range(n):
            for route in routes(w)[2:]:
                pass_on(w, route)
        for w in range(n):
            for route in routes(w):
                chip = landed(w, route)[1]
                from_sib = rows_of(w, chip, 1 - c, route)
                copy(w, 4 + route, from_sib, from_sib, me).wait_recv()
        for cp in local:
            cp.wait()
        for cp in started:
            cp.wait_send()

    return pl.pallas_call(
        body, name="allgather_weights",
        in_specs=[VMEM_WHOLE] * n, out_specs=[VMEM_WHOLE] * n,
        out_shape=[_sds((N_CHIPS,) + s.shape, s.dtype) for s in shards],
        scratch_shapes=[pltpu.SemaphoreType.DMA((n, 8)), pltpu.SemaphoreType.DMA((n, 8)),
                        pltpu.SemaphoreType.DMA((n,))],
        compiler_params=pltpu.CompilerParams(vmem_limit_bytes=VMEM_LIMIT),
    )(*shards)


SUM_ROWS = 64


def _rs_pair_sum(name, grads):
    n = len(grads)

    def body(*refs):
        g, out = refs[:n], refs[n:2 * n]
        stage, give16, land, keep = (refs[m * n:(m + 1) * n] for m in range(2, 6))
        send_sems, recv_sems, stage_sems, keep_sems = refs[6 * n:]
        x, y, c = _place()
        sibling = (x, y, 1 - c)

        def over_rows(w, fn):
            nb = g[w].shape[1] // 2 // SUM_ROWS

            def step(idx, carry):
                fn(idx // nb, pl.ds(pl.multiple_of((idx % nb) * SUM_ROWS, SUM_ROWS), SUM_ROWS))
                return carry

            lax.fori_loop(0, N_CHIPS * nb, step, 0)

        loads = []
        for w in range(n):
            hr = g[w].shape[1] // 2
            st = pltpu.make_async_copy(g[w].at[:, pl.ds((1 - c) * hr, hr)], stage[w], stage_sems.at[w])
            kp = pltpu.make_async_copy(g[w].at[:, pl.ds(c * hr, hr)], keep[w], keep_sems.at[w])
            st.start()
            kp.start()
            loads.append((st, kp))
        gives = []
        for w in range(n):
            loads[w][0].wait()

            def narrow(k, rows, w=w):
                give16[w][k, rows, :] = stage[w][k, rows, :].astype(BF16)

            over_rows(w, narrow)
            give = pltpu.make_async_remote_copy(src_ref=give16[w], dst_ref=land[w], send_sem=send_sems.at[w],
                                                recv_sem=recv_sems.at[w], device_id=sibling, device_id_type=MESH)
            give.start()
            gives.append(give)
        for w in range(n):
            loads[w][1].wait()
            gives[w].wait_recv()

            def add(k, rows, w=w):
                out[w][k, rows, :] = (keep[w][k, rows, :] + land[w][k, rows, :].astype(F32)).astype(BF16)

            over_rows(w, add)
        for give in gives:
            give.wait_send()

    half = [(N_CHIPS, a.shape[1] // 2, a.shape[2]) for a in grads]
    wide = [pltpu.VMEM(s, F32) for s in half]
    narrow_bufs = [pltpu.VMEM(s, BF16) for s in half]
    sems = pltpu.SemaphoreType.DMA((n,))
    return pl.pallas_call(
        body, name=name,
        in_specs=[ANY] * n, out_specs=[VMEM_WHOLE] * n, out_shape=[_sds(s, BF16) for s in half],
        scratch_shapes=wide + narrow_bufs + narrow_bufs + wide + [sems, sems, sems, sems],
        compiler_params=pltpu.CompilerParams(vmem_limit_bytes=VMEM_LIMIT),
    )(*grads)


def _rs_exchange_join(parts):
    n = len(parts)

    def body(*refs):
        t, full = refs[:n], refs[n:2 * n]
        got_x, got_y, pass_on, got_2 = (refs[m * n:(m + 1) * n] for m in range(2, 6))
        send_sems, recv_sems = refs[6 * n:]
        x, y, c = _place()
        mychip, sibling = 2 * x + y, (x, y, 1 - c)
        x_nbr, y_nbr, diag = 2 * (1 - x) + y, 2 * x + (1 - y), 2 * (1 - x) + (1 - y)
        to_x, to_y = (1 - x, y, c), (x, 1 - y, c)
        sends = []

        def copy(w, k, src_ref, dst_ref, to):
            return pltpu.make_async_remote_copy(src_ref=src_ref, dst_ref=dst_ref, send_sem=send_sems.at[w, k],
                                                recv_sem=recv_sems.at[w, k], device_id=to, device_id_type=MESH)

        def start(cp):
            cp.start()
            sends.append(cp)

        def add_rows(w, count, fn):
            def step(idx, carry):
                fn(pl.ds(pl.multiple_of(idx * SUM_ROWS, SUM_ROWS), SUM_ROWS), pl.multiple_of(idx * SUM_ROWS, SUM_ROWS))
                return carry
            lax.fori_loop(0, count // SUM_ROWS, step, 0)

        f32 = lambda v: v.astype(F32)
        for w in range(n):
            ha = t[w].shape[1] // 2
            part_a, part_b = pl.ds(0, ha), pl.ds(ha, ha)
            start(copy(w, 0, t[w].at[x_nbr, part_a], got_x[w].at[0], to_x))
            start(copy(w, 1, t[w].at[diag, part_a], got_x[w].at[1], to_x))
            start(copy(w, 2, t[w].at[y_nbr, part_b], got_y[w].at[0], to_y))
            start(copy(w, 3, t[w].at[diag, part_b], got_y[w].at[1], to_y))
        for w in range(n):
            hr = t[w].shape[1]
            ha = hr // 2
            for k in (0, 1):
                copy(w, k, got_x[w].at[k], got_x[w].at[k], to_x).wait_recv()

            def sum_a(rows, r, w=w, hr=hr):
                full[w][pl.ds(pl.multiple_of(c * hr + r, SUM_ROWS), SUM_ROWS), :] = \
                    f32(t[w][mychip, rows, :]) + f32(got_x[w][0, rows, :])
                pass_on[w][rows, :] = (f32(t[w][y_nbr, rows, :]) + f32(got_x[w][1, rows, :])).astype(BF16)

            add_rows(w, ha, sum_a)
            start(copy(w, 4, pass_on[w].at[pl.ds(0, ha)], got_2[w].at[pl.ds(0, ha)], to_y))
            for k in (2, 3):
                copy(w, k, got_y[w].at[k - 2], got_y[w].at[k - 2], to_y).wait_recv()

            def sum_b(rows, r, w=w, hr=hr, ha=ha):
                lower = pl.ds(pl.multiple_of(ha + r, SUM_ROWS), SUM_ROWS)
                full[w][pl.ds(pl.multiple_of(c * hr + ha + r, SUM_ROWS), SUM_ROWS), :] = \
                    f32(t[w][mychip, lower, :]) + f32(got_y[w][0, rows, :])
                pass_on[w][lower, :] = (f32(t[w][x_nbr, lower, :]) + f32(got_y[w][1, rows, :])).astype(BF16)

            add_rows(w, ha, sum_b)
            start(copy(w, 5, pass_on[w].at[pl.ds(ha, ha)], got_2[w].at[pl.ds(ha, ha)], to_x))
        for w in range(n):
            hr = t[w].shape[1]
            ha = hr // 2
            copy(w, 4, got_2[w].at[pl.ds(0, ha)], got_2[w].at[pl.ds(0, ha)], to_y).wait_recv()
            copy(w, 5, got_2[w].at[pl.ds(ha, ha)], got_2[w].at[pl.ds(ha, ha)], to_x).wait_recv()

            def finish(rows, r, w=w, hr=hr):
                out_rows = pl.ds(pl.multiple_of(c * hr + r, SUM_ROWS), SUM_ROWS)
                full[w][out_rows, :] = full[w][out_rows, :] + f32(got_2[w][rows, :])

            add_rows(w, hr, finish)
            mine = full[w].at[pl.ds(c * hr, hr)]
            start(copy(w, 6, mine, mine, sibling))
        for w in range(n):
            hr = t[w].shape[1]
            theirs = full[w].at[pl.ds((1 - c) * hr, hr)]
            copy(w, 6, theirs, theirs, sibling).wait_recv()
        for cp in sends:
            cp.wait_send()

    half = lambda a: pltpu.VMEM((2, a.shape[1] // 2, a.shape[2]), a.dtype)
    whole = lambda a: pltpu.VMEM(a.shape[1:], a.dtype)
    return pl.pallas_call(
        body, name="rs_exchange_join",
        in_specs=[VMEM_WHOLE] * n, out_specs=[VMEM_WHOLE] * n,
        out_shape=[_sds((2 * a.shape[1], a.shape[2]), F32) for a in parts],
        scratch_shapes=[half(a) for a in parts] + [half(a) for a in parts] + [whole(a) for a in parts]
        + [whole(a) for a in parts] + [pltpu.SemaphoreType.DMA((n, 7)), pltpu.SemaphoreType.DMA((n, 7))],
        compiler_params=pltpu.CompilerParams(vmem_limit_bytes=VMEM_LIMIT),
    )(*parts)


def _small_allreduce(loss_p, dg_parts, dbg_a, dbg_c, dwc):
    ins = [loss_p] + list(dg_parts) + [dbg_a, dbg_c, dwc]
    n_in = len(ins)
    vmem = pl.BlockSpec(memory_space=pltpu.VMEM)

    def body(*refs):
        in_refs = refs[:n_in]
        out_ref, vec, buf, send_sems, recv_sems = refs[n_in:]
        x, y, c = _place()
        me = 4 * x + 2 * y + c
        vec[...] = jnp.zeros_like(vec)
        vec[0:1, :] = jnp.sum(in_refs[0][...], axis=0)
        for r in range(5):
            vec[1 + r:2 + r, :] = jnp.sum(in_refs[1 + r][...], axis=0)
        vec[6:7, :] = jnp.sum(in_refs[6][...], axis=0)
        vec[7:8, :] = jnp.sum(in_refs[7][...], axis=0)
        vec[8:16, 0:CONV_W] = jnp.sum(in_refs[8][...], axis=0)
        buf[pl.ds(me, 1)] = vec[...][None]
        copies = []
        for r in range(1, 8):
            fx, fy, fc = (r >> 2) & 1, (r >> 1) & 1, r & 1
            to = (1 - x if fx else x, 1 - y if fy else y, 1 - c if fc else c)
            cp = pltpu.make_async_remote_copy(src_ref=vec, dst_ref=buf.at[me], send_sem=send_sems.at[r - 1],
                                              recv_sem=recv_sems.at[r - 1], device_id=to, device_id_type=MESH)
            cp.start()
            copies.append(cp)
        for cp in copies:
            cp.wait()
        total = buf[0]
        for s in range(1, 8):
            total = total + buf[s]
        out_ref[...] = total
        out_ref[0:1, :] = jnp.broadcast_to(jnp.sum(total[0:1, :], axis=-1, keepdims=True), (1, D_MODEL))

    return pl.pallas_call(
        body, name="small_allreduce",
        in_specs=[vmem] * n_in, out_specs=vmem, out_shape=_sds((SMALL_ROWS, D_MODEL), F32),
        scratch_shapes=[pltpu.VMEM((SMALL_ROWS, D_MODEL), F32), pltpu.VMEM((8, SMALL_ROWS, D_MODEL), F32),
                        pltpu.SemaphoreType.DMA((7,)), pltpu.SemaphoreType.DMA((7,))],
    )(*ins)


def _local_step(x, p, tgt, g, b_gate, w_conv, wf):
    seq = x.shape[0]
    tm = min(seq, 1024)
    th = min(seq, 512)
    tl = min(seq, 2048)
    ni, nh, nl = seq // tm, seq // th, seq // tl
    g_pre_mix, g_post_mix, g_pre_mlp, g_post_mlp, g_ple = g
    w_in_nat, w_ao, w_co, w_o, w_up_nat, w_down, w_pg, w_pp = wf
    D = D_MODEL
    vec = lambda a, blk=0: (a, _bs((1, D), lambda i, j, k: (0, blk)))
    rows_i = lambda a, t, blk=0: (a, _bs((t, D), lambda i, j, k: (i, blk)))
    rows_k = lambda a, t, blk=0: (a, _bs((t, D), lambda i, j, k: (k, blk)))
    part = lambda n: (_sds((n, 1, D), F32), _bs((None, 1, D), lambda i, j, k: (i, 0, 0)))
    full2 = lambda a: (a, _bs(a.shape, lambda i, j, k: (0, 0)))

    normed = lambda xb, gb: (_rms(xb, gb).astype(BF16),) * 2
    keep_a = lambda t: [(_sds((seq, D), BF16), _bs((t, D), lambda i, j, k: (i, 0)))]
    qkv_w, conv_w = 3 * ATTN_W, 3 * CONV_W
    qkv, proj_conv, gates, h1 = _mm(
        "proj_in", "nn", (nh, 1, 1),
        a_ins=[rows_i(x, th), vec(g_pre_mix)], a_fn=normed, b_ins=[full2(w_in_nat)], b_fn=_ident,
        epi_fn=lambda acc: (acc[:, :qkv_w], acc[:, qkv_w:qkv_w + conv_w], acc[:, qkv_w + conv_w:]),
        outs=[(_sds((seq, qkv_w), BF16), _bs((th, qkv_w), lambda i, j, k: (i, 0))),
              (_sds((seq, conv_w), F32), _bs((th, conv_w), lambda i, j, k: (i, 0))),
              (_sds((seq, 2 * D), BF16), _bs((th, 2 * D), lambda i, j, k: (i, 0)))],
        acc_shape=(th, D_IN), a_cache=((th, D), BF16), a_outs=keep_a(th))
    o = _attn_fwd(qkv, seq)
    e = _conv_fwd(proj_conv, w_conv, seq, tm)

    def gate_values(ga, gc, ba, bc):
        return _sig(ga.astype(F32) + ba), _sig(gc.astype(F32) + bc)

    def branch_outputs(ob, eb, wao, wco):
        return _nn(ob, wao).astype(BF16).astype(F32), _nn(eb, wco).astype(BF16).astype(F32)

    def mix_fn(ga, gc, ob, eb, ba, bc, wao, wco):
        sa, sc = gate_values(ga, gc, ba, bc)
        ya, yc = branch_outputs(ob, eb, wao, wco)
        return ((sa * ya + sc * yc).astype(BF16),) * 2

    def post_mix(acc, xb, gb):
        return acc, xb + _rms(acc, gb)

    half_rows = lambda a: (a, _bs((th, a.shape[1]), lambda i, j, k: (i, 0)))
    mix_ins = [rows_i(gates, th, 0), rows_i(gates, th, 1), half_rows(o), half_rows(e), vec(b_gate, 0), vec(b_gate, 1),
               full2(w_ao), full2(w_co)]
    mixed, x1, mixin = _mm(
        "mix_out", "nn", (nh, 1, 1),
        a_ins=mix_ins, a_fn=mix_fn, b_ins=[full2(w_o)], b_fn=_ident,
        epi_ins=[rows_i(x, th), vec(g_post_mix)], epi_fn=post_mix,
        outs=[(_sds((seq, D), BF16), _bs((th, D), lambda i, j, k: (i, 0))),
              (_sds((seq, D), F32), _bs((th, D), lambda i, j, k: (i, 0)))],
        acc_shape=(th, D), a_cache=((th, D), BF16), a_outs=keep_a(th))
    up, h2 = _mm("mlp_up", "nn", (nh, 1, 1),
                 a_ins=[rows_i(x1, th), vec(g_pre_mlp)], a_fn=normed,
                 b_ins=[full2(w_up_nat)], b_fn=_ident,
                 outs=[(_sds((seq, D_FF), BF16), _bs((th, D_FF), lambda i, j, k: (i, 0)))],
                 acc_shape=(th, D_FF), a_cache=((th, D), BF16), a_outs=keep_a(th))

    def relu2(ub):
        r = jnp.maximum(ub.astype(F32), 0.0)
        return (r * r).astype(BF16)

    dx2, df, dpre, h3, dpp, loss_p, dg_ple_p, dg_post_mlp_p = _mlp_down_ple_head(
        up, x1, p, tgt, g_ple, g_post_mlp, w_down, w_pg, w_pp, seq, th)

    (dw_pp,) = _mm("dw_ple_proj", "tn", (1, 1, nh),
                   a_ins=[(p, _bs((th, PLE_DIM), lambda i, j, k: (k, 0)))], a_fn=_to_bf16,
                   b_ins=[rows_k(dpp, th)], b_fn=_ident,
                   outs=[(_sds((PLE_DIM, D), F32), _bs((PLE_DIM, D), lambda i, j, k: (0, 0)))],
                   acc_shape=(PLE_DIM, D))
    tx = min(seq, 4096)
    nx = seq // tx
    (dw_pg,) = _mm("dw_ple_gate", "tn", (1, 1, nx),
                   a_ins=[rows_k(h3, tx)], a_fn=_ident, b_ins=[rows_k(dpre, tx)], b_fn=_ident,
                   outs=[(_sds((D, D), F32), _bs((D, D), lambda i, j, k: (0, 0)))], acc_shape=(D, D))

    def dup_fn(acc, ub):
        return (acc * (2.0 * jnp.maximum(ub.astype(F32), 0.0)),)

    (dup,) = _mm("d_mlp_down", "nt", (nh, 1, 1),
                 a_ins=[rows_i(df, th)], a_fn=_ident, b_ins=[full2(w_down)], b_fn=_ident,
                 epi_ins=[(up, _bs((th, D_FF), lambda i, j, k: (i, 0)))], epi_fn=dup_fn,
                 outs=[(_sds((seq, D_FF), BF16), _bs((th, D_FF), lambda i, j, k: (i, 0)))],
                 acc_shape=(th, D_FF))
    (dw_down,) = _mm("dw_mlp_down", "tn", (4, 1, nx),
                     a_ins=[(up, _bs((tx, D), lambda i, j, k: (k, i)))], a_fn=relu2,
                     b_ins=[rows_k(df, tx)], b_fn=_ident,
                     outs=[(_sds((D_FF, D), F32), _bs((D, D), lambda i, j, k: (i, 0)))], acc_shape=(D, D))
    (dw_up,) = _mm("dw_mlp_up", "tn", (1, 4, nx),
                   a_ins=[rows_k(h2, tx)], a_fn=_ident,
                   b_ins=[(dup, _bs((tx, D), lambda i, j, k: (k, j)))], b_fn=_ident,
                   outs=[(_sds((N_CHIPS, D, D), F32), _bs((None, D, D), lambda i, j, k: (j, 0, 0)))],
                   acc_shape=(D, D))

    def mlp_norm_bwd(acc, x1b, dx2b, mixedb, g_mlp, g_mix):
        dxn, dg_mlp = _rms_bwd(x1b, g_mlp, acc)
        dx1b = dx2b + dxn
        dmixedb, dg_mix = _rms_bwd(mixedb.astype(F32), g_mix, dx1b)
        return dx1b, dmixedb, dg_mlp, dg_mix

    dx1, dmixed, dg_pre_mlp_p, dg_post_mix_p = _mm(
        "d_mlp_up", "nt", (nh, 1, 1),
        a_ins=[(dup, _bs((th, D_FF), lambda i, j, k: (i, 0)))], a_fn=_ident,
        b_ins=[full2(w_up_nat)], b_fn=_ident,
        epi_ins=[rows_i(x1, th), rows_i(dx2, th), rows_i(mixed, th), vec(g_pre_mlp), vec(g_post_mix)],
        epi_fn=mlp_norm_bwd,
        outs=[(_sds((seq, D), F32), _bs((th, D), lambda i, j, k: (i, 0))),
              (_sds((seq, D), BF16), _bs((th, D), lambda i, j, k: (i, 0))), part(nh), part(nh)],
        acc_shape=(th, D))
    (dw_o,) = _mm("dw_mix_out", "tn", (1, 1, nx),
                  a_ins=[rows_k(mixin, tx)], a_fn=_ident, b_ins=[rows_k(dmixed, tx)], b_fn=_ident,
                  outs=[(_sds((D, D), F32), _bs((D, D), lambda i, j, k: (0, 0)))], acc_shape=(D, D))

    def gate_bwd(acc, ga, gc, ob, eb, ba, bc, wao, wco):
        sa, sc = gate_values(ga, gc, ba, bc)
        ya, yc = branch_outputs(ob, eb, wao, wco)
        dga = acc * ya * sa * (1.0 - sa)
        dgc = acc * yc * sc * (1.0 - sc)
        dya, dyc = (acc * sa).astype(BF16), (acc * sc).astype(BF16)
        return (dya, dyc, jnp.concatenate([dga, dgc], axis=1), _nt(dya, wao), _nt(dyc, wco),
                jnp.sum(dga, axis=0, keepdims=True), jnp.sum(dgc, axis=0, keepdims=True))

    dya, dyc, dgate, do, de, dbg_a_p, dbg_c_p = _mm(
        "d_mix_out", "nt", (nh, 1, 1),
        a_ins=[rows_i(dmixed, th)], a_fn=_ident, b_ins=[full2(w_o)], b_fn=_ident,
        epi_ins=mix_ins, epi_fn=gate_bwd,
        outs=[(_sds((seq, D), BF16), _bs((th, D), lambda i, j, k: (i, 0)))] * 2
             + [(_sds((seq, 2 * D), BF16), _bs((th, 2 * D), lambda i, j, k: (i, 0))),
                (_sds((seq, ATTN_W), BF16), _bs((th, ATTN_W), lambda i, j, k: (i, 0))),
                (_sds((seq, CONV_W), F32), _bs((th, CONV_W), lambda i, j, k: (i, 0))), part(nh), part(nh)],
        acc_shape=(th, D))
    (dw_ao,) = _mm("dw_attn_out", "tn", (1, 1, nh),
                   a_ins=[(o, _bs((th, ATTN_W), lambda i, j, k: (k, 0)))], a_fn=_ident,
                   b_ins=[rows_k(dya, th)], b_fn=_ident,
                   outs=[(_sds((ATTN_W, D), F32), _bs((ATTN_W, D), lambda i, j, k: (0, 0)))], acc_shape=(ATTN_W, D))
    dq, dk, dv = _attn_bwd(qkv, do, seq)
    (dw_co,) = _mm("dw_conv_out", "tn", (1, 1, nh),
                   a_ins=[(e, _bs((th, CONV_W), lambda i, j, k: (k, 0)))], a_fn=_ident,
                   b_ins=[rows_k(dyc, th)], b_fn=_ident,
                   outs=[(_sds((CONV_W, D), F32), _bs((CONV_W, D), lambda i, j, k: (0, 0)))], acc_shape=(CONV_W, D))
    dconv, dwc_p = _conv_bwd(proj_conv, de, w_conv, seq, tm)
    qkv_w = 3 * ATTN_W
    join_bf16 = lambda *blocks: jnp.concatenate([b.astype(BF16) for b in blocks], axis=1)
    piece = lambda a, t, rows, blk=0: (a, _bs((t, a.shape[1]), (lambda i, j, k: (k, blk)) if rows == "k"
                                             else (lambda i, j, k: (i, blk))))
    (dw_in_qkv,) = _mm("dw_proj_in_qkv", "tn", (1, 1, ni),
                       a_ins=[rows_k(h1, tm)], a_fn=_ident,
                       b_ins=[piece(dq, tm, "k"), piece(dk, tm, "k"), piece(dv, tm, "k")], b_fn=join_bf16,
                       outs=[(_sds((D, qkv_w), F32), _bs((D, qkv_w), lambda i, j, k: (0, 0)))], acc_shape=(D, qkv_w))
    (dw_in_conv,) = _mm("dw_proj_in_conv", "tn", (1, 1, nl),
                        a_ins=[rows_k(h1, tl)], a_fn=_ident, b_ins=[piece(dconv, tl, "k")], b_fn=_ident,
                        outs=[(_sds((D, 3 * CONV_W), F32), _bs((D, 3 * CONV_W), lambda i, j, k: (0, 0)))],
                        acc_shape=(D, 3 * CONV_W))
    (dw_in_gate,) = _mm("dw_proj_in_gate", "tn", (1, 2, nx),
                        a_ins=[rows_k(h1, tx)], a_fn=_ident,
                        b_ins=[(dgate, _bs((tx, D), lambda i, j, k: (k, j)))], b_fn=_ident,
                        outs=[(_sds((D, 2 * D), F32), _bs((D, D), lambda i, j, k: (0, j)))], acc_shape=(D, D))
    dw_in = jnp.concatenate([dw_in_qkv, dw_in_conv, dw_in_gate], axis=1)

    def in_norm_bwd(acc, xb, dx1b, gb):
        dxn, dg = _rms_bwd(xb, gb, acc)
        return dx1b + dxn, dg

    grad_x, dg_pre_mix_p = _mm("d_proj_in", "nt", (nh, 1, 1),
                               a_ins=[piece(dq, th, "i"), piece(dk, th, "i"), piece(dv, th, "i"),
                                      piece(dconv, th, "i"), piece(dgate, th, "i")], a_fn=join_bf16,
                               b_ins=[full2(w_in_nat)], b_fn=_ident,
                               epi_ins=[rows_i(x, th), rows_i(dx1, th), vec(g_pre_mix)], epi_fn=in_norm_bwd,
                               outs=[(_sds((seq, D), F32), _bs((th, D), lambda i, j, k: (i, 0))), part(nh)],
                               acc_shape=(th, D))

    chip_major = lambda a: a.reshape(a.shape[0], N_CHIPS, a.shape[1] // N_CHIPS).transpose(1, 0, 2)
    big = [chip_major(dw_in), chip_major(dw_ao), chip_major(dw_co), dw_o.reshape(N_CHIPS, D // N_CHIPS, D), dw_up,
           dw_down.reshape(N_CHIPS, D_FF // N_CHIPS, D), dw_pg.reshape(N_CHIPS, D // N_CHIPS, D), chip_major(dw_pp)]
    small = (loss_p, [dg_pre_mix_p, dg_post_mix_p, dg_pre_mlp_p, dg_post_mlp_p, dg_ple_p], dbg_a_p, dbg_c_p, dwc_p)
    return grad_x, big, small


RS_GROUPS = ((0,), (4,), (5,), (1, 2, 3, 6, 7))


def _reduce_scatter(big):
    pair = [None] * len(big)
    for gi, group in enumerate(RS_GROUPS):
        for w, s in zip(group, _rs_pair_sum(f"rs_pair_sum_{gi}", [big[w] for w in group])):
            pair[w] = s
    return _rs_exchange_join(pair)


def kernel(x, p, g_pre_mix, w_in, b_gate, w_conv, w_attn_out, w_conv_out, w_o, g_post_mix, g_pre_mlp, w_up, w_down, g_post_mlp, g_ple, w_ple_gate, w_ple_proj, loss_target, m_g_pre_mix, m_w_in, m_b_gate, m_w_conv, m_w_attn_out, m_w_conv_out, m_w_o, m_g_post_mix, m_g_pre_mlp, m_w_up, m_w_down, m_g_post_mlp, m_g_ple, m_w_ple_gate, m_w_ple_proj, v_g_pre_mix, v_w_in, v_b_gate, v_w_conv, v_w_attn_out, v_w_conv_out, v_w_o, v_g_post_mix, v_g_pre_mlp, v_w_up, v_w_down, v_g_post_mlp, v_g_ple, v_w_ple_gate, v_w_ple_proj):
    mats = [w_in, w_attn_out, w_conv_out, w_o, w_up, w_down, w_ple_gate, w_ple_proj]
    mats_m = [m_w_in, m_w_attn_out, m_w_conv_out, m_w_o, m_w_up, m_w_down, m_w_ple_gate, m_w_ple_proj]
    mats_v = [v_w_in, v_w_attn_out, v_w_conv_out, v_w_o, v_w_up, v_w_down, v_w_ple_gate, v_w_ple_proj]
    gains = [g_pre_mix, g_post_mix, g_pre_mlp, g_post_mlp, g_ple]
    gains_m = [m_g_pre_mix, m_g_post_mix, m_g_pre_mlp, m_g_post_mlp, m_g_ple]
    gains_v = [v_g_pre_mix, v_g_post_mix, v_g_pre_mlp, v_g_post_mlp, v_g_ple]

    taps = jnp.concatenate([w_conv[0], jnp.zeros((CONV_PAD_ROWS - 3, LANES), F32)], axis=0)
    gathered = _allgather_weights([w[0].astype(BF16) for w in mats] + [taps])
    cols_joined = lambda a: a.transpose(1, 0, 2).reshape(a.shape[1], N_CHIPS * a.shape[2])
    rows_joined = lambda a: a.reshape(N_CHIPS * a.shape[1], a.shape[2])
    col_sharded = (0, 1, 2, 4, 7)
    wf = [cols_joined(gathered[n]) if n in col_sharded else rows_joined(gathered[n]) for n in range(8)]
    w_conv_full = cols_joined(gathered[8])[0:3, :]
    chip = 2 * lax.axis_index("x") + lax.axis_index("y")

    grad_x, big, small = _local_step(x[0], p[0, 0], loss_target[0], gains, b_gate, w_conv_full, wf)

    shard_grads = _reduce_scatter(big)
    red = _small_allreduce(*small)
    loss = red[0, 0]
    grad_gains = [red[1 + r:2 + r, :] for r in range(5)]
    grad_b_gate = jnp.concatenate([red[6:7, :], red[7:8, :]], axis=1)
    grad_w_conv = lax.dynamic_slice(red[8:11, :], (0, chip * LANES), (3, LANES))[None]

    grads_big = [gr.reshape(w.shape) for gr, w in zip(shard_grads, mats)]
    upd_big = [_adamw(f"adamw_{i}", w, gr, m, v) for i, (w, gr, m, v) in enumerate(zip(mats, grads_big, mats_m, mats_v))]
    pack = lambda vs, bg: jnp.concatenate(list(vs) + [bg.reshape(2, D_MODEL), jnp.zeros((1, D_MODEL), F32)], axis=0)
    upd_small = _adamw("adamw_small", pack(gains, b_gate), pack(grad_gains, grad_b_gate),
                       pack(gains_m, m_b_gate), pack(gains_v, v_b_gate))
    upd_conv = _adamw("adamw_conv", w_conv, grad_w_conv, m_w_conv, v_w_conv)

    def small_out(a, which):
        gains_out = [a[r:r + 1, :] for r in range(5)]
        return gains_out, a[5:7, :].reshape(1, 2 * D_MODEL)

    def ordered(g_pre_mix_, big_, b_gate_, conv_, g_rest):
        return [g_pre_mix_, big_[0], b_gate_, conv_, big_[1], big_[2], big_[3], g_rest[0], g_rest[1], big_[4], big_[5],
                g_rest[2], g_rest[3], big_[6], big_[7]]

    outs = [loss, grad_x[None]]
    outs += ordered(grad_gains[0], grads_big, grad_b_gate, grad_w_conv, grad_gains[1:])
    for which in range(3):
        g_out, b_out = small_out(upd_small[which], which)
        outs += ordered(g_out[0], [u[which] for u in upd_big], b_out, upd_conv[which], g_out[1:])
    return tuple(outs)
```

```python
import jax
import jax.numpy as jnp
from jax import lax
from jax.experimental import pallas as pl
from jax.experimental.pallas import tpu as pltpu

F32 = jnp.float32
BF16 = jnp.bfloat16
MESH = pl.DeviceIdType.MESH

D_MODEL = 1024
N_HEADS = 8
HEAD_DIM = 64
ATTN_W = N_HEADS * HEAD_DIM
CONV_W = 512
D_FF = 4096
PLE_DIM = 256
D_IN = 5120
N_CHIPS = 4
EPS = 1e-6
Q_SCALE = HEAD_DIM ** -0.5

ADAM_LR = 0.001
ADAM_B1 = 0.9
ADAM_B2 = 0.999
ADAM_EPS = 1e-08
ADAM_WD = 0.01
ADAM_STEP = 10

V7X_VMEM_BYTES = 64 * 1024 * 1024
VMEM_LIMIT = V7X_VMEM_BYTES - 8 * 1024 * 1024
LANES = 128
ATT_BLK = 256
SMALL_ROWS = 16
CONV_PAD_ROWS = 16


def _cparams(n_grid):
    return pltpu.CompilerParams(dimension_semantics=("arbitrary",) * n_grid, vmem_limit_bytes=VMEM_LIMIT)


def _bs(shape, fn):
    return pl.BlockSpec(shape, fn)


def _rms_stats(xf):
    return lax.rsqrt(jnp.mean(xf * xf, axis=-1, keepdims=True) + EPS)


def _rms(xf, g):
    return xf * _rms_stats(xf) * g


def _rms_bwd(xf, g, dy):
    r = _rms_stats(xf)
    xh = xf * r
    dyg = dy * g
    dx = r * (dyg - xh * jnp.mean(dyg * xh, axis=-1, keepdims=True))
    return dx, jnp.sum(dy * xh, axis=0, keepdims=True)


def _sig(z):
    return 1.0 / (1.0 + jnp.exp(-z))


def _ident(a):
    return a


def _to_bf16(a):
    return a.astype(BF16)


_DIMS = {"nn": (((1,), (0,)), ((), ())), "nt": (((1,), (1,)), ((), ())), "tn": (((0,), (0,)), ((), ()))}


def _mm(name, mode, grid, a_ins, a_fn, b_ins, b_fn, outs, acc_shape, epi_ins=(), epi_fn=None,
        a_cache=None, a_outs=(), epi_a=()):
    nk = grid[2]
    na, nb, ne, no, nao = len(a_ins), len(b_ins), len(epi_ins), len(outs), len(a_outs)
    assert a_cache is None or nk == 1
    assert not a_outs or a_cache is not None
    dims = _DIMS[mode]
    if epi_fn is None:
        epi_fn = lambda acc: (acc,)

    def body(*refs):
        a_refs = refs[:na]
        b_refs = refs[na:na + nb]
        e_refs = refs[na + nb:na + nb + ne]
        o_refs = refs[na + nb + ne:na + nb + ne + no]
        ao_refs = refs[na + nb + ne + no:na + nb + ne + no + nao]
        scratch = list(refs[na + nb + ne + no + nao:])
        acc_ref = scratch.pop(0) if nk > 1 else None
        a_sc = scratch.pop(0) if a_cache is not None else None
        j = pl.program_id(1)
        k = pl.program_id(2)

        def finish(acc):
            res = epi_fn(acc, *[a_refs[t][...] for t in epi_a], *[r[...] for r in e_refs])
            for r, val in zip(o_refs, res):
                r[...] = val.astype(r.dtype)

        if a_sc is not None:
            @pl.when(j == 0)
            def _():
                res = a_fn(*[r[...] for r in a_refs])
                if nao:
                    for r, val in zip(ao_refs, res[1:]):
                        r[...] = val.astype(r.dtype)
                    res = res[0]
                a_sc[...] = res
            a = a_sc[...]
        else:
            a = a_fn(*[r[...] for r in a_refs])
        b = b_fn(*[r[...] for r in b_refs])
        prod = lax.dot_general(a, b, dims, preferred_element_type=F32)
        if nk == 1:
            finish(prod)
        else:
            @pl.when(k == 0)
            def _():
                acc_ref[...] = prod

            @pl.when(k > 0)
            def _():
                acc_ref[...] += prod

            @pl.when(k == nk - 1)
            def _():
                finish(acc_ref[...])

    scratch_shapes = []
    if nk > 1:
        scratch_shapes.append(pltpu.VMEM(acc_shape, F32))
    if a_cache is not None:
        scratch_shapes.append(pltpu.VMEM(*a_cache))
    all_outs = list(outs) + list(a_outs)
    res = pl.pallas_call(
        body, name=name, grid=grid,
        in_specs=[s for _, s in a_ins] + [s for _, s in b_ins] + [s for _, s in epi_ins],
        out_specs=[s for _, s in all_outs],
        out_shape=[o for o, _ in all_outs],
        scratch_shapes=scratch_shapes,
        compiler_params=_cparams(3),
    )(*[a for a, _ in a_ins], *[a for a, _ in b_ins], *[a for a, _ in epi_ins])
    return res


def _sds(shape, dtype):
    return jax.ShapeDtypeStruct(shape, dtype)


def _nt(a, b):
    return lax.dot_general(a, b, _DIMS["nt"], preferred_element_type=F32)


def _tn(a, b):
    return lax.dot_general(a, b, _DIMS["tn"], preferred_element_type=F32)


def _nn(a, b):
    return lax.dot_general(a, b, _DIMS["nn"], preferred_element_type=F32)


HEAD_PARTS = 2


def _mlp_down_ple_head(up, x1, p, tgt, g_ple, g_post_mlp, w_down, w_pg, w_pp, seq, tr):
    nblk = seq // tr
    D = D_MODEL

    def body(up_ref, x1_ref, p_ref, t_ref, gp_ref, gm_ref, wd_ref, wpg_ref, wpp_ref,
             dx2_ref, df_ref, dpre_ref, h3_ref, dpp_ref, loss_ref, dgp_ref, dgm_ref):
        gp, gm, wpg, wpp = gp_ref[...], gm_ref[...], wpg_ref[...], wpp_ref[...]
        halves = [pl.ds(n * (tr // HEAD_PARTS), tr // HEAD_PARTS) for n in range(HEAD_PARTS)]
        w_down = wd_ref[...]
        fb = []
        for r in halves:
            hidden = jnp.maximum(up_ref[r, :].astype(F32), 0.0)
            fb.append(_nn((hidden * hidden).astype(BF16), w_down))
        loss, dgp_sum, dgm_sum = 0.0, 0.0, 0.0
        for s, r in enumerate(halves):
            x2b = x1_ref[r, :] + _rms(fb[s], gm)
            h3 = _rms(x2b, gp).astype(BF16)
            gate = _sig(_nn(h3, wpg))
            pp = _nn(p_ref[r, :].astype(BF16), wpp)
            err = x2b + gate * pp - t_ref[r, :]
            dx3 = err * (1.0 / D)
            dpre = (dx3 * pp * gate * (1.0 - gate)).astype(BF16)
            h3_ref[r, :] = h3
            dpp_ref[r, :] = (dx3 * gate).astype(BF16)
            dpre_ref[r, :] = dpre
            dxn, dgp = _rms_bwd(x2b, gp, _nt(dpre, wpg))
            dx2 = dx3 + dxn
            dx2_ref[r, :] = dx2
            dfb, dgm = _rms_bwd(fb[s], gm, dx2)
            df_ref[r, :] = dfb.astype(BF16)
            loss = loss + jnp.sum(err * err, axis=0, keepdims=True)
            dgp_sum, dgm_sum = dgp_sum + dgp, dgm_sum + dgm
        loss_ref[...] = loss * (0.5 / D)
        dgp_ref[...] = dgp_sum
        dgm_ref[...] = dgm_sum

    rows = _bs((tr, D), lambda i: (i, 0))
    vec = _bs((1, D), lambda i: (0, 0))
    part = _bs((None, 1, D), lambda i: (i, 0, 0))
    return pl.pallas_call(
        body, name="mlp_down_ple_head", grid=(nblk,),
        in_specs=[_bs((tr, D_FF), lambda i: (i, 0)), rows, _bs((tr, PLE_DIM), lambda i: (i, 0)), rows, vec, vec,
                  _bs((D_FF, D), lambda i: (0, 0)), _bs((D, D), lambda i: (0, 0)), _bs((PLE_DIM, D), lambda i: (0, 0))],
        out_specs=[rows] * 5 + [part] * 3,
        out_shape=[_sds((seq, D), F32)] + [_sds((seq, D), BF16)] * 4 + [_sds((nblk, 1, D), F32)] * 3,
        compiler_params=_cparams(1),
    )(up, x1, p, tgt, g_ple, g_post_mlp, w_down, w_pg, w_pp)


def _shift_rows_down(u, prev, n):
    rows = u.shape[0]
    ridx = lax.broadcasted_iota(jnp.int32, u.shape, 0)
    out = pltpu.roll(u, n, 0)
    for r in range(n):
        out = jnp.where(ridx == r, prev[8 - n + r:8 - n + r + 1, :], out)
    del rows
    return out


def _shift_rows_up(u, nxt, n):
    rows = u.shape[0]
    ridx = lax.broadcasted_iota(jnp.int32, u.shape, 0)
    out = pltpu.roll(u, rows - n, 0)
    for r in range(n):
        out = jnp.where(ridx == rows - n + r, nxt[r:r + 1, :], out)
    return out


CONV_COL0 = 0


def _conv_fwd(proj, w_conv, seq, tr):
    hb = tr // 8

    def body(cb_ref, cc_ref, cu_ref, ccp_ref, cup_ref, w_ref, e_ref):
        i = pl.program_id(0)
        u = cc_ref[...] * cu_ref[...]
        up = jnp.where(i > 0, ccp_ref[...] * cup_ref[...], 0.0)
        w = w_ref[...]
        d = w[0:1, :] * _shift_rows_down(u, up, 2) + w[1:2, :] * _shift_rows_down(u, up, 1) + w[2:3, :] * u
        e_ref[...] = (cb_ref[...] * d).astype(BF16)

    prev = lambda c: (lambda i: (jnp.maximum(i * hb - 1, 0), c))
    return pl.pallas_call(
        body, name="conv_fwd", grid=(seq // tr,),
        in_specs=[_bs((tr, CONV_W), lambda i: (i, CONV_COL0)),
                  _bs((tr, CONV_W), lambda i: (i, CONV_COL0 + 1)),
                  _bs((tr, CONV_W), lambda i: (i, CONV_COL0 + 2)),
                  _bs((8, CONV_W), prev(CONV_COL0 + 1)),
                  _bs((8, CONV_W), prev(CONV_COL0 + 2)),
                  _bs((3, CONV_W), lambda i: (0, 0))],
        out_specs=_bs((tr, CONV_W), lambda i: (i, 0)),
        out_shape=_sds((seq, CONV_W), BF16),
        compiler_params=_cparams(1),
    )(proj, proj, proj, proj, proj, w_conv)


def _conv_bwd(proj, de, w_conv, seq, tr):
    hb = tr // 8
    nblk = seq // tr

    def body(cb_ref, cc_ref, cu_ref, ccp_ref, cup_ref, cbn_ref, de_ref, den_ref, w_ref, o_ref, dw_ref):
        i = pl.program_id(0)
        cc, cu, cb = cc_ref[...], cu_ref[...], cb_ref[...]
        u = cc * cu
        up = jnp.where(i > 0, ccp_ref[...] * cup_ref[...], 0.0)
        u1 = _shift_rows_down(u, up, 1)
        u2 = _shift_rows_down(u, up, 2)
        de_ = de_ref[...]
        dd = de_ * cb
        ddn = jnp.where(i < nblk - 1, den_ref[...] * cbn_ref[...], 0.0)
        w = w_ref[...]
        du = w[2:3, :] * dd + w[1:2, :] * _shift_rows_up(dd, ddn, 1) + w[0:1, :] * _shift_rows_up(dd, ddn, 2)
        o_ref[:, 0:CONV_W] = (de_ * (w[0:1, :] * u2 + w[1:2, :] * u1 + w[2:3, :] * u)).astype(BF16)
        o_ref[:, CONV_W:2 * CONV_W] = (du * cu).astype(BF16)
        o_ref[:, 2 * CONV_W:3 * CONV_W] = (du * cc).astype(BF16)
        ridx = lax.broadcasted_iota(jnp.int32, (8, CONV_W), 0)
        dw0 = jnp.sum(dd * u2, axis=0, keepdims=True)
        dw1 = jnp.sum(dd * u1, axis=0, keepdims=True)
        dw2 = jnp.sum(dd * u, axis=0, keepdims=True)
        dw_ref[...] = jnp.where(ridx == 0, dw0, jnp.where(ridx == 1, dw1, jnp.where(ridx == 2, dw2, 0.0)))

    prev = lambda c: (lambda i: (jnp.maximum(i * hb - 1, 0), c))
    nxt = lambda c: (lambda i: (jnp.minimum((i + 1) * hb, seq // 8 - 1), c))
    return pl.pallas_call(
        body, name="conv_bwd", grid=(nblk,),
        in_specs=[_bs((tr, CONV_W), lambda i: (i, CONV_COL0)),
                  _bs((tr, CONV_W), lambda i: (i, CONV_COL0 + 1)),
                  _bs((tr, CONV_W), lambda i: (i, CONV_COL0 + 2)),
                  _bs((8, CONV_W), prev(CONV_COL0 + 1)),
                  _bs((8, CONV_W), prev(CONV_COL0 + 2)),
                  _bs((8, CONV_W), nxt(CONV_COL0)),
                  _bs((tr, CONV_W), lambda i: (i, 0)),
                  _bs((8, CONV_W), nxt(0)),
                  _bs((3, CONV_W), lambda i: (0, 0))],
        out_specs=[_bs((tr, 3 * CONV_W), lambda i: (i, 0)), _bs((None, 8, CONV_W), lambda i: (i, 0, 0))],
        out_shape=[_sds((seq, 3 * CONV_W), BF16), _sds((nblk, 8, CONV_W), F32)],
        compiler_params=_cparams(1),
    )(proj, proj, proj, proj, proj, proj, de, de, w_conv)


def _log_gates(z):
    lse = jnp.log(1.0 + jnp.exp(-jnp.abs(z)))
    log_beta = jnp.minimum(z, 0.0) - lse
    return log_beta, log_beta - z


DEAD_LOG_WEIGHT = -110.0
NO_TILE = -1e30


def _first_live_tile(start, scores, live_sc):
    def alive():
        return jnp.max(jnp.maximum(live_sc[0], live_sc[1])) > DEAD_LOG_WEIGHT

    def step(c):
        for h, z in enumerate(scores(c[0])):
            live_sc[h] = live_sc[h] + jnp.sum(_log_gates(z)[1], axis=-1, keepdims=True)
        return c[0] - 1, alive()

    j_end, _ = lax.while_loop(lambda c: jnp.logical_and(c[0] >= 0, c[1]), step, (start, alive()))
    return j_end + 1


def _attn_fwd(proj, seq):
    blk = ATT_BLK
    nq = seq // blk
    npair = N_HEADS // 2

    def body(q_ref, k_ref, v_ref, o_ref, z0_sc, z1_sc, w0_sc, w1_sc, tot_sc, acc_sc):
        i = pl.program_id(1)
        is_a = lax.broadcasted_iota(jnp.int32, (1, LANES), 1) < HEAD_DIM
        q2 = (q_ref[...] * Q_SCALE).astype(BF16)
        zero = jnp.zeros_like(q2)
        qs = (jnp.where(is_a, q2, zero), jnp.where(is_a, zero, q2))
        row = lax.broadcasted_iota(jnp.int32, (blk, blk), 0)
        col = lax.broadcasted_iota(jnp.int32, (blk, blk), 1)
        tri = (row > col).astype(BF16)
        causal = col < row

        def tile_of(ref, j):
            return ref[pl.ds(pl.multiple_of(j * blk, blk), blk), :].astype(BF16)

        def scores(j):
            k2 = tile_of(k_ref, j)
            return [_nt(qs[h], k2) for h in range(2)]

        has_left = i > 0
        left = jnp.maximum(i - 1, 0)

        g_d = [_log_gates(z) for z in scores(i)]
        g_l = [_log_gates(z) for z in scores(left)]
        keep_d = [jnp.where(causal, g[1], 0.0) for g in g_d]
        suf_d = [_nn(lk.astype(BF16), tri) for lk in keep_d]
        suf_l = [_nn(g[1].astype(BF16), tri) for g in g_l]
        v_d, v_l = tile_of(v_ref, i), tile_of(v_ref, left)
        pv = []
        for h in range(2):
            sum_d = jnp.sum(keep_d[h], axis=-1, keepdims=True)
            w_d = jnp.where(causal, jnp.exp(g_d[h][0] + suf_d[h]), 0.0)
            w_l = jnp.exp(g_l[h][0] + (jnp.where(has_left, sum_d, NO_TILE) + suf_l[h]))
            pv.append(_nn(w_d.astype(BF16), v_d) + _nn(w_l.astype(BF16), v_l))
            tot_sc[h] = sum_d + jnp.sum(g_l[h][1], axis=-1, keepdims=True)
        acc_sc[...] = jnp.where(is_a, pv[0], pv[1])

        z_bufs, w_bufs = (z0_sc, z1_sc), (w0_sc, w1_sc)

        def alive():
            return jnp.max(jnp.maximum(tot_sc[0], tot_sc[1])) > DEAD_LOG_WEIGHT

        def put(ref, vals):
            for h in range(2):
                ref[h] = vals[h]

        def weights(zs):
            gates = [_log_gates(z) for z in zs]
            sums = [_nn(g[1].astype(BF16), tri) for g in gates]
            ws = []
            for h in range(2):
                ws.append(jnp.exp(gates[h][0] + (tot_sc[h] + sums[h])).astype(BF16))
                tot_sc[h] = tot_sc[h] + jnp.sum(gates[h][1], axis=-1, keepdims=True)
            return ws

        def add_values(w_buf, j):
            v2 = tile_of(v_ref, j)
            acc_sc[...] += jnp.where(is_a, _nn(w_buf[0], v2), _nn(w_buf[1], v2))

        def trip(j, s):
            add_values(w_bufs[s], j + 1)
            put(z_bufs[1 - s], scores(jnp.maximum(j - 1, 0)))
            put(w_bufs[1 - s], weights((z_bufs[s][0], z_bufs[s][1])))

        @pl.when(jnp.logical_and(i >= 2, alive()))
        def _():
            put(z0_sc, scores(i - 2))
            w0_sc[...] = jnp.zeros_like(w0_sc)

            def two_trips(c):
                trip(c[0], 0)
                trip(c[0] - 1, 1)
                return c[0] - 2, alive()

            j_next, still = lax.while_loop(lambda c: jnp.logical_and(c[0] >= 1, c[1]), two_trips, (i - 2, i >= 2))
            one_left = jnp.logical_and(j_next == 0, still)

            @pl.when(one_left)
            def _():
                trip(0, 0)
                add_values(w1_sc, 0)

            @pl.when(jnp.logical_not(one_left))
            def _():
                add_values(w0_sc, j_next + 1)

        o_ref[...] = acc_sc[...].astype(BF16)

    return pl.pallas_call(
        body, name="attn_fwd", grid=(npair, nq),
        in_specs=[_bs((blk, LANES), lambda p, i: (i, p)),
                  _bs((seq, LANES), lambda p, i: (0, npair + p)),
                  _bs((seq, LANES), lambda p, i: (0, 2 * npair + p))],
        out_specs=_bs((blk, LANES), lambda p, i: (i, p)),
        out_shape=_sds((seq, ATTN_W), BF16),
        scratch_shapes=[pltpu.VMEM((2, blk, blk), F32), pltpu.VMEM((2, blk, blk), F32),
                        pltpu.VMEM((2, blk, blk), BF16), pltpu.VMEM((2, blk, blk), BF16),
                        pltpu.VMEM((2, blk, 1), F32), pltpu.VMEM((blk, LANES), F32)],
        compiler_params=_cparams(2),
    )(proj, proj, proj)


def _attn_bwd(proj, do, seq):
    blk = ATT_BLK
    nq = seq // blk
    npair = N_HEADS // 2

    def body(q_ref, k_ref, v_ref, do_ref, dq_ref, dk_out, dv_out,
             prod0_sc, prod1_sc, pend0_sc, pend1_sc, tot_sc, live_sc, cum_sc, pre_sc, dq_sc, dk_ref, dv_ref):
        i = pl.program_id(1)

        @pl.when(i == 0)
        def _():
            dk_ref[...] = jnp.zeros_like(dk_ref)
            dv_ref[...] = jnp.zeros_like(dv_ref)

        is_a = lax.broadcasted_iota(jnp.int32, (1, LANES), 1) < HEAD_DIM
        q2 = (q_ref[...] * Q_SCALE).astype(BF16)
        do2 = do_ref[...]
        zero = jnp.zeros_like(q2)
        qs = (jnp.where(is_a, q2, zero), jnp.where(is_a, zero, q2))
        dos = (jnp.where(is_a, do2, zero), jnp.where(is_a, zero, do2))
        row = lax.broadcasted_iota(jnp.int32, (blk, blk), 0)
        col = lax.broadcasted_iota(jnp.int32, (blk, blk), 1)
        tri_after = (row > col).astype(BF16)
        tri_excl = (row < col).astype(BF16)
        causal = col < row

        def tile_of(ref, j):
            return ref[pl.ds(pl.multiple_of(j * blk, blk), blk), :].astype(BF16)

        def scores(j):
            k2 = tile_of(k_ref, j)
            return [_nt(qs[h], k2) for h in range(2)]

        def products(j):
            v2 = tile_of(v_ref, j)
            return scores(j) + [_nt(dos[h], v2) for h in range(2)]

        def row_sum(a):
            return jnp.sum(a, axis=-1, keepdims=True)

        def grad_matmuls(ws, dzs, j):
            rows = pl.ds(pl.multiple_of(j * blk, blk), blk)
            k2 = tile_of(k_ref, j)
            dq_sc[...] += jnp.where(is_a, _nn(dzs[0], k2), _nn(dzs[1], k2))
            dk_ref[rows, :] += jnp.where(is_a, _tn(dzs[0], q2), _tn(dzs[1], q2))
            if ws is not None:
                dv_ref[rows, :] += jnp.where(is_a, _tn(ws[0], do2), _tn(ws[1], do2))

        has_left = i > 0
        left = jnp.maximum(i - 1, 0)

        p_d, p_l = products(i), products(left)
        g_d = [_log_gates(z) for z in p_d[:2]]
        g_l = [_log_gates(z) for z in p_l[:2]]
        keep_d = [jnp.where(causal, g[1], 0.0) for g in g_d]
        suf_d = [_nn(lk.astype(BF16), tri_after) for lk in keep_d]
        suf_l = [_nn(g[1].astype(BF16), tri_after) for g in g_l]
        w_d, w_l, gg_d, gg_l = [], [], [], []
        for h in range(2):
            sum_d = row_sum(keep_d[h])
            w_d.append(jnp.where(causal, jnp.exp(g_d[h][0] + suf_d[h]), 0.0))
            w_l.append(jnp.exp(g_l[h][0] + (jnp.where(has_left, sum_d, NO_TILE) + suf_l[h])))
            gg_d.append(p_d[2 + h] * w_d[h])
            gg_l.append(p_l[2 + h] * w_l[h])
            tot_sc[h] = sum_d + row_sum(g_l[h][1])
        before_d = [_nn(g.astype(BF16), tri_excl) for g in gg_d]
        before_l = [_nn(g.astype(BF16), tri_excl) for g in gg_l]
        dz_d, dz_l = [], []
        for h in range(2):
            beta_d, beta_l = jnp.exp(g_d[h][0]), jnp.exp(g_l[h][0])
            dz_l.append((gg_l[h] * (1.0 - beta_l) - before_l[h] * beta_l).astype(BF16))
            dz = gg_d[h] * (1.0 - beta_d) - (row_sum(gg_l[h]) + before_d[h]) * beta_d
            dz_d.append(jnp.where(causal, dz, 0.0).astype(BF16))
        dq_sc[...] = jnp.zeros_like(dq_sc)
        grad_matmuls([w.astype(BF16) for w in w_l], dz_l, left)
        grad_matmuls([w.astype(BF16) for w in w_d], dz_d, i)

        live_sc[...] = tot_sc[...]
        first = _first_live_tile(i - 2, scores, live_sc)
        trips = i - 1 - first
        prod_bufs, pend_bufs = (prod0_sc, prod1_sc), (pend0_sc, pend1_sc)

        def local_grads(prods):
            zs, dws = prods[:2], prods[2:]
            gates = [_log_gates(z) for z in zs]
            sums = [_nn(g[1].astype(BF16), tri_after) for g in gates]
            ws, gs = [], []
            for h in range(2):
                cum = cum_sc[h] + row_sum(gates[h][1])
                cum_sc[h] = cum
                ws.append(jnp.exp(gates[h][0] + ((live_sc[h] - cum) + sums[h])))
                gs.append(dws[h] * ws[h])
            befores = [_nn(g.astype(BF16), tri_excl) for g in gs]
            dzs = []
            for h in range(2):
                beta = jnp.exp(gates[h][0])
                dzs.append((gs[h] * (1.0 - beta) - (pre_sc[h] + befores[h]) * beta).astype(BF16))
                pre_sc[h] = pre_sc[h] + row_sum(gs[h])
            return [w.astype(BF16) for w in ws] + dzs

        def put(ref, vals):
            for n, val in enumerate(vals):
                ref[n] = val

        def flush(pend, j):
            grad_matmuls([pend[0], pend[1]], [pend[2], pend[3]], j)

        def trip(j, s):
            flush(pend_bufs[s], jnp.maximum(j - 1, first))
            put(prod_bufs[1 - s], products(j + 1))
            put(pend_bufs[1 - s], local_grads([prod_bufs[s][n] for n in range(4)]))

        def earlier_keys_share(j, mask):
            dzs = []
            for h, z in enumerate(scores(j)):
                beta = jnp.exp(_log_gates(z)[0])
                dzs.append(jnp.where(mask, -pre_sc[h] * beta, 0.0).astype(BF16))
            grad_matmuls(None, dzs, j)

        @pl.when(trips > 0)
        def _():
            cum_sc[...] = jnp.zeros_like(cum_sc)
            pre_sc[...] = jnp.zeros_like(pre_sc)
            pend0_sc[...] = jnp.zeros_like(pend0_sc)
            put(prod0_sc, products(first))

            def two_trips(pp, carry):
                trip(first + 2 * pp, 0)
                trip(first + 2 * pp + 1, 1)
                return carry

            lax.fori_loop(0, trips // 2, two_trips, 0)
            odd = trips % 2 == 1

            @pl.when(odd)
            def _():
                trip(i - 2, 0)
                flush(pend1_sc, i - 2)

            @pl.when(jnp.logical_not(odd))
            def _():
                flush(pend0_sc, i - 2)

            earlier_keys_share(i - 1, True)
            earlier_keys_share(i, causal)

        dq_ref[...] = (dq_sc[...] * Q_SCALE).astype(BF16)

        @pl.when(i == nq - 1)
        def _():
            dk_out[...] = dk_ref[...].astype(BF16)
            dv_out[...] = dv_ref[...].astype(BF16)

    qmap = lambda p, i: (i, p)
    return pl.pallas_call(
        body, name="attn_bwd", grid=(npair, nq),
        in_specs=[_bs((blk, LANES), qmap),
                  _bs((seq, LANES), lambda p, i: (0, npair + p)),
                  _bs((seq, LANES), lambda p, i: (0, 2 * npair + p)),
                  _bs((blk, LANES), qmap)],
        out_specs=[_bs((blk, LANES), qmap),
                   _bs((seq, LANES), lambda p, i: (0, p)),
                   _bs((seq, LANES), lambda p, i: (0, p))],
        out_shape=[_sds((seq, ATTN_W), BF16)] * 3,
        scratch_shapes=[pltpu.VMEM((4, blk, blk), F32), pltpu.VMEM((4, blk, blk), F32),
                        pltpu.VMEM((4, blk, blk), BF16), pltpu.VMEM((4, blk, blk), BF16),
                        pltpu.VMEM((2, blk, 1), F32), pltpu.VMEM((2, blk, 1), F32), pltpu.VMEM((2, blk, 1), F32),
                        pltpu.VMEM((2, blk, 1), F32), pltpu.VMEM((blk, LANES), F32),
                        pltpu.VMEM((seq, LANES), F32), pltpu.VMEM((seq, LANES), F32)],
        compiler_params=_cparams(2),
    )(proj, proj, proj, do)


def _elementwise(name, fn, ins, out_dtypes):
    rows, cols = ins[0].shape
    tr = rows
    for cand in (512, 256, 128, 64, 32, 16, 8):
        if rows % cand == 0 and cand * cols * 4 <= 2 * 1024 * 1024:
            tr = cand
            break
    n_in = len(ins)

    def body(*refs):
        res = fn(*[r[...] for r in refs[:n_in]])
        for r, val in zip(refs[n_in:], res):
            r[...] = val.astype(r.dtype)

    spec = _bs((tr, cols), lambda i: (i, 0))
    return pl.pallas_call(
        body, name=name, grid=(rows // tr,),
        in_specs=[spec] * n_in, out_specs=[spec] * len(out_dtypes),
        out_shape=[_sds((rows, cols), dt) for dt in out_dtypes],
        compiler_params=_cparams(1),
    )(*ins)


def _adamw_fn(w, g, m, v):
    m = ADAM_B1 * m + (1.0 - ADAM_B1) * g
    v = ADAM_B2 * v + (1.0 - ADAM_B2) * (g * g)
    m_hat = m / (1.0 - ADAM_B1 ** ADAM_STEP)
    v_hat = v / (1.0 - ADAM_B2 ** ADAM_STEP)
    delta = -ADAM_LR * (m_hat / (jnp.sqrt(v_hat) + ADAM_EPS) + ADAM_WD * w)
    return delta, m, v


def _adamw(name, w, g, m, v):
    shape = w.shape
    as2d = lambda a: a.reshape(-1, shape[-1])
    delta, nm, nv = _elementwise(name, _adamw_fn, [as2d(w), as2d(g), as2d(m), as2d(v)], [F32, F32, F32])
    return delta.reshape(shape), nm.reshape(shape), nv.reshape(shape)


def _place():
    return lax.axis_index("x"), lax.axis_index("y"), lax.axis_index("c")


ANY = pl.BlockSpec(memory_space=pl.ANY)
VMEM_WHOLE = pl.BlockSpec(memory_space=pltpu.VMEM)


def _allgather_weights(shards):
    n = len(shards)

    def body(*refs):
        src, dst = refs[:n], refs[n:2 * n]
        send_sems, recv_sems, local_sems = refs[2 * n:]
        x, y, c = _place()
        me, sibling, mychip = (x, y, c), (x, y, 1 - c), 2 * x + y

        x_nbr, y_nbr, diag = 2 * (1 - x) + y, 2 * x + (1 - y), 2 * (1 - x) + (1 - y)
        to_x, to_y = (1 - x, y, c), (x, 1 - y, c)

        def parts(w):
            hr = src[w].shape[0] // 2
            first = hr // 2 if hr % 32 == 0 else hr
            return first, hr - first

        def rows_of(w, chip, half, route):
            hr = src[w].shape[0] // 2
            first, second = parts(w)
            start, size = {0: (0, hr), 1: (0, hr), 2: (0, first), 3: (first, second)}[route]
            return dst[w].at[chip, pl.ds(half * hr + start, size)]

        def copy(w, k, src_ref, dst_ref, to):
            return pltpu.make_async_remote_copy(src_ref=src_ref, dst_ref=dst_ref, send_sem=send_sems.at[w, k],
                                                recv_sem=recv_sems.at[w, k], device_id=to, device_id_type=MESH)

        def landed(w, route):
            chip = {0: x_nbr, 1: y_nbr, 2: diag, 3: diag}[route]
            return rows_of(w, chip, c, route), chip

        def routes(w):
            return (0, 1, 2, 3) if parts(w)[1] else (0, 1, 2)

        started, local = [], []
        for w in range(n):
            hr = src[w].shape[0] // 2
            own = pltpu.make_async_copy(src[w], dst[w].at[mychip], local_sems.at[w])
            own.start()
            local.append(own)
            mine = src[w].at[pl.ds(c * hr, hr)]
            for route, to in ((0, to_x), (1, to_y)):
                cp = copy(w, route, mine, rows_of(w, mychip, c, route), to)
                cp.start()
                started.append(cp)

        def pass_on(w, route):
            got, chip = landed(w, route)
            copy(w, route, got, got, me).wait_recv()
            if route == 1:
                part = rows_of(w, chip, c, 2)
                started.append(copy(w, 2, part, part, to_x))
                started[-1].start()
            if route == 0 and parts(w)[1]:
                part = rows_of(w, chip, c, 3)
                started.append(copy(w, 3, part, part, to_y))
                started[-1].start()
            started.append(copy(w, 4 + route, got, got, sibling))
            started[-1].start()

        for w in range(n):
            pass_on(w, 1)
            pass_on(w, 0)
        for w in range(n):
            for route in routes(w)[2:]:
                pass_on(w, route)
        for w in range(n):
            for route in routes(w):
                chip = landed(w, route)[1]
                from_sib = rows_of(w, chip, 1 - c, route)
                copy(w, 4 + route, from_sib, from_sib, me).wait_recv()
        for cp in local:
            cp.wait()
        for cp in started:
            cp.wait_send()

    return pl.pallas_call(
        body, name="allgather_weights",
        in_specs=[VMEM_WHOLE] * n, out_specs=[VMEM_WHOLE] * n,
        out_shape=[_sds((N_CHIPS,) + s.shape, s.dtype) for s in shards],
        scratch_shapes=[pltpu.SemaphoreType.DMA((n, 8)), pltpu.SemaphoreType.DMA((n, 8)),
                        pltpu.SemaphoreType.DMA((n,))],
        compiler_params=pltpu.CompilerParams(vmem_limit_bytes=VMEM_LIMIT),
    )(*shards)


SUM_ROWS = 64


def _rs_pair_sum(name, grads):
    n = len(grads)

    def body(*refs):
        g, out = refs[:n], refs[n:2 * n]
        stage, give16, land, keep = (refs[m * n:(m + 1) * n] for m in range(2, 6))
        send_sems, recv_sems, stage_sems, keep_sems = refs[6 * n:]
        x, y, c = _place()
        sibling = (x, y, 1 - c)

        def over_rows(w, fn):
            nb = g[w].shape[1] // 2 // SUM_ROWS

            def step(idx, carry):
                fn(idx // nb, pl.ds(pl.multiple_of((idx % nb) * SUM_ROWS, SUM_ROWS), SUM_ROWS))
                return carry

            lax.fori_loop(0, N_CHIPS * nb, step, 0)

        loads = []
        for w in range(n):
            hr = g[w].shape[1] // 2
            st = pltpu.make_async_copy(g[w].at[:, pl.ds((1 - c) * hr, hr)], stage[w], stage_sems.at[w])
            kp = pltpu.make_async_copy(g[w].at[:, pl.ds(c * hr, hr)], keep[w], keep_sems.at[w])
            st.start()
            kp.start()
            loads.append((st, kp))
        gives = []
        for w in range(n):
            loads[w][0].wait()

            def narrow(k, rows, w=w):
                give16[w][k, rows, :] = stage[w][k, rows, :].astype(BF16)

            over_rows(w, narrow)
            give = pltpu.make_async_remote_copy(src_ref=give16[w], dst_ref=land[w], send_sem=send_sems.at[w],
                                                recv_sem=recv_sems.at[w], device_id=sibling, device_id_type=MESH)
            give.start()
            gives.append(give)
        for w in range(n):
            loads[w][1].wait()
            gives[w].wait_recv()

            def add(k, rows, w=w):
                out[w][k, rows, :] = (keep[w][k, rows, :] + land[w][k, rows, :].astype(F32)).astype(BF16)

            over_rows(w, add)
        for give in gives:
            give.wait_send()

    half = [(N_CHIPS, a.shape[1] // 2, a.shape[2]) for a in grads]
    wide = [pltpu.VMEM(s, F32) for s in half]
    narrow_bufs = [pltpu.VMEM(s, BF16) for s in half]
    sems = pltpu.SemaphoreType.DMA((n,))
    return pl.pallas_call(
        body, name=name,
        in_specs=[ANY] * n, out_specs=[VMEM_WHOLE] * n, out_shape=[_sds(s, BF16) for s in half],
        scratch_shapes=wide + narrow_bufs + narrow_bufs + wide + [sems, sems, sems, sems],
        compiler_params=pltpu.CompilerParams(vmem_limit_bytes=VMEM_LIMIT),
    )(*grads)


def _rs_exchange_join(parts):
    n = len(parts)

    def body(*refs):
        t, full = refs[:n], refs[n:2 * n]
        got_x, got_y, pass_on, got_2 = (refs[m * n:(m + 1) * n] for m in range(2, 6))
        send_sems, recv_sems = refs[6 * n:]
        x, y, c = _place()
        mychip, sibling = 2 * x + y, (x, y, 1 - c)
        x_nbr, y_nbr, diag = 2 * (1 - x) + y, 2 * x + (1 - y), 2 * (1 - x) + (1 - y)
        to_x, to_y = (1 - x, y, c), (x, 1 - y, c)
        sends = []

        def copy(w, k, src_ref, dst_ref, to):
            return pltpu.make_async_remote_copy(src_ref=src_ref, dst_ref=dst_ref, send_sem=send_sems.at[w, k],
                                                recv_sem=recv_sems.at[w, k], device_id=to, device_id_type=MESH)

        def start(cp):
            cp.start()
            sends.append(cp)

        def add_rows(w, count, fn):
            def step(idx, carry):
                fn(pl.ds(pl.multiple_of(idx * SUM_ROWS, SUM_ROWS), SUM_ROWS), pl.multiple_of(idx * SUM_ROWS, SUM_ROWS))
                return carry
            lax.fori_loop(0, count // SUM_ROWS, step, 0)

        f32 = lambda v: v.astype(F32)
        for w in range(n):
            ha = t[w].shape[1] // 2
            part_a, part_b = pl.ds(0, ha), pl.ds(ha, ha)
            start(copy(w, 0, t[w].at[x_nbr, part_a], got_x[w].at[0], to_x))
            start(copy(w, 1, t[w].at[diag, part_a], got_x[w].at[1], to_x))
            start(copy(w, 2, t[w].at[y_nbr, part_b], got_y[w].at[0], to_y))
            start(copy(w, 3, t[w].at[diag, part_b], got_y[w].at[1], to_y))
        for w in range(n):
            hr = t[w].shape[1]
            ha = hr // 2
            for k in (0, 1):
                copy(w, k, got_x[w].at[k], got_x[w].at[k], to_x).wait_recv()

            def sum_a(rows, r, w=w, hr=hr):
                full[w][pl.ds(pl.multiple_of(c * hr + r, SUM_ROWS), SUM_ROWS), :] = \
                    f32(t[w][mychip, rows, :]) + f32(got_x[w][0, rows, :])
                pass_on[w][rows, :] = (f32(t[w][y_nbr, rows, :]) + f32(got_x[w][1, rows, :])).astype(BF16)

            add_rows(w, ha, sum_a)
            start(copy(w, 4, pass_on[w].at[pl.ds(0, ha)], got_2[w].at[pl.ds(0, ha)], to_y))
            for k in (2, 3):
                copy(w, k, got_y[w].at[k - 2], got_y[w].at[k - 2], to_y).wait_recv()

            def sum_b(rows, r, w=w, hr=hr, ha=ha):
                lower = pl.ds(pl.multiple_of(ha + r, SUM_ROWS), SUM_ROWS)
                full[w][pl.ds(pl.multiple_of(c * hr + ha + r, SUM_ROWS), SUM_ROWS), :] = \
                    f32(t[w][mychip, lower, :]) + f32(got_y[w][0, rows, :])
                pass_on[w][lower, :] = (f32(t[w][x_nbr, lower, :]) + f32(got_y[w][1, rows, :])).astype(BF16)

            add_rows(w, ha, sum_b)
            start(copy(w, 5, pass_on[w].at[pl.ds(ha, ha)], got_2[w].at[pl.ds(ha, ha)], to_x))
        for w in range(n):
            hr = t[w].shape[1]
            ha = hr // 2
            copy(w, 4, got_2[w].at[pl.ds(0, ha)], got_2[w].at[pl.ds(0, ha)], to_y).wait_recv()
            copy(w, 5, got_2[w].at[pl.ds(ha, ha)], got_2[w].at[pl.ds(ha, ha)], to_x).wait_recv()

            def finish(rows, r, w=w, hr=hr):
                out_rows = pl.ds(pl.multiple_of(c * hr + r, SUM_ROWS), SUM_ROWS)
                full[w][out_rows, :] = full[w][out_rows, :] + f32(got_2[w][rows, :])

            add_rows(w, hr, finish)
            mine = full[w].at[pl.ds(c * hr, hr)]
            start(copy(w, 6, mine, mine, sibling))
        for w in range(n):
            hr = t[w].shape[1]
            theirs = full[w].at[pl.ds((1 - c) * hr, hr)]
            copy(w, 6, theirs, theirs, sibling).wait_recv()
        for cp in sends:
            cp.wait_send()

    half = lambda a: pltpu.VMEM((2, a.shape[1] // 2, a.shape[2]), a.dtype)
    whole = lambda a: pltpu.VMEM(a.shape[1:], a.dtype)
    return pl.pallas_call(
        body, name="rs_exchange_join",
        in_specs=[VMEM_WHOLE] * n, out_specs=[VMEM_WHOLE] * n,
        out_shape=[_sds((2 * a.shape[1], a.shape[2]), F32) for a in parts],
        scratch_shapes=[half(a) for a in parts] + [half(a) for a in parts] + [whole(a) for a in parts]
        + [whole(a) for a in parts] + [pltpu.SemaphoreType.DMA((n, 7)), pltpu.SemaphoreType.DMA((n, 7))],
        compiler_params=pltpu.CompilerParams(vmem_limit_bytes=VMEM_LIMIT),
    )(*parts)


def _small_allreduce(loss_p, dg_parts, dbg_a, dbg_c, dwc):
    ins = [loss_p] + list(dg_parts) + [dbg_a, dbg_c, dwc]
    n_in = len(ins)
    vmem = pl.BlockSpec(memory_space=pltpu.VMEM)

    def body(*refs):
        in_refs = refs[:n_in]
        out_ref, vec, buf, send_sems, recv_sems = refs[n_in:]
        x, y, c = _place()
        me = 4 * x + 2 * y + c
        vec[...] = jnp.zeros_like(vec)
        vec[0:1, :] = jnp.sum(in_refs[0][...], axis=0)
        for r in range(5):
            vec[1 + r:2 + r, :] = jnp.sum(in_refs[1 + r][...], axis=0)
        vec[6:7, :] = jnp.sum(in_refs[6][...], axis=0)
        vec[7:8, :] = jnp.sum(in_refs[7][...], axis=0)
        vec[8:16, 0:CONV_W] = jnp.sum(in_refs[8][...], axis=0)
        buf[pl.ds(me, 1)] = vec[...][None]
        copies = []
        for r in range(1, 8):
            fx, fy, fc = (r >> 2) & 1, (r >> 1) & 1, r & 1
            to = (1 - x if fx else x, 1 - y if fy else y, 1 - c if fc else c)
            cp = pltpu.make_async_remote_copy(src_ref=vec, dst_ref=buf.at[me], send_sem=send_sems.at[r - 1],
                                              recv_sem=recv_sems.at[r - 1], device_id=to, device_id_type=MESH)
            cp.start()
            copies.append(cp)
        for cp in copies:
            cp.wait()
        total = buf[0]
        for s in range(1, 8):
            total = total + buf[s]
        out_ref[...] = total
        out_ref[0:1, :] = jnp.broadcast_to(jnp.sum(total[0:1, :], axis=-1, keepdims=True), (1, D_MODEL))

    return pl.pallas_call(
        body, name="small_allreduce",
        in_specs=[vmem] * n_in, out_specs=vmem, out_shape=_sds((SMALL_ROWS, D_MODEL), F32),
        scratch_shapes=[pltpu.VMEM((SMALL_ROWS, D_MODEL), F32), pltpu.VMEM((8, SMALL_ROWS, D_MODEL), F32),
                        pltpu.SemaphoreType.DMA((7,)), pltpu.SemaphoreType.DMA((7,))],
    )(*ins)


def _local_step(x, p, tgt, g, b_gate, w_conv, wf):
    seq = x.shape[0]
    tm = min(seq, 1024)
    th = min(seq, 512)
    tl = min(seq, 1024)
    ni, nh, nl = seq // tm, seq // th, seq // tl
    g_pre_mix, g_post_mix, g_pre_mlp, g_post_mlp, g_ple = g
    w_in_nat, w_ao, w_co, w_o, w_up_nat, w_down, w_pg, w_pp = wf
    D = D_MODEL
    vec = lambda a, blk=0: (a, _bs((1, D), lambda i, j, k: (0, blk)))
    rows_i = lambda a, t, blk=0: (a, _bs((t, D), lambda i, j, k: (i, blk)))
    rows_k = lambda a, t, blk=0: (a, _bs((t, D), lambda i, j, k: (k, blk)))
    part = lambda n: (_sds((n, 1, D), F32), _bs((None, 1, D), lambda i, j, k: (i, 0, 0)))
    full2 = lambda a: (a, _bs(a.shape, lambda i, j, k: (0, 0)))

    normed = lambda xb, gb: (_rms(xb, gb).astype(BF16),) * 2
    keep_a = lambda t: [(_sds((seq, D), BF16), _bs((t, D), lambda i, j, k: (i, 0)))]
    qkv_w, conv_w = 3 * ATTN_W, 3 * CONV_W
    qkv, proj_conv, gates, h1 = _mm(
        "proj_in", "nn", (nh, 1, 1),
        a_ins=[rows_i(x, th), vec(g_pre_mix)], a_fn=normed, b_ins=[full2(w_in_nat)], b_fn=_ident,
        epi_fn=lambda acc: (acc[:, :qkv_w], acc[:, qkv_w:qkv_w + conv_w], acc[:, qkv_w + conv_w:]),
        outs=[(_sds((seq, qkv_w), BF16), _bs((th, qkv_w), lambda i, j, k: (i, 0))),
              (_sds((seq, conv_w), F32), _bs((th, conv_w), lambda i, j, k: (i, 0))),
              (_sds((seq, 2 * D), BF16), _bs((th, 2 * D), lambda i, j, k: (i, 0)))],
        acc_shape=(th, D_IN), a_cache=((th, D), BF16), a_outs=keep_a(th))
    o = _attn_fwd(qkv, seq)
    e = _conv_fwd(proj_conv, w_conv, seq, tm)

    def gate_values(ga, gc, ba, bc):
        return _sig(ga.astype(F32) + ba), _sig(gc.astype(F32) + bc)

    def branch_outputs(ob, eb, wao, wco):
        return _nn(ob, wao).astype(BF16).astype(F32), _nn(eb, wco).astype(BF16).astype(F32)

    def mix_fn(ga, gc, ob, eb, ba, bc, wao, wco):
        sa, sc = gate_values(ga, gc, ba, bc)
        ya, yc = branch_outputs(ob, eb, wao, wco)
        return ((sa * ya + sc * yc).astype(BF16),) * 2

    def post_mix(acc, xb, gb):
        return acc, xb + _rms(acc, gb)

    half_rows = lambda a: (a, _bs((th, a.shape[1]), lambda i, j, k: (i, 0)))
    mix_ins = [rows_i(gates, th, 0), rows_i(gates, th, 1), half_rows(o), half_rows(e), vec(b_gate, 0), vec(b_gate, 1),
               full2(w_ao), full2(w_co)]
    mixed, x1, mixin = _mm(
        "mix_out", "nn", (nh, 1, 1),
        a_ins=mix_ins, a_fn=mix_fn, b_ins=[full2(w_o)], b_fn=_ident,
        epi_ins=[rows_i(x, th), vec(g_post_mix)], epi_fn=post_mix,
        outs=[(_sds((seq, D), BF16), _bs((th, D), lambda i, j, k: (i, 0))),
              (_sds((seq, D), F32), _bs((th, D), lambda i, j, k: (i, 0)))],
        acc_shape=(th, D), a_cache=((th, D), BF16), a_outs=keep_a(th))
    up, h2 = _mm("mlp_up", "nn", (nh, 1, 1),
                 a_ins=[rows_i(x1, th), vec(g_pre_mlp)], a_fn=normed,
                 b_ins=[full2(w_up_nat)], b_fn=_ident,
                 outs=[(_sds((seq, D_FF), BF16), _bs((th, D_FF), lambda i, j, k: (i, 0)))],
                 acc_shape=(th, D_FF), a_cache=((th, D), BF16), a_outs=keep_a(th))

    def relu2(ub):
        r = jnp.maximum(ub.astype(F32), 0.0)
        return (r * r).astype(BF16)

    dx2, df, dpre, h3, dpp, loss_p, dg_ple_p, dg_post_mlp_p = _mlp_down_ple_head(
        up, x1, p, tgt, g_ple, g_post_mlp, w_down, w_pg, w_pp, seq, th)

    (dw_pp,) = _mm("dw_ple_proj", "tn", (1, 1, nh),
                   a_ins=[(p, _bs((th, PLE_DIM), lambda i, j, k: (k, 0)))], a_fn=_to_bf16,
                   b_ins=[rows_k(dpp, th)], b_fn=_ident,
                   outs=[(_sds((PLE_DIM, D), F32), _bs((PLE_DIM, D), lambda i, j, k: (0, 0)))],
                   acc_shape=(PLE_DIM, D))
    (dw_pg,) = _mm("dw_ple_gate", "tn", (1, 1, nl),
                   a_ins=[rows_k(h3, tl)], a_fn=_ident, b_ins=[rows_k(dpre, tl)], b_fn=_ident,
                   outs=[(_sds((D, D), F32), _bs((D, D), lambda i, j, k: (0, 0)))], acc_shape=(D, D))

    def dup_fn(acc, ub):
        return (acc * (2.0 * jnp.maximum(ub.astype(F32), 0.0)),)

    (dup,) = _mm("d_mlp_down", "nt", (nh, 1, 1),
                 a_ins=[rows_i(df, th)], a_fn=_ident, b_ins=[full2(w_down)], b_fn=_ident,
                 epi_ins=[(up, _bs((th, D_FF), lambda i, j, k: (i, 0)))], epi_fn=dup_fn,
                 outs=[(_sds((seq, D_FF), BF16), _bs((th, D_FF), lambda i, j, k: (i, 0)))],
                 acc_shape=(th, D_FF))
    tx = min(seq, 4096)
    (dw_down,) = _mm("dw_mlp_down", "tn", (4, 1, seq // tx),
                     a_ins=[(up, _bs((tx, D), lambda i, j, k: (k, i)))], a_fn=relu2,
                     b_ins=[rows_k(df, tx)], b_fn=_ident,
                     outs=[(_sds((D_FF, D), F32), _bs((D, D), lambda i, j, k: (i, 0)))], acc_shape=(D, D))
    (dw_up,) = _mm("dw_mlp_up", "tn", (1, 4, seq // tx),
                   a_ins=[rows_k(h2, tx)], a_fn=_ident,
                   b_ins=[(dup, _bs((tx, D), lambda i, j, k: (k, j)))], b_fn=_ident,
                   outs=[(_sds((N_CHIPS, D, D), F32), _bs((None, D, D), lambda i, j, k: (j, 0, 0)))],
                   acc_shape=(D, D))

    def mlp_norm_bwd(acc, x1b, dx2b, mixedb, g_mlp, g_mix):
        dxn, dg_mlp = _rms_bwd(x1b, g_mlp, acc)
        dx1b = dx2b + dxn
        dmixedb, dg_mix = _rms_bwd(mixedb.astype(F32), g_mix, dx1b)
        return dx1b, dmixedb, dg_mlp, dg_mix

    dx1, dmixed, dg_pre_mlp_p, dg_post_mix_p = _mm(
        "d_mlp_up", "nt", (nh, 1, 1),
        a_ins=[(dup, _bs((th, D_FF), lambda i, j, k: (i, 0)))], a_fn=_ident,
        b_ins=[full2(w_up_nat)], b_fn=_ident,
        epi_ins=[rows_i(x1, th), rows_i(dx2, th), rows_i(mixed, th), vec(g_pre_mlp), vec(g_post_mix)],
        epi_fn=mlp_norm_bwd,
        outs=[(_sds((seq, D), F32), _bs((th, D), lambda i, j, k: (i, 0))),
              (_sds((seq, D), BF16), _bs((th, D), lambda i, j, k: (i, 0))), part(nh), part(nh)],
        acc_shape=(th, D))
    (dw_o,) = _mm("dw_mix_out", "tn", (1, 1, nl),
                  a_ins=[rows_k(mixin, tl)], a_fn=_ident, b_ins=[rows_k(dmixed, tl)], b_fn=_ident,
                  outs=[(_sds((D, D), F32), _bs((D, D), lambda i, j, k: (0, 0)))], acc_shape=(D, D))

    def gate_bwd(acc, ga, gc, ob, eb, ba, bc, wao, wco):
        sa, sc = gate_values(ga, gc, ba, bc)
        ya, yc = branch_outputs(ob, eb, wao, wco)
        dga = acc * ya * sa * (1.0 - sa)
        dgc = acc * yc * sc * (1.0 - sc)
        dya, dyc = (acc * sa).astype(BF16), (acc * sc).astype(BF16)
        return (dya, dyc, jnp.concatenate([dga, dgc], axis=1), _nt(dya, wao), _nt(dyc, wco),
                jnp.sum(dga, axis=0, keepdims=True), jnp.sum(dgc, axis=0, keepdims=True))

    dya, dyc, dgate, do, de, dbg_a_p, dbg_c_p = _mm(
        "d_mix_out", "nt", (nh, 1, 1),
        a_ins=[rows_i(dmixed, th)], a_fn=_ident, b_ins=[full2(w_o)], b_fn=_ident,
        epi_ins=mix_ins, epi_fn=gate_bwd,
        outs=[(_sds((seq, D), BF16), _bs((th, D), lambda i, j, k: (i, 0)))] * 2
             + [(_sds((seq, 2 * D), BF16), _bs((th, 2 * D), lambda i, j, k: (i, 0))),
                (_sds((seq, ATTN_W), BF16), _bs((th, ATTN_W), lambda i, j, k: (i, 0))),
                (_sds((seq, CONV_W), F32), _bs((th, CONV_W), lambda i, j, k: (i, 0))), part(nh), part(nh)],
        acc_shape=(th, D))
    (dw_ao,) = _mm("dw_attn_out", "tn", (1, 1, nh),
                   a_ins=[(o, _bs((th, ATTN_W), lambda i, j, k: (k, 0)))], a_fn=_ident,
                   b_ins=[rows_k(dya, th)], b_fn=_ident,
                   outs=[(_sds((ATTN_W, D), F32), _bs((ATTN_W, D), lambda i, j, k: (0, 0)))], acc_shape=(ATTN_W, D))
    dq, dk, dv = _attn_bwd(qkv, do, seq)
    (dw_co,) = _mm("dw_conv_out", "tn", (1, 1, nh),
                   a_ins=[(e, _bs((th, CONV_W), lambda i, j, k: (k, 0)))], a_fn=_ident,
                   b_ins=[rows_k(dyc, th)], b_fn=_ident,
                   outs=[(_sds((CONV_W, D), F32), _bs((CONV_W, D), lambda i, j, k: (0, 0)))], acc_shape=(CONV_W, D))
    dconv, dwc_p = _conv_bwd(proj_conv, de, w_conv, seq, tm)
    qkv_w = 3 * ATTN_W
    join_bf16 = lambda *blocks: jnp.concatenate([b.astype(BF16) for b in blocks], axis=1)
    piece = lambda a, t, rows, blk=0: (a, _bs((t, a.shape[1]), (lambda i, j, k: (k, blk)) if rows == "k"
                                             else (lambda i, j, k: (i, blk))))
    (dw_in_qkv,) = _mm("dw_proj_in_qkv", "tn", (1, 1, ni),
                       a_ins=[rows_k(h1, tm)], a_fn=_ident,
                       b_ins=[piece(dq, tm, "k"), piece(dk, tm, "k"), piece(dv, tm, "k")], b_fn=join_bf16,
                       outs=[(_sds((D, qkv_w), F32), _bs((D, qkv_w), lambda i, j, k: (0, 0)))], acc_shape=(D, qkv_w))
    (dw_in_conv,) = _mm("dw_proj_in_conv", "tn", (1, 1, nl),
                        a_ins=[rows_k(h1, tl)], a_fn=_ident, b_ins=[piece(dconv, tl, "k")], b_fn=_ident,
                        outs=[(_sds((D, 3 * CONV_W), F32), _bs((D, 3 * CONV_W), lambda i, j, k: (0, 0)))],
                        acc_shape=(D, 3 * CONV_W))
    (dw_in_gate,) = _mm("dw_proj_in_gate", "tn", (1, 2, nl),
                        a_ins=[rows_k(h1, tl)], a_fn=_ident,
                        b_ins=[(dgate, _bs((tl, D), lambda i, j, k: (k, j)))], b_fn=_ident,
                        outs=[(_sds((D, 2 * D), F32), _bs((D, D), lambda i, j, k: (0, j)))], acc_shape=(D, D))
    dw_in = jnp.concatenate([dw_in_qkv, dw_in_conv, dw_in_gate], axis=1)

    def in_norm_bwd(acc, xb, dx1b, gb):
        dxn, dg = _rms_bwd(xb, gb, acc)
        return dx1b + dxn, dg

    grad_x, dg_pre_mix_p = _mm("d_proj_in", "nt", (nh, 1, 1),
                               a_ins=[piece(dq, th, "i"), piece(dk, th, "i"), piece(dv, th, "i"),
                                      piece(dconv, th, "i"), piece(dgate, th, "i")], a_fn=join_bf16,
                               b_ins=[full2(w_in_nat)], b_fn=_ident,
                               epi_ins=[rows_i(x, th), rows_i(dx1, th), vec(g_pre_mix)], epi_fn=in_norm_bwd,
                               outs=[(_sds((seq, D), F32), _bs((th, D), lambda i, j, k: (i, 0))), part(nh)],
                               acc_shape=(th, D))

    chip_major = lambda a: a.reshape(a.shape[0], N_CHIPS, a.shape[1] // N_CHIPS).transpose(1, 0, 2)
    big = [chip_major(dw_in), chip_major(dw_ao), chip_major(dw_co), dw_o.reshape(N_CHIPS, D // N_CHIPS, D), dw_up,
           dw_down.reshape(N_CHIPS, D_FF // N_CHIPS, D), dw_pg.reshape(N_CHIPS, D // N_CHIPS, D), chip_major(dw_pp)]
    small = (loss_p, [dg_pre_mix_p, dg_post_mix_p, dg_pre_mlp_p, dg_post_mlp_p, dg_ple_p], dbg_a_p, dbg_c_p, dwc_p)
    return grad_x, big, small


RS_GROUPS = ((0,), (4,), (5,), (1, 2, 3, 6, 7))


def _reduce_scatter(big):
    pair = [None] * len(big)
    for gi, group in enumerate(RS_GROUPS):
        for w, s in zip(group, _rs_pair_sum(f"rs_pair_sum_{gi}", [big[w] for w in group])):
            pair[w] = s
    return _rs_exchange_join(pair)


def kernel(x, p, g_pre_mix, w_in, b_gate, w_conv, w_attn_out, w_conv_out, w_o, g_post_mix, g_pre_mlp, w_up, w_down, g_post_mlp, g_ple, w_ple_gate, w_ple_proj, loss_target, m_g_pre_mix, m_w_in, m_b_gate, m_w_conv, m_w_attn_out, m_w_conv_out, m_w_o, m_g_post_mix, m_g_pre_mlp, m_w_up, m_w_down, m_g_post_mlp, m_g_ple, m_w_ple_gate, m_w_ple_proj, v_g_pre_mix, v_w_in, v_b_gate, v_w_conv, v_w_attn_out, v_w_conv_out, v_w_o, v_g_post_mix, v_g_pre_mlp, v_w_up, v_w_down, v_g_post_mlp, v_g_ple, v_w_ple_gate, v_w_ple_proj):
    mats = [w_in, w_attn_out, w_conv_out, w_o, w_up, w_down, w_ple_gate, w_ple_proj]
    mats_m = [m_w_in, m_w_attn_out, m_w_conv_out, m_w_o, m_w_up, m_w_down, m_w_ple_gate, m_w_ple_proj]
    mats_v = [v_w_in, v_w_attn_out, v_w_conv_out, v_w_o, v_w_up, v_w_down, v_w_ple_gate, v_w_ple_proj]
    gains = [g_pre_mix, g_post_mix, g_pre_mlp, g_post_mlp, g_ple]
    gains_m = [m_g_pre_mix, m_g_post_mix, m_g_pre_mlp, m_g_post_mlp, m_g_ple]
    gains_v = [v_g_pre_mix, v_g_post_mix, v_g_pre_mlp, v_g_post_mlp, v_g_ple]

    taps = jnp.concatenate([w_conv[0], jnp.zeros((CONV_PAD_ROWS - 3, LANES), F32)], axis=0)
    gathered = _allgather_weights([w[0].astype(BF16) for w in mats] + [taps])
    cols_joined = lambda a: a.transpose(1, 0, 2).reshape(a.shape[1], N_CHIPS * a.shape[2])
    rows_joined = lambda a: a.reshape(N_CHIPS * a.shape[1], a.shape[2])
    col_sharded = (0, 1, 2, 4, 7)
    wf = [cols_joined(gathered[n]) if n in col_sharded else rows_joined(gathered[n]) for n in range(8)]
    w_conv_full = cols_joined(gathered[8])[0:3, :]
    chip = 2 * lax.axis_index("x") + lax.axis_index("y")

    grad_x, big, small = _local_step(x[0], p[0, 0], loss_target[0], gains, b_gate, w_conv_full, wf)

    shard_grads = _reduce_scatter(big)
    red = _small_allreduce(*small)
    loss = red[0, 0]
    grad_gains = [red[1 + r:2 + r, :] for r in range(5)]
    grad_b_gate = jnp.concatenate([red[6:7, :], red[7:8, :]], axis=1)
    grad_w_conv = lax.dynamic_slice(red[8:11, :], (0, chip * LANES), (3, LANES))[None]

    grads_big = [gr.reshape(w.shape) for gr, w in zip(shard_grads, mats)]
    upd_big = [_adamw(f"adamw_{i}", w, gr, m, v) for i, (w, gr, m, v) in enumerate(zip(mats, grads_big, mats_m, mats_v))]
    pack = lambda vs, bg: jnp.concatenate(list(vs) + [bg.reshape(2, D_MODEL), jnp.zeros((1, D_MODEL), F32)], axis=0)
    upd_small = _adamw("adamw_small", pack(gains, b_gate), pack(grad_gains, grad_b_gate),
                       pack(gains_m, m_b_gate), pack(gains_v, v_b_gate))
    upd_conv = _adamw("adamw_conv", w_conv, grad_w_conv, m_w_conv, v_w_conv)

    def small_out(a, which):
        gains_out = [a[r:r + 1, :] for r in range(5)]
        return gains_out, a[5:7, :].reshape(1, 2 * D_MODEL)

    def ordered(g_pre_mix_, big_, b_gate_, conv_, g_rest):
        return [g_pre_mix_, big_[0], b_gate_, conv_, big_[1], big_[2], big_[3], g_rest[0], g_rest[1], big_[4], big_[5],
                g_rest[2], g_rest[3], big_[6], big_[7]]

    outs = [loss, grad_x[None]]
    outs += ordered(grad_gains[0], grads_big, grad_b_gate, grad_w_conv, grad_gains[1:])
    for which in range(3):
        g_out, b_out = small_out(upd_small[which], which)
        outs += ordered(g_out[0], [u[which] for u in upd_big], b_out, upd_conv[which], g_out[1:])
    return tuple(outs)
```
